```python
import jax, jax.numpy as jnp
from jax import lax
import numpy as np

D_MODEL = 1024
BATCH = 8
SEQ = 8192
DEPTH = 2

D_MIX = D_MODEL
GROUP_WIDTH = D_MIX // 4
POOL_WINDOWS = (2, 4, 8, 16)
POOL_GROUP = GROUP_WIDTH // len(POOL_WINDOWS)
SCONV_K = 3
SSD_D_INNER = GROUP_WIDTH
SSD_HEAD_DIM = 64
SSD_HEADS = SSD_D_INNER // SSD_HEAD_DIM
SSD_GROUPS = 2
SSD_STATE = 128
SSD_CONV_K = 4
SSD_CHUNK = 128
SSD_CONV_DIM = SSD_D_INNER + 2 * SSD_GROUPS * SSD_STATE
S5_WIDTH = GROUP_WIDTH
S5_GROUP = 16
S5_GROUPS = S5_WIDTH // S5_GROUP
S5_STATE = 64
MLP_HIDDEN = 4 * D_MODEL
EPS = 1e-6

IN_SPLITS = (GROUP_WIDTH,
             GROUP_WIDTH, GROUP_WIDTH, GROUP_WIDTH,
             SSD_D_INNER, SSD_CONV_DIM, SSD_HEADS,
             S5_WIDTH)
D_IN_PROJ = sum(IN_SPLITS)
IN_SPLIT_IDX = tuple(int(v) for v in np.cumsum(IN_SPLITS)[:-1])

kernel_name = "hymba_style_pool_conv_ssd_s5_hybrid"

f32 = jnp.float32


def rmsnorm(x, w):
    xf = x.astype(f32)
    y = xf * lax.rsqrt(jnp.mean(xf * xf, axis=-1, keepdims=True) + EPS)
    return (y * w.astype(f32)).astype(x.dtype)


def causal_dwconv(x, w):
    k, ch = w.shape
    return lax.conv_general_dilated(
        x, w[:, None, :].astype(x.dtype), window_strides=(1,),
        padding=[(k - 1, 0)], dimension_numbers=("NWC", "WIO", "NWC"),
        feature_group_count=ch)


def pool_mixer(v, w_grp, scale):
    b, s, _ = v.shape
    vf = v.astype(f32)
    cs = jnp.pad(jnp.cumsum(vf, axis=1), ((0, 0), (1, 0), (0, 0)))
    t = jnp.arange(s)
    outs = []
    for g, win in enumerate(POOL_WINDOWS):
        csg = cs[..., g * POOL_GROUP:(g + 1) * POOL_GROUP]
        start = jnp.maximum(t + 1 - win, 0)
        wsum = csg[:, 1:] - csg[:, start]
        count = jnp.minimum(t + 1, win).astype(f32)[None, :, None]
        outs.append(wsum / count - vf[..., g * POOL_GROUP:(g + 1) * POOL_GROUP])
    p = jnp.stack(outs, axis=2)
    y = jnp.einsum('bsgc,gcd->bsgd', p, w_grp.astype(f32)).reshape(b, s, GROUP_WIDTH)
    return y * scale.astype(f32)


def short_conv_mixer(gate_b, gate_c, h, w):
    return gate_b * causal_dwconv(gate_c * h, w)


def ssd_mixer(z, xbc, dt_raw, conv_w, conv_b, dt_bias, a_log, d_skip):
    b, s, _ = z.shape
    nc = s // SSD_CHUNK
    rep = SSD_HEADS // SSD_GROUPS
    xbc = jax.nn.silu((causal_dwconv(xbc, conv_w) + conv_b).astype(f32))
    xs, bm, cm = jnp.split(xbc, [SSD_D_INNER, SSD_D_INNER + SSD_GROUPS * SSD_STATE], axis=-1)
    xs = xs.reshape(b, nc, SSD_CHUNK, SSD_HEADS, SSD_HEAD_DIM)
    bm = jnp.repeat(bm.reshape(b, nc, SSD_CHUNK, SSD_GROUPS, SSD_STATE), rep, axis=3)
    cm = jnp.repeat(cm.reshape(b, nc, SSD_CHUNK, SSD_GROUPS, SSD_STATE), rep, axis=3)
    dt = jax.nn.softplus(dt_raw.astype(f32) + dt_bias.astype(f32))
    dt = dt.reshape(b, nc, SSD_CHUNK, SSD_HEADS)
    a = -jnp.exp(a_log.astype(f32))
    a_dt = (dt * a).transpose(0, 3, 1, 2)
    x_dt = xs * dt[..., None]
    a_cs = jnp.cumsum(a_dt, axis=-1)
    diff = a_cs[..., :, None] - a_cs[..., None, :]
    causal = jnp.tril(jnp.ones((SSD_CHUNK, SSD_CHUNK), dtype=bool))
    decay = jnp.exp(jnp.where(causal, diff, -jnp.inf))
    scores = jnp.einsum('bclhn,bcshn->bhcls', cm, bm) * decay
    y_diag = jnp.einsum('bhcls,bcshp->bclhp', scores, x_dt)
    decay_to_end = jnp.exp(a_cs[..., -1:] - a_cs)
    states = jnp.einsum('bclhn,bhcl,bclhp->bchpn', bm, decay_to_end, x_dt)
    chunk_decay = jnp.exp(a_cs[..., -1])

    def step(carry, inp):
        st, dec = inp
        return carry * dec[..., None, None] + st, carry

    init = jnp.zeros((b, SSD_HEADS, SSD_HEAD_DIM, SSD_STATE), f32)
    _, prev = lax.scan(step, init, (states.transpose(1, 0, 2, 3, 4), chunk_decay.transpose(2, 0, 1)))
    prev = prev.transpose(1, 0, 2, 3, 4)
    y_off = jnp.einsum('bclhn,bchpn,bhcl->bclhp', cm, prev, jnp.exp(a_cs))
    y = y_diag + y_off + xs * d_skip.astype(f32)[:, None]
    y = y.reshape(b, s, SSD_D_INNER)
    return y * jax.nn.silu(z.astype(f32))


def _complex_affine_combine(e1, e2):
    a1r, a1i, b1r, b1i = e1
    a2r, a2i, b2r, b2i = e2
    return (a2r * a1r - a2i * a1i,
            a2r * a1i + a2i * a1r,
            a2r * b1r - a2i * b1i + b2r,
            a2r * b1i + a2i * b1r + b2i)


def s5_mixer(u, a_re, a_im, log_step, b_re, b_im, c_re, c_im, d_skip, glu_w, glu_b):
    bsz, s, _ = u.shape
    uf = u.astype(f32)
    ug = uf.reshape(bsz, s, S5_GROUPS, S5_GROUP)
    a_re = a_re.astype(f32); a_im = a_im.astype(f32)
    step = jnp.exp(log_step.astype(f32))[:, None]
    mag = jnp.exp(a_re * step)
    lam_re = mag * jnp.cos(a_im * step)
    lam_im = mag * jnp.sin(a_im * step)
    den = a_re * a_re + a_im * a_im
    nr = lam_re - 1.0
    f_re = (nr * a_re + lam_im * a_im) / den
    f_im = (lam_im * a_re - nr * a_im) / den
    b_re = b_re.astype(f32); b_im = b_im.astype(f32)
    bb_re = f_re[..., None] * b_re - f_im[..., None] * b_im
    bb_im = f_re[..., None] * b_im + f_im[..., None] * b_re
    bu_re = jnp.einsum('bsgh,gph->bsgp', ug, bb_re)
    bu_im = jnp.einsum('bsgh,gph->bsgp', ug, bb_im)
    lr = jnp.broadcast_to(lam_re, bu_re.shape)
    li = jnp.broadcast_to(lam_im, bu_re.shape)
    _, _, st_re, st_im = lax.associative_scan(_complex_affine_combine, (lr, li, bu_re, bu_im), axis=1)
    y = (jnp.einsum('bsgp,ghp->bsgh', st_re, c_re.astype(f32))
         - jnp.einsum('bsgp,ghp->bsgh', st_im, c_im.astype(f32)))
    y = y.reshape(bsz, s, S5_WIDTH) + d_skip.astype(f32) * uf
    g = jax.nn.gelu(y)
    return g * jax.nn.sigmoid(g @ glu_w.astype(f32) + glu_b.astype(f32))


def _fwd_setup_inputs(seed: int = 0) -> dict:
    key = jax.random.key(seed)
    ks = jax.random.split(key, 32)
    L, D = DEPTH, D_MODEL
    nrm = lambda k, shape, sc: jax.random.normal(k, shape, f32) * sc
    gain = lambda k, shape: 1.0 + 0.01 * jax.random.normal(k, shape, f32)
    ssd_dt = jnp.exp(jax.random.uniform(ks[10], (L, SSD_HEADS), f32, np.log(1e-3), np.log(1e-1)))
    s5_a_im = (jnp.pi * jnp.arange(S5_STATE, dtype=f32))[None, None, :] + 0.01 * jax.random.normal(ks[14], (L, S5_GROUPS, S5_STATE), f32)
    return {
        "x": jax.random.normal(ks[0], (BATCH, SEQ, D), f32),
        "c": jax.random.normal(ks[1], (BATCH, D), f32),
        "norm_mix_w": gain(ks[2], (L, D)),
        "norm_mlp_w": gain(ks[3], (L, D)),
        "ada_w": nrm(ks[4], (L, D, 6 * D), 0.5 * D ** -0.5),
        "ada_b": nrm(ks[5], (L, 6 * D), 0.01),
        "w_in": nrm(ks[6], (L, D, D_IN_PROJ), D ** -0.5),
        "pool_w": nrm(ks[7], (L, len(POOL_WINDOWS), POOL_GROUP, POOL_GROUP), POOL_GROUP ** -0.5),
        "pool_scale": 1.0 + 0.1 * jax.random.normal(ks[8], (L, GROUP_WIDTH), f32),
        "sconv_w": nrm(ks[9], (L, SCONV_K, GROUP_WIDTH), SCONV_K ** -0.5),
        "ssd_conv_w": nrm(ks[11], (L, SSD_CONV_K, SSD_CONV_DIM), SSD_CONV_K ** -0.5),
        "ssd_conv_b": nrm(ks[12], (L, SSD_CONV_DIM), 0.01),
        "ssd_dt_bias": ssd_dt + jnp.log(-jnp.expm1(-ssd_dt)),
        "ssd_a_log": jnp.log(jax.random.uniform(ks[13], (L, SSD_HEADS), f32, 1.0, 16.0)),
        "ssd_d": gain(ks[15], (L, SSD_HEADS)),
        "s5_a_re": -0.5 + 0.01 * jax.random.normal(ks[16], (L, S5_GROUPS, S5_STATE), f32),
        "s5_a_im": s5_a_im,
        "s5_log_step": jax.random.uniform(ks[17], (L, S5_GROUPS), f32, np.log(1e-3), np.log(1e-1)),
        "s5_b_re": nrm(ks[18], (L, S5_GROUPS, S5_STATE, S5_GROUP), (2 * S5_GROUP) ** -0.5),
        "s5_b_im": nrm(ks[19], (L, S5_GROUPS, S5_STATE, S5_GROUP), (2 * S5_GROUP) ** -0.5),
        "s5_c_re": nrm(ks[20], (L, S5_GROUPS, S5_GROUP, S5_STATE), S5_STATE ** -0.5),
        "s5_c_im": nrm(ks[21], (L, S5_GROUPS, S5_GROUP, S5_STATE), S5_STATE ** -0.5),
        "s5_d": nrm(ks[22], (L, S5_WIDTH), 1.0),
        "s5_glu_w": nrm(ks[23], (L, S5_WIDTH, S5_WIDTH), S5_WIDTH ** -0.5),
        "s5_glu_b": nrm(ks[24], (L, S5_WIDTH), 0.01),
        "branch_norm_w": gain(ks[25], (L, D_MIX)),
        "w_out": nrm(ks[26], (L, D_MIX, D), D_MIX ** -0.5),
        "mlp_w1": nrm(ks[27], (L, D, MLP_HIDDEN), D ** -0.5),
        "mlp_w2": nrm(ks[28], (L, MLP_HIDDEN, D), MLP_HIDDEN ** -0.5),
        "final_norm_w": gain(ks[29], (D,)),
    }


def _fwd_reference(x, c, norm_mix_w, norm_mlp_w, ada_w, ada_b, w_in, pool_w, pool_scale,
              sconv_w, ssd_conv_w, ssd_conv_b, ssd_dt_bias, ssd_a_log, ssd_d,
              s5_a_re, s5_a_im, s5_log_step, s5_b_re, s5_b_im, s5_c_re, s5_c_im,
              s5_d, s5_glu_w, s5_glu_b, branch_norm_w, w_out, mlp_w1, mlp_w2,
              final_norm_w):
    dtype = x.dtype
    b, s, _ = x.shape
    cond = jax.nn.silu(c)
    h = x
    for l in range(DEPTH):
        mod = (cond @ ada_w[l] + ada_b[l])[:, None, :]
        sh1, sc1, g1, sh2, sc2, g2 = jnp.split(mod, 6, axis=-1)
        u = rmsnorm(h, norm_mix_w[l]) * (1.0 + sc1) + sh1
        proj = u @ w_in[l]
        (p_pool, p_gb, p_gc, p_h, p_z, p_xbc, p_dt, p_s5) = jnp.split(proj, IN_SPLIT_IDX, axis=-1)
        y_a = pool_mixer(p_pool, pool_w[l], pool_scale[l])
        y_b = short_conv_mixer(p_gb, p_gc, p_h, sconv_w[l])
        y_c = ssd_mixer(p_z, p_xbc, p_dt, ssd_conv_w[l], ssd_conv_b[l],
                        ssd_dt_bias[l], ssd_a_log[l], ssd_d[l])
        y_d = s5_mixer(p_s5, s5_a_re[l], s5_a_im[l], s5_log_step[l], s5_b_re[l],
                       s5_b_im[l], s5_c_re[l], s5_c_im[l], s5_d[l], s5_glu_w[l], s5_glu_b[l])
        groups = jnp.stack([y_a.astype(dtype), y_b.astype(dtype),
                            y_c.astype(dtype), y_d.astype(dtype)], axis=2)
        groups = rmsnorm(groups, branch_norm_w[l].reshape(4, GROUP_WIDTH)).reshape(b, s, D_MIX)
        h = h + g1 * (groups @ w_out[l])
        v = rmsnorm(h, norm_mlp_w[l]) * (1.0 + sc2) + sh2
        h = h + g2 * (jnp.square(jax.nn.relu(v @ mlp_w1[l])) @ mlp_w2[l])
    return rmsnorm(h, final_norm_w)


import jax as _jax
import jax.numpy as _jnp

TWIN_FORMAT = 'train_step'
FWD_PARAMS = ['x', 'c', 'norm_mix_w', 'norm_mlp_w', 'ada_w', 'ada_b', 'w_in', 'pool_w', 'pool_scale', 'sconv_w', 'ssd_conv_w', 'ssd_conv_b', 'ssd_dt_bias', 'ssd_a_log', 'ssd_d', 's5_a_re', 's5_a_im', 's5_log_step', 's5_b_re', 's5_b_im', 's5_c_re', 's5_c_im', 's5_d', 's5_glu_w', 's5_glu_b', 'branch_norm_w', 'w_out', 'mlp_w1', 'mlp_w2', 'final_norm_w']
TWIN_WEIGHTS = ['norm_mix_w', 'norm_mlp_w', 'ada_w', 'ada_b', 'w_in', 'pool_w', 'pool_scale', 'sconv_w', 'ssd_conv_w', 'ssd_conv_b', 'ssd_dt_bias', 'ssd_a_log', 'ssd_d', 's5_a_re', 's5_a_im', 's5_log_step', 's5_b_re', 's5_b_im', 's5_c_re', 's5_c_im', 's5_d', 's5_glu_w', 's5_glu_b', 'branch_norm_w', 'w_out', 'mlp_w1', 'mlp_w2', 'final_norm_w']
TWIN_DIFF_INPUT = 'x'
TWIN_INPUTS = ['x', 'c', 'norm_mix_w', 'norm_mlp_w', 'ada_w', 'ada_b', 'w_in', 'pool_w', 'pool_scale', 'sconv_w', 'ssd_conv_w', 'ssd_conv_b', 'ssd_dt_bias', 'ssd_a_log', 'ssd_d', 's5_a_re', 's5_a_im', 's5_log_step', 's5_b_re', 's5_b_im', 's5_c_re', 's5_c_im', 's5_d', 's5_glu_w', 's5_glu_b', 'branch_norm_w', 'w_out', 'mlp_w1', 'mlp_w2', 'final_norm_w', 'loss_target', 'm_norm_mix_w', 'm_norm_mlp_w', 'm_ada_w', 'm_ada_b', 'm_w_in', 'm_pool_w', 'm_pool_scale', 'm_sconv_w', 'm_ssd_conv_w', 'm_ssd_conv_b', 'm_ssd_dt_bias', 'm_ssd_a_log', 'm_ssd_d', 'm_s5_a_re', 'm_s5_a_im', 'm_s5_log_step', 'm_s5_b_re', 'm_s5_b_im', 'm_s5_c_re', 'm_s5_c_im', 'm_s5_d', 'm_s5_glu_w', 'm_s5_glu_b', 'm_branch_norm_w', 'm_w_out', 'm_mlp_w1', 'm_mlp_w2', 'm_final_norm_w', 'v_norm_mix_w', 'v_norm_mlp_w', 'v_ada_w', 'v_ada_b', 'v_w_in', 'v_pool_w', 'v_pool_scale', 'v_sconv_w', 'v_ssd_conv_w', 'v_ssd_conv_b', 'v_ssd_dt_bias', 'v_ssd_a_log', 'v_ssd_d', 'v_s5_a_re', 'v_s5_a_im', 'v_s5_log_step', 'v_s5_b_re', 'v_s5_b_im', 'v_s5_c_re', 'v_s5_c_im', 'v_s5_d', 'v_s5_glu_w', 'v_s5_glu_b', 'v_branch_norm_w', 'v_w_out', 'v_mlp_w1', 'v_mlp_w2', 'v_final_norm_w']
TWIN_OUTPUTS = ['loss', 'grad_x', 'grad_norm_mix_w', 'grad_norm_mlp_w', 'grad_ada_w', 'grad_ada_b', 'grad_w_in', 'grad_pool_w', 'grad_pool_scale', 'grad_sconv_w', 'grad_ssd_conv_w', 'grad_ssd_conv_b', 'grad_ssd_dt_bias', 'grad_ssd_a_log', 'grad_ssd_d', 'grad_s5_a_re', 'grad_s5_a_im', 'grad_s5_log_step', 'grad_s5_b_re', 'grad_s5_b_im', 'grad_s5_c_re', 'grad_s5_c_im', 'grad_s5_d', 'grad_s5_glu_w', 'grad_s5_glu_b', 'grad_branch_norm_w', 'grad_w_out', 'grad_mlp_w1', 'grad_mlp_w2', 'grad_final_norm_w', 'delta_norm_mix_w', 'delta_norm_mlp_w', 'delta_ada_w', 'delta_ada_b', 'delta_w_in', 'delta_pool_w', 'delta_pool_scale', 'delta_sconv_w', 'delta_ssd_conv_w', 'delta_ssd_conv_b', 'delta_ssd_dt_bias', 'delta_ssd_a_log', 'delta_ssd_d', 'delta_s5_a_re', 'delta_s5_a_im', 'delta_s5_log_step', 'delta_s5_b_re', 'delta_s5_b_im', 'delta_s5_c_re', 'delta_s5_c_im', 'delta_s5_d', 'delta_s5_glu_w', 'delta_s5_glu_b', 'delta_branch_norm_w', 'delta_w_out', 'delta_mlp_w1', 'delta_mlp_w2', 'delta_final_norm_w', 'new_m_norm_mix_w', 'new_m_norm_mlp_w', 'new_m_ada_w', 'new_m_ada_b', 'new_m_w_in', 'new_m_pool_w', 'new_m_pool_scale', 'new_m_sconv_w', 'new_m_ssd_conv_w', 'new_m_ssd_conv_b', 'new_m_ssd_dt_bias', 'new_m_ssd_a_log', 'new_m_ssd_d', 'new_m_s5_a_re', 'new_m_s5_a_im', 'new_m_s5_log_step', 'new_m_s5_b_re', 'new_m_s5_b_im', 'new_m_s5_c_re', 'new_m_s5_c_im', 'new_m_s5_d', 'new_m_s5_glu_w', 'new_m_s5_glu_b', 'new_m_branch_norm_w', 'new_m_w_out', 'new_m_mlp_w1', 'new_m_mlp_w2', 'new_m_final_norm_w', 'new_v_norm_mix_w', 'new_v_norm_mlp_w', 'new_v_ada_w', 'new_v_ada_b', 'new_v_w_in', 'new_v_pool_w', 'new_v_pool_scale', 'new_v_sconv_w', 'new_v_ssd_conv_w', 'new_v_ssd_conv_b', 'new_v_ssd_dt_bias', 'new_v_ssd_a_log', 'new_v_ssd_d', 'new_v_s5_a_re', 'new_v_s5_a_im', 'new_v_s5_log_step', 'new_v_s5_b_re', 'new_v_s5_b_im', 'new_v_s5_c_re', 'new_v_s5_c_im', 'new_v_s5_d', 'new_v_s5_glu_w', 'new_v_s5_glu_b', 'new_v_branch_norm_w', 'new_v_w_out', 'new_v_mlp_w1', 'new_v_mlp_w2', 'new_v_final_norm_w']
TWIN_LEAF_KINDS = {'loss': 'loss', 'grad_x': 'grad_x', 'grad_norm_mix_w': 'grad_w', 'grad_norm_mlp_w': 'grad_w', 'grad_ada_w': 'grad_w', 'grad_ada_b': 'grad_w', 'grad_w_in': 'grad_w', 'grad_pool_w': 'grad_w', 'grad_pool_scale': 'grad_w', 'grad_sconv_w': 'grad_w', 'grad_ssd_conv_w': 'grad_w', 'grad_ssd_conv_b': 'grad_w', 'grad_ssd_dt_bias': 'grad_w', 'grad_ssd_a_log': 'grad_w', 'grad_ssd_d': 'grad_w', 'grad_s5_a_re': 'grad_w', 'grad_s5_a_im': 'grad_w', 'grad_s5_log_step': 'grad_w', 'grad_s5_b_re': 'grad_w', 'grad_s5_b_im': 'grad_w', 'grad_s5_c_re': 'grad_w', 'grad_s5_c_im': 'grad_w', 'grad_s5_d': 'grad_w', 'grad_s5_glu_w': 'grad_w', 'grad_s5_glu_b': 'grad_w', 'grad_branch_norm_w': 'grad_w', 'grad_w_out': 'grad_w', 'grad_mlp_w1': 'grad_w', 'grad_mlp_w2': 'grad_w', 'grad_final_norm_w': 'grad_w', 'delta_norm_mix_w': 'delta_w', 'delta_norm_mlp_w': 'delta_w', 'delta_ada_w': 'delta_w', 'delta_ada_b': 'delta_w', 'delta_w_in': 'delta_w', 'delta_pool_w': 'delta_w', 'delta_pool_scale': 'delta_w', 'delta_sconv_w': 'delta_w', 'delta_ssd_conv_w': 'delta_w', 'delta_ssd_conv_b': 'delta_w', 'delta_ssd_dt_bias': 'delta_w', 'delta_ssd_a_log': 'delta_w', 'delta_ssd_d': 'delta_w', 'delta_s5_a_re': 'delta_w', 'delta_s5_a_im': 'delta_w', 'delta_s5_log_step': 'delta_w', 'delta_s5_b_re': 'delta_w', 'delta_s5_b_im': 'delta_w', 'delta_s5_c_re': 'delta_w', 'delta_s5_c_im': 'delta_w', 'delta_s5_d': 'delta_w', 'delta_s5_glu_w': 'delta_w', 'delta_s5_glu_b': 'delta_w', 'delta_branch_norm_w': 'delta_w', 'delta_w_out': 'delta_w', 'delta_mlp_w1': 'delta_w', 'delta_mlp_w2': 'delta_w', 'delta_final_norm_w': 'delta_w', 'new_m_norm_mix_w': 'new_m', 'new_m_norm_mlp_w': 'new_m', 'new_m_ada_w': 'new_m', 'new_m_ada_b': 'new_m', 'new_m_w_in': 'new_m', 'new_m_pool_w': 'new_m', 'new_m_pool_scale': 'new_m', 'new_m_sconv_w': 'new_m', 'new_m_ssd_conv_w': 'new_m', 'new_m_ssd_conv_b': 'new_m', 'new_m_ssd_dt_bias': 'new_m', 'new_m_ssd_a_log': 'new_m', 'new_m_ssd_d': 'new_m', 'new_m_s5_a_re': 'new_m', 'new_m_s5_a_im': 'new_m', 'new_m_s5_log_step': 'new_m', 'new_m_s5_b_re': 'new_m', 'new_m_s5_b_im': 'new_m', 'new_m_s5_c_re': 'new_m', 'new_m_s5_c_im': 'new_m', 'new_m_s5_d': 'new_m', 'new_m_s5_glu_w': 'new_m', 'new_m_s5_glu_b': 'new_m', 'new_m_branch_norm_w': 'new_m', 'new_m_w_out': 'new_m', 'new_m_mlp_w1': 'new_m', 'new_m_mlp_w2': 'new_m', 'new_m_final_norm_w': 'new_m', 'new_v_norm_mix_w': 'new_v', 'new_v_norm_mlp_w': 'new_v', 'new_v_ada_w': 'new_v', 'new_v_ada_b': 'new_v', 'new_v_w_in': 'new_v', 'new_v_pool_w': 'new_v', 'new_v_pool_scale': 'new_v', 'new_v_sconv_w': 'new_v', 'new_v_ssd_conv_w': 'new_v', 'new_v_ssd_conv_b': 'new_v', 'new_v_ssd_dt_bias': 'new_v', 'new_v_ssd_a_log': 'new_v', 'new_v_ssd_d': 'new_v', 'new_v_s5_a_re': 'new_v', 'new_v_s5_a_im': 'new_v', 'new_v_s5_log_step': 'new_v', 'new_v_s5_b_re': 'new_v', 'new_v_s5_b_im': 'new_v', 'new_v_s5_c_re': 'new_v', 'new_v_s5_c_im': 'new_v', 'new_v_s5_d': 'new_v', 'new_v_s5_glu_w': 'new_v', 'new_v_s5_glu_b': 'new_v', 'new_v_branch_norm_w': 'new_v', 'new_v_w_out': 'new_v', 'new_v_mlp_w1': 'new_v', 'new_v_mlp_w2': 'new_v', 'new_v_final_norm_w': 'new_v'}


def _forward(args):
    return _fwd_reference(*[args[k] for k in FWD_PARAMS])


def _output_shape():
    def fwd():
        inp = _fwd_setup_inputs(0)
        return _fwd_reference(*[inp[k] for k in FWD_PARAMS])
    out = _jax.eval_shape(fwd)
    return out.shape, out.dtype

N_MICROBATCH = 1
ADAM_LR = 0.001
ADAM_B1 = 0.9
ADAM_B2 = 0.999
ADAM_EPS = 1e-08
ADAM_WD = 0.01
ADAM_STEP = 10
PER_EXAMPLE_BATCH_AXIS = {'x': 0, 'c': 0, 'loss_target': 0}
SHARED_INPUTS = []
_WEIGHT_DTYPES = {'norm_mix_w': _jnp.float32, 'norm_mlp_w': _jnp.float32, 'ada_w': _jnp.float32, 'ada_b': _jnp.float32, 'w_in': _jnp.float32, 'pool_w': _jnp.float32, 'pool_scale': _jnp.float32, 'sconv_w': _jnp.float32, 'ssd_conv_w': _jnp.float32, 'ssd_conv_b': _jnp.float32, 'ssd_dt_bias': _jnp.float32, 'ssd_a_log': _jnp.float32, 'ssd_d': _jnp.float32, 's5_a_re': _jnp.float32, 's5_a_im': _jnp.float32, 's5_log_step': _jnp.float32, 's5_b_re': _jnp.float32, 's5_b_im': _jnp.float32, 's5_c_re': _jnp.float32, 's5_c_im': _jnp.float32, 's5_d': _jnp.float32, 's5_glu_w': _jnp.float32, 's5_glu_b': _jnp.float32, 'branch_norm_w': _jnp.float32, 'w_out': _jnp.float32, 'mlp_w1': _jnp.float32, 'mlp_w2': _jnp.float32, 'final_norm_w': _jnp.float32}
MOMENT_SCALE = {'norm_mix_w': 9.892957e-02, 'norm_mlp_w': 1.049211e-01, 'ada_w': 1.134815e-01, 'ada_b': 1.941519e-01, 'w_in': 6.602352e-02, 'pool_w': 7.186557e-02, 'pool_scale': 6.962486e-02, 'sconv_w': 7.680708e-02, 'ssd_conv_w': 4.617026e-02, 'ssd_conv_b': 5.330349e-02, 'ssd_dt_bias': 9.120809e-02, 'ssd_a_log': 2.556196e-01, 'ssd_d': 3.513350e-01, 's5_a_re': 1.107627e-02, 's5_a_im': 9.511089e-03, 's5_log_step': 6.918891e+00, 's5_b_re': 4.524810e-03, 's5_b_im': 4.907571e-03, 's5_c_re': 6.886492e-03, 's5_c_im': 7.723942e-03, 's5_d': 7.958255e-02, 's5_glu_w': 2.075885e-02, 's5_glu_b': 2.826589e-02, 'branch_norm_w': 7.307724e-02, 'w_out': 7.114409e-02, 'mlp_w1': 5.468236e-02, 'mlp_w2': 9.461389e-02, 'final_norm_w': 6.441890e+01}


def _to_microbatches(a, axis):
    t = _jnp.moveaxis(a, axis, 0)
    t = t.reshape((N_MICROBATCH, t.shape[0] // N_MICROBATCH) + t.shape[1:])
    return _jnp.moveaxis(t, 1, axis + 1)


def setup_inputs(seed: int = 0) -> dict:
    inp = _fwd_setup_inputs(seed)
    key = _jax.random.fold_in(_jax.random.key(seed), 7919)
    shape, _ = _output_shape()
    out = dict(inp)
    out["loss_target"] = _jax.random.normal(_jax.random.fold_in(key, 0), shape, _jnp.float32)
    for i, name in enumerate(TWIN_WEIGHTS):
        w = inp[name].astype(_jnp.float32)
        if MOMENT_SCALE is None:
            s = _jnp.sqrt(_jnp.mean(_jnp.square(w)) + 1e-30)
        else:
            s = MOMENT_SCALE[name]
        km, kv = _jax.random.split(_jax.random.fold_in(key, i + 1))
        out[name] = w
        out["m_" + name] = s * _jax.random.normal(km, w.shape, _jnp.float32)
        out["v_" + name] = (s * s) * _jax.random.uniform(kv, w.shape, _jnp.float32, 0.5, 1.5)
    if N_MICROBATCH > 1:
        for name, axis in PER_EXAMPLE_BATCH_AXIS.items():
            out[name] = _to_microbatches(out[name], axis)
    return {'x': out['x'], 'c': out['c'], 'norm_mix_w': out['norm_mix_w'], 'norm_mlp_w': out['norm_mlp_w'], 'ada_w': out['ada_w'], 'ada_b': out['ada_b'], 'w_in': out['w_in'], 'pool_w': out['pool_w'], 'pool_scale': out['pool_scale'], 'sconv_w': out['sconv_w'], 'ssd_conv_w': out['ssd_conv_w'], 'ssd_conv_b': out['ssd_conv_b'], 'ssd_dt_bias': out['ssd_dt_bias'], 'ssd_a_log': out['ssd_a_log'], 'ssd_d': out['ssd_d'], 's5_a_re': out['s5_a_re'], 's5_a_im': out['s5_a_im'], 's5_log_step': out['s5_log_step'], 's5_b_re': out['s5_b_re'], 's5_b_im': out['s5_b_im'], 's5_c_re': out['s5_c_re'], 's5_c_im': out['s5_c_im'], 's5_d': out['s5_d'], 's5_glu_w': out['s5_glu_w'], 's5_glu_b': out['s5_glu_b'], 'branch_norm_w': out['branch_norm_w'], 'w_out': out['w_out'], 'mlp_w1': out['mlp_w1'], 'mlp_w2': out['mlp_w2'], 'final_norm_w': out['final_norm_w'], 'loss_target': out['loss_target'], 'm_norm_mix_w': out['m_norm_mix_w'], 'm_norm_mlp_w': out['m_norm_mlp_w'], 'm_ada_w': out['m_ada_w'], 'm_ada_b': out['m_ada_b'], 'm_w_in': out['m_w_in'], 'm_pool_w': out['m_pool_w'], 'm_pool_scale': out['m_pool_scale'], 'm_sconv_w': out['m_sconv_w'], 'm_ssd_conv_w': out['m_ssd_conv_w'], 'm_ssd_conv_b': out['m_ssd_conv_b'], 'm_ssd_dt_bias': out['m_ssd_dt_bias'], 'm_ssd_a_log': out['m_ssd_a_log'], 'm_ssd_d': out['m_ssd_d'], 'm_s5_a_re': out['m_s5_a_re'], 'm_s5_a_im': out['m_s5_a_im'], 'm_s5_log_step': out['m_s5_log_step'], 'm_s5_b_re': out['m_s5_b_re'], 'm_s5_b_im': out['m_s5_b_im'], 'm_s5_c_re': out['m_s5_c_re'], 'm_s5_c_im': out['m_s5_c_im'], 'm_s5_d': out['m_s5_d'], 'm_s5_glu_w': out['m_s5_glu_w'], 'm_s5_glu_b': out['m_s5_glu_b'], 'm_branch_norm_w': out['m_branch_norm_w'], 'm_w_out': out['m_w_out'], 'm_mlp_w1': out['m_mlp_w1'], 'm_mlp_w2': out['m_mlp_w2'], 'm_final_norm_w': out['m_final_norm_w'], 'v_norm_mix_w': out['v_norm_mix_w'], 'v_norm_mlp_w': out['v_norm_mlp_w'], 'v_ada_w': out['v_ada_w'], 'v_ada_b': out['v_ada_b'], 'v_w_in': out['v_w_in'], 'v_pool_w': out['v_pool_w'], 'v_pool_scale': out['v_pool_scale'], 'v_sconv_w': out['v_sconv_w'], 'v_ssd_conv_w': out['v_ssd_conv_w'], 'v_ssd_conv_b': out['v_ssd_conv_b'], 'v_ssd_dt_bias': out['v_ssd_dt_bias'], 'v_ssd_a_log': out['v_ssd_a_log'], 'v_ssd_d': out['v_ssd_d'], 'v_s5_a_re': out['v_s5_a_re'], 'v_s5_a_im': out['v_s5_a_im'], 'v_s5_log_step': out['v_s5_log_step'], 'v_s5_b_re': out['v_s5_b_re'], 'v_s5_b_im': out['v_s5_b_im'], 'v_s5_c_re': out['v_s5_c_re'], 'v_s5_c_im': out['v_s5_c_im'], 'v_s5_d': out['v_s5_d'], 'v_s5_glu_w': out['v_s5_glu_w'], 'v_s5_glu_b': out['v_s5_glu_b'], 'v_branch_norm_w': out['v_branch_norm_w'], 'v_w_out': out['v_w_out'], 'v_mlp_w1': out['v_mlp_w1'], 'v_mlp_w2': out['v_mlp_w2'], 'v_final_norm_w': out['v_final_norm_w']}


def _loss(weights, diff, rest, loss_target):
    with _jax.named_scope("forward"):
        args = {**rest, TWIN_DIFF_INPUT: diff, **{k: w.astype(_WEIGHT_DTYPES[k]) for k, w in weights.items()}}
        y = _forward(args)
    with _jax.named_scope("loss_head"):
        err = _jnp.square(y.astype(_jnp.float32) - loss_target)
        return 0.5 * _jnp.sum(_jnp.mean(err, axis=-1)) if err.ndim else 0.5 * err


def _adamw(w, g, m, v):
    m = ADAM_B1 * m + (1.0 - ADAM_B1) * g
    v = ADAM_B2 * v + (1.0 - ADAM_B2) * _jnp.square(g)
    m_hat = m / (1.0 - ADAM_B1 ** ADAM_STEP)
    v_hat = v / (1.0 - ADAM_B2 ** ADAM_STEP)
    delta = -ADAM_LR * (m_hat / (_jnp.sqrt(v_hat) + ADAM_EPS) + ADAM_WD * w)
    return delta, m, v


def reference(x, c, norm_mix_w, norm_mlp_w, ada_w, ada_b, w_in, pool_w, pool_scale, sconv_w, ssd_conv_w, ssd_conv_b, ssd_dt_bias, ssd_a_log, ssd_d, s5_a_re, s5_a_im, s5_log_step, s5_b_re, s5_b_im, s5_c_re, s5_c_im, s5_d, s5_glu_w, s5_glu_b, branch_norm_w, w_out, mlp_w1, mlp_w2, final_norm_w, loss_target, m_norm_mix_w, m_norm_mlp_w, m_ada_w, m_ada_b, m_w_in, m_pool_w, m_pool_scale, m_sconv_w, m_ssd_conv_w, m_ssd_conv_b, m_ssd_dt_bias, m_ssd_a_log, m_ssd_d, m_s5_a_re, m_s5_a_im, m_s5_log_step, m_s5_b_re, m_s5_b_im, m_s5_c_re, m_s5_c_im, m_s5_d, m_s5_glu_w, m_s5_glu_b, m_branch_norm_w, m_w_out, m_mlp_w1, m_mlp_w2, m_final_norm_w, v_norm_mix_w, v_norm_mlp_w, v_ada_w, v_ada_b, v_w_in, v_pool_w, v_pool_scale, v_sconv_w, v_ssd_conv_w, v_ssd_conv_b, v_ssd_dt_bias, v_ssd_a_log, v_ssd_d, v_s5_a_re, v_s5_a_im, v_s5_log_step, v_s5_b_re, v_s5_b_im, v_s5_c_re, v_s5_c_im, v_s5_d, v_s5_glu_w, v_s5_glu_b, v_branch_norm_w, v_w_out, v_mlp_w1, v_mlp_w2, v_final_norm_w):
    given = dict(x=x, c=c, norm_mix_w=norm_mix_w, norm_mlp_w=norm_mlp_w, ada_w=ada_w, ada_b=ada_b, w_in=w_in, pool_w=pool_w, pool_scale=pool_scale, sconv_w=sconv_w, ssd_conv_w=ssd_conv_w, ssd_conv_b=ssd_conv_b, ssd_dt_bias=ssd_dt_bias, ssd_a_log=ssd_a_log, ssd_d=ssd_d, s5_a_re=s5_a_re, s5_a_im=s5_a_im, s5_log_step=s5_log_step, s5_b_re=s5_b_re, s5_b_im=s5_b_im, s5_c_re=s5_c_re, s5_c_im=s5_c_im, s5_d=s5_d, s5_glu_w=s5_glu_w, s5_glu_b=s5_glu_b, branch_norm_w=branch_norm_w, w_out=w_out, mlp_w1=mlp_w1, mlp_w2=mlp_w2, final_norm_w=final_norm_w, loss_target=loss_target, m_norm_mix_w=m_norm_mix_w, m_norm_mlp_w=m_norm_mlp_w, m_ada_w=m_ada_w, m_ada_b=m_ada_b, m_w_in=m_w_in, m_pool_w=m_pool_w, m_pool_scale=m_pool_scale, m_sconv_w=m_sconv_w, m_ssd_conv_w=m_ssd_conv_w, m_ssd_conv_b=m_ssd_conv_b, m_ssd_dt_bias=m_ssd_dt_bias, m_ssd_a_log=m_ssd_a_log, m_ssd_d=m_ssd_d, m_s5_a_re=m_s5_a_re, m_s5_a_im=m_s5_a_im, m_s5_log_step=m_s5_log_step, m_s5_b_re=m_s5_b_re, m_s5_b_im=m_s5_b_im, m_s5_c_re=m_s5_c_re, m_s5_c_im=m_s5_c_im, m_s5_d=m_s5_d, m_s5_glu_w=m_s5_glu_w, m_s5_glu_b=m_s5_glu_b, m_branch_norm_w=m_branch_norm_w, m_w_out=m_w_out, m_mlp_w1=m_mlp_w1, m_mlp_w2=m_mlp_w2, m_final_norm_w=m_final_norm_w, v_norm_mix_w=v_norm_mix_w, v_norm_mlp_w=v_norm_mlp_w, v_ada_w=v_ada_w, v_ada_b=v_ada_b, v_w_in=v_w_in, v_pool_w=v_pool_w, v_pool_scale=v_pool_scale, v_sconv_w=v_sconv_w, v_ssd_conv_w=v_ssd_conv_w, v_ssd_conv_b=v_ssd_conv_b, v_ssd_dt_bias=v_ssd_dt_bias, v_ssd_a_log=v_ssd_a_log, v_ssd_d=v_ssd_d, v_s5_a_re=v_s5_a_re, v_s5_a_im=v_s5_a_im, v_s5_log_step=v_s5_log_step, v_s5_b_re=v_s5_b_re, v_s5_b_im=v_s5_b_im, v_s5_c_re=v_s5_c_re, v_s5_c_im=v_s5_c_im, v_s5_d=v_s5_d, v_s5_glu_w=v_s5_glu_w, v_s5_glu_b=v_s5_glu_b, v_branch_norm_w=v_branch_norm_w, v_w_out=v_w_out, v_mlp_w1=v_mlp_w1, v_mlp_w2=v_mlp_w2, v_final_norm_w=v_final_norm_w)
    weights = {n: given[n] for n in TWIN_WEIGHTS}
    shared = {n: given[n] for n in SHARED_INPUTS}
    per_example = {n: given[n] for n in ['x', 'c']}
    grad_fn = _jax.value_and_grad(_loss, argnums=(0, 1))

    def one_microbatch(ex, loss_target):
        ex = dict(ex)
        diff = ex.pop(TWIN_DIFF_INPUT)
        return grad_fn(weights, diff, {**shared, **ex}, loss_target)

    if N_MICROBATCH == 1:
        loss, (grad_w, grad_x) = one_microbatch(per_example, given["loss_target"])
    else:
        def body(carry, xs):
            loss_sum, grad_sum = carry
            l_k, (gw_k, gx_k) = one_microbatch(xs[0], xs[1])
            with _jax.named_scope("update"):
                return (loss_sum + l_k, _jax.tree.map(_jnp.add, grad_sum, gw_k)), gx_k

        init = (_jnp.zeros((), _jnp.float32), _jax.tree.map(_jnp.zeros_like, weights))
        (loss, grad_w), grad_x = _jax.lax.scan(body, init, (per_example, given["loss_target"]))
    with _jax.named_scope("update"):
        delta_w, new_m, new_v = {}, {}, {}
        for n in TWIN_WEIGHTS:
            delta_w[n], new_m[n], new_v[n] = _adamw(weights[n], grad_w[n], given["m_" + n], given["v_" + n])
    return (loss, grad_x, *[grad_w[n] for n in TWIN_WEIGHTS], *[delta_w[n] for n in TWIN_WEIGHTS],
            *[new_m[n] for n in TWIN_WEIGHTS], *[new_v[n] for n in TWIN_WEIGHTS])
```

```python
import functools

import jax
import jax.numpy as jnp
from jax import lax
from jax.experimental import pallas as pl
from jax.experimental.pallas import tpu as pltpu

f32 = jnp.float32
bf16 = jnp.bfloat16

N_DEV = 8
D_MODEL = 1024
GROUP_W = 256
P_IN = 2432
DT_COL = 2304
SSD_CHUNK = 128
SSD_HEADS = 4
SSD_P = 64
S5_N = 1024
MLP_HB = 512
EPS = 1e-6
LANES = 128
VMEM_LIMIT = 56 * 1024 * 1024
ADAM_LR, ADAM_B1, ADAM_B2, ADAM_EPS, ADAM_WD, ADAM_STEP = 0.001, 0.9, 0.999, 1e-08, 0.01, 10
POOL_WINDOWS = (2, 4, 8, 16)

C_POOL, C_GB, C_GC, C_HH, C_Z, C_XS, C_BM, C_CM, C_S5 = range(9)
C_DT128 = DT_COL // LANES

MESH = pl.DeviceIdType.MESH
ANY = pl.BlockSpec(memory_space=pl.ANY)
VMEM = pl.BlockSpec(memory_space=pltpu.VMEM)


def _dot(a, b):
    return jnp.dot(a, b, preferred_element_type=f32)


def _dot_nt(a, b):
    return lax.dot_general(a, b, (((1,), (1,)), ((), ())), preferred_element_type=f32)


def _dot_tn(a, b):
    return lax.dot_general(a, b, (((0,), (0,)), ((), ())), preferred_element_type=f32)


def _dot_exact(a, b):
    return jnp.dot(a, b, preferred_element_type=f32, precision=lax.Precision.HIGHEST)


def _b(x):
    return x.astype(bf16)


def _silu(x):
    return x * jax.nn.sigmoid(x)


def _dsilu(x):
    s = jax.nn.sigmoid(x)
    return s * (1.0 + x * (1.0 - s))


def _softplus(x):
    return jnp.maximum(x, 0.0) + jnp.log1p(jnp.exp(-jnp.abs(x)))


_GELU_K = 0.7978845608028654
_GELU_C = 0.044715


def _gelu(x):
    return 0.5 * x * (1.0 + jnp.tanh(_GELU_K * (x + _GELU_C * x * x * x)))


def _dgelu(x):
    th = jnp.tanh(_GELU_K * (x + _GELU_C * x * x * x))
    return 0.5 * (1.0 + th) + 0.5 * x * (1.0 - th * th) * _GELU_K * (1.0 + 3.0 * _GELU_C * x * x)


def _rms(h):
    r = lax.rsqrt(jnp.mean(h * h, axis=-1, keepdims=True) + EPS)
    return h * r, r


def _rms_bwd(dn, n, r):
    return r * (dn - n * jnp.mean(dn * n, axis=-1, keepdims=True))


def _colsum(x):
    return jnp.sum(x, axis=0, keepdims=True)


def _params(sem=None):
    return pltpu.CompilerParams(dimension_semantics=sem, vmem_limit_bytes=VMEM_LIMIT)


def _full(shape):
    return pl.BlockSpec(shape, lambda *_: (0,) * len(shape))


def _acc(ref, val):
    @pl.when(pl.program_id(0) == 0)
    def _():
        ref[...] = val

    @pl.when(pl.program_id(0) != 0)
    def _():
        ref[...] += val


def _me():
    return lax.axis_index("x"), lax.axis_index("y"), lax.axis_index("c")


def _my_index():
    x, y, c = _me()
    return 4 * x + 2 * y + c


def _coords(p):
    return (p // 4, (p // 2) % 2, p % 2)


def _all_gather_bf16(shards):
    n = len(shards)

    def body(*refs):
        ins, outs = refs[:n], refs[n:2 * n]
        bufs = refs[2 * n:3 * n]
        send_sems, recv_sems, local_sems = refs[3 * n:]
        me = _my_index()
        for t in range(n):
            bufs[t][...] = ins[t][...].astype(bf16)
        locals_, sends = [], []
        for t in range(n):
            for l in range(2):
                cp = pltpu.make_async_copy(bufs[t].at[l], outs[t].at[l, me], local_sems.at[t, l])
                cp.start()
                locals_.append(cp)
                for k in range(1, N_DEV):
                    peer = (me + k) % N_DEV
                    rc = pltpu.make_async_remote_copy(
                        src_ref=bufs[t].at[l], dst_ref=outs[t].at[l, me],
                        send_sem=send_sems.at[2 * t + l, k], recv_sem=recv_sems.at[2 * t + l, k],
                        device_id=_coords(peer), device_id_type=MESH)
                    rc.start()
                    sends.append(rc)
        for t in range(n):
            for l in range(2):
                for k in range(1, N_DEV):
                    src = (me + N_DEV - k) % N_DEV
                    pltpu.make_async_remote_copy(
                        src_ref=bufs[t].at[l], dst_ref=outs[t].at[l, src],
                        send_sem=send_sems.at[2 * t + l, k], recv_sem=recv_sems.at[2 * t + l, k],
                        device_id=_coords(src), device_id_type=MESH).wait_recv()
        for rc in sends:
            rc.wait_send()
        for cp in locals_:
            cp.wait()

    return pl.pallas_call(
        body, name="gather_weights",
        out_shape=[jax.ShapeDtypeStruct((2, N_DEV) + s.shape[1:], bf16) for s in shards],
        in_specs=[VMEM] * n, out_specs=[ANY] * n,
        scratch_shapes=[pltpu.VMEM(s.shape, bf16) for s in shards]
        + [pltpu.SemaphoreType.DMA((2 * n, N_DEV)), pltpu.SemaphoreType.DMA((2 * n, N_DEV)), pltpu.SemaphoreType.DMA((n, 2))],
        compiler_params=pltpu.CompilerParams(vmem_limit_bytes=VMEM_LIMIT),
    )(*shards)


def _exchange_partials(parts):
    n = len(parts)

    def body(*refs):
        ins, outs = refs[:n], refs[n:2 * n]
        send_sems, recv_sems, local_sems = refs[2 * n:]
        me = _my_index()
        locals_, sends = [], []
        for t in range(n):
            for l in range(2):
                cp = pltpu.make_async_copy(ins[t].at[l, me], outs[t].at[l, me], local_sems.at[t, l])
                cp.start()
                locals_.append(cp)
                for k in range(1, N_DEV):
                    peer = (me + k) % N_DEV
                    rc = pltpu.make_async_remote_copy(
                        src_ref=ins[t].at[l, peer], dst_ref=outs[t].at[l, me],
                        send_sem=send_sems.at[2 * t + l, k], recv_sem=recv_sems.at[2 * t + l, k],
                        device_id=_coords(peer), device_id_type=MESH)
                    rc.start()
                    sends.append(rc)
        for t in range(n):
            for l in range(2):
                for k in range(1, N_DEV):
                    src = (me + N_DEV - k) % N_DEV
                    pltpu.make_async_remote_copy(
                        src_ref=ins[t].at[l, src], dst_ref=outs[t].at[l, src],
                        send_sem=send_sems.at[2 * t + l, k], recv_sem=recv_sems.at[2 * t + l, k],
                        device_id=_coords(src), device_id_type=MESH).wait_recv()
        for rc in sends:
            rc.wait_send()
        for cp in locals_:
            cp.wait()

    return pl.pallas_call(
        body, name="exchange_grads",
        out_shape=[jax.ShapeDtypeStruct(p.shape, p.dtype) for p in parts],
        in_specs=[ANY] * n, out_specs=[ANY] * n,
        scratch_shapes=[pltpu.SemaphoreType.DMA((2 * n, N_DEV)), pltpu.SemaphoreType.DMA((2 * n, N_DEV)),
                        pltpu.SemaphoreType.DMA((n, 2))],
    )(*parts)


def _gather_rows(v, name):
    def body(v_ref, o_ref, send_sems, recv_sems):
        me = _my_index()
        o_ref[me] = v_ref[...]
        sends = []
        for k in range(1, N_DEV):
            peer = (me + k) % N_DEV
            rc = pltpu.make_async_remote_copy(src_ref=v_ref, dst_ref=o_ref.at[me], send_sem=send_sems.at[k],
                                              recv_sem=recv_sems.at[k], device_id=_coords(peer), device_id_type=MESH)
            rc.start()
            sends.append(rc)
        for k in range(1, N_DEV):
            src = (me + N_DEV - k) % N_DEV
            pltpu.make_async_remote_copy(src_ref=v_ref, dst_ref=o_ref.at[src], send_sem=send_sems.at[k],
                                         recv_sem=recv_sems.at[k], device_id=_coords(src), device_id_type=MESH).wait_recv()
        for rc in sends:
            rc.wait_send()

    return pl.pallas_call(
        body, name=name, out_shape=jax.ShapeDtypeStruct((N_DEV,) + v.shape, v.dtype),
        in_specs=[VMEM], out_specs=VMEM,
        scratch_shapes=[pltpu.SemaphoreType.DMA((N_DEV,)), pltpu.SemaphoreType.DMA((N_DEV,))],
        compiler_params=pltpu.CompilerParams(vmem_limit_bytes=VMEM_LIMIT),
    )(v)


def _all_to_all_rows(v, name):
    def body(v_ref, o_ref, send_sems, recv_sems):
        me = _my_index()
        o_ref[me] = v_ref[me]
        sends = []
        for k in range(1, N_DEV):
            peer = (me + k) % N_DEV
            rc = pltpu.make_async_remote_copy(src_ref=v_ref.at[peer], dst_ref=o_ref.at[me], send_sem=send_sems.at[k],
                                              recv_sem=recv_sems.at[k], device_id=_coords(peer), device_id_type=MESH)
            rc.start()
            sends.append(rc)
        for k in range(1, N_DEV):
            src = (me + N_DEV - k) % N_DEV
            pltpu.make_async_remote_copy(src_ref=v_ref.at[src], dst_ref=o_ref.at[src], send_sem=send_sems.at[k],
                                         recv_sem=recv_sems.at[k], device_id=_coords(src), device_id_type=MESH).wait_recv()
        for rc in sends:
            rc.wait_send()

    return pl.pallas_call(
        body, name=name, out_shape=jax.ShapeDtypeStruct(v.shape, v.dtype),
        in_specs=[VMEM], out_specs=VMEM,
        scratch_shapes=[pltpu.SemaphoreType.DMA((N_DEV,)), pltpu.SemaphoreType.DMA((N_DEV,))],
    )(v)


def _ada_forward(c_all, ada_w, ada_b_cols):
    def body(c_ref, w_ref, b_ref, cond_ref, o_ref):
        cond = _silu(c_ref[...])
        cond_ref[...] = cond
        for l in range(2):
            o_ref[l] = _dot(_b(cond), _b(w_ref[l])) + b_ref[l]

    return pl.pallas_call(
        body, name="ada_forward",
        out_shape=[jax.ShapeDtypeStruct((N_DEV, D_MODEL), f32), jax.ShapeDtypeStruct((2, N_DEV, 768), f32)],
        in_specs=[VMEM] * 3, out_specs=[VMEM] * 2, compiler_params=_params(),
    )(c_all, ada_w, ada_b_cols)


def _ada_backward(cond, dmod_rows):
    def body(c_ref, d_ref, o_ref):
        cb = _b(c_ref[...])
        for l in range(2):
            o_ref[l] = _dot_tn(cb, _b(d_ref[l]))

    return pl.pallas_call(
        body, name="ada_backward", out_shape=jax.ShapeDtypeStruct((2, D_MODEL, 768), f32),
        in_specs=[VMEM] * 2, out_specs=VMEM, compiler_params=_params(),
    )(cond, dmod_rows)


def _inproj_fwd(h, norm_w, sc, sh, w_in, tb):
    t = h.shape[0]

    def body(h_ref, nw_ref, sc_ref, sh_ref, w_ref, proj_ref, u_ref):
        n, _ = _rms(h_ref[...])
        u = _b(n * nw_ref[...] * (1.0 + sc_ref[...]) + sh_ref[...])
        u_ref[...] = u
        proj_ref[...] = _dot(u, w_ref[...])

    row = pl.BlockSpec((tb, D_MODEL), lambda i: (i, 0))
    vec = _full((1, D_MODEL))
    return pl.pallas_call(
        body, name="inproj_fwd", grid=(t // tb,),
        out_shape=[jax.ShapeDtypeStruct((t, P_IN), f32), jax.ShapeDtypeStruct((t, D_MODEL), bf16)],
        in_specs=[row, vec, vec, vec, _full((D_MODEL, P_IN))],
        out_specs=[pl.BlockSpec((tb, P_IN), lambda i: (i, 0)), row],
        compiler_params=_params(("parallel",)),
    )(h, norm_w, sc, sh, w_in)


def _inproj_bwd(dparts, dh_res, h, norm_w, sc, sh, w_in, tb):
    t = h.shape[0]

    def body(*refs):
        parts = refs[:10]
        dres_ref, h_ref, nw_ref, sc_ref, sh_ref, w_ref = refs[10:16]
        dh_ref, dproj_ref, dsh_ref, dsc_ref, dnw_ref = refs[16:]
        dproj = _b(jnp.concatenate([p[...] for p in parts], axis=1))
        dproj_ref[...] = dproj
        du = _dot_nt(dproj, w_ref[...])
        n, r = _rms(h_ref[...])
        nw = nw_ref[...]
        gain = 1.0 + sc_ref[...]
        _acc(dsh_ref, _colsum(du))
        _acc(dsc_ref, _colsum(du * n * nw))
        _acc(dnw_ref, _colsum(du * gain * n))
        dh_ref[...] = dres_ref[...] + _rms_bwd(du * nw * gain, n, r)

    row = pl.BlockSpec((tb, D_MODEL), lambda i: (i, 0))
    vec = _full((1, D_MODEL))
    part_specs = [pl.BlockSpec((tb, GROUP_W), lambda i: (i, 0))] * 9 + [pl.BlockSpec((tb, LANES), lambda i: (i, 0))]
    return pl.pallas_call(
        body, name="inproj_bwd", grid=(t // tb,),
        out_shape=[jax.ShapeDtypeStruct((t, D_MODEL), f32), jax.ShapeDtypeStruct((t, P_IN), bf16)]
        + [jax.ShapeDtypeStruct((1, D_MODEL), f32)] * 3,
        in_specs=part_specs + [row, row, vec, vec, vec, _full((D_MODEL, P_IN))],
        out_specs=[row, pl.BlockSpec((tb, P_IN), lambda i: (i, 0)), vec, vec, vec],
        compiler_params=_params(("arbitrary",)),
    )(*dparts, dh_res, h, norm_w, sc, sh, w_in)


def _wgrad(a, b, n_blocks, name, tm, tk=512):
    t, m = a.shape
    nb = b.shape[1] // n_blocks
    tk = min(tk, t)
    nk = t // tk

    def body(a_ref, b_ref, o_ref, acc_ref):
        k = pl.program_id(2)
        p = _dot_tn(a_ref[...], b_ref[...])

        @pl.when(k == 0)
        def _():
            acc_ref[...] = p

        @pl.when(k != 0)
        def _():
            acc_ref[...] += p

        @pl.when(k == nk - 1)
        def _():
            o_ref[0] = acc_ref[...].astype(o_ref.dtype)

    return pl.pallas_call(
        body, name=name, grid=(m // tm, n_blocks, nk),
        out_shape=jax.ShapeDtypeStruct((n_blocks, m, nb), bf16),
        in_specs=[pl.BlockSpec((tk, tm), lambda i, j, k: (k, i)), pl.BlockSpec((tk, nb), lambda i, j, k: (k, j))],
        out_specs=pl.BlockSpec((1, tm, nb), lambda i, j, k: (j, i, 0)),
        scratch_shapes=[pltpu.VMEM((tm, nb), f32)],
        compiler_params=_params(("parallel", "parallel", "arbitrary")),
    )(a, b)


def _pool_counts(rows, t0):
    tpos = (lax.broadcasted_iota(jnp.int32, (rows, GROUP_W), 0) + t0 + 1).astype(f32)
    grp = lax.broadcasted_iota(jnp.int32, (rows, GROUP_W), 1) // 64
    win = jnp.where(grp == 0, 2.0, jnp.where(grp == 1, 4.0, jnp.where(grp == 2, 8.0, 16.0)))
    return jnp.minimum(tpos, win), grp


def _pool_select(grp, l1, l2, l3, l4):
    return jnp.where(grp == 0, l1, jnp.where(grp == 1, l2, jnp.where(grp == 2, l3, l4)))


def _pool_means(v, halo, t0):
    tb = v.shape[0]
    ext = jnp.concatenate([halo, v], axis=0)
    n = tb + 16
    s1 = ext[1:n] + ext[0:n - 1]
    s2 = s1[2:n - 1] + s1[0:n - 3]
    s3 = s2[4:n - 3] + s2[0:n - 7]
    s4 = s3[8:n - 7] + s3[0:n - 15]
    cnt, grp = _pool_counts(tb, t0)
    wsum = _pool_select(grp, s1[15:15 + tb], s2[13:13 + tb], s3[9:9 + tb], s4[1:1 + tb])
    return wsum / cnt - v


def _pool_fwd(proj, pw_bd, scale, tb):
    t = proj.shape[0]

    def body(v_ref, vh_ref, pw_ref, sc_ref, o_ref):
        i = pl.program_id(0)
        halo = jnp.where(i > 0, vh_ref[...], 0.0)
        p = _pool_means(v_ref[...], halo, i * tb)
        o_ref[...] = _dot(_b(p), _b(pw_ref[...])) * sc_ref[...]

    return pl.pallas_call(
        body, name="pool_fwd", grid=(t // tb,),
        out_shape=jax.ShapeDtypeStruct((t, GROUP_W), f32),
        in_specs=[pl.BlockSpec((tb, GROUP_W), lambda i: (i, C_POOL)),
                  pl.BlockSpec((16, GROUP_W), lambda i: (jnp.maximum(i * (tb // 16) - 1, 0), C_POOL)),
                  _full((GROUP_W, GROUP_W)), _full((1, GROUP_W))],
        out_specs=pl.BlockSpec((tb, GROUP_W), lambda i: (i, 0)),
        compiler_params=_params(("parallel",)),
    )(proj, proj, pw_bd, scale)


def _pool_bwd(proj, dy, pw_bd, scale, tb):
    t = proj.shape[0]
    nt = t // tb
    last16 = t // 16 - 1

    def body(v_ref, vh_ref, dy_ref, dyh_ref, pw_ref, sc_ref, dv_ref, dpw_ref, dsc_ref):
        i = pl.program_id(0)
        halo = jnp.where(i > 0, vh_ref[...], 0.0)
        p = _pool_means(v_ref[...], halo, i * tb)
        pw = _b(pw_ref[...])
        sc = sc_ref[...]
        dy = dy_ref[...]
        ypre = _dot(_b(p), pw)
        _acc(dsc_ref, _colsum(dy * ypre))
        dys = _b(dy * sc)
        _acc(dpw_ref, _dot_tn(_b(p), dys))
        dp = _dot_nt(dys, pw)
        dph = _dot_nt(_b(jnp.where(i < nt - 1, dyh_ref[...], 0.0) * sc), pw)
        cnt, grp = _pool_counts(tb, i * tb)
        cnth, _ = _pool_counts(16, (i + 1) * tb)
        ext = jnp.concatenate([dp / cnt, dph / cnth], axis=0)
        n = tb + 16
        f1 = ext[0:n - 1] + ext[1:n]
        f2 = f1[0:n - 3] + f1[2:n - 1]
        f3 = f2[0:n - 7] + f2[4:n - 3]
        f4 = f3[0:n - 15] + f3[8:n - 7]
        dv_ref[...] = _pool_select(grp, f1[0:tb], f2[0:tb], f3[0:tb], f4[0:tb]) - dp

    return pl.pallas_call(
        body, name="pool_bwd", grid=(nt,),
        out_shape=[jax.ShapeDtypeStruct((t, GROUP_W), f32), jax.ShapeDtypeStruct((GROUP_W, GROUP_W), f32),
                   jax.ShapeDtypeStruct((1, GROUP_W), f32)],
        in_specs=[pl.BlockSpec((tb, GROUP_W), lambda i: (i, C_POOL)),
                  pl.BlockSpec((16, GROUP_W), lambda i: (jnp.maximum(i * (tb // 16) - 1, 0), C_POOL)),
                  pl.BlockSpec((tb, GROUP_W), lambda i: (i, 0)),
                  pl.BlockSpec((16, GROUP_W), lambda i: (jnp.minimum((i + 1) * (tb // 16), last16), 0)),
                  _full((GROUP_W, GROUP_W)), _full((1, GROUP_W))],
        out_specs=[pl.BlockSpec((tb, GROUP_W), lambda i: (i, 0)), _full((GROUP_W, GROUP_W)), _full((1, GROUP_W))],
        compiler_params=_params(("arbitrary",)),
    )(proj, proj, dy, dy, pw_bd, scale)


def _sconv_fwd(proj, w, tb):
    t = proj.shape[0]

    def body(gb_ref, gc_ref, hh_ref, gch_ref, hhh_ref, w_ref, o_ref):
        i = pl.program_id(0)
        q = gc_ref[...] * hh_ref[...]
        qh = jnp.where(i > 0, gch_ref[...] * hhh_ref[...], 0.0)
        ext = jnp.concatenate([qh, q], axis=0)
        w = w_ref[...]
        conv = w[0:1] * ext[6:6 + tb] + w[1:2] * ext[7:7 + tb] + w[2:3] * ext[8:8 + tb]
        o_ref[...] = gb_ref[...] * conv

    def col(c):
        return pl.BlockSpec((tb, GROUP_W), lambda i: (i, c))

    def prev(c):
        return pl.BlockSpec((8, GROUP_W), lambda i: (jnp.maximum(i * (tb // 8) - 1, 0), c))

    return pl.pallas_call(
        body, name="sconv_fwd", grid=(t // tb,),
        out_shape=jax.ShapeDtypeStruct((t, GROUP_W), f32),
        in_specs=[col(C_GB), col(C_GC), col(C_HH), prev(C_GC), prev(C_HH), _full((8, GROUP_W))],
        out_specs=pl.BlockSpec((tb, GROUP_W), lambda i: (i, 0)),
        compiler_params=_params(("parallel",)),
    )(proj, proj, proj, proj, proj, w)


def _sconv_bwd(proj, dy, w, tb):
    t = proj.shape[0]
    nt = t // tb
    last8 = t // 8 - 1

    def body(gb_ref, gc_ref, hh_ref, gch_ref, hhh_ref, gbn_ref, dy_ref, dyn_ref, w_ref, dgb_ref, dgc_ref, dhh_ref, dw_ref):
        i = pl.program_id(0)
        gc, hh, gb, dy = gc_ref[...], hh_ref[...], gb_ref[...], dy_ref[...]
        q = gc * hh
        qh = jnp.where(i > 0, gch_ref[...] * hhh_ref[...], 0.0)
        ext = jnp.concatenate([qh, q], axis=0)
        w = w_ref[...]
        conv = w[0:1] * ext[6:6 + tb] + w[1:2] * ext[7:7 + tb] + w[2:3] * ext[8:8 + tb]
        dgb_ref[...] = dy * conv
        e = dy * gb
        en = jnp.where(i < nt - 1, dyn_ref[...] * gbn_ref[...], 0.0)
        exte = jnp.concatenate([e, en], axis=0)
        dq = w[2:3] * exte[0:tb] + w[1:2] * exte[1:1 + tb] + w[0:1] * exte[2:2 + tb]
        dgc_ref[...] = dq * hh
        dhh_ref[...] = dq * gc
        dw = jnp.concatenate([_colsum(e * ext[6:6 + tb]), _colsum(e * ext[7:7 + tb]), _colsum(e * ext[8:8 + tb]),
                              jnp.zeros((5, GROUP_W), f32)], axis=0)
        _acc(dw_ref, dw)

    def col(c):
        return pl.BlockSpec((tb, GROUP_W), lambda i: (i, c))

    def prev(c):
        return pl.BlockSpec((8, GROUP_W), lambda i: (jnp.maximum(i * (tb // 8) - 1, 0), c))

    def nxt(c):
        return pl.BlockSpec((8, GROUP_W), lambda i: (jnp.minimum((i + 1) * (tb // 8), last8), c))

    out = pl.BlockSpec((tb, GROUP_W), lambda i: (i, 0))
    return pl.pallas_call(
        body, name="sconv_bwd", grid=(nt,),
        out_shape=[jax.ShapeDtypeStruct((t, GROUP_W), f32)] * 3 + [jax.ShapeDtypeStruct((8, GROUP_W), f32)],
        in_specs=[col(C_GB), col(C_GC), col(C_HH), prev(C_GC), prev(C_HH), nxt(C_GB), col(0), nxt(0), _full((8, GROUP_W))],
        out_specs=[out, out, out, _full((8, GROUP_W))],
        compiler_params=_params(("arbitrary",)),
    )(proj, proj, proj, proj, proj, proj, dy, dy, w)


def _conv4(xr, halo, w, bias):
    tb = xr.shape[0]
    ext = jnp.concatenate([halo, xr], axis=0)
    pre = w[0:1] * ext[5:5 + tb] + w[1:2] * ext[6:6 + tb] + w[2:3] * ext[7:7 + tb] + w[3:4] * ext[8:8 + tb] + bias
    return pre, ext


def _tri():
    r = lax.broadcasted_iota(jnp.int32, (SSD_CHUNK, SSD_CHUNK), 0)
    c = lax.broadcasted_iota(jnp.int32, (SSD_CHUNK, SSD_CHUNK), 1)
    return r >= c


def _lane_pick(vals):
    rows = vals[0].shape[0]
    lane = lax.broadcasted_iota(jnp.int32, (rows, LANES), 1)
    out = jnp.zeros((rows, LANES), f32)
    for h, v in enumerate(vals):
        out = jnp.where(lane == h, v, out)
    return out


def _ssd_fwd(proj, conv_w, conv_b, dt_bias, a_log, d_cols, tb):
    t = proj.shape[0]
    cpt = tb // SSD_CHUNK

    def body(z_ref, xs_ref, bm_ref, cm_ref, xsh_ref, bmh_ref, cmh_ref, dt_ref, cw_ref, cb_ref, dtb_ref, al_ref, dk_ref,
             o_ref, y_ref, st_ref, state):
        i = pl.program_id(0)

        @pl.when(i == 0)
        def _():
            state[...] = jnp.zeros_like(state)

        cw, cb = cw_ref[...], cb_ref[...]
        acts = []
        for j, (r, hr) in enumerate(((xs_ref, xsh_ref), (bm_ref, bmh_ref), (cm_ref, cmh_ref))):
            halo = jnp.where(i > 0, hr[...], 0.0)
            pre, _ = _conv4(r[...], halo, cw[:, j * 256:(j + 1) * 256], cb[:, j * 256:(j + 1) * 256])
            acts.append(_silu(pre))
        xs, bm, cm = acts
        dt = _softplus(dt_ref[...] + dtb_ref[...])
        a = -jnp.exp(al_ref[...])
        adt = dt * a
        tri = _tri()
        trif = tri.astype(f32)
        dk = dk_ref[...]
        for c in range(cpt):
            rows = slice(c * SSD_CHUNK, (c + 1) * SSD_CHUNK)
            acol = _dot_exact(trif, adt[rows])
            arow = acol.T
            dt_c = dt[rows]
            ys = []
            for h in range(SSD_HEADS):
                g = h // 2
                ac = acol[:, h:h + 1]
                lm = jnp.exp(jnp.where(tri, ac - arow[h:h + 1, :], -jnp.inf))
                cg = _b(cm[rows, g * 128:(g + 1) * 128])
                bg = _b(bm[rows, g * 128:(g + 1) * 128])
                xh = xs[rows, h * SSD_P:(h + 1) * SSD_P]
                xdt = xh * dt_c[:, h:h + 1]
                m = _dot_nt(cg, bg) * lm
                s_in = state[h]
                st_ref[c, h] = s_in
                y = _dot(_b(m), _b(xdt)) + jnp.exp(ac) * _dot_nt(cg, _b(s_in)) + xh * dk[:, h * SSD_P:(h + 1) * SSD_P]
                ys.append(y)
                alast = ac[SSD_CHUNK - 1:SSD_CHUNK]
                wdec = jnp.exp(alast - ac)
                state[h] = jnp.exp(alast) * s_in + _dot_tn(_b(xdt * wdec), bg)
            yc = jnp.concatenate(ys, axis=1)
            y_ref[rows, :] = yc
            o_ref[rows, :] = yc * _silu(z_ref[rows, :])

    def col(c):
        return pl.BlockSpec((tb, GROUP_W), lambda i: (i, c))

    def prev(c):
        return pl.BlockSpec((8, GROUP_W), lambda i: (jnp.maximum(i * (tb // 8) - 1, 0), c))

    out = pl.BlockSpec((tb, GROUP_W), lambda i: (i, 0))
    return pl.pallas_call(
        body, name="ssd_fwd", grid=(t // tb,),
        out_shape=[jax.ShapeDtypeStruct((t, GROUP_W), f32), jax.ShapeDtypeStruct((t, GROUP_W), f32),
                   jax.ShapeDtypeStruct((t // SSD_CHUNK, SSD_HEADS, SSD_P, 128), f32)],
        in_specs=[col(C_Z), col(C_XS), col(C_BM), col(C_CM), prev(C_XS), prev(C_BM), prev(C_CM),
                  pl.BlockSpec((tb, LANES), lambda i: (i, C_DT128)),
                  _full((8, 768)), _full((1, 768)), _full((1, LANES)), _full((1, LANES)), _full((1, GROUP_W))],
        out_specs=[out, out, pl.BlockSpec((cpt, SSD_HEADS, SSD_P, 128), lambda i: (i, 0, 0, 0))],
        scratch_shapes=[pltpu.VMEM((SSD_HEADS, SSD_P, 128), f32)],
        compiler_params=_params(("arbitrary",)),
    )(proj, proj, proj, proj, proj, proj, proj, proj, conv_w, conv_b, dt_bias, a_log, d_cols)


def _ssd_bwd(proj, dyc, y_pre, states, conv_w, conv_b, dt_bias, a_log, d_cols, tb):
    t = proj.shape[0]
    nt = t // tb
    cpt = tb // SSD_CHUNK

    def body(z_ref, xs_ref, bm_ref, cm_ref, xsh_ref, bmh_ref, cmh_ref, dt_ref, dy_ref, yp_ref, st_ref,
             cw_ref, cb_ref, dtb_ref, al_ref, dk_ref,
             dz_ref, dxs_ref, dbm_ref, dcm_ref, ddt_ref, dcw_ref, dcb_ref, ddtb_ref, dal_ref, ddk_ref,
             dstate, carry):
        i = pl.program_id(0)
        ti = nt - 1 - i

        @pl.when(i == 0)
        def _():
            dstate[...] = jnp.zeros_like(dstate)
            carry[...] = jnp.zeros_like(carry)

        cw, cb = cw_ref[...], cb_ref[...]
        pres, exts, acts = [], [], []
        for j, (r, hr) in enumerate(((xs_ref, xsh_ref), (bm_ref, bmh_ref), (cm_ref, cmh_ref))):
            halo = jnp.where(ti > 0, hr[...], 0.0)
            pre, ext = _conv4(r[...], halo, cw[:, j * 256:(j + 1) * 256], cb[:, j * 256:(j + 1) * 256])
            pres.append(pre)
            exts.append(ext)
            acts.append(_silu(pre))
        xs, bm, cm = acts
        raw = dt_ref[...] + dtb_ref[...]
        dt = _softplus(raw)
        a = -jnp.exp(al_ref[...])
        adt = dt * a
        tri = _tri()
        trif = tri.astype(f32)
        dk = dk_ref[...]
        z = z_ref[...]
        dyc = dy_ref[...]
        dz_ref[...] = dyc * yp_ref[...] * _dsilu(z)
        dy_all = dyc * _silu(z)
        lane = lax.broadcasted_iota(jnp.int32, (1, LANES), 1)
        ddk_acc = jnp.zeros((1, LANES), f32)
        dal_acc = jnp.zeros((1, LANES), f32)
        dxs_c, dbm_c, dcm_c, ddt_c = [None] * cpt, [None] * cpt, [None] * cpt, [None] * cpt
        for c in reversed(range(cpt)):
            rows = slice(c * SSD_CHUNK, (c + 1) * SSD_CHUNK)
            acol = _dot_exact(trif, adt[rows])
            arow = acol.T
            dt_c = dt[rows]
            da_cols, da_rows, ddt_heads, dxs_heads = [], [], [], []
            dbg = [jnp.zeros((SSD_CHUNK, 128), f32), jnp.zeros((SSD_CHUNK, 128), f32)]
            dcg = [jnp.zeros((SSD_CHUNK, 128), f32), jnp.zeros((SSD_CHUNK, 128), f32)]
            for h in range(SSD_HEADS):
                g = h // 2
                ac = acol[:, h:h + 1]
                lm = jnp.exp(jnp.where(tri, ac - arow[h:h + 1, :], -jnp.inf))
                cgf = cm[rows, g * 128:(g + 1) * 128]
                bgf = bm[rows, g * 128:(g + 1) * 128]
                cg, bg = _b(cgf), _b(bgf)
                xh = xs[rows, h * SSD_P:(h + 1) * SSD_P]
                dth = dt_c[:, h:h + 1]
                xdt = xh * dth
                xb = _b(xdt)
                dy = dy_all[rows, h * SSD_P:(h + 1) * SSD_P]
                dyb = _b(dy)
                s_in = st_ref[c, h]
                sb = _b(s_in)
                dsn = dstate[h]
                dsnb = _b(dsn)
                ea = jnp.exp(ac)
                alast = ac[SSD_CHUNK - 1:SSD_CHUNK]
                wdec = jnp.exp(alast - ac)
                el = jnp.exp(alast)
                m = _dot_nt(cg, bg) * lm
                dm = _dot_nt(dyb, xb)
                dx = _dot_tn(_b(m), dyb)
                dg = _b(dm * lm)
                dcg[g] = dcg[g] + _dot(dg, bg)
                dbg[g] = dbg[g] + _dot_tn(dg, cg)
                wm = dm * m
                da = jnp.sum(wm, axis=1, keepdims=True)
                da_rows.append(jnp.sum(wm, axis=0, keepdims=True))
                yoff = ea * _dot_nt(cg, sb)
                da = da + jnp.sum(dy * yoff, axis=1, keepdims=True)
                dye = _b(dy * ea)
                dcg[g] = dcg[g] + _dot(dye, sb)
                ds_y = _dot_tn(dye, cg)
                t1 = _dot(xb, dsnb)
                dbg[g] = dbg[g] + wdec * t1
                dwv = jnp.sum(t1 * bgf, axis=1, keepdims=True) * wdec
                dx = dx + _dot_nt(_b(bgf * wdec), dsnb)
                dalast = jnp.sum(dwv, axis=0, keepdims=True) + el * jnp.sum(jnp.sum(dsn * s_in, axis=1, keepdims=True), axis=0, keepdims=True)
                da = da - dwv
                rowi = lax.broadcasted_iota(jnp.int32, (SSD_CHUNK, 1), 0)
                da = da + jnp.where(rowi == SSD_CHUNK - 1, dalast, 0.0)
                dstate[h] = el * dsn + ds_y
                da_cols.append(da)
                ddt_heads.append(jnp.sum(dx * xh, axis=1, keepdims=True))
                dkh = dk[:, h * SSD_P:(h + 1) * SSD_P]
                dxs_heads.append(dx * dth + dy * dkh)
                ddk_acc = ddk_acc + jnp.where(lane == h, jnp.sum(_colsum(dy * xh), axis=1, keepdims=True), 0.0)
            da_blk = _lane_pick(da_cols)
            rowsel = lax.broadcasted_iota(jnp.int32, (SSD_CHUNK, SSD_CHUNK), 0)
            da_rows_blk = jnp.zeros((SSD_CHUNK, SSD_CHUNK), f32)
            for h in range(SSD_HEADS):
                da_rows_blk = jnp.where(rowsel == h, da_rows[h], da_rows_blk)
            da_blk = da_blk - da_rows_blk.T
            dadt = lax.dot_general(trif, da_blk, (((0,), (0,)), ((), ())), preferred_element_type=f32,
                                   precision=lax.Precision.HIGHEST)
            dal_acc = dal_acc + _colsum(dadt * dt_c)
            ddt_c[c] = dadt * a + _lane_pick(ddt_heads)
            dxs_c[c] = jnp.concatenate(dxs_heads, axis=1)
            dbm_c[c] = jnp.concatenate(dbg, axis=1)
            dcm_c[c] = jnp.concatenate(dcg, axis=1)
        ddt = jnp.concatenate(ddt_c, axis=0) if cpt > 1 else ddt_c[0]
        ddraw = jnp.where(lane < SSD_HEADS, ddt * jax.nn.sigmoid(raw), 0.0)
        ddt_ref[...] = ddraw
        _acc(ddtb_ref, _colsum(ddraw))
        _acc(dal_ref, jnp.where(lane < SSD_HEADS, dal_acc * a, 0.0))
        _acc(ddk_ref, ddk_acc)
        dcw_parts, dcb_parts = [], []
        for j, (dparts, out_ref) in enumerate(((dxs_c, dxs_ref), (dbm_c, dbm_ref), (dcm_c, dcm_ref))):
            dact = jnp.concatenate(dparts, axis=0) if cpt > 1 else dparts[0]
            dpre = dact * _dsilu(pres[j])
            w = cw[:, j * 256:(j + 1) * 256]
            ext = jnp.concatenate([dpre, carry[:, j * 256:(j + 1) * 256]], axis=0)
            out_ref[...] = w[3:4] * ext[0:tb] + w[2:3] * ext[1:1 + tb] + w[1:2] * ext[2:2 + tb] + w[0:1] * ext[3:3 + tb]
            carry[:, j * 256:(j + 1) * 256] = dpre[0:8]
            xe = exts[j]
            dcw_parts.append(jnp.concatenate([_colsum(dpre * xe[5 + k:5 + k + tb]) for k in range(4)]
                                             + [jnp.zeros((4, GROUP_W), f32)], axis=0))
            dcb_parts.append(_colsum(dpre))
        _acc(dcw_ref, jnp.concatenate(dcw_parts, axis=1))
        _acc(dcb_ref, jnp.concatenate(dcb_parts, axis=1))

    def col(c):
        return pl.BlockSpec((tb, GROUP_W), lambda i: (nt - 1 - i, c))

    def prev(c):
        return pl.BlockSpec((8, GROUP_W), lambda i: (jnp.maximum((nt - 1 - i) * (tb // 8) - 1, 0), c))

    out = pl.BlockSpec((tb, GROUP_W), lambda i: (nt - 1 - i, 0))
    vec = _full((1, LANES))
    return pl.pallas_call(
        body, name="ssd_bwd", grid=(nt,),
        out_shape=[jax.ShapeDtypeStruct((t, GROUP_W), f32)] * 4 + [jax.ShapeDtypeStruct((t, LANES), f32),
                   jax.ShapeDtypeStruct((8, 768), f32), jax.ShapeDtypeStruct((1, 768), f32)]
        + [jax.ShapeDtypeStruct((1, LANES), f32)] * 3,
        in_specs=[col(C_Z), col(C_XS), col(C_BM), col(C_CM), prev(C_XS), prev(C_BM), prev(C_CM),
                  pl.BlockSpec((tb, LANES), lambda i: (nt - 1 - i, C_DT128)), out, out,
                  pl.BlockSpec((cpt, SSD_HEADS, SSD_P, 128), lambda i: (nt - 1 - i, 0, 0, 0)),
                  _full((8, 768)), _full((1, 768)), vec, vec, _full((1, GROUP_W))],
        out_specs=[out, out, out, out, pl.BlockSpec((tb, LANES), lambda i: (nt - 1 - i, 0)),
                   _full((8, 768)), _full((1, 768)), vec, vec, vec],
        scratch_shapes=[pltpu.VMEM((SSD_HEADS, SSD_P, 128), f32), pltpu.VMEM((8, 768), f32)],
        compiler_params=_params(("arbitrary",)),
    )(proj, proj, proj, proj, proj, proj, proj, proj, dyc, y_pre, states, conv_w, conv_b, dt_bias, a_log, d_cols)


def _s5_coeffs(are, aim, ls):
    step = jnp.exp(ls)
    mag = jnp.exp(are * step)
    th = aim * step
    lre, lim = mag * jnp.cos(th), mag * jnp.sin(th)
    den = are * are + aim * aim
    nr = lre - 1.0
    fre = (nr * are + lim * aim) / den
    fim = (lim * are - nr * aim) / den
    return step, lre, lim, den, fre, fim


def _s5_prep(are, aim, ls, bre_bd, bim_bd):
    def body(are_ref, aim_ref, ls_ref, bre_ref, bim_ref, lre_ref, lim_ref, bbr_ref, bbi_ref):
        _, lre, lim, _, fre, fim = _s5_coeffs(are_ref[...], aim_ref[...], ls_ref[...])
        lre_ref[...] = lre
        lim_ref[...] = lim
        bre, bim = bre_ref[...], bim_ref[...]
        bbr_ref[...] = fre * bre - fim * bim
        bbi_ref[...] = fre * bim + fim * bre

    col = jax.ShapeDtypeStruct((S5_N, 1), f32)
    mat = jax.ShapeDtypeStruct((S5_N, GROUP_W), f32)
    return pl.pallas_call(body, name="s5_prep", out_shape=[col, col, mat, mat], in_specs=[VMEM] * 5, out_specs=[VMEM] * 4,
                          compiler_params=_params())(are, aim, ls, bre_bd, bim_bd)


def _s5_prep_bwd(are, aim, ls, bre_bd, bim_bd, dlre, dlim, dbbr, dbbi):
    def body(are_ref, aim_ref, ls_ref, bre_ref, bim_ref, dlre_ref, dlim_ref, dbbr_ref, dbbi_ref,
             dare_ref, daim_ref, dls_ref, dbre_ref, dbim_ref):
        are, aim = are_ref[...], aim_ref[...]
        step, lre, lim, den, fre, fim = _s5_coeffs(are, aim, ls_ref[...])
        r = lax.broadcasted_iota(jnp.int32, (S5_N, GROUP_W), 0) // 64
        c = lax.broadcasted_iota(jnp.int32, (S5_N, GROUP_W), 1) // 16
        mask = r == c
        gr = jnp.where(mask, dbbr_ref[...], 0.0)
        gi = jnp.where(mask, dbbi_ref[...], 0.0)
        bre, bim = bre_ref[...], bim_ref[...]
        dbre_ref[...] = fre * gr + fim * gi
        dbim_ref[...] = fre * gi - fim * gr
        dfre = jnp.sum(bre * gr + bim * gi, axis=1, keepdims=True)
        dfim = jnp.sum(bre * gi - bim * gr, axis=1, keepdims=True)
        ire, iim = are / den, aim / den
        tre = dlre_ref[...] + ire * dfre - iim * dfim
        tim = dlim_ref[...] + ire * dfim + iim * dfre
        dzre = lre * tre + lim * tim
        dzim = lre * tim - lim * tre
        qre = (fre * are + fim * aim) / den
        qim = (fim * are - fre * aim) / den
        dare_ref[...] = step * dzre - (qre * dfre + qim * dfim)
        daim_ref[...] = step * dzim - (qre * dfim - qim * dfre)
        dls = (are * dzre + aim * dzim) * step
        sel = (lax.broadcasted_iota(jnp.int32, (S5_N, LANES), 0) // 64 == lax.broadcasted_iota(jnp.int32, (S5_N, LANES), 1)).astype(f32)
        dls_ref[...] = lax.dot_general(sel, jnp.broadcast_to(dls, (S5_N, LANES)), (((0,), (0,)), ((), ())),
                                       preferred_element_type=f32, precision=lax.Precision.HIGHEST)

    col = jax.ShapeDtypeStruct((S5_N, 1), f32)
    mat = jax.ShapeDtypeStruct((S5_N, GROUP_W), f32)
    return pl.pallas_call(body, name="s5_prep_bwd", out_shape=[col, col, jax.ShapeDtypeStruct((LANES, LANES), f32), mat, mat],
                          in_specs=[VMEM] * 9, out_specs=[VMEM] * 5, compiler_params=_params(),
                          )(are, aim, ls, bre_bd, bim_bd, dlre, dlim, dbbr, dbbi)


def _cmul(ar, ai, br, bi):
    return ar * br - ai * bi, ar * bi + ai * br


def _s5_scan(re_ref, im_ref, carry_ref, mr, mi, n_groups, reverse):
    p1 = (mr, mi)
    p2 = _cmul(*p1, *p1)
    p3 = _cmul(*p2, *p1)
    p4 = _cmul(*p2, *p2)
    p5 = _cmul(*p4, *p1)
    p6 = _cmul(*p4, *p2)
    p7 = _cmul(*p4, *p3)
    p8 = _cmul(*p4, *p4)
    pows = [p1, p2, p3, p4, p5, p6, p7, p8]
    row = lax.broadcasted_iota(jnp.int32, (8, S5_N), 0)
    tr = jnp.zeros((8, S5_N), f32)
    ti = jnp.zeros((8, S5_N), f32)
    for i in range(8):
        p = pows[7 - i] if reverse else pows[i]
        tr = jnp.where(row == i, p[0], tr)
        ti = jnp.where(row == i, p[1], ti)
    steps = []
    for k, p in ((1, p1), (2, p2), (4, p4)):
        keep = (row + k < 8) if reverse else (row >= k)
        steps.append((8 - k if reverse else k, keep, jnp.broadcast_to(p[0], (8, S5_N)), jnp.broadcast_to(p[1], (8, S5_N))))
    edge = 0 if reverse else 7

    def step(j, carry):
        cr, ci = carry
        g = (n_groups - 1 - j) if reverse else j
        r0 = pl.multiple_of(g * 8, 8)
        xr = re_ref[pl.ds(r0, 8), :]
        xi = im_ref[pl.ds(r0, 8), :]
        for shift, keep, br, bi in steps:
            sr = jnp.where(keep, pltpu.roll(xr, shift, 0), 0.0)
            si = jnp.where(keep, pltpu.roll(xi, shift, 0), 0.0)
            xr, xi = xr + br * sr - bi * si, xi + br * si + bi * sr
        xr, xi = xr + tr * cr - ti * ci, xi + tr * ci + ti * cr
        re_ref[pl.ds(r0, 8), :] = xr
        im_ref[pl.ds(r0, 8), :] = xi
        return (jnp.broadcast_to(xr[edge:edge + 1, :], (8, S5_N)), jnp.broadcast_to(xi[edge:edge + 1, :], (8, S5_N)))

    cr, ci = lax.fori_loop(0, n_groups, step, (carry_ref[0], carry_ref[1]))
    carry_ref[0] = cr
    carry_ref[1] = ci


def _s5_output(u, xr, xi, ctr, cti, d):
    return _dot_nt(_b(xr), _b(ctr)) - _dot_nt(_b(xi), _b(cti)) + d * u


def _s5_fwd(proj, bbr, bbi, ctr, cti, lre, lim, d, glu_w, glu_b, tb):
    t = proj.shape[0]

    def body(u_ref, bbr_ref, bbi_ref, ctr_ref, cti_ref, lr_ref, li_ref, d_ref, gw_ref, gb_ref, o_ref, xr_ref, xi_ref, carry):
        @pl.when(pl.program_id(0) == 0)
        def _():
            carry[...] = jnp.zeros_like(carry)

        u = u_ref[...]
        ub = _b(u)
        xr_ref[...] = _dot_nt(ub, _b(bbr_ref[...]))
        xi_ref[...] = _dot_nt(ub, _b(bbi_ref[...]))
        _s5_scan(xr_ref, xi_ref, carry, lr_ref[...], li_ref[...], tb // 8, reverse=False)
        y = _s5_output(u, xr_ref[...], xi_ref[...], ctr_ref[...], cti_ref[...], d_ref[...])
        gl = _gelu(y)
        o_ref[...] = gl * jax.nn.sigmoid(_dot(_b(gl), _b(gw_ref[...])) + gb_ref[...])

    state = pl.BlockSpec((tb, S5_N), lambda i: (i, 0))
    return pl.pallas_call(
        body, name="s5_fwd", grid=(t // tb,),
        out_shape=[jax.ShapeDtypeStruct((t, GROUP_W), f32), jax.ShapeDtypeStruct((t, S5_N), f32), jax.ShapeDtypeStruct((t, S5_N), f32)],
        in_specs=[pl.BlockSpec((tb, GROUP_W), lambda i: (i, C_S5)), _full((S5_N, GROUP_W)), _full((S5_N, GROUP_W)),
                  _full((GROUP_W, S5_N)), _full((GROUP_W, S5_N)), _full((1, S5_N)), _full((1, S5_N)),
                  _full((1, GROUP_W)), _full((GROUP_W, GROUP_W)), _full((1, GROUP_W))],
        out_specs=[pl.BlockSpec((tb, GROUP_W), lambda i: (i, 0)), state, state],
        scratch_shapes=[pltpu.VMEM((2, 8, S5_N), f32)],
        compiler_params=_params(("arbitrary",)),
    )(proj, bbr, bbi, ctr, cti, lre, lim, d, glu_w, glu_b)


def _s5_bwd(proj, dyd, xr_all, xi_all, bbr, bbi, ctr, cti, lre, lim, d, glu_w, glu_b, tb):
    t = proj.shape[0]
    nt = t // tb

    def body(u_ref, dy_ref, xr_ref, xi_ref, xrh_ref, xih_ref, bbr_ref, bbi_ref, ctr_ref, cti_ref, lr_ref, li_ref,
             d_ref, gw_ref, gb_ref,
             du_ref, dlr_ref, dli_ref, dbbr_ref, dbbi_ref, dctr_ref, dcti_ref, dd_ref, dgw_ref, dgb_ref,
             gr_ref, gi_ref, carry):
        i = pl.program_id(0)
        ti = nt - 1 - i

        @pl.when(i == 0)
        def _():
            carry[...] = jnp.zeros_like(carry)

        u = u_ref[...]
        ub = _b(u)
        xr, xi = xr_ref[...], xi_ref[...]
        ctr, cti = _b(ctr_ref[...]), _b(cti_ref[...])
        d = d_ref[...]
        gw = _b(gw_ref[...])
        y = _s5_output(u, xr, xi, ctr, cti, d)
        gl = _gelu(y)
        sg = jax.nn.sigmoid(_dot(_b(gl), gw) + gb_ref[...])
        dout = dy_ref[...]
        q = dout * gl * sg * (1.0 - sg)
        qb = _b(q)
        dgl = dout * sg + _dot_nt(qb, gw)
        _acc(dgw_ref, _dot_tn(_b(gl), qb))
        _acc(dgb_ref, _colsum(q))
        dyv = dgl * _dgelu(y)
        _acc(dd_ref, _colsum(dyv * u))
        dyb = _b(dyv)
        gr_ref[...] = _dot(dyb, ctr)
        gi_ref[...] = -_dot(dyb, cti)
        _acc(dctr_ref, _dot_tn(dyb, _b(xr)))
        _acc(dcti_ref, -_dot_tn(dyb, _b(xi)))
        _s5_scan(gr_ref, gi_ref, carry, lr_ref[...], -li_ref[...], tb // 8, reverse=True)
        gr, gi = gr_ref[...], gi_ref[...]
        xpr = jnp.concatenate([jnp.where(ti > 0, xrh_ref[...], 0.0), xr], axis=0)[7:7 + tb]
        xpi = jnp.concatenate([jnp.where(ti > 0, xih_ref[...], 0.0), xi], axis=0)[7:7 + tb]
        _acc(dlr_ref, _colsum(gr * xpr + gi * xpi))
        _acc(dli_ref, _colsum(gi * xpr - gr * xpi))
        grb, gib = _b(gr), _b(gi)
        _acc(dbbr_ref, _dot_tn(grb, ub))
        _acc(dbbi_ref, _dot_tn(gib, ub))
        du_ref[...] = dyv * d + _dot(grb, _b(bbr_ref[...])) + _dot(gib, _b(bbi_ref[...]))

    state = pl.BlockSpec((tb, S5_N), lambda i: (nt - 1 - i, 0))
    prev = pl.BlockSpec((8, S5_N), lambda i: (jnp.maximum((nt - 1 - i) * (tb // 8) - 1, 0), 0))
    tile = pl.BlockSpec((tb, GROUP_W), lambda i: (nt - 1 - i, 0))
    return pl.pallas_call(
        body, name="s5_bwd", grid=(nt,),
        out_shape=[jax.ShapeDtypeStruct((t, GROUP_W), f32), jax.ShapeDtypeStruct((1, S5_N), f32), jax.ShapeDtypeStruct((1, S5_N), f32),
                   jax.ShapeDtypeStruct((S5_N, GROUP_W), f32), jax.ShapeDtypeStruct((S5_N, GROUP_W), f32),
                   jax.ShapeDtypeStruct((GROUP_W, S5_N), f32), jax.ShapeDtypeStruct((GROUP_W, S5_N), f32),
                   jax.ShapeDtypeStruct((1, GROUP_W), f32), jax.ShapeDtypeStruct((GROUP_W, GROUP_W), f32),
                   jax.ShapeDtypeStruct((1, GROUP_W), f32)],
        in_specs=[pl.BlockSpec((tb, GROUP_W), lambda i: (nt - 1 - i, C_S5)), tile, state, state, prev, prev,
                  _full((S5_N, GROUP_W)), _full((S5_N, GROUP_W)), _full((GROUP_W, S5_N)), _full((GROUP_W, S5_N)),
                  _full((1, S5_N)), _full((1, S5_N)), _full((1, GROUP_W)), _full((GROUP_W, GROUP_W)), _full((1, GROUP_W))],
        out_specs=[tile, _full((1, S5_N)), _full((1, S5_N)), _full((S5_N, GROUP_W)), _full((S5_N, GROUP_W)),
                   _full((GROUP_W, S5_N)), _full((GROUP_W, S5_N)), _full((1, GROUP_W)), _full((GROUP_W, GROUP_W)), _full((1, GROUP_W))],
        scratch_shapes=[pltpu.VMEM((tb, S5_N), f32), pltpu.VMEM((tb, S5_N), f32), pltpu.VMEM((2, 8, S5_N), f32)],
        compiler_params=_params(("arbitrary",)),
    )(proj, dyd, xr_all, xi_all, xr_all, xi_all, bbr, bbi, ctr, cti, lre, lim, d, glu_w, glu_b)


def _outproj_fwd(ys, h, bn_w, g1, w_out, tb):
    t = h.shape[0]

    def body(ya_ref, yb_ref, yc_ref, yd_ref, h_ref, bn_ref, g1_ref, w_ref, h1_ref, o_ref, gr_ref):
        bn = bn_ref[...]
        parts = []
        for g, r in enumerate((ya_ref, yb_ref, yc_ref, yd_ref)):
            n, _ = _rms(r[...])
            parts.append(n * bn[:, g * GROUP_W:(g + 1) * GROUP_W])
        groups = _b(jnp.concatenate(parts, axis=1))
        gr_ref[...] = groups
        o = _dot(groups, w_ref[...])
        o_ref[...] = o
        h1_ref[...] = h_ref[...] + g1_ref[...] * o

    grp = pl.BlockSpec((tb, GROUP_W), lambda i: (i, 0))
    row = pl.BlockSpec((tb, D_MODEL), lambda i: (i, 0))
    vec = _full((1, D_MODEL))
    return pl.pallas_call(
        body, name="outproj_fwd", grid=(t // tb,),
        out_shape=[jax.ShapeDtypeStruct((t, D_MODEL), f32), jax.ShapeDtypeStruct((t, D_MODEL), f32),
                   jax.ShapeDtypeStruct((t, D_MODEL), bf16)],
        in_specs=[grp, grp, grp, grp, row, vec, vec, _full((D_MODEL, D_MODEL))],
        out_specs=[row, row, row],
        compiler_params=_params(("parallel",)),
    )(*ys, h, bn_w, g1, w_out)


def _outproj_bwd(dh1, o, ys, bn_w, g1, w_out, tb):
    t = dh1.shape[0]

    def body(dh_ref, o_ref, ya_ref, yb_ref, yc_ref, yd_ref, bn_ref, g1_ref, w_ref,
             da_ref, db_ref, dc_ref, dd_ref, do_ref, dg1_ref, dbn_ref):
        dh = dh_ref[...]
        _acc(dg1_ref, _colsum(dh * o_ref[...]))
        do = _b(dh * g1_ref[...])
        do_ref[...] = do
        dgroups = _dot_nt(do, w_ref[...])
        bn = bn_ref[...]
        dbn = []
        for g, (r, dr) in enumerate(((ya_ref, da_ref), (yb_ref, db_ref), (yc_ref, dc_ref), (yd_ref, dd_ref))):
            n, rr = _rms(r[...])
            dgr = dgroups[:, g * GROUP_W:(g + 1) * GROUP_W]
            dbn.append(_colsum(dgr * n))
            dr[...] = _rms_bwd(dgr * bn[:, g * GROUP_W:(g + 1) * GROUP_W], n, rr)
        _acc(dbn_ref, jnp.concatenate(dbn, axis=1))

    grp = pl.BlockSpec((tb, GROUP_W), lambda i: (i, 0))
    row = pl.BlockSpec((tb, D_MODEL), lambda i: (i, 0))
    vec = _full((1, D_MODEL))
    return pl.pallas_call(
        body, name="outproj_bwd", grid=(t // tb,),
        out_shape=[jax.ShapeDtypeStruct((t, GROUP_W), f32)] * 4 + [jax.ShapeDtypeStruct((t, D_MODEL), bf16),
                   jax.ShapeDtypeStruct((1, D_MODEL), f32), jax.ShapeDtypeStruct((1, D_MODEL), f32)],
        in_specs=[row, row, grp, grp, grp, grp, vec, vec, _full((D_MODEL, D_MODEL))],
        out_specs=[grp, grp, grp, grp, row, vec, vec],
        compiler_params=_params(("arbitrary",)),
    )(dh1, o, *ys, bn_w, g1, w_out)


def _mlp_fwd(h1, norm_w, sc, sh, g2, w1, w2, tb):
    t = h1.shape[0]
    nh = w1.shape[0]

    def body(h_ref, nw_ref, sc_ref, sh_ref, g2_ref, w1_ref, w2_ref, h2_ref, m_ref, a_ref, v_ref, r_ref, acc):
        j = pl.program_id(1)

        @pl.when(j == 0)
        def _():
            n, _ = _rms(h_ref[...])
            v_ref[...] = _b(n * nw_ref[...] * (1.0 + sc_ref[...]) + sh_ref[...])

        a = _dot(v_ref[...], w1_ref[0])
        a_ref[...] = a
        ra = jnp.maximum(a, 0.0)
        r = _b(ra * ra)
        r_ref[...] = r
        p = _dot(r, w2_ref[0])

        @pl.when(j == 0)
        def _():
            acc[...] = p

        @pl.when(j != 0)
        def _():
            acc[...] += p

        @pl.when(j == nh - 1)
        def _():
            m = acc[...]
            m_ref[...] = m
            h2_ref[...] = h_ref[...] + g2_ref[...] * m

    row = pl.BlockSpec((tb, D_MODEL), lambda i, j: (i, 0))
    hid = pl.BlockSpec((tb, MLP_HB), lambda i, j: (i, j))
    vec = _full((1, D_MODEL))
    return pl.pallas_call(
        body, name="mlp_fwd", grid=(t // tb, nh),
        out_shape=[jax.ShapeDtypeStruct((t, D_MODEL), f32), jax.ShapeDtypeStruct((t, D_MODEL), f32),
                   jax.ShapeDtypeStruct((t, nh * MLP_HB), f32), jax.ShapeDtypeStruct((t, D_MODEL), bf16),
                   jax.ShapeDtypeStruct((t, nh * MLP_HB), bf16)],
        in_specs=[row, vec, vec, vec, vec, pl.BlockSpec((1, D_MODEL, MLP_HB), lambda i, j: (j, 0, 0)),
                  pl.BlockSpec((1, MLP_HB, D_MODEL), lambda i, j: (j, 0, 0))],
        out_specs=[row, row, hid, row, hid],
        scratch_shapes=[pltpu.VMEM((tb, D_MODEL), f32)],
        compiler_params=_params(("arbitrary", "arbitrary")),
    )(h1, norm_w, sc, sh, g2, w1, w2)


def _mlp_bwd(dh2, m, h1, a, norm_w, sc, sh, g2, w1, w2, tb):
    t = h1.shape[0]
    nh = w1.shape[0]

    def body(dh_ref, m_ref, h_ref, a_ref, nw_ref, sc_ref, sh_ref, g2_ref, w1_ref, w2_ref,
             dh1_ref, do_ref, da_ref, dg2_ref, dsh_ref, dsc_ref, dnw_ref, acc):
        i, j = pl.program_id(0), pl.program_id(1)

        @pl.when(j == 0)
        def _():
            dh = dh_ref[...]
            _acc(dg2_ref, _colsum(dh * m_ref[...]))
            do_ref[...] = _b(dh * g2_ref[...])

        dr = _dot_nt(do_ref[...], w2_ref[0])
        da = _b(dr * 2.0 * jnp.maximum(a_ref[...], 0.0))
        da_ref[...] = da
        p = _dot_nt(da, w1_ref[0])

        @pl.when(j == 0)
        def _():
            acc[...] = p

        @pl.when(j != 0)
        def _():
            acc[...] += p

        @pl.when(j == nh - 1)
        def _():
            dv = acc[...]
            n, r = _rms(h_ref[...])
            nw = nw_ref[...]
            gain = 1.0 + sc_ref[...]
            _acc(dsh_ref, _colsum(dv))
            _acc(dsc_ref, _colsum(dv * n * nw))
            _acc(dnw_ref, _colsum(dv * gain * n))
            dh1_ref[...] = dh_ref[...] + _rms_bwd(dv * nw * gain, n, r)

    row = pl.BlockSpec((tb, D_MODEL), lambda i, j: (i, 0))
    hid = pl.BlockSpec((tb, MLP_HB), lambda i, j: (i, j))
    vec = _full((1, D_MODEL))
    return pl.pallas_call(
        body, name="mlp_bwd", grid=(t // tb, nh),
        out_shape=[jax.ShapeDtypeStruct((t, D_MODEL), f32), jax.ShapeDtypeStruct((t, D_MODEL), bf16),
                   jax.ShapeDtypeStruct((t, nh * MLP_HB), bf16)] + [jax.ShapeDtypeStruct((1, D_MODEL), f32)] * 4,
        in_specs=[row, row, row, hid, vec, vec, vec, vec, pl.BlockSpec((1, D_MODEL, MLP_HB), lambda i, j: (j, 0, 0)),
                  pl.BlockSpec((1, MLP_HB, D_MODEL), lambda i, j: (j, 0, 0))],
        out_specs=[row, row, hid, vec, vec, vec, vec],
        scratch_shapes=[pltpu.VMEM((tb, D_MODEL), f32)],
        compiler_params=_params(("arbitrary", "arbitrary")),
    )(dh2, m, h1, a, norm_w, sc, sh, g2, w1, w2)


def _loss_head(h, target, norm_w, tb):
    t = h.shape[0]

    def body(h_ref, t_ref, w_ref, loss_ref, dh_ref, dw_ref):
        n, r = _rms(h_ref[...])
        w = w_ref[...]
        err = n * w - t_ref[...]
        part = 0.5 * jnp.sum(jnp.sum(err * err, axis=1, keepdims=True), axis=0, keepdims=True) / D_MODEL
        _acc(loss_ref, jnp.broadcast_to(part, (8, LANES)))
        dy = err / D_MODEL
        _acc(dw_ref, _colsum(dy * n))
        dh_ref[...] = _rms_bwd(dy * w, n, r)

    row = pl.BlockSpec((tb, D_MODEL), lambda i: (i, 0))
    return pl.pallas_call(
        body, name="loss_head", grid=(t // tb,),
        out_shape=[jax.ShapeDtypeStruct((8, LANES), f32), jax.ShapeDtypeStruct((t, D_MODEL), f32),
                   jax.ShapeDtypeStruct((1, D_MODEL), f32)],
        in_specs=[row, row, _full((1, D_MODEL))],
        out_specs=[_full((8, LANES)), row, _full((1, D_MODEL))],
        compiler_params=_params(("arbitrary",)),
    )(h, target, norm_w)


def _adam_math(w, g, m, v):
    m2 = ADAM_B1 * m + (1.0 - ADAM_B1) * g
    v2 = ADAM_B2 * v + (1.0 - ADAM_B2) * (g * g)
    mh = m2 / (1.0 - ADAM_B1 ** ADAM_STEP)
    vh = v2 / (1.0 - ADAM_B2 ** ADAM_STEP)
    return -ADAM_LR * (mh / (jnp.sqrt(vh) + ADAM_EPS) + ADAM_WD * w), m2, v2


def _sum_adamw(parts, w, m, v, name, rb):
    n_src, r, c = parts.shape

    def body(p_ref, w_ref, m_ref, v_ref, g_ref, d_ref, m2_ref, v2_ref):
        g = p_ref[0].astype(f32)
        for s in range(1, n_src):
            g = g + p_ref[s].astype(f32)
        g_ref[...] = g
        d, m2, v2 = _adam_math(w_ref[...], g, m_ref[...], v_ref[...])
        d_ref[...] = d
        m2_ref[...] = m2
        v2_ref[...] = v2

    blk = pl.BlockSpec((rb, c), lambda i: (i, 0))
    return pl.pallas_call(
        body, name=name, grid=(r // rb,),
        out_shape=[jax.ShapeDtypeStruct((r, c), f32)] * 4,
        in_specs=[pl.BlockSpec((n_src, rb, c), lambda i: (0, i, 0)), blk, blk, blk],
        out_specs=[blk] * 4,
        compiler_params=_params(("parallel",)),
    )(parts, w, m, v)


def _sum_rows(parts):
    n_src, r, c = parts.shape

    def body(p_ref, o_ref):
        g = p_ref[0]
        for s in range(1, n_src):
            g = g + p_ref[s]
        o_ref[...] = g

    return pl.pallas_call(body, name="sum_small", out_shape=jax.ShapeDtypeStruct((r, c), f32),
                          in_specs=[VMEM], out_specs=VMEM, compiler_params=_params())(parts)


def _sum_adamw_layers(parts, w, m, v, name, rb):
    _, n_src, r, c = parts.shape

    def body(p_ref, w_ref, m_ref, v_ref, g_ref, d_ref, m2_ref, v2_ref):
        g = p_ref[0, 0].astype(f32)
        for s in range(1, n_src):
            g = g + p_ref[0, s].astype(f32)
        g_ref[0] = g
        d, m2, v2 = _adam_math(w_ref[0], g, m_ref[0], v_ref[0])
        d_ref[0] = d
        m2_ref[0] = m2
        v2_ref[0] = v2

    blk = pl.BlockSpec((1, rb, c), lambda l, i: (l, i, 0))
    return pl.pallas_call(
        body, name=name, grid=(2, r // rb),
        out_shape=[jax.ShapeDtypeStruct((2, r, c), f32)] * 4,
        in_specs=[pl.BlockSpec((1, n_src, rb, c), lambda l, i: (l, 0, i, 0)), blk, blk, blk],
        out_specs=[blk] * 4,
        compiler_params=_params(("parallel", "parallel")),
    )(parts, w, m, v)


def _reorder_in(w):
    pad = jnp.zeros(w.shape[:-1] + (P_IN - 2308,), w.dtype)
    return jnp.concatenate([w[..., :2048], w[..., 2052:2308], w[..., 2048:2052], pad], axis=-1)


def _unreorder_in(w):
    return jnp.concatenate([w[..., :2048], w[..., 2304:2308], w[..., 2048:2304]], axis=-1)


def _block_diag(w2d, n_blocks):
    rows, cols = w2d.shape
    tiled = jnp.tile(w2d, (1, n_blocks))
    rb = lax.broadcasted_iota(jnp.int32, tiled.shape, 0) // (rows // n_blocks)
    cb = lax.broadcasted_iota(jnp.int32, tiled.shape, 1) // cols
    return jnp.where(rb == cb, tiled, jnp.zeros_like(tiled))


def _block_diag_extract(w_bd, n_blocks):
    rows, wide = w_bd.shape
    r, c = rows // n_blocks, wide // n_blocks
    w4 = w_bd.reshape(n_blocks, r, n_blocks, c)
    idx = jnp.arange(n_blocks)
    return w4[idx, :, idx, :]


def _lanes128(v):
    return jnp.pad(v.reshape(1, -1), ((0, 0), (0, LANES - v.size)))


def _flat_pack(arrs):
    flat = jnp.concatenate([a.reshape(-1) for a in arrs])
    n = flat.shape[0]
    rows = -(-n // 1024) * 8
    return jnp.pad(flat, (0, rows * LANES - n)).reshape(rows, LANES)


def _flat_unpack(packed, shapes):
    flat = packed.reshape(-1)
    out, off = [], 0
    for s in shapes:
        n = 1
        for d in s:
            n *= d
        out.append(flat[off:off + n].reshape(s))
        off += n
    return out


_W_NAMES = ['norm_mix_w', 'norm_mlp_w', 'ada_w', 'ada_b', 'w_in', 'pool_w', 'pool_scale', 'sconv_w', 'ssd_conv_w',
            'ssd_conv_b', 'ssd_dt_bias', 'ssd_a_log', 'ssd_d', 's5_a_re', 's5_a_im', 's5_log_step', 's5_b_re', 's5_b_im',
            's5_c_re', 's5_c_im', 's5_d', 's5_glu_w', 's5_glu_b', 'branch_norm_w', 'w_out', 'mlp_w1', 'mlp_w2',
            'final_norm_w']
_BIG = ('ada_w', 'w_in', 'w_out', 'mlp_w1', 'mlp_w2')
_SMALL = [n for n in _W_NAMES if n not in _BIG]
_SHARDED_SMALL = {'sconv_w': (2, 32), 'ssd_conv_w': (2, 96), 's5_glu_w': (1, 32)}


def _layer_forward(l, h, p, tb):
    proj, u_b = _inproj_fwd(h, p['norm_mix_w'][l], p['sc1'][l], p['sh1'][l], p['w_in'][l], tb)
    ya = _pool_fwd(proj, p['pool_bd'][l], p['pool_scale'][l], tb)
    yb = _sconv_fwd(proj, p['sconv_w8'][l], tb)
    yc, yc_pre, states = _ssd_fwd(proj, p['ssd_conv_w8'][l], p['ssd_conv_b'][l], p['ssd_dt_bias'][l], p['ssd_a_log'][l],
                                  p['ssd_d_cols'][l], tb)
    yd, xr, xi = _s5_fwd(proj, p['bbr'][l], p['bbi'][l], p['ctr'][l], p['cti'][l], p['lre'][l], p['lim'][l],
                         p['s5_d'][l], p['glu_w'][l], p['glu_b'][l], tb)
    ys = (ya, yb, yc, yd)
    h1, o, groups_b = _outproj_fwd(ys, h, p['branch_norm_w'][l], p['g1'][l], p['w_out'][l], tb)
    h2, m, a, v_b, r_b = _mlp_fwd(h1, p['norm_mlp_w'][l], p['sc2'][l], p['sh2'][l], p['g2'][l], p['w1'][l], p['w2'][l], tb)
    saved = dict(h=h, proj=proj, u_b=u_b, ys=ys, yc_pre=yc_pre, states=states, xr=xr, xi=xi, h1=h1, o=o,
                 groups_b=groups_b, m=m, a=a, v_b=v_b, r_b=r_b)
    return h2, saved


def _layer_backward(l, dh2, s, p, tb):
    dh1, do2_b, da_b, dg2, dsh2, dsc2, dnw_mlp = _mlp_bwd(dh2, s['m'], s['h1'], s['a'], p['norm_mlp_w'][l], p['sc2'][l],
                                                         p['sh2'][l], p['g2'][l], p['w1'][l], p['w2'][l], tb)
    dw2 = _wgrad(s['r_b'], do2_b, 1, "wgrad_w2", tm=512).reshape(N_DEV, MLP_HB, D_MODEL)
    dw1 = _wgrad(s['v_b'], da_b, N_DEV, "wgrad_w1", tm=1024)
    dya, dyb, dyc, dyd, do1_b, dg1, dbn = _outproj_bwd(dh1, s['o'], s['ys'], p['branch_norm_w'][l], p['g1'][l], p['w_out'][l], tb)
    dwo = _wgrad(s['groups_b'], do1_b, 1, "wgrad_wout", tm=512).reshape(N_DEV, D_MODEL // N_DEV, D_MODEL)
    proj = s['proj']
    dv, dpool_bd, dpool_scale = _pool_bwd(proj, dya, p['pool_bd'][l], p['pool_scale'][l], tb)
    dgb, dgc, dhh, dsconv = _sconv_bwd(proj, dyb, p['sconv_w8'][l], tb)
    (dz, dxs, dbm, dcm, ddt, dconv_w, dconv_b, ddtb, dalog, ddskip) = _ssd_bwd(
        proj, dyc, s['yc_pre'], s['states'], p['ssd_conv_w8'][l], p['ssd_conv_b'][l], p['ssd_dt_bias'][l], p['ssd_a_log'][l],
        p['ssd_d_cols'][l], tb)
    (du5, dlr, dli, dbbr, dbbi, dctr, dcti, dd5, dgw, dgb5) = _s5_bwd(
        proj, dyd, s['xr'], s['xi'], p['bbr'][l], p['bbi'][l], p['ctr'][l], p['cti'][l], p['lre'][l], p['lim'][l],
        p['s5_d'][l], p['glu_w'][l], p['glu_b'][l], tb)
    dare, daim, dls, dbre_bd, dbim_bd = _s5_prep_bwd(p['are_c'][l], p['aim_c'][l], p['ls_c'][l], p['bre_bd'][l], p['bim_bd'][l],
                                                     dlr.reshape(S5_N, 1), dli.reshape(S5_N, 1), dbbr, dbbi)
    dparts = (dv, dgb, dgc, dhh, dz, dxs, dbm, dcm, du5, ddt)
    dh, dproj_b, dsh1, dsc1, dnw_mix = _inproj_bwd(dparts, dh1, s['h'], p['norm_mix_w'][l], p['sc1'][l], p['sh1'][l], p['w_in'][l], tb)
    dwi = _wgrad(s['u_b'], dproj_b, 1, "wgrad_win", tm=256).reshape(N_DEV, D_MODEL // N_DEV, P_IN)
    small = {
        'norm_mix_w': dnw_mix.reshape(D_MODEL), 'norm_mlp_w': dnw_mlp.reshape(D_MODEL),
        'ada_b': jnp.concatenate([dsh1, dsc1, dg1, dsh2, dsc2, dg2], axis=1).reshape(6 * D_MODEL),
        'pool_w': _block_diag_extract(dpool_bd, 4), 'pool_scale': dpool_scale.reshape(GROUP_W),
        'sconv_w': dsconv[0:3], 'ssd_conv_w': dconv_w[0:4], 'ssd_conv_b': dconv_b.reshape(768),
        'ssd_dt_bias': ddtb[0, 0:4], 'ssd_a_log': dalog[0, 0:4], 'ssd_d': ddskip[0, 0:4],
        's5_a_re': dare.reshape(16, 64), 's5_a_im': daim.reshape(16, 64), 's5_log_step': dls[0:16, 0],
        's5_b_re': _block_diag_extract(dbre_bd, 16), 's5_b_im': _block_diag_extract(dbim_bd, 16),
        's5_c_re': _block_diag_extract(dctr, 16), 's5_c_im': _block_diag_extract(dcti, 16),
        's5_d': dd5.reshape(GROUP_W), 's5_glu_w': dgw, 's5_glu_b': dgb5.reshape(GROUP_W),
        'branch_norm_w': dbn.reshape(D_MODEL),
    }
    return dh, (dwi, dwo, dw1, dw2), small


def _prepare_params(a, me):
    pack_shapes = [(1, D_MODEL), (2, 3, 32), (2, 4, 96), (2, 32, GROUP_W)]
    packed = _flat_pack([a['c'], a['sconv_w'], a['ssd_conv_w'], a['s5_glu_w']])
    gathered = _gather_rows(packed, "gather_small")
    pieces = [_flat_unpack(gathered[d], pack_shapes) for d in range(N_DEV)]
    c_all = jnp.concatenate([pc[0] for pc in pieces], axis=0)
    sconv_full = jnp.concatenate([pc[1] for pc in pieces], axis=2)
    ssd_conv_full = jnp.concatenate([pc[2] for pc in pieces], axis=2)
    glu_full = jnp.concatenate([pc[3] for pc in pieces], axis=1)

    ada_b_cols = lax.dynamic_slice_in_dim(a['ada_b'], me * 768, 768, axis=1).reshape(2, 1, 768)
    cond, modrows = _ada_forward(c_all, a['ada_w'], ada_b_cols)
    mod_recv = _all_to_all_rows(modrows.transpose(1, 0, 2), "exchange_mod")
    mod = mod_recv.transpose(1, 0, 2).reshape(2, 6 * D_MODEL)
    p = {'cond': cond}
    for k, name in enumerate(('sh1', 'sc1', 'g1', 'sh2', 'sc2', 'g2')):
        p[name] = mod[:, k * D_MODEL:(k + 1) * D_MODEL].reshape(2, 1, D_MODEL)

    gw_in, gw_out, gw1, gw2 = _all_gather_bf16([_reorder_in(a['w_in']), a['w_out'], a['mlp_w1'], a['mlp_w2']])
    p['w_in'] = gw_in.reshape(2, D_MODEL, P_IN)
    p['w_out'] = gw_out.reshape(2, D_MODEL, D_MODEL)
    p['w1'], p['w2'] = gw1, gw2

    for name in ('norm_mix_w', 'norm_mlp_w', 'branch_norm_w'):
        p[name] = a[name].reshape(2, 1, D_MODEL)
    p['pool_bd'] = jnp.stack([_block_diag(a['pool_w'][l].reshape(GROUP_W, 64), 4) for l in range(2)])
    p['pool_scale'] = a['pool_scale'].reshape(2, 1, GROUP_W)
    p['sconv_w8'] = jnp.pad(sconv_full, ((0, 0), (0, 5), (0, 0)))
    p['ssd_conv_w8'] = jnp.pad(ssd_conv_full, ((0, 0), (0, 4), (0, 0)))
    p['ssd_conv_b'] = a['ssd_conv_b'].reshape(2, 1, 768)
    p['ssd_dt_bias'] = jnp.pad(a['ssd_dt_bias'], ((0, 0), (0, LANES - 4))).reshape(2, 1, LANES)
    p['ssd_a_log'] = jnp.pad(a['ssd_a_log'], ((0, 0), (0, LANES - 4))).reshape(2, 1, LANES)
    p['ssd_d_cols'] = jnp.repeat(a['ssd_d'], SSD_P, axis=1).reshape(2, 1, GROUP_W)
    p['are_c'] = a['s5_a_re'].reshape(2, S5_N, 1)
    p['aim_c'] = a['s5_a_im'].reshape(2, S5_N, 1)
    p['ls_c'] = jnp.repeat(a['s5_log_step'], 64, axis=1).reshape(2, S5_N, 1)
    p['bre_bd'] = jnp.stack([_block_diag(a['s5_b_re'][l].reshape(S5_N, 16), 16) for l in range(2)])
    p['bim_bd'] = jnp.stack([_block_diag(a['s5_b_im'][l].reshape(S5_N, 16), 16) for l in range(2)])
    p['ctr'] = jnp.stack([_block_diag(a['s5_c_re'][l].reshape(GROUP_W, 64), 16) for l in range(2)])
    p['cti'] = jnp.stack([_block_diag(a['s5_c_im'][l].reshape(GROUP_W, 64), 16) for l in range(2)])
    p['s5_d'] = a['s5_d'].reshape(2, 1, GROUP_W)
    p['glu_w'] = glu_full
    p['glu_b'] = a['s5_glu_b'].reshape(2, 1, GROUP_W)
    lre, lim, bbr, bbi = [], [], [], []
    for l in range(2):
        r = _s5_prep(p['are_c'][l], p['aim_c'][l], p['ls_c'][l], p['bre_bd'][l], p['bim_bd'][l])
        lre.append(r[0].reshape(1, S5_N))
        lim.append(r[1].reshape(1, S5_N))
        bbr.append(r[2])
        bbi.append(r[3])
    p['lre'], p['lim'], p['bbr'], p['bbi'] = lre, lim, bbr, bbi
    return p


def kernel(x, c, norm_mix_w, norm_mlp_w, ada_w, ada_b, w_in, pool_w, pool_scale, sconv_w, ssd_conv_w, ssd_conv_b, ssd_dt_bias, ssd_a_log, ssd_d, s5_a_re, s5_a_im, s5_log_step, s5_b_re, s5_b_im, s5_c_re, s5_c_im, s5_d, s5_glu_w, s5_glu_b, branch_norm_w, w_out, mlp_w1, mlp_w2, final_norm_w, loss_target, m_norm_mix_w, m_norm_mlp_w, m_ada_w, m_ada_b, m_w_in, m_pool_w, m_pool_scale, m_sconv_w, m_ssd_conv_w, m_ssd_conv_b, m_ssd_dt_bias, m_ssd_a_log, m_ssd_d, m_s5_a_re, m_s5_a_im, m_s5_log_step, m_s5_b_re, m_s5_b_im, m_s5_c_re, m_s5_c_im, m_s5_d, m_s5_glu_w, m_s5_glu_b, m_branch_norm_w, m_w_out, m_mlp_w1, m_mlp_w2, m_final_norm_w, v_norm_mix_w, v_norm_mlp_w, v_ada_w, v_ada_b, v_w_in, v_pool_w, v_pool_scale, v_sconv_w, v_ssd_conv_w, v_ssd_conv_b, v_ssd_dt_bias, v_ssd_a_log, v_ssd_d, v_s5_a_re, v_s5_a_im, v_s5_log_step, v_s5_b_re, v_s5_b_im, v_s5_c_re, v_s5_c_im, v_s5_d, v_s5_glu_w, v_s5_glu_b, v_branch_norm_w, v_w_out, v_mlp_w1, v_mlp_w2, v_final_norm_w):
    a = dict(locals())
    t = x.shape[1]
    tb = min(512, t)
    me = _my_index()
    p = _prepare_params(a, me)

    h = x.reshape(t, D_MODEL)
    saved = []
    for l in range(2):
        h, s = _layer_forward(l, h, p, tb)
        saved.append(s)
    loss_blk, dh, dfinal = _loss_head(h, loss_target.reshape(t, D_MODEL), final_norm_w.reshape(1, D_MODEL), tb)
    loss = lax.psum(loss_blk[0, 0], ("x", "y", "c"))

    big_parts, small_parts = [None, None], [None, None]
    for l in (1, 0):
        dh, big_parts[l], small_parts[l] = _layer_backward(l, dh, saved[l], p, tb)
    grad_x = dh.reshape(1, t, D_MODEL)

    grads, deltas, new_m, new_v = {}, {}, {}, {}

    recv = _exchange_partials([jnp.stack([big_parts[0][k], big_parts[1][k]]) for k in range(4)])
    wmv_in = [_reorder_in(a[n]) for n in ('w_in', 'm_w_in', 'v_w_in')]
    outs = _sum_adamw_layers(recv[0], *wmv_in, "adamw_w_in", 128)
    grads['w_in'], deltas['w_in'], new_m['w_in'], new_v['w_in'] = [_unreorder_in(o) for o in outs]
    for name, r, rb in (('w_out', recv[1], 128), ('mlp_w1', recv[2], 256), ('mlp_w2', recv[3], 256)):
        grads[name], deltas[name], new_m[name], new_v[name] = _sum_adamw_layers(r, a[name], a['m_' + name], a['v_' + name],
                                                                               "adamw_" + name, rb)

    dmod = jnp.stack([small_parts[0]['ada_b'], small_parts[1]['ada_b']])
    dmod_recv = _all_to_all_rows(dmod.reshape(2, N_DEV, 768).transpose(1, 0, 2), "exchange_dmod")
    g_ada = _ada_backward(p['cond'], dmod_recv.transpose(1, 0, 2))
    grads['ada_w'], deltas['ada_w'], new_m['ada_w'], new_v['ada_w'] = _sum_adamw_layers(
        g_ada.reshape(2, 1, D_MODEL, 768), ada_w, m_ada_w, v_ada_w, "adamw_ada_w", 256)

    layered = [n for n in _SMALL if n != 'final_norm_w']
    full = [jnp.stack([small_parts[0][n], small_parts[1][n]]) for n in layered] + [dfinal.reshape(D_MODEL)]
    full_shapes = [f.shape for f in full]
    summed = _flat_unpack(_sum_rows(_gather_rows(_flat_pack(full), "gather_small_grads")), full_shapes)
    local = []
    for n, g in zip(_SMALL, summed):
        if n in _SHARDED_SMALL:
            axis, size = _SHARDED_SMALL[n]
            g = lax.dynamic_slice_in_dim(g, me * size, size, axis=axis)
        local.append(g.reshape(a[n].shape))
    local_shapes = [g.shape for g in local]
    packed = [_flat_pack(xs) for xs in (local, [a[n] for n in _SMALL], [a['m_' + n] for n in _SMALL], [a['v_' + n] for n in _SMALL])]
    outs = _sum_adamw(packed[0][None], packed[1], packed[2], packed[3], "adamw_small", packed[0].shape[0])
    for store, o in zip((grads, deltas, new_m, new_v), outs):
        for n, val in zip(_SMALL, _flat_unpack(o, local_shapes)):
            store[n] = val

    return (loss, grad_x, *[grads[n] for n in _W_NAMES], *[deltas[n] for n in _W_NAMES],
            *[new_m[n] for n in _W_NAMES], *[new_v[n] for n in _W_NAMES])
```

```python
import functools

import jax
import jax.numpy as jnp
from jax import lax
from jax.experimental import pallas as pl
from jax.experimental.pallas import tpu as pltpu

f32 = jnp.float32
bf16 = jnp.bfloat16

N_DEV = 8
D_MODEL = 1024
GROUP_W = 256
P_IN = 2432
DT_COL = 2304
SSD_CHUNK = 128
SSD_HEADS = 4
SSD_P = 64
S5_N = 1024
MLP_HB = 512
EPS = 1e-6
LANES = 128
VMEM_LIMIT = 56 * 1024 * 1024
ADAM_LR, ADAM_B1, ADAM_B2, ADAM_EPS, ADAM_WD, ADAM_STEP = 0.001, 0.9, 0.999, 1e-08, 0.01, 10
POOL_WINDOWS = (2, 4, 8, 16)

C_POOL, C_GB, C_GC, C_HH, C_Z, C_XS, C_BM, C_CM, C_S5 = range(9)
C_DT128 = DT_COL // LANES

MESH = pl.DeviceIdType.MESH
ANY = pl.BlockSpec(memory_space=pl.ANY)
VMEM = pl.BlockSpec(memory_space=pltpu.VMEM)


def _dot(a, b):
    return jnp.dot(a, b, preferred_element_type=f32)


def _dot_nt(a, b):
    return lax.dot_general(a, b, (((1,), (1,)), ((), ())), preferred_element_type=f32)


def _dot_tn(a, b):
    return lax.dot_general(a, b, (((0,), (0,)), ((), ())), preferred_element_type=f32)


def _dot_exact(a, b):
    return jnp.dot(a, b, preferred_element_type=f32, precision=lax.Precision.HIGHEST)


def _b(x):
    return x.astype(bf16)


def _silu(x):
    return x * jax.nn.sigmoid(x)


def _dsilu(x):
    s = jax.nn.sigmoid(x)
    return s * (1.0 + x * (1.0 - s))


def _softplus(x):
    return jnp.maximum(x, 0.0) + jnp.log1p(jnp.exp(-jnp.abs(x)))


_GELU_K = 0.7978845608028654
_GELU_C = 0.044715


def _gelu(x):
    return 0.5 * x * (1.0 + jnp.tanh(_GELU_K * (x + _GELU_C * x * x * x)))


def _dgelu(x):
    th = jnp.tanh(_GELU_K * (x + _GELU_C * x * x * x))
    return 0.5 * (1.0 + th) + 0.5 * x * (1.0 - th * th) * _GELU_K * (1.0 + 3.0 * _GELU_C * x * x)


def _rms(h):
    r = lax.rsqrt(jnp.mean(h * h, axis=-1, keepdims=True) + EPS)
    return h * r, r


def _rms_bwd(dn, n, r):
    return r * (dn - n * jnp.mean(dn * n, axis=-1, keepdims=True))


def _colsum(x):
    return jnp.sum(x, axis=0, keepdims=True)


def _params(sem=None):
    return pltpu.CompilerParams(dimension_semantics=sem, vmem_limit_bytes=VMEM_LIMIT)


def _full(shape):
    return pl.BlockSpec(shape, lambda *_: (0,) * len(shape))


def _acc(ref, val):
    @pl.when(pl.program_id(0) == 0)
    def _():
        ref[...] = val

    @pl.when(pl.program_id(0) != 0)
    def _():
        ref[...] += val


def _me():
    return lax.axis_index("x"), lax.axis_index("y"), lax.axis_index("c")


def _my_index():
    x, y, c = _me()
    return 4 * x + 2 * y + c


def _coords(p):
    return (p // 4, (p // 2) % 2, p % 2)


class _Exchange:
    def __init__(self, srcs, gather):
        self.srcs = list(srcs)
        self.gather = gather
        self.n = len(self.srcs)
        self.out_shape = [jax.ShapeDtypeStruct(((N_DEV,) + s.shape) if gather else s.shape, s.dtype) for s in self.srcs]
        self.scratch = [pltpu.SemaphoreType.DMA((self.n, N_DEV)), pltpu.SemaphoreType.DMA((self.n, N_DEV)),
                        pltpu.SemaphoreType.DMA((self.n,))]

    def _src(self, refs, t, dev):
        return refs[t] if self.gather else refs[t].at[dev]

    def _remote(self, xin, xout, sems, t, k, me, to):
        return pltpu.make_async_remote_copy(
            src_ref=self._src(xin, t, to), dst_ref=xout[t].at[me], send_sem=sems[0].at[t, k], recv_sem=sems[1].at[t, k],
            device_id=_coords(to), device_id_type=MESH)

    def start(self, xin, xout, sems):
        me = _my_index()
        for t in range(self.n):
            pltpu.make_async_copy(self._src(xin, t, me), xout[t].at[me], sems[2].at[t]).start()
            for k in range(1, N_DEV):
                self._remote(xin, xout, sems, t, k, me, (me + k) % N_DEV).start()

    def wait(self, xin, xout, sems):
        me = _my_index()
        for t in range(self.n):
            for k in range(1, N_DEV):
                src = (me + N_DEV - k) % N_DEV
                pltpu.make_async_remote_copy(
                    src_ref=self._src(xin, t, src), dst_ref=xout[t].at[src], send_sem=sems[0].at[t, k],
                    recv_sem=sems[1].at[t, k], device_id=_coords(src), device_id_type=MESH).wait_recv()
        for t in range(self.n):
            for k in range(1, N_DEV):
                self._remote(xin, xout, sems, t, k, me, (me + k) % N_DEV).wait_send()
            pltpu.make_async_copy(self._src(xin, t, me), xout[t].at[me], sems[2].at[t]).wait()


def _call(body, *, name, grid, in_specs, out_specs, out_shape, args, semantics, scratch_shapes=(), xchg=None):
    if xchg is None:
        outs = pl.pallas_call(body, name=name, grid=grid, in_specs=in_specs, out_specs=out_specs, out_shape=out_shape,
                              scratch_shapes=list(scratch_shapes), compiler_params=_params(semantics))(*args)
        return outs, ()
    n_in, n_out, n_scr, n = len(in_specs), len(out_specs), len(scratch_shapes), xchg.n

    def carried(*refs):
        ins, xin = refs[:n_in], refs[n_in:n_in + n]
        outs, xout = refs[n_in + n:n_in + n + n_out], refs[n_in + n + n_out:n_in + 2 * n + n_out]
        scr, sems = refs[n_in + 2 * n + n_out:n_in + 2 * n + n_out + n_scr], refs[n_in + 2 * n + n_out + n_scr:]
        first = functools.reduce(jnp.logical_and, [pl.program_id(d) == 0 for d in range(len(grid))])
        last = functools.reduce(jnp.logical_and, [pl.program_id(d) == g - 1 for d, g in enumerate(grid)])

        @pl.when(first)
        def _():
            xchg.start(xin, xout, sems)

        body(*ins, *outs, *scr)

        @pl.when(last)
        def _():
            xchg.wait(xin, xout, sems)

    res = pl.pallas_call(
        carried, name=name, grid=grid, in_specs=list(in_specs) + [ANY] * n, out_specs=list(out_specs) + [ANY] * n,
        out_shape=list(out_shape) + xchg.out_shape, scratch_shapes=list(scratch_shapes) + xchg.scratch,
        compiler_params=_params(("arbitrary",) * len(grid)))(*args, *xchg.srcs)
    return res[:n_out], tuple(res[n_out:])


def _exchange_alone(xchg, name):
    def body(*refs):
        xin, xout, sems = refs[:xchg.n], refs[xchg.n:2 * xchg.n], refs[2 * xchg.n:]
        xchg.start(xin, xout, sems)
        xchg.wait(xin, xout, sems)

    return pl.pallas_call(body, name=name, out_shape=xchg.out_shape, in_specs=[ANY] * xchg.n, out_specs=[ANY] * xchg.n,
                          scratch_shapes=xchg.scratch)(*xchg.srcs)


def _cast_shards(shards):
    n = len(shards)

    def body(*refs):
        for i, o in zip(refs[:n], refs[n:]):
            o[...] = i[...].astype(bf16)

    return pl.pallas_call(body, name="cast_shards", out_shape=[jax.ShapeDtypeStruct(s.shape, bf16) for s in shards],
                          in_specs=[VMEM] * n, out_specs=[VMEM] * n, compiler_params=_params())(*shards)


def _allreduce_rows(v):
    r = v.shape[0]
    rp = r // N_DEV

    def body(v_ref, o_ref, parts, sums, send1, recv1, send2, recv2):
        me = _my_index()

        def piece(ref, d):
            return ref.at[pl.ds(pl.multiple_of(d * rp, 8), rp), :]

        def copy1(k, src_dev, to):
            return pltpu.make_async_remote_copy(src_ref=piece(v_ref, to), dst_ref=parts.at[src_dev], send_sem=send1.at[k],
                                                recv_sem=recv1.at[k], device_id=_coords(to), device_id_type=MESH)

        def copy2(k, owner, to):
            return pltpu.make_async_remote_copy(src_ref=sums, dst_ref=piece(o_ref, owner), send_sem=send2.at[k],
                                                recv_sem=recv2.at[k], device_id=_coords(to), device_id_type=MESH)

        for k in range(1, N_DEV):
            copy1(k, me, (me + k) % N_DEV).start()
        parts[me] = v_ref[pl.ds(pl.multiple_of(me * rp, 8), rp), :]
        for k in range(1, N_DEV):
            copy1(k, (me + N_DEV - k) % N_DEV, me).wait_recv()
        total = parts[0]
        for s in range(1, N_DEV):
            total = total + parts[s]
        sums[...] = total
        o_ref[pl.ds(pl.multiple_of(me * rp, 8), rp), :] = total
        for k in range(1, N_DEV):
            copy2(k, me, (me + k) % N_DEV).start()
        for k in range(1, N_DEV):
            copy2(k, (me + N_DEV - k) % N_DEV, me).wait_recv()
        for k in range(1, N_DEV):
            copy1(k, me, (me + k) % N_DEV).wait_send()
            copy2(k, me, (me + k) % N_DEV).wait_send()

    return pl.pallas_call(
        body, name="allreduce_small_grads", out_shape=jax.ShapeDtypeStruct(v.shape, v.dtype),
        in_specs=[VMEM], out_specs=VMEM,
        scratch_shapes=[pltpu.VMEM((N_DEV, rp, LANES), f32), pltpu.VMEM((rp, LANES), f32)]
        + [pltpu.SemaphoreType.DMA((N_DEV,))] * 4,
        compiler_params=_params(),
    )(v)


def _gather_rows(v, name):
    def body(v_ref, o_ref, send_sems, recv_sems):
        me = _my_index()
        o_ref[me] = v_ref[...]
        sends = []
        for k in range(1, N_DEV):
            peer = (me + k) % N_DEV
            rc = pltpu.make_async_remote_copy(src_ref=v_ref, dst_ref=o_ref.at[me], send_sem=send_sems.at[k],
                                              recv_sem=recv_sems.at[k], device_id=_coords(peer), device_id_type=MESH)
            rc.start()
            sends.append(rc)
        for k in range(1, N_DEV):
            src = (me + N_DEV - k) % N_DEV
            pltpu.make_async_remote_copy(src_ref=v_ref, dst_ref=o_ref.at[src], send_sem=send_sems.at[k],
                                         recv_sem=recv_sems.at[k], device_id=_coords(src), device_id_type=MESH).wait_recv()
        for rc in sends:
            rc.wait_send()

    return pl.pallas_call(
        body, name=name, out_shape=jax.ShapeDtypeStruct((N_DEV,) + v.shape, v.dtype),
        in_specs=[VMEM], out_specs=VMEM,
        scratch_shapes=[pltpu.SemaphoreType.DMA((N_DEV,)), pltpu.SemaphoreType.DMA((N_DEV,))],
        compiler_params=pltpu.CompilerParams(vmem_limit_bytes=VMEM_LIMIT),
    )(v)


def _all_to_all_rows(v, name):
    def body(v_ref, o_ref, send_sems, recv_sems):
        me = _my_index()
        o_ref[me] = v_ref[me]
        sends = []
        for k in range(1, N_DEV):
            peer = (me + k) % N_DEV
            rc = pltpu.make_async_remote_copy(src_ref=v_ref.at[peer], dst_ref=o_ref.at[me], send_sem=send_sems.at[k],
                                              recv_sem=recv_sems.at[k], device_id=_coords(peer), device_id_type=MESH)
            rc.start()
            sends.append(rc)
        for k in range(1, N_DEV):
            src = (me + N_DEV - k) % N_DEV
            pltpu.make_async_remote_copy(src_ref=v_ref.at[src], dst_ref=o_ref.at[src], send_sem=send_sems.at[k],
                                         recv_sem=recv_sems.at[k], device_id=_coords(src), device_id_type=MESH).wait_recv()
        for rc in sends:
            rc.wait_send()

    return pl.pallas_call(
        body, name=name, out_shape=jax.ShapeDtypeStruct(v.shape, v.dtype),
        in_specs=[VMEM], out_specs=VMEM,
        scratch_shapes=[pltpu.SemaphoreType.DMA((N_DEV,)), pltpu.SemaphoreType.DMA((N_DEV,))],
    )(v)


def _ada_forward(c_all, ada_w, ada_b_cols):
    def body(c_ref, w_ref, b_ref, cond_ref, o_ref):
        cond = _silu(c_ref[...])
        cond_ref[...] = cond
        for l in range(2):
            o_ref[l] = _dot(_b(cond), _b(w_ref[l])) + b_ref[l]

    return pl.pallas_call(
        body, name="ada_forward",
        out_shape=[jax.ShapeDtypeStruct((N_DEV, D_MODEL), f32), jax.ShapeDtypeStruct((2, N_DEV, 768), f32)],
        in_specs=[VMEM] * 3, out_specs=[VMEM] * 2, compiler_params=_params(),
    )(c_all, ada_w, ada_b_cols)


def _ada_backward(cond, dmod_rows):
    def body(c_ref, d_ref, o_ref):
        cb = _b(c_ref[...])
        for l in range(2):
            o_ref[l] = _dot_tn(cb, _b(d_ref[l]))

    return pl.pallas_call(
        body, name="ada_backward", out_shape=jax.ShapeDtypeStruct((2, D_MODEL, 768), f32),
        in_specs=[VMEM] * 2, out_specs=VMEM, compiler_params=_params(),
    )(cond, dmod_rows)


def _inproj_fwd(h, norm_w, sc, sh, w_in, tb, xchg=None):
    t = h.shape[0]

    def body(h_ref, nw_ref, sc_ref, sh_ref, w_ref, proj_ref, u_ref):
        n, _ = _rms(h_ref[...])
        u = _b(n * nw_ref[...] * (1.0 + sc_ref[...]) + sh_ref[...])
        u_ref[...] = u
        proj_ref[...] = _dot(u, w_ref[...])

    row = pl.BlockSpec((tb, D_MODEL), lambda i: (i, 0))
    vec = _full((1, D_MODEL))
    return _call(
        body, name="inproj_fwd", grid=(t // tb,),
        out_shape=[jax.ShapeDtypeStruct((t, P_IN), f32), jax.ShapeDtypeStruct((t, D_MODEL), bf16)],
        in_specs=[row, vec, vec, vec, _full((D_MODEL, P_IN))],
        out_specs=[pl.BlockSpec((tb, P_IN), lambda i: (i, 0)), row],
        semantics=("parallel",), args=(h, norm_w, sc, sh, w_in), xchg=xchg)


def _inproj_bwd(dparts, dh_res, h, norm_w, sc, sh, w_in, tb):
    t = h.shape[0]

    def body(*refs):
        parts = refs[:10]
        dres_ref, h_ref, nw_ref, sc_ref, sh_ref, w_ref = refs[10:16]
        dh_ref, dproj_ref, dsh_ref, dsc_ref, dnw_ref = refs[16:]
        dproj = _b(jnp.concatenate([p[...] for p in parts], axis=1))
        dproj_ref[...] = dproj
        du = _dot_nt(dproj, w_ref[...])
        n, r = _rms(h_ref[...])
        nw = nw_ref[...]
        gain = 1.0 + sc_ref[...]
        _acc(dsh_ref, _colsum(du))
        _acc(dsc_ref, _colsum(du * n * nw))
        _acc(dnw_ref, _colsum(du * gain * n))
        dh_ref[...] = dres_ref[...] + _rms_bwd(du * nw * gain, n, r)

    row = pl.BlockSpec((tb, D_MODEL), lambda i: (i, 0))
    vec = _full((1, D_MODEL))
    part_specs = [pl.BlockSpec((tb, GROUP_W), lambda i: (i, 0))] * 9 + [pl.BlockSpec((tb, LANES), lambda i: (i, 0))]
    return pl.pallas_call(
        body, name="inproj_bwd", grid=(t // tb,),
        out_shape=[jax.ShapeDtypeStruct((t, D_MODEL), f32), jax.ShapeDtypeStruct((t, P_IN), bf16)]
        + [jax.ShapeDtypeStruct((1, D_MODEL), f32)] * 3,
        in_specs=part_specs + [row, row, vec, vec, vec, _full((D_MODEL, P_IN))],
        out_specs=[row, pl.BlockSpec((tb, P_IN), lambda i: (i, 0)), vec, vec, vec],
        compiler_params=_params(("arbitrary",)),
    )(*dparts, dh_res, h, norm_w, sc, sh, w_in)


def _wgrad(a, b, n_blocks, name, tm, tk=512):
    t, m = a.shape
    nb = b.shape[1] // n_blocks
    tk = min(tk, t)
    nk = t // tk

    def body(a_ref, b_ref, o_ref, acc_ref):
        k = pl.program_id(2)
        p = _dot_tn(a_ref[...], b_ref[...])

        @pl.when(k == 0)
        def _():
            acc_ref[...] = p

        @pl.when(k != 0)
        def _():
            acc_ref[...] += p

        @pl.when(k == nk - 1)
        def _():
            o_ref[0] = acc_ref[...].astype(o_ref.dtype)

    return pl.pallas_call(
        body, name=name, grid=(m // tm, n_blocks, nk),
        out_shape=jax.ShapeDtypeStruct((n_blocks, m, nb), bf16),
        in_specs=[pl.BlockSpec((tk, tm), lambda i, j, k: (k, i)), pl.BlockSpec((tk, nb), lambda i, j, k: (k, j))],
        out_specs=pl.BlockSpec((1, tm, nb), lambda i, j, k: (j, i, 0)),
        scratch_shapes=[pltpu.VMEM((tm, nb), f32)],
        compiler_params=_params(("parallel", "parallel", "arbitrary")),
    )(a, b)


def _pool_counts(rows, t0):
    tpos = (lax.broadcasted_iota(jnp.int32, (rows, GROUP_W), 0) + t0 + 1).astype(f32)
    grp = lax.broadcasted_iota(jnp.int32, (rows, GROUP_W), 1) // 64
    win = jnp.where(grp == 0, 2.0, jnp.where(grp == 1, 4.0, jnp.where(grp == 2, 8.0, 16.0)))
    return jnp.minimum(tpos, win), grp


def _pool_select(grp, l1, l2, l3, l4):
    return jnp.where(grp == 0, l1, jnp.where(grp == 1, l2, jnp.where(grp == 2, l3, l4)))


def _pool_means(v, halo, t0):
    tb = v.shape[0]
    ext = jnp.concatenate([halo, v], axis=0)
    n = tb + 16
    s1 = ext[1:n] + ext[0:n - 1]
    s2 = s1[2:n - 1] + s1[0:n - 3]
    s3 = s2[4:n - 3] + s2[0:n - 7]
    s4 = s3[8:n - 7] + s3[0:n - 15]
    cnt, grp = _pool_counts(tb, t0)
    wsum = _pool_select(grp, s1[15:15 + tb], s2[13:13 + tb], s3[9:9 + tb], s4[1:1 + tb])
    return wsum / cnt - v


def _pool_fwd(proj, pw_bd, scale, tb):
    t = proj.shape[0]

    def body(v_ref, vh_ref, pw_ref, sc_ref, o_ref):
        i = pl.program_id(0)
        halo = jnp.where(i > 0, vh_ref[...], 0.0)
        p = _pool_means(v_ref[...], halo, i * tb)
        o_ref[...] = _dot(_b(p), _b(pw_ref[...])) * sc_ref[...]

    return pl.pallas_call(
        body, name="pool_fwd", grid=(t // tb,),
        out_shape=jax.ShapeDtypeStruct((t, GROUP_W), f32),
        in_specs=[pl.BlockSpec((tb, GROUP_W), lambda i: (i, C_POOL)),
                  pl.BlockSpec((16, GROUP_W), lambda i: (jnp.maximum(i * (tb // 16) - 1, 0), C_POOL)),
                  _full((GROUP_W, GROUP_W)), _full((1, GROUP_W))],
        out_specs=pl.BlockSpec((tb, GROUP_W), lambda i: (i, 0)),
        compiler_params=_params(("parallel",)),
    )(proj, proj, pw_bd, scale)


def _pool_bwd(proj, dy, pw_bd, scale, tb):
    t = proj.shape[0]
    nt = t // tb
    last16 = t // 16 - 1

    def body(v_ref, vh_ref, dy_ref, dyh_ref, pw_ref, sc_ref, dv_ref, dpw_ref, dsc_ref):
        i = pl.program_id(0)
        halo = jnp.where(i > 0, vh_ref[...], 0.0)
        p = _pool_means(v_ref[...], halo, i * tb)
        pw = _b(pw_ref[...])
        sc = sc_ref[...]
        dy = dy_ref[...]
        ypre = _dot(_b(p), pw)
        _acc(dsc_ref, _colsum(dy * ypre))
        dys = _b(dy * sc)
        _acc(dpw_ref, _dot_tn(_b(p), dys))
        dp = _dot_nt(dys, pw)
        dph = _dot_nt(_b(jnp.where(i < nt - 1, dyh_ref[...], 0.0) * sc), pw)
        cnt, grp = _pool_counts(tb, i * tb)
        cnth, _ = _pool_counts(16, (i + 1) * tb)
        ext = jnp.concatenate([dp / cnt, dph / cnth], axis=0)
        n = tb + 16
        f1 = ext[0:n - 1] + ext[1:n]
        f2 = f1[0:n - 3] + f1[2:n - 1]
        f3 = f2[0:n - 7] + f2[4:n - 3]
        f4 = f3[0:n - 15] + f3[8:n - 7]
        dv_ref[...] = _pool_select(grp, f1[0:tb], f2[0:tb], f3[0:tb], f4[0:tb]) - dp

    return pl.pallas_call(
        body, name="pool_bwd", grid=(nt,),
        out_shape=[jax.ShapeDtypeStruct((t, GROUP_W), f32), jax.ShapeDtypeStruct((GROUP_W, GROUP_W), f32),
                   jax.ShapeDtypeStruct((1, GROUP_W), f32)],
        in_specs=[pl.BlockSpec((tb, GROUP_W), lambda i: (i, C_POOL)),
                  pl.BlockSpec((16, GROUP_W), lambda i: (jnp.maximum(i * (tb // 16) - 1, 0), C_POOL)),
                  pl.BlockSpec((tb, GROUP_W), lambda i: (i, 0)),
                  pl.BlockSpec((16, GROUP_W), lambda i: (jnp.minimum((i + 1) * (tb // 16), last16), 0)),
                  _full((GROUP_W, GROUP_W)), _full((1, GROUP_W))],
        out_specs=[pl.BlockSpec((tb, GROUP_W), lambda i: (i, 0)), _full((GROUP_W, GROUP_W)), _full((1, GROUP_W))],
        compiler_params=_params(("arbitrary",)),
    )(proj, proj, dy, dy, pw_bd, scale)


def _sconv_fwd(proj, w, tb):
    t = proj.shape[0]

    def body(gb_ref, gc_ref, hh_ref, gch_ref, hhh_ref, w_ref, o_ref):
        i = pl.program_id(0)
        q = gc_ref[...] * hh_ref[...]
        qh = jnp.where(i > 0, gch_ref[...] * hhh_ref[...], 0.0)
        ext = jnp.concatenate([qh, q], axis=0)
        w = w_ref[...]
        conv = w[0:1] * ext[6:6 + tb] + w[1:2] * ext[7:7 + tb] + w[2:3] * ext[8:8 + tb]
        o_ref[...] = gb_ref[...] * conv

    def col(c):
        return pl.BlockSpec((tb, GROUP_W), lambda i: (i, c))

    def prev(c):
        return pl.BlockSpec((8, GROUP_W), lambda i: (jnp.maximum(i * (tb // 8) - 1, 0), c))

    return pl.pallas_call(
        body, name="sconv_fwd", grid=(t // tb,),
        out_shape=jax.ShapeDtypeStruct((t, GROUP_W), f32),
        in_specs=[col(C_GB), col(C_GC), col(C_HH), prev(C_GC), prev(C_HH), _full((8, GROUP_W))],
        out_specs=pl.BlockSpec((tb, GROUP_W), lambda i: (i, 0)),
        compiler_params=_params(("parallel",)),
    )(proj, proj, proj, proj, proj, w)


def _sconv_bwd(proj, dy, w, tb):
    t = proj.shape[0]
    nt = t // tb
    last8 = t // 8 - 1

    def body(gb_ref, gc_ref, hh_ref, gch_ref, hhh_ref, gbn_ref, dy_ref, dyn_ref, w_ref, dgb_ref, dgc_ref, dhh_ref, dw_ref):
        i = pl.program_id(0)
        gc, hh, gb, dy = gc_ref[...], hh_ref[...], gb_ref[...], dy_ref[...]
        q = gc * hh
        qh = jnp.where(i > 0, gch_ref[...] * hhh_ref[...], 0.0)
        ext = jnp.concatenate([qh, q], axis=0)
        w = w_ref[...]
        conv = w[0:1] * ext[6:6 + tb] + w[1:2] * ext[7:7 + tb] + w[2:3] * ext[8:8 + tb]
        dgb_ref[...] = dy * conv
        e = dy * gb
        en = jnp.where(i < nt - 1, dyn_ref[...] * gbn_ref[...], 0.0)
        exte = jnp.concatenate([e, en], axis=0)
        dq = w[2:3] * exte[0:tb] + w[1:2] * exte[1:1 + tb] + w[0:1] * exte[2:2 + tb]
        dgc_ref[...] = dq * hh
        dhh_ref[...] = dq * gc
        dw = jnp.concatenate([_colsum(e * ext[6:6 + tb]), _colsum(e * ext[7:7 + tb]), _colsum(e * ext[8:8 + tb]),
                              jnp.zeros((5, GROUP_W), f32)], axis=0)
        _acc(dw_ref, dw)

    def col(c):
        return pl.BlockSpec((tb, GROUP_W), lambda i: (i, c))

    def prev(c):
        return pl.BlockSpec((8, GROUP_W), lambda i: (jnp.maximum(i * (tb // 8) - 1, 0), c))

    def nxt(c):
        return pl.BlockSpec((8, GROUP_W), lambda i: (jnp.minimum((i + 1) * (tb // 8), last8), c))

    out = pl.BlockSpec((tb, GROUP_W), lambda i: (i, 0))
    return pl.pallas_call(
        body, name="sconv_bwd", grid=(nt,),
        out_shape=[jax.ShapeDtypeStruct((t, GROUP_W), f32)] * 3 + [jax.ShapeDtypeStruct((8, GROUP_W), f32)],
        in_specs=[col(C_GB), col(C_GC), col(C_HH), prev(C_GC), prev(C_HH), nxt(C_GB), col(0), nxt(0), _full((8, GROUP_W))],
        out_specs=[out, out, out, _full((8, GROUP_W))],
        compiler_params=_params(("arbitrary",)),
    )(proj, proj, proj, proj, proj, proj, dy, dy, w)


def _conv4(xr, halo, w, bias):
    tb = xr.shape[0]
    ext = jnp.concatenate([halo, xr], axis=0)
    pre = w[0:1] * ext[5:5 + tb] + w[1:2] * ext[6:6 + tb] + w[2:3] * ext[7:7 + tb] + w[3:4] * ext[8:8 + tb] + bias
    return pre, ext


def _tri():
    r = lax.broadcasted_iota(jnp.int32, (SSD_CHUNK, SSD_CHUNK), 0)
    c = lax.broadcasted_iota(jnp.int32, (SSD_CHUNK, SSD_CHUNK), 1)
    return r >= c


def _lane_pick(vals):
    rows = vals[0].shape[0]
    lane = lax.broadcasted_iota(jnp.int32, (rows, LANES), 1)
    out = jnp.zeros((rows, LANES), f32)
    for h, v in enumerate(vals):
        out = jnp.where(lane == h, v, out)
    return out


def _ssd_fwd(proj, conv_w, conv_b, dt_bias, a_log, d_cols, tb, xchg=None):
    t = proj.shape[0]
    cpt = tb // SSD_CHUNK

    def body(z_ref, xs_ref, bm_ref, cm_ref, xsh_ref, bmh_ref, cmh_ref, dt_ref, cw_ref, cb_ref, dtb_ref, al_ref, dk_ref,
             o_ref, y_ref, st_ref, state):
        i = pl.program_id(0)

        @pl.when(i == 0)
        def _():
            state[...] = jnp.zeros_like(state)

        cw, cb = cw_ref[...], cb_ref[...]
        acts = []
        for j, (r, hr) in enumerate(((xs_ref, xsh_ref), (bm_ref, bmh_ref), (cm_ref, cmh_ref))):
            halo = jnp.where(i > 0, hr[...], 0.0)
            pre, _ = _conv4(r[...], halo, cw[:, j * 256:(j + 1) * 256], cb[:, j * 256:(j + 1) * 256])
            acts.append(_silu(pre))
        xs, bm, cm = acts
        dt = _softplus(dt_ref[...] + dtb_ref[...])
        a = -jnp.exp(al_ref[...])
        adt = dt * a
        tri = _tri()
        trif = tri.astype(f32)
        dk = dk_ref[...]
        for c in range(cpt):
            rows = slice(c * SSD_CHUNK, (c + 1) * SSD_CHUNK)
            acol = _dot_exact(trif, adt[rows])
            arow = acol.T
            dt_c = dt[rows]
            ys = []
            for h in range(SSD_HEADS):
                g = h // 2
                ac = acol[:, h:h + 1]
                lm = jnp.exp(jnp.where(tri, ac - arow[h:h + 1, :], -jnp.inf))
                cg = _b(cm[rows, g * 128:(g + 1) * 128])
                bg = _b(bm[rows, g * 128:(g + 1) * 128])
                xh = xs[rows, h * SSD_P:(h + 1) * SSD_P]
                xdt = xh * dt_c[:, h:h + 1]
                m = _dot_nt(cg, bg) * lm
                s_in = state[h]
                st_ref[c, h] = s_in
                y = _dot(_b(m), _b(xdt)) + jnp.exp(ac) * _dot_nt(cg, _b(s_in)) + xh * dk[:, h * SSD_P:(h + 1) * SSD_P]
                ys.append(y)
                alast = ac[SSD_CHUNK - 1:SSD_CHUNK]
                wdec = jnp.exp(alast - ac)
                state[h] = jnp.exp(alast) * s_in + _dot_tn(_b(xdt * wdec), bg)
            yc = jnp.concatenate(ys, axis=1)
            y_ref[rows, :] = yc
            o_ref[rows, :] = yc * _silu(z_ref[rows, :])

    def col(c):
        return pl.BlockSpec((tb, GROUP_W), lambda i: (i, c))

    def prev(c):
        return pl.BlockSpec((8, GROUP_W), lambda i: (jnp.maximum(i * (tb // 8) - 1, 0), c))

    out = pl.BlockSpec((tb, GROUP_W), lambda i: (i, 0))
    return _call(
        body, name="ssd_fwd", grid=(t // tb,),
        out_shape=[jax.ShapeDtypeStruct((t, GROUP_W), f32), jax.ShapeDtypeStruct((t, GROUP_W), f32),
                   jax.ShapeDtypeStruct((t // SSD_CHUNK, SSD_HEADS, SSD_P, 128), f32)],
        in_specs=[col(C_Z), col(C_XS), col(C_BM), col(C_CM), prev(C_XS), prev(C_BM), prev(C_CM),
                  pl.BlockSpec((tb, LANES), lambda i: (i, C_DT128)),
                  _full((8, 768)), _full((1, 768)), _full((1, LANES)), _full((1, LANES)), _full((1, GROUP_W))],
        out_specs=[out, out, pl.BlockSpec((cpt, SSD_HEADS, SSD_P, 128), lambda i: (i, 0, 0, 0))],
        scratch_shapes=[pltpu.VMEM((SSD_HEADS, SSD_P, 128), f32)],
        semantics=("arbitrary",), xchg=xchg,
        args=(proj, proj, proj, proj, proj, proj, proj, proj, conv_w, conv_b, dt_bias, a_log, d_cols))


def _ssd_bwd(proj, dyc, y_pre, states, conv_w, conv_b, dt_bias, a_log, d_cols, tb, xchg=None):
    t = proj.shape[0]
    nt = t // tb
    cpt = tb // SSD_CHUNK

    def body(z_ref, xs_ref, bm_ref, cm_ref, xsh_ref, bmh_ref, cmh_ref, dt_ref, dy_ref, yp_ref, st_ref,
             cw_ref, cb_ref, dtb_ref, al_ref, dk_ref,
             dz_ref, dxs_ref, dbm_ref, dcm_ref, ddt_ref, dcw_ref, dcb_ref, ddtb_ref, dal_ref, ddk_ref,
             dstate, carry):
        i = pl.program_id(0)
        ti = nt - 1 - i

        @pl.when(i == 0)
        def _():
            dstate[...] = jnp.zeros_like(dstate)
            carry[...] = jnp.zeros_like(carry)

        cw, cb = cw_ref[...], cb_ref[...]
        pres, exts, acts = [], [], []
        for j, (r, hr) in enumerate(((xs_ref, xsh_ref), (bm_ref, bmh_ref), (cm_ref, cmh_ref))):
            halo = jnp.where(ti > 0, hr[...], 0.0)
            pre, ext = _conv4(r[...], halo, cw[:, j * 256:(j + 1) * 256], cb[:, j * 256:(j + 1) * 256])
            pres.append(pre)
            exts.append(ext)
            acts.append(_silu(pre))
        xs, bm, cm = acts
        raw = dt_ref[...] + dtb_ref[...]
        dt = _softplus(raw)
        a = -jnp.exp(al_ref[...])
        adt = dt * a
        tri = _tri()
        trif = tri.astype(f32)
        dk = dk_ref[...]
        z = z_ref[...]
        dyc = dy_ref[...]
        dz_ref[...] = dyc * yp_ref[...] * _dsilu(z)
        dy_all = dyc * _silu(z)
        lane = lax.broadcasted_iota(jnp.int32, (1, LANES), 1)
        ddk_acc = jnp.zeros((1, LANES), f32)
        dal_acc = jnp.zeros((1, LANES), f32)
        dxs_c, dbm_c, dcm_c, ddt_c = [None] * cpt, [None] * cpt, [None] * cpt, [None] * cpt
        for c in reversed(range(cpt)):
            rows = slice(c * SSD_CHUNK, (c + 1) * SSD_CHUNK)
            acol = _dot_exact(trif, adt[rows])
            arow = acol.T
            dt_c = dt[rows]
            da_cols, da_rows, ddt_heads, dxs_heads = [], [], [], []
            dbg = [jnp.zeros((SSD_CHUNK, 128), f32), jnp.zeros((SSD_CHUNK, 128), f32)]
            dcg = [jnp.zeros((SSD_CHUNK, 128), f32), jnp.zeros((SSD_CHUNK, 128), f32)]
            for h in range(SSD_HEADS):
                g = h // 2
                ac = acol[:, h:h + 1]
                lm = jnp.exp(jnp.where(tri, ac - arow[h:h + 1, :], -jnp.inf))
                cgf = cm[rows, g * 128:(g + 1) * 128]
                bgf = bm[rows, g * 128:(g + 1) * 128]
                cg, bg = _b(cgf), _b(bgf)
                xh = xs[rows, h * SSD_P:(h + 1) * SSD_P]
                dth = dt_c[:, h:h + 1]
                xdt = xh * dth
                xb = _b(xdt)
                dy = dy_all[rows, h * SSD_P:(h + 1) * SSD_P]
                dyb = _b(dy)
                s_in = st_ref[c, h]
                sb = _b(s_in)
                dsn = dstate[h]
                dsnb = _b(dsn)
                ea = jnp.exp(ac)
                alast = ac[SSD_CHUNK - 1:SSD_CHUNK]
                wdec = jnp.exp(alast - ac)
                el = jnp.exp(alast)
                m = _dot_nt(cg, bg) * lm
                dm = _dot_nt(dyb, xb)
                dx = _dot_tn(_b(m), dyb)
                dg = _b(dm * lm)
                dcg[g] = dcg[g] + _dot(dg, bg)
                dbg[g] = dbg[g] + _dot_tn(dg, cg)
                wm = dm * m
                da = jnp.sum(wm, axis=1, keepdims=True)
                da_rows.append(jnp.sum(wm, axis=0, keepdims=True))
                yoff = ea * _dot_nt(cg, sb)
                da = da + jnp.sum(dy * yoff, axis=1, keepdims=True)
                dye = _b(dy * ea)
                dcg[g] = dcg[g] + _dot(dye, sb)
                ds_y = _dot_tn(dye, cg)
                t1 = _dot(xb, dsnb)
                dbg[g] = dbg[g] + wdec * t1
                dwv = jnp.sum(t1 * bgf, axis=1, keepdims=True) * wdec
                dx = dx + _dot_nt(_b(bgf * wdec), dsnb)
                dalast = jnp.sum(dwv, axis=0, keepdims=True) + el * jnp.sum(jnp.sum(dsn * s_in, axis=1, keepdims=True), axis=0, keepdims=True)
                da = da - dwv
                rowi = lax.broadcasted_iota(jnp.int32, (SSD_CHUNK, 1), 0)
                da = da + jnp.where(rowi == SSD_CHUNK - 1, dalast, 0.0)
                dstate[h] = el * dsn + ds_y
                da_cols.append(da)
                ddt_heads.append(jnp.sum(dx * xh, axis=1, keepdims=True))
                dkh = dk[:, h * SSD_P:(h + 1) * SSD_P]
                dxs_heads.append(dx * dth + dy * dkh)
                ddk_acc = ddk_acc + jnp.where(lane == h, jnp.sum(_colsum(dy * xh), axis=1, keepdims=True), 0.0)
            da_blk = _lane_pick(da_cols)
            rowsel = lax.broadcasted_iota(jnp.int32, (SSD_CHUNK, SSD_CHUNK), 0)
            da_rows_blk = jnp.zeros((SSD_CHUNK, SSD_CHUNK), f32)
            for h in range(SSD_HEADS):
                da_rows_blk = jnp.where(rowsel == h, da_rows[h], da_rows_blk)
            da_blk = da_blk - da_rows_blk.T
            dadt = lax.dot_general(trif, da_blk, (((0,), (0,)), ((), ())), preferred_element_type=f32,
                                   precision=lax.Precision.HIGHEST)
            dal_acc = dal_acc + _colsum(dadt * dt_c)
            ddt_c[c] = dadt * a + _lane_pick(ddt_heads)
            dxs_c[c] = jnp.concatenate(dxs_heads, axis=1)
            dbm_c[c] = jnp.concatenate(dbg, axis=1)
            dcm_c[c] = jnp.concatenate(dcg, axis=1)
        ddt = jnp.concatenate(ddt_c, axis=0) if cpt > 1 else ddt_c[0]
        ddraw = jnp.where(lane < SSD_HEADS, ddt * jax.nn.sigmoid(raw), 0.0)
        ddt_ref[...] = ddraw
        _acc(ddtb_ref, _colsum(ddraw))
        _acc(dal_ref, jnp.where(lane < SSD_HEADS, dal_acc * a, 0.0))
        _acc(ddk_ref, ddk_acc)
        dcw_parts, dcb_parts = [], []
        for j, (dparts, out_ref) in enumerate(((dxs_c, dxs_ref), (dbm_c, dbm_ref), (dcm_c, dcm_ref))):
            dact = jnp.concatenate(dparts, axis=0) if cpt > 1 else dparts[0]
            dpre = dact * _dsilu(pres[j])
            w = cw[:, j * 256:(j + 1) * 256]
            ext = jnp.concatenate([dpre, carry[:, j * 256:(j + 1) * 256]], axis=0)
            out_ref[...] = w[3:4] * ext[0:tb] + w[2:3] * ext[1:1 + tb] + w[1:2] * ext[2:2 + tb] + w[0:1] * ext[3:3 + tb]
            carry[:, j * 256:(j + 1) * 256] = dpre[0:8]
            xe = exts[j]
            dcw_parts.append(jnp.concatenate([_colsum(dpre * xe[5 + k:5 + k + tb]) for k in range(4)]
                                             + [jnp.zeros((4, GROUP_W), f32)], axis=0))
            dcb_parts.append(_colsum(dpre))
        _acc(dcw_ref, jnp.concatenate(dcw_parts, axis=1))
        _acc(dcb_ref, jnp.concatenate(dcb_parts, axis=1))

    def col(c):
        return pl.BlockSpec((tb, GROUP_W), lambda i: (nt - 1 - i, c))

    def prev(c):
        return pl.BlockSpec((8, GROUP_W), lambda i: (jnp.maximum((nt - 1 - i) * (tb // 8) - 1, 0), c))

    out = pl.BlockSpec((tb, GROUP_W), lambda i: (nt - 1 - i, 0))
    vec = _full((1, LANES))
    return _call(
        body, name="ssd_bwd", grid=(nt,),
        out_shape=[jax.ShapeDtypeStruct((t, GROUP_W), f32)] * 4 + [jax.ShapeDtypeStruct((t, LANES), f32),
                   jax.ShapeDtypeStruct((8, 768), f32), jax.ShapeDtypeStruct((1, 768), f32)]
        + [jax.ShapeDtypeStruct((1, LANES), f32)] * 3,
        in_specs=[col(C_Z), col(C_XS), col(C_BM), col(C_CM), prev(C_XS), prev(C_BM), prev(C_CM),
                  pl.BlockSpec((tb, LANES), lambda i: (nt - 1 - i, C_DT128)), out, out,
                  pl.BlockSpec((cpt, SSD_HEADS, SSD_P, 128), lambda i: (nt - 1 - i, 0, 0, 0)),
                  _full((8, 768)), _full((1, 768)), vec, vec, _full((1, GROUP_W))],
        out_specs=[out, out, out, out, pl.BlockSpec((tb, LANES), lambda i: (nt - 1 - i, 0)),
                   _full((8, 768)), _full((1, 768)), vec, vec, vec],
        scratch_shapes=[pltpu.VMEM((SSD_HEADS, SSD_P, 128), f32), pltpu.VMEM((8, 768), f32)],
        semantics=("arbitrary",), xchg=xchg,
        args=(proj, proj, proj, proj, proj, proj, proj, proj, dyc, y_pre, states, conv_w, conv_b, dt_bias, a_log, d_cols))


def _s5_coeffs(are, aim, ls):
    step = jnp.exp(ls)
    mag = jnp.exp(are * step)
    th = aim * step
    lre, lim = mag * jnp.cos(th), mag * jnp.sin(th)
    den = are * are + aim * aim
    nr = lre - 1.0
    fre = (nr * are + lim * aim) / den
    fim = (lim * are - nr * aim) / den
    return step, lre, lim, den, fre, fim


def _s5_prep(are, aim, ls, bre_bd, bim_bd):
    def body(are_ref, aim_ref, ls_ref, bre_ref, bim_ref, lre_ref, lim_ref, bbr_ref, bbi_ref):
        _, lre, lim, _, fre, fim = _s5_coeffs(are_ref[...], aim_ref[...], ls_ref[...])
        lre_ref[...] = lre
        lim_ref[...] = lim
        bre, bim = bre_ref[...], bim_ref[...]
        bbr_ref[...] = fre * bre - fim * bim
        bbi_ref[...] = fre * bim + fim * bre

    col = jax.ShapeDtypeStruct((S5_N, 1), f32)
    mat = jax.ShapeDtypeStruct((S5_N, GROUP_W), f32)
    return pl.pallas_call(body, name="s5_prep", out_shape=[col, col, mat, mat], in_specs=[VMEM] * 5, out_specs=[VMEM] * 4,
                          compiler_params=_params())(are, aim, ls, bre_bd, bim_bd)


def _s5_prep_bwd(are, aim, ls, bre_bd, bim_bd, dlre, dlim, dbbr, dbbi):
    def body(are_ref, aim_ref, ls_ref, bre_ref, bim_ref, dlre_ref, dlim_ref, dbbr_ref, dbbi_ref,
             dare_ref, daim_ref, dls_ref, dbre_ref, dbim_ref):
        are, aim = are_ref[...], aim_ref[...]
        step, lre, lim, den, fre, fim = _s5_coeffs(are, aim, ls_ref[...])
        r = lax.broadcasted_iota(jnp.int32, (S5_N, GROUP_W), 0) // 64
        c = lax.broadcasted_iota(jnp.int32, (S5_N, GROUP_W), 1) // 16
        mask = r == c
        gr = jnp.where(mask, dbbr_ref[...], 0.0)
        gi = jnp.where(mask, dbbi_ref[...], 0.0)
        bre, bim = bre_ref[...], bim_ref[...]
        dbre_ref[...] = fre * gr + fim * gi
        dbim_ref[...] = fre * gi - fim * gr
        dfre = jnp.sum(bre * gr + bim * gi, axis=1, keepdims=True)
        dfim = jnp.sum(bre * gi - bim * gr, axis=1, keepdims=True)
        ire, iim = are / den, aim / den
        tre = dlre_ref[...] + ire * dfre - iim * dfim
        tim = dlim_ref[...] + ire * dfim + iim * dfre
        dzre = lre * tre + lim * tim
        dzim = lre * tim - lim * tre
        qre = (fre * are + fim * aim) / den
        qim = (fim * are - fre * aim) / den
        dare_ref[...] = step * dzre - (qre * dfre + qim * dfim)
        daim_ref[...] = step * dzim - (qre * dfim - qim * dfre)
        dls = (are * dzre + aim * dzim) * step
        sel = (lax.broadcasted_iota(jnp.int32, (S5_N, LANES), 0) // 64 == lax.broadcasted_iota(jnp.int32, (S5_N, LANES), 1)).astype(f32)
        dls_ref[...] = lax.dot_general(sel, jnp.broadcast_to(dls, (S5_N, LANES)), (((0,), (0,)), ((), ())),
                                       preferred_element_type=f32, precision=lax.Precision.HIGHEST)

    col = jax.ShapeDtypeStruct((S5_N, 1), f32)
    mat = jax.ShapeDtypeStruct((S5_N, GROUP_W), f32)
    return pl.pallas_call(body, name="s5_prep_bwd", out_shape=[col, col, jax.ShapeDtypeStruct((LANES, LANES), f32), mat, mat],
                          in_specs=[VMEM] * 9, out_specs=[VMEM] * 5, compiler_params=_params(),
                          )(are, aim, ls, bre_bd, bim_bd, dlre, dlim, dbbr, dbbi)


def _cmul(ar, ai, br, bi):
    return ar * br - ai * bi, ar * bi + ai * br


def _s5_scan(re_ref, im_ref, carry_ref, mr, mi, n_groups, reverse):
    p1 = (mr, mi)
    p2 = _cmul(*p1, *p1)
    p3 = _cmul(*p2, *p1)
    p4 = _cmul(*p2, *p2)
    p5 = _cmul(*p4, *p1)
    p6 = _cmul(*p4, *p2)
    p7 = _cmul(*p4, *p3)
    p8 = _cmul(*p4, *p4)
    pows = [p1, p2, p3, p4, p5, p6, p7, p8]
    row = lax.broadcasted_iota(jnp.int32, (8, S5_N), 0)
    tr = jnp.zeros((8, S5_N), f32)
    ti = jnp.zeros((8, S5_N), f32)
    for i in range(8):
        p = pows[7 - i] if reverse else pows[i]
        tr = jnp.where(row == i, p[0], tr)
        ti = jnp.where(row == i, p[1], ti)
    steps = []
    for k, p in ((1, p1), (2, p2), (4, p4)):
        keep = (row + k < 8) if reverse else (row >= k)
        steps.append((8 - k if reverse else k, keep, jnp.broadcast_to(p[0], (8, S5_N)), jnp.broadcast_to(p[1], (8, S5_N))))
    edge = 0 if reverse else 7

    def step(j, carry):
        cr, ci = carry
        g = (n_groups - 1 - j) if reverse else j
        r0 = pl.multiple_of(g * 8, 8)
        xr = re_ref[pl.ds(r0, 8), :]
        xi = im_ref[pl.ds(r0, 8), :]
        for shift, keep, br, bi in steps:
            sr = jnp.where(keep, pltpu.roll(xr, shift, 0), 0.0)
            si = jnp.where(keep, pltpu.roll(xi, shift, 0), 0.0)
            xr, xi = xr + br * sr - bi * si, xi + br * si + bi * sr
        xr, xi = xr + tr * cr - ti * ci, xi + tr * ci + ti * cr
        re_ref[pl.ds(r0, 8), :] = xr
        im_ref[pl.ds(r0, 8), :] = xi
        return (jnp.broadcast_to(xr[edge:edge + 1, :], (8, S5_N)), jnp.broadcast_to(xi[edge:edge + 1, :], (8, S5_N)))

    cr, ci = lax.fori_loop(0, n_groups, step, (carry_ref[0], carry_ref[1]))
    carry_ref[0] = cr
    carry_ref[1] = ci


def _s5_output(u, xr, xi, ctr, cti, d):
    return _dot_nt(_b(xr), _b(ctr)) - _dot_nt(_b(xi), _b(cti)) + d * u


def _s5_fwd(proj, bbr, bbi, ctr, cti, lre, lim, d, glu_w, glu_b, tb, xchg=None):
    t = proj.shape[0]

    def body(u_ref, bbr_ref, bbi_ref, ctr_ref, cti_ref, lr_ref, li_ref, d_ref, gw_ref, gb_ref, o_ref, xr_ref, xi_ref, carry):
        @pl.when(pl.program_id(0) == 0)
        def _():
            carry[...] = jnp.zeros_like(carry)

        u = u_ref[...]
        ub = _b(u)
        xr_ref[...] = _dot_nt(ub, _b(bbr_ref[...]))
        xi_ref[...] = _dot_nt(ub, _b(bbi_ref[...]))
        _s5_scan(xr_ref, xi_ref, carry, lr_ref[...], li_ref[...], tb // 8, reverse=False)
        y = _s5_output(u, xr_ref[...], xi_ref[...], ctr_ref[...], cti_ref[...], d_ref[...])
        gl = _gelu(y)
        o_ref[...] = gl * jax.nn.sigmoid(_dot(_b(gl), _b(gw_ref[...])) + gb_ref[...])

    state = pl.BlockSpec((tb, S5_N), lambda i: (i, 0))
    return _call(
        body, name="s5_fwd", grid=(t // tb,),
        out_shape=[jax.ShapeDtypeStruct((t, GROUP_W), f32), jax.ShapeDtypeStruct((t, S5_N), f32), jax.ShapeDtypeStruct((t, S5_N), f32)],
        in_specs=[pl.BlockSpec((tb, GROUP_W), lambda i: (i, C_S5)), _full((S5_N, GROUP_W)), _full((S5_N, GROUP_W)),
                  _full((GROUP_W, S5_N)), _full((GROUP_W, S5_N)), _full((1, S5_N)), _full((1, S5_N)),
                  _full((1, GROUP_W)), _full((GROUP_W, GROUP_W)), _full((1, GROUP_W))],
        out_specs=[pl.BlockSpec((tb, GROUP_W), lambda i: (i, 0)), state, state],
        scratch_shapes=[pltpu.VMEM((2, 8, S5_N), f32)],
        semantics=("arbitrary",), xchg=xchg, args=(proj, bbr, bbi, ctr, cti, lre, lim, d, glu_w, glu_b))


def _s5_bwd(proj, dyd, xr_all, xi_all, bbr, bbi, ctr, cti, lre, lim, d, glu_w, glu_b, tb, xchg=None):
    t = proj.shape[0]
    nt = t // tb

    def body(u_ref, dy_ref, xr_ref, xi_ref, xrh_ref, xih_ref, bbr_ref, bbi_ref, ctr_ref, cti_ref, lr_ref, li_ref,
             d_ref, gw_ref, gb_ref,
             du_ref, dlr_ref, dli_ref, dbbr_ref, dbbi_ref, dctr_ref, dcti_ref, dd_ref, dgw_ref, dgb_ref,
             gr_ref, gi_ref, carry):
        i = pl.program_id(0)
        ti = nt - 1 - i

        @pl.when(i == 0)
        def _():
            carry[...] = jnp.zeros_like(carry)

        u = u_ref[...]
        ub = _b(u)
        xr, xi = xr_ref[...], xi_ref[...]
        ctr, cti = _b(ctr_ref[...]), _b(cti_ref[...])
        d = d_ref[...]
        gw = _b(gw_ref[...])
        y = _s5_output(u, xr, xi, ctr, cti, d)
        gl = _gelu(y)
        sg = jax.nn.sigmoid(_dot(_b(gl), gw) + gb_ref[...])
        dout = dy_ref[...]
        q = dout * gl * sg * (1.0 - sg)
        qb = _b(q)
        dgl = dout * sg + _dot_nt(qb, gw)
        _acc(dgw_ref, _dot_tn(_b(gl), qb))
        _acc(dgb_ref, _colsum(q))
        dyv = dgl * _dgelu(y)
        _acc(dd_ref, _colsum(dyv * u))
        dyb = _b(dyv)
        gr_ref[...] = _dot(dyb, ctr)
        gi_ref[...] = -_dot(dyb, cti)
        _acc(dctr_ref, _dot_tn(dyb, _b(xr)))
        _acc(dcti_ref, -_dot_tn(dyb, _b(xi)))
        _s5_scan(gr_ref, gi_ref, carry, lr_ref[...], -li_ref[...], tb // 8, reverse=True)
        gr, gi = gr_ref[...], gi_ref[...]
        xpr = jnp.concatenate([jnp.where(ti > 0, xrh_ref[...], 0.0), xr], axis=0)[7:7 + tb]
        xpi = jnp.concatenate([jnp.where(ti > 0, xih_ref[...], 0.0), xi], axis=0)[7:7 + tb]
        _acc(dlr_ref, _colsum(gr * xpr + gi * xpi))
        _acc(dli_ref, _colsum(gi * xpr - gr * xpi))
        grb, gib = _b(gr), _b(gi)
        _acc(dbbr_ref, _dot_tn(grb, ub))
        _acc(dbbi_ref, _dot_tn(gib, ub))
        du_ref[...] = dyv * d + _dot(grb, _b(bbr_ref[...])) + _dot(gib, _b(bbi_ref[...]))

    state = pl.BlockSpec((tb, S5_N), lambda i: (nt - 1 - i, 0))
    prev = pl.BlockSpec((8, S5_N), lambda i: (jnp.maximum((nt - 1 - i) * (tb // 8) - 1, 0), 0))
    tile = pl.BlockSpec((tb, GROUP_W), lambda i: (nt - 1 - i, 0))
    return _call(
        body, name="s5_bwd", grid=(nt,),
        out_shape=[jax.ShapeDtypeStruct((t, GROUP_W), f32), jax.ShapeDtypeStruct((1, S5_N), f32), jax.ShapeDtypeStruct((1, S5_N), f32),
                   jax.ShapeDtypeStruct((S5_N, GROUP_W), f32), jax.ShapeDtypeStruct((S5_N, GROUP_W), f32),
                   jax.ShapeDtypeStruct((GROUP_W, S5_N), f32), jax.ShapeDtypeStruct((GROUP_W, S5_N), f32),
                   jax.ShapeDtypeStruct((1, GROUP_W), f32), jax.ShapeDtypeStruct((GROUP_W, GROUP_W), f32),
                   jax.ShapeDtypeStruct((1, GROUP_W), f32)],
        in_specs=[pl.BlockSpec((tb, GROUP_W), lambda i: (nt - 1 - i, C_S5)), tile, state, state, prev, prev,
                  _full((S5_N, GROUP_W)), _full((S5_N, GROUP_W)), _full((GROUP_W, S5_N)), _full((GROUP_W, S5_N)),
                  _full((1, S5_N)), _full((1, S5_N)), _full((1, GROUP_W)), _full((GROUP_W, GROUP_W)), _full((1, GROUP_W))],
        out_specs=[tile, _full((1, S5_N)), _full((1, S5_N)), _full((S5_N, GROUP_W)), _full((S5_N, GROUP_W)),
                   _full((GROUP_W, S5_N)), _full((GROUP_W, S5_N)), _full((1, GROUP_W)), _full((GROUP_W, GROUP_W)), _full((1, GROUP_W))],
        scratch_shapes=[pltpu.VMEM((tb, S5_N), f32), pltpu.VMEM((tb, S5_N), f32), pltpu.VMEM((2, 8, S5_N), f32)],
        semantics=("arbitrary",), xchg=xchg,
        args=(proj, dyd, xr_all, xi_all, xr_all, xi_all, bbr, bbi, ctr, cti, lre, lim, d, glu_w, glu_b))


def _outproj_fwd(ys, h, bn_w, g1, w_out, tb):
    t = h.shape[0]

    def body(ya_ref, yb_ref, yc_ref, yd_ref, h_ref, bn_ref, g1_ref, w_ref, h1_ref, o_ref, gr_ref):
        bn = bn_ref[...]
        parts = []
        for g, r in enumerate((ya_ref, yb_ref, yc_ref, yd_ref)):
            n, _ = _rms(r[...])
            parts.append(n * bn[:, g * GROUP_W:(g + 1) * GROUP_W])
        groups = _b(jnp.concatenate(parts, axis=1))
        gr_ref[...] = groups
        o = _dot(groups, w_ref[...])
        o_ref[...] = o
        h1_ref[...] = h_ref[...] + g1_ref[...] * o

    grp = pl.BlockSpec((tb, GROUP_W), lambda i: (i, 0))
    row = pl.BlockSpec((tb, D_MODEL), lambda i: (i, 0))
    vec = _full((1, D_MODEL))
    return pl.pallas_call(
        body, name="outproj_fwd", grid=(t // tb,),
        out_shape=[jax.ShapeDtypeStruct((t, D_MODEL), f32), jax.ShapeDtypeStruct((t, D_MODEL), f32),
                   jax.ShapeDtypeStruct((t, D_MODEL), bf16)],
        in_specs=[grp, grp, grp, grp, row, vec, vec, _full((D_MODEL, D_MODEL))],
        out_specs=[row, row, row],
        compiler_params=_params(("parallel",)),
    )(*ys, h, bn_w, g1, w_out)


def _outproj_bwd(dh1, o, ys, bn_w, g1, w_out, tb):
    t = dh1.shape[0]

    def body(dh_ref, o_ref, ya_ref, yb_ref, yc_ref, yd_ref, bn_ref, g1_ref, w_ref,
             da_ref, db_ref, dc_ref, dd_ref, do_ref, dg1_ref, dbn_ref):
        dh = dh_ref[...]
        _acc(dg1_ref, _colsum(dh * o_ref[...]))
        do = _b(dh * g1_ref[...])
        do_ref[...] = do
        dgroups = _dot_nt(do, w_ref[...])
        bn = bn_ref[...]
        dbn = []
        for g, (r, dr) in enumerate(((ya_ref, da_ref), (yb_ref, db_ref), (yc_ref, dc_ref), (yd_ref, dd_ref))):
            n, rr = _rms(r[...])
            dgr = dgroups[:, g * GROUP_W:(g + 1) * GROUP_W]
            dbn.append(_colsum(dgr * n))
            dr[...] = _rms_bwd(dgr * bn[:, g * GROUP_W:(g + 1) * GROUP_W], n, rr)
        _acc(dbn_ref, jnp.concatenate(dbn, axis=1))

    grp = pl.BlockSpec((tb, GROUP_W), lambda i: (i, 0))
    row = pl.BlockSpec((tb, D_MODEL), lambda i: (i, 0))
    vec = _full((1, D_MODEL))
    return pl.pallas_call(
        body, name="outproj_bwd", grid=(t // tb,),
        out_shape=[jax.ShapeDtypeStruct((t, GROUP_W), f32)] * 4 + [jax.ShapeDtypeStruct((t, D_MODEL), bf16),
                   jax.ShapeDtypeStruct((1, D_MODEL), f32), jax.ShapeDtypeStruct((1, D_MODEL), f32)],
        in_specs=[row, row, grp, grp, grp, grp, vec, vec, _full((D_MODEL, D_MODEL))],
        out_specs=[grp, grp, grp, grp, row, vec, vec],
        compiler_params=_params(("arbitrary",)),
    )(dh1, o, *ys, bn_w, g1, w_out)


def _mlp_fwd(h1, norm_w, sc, sh, g2, w1, w2, tb, xchg=None):
    t = h1.shape[0]
    nh = w1.shape[0]

    def body(h_ref, nw_ref, sc_ref, sh_ref, g2_ref, w1_ref, w2_ref, h2_ref, m_ref, a_ref, v_ref, r_ref, acc):
        j = pl.program_id(1)

        @pl.when(j == 0)
        def _():
            n, _ = _rms(h_ref[...])
            v_ref[...] = _b(n * nw_ref[...] * (1.0 + sc_ref[...]) + sh_ref[...])

        a = _dot(v_ref[...], w1_ref[0])
        a_ref[...] = a
        ra = jnp.maximum(a, 0.0)
        r = _b(ra * ra)
        r_ref[...] = r
        p = _dot(r, w2_ref[0])

        @pl.when(j == 0)
        def _():
            acc[...] = p

        @pl.when(j != 0)
        def _():
            acc[...] += p

        @pl.when(j == nh - 1)
        def _():
            m = acc[...]
            m_ref[...] = m
            h2_ref[...] = h_ref[...] + g2_ref[...] * m

    row = pl.BlockSpec((tb, D_MODEL), lambda i, j: (i, 0))
    hid = pl.BlockSpec((tb, MLP_HB), lambda i, j: (i, j))
    vec = _full((1, D_MODEL))
    return _call(
        body, name="mlp_fwd", grid=(t // tb, nh),
        out_shape=[jax.ShapeDtypeStruct((t, D_MODEL), f32), jax.ShapeDtypeStruct((t, D_MODEL), f32),
                   jax.ShapeDtypeStruct((t, nh * MLP_HB), f32), jax.ShapeDtypeStruct((t, D_MODEL), bf16),
                   jax.ShapeDtypeStruct((t, nh * MLP_HB), bf16)],
        in_specs=[row, vec, vec, vec, vec, pl.BlockSpec((1, D_MODEL, MLP_HB), lambda i, j: (j, 0, 0)),
                  pl.BlockSpec((1, MLP_HB, D_MODEL), lambda i, j: (j, 0, 0))],
        out_specs=[row, row, hid, row, hid],
        scratch_shapes=[pltpu.VMEM((tb, D_MODEL), f32)],
        semantics=("arbitrary", "arbitrary"), xchg=xchg, args=(h1, norm_w, sc, sh, g2, w1, w2))


def _mlp_bwd(dh2, m, h1, a, norm_w, sc, sh, g2, w1, w2, tb, xchg=None):
    t = h1.shape[0]
    nh = w1.shape[0]

    def body(dh_ref, m_ref, h_ref, a_ref, nw_ref, sc_ref, sh_ref, g2_ref, w1_ref, w2_ref,
             dh1_ref, do_ref, da_ref, dg2_ref, dsh_ref, dsc_ref, dnw_ref, acc):
        i, j = pl.program_id(0), pl.program_id(1)

        @pl.when(j == 0)
        def _():
            dh = dh_ref[...]
            _acc(dg2_ref, _colsum(dh * m_ref[...]))
            do_ref[...] = _b(dh * g2_ref[...])

        dr = _dot_nt(do_ref[...], w2_ref[0])
        da = _b(dr * 2.0 * jnp.maximum(a_ref[...], 0.0))
        da_ref[...] = da
        p = _dot_nt(da, w1_ref[0])

        @pl.when(j == 0)
        def _():
            acc[...] = p

        @pl.when(j != 0)
        def _():
            acc[...] += p

        @pl.when(j == nh - 1)
        def _():
            dv = acc[...]
            n, r = _rms(h_ref[...])
            nw = nw_ref[...]
            gain = 1.0 + sc_ref[...]
            _acc(dsh_ref, _colsum(dv))
            _acc(dsc_ref, _colsum(dv * n * nw))
            _acc(dnw_ref, _colsum(dv * gain * n))
            dh1_ref[...] = dh_ref[...] + _rms_bwd(dv * nw * gain, n, r)

    row = pl.BlockSpec((tb, D_MODEL), lambda i, j: (i, 0))
    hid = pl.BlockSpec((tb, MLP_HB), lambda i, j: (i, j))
    vec = _full((1, D_MODEL))
    return _call(
        body, name="mlp_bwd", grid=(t // tb, nh),
        out_shape=[jax.ShapeDtypeStruct((t, D_MODEL), f32), jax.ShapeDtypeStruct((t, D_MODEL), bf16),
                   jax.ShapeDtypeStruct((t, nh * MLP_HB), bf16)] + [jax.ShapeDtypeStruct((1, D_MODEL), f32)] * 4,
        in_specs=[row, row, row, hid, vec, vec, vec, vec, pl.BlockSpec((1, D_MODEL, MLP_HB), lambda i, j: (j, 0, 0)),
                  pl.BlockSpec((1, MLP_HB, D_MODEL), lambda i, j: (j, 0, 0))],
        out_specs=[row, row, hid, vec, vec, vec, vec],
        scratch_shapes=[pltpu.VMEM((tb, D_MODEL), f32)],
        semantics=("arbitrary", "arbitrary"), xchg=xchg, args=(dh2, m, h1, a, norm_w, sc, sh, g2, w1, w2))


def _loss_head(h, target, norm_w, tb):
    t = h.shape[0]

    def body(h_ref, t_ref, w_ref, loss_ref, dh_ref, dw_ref):
        n, r = _rms(h_ref[...])
        w = w_ref[...]
        err = n * w - t_ref[...]
        part = 0.5 * jnp.sum(jnp.sum(err * err, axis=1, keepdims=True), axis=0, keepdims=True) / D_MODEL
        _acc(loss_ref, jnp.broadcast_to(part, (8, LANES)))
        dy = err / D_MODEL
        _acc(dw_ref, _colsum(dy * n))
        dh_ref[...] = _rms_bwd(dy * w, n, r)

    row = pl.BlockSpec((tb, D_MODEL), lambda i: (i, 0))
    return pl.pallas_call(
        body, name="loss_head", grid=(t // tb,),
        out_shape=[jax.ShapeDtypeStruct((8, LANES), f32), jax.ShapeDtypeStruct((t, D_MODEL), f32),
                   jax.ShapeDtypeStruct((1, D_MODEL), f32)],
        in_specs=[row, row, _full((1, D_MODEL))],
        out_specs=[_full((8, LANES)), row, _full((1, D_MODEL))],
        compiler_params=_params(("arbitrary",)),
    )(h, target, norm_w)


def _adam_math(w, g, m, v):
    m2 = ADAM_B1 * m + (1.0 - ADAM_B1) * g
    v2 = ADAM_B2 * v + (1.0 - ADAM_B2) * (g * g)
    mh = m2 / (1.0 - ADAM_B1 ** ADAM_STEP)
    vh = v2 / (1.0 - ADAM_B2 ** ADAM_STEP)
    return -ADAM_LR * (mh / (jnp.sqrt(vh) + ADAM_EPS) + ADAM_WD * w), m2, v2


def _sum_adamw(parts, w, m, v, name, rb):
    n_src, r, c = parts.shape

    def body(p_ref, w_ref, m_ref, v_ref, g_ref, d_ref, m2_ref, v2_ref):
        g = p_ref[0].astype(f32)
        for s in range(1, n_src):
            g = g + p_ref[s].astype(f32)
        g_ref[...] = g
        d, m2, v2 = _adam_math(w_ref[...], g, m_ref[...], v_ref[...])
        d_ref[...] = d
        m2_ref[...] = m2
        v2_ref[...] = v2

    blk = pl.BlockSpec((rb, c), lambda i: (i, 0))
    return pl.pallas_call(
        body, name=name, grid=(r // rb,),
        out_shape=[jax.ShapeDtypeStruct((r, c), f32)] * 4,
        in_specs=[pl.BlockSpec((n_src, rb, c), lambda i: (0, i, 0)), blk, blk, blk],
        out_specs=[blk] * 4,
        compiler_params=_params(("parallel",)),
    )(parts, w, m, v)


def _sum_adamw_layers(parts0, parts1, w, m, v, name, rb):
    n_src, r, c = parts0.shape
    nb = r // rb

    def body(p0_ref, p1_ref, w_ref, m_ref, v_ref, g_ref, d_ref, m2_ref, v2_ref):
        def update(p_ref):
            g = p_ref[0].astype(f32)
            for s in range(1, n_src):
                g = g + p_ref[s].astype(f32)
            g_ref[0] = g
            d, m2, v2 = _adam_math(w_ref[0], g, m_ref[0], v_ref[0])
            d_ref[0] = d
            m2_ref[0] = m2
            v2_ref[0] = v2

        @pl.when(pl.program_id(0) == 0)
        def _():
            update(p0_ref)

        @pl.when(pl.program_id(0) == 1)
        def _():
            update(p1_ref)

    blk = pl.BlockSpec((1, rb, c), lambda l, i: (l, i, 0))
    return pl.pallas_call(
        body, name=name, grid=(2, nb),
        out_shape=[jax.ShapeDtypeStruct((2, r, c), f32)] * 4,
        in_specs=[pl.BlockSpec((n_src, rb, c), lambda l, i: (0, jnp.where(l == 0, i, nb - 1), 0)),
                  pl.BlockSpec((n_src, rb, c), lambda l, i: (0, jnp.where(l == 1, i, 0), 0)), blk, blk, blk],
        out_specs=[blk] * 4,
        compiler_params=_params(("arbitrary", "arbitrary")),
    )(parts0, parts1, w, m, v)


def _reorder_in(w):
    pad = jnp.zeros(w.shape[:-1] + (P_IN - 2308,), w.dtype)
    return jnp.concatenate([w[..., :2048], w[..., 2052:2308], w[..., 2048:2052], pad], axis=-1)


def _unreorder_in(w):
    return jnp.concatenate([w[..., :2048], w[..., 2304:2308], w[..., 2048:2304]], axis=-1)


def _block_diag(w2d, n_blocks):
    rows, cols = w2d.shape
    tiled = jnp.tile(w2d, (1, n_blocks))
    rb = lax.broadcasted_iota(jnp.int32, tiled.shape, 0) // (rows // n_blocks)
    cb = lax.broadcasted_iota(jnp.int32, tiled.shape, 1) // cols
    return jnp.where(rb == cb, tiled, jnp.zeros_like(tiled))


def _block_diag_extract(w_bd, n_blocks):
    rows, wide = w_bd.shape
    r, c = rows // n_blocks, wide // n_blocks
    w4 = w_bd.reshape(n_blocks, r, n_blocks, c)
    idx = jnp.arange(n_blocks)
    return w4[idx, :, idx, :]


def _lanes128(v):
    return jnp.pad(v.reshape(1, -1), ((0, 0), (0, LANES - v.size)))


def _rows_of(shape):
    n = 1
    for d in shape:
        n *= d
    return -(-n // (8 * LANES)) * 8, n


def _flat_pack(arrs, row_multiple=8):
    blocks = []
    for a in arrs:
        rows, n = _rows_of(a.shape)
        blocks.append(jnp.pad(a.reshape(-1), (0, rows * LANES - n)).reshape(rows, LANES))
    total = sum(b.shape[0] for b in blocks)
    pad = -total % row_multiple
    if pad:
        blocks.append(jnp.zeros((pad, LANES), blocks[0].dtype))
    return jnp.concatenate(blocks, axis=0)


def _flat_unpack(packed, shapes):
    out, off = [], 0
    for s in shapes:
        rows, n = _rows_of(s)
        out.append(packed[off:off + rows].reshape(-1)[:n].reshape(s))
        off += rows
    return out


_W_NAMES = ['norm_mix_w', 'norm_mlp_w', 'ada_w', 'ada_b', 'w_in', 'pool_w', 'pool_scale', 'sconv_w', 'ssd_conv_w',
            'ssd_conv_b', 'ssd_dt_bias', 'ssd_a_log', 'ssd_d', 's5_a_re', 's5_a_im', 's5_log_step', 's5_b_re', 's5_b_im',
            's5_c_re', 's5_c_im', 's5_d', 's5_glu_w', 's5_glu_b', 'branch_norm_w', 'w_out', 'mlp_w1', 'mlp_w2',
            'final_norm_w']
_BIG = ('ada_w', 'w_in', 'w_out', 'mlp_w1', 'mlp_w2')
_SMALL = [n for n in _W_NAMES if n not in _BIG]
_SHARDED_SMALL = {'sconv_w': (2, 32), 'ssd_conv_w': (2, 96), 's5_glu_w': (1, 32)}


def _gather(*blocks):
    return _Exchange(blocks, gather=True)


def _scatter(*parts):
    return _Exchange(parts, gather=False)


def _layer_forward(l, h, p, w, sh_b, tb):
    first = l == 0
    (proj, u_b), got = _inproj_fwd(h, p['norm_mix_w'][l], p['sc1'][l], p['sh1'][l], w['w_in', l], tb,
                                   xchg=_gather(sh_b[1][0]) if first else None)
    if first:
        w['w_out', 0] = got[0].reshape(D_MODEL, D_MODEL)
    ya = _pool_fwd(proj, p['pool_bd'][l], p['pool_scale'][l], tb)
    yb = _sconv_fwd(proj, p['sconv_w8'][l], tb)
    (yc, yc_pre, states), got = _ssd_fwd(proj, p['ssd_conv_w8'][l], p['ssd_conv_b'][l], p['ssd_dt_bias'][l], p['ssd_a_log'][l],
                                         p['ssd_d_cols'][l], tb, xchg=_gather(sh_b[2][0]) if first else None)
    if first:
        w['w1', 0] = got[0]
    (yd, xr, xi), got = _s5_fwd(proj, p['bbr'][l], p['bbi'][l], p['ctr'][l], p['cti'][l], p['lre'][l], p['lim'][l],
                                p['s5_d'][l], p['glu_w'][l], p['glu_b'][l], tb, xchg=_gather(sh_b[3][0]) if first else None)
    if first:
        w['w2', 0] = got[0]
    ys = (ya, yb, yc, yd)
    h1, o, groups_b = _outproj_fwd(ys, h, p['branch_norm_w'][l], p['g1'][l], w['w_out', l], tb)
    (h2, m, a, v_b, r_b), got = _mlp_fwd(h1, p['norm_mlp_w'][l], p['sc2'][l], p['sh2'][l], p['g2'][l], w['w1', l], w['w2', l], tb,
                                         xchg=_gather(*[sh_b[k][1] for k in range(4)]) if first else None)
    if first:
        w['w_in', 1] = got[0].reshape(D_MODEL, P_IN)
        w['w_out', 1] = got[1].reshape(D_MODEL, D_MODEL)
        w['w1', 1], w['w2', 1] = got[2], got[3]
    saved = dict(h=h, proj=proj, u_b=u_b, ys=ys, yc_pre=yc_pre, states=states, xr=xr, xi=xi, h1=h1, o=o,
                 groups_b=groups_b, m=m, a=a, v_b=v_b, r_b=r_b)
    return h2, saved


def _layer_backward(l, dh2, s, p, w, pending, recv, tb):
    def carry(names):
        names = [n for n in names if n in pending]
        return names, (_scatter(*[pending.pop(n) for n in names]) if names else None)

    def landed(names, got):
        for n, g in zip(names, got):
            recv[n] = g

    names, xchg = carry([('w_out', 1), ('w_in', 1)])
    (dh1, do2_b, da_b, dg2, dsh2, dsc2, dnw_mlp), got = _mlp_bwd(dh2, s['m'], s['h1'], s['a'], p['norm_mlp_w'][l], p['sc2'][l],
                                                                p['sh2'][l], p['g2'][l], w['w1', l], w['w2', l], tb, xchg=xchg)
    landed(names, got)
    pending['mlp_w2', l] = _wgrad(s['r_b'], do2_b, 1, "wgrad_w2", tm=512).reshape(N_DEV, MLP_HB, D_MODEL)
    pending['mlp_w1', l] = _wgrad(s['v_b'], da_b, N_DEV, "wgrad_w1", tm=1024)
    dya, dyb, dyc, dyd, do1_b, dg1, dbn = _outproj_bwd(dh1, s['o'], s['ys'], p['branch_norm_w'][l], p['g1'][l], w['w_out', l], tb)
    pending['w_out', l] = _wgrad(s['groups_b'], do1_b, 1, "wgrad_wout", tm=512).reshape(N_DEV, D_MODEL // N_DEV, D_MODEL)
    proj = s['proj']
    dv, dpool_bd, dpool_scale = _pool_bwd(proj, dya, p['pool_bd'][l], p['pool_scale'][l], tb)
    dgb, dgc, dhh, dsconv = _sconv_bwd(proj, dyb, p['sconv_w8'][l], tb)
    names, xchg = carry([('mlp_w1', l)] + ([('w_out', 0)] if l == 0 else []))
    (dz, dxs, dbm, dcm, ddt, dconv_w, dconv_b, ddtb, dalog, ddskip), got = _ssd_bwd(
        proj, dyc, s['yc_pre'], s['states'], p['ssd_conv_w8'][l], p['ssd_conv_b'][l], p['ssd_dt_bias'][l], p['ssd_a_log'][l],
        p['ssd_d_cols'][l], tb, xchg=xchg)
    landed(names, got)
    names, xchg = carry([('mlp_w2', l)])
    (du5, dlr, dli, dbbr, dbbi, dctr, dcti, dd5, dgw, dgb5), got = _s5_bwd(
        proj, dyd, s['xr'], s['xi'], p['bbr'][l], p['bbi'][l], p['ctr'][l], p['cti'][l], p['lre'][l], p['lim'][l],
        p['s5_d'][l], p['glu_w'][l], p['glu_b'][l], tb, xchg=xchg)
    landed(names, got)
    dare, daim, dls, dbre_bd, dbim_bd = _s5_prep_bwd(p['are_c'][l], p['aim_c'][l], p['ls_c'][l], p['bre_bd'][l], p['bim_bd'][l],
                                                     dlr.reshape(S5_N, 1), dli.reshape(S5_N, 1), dbbr, dbbi)
    dparts = (dv, dgb, dgc, dhh, dz, dxs, dbm, dcm, du5, ddt)
    dh, dproj_b, dsh1, dsc1, dnw_mix = _inproj_bwd(dparts, dh1, s['h'], p['norm_mix_w'][l], p['sc1'][l], p['sh1'][l], w['w_in', l], tb)
    pending['w_in', l] = _wgrad(s['u_b'], dproj_b, 1, "wgrad_win", tm=256).reshape(N_DEV, D_MODEL // N_DEV, P_IN)
    small = {
        'norm_mix_w': dnw_mix.reshape(D_MODEL), 'norm_mlp_w': dnw_mlp.reshape(D_MODEL),
        'ada_b': jnp.concatenate([dsh1, dsc1, dg1, dsh2, dsc2, dg2], axis=1).reshape(6 * D_MODEL),
        'pool_w': _block_diag_extract(dpool_bd, 4), 'pool_scale': dpool_scale.reshape(GROUP_W),
        'sconv_w': dsconv[0:3], 'ssd_conv_w': dconv_w[0:4], 'ssd_conv_b': dconv_b.reshape(768),
        'ssd_dt_bias': ddtb[0, 0:4], 'ssd_a_log': dalog[0, 0:4], 'ssd_d': ddskip[0, 0:4],
        's5_a_re': dare.reshape(16, 64), 's5_a_im': daim.reshape(16, 64), 's5_log_step': dls[0:16, 0],
        's5_b_re': _block_diag_extract(dbre_bd, 16), 's5_b_im': _block_diag_extract(dbim_bd, 16),
        's5_c_re': _block_diag_extract(dctr, 16), 's5_c_im': _block_diag_extract(dcti, 16),
        's5_d': dd5.reshape(GROUP_W), 's5_glu_w': dgw, 's5_glu_b': dgb5.reshape(GROUP_W),
        'branch_norm_w': dbn.reshape(D_MODEL),
    }
    return dh, small


def _prepare_params(a, me):
    pack_shapes = [(1, D_MODEL), (2, 3, 32), (2, 4, 96), (2, 32, GROUP_W)]
    packed = _flat_pack([a['c'], a['sconv_w'], a['ssd_conv_w'], a['s5_glu_w']])
    gathered = _gather_rows(packed, "gather_small")
    pieces = [_flat_unpack(gathered[d], pack_shapes) for d in range(N_DEV)]
    c_all = jnp.concatenate([pc[0] for pc in pieces], axis=0)
    sconv_full = jnp.concatenate([pc[1] for pc in pieces], axis=2)
    ssd_conv_full = jnp.concatenate([pc[2] for pc in pieces], axis=2)
    glu_full = jnp.concatenate([pc[3] for pc in pieces], axis=1)

    ada_b_cols = lax.dynamic_slice_in_dim(a['ada_b'], me * 768, 768, axis=1).reshape(2, 1, 768)
    cond, modrows = _ada_forward(c_all, a['ada_w'], ada_b_cols)
    mod_recv = _all_to_all_rows(modrows.transpose(1, 0, 2), "exchange_mod")
    mod = mod_recv.transpose(1, 0, 2).reshape(2, 6 * D_MODEL)
    p = {'cond': cond}
    for k, name in enumerate(('sh1', 'sc1', 'g1', 'sh2', 'sc2', 'g2')):
        p[name] = mod[:, k * D_MODEL:(k + 1) * D_MODEL].reshape(2, 1, D_MODEL)

    for name in ('norm_mix_w', 'norm_mlp_w', 'branch_norm_w'):
        p[name] = a[name].reshape(2, 1, D_MODEL)
    p['pool_bd'] = jnp.stack([_block_diag(a['pool_w'][l].reshape(GROUP_W, 64), 4) for l in range(2)])
    p['pool_scale'] = a['pool_scale'].reshape(2, 1, GROUP_W)
    p['sconv_w8'] = jnp.pad(sconv_full, ((0, 0), (0, 5), (0, 0)))
    p['ssd_conv_w8'] = jnp.pad(ssd_conv_full, ((0, 0), (0, 4), (0, 0)))
    p['ssd_conv_b'] = a['ssd_conv_b'].reshape(2, 1, 768)
    p['ssd_dt_bias'] = jnp.pad(a['ssd_dt_bias'], ((0, 0), (0, LANES - 4))).reshape(2, 1, LANES)
    p['ssd_a_log'] = jnp.pad(a['ssd_a_log'], ((0, 0), (0, LANES - 4))).reshape(2, 1, LANES)
    p['ssd_d_cols'] = jnp.repeat(a['ssd_d'], SSD_P, axis=1).reshape(2, 1, GROUP_W)
    p['are_c'] = a['s5_a_re'].reshape(2, S5_N, 1)
    p['aim_c'] = a['s5_a_im'].reshape(2, S5_N, 1)
    p['ls_c'] = jnp.repeat(a['s5_log_step'], 64, axis=1).reshape(2, S5_N, 1)
    p['bre_bd'] = jnp.stack([_block_diag(a['s5_b_re'][l].reshape(S5_N, 16), 16) for l in range(2)])
    p['bim_bd'] = jnp.stack([_block_diag(a['s5_b_im'][l].reshape(S5_N, 16), 16) for l in range(2)])
    p['ctr'] = jnp.stack([_block_diag(a['s5_c_re'][l].reshape(GROUP_W, 64), 16) for l in range(2)])
    p['cti'] = jnp.stack([_block_diag(a['s5_c_im'][l].reshape(GROUP_W, 64), 16) for l in range(2)])
    p['s5_d'] = a['s5_d'].reshape(2, 1, GROUP_W)
    p['glu_w'] = glu_full
    p['glu_b'] = a['s5_glu_b'].reshape(2, 1, GROUP_W)
    lre, lim, bbr, bbi = [], [], [], []
    for l in range(2):
        r = _s5_prep(p['are_c'][l], p['aim_c'][l], p['ls_c'][l], p['bre_bd'][l], p['bim_bd'][l])
        lre.append(r[0].reshape(1, S5_N))
        lim.append(r[1].reshape(1, S5_N))
        bbr.append(r[2])
        bbi.append(r[3])
    p['lre'], p['lim'], p['bbr'], p['bbi'] = lre, lim, bbr, bbi
    return p


def kernel(x, c, norm_mix_w, norm_mlp_w, ada_w, ada_b, w_in, pool_w, pool_scale, sconv_w, ssd_conv_w, ssd_conv_b, ssd_dt_bias, ssd_a_log, ssd_d, s5_a_re, s5_a_im, s5_log_step, s5_b_re, s5_b_im, s5_c_re, s5_c_im, s5_d, s5_glu_w, s5_glu_b, branch_norm_w, w_out, mlp_w1, mlp_w2, final_norm_w, loss_target, m_norm_mix_w, m_norm_mlp_w, m_ada_w, m_ada_b, m_w_in, m_pool_w, m_pool_scale, m_sconv_w, m_ssd_conv_w, m_ssd_conv_b, m_ssd_dt_bias, m_ssd_a_log, m_ssd_d, m_s5_a_re, m_s5_a_im, m_s5_log_step, m_s5_b_re, m_s5_b_im, m_s5_c_re, m_s5_c_im, m_s5_d, m_s5_glu_w, m_s5_glu_b, m_branch_norm_w, m_w_out, m_mlp_w1, m_mlp_w2, m_final_norm_w, v_norm_mix_w, v_norm_mlp_w, v_ada_w, v_ada_b, v_w_in, v_pool_w, v_pool_scale, v_sconv_w, v_ssd_conv_w, v_ssd_conv_b, v_ssd_dt_bias, v_ssd_a_log, v_ssd_d, v_s5_a_re, v_s5_a_im, v_s5_log_step, v_s5_b_re, v_s5_b_im, v_s5_c_re, v_s5_c_im, v_s5_d, v_s5_glu_w, v_s5_glu_b, v_branch_norm_w, v_w_out, v_mlp_w1, v_mlp_w2, v_final_norm_w):
    a = dict(locals())
    t = x.shape[1]
    tb = min(512, t)
    me = _my_index()
    p = _prepare_params(a, me)

    sh_b = _cast_shards([_reorder_in(w_in), w_out, mlp_w1, mlp_w2])
    w = {('w_in', 0): _exchange_alone(_gather(sh_b[0][0]), "gather_w_in0")[0].reshape(D_MODEL, P_IN)}

    h = x.reshape(t, D_MODEL)
    saved = []
    for l in range(2):
        h, s = _layer_forward(l, h, p, w, sh_b, tb)
        saved.append(s)
    loss_blk, dh, dfinal = _loss_head(h, loss_target.reshape(t, D_MODEL), final_norm_w.reshape(1, D_MODEL), tb)
    loss = lax.psum(loss_blk[0, 0], ("x", "y", "c"))

    pending, recv, small_parts = {}, {}, [None, None]
    for l in (1, 0):
        dh, small_parts[l] = _layer_backward(l, dh, saved[l], p, w, pending, recv, tb)
    grad_x = dh.reshape(1, t, D_MODEL)
    recv['w_in', 0] = _exchange_alone(_scatter(pending.pop(('w_in', 0))), "exchange_w_in0")[0]

    grads, deltas, new_m, new_v = {}, {}, {}, {}

    wmv_in = [_reorder_in(a[n]) for n in ('w_in', 'm_w_in', 'v_w_in')]
    outs = _sum_adamw_layers(recv['w_in', 0], recv['w_in', 1], *wmv_in, "adamw_w_in", 128)
    grads['w_in'], deltas['w_in'], new_m['w_in'], new_v['w_in'] = [_unreorder_in(o) for o in outs]
    for name, rb in (('w_out', 128), ('mlp_w1', 256), ('mlp_w2', 256)):
        grads[name], deltas[name], new_m[name], new_v[name] = _sum_adamw_layers(
            recv[name, 0], recv[name, 1], a[name], a['m_' + name], a['v_' + name], "adamw_" + name, rb)

    dmod = jnp.stack([small_parts[0]['ada_b'], small_parts[1]['ada_b']])
    dmod_recv = _all_to_all_rows(dmod.reshape(2, N_DEV, 768).transpose(1, 0, 2), "exchange_dmod")
    g_ada = _ada_backward(p['cond'], dmod_recv.transpose(1, 0, 2))
    grads['ada_w'], deltas['ada_w'], new_m['ada_w'], new_v['ada_w'] = _sum_adamw_layers(
        g_ada[0:1], g_ada[1:2], ada_w, m_ada_w, v_ada_w, "adamw_ada_w", 256)

    layered = [n for n in _SMALL if n != 'final_norm_w']
    full = [jnp.stack([small_parts[0][n], small_parts[1][n]]) for n in layered] + [dfinal.reshape(D_MODEL)]
    full_shapes = [f.shape for f in full]
    summed = _flat_unpack(_allreduce_rows(_flat_pack(full, row_multiple=64)), full_shapes)
    local = []
    for n, g in zip(_SMALL, summed):
        if n in _SHARDED_SMALL:
            axis, size = _SHARDED_SMALL[n]
            g = lax.dynamic_slice_in_dim(g, me * size, size, axis=axis)
        local.append(g.reshape(a[n].shape))
    local_shapes = [g.shape for g in local]
    packed = [_flat_pack(xs) for xs in (local, [a[n] for n in _SMALL], [a['m_' + n] for n in _SMALL], [a['v_' + n] for n in _SMALL])]
    outs = _sum_adamw(packed[0][None], packed[1], packed[2], packed[3], "adamw_small", packed[0].shape[0])
    for store, o in zip((grads, deltas, new_m, new_v), outs):
        for n, val in zip(_SMALL, _flat_unpack(o, local_shapes)):
            store[n] = val

    return (loss, grad_x, *[grads[n] for n in _W_NAMES], *[deltas[n] for n in _W_NAMES],
            *[new_m[n] for n in _W_NAMES], *[new_v[n] for n in _W_NAMES])
```

```python
import functools

import jax
import jax.numpy as jnp
from jax import lax
from jax.experimental import pallas as pl
from jax.experimental.pallas import tpu as pltpu

f32 = jnp.float32
bf16 = jnp.bfloat16

N_DEV = 8
D_MODEL = 1024
GROUP_W = 256
P_IN = 2432
DT_COL = 2304
SSD_CHUNK = 128
SSD_HEADS = 4
SSD_P = 64
S5_N = 1024
MLP_HB = 512
MLP_TB = 1024
EPS = 1e-6
LANES = 128
VMEM_LIMIT = 56 * 1024 * 1024
ADAM_LR, ADAM_B1, ADAM_B2, ADAM_EPS, ADAM_WD, ADAM_STEP = 0.001, 0.9, 0.999, 1e-08, 0.01, 10
POOL_WINDOWS = (2, 4, 8, 16)

C_POOL, C_GB, C_GC, C_HH, C_Z, C_XS, C_BM, C_CM, C_S5 = range(9)
C_DT128 = DT_COL // LANES

MESH = pl.DeviceIdType.MESH
ANY = pl.BlockSpec(memory_space=pl.ANY)
VMEM = pl.BlockSpec(memory_space=pltpu.VMEM)


def _dot(a, b):
    return jnp.dot(a, b, preferred_element_type=f32)


def _dot_nt(a, b):
    return lax.dot_general(a, b, (((1,), (1,)), ((), ())), preferred_element_type=f32)


def _dot_tn(a, b):
    return lax.dot_general(a, b, (((0,), (0,)), ((), ())), preferred_element_type=f32)


def _dot_exact(a, b):
    return jnp.dot(a, b, preferred_element_type=f32, precision=lax.Precision.HIGHEST)


def _b(x):
    return x.astype(bf16)


def _silu(x):
    return x * jax.nn.sigmoid(x)


def _dsilu(x):
    s = jax.nn.sigmoid(x)
    return s * (1.0 + x * (1.0 - s))


def _softplus(x):
    return jnp.maximum(x, 0.0) + jnp.log1p(jnp.exp(-jnp.abs(x)))


_GELU_K = 0.7978845608028654
_GELU_C = 0.044715


def _gelu(x):
    return 0.5 * x * (1.0 + jnp.tanh(_GELU_K * (x + _GELU_C * x * x * x)))


def _dgelu(x):
    th = jnp.tanh(_GELU_K * (x + _GELU_C * x * x * x))
    return 0.5 * (1.0 + th) + 0.5 * x * (1.0 - th * th) * _GELU_K * (1.0 + 3.0 * _GELU_C * x * x)


def _rms(h):
    r = lax.rsqrt(jnp.mean(h * h, axis=-1, keepdims=True) + EPS)
    return h * r, r


def _rms_bwd(dn, n, r):
    return r * (dn - n * jnp.mean(dn * n, axis=-1, keepdims=True))


def _colsum(x):
    return jnp.sum(x, axis=0, keepdims=True)


def _params(sem=None):
    return pltpu.CompilerParams(dimension_semantics=sem, vmem_limit_bytes=VMEM_LIMIT)


def _full(shape):
    return pl.BlockSpec(shape, lambda *_: (0,) * len(shape))


def _acc(ref, val):
    @pl.when(pl.program_id(0) == 0)
    def _():
        ref[...] = val

    @pl.when(pl.program_id(0) != 0)
    def _():
        ref[...] += val


def _me():
    return lax.axis_index("x"), lax.axis_index("y"), lax.axis_index("c")


def _my_index():
    x, y, c = _me()
    return 4 * x + 2 * y + c


def _coords(p):
    return (p // 4, (p // 2) % 2, p % 2)


class _Exchange:
    def __init__(self, srcs, gather):
        self.srcs = list(srcs)
        self.gather = gather
        self.n = len(self.srcs)
        self.out_shape = [jax.ShapeDtypeStruct(((N_DEV,) + s.shape) if gather else s.shape, s.dtype) for s in self.srcs]
        self.scratch = [pltpu.SemaphoreType.DMA((self.n, N_DEV)), pltpu.SemaphoreType.DMA((self.n, N_DEV)),
                        pltpu.SemaphoreType.DMA((self.n,))]

    def _src(self, refs, t, dev):
        return refs[t] if self.gather else refs[t].at[dev]

    def _remote(self, xin, xout, sems, t, k, me, to):
        return pltpu.make_async_remote_copy(
            src_ref=self._src(xin, t, to), dst_ref=xout[t].at[me], send_sem=sems[0].at[t, k], recv_sem=sems[1].at[t, k],
            device_id=_coords(to), device_id_type=MESH)

    def start(self, xin, xout, sems):
        me = _my_index()
        for t in range(self.n):
            pltpu.make_async_copy(self._src(xin, t, me), xout[t].at[me], sems[2].at[t]).start()
            for k in range(1, N_DEV):
                self._remote(xin, xout, sems, t, k, me, (me + k) % N_DEV).start()

    def wait(self, xin, xout, sems):
        me = _my_index()
        for t in range(self.n):
            for k in range(1, N_DEV):
                src = (me + N_DEV - k) % N_DEV
                pltpu.make_async_remote_copy(
                    src_ref=self._src(xin, t, src), dst_ref=xout[t].at[src], send_sem=sems[0].at[t, k],
                    recv_sem=sems[1].at[t, k], device_id=_coords(src), device_id_type=MESH).wait_recv()
        for t in range(self.n):
            for k in range(1, N_DEV):
                self._remote(xin, xout, sems, t, k, me, (me + k) % N_DEV).wait_send()
            pltpu.make_async_copy(self._src(xin, t, me), xout[t].at[me], sems[2].at[t]).wait()


def _call(body, *, name, grid, in_specs, out_specs, out_shape, args, semantics, scratch_shapes=(), xchg=None):
    if xchg is None:
        outs = pl.pallas_call(body, name=name, grid=grid, in_specs=in_specs, out_specs=out_specs, out_shape=out_shape,
                              scratch_shapes=list(scratch_shapes), compiler_params=_params(semantics))(*args)
        return outs, ()
    n_in, n_out, n_scr, n = len(in_specs), len(out_specs), len(scratch_shapes), xchg.n

    def carried(*refs):
        ins, xin = refs[:n_in], refs[n_in:n_in + n]
        outs, xout = refs[n_in + n:n_in + n + n_out], refs[n_in + n + n_out:n_in + 2 * n + n_out]
        scr, sems = refs[n_in + 2 * n + n_out:n_in + 2 * n + n_out + n_scr], refs[n_in + 2 * n + n_out + n_scr:]
        first = functools.reduce(jnp.logical_and, [pl.program_id(d) == 0 for d in range(len(grid))])
        last = functools.reduce(jnp.logical_and, [pl.program_id(d) == g - 1 for d, g in enumerate(grid)])

        @pl.when(first)
        def _():
            xchg.start(xin, xout, sems)

        body(*ins, *outs, *scr)

        @pl.when(last)
        def _():
            xchg.wait(xin, xout, sems)

    res = pl.pallas_call(
        carried, name=name, grid=grid, in_specs=list(in_specs) + [ANY] * n, out_specs=list(out_specs) + [ANY] * n,
        out_shape=list(out_shape) + xchg.out_shape, scratch_shapes=list(scratch_shapes) + xchg.scratch,
        compiler_params=_params(("arbitrary",) * len(grid)))(*args, *xchg.srcs)
    return res[:n_out], tuple(res[n_out:])


def _exchange_alone(xchg, name):
    def body(*refs):
        xin, xout, sems = refs[:xchg.n], refs[xchg.n:2 * xchg.n], refs[2 * xchg.n:]
        xchg.start(xin, xout, sems)
        xchg.wait(xin, xout, sems)

    return pl.pallas_call(body, name=name, out_shape=xchg.out_shape, in_specs=[ANY] * xchg.n, out_specs=[ANY] * xchg.n,
                          scratch_shapes=xchg.scratch)(*xchg.srcs)


def _cast_shards(shards):
    n = len(shards)

    def body(*refs):
        for i, o in zip(refs[:n], refs[n:]):
            o[...] = i[...].astype(bf16)

    return pl.pallas_call(body, name="cast_shards", out_shape=[jax.ShapeDtypeStruct(s.shape, bf16) for s in shards],
                          in_specs=[VMEM] * n, out_specs=[VMEM] * n, compiler_params=_params())(*shards)


def _allreduce_rows(v):
    r = v.shape[0]
    rp = r // N_DEV

    def body(v_ref, o_ref, parts, sums, send1, recv1, send2, recv2):
        me = _my_index()

        def piece(ref, d):
            return ref.at[pl.ds(pl.multiple_of(d * rp, 8), rp), :]

        def copy1(k, src_dev, to):
            return pltpu.make_async_remote_copy(src_ref=piece(v_ref, to), dst_ref=parts.at[src_dev], send_sem=send1.at[k],
                                                recv_sem=recv1.at[k], device_id=_coords(to), device_id_type=MESH)

        def copy2(k, owner, to):
            return pltpu.make_async_remote_copy(src_ref=sums, dst_ref=piece(o_ref, owner), send_sem=send2.at[k],
                                                recv_sem=recv2.at[k], device_id=_coords(to), device_id_type=MESH)

        for k in range(1, N_DEV):
            copy1(k, me, (me + k) % N_DEV).start()
        parts[me] = v_ref[pl.ds(pl.multiple_of(me * rp, 8), rp), :]
        for k in range(1, N_DEV):
            copy1(k, (me + N_DEV - k) % N_DEV, me).wait_recv()
        total = parts[0]
        for s in range(1, N_DEV):
            total = total + parts[s]
        sums[...] = total
        o_ref[pl.ds(pl.multiple_of(me * rp, 8), rp), :] = total
        for k in range(1, N_DEV):
            copy2(k, me, (me + k) % N_DEV).start()
        for k in range(1, N_DEV):
            copy2(k, (me + N_DEV - k) % N_DEV, me).wait_recv()
        for k in range(1, N_DEV):
            copy1(k, me, (me + k) % N_DEV).wait_send()
            copy2(k, me, (me + k) % N_DEV).wait_send()

    return pl.pallas_call(
        body, name="allreduce_small_grads", out_shape=jax.ShapeDtypeStruct(v.shape, v.dtype),
        in_specs=[VMEM], out_specs=VMEM,
        scratch_shapes=[pltpu.VMEM((N_DEV, rp, LANES), f32), pltpu.VMEM((rp, LANES), f32)]
        + [pltpu.SemaphoreType.DMA((N_DEV,))] * 4,
        compiler_params=_params(),
    )(v)


def _gather_rows(v, name):
    def body(v_ref, o_ref, send_sems, recv_sems):
        me = _my_index()
        o_ref[me] = v_ref[...]
        sends = []
        for k in range(1, N_DEV):
            peer = (me + k) % N_DEV
            rc = pltpu.make_async_remote_copy(src_ref=v_ref, dst_ref=o_ref.at[me], send_sem=send_sems.at[k],
                                              recv_sem=recv_sems.at[k], device_id=_coords(peer), device_id_type=MESH)
            rc.start()
            sends.append(rc)
        for k in range(1, N_DEV):
            src = (me + N_DEV - k) % N_DEV
            pltpu.make_async_remote_copy(src_ref=v_ref, dst_ref=o_ref.at[src], send_sem=send_sems.at[k],
                                         recv_sem=recv_sems.at[k], device_id=_coords(src), device_id_type=MESH).wait_recv()
        for rc in sends:
            rc.wait_send()

    return pl.pallas_call(
        body, name=name, out_shape=jax.ShapeDtypeStruct((N_DEV,) + v.shape, v.dtype),
        in_specs=[VMEM], out_specs=VMEM,
        scratch_shapes=[pltpu.SemaphoreType.DMA((N_DEV,)), pltpu.SemaphoreType.DMA((N_DEV,))],
        compiler_params=pltpu.CompilerParams(vmem_limit_bytes=VMEM_LIMIT),
    )(v)


def _all_to_all_rows(v, name):
    def body(v_ref, o_ref, send_sems, recv_sems):
        me = _my_index()
        o_ref[me] = v_ref[me]
        sends = []
        for k in range(1, N_DEV):
            peer = (me + k) % N_DEV
            rc = pltpu.make_async_remote_copy(src_ref=v_ref.at[peer], dst_ref=o_ref.at[me], send_sem=send_sems.at[k],
                                              recv_sem=recv_sems.at[k], device_id=_coords(peer), device_id_type=MESH)
            rc.start()
            sends.append(rc)
        for k in range(1, N_DEV):
            src = (me + N_DEV - k) % N_DEV
            pltpu.make_async_remote_copy(src_ref=v_ref.at[src], dst_ref=o_ref.at[src], send_sem=send_sems.at[k],
                                         recv_sem=recv_sems.at[k], device_id=_coords(src), device_id_type=MESH).wait_recv()
        for rc in sends:
            rc.wait_send()

    return pl.pallas_call(
        body, name=name, out_shape=jax.ShapeDtypeStruct(v.shape, v.dtype),
        in_specs=[VMEM], out_specs=VMEM,
        scratch_shapes=[pltpu.SemaphoreType.DMA((N_DEV,)), pltpu.SemaphoreType.DMA((N_DEV,))],
    )(v)


def _ada_forward(c_all, ada_w, ada_b_cols):
    def body(c_ref, w_ref, b_ref, cond_ref, o_ref):
        cond = _silu(c_ref[...])
        cond_ref[...] = cond
        for l in range(2):
            o_ref[l] = _dot(_b(cond), _b(w_ref[l])) + b_ref[l]

    return pl.pallas_call(
        body, name="ada_forward",
        out_shape=[jax.ShapeDtypeStruct((N_DEV, D_MODEL), f32), jax.ShapeDtypeStruct((2, N_DEV, 768), f32)],
        in_specs=[VMEM] * 3, out_specs=[VMEM] * 2, compiler_params=_params(),
    )(c_all, ada_w, ada_b_cols)


def _ada_backward(cond, dmod_rows):
    def body(c_ref, d_ref, o_ref):
        cb = _b(c_ref[...])
        for l in range(2):
            o_ref[l] = _dot_tn(cb, _b(d_ref[l]))

    return pl.pallas_call(
        body, name="ada_backward", out_shape=jax.ShapeDtypeStruct((2, D_MODEL, 768), f32),
        in_specs=[VMEM] * 2, out_specs=VMEM, compiler_params=_params(),
    )(cond, dmod_rows)


def _inproj_fwd(h, norm_w, sc, sh, w_in, tb, xchg=None):
    t = h.shape[0]

    def body(h_ref, nw_ref, sc_ref, sh_ref, w_ref, proj_ref, u_ref):
        n, _ = _rms(h_ref[...])
        u = _b(n * nw_ref[...] * (1.0 + sc_ref[...]) + sh_ref[...])
        u_ref[...] = u
        proj_ref[...] = _dot(u, w_ref[...])

    row = pl.BlockSpec((tb, D_MODEL), lambda i: (i, 0))
    vec = _full((1, D_MODEL))
    return _call(
        body, name="inproj_fwd", grid=(t // tb,),
        out_shape=[jax.ShapeDtypeStruct((t, P_IN), f32), jax.ShapeDtypeStruct((t, D_MODEL), bf16)],
        in_specs=[row, vec, vec, vec, _full((D_MODEL, P_IN))],
        out_specs=[pl.BlockSpec((tb, P_IN), lambda i: (i, 0)), row],
        semantics=("parallel",), args=(h, norm_w, sc, sh, w_in), xchg=xchg)


def _inproj_bwd(dparts, dh_res, h, norm_w, sc, sh, w_in, tb):
    t = h.shape[0]

    def body(*refs):
        parts = refs[:10]
        dres_ref, h_ref, nw_ref, sc_ref, sh_ref, w_ref = refs[10:16]
        dh_ref, dproj_ref, dsh_ref, dsc_ref, dnw_ref = refs[16:]
        dproj = _b(jnp.concatenate([p[...] for p in parts], axis=1))
        dproj_ref[...] = dproj
        du = _dot_nt(dproj, w_ref[...])
        n, r = _rms(h_ref[...])
        nw = nw_ref[...]
        gain = 1.0 + sc_ref[...]
        _acc(dsh_ref, _colsum(du))
        _acc(dsc_ref, _colsum(du * n * nw))
        _acc(dnw_ref, _colsum(du * gain * n))
        dh_ref[...] = dres_ref[...] + _rms_bwd(du * nw * gain, n, r)

    row = pl.BlockSpec((tb, D_MODEL), lambda i: (i, 0))
    vec = _full((1, D_MODEL))
    part_specs = [pl.BlockSpec((tb, GROUP_W), lambda i: (i, 0))] * 9 + [pl.BlockSpec((tb, LANES), lambda i: (i, 0))]
    return pl.pallas_call(
        body, name="inproj_bwd", grid=(t // tb,),
        out_shape=[jax.ShapeDtypeStruct((t, D_MODEL), f32), jax.ShapeDtypeStruct((t, P_IN), bf16)]
        + [jax.ShapeDtypeStruct((1, D_MODEL), f32)] * 3,
        in_specs=part_specs + [row, row, vec, vec, vec, _full((D_MODEL, P_IN))],
        out_specs=[row, pl.BlockSpec((tb, P_IN), lambda i: (i, 0)), vec, vec, vec],
        compiler_params=_params(("arbitrary",)),
    )(*dparts, dh_res, h, norm_w, sc, sh, w_in)


def _wgrad(a, b, n_blocks, name, tm, tk=512):
    t, m = a.shape
    nb = b.shape[1] // n_blocks
    tk = min(tk, t)
    nk = t // tk

    def body(a_ref, b_ref, o_ref, acc_ref):
        k = pl.program_id(2)
        p = _dot_tn(a_ref[...], b_ref[...])

        @pl.when(k == 0)
        def _():
            acc_ref[...] = p

        @pl.when(k != 0)
        def _():
            acc_ref[...] += p

        @pl.when(k == nk - 1)
        def _():
            o_ref[0] = acc_ref[...].astype(o_ref.dtype)

    return pl.pallas_call(
        body, name=name, grid=(m // tm, n_blocks, nk),
        out_shape=jax.ShapeDtypeStruct((n_blocks, m, nb), bf16),
        in_specs=[pl.BlockSpec((tk, tm), lambda i, j, k: (k, i)), pl.BlockSpec((tk, nb), lambda i, j, k: (k, j))],
        out_specs=pl.BlockSpec((1, tm, nb), lambda i, j, k: (j, i, 0)),
        scratch_shapes=[pltpu.VMEM((tm, nb), f32)],
        compiler_params=_params(("parallel", "parallel", "arbitrary")),
    )(a, b)


def _pool_counts(rows, t0):
    tpos = (lax.broadcasted_iota(jnp.int32, (rows, GROUP_W), 0) + t0 + 1).astype(f32)
    grp = lax.broadcasted_iota(jnp.int32, (rows, GROUP_W), 1) // 64
    win = jnp.where(grp == 0, 2.0, jnp.where(grp == 1, 4.0, jnp.where(grp == 2, 8.0, 16.0)))
    return jnp.minimum(tpos, win), grp


def _pool_select(grp, l1, l2, l3, l4):
    return jnp.where(grp == 0, l1, jnp.where(grp == 1, l2, jnp.where(grp == 2, l3, l4)))


def _pool_means(v, halo, t0):
    tb = v.shape[0]
    ext = jnp.concatenate([halo, v], axis=0)
    n = tb + 16
    s1 = ext[1:n] + ext[0:n - 1]
    s2 = s1[2:n - 1] + s1[0:n - 3]
    s3 = s2[4:n - 3] + s2[0:n - 7]
    s4 = s3[8:n - 7] + s3[0:n - 15]
    cnt, grp = _pool_counts(tb, t0)
    wsum = _pool_select(grp, s1[15:15 + tb], s2[13:13 + tb], s3[9:9 + tb], s4[1:1 + tb])
    return wsum / cnt - v


def _pool_fwd(proj, pw_bd, scale, tb):
    t = proj.shape[0]

    def body(v_ref, vh_ref, pw_ref, sc_ref, o_ref):
        i = pl.program_id(0)
        halo = jnp.where(i > 0, vh_ref[...], 0.0)
        p = _pool_means(v_ref[...], halo, i * tb)
        o_ref[...] = _dot(_b(p), _b(pw_ref[...])) * sc_ref[...]

    return pl.pallas_call(
        body, name="pool_fwd", grid=(t // tb,),
        out_shape=jax.ShapeDtypeStruct((t, GROUP_W), f32),
        in_specs=[pl.BlockSpec((tb, GROUP_W), lambda i: (i, C_POOL)),
                  pl.BlockSpec((16, GROUP_W), lambda i: (jnp.maximum(i * (tb // 16) - 1, 0), C_POOL)),
                  _full((GROUP_W, GROUP_W)), _full((1, GROUP_W))],
        out_specs=pl.BlockSpec((tb, GROUP_W), lambda i: (i, 0)),
        compiler_params=_params(("parallel",)),
    )(proj, proj, pw_bd, scale)


def _pool_bwd(proj, dy, pw_bd, scale, tb):
    t = proj.shape[0]
    nt = t // tb
    last16 = t // 16 - 1

    def body(v_ref, vh_ref, dy_ref, dyh_ref, pw_ref, sc_ref, dv_ref, dpw_ref, dsc_ref):
        i = pl.program_id(0)
        halo = jnp.where(i > 0, vh_ref[...], 0.0)
        p = _pool_means(v_ref[...], halo, i * tb)
        pw = _b(pw_ref[...])
        sc = sc_ref[...]
        dy = dy_ref[...]
        ypre = _dot(_b(p), pw)
        _acc(dsc_ref, _colsum(dy * ypre))
        dys = _b(dy * sc)
        _acc(dpw_ref, _dot_tn(_b(p), dys))
        dp = _dot_nt(dys, pw)
        dph = _dot_nt(_b(jnp.where(i < nt - 1, dyh_ref[...], 0.0) * sc), pw)
        cnt, grp = _pool_counts(tb, i * tb)
        cnth, _ = _pool_counts(16, (i + 1) * tb)
        ext = jnp.concatenate([dp / cnt, dph / cnth], axis=0)
        n = tb + 16
        f1 = ext[0:n - 1] + ext[1:n]
        f2 = f1[0:n - 3] + f1[2:n - 1]
        f3 = f2[0:n - 7] + f2[4:n - 3]
        f4 = f3[0:n - 15] + f3[8:n - 7]
        dv_ref[...] = _pool_select(grp, f1[0:tb], f2[0:tb], f3[0:tb], f4[0:tb]) - dp

    return pl.pallas_call(
        body, name="pool_bwd", grid=(nt,),
        out_shape=[jax.ShapeDtypeStruct((t, GROUP_W), f32), jax.ShapeDtypeStruct((GROUP_W, GROUP_W), f32),
                   jax.ShapeDtypeStruct((1, GROUP_W), f32)],
        in_specs=[pl.BlockSpec((tb, GROUP_W), lambda i: (i, C_POOL)),
                  pl.BlockSpec((16, GROUP_W), lambda i: (jnp.maximum(i * (tb // 16) - 1, 0), C_POOL)),
                  pl.BlockSpec((tb, GROUP_W), lambda i: (i, 0)),
                  pl.BlockSpec((16, GROUP_W), lambda i: (jnp.minimum((i + 1) * (tb // 16), last16), 0)),
                  _full((GROUP_W, GROUP_W)), _full((1, GROUP_W))],
        out_specs=[pl.BlockSpec((tb, GROUP_W), lambda i: (i, 0)), _full((GROUP_W, GROUP_W)), _full((1, GROUP_W))],
        compiler_params=_params(("arbitrary",)),
    )(proj, proj, dy, dy, pw_bd, scale)


def _sconv_fwd(proj, w, tb):
    t = proj.shape[0]

    def body(gb_ref, gc_ref, hh_ref, gch_ref, hhh_ref, w_ref, o_ref):
        i = pl.program_id(0)
        q = gc_ref[...] * hh_ref[...]
        qh = jnp.where(i > 0, gch_ref[...] * hhh_ref[...], 0.0)
        ext = jnp.concatenate([qh, q], axis=0)
        w = w_ref[...]
        conv = w[0:1] * ext[6:6 + tb] + w[1:2] * ext[7:7 + tb] + w[2:3] * ext[8:8 + tb]
        o_ref[...] = gb_ref[...] * conv

    def col(c):
        return pl.BlockSpec((tb, GROUP_W), lambda i: (i, c))

    def prev(c):
        return pl.BlockSpec((8, GROUP_W), lambda i: (jnp.maximum(i * (tb // 8) - 1, 0), c))

    return pl.pallas_call(
        body, name="sconv_fwd", grid=(t // tb,),
        out_shape=jax.ShapeDtypeStruct((t, GROUP_W), f32),
        in_specs=[col(C_GB), col(C_GC), col(C_HH), prev(C_GC), prev(C_HH), _full((8, GROUP_W))],
        out_specs=pl.BlockSpec((tb, GROUP_W), lambda i: (i, 0)),
        compiler_params=_params(("parallel",)),
    )(proj, proj, proj, proj, proj, w)


def _sconv_bwd(proj, dy, w, tb):
    t = proj.shape[0]
    nt = t // tb
    last8 = t // 8 - 1

    def body(gb_ref, gc_ref, hh_ref, gch_ref, hhh_ref, gbn_ref, dy_ref, dyn_ref, w_ref, dgb_ref, dgc_ref, dhh_ref, dw_ref):
        i = pl.program_id(0)
        gc, hh, gb, dy = gc_ref[...], hh_ref[...], gb_ref[...], dy_ref[...]
        q = gc * hh
        qh = jnp.where(i > 0, gch_ref[...] * hhh_ref[...], 0.0)
        ext = jnp.concatenate([qh, q], axis=0)
        w = w_ref[...]
        conv = w[0:1] * ext[6:6 + tb] + w[1:2] * ext[7:7 + tb] + w[2:3] * ext[8:8 + tb]
        dgb_ref[...] = dy * conv
        e = dy * gb
        en = jnp.where(i < nt - 1, dyn_ref[...] * gbn_ref[...], 0.0)
        exte = jnp.concatenate([e, en], axis=0)
        dq = w[2:3] * exte[0:tb] + w[1:2] * exte[1:1 + tb] + w[0:1] * exte[2:2 + tb]
        dgc_ref[...] = dq * hh
        dhh_ref[...] = dq * gc
        dw = jnp.concatenate([_colsum(e * ext[6:6 + tb]), _colsum(e * ext[7:7 + tb]), _colsum(e * ext[8:8 + tb]),
                              jnp.zeros((5, GROUP_W), f32)], axis=0)
        _acc(dw_ref, dw)

    def col(c):
        return pl.BlockSpec((tb, GROUP_W), lambda i: (i, c))

    def prev(c):
        return pl.BlockSpec((8, GROUP_W), lambda i: (jnp.maximum(i * (tb // 8) - 1, 0), c))

    def nxt(c):
        return pl.BlockSpec((8, GROUP_W), lambda i: (jnp.minimum((i + 1) * (tb // 8), last8), c))

    out = pl.BlockSpec((tb, GROUP_W), lambda i: (i, 0))
    return pl.pallas_call(
        body, name="sconv_bwd", grid=(nt,),
        out_shape=[jax.ShapeDtypeStruct((t, GROUP_W), f32)] * 3 + [jax.ShapeDtypeStruct((8, GROUP_W), f32)],
        in_specs=[col(C_GB), col(C_GC), col(C_HH), prev(C_GC), prev(C_HH), nxt(C_GB), col(0), nxt(0), _full((8, GROUP_W))],
        out_specs=[out, out, out, _full((8, GROUP_W))],
        compiler_params=_params(("arbitrary",)),
    )(proj, proj, proj, proj, proj, proj, dy, dy, w)


def _conv4(xr, halo, w, bias):
    tb = xr.shape[0]
    ext = jnp.concatenate([halo, xr], axis=0)
    pre = w[0:1] * ext[5:5 + tb] + w[1:2] * ext[6:6 + tb] + w[2:3] * ext[7:7 + tb] + w[3:4] * ext[8:8 + tb] + bias
    return pre, ext


def _tri():
    r = lax.broadcasted_iota(jnp.int32, (SSD_CHUNK, SSD_CHUNK), 0)
    c = lax.broadcasted_iota(jnp.int32, (SSD_CHUNK, SSD_CHUNK), 1)
    return r >= c


def _lane_pick(vals):
    rows = vals[0].shape[0]
    lane = lax.broadcasted_iota(jnp.int32, (rows, LANES), 1)
    out = jnp.zeros((rows, LANES), f32)
    for h, v in enumerate(vals):
        out = jnp.where(lane == h, v, out)
    return out


def _ssd_fwd(proj, conv_w, conv_b, dt_bias, a_log, d_cols, tb, xchg=None):
    t = proj.shape[0]
    cpt = tb // SSD_CHUNK

    def body(z_ref, xs_ref, bm_ref, cm_ref, xsh_ref, bmh_ref, cmh_ref, dt_ref, cw_ref, cb_ref, dtb_ref, al_ref, dk_ref,
             o_ref, y_ref, st_ref, state):
        i = pl.program_id(0)

        @pl.when(i == 0)
        def _():
            state[...] = jnp.zeros_like(state)

        cw, cb = cw_ref[...], cb_ref[...]
        acts = []
        for j, (r, hr) in enumerate(((xs_ref, xsh_ref), (bm_ref, bmh_ref), (cm_ref, cmh_ref))):
            halo = jnp.where(i > 0, hr[...], 0.0)
            pre, _ = _conv4(r[...], halo, cw[:, j * 256:(j + 1) * 256], cb[:, j * 256:(j + 1) * 256])
            acts.append(_silu(pre))
        xs, bm, cm = acts
        dt = _softplus(dt_ref[...] + dtb_ref[...])
        a = -jnp.exp(al_ref[...])
        adt = dt * a
        tri = _tri()
        trif = tri.astype(f32)
        dk = dk_ref[...]
        for c in range(cpt):
            rows = slice(c * SSD_CHUNK, (c + 1) * SSD_CHUNK)
            acol = _dot_exact(trif, adt[rows])
            arow = acol.T
            dt_c = dt[rows]
            ys = []
            for h in range(SSD_HEADS):
                g = h // 2
                ac = acol[:, h:h + 1]
                lm = jnp.exp(jnp.where(tri, ac - arow[h:h + 1, :], -jnp.inf))
                cg = _b(cm[rows, g * 128:(g + 1) * 128])
                bg = _b(bm[rows, g * 128:(g + 1) * 128])
                xh = xs[rows, h * SSD_P:(h + 1) * SSD_P]
                xdt = xh * dt_c[:, h:h + 1]
                m = _dot_nt(cg, bg) * lm
                s_in = state[h]
                st_ref[c, h] = s_in
                y = _dot(_b(m), _b(xdt)) + jnp.exp(ac) * _dot_nt(cg, _b(s_in)) + xh * dk[:, h * SSD_P:(h + 1) * SSD_P]
                ys.append(y)
                alast = ac[SSD_CHUNK - 1:SSD_CHUNK]
                wdec = jnp.exp(alast - ac)
                state[h] = jnp.exp(alast) * s_in + _dot_tn(_b(xdt * wdec), bg)
            yc = jnp.concatenate(ys, axis=1)
            y_ref[rows, :] = yc
            o_ref[rows, :] = yc * _silu(z_ref[rows, :])

    def col(c):
        return pl.BlockSpec((tb, GROUP_W), lambda i: (i, c))

    def prev(c):
        return pl.BlockSpec((8, GROUP_W), lambda i: (jnp.maximum(i * (tb // 8) - 1, 0), c))

    out = pl.BlockSpec((tb, GROUP_W), lambda i: (i, 0))
    return _call(
        body, name="ssd_fwd", grid=(t // tb,),
        out_shape=[jax.ShapeDtypeStruct((t, GROUP_W), f32), jax.ShapeDtypeStruct((t, GROUP_W), f32),
                   jax.ShapeDtypeStruct((t // SSD_CHUNK, SSD_HEADS, SSD_P, 128), f32)],
        in_specs=[col(C_Z), col(C_XS), col(C_BM), col(C_CM), prev(C_XS), prev(C_BM), prev(C_CM),
                  pl.BlockSpec((tb, LANES), lambda i: (i, C_DT128)),
                  _full((8, 768)), _full((1, 768)), _full((1, LANES)), _full((1, LANES)), _full((1, GROUP_W))],
        out_specs=[out, out, pl.BlockSpec((cpt, SSD_HEADS, SSD_P, 128), lambda i: (i, 0, 0, 0))],
        scratch_shapes=[pltpu.VMEM((SSD_HEADS, SSD_P, 128), f32)],
        semantics=("arbitrary",), xchg=xchg,
        args=(proj, proj, proj, proj, proj, proj, proj, proj, conv_w, conv_b, dt_bias, a_log, d_cols))


def _ssd_bwd(proj, dyc, y_pre, states, conv_w, conv_b, dt_bias, a_log, d_cols, tb, xchg=None):
    t = proj.shape[0]
    nt = t // tb
    cpt = tb // SSD_CHUNK

    def body(z_ref, xs_ref, bm_ref, cm_ref, xsh_ref, bmh_ref, cmh_ref, dt_ref, dy_ref, yp_ref, st_ref,
             cw_ref, cb_ref, dtb_ref, al_ref, dk_ref,
             dz_ref, dxs_ref, dbm_ref, dcm_ref, ddt_ref, dcw_ref, dcb_ref, ddtb_ref, dal_ref, ddk_ref,
             dstate, carry):
        i = pl.program_id(0)
        ti = nt - 1 - i

        @pl.when(i == 0)
        def _():
            dstate[...] = jnp.zeros_like(dstate)
            carry[...] = jnp.zeros_like(carry)

        cw, cb = cw_ref[...], cb_ref[...]
        pres, exts, acts = [], [], []
        for j, (r, hr) in enumerate(((xs_ref, xsh_ref), (bm_ref, bmh_ref), (cm_ref, cmh_ref))):
            halo = jnp.where(ti > 0, hr[...], 0.0)
            pre, ext = _conv4(r[...], halo, cw[:, j * 256:(j + 1) * 256], cb[:, j * 256:(j + 1) * 256])
            pres.append(pre)
            exts.append(ext)
            acts.append(_silu(pre))
        xs, bm, cm = acts
        raw = dt_ref[...] + dtb_ref[...]
        dt = _softplus(raw)
        a = -jnp.exp(al_ref[...])
        adt = dt * a
        tri = _tri()
        trif = tri.astype(f32)
        dk = dk_ref[...]
        z = z_ref[...]
        dyc = dy_ref[...]
        dz_ref[...] = dyc * yp_ref[...] * _dsilu(z)
        dy_all = dyc * _silu(z)
        lane = lax.broadcasted_iota(jnp.int32, (1, LANES), 1)
        ddk_acc = jnp.zeros((1, LANES), f32)
        dal_acc = jnp.zeros((1, LANES), f32)
        dxs_c, dbm_c, dcm_c, ddt_c = [None] * cpt, [None] * cpt, [None] * cpt, [None] * cpt
        for c in reversed(range(cpt)):
            rows = slice(c * SSD_CHUNK, (c + 1) * SSD_CHUNK)
            acol = _dot_exact(trif, adt[rows])
            arow = acol.T
            dt_c = dt[rows]
            da_cols, da_rows, ddt_heads, dxs_heads = [], [], [], []
            dbg = [jnp.zeros((SSD_CHUNK, 128), f32), jnp.zeros((SSD_CHUNK, 128), f32)]
            dcg = [jnp.zeros((SSD_CHUNK, 128), f32), jnp.zeros((SSD_CHUNK, 128), f32)]
            for h in range(SSD_HEADS):
                g = h // 2
                ac = acol[:, h:h + 1]
                lm = jnp.exp(jnp.where(tri, ac - arow[h:h + 1, :], -jnp.inf))
                cgf = cm[rows, g * 128:(g + 1) * 128]
                bgf = bm[rows, g * 128:(g + 1) * 128]
                cg, bg = _b(cgf), _b(bgf)
                xh = xs[rows, h * SSD_P:(h + 1) * SSD_P]
                dth = dt_c[:, h:h + 1]
                xdt = xh * dth
                xb = _b(xdt)
                dy = dy_all[rows, h * SSD_P:(h + 1) * SSD_P]
                dyb = _b(dy)
                s_in = st_ref[c, h]
                sb = _b(s_in)
                dsn = dstate[h]
                dsnb = _b(dsn)
                ea = jnp.exp(ac)
                alast = ac[SSD_CHUNK - 1:SSD_CHUNK]
                wdec = jnp.exp(alast - ac)
                el = jnp.exp(alast)
                m = _dot_nt(cg, bg) * lm
                dm = _dot_nt(dyb, xb)
                dx = _dot_tn(_b(m), dyb)
                dg = _b(dm * lm)
                dcg[g] = dcg[g] + _dot(dg, bg)
                dbg[g] = dbg[g] + _dot_tn(dg, cg)
                wm = dm * m
                da = jnp.sum(wm, axis=1, keepdims=True)
                da_rows.append(jnp.sum(wm, axis=0, keepdims=True))
                yoff = ea * _dot_nt(cg, sb)
                da = da + jnp.sum(dy * yoff, axis=1, keepdims=True)
                dye = _b(dy * ea)
                dcg[g] = dcg[g] + _dot(dye, sb)
                ds_y = _dot_tn(dye, cg)
                t1 = _dot(xb, dsnb)
                dbg[g] = dbg[g] + wdec * t1
                dwv = jnp.sum(t1 * bgf, axis=1, keepdims=True) * wdec
                dx = dx + _dot_nt(_b(bgf * wdec), dsnb)
                dalast = jnp.sum(dwv, axis=0, keepdims=True) + el * jnp.sum(jnp.sum(dsn * s_in, axis=1, keepdims=True), axis=0, keepdims=True)
                da = da - dwv
                rowi = lax.broadcasted_iota(jnp.int32, (SSD_CHUNK, 1), 0)
                da = da + jnp.where(rowi == SSD_CHUNK - 1, dalast, 0.0)
                dstate[h] = el * dsn + ds_y
                da_cols.append(da)
                ddt_heads.append(jnp.sum(dx * xh, axis=1, keepdims=True))
                dkh = dk[:, h * SSD_P:(h + 1) * SSD_P]
                dxs_heads.append(dx * dth + dy * dkh)
                ddk_acc = ddk_acc + jnp.where(lane == h, jnp.sum(_colsum(dy * xh), axis=1, keepdims=True), 0.0)
            da_blk = _lane_pick(da_cols)
            rowsel = lax.broadcasted_iota(jnp.int32, (SSD_CHUNK, SSD_CHUNK), 0)
            da_rows_blk = jnp.zeros((SSD_CHUNK, SSD_CHUNK), f32)
            for h in range(SSD_HEADS):
                da_rows_blk = jnp.where(rowsel == h, da_rows[h], da_rows_blk)
            da_blk = da_blk - da_rows_blk.T
            dadt = lax.dot_general(trif, da_blk, (((0,), (0,)), ((), ())), preferred_element_type=f32,
                                   precision=lax.Precision.HIGHEST)
            dal_acc = dal_acc + _colsum(dadt * dt_c)
            ddt_c[c] = dadt * a + _lane_pick(ddt_heads)
            dxs_c[c] = jnp.concatenate(dxs_heads, axis=1)
            dbm_c[c] = jnp.concatenate(dbg, axis=1)
            dcm_c[c] = jnp.concatenate(dcg, axis=1)
        ddt = jnp.concatenate(ddt_c, axis=0) if cpt > 1 else ddt_c[0]
        ddraw = jnp.where(lane < SSD_HEADS, ddt * jax.nn.sigmoid(raw), 0.0)
        ddt_ref[...] = ddraw
        _acc(ddtb_ref, _colsum(ddraw))
        _acc(dal_ref, jnp.where(lane < SSD_HEADS, dal_acc * a, 0.0))
        _acc(ddk_ref, ddk_acc)
        dcw_parts, dcb_parts = [], []
        for j, (dparts, out_ref) in enumerate(((dxs_c, dxs_ref), (dbm_c, dbm_ref), (dcm_c, dcm_ref))):
            dact = jnp.concatenate(dparts, axis=0) if cpt > 1 else dparts[0]
            dpre = dact * _dsilu(pres[j])
            w = cw[:, j * 256:(j + 1) * 256]
            ext = jnp.concatenate([dpre, carry[:, j * 256:(j + 1) * 256]], axis=0)
            out_ref[...] = w[3:4] * ext[0:tb] + w[2:3] * ext[1:1 + tb] + w[1:2] * ext[2:2 + tb] + w[0:1] * ext[3:3 + tb]
            carry[:, j * 256:(j + 1) * 256] = dpre[0:8]
            xe = exts[j]
            dcw_parts.append(jnp.concatenate([_colsum(dpre * xe[5 + k:5 + k + tb]) for k in range(4)]
                                             + [jnp.zeros((4, GROUP_W), f32)], axis=0))
            dcb_parts.append(_colsum(dpre))
        _acc(dcw_ref, jnp.concatenate(dcw_parts, axis=1))
        _acc(dcb_ref, jnp.concatenate(dcb_parts, axis=1))

    def col(c):
        return pl.BlockSpec((tb, GROUP_W), lambda i: (nt - 1 - i, c))

    def prev(c):
        return pl.BlockSpec((8, GROUP_W), lambda i: (jnp.maximum((nt - 1 - i) * (tb // 8) - 1, 0), c))

    out = pl.BlockSpec((tb, GROUP_W), lambda i: (nt - 1 - i, 0))
    vec = _full((1, LANES))
    return _call(
        body, name="ssd_bwd", grid=(nt,),
        out_shape=[jax.ShapeDtypeStruct((t, GROUP_W), f32)] * 4 + [jax.ShapeDtypeStruct((t, LANES), f32),
                   jax.ShapeDtypeStruct((8, 768), f32), jax.ShapeDtypeStruct((1, 768), f32)]
        + [jax.ShapeDtypeStruct((1, LANES), f32)] * 3,
        in_specs=[col(C_Z), col(C_XS), col(C_BM), col(C_CM), prev(C_XS), prev(C_BM), prev(C_CM),
                  pl.BlockSpec((tb, LANES), lambda i: (nt - 1 - i, C_DT128)), out, out,
                  pl.BlockSpec((cpt, SSD_HEADS, SSD_P, 128), lambda i: (nt - 1 - i, 0, 0, 0)),
                  _full((8, 768)), _full((1, 768)), vec, vec, _full((1, GROUP_W))],
        out_specs=[out, out, out, out, pl.BlockSpec((tb, LANES), lambda i: (nt - 1 - i, 0)),
                   _full((8, 768)), _full((1, 768)), vec, vec, vec],
        scratch_shapes=[pltpu.VMEM((SSD_HEADS, SSD_P, 128), f32), pltpu.VMEM((8, 768), f32)],
        semantics=("arbitrary",), xchg=xchg,
        args=(proj, proj, proj, proj, proj, proj, proj, proj, dyc, y_pre, states, conv_w, conv_b, dt_bias, a_log, d_cols))


def _s5_coeffs(are, aim, ls):
    step = jnp.exp(ls)
    mag = jnp.exp(are * step)
    th = aim * step
    lre, lim = mag * jnp.cos(th), mag * jnp.sin(th)
    den = are * are + aim * aim
    nr = lre - 1.0
    fre = (nr * are + lim * aim) / den
    fim = (lim * are - nr * aim) / den
    return step, lre, lim, den, fre, fim


def _s5_prep(are, aim, ls, bre_bd, bim_bd):
    def body(are_ref, aim_ref, ls_ref, bre_ref, bim_ref, lre_ref, lim_ref, bbr_ref, bbi_ref):
        _, lre, lim, _, fre, fim = _s5_coeffs(are_ref[...], aim_ref[...], ls_ref[...])
        lre_ref[...] = lre
        lim_ref[...] = lim
        bre, bim = bre_ref[...], bim_ref[...]
        bbr_ref[...] = fre * bre - fim * bim
        bbi_ref[...] = fre * bim + fim * bre

    col = jax.ShapeDtypeStruct((S5_N, 1), f32)
    mat = jax.ShapeDtypeStruct((S5_N, GROUP_W), f32)
    return pl.pallas_call(body, name="s5_prep", out_shape=[col, col, mat, mat], in_specs=[VMEM] * 5, out_specs=[VMEM] * 4,
                          compiler_params=_params())(are, aim, ls, bre_bd, bim_bd)


def _s5_prep_bwd(are, aim, ls, bre_bd, bim_bd, dlre, dlim, dbbr, dbbi):
    def body(are_ref, aim_ref, ls_ref, bre_ref, bim_ref, dlre_ref, dlim_ref, dbbr_ref, dbbi_ref,
             dare_ref, daim_ref, dls_ref, dbre_ref, dbim_ref):
        are, aim = are_ref[...], aim_ref[...]
        step, lre, lim, den, fre, fim = _s5_coeffs(are, aim, ls_ref[...])
        r = lax.broadcasted_iota(jnp.int32, (S5_N, GROUP_W), 0) // 64
        c = lax.broadcasted_iota(jnp.int32, (S5_N, GROUP_W), 1) // 16
        mask = r == c
        gr = jnp.where(mask, dbbr_ref[...], 0.0)
        gi = jnp.where(mask, dbbi_ref[...], 0.0)
        bre, bim = bre_ref[...], bim_ref[...]
        dbre_ref[...] = fre * gr + fim * gi
        dbim_ref[...] = fre * gi - fim * gr
        dfre = jnp.sum(bre * gr + bim * gi, axis=1, keepdims=True)
        dfim = jnp.sum(bre * gi - bim * gr, axis=1, keepdims=True)
        ire, iim = are / den, aim / den
        tre = dlre_ref[...] + ire * dfre - iim * dfim
        tim = dlim_ref[...] + ire * dfim + iim * dfre
        dzre = lre * tre + lim * tim
        dzim = lre * tim - lim * tre
        qre = (fre * are + fim * aim) / den
        qim = (fim * are - fre * aim) / den
        dare_ref[...] = step * dzre - (qre * dfre + qim * dfim)
        daim_ref[...] = step * dzim - (qre * dfim - qim * dfre)
        dls = (are * dzre + aim * dzim) * step
        sel = (lax.broadcasted_iota(jnp.int32, (S5_N, LANES), 0) // 64 == lax.broadcasted_iota(jnp.int32, (S5_N, LANES), 1)).astype(f32)
        dls_ref[...] = lax.dot_general(sel, jnp.broadcast_to(dls, (S5_N, LANES)), (((0,), (0,)), ((), ())),
                                       preferred_element_type=f32, precision=lax.Precision.HIGHEST)

    col = jax.ShapeDtypeStruct((S5_N, 1), f32)
    mat = jax.ShapeDtypeStruct((S5_N, GROUP_W), f32)
    return pl.pallas_call(body, name="s5_prep_bwd", out_shape=[col, col, jax.ShapeDtypeStruct((LANES, LANES), f32), mat, mat],
                          in_specs=[VMEM] * 9, out_specs=[VMEM] * 5, compiler_params=_params(),
                          )(are, aim, ls, bre_bd, bim_bd, dlre, dlim, dbbr, dbbi)


def _cmul(ar, ai, br, bi):
    return ar * br - ai * bi, ar * bi + ai * br


def _s5_scan(re_ref, im_ref, carry_ref, mr, mi, n_groups, reverse):
    p1 = (mr, mi)
    p2 = _cmul(*p1, *p1)
    p3 = _cmul(*p2, *p1)
    p4 = _cmul(*p2, *p2)
    p5 = _cmul(*p4, *p1)
    p6 = _cmul(*p4, *p2)
    p7 = _cmul(*p4, *p3)
    p8 = _cmul(*p4, *p4)
    pows = [p1, p2, p3, p4, p5, p6, p7, p8]
    row = lax.broadcasted_iota(jnp.int32, (8, S5_N), 0)
    tr = jnp.zeros((8, S5_N), f32)
    ti = jnp.zeros((8, S5_N), f32)
    for i in range(8):
        p = pows[7 - i] if reverse else pows[i]
        tr = jnp.where(row == i, p[0], tr)
        ti = jnp.where(row == i, p[1], ti)
    steps = []
    for k, p in ((1, p1), (2, p2), (4, p4)):
        keep = (row + k < 8) if reverse else (row >= k)
        steps.append((8 - k if reverse else k, keep, jnp.broadcast_to(p[0], (8, S5_N)), jnp.broadcast_to(p[1], (8, S5_N))))
    edge = 0 if reverse else 7

    def step(j, carry):
        cr, ci = carry
        g = (n_groups - 1 - j) if reverse else j
        r0 = pl.multiple_of(g * 8, 8)
        xr = re_ref[pl.ds(r0, 8), :]
        xi = im_ref[pl.ds(r0, 8), :]
        for shift, keep, br, bi in steps:
            sr = jnp.where(keep, pltpu.roll(xr, shift, 0), 0.0)
            si = jnp.where(keep, pltpu.roll(xi, shift, 0), 0.0)
            xr, xi = xr + br * sr - bi * si, xi + br * si + bi * sr
        xr, xi = xr + tr * cr - ti * ci, xi + tr * ci + ti * cr
        re_ref[pl.ds(r0, 8), :] = xr
        im_ref[pl.ds(r0, 8), :] = xi
        return (jnp.broadcast_to(xr[edge:edge + 1, :], (8, S5_N)), jnp.broadcast_to(xi[edge:edge + 1, :], (8, S5_N)))

    cr, ci = lax.fori_loop(0, n_groups, step, (carry_ref[0], carry_ref[1]))
    carry_ref[0] = cr
    carry_ref[1] = ci


def _s5_output(u, xr, xi, ctr, cti, d):
    return _dot_nt(_b(xr), _b(ctr)) - _dot_nt(_b(xi), _b(cti)) + d * u


def _s5_fwd(proj, bbr, bbi, ctr, cti, lre, lim, d, glu_w, glu_b, tb, xchg=None):
    t = proj.shape[0]

    def body(u_ref, bbr_ref, bbi_ref, ctr_ref, cti_ref, lr_ref, li_ref, d_ref, gw_ref, gb_ref, o_ref, xr_ref, xi_ref, carry):
        @pl.when(pl.program_id(0) == 0)
        def _():
            carry[...] = jnp.zeros_like(carry)

        u = u_ref[...]
        ub = _b(u)
        xr_ref[...] = _dot_nt(ub, _b(bbr_ref[...]))
        xi_ref[...] = _dot_nt(ub, _b(bbi_ref[...]))
        _s5_scan(xr_ref, xi_ref, carry, lr_ref[...], li_ref[...], tb // 8, reverse=False)
        y = _s5_output(u, xr_ref[...], xi_ref[...], ctr_ref[...], cti_ref[...], d_ref[...])
        gl = _gelu(y)
        o_ref[...] = gl * jax.nn.sigmoid(_dot(_b(gl), _b(gw_ref[...])) + gb_ref[...])

    state = pl.BlockSpec((tb, S5_N), lambda i: (i, 0))
    return _call(
        body, name="s5_fwd", grid=(t // tb,),
        out_shape=[jax.ShapeDtypeStruct((t, GROUP_W), f32), jax.ShapeDtypeStruct((t, S5_N), f32), jax.ShapeDtypeStruct((t, S5_N), f32)],
        in_specs=[pl.BlockSpec((tb, GROUP_W), lambda i: (i, C_S5)), _full((S5_N, GROUP_W)), _full((S5_N, GROUP_W)),
                  _full((GROUP_W, S5_N)), _full((GROUP_W, S5_N)), _full((1, S5_N)), _full((1, S5_N)),
                  _full((1, GROUP_W)), _full((GROUP_W, GROUP_W)), _full((1, GROUP_W))],
        out_specs=[pl.BlockSpec((tb, GROUP_W), lambda i: (i, 0)), state, state],
        scratch_shapes=[pltpu.VMEM((2, 8, S5_N), f32)],
        semantics=("arbitrary",), xchg=xchg, args=(proj, bbr, bbi, ctr, cti, lre, lim, d, glu_w, glu_b))


def _s5_bwd(proj, dyd, xr_all, xi_all, bbr, bbi, ctr, cti, lre, lim, d, glu_w, glu_b, tb, xchg=None):
    t = proj.shape[0]
    nt = t // tb

    def body(u_ref, dy_ref, xr_ref, xi_ref, xrh_ref, xih_ref, bbr_ref, bbi_ref, ctr_ref, cti_ref, lr_ref, li_ref,
             d_ref, gw_ref, gb_ref,
             du_ref, dlr_ref, dli_ref, dbbr_ref, dbbi_ref, dctr_ref, dcti_ref, dd_ref, dgw_ref, dgb_ref,
             gr_ref, gi_ref, carry):
        i = pl.program_id(0)
        ti = nt - 1 - i

        @pl.when(i == 0)
        def _():
            carry[...] = jnp.zeros_like(carry)

        u = u_ref[...]
        ub = _b(u)
        xr, xi = xr_ref[...], xi_ref[...]
        ctr, cti = _b(ctr_ref[...]), _b(cti_ref[...])
        d = d_ref[...]
        gw = _b(gw_ref[...])
        y = _s5_output(u, xr, xi, ctr, cti, d)
        gl = _gelu(y)
        sg = jax.nn.sigmoid(_dot(_b(gl), gw) + gb_ref[...])
        dout = dy_ref[...]
        q = dout * gl * sg * (1.0 - sg)
        qb = _b(q)
        dgl = dout * sg + _dot_nt(qb, gw)
        _acc(dgw_ref, _dot_tn(_b(gl), qb))
        _acc(dgb_ref, _colsum(q))
        dyv = dgl * _dgelu(y)
        _acc(dd_ref, _colsum(dyv * u))
        dyb = _b(dyv)
        gr_ref[...] = _dot(dyb, ctr)
        gi_ref[...] = -_dot(dyb, cti)
        _acc(dctr_ref, _dot_tn(dyb, _b(xr)))
        _acc(dcti_ref, -_dot_tn(dyb, _b(xi)))
        _s5_scan(gr_ref, gi_ref, carry, lr_ref[...], -li_ref[...], tb // 8, reverse=True)
        gr, gi = gr_ref[...], gi_ref[...]
        xpr = jnp.concatenate([jnp.where(ti > 0, xrh_ref[...], 0.0), xr], axis=0)[7:7 + tb]
        xpi = jnp.concatenate([jnp.where(ti > 0, xih_ref[...], 0.0), xi], axis=0)[7:7 + tb]
        _acc(dlr_ref, _colsum(gr * xpr + gi * xpi))
        _acc(dli_ref, _colsum(gi * xpr - gr * xpi))
        grb, gib = _b(gr), _b(gi)
        _acc(dbbr_ref, _dot_tn(grb, ub))
        _acc(dbbi_ref, _dot_tn(gib, ub))
        du_ref[...] = dyv * d + _dot(grb, _b(bbr_ref[...])) + _dot(gib, _b(bbi_ref[...]))

    state = pl.BlockSpec((tb, S5_N), lambda i: (nt - 1 - i, 0))
    prev = pl.BlockSpec((8, S5_N), lambda i: (jnp.maximum((nt - 1 - i) * (tb // 8) - 1, 0), 0))
    tile = pl.BlockSpec((tb, GROUP_W), lambda i: (nt - 1 - i, 0))
    return _call(
        body, name="s5_bwd", grid=(nt,),
        out_shape=[jax.ShapeDtypeStruct((t, GROUP_W), f32), jax.ShapeDtypeStruct((1, S5_N), f32), jax.ShapeDtypeStruct((1, S5_N), f32),
                   jax.ShapeDtypeStruct((S5_N, GROUP_W), f32), jax.ShapeDtypeStruct((S5_N, GROUP_W), f32),
                   jax.ShapeDtypeStruct((GROUP_W, S5_N), f32), jax.ShapeDtypeStruct((GROUP_W, S5_N), f32),
                   jax.ShapeDtypeStruct((1, GROUP_W), f32), jax.ShapeDtypeStruct((GROUP_W, GROUP_W), f32),
                   jax.ShapeDtypeStruct((1, GROUP_W), f32)],
        in_specs=[pl.BlockSpec((tb, GROUP_W), lambda i: (nt - 1 - i, C_S5)), tile, state, state, prev, prev,
                  _full((S5_N, GROUP_W)), _full((S5_N, GROUP_W)), _full((GROUP_W, S5_N)), _full((GROUP_W, S5_N)),
                  _full((1, S5_N)), _full((1, S5_N)), _full((1, GROUP_W)), _full((GROUP_W, GROUP_W)), _full((1, GROUP_W))],
        out_specs=[tile, _full((1, S5_N)), _full((1, S5_N)), _full((S5_N, GROUP_W)), _full((S5_N, GROUP_W)),
                   _full((GROUP_W, S5_N)), _full((GROUP_W, S5_N)), _full((1, GROUP_W)), _full((GROUP_W, GROUP_W)), _full((1, GROUP_W))],
        scratch_shapes=[pltpu.VMEM((tb, S5_N), f32), pltpu.VMEM((tb, S5_N), f32), pltpu.VMEM((2, 8, S5_N), f32)],
        semantics=("arbitrary",), xchg=xchg,
        args=(proj, dyd, xr_all, xi_all, xr_all, xi_all, bbr, bbi, ctr, cti, lre, lim, d, glu_w, glu_b))


def _outproj_fwd(ys, h, bn_w, g1, w_out, tb):
    t = h.shape[0]

    def body(ya_ref, yb_ref, yc_ref, yd_ref, h_ref, bn_ref, g1_ref, w_ref, h1_ref, o_ref, gr_ref):
        bn = bn_ref[...]
        parts = []
        for g, r in enumerate((ya_ref, yb_ref, yc_ref, yd_ref)):
            n, _ = _rms(r[...])
            parts.append(n * bn[:, g * GROUP_W:(g + 1) * GROUP_W])
        groups = _b(jnp.concatenate(parts, axis=1))
        gr_ref[...] = groups
        o = _dot(groups, w_ref[...])
        o_ref[...] = o
        h1_ref[...] = h_ref[...] + g1_ref[...] * o

    grp = pl.BlockSpec((tb, GROUP_W), lambda i: (i, 0))
    row = pl.BlockSpec((tb, D_MODEL), lambda i: (i, 0))
    vec = _full((1, D_MODEL))
    return pl.pallas_call(
        body, name="outproj_fwd", grid=(t // tb,),
        out_shape=[jax.ShapeDtypeStruct((t, D_MODEL), f32), jax.ShapeDtypeStruct((t, D_MODEL), f32),
                   jax.ShapeDtypeStruct((t, D_MODEL), bf16)],
        in_specs=[grp, grp, grp, grp, row, vec, vec, _full((D_MODEL, D_MODEL))],
        out_specs=[row, row, row],
        compiler_params=_params(("parallel",)),
    )(*ys, h, bn_w, g1, w_out)


def _outproj_bwd(dh1, o, ys, bn_w, g1, w_out, tb):
    t = dh1.shape[0]

    def body(dh_ref, o_ref, ya_ref, yb_ref, yc_ref, yd_ref, bn_ref, g1_ref, w_ref,
             da_ref, db_ref, dc_ref, dd_ref, do_ref, dg1_ref, dbn_ref):
        dh = dh_ref[...]
        _acc(dg1_ref, _colsum(dh * o_ref[...]))
        do = _b(dh * g1_ref[...])
        do_ref[...] = do
        dgroups = _dot_nt(do, w_ref[...])
        bn = bn_ref[...]
        dbn = []
        for g, (r, dr) in enumerate(((ya_ref, da_ref), (yb_ref, db_ref), (yc_ref, dc_ref), (yd_ref, dd_ref))):
            n, rr = _rms(r[...])
            dgr = dgroups[:, g * GROUP_W:(g + 1) * GROUP_W]
            dbn.append(_colsum(dgr * n))
            dr[...] = _rms_bwd(dgr * bn[:, g * GROUP_W:(g + 1) * GROUP_W], n, rr)
        _acc(dbn_ref, jnp.concatenate(dbn, axis=1))

    grp = pl.BlockSpec((tb, GROUP_W), lambda i: (i, 0))
    row = pl.BlockSpec((tb, D_MODEL), lambda i: (i, 0))
    vec = _full((1, D_MODEL))
    return pl.pallas_call(
        body, name="outproj_bwd", grid=(t // tb,),
        out_shape=[jax.ShapeDtypeStruct((t, GROUP_W), f32)] * 4 + [jax.ShapeDtypeStruct((t, D_MODEL), bf16),
                   jax.ShapeDtypeStruct((1, D_MODEL), f32), jax.ShapeDtypeStruct((1, D_MODEL), f32)],
        in_specs=[row, row, grp, grp, grp, grp, vec, vec, _full((D_MODEL, D_MODEL))],
        out_specs=[grp, grp, grp, grp, row, vec, vec],
        compiler_params=_params(("arbitrary",)),
    )(dh1, o, *ys, bn_w, g1, w_out)


def _mlp_fwd(h1, norm_w, sc, sh, g2, w1, w2, tb, xchg=None):
    t = h1.shape[0]
    nh = w1.shape[0]

    def body(h_ref, nw_ref, sc_ref, sh_ref, g2_ref, w1_ref, w2_ref, h2_ref, m_ref, v_ref, r_ref, acc):
        j = pl.program_id(1)

        @pl.when(j == 0)
        def _():
            n, _ = _rms(h_ref[...])
            v_ref[...] = _b(n * nw_ref[...] * (1.0 + sc_ref[...]) + sh_ref[...])

        ra = jnp.maximum(_dot(v_ref[...], w1_ref[0]), 0.0)
        r = _b(ra * ra)
        r_ref[...] = r
        p = _dot(r, w2_ref[0])

        @pl.when(j == 0)
        def _():
            acc[...] = p

        @pl.when(j != 0)
        def _():
            acc[...] += p

        @pl.when(j == nh - 1)
        def _():
            m = acc[...]
            m_ref[...] = _b(m)
            h2_ref[...] = h_ref[...] + g2_ref[...] * m

    row = pl.BlockSpec((tb, D_MODEL), lambda i, j: (i, 0))
    hid = pl.BlockSpec((tb, MLP_HB), lambda i, j: (i, j))
    vec = _full((1, D_MODEL))
    return _call(
        body, name="mlp_fwd", grid=(t // tb, nh),
        out_shape=[jax.ShapeDtypeStruct((t, D_MODEL), f32), jax.ShapeDtypeStruct((t, D_MODEL), bf16),
                   jax.ShapeDtypeStruct((t, D_MODEL), bf16), jax.ShapeDtypeStruct((t, nh * MLP_HB), bf16)],
        in_specs=[row, vec, vec, vec, vec, pl.BlockSpec((1, D_MODEL, MLP_HB), lambda i, j: (j, 0, 0)),
                  pl.BlockSpec((1, MLP_HB, D_MODEL), lambda i, j: (j, 0, 0))],
        out_specs=[row, row, row, hid],
        scratch_shapes=[pltpu.VMEM((tb, D_MODEL), f32)],
        semantics=("arbitrary", "arbitrary"), xchg=xchg, args=(h1, norm_w, sc, sh, g2, w1, w2))


def _mlp_bwd(dh2, m, h1, r, norm_w, sc, sh, g2, w1, w2, tb, xchg=None):
    t = h1.shape[0]
    nh = w1.shape[0]

    def body(dh_ref, m_ref, h_ref, r_ref, nw_ref, sc_ref, sh_ref, g2_ref, w1_ref, w2_ref,
             dh1_ref, do_ref, da_ref, dg2_ref, dsh_ref, dsc_ref, dnw_ref, acc):
        i, j = pl.program_id(0), pl.program_id(1)

        @pl.when(j == 0)
        def _():
            dh = dh_ref[...]
            _acc(dg2_ref, _colsum(dh * m_ref[...].astype(f32)))
            do_ref[...] = _b(dh * g2_ref[...])

        dr = _dot_nt(do_ref[...], w2_ref[0])
        da = _b(dr * 2.0 * jnp.sqrt(r_ref[...].astype(f32)))
        da_ref[...] = da
        p = _dot_nt(da, w1_ref[0])

        @pl.when(j == 0)
        def _():
            acc[...] = p

        @pl.when(j != 0)
        def _():
            acc[...] += p

        @pl.when(j == nh - 1)
        def _():
            dv = acc[...]
            n, r = _rms(h_ref[...])
            nw = nw_ref[...]
            gain = 1.0 + sc_ref[...]
            _acc(dsh_ref, _colsum(dv))
            _acc(dsc_ref, _colsum(dv * n * nw))
            _acc(dnw_ref, _colsum(dv * gain * n))
            dh1_ref[...] = dh_ref[...] + _rms_bwd(dv * nw * gain, n, r)

    row = pl.BlockSpec((tb, D_MODEL), lambda i, j: (i, 0))
    hid = pl.BlockSpec((tb, MLP_HB), lambda i, j: (i, j))
    vec = _full((1, D_MODEL))
    return _call(
        body, name="mlp_bwd", grid=(t // tb, nh),
        out_shape=[jax.ShapeDtypeStruct((t, D_MODEL), f32), jax.ShapeDtypeStruct((t, D_MODEL), bf16),
                   jax.ShapeDtypeStruct((t, nh * MLP_HB), bf16)] + [jax.ShapeDtypeStruct((1, D_MODEL), f32)] * 4,
        in_specs=[row, row, row, hid, vec, vec, vec, vec, pl.BlockSpec((1, D_MODEL, MLP_HB), lambda i, j: (j, 0, 0)),
                  pl.BlockSpec((1, MLP_HB, D_MODEL), lambda i, j: (j, 0, 0))],
        out_specs=[row, row, hid, vec, vec, vec, vec],
        scratch_shapes=[pltpu.VMEM((tb, D_MODEL), f32)],
        semantics=("arbitrary", "arbitrary"), xchg=xchg, args=(dh2, m, h1, r, norm_w, sc, sh, g2, w1, w2))


def _loss_head(h, target, norm_w, tb):
    t = h.shape[0]

    def body(h_ref, t_ref, w_ref, loss_ref, dh_ref, dw_ref):
        n, r = _rms(h_ref[...])
        w = w_ref[...]
        err = n * w - t_ref[...]
        part = 0.5 * jnp.sum(jnp.sum(err * err, axis=1, keepdims=True), axis=0, keepdims=True) / D_MODEL
        _acc(loss_ref, jnp.broadcast_to(part, (8, LANES)))
        dy = err / D_MODEL
        _acc(dw_ref, _colsum(dy * n))
        dh_ref[...] = _rms_bwd(dy * w, n, r)

    row = pl.BlockSpec((tb, D_MODEL), lambda i: (i, 0))
    return pl.pallas_call(
        body, name="loss_head", grid=(t // tb,),
        out_shape=[jax.ShapeDtypeStruct((8, LANES), f32), jax.ShapeDtypeStruct((t, D_MODEL), f32),
                   jax.ShapeDtypeStruct((1, D_MODEL), f32)],
        in_specs=[row, row, _full((1, D_MODEL))],
        out_specs=[_full((8, LANES)), row, _full((1, D_MODEL))],
        compiler_params=_params(("arbitrary",)),
    )(h, target, norm_w)


def _adam_math(w, g, m, v):
    m2 = ADAM_B1 * m + (1.0 - ADAM_B1) * g
    v2 = ADAM_B2 * v + (1.0 - ADAM_B2) * (g * g)
    mh = m2 / (1.0 - ADAM_B1 ** ADAM_STEP)
    vh = v2 / (1.0 - ADAM_B2 ** ADAM_STEP)
    return -ADAM_LR * (mh / (jnp.sqrt(vh) + ADAM_EPS) + ADAM_WD * w), m2, v2


def _sum_adamw(parts, w, m, v, name, rb):
    n_src, r, c = parts.shape

    def body(p_ref, w_ref, m_ref, v_ref, g_ref, d_ref, m2_ref, v2_ref):
        g = p_ref[0].astype(f32)
        for s in range(1, n_src):
            g = g + p_ref[s].astype(f32)
        g_ref[...] = g
        d, m2, v2 = _adam_math(w_ref[...], g, m_ref[...], v_ref[...])
        d_ref[...] = d
        m2_ref[...] = m2
        v2_ref[...] = v2

    blk = pl.BlockSpec((rb, c), lambda i: (i, 0))
    return pl.pallas_call(
        body, name=name, grid=(r // rb,),
        out_shape=[jax.ShapeDtypeStruct((r, c), f32)] * 4,
        in_specs=[pl.BlockSpec((n_src, rb, c), lambda i: (0, i, 0)), blk, blk, blk],
        out_specs=[blk] * 4,
        compiler_params=_params(("parallel",)),
    )(parts, w, m, v)


def _sum_adamw_layers(parts0, parts1, w, m, v, name, rb):
    n_src, r, c = parts0.shape
    nb = r // rb

    def body(p0_ref, p1_ref, w_ref, m_ref, v_ref, g_ref, d_ref, m2_ref, v2_ref):
        def update(p_ref):
            g = p_ref[0].astype(f32)
            for s in range(1, n_src):
                g = g + p_ref[s].astype(f32)
            g_ref[0] = g
            d, m2, v2 = _adam_math(w_ref[0], g, m_ref[0], v_ref[0])
            d_ref[0] = d
            m2_ref[0] = m2
            v2_ref[0] = v2

        @pl.when(pl.program_id(0) == 0)
        def _():
            update(p0_ref)

        @pl.when(pl.program_id(0) == 1)
        def _():
            update(p1_ref)

    blk = pl.BlockSpec((1, rb, c), lambda l, i: (l, i, 0))
    return pl.pallas_call(
        body, name=name, grid=(2, nb),
        out_shape=[jax.ShapeDtypeStruct((2, r, c), f32)] * 4,
        in_specs=[pl.BlockSpec((n_src, rb, c), lambda l, i: (0, jnp.where(l == 0, i, nb - 1), 0)),
                  pl.BlockSpec((n_src, rb, c), lambda l, i: (0, jnp.where(l == 1, i, 0), 0)), blk, blk, blk],
        out_specs=[blk] * 4,
        compiler_params=_params(("arbitrary", "arbitrary")),
    )(parts0, parts1, w, m, v)


def _reorder_in(w):
    pad = jnp.zeros(w.shape[:-1] + (P_IN - 2308,), w.dtype)
    return jnp.concatenate([w[..., :2048], w[..., 2052:2308], w[..., 2048:2052], pad], axis=-1)


def _unreorder_in(w):
    return jnp.concatenate([w[..., :2048], w[..., 2304:2308], w[..., 2048:2304]], axis=-1)


def _block_diag(w2d, n_blocks):
    rows, cols = w2d.shape
    tiled = jnp.tile(w2d, (1, n_blocks))
    rb = lax.broadcasted_iota(jnp.int32, tiled.shape, 0) // (rows // n_blocks)
    cb = lax.broadcasted_iota(jnp.int32, tiled.shape, 1) // cols
    return jnp.where(rb == cb, tiled, jnp.zeros_like(tiled))


def _block_diag_extract(w_bd, n_blocks):
    rows, wide = w_bd.shape
    r, c = rows // n_blocks, wide // n_blocks
    w4 = w_bd.reshape(n_blocks, r, n_blocks, c)
    idx = jnp.arange(n_blocks)
    return w4[idx, :, idx, :]


def _lanes128(v):
    return jnp.pad(v.reshape(1, -1), ((0, 0), (0, LANES - v.size)))


def _rows_of(shape):
    n = 1
    for d in shape:
        n *= d
    return -(-n // (8 * LANES)) * 8, n


def _flat_pack(arrs, row_multiple=8):
    blocks = []
    for a in arrs:
        rows, n = _rows_of(a.shape)
        blocks.append(jnp.pad(a.reshape(-1), (0, rows * LANES - n)).reshape(rows, LANES))
    total = sum(b.shape[0] for b in blocks)
    pad = -total % row_multiple
    if pad:
        blocks.append(jnp.zeros((pad, LANES), blocks[0].dtype))
    return jnp.concatenate(blocks, axis=0)


def _flat_unpack(packed, shapes):
    out, off = [], 0
    for s in shapes:
        rows, n = _rows_of(s)
        out.append(packed[off:off + rows].reshape(-1)[:n].reshape(s))
        off += rows
    return out


_W_NAMES = ['norm_mix_w', 'norm_mlp_w', 'ada_w', 'ada_b', 'w_in', 'pool_w', 'pool_scale', 'sconv_w', 'ssd_conv_w',
            'ssd_conv_b', 'ssd_dt_bias', 'ssd_a_log', 'ssd_d', 's5_a_re', 's5_a_im', 's5_log_step', 's5_b_re', 's5_b_im',
            's5_c_re', 's5_c_im', 's5_d', 's5_glu_w', 's5_glu_b', 'branch_norm_w', 'w_out', 'mlp_w1', 'mlp_w2',
            'final_norm_w']
_BIG = ('ada_w', 'w_in', 'w_out', 'mlp_w1', 'mlp_w2')
_SMALL = [n for n in _W_NAMES if n not in _BIG]
_SHARDED_SMALL = {'sconv_w': (2, 32), 'ssd_conv_w': (2, 96), 's5_glu_w': (1, 32)}


def _gather(*blocks):
    return _Exchange(blocks, gather=True)


def _scatter(*parts):
    return _Exchange(parts, gather=False)


def _layer_forward(l, h, p, w, sh_b, tb):
    first = l == 0
    (proj, u_b), got = _inproj_fwd(h, p['norm_mix_w'][l], p['sc1'][l], p['sh1'][l], w['w_in', l], tb,
                                   xchg=_gather(sh_b[1][0]) if first else None)
    if first:
        w['w_out', 0] = got[0].reshape(D_MODEL, D_MODEL)
    ya = _pool_fwd(proj, p['pool_bd'][l], p['pool_scale'][l], tb)
    yb = _sconv_fwd(proj, p['sconv_w8'][l], tb)
    (yc, yc_pre, states), got = _ssd_fwd(proj, p['ssd_conv_w8'][l], p['ssd_conv_b'][l], p['ssd_dt_bias'][l], p['ssd_a_log'][l],
                                         p['ssd_d_cols'][l], tb, xchg=_gather(sh_b[2][0]) if first else None)
    if first:
        w['w1', 0] = got[0]
    (yd, xr, xi), got = _s5_fwd(proj, p['bbr'][l], p['bbi'][l], p['ctr'][l], p['cti'][l], p['lre'][l], p['lim'][l],
                                p['s5_d'][l], p['glu_w'][l], p['glu_b'][l], tb, xchg=_gather(sh_b[3][0]) if first else None)
    if first:
        w['w2', 0] = got[0]
    ys = (ya, yb, yc, yd)
    h1, o, groups_b = _outproj_fwd(ys, h, p['branch_norm_w'][l], p['g1'][l], w['w_out', l], tb)
    (h2, m, v_b, r_b), got = _mlp_fwd(h1, p['norm_mlp_w'][l], p['sc2'][l], p['sh2'][l], p['g2'][l], w['w1', l], w['w2', l],
                                      min(MLP_TB, h.shape[0]), xchg=_gather(*[sh_b[k][1] for k in range(4)]) if first else None)
    if first:
        w['w_in', 1] = got[0].reshape(D_MODEL, P_IN)
        w['w_out', 1] = got[1].reshape(D_MODEL, D_MODEL)
        w['w1', 1], w['w2', 1] = got[2], got[3]
    saved = dict(h=h, proj=proj, u_b=u_b, ys=ys, yc_pre=yc_pre, states=states, xr=xr, xi=xi, h1=h1, o=o,
                 groups_b=groups_b, m=m, v_b=v_b, r_b=r_b)
    return h2, saved


def _layer_backward(l, dh2, s, p, w, pending, recv, tb):
    def carry(names):
        names = [n for n in names if n in pending]
        return names, (_scatter(*[pending.pop(n) for n in names]) if names else None)

    def landed(names, got):
        for n, g in zip(names, got):
            recv[n] = g

    names, xchg = carry([('w_out', 1), ('w_in', 1)])
    (dh1, do2_b, da_b, dg2, dsh2, dsc2, dnw_mlp), got = _mlp_bwd(dh2, s['m'], s['h1'], s['r_b'], p['norm_mlp_w'][l], p['sc2'][l],
                                                                p['sh2'][l], p['g2'][l], w['w1', l], w['w2', l], tb, xchg=xchg)
    landed(names, got)
    pending['mlp_w2', l] = _wgrad(s['r_b'], do2_b, 1, "wgrad_w2", tm=1024, tk=1024).reshape(N_DEV, MLP_HB, D_MODEL)
    pending['mlp_w1', l] = _wgrad(s['v_b'], da_b, N_DEV, "wgrad_w1", tm=1024, tk=2048)
    dya, dyb, dyc, dyd, do1_b, dg1, dbn = _outproj_bwd(dh1, s['o'], s['ys'], p['branch_norm_w'][l], p['g1'][l], w['w_out', l], tb)
    pending['w_out', l] = _wgrad(s['groups_b'], do1_b, 1, "wgrad_wout", tm=1024, tk=1024).reshape(N_DEV, D_MODEL // N_DEV, D_MODEL)
    proj = s['proj']
    dv, dpool_bd, dpool_scale = _pool_bwd(proj, dya, p['pool_bd'][l], p['pool_scale'][l], tb)
    dgb, dgc, dhh, dsconv = _sconv_bwd(proj, dyb, p['sconv_w8'][l], tb)
    names, xchg = carry([('mlp_w1', l)] + ([('w_out', 0)] if l == 0 else []))
    (dz, dxs, dbm, dcm, ddt, dconv_w, dconv_b, ddtb, dalog, ddskip), got = _ssd_bwd(
        proj, dyc, s['yc_pre'], s['states'], p['ssd_conv_w8'][l], p['ssd_conv_b'][l], p['ssd_dt_bias'][l], p['ssd_a_log'][l],
        p['ssd_d_cols'][l], tb, xchg=xchg)
    landed(names, got)
    names, xchg = carry([('mlp_w2', l)])
    (du5, dlr, dli, dbbr, dbbi, dctr, dcti, dd5, dgw, dgb5), got = _s5_bwd(
        proj, dyd, s['xr'], s['xi'], p['bbr'][l], p['bbi'][l], p['ctr'][l], p['cti'][l], p['lre'][l], p['lim'][l],
        p['s5_d'][l], p['glu_w'][l], p['glu_b'][l], tb, xchg=xchg)
    landed(names, got)
    dare, daim, dls, dbre_bd, dbim_bd = _s5_prep_bwd(p['are_c'][l], p['aim_c'][l], p['ls_c'][l], p['bre_bd'][l], p['bim_bd'][l],
                                                     dlr.reshape(S5_N, 1), dli.reshape(S5_N, 1), dbbr, dbbi)
    dparts = (dv, dgb, dgc, dhh, dz, dxs, dbm, dcm, du5, ddt)
    dh, dproj_b, dsh1, dsc1, dnw_mix = _inproj_bwd(dparts, dh1, s['h'], p['norm_mix_w'][l], p['sc1'][l], p['sh1'][l], w['w_in', l], tb)
    pending['w_in', l] = _wgrad(s['u_b'], dproj_b, 1, "wgrad_win", tm=512, tk=1024).reshape(N_DEV, D_MODEL // N_DEV, P_IN)
    small = {
        'norm_mix_w': dnw_mix.reshape(D_MODEL), 'norm_mlp_w': dnw_mlp.reshape(D_MODEL),
        'ada_b': jnp.concatenate([dsh1, dsc1, dg1, dsh2, dsc2, dg2], axis=1).reshape(6 * D_MODEL),
        'pool_w': _block_diag_extract(dpool_bd, 4), 'pool_scale': dpool_scale.reshape(GROUP_W),
        'sconv_w': dsconv[0:3], 'ssd_conv_w': dconv_w[0:4], 'ssd_conv_b': dconv_b.reshape(768),
        'ssd_dt_bias': ddtb[0, 0:4], 'ssd_a_log': dalog[0, 0:4], 'ssd_d': ddskip[0, 0:4],
        's5_a_re': dare.reshape(16, 64), 's5_a_im': daim.reshape(16, 64), 's5_log_step': dls[0:16, 0],
        's5_b_re': _block_diag_extract(dbre_bd, 16), 's5_b_im': _block_diag_extract(dbim_bd, 16),
        's5_c_re': _block_diag_extract(dctr, 16), 's5_c_im': _block_diag_extract(dcti, 16),
        's5_d': dd5.reshape(GROUP_W), 's5_glu_w': dgw, 's5_glu_b': dgb5.reshape(GROUP_W),
        'branch_norm_w': dbn.reshape(D_MODEL),
    }
    return dh, small


def _prepare_params(a, me):
    pack_shapes = [(1, D_MODEL), (2, 3, 32), (2, 4, 96), (2, 32, GROUP_W)]
    packed = _flat_pack([a['c'], a['sconv_w'], a['ssd_conv_w'], a['s5_glu_w']])
    gathered = _gather_rows(packed, "gather_small")
    pieces = [_flat_unpack(gathered[d], pack_shapes) for d in range(N_DEV)]
    c_all = jnp.concatenate([pc[0] for pc in pieces], axis=0)
    sconv_full = jnp.concatenate([pc[1] for pc in pieces], axis=2)
    ssd_conv_full = jnp.concatenate([pc[2] for pc in pieces], axis=2)
    glu_full = jnp.concatenate([pc[3] for pc in pieces], axis=1)

    ada_b_cols = lax.dynamic_slice_in_dim(a['ada_b'], me * 768, 768, axis=1).reshape(2, 1, 768)
    cond, modrows = _ada_forward(c_all, a['ada_w'], ada_b_cols)
    mod_recv = _all_to_all_rows(modrows.transpose(1, 0, 2), "exchange_mod")
    mod = mod_recv.transpose(1, 0, 2).reshape(2, 6 * D_MODEL)
    p = {'cond': cond}
    for k, name in enumerate(('sh1', 'sc1', 'g1', 'sh2', 'sc2', 'g2')):
        p[name] = mod[:, k * D_MODEL:(k + 1) * D_MODEL].reshape(2, 1, D_MODEL)

    for name in ('norm_mix_w', 'norm_mlp_w', 'branch_norm_w'):
        p[name] = a[name].reshape(2, 1, D_MODEL)
    p['pool_bd'] = jnp.stack([_block_diag(a['pool_w'][l].reshape(GROUP_W, 64), 4) for l in range(2)])
    p['pool_scale'] = a['pool_scale'].reshape(2, 1, GROUP_W)
    p['sconv_w8'] = jnp.pad(sconv_full, ((0, 0), (0, 5), (0, 0)))
    p['ssd_conv_w8'] = jnp.pad(ssd_conv_full, ((0, 0), (0, 4), (0, 0)))
    p['ssd_conv_b'] = a['ssd_conv_b'].reshape(2, 1, 768)
    p['ssd_dt_bias'] = jnp.pad(a['ssd_dt_bias'], ((0, 0), (0, LANES - 4))).reshape(2, 1, LANES)
    p['ssd_a_log'] = jnp.pad(a['ssd_a_log'], ((0, 0), (0, LANES - 4))).reshape(2, 1, LANES)
    p['ssd_d_cols'] = jnp.repeat(a['ssd_d'], SSD_P, axis=1).reshape(2, 1, GROUP_W)
    p['are_c'] = a['s5_a_re'].reshape(2, S5_N, 1)
    p['aim_c'] = a['s5_a_im'].reshape(2, S5_N, 1)
    p['ls_c'] = jnp.repeat(a['s5_log_step'], 64, axis=1).reshape(2, S5_N, 1)
    p['bre_bd'] = jnp.stack([_block_diag(a['s5_b_re'][l].reshape(S5_N, 16), 16) for l in range(2)])
    p['bim_bd'] = jnp.stack([_block_diag(a['s5_b_im'][l].reshape(S5_N, 16), 16) for l in range(2)])
    p['ctr'] = jnp.stack([_block_diag(a['s5_c_re'][l].reshape(GROUP_W, 64), 16) for l in range(2)])
    p['cti'] = jnp.stack([_block_diag(a['s5_c_im'][l].reshape(GROUP_W, 64), 16) for l in range(2)])
    p['s5_d'] = a['s5_d'].reshape(2, 1, GROUP_W)
    p['glu_w'] = glu_full
    p['glu_b'] = a['s5_glu_b'].reshape(2, 1, GROUP_W)
    lre, lim, bbr, bbi = [], [], [], []
    for l in range(2):
        r = _s5_prep(p['are_c'][l], p['aim_c'][l], p['ls_c'][l], p['bre_bd'][l], p['bim_bd'][l])
        lre.append(r[0].reshape(1, S5_N))
        lim.append(r[1].reshape(1, S5_N))
        bbr.append(r[2])
        bbi.append(r[3])
    p['lre'], p['lim'], p['bbr'], p['bbi'] = lre, lim, bbr, bbi
    return p


def kernel(x, c, norm_mix_w, norm_mlp_w, ada_w, ada_b, w_in, pool_w, pool_scale, sconv_w, ssd_conv_w, ssd_conv_b, ssd_dt_bias, ssd_a_log, ssd_d, s5_a_re, s5_a_im, s5_log_step, s5_b_re, s5_b_im, s5_c_re, s5_c_im, s5_d, s5_glu_w, s5_glu_b, branch_norm_w, w_out, mlp_w1, mlp_w2, final_norm_w, loss_target, m_norm_mix_w, m_norm_mlp_w, m_ada_w, m_ada_b, m_w_in, m_pool_w, m_pool_scale, m_sconv_w, m_ssd_conv_w, m_ssd_conv_b, m_ssd_dt_bias, m_ssd_a_log, m_ssd_d, m_s5_a_re, m_s5_a_im, m_s5_log_step, m_s5_b_re, m_s5_b_im, m_s5_c_re, m_s5_c_im, m_s5_d, m_s5_glu_w, m_s5_glu_b, m_branch_norm_w, m_w_out, m_mlp_w1, m_mlp_w2, m_final_norm_w, v_norm_mix_w, v_norm_mlp_w, v_ada_w, v_ada_b, v_w_in, v_pool_w, v_pool_scale, v_sconv_w, v_ssd_conv_w, v_ssd_conv_b, v_ssd_dt_bias, v_ssd_a_log, v_ssd_d, v_s5_a_re, v_s5_a_im, v_s5_log_step, v_s5_b_re, v_s5_b_im, v_s5_c_re, v_s5_c_im, v_s5_d, v_s5_glu_w, v_s5_glu_b, v_branch_norm_w, v_w_out, v_mlp_w1, v_mlp_w2, v_final_norm_w):
    a = dict(locals())
    t = x.shape[1]
    tb = min(512, t)
    me = _my_index()
    p = _prepare_params(a, me)

    sh_b = _cast_shards([_reorder_in(w_in), w_out, mlp_w1, mlp_w2])
    w = {('w_in', 0): _exchange_alone(_gather(sh_b[0][0]), "gather_w_in0")[0].reshape(D_MODEL, P_IN)}

    h = x.reshape(t, D_MODEL)
    saved = []
    for l in range(2):
        h, s = _layer_forward(l, h, p, w, sh_b, tb)
        saved.append(s)
    loss_blk, dh, dfinal = _loss_head(h, loss_target.reshape(t, D_MODEL), final_norm_w.reshape(1, D_MODEL), tb)
    loss = lax.psum(loss_blk[0, 0], ("x", "y", "c"))

    pending, recv, small_parts = {}, {}, [None, None]
    for l in (1, 0):
        dh, small_parts[l] = _layer_backward(l, dh, saved[l], p, w, pending, recv, tb)
    grad_x = dh.reshape(1, t, D_MODEL)
    recv['w_in', 0] = _exchange_alone(_scatter(pending.pop(('w_in', 0))), "exchange_w_in0")[0]

    grads, deltas, new_m, new_v = {}, {}, {}, {}

    wmv_in = [_reorder_in(a[n]) for n in ('w_in', 'm_w_in', 'v_w_in')]
    outs = _sum_adamw_layers(recv['w_in', 0], recv['w_in', 1], *wmv_in, "adamw_w_in", 128)
    grads['w_in'], deltas['w_in'], new_m['w_in'], new_v['w_in'] = [_unreorder_in(o) for o in outs]
    for name, rb in (('w_out', 128), ('mlp_w1', 256), ('mlp_w2', 256)):
        grads[name], deltas[name], new_m[name], new_v[name] = _sum_adamw_layers(
            recv[name, 0], recv[name, 1], a[name], a['m_' + name], a['v_' + name], "adamw_" + name, rb)

    dmod = jnp.stack([small_parts[0]['ada_b'], small_parts[1]['ada_b']])
    dmod_recv = _all_to_all_rows(dmod.reshape(2, N_DEV, 768).transpose(1, 0, 2), "exchange_dmod")
    g_ada = _ada_backward(p['cond'], dmod_recv.transpose(1, 0, 2))
    grads['ada_w'], deltas['ada_w'], new_m['ada_w'], new_v['ada_w'] = _sum_adamw_layers(
        g_ada[0:1], g_ada[1:2], ada_w, m_ada_w, v_ada_w, "adamw_ada_w", 256)

    layered = [n for n in _SMALL if n != 'final_norm_w']
    full = [jnp.stack([small_parts[0][n], small_parts[1][n]]) for n in layered] + [dfinal.reshape(D_MODEL)]
    full_shapes = [f.shape for f in full]
    summed = _flat_unpack(_allreduce_rows(_flat_pack(full, row_multiple=64)), full_shapes)
    local = []
    for n, g in zip(_SMALL, summed):
        if n in _SHARDED_SMALL:
            axis, size = _SHARDED_SMALL[n]
            g = lax.dynamic_slice_in_dim(g, me * size, size, axis=axis)
        local.append(g.reshape(a[n].shape))
    local_shapes = [g.shape for g in local]
    packed = [_flat_pack(xs) for xs in (local, [a[n] for n in _SMALL], [a['m_' + n] for n in _SMALL], [a['v_' + n] for n in _SMALL])]
    outs = _sum_adamw(packed[0][None], packed[1], packed[2], packed[3], "adamw_small", packed[0].shape[0])
    for store, o in zip((grads, deltas, new_m, new_v), outs):
        for n, val in zip(_SMALL, _flat_unpack(o, local_shapes)):
            store[n] = val

    return (loss, grad_x, *[grads[n] for n in _W_NAMES], *[deltas[n] for n in _W_NAMES],
            *[new_m[n] for n in _W_NAMES], *[new_v[n] for n in _W_NAMES])
```

```python
import functools

import jax
import jax.numpy as jnp
from jax import lax
from jax.experimental import pallas as pl
from jax.experimental.pallas import tpu as pltpu

f32 = jnp.float32
bf16 = jnp.bfloat16

N_DEV = 8
D_MODEL = 1024
GROUP_W = 256
P_IN = 2432
DT_COL = 2304
SSD_CHUNK = 128
SSD_HEADS = 4
SSD_P = 64
S5_N = 1024
MLP_HB = 512
MLP_TB = 1024
MLP_SLABS = 2
EPS = 1e-6
LANES = 128
VMEM_LIMIT = 56 * 1024 * 1024
ADAM_LR, ADAM_B1, ADAM_B2, ADAM_EPS, ADAM_WD, ADAM_STEP = 0.001, 0.9, 0.999, 1e-08, 0.01, 10
POOL_WINDOWS = (2, 4, 8, 16)

C_POOL, C_GB, C_GC, C_HH, C_Z, C_XS, C_BM, C_CM, C_S5 = range(9)
C_DT128 = DT_COL // LANES

MESH = pl.DeviceIdType.MESH
ANY = pl.BlockSpec(memory_space=pl.ANY)
VMEM = pl.BlockSpec(memory_space=pltpu.VMEM)


def _dot(a, b):
    return jnp.dot(a, b, preferred_element_type=f32)


def _dot_nt(a, b):
    return lax.dot_general(a, b, (((1,), (1,)), ((), ())), preferred_element_type=f32)


def _dot_tn(a, b):
    return lax.dot_general(a, b, (((0,), (0,)), ((), ())), preferred_element_type=f32)


def _dot_exact(a, b):
    return jnp.dot(a, b, preferred_element_type=f32, precision=lax.Precision.HIGHEST)


def _b(x):
    return x.astype(bf16)


def _silu(x):
    return x * jax.nn.sigmoid(x)


def _dsilu(x):
    s = jax.nn.sigmoid(x)
    return s * (1.0 + x * (1.0 - s))


def _softplus(x):
    return jnp.maximum(x, 0.0) + jnp.log1p(jnp.exp(-jnp.abs(x)))


_GELU_K = 0.7978845608028654
_GELU_C = 0.044715


def _gelu(x):
    return 0.5 * x * (1.0 + jnp.tanh(_GELU_K * (x + _GELU_C * x * x * x)))


def _dgelu(x):
    th = jnp.tanh(_GELU_K * (x + _GELU_C * x * x * x))
    return 0.5 * (1.0 + th) + 0.5 * x * (1.0 - th * th) * _GELU_K * (1.0 + 3.0 * _GELU_C * x * x)


def _rms(h):
    r = lax.rsqrt(jnp.mean(h * h, axis=-1, keepdims=True) + EPS)
    return h * r, r


def _rms_bwd(dn, n, r):
    return r * (dn - n * jnp.mean(dn * n, axis=-1, keepdims=True))


def _colsum(x):
    return jnp.sum(x, axis=0, keepdims=True)


def _params(sem=None):
    return pltpu.CompilerParams(dimension_semantics=sem, vmem_limit_bytes=VMEM_LIMIT)


def _full(shape):
    return pl.BlockSpec(shape, lambda *_: (0,) * len(shape))


def _acc(ref, val):
    @pl.when(pl.program_id(0) == 0)
    def _():
        ref[...] = val

    @pl.when(pl.program_id(0) != 0)
    def _():
        ref[...] += val


def _me():
    return lax.axis_index("x"), lax.axis_index("y"), lax.axis_index("c")


def _my_index():
    x, y, c = _me()
    return 4 * x + 2 * y + c


def _coords(p):
    return (p // 4, (p // 2) % 2, p % 2)


class _Exchange:
    def __init__(self, srcs, gather):
        self.srcs = list(srcs)
        self.gather = gather
        self.n = len(self.srcs)
        self.out_shape = [jax.ShapeDtypeStruct(((N_DEV,) + s.shape) if gather else s.shape, s.dtype) for s in self.srcs]
        self.scratch = [pltpu.SemaphoreType.DMA((self.n, N_DEV)), pltpu.SemaphoreType.DMA((self.n, N_DEV)),
                        pltpu.SemaphoreType.DMA((self.n,))]

    def _src(self, refs, t, dev):
        return refs[t] if self.gather else refs[t].at[dev]

    def _remote(self, xin, xout, sems, t, k, me, to):
        return pltpu.make_async_remote_copy(
            src_ref=self._src(xin, t, to), dst_ref=xout[t].at[me], send_sem=sems[0].at[t, k], recv_sem=sems[1].at[t, k],
            device_id=_coords(to), device_id_type=MESH)

    def start(self, xin, xout, sems):
        me = _my_index()
        for t in range(self.n):
            pltpu.make_async_copy(self._src(xin, t, me), xout[t].at[me], sems[2].at[t]).start()
            for k in range(1, N_DEV):
                self._remote(xin, xout, sems, t, k, me, (me + k) % N_DEV).start()

    def wait(self, xin, xout, sems):
        me = _my_index()
        for t in range(self.n):
            for k in range(1, N_DEV):
                src = (me + N_DEV - k) % N_DEV
                pltpu.make_async_remote_copy(
                    src_ref=self._src(xin, t, src), dst_ref=xout[t].at[src], send_sem=sems[0].at[t, k],
                    recv_sem=sems[1].at[t, k], device_id=_coords(src), device_id_type=MESH).wait_recv()
        for t in range(self.n):
            for k in range(1, N_DEV):
                self._remote(xin, xout, sems, t, k, me, (me + k) % N_DEV).wait_send()
            pltpu.make_async_copy(self._src(xin, t, me), xout[t].at[me], sems[2].at[t]).wait()

    def forward(self, xin, xout, sems):
        pass


class _ChipGather:
    def __init__(self, srcs):
        self.srcs = list(srcs)
        self.n = len(self.srcs)
        self.out_shape = [jax.ShapeDtypeStruct((N_DEV,) + s.shape, s.dtype) for s in self.srcs]
        self.scratch = [pltpu.SemaphoreType.DMA((self.n, 7)), pltpu.SemaphoreType.DMA((self.n, 7)),
                        pltpu.SemaphoreType.DMA((self.n,))]

    @staticmethod
    def _places():
        x, y, c = _me()
        chips = [(1 - x, y), (x, 1 - y), (1 - x, 1 - y)]
        return (x, y, c), (x, y, 1 - c), chips

    @staticmethod
    def _slab(ref, dev):
        return ref.at[4 * dev[0] + 2 * dev[1] + dev[2]]

    def _copy(self, xin, xout, sems, t, k, block, to, src=None):
        return pltpu.make_async_remote_copy(
            src_ref=self._slab(xout[t], block) if src is None else src, dst_ref=self._slab(xout[t], block),
            send_sem=sems[0].at[t, k], recv_sem=sems[1].at[t, k], device_id=to, device_id_type=MESH)

    def start(self, xin, xout, sems):
        me, sibling, chips = self._places()
        for t in range(self.n):
            pltpu.make_async_copy(xin[t], self._slab(xout[t], me), sems[2].at[t]).start()
            self._copy(xin, xout, sems, t, 0, me, sibling, src=xin[t]).start()
            for j, chip in enumerate(chips):
                self._copy(xin, xout, sems, t, 1 + j, me, (*chip, me[2]), src=xin[t]).start()

    def forward(self, xin, xout, sems):
        me, sibling, chips = self._places()
        for t in range(self.n):
            for j, chip in enumerate(chips):
                self._copy(xin, xout, sems, t, 1 + j, (*chip, me[2]), me).wait_recv()
                self._copy(xin, xout, sems, t, 4 + j, (*chip, me[2]), sibling).start()

    def wait(self, xin, xout, sems):
        me, sibling, chips = self._places()
        for t in range(self.n):
            self._copy(xin, xout, sems, t, 0, sibling, me).wait_recv()
            for j, chip in enumerate(chips):
                self._copy(xin, xout, sems, t, 4 + j, (*chip, 1 - me[2]), me).wait_recv()
        for t in range(self.n):
            self._copy(xin, xout, sems, t, 0, me, sibling, src=xin[t]).wait_send()
            for j, chip in enumerate(chips):
                self._copy(xin, xout, sems, t, 1 + j, me, (*chip, me[2]), src=xin[t]).wait_send()
                self._copy(xin, xout, sems, t, 4 + j, (*chip, me[2]), sibling).wait_send()
            pltpu.make_async_copy(xin[t], self._slab(xout[t], me), sems[2].at[t]).wait()


def _call(body, *, name, grid, in_specs, out_specs, out_shape, args, semantics, scratch_shapes=(), xchg=None):
    if xchg is None:
        outs = pl.pallas_call(body, name=name, grid=grid, in_specs=in_specs, out_specs=out_specs, out_shape=out_shape,
                              scratch_shapes=list(scratch_shapes), compiler_params=_params(semantics))(*args)
        return outs, ()
    n_in, n_out, n_scr, n = len(in_specs), len(out_specs), len(scratch_shapes), xchg.n

    def carried(*refs):
        ins, xin = refs[:n_in], refs[n_in:n_in + n]
        outs, xout = refs[n_in + n:n_in + n + n_out], refs[n_in + n + n_out:n_in + 2 * n + n_out]
        scr, sems = refs[n_in + 2 * n + n_out:n_in + 2 * n + n_out + n_scr], refs[n_in + 2 * n + n_out + n_scr:]
        step = pl.program_id(0)
        for d in range(1, len(grid)):
            step = step * grid[d] + pl.program_id(d)
        n_steps = functools.reduce(lambda a, b: a * b, grid)

        @pl.when(step == 0)
        def _():
            xchg.start(xin, xout, sems)

        @pl.when(step == (2 * n_steps) // 3)
        def _():
            xchg.forward(xin, xout, sems)

        body(*ins, *outs, *scr)

        @pl.when(step == n_steps - 1)
        def _():
            xchg.wait(xin, xout, sems)

    res = pl.pallas_call(
        carried, name=name, grid=grid, in_specs=list(in_specs) + [ANY] * n, out_specs=list(out_specs) + [ANY] * n,
        out_shape=list(out_shape) + xchg.out_shape, scratch_shapes=list(scratch_shapes) + xchg.scratch,
        compiler_params=_params(("arbitrary",) * len(grid)))(*args, *xchg.srcs)
    return res[:n_out], tuple(res[n_out:])


def _exchange_alone(xchg, name):
    def body(*refs):
        xin, xout, sems = refs[:xchg.n], refs[xchg.n:2 * xchg.n], refs[2 * xchg.n:]
        xchg.start(xin, xout, sems)
        xchg.forward(xin, xout, sems)
        xchg.wait(xin, xout, sems)

    return pl.pallas_call(body, name=name, out_shape=xchg.out_shape, in_specs=[ANY] * xchg.n, out_specs=[ANY] * xchg.n,
                          scratch_shapes=xchg.scratch)(*xchg.srcs)


def _cast_shards(shards):
    n = len(shards)

    def body(*refs):
        for i, o in zip(refs[:n], refs[n:]):
            o[...] = i[...].astype(bf16)

    return pl.pallas_call(body, name="cast_shards", out_shape=[jax.ShapeDtypeStruct(s.shape, bf16) for s in shards],
                          in_specs=[VMEM] * n, out_specs=[VMEM] * n, compiler_params=_params())(*shards)


def _allreduce_rows(v):
    r = v.shape[0]
    rp = r // N_DEV

    def body(v_ref, o_ref, parts, sums, send1, recv1, send2, recv2):
        me = _my_index()

        def piece(ref, d):
            return ref.at[pl.ds(pl.multiple_of(d * rp, 8), rp), :]

        def copy1(k, src_dev, to):
            return pltpu.make_async_remote_copy(src_ref=piece(v_ref, to), dst_ref=parts.at[src_dev], send_sem=send1.at[k],
                                                recv_sem=recv1.at[k], device_id=_coords(to), device_id_type=MESH)

        def copy2(k, owner, to):
            return pltpu.make_async_remote_copy(src_ref=sums, dst_ref=piece(o_ref, owner), send_sem=send2.at[k],
                                                recv_sem=recv2.at[k], device_id=_coords(to), device_id_type=MESH)

        for k in range(1, N_DEV):
            copy1(k, me, (me + k) % N_DEV).start()
        parts[me] = v_ref[pl.ds(pl.multiple_of(me * rp, 8), rp), :]
        for k in range(1, N_DEV):
            copy1(k, (me + N_DEV - k) % N_DEV, me).wait_recv()
        total = parts[0]
        for s in range(1, N_DEV):
            total = total + parts[s]
        sums[...] = total
        o_ref[pl.ds(pl.multiple_of(me * rp, 8), rp), :] = total
        for k in range(1, N_DEV):
            copy2(k, me, (me + k) % N_DEV).start()
        for k in range(1, N_DEV):
            copy2(k, (me + N_DEV - k) % N_DEV, me).wait_recv()
        for k in range(1, N_DEV):
            copy1(k, me, (me + k) % N_DEV).wait_send()
            copy2(k, me, (me + k) % N_DEV).wait_send()

    return pl.pallas_call(
        body, name="allreduce_small_grads", out_shape=jax.ShapeDtypeStruct(v.shape, v.dtype),
        in_specs=[VMEM], out_specs=VMEM,
        scratch_shapes=[pltpu.VMEM((N_DEV, rp, LANES), f32), pltpu.VMEM((rp, LANES), f32)]
        + [pltpu.SemaphoreType.DMA((N_DEV,))] * 4,
        compiler_params=_params(),
    )(v)


def _gather_rows(v, name):
    def body(v_ref, o_ref, send_sems, recv_sems):
        me = _my_index()
        o_ref[me] = v_ref[...]
        sends = []
        for k in range(1, N_DEV):
            peer = (me + k) % N_DEV
            rc = pltpu.make_async_remote_copy(src_ref=v_ref, dst_ref=o_ref.at[me], send_sem=send_sems.at[k],
                                              recv_sem=recv_sems.at[k], device_id=_coords(peer), device_id_type=MESH)
            rc.start()
            sends.append(rc)
        for k in range(1, N_DEV):
            src = (me + N_DEV - k) % N_DEV
            pltpu.make_async_remote_copy(src_ref=v_ref, dst_ref=o_ref.at[src], send_sem=send_sems.at[k],
                                         recv_sem=recv_sems.at[k], device_id=_coords(src), device_id_type=MESH).wait_recv()
        for rc in sends:
            rc.wait_send()

    return pl.pallas_call(
        body, name=name, out_shape=jax.ShapeDtypeStruct((N_DEV,) + v.shape, v.dtype),
        in_specs=[VMEM], out_specs=VMEM,
        scratch_shapes=[pltpu.SemaphoreType.DMA((N_DEV,)), pltpu.SemaphoreType.DMA((N_DEV,))],
        compiler_params=pltpu.CompilerParams(vmem_limit_bytes=VMEM_LIMIT),
    )(v)


def _all_to_all_rows(v, name):
    def body(v_ref, o_ref, send_sems, recv_sems):
        me = _my_index()
        o_ref[me] = v_ref[me]
        sends = []
        for k in range(1, N_DEV):
            peer = (me + k) % N_DEV
            rc = pltpu.make_async_remote_copy(src_ref=v_ref.at[peer], dst_ref=o_ref.at[me], send_sem=send_sems.at[k],
                                              recv_sem=recv_sems.at[k], device_id=_coords(peer), device_id_type=MESH)
            rc.start()
            sends.append(rc)
        for k in range(1, N_DEV):
            src = (me + N_DEV - k) % N_DEV
            pltpu.make_async_remote_copy(src_ref=v_ref.at[src], dst_ref=o_ref.at[src], send_sem=send_sems.at[k],
                                         recv_sem=recv_sems.at[k], device_id=_coords(src), device_id_type=MESH).wait_recv()
        for rc in sends:
            rc.wait_send()

    return pl.pallas_call(
        body, name=name, out_shape=jax.ShapeDtypeStruct(v.shape, v.dtype),
        in_specs=[VMEM], out_specs=VMEM,
        scratch_shapes=[pltpu.SemaphoreType.DMA((N_DEV,)), pltpu.SemaphoreType.DMA((N_DEV,))],
    )(v)


def _ada_forward(c_all, ada_w, ada_b_cols):
    def body(c_ref, w_ref, b_ref, cond_ref, o_ref):
        cond = _silu(c_ref[...])
        cond_ref[...] = cond
        for l in range(2):
            o_ref[l] = _dot(_b(cond), _b(w_ref[l])) + b_ref[l]

    return pl.pallas_call(
        body, name="ada_forward",
        out_shape=[jax.ShapeDtypeStruct((N_DEV, D_MODEL), f32), jax.ShapeDtypeStruct((2, N_DEV, 768), f32)],
        in_specs=[VMEM] * 3, out_specs=[VMEM] * 2, compiler_params=_params(),
    )(c_all, ada_w, ada_b_cols)


def _ada_backward(cond, dmod_rows):
    def body(c_ref, d_ref, o_ref):
        cb = _b(c_ref[...])
        for l in range(2):
            o_ref[l] = _dot_tn(cb, _b(d_ref[l]))

    return pl.pallas_call(
        body, name="ada_backward", out_shape=jax.ShapeDtypeStruct((2, D_MODEL, 768), f32),
        in_specs=[VMEM] * 2, out_specs=VMEM, compiler_params=_params(),
    )(cond, dmod_rows)


def _inproj_fwd(h, norm_w, sc, sh, w_in, tb, xchg=None):
    t = h.shape[0]

    def body(h_ref, nw_ref, sc_ref, sh_ref, w_ref, proj_ref, u_ref):
        n, _ = _rms(h_ref[...])
        u = _b(n * nw_ref[...] * (1.0 + sc_ref[...]) + sh_ref[...])
        u_ref[...] = u
        proj_ref[...] = _dot(u, w_ref[...])

    row = pl.BlockSpec((tb, D_MODEL), lambda i: (i, 0))
    vec = _full((1, D_MODEL))
    return _call(
        body, name="inproj_fwd", grid=(t // tb,),
        out_shape=[jax.ShapeDtypeStruct((t, P_IN), f32), jax.ShapeDtypeStruct((t, D_MODEL), bf16)],
        in_specs=[row, vec, vec, vec, _full((D_MODEL, P_IN))],
        out_specs=[pl.BlockSpec((tb, P_IN), lambda i: (i, 0)), row],
        semantics=("parallel",), args=(h, norm_w, sc, sh, w_in), xchg=xchg)


def _inproj_bwd(dparts, dh_res, h, norm_w, sc, sh, w_in, tb):
    t = h.shape[0]

    def body(*refs):
        parts = refs[:10]
        dres_ref, h_ref, nw_ref, sc_ref, sh_ref, w_ref = refs[10:16]
        dh_ref, dproj_ref, dsh_ref, dsc_ref, dnw_ref = refs[16:]
        dproj = _b(jnp.concatenate([p[...] for p in parts], axis=1))
        dproj_ref[...] = dproj
        du = _dot_nt(dproj, w_ref[...])
        n, r = _rms(h_ref[...])
        nw = nw_ref[...]
        gain = 1.0 + sc_ref[...]
        _acc(dsh_ref, _colsum(du))
        _acc(dsc_ref, _colsum(du * n * nw))
        _acc(dnw_ref, _colsum(du * gain * n))
        dh_ref[...] = dres_ref[...] + _rms_bwd(du * nw * gain, n, r)

    row = pl.BlockSpec((tb, D_MODEL), lambda i: (i, 0))
    vec = _full((1, D_MODEL))
    part_specs = [pl.BlockSpec((tb, GROUP_W), lambda i: (i, 0))] * 9 + [pl.BlockSpec((tb, LANES), lambda i: (i, 0))]
    return pl.pallas_call(
        body, name="inproj_bwd", grid=(t // tb,),
        out_shape=[jax.ShapeDtypeStruct((t, D_MODEL), f32), jax.ShapeDtypeStruct((t, P_IN), bf16)]
        + [jax.ShapeDtypeStruct((1, D_MODEL), f32)] * 3,
        in_specs=part_specs + [row, row, vec, vec, vec, _full((D_MODEL, P_IN))],
        out_specs=[row, pl.BlockSpec((tb, P_IN), lambda i: (i, 0)), vec, vec, vec],
        compiler_params=_params(("arbitrary",)),
    )(*dparts, dh_res, h, norm_w, sc, sh, w_in)


def _wgrad(a, b, n_blocks, name, tm, tk=512):
    t, m = a.shape
    nb = b.shape[1] // n_blocks
    tk = min(tk, t)
    nk = t // tk

    def body(a_ref, b_ref, o_ref, acc_ref):
        k = pl.program_id(2)
        p = _dot_tn(a_ref[...], b_ref[...])

        @pl.when(k == 0)
        def _():
            acc_ref[...] = p

        @pl.when(k != 0)
        def _():
            acc_ref[...] += p

        @pl.when(k == nk - 1)
        def _():
            o_ref[0] = acc_ref[...].astype(o_ref.dtype)

    return pl.pallas_call(
        body, name=name, grid=(m // tm, n_blocks, nk),
        out_shape=jax.ShapeDtypeStruct((n_blocks, m, nb), bf16),
        in_specs=[pl.BlockSpec((tk, tm), lambda i, j, k: (k, i)), pl.BlockSpec((tk, nb), lambda i, j, k: (k, j))],
        out_specs=pl.BlockSpec((1, tm, nb), lambda i, j, k: (j, i, 0)),
        scratch_shapes=[pltpu.VMEM((tm, nb), f32)],
        compiler_params=_params(("parallel", "parallel", "arbitrary")),
    )(a, b)


def _pool_counts(rows, t0):
    tpos = (lax.broadcasted_iota(jnp.int32, (rows, GROUP_W), 0) + t0 + 1).astype(f32)
    grp = lax.broadcasted_iota(jnp.int32, (rows, GROUP_W), 1) // 64
    win = jnp.where(grp == 0, 2.0, jnp.where(grp == 1, 4.0, jnp.where(grp == 2, 8.0, 16.0)))
    return jnp.minimum(tpos, win), grp


def _pool_select(grp, l1, l2, l3, l4):
    return jnp.where(grp == 0, l1, jnp.where(grp == 1, l2, jnp.where(grp == 2, l3, l4)))


def _pool_means(v, halo, t0):
    tb = v.shape[0]
    ext = jnp.concatenate([halo, v], axis=0)
    n = tb + 16
    s1 = ext[1:n] + ext[0:n - 1]
    s2 = s1[2:n - 1] + s1[0:n - 3]
    s3 = s2[4:n - 3] + s2[0:n - 7]
    s4 = s3[8:n - 7] + s3[0:n - 15]
    cnt, grp = _pool_counts(tb, t0)
    wsum = _pool_select(grp, s1[15:15 + tb], s2[13:13 + tb], s3[9:9 + tb], s4[1:1 + tb])
    return wsum / cnt - v


def _pool_fwd(proj, pw_bd, scale, tb):
    t = proj.shape[0]

    def body(v_ref, vh_ref, pw_ref, sc_ref, o_ref):
        i = pl.program_id(0)
        halo = jnp.where(i > 0, vh_ref[...], 0.0)
        p = _pool_means(v_ref[...], halo, i * tb)
        o_ref[...] = _dot(_b(p), _b(pw_ref[...])) * sc_ref[...]

    return pl.pallas_call(
        body, name="pool_fwd", grid=(t // tb,),
        out_shape=jax.ShapeDtypeStruct((t, GROUP_W), f32),
        in_specs=[pl.BlockSpec((tb, GROUP_W), lambda i: (i, C_POOL)),
                  pl.BlockSpec((16, GROUP_W), lambda i: (jnp.maximum(i * (tb // 16) - 1, 0), C_POOL)),
                  _full((GROUP_W, GROUP_W)), _full((1, GROUP_W))],
        out_specs=pl.BlockSpec((tb, GROUP_W), lambda i: (i, 0)),
        compiler_params=_params(("parallel",)),
    )(proj, proj, pw_bd, scale)


def _pool_bwd(proj, dy, pw_bd, scale, tb):
    t = proj.shape[0]
    nt = t // tb
    last16 = t // 16 - 1

    def body(v_ref, vh_ref, dy_ref, dyh_ref, pw_ref, sc_ref, dv_ref, dpw_ref, dsc_ref):
        i = pl.program_id(0)
        halo = jnp.where(i > 0, vh_ref[...], 0.0)
        p = _pool_means(v_ref[...], halo, i * tb)
        pw = _b(pw_ref[...])
        sc = sc_ref[...]
        dy = dy_ref[...]
        ypre = _dot(_b(p), pw)
        _acc(dsc_ref, _colsum(dy * ypre))
        dys = _b(dy * sc)
        _acc(dpw_ref, _dot_tn(_b(p), dys))
        dp = _dot_nt(dys, pw)
        dph = _dot_nt(_b(jnp.where(i < nt - 1, dyh_ref[...], 0.0) * sc), pw)
        cnt, grp = _pool_counts(tb, i * tb)
        cnth, _ = _pool_counts(16, (i + 1) * tb)
        ext = jnp.concatenate([dp / cnt, dph / cnth], axis=0)
        n = tb + 16
        f1 = ext[0:n - 1] + ext[1:n]
        f2 = f1[0:n - 3] + f1[2:n - 1]
        f3 = f2[0:n - 7] + f2[4:n - 3]
        f4 = f3[0:n - 15] + f3[8:n - 7]
        dv_ref[...] = _pool_select(grp, f1[0:tb], f2[0:tb], f3[0:tb], f4[0:tb]) - dp

    return pl.pallas_call(
        body, name="pool_bwd", grid=(nt,),
        out_shape=[jax.ShapeDtypeStruct((t, GROUP_W), f32), jax.ShapeDtypeStruct((GROUP_W, GROUP_W), f32),
                   jax.ShapeDtypeStruct((1, GROUP_W), f32)],
        in_specs=[pl.BlockSpec((tb, GROUP_W), lambda i: (i, C_POOL)),
                  pl.BlockSpec((16, GROUP_W), lambda i: (jnp.maximum(i * (tb // 16) - 1, 0), C_POOL)),
                  pl.BlockSpec((tb, GROUP_W), lambda i: (i, 0)),
                  pl.BlockSpec((16, GROUP_W), lambda i: (jnp.minimum((i + 1) * (tb // 16), last16), 0)),
                  _full((GROUP_W, GROUP_W)), _full((1, GROUP_W))],
        out_specs=[pl.BlockSpec((tb, GROUP_W), lambda i: (i, 0)), _full((GROUP_W, GROUP_W)), _full((1, GROUP_W))],
        compiler_params=_params(("arbitrary",)),
    )(proj, proj, dy, dy, pw_bd, scale)


def _sconv_fwd(proj, w, tb):
    t = proj.shape[0]

    def body(gb_ref, gc_ref, hh_ref, gch_ref, hhh_ref, w_ref, o_ref):
        i = pl.program_id(0)
        q = gc_ref[...] * hh_ref[...]
        qh = jnp.where(i > 0, gch_ref[...] * hhh_ref[...], 0.0)
        ext = jnp.concatenate([qh, q], axis=0)
        w = w_ref[...]
        conv = w[0:1] * ext[6:6 + tb] + w[1:2] * ext[7:7 + tb] + w[2:3] * ext[8:8 + tb]
        o_ref[...] = gb_ref[...] * conv

    def col(c):
        return pl.BlockSpec((tb, GROUP_W), lambda i: (i, c))

    def prev(c):
        return pl.BlockSpec((8, GROUP_W), lambda i: (jnp.maximum(i * (tb // 8) - 1, 0), c))

    return pl.pallas_call(
        body, name="sconv_fwd", grid=(t // tb,),
        out_shape=jax.ShapeDtypeStruct((t, GROUP_W), f32),
        in_specs=[col(C_GB), col(C_GC), col(C_HH), prev(C_GC), prev(C_HH), _full((8, GROUP_W))],
        out_specs=pl.BlockSpec((tb, GROUP_W), lambda i: (i, 0)),
        compiler_params=_params(("parallel",)),
    )(proj, proj, proj, proj, proj, w)


def _sconv_bwd(proj, dy, w, tb):
    t = proj.shape[0]
    nt = t // tb
    last8 = t // 8 - 1

    def body(gb_ref, gc_ref, hh_ref, gch_ref, hhh_ref, gbn_ref, dy_ref, dyn_ref, w_ref, dgb_ref, dgc_ref, dhh_ref, dw_ref):
        i = pl.program_id(0)
        gc, hh, gb, dy = gc_ref[...], hh_ref[...], gb_ref[...], dy_ref[...]
        q = gc * hh
        qh = jnp.where(i > 0, gch_ref[...] * hhh_ref[...], 0.0)
        ext = jnp.concatenate([qh, q], axis=0)
        w = w_ref[...]
        conv = w[0:1] * ext[6:6 + tb] + w[1:2] * ext[7:7 + tb] + w[2:3] * ext[8:8 + tb]
        dgb_ref[...] = dy * conv
        e = dy * gb
        en = jnp.where(i < nt - 1, dyn_ref[...] * gbn_ref[...], 0.0)
        exte = jnp.concatenate([e, en], axis=0)
        dq = w[2:3] * exte[0:tb] + w[1:2] * exte[1:1 + tb] + w[0:1] * exte[2:2 + tb]
        dgc_ref[...] = dq * hh
        dhh_ref[...] = dq * gc
        dw = jnp.concatenate([_colsum(e * ext[6:6 + tb]), _colsum(e * ext[7:7 + tb]), _colsum(e * ext[8:8 + tb]),
                              jnp.zeros((5, GROUP_W), f32)], axis=0)
        _acc(dw_ref, dw)

    def col(c):
        return pl.BlockSpec((tb, GROUP_W), lambda i: (i, c))

    def prev(c):
        return pl.BlockSpec((8, GROUP_W), lambda i: (jnp.maximum(i * (tb // 8) - 1, 0), c))

    def nxt(c):
        return pl.BlockSpec((8, GROUP_W), lambda i: (jnp.minimum((i + 1) * (tb // 8), last8), c))

    out = pl.BlockSpec((tb, GROUP_W), lambda i: (i, 0))
    return pl.pallas_call(
        body, name="sconv_bwd", grid=(nt,),
        out_shape=[jax.ShapeDtypeStruct((t, GROUP_W), f32)] * 3 + [jax.ShapeDtypeStruct((8, GROUP_W), f32)],
        in_specs=[col(C_GB), col(C_GC), col(C_HH), prev(C_GC), prev(C_HH), nxt(C_GB), col(0), nxt(0), _full((8, GROUP_W))],
        out_specs=[out, out, out, _full((8, GROUP_W))],
        compiler_params=_params(("arbitrary",)),
    )(proj, proj, proj, proj, proj, proj, dy, dy, w)


def _conv4(xr, halo, w, bias):
    tb = xr.shape[0]
    ext = jnp.concatenate([halo, xr], axis=0)
    pre = w[0:1] * ext[5:5 + tb] + w[1:2] * ext[6:6 + tb] + w[2:3] * ext[7:7 + tb] + w[3:4] * ext[8:8 + tb] + bias
    return pre, ext


def _tri():
    r = lax.broadcasted_iota(jnp.int32, (SSD_CHUNK, SSD_CHUNK), 0)
    c = lax.broadcasted_iota(jnp.int32, (SSD_CHUNK, SSD_CHUNK), 1)
    return r >= c


def _lane_pick(vals):
    rows = vals[0].shape[0]
    lane = lax.broadcasted_iota(jnp.int32, (rows, LANES), 1)
    out = jnp.zeros((rows, LANES), f32)
    for h, v in enumerate(vals):
        out = jnp.where(lane == h, v, out)
    return out


def _ssd_fwd(proj, conv_w, conv_b, dt_bias, a_log, d_cols, tb, xchg=None):
    t = proj.shape[0]
    cpt = tb // SSD_CHUNK

    def body(z_ref, xs_ref, bm_ref, cm_ref, xsh_ref, bmh_ref, cmh_ref, dt_ref, cw_ref, cb_ref, dtb_ref, al_ref, dk_ref,
             o_ref, y_ref, st_ref, state):
        i = pl.program_id(0)

        @pl.when(i == 0)
        def _():
            state[...] = jnp.zeros_like(state)

        cw, cb = cw_ref[...], cb_ref[...]
        acts = []
        for j, (r, hr) in enumerate(((xs_ref, xsh_ref), (bm_ref, bmh_ref), (cm_ref, cmh_ref))):
            halo = jnp.where(i > 0, hr[...], 0.0)
            pre, _ = _conv4(r[...], halo, cw[:, j * 256:(j + 1) * 256], cb[:, j * 256:(j + 1) * 256])
            acts.append(_silu(pre))
        xs, bm, cm = acts
        dt = _softplus(dt_ref[...] + dtb_ref[...])
        a = -jnp.exp(al_ref[...])
        adt = dt * a
        tri = _tri()
        trif = tri.astype(f32)
        dk = dk_ref[...]
        for c in range(cpt):
            rows = slice(c * SSD_CHUNK, (c + 1) * SSD_CHUNK)
            acol = _dot_exact(trif, adt[rows])
            arow = acol.T
            dt_c = dt[rows]
            ys = []
            for h in range(SSD_HEADS):
                g = h // 2
                ac = acol[:, h:h + 1]
                lm = jnp.exp(jnp.where(tri, ac - arow[h:h + 1, :], -jnp.inf))
                cg = _b(cm[rows, g * 128:(g + 1) * 128])
                bg = _b(bm[rows, g * 128:(g + 1) * 128])
                xh = xs[rows, h * SSD_P:(h + 1) * SSD_P]
                xdt = xh * dt_c[:, h:h + 1]
                m = _dot_nt(cg, bg) * lm
                s_in = state[h]
                st_ref[c, h] = s_in
                y = _dot(_b(m), _b(xdt)) + jnp.exp(ac) * _dot_nt(cg, _b(s_in)) + xh * dk[:, h * SSD_P:(h + 1) * SSD_P]
                ys.append(y)
                alast = ac[SSD_CHUNK - 1:SSD_CHUNK]
                wdec = jnp.exp(alast - ac)
                state[h] = jnp.exp(alast) * s_in + _dot_tn(_b(xdt * wdec), bg)
            yc = jnp.concatenate(ys, axis=1)
            y_ref[rows, :] = yc
            o_ref[rows, :] = yc * _silu(z_ref[rows, :])

    def col(c):
        return pl.BlockSpec((tb, GROUP_W), lambda i: (i, c))

    def prev(c):
        return pl.BlockSpec((8, GROUP_W), lambda i: (jnp.maximum(i * (tb // 8) - 1, 0), c))

    out = pl.BlockSpec((tb, GROUP_W), lambda i: (i, 0))
    return _call(
        body, name="ssd_fwd", grid=(t // tb,),
        out_shape=[jax.ShapeDtypeStruct((t, GROUP_W), f32), jax.ShapeDtypeStruct((t, GROUP_W), f32),
                   jax.ShapeDtypeStruct((t // SSD_CHUNK, SSD_HEADS, SSD_P, 128), f32)],
        in_specs=[col(C_Z), col(C_XS), col(C_BM), col(C_CM), prev(C_XS), prev(C_BM), prev(C_CM),
                  pl.BlockSpec((tb, LANES), lambda i: (i, C_DT128)),
                  _full((8, 768)), _full((1, 768)), _full((1, LANES)), _full((1, LANES)), _full((1, GROUP_W))],
        out_specs=[out, out, pl.BlockSpec((cpt, SSD_HEADS, SSD_P, 128), lambda i: (i, 0, 0, 0))],
        scratch_shapes=[pltpu.VMEM((SSD_HEADS, SSD_P, 128), f32)],
        semantics=("arbitrary",), xchg=xchg,
        args=(proj, proj, proj, proj, proj, proj, proj, proj, conv_w, conv_b, dt_bias, a_log, d_cols))


def _ssd_bwd(proj, dyc, y_pre, states, conv_w, conv_b, dt_bias, a_log, d_cols, tb, xchg=None):
    t = proj.shape[0]
    nt = t // tb
    cpt = tb // SSD_CHUNK

    def body(z_ref, xs_ref, bm_ref, cm_ref, xsh_ref, bmh_ref, cmh_ref, dt_ref, dy_ref, yp_ref, st_ref,
             cw_ref, cb_ref, dtb_ref, al_ref, dk_ref,
             dz_ref, dxs_ref, dbm_ref, dcm_ref, ddt_ref, dcw_ref, dcb_ref, ddtb_ref, dal_ref, ddk_ref,
             dstate, carry):
        i = pl.program_id(0)
        ti = nt - 1 - i

        @pl.when(i == 0)
        def _():
            dstate[...] = jnp.zeros_like(dstate)
            carry[...] = jnp.zeros_like(carry)

        cw, cb = cw_ref[...], cb_ref[...]
        pres, exts, acts = [], [], []
        for j, (r, hr) in enumerate(((xs_ref, xsh_ref), (bm_ref, bmh_ref), (cm_ref, cmh_ref))):
            halo = jnp.where(ti > 0, hr[...], 0.0)
            pre, ext = _conv4(r[...], halo, cw[:, j * 256:(j + 1) * 256], cb[:, j * 256:(j + 1) * 256])
            pres.append(pre)
            exts.append(ext)
            acts.append(_silu(pre))
        xs, bm, cm = acts
        raw = dt_ref[...] + dtb_ref[...]
        dt = _softplus(raw)
        a = -jnp.exp(al_ref[...])
        adt = dt * a
        tri = _tri()
        trif = tri.astype(f32)
        dk = dk_ref[...]
        z = z_ref[...]
        dyc = dy_ref[...]
        dz_ref[...] = dyc * yp_ref[...] * _dsilu(z)
        dy_all = dyc * _silu(z)
        lane = lax.broadcasted_iota(jnp.int32, (1, LANES), 1)
        ddk_acc = jnp.zeros((1, LANES), f32)
        dal_acc = jnp.zeros((1, LANES), f32)
        dxs_c, dbm_c, dcm_c, ddt_c = [None] * cpt, [None] * cpt, [None] * cpt, [None] * cpt
        for c in reversed(range(cpt)):
            rows = slice(c * SSD_CHUNK, (c + 1) * SSD_CHUNK)
            acol = _dot_exact(trif, adt[rows])
            arow = acol.T
            dt_c = dt[rows]
            da_cols, da_rows, ddt_heads, dxs_heads = [], [], [], []
            dbg = [jnp.zeros((SSD_CHUNK, 128), f32), jnp.zeros((SSD_CHUNK, 128), f32)]
            dcg = [jnp.zeros((SSD_CHUNK, 128), f32), jnp.zeros((SSD_CHUNK, 128), f32)]
            for h in range(SSD_HEADS):
                g = h // 2
                ac = acol[:, h:h + 1]
                lm = jnp.exp(jnp.where(tri, ac - arow[h:h + 1, :], -jnp.inf))
                cgf = cm[rows, g * 128:(g + 1) * 128]
                bgf = bm[rows, g * 128:(g + 1) * 128]
                cg, bg = _b(cgf), _b(bgf)
                xh = xs[rows, h * SSD_P:(h + 1) * SSD_P]
                dth = dt_c[:, h:h + 1]
                xdt = xh * dth
                xb = _b(xdt)
                dy = dy_all[rows, h * SSD_P:(h + 1) * SSD_P]
                dyb = _b(dy)
                s_in = st_ref[c, h]
                sb = _b(s_in)
                dsn = dstate[h]
                dsnb = _b(dsn)
                ea = jnp.exp(ac)
                alast = ac[SSD_CHUNK - 1:SSD_CHUNK]
                wdec = jnp.exp(alast - ac)
                el = jnp.exp(alast)
                m = _dot_nt(cg, bg) * lm
                dm = _dot_nt(dyb, xb)
                dx = _dot_tn(_b(m), dyb)
                dg = _b(dm * lm)
                dcg[g] = dcg[g] + _dot(dg, bg)
                dbg[g] = dbg[g] + _dot_tn(dg, cg)
                wm = dm * m
                da = jnp.sum(wm, axis=1, keepdims=True)
                da_rows.append(jnp.sum(wm, axis=0, keepdims=True))
                yoff = ea * _dot_nt(cg, sb)
                da = da + jnp.sum(dy * yoff, axis=1, keepdims=True)
                dye = _b(dy * ea)
                dcg[g] = dcg[g] + _dot(dye, sb)
                ds_y = _dot_tn(dye, cg)
                t1 = _dot(xb, dsnb)
                dbg[g] = dbg[g] + wdec * t1
                dwv = jnp.sum(t1 * bgf, axis=1, keepdims=True) * wdec
                dx = dx + _dot_nt(_b(bgf * wdec), dsnb)
                dalast = jnp.sum(dwv, axis=0, keepdims=True) + el * jnp.sum(jnp.sum(dsn * s_in, axis=1, keepdims=True), axis=0, keepdims=True)
                da = da - dwv
                rowi = lax.broadcasted_iota(jnp.int32, (SSD_CHUNK, 1), 0)
                da = da + jnp.where(rowi == SSD_CHUNK - 1, dalast, 0.0)
                dstate[h] = el * dsn + ds_y
                da_cols.append(da)
                ddt_heads.append(jnp.sum(dx * xh, axis=1, keepdims=True))
                dkh = dk[:, h * SSD_P:(h + 1) * SSD_P]
                dxs_heads.append(dx * dth + dy * dkh)
                ddk_acc = ddk_acc + jnp.where(lane == h, jnp.sum(_colsum(dy * xh), axis=1, keepdims=True), 0.0)
            da_blk = _lane_pick(da_cols)
            rowsel = lax.broadcasted_iota(jnp.int32, (SSD_CHUNK, SSD_CHUNK), 0)
            da_rows_blk = jnp.zeros((SSD_CHUNK, SSD_CHUNK), f32)
            for h in range(SSD_HEADS):
                da_rows_blk = jnp.where(rowsel == h, da_rows[h], da_rows_blk)
            da_blk = da_blk - da_rows_blk.T
            dadt = lax.dot_general(trif, da_blk, (((0,), (0,)), ((), ())), preferred_element_type=f32,
                                   precision=lax.Precision.HIGHEST)
            dal_acc = dal_acc + _colsum(dadt * dt_c)
            ddt_c[c] = dadt * a + _lane_pick(ddt_heads)
            dxs_c[c] = jnp.concatenate(dxs_heads, axis=1)
            dbm_c[c] = jnp.concatenate(dbg, axis=1)
            dcm_c[c] = jnp.concatenate(dcg, axis=1)
        ddt = jnp.concatenate(ddt_c, axis=0) if cpt > 1 else ddt_c[0]
        ddraw = jnp.where(lane < SSD_HEADS, ddt * jax.nn.sigmoid(raw), 0.0)
        ddt_ref[...] = ddraw
        _acc(ddtb_ref, _colsum(ddraw))
        _acc(dal_ref, jnp.where(lane < SSD_HEADS, dal_acc * a, 0.0))
        _acc(ddk_ref, ddk_acc)
        dcw_parts, dcb_parts = [], []
        for j, (dparts, out_ref) in enumerate(((dxs_c, dxs_ref), (dbm_c, dbm_ref), (dcm_c, dcm_ref))):
            dact = jnp.concatenate(dparts, axis=0) if cpt > 1 else dparts[0]
            dpre = dact * _dsilu(pres[j])
            w = cw[:, j * 256:(j + 1) * 256]
            ext = jnp.concatenate([dpre, carry[:, j * 256:(j + 1) * 256]], axis=0)
            out_ref[...] = w[3:4] * ext[0:tb] + w[2:3] * ext[1:1 + tb] + w[1:2] * ext[2:2 + tb] + w[0:1] * ext[3:3 + tb]
            carry[:, j * 256:(j + 1) * 256] = dpre[0:8]
            xe = exts[j]
            dcw_parts.append(jnp.concatenate([_colsum(dpre * xe[5 + k:5 + k + tb]) for k in range(4)]
                                             + [jnp.zeros((4, GROUP_W), f32)], axis=0))
            dcb_parts.append(_colsum(dpre))
        _acc(dcw_ref, jnp.concatenate(dcw_parts, axis=1))
        _acc(dcb_ref, jnp.concatenate(dcb_parts, axis=1))

    def col(c):
        return pl.BlockSpec((tb, GROUP_W), lambda i: (nt - 1 - i, c))

    def prev(c):
        return pl.BlockSpec((8, GROUP_W), lambda i: (jnp.maximum((nt - 1 - i) * (tb // 8) - 1, 0), c))

    out = pl.BlockSpec((tb, GROUP_W), lambda i: (nt - 1 - i, 0))
    vec = _full((1, LANES))
    return _call(
        body, name="ssd_bwd", grid=(nt,),
        out_shape=[jax.ShapeDtypeStruct((t, GROUP_W), f32)] * 4 + [jax.ShapeDtypeStruct((t, LANES), f32),
                   jax.ShapeDtypeStruct((8, 768), f32), jax.ShapeDtypeStruct((1, 768), f32)]
        + [jax.ShapeDtypeStruct((1, LANES), f32)] * 3,
        in_specs=[col(C_Z), col(C_XS), col(C_BM), col(C_CM), prev(C_XS), prev(C_BM), prev(C_CM),
                  pl.BlockSpec((tb, LANES), lambda i: (nt - 1 - i, C_DT128)), out, out,
                  pl.BlockSpec((cpt, SSD_HEADS, SSD_P, 128), lambda i: (nt - 1 - i, 0, 0, 0)),
                  _full((8, 768)), _full((1, 768)), vec, vec, _full((1, GROUP_W))],
        out_specs=[out, out, out, out, pl.BlockSpec((tb, LANES), lambda i: (nt - 1 - i, 0)),
                   _full((8, 768)), _full((1, 768)), vec, vec, vec],
        scratch_shapes=[pltpu.VMEM((SSD_HEADS, SSD_P, 128), f32), pltpu.VMEM((8, 768), f32)],
        semantics=("arbitrary",), xchg=xchg,
        args=(proj, proj, proj, proj, proj, proj, proj, proj, dyc, y_pre, states, conv_w, conv_b, dt_bias, a_log, d_cols))


def _s5_coeffs(are, aim, ls):
    step = jnp.exp(ls)
    mag = jnp.exp(are * step)
    th = aim * step
    lre, lim = mag * jnp.cos(th), mag * jnp.sin(th)
    den = are * are + aim * aim
    nr = lre - 1.0
    fre = (nr * are + lim * aim) / den
    fim = (lim * are - nr * aim) / den
    return step, lre, lim, den, fre, fim


def _s5_prep(are, aim, ls, bre_bd, bim_bd):
    def body(are_ref, aim_ref, ls_ref, bre_ref, bim_ref, lre_ref, lim_ref, bbr_ref, bbi_ref):
        _, lre, lim, _, fre, fim = _s5_coeffs(are_ref[...], aim_ref[...], ls_ref[...])
        lre_ref[...] = lre
        lim_ref[...] = lim
        bre, bim = bre_ref[...], bim_ref[...]
        bbr_ref[...] = fre * bre - fim * bim
        bbi_ref[...] = fre * bim + fim * bre

    col = jax.ShapeDtypeStruct((S5_N, 1), f32)
    mat = jax.ShapeDtypeStruct((S5_N, GROUP_W), f32)
    return pl.pallas_call(body, name="s5_prep", out_shape=[col, col, mat, mat], in_specs=[VMEM] * 5, out_specs=[VMEM] * 4,
                          compiler_params=_params())(are, aim, ls, bre_bd, bim_bd)


def _s5_prep_bwd(are, aim, ls, bre_bd, bim_bd, dlre, dlim, dbbr, dbbi):
    def body(are_ref, aim_ref, ls_ref, bre_ref, bim_ref, dlre_ref, dlim_ref, dbbr_ref, dbbi_ref,
             dare_ref, daim_ref, dls_ref, dbre_ref, dbim_ref):
        are, aim = are_ref[...], aim_ref[...]
        step, lre, lim, den, fre, fim = _s5_coeffs(are, aim, ls_ref[...])
        r = lax.broadcasted_iota(jnp.int32, (S5_N, GROUP_W), 0) // 64
        c = lax.broadcasted_iota(jnp.int32, (S5_N, GROUP_W), 1) // 16
        mask = r == c
        gr = jnp.where(mask, dbbr_ref[...], 0.0)
        gi = jnp.where(mask, dbbi_ref[...], 0.0)
        bre, bim = bre_ref[...], bim_ref[...]
        dbre_ref[...] = fre * gr + fim * gi
        dbim_ref[...] = fre * gi - fim * gr
        dfre = jnp.sum(bre * gr + bim * gi, axis=1, keepdims=True)
        dfim = jnp.sum(bre * gi - bim * gr, axis=1, keepdims=True)
        ire, iim = are / den, aim / den
        tre = dlre_ref[...] + ire * dfre - iim * dfim
        tim = dlim_ref[...] + ire * dfim + iim * dfre
        dzre = lre * tre + lim * tim
        dzim = lre * tim - lim * tre
        qre = (fre * are + fim * aim) / den
        qim = (fim * are - fre * aim) / den
        dare_ref[...] = step * dzre - (qre * dfre + qim * dfim)
        daim_ref[...] = step * dzim - (qre * dfim - qim * dfre)
        dls = (are * dzre + aim * dzim) * step
        sel = (lax.broadcasted_iota(jnp.int32, (S5_N, LANES), 0) // 64 == lax.broadcasted_iota(jnp.int32, (S5_N, LANES), 1)).astype(f32)
        dls_ref[...] = lax.dot_general(sel, jnp.broadcast_to(dls, (S5_N, LANES)), (((0,), (0,)), ((), ())),
                                       preferred_element_type=f32, precision=lax.Precision.HIGHEST)

    col = jax.ShapeDtypeStruct((S5_N, 1), f32)
    mat = jax.ShapeDtypeStruct((S5_N, GROUP_W), f32)
    return pl.pallas_call(body, name="s5_prep_bwd", out_shape=[col, col, jax.ShapeDtypeStruct((LANES, LANES), f32), mat, mat],
                          in_specs=[VMEM] * 9, out_specs=[VMEM] * 5, compiler_params=_params(),
                          )(are, aim, ls, bre_bd, bim_bd, dlre, dlim, dbbr, dbbi)


def _cmul(ar, ai, br, bi):
    return ar * br - ai * bi, ar * bi + ai * br


def _s5_scan(re_ref, im_ref, carry_ref, mr, mi, n_groups, reverse):
    p1 = (mr, mi)
    p2 = _cmul(*p1, *p1)
    p3 = _cmul(*p2, *p1)
    p4 = _cmul(*p2, *p2)
    p5 = _cmul(*p4, *p1)
    p6 = _cmul(*p4, *p2)
    p7 = _cmul(*p4, *p3)
    p8 = _cmul(*p4, *p4)
    pows = [p1, p2, p3, p4, p5, p6, p7, p8]
    row = lax.broadcasted_iota(jnp.int32, (8, S5_N), 0)
    tr = jnp.zeros((8, S5_N), f32)
    ti = jnp.zeros((8, S5_N), f32)
    for i in range(8):
        p = pows[7 - i] if reverse else pows[i]
        tr = jnp.where(row == i, p[0], tr)
        ti = jnp.where(row == i, p[1], ti)
    steps = []
    for k, p in ((1, p1), (2, p2), (4, p4)):
        keep = (row + k < 8) if reverse else (row >= k)
        steps.append((8 - k if reverse else k, keep, jnp.broadcast_to(p[0], (8, S5_N)), jnp.broadcast_to(p[1], (8, S5_N))))
    edge = 0 if reverse else 7

    def step(j, carry):
        cr, ci = carry
        g = (n_groups - 1 - j) if reverse else j
        r0 = pl.multiple_of(g * 8, 8)
        xr = re_ref[pl.ds(r0, 8), :]
        xi = im_ref[pl.ds(r0, 8), :]
        for shift, keep, br, bi in steps:
            sr = jnp.where(keep, pltpu.roll(xr, shift, 0), 0.0)
            si = jnp.where(keep, pltpu.roll(xi, shift, 0), 0.0)
            xr, xi = xr + br * sr - bi * si, xi + br * si + bi * sr
        xr, xi = xr + tr * cr - ti * ci, xi + tr * ci + ti * cr
        re_ref[pl.ds(r0, 8), :] = xr
        im_ref[pl.ds(r0, 8), :] = xi
        return (jnp.broadcast_to(xr[edge:edge + 1, :], (8, S5_N)), jnp.broadcast_to(xi[edge:edge + 1, :], (8, S5_N)))

    cr, ci = lax.fori_loop(0, n_groups, step, (carry_ref[0], carry_ref[1]))
    carry_ref[0] = cr
    carry_ref[1] = ci


def _s5_output(u, xr, xi, ctr, cti, d):
    return _dot_nt(_b(xr), _b(ctr)) - _dot_nt(_b(xi), _b(cti)) + d * u


def _s5_fwd(proj, bbr, bbi, ctr, cti, lre, lim, d, glu_w, glu_b, tb, xchg=None):
    t = proj.shape[0]

    def body(u_ref, bbr_ref, bbi_ref, ctr_ref, cti_ref, lr_ref, li_ref, d_ref, gw_ref, gb_ref, o_ref, xr_ref, xi_ref, carry):
        @pl.when(pl.program_id(0) == 0)
        def _():
            carry[...] = jnp.zeros_like(carry)

        u = u_ref[...]
        ub = _b(u)
        xr_ref[...] = _dot_nt(ub, _b(bbr_ref[...]))
        xi_ref[...] = _dot_nt(ub, _b(bbi_ref[...]))
        _s5_scan(xr_ref, xi_ref, carry, lr_ref[...], li_ref[...], tb // 8, reverse=False)
        y = _s5_output(u, xr_ref[...], xi_ref[...], ctr_ref[...], cti_ref[...], d_ref[...])
        gl = _gelu(y)
        o_ref[...] = gl * jax.nn.sigmoid(_dot(_b(gl), _b(gw_ref[...])) + gb_ref[...])

    state = pl.BlockSpec((tb, S5_N), lambda i: (i, 0))
    return _call(
        body, name="s5_fwd", grid=(t // tb,),
        out_shape=[jax.ShapeDtypeStruct((t, GROUP_W), f32), jax.ShapeDtypeStruct((t, S5_N), f32), jax.ShapeDtypeStruct((t, S5_N), f32)],
        in_specs=[pl.BlockSpec((tb, GROUP_W), lambda i: (i, C_S5)), _full((S5_N, GROUP_W)), _full((S5_N, GROUP_W)),
                  _full((GROUP_W, S5_N)), _full((GROUP_W, S5_N)), _full((1, S5_N)), _full((1, S5_N)),
                  _full((1, GROUP_W)), _full((GROUP_W, GROUP_W)), _full((1, GROUP_W))],
        out_specs=[pl.BlockSpec((tb, GROUP_W), lambda i: (i, 0)), state, state],
        scratch_shapes=[pltpu.VMEM((2, 8, S5_N), f32)],
        semantics=("arbitrary",), xchg=xchg, args=(proj, bbr, bbi, ctr, cti, lre, lim, d, glu_w, glu_b))


def _s5_bwd(proj, dyd, xr_all, xi_all, bbr, bbi, ctr, cti, lre, lim, d, glu_w, glu_b, tb, xchg=None):
    t = proj.shape[0]
    nt = t // tb

    def body(u_ref, dy_ref, xr_ref, xi_ref, xrh_ref, xih_ref, bbr_ref, bbi_ref, ctr_ref, cti_ref, lr_ref, li_ref,
             d_ref, gw_ref, gb_ref,
             du_ref, dlr_ref, dli_ref, dbbr_ref, dbbi_ref, dctr_ref, dcti_ref, dd_ref, dgw_ref, dgb_ref,
             gr_ref, gi_ref, carry):
        i = pl.program_id(0)
        ti = nt - 1 - i

        @pl.when(i == 0)
        def _():
            carry[...] = jnp.zeros_like(carry)

        u = u_ref[...]
        ub = _b(u)
        xr, xi = xr_ref[...], xi_ref[...]
        ctr, cti = _b(ctr_ref[...]), _b(cti_ref[...])
        d = d_ref[...]
        gw = _b(gw_ref[...])
        y = _s5_output(u, xr, xi, ctr, cti, d)
        gl = _gelu(y)
        sg = jax.nn.sigmoid(_dot(_b(gl), gw) + gb_ref[...])
        dout = dy_ref[...]
        q = dout * gl * sg * (1.0 - sg)
        qb = _b(q)
        dgl = dout * sg + _dot_nt(qb, gw)
        _acc(dgw_ref, _dot_tn(_b(gl), qb))
        _acc(dgb_ref, _colsum(q))
        dyv = dgl * _dgelu(y)
        _acc(dd_ref, _colsum(dyv * u))
        dyb = _b(dyv)
        gr_ref[...] = _dot(dyb, ctr)
        gi_ref[...] = -_dot(dyb, cti)
        _acc(dctr_ref, _dot_tn(dyb, _b(xr)))
        _acc(dcti_ref, -_dot_tn(dyb, _b(xi)))
        _s5_scan(gr_ref, gi_ref, carry, lr_ref[...], -li_ref[...], tb // 8, reverse=True)
        gr, gi = gr_ref[...], gi_ref[...]
        xpr = jnp.concatenate([jnp.where(ti > 0, xrh_ref[...], 0.0), xr], axis=0)[7:7 + tb]
        xpi = jnp.concatenate([jnp.where(ti > 0, xih_ref[...], 0.0), xi], axis=0)[7:7 + tb]
        _acc(dlr_ref, _colsum(gr * xpr + gi * xpi))
        _acc(dli_ref, _colsum(gi * xpr - gr * xpi))
        grb, gib = _b(gr), _b(gi)
        _acc(dbbr_ref, _dot_tn(grb, ub))
        _acc(dbbi_ref, _dot_tn(gib, ub))
        du_ref[...] = dyv * d + _dot(grb, _b(bbr_ref[...])) + _dot(gib, _b(bbi_ref[...]))

    state = pl.BlockSpec((tb, S5_N), lambda i: (nt - 1 - i, 0))
    prev = pl.BlockSpec((8, S5_N), lambda i: (jnp.maximum((nt - 1 - i) * (tb // 8) - 1, 0), 0))
    tile = pl.BlockSpec((tb, GROUP_W), lambda i: (nt - 1 - i, 0))
    return _call(
        body, name="s5_bwd", grid=(nt,),
        out_shape=[jax.ShapeDtypeStruct((t, GROUP_W), f32), jax.ShapeDtypeStruct((1, S5_N), f32), jax.ShapeDtypeStruct((1, S5_N), f32),
                   jax.ShapeDtypeStruct((S5_N, GROUP_W), f32), jax.ShapeDtypeStruct((S5_N, GROUP_W), f32),
                   jax.ShapeDtypeStruct((GROUP_W, S5_N), f32), jax.ShapeDtypeStruct((GROUP_W, S5_N), f32),
                   jax.ShapeDtypeStruct((1, GROUP_W), f32), jax.ShapeDtypeStruct((GROUP_W, GROUP_W), f32),
                   jax.ShapeDtypeStruct((1, GROUP_W), f32)],
        in_specs=[pl.BlockSpec((tb, GROUP_W), lambda i: (nt - 1 - i, C_S5)), tile, state, state, prev, prev,
                  _full((S5_N, GROUP_W)), _full((S5_N, GROUP_W)), _full((GROUP_W, S5_N)), _full((GROUP_W, S5_N)),
                  _full((1, S5_N)), _full((1, S5_N)), _full((1, GROUP_W)), _full((GROUP_W, GROUP_W)), _full((1, GROUP_W))],
        out_specs=[tile, _full((1, S5_N)), _full((1, S5_N)), _full((S5_N, GROUP_W)), _full((S5_N, GROUP_W)),
                   _full((GROUP_W, S5_N)), _full((GROUP_W, S5_N)), _full((1, GROUP_W)), _full((GROUP_W, GROUP_W)), _full((1, GROUP_W))],
        scratch_shapes=[pltpu.VMEM((tb, S5_N), f32), pltpu.VMEM((tb, S5_N), f32), pltpu.VMEM((2, 8, S5_N), f32)],
        semantics=("arbitrary",), xchg=xchg,
        args=(proj, dyd, xr_all, xi_all, xr_all, xi_all, bbr, bbi, ctr, cti, lre, lim, d, glu_w, glu_b))


def _outproj_fwd(ys, h, bn_w, g1, w_out, tb):
    t = h.shape[0]

    def body(ya_ref, yb_ref, yc_ref, yd_ref, h_ref, bn_ref, g1_ref, w_ref, h1_ref, o_ref, gr_ref):
        bn = bn_ref[...]
        parts = []
        for g, r in enumerate((ya_ref, yb_ref, yc_ref, yd_ref)):
            n, _ = _rms(r[...])
            parts.append(n * bn[:, g * GROUP_W:(g + 1) * GROUP_W])
        groups = _b(jnp.concatenate(parts, axis=1))
        gr_ref[...] = groups
        o = _dot(groups, w_ref[...])
        o_ref[...] = o
        h1_ref[...] = h_ref[...] + g1_ref[...] * o

    grp = pl.BlockSpec((tb, GROUP_W), lambda i: (i, 0))
    row = pl.BlockSpec((tb, D_MODEL), lambda i: (i, 0))
    vec = _full((1, D_MODEL))
    return pl.pallas_call(
        body, name="outproj_fwd", grid=(t // tb,),
        out_shape=[jax.ShapeDtypeStruct((t, D_MODEL), f32), jax.ShapeDtypeStruct((t, D_MODEL), f32),
                   jax.ShapeDtypeStruct((t, D_MODEL), bf16)],
        in_specs=[grp, grp, grp, grp, row, vec, vec, _full((D_MODEL, D_MODEL))],
        out_specs=[row, row, row],
        compiler_params=_params(("parallel",)),
    )(*ys, h, bn_w, g1, w_out)


def _outproj_bwd(dh1, o, ys, bn_w, g1, w_out, tb):
    t = dh1.shape[0]

    def body(dh_ref, o_ref, ya_ref, yb_ref, yc_ref, yd_ref, bn_ref, g1_ref, w_ref,
             da_ref, db_ref, dc_ref, dd_ref, do_ref, dg1_ref, dbn_ref):
        dh = dh_ref[...]
        _acc(dg1_ref, _colsum(dh * o_ref[...]))
        do = _b(dh * g1_ref[...])
        do_ref[...] = do
        dgroups = _dot_nt(do, w_ref[...])
        bn = bn_ref[...]
        dbn = []
        for g, (r, dr) in enumerate(((ya_ref, da_ref), (yb_ref, db_ref), (yc_ref, dc_ref), (yd_ref, dd_ref))):
            n, rr = _rms(r[...])
            dgr = dgroups[:, g * GROUP_W:(g + 1) * GROUP_W]
            dbn.append(_colsum(dgr * n))
            dr[...] = _rms_bwd(dgr * bn[:, g * GROUP_W:(g + 1) * GROUP_W], n, rr)
        _acc(dbn_ref, jnp.concatenate(dbn, axis=1))

    grp = pl.BlockSpec((tb, GROUP_W), lambda i: (i, 0))
    row = pl.BlockSpec((tb, D_MODEL), lambda i: (i, 0))
    vec = _full((1, D_MODEL))
    return pl.pallas_call(
        body, name="outproj_bwd", grid=(t // tb,),
        out_shape=[jax.ShapeDtypeStruct((t, GROUP_W), f32)] * 4 + [jax.ShapeDtypeStruct((t, D_MODEL), bf16),
                   jax.ShapeDtypeStruct((1, D_MODEL), f32), jax.ShapeDtypeStruct((1, D_MODEL), f32)],
        in_specs=[row, row, grp, grp, grp, grp, vec, vec, _full((D_MODEL, D_MODEL))],
        out_specs=[grp, grp, grp, grp, row, vec, vec],
        compiler_params=_params(("arbitrary",)),
    )(dh1, o, *ys, bn_w, g1, w_out)


def _mlp_fwd(h1, norm_w, sc, sh, g2, w1, w2, tb, xchg=None):
    t = h1.shape[0]
    nh = w1.shape[0] // MLP_SLABS

    def body(h_ref, nw_ref, sc_ref, sh_ref, g2_ref, w1_ref, w2_ref, h2_ref, m_ref, v_ref, r_ref, acc):
        j = pl.program_id(1)

        @pl.when(j == 0)
        def _():
            n, _ = _rms(h_ref[...])
            v_ref[...] = _b(n * nw_ref[...] * (1.0 + sc_ref[...]) + sh_ref[...])

        v = v_ref[...]
        p = None
        for s in range(MLP_SLABS):
            ra = jnp.maximum(_dot(v, w1_ref[s]), 0.0)
            r = _b(ra * ra)
            r_ref[:, s * MLP_HB:(s + 1) * MLP_HB] = r
            q = _dot(r, w2_ref[s])
            p = q if p is None else p + q

        @pl.when(j == 0)
        def _():
            acc[...] = p

        @pl.when(j != 0)
        def _():
            acc[...] += p

        @pl.when(j == nh - 1)
        def _():
            m = acc[...]
            m_ref[...] = _b(m)
            h2_ref[...] = h_ref[...] + g2_ref[...] * m

    row = pl.BlockSpec((tb, D_MODEL), lambda i, j: (i, 0))
    hid = pl.BlockSpec((tb, MLP_SLABS * MLP_HB), lambda i, j: (i, j))
    vec = _full((1, D_MODEL))
    return _call(
        body, name="mlp_fwd", grid=(t // tb, nh),
        out_shape=[jax.ShapeDtypeStruct((t, D_MODEL), f32), jax.ShapeDtypeStruct((t, D_MODEL), bf16),
                   jax.ShapeDtypeStruct((t, D_MODEL), bf16), jax.ShapeDtypeStruct((t, N_DEV * MLP_HB), bf16)],
        in_specs=[row, vec, vec, vec, vec, pl.BlockSpec((MLP_SLABS, D_MODEL, MLP_HB), lambda i, j: (j, 0, 0)),
                  pl.BlockSpec((MLP_SLABS, MLP_HB, D_MODEL), lambda i, j: (j, 0, 0))],
        out_specs=[row, row, row, hid],
        scratch_shapes=[pltpu.VMEM((tb, D_MODEL), f32)],
        semantics=("arbitrary", "arbitrary"), xchg=xchg, args=(h1, norm_w, sc, sh, g2, w1, w2))


def _mlp_bwd(dh2, m, h1, r, norm_w, sc, sh, g2, w1, w2, tb, xchg=None):
    t = h1.shape[0]
    nh = w1.shape[0] // MLP_SLABS

    def body(dh_ref, m_ref, h_ref, r_ref, nw_ref, sc_ref, sh_ref, g2_ref, w1_ref, w2_ref,
             dh1_ref, do_ref, da_ref, dg2_ref, dsh_ref, dsc_ref, dnw_ref, acc):
        j = pl.program_id(1)

        @pl.when(j == 0)
        def _():
            dh = dh_ref[...]
            _acc(dg2_ref, _colsum(dh * m_ref[...].astype(f32)))
            do_ref[...] = _b(dh * g2_ref[...])

        do = do_ref[...]
        p = None
        for s in range(MLP_SLABS):
            cols = slice(s * MLP_HB, (s + 1) * MLP_HB)
            dr = _dot_nt(do, w2_ref[s])
            da = _b(dr * 2.0 * jnp.sqrt(r_ref[:, cols].astype(f32)))
            da_ref[:, cols] = da
            q = _dot_nt(da, w1_ref[s])
            p = q if p is None else p + q

        @pl.when(j == 0)
        def _():
            acc[...] = p

        @pl.when(j != 0)
        def _():
            acc[...] += p

        @pl.when(j == nh - 1)
        def _():
            dv = acc[...]
            n, r = _rms(h_ref[...])
            nw = nw_ref[...]
            gain = 1.0 + sc_ref[...]
            _acc(dsh_ref, _colsum(dv))
            _acc(dsc_ref, _colsum(dv * n * nw))
            _acc(dnw_ref, _colsum(dv * gain * n))
            dh1_ref[...] = dh_ref[...] + _rms_bwd(dv * nw * gain, n, r)

    row = pl.BlockSpec((tb, D_MODEL), lambda i, j: (i, 0))
    hid = pl.BlockSpec((tb, MLP_SLABS * MLP_HB), lambda i, j: (i, j))
    vec = _full((1, D_MODEL))
    return _call(
        body, name="mlp_bwd", grid=(t // tb, nh),
        out_shape=[jax.ShapeDtypeStruct((t, D_MODEL), f32), jax.ShapeDtypeStruct((t, D_MODEL), bf16),
                   jax.ShapeDtypeStruct((t, N_DEV * MLP_HB), bf16)] + [jax.ShapeDtypeStruct((1, D_MODEL), f32)] * 4,
        in_specs=[row, row, row, hid, vec, vec, vec, vec,
                  pl.BlockSpec((MLP_SLABS, D_MODEL, MLP_HB), lambda i, j: (j, 0, 0)),
                  pl.BlockSpec((MLP_SLABS, MLP_HB, D_MODEL), lambda i, j: (j, 0, 0))],
        out_specs=[row, row, hid, vec, vec, vec, vec],
        scratch_shapes=[pltpu.VMEM((tb, D_MODEL), f32)],
        semantics=("arbitrary", "arbitrary"), xchg=xchg, args=(dh2, m, h1, r, norm_w, sc, sh, g2, w1, w2))


def _loss_head(h, target, norm_w, tb):
    t = h.shape[0]

    def body(h_ref, t_ref, w_ref, loss_ref, dh_ref, dw_ref):
        n, r = _rms(h_ref[...])
        w = w_ref[...]
        err = n * w - t_ref[...]
        part = 0.5 * jnp.sum(jnp.sum(err * err, axis=1, keepdims=True), axis=0, keepdims=True) / D_MODEL
        _acc(loss_ref, jnp.broadcast_to(part, (8, LANES)))
        dy = err / D_MODEL
        _acc(dw_ref, _colsum(dy * n))
        dh_ref[...] = _rms_bwd(dy * w, n, r)

    row = pl.BlockSpec((tb, D_MODEL), lambda i: (i, 0))
    return pl.pallas_call(
        body, name="loss_head", grid=(t // tb,),
        out_shape=[jax.ShapeDtypeStruct((8, LANES), f32), jax.ShapeDtypeStruct((t, D_MODEL), f32),
                   jax.ShapeDtypeStruct((1, D_MODEL), f32)],
        in_specs=[row, row, _full((1, D_MODEL))],
        out_specs=[_full((8, LANES)), row, _full((1, D_MODEL))],
        compiler_params=_params(("arbitrary",)),
    )(h, target, norm_w)


def _adam_math(w, g, m, v):
    m2 = ADAM_B1 * m + (1.0 - ADAM_B1) * g
    v2 = ADAM_B2 * v + (1.0 - ADAM_B2) * (g * g)
    mh = m2 / (1.0 - ADAM_B1 ** ADAM_STEP)
    vh = v2 / (1.0 - ADAM_B2 ** ADAM_STEP)
    return -ADAM_LR * (mh / (jnp.sqrt(vh) + ADAM_EPS) + ADAM_WD * w), m2, v2


def _sum_adamw(parts, w, m, v, name, rb):
    n_src, r, c = parts.shape

    def body(p_ref, w_ref, m_ref, v_ref, g_ref, d_ref, m2_ref, v2_ref):
        g = p_ref[0].astype(f32)
        for s in range(1, n_src):
            g = g + p_ref[s].astype(f32)
        g_ref[...] = g
        d, m2, v2 = _adam_math(w_ref[...], g, m_ref[...], v_ref[...])
        d_ref[...] = d
        m2_ref[...] = m2
        v2_ref[...] = v2

    blk = pl.BlockSpec((rb, c), lambda i: (i, 0))
    return pl.pallas_call(
        body, name=name, grid=(r // rb,),
        out_shape=[jax.ShapeDtypeStruct((r, c), f32)] * 4,
        in_specs=[pl.BlockSpec((n_src, rb, c), lambda i: (0, i, 0)), blk, blk, blk],
        out_specs=[blk] * 4,
        compiler_params=_params(("parallel",)),
    )(parts, w, m, v)


def _sum_adamw_layers(parts0, parts1, w, m, v, name, rb):
    n_src, r, c = parts0.shape
    nb = r // rb

    def body(p0_ref, p1_ref, w_ref, m_ref, v_ref, g_ref, d_ref, m2_ref, v2_ref):
        def update(p_ref):
            g = p_ref[0].astype(f32)
            for s in range(1, n_src):
                g = g + p_ref[s].astype(f32)
            g_ref[0] = g
            d, m2, v2 = _adam_math(w_ref[0], g, m_ref[0], v_ref[0])
            d_ref[0] = d
            m2_ref[0] = m2
            v2_ref[0] = v2

        @pl.when(pl.program_id(0) == 0)
        def _():
            update(p0_ref)

        @pl.when(pl.program_id(0) == 1)
        def _():
            update(p1_ref)

    blk = pl.BlockSpec((1, rb, c), lambda l, i: (l, i, 0))
    return pl.pallas_call(
        body, name=name, grid=(2, nb),
        out_shape=[jax.ShapeDtypeStruct((2, r, c), f32)] * 4,
        in_specs=[pl.BlockSpec((n_src, rb, c), lambda l, i: (0, jnp.where(l == 0, i, nb - 1), 0)),
                  pl.BlockSpec((n_src, rb, c), lambda l, i: (0, jnp.where(l == 1, i, 0), 0)), blk, blk, blk],
        out_specs=[blk] * 4,
        compiler_params=_params(("arbitrary", "arbitrary")),
    )(parts0, parts1, w, m, v)


def _reorder_in(w):
    pad = jnp.zeros(w.shape[:-1] + (P_IN - 2308,), w.dtype)
    return jnp.concatenate([w[..., :2048], w[..., 2052:2308], w[..., 2048:2052], pad], axis=-1)


def _unreorder_in(w):
    return jnp.concatenate([w[..., :2048], w[..., 2304:2308], w[..., 2048:2304]], axis=-1)


def _block_diag(w2d, n_blocks):
    rows, cols = w2d.shape
    tiled = jnp.tile(w2d, (1, n_blocks))
    rb = lax.broadcasted_iota(jnp.int32, tiled.shape, 0) // (rows // n_blocks)
    cb = lax.broadcasted_iota(jnp.int32, tiled.shape, 1) // cols
    return jnp.where(rb == cb, tiled, jnp.zeros_like(tiled))


def _block_diag_extract(w_bd, n_blocks):
    rows, wide = w_bd.shape
    r, c = rows // n_blocks, wide // n_blocks
    w4 = w_bd.reshape(n_blocks, r, n_blocks, c)
    idx = jnp.arange(n_blocks)
    return w4[idx, :, idx, :]


def _lanes128(v):
    return jnp.pad(v.reshape(1, -1), ((0, 0), (0, LANES - v.size)))


def _rows_of(shape):
    n = 1
    for d in shape:
        n *= d
    return -(-n // (8 * LANES)) * 8, n


def _flat_pack(arrs, row_multiple=8):
    blocks = []
    for a in arrs:
        rows, n = _rows_of(a.shape)
        blocks.append(jnp.pad(a.reshape(-1), (0, rows * LANES - n)).reshape(rows, LANES))
    total = sum(b.shape[0] for b in blocks)
    pad = -total % row_multiple
    if pad:
        blocks.append(jnp.zeros((pad, LANES), blocks[0].dtype))
    return jnp.concatenate(blocks, axis=0)


def _flat_unpack(packed, shapes):
    out, off = [], 0
    for s in shapes:
        rows, n = _rows_of(s)
        out.append(packed[off:off + rows].reshape(-1)[:n].reshape(s))
        off += rows
    return out


_W_NAMES = ['norm_mix_w', 'norm_mlp_w', 'ada_w', 'ada_b', 'w_in', 'pool_w', 'pool_scale', 'sconv_w', 'ssd_conv_w',
            'ssd_conv_b', 'ssd_dt_bias', 'ssd_a_log', 'ssd_d', 's5_a_re', 's5_a_im', 's5_log_step', 's5_b_re', 's5_b_im',
            's5_c_re', 's5_c_im', 's5_d', 's5_glu_w', 's5_glu_b', 'branch_norm_w', 'w_out', 'mlp_w1', 'mlp_w2',
            'final_norm_w']
_BIG = ('ada_w', 'w_in', 'w_out', 'mlp_w1', 'mlp_w2')
_SMALL = [n for n in _W_NAMES if n not in _BIG]
_SHARDED_SMALL = {'sconv_w': (2, 32), 'ssd_conv_w': (2, 96), 's5_glu_w': (1, 32)}


def _gather(*blocks):
    return _ChipGather(blocks)


def _scatter(*parts):
    return _Exchange(parts, gather=False)


def _layer_forward(l, h, p, w, sh_b, tb):
    first = l == 0
    (proj, u_b), got = _inproj_fwd(h, p['norm_mix_w'][l], p['sc1'][l], p['sh1'][l], w['w_in', l], tb,
                                   xchg=_gather(sh_b[1][0]) if first else None)
    if first:
        w['w_out', 0] = got[0].reshape(D_MODEL, D_MODEL)
    ya = _pool_fwd(proj, p['pool_bd'][l], p['pool_scale'][l], tb)
    yb = _sconv_fwd(proj, p['sconv_w8'][l], tb)
    (yc, yc_pre, states), got = _ssd_fwd(proj, p['ssd_conv_w8'][l], p['ssd_conv_b'][l], p['ssd_dt_bias'][l], p['ssd_a_log'][l],
                                         p['ssd_d_cols'][l], tb, xchg=_gather(sh_b[2][0]) if first else None)
    if first:
        w['w1', 0] = got[0]
    (yd, xr, xi), got = _s5_fwd(proj, p['bbr'][l], p['bbi'][l], p['ctr'][l], p['cti'][l], p['lre'][l], p['lim'][l],
                                p['s5_d'][l], p['glu_w'][l], p['glu_b'][l], tb, xchg=_gather(sh_b[3][0]) if first else None)
    if first:
        w['w2', 0] = got[0]
    ys = (ya, yb, yc, yd)
    h1, o, groups_b = _outproj_fwd(ys, h, p['branch_norm_w'][l], p['g1'][l], w['w_out', l], tb)
    (h2, m, v_b, r_b), got = _mlp_fwd(h1, p['norm_mlp_w'][l], p['sc2'][l], p['sh2'][l], p['g2'][l], w['w1', l], w['w2', l],
                                      min(MLP_TB, h.shape[0]), xchg=_gather(*[sh_b[k][1] for k in range(4)]) if first else None)
    if first:
        w['w_in', 1] = got[0].reshape(D_MODEL, P_IN)
        w['w_out', 1] = got[1].reshape(D_MODEL, D_MODEL)
        w['w1', 1], w['w2', 1] = got[2], got[3]
    saved = dict(h=h, proj=proj, u_b=u_b, ys=ys, yc_pre=yc_pre, states=states, xr=xr, xi=xi, h1=h1, o=o,
                 groups_b=groups_b, m=m, v_b=v_b, r_b=r_b)
    return h2, saved


def _layer_backward(l, dh2, s, p, w, pending, recv, tb):
    def carry(names):
        names = [n for n in names if n in pending]
        return names, (_scatter(*[pending.pop(n) for n in names]) if names else None)

    def landed(names, got):
        for n, g in zip(names, got):
            recv[n] = g

    names, xchg = carry([('w_out', 1), ('w_in', 1)])
    (dh1, do2_b, da_b, dg2, dsh2, dsc2, dnw_mlp), got = _mlp_bwd(dh2, s['m'], s['h1'], s['r_b'], p['norm_mlp_w'][l], p['sc2'][l],
                                                                p['sh2'][l], p['g2'][l], w['w1', l], w['w2', l], tb, xchg=xchg)
    landed(names, got)
    pending['mlp_w2', l] = _wgrad(s['r_b'], do2_b, 1, "wgrad_w2", tm=1024, tk=1024).reshape(N_DEV, MLP_HB, D_MODEL)
    pending['mlp_w1', l] = _wgrad(s['v_b'], da_b, N_DEV, "wgrad_w1", tm=1024, tk=2048)
    dya, dyb, dyc, dyd, do1_b, dg1, dbn = _outproj_bwd(dh1, s['o'], s['ys'], p['branch_norm_w'][l], p['g1'][l], w['w_out', l], tb)
    pending['w_out', l] = _wgrad(s['groups_b'], do1_b, 1, "wgrad_wout", tm=1024, tk=1024).reshape(N_DEV, D_MODEL // N_DEV, D_MODEL)
    proj = s['proj']
    dv, dpool_bd, dpool_scale = _pool_bwd(proj, dya, p['pool_bd'][l], p['pool_scale'][l], tb)
    dgb, dgc, dhh, dsconv = _sconv_bwd(proj, dyb, p['sconv_w8'][l], tb)
    names, xchg = carry([('mlp_w1', l)] + ([('w_out', 0)] if l == 0 else []))
    (dz, dxs, dbm, dcm, ddt, dconv_w, dconv_b, ddtb, dalog, ddskip), got = _ssd_bwd(
        proj, dyc, s['yc_pre'], s['states'], p['ssd_conv_w8'][l], p['ssd_conv_b'][l], p['ssd_dt_bias'][l], p['ssd_a_log'][l],
        p['ssd_d_cols'][l], tb, xchg=xchg)
    landed(names, got)
    names, xchg = carry([('mlp_w2', l)])
    (du5, dlr, dli, dbbr, dbbi, dctr, dcti, dd5, dgw, dgb5), got = _s5_bwd(
        proj, dyd, s['xr'], s['xi'], p['bbr'][l], p['bbi'][l], p['ctr'][l], p['cti'][l], p['lre'][l], p['lim'][l],
        p['s5_d'][l], p['glu_w'][l], p['glu_b'][l], tb, xchg=xchg)
    landed(names, got)
    dare, daim, dls, dbre_bd, dbim_bd = _s5_prep_bwd(p['are_c'][l], p['aim_c'][l], p['ls_c'][l], p['bre_bd'][l], p['bim_bd'][l],
                                                     dlr.reshape(S5_N, 1), dli.reshape(S5_N, 1), dbbr, dbbi)
    dparts = (dv, dgb, dgc, dhh, dz, dxs, dbm, dcm, du5, ddt)
    dh, dproj_b, dsh1, dsc1, dnw_mix = _inproj_bwd(dparts, dh1, s['h'], p['norm_mix_w'][l], p['sc1'][l], p['sh1'][l], w['w_in', l], tb)
    pending['w_in', l] = _wgrad(s['u_b'], dproj_b, 1, "wgrad_win", tm=512, tk=1024).reshape(N_DEV, D_MODEL // N_DEV, P_IN)
    small = {
        'norm_mix_w': dnw_mix.reshape(D_MODEL), 'norm_mlp_w': dnw_mlp.reshape(D_MODEL),
        'ada_b': jnp.concatenate([dsh1, dsc1, dg1, dsh2, dsc2, dg2], axis=1).reshape(6 * D_MODEL),
        'pool_w': _block_diag_extract(dpool_bd, 4), 'pool_scale': dpool_scale.reshape(GROUP_W),
        'sconv_w': dsconv[0:3], 'ssd_conv_w': dconv_w[0:4], 'ssd_conv_b': dconv_b.reshape(768),
        'ssd_dt_bias': ddtb[0, 0:4], 'ssd_a_log': dalog[0, 0:4], 'ssd_d': ddskip[0, 0:4],
        's5_a_re': dare.reshape(16, 64), 's5_a_im': daim.reshape(16, 64), 's5_log_step': dls[0:16, 0],
        's5_b_re': _block_diag_extract(dbre_bd, 16), 's5_b_im': _block_diag_extract(dbim_bd, 16),
        's5_c_re': _block_diag_extract(dctr, 16), 's5_c_im': _block_diag_extract(dcti, 16),
        's5_d': dd5.reshape(GROUP_W), 's5_glu_w': dgw, 's5_glu_b': dgb5.reshape(GROUP_W),
        'branch_norm_w': dbn.reshape(D_MODEL),
    }
    return dh, small


def _prepare_params(a, me):
    pack_shapes = [(1, D_MODEL), (2, 3, 32), (2, 4, 96), (2, 32, GROUP_W)]
    packed = _flat_pack([a['c'], a['sconv_w'], a['ssd_conv_w'], a['s5_glu_w']])
    gathered = _gather_rows(packed, "gather_small")
    pieces = [_flat_unpack(gathered[d], pack_shapes) for d in range(N_DEV)]
    c_all = jnp.concatenate([pc[0] for pc in pieces], axis=0)
    sconv_full = jnp.concatenate([pc[1] for pc in pieces], axis=2)
    ssd_conv_full = jnp.concatenate([pc[2] for pc in pieces], axis=2)
    glu_full = jnp.concatenate([pc[3] for pc in pieces], axis=1)

    ada_b_cols = lax.dynamic_slice_in_dim(a['ada_b'], me * 768, 768, axis=1).reshape(2, 1, 768)
    cond, modrows = _ada_forward(c_all, a['ada_w'], ada_b_cols)
    mod_recv = _all_to_all_rows(modrows.transpose(1, 0, 2), "exchange_mod")
    mod = mod_recv.transpose(1, 0, 2).reshape(2, 6 * D_MODEL)
    p = {'cond': cond}
    for k, name in enumerate(('sh1', 'sc1', 'g1', 'sh2', 'sc2', 'g2')):
        p[name] = mod[:, k * D_MODEL:(k + 1) * D_MODEL].reshape(2, 1, D_MODEL)

    for name in ('norm_mix_w', 'norm_mlp_w', 'branch_norm_w'):
        p[name] = a[name].reshape(2, 1, D_MODEL)
    p['pool_bd'] = jnp.stack([_block_diag(a['pool_w'][l].reshape(GROUP_W, 64), 4) for l in range(2)])
    p['pool_scale'] = a['pool_scale'].reshape(2, 1, GROUP_W)
    p['sconv_w8'] = jnp.pad(sconv_full, ((0, 0), (0, 5), (0, 0)))
    p['ssd_conv_w8'] = jnp.pad(ssd_conv_full, ((0, 0), (0, 4), (0, 0)))
    p['ssd_conv_b'] = a['ssd_conv_b'].reshape(2, 1, 768)
    p['ssd_dt_bias'] = jnp.pad(a['ssd_dt_bias'], ((0, 0), (0, LANES - 4))).reshape(2, 1, LANES)
    p['ssd_a_log'] = jnp.pad(a['ssd_a_log'], ((0, 0), (0, LANES - 4))).reshape(2, 1, LANES)
    p['ssd_d_cols'] = jnp.repeat(a['ssd_d'], SSD_P, axis=1).reshape(2, 1, GROUP_W)
    p['are_c'] = a['s5_a_re'].reshape(2, S5_N, 1)
    p['aim_c'] = a['s5_a_im'].reshape(2, S5_N, 1)
    p['ls_c'] = jnp.repeat(a['s5_log_step'], 64, axis=1).reshape(2, S5_N, 1)
    p['bre_bd'] = jnp.stack([_block_diag(a['s5_b_re'][l].reshape(S5_N, 16), 16) for l in range(2)])
    p['bim_bd'] = jnp.stack([_block_diag(a['s5_b_im'][l].reshape(S5_N, 16), 16) for l in range(2)])
    p['ctr'] = jnp.stack([_block_diag(a['s5_c_re'][l].reshape(GROUP_W, 64), 16) for l in range(2)])
    p['cti'] = jnp.stack([_block_diag(a['s5_c_im'][l].reshape(GROUP_W, 64), 16) for l in range(2)])
    p['s5_d'] = a['s5_d'].reshape(2, 1, GROUP_W)
    p['glu_w'] = glu_full
    p['glu_b'] = a['s5_glu_b'].reshape(2, 1, GROUP_W)
    lre, lim, bbr, bbi = [], [], [], []
    for l in range(2):
        r = _s5_prep(p['are_c'][l], p['aim_c'][l], p['ls_c'][l], p['bre_bd'][l], p['bim_bd'][l])
        lre.append(r[0].reshape(1, S5_N))
        lim.append(r[1].reshape(1, S5_N))
        bbr.append(r[2])
        bbi.append(r[3])
    p['lre'], p['lim'], p['bbr'], p['bbi'] = lre, lim, bbr, bbi
    return p


def kernel(x, c, norm_mix_w, norm_mlp_w, ada_w, ada_b, w_in, pool_w, pool_scale, sconv_w, ssd_conv_w, ssd_conv_b, ssd_dt_bias, ssd_a_log, ssd_d, s5_a_re, s5_a_im, s5_log_step, s5_b_re, s5_b_im, s5_c_re, s5_c_im, s5_d, s5_glu_w, s5_glu_b, branch_norm_w, w_out, mlp_w1, mlp_w2, final_norm_w, loss_target, m_norm_mix_w, m_norm_mlp_w, m_ada_w, m_ada_b, m_w_in, m_pool_w, m_pool_scale, m_sconv_w, m_ssd_conv_w, m_ssd_conv_b, m_ssd_dt_bias, m_ssd_a_log, m_ssd_d, m_s5_a_re, m_s5_a_im, m_s5_log_step, m_s5_b_re, m_s5_b_im, m_s5_c_re, m_s5_c_im, m_s5_d, m_s5_glu_w, m_s5_glu_b, m_branch_norm_w, m_w_out, m_mlp_w1, m_mlp_w2, m_final_norm_w, v_norm_mix_w, v_norm_mlp_w, v_ada_w, v_ada_b, v_w_in, v_pool_w, v_pool_scale, v_sconv_w, v_ssd_conv_w, v_ssd_conv_b, v_ssd_dt_bias, v_ssd_a_log, v_ssd_d, v_s5_a_re, v_s5_a_im, v_s5_log_step, v_s5_b_re, v_s5_b_im, v_s5_c_re, v_s5_c_im, v_s5_d, v_s5_glu_w, v_s5_glu_b, v_branch_norm_w, v_w_out, v_mlp_w1, v_mlp_w2, v_final_norm_w):
    a = dict(locals())
    t = x.shape[1]
    tb = min(512, t)
    me = _my_index()
    p = _prepare_params(a, me)

    sh_b = _cast_shards([_reorder_in(w_in), w_out, mlp_w1, mlp_w2])
    w = {('w_in', 0): _exchange_alone(_gather(sh_b[0][0]), "gather_w_in0")[0].reshape(D_MODEL, P_IN)}

    h = x.reshape(t, D_MODEL)
    saved = []
    for l in range(2):
        h, s = _layer_forward(l, h, p, w, sh_b, tb)
        saved.append(s)
    loss_blk, dh, dfinal = _loss_head(h, loss_target.reshape(t, D_MODEL), final_norm_w.reshape(1, D_MODEL), tb)
    loss = lax.psum(loss_blk[0, 0], ("x", "y", "c"))

    pending, recv, small_parts = {}, {}, [None, None]
    for l in (1, 0):
        dh, small_parts[l] = _layer_backward(l, dh, saved[l], p, w, pending, recv, tb)
    grad_x = dh.reshape(1, t, D_MODEL)
    recv['w_in', 0] = _exchange_alone(_scatter(pending.pop(('w_in', 0))), "exchange_w_in0")[0]

    grads, deltas, new_m, new_v = {}, {}, {}, {}

    wmv_in = [_reorder_in(a[n]) for n in ('w_in', 'm_w_in', 'v_w_in')]
    outs = _sum_adamw_layers(recv['w_in', 0], recv['w_in', 1], *wmv_in, "adamw_w_in", 128)
    grads['w_in'], deltas['w_in'], new_m['w_in'], new_v['w_in'] = [_unreorder_in(o) for o in outs]
    for name, rb in (('w_out', 128), ('mlp_w1', 256), ('mlp_w2', 256)):
        grads[name], deltas[name], new_m[name], new_v[name] = _sum_adamw_layers(
            recv[name, 0], recv[name, 1], a[name], a['m_' + name], a['v_' + name], "adamw_" + name, rb)

    dmod = jnp.stack([small_parts[0]['ada_b'], small_parts[1]['ada_b']])
    dmod_recv = _all_to_all_rows(dmod.reshape(2, N_DEV, 768).transpose(1, 0, 2), "exchange_dmod")
    g_ada = _ada_backward(p['cond'], dmod_recv.transpose(1, 0, 2))
    grads['ada_w'], deltas['ada_w'], new_m['ada_w'], new_v['ada_w'] = _sum_adamw_layers(
        g_ada[0:1], g_ada[1:2], ada_w, m_ada_w, v_ada_w, "adamw_ada_w", 256)

    layered = [n for n in _SMALL if n != 'final_norm_w']
    full = [jnp.stack([small_parts[0][n], small_parts[1][n]]) for n in layered] + [dfinal.reshape(D_MODEL)]
    full_shapes = [f.shape for f in full]
    summed = _flat_unpack(_allreduce_rows(_flat_pack(full, row_multiple=64)), full_shapes)
    local = []
    for n, g in zip(_SMALL, summed):
        if n in _SHARDED_SMALL:
            axis, size = _SHARDED_SMALL[n]
            g = lax.dynamic_slice_in_dim(g, me * size, size, axis=axis)
        local.append(g.reshape(a[n].shape))
    local_shapes = [g.shape for g in local]
    packed = [_flat_pack(xs) for xs in (local, [a[n] for n in _SMALL], [a['m_' + n] for n in _SMALL], [a['v_' + n] for n in _SMALL])]
    outs = _sum_adamw(packed[0][None], packed[1], packed[2], packed[3], "adamw_small", packed[0].shape[0])
    for store, o in zip((grads, deltas, new_m, new_v), outs):
        for n, val in zip(_SMALL, _flat_unpack(o, local_shapes)):
            store[n] = val

    return (loss, grad_x, *[grads[n] for n in _W_NAMES], *[deltas[n] for n in _W_NAMES],
            *[new_m[n] for n in _W_NAMES], *[new_v[n] for n in _W_NAMES])
```

```python
import functools

import jax
import jax.numpy as jnp
from jax import lax
from jax.experimental import pallas as pl
from jax.experimental.pallas import tpu as pltpu

f32 = jnp.float32
bf16 = jnp.bfloat16

N_DEV = 8
D_MODEL = 1024
GROUP_W = 256
P_IN = 2432
DT_COL = 2304
SSD_CHUNK = 128
SSD_HEADS = 4
SSD_P = 64
S5_N = 1024
MLP_HB = 512
MLP_TB = 1024
MLP_SLABS = 2
EPS = 1e-6
LANES = 128
VMEM_LIMIT = 56 * 1024 * 1024
ADAM_LR, ADAM_B1, ADAM_B2, ADAM_EPS, ADAM_WD, ADAM_STEP = 0.001, 0.9, 0.999, 1e-08, 0.01, 10
POOL_WINDOWS = (2, 4, 8, 16)

C_POOL, C_GB, C_GC, C_HH, C_Z, C_XS, C_BM, C_CM, C_S5 = range(9)
C_DT128 = DT_COL // LANES

MESH = pl.DeviceIdType.MESH
ANY = pl.BlockSpec(memory_space=pl.ANY)
VMEM = pl.BlockSpec(memory_space=pltpu.VMEM)


def _dot(a, b):
    return jnp.dot(a, b, preferred_element_type=f32)


def _dot_nt(a, b):
    return lax.dot_general(a, b, (((1,), (1,)), ((), ())), preferred_element_type=f32)


def _dot_tn(a, b):
    return lax.dot_general(a, b, (((0,), (0,)), ((), ())), preferred_element_type=f32)


def _dot_exact(a, b):
    return jnp.dot(a, b, preferred_element_type=f32, precision=lax.Precision.HIGHEST)


def _b(x):
    return x.astype(bf16)


def _silu(x):
    return x * jax.nn.sigmoid(x)


def _dsilu(x):
    s = jax.nn.sigmoid(x)
    return s * (1.0 + x * (1.0 - s))


def _softplus(x):
    return jnp.maximum(x, 0.0) + jnp.log1p(jnp.exp(-jnp.abs(x)))


_GELU_K = 0.7978845608028654
_GELU_C = 0.044715


def _gelu(x):
    return 0.5 * x * (1.0 + jnp.tanh(_GELU_K * (x + _GELU_C * x * x * x)))


def _dgelu(x):
    th = jnp.tanh(_GELU_K * (x + _GELU_C * x * x * x))
    return 0.5 * (1.0 + th) + 0.5 * x * (1.0 - th * th) * _GELU_K * (1.0 + 3.0 * _GELU_C * x * x)


def _rms(h):
    r = lax.rsqrt(jnp.mean(h * h, axis=-1, keepdims=True) + EPS)
    return h * r, r


def _rms_bwd(dn, n, r):
    return r * (dn - n * jnp.mean(dn * n, axis=-1, keepdims=True))


def _colsum(x):
    return jnp.sum(x, axis=0, keepdims=True)


def _params(sem=None):
    return pltpu.CompilerParams(dimension_semantics=sem, vmem_limit_bytes=VMEM_LIMIT)


def _full(shape):
    return pl.BlockSpec(shape, lambda *_: (0,) * len(shape))


def _acc(ref, val):
    @pl.when(pl.program_id(0) == 0)
    def _():
        ref[...] = val

    @pl.when(pl.program_id(0) != 0)
    def _():
        ref[...] += val


def _me():
    return lax.axis_index("x"), lax.axis_index("y"), lax.axis_index("c")


def _my_index():
    x, y, c = _me()
    return 4 * x + 2 * y + c


def _coords(p):
    return (p // 4, (p // 2) % 2, p % 2)


class _Exchange:
    def __init__(self, srcs, gather):
        self.srcs = list(srcs)
        self.gather = gather
        self.n = len(self.srcs)
        self.out_shape = [jax.ShapeDtypeStruct(((N_DEV,) + s.shape) if gather else s.shape, s.dtype) for s in self.srcs]
        self.scratch = [pltpu.SemaphoreType.DMA((self.n, N_DEV)), pltpu.SemaphoreType.DMA((self.n, N_DEV)),
                        pltpu.SemaphoreType.DMA((self.n,))]

    def _src(self, refs, t, dev):
        return refs[t] if self.gather else refs[t].at[dev]

    def _remote(self, xin, xout, sems, t, k, me, to):
        return pltpu.make_async_remote_copy(
            src_ref=self._src(xin, t, to), dst_ref=xout[t].at[me], send_sem=sems[0].at[t, k], recv_sem=sems[1].at[t, k],
            device_id=_coords(to), device_id_type=MESH)

    def start(self, xin, xout, sems):
        me = _my_index()
        for t in range(self.n):
            pltpu.make_async_copy(self._src(xin, t, me), xout[t].at[me], sems[2].at[t]).start()
            for k in range(1, N_DEV):
                self._remote(xin, xout, sems, t, k, me, (me + k) % N_DEV).start()

    def wait(self, xin, xout, sems):
        me = _my_index()
        for t in range(self.n):
            for k in range(1, N_DEV):
                src = (me + N_DEV - k) % N_DEV
                pltpu.make_async_remote_copy(
                    src_ref=self._src(xin, t, src), dst_ref=xout[t].at[src], send_sem=sems[0].at[t, k],
                    recv_sem=sems[1].at[t, k], device_id=_coords(src), device_id_type=MESH).wait_recv()
        for t in range(self.n):
            for k in range(1, N_DEV):
                self._remote(xin, xout, sems, t, k, me, (me + k) % N_DEV).wait_send()
            pltpu.make_async_copy(self._src(xin, t, me), xout[t].at[me], sems[2].at[t]).wait()

    def forward(self, xin, xout, sems):
        pass


class _ChipGather:
    def __init__(self, srcs):
        self.srcs = list(srcs)
        self.n = len(self.srcs)
        self.out_shape = [jax.ShapeDtypeStruct((N_DEV,) + s.shape, s.dtype) for s in self.srcs]
        self.scratch = [pltpu.SemaphoreType.DMA((self.n, 7)), pltpu.SemaphoreType.DMA((self.n, 7)),
                        pltpu.SemaphoreType.DMA((self.n,))]

    @staticmethod
    def _places():
        x, y, c = _me()
        chips = [(1 - x, y), (x, 1 - y), (1 - x, 1 - y)]
        return (x, y, c), (x, y, 1 - c), chips

    @staticmethod
    def _slab(ref, dev):
        return ref.at[4 * dev[0] + 2 * dev[1] + dev[2]]

    def _copy(self, xin, xout, sems, t, k, block, to, src=None):
        return pltpu.make_async_remote_copy(
            src_ref=self._slab(xout[t], block) if src is None else src, dst_ref=self._slab(xout[t], block),
            send_sem=sems[0].at[t, k], recv_sem=sems[1].at[t, k], device_id=to, device_id_type=MESH)

    def start(self, xin, xout, sems):
        me, sibling, chips = self._places()
        for t in range(self.n):
            pltpu.make_async_copy(xin[t], self._slab(xout[t], me), sems[2].at[t]).start()
            self._copy(xin, xout, sems, t, 0, me, sibling, src=xin[t]).start()
            for j, chip in enumerate(chips):
                self._copy(xin, xout, sems, t, 1 + j, me, (*chip, me[2]), src=xin[t]).start()

    def forward(self, xin, xout, sems):
        me, sibling, chips = self._places()
        for t in range(self.n):
            for j, chip in enumerate(chips):
                self._copy(xin, xout, sems, t, 1 + j, (*chip, me[2]), me).wait_recv()
                self._copy(xin, xout, sems, t, 4 + j, (*chip, me[2]), sibling).start()

    def wait(self, xin, xout, sems):
        me, sibling, chips = self._places()
        for t in range(self.n):
            self._copy(xin, xout, sems, t, 0, sibling, me).wait_recv()
            for j, chip in enumerate(chips):
                self._copy(xin, xout, sems, t, 4 + j, (*chip, 1 - me[2]), me).wait_recv()
        for t in range(self.n):
            self._copy(xin, xout, sems, t, 0, me, sibling, src=xin[t]).wait_send()
            for j, chip in enumerate(chips):
                self._copy(xin, xout, sems, t, 1 + j, me, (*chip, me[2]), src=xin[t]).wait_send()
                self._copy(xin, xout, sems, t, 4 + j, (*chip, me[2]), sibling).wait_send()
            pltpu.make_async_copy(xin[t], self._slab(xout[t], me), sems[2].at[t]).wait()


def _call(body, *, name, grid, in_specs, out_specs, out_shape, args, semantics, scratch_shapes=(), xchg=None):
    if xchg is None:
        outs = pl.pallas_call(body, name=name, grid=grid, in_specs=in_specs, out_specs=out_specs, out_shape=out_shape,
                              scratch_shapes=list(scratch_shapes), compiler_params=_params(semantics))(*args)
        return outs, ()
    n_in, n_out, n_scr, n = len(in_specs), len(out_specs), len(scratch_shapes), xchg.n

    def carried(*refs):
        ins, xin = refs[:n_in], refs[n_in:n_in + n]
        outs, xout = refs[n_in + n:n_in + n + n_out], refs[n_in + n + n_out:n_in + 2 * n + n_out]
        scr, sems = refs[n_in + 2 * n + n_out:n_in + 2 * n + n_out + n_scr], refs[n_in + 2 * n + n_out + n_scr:]
        step = pl.program_id(0)
        for d in range(1, len(grid)):
            step = step * grid[d] + pl.program_id(d)
        n_steps = functools.reduce(lambda a, b: a * b, grid)

        @pl.when(step == 0)
        def _():
            xchg.start(xin, xout, sems)

        @pl.when(step == (2 * n_steps) // 3)
        def _():
            xchg.forward(xin, xout, sems)

        body(*ins, *outs, *scr)

        @pl.when(step == n_steps - 1)
        def _():
            xchg.wait(xin, xout, sems)

    res = pl.pallas_call(
        carried, name=name, grid=grid, in_specs=list(in_specs) + [ANY] * n, out_specs=list(out_specs) + [ANY] * n,
        out_shape=list(out_shape) + xchg.out_shape, scratch_shapes=list(scratch_shapes) + xchg.scratch,
        compiler_params=_params(("arbitrary",) * len(grid)))(*args, *xchg.srcs)
    return res[:n_out], tuple(res[n_out:])


def _exchange_alone(xchg, name):
    def body(*refs):
        xin, xout, sems = refs[:xchg.n], refs[xchg.n:2 * xchg.n], refs[2 * xchg.n:]
        xchg.start(xin, xout, sems)
        xchg.forward(xin, xout, sems)
        xchg.wait(xin, xout, sems)

    return pl.pallas_call(body, name=name, out_shape=xchg.out_shape, in_specs=[ANY] * xchg.n, out_specs=[ANY] * xchg.n,
                          scratch_shapes=xchg.scratch)(*xchg.srcs)


def _cast_shards(shards):
    n = len(shards)

    def body(*refs):
        for i, o in zip(refs[:n], refs[n:]):
            o[...] = i[...].astype(bf16)

    return pl.pallas_call(body, name="cast_shards", out_shape=[jax.ShapeDtypeStruct(s.shape, bf16) for s in shards],
                          in_specs=[VMEM] * n, out_specs=[VMEM] * n, compiler_params=_params())(*shards)


def _allreduce_rows(v):
    r = v.shape[0]
    rp = r // N_DEV

    def body(v_ref, o_ref, parts, sums, send1, recv1, send2, recv2):
        me = _my_index()

        def piece(ref, d):
            return ref.at[pl.ds(pl.multiple_of(d * rp, 8), rp), :]

        def copy1(k, src_dev, to):
            return pltpu.make_async_remote_copy(src_ref=piece(v_ref, to), dst_ref=parts.at[src_dev], send_sem=send1.at[k],
                                                recv_sem=recv1.at[k], device_id=_coords(to), device_id_type=MESH)

        def copy2(k, owner, to):
            return pltpu.make_async_remote_copy(src_ref=sums, dst_ref=piece(o_ref, owner), send_sem=send2.at[k],
                                                recv_sem=recv2.at[k], device_id=_coords(to), device_id_type=MESH)

        for k in range(1, N_DEV):
            copy1(k, me, (me + k) % N_DEV).start()
        parts[me] = v_ref[pl.ds(pl.multiple_of(me * rp, 8), rp), :]
        for k in range(1, N_DEV):
            copy1(k, (me + N_DEV - k) % N_DEV, me).wait_recv()
        total = parts[0]
        for s in range(1, N_DEV):
            total = total + parts[s]
        sums[...] = total
        o_ref[pl.ds(pl.multiple_of(me * rp, 8), rp), :] = total
        for k in range(1, N_DEV):
            copy2(k, me, (me + k) % N_DEV).start()
        for k in range(1, N_DEV):
            copy2(k, (me + N_DEV - k) % N_DEV, me).wait_recv()
        for k in range(1, N_DEV):
            copy1(k, me, (me + k) % N_DEV).wait_send()
            copy2(k, me, (me + k) % N_DEV).wait_send()

    return pl.pallas_call(
        body, name="allreduce_small_grads", out_shape=jax.ShapeDtypeStruct(v.shape, v.dtype),
        in_specs=[VMEM], out_specs=VMEM,
        scratch_shapes=[pltpu.VMEM((N_DEV, rp, LANES), f32), pltpu.VMEM((rp, LANES), f32)]
        + [pltpu.SemaphoreType.DMA((N_DEV,))] * 4,
        compiler_params=_params(),
    )(v)


def _gather_rows(v, name):
    def body(v_ref, o_ref, send_sems, recv_sems):
        me = _my_index()
        o_ref[me] = v_ref[...]
        sends = []
        for k in range(1, N_DEV):
            peer = (me + k) % N_DEV
            rc = pltpu.make_async_remote_copy(src_ref=v_ref, dst_ref=o_ref.at[me], send_sem=send_sems.at[k],
                                              recv_sem=recv_sems.at[k], device_id=_coords(peer), device_id_type=MESH)
            rc.start()
            sends.append(rc)
        for k in range(1, N_DEV):
            src = (me + N_DEV - k) % N_DEV
            pltpu.make_async_remote_copy(src_ref=v_ref, dst_ref=o_ref.at[src], send_sem=send_sems.at[k],
                                         recv_sem=recv_sems.at[k], device_id=_coords(src), device_id_type=MESH).wait_recv()
        for rc in sends:
            rc.wait_send()

    return pl.pallas_call(
        body, name=name, out_shape=jax.ShapeDtypeStruct((N_DEV,) + v.shape, v.dtype),
        in_specs=[VMEM], out_specs=VMEM,
        scratch_shapes=[pltpu.SemaphoreType.DMA((N_DEV,)), pltpu.SemaphoreType.DMA((N_DEV,))],
        compiler_params=pltpu.CompilerParams(vmem_limit_bytes=VMEM_LIMIT),
    )(v)


def _all_to_all_rows(v, name):
    def body(v_ref, o_ref, send_sems, recv_sems):
        me = _my_index()
        o_ref[me] = v_ref[me]
        sends = []
        for k in range(1, N_DEV):
            peer = (me + k) % N_DEV
            rc = pltpu.make_async_remote_copy(src_ref=v_ref.at[peer], dst_ref=o_ref.at[me], send_sem=send_sems.at[k],
                                              recv_sem=recv_sems.at[k], device_id=_coords(peer), device_id_type=MESH)
            rc.start()
            sends.append(rc)
        for k in range(1, N_DEV):
            src = (me + N_DEV - k) % N_DEV
            pltpu.make_async_remote_copy(src_ref=v_ref.at[src], dst_ref=o_ref.at[src], send_sem=send_sems.at[k],
                                         recv_sem=recv_sems.at[k], device_id=_coords(src), device_id_type=MESH).wait_recv()
        for rc in sends:
            rc.wait_send()

    return pl.pallas_call(
        body, name=name, out_shape=jax.ShapeDtypeStruct(v.shape, v.dtype),
        in_specs=[VMEM], out_specs=VMEM,
        scratch_shapes=[pltpu.SemaphoreType.DMA((N_DEV,)), pltpu.SemaphoreType.DMA((N_DEV,))],
    )(v)


def _ada_forward(c_all, ada_w, ada_b_cols):
    def body(c_ref, w_ref, b_ref, cond_ref, o_ref):
        cond = _silu(c_ref[...])
        cond_ref[...] = cond
        for l in range(2):
            o_ref[l] = _dot(_b(cond), _b(w_ref[l])) + b_ref[l]

    return pl.pallas_call(
        body, name="ada_forward",
        out_shape=[jax.ShapeDtypeStruct((N_DEV, D_MODEL), f32), jax.ShapeDtypeStruct((2, N_DEV, 768), f32)],
        in_specs=[VMEM] * 3, out_specs=[VMEM] * 2, compiler_params=_params(),
    )(c_all, ada_w, ada_b_cols)


def _ada_backward(cond, dmod_rows):
    def body(c_ref, d_ref, o_ref):
        cb = _b(c_ref[...])
        for l in range(2):
            o_ref[l] = _dot_tn(cb, _b(d_ref[l]))

    return pl.pallas_call(
        body, name="ada_backward", out_shape=jax.ShapeDtypeStruct((2, D_MODEL, 768), f32),
        in_specs=[VMEM] * 2, out_specs=VMEM, compiler_params=_params(),
    )(cond, dmod_rows)


def _inproj_fwd(h, norm_w, sc, sh, w_in, tb, xchg=None):
    t = h.shape[0]

    def body(h_ref, nw_ref, sc_ref, sh_ref, w_ref, proj_ref, u_ref):
        n, _ = _rms(h_ref[...])
        u = _b(n * nw_ref[...] * (1.0 + sc_ref[...]) + sh_ref[...])
        u_ref[...] = u
        proj_ref[...] = _dot(u, w_ref[...])

    row = pl.BlockSpec((tb, D_MODEL), lambda i: (i, 0))
    vec = _full((1, D_MODEL))
    return _call(
        body, name="inproj_fwd", grid=(t // tb,),
        out_shape=[jax.ShapeDtypeStruct((t, P_IN), f32), jax.ShapeDtypeStruct((t, D_MODEL), bf16)],
        in_specs=[row, vec, vec, vec, _full((D_MODEL, P_IN))],
        out_specs=[pl.BlockSpec((tb, P_IN), lambda i: (i, 0)), row],
        semantics=("parallel",), args=(h, norm_w, sc, sh, w_in), xchg=xchg)


def _inproj_bwd(dparts, dh_res, h, norm_w, sc, sh, w_in, tb):
    t = h.shape[0]

    def body(*refs):
        parts = refs[:10]
        dres_ref, h_ref, nw_ref, sc_ref, sh_ref, w_ref = refs[10:16]
        dh_ref, dproj_ref, dsh_ref, dsc_ref, dnw_ref = refs[16:]
        dproj = _b(jnp.concatenate([p[...] for p in parts], axis=1))
        dproj_ref[...] = dproj
        du = _dot_nt(dproj, w_ref[...])
        n, r = _rms(h_ref[...])
        nw = nw_ref[...]
        gain = 1.0 + sc_ref[...]
        _acc(dsh_ref, _colsum(du))
        _acc(dsc_ref, _colsum(du * n * nw))
        _acc(dnw_ref, _colsum(du * gain * n))
        dh_ref[...] = dres_ref[...] + _rms_bwd(du * nw * gain, n, r)

    row = pl.BlockSpec((tb, D_MODEL), lambda i: (i, 0))
    vec = _full((1, D_MODEL))
    part_specs = [pl.BlockSpec((tb, GROUP_W), lambda i: (i, 0))] * 9 + [pl.BlockSpec((tb, LANES), lambda i: (i, 0))]
    return pl.pallas_call(
        body, name="inproj_bwd", grid=(t // tb,),
        out_shape=[jax.ShapeDtypeStruct((t, D_MODEL), f32), jax.ShapeDtypeStruct((t, P_IN), bf16)]
        + [jax.ShapeDtypeStruct((1, D_MODEL), f32)] * 3,
        in_specs=part_specs + [row, row, vec, vec, vec, _full((D_MODEL, P_IN))],
        out_specs=[row, pl.BlockSpec((tb, P_IN), lambda i: (i, 0)), vec, vec, vec],
        compiler_params=_params(("arbitrary",)),
    )(*dparts, dh_res, h, norm_w, sc, sh, w_in)


def _wgrad(a, b, n_blocks, name, tm, tk=512):
    t, m = a.shape
    nb = b.shape[1] // n_blocks
    tk = min(tk, t)
    nk = t // tk

    def body(a_ref, b_ref, o_ref, acc_ref):
        k = pl.program_id(2)
        p = _dot_tn(a_ref[...], b_ref[...])

        @pl.when(k == 0)
        def _():
            acc_ref[...] = p

        @pl.when(k != 0)
        def _():
            acc_ref[...] += p

        @pl.when(k == nk - 1)
        def _():
            o_ref[0] = acc_ref[...].astype(o_ref.dtype)

    return pl.pallas_call(
        body, name=name, grid=(m // tm, n_blocks, nk),
        out_shape=jax.ShapeDtypeStruct((n_blocks, m, nb), bf16),
        in_specs=[pl.BlockSpec((tk, tm), lambda i, j, k: (k, i)), pl.BlockSpec((tk, nb), lambda i, j, k: (k, j))],
        out_specs=pl.BlockSpec((1, tm, nb), lambda i, j, k: (j, i, 0)),
        scratch_shapes=[pltpu.VMEM((tm, nb), f32)],
        compiler_params=_params(("parallel", "parallel", "arbitrary")),
    )(a, b)


def _pool_counts(rows, t0):
    tpos = (lax.broadcasted_iota(jnp.int32, (rows, GROUP_W), 0) + t0 + 1).astype(f32)
    grp = lax.broadcasted_iota(jnp.int32, (rows, GROUP_W), 1) // 64
    win = jnp.where(grp == 0, 2.0, jnp.where(grp == 1, 4.0, jnp.where(grp == 2, 8.0, 16.0)))
    return jnp.minimum(tpos, win), grp


def _pool_select(grp, l1, l2, l3, l4):
    return jnp.where(grp == 0, l1, jnp.where(grp == 1, l2, jnp.where(grp == 2, l3, l4)))


def _pool_means(v, halo, t0):
    tb = v.shape[0]
    ext = jnp.concatenate([halo, v], axis=0)
    n = tb + 16
    s1 = ext[1:n] + ext[0:n - 1]
    s2 = s1[2:n - 1] + s1[0:n - 3]
    s3 = s2[4:n - 3] + s2[0:n - 7]
    s4 = s3[8:n - 7] + s3[0:n - 15]
    cnt, grp = _pool_counts(tb, t0)
    wsum = _pool_select(grp, s1[15:15 + tb], s2[13:13 + tb], s3[9:9 + tb], s4[1:1 + tb])
    return wsum / cnt - v


def _pool_fwd(proj, pw_bd, scale, tb):
    t = proj.shape[0]

    def body(v_ref, vh_ref, pw_ref, sc_ref, o_ref):
        i = pl.program_id(0)
        halo = jnp.where(i > 0, vh_ref[...], 0.0)
        p = _pool_means(v_ref[...], halo, i * tb)
        o_ref[...] = _dot(_b(p), _b(pw_ref[...])) * sc_ref[...]

    return pl.pallas_call(
        body, name="pool_fwd", grid=(t // tb,),
        out_shape=jax.ShapeDtypeStruct((t, GROUP_W), f32),
        in_specs=[pl.BlockSpec((tb, GROUP_W), lambda i: (i, C_POOL)),
                  pl.BlockSpec((16, GROUP_W), lambda i: (jnp.maximum(i * (tb // 16) - 1, 0), C_POOL)),
                  _full((GROUP_W, GROUP_W)), _full((1, GROUP_W))],
        out_specs=pl.BlockSpec((tb, GROUP_W), lambda i: (i, 0)),
        compiler_params=_params(("parallel",)),
    )(proj, proj, pw_bd, scale)


def _pool_bwd(proj, dy, pw_bd, scale, tb):
    t = proj.shape[0]
    nt = t // tb
    last16 = t // 16 - 1

    def body(v_ref, vh_ref, dy_ref, dyh_ref, pw_ref, sc_ref, dv_ref, dpw_ref, dsc_ref):
        i = pl.program_id(0)
        halo = jnp.where(i > 0, vh_ref[...], 0.0)
        p = _pool_means(v_ref[...], halo, i * tb)
        pw = _b(pw_ref[...])
        sc = sc_ref[...]
        dy = dy_ref[...]
        ypre = _dot(_b(p), pw)
        _acc(dsc_ref, _colsum(dy * ypre))
        dys = _b(dy * sc)
        _acc(dpw_ref, _dot_tn(_b(p), dys))
        dp = _dot_nt(dys, pw)
        dph = _dot_nt(_b(jnp.where(i < nt - 1, dyh_ref[...], 0.0) * sc), pw)
        cnt, grp = _pool_counts(tb, i * tb)
        cnth, _ = _pool_counts(16, (i + 1) * tb)
        ext = jnp.concatenate([dp / cnt, dph / cnth], axis=0)
        n = tb + 16
        f1 = ext[0:n - 1] + ext[1:n]
        f2 = f1[0:n - 3] + f1[2:n - 1]
        f3 = f2[0:n - 7] + f2[4:n - 3]
        f4 = f3[0:n - 15] + f3[8:n - 7]
        dv_ref[...] = _pool_select(grp, f1[0:tb], f2[0:tb], f3[0:tb], f4[0:tb]) - dp

    return pl.pallas_call(
        body, name="pool_bwd", grid=(nt,),
        out_shape=[jax.ShapeDtypeStruct((t, GROUP_W), f32), jax.ShapeDtypeStruct((GROUP_W, GROUP_W), f32),
                   jax.ShapeDtypeStruct((1, GROUP_W), f32)],
        in_specs=[pl.BlockSpec((tb, GROUP_W), lambda i: (i, C_POOL)),
                  pl.BlockSpec((16, GROUP_W), lambda i: (jnp.maximum(i * (tb // 16) - 1, 0), C_POOL)),
                  pl.BlockSpec((tb, GROUP_W), lambda i: (i, 0)),
                  pl.BlockSpec((16, GROUP_W), lambda i: (jnp.minimum((i + 1) * (tb // 16), last16), 0)),
                  _full((GROUP_W, GROUP_W)), _full((1, GROUP_W))],
        out_specs=[pl.BlockSpec((tb, GROUP_W), lambda i: (i, 0)), _full((GROUP_W, GROUP_W)), _full((1, GROUP_W))],
        compiler_params=_params(("arbitrary",)),
    )(proj, proj, dy, dy, pw_bd, scale)


def _sconv_fwd(proj, w, tb):
    t = proj.shape[0]

    def body(gb_ref, gc_ref, hh_ref, gch_ref, hhh_ref, w_ref, o_ref):
        i = pl.program_id(0)
        q = gc_ref[...] * hh_ref[...]
        qh = jnp.where(i > 0, gch_ref[...] * hhh_ref[...], 0.0)
        ext = jnp.concatenate([qh, q], axis=0)
        w = w_ref[...]
        conv = w[0:1] * ext[6:6 + tb] + w[1:2] * ext[7:7 + tb] + w[2:3] * ext[8:8 + tb]
        o_ref[...] = gb_ref[...] * conv

    def col(c):
        return pl.BlockSpec((tb, GROUP_W), lambda i: (i, c))

    def prev(c):
        return pl.BlockSpec((8, GROUP_W), lambda i: (jnp.maximum(i * (tb // 8) - 1, 0), c))

    return pl.pallas_call(
        body, name="sconv_fwd", grid=(t // tb,),
        out_shape=jax.ShapeDtypeStruct((t, GROUP_W), f32),
        in_specs=[col(C_GB), col(C_GC), col(C_HH), prev(C_GC), prev(C_HH), _full((8, GROUP_W))],
        out_specs=pl.BlockSpec((tb, GROUP_W), lambda i: (i, 0)),
        compiler_params=_params(("parallel",)),
    )(proj, proj, proj, proj, proj, w)


def _sconv_bwd(proj, dy, w, tb):
    t = proj.shape[0]
    nt = t // tb
    last8 = t // 8 - 1

    def body(gb_ref, gc_ref, hh_ref, gch_ref, hhh_ref, gbn_ref, dy_ref, dyn_ref, w_ref, dgb_ref, dgc_ref, dhh_ref, dw_ref):
        i = pl.program_id(0)
        gc, hh, gb, dy = gc_ref[...], hh_ref[...], gb_ref[...], dy_ref[...]
        q = gc * hh
        qh = jnp.where(i > 0, gch_ref[...] * hhh_ref[...], 0.0)
        ext = jnp.concatenate([qh, q], axis=0)
        w = w_ref[...]
        conv = w[0:1] * ext[6:6 + tb] + w[1:2] * ext[7:7 + tb] + w[2:3] * ext[8:8 + tb]
        dgb_ref[...] = dy * conv
        e = dy * gb
        en = jnp.where(i < nt - 1, dyn_ref[...] * gbn_ref[...], 0.0)
        exte = jnp.concatenate([e, en], axis=0)
        dq = w[2:3] * exte[0:tb] + w[1:2] * exte[1:1 + tb] + w[0:1] * exte[2:2 + tb]
        dgc_ref[...] = dq * hh
        dhh_ref[...] = dq * gc
        dw = jnp.concatenate([_colsum(e * ext[6:6 + tb]), _colsum(e * ext[7:7 + tb]), _colsum(e * ext[8:8 + tb]),
                              jnp.zeros((5, GROUP_W), f32)], axis=0)
        _acc(dw_ref, dw)

    def col(c):
        return pl.BlockSpec((tb, GROUP_W), lambda i: (i, c))

    def prev(c):
        return pl.BlockSpec((8, GROUP_W), lambda i: (jnp.maximum(i * (tb // 8) - 1, 0), c))

    def nxt(c):
        return pl.BlockSpec((8, GROUP_W), lambda i: (jnp.minimum((i + 1) * (tb // 8), last8), c))

    out = pl.BlockSpec((tb, GROUP_W), lambda i: (i, 0))
    return pl.pallas_call(
        body, name="sconv_bwd", grid=(nt,),
        out_shape=[jax.ShapeDtypeStruct((t, GROUP_W), f32)] * 3 + [jax.ShapeDtypeStruct((8, GROUP_W), f32)],
        in_specs=[col(C_GB), col(C_GC), col(C_HH), prev(C_GC), prev(C_HH), nxt(C_GB), col(0), nxt(0), _full((8, GROUP_W))],
        out_specs=[out, out, out, _full((8, GROUP_W))],
        compiler_params=_params(("arbitrary",)),
    )(proj, proj, proj, proj, proj, proj, dy, dy, w)


def _conv4(xr, halo, w, bias):
    tb = xr.shape[0]
    ext = jnp.concatenate([halo, xr], axis=0)
    pre = w[0:1] * ext[5:5 + tb] + w[1:2] * ext[6:6 + tb] + w[2:3] * ext[7:7 + tb] + w[3:4] * ext[8:8 + tb] + bias
    return pre, ext


def _tri():
    r = lax.broadcasted_iota(jnp.int32, (SSD_CHUNK, SSD_CHUNK), 0)
    c = lax.broadcasted_iota(jnp.int32, (SSD_CHUNK, SSD_CHUNK), 1)
    return r >= c


def _lane_pick(vals):
    rows = vals[0].shape[0]
    lane = lax.broadcasted_iota(jnp.int32, (rows, LANES), 1)
    out = jnp.zeros((rows, LANES), f32)
    for h, v in enumerate(vals):
        out = jnp.where(lane == h, v, out)
    return out


def _ssd_fwd(proj, conv_w, conv_b, dt_bias, a_log, d_cols, tb, xchg=None):
    t = proj.shape[0]
    cpt = tb // SSD_CHUNK

    def body(z_ref, xs_ref, bm_ref, cm_ref, xsh_ref, bmh_ref, cmh_ref, dt_ref, cw_ref, cb_ref, dtb_ref, al_ref, dk_ref,
             o_ref, y_ref, st_ref, state):
        i = pl.program_id(0)

        @pl.when(i == 0)
        def _():
            state[...] = jnp.zeros_like(state)

        cw, cb = cw_ref[...], cb_ref[...]
        acts = []
        for j, (r, hr) in enumerate(((xs_ref, xsh_ref), (bm_ref, bmh_ref), (cm_ref, cmh_ref))):
            halo = jnp.where(i > 0, hr[...], 0.0)
            pre, _ = _conv4(r[...], halo, cw[:, j * 256:(j + 1) * 256], cb[:, j * 256:(j + 1) * 256])
            acts.append(_silu(pre))
        xs, bm, cm = acts
        dt = _softplus(dt_ref[...] + dtb_ref[...])
        a = -jnp.exp(al_ref[...])
        adt = dt * a
        tri = _tri()
        trif = tri.astype(f32)
        dk = dk_ref[...]
        for c in range(cpt):
            rows = slice(c * SSD_CHUNK, (c + 1) * SSD_CHUNK)
            acol = _dot_exact(trif, adt[rows])
            arow = acol.T
            dt_c = dt[rows]
            ys = []
            rowi = lax.broadcasted_iota(jnp.int32, (SSD_CHUNK, 1), 0)
            first = lax.broadcasted_iota(jnp.int32, (SSD_CHUNK, SSD_CHUNK), 1) < SSD_P
            for g in range(SSD_HEADS // 2):
                cols = slice(g * 128, (g + 1) * 128)
                cg, bg = _b(cm[rows, cols]), _b(bm[rows, cols])
                xg = xs[rows, cols]
                heads = (2 * g, 2 * g + 1)
                ac = [acol[:, h:h + 1] for h in heads]
                alast = [v[SSD_CHUNK - 1:SSD_CHUNK] for v in ac]
                dtw = jnp.where(first, dt_c[:, heads[0]:heads[0] + 1], dt_c[:, heads[1]:heads[1] + 1])
                eaw = jnp.where(first, jnp.exp(ac[0]), jnp.exp(ac[1]))
                wdw = jnp.where(first, jnp.exp(alast[0] - ac[0]), jnp.exp(alast[1] - ac[1]))
                xdt = xg * dtw
                xb = _b(xdt)
                gmat = _dot_nt(cg, bg)
                ydiag = []
                for k, h in enumerate(heads):
                    lm = jnp.exp(jnp.where(tri, ac[k] - arow[h:h + 1, :], -jnp.inf))
                    ydiag.append(_dot(_b(gmat * lm), xb[:, k * SSD_P:(k + 1) * SSD_P]))
                s_in = state[g]
                st_ref[c, g] = s_in
                ys.append(jnp.concatenate(ydiag, axis=1) + eaw * _dot_nt(cg, _b(s_in)) + xg * dk[:, cols])
                state[g] = jnp.where(rowi < SSD_P, jnp.exp(alast[0]), jnp.exp(alast[1])) * s_in + _dot_tn(_b(xdt * wdw), bg)
            yc = jnp.concatenate(ys, axis=1)
            y_ref[rows, :] = yc
            o_ref[rows, :] = yc * _silu(z_ref[rows, :])

    def col(c):
        return pl.BlockSpec((tb, GROUP_W), lambda i: (i, c))

    def prev(c):
        return pl.BlockSpec((8, GROUP_W), lambda i: (jnp.maximum(i * (tb // 8) - 1, 0), c))

    out = pl.BlockSpec((tb, GROUP_W), lambda i: (i, 0))
    return _call(
        body, name="ssd_fwd", grid=(t // tb,),
        out_shape=[jax.ShapeDtypeStruct((t, GROUP_W), f32), jax.ShapeDtypeStruct((t, GROUP_W), f32),
                   jax.ShapeDtypeStruct((t // SSD_CHUNK, 2, 128, 128), f32)],
        in_specs=[col(C_Z), col(C_XS), col(C_BM), col(C_CM), prev(C_XS), prev(C_BM), prev(C_CM),
                  pl.BlockSpec((tb, LANES), lambda i: (i, C_DT128)),
                  _full((8, 768)), _full((1, 768)), _full((1, LANES)), _full((1, LANES)), _full((1, GROUP_W))],
        out_specs=[out, out, pl.BlockSpec((cpt, 2, 128, 128), lambda i: (i, 0, 0, 0))],
        scratch_shapes=[pltpu.VMEM((2, 128, 128), f32)],
        semantics=("arbitrary",), xchg=xchg,
        args=(proj, proj, proj, proj, proj, proj, proj, proj, conv_w, conv_b, dt_bias, a_log, d_cols))


def _ssd_bwd(proj, dyc, y_pre, states, conv_w, conv_b, dt_bias, a_log, d_cols, tb, xchg=None):
    t = proj.shape[0]
    nt = t // tb
    cpt = tb // SSD_CHUNK

    def body(z_ref, xs_ref, bm_ref, cm_ref, xsh_ref, bmh_ref, cmh_ref, dt_ref, dy_ref, yp_ref, st_ref,
             cw_ref, cb_ref, dtb_ref, al_ref, dk_ref,
             dz_ref, dxs_ref, dbm_ref, dcm_ref, ddt_ref, dcw_ref, dcb_ref, ddtb_ref, dal_ref, ddk_ref,
             dstate, carry):
        i = pl.program_id(0)
        ti = nt - 1 - i

        @pl.when(i == 0)
        def _():
            dstate[...] = jnp.zeros_like(dstate)
            carry[...] = jnp.zeros_like(carry)

        cw, cb = cw_ref[...], cb_ref[...]
        pres, exts, acts = [], [], []
        for j, (r, hr) in enumerate(((xs_ref, xsh_ref), (bm_ref, bmh_ref), (cm_ref, cmh_ref))):
            halo = jnp.where(ti > 0, hr[...], 0.0)
            pre, ext = _conv4(r[...], halo, cw[:, j * 256:(j + 1) * 256], cb[:, j * 256:(j + 1) * 256])
            pres.append(pre)
            exts.append(ext)
            acts.append(_silu(pre))
        xs, bm, cm = acts
        raw = dt_ref[...] + dtb_ref[...]
        dt = _softplus(raw)
        a = -jnp.exp(al_ref[...])
        adt = dt * a
        tri = _tri()
        trif = tri.astype(f32)
        dk = dk_ref[...]
        z = z_ref[...]
        dyc = dy_ref[...]
        dz_ref[...] = dyc * yp_ref[...] * _dsilu(z)
        dy_all = dyc * _silu(z)
        lane = lax.broadcasted_iota(jnp.int32, (1, LANES), 1)
        ddk_acc = jnp.zeros((1, LANES), f32)
        dal_acc = jnp.zeros((1, LANES), f32)
        dxs_c, dbm_c, dcm_c, ddt_c = [None] * cpt, [None] * cpt, [None] * cpt, [None] * cpt
        for c in reversed(range(cpt)):
            rows = slice(c * SSD_CHUNK, (c + 1) * SSD_CHUNK)
            acol = _dot_exact(trif, adt[rows])
            arow = acol.T
            dt_c = dt[rows]
            da_cols, da_rows, ddt_heads, dxs_groups, dbg, dcg = [], [], [], [], [], []
            rowi = lax.broadcasted_iota(jnp.int32, (SSD_CHUNK, 1), 0)
            first = lax.broadcasted_iota(jnp.int32, (SSD_CHUNK, SSD_CHUNK), 1) < SSD_P
            for g in range(SSD_HEADS // 2):
                cols = slice(g * 128, (g + 1) * 128)
                cgf, bgf = cm[rows, cols], bm[rows, cols]
                cg, bg = _b(cgf), _b(bgf)
                xg, dyg = xs[rows, cols], dy_all[rows, cols]
                s_in, dsn = st_ref[c, g], dstate[g]
                sb, dsnb = _b(s_in), _b(dsn)
                heads = (2 * g, 2 * g + 1)
                ac = [acol[:, h:h + 1] for h in heads]
                alast = [v[SSD_CHUNK - 1:SSD_CHUNK] for v in ac]
                el = [jnp.exp(v) for v in alast]
                dtw = jnp.where(first, dt_c[:, heads[0]:heads[0] + 1], dt_c[:, heads[1]:heads[1] + 1])
                eaw = jnp.where(first, jnp.exp(ac[0]), jnp.exp(ac[1]))
                wdw = jnp.where(first, jnp.exp(alast[0] - ac[0]), jnp.exp(alast[1] - ac[1]))
                xdt = xg * dtw
                xb, dyb = _b(xdt), _b(dyg)
                gmat = _dot_nt(cg, bg)
                dgs, dxh, da = None, [], []
                for k, h in enumerate(heads):
                    hc = slice(k * SSD_P, (k + 1) * SSD_P)
                    lm = jnp.exp(jnp.where(tri, ac[k] - arow[h:h + 1, :], -jnp.inf))
                    m = gmat * lm
                    dm = _dot_nt(dyb[:, hc], xb[:, hc])
                    dxh.append(_dot_tn(_b(m), dyb[:, hc]))
                    dgs = dm * lm if dgs is None else dgs + dm * lm
                    wm = dm * m
                    da.append(jnp.sum(wm, axis=1, keepdims=True))
                    da_rows.append(jnp.sum(wm, axis=0, keepdims=True))
                dgb = _b(dgs)
                dcg_g = _dot(dgb, bg)
                dbg_g = _dot_tn(dgb, cg)
                yoff = eaw * _dot_nt(cg, sb)
                dyoff = dyg * yoff
                dye = _b(dyg * eaw)
                dcg_g = dcg_g + _dot(dye, sb)
                ds_y = _dot_tn(dye, cg)
                u = _dot_nt(bg, dsnb)
                dx = jnp.concatenate(dxh, axis=1) + wdw * u
                dbg_g = dbg_g + _dot(_b(xdt * wdw), dsnb)
                xu = xdt * u * wdw
                ss = jnp.sum(dsn * s_in, axis=1, keepdims=True)
                dxx = dx * xg
                dyx = _colsum(dyg * xg)
                for k, h in enumerate(heads):
                    mine = first if k == 0 else jnp.logical_not(first)
                    dwv = jnp.sum(jnp.where(mine, xu, 0.0), axis=1, keepdims=True)
                    mine_rows = (rowi < SSD_P) if k == 0 else (rowi >= SSD_P)
                    dalast = jnp.sum(dwv, axis=0, keepdims=True) + el[k] * jnp.sum(jnp.where(mine_rows, ss, 0.0), axis=0, keepdims=True)
                    dah = da[k] + jnp.sum(jnp.where(mine, dyoff, 0.0), axis=1, keepdims=True) - dwv
                    da_cols.append(dah + jnp.where(rowi == SSD_CHUNK - 1, dalast, 0.0))
                    ddt_heads.append(jnp.sum(jnp.where(mine, dxx, 0.0), axis=1, keepdims=True))
                    ddk_acc = ddk_acc + jnp.where(lane == h, jnp.sum(jnp.where(mine[0:1], dyx, 0.0), axis=1, keepdims=True), 0.0)
                dstate[g] = jnp.where(rowi < SSD_P, el[0], el[1]) * dsn + ds_y
                dxs_groups.append(dx * dtw + dyg * dk[:, cols])
                dbg.append(dbg_g)
                dcg.append(dcg_g)
            da_blk = _lane_pick(da_cols)
            rowsel = lax.broadcasted_iota(jnp.int32, (SSD_CHUNK, SSD_CHUNK), 0)
            da_rows_blk = jnp.zeros((SSD_CHUNK, SSD_CHUNK), f32)
            for h in range(SSD_HEADS):
                da_rows_blk = jnp.where(rowsel == h, da_rows[h], da_rows_blk)
            da_blk = da_blk - da_rows_blk.T
            dadt = lax.dot_general(trif, da_blk, (((0,), (0,)), ((), ())), preferred_element_type=f32,
                                   precision=lax.Precision.HIGHEST)
            dal_acc = dal_acc + _colsum(dadt * dt_c)
            ddt_c[c] = dadt * a + _lane_pick(ddt_heads)
            dxs_c[c] = jnp.concatenate(dxs_groups, axis=1)
            dbm_c[c] = jnp.concatenate(dbg, axis=1)
            dcm_c[c] = jnp.concatenate(dcg, axis=1)
        ddt = jnp.concatenate(ddt_c, axis=0) if cpt > 1 else ddt_c[0]
        ddraw = jnp.where(lane < SSD_HEADS, ddt * jax.nn.sigmoid(raw), 0.0)
        ddt_ref[...] = ddraw
        _acc(ddtb_ref, _colsum(ddraw))
        _acc(dal_ref, jnp.where(lane < SSD_HEADS, dal_acc * a, 0.0))
        _acc(ddk_ref, ddk_acc)
        dcw_parts, dcb_parts = [], []
        for j, (dparts, out_ref) in enumerate(((dxs_c, dxs_ref), (dbm_c, dbm_ref), (dcm_c, dcm_ref))):
            dact = jnp.concatenate(dparts, axis=0) if cpt > 1 else dparts[0]
            dpre = dact * _dsilu(pres[j])
            w = cw[:, j * 256:(j + 1) * 256]
            ext = jnp.concatenate([dpre, carry[:, j * 256:(j + 1) * 256]], axis=0)
            out_ref[...] = w[3:4] * ext[0:tb] + w[2:3] * ext[1:1 + tb] + w[1:2] * ext[2:2 + tb] + w[0:1] * ext[3:3 + tb]
            carry[:, j * 256:(j + 1) * 256] = dpre[0:8]
            xe = exts[j]
            dcw_parts.append(jnp.concatenate([_colsum(dpre * xe[5 + k:5 + k + tb]) for k in range(4)]
                                             + [jnp.zeros((4, GROUP_W), f32)], axis=0))
            dcb_parts.append(_colsum(dpre))
        _acc(dcw_ref, jnp.concatenate(dcw_parts, axis=1))
        _acc(dcb_ref, jnp.concatenate(dcb_parts, axis=1))

    def col(c):
        return pl.BlockSpec((tb, GROUP_W), lambda i: (nt - 1 - i, c))

    def prev(c):
        return pl.BlockSpec((8, GROUP_W), lambda i: (jnp.maximum((nt - 1 - i) * (tb // 8) - 1, 0), c))

    out = pl.BlockSpec((tb, GROUP_W), lambda i: (nt - 1 - i, 0))
    vec = _full((1, LANES))
    return _call(
        body, name="ssd_bwd", grid=(nt,),
        out_shape=[jax.ShapeDtypeStruct((t, GROUP_W), f32)] * 4 + [jax.ShapeDtypeStruct((t, LANES), f32),
                   jax.ShapeDtypeStruct((8, 768), f32), jax.ShapeDtypeStruct((1, 768), f32)]
        + [jax.ShapeDtypeStruct((1, LANES), f32)] * 3,
        in_specs=[col(C_Z), col(C_XS), col(C_BM), col(C_CM), prev(C_XS), prev(C_BM), prev(C_CM),
                  pl.BlockSpec((tb, LANES), lambda i: (nt - 1 - i, C_DT128)), out, out,
                  pl.BlockSpec((cpt, 2, 128, 128), lambda i: (nt - 1 - i, 0, 0, 0)),
                  _full((8, 768)), _full((1, 768)), vec, vec, _full((1, GROUP_W))],
        out_specs=[out, out, out, out, pl.BlockSpec((tb, LANES), lambda i: (nt - 1 - i, 0)),
                   _full((8, 768)), _full((1, 768)), vec, vec, vec],
        scratch_shapes=[pltpu.VMEM((2, 128, 128), f32), pltpu.VMEM((8, 768), f32)],
        semantics=("arbitrary",), xchg=xchg,
        args=(proj, proj, proj, proj, proj, proj, proj, proj, dyc, y_pre, states, conv_w, conv_b, dt_bias, a_log, d_cols))


def _s5_coeffs(are, aim, ls):
    step = jnp.exp(ls)
    mag = jnp.exp(are * step)
    th = aim * step
    lre, lim = mag * jnp.cos(th), mag * jnp.sin(th)
    den = are * are + aim * aim
    nr = lre - 1.0
    fre = (nr * are + lim * aim) / den
    fim = (lim * are - nr * aim) / den
    return step, lre, lim, den, fre, fim


def _s5_prep(are, aim, ls, bre_bd, bim_bd):
    def body(are_ref, aim_ref, ls_ref, bre_ref, bim_ref, lre_ref, lim_ref, bbr_ref, bbi_ref):
        _, lre, lim, _, fre, fim = _s5_coeffs(are_ref[...], aim_ref[...], ls_ref[...])
        lre_ref[...] = lre
        lim_ref[...] = lim
        bre, bim = bre_ref[...], bim_ref[...]
        bbr_ref[...] = fre * bre - fim * bim
        bbi_ref[...] = fre * bim + fim * bre

    col = jax.ShapeDtypeStruct((S5_N, 1), f32)
    mat = jax.ShapeDtypeStruct((S5_N, GROUP_W), f32)
    return pl.pallas_call(body, name="s5_prep", out_shape=[col, col, mat, mat], in_specs=[VMEM] * 5, out_specs=[VMEM] * 4,
                          compiler_params=_params())(are, aim, ls, bre_bd, bim_bd)


def _s5_prep_bwd(are, aim, ls, bre_bd, bim_bd, dlre, dlim, dbbr, dbbi):
    def body(are_ref, aim_ref, ls_ref, bre_ref, bim_ref, dlre_ref, dlim_ref, dbbr_ref, dbbi_ref,
             dare_ref, daim_ref, dls_ref, dbre_ref, dbim_ref):
        are, aim = are_ref[...], aim_ref[...]
        step, lre, lim, den, fre, fim = _s5_coeffs(are, aim, ls_ref[...])
        r = lax.broadcasted_iota(jnp.int32, (S5_N, GROUP_W), 0) // 64
        c = lax.broadcasted_iota(jnp.int32, (S5_N, GROUP_W), 1) // 16
        mask = r == c
        gr = jnp.where(mask, dbbr_ref[...], 0.0)
        gi = jnp.where(mask, dbbi_ref[...], 0.0)
        bre, bim = bre_ref[...], bim_ref[...]
        dbre_ref[...] = fre * gr + fim * gi
        dbim_ref[...] = fre * gi - fim * gr
        dfre = jnp.sum(bre * gr + bim * gi, axis=1, keepdims=True)
        dfim = jnp.sum(bre * gi - bim * gr, axis=1, keepdims=True)
        ire, iim = are / den, aim / den
        tre = dlre_ref[...] + ire * dfre - iim * dfim
        tim = dlim_ref[...] + ire * dfim + iim * dfre
        dzre = lre * tre + lim * tim
        dzim = lre * tim - lim * tre
        qre = (fre * are + fim * aim) / den
        qim = (fim * are - fre * aim) / den
        dare_ref[...] = step * dzre - (qre * dfre + qim * dfim)
        daim_ref[...] = step * dzim - (qre * dfim - qim * dfre)
        dls = (are * dzre + aim * dzim) * step
        sel = (lax.broadcasted_iota(jnp.int32, (S5_N, LANES), 0) // 64 == lax.broadcasted_iota(jnp.int32, (S5_N, LANES), 1)).astype(f32)
        dls_ref[...] = lax.dot_general(sel, jnp.broadcast_to(dls, (S5_N, LANES)), (((0,), (0,)), ((), ())),
                                       preferred_element_type=f32, precision=lax.Precision.HIGHEST)

    col = jax.ShapeDtypeStruct((S5_N, 1), f32)
    mat = jax.ShapeDtypeStruct((S5_N, GROUP_W), f32)
    return pl.pallas_call(body, name="s5_prep_bwd", out_shape=[col, col, jax.ShapeDtypeStruct((LANES, LANES), f32), mat, mat],
                          in_specs=[VMEM] * 9, out_specs=[VMEM] * 5, compiler_params=_params(),
                          )(are, aim, ls, bre_bd, bim_bd, dlre, dlim, dbbr, dbbi)


def _cmul(ar, ai, br, bi):
    return ar * br - ai * bi, ar * bi + ai * br


def _s5_scan(re_ref, im_ref, carry_ref, mr, mi, n_groups, reverse):
    p1 = (mr, mi)
    p2 = _cmul(*p1, *p1)
    p3 = _cmul(*p2, *p1)
    p4 = _cmul(*p2, *p2)
    p5 = _cmul(*p4, *p1)
    p6 = _cmul(*p4, *p2)
    p7 = _cmul(*p4, *p3)
    p8 = _cmul(*p4, *p4)
    pows = [p1, p2, p3, p4, p5, p6, p7, p8]
    row = lax.broadcasted_iota(jnp.int32, (8, S5_N), 0)
    tr = jnp.zeros((8, S5_N), f32)
    ti = jnp.zeros((8, S5_N), f32)
    for i in range(8):
        p = pows[7 - i] if reverse else pows[i]
        tr = jnp.where(row == i, p[0], tr)
        ti = jnp.where(row == i, p[1], ti)
    steps = []
    for k, p in ((1, p1), (2, p2), (4, p4)):
        keep = (row + k < 8) if reverse else (row >= k)
        steps.append((8 - k if reverse else k, jnp.where(keep, p[0], 0.0), jnp.where(keep, p[1], 0.0)))
    edge = 0 if reverse else 7

    def step(j, carry):
        cr, ci = carry
        g = (n_groups - 1 - j) if reverse else j
        r0 = pl.multiple_of(g * 8, 8)
        xr = re_ref[pl.ds(r0, 8), :]
        xi = im_ref[pl.ds(r0, 8), :]
        for shift, br, bi in steps:
            sr = pltpu.roll(xr, shift, 0)
            si = pltpu.roll(xi, shift, 0)
            xr, xi = xr + br * sr - bi * si, xi + br * si + bi * sr
        xr, xi = xr + tr * cr - ti * ci, xi + tr * ci + ti * cr
        re_ref[pl.ds(r0, 8), :] = xr
        im_ref[pl.ds(r0, 8), :] = xi
        return (jnp.broadcast_to(xr[edge:edge + 1, :], (8, S5_N)), jnp.broadcast_to(xi[edge:edge + 1, :], (8, S5_N)))

    cr, ci = lax.fori_loop(0, n_groups, step, (carry_ref[0], carry_ref[1]))
    carry_ref[0] = cr
    carry_ref[1] = ci


def _s5_output(u, xr, xi, ctr, cti, d):
    return _dot_nt(_b(xr), _b(ctr)) - _dot_nt(_b(xi), _b(cti)) + d * u


def _s5_fwd(proj, bbr, bbi, ctr, cti, lre, lim, d, glu_w, glu_b, tb, xchg=None):
    t = proj.shape[0]

    def body(u_ref, bbr_ref, bbi_ref, ctr_ref, cti_ref, lr_ref, li_ref, d_ref, gw_ref, gb_ref, o_ref, xr_ref, xi_ref, carry):
        @pl.when(pl.program_id(0) == 0)
        def _():
            carry[...] = jnp.zeros_like(carry)

        u = u_ref[...]
        ub = _b(u)
        xr_ref[...] = _dot_nt(ub, _b(bbr_ref[...]))
        xi_ref[...] = _dot_nt(ub, _b(bbi_ref[...]))
        _s5_scan(xr_ref, xi_ref, carry, lr_ref[...], li_ref[...], tb // 8, reverse=False)
        y = _s5_output(u, xr_ref[...], xi_ref[...], ctr_ref[...], cti_ref[...], d_ref[...])
        gl = _gelu(y)
        o_ref[...] = gl * jax.nn.sigmoid(_dot(_b(gl), _b(gw_ref[...])) + gb_ref[...])

    state = pl.BlockSpec((tb, S5_N), lambda i: (i, 0))
    return _call(
        body, name="s5_fwd", grid=(t // tb,),
        out_shape=[jax.ShapeDtypeStruct((t, GROUP_W), f32), jax.ShapeDtypeStruct((t, S5_N), f32), jax.ShapeDtypeStruct((t, S5_N), f32)],
        in_specs=[pl.BlockSpec((tb, GROUP_W), lambda i: (i, C_S5)), _full((S5_N, GROUP_W)), _full((S5_N, GROUP_W)),
                  _full((GROUP_W, S5_N)), _full((GROUP_W, S5_N)), _full((1, S5_N)), _full((1, S5_N)),
                  _full((1, GROUP_W)), _full((GROUP_W, GROUP_W)), _full((1, GROUP_W))],
        out_specs=[pl.BlockSpec((tb, GROUP_W), lambda i: (i, 0)), state, state],
        scratch_shapes=[pltpu.VMEM((2, 8, S5_N), f32)],
        semantics=("arbitrary",), xchg=xchg, args=(proj, bbr, bbi, ctr, cti, lre, lim, d, glu_w, glu_b))


def _s5_bwd(proj, dyd, xr_all, xi_all, bbr, bbi, ctr, cti, lre, lim, d, glu_w, glu_b, tb, xchg=None):
    t = proj.shape[0]
    nt = t // tb

    def body(u_ref, dy_ref, xr_ref, xi_ref, xrh_ref, xih_ref, bbr_ref, bbi_ref, ctr_ref, cti_ref, lr_ref, li_ref,
             d_ref, gw_ref, gb_ref,
             du_ref, dlr_ref, dli_ref, dbbr_ref, dbbi_ref, dctr_ref, dcti_ref, dd_ref, dgw_ref, dgb_ref,
             gr_ref, gi_ref, carry):
        i = pl.program_id(0)
        ti = nt - 1 - i

        @pl.when(i == 0)
        def _():
            carry[...] = jnp.zeros_like(carry)

        u = u_ref[...]
        ub = _b(u)
        xr, xi = xr_ref[...], xi_ref[...]
        ctr, cti = _b(ctr_ref[...]), _b(cti_ref[...])
        d = d_ref[...]
        gw = _b(gw_ref[...])
        y = _s5_output(u, xr, xi, ctr, cti, d)
        gl = _gelu(y)
        sg = jax.nn.sigmoid(_dot(_b(gl), gw) + gb_ref[...])
        dout = dy_ref[...]
        q = dout * gl * sg * (1.0 - sg)
        qb = _b(q)
        dgl = dout * sg + _dot_nt(qb, gw)
        _acc(dgw_ref, _dot_tn(_b(gl), qb))
        _acc(dgb_ref, _colsum(q))
        dyv = dgl * _dgelu(y)
        _acc(dd_ref, _colsum(dyv * u))
        dyb = _b(dyv)
        gr_ref[...] = _dot(dyb, ctr)
        gi_ref[...] = -_dot(dyb, cti)
        _acc(dctr_ref, _dot_tn(dyb, _b(xr)))
        _acc(dcti_ref, -_dot_tn(dyb, _b(xi)))
        _s5_scan(gr_ref, gi_ref, carry, lr_ref[...], -li_ref[...], tb // 8, reverse=True)
        gr, gi = gr_ref[...], gi_ref[...]
        xpr = jnp.concatenate([jnp.where(ti > 0, xrh_ref[...], 0.0), xr], axis=0)[7:7 + tb]
        xpi = jnp.concatenate([jnp.where(ti > 0, xih_ref[...], 0.0), xi], axis=0)[7:7 + tb]
        _acc(dlr_ref, _colsum(gr * xpr + gi * xpi))
        _acc(dli_ref, _colsum(gi * xpr - gr * xpi))
        grb, gib = _b(gr), _b(gi)
        _acc(dbbr_ref, _dot_tn(grb, ub))
        _acc(dbbi_ref, _dot_tn(gib, ub))
        du_ref[...] = dyv * d + _dot(grb, _b(bbr_ref[...])) + _dot(gib, _b(bbi_ref[...]))

    state = pl.BlockSpec((tb, S5_N), lambda i: (nt - 1 - i, 0))
    prev = pl.BlockSpec((8, S5_N), lambda i: (jnp.maximum((nt - 1 - i) * (tb // 8) - 1, 0), 0))
    tile = pl.BlockSpec((tb, GROUP_W), lambda i: (nt - 1 - i, 0))
    return _call(
        body, name="s5_bwd", grid=(nt,),
        out_shape=[jax.ShapeDtypeStruct((t, GROUP_W), f32), jax.ShapeDtypeStruct((1, S5_N), f32), jax.ShapeDtypeStruct((1, S5_N), f32),
                   jax.ShapeDtypeStruct((S5_N, GROUP_W), f32), jax.ShapeDtypeStruct((S5_N, GROUP_W), f32),
                   jax.ShapeDtypeStruct((GROUP_W, S5_N), f32), jax.ShapeDtypeStruct((GROUP_W, S5_N), f32),
                   jax.ShapeDtypeStruct((1, GROUP_W), f32), jax.ShapeDtypeStruct((GROUP_W, GROUP_W), f32),
                   jax.ShapeDtypeStruct((1, GROUP_W), f32)],
        in_specs=[pl.BlockSpec((tb, GROUP_W), lambda i: (nt - 1 - i, C_S5)), tile, state, state, prev, prev,
                  _full((S5_N, GROUP_W)), _full((S5_N, GROUP_W)), _full((GROUP_W, S5_N)), _full((GROUP_W, S5_N)),
                  _full((1, S5_N)), _full((1, S5_N)), _full((1, GROUP_W)), _full((GROUP_W, GROUP_W)), _full((1, GROUP_W))],
        out_specs=[tile, _full((1, S5_N)), _full((1, S5_N)), _full((S5_N, GROUP_W)), _full((S5_N, GROUP_W)),
                   _full((GROUP_W, S5_N)), _full((GROUP_W, S5_N)), _full((1, GROUP_W)), _full((GROUP_W, GROUP_W)), _full((1, GROUP_W))],
        scratch_shapes=[pltpu.VMEM((tb, S5_N), f32), pltpu.VMEM((tb, S5_N), f32), pltpu.VMEM((2, 8, S5_N), f32)],
        semantics=("arbitrary",), xchg=xchg,
        args=(proj, dyd, xr_all, xi_all, xr_all, xi_all, bbr, bbi, ctr, cti, lre, lim, d, glu_w, glu_b))


def _outproj_fwd(ys, h, bn_w, g1, w_out, tb):
    t = h.shape[0]

    def body(ya_ref, yb_ref, yc_ref, yd_ref, h_ref, bn_ref, g1_ref, w_ref, h1_ref, o_ref, gr_ref):
        bn = bn_ref[...]
        parts = []
        for g, r in enumerate((ya_ref, yb_ref, yc_ref, yd_ref)):
            n, _ = _rms(r[...])
            parts.append(n * bn[:, g * GROUP_W:(g + 1) * GROUP_W])
        groups = _b(jnp.concatenate(parts, axis=1))
        gr_ref[...] = groups
        o = _dot(groups, w_ref[...])
        o_ref[...] = o
        h1_ref[...] = h_ref[...] + g1_ref[...] * o

    grp = pl.BlockSpec((tb, GROUP_W), lambda i: (i, 0))
    row = pl.BlockSpec((tb, D_MODEL), lambda i: (i, 0))
    vec = _full((1, D_MODEL))
    return pl.pallas_call(
        body, name="outproj_fwd", grid=(t // tb,),
        out_shape=[jax.ShapeDtypeStruct((t, D_MODEL), f32), jax.ShapeDtypeStruct((t, D_MODEL), f32),
                   jax.ShapeDtypeStruct((t, D_MODEL), bf16)],
        in_specs=[grp, grp, grp, grp, row, vec, vec, _full((D_MODEL, D_MODEL))],
        out_specs=[row, row, row],
        compiler_params=_params(("parallel",)),
    )(*ys, h, bn_w, g1, w_out)


def _outproj_bwd(dh1, o, ys, bn_w, g1, w_out, tb):
    t = dh1.shape[0]

    def body(dh_ref, o_ref, ya_ref, yb_ref, yc_ref, yd_ref, bn_ref, g1_ref, w_ref,
             da_ref, db_ref, dc_ref, dd_ref, do_ref, dg1_ref, dbn_ref):
        dh = dh_ref[...]
        _acc(dg1_ref, _colsum(dh * o_ref[...]))
        do = _b(dh * g1_ref[...])
        do_ref[...] = do
        dgroups = _dot_nt(do, w_ref[...])
        bn = bn_ref[...]
        dbn = []
        for g, (r, dr) in enumerate(((ya_ref, da_ref), (yb_ref, db_ref), (yc_ref, dc_ref), (yd_ref, dd_ref))):
            n, rr = _rms(r[...])
            dgr = dgroups[:, g * GROUP_W:(g + 1) * GROUP_W]
            dbn.append(_colsum(dgr * n))
            dr[...] = _rms_bwd(dgr * bn[:, g * GROUP_W:(g + 1) * GROUP_W], n, rr)
        _acc(dbn_ref, jnp.concatenate(dbn, axis=1))

    grp = pl.BlockSpec((tb, GROUP_W), lambda i: (i, 0))
    row = pl.BlockSpec((tb, D_MODEL), lambda i: (i, 0))
    vec = _full((1, D_MODEL))
    return pl.pallas_call(
        body, name="outproj_bwd", grid=(t // tb,),
        out_shape=[jax.ShapeDtypeStruct((t, GROUP_W), f32)] * 4 + [jax.ShapeDtypeStruct((t, D_MODEL), bf16),
                   jax.ShapeDtypeStruct((1, D_MODEL), f32), jax.ShapeDtypeStruct((1, D_MODEL), f32)],
        in_specs=[row, row, grp, grp, grp, grp, vec, vec, _full((D_MODEL, D_MODEL))],
        out_specs=[grp, grp, grp, grp, row, vec, vec],
        compiler_params=_params(("arbitrary",)),
    )(dh1, o, *ys, bn_w, g1, w_out)


def _mlp_fwd(h1, norm_w, sc, sh, g2, w1, w2, tb, xchg=None):
    t = h1.shape[0]
    nh = w1.shape[0] // MLP_SLABS

    def body(h_ref, nw_ref, sc_ref, sh_ref, g2_ref, w1_ref, w2_ref, h2_ref, m_ref, v_ref, r_ref, acc):
        j = pl.program_id(1)

        @pl.when(j == 0)
        def _():
            n, _ = _rms(h_ref[...])
            v_ref[...] = _b(n * nw_ref[...] * (1.0 + sc_ref[...]) + sh_ref[...])

        v = v_ref[...]
        p = None
        for s in range(MLP_SLABS):
            ra = jnp.maximum(_dot(v, w1_ref[s]), 0.0)
            r = _b(ra * ra)
            r_ref[:, s * MLP_HB:(s + 1) * MLP_HB] = r
            q = _dot(r, w2_ref[s])
            p = q if p is None else p + q

        @pl.when(j == 0)
        def _():
            acc[...] = p

        @pl.when(j != 0)
        def _():
            acc[...] += p

        @pl.when(j == nh - 1)
        def _():
            m = acc[...]
            m_ref[...] = _b(m)
            h2_ref[...] = h_ref[...] + g2_ref[...] * m

    row = pl.BlockSpec((tb, D_MODEL), lambda i, j: (i, 0))
    hid = pl.BlockSpec((tb, MLP_SLABS * MLP_HB), lambda i, j: (i, j))
    vec = _full((1, D_MODEL))
    return _call(
        body, name="mlp_fwd", grid=(t // tb, nh),
        out_shape=[jax.ShapeDtypeStruct((t, D_MODEL), f32), jax.ShapeDtypeStruct((t, D_MODEL), bf16),
                   jax.ShapeDtypeStruct((t, D_MODEL), bf16), jax.ShapeDtypeStruct((t, N_DEV * MLP_HB), bf16)],
        in_specs=[row, vec, vec, vec, vec, pl.BlockSpec((MLP_SLABS, D_MODEL, MLP_HB), lambda i, j: (j, 0, 0)),
                  pl.BlockSpec((MLP_SLABS, MLP_HB, D_MODEL), lambda i, j: (j, 0, 0))],
        out_specs=[row, row, row, hid],
        scratch_shapes=[pltpu.VMEM((tb, D_MODEL), f32)],
        semantics=("arbitrary", "arbitrary"), xchg=xchg, args=(h1, norm_w, sc, sh, g2, w1, w2))


def _mlp_bwd(dh2, m, h1, r, norm_w, sc, sh, g2, w1, w2, tb, xchg=None):
    t = h1.shape[0]
    nh = w1.shape[0] // MLP_SLABS

    def body(dh_ref, m_ref, h_ref, r_ref, nw_ref, sc_ref, sh_ref, g2_ref, w1_ref, w2_ref,
             dh1_ref, do_ref, da_ref, dg2_ref, dsh_ref, dsc_ref, dnw_ref, acc):
        j = pl.program_id(1)

        @pl.when(j == 0)
        def _():
            dh = dh_ref[...]
            _acc(dg2_ref, _colsum(dh * m_ref[...].astype(f32)))
            do_ref[...] = _b(dh * g2_ref[...])

        do = do_ref[...]
        p = None
        for s in range(MLP_SLABS):
            cols = slice(s * MLP_HB, (s + 1) * MLP_HB)
            dr = _dot_nt(do, w2_ref[s])
            da = _b(dr * 2.0 * jnp.sqrt(r_ref[:, cols].astype(f32)))
            da_ref[:, cols] = da
            q = _dot_nt(da, w1_ref[s])
            p = q if p is None else p + q

        @pl.when(j == 0)
        def _():
            acc[...] = p

        @pl.when(j != 0)
        def _():
            acc[...] += p

        @pl.when(j == nh - 1)
        def _():
            dv = acc[...]
            n, r = _rms(h_ref[...])
            nw = nw_ref[...]
            gain = 1.0 + sc_ref[...]
            _acc(dsh_ref, _colsum(dv))
            _acc(dsc_ref, _colsum(dv * n * nw))
            _acc(dnw_ref, _colsum(dv * gain * n))
            dh1_ref[...] = dh_ref[...] + _rms_bwd(dv * nw * gain, n, r)

    row = pl.BlockSpec((tb, D_MODEL), lambda i, j: (i, 0))
    hid = pl.BlockSpec((tb, MLP_SLABS * MLP_HB), lambda i, j: (i, j))
    vec = _full((1, D_MODEL))
    return _call(
        body, name="mlp_bwd", grid=(t // tb, nh),
        out_shape=[jax.ShapeDtypeStruct((t, D_MODEL), f32), jax.ShapeDtypeStruct((t, D_MODEL), bf16),
                   jax.ShapeDtypeStruct((t, N_DEV * MLP_HB), bf16)] + [jax.ShapeDtypeStruct((1, D_MODEL), f32)] * 4,
        in_specs=[row, row, row, hid, vec, vec, vec, vec,
                  pl.BlockSpec((MLP_SLABS, D_MODEL, MLP_HB), lambda i, j: (j, 0, 0)),
                  pl.BlockSpec((MLP_SLABS, MLP_HB, D_MODEL), lambda i, j: (j, 0, 0))],
        out_specs=[row, row, hid, vec, vec, vec, vec],
        scratch_shapes=[pltpu.VMEM((tb, D_MODEL), f32)],
        semantics=("arbitrary", "arbitrary"), xchg=xchg, args=(dh2, m, h1, r, norm_w, sc, sh, g2, w1, w2))


def _loss_head(h, target, norm_w, tb):
    t = h.shape[0]

    def body(h_ref, t_ref, w_ref, loss_ref, dh_ref, dw_ref):
        n, r = _rms(h_ref[...])
        w = w_ref[...]
        err = n * w - t_ref[...]
        part = 0.5 * jnp.sum(jnp.sum(err * err, axis=1, keepdims=True), axis=0, keepdims=True) / D_MODEL
        _acc(loss_ref, jnp.broadcast_to(part, (8, LANES)))
        dy = err / D_MODEL
        _acc(dw_ref, _colsum(dy * n))
        dh_ref[...] = _rms_bwd(dy * w, n, r)

    row = pl.BlockSpec((tb, D_MODEL), lambda i: (i, 0))
    return pl.pallas_call(
        body, name="loss_head", grid=(t // tb,),
        out_shape=[jax.ShapeDtypeStruct((8, LANES), f32), jax.ShapeDtypeStruct((t, D_MODEL), f32),
                   jax.ShapeDtypeStruct((1, D_MODEL), f32)],
        in_specs=[row, row, _full((1, D_MODEL))],
        out_specs=[_full((8, LANES)), row, _full((1, D_MODEL))],
        compiler_params=_params(("arbitrary",)),
    )(h, target, norm_w)


def _adam_math(w, g, m, v):
    m2 = ADAM_B1 * m + (1.0 - ADAM_B1) * g
    v2 = ADAM_B2 * v + (1.0 - ADAM_B2) * (g * g)
    mh = m2 / (1.0 - ADAM_B1 ** ADAM_STEP)
    vh = v2 / (1.0 - ADAM_B2 ** ADAM_STEP)
    return -ADAM_LR * (mh / (jnp.sqrt(vh) + ADAM_EPS) + ADAM_WD * w), m2, v2


def _sum_adamw(parts, w, m, v, name, rb):
    n_src, r, c = parts.shape

    def body(p_ref, w_ref, m_ref, v_ref, g_ref, d_ref, m2_ref, v2_ref):
        g = p_ref[0].astype(f32)
        for s in range(1, n_src):
            g = g + p_ref[s].astype(f32)
        g_ref[...] = g
        d, m2, v2 = _adam_math(w_ref[...], g, m_ref[...], v_ref[...])
        d_ref[...] = d
        m2_ref[...] = m2
        v2_ref[...] = v2

    blk = pl.BlockSpec((rb, c), lambda i: (i, 0))
    return pl.pallas_call(
        body, name=name, grid=(r // rb,),
        out_shape=[jax.ShapeDtypeStruct((r, c), f32)] * 4,
        in_specs=[pl.BlockSpec((n_src, rb, c), lambda i: (0, i, 0)), blk, blk, blk],
        out_specs=[blk] * 4,
        compiler_params=_params(("parallel",)),
    )(parts, w, m, v)


def _sum_adamw_layers(parts0, parts1, w, m, v, name, rb):
    n_src, r, c = parts0.shape
    nb = r // rb

    def body(p0_ref, p1_ref, w_ref, m_ref, v_ref, g_ref, d_ref, m2_ref, v2_ref):
        def update(p_ref):
            g = p_ref[0].astype(f32)
            for s in range(1, n_src):
                g = g + p_ref[s].astype(f32)
            g_ref[0] = g
            d, m2, v2 = _adam_math(w_ref[0], g, m_ref[0], v_ref[0])
            d_ref[0] = d
            m2_ref[0] = m2
            v2_ref[0] = v2

        @pl.when(pl.program_id(0) == 0)
        def _():
            update(p0_ref)

        @pl.when(pl.program_id(0) == 1)
        def _():
            update(p1_ref)

    blk = pl.BlockSpec((1, rb, c), lambda l, i: (l, i, 0))
    return pl.pallas_call(
        body, name=name, grid=(2, nb),
        out_shape=[jax.ShapeDtypeStruct((2, r, c), f32)] * 4,
        in_specs=[pl.BlockSpec((n_src, rb, c), lambda l, i: (0, jnp.where(l == 0, i, nb - 1), 0)),
                  pl.BlockSpec((n_src, rb, c), lambda l, i: (0, jnp.where(l == 1, i, 0), 0)), blk, blk, blk],
        out_specs=[blk] * 4,
        compiler_params=_params(("arbitrary", "arbitrary")),
    )(parts0, parts1, w, m, v)


def _reorder_in(w):
    pad = jnp.zeros(w.shape[:-1] + (P_IN - 2308,), w.dtype)
    return jnp.concatenate([w[..., :2048], w[..., 2052:2308], w[..., 2048:2052], pad], axis=-1)


def _unreorder_in(w):
    return jnp.concatenate([w[..., :2048], w[..., 2304:2308], w[..., 2048:2304]], axis=-1)


def _block_diag(w2d, n_blocks):
    rows, cols = w2d.shape
    tiled = jnp.tile(w2d, (1, n_blocks))
    rb = lax.broadcasted_iota(jnp.int32, tiled.shape, 0) // (rows // n_blocks)
    cb = lax.broadcasted_iota(jnp.int32, tiled.shape, 1) // cols
    return jnp.where(rb == cb, tiled, jnp.zeros_like(tiled))


def _block_diag_extract(w_bd, n_blocks):
    rows, wide = w_bd.shape
    r, c = rows // n_blocks, wide // n_blocks
    w4 = w_bd.reshape(n_blocks, r, n_blocks, c)
    idx = jnp.arange(n_blocks)
    return w4[idx, :, idx, :]


def _lanes128(v):
    return jnp.pad(v.reshape(1, -1), ((0, 0), (0, LANES - v.size)))


def _rows_of(shape):
    n = 1
    for d in shape:
        n *= d
    return -(-n // (8 * LANES)) * 8, n


def _flat_pack(arrs, row_multiple=8):
    blocks = []
    for a in arrs:
        rows, n = _rows_of(a.shape)
        blocks.append(jnp.pad(a.reshape(-1), (0, rows * LANES - n)).reshape(rows, LANES))
    total = sum(b.shape[0] for b in blocks)
    pad = -total % row_multiple
    if pad:
        blocks.append(jnp.zeros((pad, LANES), blocks[0].dtype))
    return jnp.concatenate(blocks, axis=0)


def _flat_unpack(packed, shapes):
    out, off = [], 0
    for s in shapes:
        rows, n = _rows_of(s)
        out.append(packed[off:off + rows].reshape(-1)[:n].reshape(s))
        off += rows
    return out


_W_NAMES = ['norm_mix_w', 'norm_mlp_w', 'ada_w', 'ada_b', 'w_in', 'pool_w', 'pool_scale', 'sconv_w', 'ssd_conv_w',
            'ssd_conv_b', 'ssd_dt_bias', 'ssd_a_log', 'ssd_d', 's5_a_re', 's5_a_im', 's5_log_step', 's5_b_re', 's5_b_im',
            's5_c_re', 's5_c_im', 's5_d', 's5_glu_w', 's5_glu_b', 'branch_norm_w', 'w_out', 'mlp_w1', 'mlp_w2',
            'final_norm_w']
_BIG = ('ada_w', 'w_in', 'w_out', 'mlp_w1', 'mlp_w2')
_SMALL = [n for n in _W_NAMES if n not in _BIG]
_SHARDED_SMALL = {'sconv_w': (2, 32), 'ssd_conv_w': (2, 96), 's5_glu_w': (1, 32)}


def _gather(*blocks):
    return _ChipGather(blocks)


def _scatter(*parts):
    return _Exchange(parts, gather=False)


def _layer_forward(l, h, p, w, sh_b, tb):
    first = l == 0
    (proj, u_b), got = _inproj_fwd(h, p['norm_mix_w'][l], p['sc1'][l], p['sh1'][l], w['w_in', l], tb,
                                   xchg=_gather(sh_b[1][0]) if first else None)
    if first:
        w['w_out', 0] = got[0].reshape(D_MODEL, D_MODEL)
    ya = _pool_fwd(proj, p['pool_bd'][l], p['pool_scale'][l], tb)
    yb = _sconv_fwd(proj, p['sconv_w8'][l], tb)
    (yc, yc_pre, states), got = _ssd_fwd(proj, p['ssd_conv_w8'][l], p['ssd_conv_b'][l], p['ssd_dt_bias'][l], p['ssd_a_log'][l],
                                         p['ssd_d_cols'][l], tb, xchg=_gather(sh_b[2][0]) if first else None)
    if first:
        w['w1', 0] = got[0]
    (yd, xr, xi), got = _s5_fwd(proj, p['bbr'][l], p['bbi'][l], p['ctr'][l], p['cti'][l], p['lre'][l], p['lim'][l],
                                p['s5_d'][l], p['glu_w'][l], p['glu_b'][l], tb, xchg=_gather(sh_b[3][0]) if first else None)
    if first:
        w['w2', 0] = got[0]
    ys = (ya, yb, yc, yd)
    h1, o, groups_b = _outproj_fwd(ys, h, p['branch_norm_w'][l], p['g1'][l], w['w_out', l], tb)
    (h2, m, v_b, r_b), got = _mlp_fwd(h1, p['norm_mlp_w'][l], p['sc2'][l], p['sh2'][l], p['g2'][l], w['w1', l], w['w2', l],
                                      min(MLP_TB, h.shape[0]), xchg=_gather(*[sh_b[k][1] for k in range(4)]) if first else None)
    if first:
        w['w_in', 1] = got[0].reshape(D_MODEL, P_IN)
        w['w_out', 1] = got[1].reshape(D_MODEL, D_MODEL)
        w['w1', 1], w['w2', 1] = got[2], got[3]
    saved = dict(h=h, proj=proj, u_b=u_b, ys=ys, yc_pre=yc_pre, states=states, xr=xr, xi=xi, h1=h1, o=o,
                 groups_b=groups_b, m=m, v_b=v_b, r_b=r_b)
    return h2, saved


def _layer_backward(l, dh2, s, p, w, pending, recv, tb):
    def carry(names):
        names = [n for n in names if n in pending]
        return names, (_scatter(*[pending.pop(n) for n in names]) if names else None)

    def landed(names, got):
        for n, g in zip(names, got):
            recv[n] = g

    names, xchg = carry([('w_out', 1), ('w_in', 1)])
    (dh1, do2_b, da_b, dg2, dsh2, dsc2, dnw_mlp), got = _mlp_bwd(dh2, s['m'], s['h1'], s['r_b'], p['norm_mlp_w'][l], p['sc2'][l],
                                                                p['sh2'][l], p['g2'][l], w['w1', l], w['w2', l], tb, xchg=xchg)
    landed(names, got)
    pending['mlp_w2', l] = _wgrad(s['r_b'], do2_b, 1, "wgrad_w2", tm=1024, tk=1024).reshape(N_DEV, MLP_HB, D_MODEL)
    pending['mlp_w1', l] = _wgrad(s['v_b'], da_b, N_DEV, "wgrad_w1", tm=1024, tk=2048)
    dya, dyb, dyc, dyd, do1_b, dg1, dbn = _outproj_bwd(dh1, s['o'], s['ys'], p['branch_norm_w'][l], p['g1'][l], w['w_out', l], tb)
    pending['w_out', l] = _wgrad(s['groups_b'], do1_b, 1, "wgrad_wout", tm=1024, tk=1024).reshape(N_DEV, D_MODEL // N_DEV, D_MODEL)
    proj = s['proj']
    dv, dpool_bd, dpool_scale = _pool_bwd(proj, dya, p['pool_bd'][l], p['pool_scale'][l], tb)
    dgb, dgc, dhh, dsconv = _sconv_bwd(proj, dyb, p['sconv_w8'][l], tb)
    names, xchg = carry([('mlp_w1', l)] + ([('w_out', 0)] if l == 0 else []))
    (dz, dxs, dbm, dcm, ddt, dconv_w, dconv_b, ddtb, dalog, ddskip), got = _ssd_bwd(
        proj, dyc, s['yc_pre'], s['states'], p['ssd_conv_w8'][l], p['ssd_conv_b'][l], p['ssd_dt_bias'][l], p['ssd_a_log'][l],
        p['ssd_d_cols'][l], tb, xchg=xchg)
    landed(names, got)
    names, xchg = carry([('mlp_w2', l)])
    (du5, dlr, dli, dbbr, dbbi, dctr, dcti, dd5, dgw, dgb5), got = _s5_bwd(
        proj, dyd, s['xr'], s['xi'], p['bbr'][l], p['bbi'][l], p['ctr'][l], p['cti'][l], p['lre'][l], p['lim'][l],
        p['s5_d'][l], p['glu_w'][l], p['glu_b'][l], tb, xchg=xchg)
    landed(names, got)
    dare, daim, dls, dbre_bd, dbim_bd = _s5_prep_bwd(p['are_c'][l], p['aim_c'][l], p['ls_c'][l], p['bre_bd'][l], p['bim_bd'][l],
                                                     dlr.reshape(S5_N, 1), dli.reshape(S5_N, 1), dbbr, dbbi)
    dparts = (dv, dgb, dgc, dhh, dz, dxs, dbm, dcm, du5, ddt)
    dh, dproj_b, dsh1, dsc1, dnw_mix = _inproj_bwd(dparts, dh1, s['h'], p['norm_mix_w'][l], p['sc1'][l], p['sh1'][l], w['w_in', l], tb)
    pending['w_in', l] = _wgrad(s['u_b'], dproj_b, 1, "wgrad_win", tm=512, tk=1024).reshape(N_DEV, D_MODEL // N_DEV, P_IN)
    small = {
        'norm_mix_w': dnw_mix.reshape(D_MODEL), 'norm_mlp_w': dnw_mlp.reshape(D_MODEL),
        'ada_b': jnp.concatenate([dsh1, dsc1, dg1, dsh2, dsc2, dg2], axis=1).reshape(6 * D_MODEL),
        'pool_w': _block_diag_extract(dpool_bd, 4), 'pool_scale': dpool_scale.reshape(GROUP_W),
        'sconv_w': dsconv[0:3], 'ssd_conv_w': dconv_w[0:4], 'ssd_conv_b': dconv_b.reshape(768),
        'ssd_dt_bias': ddtb[0, 0:4], 'ssd_a_log': dalog[0, 0:4], 'ssd_d': ddskip[0, 0:4],
        's5_a_re': dare.reshape(16, 64), 's5_a_im': daim.reshape(16, 64), 's5_log_step': dls[0:16, 0],
        's5_b_re': _block_diag_extract(dbre_bd, 16), 's5_b_im': _block_diag_extract(dbim_bd, 16),
        's5_c_re': _block_diag_extract(dctr, 16), 's5_c_im': _block_diag_extract(dcti, 16),
        's5_d': dd5.reshape(GROUP_W), 's5_glu_w': dgw, 's5_glu_b': dgb5.reshape(GROUP_W),
        'branch_norm_w': dbn.reshape(D_MODEL),
    }
    return dh, small


def _prepare_params(a, me):
    pack_shapes = [(1, D_MODEL), (2, 3, 32), (2, 4, 96), (2, 32, GROUP_W)]
    packed = _flat_pack([a['c'], a['sconv_w'], a['ssd_conv_w'], a['s5_glu_w']])
    gathered = _gather_rows(packed, "gather_small")
    pieces = [_flat_unpack(gathered[d], pack_shapes) for d in range(N_DEV)]
    c_all = jnp.concatenate([pc[0] for pc in pieces], axis=0)
    sconv_full = jnp.concatenate([pc[1] for pc in pieces], axis=2)
    ssd_conv_full = jnp.concatenate([pc[2] for pc in pieces], axis=2)
    glu_full = jnp.concatenate([pc[3] for pc in pieces], axis=1)

    ada_b_cols = lax.dynamic_slice_in_dim(a['ada_b'], me * 768, 768, axis=1).reshape(2, 1, 768)
    cond, modrows = _ada_forward(c_all, a['ada_w'], ada_b_cols)
    mod_recv = _all_to_all_rows(modrows.transpose(1, 0, 2), "exchange_mod")
    mod = mod_recv.transpose(1, 0, 2).reshape(2, 6 * D_MODEL)
    p = {'cond': cond}
    for k, name in enumerate(('sh1', 'sc1', 'g1', 'sh2', 'sc2', 'g2')):
        p[name] = mod[:, k * D_MODEL:(k + 1) * D_MODEL].reshape(2, 1, D_MODEL)

    for name in ('norm_mix_w', 'norm_mlp_w', 'branch_norm_w'):
        p[name] = a[name].reshape(2, 1, D_MODEL)
    p['pool_bd'] = jnp.stack([_block_diag(a['pool_w'][l].reshape(GROUP_W, 64), 4) for l in range(2)])
    p['pool_scale'] = a['pool_scale'].reshape(2, 1, GROUP_W)
    p['sconv_w8'] = jnp.pad(sconv_full, ((0, 0), (0, 5), (0, 0)))
    p['ssd_conv_w8'] = jnp.pad(ssd_conv_full, ((0, 0), (0, 4), (0, 0)))
    p['ssd_conv_b'] = a['ssd_conv_b'].reshape(2, 1, 768)
    p['ssd_dt_bias'] = jnp.pad(a['ssd_dt_bias'], ((0, 0), (0, LANES - 4))).reshape(2, 1, LANES)
    p['ssd_a_log'] = jnp.pad(a['ssd_a_log'], ((0, 0), (0, LANES - 4))).reshape(2, 1, LANES)
    p['ssd_d_cols'] = jnp.repeat(a['ssd_d'], SSD_P, axis=1).reshape(2, 1, GROUP_W)
    p['are_c'] = a['s5_a_re'].reshape(2, S5_N, 1)
    p['aim_c'] = a['s5_a_im'].reshape(2, S5_N, 1)
    p['ls_c'] = jnp.repeat(a['s5_log_step'], 64, axis=1).reshape(2, S5_N, 1)
    p['bre_bd'] = jnp.stack([_block_diag(a['s5_b_re'][l].reshape(S5_N, 16), 16) for l in range(2)])
    p['bim_bd'] = jnp.stack([_block_diag(a['s5_b_im'][l].reshape(S5_N, 16), 16) for l in range(2)])
    p['ctr'] = jnp.stack([_block_diag(a['s5_c_re'][l].reshape(GROUP_W, 64), 16) for l in range(2)])
    p['cti'] = jnp.stack([_block_diag(a['s5_c_im'][l].reshape(GROUP_W, 64), 16) for l in range(2)])
    p['s5_d'] = a['s5_d'].reshape(2, 1, GROUP_W)
    p['glu_w'] = glu_full
    p['glu_b'] = a['s5_glu_b'].reshape(2, 1, GROUP_W)
    lre, lim, bbr, bbi = [], [], [], []
    for l in range(2):
        r = _s5_prep(p['are_c'][l], p['aim_c'][l], p['ls_c'][l], p['bre_bd'][l], p['bim_bd'][l])
        lre.append(r[0].reshape(1, S5_N))
        lim.append(r[1].reshape(1, S5_N))
        bbr.append(r[2])
        bbi.append(r[3])
    p['lre'], p['lim'], p['bbr'], p['bbi'] = lre, lim, bbr, bbi
    return p


def kernel(x, c, norm_mix_w, norm_mlp_w, ada_w, ada_b, w_in, pool_w, pool_scale, sconv_w, ssd_conv_w, ssd_conv_b, ssd_dt_bias, ssd_a_log, ssd_d, s5_a_re, s5_a_im, s5_log_step, s5_b_re, s5_b_im, s5_c_re, s5_c_im, s5_d, s5_glu_w, s5_glu_b, branch_norm_w, w_out, mlp_w1, mlp_w2, final_norm_w, loss_target, m_norm_mix_w, m_norm_mlp_w, m_ada_w, m_ada_b, m_w_in, m_pool_w, m_pool_scale, m_sconv_w, m_ssd_conv_w, m_ssd_conv_b, m_ssd_dt_bias, m_ssd_a_log, m_ssd_d, m_s5_a_re, m_s5_a_im, m_s5_log_step, m_s5_b_re, m_s5_b_im, m_s5_c_re, m_s5_c_im, m_s5_d, m_s5_glu_w, m_s5_glu_b, m_branch_norm_w, m_w_out, m_mlp_w1, m_mlp_w2, m_final_norm_w, v_norm_mix_w, v_norm_mlp_w, v_ada_w, v_ada_b, v_w_in, v_pool_w, v_pool_scale, v_sconv_w, v_ssd_conv_w, v_ssd_conv_b, v_ssd_dt_bias, v_ssd_a_log, v_ssd_d, v_s5_a_re, v_s5_a_im, v_s5_log_step, v_s5_b_re, v_s5_b_im, v_s5_c_re, v_s5_c_im, v_s5_d, v_s5_glu_w, v_s5_glu_b, v_branch_norm_w, v_w_out, v_mlp_w1, v_mlp_w2, v_final_norm_w):
    a = dict(locals())
    t = x.shape[1]
    tb = min(512, t)
    me = _my_index()
    p = _prepare_params(a, me)

    sh_b = _cast_shards([_reorder_in(w_in), w_out, mlp_w1, mlp_w2])
    w = {('w_in', 0): _exchange_alone(_gather(sh_b[0][0]), "gather_w_in0")[0].reshape(D_MODEL, P_IN)}

    h = x.reshape(t, D_MODEL)
    saved = []
    for l in range(2):
        h, s = _layer_forward(l, h, p, w, sh_b, tb)
        saved.append(s)
    loss_blk, dh, dfinal = _loss_head(h, loss_target.reshape(t, D_MODEL), final_norm_w.reshape(1, D_MODEL), tb)
    loss = lax.psum(loss_blk[0, 0], ("x", "y", "c"))

    pending, recv, small_parts = {}, {}, [None, None]
    for l in (1, 0):
        dh, small_parts[l] = _layer_backward(l, dh, saved[l], p, w, pending, recv, tb)
    grad_x = dh.reshape(1, t, D_MODEL)
    recv['w_in', 0] = _exchange_alone(_scatter(pending.pop(('w_in', 0))), "exchange_w_in0")[0]

    grads, deltas, new_m, new_v = {}, {}, {}, {}

    wmv_in = [_reorder_in(a[n]) for n in ('w_in', 'm_w_in', 'v_w_in')]
    outs = _sum_adamw_layers(recv['w_in', 0], recv['w_in', 1], *wmv_in, "adamw_w_in", 128)
    grads['w_in'], deltas['w_in'], new_m['w_in'], new_v['w_in'] = [_unreorder_in(o) for o in outs]
    for name, rb in (('w_out', 128), ('mlp_w1', 256), ('mlp_w2', 256)):
        grads[name], deltas[name], new_m[name], new_v[name] = _sum_adamw_layers(
            recv[name, 0], recv[name, 1], a[name], a['m_' + name], a['v_' + name], "adamw_" + name, rb)

    dmod = jnp.stack([small_parts[0]['ada_b'], small_parts[1]['ada_b']])
    dmod_recv = _all_to_all_rows(dmod.reshape(2, N_DEV, 768).transpose(1, 0, 2), "exchange_dmod")
    g_ada = _ada_backward(p['cond'], dmod_recv.transpose(1, 0, 2))
    grads['ada_w'], deltas['ada_w'], new_m['ada_w'], new_v['ada_w'] = _sum_adamw_layers(
        g_ada[0:1], g_ada[1:2], ada_w, m_ada_w, v_ada_w, "adamw_ada_w", 256)

    layered = [n for n in _SMALL if n != 'final_norm_w']
    full = [jnp.stack([small_parts[0][n], small_parts[1][n]]) for n in layered] + [dfinal.reshape(D_MODEL)]
    full_shapes = [f.shape for f in full]
    summed = _flat_unpack(_allreduce_rows(_flat_pack(full, row_multiple=64)), full_shapes)
    local = []
    for n, g in zip(_SMALL, summed):
        if n in _SHARDED_SMALL:
            axis, size = _SHARDED_SMALL[n]
            g = lax.dynamic_slice_in_dim(g, me * size, size, axis=axis)
        local.append(g.reshape(a[n].shape))
    local_shapes = [g.shape for g in local]
    packed = [_flat_pack(xs) for xs in (local, [a[n] for n in _SMALL], [a['m_' + n] for n in _SMALL], [a['v_' + n] for n in _SMALL])]
    outs = _sum_adamw(packed[0][None], packed[1], packed[2], packed[3], "adamw_small", packed[0].shape[0])
    for store, o in zip((grads, deltas, new_m, new_v), outs):
        for n, val in zip(_SMALL, _flat_unpack(o, local_shapes)):
            store[n] = val

    return (loss, grad_x, *[grads[n] for n in _W_NAMES], *[deltas[n] for n in _W_NAMES],
            *[new_m[n] for n in _W_NAMES], *[new_v[n] for n in _W_NAMES])
```

```python
import functools

import jax
import jax.numpy as jnp
from jax import lax
from jax.experimental import pallas as pl
from jax.experimental.pallas import tpu as pltpu

f32 = jnp.float32
bf16 = jnp.bfloat16

N_DEV = 8
D_MODEL = 1024
GROUP_W = 256
P_IN = 2432
DT_COL = 2304
SSD_CHUNK = 128
SSD_HEADS = 4
SSD_P = 64
S5_N = 1024
MLP_HB = 512
MLP_TB = 1024
MLP_SLABS = 2
MLP_BWD_SLABS = 4
EPS = 1e-6
LANES = 128
VMEM_LIMIT = 56 * 1024 * 1024
ADAM_LR, ADAM_B1, ADAM_B2, ADAM_EPS, ADAM_WD, ADAM_STEP = 0.001, 0.9, 0.999, 1e-08, 0.01, 10
POOL_WINDOWS = (2, 4, 8, 16)

C_POOL, C_GB, C_GC, C_HH, C_Z, C_XS, C_BM, C_CM, C_S5 = range(9)
C_DT128 = DT_COL // LANES

MESH = pl.DeviceIdType.MESH
ANY = pl.BlockSpec(memory_space=pl.ANY)
VMEM = pl.BlockSpec(memory_space=pltpu.VMEM)


def _dot(a, b):
    return jnp.dot(a, b, preferred_element_type=f32)


def _dot_nt(a, b):
    return lax.dot_general(a, b, (((1,), (1,)), ((), ())), preferred_element_type=f32)


def _dot_tn(a, b):
    return lax.dot_general(a, b, (((0,), (0,)), ((), ())), preferred_element_type=f32)


def _dot_exact(a, b):
    return jnp.dot(a, b, preferred_element_type=f32, precision=lax.Precision.HIGHEST)


def _b(x):
    return x.astype(bf16)


def _silu(x):
    return x * jax.nn.sigmoid(x)


def _dsilu(x):
    s = jax.nn.sigmoid(x)
    return s * (1.0 + x * (1.0 - s))


def _softplus(x):
    return jnp.maximum(x, 0.0) + jnp.log1p(jnp.exp(-jnp.abs(x)))


_GELU_K = 0.7978845608028654
_GELU_C = 0.044715


def _gelu(x):
    return 0.5 * x * (1.0 + jnp.tanh(_GELU_K * (x + _GELU_C * x * x * x)))


def _dgelu(x):
    th = jnp.tanh(_GELU_K * (x + _GELU_C * x * x * x))
    return 0.5 * (1.0 + th) + 0.5 * x * (1.0 - th * th) * _GELU_K * (1.0 + 3.0 * _GELU_C * x * x)


def _rms(h):
    r = lax.rsqrt(jnp.mean(h * h, axis=-1, keepdims=True) + EPS)
    return h * r, r


def _rms_bwd(dn, n, r):
    return r * (dn - n * jnp.mean(dn * n, axis=-1, keepdims=True))


def _colsum(x):
    return jnp.sum(x, axis=0, keepdims=True)


def _params(sem=None):
    return pltpu.CompilerParams(dimension_semantics=sem, vmem_limit_bytes=VMEM_LIMIT)


def _full(shape):
    return pl.BlockSpec(shape, lambda *_: (0,) * len(shape))


def _acc(ref, val):
    @pl.when(pl.program_id(0) == 0)
    def _():
        ref[...] = val

    @pl.when(pl.program_id(0) != 0)
    def _():
        ref[...] += val


def _me():
    return lax.axis_index("x"), lax.axis_index("y"), lax.axis_index("c")


def _my_index():
    x, y, c = _me()
    return 4 * x + 2 * y + c


def _coords(p):
    return (p // 4, (p // 2) % 2, p % 2)


class _Exchange:
    def __init__(self, srcs, gather):
        self.srcs = list(srcs)
        self.gather = gather
        self.n = len(self.srcs)
        self.out_shape = [jax.ShapeDtypeStruct(((N_DEV,) + s.shape) if gather else s.shape, s.dtype) for s in self.srcs]
        self.scratch = [pltpu.SemaphoreType.DMA((self.n, N_DEV)), pltpu.SemaphoreType.DMA((self.n, N_DEV)),
                        pltpu.SemaphoreType.DMA((self.n,))]

    def _src(self, refs, t, dev):
        return refs[t] if self.gather else refs[t].at[dev]

    def _remote(self, xin, xout, sems, t, k, me, to):
        return pltpu.make_async_remote_copy(
            src_ref=self._src(xin, t, to), dst_ref=xout[t].at[me], send_sem=sems[0].at[t, k], recv_sem=sems[1].at[t, k],
            device_id=_coords(to), device_id_type=MESH)

    def start(self, xin, xout, sems):
        me = _my_index()
        for t in range(self.n):
            pltpu.make_async_copy(self._src(xin, t, me), xout[t].at[me], sems[2].at[t]).start()
            for k in range(1, N_DEV):
                self._remote(xin, xout, sems, t, k, me, (me + k) % N_DEV).start()

    def wait(self, xin, xout, sems):
        me = _my_index()
        for t in range(self.n):
            for k in range(1, N_DEV):
                src = (me + N_DEV - k) % N_DEV
                pltpu.make_async_remote_copy(
                    src_ref=self._src(xin, t, src), dst_ref=xout[t].at[src], send_sem=sems[0].at[t, k],
                    recv_sem=sems[1].at[t, k], device_id=_coords(src), device_id_type=MESH).wait_recv()
        for t in range(self.n):
            for k in range(1, N_DEV):
                self._remote(xin, xout, sems, t, k, me, (me + k) % N_DEV).wait_send()
            pltpu.make_async_copy(self._src(xin, t, me), xout[t].at[me], sems[2].at[t]).wait()

    def forward(self, xin, xout, sems):
        pass


class _ChipGather:
    def __init__(self, srcs):
        self.srcs = list(srcs)
        self.n = len(self.srcs)
        self.out_shape = [jax.ShapeDtypeStruct((N_DEV,) + s.shape, s.dtype) for s in self.srcs]
        self.scratch = [pltpu.SemaphoreType.DMA((self.n, 7)), pltpu.SemaphoreType.DMA((self.n, 7)),
                        pltpu.SemaphoreType.DMA((self.n,))]

    @staticmethod
    def _places():
        x, y, c = _me()
        chips = [(1 - x, y), (x, 1 - y), (1 - x, 1 - y)]
        return (x, y, c), (x, y, 1 - c), chips

    @staticmethod
    def _slab(ref, dev):
        return ref.at[4 * dev[0] + 2 * dev[1] + dev[2]]

    def _copy(self, xin, xout, sems, t, k, block, to, src=None):
        return pltpu.make_async_remote_copy(
            src_ref=self._slab(xout[t], block) if src is None else src, dst_ref=self._slab(xout[t], block),
            send_sem=sems[0].at[t, k], recv_sem=sems[1].at[t, k], device_id=to, device_id_type=MESH)

    def start(self, xin, xout, sems):
        me, sibling, chips = self._places()
        for t in range(self.n):
            pltpu.make_async_copy(xin[t], self._slab(xout[t], me), sems[2].at[t]).start()
            self._copy(xin, xout, sems, t, 0, me, sibling, src=xin[t]).start()
            for j, chip in enumerate(chips):
                self._copy(xin, xout, sems, t, 1 + j, me, (*chip, me[2]), src=xin[t]).start()

    def forward(self, xin, xout, sems):
        me, sibling, chips = self._places()
        for t in range(self.n):
            for j, chip in enumerate(chips):
                self._copy(xin, xout, sems, t, 1 + j, (*chip, me[2]), me).wait_recv()
                self._copy(xin, xout, sems, t, 4 + j, (*chip, me[2]), sibling).start()

    def wait(self, xin, xout, sems):
        me, sibling, chips = self._places()
        for t in range(self.n):
            self._copy(xin, xout, sems, t, 0, sibling, me).wait_recv()
            for j, chip in enumerate(chips):
                self._copy(xin, xout, sems, t, 4 + j, (*chip, 1 - me[2]), me).wait_recv()
        for t in range(self.n):
            self._copy(xin, xout, sems, t, 0, me, sibling, src=xin[t]).wait_send()
            for j, chip in enumerate(chips):
                self._copy(xin, xout, sems, t, 1 + j, me, (*chip, me[2]), src=xin[t]).wait_send()
                self._copy(xin, xout, sems, t, 4 + j, (*chip, me[2]), sibling).wait_send()
            pltpu.make_async_copy(xin[t], self._slab(xout[t], me), sems[2].at[t]).wait()


def _call(body, *, name, grid, in_specs, out_specs, out_shape, args, semantics, scratch_shapes=(), xchg=None):
    if xchg is None:
        outs = pl.pallas_call(body, name=name, grid=grid, in_specs=in_specs, out_specs=out_specs, out_shape=out_shape,
                              scratch_shapes=list(scratch_shapes), compiler_params=_params(semantics))(*args)
        return outs, ()
    n_in, n_out, n_scr, n = len(in_specs), len(out_specs), len(scratch_shapes), xchg.n

    def carried(*refs):
        ins, xin = refs[:n_in], refs[n_in:n_in + n]
        outs, xout = refs[n_in + n:n_in + n + n_out], refs[n_in + n + n_out:n_in + 2 * n + n_out]
        scr, sems = refs[n_in + 2 * n + n_out:n_in + 2 * n + n_out + n_scr], refs[n_in + 2 * n + n_out + n_scr:]
        step = pl.program_id(0)
        for d in range(1, len(grid)):
            step = step * grid[d] + pl.program_id(d)
        n_steps = functools.reduce(lambda a, b: a * b, grid)

        @pl.when(step == 0)
        def _():
            xchg.start(xin, xout, sems)

        @pl.when(step == (2 * n_steps) // 3)
        def _():
            xchg.forward(xin, xout, sems)

        body(*ins, *outs, *scr)

        @pl.when(step == n_steps - 1)
        def _():
            xchg.wait(xin, xout, sems)

    res = pl.pallas_call(
        carried, name=name, grid=grid, in_specs=list(in_specs) + [ANY] * n, out_specs=list(out_specs) + [ANY] * n,
        out_shape=list(out_shape) + xchg.out_shape, scratch_shapes=list(scratch_shapes) + xchg.scratch,
        compiler_params=_params(("arbitrary",) * len(grid)))(*args, *xchg.srcs)
    return res[:n_out], tuple(res[n_out:])


def _exchange_alone(xchg, name):
    def body(*refs):
        xin, xout, sems = refs[:xchg.n], refs[xchg.n:2 * xchg.n], refs[2 * xchg.n:]
        xchg.start(xin, xout, sems)
        xchg.forward(xin, xout, sems)
        xchg.wait(xin, xout, sems)

    return pl.pallas_call(body, name=name, out_shape=xchg.out_shape, in_specs=[ANY] * xchg.n, out_specs=[ANY] * xchg.n,
                          scratch_shapes=xchg.scratch)(*xchg.srcs)


def _cast_shards(shards):
    n = len(shards)

    def body(*refs):
        for i, o in zip(refs[:n], refs[n:]):
            o[...] = i[...].astype(bf16)

    return pl.pallas_call(body, name="cast_shards", out_shape=[jax.ShapeDtypeStruct(s.shape, bf16) for s in shards],
                          in_specs=[VMEM] * n, out_specs=[VMEM] * n, compiler_params=_params())(*shards)


def _allreduce_rows(v):
    r = v.shape[0]
    rp = r // N_DEV

    def body(v_ref, o_ref, parts, sums, send1, recv1, send2, recv2):
        me = _my_index()

        def piece(ref, d):
            return ref.at[pl.ds(pl.multiple_of(d * rp, 8), rp), :]

        def copy1(k, src_dev, to):
            return pltpu.make_async_remote_copy(src_ref=piece(v_ref, to), dst_ref=parts.at[src_dev], send_sem=send1.at[k],
                                                recv_sem=recv1.at[k], device_id=_coords(to), device_id_type=MESH)

        def copy2(k, owner, to):
            return pltpu.make_async_remote_copy(src_ref=sums, dst_ref=piece(o_ref, owner), send_sem=send2.at[k],
                                                recv_sem=recv2.at[k], device_id=_coords(to), device_id_type=MESH)

        for k in range(1, N_DEV):
            copy1(k, me, (me + k) % N_DEV).start()
        parts[me] = v_ref[pl.ds(pl.multiple_of(me * rp, 8), rp), :]
        for k in range(1, N_DEV):
            copy1(k, (me + N_DEV - k) % N_DEV, me).wait_recv()
        total = parts[0]
        for s in range(1, N_DEV):
            total = total + parts[s]
        sums[...] = total
        o_ref[pl.ds(pl.multiple_of(me * rp, 8), rp), :] = total
        for k in range(1, N_DEV):
            copy2(k, me, (me + k) % N_DEV).start()
        for k in range(1, N_DEV):
            copy2(k, (me + N_DEV - k) % N_DEV, me).wait_recv()
        for k in range(1, N_DEV):
            copy1(k, me, (me + k) % N_DEV).wait_send()
            copy2(k, me, (me + k) % N_DEV).wait_send()

    return pl.pallas_call(
        body, name="allreduce_small_grads", out_shape=jax.ShapeDtypeStruct(v.shape, v.dtype),
        in_specs=[VMEM], out_specs=VMEM,
        scratch_shapes=[pltpu.VMEM((N_DEV, rp, LANES), f32), pltpu.VMEM((rp, LANES), f32)]
        + [pltpu.SemaphoreType.DMA((N_DEV,))] * 4,
        compiler_params=_params(),
    )(v)


def _gather_rows(v, name):
    def body(v_ref, o_ref, send_sems, recv_sems):
        me = _my_index()
        o_ref[me] = v_ref[...]
        sends = []
        for k in range(1, N_DEV):
            peer = (me + k) % N_DEV
            rc = pltpu.make_async_remote_copy(src_ref=v_ref, dst_ref=o_ref.at[me], send_sem=send_sems.at[k],
                                              recv_sem=recv_sems.at[k], device_id=_coords(peer), device_id_type=MESH)
            rc.start()
            sends.append(rc)
        for k in range(1, N_DEV):
            src = (me + N_DEV - k) % N_DEV
            pltpu.make_async_remote_copy(src_ref=v_ref, dst_ref=o_ref.at[src], send_sem=send_sems.at[k],
                                         recv_sem=recv_sems.at[k], device_id=_coords(src), device_id_type=MESH).wait_recv()
        for rc in sends:
            rc.wait_send()

    return pl.pallas_call(
        body, name=name, out_shape=jax.ShapeDtypeStruct((N_DEV,) + v.shape, v.dtype),
        in_specs=[VMEM], out_specs=VMEM,
        scratch_shapes=[pltpu.SemaphoreType.DMA((N_DEV,)), pltpu.SemaphoreType.DMA((N_DEV,))],
        compiler_params=pltpu.CompilerParams(vmem_limit_bytes=VMEM_LIMIT),
    )(v)


def _all_to_all_rows(v, name):
    def body(v_ref, o_ref, send_sems, recv_sems):
        me = _my_index()
        o_ref[me] = v_ref[me]
        sends = []
        for k in range(1, N_DEV):
            peer = (me + k) % N_DEV
            rc = pltpu.make_async_remote_copy(src_ref=v_ref.at[peer], dst_ref=o_ref.at[me], send_sem=send_sems.at[k],
                                              recv_sem=recv_sems.at[k], device_id=_coords(peer), device_id_type=MESH)
            rc.start()
            sends.append(rc)
        for k in range(1, N_DEV):
            src = (me + N_DEV - k) % N_DEV
            pltpu.make_async_remote_copy(src_ref=v_ref.at[src], dst_ref=o_ref.at[src], send_sem=send_sems.at[k],
                                         recv_sem=recv_sems.at[k], device_id=_coords(src), device_id_type=MESH).wait_recv()
        for rc in sends:
            rc.wait_send()

    return pl.pallas_call(
        body, name=name, out_shape=jax.ShapeDtypeStruct(v.shape, v.dtype),
        in_specs=[VMEM], out_specs=VMEM,
        scratch_shapes=[pltpu.SemaphoreType.DMA((N_DEV,)), pltpu.SemaphoreType.DMA((N_DEV,))],
    )(v)


def _ada_forward(c_all, ada_w, ada_b_cols):
    def body(c_ref, w_ref, b_ref, cond_ref, o_ref):
        cond = _silu(c_ref[...])
        cond_ref[...] = cond
        for l in range(2):
            o_ref[l] = _dot(_b(cond), _b(w_ref[l])) + b_ref[l]

    return pl.pallas_call(
        body, name="ada_forward",
        out_shape=[jax.ShapeDtypeStruct((N_DEV, D_MODEL), f32), jax.ShapeDtypeStruct((2, N_DEV, 768), f32)],
        in_specs=[VMEM] * 3, out_specs=[VMEM] * 2, compiler_params=_params(),
    )(c_all, ada_w, ada_b_cols)


def _ada_backward(cond, dmod_rows):
    def body(c_ref, d_ref, o_ref):
        cb = _b(c_ref[...])
        for l in range(2):
            o_ref[l] = _dot_tn(cb, _b(d_ref[l]))

    return pl.pallas_call(
        body, name="ada_backward", out_shape=jax.ShapeDtypeStruct((2, D_MODEL, 768), f32),
        in_specs=[VMEM] * 2, out_specs=VMEM, compiler_params=_params(),
    )(cond, dmod_rows)


def _inproj_fwd(h, norm_w, sc, sh, w_in, tb, xchg=None):
    t = h.shape[0]

    def body(h_ref, nw_ref, sc_ref, sh_ref, w_ref, proj_ref, u_ref):
        n, _ = _rms(h_ref[...])
        u = _b(n * nw_ref[...] * (1.0 + sc_ref[...]) + sh_ref[...])
        u_ref[...] = u
        proj_ref[...] = _dot(u, w_ref[...])

    row = pl.BlockSpec((tb, D_MODEL), lambda i: (i, 0))
    vec = _full((1, D_MODEL))
    return _call(
        body, name="inproj_fwd", grid=(t // tb,),
        out_shape=[jax.ShapeDtypeStruct((t, P_IN), f32), jax.ShapeDtypeStruct((t, D_MODEL), bf16)],
        in_specs=[row, vec, vec, vec, _full((D_MODEL, P_IN))],
        out_specs=[pl.BlockSpec((tb, P_IN), lambda i: (i, 0)), row],
        semantics=("parallel",), args=(h, norm_w, sc, sh, w_in), xchg=xchg)


def _inproj_bwd(dparts, dh_res, h, norm_w, sc, sh, w_in, tb, xchg=None):
    t = h.shape[0]

    def body(*refs):
        parts = refs[:10]
        dres_ref, h_ref, nw_ref, sc_ref, sh_ref, w_ref = refs[10:16]
        dh_ref, dsh_ref, dsc_ref, dnw_ref = refs[16:]
        dproj = jnp.concatenate([p[...] for p in parts], axis=1)
        du = _dot_nt(dproj, w_ref[...])
        n, r = _rms(h_ref[...])
        nw = nw_ref[...]
        gain = 1.0 + sc_ref[...]
        _acc(dsh_ref, _colsum(du))
        _acc(dsc_ref, _colsum(du * n * nw))
        _acc(dnw_ref, _colsum(du * gain * n))
        dh_ref[...] = dres_ref[...] + _rms_bwd(du * nw * gain, n, r)

    row = pl.BlockSpec((tb, D_MODEL), lambda i: (i, 0))
    vec = _full((1, D_MODEL))
    part_specs = [pl.BlockSpec((tb, GROUP_W), lambda i: (i, 0))] * 9 + [pl.BlockSpec((tb, LANES), lambda i: (i, 0))]
    return _call(
        body, name="inproj_bwd", grid=(t // tb,),
        out_shape=[jax.ShapeDtypeStruct((t, D_MODEL), f32)] + [jax.ShapeDtypeStruct((1, D_MODEL), f32)] * 3,
        in_specs=part_specs + [row, row, vec, vec, vec, _full((D_MODEL, P_IN))],
        out_specs=[row, vec, vec, vec],
        semantics=("arbitrary",), xchg=xchg, args=(*dparts, dh_res, h, norm_w, sc, sh, w_in))


def _wgrad(a, b, n_blocks, name, tm, tk=512):
    t, m = a.shape
    nb = b.shape[1] // n_blocks
    tk = min(tk, t)
    nk = t // tk

    def body(a_ref, b_ref, o_ref, acc_ref):
        k = pl.program_id(2)
        p = _dot_tn(a_ref[...], b_ref[...])

        @pl.when(k == 0)
        def _():
            acc_ref[...] = p

        @pl.when(k != 0)
        def _():
            acc_ref[...] += p

        @pl.when(k == nk - 1)
        def _():
            o_ref[0] = acc_ref[...].astype(o_ref.dtype)

    return pl.pallas_call(
        body, name=name, grid=(m // tm, n_blocks, nk),
        out_shape=jax.ShapeDtypeStruct((n_blocks, m, nb), bf16),
        in_specs=[pl.BlockSpec((tk, tm), lambda i, j, k: (k, i)), pl.BlockSpec((tk, nb), lambda i, j, k: (k, j))],
        out_specs=pl.BlockSpec((1, tm, nb), lambda i, j, k: (j, i, 0)),
        scratch_shapes=[pltpu.VMEM((tm, nb), f32)],
        compiler_params=_params(("parallel", "parallel", "arbitrary")),
    )(a, b)


def _wgrad_parts(a, parts, name, tm, tk):
    t, m = a.shape
    n = sum(p.shape[1] for p in parts)
    n_parts = len(parts)
    tk = min(tk, t)
    nk = t // tk

    def body(*refs):
        a_ref, part_refs, o_ref, acc_ref = refs[0], refs[1:1 + n_parts], refs[1 + n_parts], refs[2 + n_parts]
        k = pl.program_id(1)
        p = _dot_tn(a_ref[...], jnp.concatenate([r[...] for r in part_refs], axis=1))

        @pl.when(k == 0)
        def _():
            acc_ref[...] = p

        @pl.when(k != 0)
        def _():
            acc_ref[...] += p

        @pl.when(k == nk - 1)
        def _():
            o_ref[...] = acc_ref[...].astype(o_ref.dtype)

    return pl.pallas_call(
        body, name=name, grid=(m // tm, nk),
        out_shape=jax.ShapeDtypeStruct((m, n), bf16),
        in_specs=[pl.BlockSpec((tk, tm), lambda i, k: (k, i))]
        + [pl.BlockSpec((tk, p.shape[1]), lambda i, k: (k, 0)) for p in parts],
        out_specs=pl.BlockSpec((tm, n), lambda i, k: (i, 0)),
        scratch_shapes=[pltpu.VMEM((tm, n), f32)],
        compiler_params=_params(("parallel", "arbitrary")),
    )(a, *parts)


def _pool_counts(rows, t0):
    tpos = (lax.broadcasted_iota(jnp.int32, (rows, GROUP_W), 0) + t0 + 1).astype(f32)
    grp = lax.broadcasted_iota(jnp.int32, (rows, GROUP_W), 1) // 64
    win = jnp.where(grp == 0, 2.0, jnp.where(grp == 1, 4.0, jnp.where(grp == 2, 8.0, 16.0)))
    return jnp.minimum(tpos, win), grp


def _pool_select(grp, l1, l2, l3, l4):
    return jnp.where(grp == 0, l1, jnp.where(grp == 1, l2, jnp.where(grp == 2, l3, l4)))


def _pool_means(v, halo, t0):
    tb = v.shape[0]
    ext = jnp.concatenate([halo, v], axis=0)
    n = tb + 16
    s1 = ext[1:n] + ext[0:n - 1]
    s2 = s1[2:n - 1] + s1[0:n - 3]
    s3 = s2[4:n - 3] + s2[0:n - 7]
    s4 = s3[8:n - 7] + s3[0:n - 15]
    cnt, grp = _pool_counts(tb, t0)
    wsum = _pool_select(grp, s1[15:15 + tb], s2[13:13 + tb], s3[9:9 + tb], s4[1:1 + tb])
    return wsum / cnt - v


def _pool_fwd(proj, pw_bd, scale, tb):
    t = proj.shape[0]

    def body(v_ref, vh_ref, pw_ref, sc_ref, o_ref):
        i = pl.program_id(0)
        halo = jnp.where(i > 0, vh_ref[...], 0.0)
        p = _pool_means(v_ref[...], halo, i * tb)
        o_ref[...] = _dot(_b(p), _b(pw_ref[...])) * sc_ref[...]

    return pl.pallas_call(
        body, name="pool_fwd", grid=(t // tb,),
        out_shape=jax.ShapeDtypeStruct((t, GROUP_W), f32),
        in_specs=[pl.BlockSpec((tb, GROUP_W), lambda i: (i, C_POOL)),
                  pl.BlockSpec((16, GROUP_W), lambda i: (jnp.maximum(i * (tb // 16) - 1, 0), C_POOL)),
                  _full((GROUP_W, GROUP_W)), _full((1, GROUP_W))],
        out_specs=pl.BlockSpec((tb, GROUP_W), lambda i: (i, 0)),
        compiler_params=_params(("parallel",)),
    )(proj, proj, pw_bd, scale)


def _pool_bwd(proj, dy, pw_bd, scale, tb):
    t = proj.shape[0]
    nt = t // tb
    last16 = t // 16 - 1

    def body(v_ref, vh_ref, dy_ref, dyh_ref, pw_ref, sc_ref, dv_ref, dpw_ref, dsc_ref):
        i = pl.program_id(0)
        halo = jnp.where(i > 0, vh_ref[...], 0.0)
        p = _pool_means(v_ref[...], halo, i * tb)
        pw = _b(pw_ref[...])
        sc = sc_ref[...]
        dy = dy_ref[...]
        ypre = _dot(_b(p), pw)
        _acc(dsc_ref, _colsum(dy * ypre))
        dys = _b(dy * sc)
        _acc(dpw_ref, _dot_tn(_b(p), dys))
        dp = _dot_nt(dys, pw)
        dph = _dot_nt(_b(jnp.where(i < nt - 1, dyh_ref[...], 0.0) * sc), pw)
        cnt, grp = _pool_counts(tb, i * tb)
        cnth, _ = _pool_counts(16, (i + 1) * tb)
        ext = jnp.concatenate([dp / cnt, dph / cnth], axis=0)
        n = tb + 16
        f1 = ext[0:n - 1] + ext[1:n]
        f2 = f1[0:n - 3] + f1[2:n - 1]
        f3 = f2[0:n - 7] + f2[4:n - 3]
        f4 = f3[0:n - 15] + f3[8:n - 7]
        dv_ref[...] = _b(_pool_select(grp, f1[0:tb], f2[0:tb], f3[0:tb], f4[0:tb]) - dp)

    return pl.pallas_call(
        body, name="pool_bwd", grid=(nt,),
        out_shape=[jax.ShapeDtypeStruct((t, GROUP_W), bf16), jax.ShapeDtypeStruct((GROUP_W, GROUP_W), f32),
                   jax.ShapeDtypeStruct((1, GROUP_W), f32)],
        in_specs=[pl.BlockSpec((tb, GROUP_W), lambda i: (i, C_POOL)),
                  pl.BlockSpec((16, GROUP_W), lambda i: (jnp.maximum(i * (tb // 16) - 1, 0), C_POOL)),
                  pl.BlockSpec((tb, GROUP_W), lambda i: (i, 0)),
                  pl.BlockSpec((16, GROUP_W), lambda i: (jnp.minimum((i + 1) * (tb // 16), last16), 0)),
                  _full((GROUP_W, GROUP_W)), _full((1, GROUP_W))],
        out_specs=[pl.BlockSpec((tb, GROUP_W), lambda i: (i, 0)), _full((GROUP_W, GROUP_W)), _full((1, GROUP_W))],
        compiler_params=_params(("arbitrary",)),
    )(proj, proj, dy, dy, pw_bd, scale)


def _sconv_fwd(proj, w, tb):
    t = proj.shape[0]

    def body(gb_ref, gc_ref, hh_ref, gch_ref, hhh_ref, w_ref, o_ref):
        i = pl.program_id(0)
        q = gc_ref[...] * hh_ref[...]
        qh = jnp.where(i > 0, gch_ref[...] * hhh_ref[...], 0.0)
        ext = jnp.concatenate([qh, q], axis=0)
        w = w_ref[...]
        conv = w[0:1] * ext[6:6 + tb] + w[1:2] * ext[7:7 + tb] + w[2:3] * ext[8:8 + tb]
        o_ref[...] = gb_ref[...] * conv

    def col(c):
        return pl.BlockSpec((tb, GROUP_W), lambda i: (i, c))

    def prev(c):
        return pl.BlockSpec((8, GROUP_W), lambda i: (jnp.maximum(i * (tb // 8) - 1, 0), c))

    return pl.pallas_call(
        body, name="sconv_fwd", grid=(t // tb,),
        out_shape=jax.ShapeDtypeStruct((t, GROUP_W), f32),
        in_specs=[col(C_GB), col(C_GC), col(C_HH), prev(C_GC), prev(C_HH), _full((8, GROUP_W))],
        out_specs=pl.BlockSpec((tb, GROUP_W), lambda i: (i, 0)),
        compiler_params=_params(("parallel",)),
    )(proj, proj, proj, proj, proj, w)


def _sconv_bwd(proj, dy, w, tb):
    t = proj.shape[0]
    nt = t // tb
    last8 = t // 8 - 1

    def body(gb_ref, gc_ref, hh_ref, gch_ref, hhh_ref, gbn_ref, dy_ref, dyn_ref, w_ref, dgb_ref, dgc_ref, dhh_ref, dw_ref):
        i = pl.program_id(0)
        gc, hh, gb, dy = gc_ref[...], hh_ref[...], gb_ref[...], dy_ref[...]
        q = gc * hh
        qh = jnp.where(i > 0, gch_ref[...] * hhh_ref[...], 0.0)
        ext = jnp.concatenate([qh, q], axis=0)
        w = w_ref[...]
        conv = w[0:1] * ext[6:6 + tb] + w[1:2] * ext[7:7 + tb] + w[2:3] * ext[8:8 + tb]
        dgb_ref[...] = _b(dy * conv)
        e = dy * gb
        en = jnp.where(i < nt - 1, dyn_ref[...] * gbn_ref[...], 0.0)
        exte = jnp.concatenate([e, en], axis=0)
        dq = w[2:3] * exte[0:tb] + w[1:2] * exte[1:1 + tb] + w[0:1] * exte[2:2 + tb]
        dgc_ref[...] = _b(dq * hh)
        dhh_ref[...] = _b(dq * gc)
        dw = jnp.concatenate([_colsum(e * ext[6:6 + tb]), _colsum(e * ext[7:7 + tb]), _colsum(e * ext[8:8 + tb]),
                              jnp.zeros((5, GROUP_W), f32)], axis=0)
        _acc(dw_ref, dw)

    def col(c):
        return pl.BlockSpec((tb, GROUP_W), lambda i: (i, c))

    def prev(c):
        return pl.BlockSpec((8, GROUP_W), lambda i: (jnp.maximum(i * (tb // 8) - 1, 0), c))

    def nxt(c):
        return pl.BlockSpec((8, GROUP_W), lambda i: (jnp.minimum((i + 1) * (tb // 8), last8), c))

    out = pl.BlockSpec((tb, GROUP_W), lambda i: (i, 0))
    return pl.pallas_call(
        body, name="sconv_bwd", grid=(nt,),
        out_shape=[jax.ShapeDtypeStruct((t, GROUP_W), bf16)] * 3 + [jax.ShapeDtypeStruct((8, GROUP_W), f32)],
        in_specs=[col(C_GB), col(C_GC), col(C_HH), prev(C_GC), prev(C_HH), nxt(C_GB), col(0), nxt(0), _full((8, GROUP_W))],
        out_specs=[out, out, out, _full((8, GROUP_W))],
        compiler_params=_params(("arbitrary",)),
    )(proj, proj, proj, proj, proj, proj, dy, dy, w)


def _conv4(xr, halo, w, bias):
    tb = xr.shape[0]
    ext = jnp.concatenate([halo, xr], axis=0)
    pre = w[0:1] * ext[5:5 + tb] + w[1:2] * ext[6:6 + tb] + w[2:3] * ext[7:7 + tb] + w[3:4] * ext[8:8 + tb] + bias
    return pre, ext


def _tri():
    r = lax.broadcasted_iota(jnp.int32, (SSD_CHUNK, SSD_CHUNK), 0)
    c = lax.broadcasted_iota(jnp.int32, (SSD_CHUNK, SSD_CHUNK), 1)
    return r >= c


def _lane_pick(vals):
    rows = vals[0].shape[0]
    lane = lax.broadcasted_iota(jnp.int32, (rows, LANES), 1)
    out = jnp.zeros((rows, LANES), f32)
    for h, v in enumerate(vals):
        out = jnp.where(lane == h, v, out)
    return out


def _ssd_fwd(proj, conv_w, conv_b, dt_bias, a_log, d_cols, tb, xchg=None):
    t = proj.shape[0]
    cpt = tb // SSD_CHUNK

    def body(z_ref, xs_ref, bm_ref, cm_ref, xsh_ref, bmh_ref, cmh_ref, dt_ref, cw_ref, cb_ref, dtb_ref, al_ref, dk_ref,
             o_ref, y_ref, st_ref, state):
        i = pl.program_id(0)

        @pl.when(i == 0)
        def _():
            state[...] = jnp.zeros_like(state)

        cw, cb = cw_ref[...], cb_ref[...]
        acts = []
        for j, (r, hr) in enumerate(((xs_ref, xsh_ref), (bm_ref, bmh_ref), (cm_ref, cmh_ref))):
            halo = jnp.where(i > 0, hr[...], 0.0)
            pre, _ = _conv4(r[...], halo, cw[:, j * 256:(j + 1) * 256], cb[:, j * 256:(j + 1) * 256])
            acts.append(_silu(pre))
        xs, bm, cm = acts
        dt = _softplus(dt_ref[...] + dtb_ref[...])
        a = -jnp.exp(al_ref[...])
        adt = dt * a
        tri = _tri()
        trif = tri.astype(f32)
        dk = dk_ref[...]
        for c in range(cpt):
            rows = slice(c * SSD_CHUNK, (c + 1) * SSD_CHUNK)
            acol = _dot_exact(trif, adt[rows])
            arow = acol.T
            dt_c = dt[rows]
            ys = []
            rowi = lax.broadcasted_iota(jnp.int32, (SSD_CHUNK, 1), 0)
            first = lax.broadcasted_iota(jnp.int32, (SSD_CHUNK, SSD_CHUNK), 1) < SSD_P
            for g in range(SSD_HEADS // 2):
                cols = slice(g * 128, (g + 1) * 128)
                cg, bg = _b(cm[rows, cols]), _b(bm[rows, cols])
                xg = xs[rows, cols]
                heads = (2 * g, 2 * g + 1)
                ac = [acol[:, h:h + 1] for h in heads]
                alast = [v[SSD_CHUNK - 1:SSD_CHUNK] for v in ac]
                dtw = jnp.where(first, dt_c[:, heads[0]:heads[0] + 1], dt_c[:, heads[1]:heads[1] + 1])
                eaw = jnp.where(first, jnp.exp(ac[0]), jnp.exp(ac[1]))
                wdw = jnp.where(first, jnp.exp(alast[0] - ac[0]), jnp.exp(alast[1] - ac[1]))
                xdt = xg * dtw
                xb = _b(xdt)
                gmat = _dot_nt(cg, bg)
                ydiag = []
                for k, h in enumerate(heads):
                    lm = jnp.exp(jnp.where(tri, ac[k] - arow[h:h + 1, :], -jnp.inf))
                    ydiag.append(_dot(_b(gmat * lm), xb[:, k * SSD_P:(k + 1) * SSD_P]))
                s_in = state[g]
                st_ref[c, g] = s_in
                ys.append(jnp.concatenate(ydiag, axis=1) + eaw * _dot_nt(cg, _b(s_in)) + xg * dk[:, cols])
                state[g] = jnp.where(rowi < SSD_P, jnp.exp(alast[0]), jnp.exp(alast[1])) * s_in + _dot_tn(_b(xdt * wdw), bg)
            yc = jnp.concatenate(ys, axis=1)
            y_ref[rows, :] = yc
            o_ref[rows, :] = yc * _silu(z_ref[rows, :])

    def col(c):
        return pl.BlockSpec((tb, GROUP_W), lambda i: (i, c))

    def prev(c):
        return pl.BlockSpec((8, GROUP_W), lambda i: (jnp.maximum(i * (tb // 8) - 1, 0), c))

    out = pl.BlockSpec((tb, GROUP_W), lambda i: (i, 0))
    return _call(
        body, name="ssd_fwd", grid=(t // tb,),
        out_shape=[jax.ShapeDtypeStruct((t, GROUP_W), f32), jax.ShapeDtypeStruct((t, GROUP_W), f32),
                   jax.ShapeDtypeStruct((t // SSD_CHUNK, 2, 128, 128), f32)],
        in_specs=[col(C_Z), col(C_XS), col(C_BM), col(C_CM), prev(C_XS), prev(C_BM), prev(C_CM),
                  pl.BlockSpec((tb, LANES), lambda i: (i, C_DT128)),
                  _full((8, 768)), _full((1, 768)), _full((1, LANES)), _full((1, LANES)), _full((1, GROUP_W))],
        out_specs=[out, out, pl.BlockSpec((cpt, 2, 128, 128), lambda i: (i, 0, 0, 0))],
        scratch_shapes=[pltpu.VMEM((2, 128, 128), f32)],
        semantics=("arbitrary",), xchg=xchg,
        args=(proj, proj, proj, proj, proj, proj, proj, proj, conv_w, conv_b, dt_bias, a_log, d_cols))


def _ssd_bwd(proj, dyc, y_pre, states, conv_w, conv_b, dt_bias, a_log, d_cols, tb, xchg=None):
    t = proj.shape[0]
    nt = t // tb
    cpt = tb // SSD_CHUNK

    def body(z_ref, xs_ref, bm_ref, cm_ref, xsh_ref, bmh_ref, cmh_ref, dt_ref, dy_ref, yp_ref, st_ref,
             cw_ref, cb_ref, dtb_ref, al_ref, dk_ref,
             dz_ref, dxs_ref, dbm_ref, dcm_ref, ddt_ref, dcw_ref, dcb_ref, ddtb_ref, dal_ref, ddk_ref,
             dstate, carry):
        i = pl.program_id(0)
        ti = nt - 1 - i

        @pl.when(i == 0)
        def _():
            dstate[...] = jnp.zeros_like(dstate)
            carry[...] = jnp.zeros_like(carry)

        cw, cb = cw_ref[...], cb_ref[...]
        pres, exts, acts = [], [], []
        for j, (r, hr) in enumerate(((xs_ref, xsh_ref), (bm_ref, bmh_ref), (cm_ref, cmh_ref))):
            halo = jnp.where(ti > 0, hr[...], 0.0)
            pre, ext = _conv4(r[...], halo, cw[:, j * 256:(j + 1) * 256], cb[:, j * 256:(j + 1) * 256])
            pres.append(pre)
            exts.append(ext)
            acts.append(_silu(pre))
        xs, bm, cm = acts
        raw = dt_ref[...] + dtb_ref[...]
        dt = _softplus(raw)
        a = -jnp.exp(al_ref[...])
        adt = dt * a
        tri = _tri()
        trif = tri.astype(f32)
        dk = dk_ref[...]
        z = z_ref[...]
        dyc = dy_ref[...]
        dz_ref[...] = _b(dyc * yp_ref[...] * _dsilu(z))
        dy_all = dyc * _silu(z)
        lane = lax.broadcasted_iota(jnp.int32, (1, LANES), 1)
        ddk_acc = jnp.zeros((1, LANES), f32)
        dal_acc = jnp.zeros((1, LANES), f32)
        dxs_c, dbm_c, dcm_c, ddt_c = [None] * cpt, [None] * cpt, [None] * cpt, [None] * cpt
        for c in reversed(range(cpt)):
            rows = slice(c * SSD_CHUNK, (c + 1) * SSD_CHUNK)
            acol = _dot_exact(trif, adt[rows])
            arow = acol.T
            dt_c = dt[rows]
            da_cols, da_rows, ddt_heads, dxs_groups, dbg, dcg = [], [], [], [], [], []
            rowi = lax.broadcasted_iota(jnp.int32, (SSD_CHUNK, 1), 0)
            first = lax.broadcasted_iota(jnp.int32, (SSD_CHUNK, SSD_CHUNK), 1) < SSD_P
            for g in range(SSD_HEADS // 2):
                cols = slice(g * 128, (g + 1) * 128)
                cgf, bgf = cm[rows, cols], bm[rows, cols]
                cg, bg = _b(cgf), _b(bgf)
                xg, dyg = xs[rows, cols], dy_all[rows, cols]
                s_in, dsn = st_ref[c, g], dstate[g]
                sb, dsnb = _b(s_in), _b(dsn)
                heads = (2 * g, 2 * g + 1)
                ac = [acol[:, h:h + 1] for h in heads]
                alast = [v[SSD_CHUNK - 1:SSD_CHUNK] for v in ac]
                el = [jnp.exp(v) for v in alast]
                dtw = jnp.where(first, dt_c[:, heads[0]:heads[0] + 1], dt_c[:, heads[1]:heads[1] + 1])
                eaw = jnp.where(first, jnp.exp(ac[0]), jnp.exp(ac[1]))
                wdw = jnp.where(first, jnp.exp(alast[0] - ac[0]), jnp.exp(alast[1] - ac[1]))
                xdt = xg * dtw
                xb, dyb = _b(xdt), _b(dyg)
                gmat = _dot_nt(cg, bg)
                dgs, dxh, da = None, [], []
                for k, h in enumerate(heads):
                    hc = slice(k * SSD_P, (k + 1) * SSD_P)
                    lm = jnp.exp(jnp.where(tri, ac[k] - arow[h:h + 1, :], -jnp.inf))
                    m = gmat * lm
                    dm = _dot_nt(dyb[:, hc], xb[:, hc])
                    dxh.append(_dot_tn(_b(m), dyb[:, hc]))
                    dgs = dm * lm if dgs is None else dgs + dm * lm
                    wm = dm * m
                    da.append(jnp.sum(wm, axis=1, keepdims=True))
                    da_rows.append(jnp.sum(wm, axis=0, keepdims=True))
                dgb = _b(dgs)
                dcg_g = _dot(dgb, bg)
                dbg_g = _dot_tn(dgb, cg)
                yoff = eaw * _dot_nt(cg, sb)
                dyoff = dyg * yoff
                dye = _b(dyg * eaw)
                dcg_g = dcg_g + _dot(dye, sb)
                ds_y = _dot_tn(dye, cg)
                u = _dot_nt(bg, dsnb)
                dx = jnp.concatenate(dxh, axis=1) + wdw * u
                dbg_g = dbg_g + _dot(_b(xdt * wdw), dsnb)
                xu = xdt * u * wdw
                ss = jnp.sum(dsn * s_in, axis=1, keepdims=True)
                dxx = dx * xg
                dyx = _colsum(dyg * xg)
                for k, h in enumerate(heads):
                    mine = first if k == 0 else jnp.logical_not(first)
                    dwv = jnp.sum(jnp.where(mine, xu, 0.0), axis=1, keepdims=True)
                    mine_rows = (rowi < SSD_P) if k == 0 else (rowi >= SSD_P)
                    dalast = jnp.sum(dwv, axis=0, keepdims=True) + el[k] * jnp.sum(jnp.where(mine_rows, ss, 0.0), axis=0, keepdims=True)
                    dah = da[k] + jnp.sum(jnp.where(mine, dyoff, 0.0), axis=1, keepdims=True) - dwv
                    da_cols.append(dah + jnp.where(rowi == SSD_CHUNK - 1, dalast, 0.0))
                    ddt_heads.append(jnp.sum(jnp.where(mine, dxx, 0.0), axis=1, keepdims=True))
                    ddk_acc = ddk_acc + jnp.where(lane == h, jnp.sum(jnp.where(mine[0:1], dyx, 0.0), axis=1, keepdims=True), 0.0)
                dstate[g] = jnp.where(rowi < SSD_P, el[0], el[1]) * dsn + ds_y
                dxs_groups.append(dx * dtw + dyg * dk[:, cols])
                dbg.append(dbg_g)
                dcg.append(dcg_g)
            da_blk = _lane_pick(da_cols)
            rowsel = lax.broadcasted_iota(jnp.int32, (SSD_CHUNK, SSD_CHUNK), 0)
            da_rows_blk = jnp.zeros((SSD_CHUNK, SSD_CHUNK), f32)
            for h in range(SSD_HEADS):
                da_rows_blk = jnp.where(rowsel == h, da_rows[h], da_rows_blk)
            da_blk = da_blk - da_rows_blk.T
            dadt = lax.dot_general(trif, da_blk, (((0,), (0,)), ((), ())), preferred_element_type=f32,
                                   precision=lax.Precision.HIGHEST)
            dal_acc = dal_acc + _colsum(dadt * dt_c)
            ddt_c[c] = dadt * a + _lane_pick(ddt_heads)
            dxs_c[c] = jnp.concatenate(dxs_groups, axis=1)
            dbm_c[c] = jnp.concatenate(dbg, axis=1)
            dcm_c[c] = jnp.concatenate(dcg, axis=1)
        ddt = jnp.concatenate(ddt_c, axis=0) if cpt > 1 else ddt_c[0]
        ddraw = jnp.where(lane < SSD_HEADS, ddt * jax.nn.sigmoid(raw), 0.0)
        ddt_ref[...] = _b(ddraw)
        _acc(ddtb_ref, _colsum(ddraw))
        _acc(dal_ref, jnp.where(lane < SSD_HEADS, dal_acc * a, 0.0))
        _acc(ddk_ref, ddk_acc)
        dcw_parts, dcb_parts = [], []
        for j, (dparts, out_ref) in enumerate(((dxs_c, dxs_ref), (dbm_c, dbm_ref), (dcm_c, dcm_ref))):
            dact = jnp.concatenate(dparts, axis=0) if cpt > 1 else dparts[0]
            dpre = dact * _dsilu(pres[j])
            w = cw[:, j * 256:(j + 1) * 256]
            ext = jnp.concatenate([dpre, carry[:, j * 256:(j + 1) * 256]], axis=0)
            out_ref[...] = _b(w[3:4] * ext[0:tb] + w[2:3] * ext[1:1 + tb] + w[1:2] * ext[2:2 + tb] + w[0:1] * ext[3:3 + tb])
            carry[:, j * 256:(j + 1) * 256] = dpre[0:8]
            xe = exts[j]
            dcw_parts.append(jnp.concatenate([_colsum(dpre * xe[5 + k:5 + k + tb]) for k in range(4)]
                                             + [jnp.zeros((4, GROUP_W), f32)], axis=0))
            dcb_parts.append(_colsum(dpre))
        _acc(dcw_ref, jnp.concatenate(dcw_parts, axis=1))
        _acc(dcb_ref, jnp.concatenate(dcb_parts, axis=1))

    def col(c):
        return pl.BlockSpec((tb, GROUP_W), lambda i: (nt - 1 - i, c))

    def prev(c):
        return pl.BlockSpec((8, GROUP_W), lambda i: (jnp.maximum((nt - 1 - i) * (tb // 8) - 1, 0), c))

    out = pl.BlockSpec((tb, GROUP_W), lambda i: (nt - 1 - i, 0))
    vec = _full((1, LANES))
    return _call(
        body, name="ssd_bwd", grid=(nt,),
        out_shape=[jax.ShapeDtypeStruct((t, GROUP_W), bf16)] * 4 + [jax.ShapeDtypeStruct((t, LANES), bf16),
                   jax.ShapeDtypeStruct((8, 768), f32), jax.ShapeDtypeStruct((1, 768), f32)]
        + [jax.ShapeDtypeStruct((1, LANES), f32)] * 3,
        in_specs=[col(C_Z), col(C_XS), col(C_BM), col(C_CM), prev(C_XS), prev(C_BM), prev(C_CM),
                  pl.BlockSpec((tb, LANES), lambda i: (nt - 1 - i, C_DT128)), out, out,
                  pl.BlockSpec((cpt, 2, 128, 128), lambda i: (nt - 1 - i, 0, 0, 0)),
                  _full((8, 768)), _full((1, 768)), vec, vec, _full((1, GROUP_W))],
        out_specs=[out, out, out, out, pl.BlockSpec((tb, LANES), lambda i: (nt - 1 - i, 0)),
                   _full((8, 768)), _full((1, 768)), vec, vec, vec],
        scratch_shapes=[pltpu.VMEM((2, 128, 128), f32), pltpu.VMEM((8, 768), f32)],
        semantics=("arbitrary",), xchg=xchg,
        args=(proj, proj, proj, proj, proj, proj, proj, proj, dyc, y_pre, states, conv_w, conv_b, dt_bias, a_log, d_cols))


def _s5_coeffs(are, aim, ls):
    step = jnp.exp(ls)
    mag = jnp.exp(are * step)
    th = aim * step
    lre, lim = mag * jnp.cos(th), mag * jnp.sin(th)
    den = are * are + aim * aim
    nr = lre - 1.0
    fre = (nr * are + lim * aim) / den
    fim = (lim * are - nr * aim) / den
    return step, lre, lim, den, fre, fim


def _s5_prep(are, aim, ls, bre_bd, bim_bd):
    def body(are_ref, aim_ref, ls_ref, bre_ref, bim_ref, lre_ref, lim_ref, bbr_ref, bbi_ref):
        _, lre, lim, _, fre, fim = _s5_coeffs(are_ref[...], aim_ref[...], ls_ref[...])
        lre_ref[...] = lre
        lim_ref[...] = lim
        bre, bim = bre_ref[...], bim_ref[...]
        bbr_ref[...] = fre * bre - fim * bim
        bbi_ref[...] = fre * bim + fim * bre

    col = jax.ShapeDtypeStruct((S5_N, 1), f32)
    mat = jax.ShapeDtypeStruct((S5_N, GROUP_W), f32)
    return pl.pallas_call(body, name="s5_prep", out_shape=[col, col, mat, mat], in_specs=[VMEM] * 5, out_specs=[VMEM] * 4,
                          compiler_params=_params())(are, aim, ls, bre_bd, bim_bd)


def _s5_prep_bwd(are, aim, ls, bre_bd, bim_bd, dlre, dlim, dbbr, dbbi):
    def body(are_ref, aim_ref, ls_ref, bre_ref, bim_ref, dlre_ref, dlim_ref, dbbr_ref, dbbi_ref,
             dare_ref, daim_ref, dls_ref, dbre_ref, dbim_ref):
        are, aim = are_ref[...], aim_ref[...]
        step, lre, lim, den, fre, fim = _s5_coeffs(are, aim, ls_ref[...])
        r = lax.broadcasted_iota(jnp.int32, (S5_N, GROUP_W), 0) // 64
        c = lax.broadcasted_iota(jnp.int32, (S5_N, GROUP_W), 1) // 16
        mask = r == c
        gr = jnp.where(mask, dbbr_ref[...], 0.0)
        gi = jnp.where(mask, dbbi_ref[...], 0.0)
        bre, bim = bre_ref[...], bim_ref[...]
        dbre_ref[...] = fre * gr + fim * gi
        dbim_ref[...] = fre * gi - fim * gr
        dfre = jnp.sum(bre * gr + bim * gi, axis=1, keepdims=True)
        dfim = jnp.sum(bre * gi - bim * gr, axis=1, keepdims=True)
        ire, iim = are / den, aim / den
        tre = dlre_ref[...] + ire * dfre - iim * dfim
        tim = dlim_ref[...] + ire * dfim + iim * dfre
        dzre = lre * tre + lim * tim
        dzim = lre * tim - lim * tre
        qre = (fre * are + fim * aim) / den
        qim = (fim * are - fre * aim) / den
        dare_ref[...] = step * dzre - (qre * dfre + qim * dfim)
        daim_ref[...] = step * dzim - (qre * dfim - qim * dfre)
        dls = (are * dzre + aim * dzim) * step
        sel = (lax.broadcasted_iota(jnp.int32, (S5_N, LANES), 0) // 64 == lax.broadcasted_iota(jnp.int32, (S5_N, LANES), 1)).astype(f32)
        dls_ref[...] = lax.dot_general(sel, jnp.broadcast_to(dls, (S5_N, LANES)), (((0,), (0,)), ((), ())),
                                       preferred_element_type=f32, precision=lax.Precision.HIGHEST)

    col = jax.ShapeDtypeStruct((S5_N, 1), f32)
    mat = jax.ShapeDtypeStruct((S5_N, GROUP_W), f32)
    return pl.pallas_call(body, name="s5_prep_bwd", out_shape=[col, col, jax.ShapeDtypeStruct((LANES, LANES), f32), mat, mat],
                          in_specs=[VMEM] * 9, out_specs=[VMEM] * 5, compiler_params=_params(),
                          )(are, aim, ls, bre_bd, bim_bd, dlre, dlim, dbbr, dbbi)


def _cmul(ar, ai, br, bi):
    return ar * br - ai * bi, ar * bi + ai * br


def _s5_scan(re_ref, im_ref, carry_ref, mr, mi, n_groups, reverse):
    p1 = (mr, mi)
    p2 = _cmul(*p1, *p1)
    p3 = _cmul(*p2, *p1)
    p4 = _cmul(*p2, *p2)
    p5 = _cmul(*p4, *p1)
    p6 = _cmul(*p4, *p2)
    p7 = _cmul(*p4, *p3)
    p8 = _cmul(*p4, *p4)
    pows = [p1, p2, p3, p4, p5, p6, p7, p8]
    row = lax.broadcasted_iota(jnp.int32, (8, S5_N), 0)
    tr = jnp.zeros((8, S5_N), f32)
    ti = jnp.zeros((8, S5_N), f32)
    for i in range(8):
        p = pows[7 - i] if reverse else pows[i]
        tr = jnp.where(row == i, p[0], tr)
        ti = jnp.where(row == i, p[1], ti)
    steps = []
    for k, p in ((1, p1), (2, p2), (4, p4)):
        keep = (row + k < 8) if reverse else (row >= k)
        steps.append((8 - k if reverse else k, jnp.where(keep, p[0], 0.0), jnp.where(keep, p[1], 0.0)))
    edge = 0 if reverse else 7

    def step(j, carry):
        cr, ci = carry
        g = (n_groups - 1 - j) if reverse else j
        r0 = pl.multiple_of(g * 8, 8)
        xr = re_ref[pl.ds(r0, 8), :]
        xi = im_ref[pl.ds(r0, 8), :]
        for shift, br, bi in steps:
            sr = pltpu.roll(xr, shift, 0)
            si = pltpu.roll(xi, shift, 0)
            xr, xi = xr + br * sr - bi * si, xi + br * si + bi * sr
        xr, xi = xr + tr * cr - ti * ci, xi + tr * ci + ti * cr
        re_ref[pl.ds(r0, 8), :] = xr
        im_ref[pl.ds(r0, 8), :] = xi
        return (jnp.broadcast_to(xr[edge:edge + 1, :], (8, S5_N)), jnp.broadcast_to(xi[edge:edge + 1, :], (8, S5_N)))

    cr, ci = lax.fori_loop(0, n_groups, step, (carry_ref[0], carry_ref[1]))
    carry_ref[0] = cr
    carry_ref[1] = ci


def _s5_output(u, xr, xi, ctr, cti, d):
    return _dot_nt(_b(xr), _b(ctr)) - _dot_nt(_b(xi), _b(cti)) + d * u


def _s5_fwd(proj, bbr, bbi, ctr, cti, lre, lim, d, glu_w, glu_b, tb, xchg=None):
    t = proj.shape[0]

    def body(u_ref, bbr_ref, bbi_ref, ctr_ref, cti_ref, lr_ref, li_ref, d_ref, gw_ref, gb_ref, o_ref, xr_ref, xi_ref, carry):
        @pl.when(pl.program_id(0) == 0)
        def _():
            carry[...] = jnp.zeros_like(carry)

        u = u_ref[...]
        ub = _b(u)
        xr_ref[...] = _dot_nt(ub, _b(bbr_ref[...]))
        xi_ref[...] = _dot_nt(ub, _b(bbi_ref[...]))
        _s5_scan(xr_ref, xi_ref, carry, lr_ref[...], li_ref[...], tb // 8, reverse=False)
        y = _s5_output(u, xr_ref[...], xi_ref[...], ctr_ref[...], cti_ref[...], d_ref[...])
        gl = _gelu(y)
        o_ref[...] = gl * jax.nn.sigmoid(_dot(_b(gl), _b(gw_ref[...])) + gb_ref[...])

    state = pl.BlockSpec((tb, S5_N), lambda i: (i, 0))
    return _call(
        body, name="s5_fwd", grid=(t // tb,),
        out_shape=[jax.ShapeDtypeStruct((t, GROUP_W), f32), jax.ShapeDtypeStruct((t, S5_N), f32), jax.ShapeDtypeStruct((t, S5_N), f32)],
        in_specs=[pl.BlockSpec((tb, GROUP_W), lambda i: (i, C_S5)), _full((S5_N, GROUP_W)), _full((S5_N, GROUP_W)),
                  _full((GROUP_W, S5_N)), _full((GROUP_W, S5_N)), _full((1, S5_N)), _full((1, S5_N)),
                  _full((1, GROUP_W)), _full((GROUP_W, GROUP_W)), _full((1, GROUP_W))],
        out_specs=[pl.BlockSpec((tb, GROUP_W), lambda i: (i, 0)), state, state],
        scratch_shapes=[pltpu.VMEM((2, 8, S5_N), f32)],
        semantics=("arbitrary",), xchg=xchg, args=(proj, bbr, bbi, ctr, cti, lre, lim, d, glu_w, glu_b))


def _s5_bwd(proj, dyd, xr_all, xi_all, bbr, bbi, ctr, cti, lre, lim, d, glu_w, glu_b, tb, xchg=None):
    t = proj.shape[0]
    nt = t // tb

    def body(u_ref, dy_ref, xr_ref, xi_ref, xrh_ref, xih_ref, bbr_ref, bbi_ref, ctr_ref, cti_ref, lr_ref, li_ref,
             d_ref, gw_ref, gb_ref,
             du_ref, dlr_ref, dli_ref, dbbr_ref, dbbi_ref, dctr_ref, dcti_ref, dd_ref, dgw_ref, dgb_ref,
             gr_ref, gi_ref, carry):
        i = pl.program_id(0)
        ti = nt - 1 - i

        @pl.when(i == 0)
        def _():
            carry[...] = jnp.zeros_like(carry)

        u = u_ref[...]
        ub = _b(u)
        xr, xi = xr_ref[...], xi_ref[...]
        ctr, cti = _b(ctr_ref[...]), _b(cti_ref[...])
        d = d_ref[...]
        gw = _b(gw_ref[...])
        y = _s5_output(u, xr, xi, ctr, cti, d)
        gl = _gelu(y)
        sg = jax.nn.sigmoid(_dot(_b(gl), gw) + gb_ref[...])
        dout = dy_ref[...]
        q = dout * gl * sg * (1.0 - sg)
        qb = _b(q)
        dgl = dout * sg + _dot_nt(qb, gw)
        _acc(dgw_ref, _dot_tn(_b(gl), qb))
        _acc(dgb_ref, _colsum(q))
        dyv = dgl * _dgelu(y)
        _acc(dd_ref, _colsum(dyv * u))
        dyb = _b(dyv)
        gr_ref[...] = _dot(dyb, ctr)
        gi_ref[...] = -_dot(dyb, cti)
        _acc(dctr_ref, _dot_tn(dyb, _b(xr)))
        _acc(dcti_ref, -_dot_tn(dyb, _b(xi)))
        _s5_scan(gr_ref, gi_ref, carry, lr_ref[...], -li_ref[...], tb // 8, reverse=True)
        gr, gi = gr_ref[...], gi_ref[...]
        xpr = jnp.concatenate([jnp.where(ti > 0, xrh_ref[...], 0.0), xr], axis=0)[7:7 + tb]
        xpi = jnp.concatenate([jnp.where(ti > 0, xih_ref[...], 0.0), xi], axis=0)[7:7 + tb]
        _acc(dlr_ref, _colsum(gr * xpr + gi * xpi))
        _acc(dli_ref, _colsum(gi * xpr - gr * xpi))
        grb, gib = _b(gr), _b(gi)
        _acc(dbbr_ref, _dot_tn(grb, ub))
        _acc(dbbi_ref, _dot_tn(gib, ub))
        du_ref[...] = _b(dyv * d + _dot(grb, _b(bbr_ref[...])) + _dot(gib, _b(bbi_ref[...])))

    state = pl.BlockSpec((tb, S5_N), lambda i: (nt - 1 - i, 0))
    prev = pl.BlockSpec((8, S5_N), lambda i: (jnp.maximum((nt - 1 - i) * (tb // 8) - 1, 0), 0))
    tile = pl.BlockSpec((tb, GROUP_W), lambda i: (nt - 1 - i, 0))
    return _call(
        body, name="s5_bwd", grid=(nt,),
        out_shape=[jax.ShapeDtypeStruct((t, GROUP_W), bf16), jax.ShapeDtypeStruct((1, S5_N), f32), jax.ShapeDtypeStruct((1, S5_N), f32),
                   jax.ShapeDtypeStruct((S5_N, GROUP_W), f32), jax.ShapeDtypeStruct((S5_N, GROUP_W), f32),
                   jax.ShapeDtypeStruct((GROUP_W, S5_N), f32), jax.ShapeDtypeStruct((GROUP_W, S5_N), f32),
                   jax.ShapeDtypeStruct((1, GROUP_W), f32), jax.ShapeDtypeStruct((GROUP_W, GROUP_W), f32),
                   jax.ShapeDtypeStruct((1, GROUP_W), f32)],
        in_specs=[pl.BlockSpec((tb, GROUP_W), lambda i: (nt - 1 - i, C_S5)), tile, state, state, prev, prev,
                  _full((S5_N, GROUP_W)), _full((S5_N, GROUP_W)), _full((GROUP_W, S5_N)), _full((GROUP_W, S5_N)),
                  _full((1, S5_N)), _full((1, S5_N)), _full((1, GROUP_W)), _full((GROUP_W, GROUP_W)), _full((1, GROUP_W))],
        out_specs=[tile, _full((1, S5_N)), _full((1, S5_N)), _full((S5_N, GROUP_W)), _full((S5_N, GROUP_W)),
                   _full((GROUP_W, S5_N)), _full((GROUP_W, S5_N)), _full((1, GROUP_W)), _full((GROUP_W, GROUP_W)), _full((1, GROUP_W))],
        scratch_shapes=[pltpu.VMEM((tb, S5_N), f32), pltpu.VMEM((tb, S5_N), f32), pltpu.VMEM((2, 8, S5_N), f32)],
        semantics=("arbitrary",), xchg=xchg,
        args=(proj, dyd, xr_all, xi_all, xr_all, xi_all, bbr, bbi, ctr, cti, lre, lim, d, glu_w, glu_b))


def _outproj_fwd(ys, h, bn_w, g1, w_out, tb):
    t = h.shape[0]

    def body(ya_ref, yb_ref, yc_ref, yd_ref, h_ref, bn_ref, g1_ref, w_ref, h1_ref, o_ref, gr_ref):
        bn = bn_ref[...]
        parts = []
        for g, r in enumerate((ya_ref, yb_ref, yc_ref, yd_ref)):
            n, _ = _rms(r[...])
            parts.append(n * bn[:, g * GROUP_W:(g + 1) * GROUP_W])
        groups = _b(jnp.concatenate(parts, axis=1))
        gr_ref[...] = groups
        o = _dot(groups, w_ref[...])
        o_ref[...] = o
        h1_ref[...] = h_ref[...] + g1_ref[...] * o

    grp = pl.BlockSpec((tb, GROUP_W), lambda i: (i, 0))
    row = pl.BlockSpec((tb, D_MODEL), lambda i: (i, 0))
    vec = _full((1, D_MODEL))
    return pl.pallas_call(
        body, name="outproj_fwd", grid=(t // tb,),
        out_shape=[jax.ShapeDtypeStruct((t, D_MODEL), f32), jax.ShapeDtypeStruct((t, D_MODEL), f32),
                   jax.ShapeDtypeStruct((t, D_MODEL), bf16)],
        in_specs=[grp, grp, grp, grp, row, vec, vec, _full((D_MODEL, D_MODEL))],
        out_specs=[row, row, row],
        compiler_params=_params(("parallel",)),
    )(*ys, h, bn_w, g1, w_out)


def _outproj_bwd(dh1, o, ys, bn_w, g1, w_out, tb):
    t = dh1.shape[0]

    def body(dh_ref, o_ref, ya_ref, yb_ref, yc_ref, yd_ref, bn_ref, g1_ref, w_ref,
             da_ref, db_ref, dc_ref, dd_ref, do_ref, dg1_ref, dbn_ref):
        dh = dh_ref[...]
        _acc(dg1_ref, _colsum(dh * o_ref[...]))
        do = _b(dh * g1_ref[...])
        do_ref[...] = do
        dgroups = _dot_nt(do, w_ref[...])
        bn = bn_ref[...]
        dbn = []
        for g, (r, dr) in enumerate(((ya_ref, da_ref), (yb_ref, db_ref), (yc_ref, dc_ref), (yd_ref, dd_ref))):
            n, rr = _rms(r[...])
            dgr = dgroups[:, g * GROUP_W:(g + 1) * GROUP_W]
            dbn.append(_colsum(dgr * n))
            dr[...] = _rms_bwd(dgr * bn[:, g * GROUP_W:(g + 1) * GROUP_W], n, rr)
        _acc(dbn_ref, jnp.concatenate(dbn, axis=1))

    grp = pl.BlockSpec((tb, GROUP_W), lambda i: (i, 0))
    row = pl.BlockSpec((tb, D_MODEL), lambda i: (i, 0))
    vec = _full((1, D_MODEL))
    return pl.pallas_call(
        body, name="outproj_bwd", grid=(t // tb,),
        out_shape=[jax.ShapeDtypeStruct((t, GROUP_W), f32)] * 4 + [jax.ShapeDtypeStruct((t, D_MODEL), bf16),
                   jax.ShapeDtypeStruct((1, D_MODEL), f32), jax.ShapeDtypeStruct((1, D_MODEL), f32)],
        in_specs=[row, row, grp, grp, grp, grp, vec, vec, _full((D_MODEL, D_MODEL))],
        out_specs=[grp, grp, grp, grp, row, vec, vec],
        compiler_params=_params(("arbitrary",)),
    )(dh1, o, *ys, bn_w, g1, w_out)


def _mlp_fwd(h1, norm_w, sc, sh, g2, w1, w2, tb, xchg=None):
    t = h1.shape[0]
    nh = w1.shape[0] // MLP_SLABS

    def body(h_ref, nw_ref, sc_ref, sh_ref, g2_ref, w1_ref, w2_ref, h2_ref, m_ref, v_ref, r_ref, acc):
        j = pl.program_id(1)

        @pl.when(j == 0)
        def _():
            n, _ = _rms(h_ref[...])
            v_ref[...] = _b(n * nw_ref[...] * (1.0 + sc_ref[...]) + sh_ref[...])

        v = v_ref[...]
        p = None
        for s in range(MLP_SLABS):
            ra = jnp.maximum(_dot(v, w1_ref[s]), 0.0)
            r = _b(ra * ra)
            r_ref[:, s * MLP_HB:(s + 1) * MLP_HB] = r
            q = _dot(r, w2_ref[s])
            p = q if p is None else p + q

        @pl.when(j == 0)
        def _():
            acc[...] = p

        @pl.when(j != 0)
        def _():
            acc[...] += p

        @pl.when(j == nh - 1)
        def _():
            m = acc[...]
            m_ref[...] = _b(m)
            h2_ref[...] = h_ref[...] + g2_ref[...] * m

    row = pl.BlockSpec((tb, D_MODEL), lambda i, j: (i, 0))
    hid = pl.BlockSpec((tb, MLP_SLABS * MLP_HB), lambda i, j: (i, j))
    vec = _full((1, D_MODEL))
    return _call(
        body, name="mlp_fwd", grid=(t // tb, nh),
        out_shape=[jax.ShapeDtypeStruct((t, D_MODEL), f32), jax.ShapeDtypeStruct((t, D_MODEL), bf16),
                   jax.ShapeDtypeStruct((t, D_MODEL), bf16), jax.ShapeDtypeStruct((t, N_DEV * MLP_HB), bf16)],
        in_specs=[row, vec, vec, vec, vec, pl.BlockSpec((MLP_SLABS, D_MODEL, MLP_HB), lambda i, j: (j, 0, 0)),
                  pl.BlockSpec((MLP_SLABS, MLP_HB, D_MODEL), lambda i, j: (j, 0, 0))],
        out_specs=[row, row, row, hid],
        scratch_shapes=[pltpu.VMEM((tb, D_MODEL), f32)],
        semantics=("arbitrary", "arbitrary"), xchg=xchg, args=(h1, norm_w, sc, sh, g2, w1, w2))


def _mlp_bwd(dh2, m, h1, r, norm_w, sc, sh, g2, w1, w2, tb, xchg=None):
    t = h1.shape[0]
    slabs = MLP_BWD_SLABS
    nh = w1.shape[0] // slabs

    def body(dh_ref, m_ref, h_ref, r_ref, nw_ref, sc_ref, sh_ref, g2_ref, w1_ref, w2_ref,
             dh1_ref, do_ref, da_ref, dg2_ref, dsh_ref, dsc_ref, dnw_ref, acc):
        j = pl.program_id(1)

        @pl.when(j == 0)
        def _():
            dh = dh_ref[...]
            _acc(dg2_ref, _colsum(dh * m_ref[...].astype(f32)))
            do_ref[...] = _b(dh * g2_ref[...])

        do = do_ref[...]
        p = None
        for s in range(slabs):
            cols = slice(s * MLP_HB, (s + 1) * MLP_HB)
            dr = _dot_nt(do, w2_ref[s])
            da = _b(dr * 2.0 * jnp.sqrt(r_ref[:, cols].astype(f32)))
            da_ref[:, cols] = da
            q = _dot_nt(da, w1_ref[s])
            p = q if p is None else p + q

        @pl.when(j == 0)
        def _():
            acc[...] = p

        @pl.when(j != 0)
        def _():
            acc[...] += p

        @pl.when(j == nh - 1)
        def _():
            dv = acc[...]
            n, r = _rms(h_ref[...])
            nw = nw_ref[...]
            gain = 1.0 + sc_ref[...]
            _acc(dsh_ref, _colsum(dv))
            _acc(dsc_ref, _colsum(dv * n * nw))
            _acc(dnw_ref, _colsum(dv * gain * n))
            dh1_ref[...] = dh_ref[...] + _rms_bwd(dv * nw * gain, n, r)

    row = pl.BlockSpec((tb, D_MODEL), lambda i, j: (i, 0))
    hid = pl.BlockSpec((tb, slabs * MLP_HB), lambda i, j: (i, j))
    vec = _full((1, D_MODEL))
    return _call(
        body, name="mlp_bwd", grid=(t // tb, nh),
        out_shape=[jax.ShapeDtypeStruct((t, D_MODEL), f32), jax.ShapeDtypeStruct((t, D_MODEL), bf16),
                   jax.ShapeDtypeStruct((t, N_DEV * MLP_HB), bf16)] + [jax.ShapeDtypeStruct((1, D_MODEL), f32)] * 4,
        in_specs=[row, row, row, hid, vec, vec, vec, vec,
                  pl.BlockSpec((slabs, D_MODEL, MLP_HB), lambda i, j: (j, 0, 0)),
                  pl.BlockSpec((slabs, MLP_HB, D_MODEL), lambda i, j: (j, 0, 0))],
        out_specs=[row, row, hid, vec, vec, vec, vec],
        scratch_shapes=[pltpu.VMEM((tb, D_MODEL), f32)],
        semantics=("arbitrary", "arbitrary"), xchg=xchg, args=(dh2, m, h1, r, norm_w, sc, sh, g2, w1, w2))


def _loss_head(h, target, norm_w, tb):
    t = h.shape[0]

    def body(h_ref, t_ref, w_ref, loss_ref, dh_ref, dw_ref):
        n, r = _rms(h_ref[...])
        w = w_ref[...]
        err = n * w - t_ref[...]
        part = 0.5 * jnp.sum(jnp.sum(err * err, axis=1, keepdims=True), axis=0, keepdims=True) / D_MODEL
        _acc(loss_ref, jnp.broadcast_to(part, (8, LANES)))
        dy = err / D_MODEL
        _acc(dw_ref, _colsum(dy * n))
        dh_ref[...] = _rms_bwd(dy * w, n, r)

    row = pl.BlockSpec((tb, D_MODEL), lambda i: (i, 0))
    return pl.pallas_call(
        body, name="loss_head", grid=(t // tb,),
        out_shape=[jax.ShapeDtypeStruct((8, LANES), f32), jax.ShapeDtypeStruct((t, D_MODEL), f32),
                   jax.ShapeDtypeStruct((1, D_MODEL), f32)],
        in_specs=[row, row, _full((1, D_MODEL))],
        out_specs=[_full((8, LANES)), row, _full((1, D_MODEL))],
        compiler_params=_params(("arbitrary",)),
    )(h, target, norm_w)


def _adam_math(w, g, m, v):
    m2 = ADAM_B1 * m + (1.0 - ADAM_B1) * g
    v2 = ADAM_B2 * v + (1.0 - ADAM_B2) * (g * g)
    mh = m2 / (1.0 - ADAM_B1 ** ADAM_STEP)
    vh = v2 / (1.0 - ADAM_B2 ** ADAM_STEP)
    return -ADAM_LR * (mh / (jnp.sqrt(vh) + ADAM_EPS) + ADAM_WD * w), m2, v2


def _sum_adamw(parts, w, m, v, name, rb):
    n_src, r, c = parts.shape

    def body(p_ref, w_ref, m_ref, v_ref, g_ref, d_ref, m2_ref, v2_ref):
        g = p_ref[0].astype(f32)
        for s in range(1, n_src):
            g = g + p_ref[s].astype(f32)
        g_ref[...] = g
        d, m2, v2 = _adam_math(w_ref[...], g, m_ref[...], v_ref[...])
        d_ref[...] = d
        m2_ref[...] = m2
        v2_ref[...] = v2

    blk = pl.BlockSpec((rb, c), lambda i: (i, 0))
    return pl.pallas_call(
        body, name=name, grid=(r // rb,),
        out_shape=[jax.ShapeDtypeStruct((r, c), f32)] * 4,
        in_specs=[pl.BlockSpec((n_src, rb, c), lambda i: (0, i, 0)), blk, blk, blk],
        out_specs=[blk] * 4,
        compiler_params=_params(("parallel",)),
    )(parts, w, m, v)


def _sum_adamw_layers(parts0, parts1, w, m, v, name, rb):
    n_src, r, c = parts0.shape
    nb = r // rb

    def body(p0_ref, p1_ref, w_ref, m_ref, v_ref, g_ref, d_ref, m2_ref, v2_ref):
        def update(p_ref):
            g = p_ref[0].astype(f32)
            for s in range(1, n_src):
                g = g + p_ref[s].astype(f32)
            g_ref[0] = g
            d, m2, v2 = _adam_math(w_ref[0], g, m_ref[0], v_ref[0])
            d_ref[0] = d
            m2_ref[0] = m2
            v2_ref[0] = v2

        @pl.when(pl.program_id(0) == 0)
        def _():
            update(p0_ref)

        @pl.when(pl.program_id(0) == 1)
        def _():
            update(p1_ref)

    blk = pl.BlockSpec((1, rb, c), lambda l, i: (l, i, 0))
    return pl.pallas_call(
        body, name=name, grid=(2, nb),
        out_shape=[jax.ShapeDtypeStruct((2, r, c), f32)] * 4,
        in_specs=[pl.BlockSpec((n_src, rb, c), lambda l, i: (0, jnp.where(l == 0, i, nb - 1), 0)),
                  pl.BlockSpec((n_src, rb, c), lambda l, i: (0, jnp.where(l == 1, i, 0), 0)), blk, blk, blk],
        out_specs=[blk] * 4,
        compiler_params=_params(("arbitrary", "arbitrary")),
    )(parts0, parts1, w, m, v)


def _reorder_in(w):
    pad = jnp.zeros(w.shape[:-1] + (P_IN - 2308,), w.dtype)
    return jnp.concatenate([w[..., :2048], w[..., 2052:2308], w[..., 2048:2052], pad], axis=-1)


def _unreorder_in(w):
    return jnp.concatenate([w[..., :2048], w[..., 2304:2308], w[..., 2048:2304]], axis=-1)


def _block_diag(w2d, n_blocks):
    rows, cols = w2d.shape
    tiled = jnp.tile(w2d, (1, n_blocks))
    rb = lax.broadcasted_iota(jnp.int32, tiled.shape, 0) // (rows // n_blocks)
    cb = lax.broadcasted_iota(jnp.int32, tiled.shape, 1) // cols
    return jnp.where(rb == cb, tiled, jnp.zeros_like(tiled))


def _block_diag_extract(w_bd, n_blocks):
    rows, wide = w_bd.shape
    r, c = rows // n_blocks, wide // n_blocks
    w4 = w_bd.reshape(n_blocks, r, n_blocks, c)
    idx = jnp.arange(n_blocks)
    return w4[idx, :, idx, :]


def _lanes128(v):
    return jnp.pad(v.reshape(1, -1), ((0, 0), (0, LANES - v.size)))


def _rows_of(shape):
    n = 1
    for d in shape:
        n *= d
    return -(-n // (8 * LANES)) * 8, n


def _flat_pack(arrs, row_multiple=8):
    blocks = []
    for a in arrs:
        rows, n = _rows_of(a.shape)
        blocks.append(jnp.pad(a.reshape(-1), (0, rows * LANES - n)).reshape(rows, LANES))
    total = sum(b.shape[0] for b in blocks)
    pad = -total % row_multiple
    if pad:
        blocks.append(jnp.zeros((pad, LANES), blocks[0].dtype))
    return jnp.concatenate(blocks, axis=0)


def _flat_unpack(packed, shapes):
    out, off = [], 0
    for s in shapes:
        rows, n = _rows_of(s)
        out.append(packed[off:off + rows].reshape(-1)[:n].reshape(s))
        off += rows
    return out


_W_NAMES = ['norm_mix_w', 'norm_mlp_w', 'ada_w', 'ada_b', 'w_in', 'pool_w', 'pool_scale', 'sconv_w', 'ssd_conv_w',
            'ssd_conv_b', 'ssd_dt_bias', 'ssd_a_log', 'ssd_d', 's5_a_re', 's5_a_im', 's5_log_step', 's5_b_re', 's5_b_im',
            's5_c_re', 's5_c_im', 's5_d', 's5_glu_w', 's5_glu_b', 'branch_norm_w', 'w_out', 'mlp_w1', 'mlp_w2',
            'final_norm_w']
_BIG = ('ada_w', 'w_in', 'w_out', 'mlp_w1', 'mlp_w2')
_SMALL = [n for n in _W_NAMES if n not in _BIG]
_SHARDED_SMALL = {'sconv_w': (2, 32), 'ssd_conv_w': (2, 96), 's5_glu_w': (1, 32)}


def _gather(*blocks):
    return _ChipGather(blocks)


def _scatter(*parts):
    return _Exchange(parts, gather=False)


def _layer_forward(l, h, p, w, sh_b, tb):
    first = l == 0
    (proj, u_b), got = _inproj_fwd(h, p['norm_mix_w'][l], p['sc1'][l], p['sh1'][l], w['w_in', l], tb,
                                   xchg=_gather(sh_b[1][0]) if first else None)
    if first:
        w['w_out', 0] = got[0].reshape(D_MODEL, D_MODEL)
    ya = _pool_fwd(proj, p['pool_bd'][l], p['pool_scale'][l], tb)
    yb = _sconv_fwd(proj, p['sconv_w8'][l], tb)
    (yc, yc_pre, states), got = _ssd_fwd(proj, p['ssd_conv_w8'][l], p['ssd_conv_b'][l], p['ssd_dt_bias'][l], p['ssd_a_log'][l],
                                         p['ssd_d_cols'][l], tb, xchg=_gather(sh_b[2][0]) if first else None)
    if first:
        w['w1', 0] = got[0]
    (yd, xr, xi), got = _s5_fwd(proj, p['bbr'][l], p['bbi'][l], p['ctr'][l], p['cti'][l], p['lre'][l], p['lim'][l],
                                p['s5_d'][l], p['glu_w'][l], p['glu_b'][l], tb, xchg=_gather(sh_b[3][0]) if first else None)
    if first:
        w['w2', 0] = got[0]
    ys = (ya, yb, yc, yd)
    h1, o, groups_b = _outproj_fwd(ys, h, p['branch_norm_w'][l], p['g1'][l], w['w_out', l], tb)
    (h2, m, v_b, r_b), got = _mlp_fwd(h1, p['norm_mlp_w'][l], p['sc2'][l], p['sh2'][l], p['g2'][l], w['w1', l], w['w2', l],
                                      min(MLP_TB, h.shape[0]), xchg=_gather(*[sh_b[k][1] for k in range(4)]) if first else None)
    if first:
        w['w_in', 1] = got[0].reshape(D_MODEL, P_IN)
        w['w_out', 1] = got[1].reshape(D_MODEL, D_MODEL)
        w['w1', 1], w['w2', 1] = got[2], got[3]
    saved = dict(h=h, proj=proj, u_b=u_b, ys=ys, yc_pre=yc_pre, states=states, xr=xr, xi=xi, h1=h1, o=o,
                 groups_b=groups_b, m=m, v_b=v_b, r_b=r_b)
    return h2, saved


def _layer_backward(l, dh2, s, p, w, pending, recv, tb):
    def carry(names):
        names = [n for n in names if n in pending]
        return names, (_scatter(*[pending.pop(n) for n in names]) if names else None)

    def landed(names, got):
        for n, g in zip(names, got):
            recv[n] = g

    names, xchg = carry([('w_out', 1)])
    (dh1, do2_b, da_b, dg2, dsh2, dsc2, dnw_mlp), got = _mlp_bwd(dh2, s['m'], s['h1'], s['r_b'], p['norm_mlp_w'][l], p['sc2'][l],
                                                                p['sh2'][l], p['g2'][l], w['w1', l], w['w2', l], tb, xchg=xchg)
    landed(names, got)
    pending['mlp_w2', l] = _wgrad(s['r_b'], do2_b, 1, "wgrad_w2", tm=1024, tk=1024).reshape(N_DEV, MLP_HB, D_MODEL)
    pending['mlp_w1', l] = _wgrad(s['v_b'], da_b, N_DEV, "wgrad_w1", tm=1024, tk=2048)
    dya, dyb, dyc, dyd, do1_b, dg1, dbn = _outproj_bwd(dh1, s['o'], s['ys'], p['branch_norm_w'][l], p['g1'][l], w['w_out', l], tb)
    pending['w_out', l] = _wgrad(s['groups_b'], do1_b, 1, "wgrad_wout", tm=1024, tk=1024).reshape(N_DEV, D_MODEL // N_DEV, D_MODEL)
    proj = s['proj']
    dv, dpool_bd, dpool_scale = _pool_bwd(proj, dya, p['pool_bd'][l], p['pool_scale'][l], tb)
    dgb, dgc, dhh, dsconv = _sconv_bwd(proj, dyb, p['sconv_w8'][l], tb)
    names, xchg = carry([('mlp_w1', l)] + ([('w_out', 0)] if l == 0 else []))
    (dz, dxs, dbm, dcm, ddt, dconv_w, dconv_b, ddtb, dalog, ddskip), got = _ssd_bwd(
        proj, dyc, s['yc_pre'], s['states'], p['ssd_conv_w8'][l], p['ssd_conv_b'][l], p['ssd_dt_bias'][l], p['ssd_a_log'][l],
        p['ssd_d_cols'][l], tb, xchg=xchg)
    landed(names, got)
    names, xchg = carry([('mlp_w2', l)])
    (du5, dlr, dli, dbbr, dbbi, dctr, dcti, dd5, dgw, dgb5), got = _s5_bwd(
        proj, dyd, s['xr'], s['xi'], p['bbr'][l], p['bbi'][l], p['ctr'][l], p['cti'][l], p['lre'][l], p['lim'][l],
        p['s5_d'][l], p['glu_w'][l], p['glu_b'][l], tb, xchg=xchg)
    landed(names, got)
    dare, daim, dls, dbre_bd, dbim_bd = _s5_prep_bwd(p['are_c'][l], p['aim_c'][l], p['ls_c'][l], p['bre_bd'][l], p['bim_bd'][l],
                                                     dlr.reshape(S5_N, 1), dli.reshape(S5_N, 1), dbbr, dbbi)
    dparts = (dv, dgb, dgc, dhh, dz, dxs, dbm, dcm, du5, ddt)
    pending['w_in', l] = _wgrad_parts(s['u_b'], dparts, "wgrad_win", tm=512, tk=1024).reshape(N_DEV, D_MODEL // N_DEV, P_IN)
    names, xchg = carry([('w_in', l)])
    (dh, dsh1, dsc1, dnw_mix), got = _inproj_bwd(dparts, dh1, s['h'], p['norm_mix_w'][l], p['sc1'][l], p['sh1'][l], w['w_in', l], tb,
                                                 xchg=xchg)
    landed(names, got)
    small = {
        'norm_mix_w': dnw_mix.reshape(D_MODEL), 'norm_mlp_w': dnw_mlp.reshape(D_MODEL),
        'ada_b': jnp.concatenate([dsh1, dsc1, dg1, dsh2, dsc2, dg2], axis=1).reshape(6 * D_MODEL),
        'pool_w': _block_diag_extract(dpool_bd, 4), 'pool_scale': dpool_scale.reshape(GROUP_W),
        'sconv_w': dsconv[0:3], 'ssd_conv_w': dconv_w[0:4], 'ssd_conv_b': dconv_b.reshape(768),
        'ssd_dt_bias': ddtb[0, 0:4], 'ssd_a_log': dalog[0, 0:4], 'ssd_d': ddskip[0, 0:4],
        's5_a_re': dare.reshape(16, 64), 's5_a_im': daim.reshape(16, 64), 's5_log_step': dls[0:16, 0],
        's5_b_re': _block_diag_extract(dbre_bd, 16), 's5_b_im': _block_diag_extract(dbim_bd, 16),
        's5_c_re': _block_diag_extract(dctr, 16), 's5_c_im': _block_diag_extract(dcti, 16),
        's5_d': dd5.reshape(GROUP_W), 's5_glu_w': dgw, 's5_glu_b': dgb5.reshape(GROUP_W),
        'branch_norm_w': dbn.reshape(D_MODEL),
    }
    return dh, small


def _prepare_params(a, me):
    pack_shapes = [(1, D_MODEL), (2, 3, 32), (2, 4, 96), (2, 32, GROUP_W)]
    packed = _flat_pack([a['c'], a['sconv_w'], a['ssd_conv_w'], a['s5_glu_w']])
    gathered = _gather_rows(packed, "gather_small")
    pieces = [_flat_unpack(gathered[d], pack_shapes) for d in range(N_DEV)]
    c_all = jnp.concatenate([pc[0] for pc in pieces], axis=0)
    sconv_full = jnp.concatenate([pc[1] for pc in pieces], axis=2)
    ssd_conv_full = jnp.concatenate([pc[2] for pc in pieces], axis=2)
    glu_full = jnp.concatenate([pc[3] for pc in pieces], axis=1)

    ada_b_cols = lax.dynamic_slice_in_dim(a['ada_b'], me * 768, 768, axis=1).reshape(2, 1, 768)
    cond, modrows = _ada_forward(c_all, a['ada_w'], ada_b_cols)
    mod_recv = _all_to_all_rows(modrows.transpose(1, 0, 2), "exchange_mod")
    mod = mod_recv.transpose(1, 0, 2).reshape(2, 6 * D_MODEL)
    p = {'cond': cond}
    for k, name in enumerate(('sh1', 'sc1', 'g1', 'sh2', 'sc2', 'g2')):
        p[name] = mod[:, k * D_MODEL:(k + 1) * D_MODEL].reshape(2, 1, D_MODEL)

    for name in ('norm_mix_w', 'norm_mlp_w', 'branch_norm_w'):
        p[name] = a[name].reshape(2, 1, D_MODEL)
    p['pool_bd'] = jnp.stack([_block_diag(a['pool_w'][l].reshape(GROUP_W, 64), 4) for l in range(2)])
    p['pool_scale'] = a['pool_scale'].reshape(2, 1, GROUP_W)
    p['sconv_w8'] = jnp.pad(sconv_full, ((0, 0), (0, 5), (0, 0)))
    p['ssd_conv_w8'] = jnp.pad(ssd_conv_full, ((0, 0), (0, 4), (0, 0)))
    p['ssd_conv_b'] = a['ssd_conv_b'].reshape(2, 1, 768)
    p['ssd_dt_bias'] = jnp.pad(a['ssd_dt_bias'], ((0, 0), (0, LANES - 4))).reshape(2, 1, LANES)
    p['ssd_a_log'] = jnp.pad(a['ssd_a_log'], ((0, 0), (0, LANES - 4))).reshape(2, 1, LANES)
    p['ssd_d_cols'] = jnp.repeat(a['ssd_d'], SSD_P, axis=1).reshape(2, 1, GROUP_W)
    p['are_c'] = a['s5_a_re'].reshape(2, S5_N, 1)
    p['aim_c'] = a['s5_a_im'].reshape(2, S5_N, 1)
    p['ls_c'] = jnp.repeat(a['s5_log_step'], 64, axis=1).reshape(2, S5_N, 1)
    p['bre_bd'] = jnp.stack([_block_diag(a['s5_b_re'][l].reshape(S5_N, 16), 16) for l in range(2)])
    p['bim_bd'] = jnp.stack([_block_diag(a['s5_b_im'][l].reshape(S5_N, 16), 16) for l in range(2)])
    p['ctr'] = jnp.stack([_block_diag(a['s5_c_re'][l].reshape(GROUP_W, 64), 16) for l in range(2)])
    p['cti'] = jnp.stack([_block_diag(a['s5_c_im'][l].reshape(GROUP_W, 64), 16) for l in range(2)])
    p['s5_d'] = a['s5_d'].reshape(2, 1, GROUP_W)
    p['glu_w'] = glu_full
    p['glu_b'] = a['s5_glu_b'].reshape(2, 1, GROUP_W)
    lre, lim, bbr, bbi = [], [], [], []
    for l in range(2):
        r = _s5_prep(p['are_c'][l], p['aim_c'][l], p['ls_c'][l], p['bre_bd'][l], p['bim_bd'][l])
        lre.append(r[0].reshape(1, S5_N))
        lim.append(r[1].reshape(1, S5_N))
        bbr.append(r[2])
        bbi.append(r[3])
    p['lre'], p['lim'], p['bbr'], p['bbi'] = lre, lim, bbr, bbi
    return p


def kernel(x, c, norm_mix_w, norm_mlp_w, ada_w, ada_b, w_in, pool_w, pool_scale, sconv_w, ssd_conv_w, ssd_conv_b, ssd_dt_bias, ssd_a_log, ssd_d, s5_a_re, s5_a_im, s5_log_step, s5_b_re, s5_b_im, s5_c_re, s5_c_im, s5_d, s5_glu_w, s5_glu_b, branch_norm_w, w_out, mlp_w1, mlp_w2, final_norm_w, loss_target, m_norm_mix_w, m_norm_mlp_w, m_ada_w, m_ada_b, m_w_in, m_pool_w, m_pool_scale, m_sconv_w, m_ssd_conv_w, m_ssd_conv_b, m_ssd_dt_bias, m_ssd_a_log, m_ssd_d, m_s5_a_re, m_s5_a_im, m_s5_log_step, m_s5_b_re, m_s5_b_im, m_s5_c_re, m_s5_c_im, m_s5_d, m_s5_glu_w, m_s5_glu_b, m_branch_norm_w, m_w_out, m_mlp_w1, m_mlp_w2, m_final_norm_w, v_norm_mix_w, v_norm_mlp_w, v_ada_w, v_ada_b, v_w_in, v_pool_w, v_pool_scale, v_sconv_w, v_ssd_conv_w, v_ssd_conv_b, v_ssd_dt_bias, v_ssd_a_log, v_ssd_d, v_s5_a_re, v_s5_a_im, v_s5_log_step, v_s5_b_re, v_s5_b_im, v_s5_c_re, v_s5_c_im, v_s5_d, v_s5_glu_w, v_s5_glu_b, v_branch_norm_w, v_w_out, v_mlp_w1, v_mlp_w2, v_final_norm_w):
    a = dict(locals())
    t = x.shape[1]
    tb = min(512, t)
    me = _my_index()
    p = _prepare_params(a, me)

    sh_b = _cast_shards([_reorder_in(w_in), w_out, mlp_w1, mlp_w2])
    w = {('w_in', 0): _exchange_alone(_gather(sh_b[0][0]), "gather_w_in0")[0].reshape(D_MODEL, P_IN)}

    h = x.reshape(t, D_MODEL)
    saved = []
    for l in range(2):
        h, s = _layer_forward(l, h, p, w, sh_b, tb)
        saved.append(s)
    loss_blk, dh, dfinal = _loss_head(h, loss_target.reshape(t, D_MODEL), final_norm_w.reshape(1, D_MODEL), tb)
    loss = lax.psum(loss_blk[0, 0], ("x", "y", "c"))

    pending, recv, small_parts = {}, {}, [None, None]
    for l in (1, 0):
        dh, small_parts[l] = _layer_backward(l, dh, saved[l], p, w, pending, recv, tb)
    grad_x = dh.reshape(1, t, D_MODEL)

    grads, deltas, new_m, new_v = {}, {}, {}, {}

    wmv_in = [_reorder_in(a[n]) for n in ('w_in', 'm_w_in', 'v_w_in')]
    outs = _sum_adamw_layers(recv['w_in', 0], recv['w_in', 1], *wmv_in, "adamw_w_in", 128)
    grads['w_in'], deltas['w_in'], new_m['w_in'], new_v['w_in'] = [_unreorder_in(o) for o in outs]
    for name, rb in (('w_out', 128), ('mlp_w1', 256), ('mlp_w2', 256)):
        grads[name], deltas[name], new_m[name], new_v[name] = _sum_adamw_layers(
            recv[name, 0], recv[name, 1], a[name], a['m_' + name], a['v_' + name], "adamw_" + name, rb)

    dmod = jnp.stack([small_parts[0]['ada_b'], small_parts[1]['ada_b']])
    dmod_recv = _all_to_all_rows(dmod.reshape(2, N_DEV, 768).transpose(1, 0, 2), "exchange_dmod")
    g_ada = _ada_backward(p['cond'], dmod_recv.transpose(1, 0, 2))
    grads['ada_w'], deltas['ada_w'], new_m['ada_w'], new_v['ada_w'] = _sum_adamw_layers(
        g_ada[0:1], g_ada[1:2], ada_w, m_ada_w, v_ada_w, "adamw_ada_w", 256)

    layered = [n for n in _SMALL if n != 'final_norm_w']
    full = [jnp.stack([small_parts[0][n], small_parts[1][n]]) for n in layered] + [dfinal.reshape(D_MODEL)]
    full_shapes = [f.shape for f in full]
    summed = _flat_unpack(_allreduce_rows(_flat_pack(full, row_multiple=64)), full_shapes)
    local = []
    for n, g in zip(_SMALL, summed):
        if n in _SHARDED_SMALL:
            axis, size = _SHARDED_SMALL[n]
            g = lax.dynamic_slice_in_dim(g, me * size, size, axis=axis)
        local.append(g.reshape(a[n].shape))
    local_shapes = [g.shape for g in local]
    packed = [_flat_pack(xs) for xs in (local, [a[n] for n in _SMALL], [a['m_' + n] for n in _SMALL], [a['v_' + n] for n in _SMALL])]
    outs = _sum_adamw(packed[0][None], packed[1], packed[2], packed[3], "adamw_small", packed[0].shape[0])
    for store, o in zip((grads, deltas, new_m, new_v), outs):
        for n, val in zip(_SMALL, _flat_unpack(o, local_shapes)):
            store[n] = val

    return (loss, grad_x, *[grads[n] for n in _W_NAMES], *[deltas[n] for n in _W_NAMES],
            *[new_m[n] for n in _W_NAMES], *[new_v[n] for n in _W_NAMES])
```

```python
import functools

import jax
import jax.numpy as jnp
from jax import lax
from jax.experimental import pallas as pl
from jax.experimental.pallas import tpu as pltpu

f32 = jnp.float32
bf16 = jnp.bfloat16

N_DEV = 8
D_MODEL = 1024
GROUP_W = 256
P_IN = 2432
DT_COL = 2304
SSD_CHUNK = 128
SSD_HEADS = 4
SSD_P = 64
S5_N = 1024
MLP_HB = 512
MLP_TB = 1024
MLP_SLABS = 2
MLP_BWD_SLABS = 4
EPS = 1e-6
LANES = 128
VMEM_LIMIT = 56 * 1024 * 1024
ADAM_LR, ADAM_B1, ADAM_B2, ADAM_EPS, ADAM_WD, ADAM_STEP = 0.001, 0.9, 0.999, 1e-08, 0.01, 10
POOL_WINDOWS = (2, 4, 8, 16)

C_POOL, C_GB, C_GC, C_HH, C_Z, C_XS, C_BM, C_CM, C_S5 = range(9)
C_DT128 = DT_COL // LANES

MESH = pl.DeviceIdType.MESH
ANY = pl.BlockSpec(memory_space=pl.ANY)
VMEM = pl.BlockSpec(memory_space=pltpu.VMEM)


def _dot(a, b):
    return jnp.dot(a, b, preferred_element_type=f32)


def _dot_nt(a, b):
    return lax.dot_general(a, b, (((1,), (1,)), ((), ())), preferred_element_type=f32)


def _dot_tn(a, b):
    return lax.dot_general(a, b, (((0,), (0,)), ((), ())), preferred_element_type=f32)


def _dot_exact(a, b):
    return jnp.dot(a, b, preferred_element_type=f32, precision=lax.Precision.HIGHEST)


def _b(x):
    return x.astype(bf16)


def _silu(x):
    return x * jax.nn.sigmoid(x)


def _dsilu(x):
    s = jax.nn.sigmoid(x)
    return s * (1.0 + x * (1.0 - s))


def _softplus(x):
    return jnp.maximum(x, 0.0) + jnp.log1p(jnp.exp(-jnp.abs(x)))


_GELU_K = 0.7978845608028654
_GELU_C = 0.044715


def _gelu(x):
    return 0.5 * x * (1.0 + jnp.tanh(_GELU_K * (x + _GELU_C * x * x * x)))


def _dgelu(x):
    th = jnp.tanh(_GELU_K * (x + _GELU_C * x * x * x))
    return 0.5 * (1.0 + th) + 0.5 * x * (1.0 - th * th) * _GELU_K * (1.0 + 3.0 * _GELU_C * x * x)


def _rms(h):
    r = lax.rsqrt(jnp.mean(h * h, axis=-1, keepdims=True) + EPS)
    return h * r, r


def _rms_bwd(dn, n, r):
    return r * (dn - n * jnp.mean(dn * n, axis=-1, keepdims=True))


def _colsum(x):
    return jnp.sum(x, axis=0, keepdims=True)


def _params(sem=None):
    return pltpu.CompilerParams(dimension_semantics=sem, vmem_limit_bytes=VMEM_LIMIT)


def _full(shape):
    return pl.BlockSpec(shape, lambda *_: (0,) * len(shape))


def _acc(ref, val):
    @pl.when(pl.program_id(0) == 0)
    def _():
        ref[...] = val

    @pl.when(pl.program_id(0) != 0)
    def _():
        ref[...] += val


def _me():
    return lax.axis_index("x"), lax.axis_index("y"), lax.axis_index("c")


def _my_index():
    x, y, c = _me()
    return 4 * x + 2 * y + c


def _coords(p):
    return (p // 4, (p // 2) % 2, p % 2)


class _Exchange:
    def __init__(self, srcs, gather):
        self.srcs = list(srcs)
        self.gather = gather
        self.n = len(self.srcs)
        self.out_shape = [jax.ShapeDtypeStruct(((N_DEV,) + s.shape) if gather else s.shape, s.dtype) for s in self.srcs]
        self.scratch = [pltpu.SemaphoreType.DMA((self.n, N_DEV)), pltpu.SemaphoreType.DMA((self.n, N_DEV)),
                        pltpu.SemaphoreType.DMA((self.n,))]

    def _src(self, refs, t, dev):
        return refs[t] if self.gather else refs[t].at[dev]

    def _remote(self, xin, xout, sems, t, k, me, to):
        return pltpu.make_async_remote_copy(
            src_ref=self._src(xin, t, to), dst_ref=xout[t].at[me], send_sem=sems[0].at[t, k], recv_sem=sems[1].at[t, k],
            device_id=_coords(to), device_id_type=MESH)

    def start(self, xin, xout, sems):
        me = _my_index()
        for t in range(self.n):
            pltpu.make_async_copy(self._src(xin, t, me), xout[t].at[me], sems[2].at[t]).start()
            for k in range(1, N_DEV):
                self._remote(xin, xout, sems, t, k, me, (me + k) % N_DEV).start()

    def wait(self, xin, xout, sems):
        me = _my_index()
        for t in range(self.n):
            for k in range(1, N_DEV):
                src = (me + N_DEV - k) % N_DEV
                pltpu.make_async_remote_copy(
                    src_ref=self._src(xin, t, src), dst_ref=xout[t].at[src], send_sem=sems[0].at[t, k],
                    recv_sem=sems[1].at[t, k], device_id=_coords(src), device_id_type=MESH).wait_recv()
        for t in range(self.n):
            for k in range(1, N_DEV):
                self._remote(xin, xout, sems, t, k, me, (me + k) % N_DEV).wait_send()
            pltpu.make_async_copy(self._src(xin, t, me), xout[t].at[me], sems[2].at[t]).wait()

    def forward(self, xin, xout, sems):
        pass


class _ChipGather:
    def __init__(self, srcs):
        self.srcs = list(srcs)
        self.n = len(self.srcs)
        self.out_shape = [jax.ShapeDtypeStruct((N_DEV,) + s.shape, s.dtype) for s in self.srcs]
        self.scratch = [pltpu.SemaphoreType.DMA((self.n, 7)), pltpu.SemaphoreType.DMA((self.n, 7)),
                        pltpu.SemaphoreType.DMA((self.n,))]

    @staticmethod
    def _places():
        x, y, c = _me()
        chips = [(1 - x, y), (x, 1 - y), (1 - x, 1 - y)]
        return (x, y, c), (x, y, 1 - c), chips

    @staticmethod
    def _slab(ref, dev):
        return ref.at[4 * dev[0] + 2 * dev[1] + dev[2]]

    def _copy(self, xin, xout, sems, t, k, block, to, src=None):
        return pltpu.make_async_remote_copy(
            src_ref=self._slab(xout[t], block) if src is None else src, dst_ref=self._slab(xout[t], block),
            send_sem=sems[0].at[t, k], recv_sem=sems[1].at[t, k], device_id=to, device_id_type=MESH)

    def start(self, xin, xout, sems):
        me, sibling, chips = self._places()
        for t in range(self.n):
            pltpu.make_async_copy(xin[t], self._slab(xout[t], me), sems[2].at[t]).start()
            self._copy(xin, xout, sems, t, 0, me, sibling, src=xin[t]).start()
            for j, chip in enumerate(chips):
                self._copy(xin, xout, sems, t, 1 + j, me, (*chip, me[2]), src=xin[t]).start()

    def forward(self, xin, xout, sems):
        me, sibling, chips = self._places()
        for t in range(self.n):
            for j, chip in enumerate(chips):
                self._copy(xin, xout, sems, t, 1 + j, (*chip, me[2]), me).wait_recv()
                self._copy(xin, xout, sems, t, 4 + j, (*chip, me[2]), sibling).start()

    def wait(self, xin, xout, sems):
        me, sibling, chips = self._places()
        for t in range(self.n):
            self._copy(xin, xout, sems, t, 0, sibling, me).wait_recv()
            for j, chip in enumerate(chips):
                self._copy(xin, xout, sems, t, 4 + j, (*chip, 1 - me[2]), me).wait_recv()
        for t in range(self.n):
            self._copy(xin, xout, sems, t, 0, me, sibling, src=xin[t]).wait_send()
            for j, chip in enumerate(chips):
                self._copy(xin, xout, sems, t, 1 + j, me, (*chip, me[2]), src=xin[t]).wait_send()
                self._copy(xin, xout, sems, t, 4 + j, (*chip, me[2]), sibling).wait_send()
            pltpu.make_async_copy(xin[t], self._slab(xout[t], me), sems[2].at[t]).wait()


def _call(body, *, name, grid, in_specs, out_specs, out_shape, args, semantics, scratch_shapes=(), xchg=None):
    if xchg is None:
        outs = pl.pallas_call(body, name=name, grid=grid, in_specs=in_specs, out_specs=out_specs, out_shape=out_shape,
                              scratch_shapes=list(scratch_shapes), compiler_params=_params(semantics))(*args)
        return outs, ()
    n_in, n_out, n_scr, n = len(in_specs), len(out_specs), len(scratch_shapes), xchg.n

    def carried(*refs):
        ins, xin = refs[:n_in], refs[n_in:n_in + n]
        outs, xout = refs[n_in + n:n_in + n + n_out], refs[n_in + n + n_out:n_in + 2 * n + n_out]
        scr, sems = refs[n_in + 2 * n + n_out:n_in + 2 * n + n_out + n_scr], refs[n_in + 2 * n + n_out + n_scr:]
        step = pl.program_id(0)
        for d in range(1, len(grid)):
            step = step * grid[d] + pl.program_id(d)
        n_steps = functools.reduce(lambda a, b: a * b, grid)

        @pl.when(step == 0)
        def _():
            xchg.start(xin, xout, sems)

        @pl.when(step == (2 * n_steps) // 3)
        def _():
            xchg.forward(xin, xout, sems)

        body(*ins, *outs, *scr)

        @pl.when(step == n_steps - 1)
        def _():
            xchg.wait(xin, xout, sems)

    res = pl.pallas_call(
        carried, name=name, grid=grid, in_specs=list(in_specs) + [ANY] * n, out_specs=list(out_specs) + [ANY] * n,
        out_shape=list(out_shape) + xchg.out_shape, scratch_shapes=list(scratch_shapes) + xchg.scratch,
        compiler_params=_params(("arbitrary",) * len(grid)))(*args, *xchg.srcs)
    return res[:n_out], tuple(res[n_out:])


def _exchange_alone(xchg, name):
    def body(*refs):
        xin, xout, sems = refs[:xchg.n], refs[xchg.n:2 * xchg.n], refs[2 * xchg.n:]
        xchg.start(xin, xout, sems)
        xchg.forward(xin, xout, sems)
        xchg.wait(xin, xout, sems)

    return pl.pallas_call(body, name=name, out_shape=xchg.out_shape, in_specs=[ANY] * xchg.n, out_specs=[ANY] * xchg.n,
                          scratch_shapes=xchg.scratch)(*xchg.srcs)


def _cast_shards(shards):
    n = len(shards)

    def body(*refs):
        for i, o in zip(refs[:n], refs[n:]):
            o[...] = i[...].astype(bf16)

    return pl.pallas_call(body, name="cast_shards", out_shape=[jax.ShapeDtypeStruct(s.shape, bf16) for s in shards],
                          in_specs=[VMEM] * n, out_specs=[VMEM] * n, compiler_params=_params())(*shards)


def _allreduce_rows(v):
    r = v.shape[0]
    rp = r // N_DEV

    def body(v_ref, o_ref, parts, sums, send1, recv1, send2, recv2):
        me = _my_index()

        def piece(ref, d):
            return ref.at[pl.ds(pl.multiple_of(d * rp, 8), rp), :]

        def copy1(k, src_dev, to):
            return pltpu.make_async_remote_copy(src_ref=piece(v_ref, to), dst_ref=parts.at[src_dev], send_sem=send1.at[k],
                                                recv_sem=recv1.at[k], device_id=_coords(to), device_id_type=MESH)

        def copy2(k, owner, to):
            return pltpu.make_async_remote_copy(src_ref=sums, dst_ref=piece(o_ref, owner), send_sem=send2.at[k],
                                                recv_sem=recv2.at[k], device_id=_coords(to), device_id_type=MESH)

        for k in range(1, N_DEV):
            copy1(k, me, (me + k) % N_DEV).start()
        parts[me] = v_ref[pl.ds(pl.multiple_of(me * rp, 8), rp), :]
        for k in range(1, N_DEV):
            copy1(k, (me + N_DEV - k) % N_DEV, me).wait_recv()
        total = parts[0]
        for s in range(1, N_DEV):
            total = total + parts[s]
        sums[...] = total
        o_ref[pl.ds(pl.multiple_of(me * rp, 8), rp), :] = total
        for k in range(1, N_DEV):
            copy2(k, me, (me + k) % N_DEV).start()
        for k in range(1, N_DEV):
            copy2(k, (me + N_DEV - k) % N_DEV, me).wait_recv()
        for k in range(1, N_DEV):
            copy1(k, me, (me + k) % N_DEV).wait_send()
            copy2(k, me, (me + k) % N_DEV).wait_send()

    return pl.pallas_call(
        body, name="allreduce_small_grads", out_shape=jax.ShapeDtypeStruct(v.shape, v.dtype),
        in_specs=[VMEM], out_specs=VMEM,
        scratch_shapes=[pltpu.VMEM((N_DEV, rp, LANES), f32), pltpu.VMEM((rp, LANES), f32)]
        + [pltpu.SemaphoreType.DMA((N_DEV,))] * 4,
        compiler_params=_params(),
    )(v)


def _gather_rows(v, name):
    def body(v_ref, o_ref, send_sems, recv_sems):
        me = _my_index()
        o_ref[me] = v_ref[...]
        sends = []
        for k in range(1, N_DEV):
            peer = (me + k) % N_DEV
            rc = pltpu.make_async_remote_copy(src_ref=v_ref, dst_ref=o_ref.at[me], send_sem=send_sems.at[k],
                                              recv_sem=recv_sems.at[k], device_id=_coords(peer), device_id_type=MESH)
            rc.start()
            sends.append(rc)
        for k in range(1, N_DEV):
            src = (me + N_DEV - k) % N_DEV
            pltpu.make_async_remote_copy(src_ref=v_ref, dst_ref=o_ref.at[src], send_sem=send_sems.at[k],
                                         recv_sem=recv_sems.at[k], device_id=_coords(src), device_id_type=MESH).wait_recv()
        for rc in sends:
            rc.wait_send()

    return pl.pallas_call(
        body, name=name, out_shape=jax.ShapeDtypeStruct((N_DEV,) + v.shape, v.dtype),
        in_specs=[VMEM], out_specs=VMEM,
        scratch_shapes=[pltpu.SemaphoreType.DMA((N_DEV,)), pltpu.SemaphoreType.DMA((N_DEV,))],
        compiler_params=pltpu.CompilerParams(vmem_limit_bytes=VMEM_LIMIT),
    )(v)


def _all_to_all_rows(v, name):
    def body(v_ref, o_ref, send_sems, recv_sems):
        me = _my_index()
        o_ref[me] = v_ref[me]
        sends = []
        for k in range(1, N_DEV):
            peer = (me + k) % N_DEV
            rc = pltpu.make_async_remote_copy(src_ref=v_ref.at[peer], dst_ref=o_ref.at[me], send_sem=send_sems.at[k],
                                              recv_sem=recv_sems.at[k], device_id=_coords(peer), device_id_type=MESH)
            rc.start()
            sends.append(rc)
        for k in range(1, N_DEV):
            src = (me + N_DEV - k) % N_DEV
            pltpu.make_async_remote_copy(src_ref=v_ref.at[src], dst_ref=o_ref.at[src], send_sem=send_sems.at[k],
                                         recv_sem=recv_sems.at[k], device_id=_coords(src), device_id_type=MESH).wait_recv()
        for rc in sends:
            rc.wait_send()

    return pl.pallas_call(
        body, name=name, out_shape=jax.ShapeDtypeStruct(v.shape, v.dtype),
        in_specs=[VMEM], out_specs=VMEM,
        scratch_shapes=[pltpu.SemaphoreType.DMA((N_DEV,)), pltpu.SemaphoreType.DMA((N_DEV,))],
    )(v)


def _ada_forward(c_all, ada_w, ada_b_cols):
    def body(c_ref, w_ref, b_ref, cond_ref, o_ref):
        cond = _silu(c_ref[...])
        cond_ref[...] = cond
        for l in range(2):
            o_ref[l] = _dot(_b(cond), _b(w_ref[l])) + b_ref[l]

    return pl.pallas_call(
        body, name="ada_forward",
        out_shape=[jax.ShapeDtypeStruct((N_DEV, D_MODEL), f32), jax.ShapeDtypeStruct((2, N_DEV, 768), f32)],
        in_specs=[VMEM] * 3, out_specs=[VMEM] * 2, compiler_params=_params(),
    )(c_all, ada_w, ada_b_cols)


def _ada_backward(cond, dmod_rows):
    def body(c_ref, d_ref, o_ref):
        cb = _b(c_ref[...])
        for l in range(2):
            o_ref[l] = _dot_tn(cb, _b(d_ref[l]))

    return pl.pallas_call(
        body, name="ada_backward", out_shape=jax.ShapeDtypeStruct((2, D_MODEL, 768), f32),
        in_specs=[VMEM] * 2, out_specs=VMEM, compiler_params=_params(),
    )(cond, dmod_rows)


def _inproj_fwd(h, norm_w, sc, sh, w_in, tb, xchg=None):
    t = h.shape[0]

    def body(h_ref, nw_ref, sc_ref, sh_ref, w_ref, proj_ref, u_ref):
        n, _ = _rms(h_ref[...])
        u = _b(n * nw_ref[...] * (1.0 + sc_ref[...]) + sh_ref[...])
        u_ref[...] = u
        proj_ref[...] = _dot(u, w_ref[...])

    row = pl.BlockSpec((tb, D_MODEL), lambda i: (i, 0))
    vec = _full((1, D_MODEL))
    return _call(
        body, name="inproj_fwd", grid=(t // tb,),
        out_shape=[jax.ShapeDtypeStruct((t, P_IN), f32), jax.ShapeDtypeStruct((t, D_MODEL), bf16)],
        in_specs=[row, vec, vec, vec, _full((D_MODEL, P_IN))],
        out_specs=[pl.BlockSpec((tb, P_IN), lambda i: (i, 0)), row],
        semantics=("parallel",), args=(h, norm_w, sc, sh, w_in), xchg=xchg)


def _inproj_bwd(dparts, dh_res, h, norm_w, sc, sh, w_in_t, tb, xchg=None):
    t = h.shape[0]

    def body(*refs):
        parts = refs[:10]
        dres_ref, h_ref, nw_ref, sc_ref, sh_ref, w_ref = refs[10:16]
        dh_ref, dsh_ref, dsc_ref, dnw_ref = refs[16:]
        dproj = jnp.concatenate([p[...] for p in parts], axis=1)
        du = _dot(dproj, w_ref[...])
        n, r = _rms(h_ref[...])
        nw = nw_ref[...]
        gain = 1.0 + sc_ref[...]
        _acc(dsh_ref, _colsum(du))
        _acc(dsc_ref, _colsum(du * n * nw))
        _acc(dnw_ref, _colsum(du * gain * n))
        dh_ref[...] = dres_ref[...] + _rms_bwd(du * nw * gain, n, r)

    row = pl.BlockSpec((tb, D_MODEL), lambda i: (i, 0))
    vec = _full((1, D_MODEL))
    part_specs = [pl.BlockSpec((tb, GROUP_W), lambda i: (i, 0))] * 9 + [pl.BlockSpec((tb, LANES), lambda i: (i, 0))]
    return _call(
        body, name="inproj_bwd", grid=(t // tb,),
        out_shape=[jax.ShapeDtypeStruct((t, D_MODEL), f32)] + [jax.ShapeDtypeStruct((1, D_MODEL), f32)] * 3,
        in_specs=part_specs + [row, row, vec, vec, vec, _full((P_IN, D_MODEL))],
        out_specs=[row, vec, vec, vec],
        semantics=("arbitrary",), xchg=xchg, args=(*dparts, dh_res, h, norm_w, sc, sh, w_in_t))


def _wgrad(a, b, n_blocks, name, tm, tk=512):
    t, m = a.shape
    nb = b.shape[1] // n_blocks
    tk = min(tk, t)
    nk = t // tk

    def body(a_ref, b_ref, o_ref, acc_ref):
        k = pl.program_id(2)
        p = _dot_tn(a_ref[...], b_ref[...])

        @pl.when(k == 0)
        def _():
            acc_ref[...] = p

        @pl.when(k != 0)
        def _():
            acc_ref[...] += p

        @pl.when(k == nk - 1)
        def _():
            o_ref[0] = acc_ref[...].astype(o_ref.dtype)

    return pl.pallas_call(
        body, name=name, grid=(m // tm, n_blocks, nk),
        out_shape=jax.ShapeDtypeStruct((n_blocks, m, nb), bf16),
        in_specs=[pl.BlockSpec((tk, tm), lambda i, j, k: (k, i)), pl.BlockSpec((tk, nb), lambda i, j, k: (k, j))],
        out_specs=pl.BlockSpec((1, tm, nb), lambda i, j, k: (j, i, 0)),
        scratch_shapes=[pltpu.VMEM((tm, nb), f32)],
        compiler_params=_params(("parallel", "parallel", "arbitrary")),
    )(a, b)


def _wgrad_parts(a, parts, name, tm, tk):
    t, m = a.shape
    n = sum(p.shape[1] for p in parts)
    n_parts = len(parts)
    tk = min(tk, t)
    nk = t // tk

    def body(*refs):
        a_ref, part_refs, o_ref, acc_ref = refs[0], refs[1:1 + n_parts], refs[1 + n_parts], refs[2 + n_parts]
        k = pl.program_id(1)
        p = _dot_tn(a_ref[...], jnp.concatenate([r[...] for r in part_refs], axis=1))

        @pl.when(k == 0)
        def _():
            acc_ref[...] = p

        @pl.when(k != 0)
        def _():
            acc_ref[...] += p

        @pl.when(k == nk - 1)
        def _():
            o_ref[...] = acc_ref[...].astype(o_ref.dtype)

    return pl.pallas_call(
        body, name=name, grid=(m // tm, nk),
        out_shape=jax.ShapeDtypeStruct((m, n), bf16),
        in_specs=[pl.BlockSpec((tk, tm), lambda i, k: (k, i))]
        + [pl.BlockSpec((tk, p.shape[1]), lambda i, k: (k, 0)) for p in parts],
        out_specs=pl.BlockSpec((tm, n), lambda i, k: (i, 0)),
        scratch_shapes=[pltpu.VMEM((tm, n), f32)],
        compiler_params=_params(("parallel", "arbitrary")),
    )(a, *parts)


def _pool_counts(rows, t0):
    tpos = (lax.broadcasted_iota(jnp.int32, (rows, GROUP_W), 0) + t0 + 1).astype(f32)
    grp = lax.broadcasted_iota(jnp.int32, (rows, GROUP_W), 1) // 64
    win = jnp.where(grp == 0, 2.0, jnp.where(grp == 1, 4.0, jnp.where(grp == 2, 8.0, 16.0)))
    return jnp.minimum(tpos, win), grp


def _pool_select(grp, l1, l2, l3, l4):
    return jnp.where(grp == 0, l1, jnp.where(grp == 1, l2, jnp.where(grp == 2, l3, l4)))


def _pool_means(v, halo, t0):
    tb = v.shape[0]
    ext = jnp.concatenate([halo, v], axis=0)
    n = tb + 16
    s1 = ext[1:n] + ext[0:n - 1]
    s2 = s1[2:n - 1] + s1[0:n - 3]
    s3 = s2[4:n - 3] + s2[0:n - 7]
    s4 = s3[8:n - 7] + s3[0:n - 15]
    cnt, grp = _pool_counts(tb, t0)
    wsum = _pool_select(grp, s1[15:15 + tb], s2[13:13 + tb], s3[9:9 + tb], s4[1:1 + tb])
    return wsum / cnt - v


def _pool_fwd(proj, pw_bd, scale, tb):
    t = proj.shape[0]

    def body(v_ref, vh_ref, pw_ref, sc_ref, o_ref):
        i = pl.program_id(0)
        halo = jnp.where(i > 0, vh_ref[...], 0.0)
        p = _pool_means(v_ref[...], halo, i * tb)
        o_ref[...] = _dot(_b(p), _b(pw_ref[...])) * sc_ref[...]

    return pl.pallas_call(
        body, name="pool_fwd", grid=(t // tb,),
        out_shape=jax.ShapeDtypeStruct((t, GROUP_W), f32),
        in_specs=[pl.BlockSpec((tb, GROUP_W), lambda i: (i, C_POOL)),
                  pl.BlockSpec((16, GROUP_W), lambda i: (jnp.maximum(i * (tb // 16) - 1, 0), C_POOL)),
                  _full((GROUP_W, GROUP_W)), _full((1, GROUP_W))],
        out_specs=pl.BlockSpec((tb, GROUP_W), lambda i: (i, 0)),
        compiler_params=_params(("parallel",)),
    )(proj, proj, pw_bd, scale)


def _pool_bwd(proj, dy, pw_bd, scale, tb):
    t = proj.shape[0]
    nt = t // tb
    last16 = t // 16 - 1

    def body(v_ref, vh_ref, dy_ref, dyh_ref, pw_ref, sc_ref, dv_ref, dpw_ref, dsc_ref):
        i = pl.program_id(0)
        halo = jnp.where(i > 0, vh_ref[...], 0.0)
        p = _pool_means(v_ref[...], halo, i * tb)
        pw = _b(pw_ref[...])
        sc = sc_ref[...]
        dy = dy_ref[...]
        ypre = _dot(_b(p), pw)
        _acc(dsc_ref, _colsum(dy * ypre))
        dys = _b(dy * sc)
        _acc(dpw_ref, _dot_tn(_b(p), dys))
        dp = _dot_nt(dys, pw)
        dph = _dot_nt(_b(jnp.where(i < nt - 1, dyh_ref[...], 0.0) * sc), pw)
        cnt, grp = _pool_counts(tb, i * tb)
        cnth, _ = _pool_counts(16, (i + 1) * tb)
        ext = jnp.concatenate([dp / cnt, dph / cnth], axis=0)
        n = tb + 16
        f1 = ext[0:n - 1] + ext[1:n]
        f2 = f1[0:n - 3] + f1[2:n - 1]
        f3 = f2[0:n - 7] + f2[4:n - 3]
        f4 = f3[0:n - 15] + f3[8:n - 7]
        dv_ref[...] = _b(_pool_select(grp, f1[0:tb], f2[0:tb], f3[0:tb], f4[0:tb]) - dp)

    return pl.pallas_call(
        body, name="pool_bwd", grid=(nt,),
        out_shape=[jax.ShapeDtypeStruct((t, GROUP_W), bf16), jax.ShapeDtypeStruct((GROUP_W, GROUP_W), f32),
                   jax.ShapeDtypeStruct((1, GROUP_W), f32)],
        in_specs=[pl.BlockSpec((tb, GROUP_W), lambda i: (i, C_POOL)),
                  pl.BlockSpec((16, GROUP_W), lambda i: (jnp.maximum(i * (tb // 16) - 1, 0), C_POOL)),
                  pl.BlockSpec((tb, GROUP_W), lambda i: (i, 0)),
                  pl.BlockSpec((16, GROUP_W), lambda i: (jnp.minimum((i + 1) * (tb // 16), last16), 0)),
                  _full((GROUP_W, GROUP_W)), _full((1, GROUP_W))],
        out_specs=[pl.BlockSpec((tb, GROUP_W), lambda i: (i, 0)), _full((GROUP_W, GROUP_W)), _full((1, GROUP_W))],
        compiler_params=_params(("arbitrary",)),
    )(proj, proj, dy, dy, pw_bd, scale)


def _sconv_fwd(proj, w, tb):
    t = proj.shape[0]

    def body(gb_ref, gc_ref, hh_ref, gch_ref, hhh_ref, w_ref, o_ref):
        i = pl.program_id(0)
        q = gc_ref[...] * hh_ref[...]
        qh = jnp.where(i > 0, gch_ref[...] * hhh_ref[...], 0.0)
        ext = jnp.concatenate([qh, q], axis=0)
        w = w_ref[...]
        conv = w[0:1] * ext[6:6 + tb] + w[1:2] * ext[7:7 + tb] + w[2:3] * ext[8:8 + tb]
        o_ref[...] = gb_ref[...] * conv

    def col(c):
        return pl.BlockSpec((tb, GROUP_W), lambda i: (i, c))

    def prev(c):
        return pl.BlockSpec((8, GROUP_W), lambda i: (jnp.maximum(i * (tb // 8) - 1, 0), c))

    return pl.pallas_call(
        body, name="sconv_fwd", grid=(t // tb,),
        out_shape=jax.ShapeDtypeStruct((t, GROUP_W), f32),
        in_specs=[col(C_GB), col(C_GC), col(C_HH), prev(C_GC), prev(C_HH), _full((8, GROUP_W))],
        out_specs=pl.BlockSpec((tb, GROUP_W), lambda i: (i, 0)),
        compiler_params=_params(("parallel",)),
    )(proj, proj, proj, proj, proj, w)


def _sconv_bwd(proj, dy, w, tb):
    t = proj.shape[0]
    nt = t // tb
    last8 = t // 8 - 1

    def body(gb_ref, gc_ref, hh_ref, gch_ref, hhh_ref, gbn_ref, dy_ref, dyn_ref, w_ref, dgb_ref, dgc_ref, dhh_ref, dw_ref):
        i = pl.program_id(0)
        gc, hh, gb, dy = gc_ref[...], hh_ref[...], gb_ref[...], dy_ref[...]
        q = gc * hh
        qh = jnp.where(i > 0, gch_ref[...] * hhh_ref[...], 0.0)
        ext = jnp.concatenate([qh, q], axis=0)
        w = w_ref[...]
        conv = w[0:1] * ext[6:6 + tb] + w[1:2] * ext[7:7 + tb] + w[2:3] * ext[8:8 + tb]
        dgb_ref[...] = _b(dy * conv)
        e = dy * gb
        en = jnp.where(i < nt - 1, dyn_ref[...] * gbn_ref[...], 0.0)
        exte = jnp.concatenate([e, en], axis=0)
        dq = w[2:3] * exte[0:tb] + w[1:2] * exte[1:1 + tb] + w[0:1] * exte[2:2 + tb]
        dgc_ref[...] = _b(dq * hh)
        dhh_ref[...] = _b(dq * gc)
        dw = jnp.concatenate([_colsum(e * ext[6:6 + tb]), _colsum(e * ext[7:7 + tb]), _colsum(e * ext[8:8 + tb]),
                              jnp.zeros((5, GROUP_W), f32)], axis=0)
        _acc(dw_ref, dw)

    def col(c):
        return pl.BlockSpec((tb, GROUP_W), lambda i: (i, c))

    def prev(c):
        return pl.BlockSpec((8, GROUP_W), lambda i: (jnp.maximum(i * (tb // 8) - 1, 0), c))

    def nxt(c):
        return pl.BlockSpec((8, GROUP_W), lambda i: (jnp.minimum((i + 1) * (tb // 8), last8), c))

    out = pl.BlockSpec((tb, GROUP_W), lambda i: (i, 0))
    return pl.pallas_call(
        body, name="sconv_bwd", grid=(nt,),
        out_shape=[jax.ShapeDtypeStruct((t, GROUP_W), bf16)] * 3 + [jax.ShapeDtypeStruct((8, GROUP_W), f32)],
        in_specs=[col(C_GB), col(C_GC), col(C_HH), prev(C_GC), prev(C_HH), nxt(C_GB), col(0), nxt(0), _full((8, GROUP_W))],
        out_specs=[out, out, out, _full((8, GROUP_W))],
        compiler_params=_params(("arbitrary",)),
    )(proj, proj, proj, proj, proj, proj, dy, dy, w)


def _conv4(xr, halo, w, bias):
    tb = xr.shape[0]
    ext = jnp.concatenate([halo, xr], axis=0)
    pre = w[0:1] * ext[5:5 + tb] + w[1:2] * ext[6:6 + tb] + w[2:3] * ext[7:7 + tb] + w[3:4] * ext[8:8 + tb] + bias
    return pre, ext


def _tri():
    r = lax.broadcasted_iota(jnp.int32, (SSD_CHUNK, SSD_CHUNK), 0)
    c = lax.broadcasted_iota(jnp.int32, (SSD_CHUNK, SSD_CHUNK), 1)
    return r >= c


def _lane_pick(vals):
    rows = vals[0].shape[0]
    lane = lax.broadcasted_iota(jnp.int32, (rows, LANES), 1)
    out = jnp.zeros((rows, LANES), f32)
    for h, v in enumerate(vals):
        out = jnp.where(lane == h, v, out)
    return out


def _ssd_fwd(proj, conv_w, conv_b, dt_bias, a_log, d_cols, tb, xchg=None):
    t = proj.shape[0]
    cpt = tb // SSD_CHUNK

    def body(z_ref, xs_ref, bm_ref, cm_ref, xsh_ref, bmh_ref, cmh_ref, dt_ref, cw_ref, cb_ref, dtb_ref, al_ref, dk_ref,
             o_ref, y_ref, st_ref, state):
        i = pl.program_id(0)

        @pl.when(i == 0)
        def _():
            state[...] = jnp.zeros_like(state)

        cw, cb = cw_ref[...], cb_ref[...]
        acts = []
        for j, (r, hr) in enumerate(((xs_ref, xsh_ref), (bm_ref, bmh_ref), (cm_ref, cmh_ref))):
            halo = jnp.where(i > 0, hr[...], 0.0)
            pre, _ = _conv4(r[...], halo, cw[:, j * 256:(j + 1) * 256], cb[:, j * 256:(j + 1) * 256])
            acts.append(_silu(pre))
        xs, bm, cm = acts
        dt = _softplus(dt_ref[...] + dtb_ref[...])
        a = -jnp.exp(al_ref[...])
        adt = dt * a
        tri = _tri()
        trif = tri.astype(f32)
        dk = dk_ref[...]
        for c in range(cpt):
            rows = slice(c * SSD_CHUNK, (c + 1) * SSD_CHUNK)
            acol = _dot_exact(trif, adt[rows])
            arow = acol.T
            dt_c = dt[rows]
            ys = []
            rowi = lax.broadcasted_iota(jnp.int32, (SSD_CHUNK, 1), 0)
            first = lax.broadcasted_iota(jnp.int32, (SSD_CHUNK, SSD_CHUNK), 1) < SSD_P
            for g in range(SSD_HEADS // 2):
                cols = slice(g * 128, (g + 1) * 128)
                cg, bg = _b(cm[rows, cols]), _b(bm[rows, cols])
                xg = xs[rows, cols]
                heads = (2 * g, 2 * g + 1)
                ac = [acol[:, h:h + 1] for h in heads]
                alast = [v[SSD_CHUNK - 1:SSD_CHUNK] for v in ac]
                dtw = jnp.where(first, dt_c[:, heads[0]:heads[0] + 1], dt_c[:, heads[1]:heads[1] + 1])
                eaw = jnp.where(first, jnp.exp(ac[0]), jnp.exp(ac[1]))
                wdw = jnp.where(first, jnp.exp(alast[0] - ac[0]), jnp.exp(alast[1] - ac[1]))
                xdt = xg * dtw
                xb = _b(xdt)
                gmat = _dot_nt(cg, bg)
                ydiag = []
                for k, h in enumerate(heads):
                    lm = jnp.exp(jnp.where(tri, ac[k] - arow[h:h + 1, :], -jnp.inf))
                    ydiag.append(_dot(_b(gmat * lm), xb[:, k * SSD_P:(k + 1) * SSD_P]))
                s_in = state[g]
                st_ref[c, g] = s_in
                ys.append(jnp.concatenate(ydiag, axis=1) + eaw * _dot_nt(cg, _b(s_in)) + xg * dk[:, cols])
                state[g] = jnp.where(rowi < SSD_P, jnp.exp(alast[0]), jnp.exp(alast[1])) * s_in + _dot_tn(_b(xdt * wdw), bg)
            yc = jnp.concatenate(ys, axis=1)
            y_ref[rows, :] = yc
            o_ref[rows, :] = yc * _silu(z_ref[rows, :])

    def col(c):
        return pl.BlockSpec((tb, GROUP_W), lambda i: (i, c))

    def prev(c):
        return pl.BlockSpec((8, GROUP_W), lambda i: (jnp.maximum(i * (tb // 8) - 1, 0), c))

    out = pl.BlockSpec((tb, GROUP_W), lambda i: (i, 0))
    return _call(
        body, name="ssd_fwd", grid=(t // tb,),
        out_shape=[jax.ShapeDtypeStruct((t, GROUP_W), f32), jax.ShapeDtypeStruct((t, GROUP_W), f32),
                   jax.ShapeDtypeStruct((t // SSD_CHUNK, 2, 128, 128), f32)],
        in_specs=[col(C_Z), col(C_XS), col(C_BM), col(C_CM), prev(C_XS), prev(C_BM), prev(C_CM),
                  pl.BlockSpec((tb, LANES), lambda i: (i, C_DT128)),
                  _full((8, 768)), _full((1, 768)), _full((1, LANES)), _full((1, LANES)), _full((1, GROUP_W))],
        out_specs=[out, out, pl.BlockSpec((cpt, 2, 128, 128), lambda i: (i, 0, 0, 0))],
        scratch_shapes=[pltpu.VMEM((2, 128, 128), f32)],
        semantics=("arbitrary",), xchg=xchg,
        args=(proj, proj, proj, proj, proj, proj, proj, proj, conv_w, conv_b, dt_bias, a_log, d_cols))


def _ssd_bwd(proj, dyc, y_pre, states, conv_w, conv_b, dt_bias, a_log, d_cols, tb, xchg=None):
    t = proj.shape[0]
    nt = t // tb
    cpt = tb // SSD_CHUNK

    def body(z_ref, xs_ref, bm_ref, cm_ref, xsh_ref, bmh_ref, cmh_ref, dt_ref, dy_ref, yp_ref, st_ref,
             cw_ref, cb_ref, dtb_ref, al_ref, dk_ref,
             dz_ref, dxs_ref, dbm_ref, dcm_ref, ddt_ref, dcw_ref, dcb_ref, ddtb_ref, dal_ref, ddk_ref,
             dstate, carry):
        i = pl.program_id(0)
        ti = nt - 1 - i

        @pl.when(i == 0)
        def _():
            dstate[...] = jnp.zeros_like(dstate)
            carry[...] = jnp.zeros_like(carry)

        cw, cb = cw_ref[...], cb_ref[...]
        pres, exts, acts = [], [], []
        for j, (r, hr) in enumerate(((xs_ref, xsh_ref), (bm_ref, bmh_ref), (cm_ref, cmh_ref))):
            halo = jnp.where(ti > 0, hr[...], 0.0)
            pre, ext = _conv4(r[...], halo, cw[:, j * 256:(j + 1) * 256], cb[:, j * 256:(j + 1) * 256])
            pres.append(pre)
            exts.append(ext)
            acts.append(_silu(pre))
        xs, bm, cm = acts
        raw = dt_ref[...] + dtb_ref[...]
        dt = _softplus(raw)
        a = -jnp.exp(al_ref[...])
        adt = dt * a
        tri = _tri()
        trif = tri.astype(f32)
        dk = dk_ref[...]
        z = z_ref[...]
        dyc = dy_ref[...]
        dz_ref[...] = _b(dyc * yp_ref[...] * _dsilu(z))
        dy_all = dyc * _silu(z)
        lane = lax.broadcasted_iota(jnp.int32, (1, LANES), 1)
        ddk_acc = jnp.zeros((1, LANES), f32)
        dal_acc = jnp.zeros((1, LANES), f32)
        dxs_c, dbm_c, dcm_c, ddt_c = [None] * cpt, [None] * cpt, [None] * cpt, [None] * cpt
        for c in reversed(range(cpt)):
            rows = slice(c * SSD_CHUNK, (c + 1) * SSD_CHUNK)
            acol = _dot_exact(trif, adt[rows])
            arow = acol.T
            dt_c = dt[rows]
            da_cols, da_rows, ddt_heads, dxs_groups, dbg, dcg = [], [], [], [], [], []
            rowi = lax.broadcasted_iota(jnp.int32, (SSD_CHUNK, 1), 0)
            first = lax.broadcasted_iota(jnp.int32, (SSD_CHUNK, SSD_CHUNK), 1) < SSD_P
            for g in range(SSD_HEADS // 2):
                cols = slice(g * 128, (g + 1) * 128)
                cgf, bgf = cm[rows, cols], bm[rows, cols]
                cg, bg = _b(cgf), _b(bgf)
                xg, dyg = xs[rows, cols], dy_all[rows, cols]
                s_in, dsn = st_ref[c, g], dstate[g]
                sb, dsnb = _b(s_in), _b(dsn)
                heads = (2 * g, 2 * g + 1)
                ac = [acol[:, h:h + 1] for h in heads]
                alast = [v[SSD_CHUNK - 1:SSD_CHUNK] for v in ac]
                el = [jnp.exp(v) for v in alast]
                dtw = jnp.where(first, dt_c[:, heads[0]:heads[0] + 1], dt_c[:, heads[1]:heads[1] + 1])
                eaw = jnp.where(first, jnp.exp(ac[0]), jnp.exp(ac[1]))
                wdw = jnp.where(first, jnp.exp(alast[0] - ac[0]), jnp.exp(alast[1] - ac[1]))
                xdt = xg * dtw
                xb, dyb = _b(xdt), _b(dyg)
                gmat = _dot_nt(cg, bg)
                dgs, dxh, da = None, [], []
                for k, h in enumerate(heads):
                    hc = slice(k * SSD_P, (k + 1) * SSD_P)
                    lm = jnp.exp(jnp.where(tri, ac[k] - arow[h:h + 1, :], -jnp.inf))
                    m = gmat * lm
                    dm = _dot_nt(dyb[:, hc], xb[:, hc])
                    dxh.append(_dot_tn(_b(m), dyb[:, hc]))
                    dgs = dm * lm if dgs is None else dgs + dm * lm
                    wm = dm * m
                    da.append(jnp.sum(wm, axis=1, keepdims=True))
                    da_rows.append(jnp.sum(wm, axis=0, keepdims=True))
                dgb = _b(dgs)
                dcg_g = _dot(dgb, bg)
                dbg_g = _dot_tn(dgb, cg)
                yoff = eaw * _dot_nt(cg, sb)
                dyoff = dyg * yoff
                dye = _b(dyg * eaw)
                dcg_g = dcg_g + _dot(dye, sb)
                ds_y = _dot_tn(dye, cg)
                u = _dot_nt(bg, dsnb)
                dx = jnp.concatenate(dxh, axis=1) + wdw * u
                dbg_g = dbg_g + _dot(_b(xdt * wdw), dsnb)
                xu = xdt * u * wdw
                ss = jnp.sum(dsn * s_in, axis=1, keepdims=True)
                dxx = dx * xg
                dyx = _colsum(dyg * xg)
                for k, h in enumerate(heads):
                    mine = first if k == 0 else jnp.logical_not(first)
                    dwv = jnp.sum(jnp.where(mine, xu, 0.0), axis=1, keepdims=True)
                    mine_rows = (rowi < SSD_P) if k == 0 else (rowi >= SSD_P)
                    dalast = jnp.sum(dwv, axis=0, keepdims=True) + el[k] * jnp.sum(jnp.where(mine_rows, ss, 0.0), axis=0, keepdims=True)
                    dah = da[k] + jnp.sum(jnp.where(mine, dyoff, 0.0), axis=1, keepdims=True) - dwv
                    da_cols.append(dah + jnp.where(rowi == SSD_CHUNK - 1, dalast, 0.0))
                    ddt_heads.append(jnp.sum(jnp.where(mine, dxx, 0.0), axis=1, keepdims=True))
                    ddk_acc = ddk_acc + jnp.where(lane == h, jnp.sum(jnp.where(mine[0:1], dyx, 0.0), axis=1, keepdims=True), 0.0)
                dstate[g] = jnp.where(rowi < SSD_P, el[0], el[1]) * dsn + ds_y
                dxs_groups.append(dx * dtw + dyg * dk[:, cols])
                dbg.append(dbg_g)
                dcg.append(dcg_g)
            da_blk = _lane_pick(da_cols)
            rowsel = lax.broadcasted_iota(jnp.int32, (SSD_CHUNK, SSD_CHUNK), 0)
            da_rows_blk = jnp.zeros((SSD_CHUNK, SSD_CHUNK), f32)
            for h in range(SSD_HEADS):
                da_rows_blk = jnp.where(rowsel == h, da_rows[h], da_rows_blk)
            da_blk = da_blk - da_rows_blk.T
            dadt = lax.dot_general(trif, da_blk, (((0,), (0,)), ((), ())), preferred_element_type=f32,
                                   precision=lax.Precision.HIGHEST)
            dal_acc = dal_acc + _colsum(dadt * dt_c)
            ddt_c[c] = dadt * a + _lane_pick(ddt_heads)
            dxs_c[c] = jnp.concatenate(dxs_groups, axis=1)
            dbm_c[c] = jnp.concatenate(dbg, axis=1)
            dcm_c[c] = jnp.concatenate(dcg, axis=1)
        ddt = jnp.concatenate(ddt_c, axis=0) if cpt > 1 else ddt_c[0]
        ddraw = jnp.where(lane < SSD_HEADS, ddt * jax.nn.sigmoid(raw), 0.0)
        ddt_ref[...] = _b(ddraw)
        _acc(ddtb_ref, _colsum(ddraw))
        _acc(dal_ref, jnp.where(lane < SSD_HEADS, dal_acc * a, 0.0))
        _acc(ddk_ref, ddk_acc)
        dcw_parts, dcb_parts = [], []
        for j, (dparts, out_ref) in enumerate(((dxs_c, dxs_ref), (dbm_c, dbm_ref), (dcm_c, dcm_ref))):
            dact = jnp.concatenate(dparts, axis=0) if cpt > 1 else dparts[0]
            dpre = dact * _dsilu(pres[j])
            w = cw[:, j * 256:(j + 1) * 256]
            ext = jnp.concatenate([dpre, carry[:, j * 256:(j + 1) * 256]], axis=0)
            out_ref[...] = _b(w[3:4] * ext[0:tb] + w[2:3] * ext[1:1 + tb] + w[1:2] * ext[2:2 + tb] + w[0:1] * ext[3:3 + tb])
            carry[:, j * 256:(j + 1) * 256] = dpre[0:8]
            xe = exts[j]
            dcw_parts.append(jnp.concatenate([_colsum(dpre * xe[5 + k:5 + k + tb]) for k in range(4)]
                                             + [jnp.zeros((4, GROUP_W), f32)], axis=0))
            dcb_parts.append(_colsum(dpre))
        _acc(dcw_ref, jnp.concatenate(dcw_parts, axis=1))
        _acc(dcb_ref, jnp.concatenate(dcb_parts, axis=1))

    def col(c):
        return pl.BlockSpec((tb, GROUP_W), lambda i: (nt - 1 - i, c))

    def prev(c):
        return pl.BlockSpec((8, GROUP_W), lambda i: (jnp.maximum((nt - 1 - i) * (tb // 8) - 1, 0), c))

    out = pl.BlockSpec((tb, GROUP_W), lambda i: (nt - 1 - i, 0))
    vec = _full((1, LANES))
    return _call(
        body, name="ssd_bwd", grid=(nt,),
        out_shape=[jax.ShapeDtypeStruct((t, GROUP_W), bf16)] * 4 + [jax.ShapeDtypeStruct((t, LANES), bf16),
                   jax.ShapeDtypeStruct((8, 768), f32), jax.ShapeDtypeStruct((1, 768), f32)]
        + [jax.ShapeDtypeStruct((1, LANES), f32)] * 3,
        in_specs=[col(C_Z), col(C_XS), col(C_BM), col(C_CM), prev(C_XS), prev(C_BM), prev(C_CM),
                  pl.BlockSpec((tb, LANES), lambda i: (nt - 1 - i, C_DT128)), out, out,
                  pl.BlockSpec((cpt, 2, 128, 128), lambda i: (nt - 1 - i, 0, 0, 0)),
                  _full((8, 768)), _full((1, 768)), vec, vec, _full((1, GROUP_W))],
        out_specs=[out, out, out, out, pl.BlockSpec((tb, LANES), lambda i: (nt - 1 - i, 0)),
                   _full((8, 768)), _full((1, 768)), vec, vec, vec],
        scratch_shapes=[pltpu.VMEM((2, 128, 128), f32), pltpu.VMEM((8, 768), f32)],
        semantics=("arbitrary",), xchg=xchg,
        args=(proj, proj, proj, proj, proj, proj, proj, proj, dyc, y_pre, states, conv_w, conv_b, dt_bias, a_log, d_cols))


def _s5_coeffs(are, aim, ls):
    step = jnp.exp(ls)
    mag = jnp.exp(are * step)
    th = aim * step
    lre, lim = mag * jnp.cos(th), mag * jnp.sin(th)
    den = are * are + aim * aim
    nr = lre - 1.0
    fre = (nr * are + lim * aim) / den
    fim = (lim * are - nr * aim) / den
    return step, lre, lim, den, fre, fim


def _s5_prep(are, aim, ls, bre_bd, bim_bd):
    def body(are_ref, aim_ref, ls_ref, bre_ref, bim_ref, lre_ref, lim_ref, bbr_ref, bbi_ref):
        _, lre, lim, _, fre, fim = _s5_coeffs(are_ref[...], aim_ref[...], ls_ref[...])
        lre_ref[...] = lre
        lim_ref[...] = lim
        bre, bim = bre_ref[...], bim_ref[...]
        bbr_ref[...] = fre * bre - fim * bim
        bbi_ref[...] = fre * bim + fim * bre

    col = jax.ShapeDtypeStruct((S5_N, 1), f32)
    mat = jax.ShapeDtypeStruct((S5_N, GROUP_W), f32)
    return pl.pallas_call(body, name="s5_prep", out_shape=[col, col, mat, mat], in_specs=[VMEM] * 5, out_specs=[VMEM] * 4,
                          compiler_params=_params())(are, aim, ls, bre_bd, bim_bd)


def _s5_prep_bwd(are, aim, ls, bre_bd, bim_bd, dlre, dlim, dbbr, dbbi):
    def body(are_ref, aim_ref, ls_ref, bre_ref, bim_ref, dlre_ref, dlim_ref, dbbr_ref, dbbi_ref,
             dare_ref, daim_ref, dls_ref, dbre_ref, dbim_ref):
        are, aim = are_ref[...], aim_ref[...]
        step, lre, lim, den, fre, fim = _s5_coeffs(are, aim, ls_ref[...])
        r = lax.broadcasted_iota(jnp.int32, (S5_N, GROUP_W), 0) // 64
        c = lax.broadcasted_iota(jnp.int32, (S5_N, GROUP_W), 1) // 16
        mask = r == c
        gr = jnp.where(mask, dbbr_ref[...], 0.0)
        gi = jnp.where(mask, dbbi_ref[...], 0.0)
        bre, bim = bre_ref[...], bim_ref[...]
        dbre_ref[...] = fre * gr + fim * gi
        dbim_ref[...] = fre * gi - fim * gr
        dfre = jnp.sum(bre * gr + bim * gi, axis=1, keepdims=True)
        dfim = jnp.sum(bre * gi - bim * gr, axis=1, keepdims=True)
        ire, iim = are / den, aim / den
        tre = dlre_ref[...] + ire * dfre - iim * dfim
        tim = dlim_ref[...] + ire * dfim + iim * dfre
        dzre = lre * tre + lim * tim
        dzim = lre * tim - lim * tre
        qre = (fre * are + fim * aim) / den
        qim = (fim * are - fre * aim) / den
        dare_ref[...] = step * dzre - (qre * dfre + qim * dfim)
        daim_ref[...] = step * dzim - (qre * dfim - qim * dfre)
        dls = (are * dzre + aim * dzim) * step
        sel = (lax.broadcasted_iota(jnp.int32, (S5_N, LANES), 0) // 64 == lax.broadcasted_iota(jnp.int32, (S5_N, LANES), 1)).astype(f32)
        dls_ref[...] = lax.dot_general(sel, jnp.broadcast_to(dls, (S5_N, LANES)), (((0,), (0,)), ((), ())),
                                       preferred_element_type=f32, precision=lax.Precision.HIGHEST)

    col = jax.ShapeDtypeStruct((S5_N, 1), f32)
    mat = jax.ShapeDtypeStruct((S5_N, GROUP_W), f32)
    return pl.pallas_call(body, name="s5_prep_bwd", out_shape=[col, col, jax.ShapeDtypeStruct((LANES, LANES), f32), mat, mat],
                          in_specs=[VMEM] * 9, out_specs=[VMEM] * 5, compiler_params=_params(),
                          )(are, aim, ls, bre_bd, bim_bd, dlre, dlim, dbbr, dbbi)


def _cmul(ar, ai, br, bi):
    return ar * br - ai * bi, ar * bi + ai * br


def _s5_scan(re_ref, im_ref, carry_ref, mr, mi, n_groups, reverse):
    p1 = (mr, mi)
    p2 = _cmul(*p1, *p1)
    p3 = _cmul(*p2, *p1)
    p4 = _cmul(*p2, *p2)
    p5 = _cmul(*p4, *p1)
    p6 = _cmul(*p4, *p2)
    p7 = _cmul(*p4, *p3)
    p8 = _cmul(*p4, *p4)
    pows = [p1, p2, p3, p4, p5, p6, p7, p8]
    row = lax.broadcasted_iota(jnp.int32, (8, S5_N), 0)
    tr = jnp.zeros((8, S5_N), f32)
    ti = jnp.zeros((8, S5_N), f32)
    for i in range(8):
        p = pows[7 - i] if reverse else pows[i]
        tr = jnp.where(row == i, p[0], tr)
        ti = jnp.where(row == i, p[1], ti)
    steps = []
    for k, p in ((1, p1), (2, p2), (4, p4)):
        keep = (row + k < 8) if reverse else (row >= k)
        steps.append((8 - k if reverse else k, jnp.where(keep, p[0], 0.0), jnp.where(keep, p[1], 0.0)))
    edge = 0 if reverse else 7

    def step(j, carry):
        cr, ci = carry
        g = (n_groups - 1 - j) if reverse else j
        r0 = pl.multiple_of(g * 8, 8)
        xr = re_ref[pl.ds(r0, 8), :]
        xi = im_ref[pl.ds(r0, 8), :]
        for shift, br, bi in steps:
            sr = pltpu.roll(xr, shift, 0)
            si = pltpu.roll(xi, shift, 0)
            xr, xi = xr + br * sr - bi * si, xi + br * si + bi * sr
        xr, xi = xr + tr * cr - ti * ci, xi + tr * ci + ti * cr
        re_ref[pl.ds(r0, 8), :] = xr
        im_ref[pl.ds(r0, 8), :] = xi
        return (jnp.broadcast_to(xr[edge:edge + 1, :], (8, S5_N)), jnp.broadcast_to(xi[edge:edge + 1, :], (8, S5_N)))

    cr, ci = lax.fori_loop(0, n_groups, step, (carry_ref[0], carry_ref[1]))
    carry_ref[0] = cr
    carry_ref[1] = ci


def _s5_output(u, xr, xi, cnr, cni, d):
    return _dot(_b(xr), _b(cnr)) - _dot(_b(xi), _b(cni)) + d * u


def _s5_fwd(proj, bbr, bbi, ctr, cti, lre, lim, d, glu_w, glu_b, tb, xchg=None):
    t = proj.shape[0]

    def body(u_ref, bbr_ref, bbi_ref, ctr_ref, cti_ref, lr_ref, li_ref, d_ref, gw_ref, gb_ref, o_ref, xr_ref, xi_ref, carry):
        @pl.when(pl.program_id(0) == 0)
        def _():
            carry[...] = jnp.zeros_like(carry)

        u = u_ref[...]
        ub = _b(u)
        xr_ref[...] = _dot(ub, _b(bbr_ref[...]))
        xi_ref[...] = _dot(ub, _b(bbi_ref[...]))
        _s5_scan(xr_ref, xi_ref, carry, lr_ref[...], li_ref[...], tb // 8, reverse=False)
        y = _s5_output(u, xr_ref[...], xi_ref[...], ctr_ref[...], cti_ref[...], d_ref[...])
        gl = _gelu(y)
        o_ref[...] = gl * jax.nn.sigmoid(_dot(_b(gl), _b(gw_ref[...])) + gb_ref[...])

    state = pl.BlockSpec((tb, S5_N), lambda i: (i, 0))
    return _call(
        body, name="s5_fwd", grid=(t // tb,),
        out_shape=[jax.ShapeDtypeStruct((t, GROUP_W), f32), jax.ShapeDtypeStruct((t, S5_N), f32), jax.ShapeDtypeStruct((t, S5_N), f32)],
        in_specs=[pl.BlockSpec((tb, GROUP_W), lambda i: (i, C_S5)), _full((GROUP_W, S5_N)), _full((GROUP_W, S5_N)),
                  _full((S5_N, GROUP_W)), _full((S5_N, GROUP_W)), _full((1, S5_N)), _full((1, S5_N)),
                  _full((1, GROUP_W)), _full((GROUP_W, GROUP_W)), _full((1, GROUP_W))],
        out_specs=[pl.BlockSpec((tb, GROUP_W), lambda i: (i, 0)), state, state],
        scratch_shapes=[pltpu.VMEM((2, 8, S5_N), f32)],
        semantics=("arbitrary",), xchg=xchg, args=(proj, bbr, bbi, ctr, cti, lre, lim, d, glu_w, glu_b))


def _s5_bwd(proj, dyd, xr_all, xi_all, bbr, bbi, ctr, cti, cnr, cni, lre, lim, d, glu_w, glu_wt, glu_b, tb, xchg=None):
    t = proj.shape[0]
    nt = t // tb

    def body(u_ref, dy_ref, xr_ref, xi_ref, xrh_ref, xih_ref, bbr_ref, bbi_ref, ctr_ref, cti_ref, cnr_ref, cni_ref,
             lr_ref, li_ref, d_ref, gw_ref, gwt_ref, gb_ref,
             du_ref, dlr_ref, dli_ref, dbbr_ref, dbbi_ref, dctr_ref, dcti_ref, dd_ref, dgw_ref, dgb_ref,
             gr_ref, gi_ref, carry):
        i = pl.program_id(0)
        ti = nt - 1 - i

        @pl.when(i == 0)
        def _():
            carry[...] = jnp.zeros_like(carry)

        u = u_ref[...]
        ub = _b(u)
        xr, xi = xr_ref[...], xi_ref[...]
        ctr, cti = _b(ctr_ref[...]), _b(cti_ref[...])
        d = d_ref[...]
        gw = _b(gw_ref[...])
        y = _s5_output(u, xr, xi, cnr_ref[...], cni_ref[...], d)
        gl = _gelu(y)
        sg = jax.nn.sigmoid(_dot(_b(gl), gw) + gb_ref[...])
        dout = dy_ref[...]
        q = dout * gl * sg * (1.0 - sg)
        qb = _b(q)
        dgl = dout * sg + _dot(qb, _b(gwt_ref[...]))
        _acc(dgw_ref, _dot_tn(_b(gl), qb))
        _acc(dgb_ref, _colsum(q))
        dyv = dgl * _dgelu(y)
        _acc(dd_ref, _colsum(dyv * u))
        dyb = _b(dyv)
        gr_ref[...] = _dot(dyb, ctr)
        gi_ref[...] = -_dot(dyb, cti)
        _acc(dctr_ref, _dot_tn(dyb, _b(xr)))
        _acc(dcti_ref, -_dot_tn(dyb, _b(xi)))
        _s5_scan(gr_ref, gi_ref, carry, lr_ref[...], -li_ref[...], tb // 8, reverse=True)
        gr, gi = gr_ref[...], gi_ref[...]
        xpr = jnp.concatenate([jnp.where(ti > 0, xrh_ref[...], 0.0), xr], axis=0)[7:7 + tb]
        xpi = jnp.concatenate([jnp.where(ti > 0, xih_ref[...], 0.0), xi], axis=0)[7:7 + tb]
        _acc(dlr_ref, _colsum(gr * xpr + gi * xpi))
        _acc(dli_ref, _colsum(gi * xpr - gr * xpi))
        grb, gib = _b(gr), _b(gi)
        _acc(dbbr_ref, _dot_tn(grb, ub))
        _acc(dbbi_ref, _dot_tn(gib, ub))
        du_ref[...] = _b(dyv * d + _dot(grb, _b(bbr_ref[...])) + _dot(gib, _b(bbi_ref[...])))

    state = pl.BlockSpec((tb, S5_N), lambda i: (nt - 1 - i, 0))
    prev = pl.BlockSpec((8, S5_N), lambda i: (jnp.maximum((nt - 1 - i) * (tb // 8) - 1, 0), 0))
    tile = pl.BlockSpec((tb, GROUP_W), lambda i: (nt - 1 - i, 0))
    return _call(
        body, name="s5_bwd", grid=(nt,),
        out_shape=[jax.ShapeDtypeStruct((t, GROUP_W), bf16), jax.ShapeDtypeStruct((1, S5_N), f32), jax.ShapeDtypeStruct((1, S5_N), f32),
                   jax.ShapeDtypeStruct((S5_N, GROUP_W), f32), jax.ShapeDtypeStruct((S5_N, GROUP_W), f32),
                   jax.ShapeDtypeStruct((GROUP_W, S5_N), f32), jax.ShapeDtypeStruct((GROUP_W, S5_N), f32),
                   jax.ShapeDtypeStruct((1, GROUP_W), f32), jax.ShapeDtypeStruct((GROUP_W, GROUP_W), f32),
                   jax.ShapeDtypeStruct((1, GROUP_W), f32)],
        in_specs=[pl.BlockSpec((tb, GROUP_W), lambda i: (nt - 1 - i, C_S5)), tile, state, state, prev, prev,
                  _full((S5_N, GROUP_W)), _full((S5_N, GROUP_W)), _full((GROUP_W, S5_N)), _full((GROUP_W, S5_N)),
                  _full((S5_N, GROUP_W)), _full((S5_N, GROUP_W)),
                  _full((1, S5_N)), _full((1, S5_N)), _full((1, GROUP_W)), _full((GROUP_W, GROUP_W)), _full((GROUP_W, GROUP_W)),
                  _full((1, GROUP_W))],
        out_specs=[tile, _full((1, S5_N)), _full((1, S5_N)), _full((S5_N, GROUP_W)), _full((S5_N, GROUP_W)),
                   _full((GROUP_W, S5_N)), _full((GROUP_W, S5_N)), _full((1, GROUP_W)), _full((GROUP_W, GROUP_W)), _full((1, GROUP_W))],
        scratch_shapes=[pltpu.VMEM((tb, S5_N), f32), pltpu.VMEM((tb, S5_N), f32), pltpu.VMEM((2, 8, S5_N), f32)],
        semantics=("arbitrary",), xchg=xchg,
        args=(proj, dyd, xr_all, xi_all, xr_all, xi_all, bbr, bbi, ctr, cti, cnr, cni, lre, lim, d, glu_w, glu_wt, glu_b))


def _outproj_fwd(ys, h, bn_w, g1, w_out, tb):
    t = h.shape[0]

    def body(ya_ref, yb_ref, yc_ref, yd_ref, h_ref, bn_ref, g1_ref, w_ref, h1_ref, o_ref, gr_ref):
        bn = bn_ref[...]
        parts = []
        for g, r in enumerate((ya_ref, yb_ref, yc_ref, yd_ref)):
            n, _ = _rms(r[...])
            parts.append(n * bn[:, g * GROUP_W:(g + 1) * GROUP_W])
        groups = _b(jnp.concatenate(parts, axis=1))
        gr_ref[...] = groups
        o = _dot(groups, w_ref[...])
        o_ref[...] = o
        h1_ref[...] = h_ref[...] + g1_ref[...] * o

    grp = pl.BlockSpec((tb, GROUP_W), lambda i: (i, 0))
    row = pl.BlockSpec((tb, D_MODEL), lambda i: (i, 0))
    vec = _full((1, D_MODEL))
    return pl.pallas_call(
        body, name="outproj_fwd", grid=(t // tb,),
        out_shape=[jax.ShapeDtypeStruct((t, D_MODEL), f32), jax.ShapeDtypeStruct((t, D_MODEL), f32),
                   jax.ShapeDtypeStruct((t, D_MODEL), bf16)],
        in_specs=[grp, grp, grp, grp, row, vec, vec, _full((D_MODEL, D_MODEL))],
        out_specs=[row, row, row],
        compiler_params=_params(("parallel",)),
    )(*ys, h, bn_w, g1, w_out)


def _outproj_bwd(dh1, o, ys, bn_w, g1, w_out_t, tb):
    t = dh1.shape[0]

    def body(dh_ref, o_ref, ya_ref, yb_ref, yc_ref, yd_ref, bn_ref, g1_ref, w_ref,
             da_ref, db_ref, dc_ref, dd_ref, do_ref, dg1_ref, dbn_ref):
        dh = dh_ref[...]
        _acc(dg1_ref, _colsum(dh * o_ref[...]))
        do = _b(dh * g1_ref[...])
        do_ref[...] = do
        dgroups = _dot(do, w_ref[...])
        bn = bn_ref[...]
        dbn = []
        for g, (r, dr) in enumerate(((ya_ref, da_ref), (yb_ref, db_ref), (yc_ref, dc_ref), (yd_ref, dd_ref))):
            n, rr = _rms(r[...])
            dgr = dgroups[:, g * GROUP_W:(g + 1) * GROUP_W]
            dbn.append(_colsum(dgr * n))
            dr[...] = _rms_bwd(dgr * bn[:, g * GROUP_W:(g + 1) * GROUP_W], n, rr)
        _acc(dbn_ref, jnp.concatenate(dbn, axis=1))

    grp = pl.BlockSpec((tb, GROUP_W), lambda i: (i, 0))
    row = pl.BlockSpec((tb, D_MODEL), lambda i: (i, 0))
    vec = _full((1, D_MODEL))
    return pl.pallas_call(
        body, name="outproj_bwd", grid=(t // tb,),
        out_shape=[jax.ShapeDtypeStruct((t, GROUP_W), f32)] * 4 + [jax.ShapeDtypeStruct((t, D_MODEL), bf16),
                   jax.ShapeDtypeStruct((1, D_MODEL), f32), jax.ShapeDtypeStruct((1, D_MODEL), f32)],
        in_specs=[row, row, grp, grp, grp, grp, vec, vec, _full((D_MODEL, D_MODEL))],
        out_specs=[grp, grp, grp, grp, row, vec, vec],
        compiler_params=_params(("arbitrary",)),
    )(dh1, o, *ys, bn_w, g1, w_out_t)


def _mlp_fwd(h1, norm_w, sc, sh, g2, w1, w2, tb, xchg=None):
    t = h1.shape[0]
    nh = w1.shape[0] // MLP_SLABS

    def body(h_ref, nw_ref, sc_ref, sh_ref, g2_ref, w1_ref, w2_ref, h2_ref, m_ref, v_ref, r_ref, acc):
        j = pl.program_id(1)

        @pl.when(j == 0)
        def _():
            n, _ = _rms(h_ref[...])
            v_ref[...] = _b(n * nw_ref[...] * (1.0 + sc_ref[...]) + sh_ref[...])

        v = v_ref[...]
        p = None
        for s in range(MLP_SLABS):
            ra = jnp.maximum(_dot(v, w1_ref[s]), 0.0)
            r = _b(ra * ra)
            r_ref[:, s * MLP_HB:(s + 1) * MLP_HB] = r
            q = _dot(r, w2_ref[s])
            p = q if p is None else p + q

        @pl.when(j == 0)
        def _():
            acc[...] = p

        @pl.when(j != 0)
        def _():
            acc[...] += p

        @pl.when(j == nh - 1)
        def _():
            m = acc[...]
            m_ref[...] = _b(m)
            h2_ref[...] = h_ref[...] + g2_ref[...] * m

    row = pl.BlockSpec((tb, D_MODEL), lambda i, j: (i, 0))
    hid = pl.BlockSpec((tb, MLP_SLABS * MLP_HB), lambda i, j: (i, j))
    vec = _full((1, D_MODEL))
    return _call(
        body, name="mlp_fwd", grid=(t // tb, nh),
        out_shape=[jax.ShapeDtypeStruct((t, D_MODEL), f32), jax.ShapeDtypeStruct((t, D_MODEL), bf16),
                   jax.ShapeDtypeStruct((t, D_MODEL), bf16), jax.ShapeDtypeStruct((t, N_DEV * MLP_HB), bf16)],
        in_specs=[row, vec, vec, vec, vec, pl.BlockSpec((MLP_SLABS, D_MODEL, MLP_HB), lambda i, j: (j, 0, 0)),
                  pl.BlockSpec((MLP_SLABS, MLP_HB, D_MODEL), lambda i, j: (j, 0, 0))],
        out_specs=[row, row, row, hid],
        scratch_shapes=[pltpu.VMEM((tb, D_MODEL), f32)],
        semantics=("arbitrary", "arbitrary"), xchg=xchg, args=(h1, norm_w, sc, sh, g2, w1, w2))


def _mlp_bwd(dh2, m, h1, r, norm_w, sc, sh, g2, w1, w2, tb, xchg=None):
    t = h1.shape[0]
    slabs = MLP_BWD_SLABS
    nh = w1.shape[0] // slabs

    def body(dh_ref, m_ref, h_ref, r_ref, nw_ref, sc_ref, sh_ref, g2_ref, w1_ref, w2_ref,
             dh1_ref, do_ref, da_ref, dg2_ref, dsh_ref, dsc_ref, dnw_ref, acc):
        j = pl.program_id(1)

        @pl.when(j == 0)
        def _():
            dh = dh_ref[...]
            _acc(dg2_ref, _colsum(dh * m_ref[...].astype(f32)))
            do_ref[...] = _b(dh * g2_ref[...])

        do = do_ref[...]
        p = None
        for s in range(slabs):
            cols = slice(s * MLP_HB, (s + 1) * MLP_HB)
            dr = _dot(do, w2_ref[s])
            da = _b(dr * 2.0 * jnp.sqrt(r_ref[:, cols].astype(f32)))
            da_ref[:, cols] = da
            q = _dot(da, w1_ref[s])
            p = q if p is None else p + q

        @pl.when(j == 0)
        def _():
            acc[...] = p

        @pl.when(j != 0)
        def _():
            acc[...] += p

        @pl.when(j == nh - 1)
        def _():
            dv = acc[...]
            n, r = _rms(h_ref[...])
            nw = nw_ref[...]
            gain = 1.0 + sc_ref[...]
            _acc(dsh_ref, _colsum(dv))
            _acc(dsc_ref, _colsum(dv * n * nw))
            _acc(dnw_ref, _colsum(dv * gain * n))
            dh1_ref[...] = dh_ref[...] + _rms_bwd(dv * nw * gain, n, r)

    row = pl.BlockSpec((tb, D_MODEL), lambda i, j: (i, 0))
    hid = pl.BlockSpec((tb, slabs * MLP_HB), lambda i, j: (i, j))
    vec = _full((1, D_MODEL))
    return _call(
        body, name="mlp_bwd", grid=(t // tb, nh),
        out_shape=[jax.ShapeDtypeStruct((t, D_MODEL), f32), jax.ShapeDtypeStruct((t, D_MODEL), bf16),
                   jax.ShapeDtypeStruct((t, N_DEV * MLP_HB), bf16)] + [jax.ShapeDtypeStruct((1, D_MODEL), f32)] * 4,
        in_specs=[row, row, row, hid, vec, vec, vec, vec,
                  pl.BlockSpec((slabs, MLP_HB, D_MODEL), lambda i, j: (j, 0, 0)),
                  pl.BlockSpec((slabs, D_MODEL, MLP_HB), lambda i, j: (j, 0, 0))],
        out_specs=[row, row, hid, vec, vec, vec, vec],
        scratch_shapes=[pltpu.VMEM((tb, D_MODEL), f32)],
        semantics=("arbitrary", "arbitrary"), xchg=xchg, args=(dh2, m, h1, r, norm_w, sc, sh, g2, w1, w2))


def _loss_head(h, target, norm_w, tb):
    t = h.shape[0]

    def body(h_ref, t_ref, w_ref, loss_ref, dh_ref, dw_ref):
        n, r = _rms(h_ref[...])
        w = w_ref[...]
        err = n * w - t_ref[...]
        part = 0.5 * jnp.sum(jnp.sum(err * err, axis=1, keepdims=True), axis=0, keepdims=True) / D_MODEL
        _acc(loss_ref, jnp.broadcast_to(part, (8, LANES)))
        dy = err / D_MODEL
        _acc(dw_ref, _colsum(dy * n))
        dh_ref[...] = _rms_bwd(dy * w, n, r)

    row = pl.BlockSpec((tb, D_MODEL), lambda i: (i, 0))
    return pl.pallas_call(
        body, name="loss_head", grid=(t // tb,),
        out_shape=[jax.ShapeDtypeStruct((8, LANES), f32), jax.ShapeDtypeStruct((t, D_MODEL), f32),
                   jax.ShapeDtypeStruct((1, D_MODEL), f32)],
        in_specs=[row, row, _full((1, D_MODEL))],
        out_specs=[_full((8, LANES)), row, _full((1, D_MODEL))],
        compiler_params=_params(("arbitrary",)),
    )(h, target, norm_w)


def _adam_math(w, g, m, v):
    m2 = ADAM_B1 * m + (1.0 - ADAM_B1) * g
    v2 = ADAM_B2 * v + (1.0 - ADAM_B2) * (g * g)
    mh = m2 / (1.0 - ADAM_B1 ** ADAM_STEP)
    vh = v2 / (1.0 - ADAM_B2 ** ADAM_STEP)
    return -ADAM_LR * (mh / (jnp.sqrt(vh) + ADAM_EPS) + ADAM_WD * w), m2, v2


def _sum_adamw(parts, w, m, v, name, rb):
    n_src, r, c = parts.shape

    def body(p_ref, w_ref, m_ref, v_ref, g_ref, d_ref, m2_ref, v2_ref):
        g = p_ref[0].astype(f32)
        for s in range(1, n_src):
            g = g + p_ref[s].astype(f32)
        g_ref[...] = g
        d, m2, v2 = _adam_math(w_ref[...], g, m_ref[...], v_ref[...])
        d_ref[...] = d
        m2_ref[...] = m2
        v2_ref[...] = v2

    blk = pl.BlockSpec((rb, c), lambda i: (i, 0))
    return pl.pallas_call(
        body, name=name, grid=(r // rb,),
        out_shape=[jax.ShapeDtypeStruct((r, c), f32)] * 4,
        in_specs=[pl.BlockSpec((n_src, rb, c), lambda i: (0, i, 0)), blk, blk, blk],
        out_specs=[blk] * 4,
        compiler_params=_params(("parallel",)),
    )(parts, w, m, v)


def _sum_adamw_layers(parts0, parts1, w, m, v, name, rb):
    n_src, r, c = parts0.shape
    nb = r // rb

    def body(p0_ref, p1_ref, w_ref, m_ref, v_ref, g_ref, d_ref, m2_ref, v2_ref):
        def update(p_ref):
            g = p_ref[0].astype(f32)
            for s in range(1, n_src):
                g = g + p_ref[s].astype(f32)
            g_ref[0] = g
            d, m2, v2 = _adam_math(w_ref[0], g, m_ref[0], v_ref[0])
            d_ref[0] = d
            m2_ref[0] = m2
            v2_ref[0] = v2

        @pl.when(pl.program_id(0) == 0)
        def _():
            update(p0_ref)

        @pl.when(pl.program_id(0) == 1)
        def _():
            update(p1_ref)

    blk = pl.BlockSpec((1, rb, c), lambda l, i: (l, i, 0))
    return pl.pallas_call(
        body, name=name, grid=(2, nb),
        out_shape=[jax.ShapeDtypeStruct((2, r, c), f32)] * 4,
        in_specs=[pl.BlockSpec((n_src, rb, c), lambda l, i: (0, jnp.where(l == 0, i, nb - 1), 0)),
                  pl.BlockSpec((n_src, rb, c), lambda l, i: (0, jnp.where(l == 1, i, 0), 0)), blk, blk, blk],
        out_specs=[blk] * 4,
        compiler_params=_params(("arbitrary", "arbitrary")),
    )(parts0, parts1, w, m, v)


def _reorder_in(w):
    pad = jnp.zeros(w.shape[:-1] + (P_IN - 2308,), w.dtype)
    return jnp.concatenate([w[..., :2048], w[..., 2052:2308], w[..., 2048:2052], pad], axis=-1)


def _unreorder_in(w):
    return jnp.concatenate([w[..., :2048], w[..., 2304:2308], w[..., 2048:2304]], axis=-1)


def _block_diag(w2d, n_blocks):
    rows, cols = w2d.shape
    tiled = jnp.tile(w2d, (1, n_blocks))
    rb = lax.broadcasted_iota(jnp.int32, tiled.shape, 0) // (rows // n_blocks)
    cb = lax.broadcasted_iota(jnp.int32, tiled.shape, 1) // cols
    return jnp.where(rb == cb, tiled, jnp.zeros_like(tiled))


def _block_diag_extract(w_bd, n_blocks):
    rows, wide = w_bd.shape
    r, c = rows // n_blocks, wide // n_blocks
    w4 = w_bd.reshape(n_blocks, r, n_blocks, c)
    idx = jnp.arange(n_blocks)
    return w4[idx, :, idx, :]


def _lanes128(v):
    return jnp.pad(v.reshape(1, -1), ((0, 0), (0, LANES - v.size)))


def _rows_of(shape):
    n = 1
    for d in shape:
        n *= d
    return -(-n // (8 * LANES)) * 8, n


def _flat_pack(arrs, row_multiple=8):
    blocks = []
    for a in arrs:
        rows, n = _rows_of(a.shape)
        blocks.append(jnp.pad(a.reshape(-1), (0, rows * LANES - n)).reshape(rows, LANES))
    total = sum(b.shape[0] for b in blocks)
    pad = -total % row_multiple
    if pad:
        blocks.append(jnp.zeros((pad, LANES), blocks[0].dtype))
    return jnp.concatenate(blocks, axis=0)


def _flat_unpack(packed, shapes):
    out, off = [], 0
    for s in shapes:
        rows, n = _rows_of(s)
        out.append(packed[off:off + rows].reshape(-1)[:n].reshape(s))
        off += rows
    return out


_W_NAMES = ['norm_mix_w', 'norm_mlp_w', 'ada_w', 'ada_b', 'w_in', 'pool_w', 'pool_scale', 'sconv_w', 'ssd_conv_w',
            'ssd_conv_b', 'ssd_dt_bias', 'ssd_a_log', 'ssd_d', 's5_a_re', 's5_a_im', 's5_log_step', 's5_b_re', 's5_b_im',
            's5_c_re', 's5_c_im', 's5_d', 's5_glu_w', 's5_glu_b', 'branch_norm_w', 'w_out', 'mlp_w1', 'mlp_w2',
            'final_norm_w']
_BIG = ('ada_w', 'w_in', 'w_out', 'mlp_w1', 'mlp_w2')
_SMALL = [n for n in _W_NAMES if n not in _BIG]
_SHARDED_SMALL = {'sconv_w': (2, 32), 'ssd_conv_w': (2, 96), 's5_glu_w': (1, 32)}


def _gather(*blocks):
    return _ChipGather(blocks)


def _scatter(*parts):
    return _Exchange(parts, gather=False)


def _layer_forward(l, h, p, w, sh_b, tb):
    first = l == 0
    (proj, u_b), got = _inproj_fwd(h, p['norm_mix_w'][l], p['sc1'][l], p['sh1'][l], w['w_in', l], tb,
                                   xchg=_gather(sh_b[1][0]) if first else None)
    if first:
        w['w_out', 0] = got[0].reshape(D_MODEL, D_MODEL)
    ya = _pool_fwd(proj, p['pool_bd'][l], p['pool_scale'][l], tb)
    yb = _sconv_fwd(proj, p['sconv_w8'][l], tb)
    (yc, yc_pre, states), got = _ssd_fwd(proj, p['ssd_conv_w8'][l], p['ssd_conv_b'][l], p['ssd_dt_bias'][l], p['ssd_a_log'][l],
                                         p['ssd_d_cols'][l], tb, xchg=_gather(sh_b[2][0]) if first else None)
    if first:
        w['w1', 0] = got[0]
    (yd, xr, xi), got = _s5_fwd(proj, p['bbr_t'][l], p['bbi_t'][l], p['ctr_t'][l], p['cti_t'][l], p['lre'][l], p['lim'][l],
                                p['s5_d'][l], p['glu_w'][l], p['glu_b'][l], tb, xchg=_gather(sh_b[3][0]) if first else None)
    if first:
        w['w2', 0] = got[0]
    ys = (ya, yb, yc, yd)
    h1, o, groups_b = _outproj_fwd(ys, h, p['branch_norm_w'][l], p['g1'][l], w['w_out', l], tb)
    (h2, m, v_b, r_b), got = _mlp_fwd(h1, p['norm_mlp_w'][l], p['sc2'][l], p['sh2'][l], p['g2'][l], w['w1', l], w['w2', l],
                                      min(MLP_TB, h.shape[0]), xchg=_gather(*[sh_b[k][1] for k in range(4)]) if first else None)
    if first:
        w['w_in', 1] = got[0].reshape(D_MODEL, P_IN)
        w['w_out', 1] = got[1].reshape(D_MODEL, D_MODEL)
        w['w1', 1], w['w2', 1] = got[2], got[3]
    saved = dict(h=h, proj=proj, u_b=u_b, ys=ys, yc_pre=yc_pre, states=states, xr=xr, xi=xi, h1=h1, o=o,
                 groups_b=groups_b, m=m, v_b=v_b, r_b=r_b)
    return h2, saved


def _layer_backward(l, dh2, s, p, w, pending, recv, tb):
    def carry(names):
        names = [n for n in names if n in pending]
        return names, (_scatter(*[pending.pop(n) for n in names]) if names else None)

    def landed(names, got):
        for n, g in zip(names, got):
            recv[n] = g

    names, xchg = carry([('w_out', 1)])
    (dh1, do2_b, da_b, dg2, dsh2, dsc2, dnw_mlp), got = _mlp_bwd(dh2, s['m'], s['h1'], s['r_b'], p['norm_mlp_w'][l], p['sc2'][l],
                                                                p['sh2'][l], p['g2'][l], jnp.swapaxes(w['w1', l], 1, 2),
                                                                jnp.swapaxes(w['w2', l], 1, 2), tb, xchg=xchg)
    landed(names, got)
    pending['mlp_w2', l] = _wgrad(s['r_b'], do2_b, 1, "wgrad_w2", tm=1024, tk=1024).reshape(N_DEV, MLP_HB, D_MODEL)
    pending['mlp_w1', l] = _wgrad(s['v_b'], da_b, N_DEV, "wgrad_w1", tm=1024, tk=2048)
    dya, dyb, dyc, dyd, do1_b, dg1, dbn = _outproj_bwd(dh1, s['o'], s['ys'], p['branch_norm_w'][l], p['g1'][l], w['w_out', l].T, tb)
    pending['w_out', l] = _wgrad(s['groups_b'], do1_b, 1, "wgrad_wout", tm=1024, tk=1024).reshape(N_DEV, D_MODEL // N_DEV, D_MODEL)
    proj = s['proj']
    dv, dpool_bd, dpool_scale = _pool_bwd(proj, dya, p['pool_bd'][l], p['pool_scale'][l], tb)
    dgb, dgc, dhh, dsconv = _sconv_bwd(proj, dyb, p['sconv_w8'][l], tb)
    names, xchg = carry([('mlp_w1', l)] + ([('w_out', 0)] if l == 0 else []))
    (dz, dxs, dbm, dcm, ddt, dconv_w, dconv_b, ddtb, dalog, ddskip), got = _ssd_bwd(
        proj, dyc, s['yc_pre'], s['states'], p['ssd_conv_w8'][l], p['ssd_conv_b'][l], p['ssd_dt_bias'][l], p['ssd_a_log'][l],
        p['ssd_d_cols'][l], tb, xchg=xchg)
    landed(names, got)
    names, xchg = carry([('mlp_w2', l)])
    (du5, dlr, dli, dbbr, dbbi, dctr, dcti, dd5, dgw, dgb5), got = _s5_bwd(
        proj, dyd, s['xr'], s['xi'], p['bbr'][l], p['bbi'][l], p['ctr'][l], p['cti'][l], p['ctr_t'][l], p['cti_t'][l],
        p['lre'][l], p['lim'][l], p['s5_d'][l], p['glu_w'][l], p['glu_w'][l].T, p['glu_b'][l], tb, xchg=xchg)
    landed(names, got)
    dare, daim, dls, dbre_bd, dbim_bd = _s5_prep_bwd(p['are_c'][l], p['aim_c'][l], p['ls_c'][l], p['bre_bd'][l], p['bim_bd'][l],
                                                     dlr.reshape(S5_N, 1), dli.reshape(S5_N, 1), dbbr, dbbi)
    dparts = (dv, dgb, dgc, dhh, dz, dxs, dbm, dcm, du5, ddt)
    pending['w_in', l] = _wgrad_parts(s['u_b'], dparts, "wgrad_win", tm=512, tk=1024).reshape(N_DEV, D_MODEL // N_DEV, P_IN)
    names, xchg = carry([('w_in', l)])
    (dh, dsh1, dsc1, dnw_mix), got = _inproj_bwd(dparts, dh1, s['h'], p['norm_mix_w'][l], p['sc1'][l], p['sh1'][l], w['w_in', l].T, tb,
                                                 xchg=xchg)
    landed(names, got)
    small = {
        'norm_mix_w': dnw_mix.reshape(D_MODEL), 'norm_mlp_w': dnw_mlp.reshape(D_MODEL),
        'ada_b': jnp.concatenate([dsh1, dsc1, dg1, dsh2, dsc2, dg2], axis=1).reshape(6 * D_MODEL),
        'pool_w': _block_diag_extract(dpool_bd, 4), 'pool_scale': dpool_scale.reshape(GROUP_W),
        'sconv_w': dsconv[0:3], 'ssd_conv_w': dconv_w[0:4], 'ssd_conv_b': dconv_b.reshape(768),
        'ssd_dt_bias': ddtb[0, 0:4], 'ssd_a_log': dalog[0, 0:4], 'ssd_d': ddskip[0, 0:4],
        's5_a_re': dare.reshape(16, 64), 's5_a_im': daim.reshape(16, 64), 's5_log_step': dls[0:16, 0],
        's5_b_re': _block_diag_extract(dbre_bd, 16), 's5_b_im': _block_diag_extract(dbim_bd, 16),
        's5_c_re': _block_diag_extract(dctr, 16), 's5_c_im': _block_diag_extract(dcti, 16),
        's5_d': dd5.reshape(GROUP_W), 's5_glu_w': dgw, 's5_glu_b': dgb5.reshape(GROUP_W),
        'branch_norm_w': dbn.reshape(D_MODEL),
    }
    return dh, small


def _prepare_params(a, me):
    pack_shapes = [(1, D_MODEL), (2, 3, 32), (2, 4, 96), (2, 32, GROUP_W)]
    packed = _flat_pack([a['c'], a['sconv_w'], a['ssd_conv_w'], a['s5_glu_w']])
    gathered = _gather_rows(packed, "gather_small")
    pieces = [_flat_unpack(gathered[d], pack_shapes) for d in range(N_DEV)]
    c_all = jnp.concatenate([pc[0] for pc in pieces], axis=0)
    sconv_full = jnp.concatenate([pc[1] for pc in pieces], axis=2)
    ssd_conv_full = jnp.concatenate([pc[2] for pc in pieces], axis=2)
    glu_full = jnp.concatenate([pc[3] for pc in pieces], axis=1)

    ada_b_cols = lax.dynamic_slice_in_dim(a['ada_b'], me * 768, 768, axis=1).reshape(2, 1, 768)
    cond, modrows = _ada_forward(c_all, a['ada_w'], ada_b_cols)
    mod_recv = _all_to_all_rows(modrows.transpose(1, 0, 2), "exchange_mod")
    mod = mod_recv.transpose(1, 0, 2).reshape(2, 6 * D_MODEL)
    p = {'cond': cond}
    for k, name in enumerate(('sh1', 'sc1', 'g1', 'sh2', 'sc2', 'g2')):
        p[name] = mod[:, k * D_MODEL:(k + 1) * D_MODEL].reshape(2, 1, D_MODEL)

    for name in ('norm_mix_w', 'norm_mlp_w', 'branch_norm_w'):
        p[name] = a[name].reshape(2, 1, D_MODEL)
    p['pool_bd'] = jnp.stack([_block_diag(a['pool_w'][l].reshape(GROUP_W, 64), 4) for l in range(2)])
    p['pool_scale'] = a['pool_scale'].reshape(2, 1, GROUP_W)
    p['sconv_w8'] = jnp.pad(sconv_full, ((0, 0), (0, 5), (0, 0)))
    p['ssd_conv_w8'] = jnp.pad(ssd_conv_full, ((0, 0), (0, 4), (0, 0)))
    p['ssd_conv_b'] = a['ssd_conv_b'].reshape(2, 1, 768)
    p['ssd_dt_bias'] = jnp.pad(a['ssd_dt_bias'], ((0, 0), (0, LANES - 4))).reshape(2, 1, LANES)
    p['ssd_a_log'] = jnp.pad(a['ssd_a_log'], ((0, 0), (0, LANES - 4))).reshape(2, 1, LANES)
    p['ssd_d_cols'] = jnp.repeat(a['ssd_d'], SSD_P, axis=1).reshape(2, 1, GROUP_W)
    p['are_c'] = a['s5_a_re'].reshape(2, S5_N, 1)
    p['aim_c'] = a['s5_a_im'].reshape(2, S5_N, 1)
    p['ls_c'] = jnp.repeat(a['s5_log_step'], 64, axis=1).reshape(2, S5_N, 1)
    p['bre_bd'] = jnp.stack([_block_diag(a['s5_b_re'][l].reshape(S5_N, 16), 16) for l in range(2)])
    p['bim_bd'] = jnp.stack([_block_diag(a['s5_b_im'][l].reshape(S5_N, 16), 16) for l in range(2)])
    p['ctr'] = jnp.stack([_block_diag(a['s5_c_re'][l].reshape(GROUP_W, 64), 16) for l in range(2)])
    p['cti'] = jnp.stack([_block_diag(a['s5_c_im'][l].reshape(GROUP_W, 64), 16) for l in range(2)])
    p['s5_d'] = a['s5_d'].reshape(2, 1, GROUP_W)
    p['glu_w'] = glu_full
    p['glu_b'] = a['s5_glu_b'].reshape(2, 1, GROUP_W)
    lre, lim, bbr, bbi = [], [], [], []
    for l in range(2):
        r = _s5_prep(p['are_c'][l], p['aim_c'][l], p['ls_c'][l], p['bre_bd'][l], p['bim_bd'][l])
        lre.append(r[0].reshape(1, S5_N))
        lim.append(r[1].reshape(1, S5_N))
        bbr.append(r[2])
        bbi.append(r[3])
    p['lre'], p['lim'], p['bbr'], p['bbi'] = lre, lim, bbr, bbi
    for name in ('bbr', 'bbi', 'ctr', 'cti'):
        p[name + '_t'] = [p[name][l].T for l in range(2)]
    return p


def kernel(x, c, norm_mix_w, norm_mlp_w, ada_w, ada_b, w_in, pool_w, pool_scale, sconv_w, ssd_conv_w, ssd_conv_b, ssd_dt_bias, ssd_a_log, ssd_d, s5_a_re, s5_a_im, s5_log_step, s5_b_re, s5_b_im, s5_c_re, s5_c_im, s5_d, s5_glu_w, s5_glu_b, branch_norm_w, w_out, mlp_w1, mlp_w2, final_norm_w, loss_target, m_norm_mix_w, m_norm_mlp_w, m_ada_w, m_ada_b, m_w_in, m_pool_w, m_pool_scale, m_sconv_w, m_ssd_conv_w, m_ssd_conv_b, m_ssd_dt_bias, m_ssd_a_log, m_ssd_d, m_s5_a_re, m_s5_a_im, m_s5_log_step, m_s5_b_re, m_s5_b_im, m_s5_c_re, m_s5_c_im, m_s5_d, m_s5_glu_w, m_s5_glu_b, m_branch_norm_w, m_w_out, m_mlp_w1, m_mlp_w2, m_final_norm_w, v_norm_mix_w, v_norm_mlp_w, v_ada_w, v_ada_b, v_w_in, v_pool_w, v_pool_scale, v_sconv_w, v_ssd_conv_w, v_ssd_conv_b, v_ssd_dt_bias, v_ssd_a_log, v_ssd_d, v_s5_a_re, v_s5_a_im, v_s5_log_step, v_s5_b_re, v_s5_b_im, v_s5_c_re, v_s5_c_im, v_s5_d, v_s5_glu_w, v_s5_glu_b, v_branch_norm_w, v_w_out, v_mlp_w1, v_mlp_w2, v_final_norm_w):
    a = dict(locals())
    t = x.shape[1]
    tb = min(512, t)
    me = _my_index()
    p = _prepare_params(a, me)

    sh_b = _cast_shards([_reorder_in(w_in), w_out, mlp_w1, mlp_w2])
    w = {('w_in', 0): _exchange_alone(_gather(sh_b[0][0]), "gather_w_in0")[0].reshape(D_MODEL, P_IN)}

    h = x.reshape(t, D_MODEL)
    saved = []
    for l in range(2):
        h, s = _layer_forward(l, h, p, w, sh_b, tb)
        saved.append(s)
    loss_blk, dh, dfinal = _loss_head(h, loss_target.reshape(t, D_MODEL), final_norm_w.reshape(1, D_MODEL), tb)
    loss = lax.psum(loss_blk[0, 0], ("x", "y", "c"))

    pending, recv, small_parts = {}, {}, [None, None]
    for l in (1, 0):
        dh, small_parts[l] = _layer_backward(l, dh, saved[l], p, w, pending, recv, tb)
    grad_x = dh.reshape(1, t, D_MODEL)

    grads, deltas, new_m, new_v = {}, {}, {}, {}

    wmv_in = [_reorder_in(a[n]) for n in ('w_in', 'm_w_in', 'v_w_in')]
    outs = _sum_adamw_layers(recv['w_in', 0], recv['w_in', 1], *wmv_in, "adamw_w_in", 128)
    grads['w_in'], deltas['w_in'], new_m['w_in'], new_v['w_in'] = [_unreorder_in(o) for o in outs]
    for name, rb in (('w_out', 128), ('mlp_w1', 256), ('mlp_w2', 256)):
        grads[name], deltas[name], new_m[name], new_v[name] = _sum_adamw_layers(
            recv[name, 0], recv[name, 1], a[name], a['m_' + name], a['v_' + name], "adamw_" + name, rb)

    dmod = jnp.stack([small_parts[0]['ada_b'], small_parts[1]['ada_b']])
    dmod_recv = _all_to_all_rows(dmod.reshape(2, N_DEV, 768).transpose(1, 0, 2), "exchange_dmod")
    g_ada = _ada_backward(p['cond'], dmod_recv.transpose(1, 0, 2))
    grads['ada_w'], deltas['ada_w'], new_m['ada_w'], new_v['ada_w'] = _sum_adamw_layers(
        g_ada[0:1], g_ada[1:2], ada_w, m_ada_w, v_ada_w, "adamw_ada_w", 256)

    layered = [n for n in _SMALL if n != 'final_norm_w']
    full = [jnp.stack([small_parts[0][n], small_parts[1][n]]) for n in layered] + [dfinal.reshape(D_MODEL)]
    full_shapes = [f.shape for f in full]
    summed = _flat_unpack(_allreduce_rows(_flat_pack(full, row_multiple=64)), full_shapes)
    local = []
    for n, g in zip(_SMALL, summed):
        if n in _SHARDED_SMALL:
            axis, size = _SHARDED_SMALL[n]
            g = lax.dynamic_slice_in_dim(g, me * size, size, axis=axis)
        local.append(g.reshape(a[n].shape))
    local_shapes = [g.shape for g in local]
    packed = [_flat_pack(xs) for xs in (local, [a[n] for n in _SMALL], [a['m_' + n] for n in _SMALL], [a['v_' + n] for n in _SMALL])]
    outs = _sum_adamw(packed[0][None], packed[1], packed[2], packed[3], "adamw_small", packed[0].shape[0])
    for store, o in zip((grads, deltas, new_m, new_v), outs):
        for n, val in zip(_SMALL, _flat_unpack(o, local_shapes)):
            store[n] = val

    return (loss, grad_x, *[grads[n] for n in _W_NAMES], *[deltas[n] for n in _W_NAMES],
            *[new_m[n] for n in _W_NAMES], *[new_v[n] for n in _W_NAMES])
```

```python
import functools

import jax
import jax.numpy as jnp
from jax import lax
from jax.experimental import pallas as pl
from jax.experimental.pallas import tpu as pltpu

f32 = jnp.float32
bf16 = jnp.bfloat16

N_DEV = 8
D_MODEL = 1024
GROUP_W = 256
P_IN = 2432
DT_COL = 2304
SSD_CHUNK = 128
SSD_HEADS = 4
SSD_P = 64
S5_N = 1024
MLP_HB = 512
TB = 1024
TB_BWD = 512
MLP_TB = 1024
MLP_SLABS = 2
MLP_BWD_SLABS = 4
EPS = 1e-6
LANES = 128
VMEM_LIMIT = 56 * 1024 * 1024
ADAM_LR, ADAM_B1, ADAM_B2, ADAM_EPS, ADAM_WD, ADAM_STEP = 0.001, 0.9, 0.999, 1e-08, 0.01, 10
POOL_WINDOWS = (2, 4, 8, 16)

C_POOL, C_GB, C_GC, C_HH, C_Z, C_XS, C_BM, C_CM, C_S5 = range(9)
C_DT128 = DT_COL // LANES

MESH = pl.DeviceIdType.MESH
ANY = pl.BlockSpec(memory_space=pl.ANY)
VMEM = pl.BlockSpec(memory_space=pltpu.VMEM)


def _dot(a, b):
    return jnp.dot(a, b, preferred_element_type=f32)


def _dot_nt(a, b):
    return lax.dot_general(a, b, (((1,), (1,)), ((), ())), preferred_element_type=f32)


def _dot_tn(a, b):
    return lax.dot_general(a, b, (((0,), (0,)), ((), ())), preferred_element_type=f32)


def _dot_exact(a, b):
    return jnp.dot(a, b, preferred_element_type=f32, precision=lax.Precision.HIGHEST)


def _b(x):
    return x.astype(bf16)


def _silu(x):
    return x * jax.nn.sigmoid(x)


def _dsilu(x):
    s = jax.nn.sigmoid(x)
    return s * (1.0 + x * (1.0 - s))


def _softplus(x):
    return jnp.maximum(x, 0.0) + jnp.log1p(jnp.exp(-jnp.abs(x)))


_GELU_K = 0.7978845608028654
_GELU_C = 0.044715


def _gelu(x):
    return 0.5 * x * (1.0 + jnp.tanh(_GELU_K * (x + _GELU_C * x * x * x)))


def _dgelu(x):
    th = jnp.tanh(_GELU_K * (x + _GELU_C * x * x * x))
    return 0.5 * (1.0 + th) + 0.5 * x * (1.0 - th * th) * _GELU_K * (1.0 + 3.0 * _GELU_C * x * x)


def _rms(h):
    r = lax.rsqrt(jnp.mean(h * h, axis=-1, keepdims=True) + EPS)
    return h * r, r


def _rms_bwd(dn, n, r):
    return r * (dn - n * jnp.mean(dn * n, axis=-1, keepdims=True))


def _colsum(x):
    return jnp.sum(x, axis=0, keepdims=True)


def _params(sem=None):
    return pltpu.CompilerParams(dimension_semantics=sem, vmem_limit_bytes=VMEM_LIMIT)


def _full(shape):
    return pl.BlockSpec(shape, lambda *_: (0,) * len(shape))


def _acc(ref, val):
    @pl.when(pl.program_id(0) == 0)
    def _():
        ref[...] = val

    @pl.when(pl.program_id(0) != 0)
    def _():
        ref[...] += val


def _me():
    return lax.axis_index("x"), lax.axis_index("y"), lax.axis_index("c")


def _my_index():
    x, y, c = _me()
    return 4 * x + 2 * y + c


def _coords(p):
    return (p // 4, (p // 2) % 2, p % 2)


class _Exchange:
    def __init__(self, srcs, gather):
        self.srcs = list(srcs)
        self.gather = gather
        self.n = len(self.srcs)
        self.out_shape = [jax.ShapeDtypeStruct(((N_DEV,) + s.shape) if gather else s.shape, s.dtype) for s in self.srcs]
        self.scratch = [pltpu.SemaphoreType.DMA((self.n, N_DEV)), pltpu.SemaphoreType.DMA((self.n, N_DEV)),
                        pltpu.SemaphoreType.DMA((self.n,))]

    def _src(self, refs, t, dev):
        return refs[t] if self.gather else refs[t].at[dev]

    def _remote(self, xin, xout, sems, t, k, me, to):
        return pltpu.make_async_remote_copy(
            src_ref=self._src(xin, t, to), dst_ref=xout[t].at[me], send_sem=sems[0].at[t, k], recv_sem=sems[1].at[t, k],
            device_id=_coords(to), device_id_type=MESH)

    def start(self, xin, xout, sems):
        me = _my_index()
        for t in range(self.n):
            pltpu.make_async_copy(self._src(xin, t, me), xout[t].at[me], sems[2].at[t]).start()
            for k in range(1, N_DEV):
                self._remote(xin, xout, sems, t, k, me, (me + k) % N_DEV).start()

    def wait(self, xin, xout, sems):
        me = _my_index()
        for t in range(self.n):
            for k in range(1, N_DEV):
                src = (me + N_DEV - k) % N_DEV
                pltpu.make_async_remote_copy(
                    src_ref=self._src(xin, t, src), dst_ref=xout[t].at[src], send_sem=sems[0].at[t, k],
                    recv_sem=sems[1].at[t, k], device_id=_coords(src), device_id_type=MESH).wait_recv()
        for t in range(self.n):
            for k in range(1, N_DEV):
                self._remote(xin, xout, sems, t, k, me, (me + k) % N_DEV).wait_send()
            pltpu.make_async_copy(self._src(xin, t, me), xout[t].at[me], sems[2].at[t]).wait()

    def forward(self, xin, xout, sems):
        pass


class _ChipGather:
    def __init__(self, srcs):
        self.srcs = list(srcs)
        self.n = len(self.srcs)
        self.out_shape = [jax.ShapeDtypeStruct((N_DEV,) + s.shape, s.dtype) for s in self.srcs]
        self.scratch = [pltpu.SemaphoreType.DMA((self.n, 7)), pltpu.SemaphoreType.DMA((self.n, 7)),
                        pltpu.SemaphoreType.DMA((self.n,))]

    @staticmethod
    def _places():
        x, y, c = _me()
        chips = [(1 - x, y), (x, 1 - y), (1 - x, 1 - y)]
        return (x, y, c), (x, y, 1 - c), chips

    @staticmethod
    def _slab(ref, dev):
        return ref.at[4 * dev[0] + 2 * dev[1] + dev[2]]

    def _copy(self, xin, xout, sems, t, k, block, to, src=None):
        return pltpu.make_async_remote_copy(
            src_ref=self._slab(xout[t], block) if src is None else src, dst_ref=self._slab(xout[t], block),
            send_sem=sems[0].at[t, k], recv_sem=sems[1].at[t, k], device_id=to, device_id_type=MESH)

    def start(self, xin, xout, sems):
        me, sibling, chips = self._places()
        for t in range(self.n):
            pltpu.make_async_copy(xin[t], self._slab(xout[t], me), sems[2].at[t]).start()
            self._copy(xin, xout, sems, t, 0, me, sibling, src=xin[t]).start()
            for j, chip in enumerate(chips):
                self._copy(xin, xout, sems, t, 1 + j, me, (*chip, me[2]), src=xin[t]).start()

    def forward(self, xin, xout, sems):
        me, sibling, chips = self._places()
        for t in range(self.n):
            for j, chip in enumerate(chips):
                self._copy(xin, xout, sems, t, 1 + j, (*chip, me[2]), me).wait_recv()
                self._copy(xin, xout, sems, t, 4 + j, (*chip, me[2]), sibling).start()

    def wait(self, xin, xout, sems):
        me, sibling, chips = self._places()
        for t in range(self.n):
            self._copy(xin, xout, sems, t, 0, sibling, me).wait_recv()
            for j, chip in enumerate(chips):
                self._copy(xin, xout, sems, t, 4 + j, (*chip, 1 - me[2]), me).wait_recv()
        for t in range(self.n):
            self._copy(xin, xout, sems, t, 0, me, sibling, src=xin[t]).wait_send()
            for j, chip in enumerate(chips):
                self._copy(xin, xout, sems, t, 1 + j, me, (*chip, me[2]), src=xin[t]).wait_send()
                self._copy(xin, xout, sems, t, 4 + j, (*chip, me[2]), sibling).wait_send()
            pltpu.make_async_copy(xin[t], self._slab(xout[t], me), sems[2].at[t]).wait()


def _call(body, *, name, grid, in_specs, out_specs, out_shape, args, semantics, scratch_shapes=(), xchg=None):
    if xchg is None:
        outs = pl.pallas_call(body, name=name, grid=grid, in_specs=in_specs, out_specs=out_specs, out_shape=out_shape,
                              scratch_shapes=list(scratch_shapes), compiler_params=_params(semantics))(*args)
        return outs, ()
    n_in, n_out, n_scr, n = len(in_specs), len(out_specs), len(scratch_shapes), xchg.n

    def carried(*refs):
        ins, xin = refs[:n_in], refs[n_in:n_in + n]
        outs, xout = refs[n_in + n:n_in + n + n_out], refs[n_in + n + n_out:n_in + 2 * n + n_out]
        scr, sems = refs[n_in + 2 * n + n_out:n_in + 2 * n + n_out + n_scr], refs[n_in + 2 * n + n_out + n_scr:]
        step = pl.program_id(0)
        for d in range(1, len(grid)):
            step = step * grid[d] + pl.program_id(d)
        n_steps = functools.reduce(lambda a, b: a * b, grid)

        @pl.when(step == 0)
        def _():
            xchg.start(xin, xout, sems)

        @pl.when(step == (2 * n_steps) // 3)
        def _():
            xchg.forward(xin, xout, sems)

        body(*ins, *outs, *scr)

        @pl.when(step == n_steps - 1)
        def _():
            xchg.wait(xin, xout, sems)

    res = pl.pallas_call(
        carried, name=name, grid=grid, in_specs=list(in_specs) + [ANY] * n, out_specs=list(out_specs) + [ANY] * n,
        out_shape=list(out_shape) + xchg.out_shape, scratch_shapes=list(scratch_shapes) + xchg.scratch,
        compiler_params=_params(("arbitrary",) * len(grid)))(*args, *xchg.srcs)
    return res[:n_out], tuple(res[n_out:])


def _exchange_alone(xchg, name):
    def body(*refs):
        xin, xout, sems = refs[:xchg.n], refs[xchg.n:2 * xchg.n], refs[2 * xchg.n:]
        xchg.start(xin, xout, sems)
        xchg.forward(xin, xout, sems)
        xchg.wait(xin, xout, sems)

    return pl.pallas_call(body, name=name, out_shape=xchg.out_shape, in_specs=[ANY] * xchg.n, out_specs=[ANY] * xchg.n,
                          scratch_shapes=xchg.scratch)(*xchg.srcs)


def _cast_shards(shards):
    n = len(shards)

    def body(*refs):
        for i, o in zip(refs[:n], refs[n:]):
            o[...] = i[...].astype(bf16)

    return pl.pallas_call(body, name="cast_shards", out_shape=[jax.ShapeDtypeStruct(s.shape, bf16) for s in shards],
                          in_specs=[VMEM] * n, out_specs=[VMEM] * n, compiler_params=_params())(*shards)


def _allreduce_rows(v):
    r = v.shape[0]
    rp = r // N_DEV

    def body(v_ref, o_ref, parts, sums, send1, recv1, send2, recv2):
        me = _my_index()

        def piece(ref, d):
            return ref.at[pl.ds(pl.multiple_of(d * rp, 8), rp), :]

        def copy1(k, src_dev, to):
            return pltpu.make_async_remote_copy(src_ref=piece(v_ref, to), dst_ref=parts.at[src_dev], send_sem=send1.at[k],
                                                recv_sem=recv1.at[k], device_id=_coords(to), device_id_type=MESH)

        def copy2(k, owner, to):
            return pltpu.make_async_remote_copy(src_ref=sums, dst_ref=piece(o_ref, owner), send_sem=send2.at[k],
                                                recv_sem=recv2.at[k], device_id=_coords(to), device_id_type=MESH)

        for k in range(1, N_DEV):
            copy1(k, me, (me + k) % N_DEV).start()
        parts[me] = v_ref[pl.ds(pl.multiple_of(me * rp, 8), rp), :]
        for k in range(1, N_DEV):
            copy1(k, (me + N_DEV - k) % N_DEV, me).wait_recv()
        total = parts[0]
        for s in range(1, N_DEV):
            total = total + parts[s]
        sums[...] = total
        o_ref[pl.ds(pl.multiple_of(me * rp, 8), rp), :] = total
        for k in range(1, N_DEV):
            copy2(k, me, (me + k) % N_DEV).start()
        for k in range(1, N_DEV):
            copy2(k, (me + N_DEV - k) % N_DEV, me).wait_recv()
        for k in range(1, N_DEV):
            copy1(k, me, (me + k) % N_DEV).wait_send()
            copy2(k, me, (me + k) % N_DEV).wait_send()

    return pl.pallas_call(
        body, name="allreduce_small_grads", out_shape=jax.ShapeDtypeStruct(v.shape, v.dtype),
        in_specs=[VMEM], out_specs=VMEM,
        scratch_shapes=[pltpu.VMEM((N_DEV, rp, LANES), f32), pltpu.VMEM((rp, LANES), f32)]
        + [pltpu.SemaphoreType.DMA((N_DEV,))] * 4,
        compiler_params=_params(),
    )(v)


def _gather_rows(v, name):
    def body(v_ref, o_ref, send_sems, recv_sems):
        me = _my_index()
        o_ref[me] = v_ref[...]
        sends = []
        for k in range(1, N_DEV):
            peer = (me + k) % N_DEV
            rc = pltpu.make_async_remote_copy(src_ref=v_ref, dst_ref=o_ref.at[me], send_sem=send_sems.at[k],
                                              recv_sem=recv_sems.at[k], device_id=_coords(peer), device_id_type=MESH)
            rc.start()
            sends.append(rc)
        for k in range(1, N_DEV):
            src = (me + N_DEV - k) % N_DEV
            pltpu.make_async_remote_copy(src_ref=v_ref, dst_ref=o_ref.at[src], send_sem=send_sems.at[k],
                                         recv_sem=recv_sems.at[k], device_id=_coords(src), device_id_type=MESH).wait_recv()
        for rc in sends:
            rc.wait_send()

    return pl.pallas_call(
        body, name=name, out_shape=jax.ShapeDtypeStruct((N_DEV,) + v.shape, v.dtype),
        in_specs=[VMEM], out_specs=VMEM,
        scratch_shapes=[pltpu.SemaphoreType.DMA((N_DEV,)), pltpu.SemaphoreType.DMA((N_DEV,))],
        compiler_params=pltpu.CompilerParams(vmem_limit_bytes=VMEM_LIMIT),
    )(v)


def _all_to_all_rows(v, name):
    def body(v_ref, o_ref, send_sems, recv_sems):
        me = _my_index()
        o_ref[me] = v_ref[me]
        sends = []
        for k in range(1, N_DEV):
            peer = (me + k) % N_DEV
            rc = pltpu.make_async_remote_copy(src_ref=v_ref.at[peer], dst_ref=o_ref.at[me], send_sem=send_sems.at[k],
                                              recv_sem=recv_sems.at[k], device_id=_coords(peer), device_id_type=MESH)
            rc.start()
            sends.append(rc)
        for k in range(1, N_DEV):
            src = (me + N_DEV - k) % N_DEV
            pltpu.make_async_remote_copy(src_ref=v_ref.at[src], dst_ref=o_ref.at[src], send_sem=send_sems.at[k],
                                         recv_sem=recv_sems.at[k], device_id=_coords(src), device_id_type=MESH).wait_recv()
        for rc in sends:
            rc.wait_send()

    return pl.pallas_call(
        body, name=name, out_shape=jax.ShapeDtypeStruct(v.shape, v.dtype),
        in_specs=[VMEM], out_specs=VMEM,
        scratch_shapes=[pltpu.SemaphoreType.DMA((N_DEV,)), pltpu.SemaphoreType.DMA((N_DEV,))],
    )(v)


def _ada_forward(c_all, ada_w, ada_b_cols):
    def body(c_ref, w_ref, b_ref, cond_ref, o_ref):
        cond = _silu(c_ref[...])
        cond_ref[...] = cond
        for l in range(2):
            o_ref[l] = _dot(_b(cond), _b(w_ref[l])) + b_ref[l]

    return pl.pallas_call(
        body, name="ada_forward",
        out_shape=[jax.ShapeDtypeStruct((N_DEV, D_MODEL), f32), jax.ShapeDtypeStruct((2, N_DEV, 768), f32)],
        in_specs=[VMEM] * 3, out_specs=[VMEM] * 2, compiler_params=_params(),
    )(c_all, ada_w, ada_b_cols)


def _ada_backward(cond, dmod_rows):
    def body(c_ref, d_ref, o_ref):
        cb = _b(c_ref[...])
        for l in range(2):
            o_ref[l] = _dot_tn(cb, _b(d_ref[l]))

    return pl.pallas_call(
        body, name="ada_backward", out_shape=jax.ShapeDtypeStruct((2, D_MODEL, 768), f32),
        in_specs=[VMEM] * 2, out_specs=VMEM, compiler_params=_params(),
    )(cond, dmod_rows)


def _inproj_fwd(h, norm_w, sc, sh, w_in, tb, xchg=None):
    t = h.shape[0]

    def body(h_ref, nw_ref, sc_ref, sh_ref, w_ref, proj_ref, u_ref):
        n, _ = _rms(h_ref[...])
        u = _b(n * nw_ref[...] * (1.0 + sc_ref[...]) + sh_ref[...])
        u_ref[...] = u
        proj_ref[...] = _dot(u, w_ref[...])

    row = pl.BlockSpec((tb, D_MODEL), lambda i: (i, 0))
    vec = _full((1, D_MODEL))
    return _call(
        body, name="inproj_fwd", grid=(t // tb,),
        out_shape=[jax.ShapeDtypeStruct((t, P_IN), f32), jax.ShapeDtypeStruct((t, D_MODEL), bf16)],
        in_specs=[row, vec, vec, vec, _full((D_MODEL, P_IN))],
        out_specs=[pl.BlockSpec((tb, P_IN), lambda i: (i, 0)), row],
        semantics=("parallel",), args=(h, norm_w, sc, sh, w_in), xchg=xchg)


def _inproj_bwd(dparts, dh_res, h, norm_w, sc, sh, w_in, tb, xchg=None):
    t = h.shape[0]

    def body(*refs):
        parts = refs[:10]
        dres_ref, h_ref, nw_ref, sc_ref, sh_ref, w_ref = refs[10:16]
        dh_ref, dsh_ref, dsc_ref, dnw_ref = refs[16:]
        dproj = jnp.concatenate([p[...] for p in parts], axis=1)
        du = _dot_nt(dproj, w_ref[...])
        n, r = _rms(h_ref[...])
        nw = nw_ref[...]
        gain = 1.0 + sc_ref[...]
        _acc(dsh_ref, _colsum(du))
        _acc(dsc_ref, _colsum(du * n * nw))
        _acc(dnw_ref, _colsum(du * gain * n))
        dh_ref[...] = dres_ref[...] + _rms_bwd(du * nw * gain, n, r)

    row = pl.BlockSpec((tb, D_MODEL), lambda i: (i, 0))
    vec = _full((1, D_MODEL))
    part_specs = [pl.BlockSpec((tb, GROUP_W), lambda i: (i, 0))] * 9 + [pl.BlockSpec((tb, LANES), lambda i: (i, 0))]
    return _call(
        body, name="inproj_bwd", grid=(t // tb,),
        out_shape=[jax.ShapeDtypeStruct((t, D_MODEL), f32)] + [jax.ShapeDtypeStruct((1, D_MODEL), f32)] * 3,
        in_specs=part_specs + [row, row, vec, vec, vec, _full((D_MODEL, P_IN))],
        out_specs=[row, vec, vec, vec],
        semantics=("arbitrary",), xchg=xchg, args=(*dparts, dh_res, h, norm_w, sc, sh, w_in))


def _wgrad(a, b, n_blocks, name, tm, tk=512):
    t, m = a.shape
    nb = b.shape[1] // n_blocks
    tk = min(tk, t)
    nk = t // tk

    def body(a_ref, b_ref, o_ref, acc_ref):
        k = pl.program_id(2)
        p = _dot_tn(a_ref[...], b_ref[...])

        @pl.when(k == 0)
        def _():
            acc_ref[...] = p

        @pl.when(k != 0)
        def _():
            acc_ref[...] += p

        @pl.when(k == nk - 1)
        def _():
            o_ref[0] = acc_ref[...].astype(o_ref.dtype)

    return pl.pallas_call(
        body, name=name, grid=(m // tm, n_blocks, nk),
        out_shape=jax.ShapeDtypeStruct((n_blocks, m, nb), bf16),
        in_specs=[pl.BlockSpec((tk, tm), lambda i, j, k: (k, i)), pl.BlockSpec((tk, nb), lambda i, j, k: (k, j))],
        out_specs=pl.BlockSpec((1, tm, nb), lambda i, j, k: (j, i, 0)),
        scratch_shapes=[pltpu.VMEM((tm, nb), f32)],
        compiler_params=_params(("parallel", "parallel", "arbitrary")),
    )(a, b)


def _wgrad_parts(a, parts, name, tm, tk):
    t, m = a.shape
    n = sum(p.shape[1] for p in parts)
    n_parts = len(parts)
    tk = min(tk, t)
    nk = t // tk

    def body(*refs):
        a_ref, part_refs, o_ref, acc_ref = refs[0], refs[1:1 + n_parts], refs[1 + n_parts], refs[2 + n_parts]
        k = pl.program_id(1)
        p = _dot_tn(a_ref[...], jnp.concatenate([r[...] for r in part_refs], axis=1))

        @pl.when(k == 0)
        def _():
            acc_ref[...] = p

        @pl.when(k != 0)
        def _():
            acc_ref[...] += p

        @pl.when(k == nk - 1)
        def _():
            o_ref[...] = acc_ref[...].astype(o_ref.dtype)

    return pl.pallas_call(
        body, name=name, grid=(m // tm, nk),
        out_shape=jax.ShapeDtypeStruct((m, n), bf16),
        in_specs=[pl.BlockSpec((tk, tm), lambda i, k: (k, i))]
        + [pl.BlockSpec((tk, p.shape[1]), lambda i, k: (k, 0)) for p in parts],
        out_specs=pl.BlockSpec((tm, n), lambda i, k: (i, 0)),
        scratch_shapes=[pltpu.VMEM((tm, n), f32)],
        compiler_params=_params(("parallel", "arbitrary")),
    )(a, *parts)


def _pool_counts(rows, t0):
    tpos = (lax.broadcasted_iota(jnp.int32, (rows, GROUP_W), 0) + t0 + 1).astype(f32)
    grp = lax.broadcasted_iota(jnp.int32, (rows, GROUP_W), 1) // 64
    win = jnp.where(grp == 0, 2.0, jnp.where(grp == 1, 4.0, jnp.where(grp == 2, 8.0, 16.0)))
    return jnp.minimum(tpos, win), grp


def _pool_select(grp, l1, l2, l3, l4):
    return jnp.where(grp == 0, l1, jnp.where(grp == 1, l2, jnp.where(grp == 2, l3, l4)))


def _pool_means(v, halo, t0):
    tb = v.shape[0]
    ext = jnp.concatenate([halo, v], axis=0)
    n = tb + 16
    s1 = ext[1:n] + ext[0:n - 1]
    s2 = s1[2:n - 1] + s1[0:n - 3]
    s3 = s2[4:n - 3] + s2[0:n - 7]
    s4 = s3[8:n - 7] + s3[0:n - 15]
    cnt, grp = _pool_counts(tb, t0)
    wsum = _pool_select(grp, s1[15:15 + tb], s2[13:13 + tb], s3[9:9 + tb], s4[1:1 + tb])
    return wsum / cnt - v


def _pool_fwd(proj, pw_bd, scale, tb):
    t = proj.shape[0]

    def body(v_ref, vh_ref, pw_ref, sc_ref, o_ref):
        i = pl.program_id(0)
        halo = jnp.where(i > 0, vh_ref[...], 0.0)
        p = _pool_means(v_ref[...], halo, i * tb)
        o_ref[...] = _dot(_b(p), _b(pw_ref[...])) * sc_ref[...]

    return pl.pallas_call(
        body, name="pool_fwd", grid=(t // tb,),
        out_shape=jax.ShapeDtypeStruct((t, GROUP_W), f32),
        in_specs=[pl.BlockSpec((tb, GROUP_W), lambda i: (i, C_POOL)),
                  pl.BlockSpec((16, GROUP_W), lambda i: (jnp.maximum(i * (tb // 16) - 1, 0), C_POOL)),
                  _full((GROUP_W, GROUP_W)), _full((1, GROUP_W))],
        out_specs=pl.BlockSpec((tb, GROUP_W), lambda i: (i, 0)),
        compiler_params=_params(("parallel",)),
    )(proj, proj, pw_bd, scale)


def _pool_bwd(proj, dy, pw_bd, scale, tb):
    t = proj.shape[0]
    nt = t // tb
    last16 = t // 16 - 1

    def body(v_ref, vh_ref, dy_ref, dyh_ref, pw_ref, sc_ref, dv_ref, dpw_ref, dsc_ref):
        i = pl.program_id(0)
        halo = jnp.where(i > 0, vh_ref[...], 0.0)
        p = _pool_means(v_ref[...], halo, i * tb)
        pw = _b(pw_ref[...])
        sc = sc_ref[...]
        dy = dy_ref[...]
        ypre = _dot(_b(p), pw)
        _acc(dsc_ref, _colsum(dy * ypre))
        dys = _b(dy * sc)
        _acc(dpw_ref, _dot_tn(_b(p), dys))
        dp = _dot_nt(dys, pw)
        dph = _dot_nt(_b(jnp.where(i < nt - 1, dyh_ref[...], 0.0) * sc), pw)
        cnt, grp = _pool_counts(tb, i * tb)
        cnth, _ = _pool_counts(16, (i + 1) * tb)
        ext = jnp.concatenate([dp / cnt, dph / cnth], axis=0)
        n = tb + 16
        f1 = ext[0:n - 1] + ext[1:n]
        f2 = f1[0:n - 3] + f1[2:n - 1]
        f3 = f2[0:n - 7] + f2[4:n - 3]
        f4 = f3[0:n - 15] + f3[8:n - 7]
        dv_ref[...] = _b(_pool_select(grp, f1[0:tb], f2[0:tb], f3[0:tb], f4[0:tb]) - dp)

    return pl.pallas_call(
        body, name="pool_bwd", grid=(nt,),
        out_shape=[jax.ShapeDtypeStruct((t, GROUP_W), bf16), jax.ShapeDtypeStruct((GROUP_W, GROUP_W), f32),
                   jax.ShapeDtypeStruct((1, GROUP_W), f32)],
        in_specs=[pl.BlockSpec((tb, GROUP_W), lambda i: (i, C_POOL)),
                  pl.BlockSpec((16, GROUP_W), lambda i: (jnp.maximum(i * (tb // 16) - 1, 0), C_POOL)),
                  pl.BlockSpec((tb, GROUP_W), lambda i: (i, 0)),
                  pl.BlockSpec((16, GROUP_W), lambda i: (jnp.minimum((i + 1) * (tb // 16), last16), 0)),
                  _full((GROUP_W, GROUP_W)), _full((1, GROUP_W))],
        out_specs=[pl.BlockSpec((tb, GROUP_W), lambda i: (i, 0)), _full((GROUP_W, GROUP_W)), _full((1, GROUP_W))],
        compiler_params=_params(("arbitrary",)),
    )(proj, proj, dy, dy, pw_bd, scale)


def _sconv_fwd(proj, w, tb):
    t = proj.shape[0]

    def body(gb_ref, gc_ref, hh_ref, gch_ref, hhh_ref, w_ref, o_ref):
        i = pl.program_id(0)
        q = gc_ref[...] * hh_ref[...]
        qh = jnp.where(i > 0, gch_ref[...] * hhh_ref[...], 0.0)
        ext = jnp.concatenate([qh, q], axis=0)
        w = w_ref[...]
        conv = w[0:1] * ext[6:6 + tb] + w[1:2] * ext[7:7 + tb] + w[2:3] * ext[8:8 + tb]
        o_ref[...] = gb_ref[...] * conv

    def col(c):
        return pl.BlockSpec((tb, GROUP_W), lambda i: (i, c))

    def prev(c):
        return pl.BlockSpec((8, GROUP_W), lambda i: (jnp.maximum(i * (tb // 8) - 1, 0), c))

    return pl.pallas_call(
        body, name="sconv_fwd", grid=(t // tb,),
        out_shape=jax.ShapeDtypeStruct((t, GROUP_W), f32),
        in_specs=[col(C_GB), col(C_GC), col(C_HH), prev(C_GC), prev(C_HH), _full((8, GROUP_W))],
        out_specs=pl.BlockSpec((tb, GROUP_W), lambda i: (i, 0)),
        compiler_params=_params(("parallel",)),
    )(proj, proj, proj, proj, proj, w)


def _sconv_bwd(proj, dy, w, tb):
    t = proj.shape[0]
    nt = t // tb
    last8 = t // 8 - 1

    def body(gb_ref, gc_ref, hh_ref, gch_ref, hhh_ref, gbn_ref, dy_ref, dyn_ref, w_ref, dgb_ref, dgc_ref, dhh_ref, dw_ref):
        i = pl.program_id(0)
        gc, hh, gb, dy = gc_ref[...], hh_ref[...], gb_ref[...], dy_ref[...]
        q = gc * hh
        qh = jnp.where(i > 0, gch_ref[...] * hhh_ref[...], 0.0)
        ext = jnp.concatenate([qh, q], axis=0)
        w = w_ref[...]
        conv = w[0:1] * ext[6:6 + tb] + w[1:2] * ext[7:7 + tb] + w[2:3] * ext[8:8 + tb]
        dgb_ref[...] = _b(dy * conv)
        e = dy * gb
        en = jnp.where(i < nt - 1, dyn_ref[...] * gbn_ref[...], 0.0)
        exte = jnp.concatenate([e, en], axis=0)
        dq = w[2:3] * exte[0:tb] + w[1:2] * exte[1:1 + tb] + w[0:1] * exte[2:2 + tb]
        dgc_ref[...] = _b(dq * hh)
        dhh_ref[...] = _b(dq * gc)
        dw = jnp.concatenate([_colsum(e * ext[6:6 + tb]), _colsum(e * ext[7:7 + tb]), _colsum(e * ext[8:8 + tb]),
                              jnp.zeros((5, GROUP_W), f32)], axis=0)
        _acc(dw_ref, dw)

    def col(c):
        return pl.BlockSpec((tb, GROUP_W), lambda i: (i, c))

    def prev(c):
        return pl.BlockSpec((8, GROUP_W), lambda i: (jnp.maximum(i * (tb // 8) - 1, 0), c))

    def nxt(c):
        return pl.BlockSpec((8, GROUP_W), lambda i: (jnp.minimum((i + 1) * (tb // 8), last8), c))

    out = pl.BlockSpec((tb, GROUP_W), lambda i: (i, 0))
    return pl.pallas_call(
        body, name="sconv_bwd", grid=(nt,),
        out_shape=[jax.ShapeDtypeStruct((t, GROUP_W), bf16)] * 3 + [jax.ShapeDtypeStruct((8, GROUP_W), f32)],
        in_specs=[col(C_GB), col(C_GC), col(C_HH), prev(C_GC), prev(C_HH), nxt(C_GB), col(0), nxt(0), _full((8, GROUP_W))],
        out_specs=[out, out, out, _full((8, GROUP_W))],
        compiler_params=_params(("arbitrary",)),
    )(proj, proj, proj, proj, proj, proj, dy, dy, w)


def _conv4(xr, halo, w, bias):
    tb = xr.shape[0]
    ext = jnp.concatenate([halo, xr], axis=0)
    pre = w[0:1] * ext[5:5 + tb] + w[1:2] * ext[6:6 + tb] + w[2:3] * ext[7:7 + tb] + w[3:4] * ext[8:8 + tb] + bias
    return pre, ext


def _tri():
    r = lax.broadcasted_iota(jnp.int32, (SSD_CHUNK, SSD_CHUNK), 0)
    c = lax.broadcasted_iota(jnp.int32, (SSD_CHUNK, SSD_CHUNK), 1)
    return r >= c


def _lane_pick(vals):
    rows = vals[0].shape[0]
    lane = lax.broadcasted_iota(jnp.int32, (rows, LANES), 1)
    out = jnp.zeros((rows, LANES), f32)
    for h, v in enumerate(vals):
        out = jnp.where(lane == h, v, out)
    return out


def _ssd_fwd(proj, conv_w, conv_b, dt_bias, a_log, d_cols, tb, xchg=None):
    t = proj.shape[0]
    cpt = tb // SSD_CHUNK

    def body(z_ref, xs_ref, bm_ref, cm_ref, xsh_ref, bmh_ref, cmh_ref, dt_ref, cw_ref, cb_ref, dtb_ref, al_ref, dk_ref,
             o_ref, y_ref, st_ref, state):
        i = pl.program_id(0)

        @pl.when(i == 0)
        def _():
            state[...] = jnp.zeros_like(state)

        cw, cb = cw_ref[...], cb_ref[...]
        acts = []
        for j, (r, hr) in enumerate(((xs_ref, xsh_ref), (bm_ref, bmh_ref), (cm_ref, cmh_ref))):
            halo = jnp.where(i > 0, hr[...], 0.0)
            pre, _ = _conv4(r[...], halo, cw[:, j * 256:(j + 1) * 256], cb[:, j * 256:(j + 1) * 256])
            acts.append(_silu(pre))
        xs, bm, cm = acts
        dt = _softplus(dt_ref[...] + dtb_ref[...])
        a = -jnp.exp(al_ref[...])
        adt = dt * a
        tri = _tri()
        trif = tri.astype(f32)
        dk = dk_ref[...]
        for c in range(cpt):
            rows = slice(c * SSD_CHUNK, (c + 1) * SSD_CHUNK)
            acol = _dot_exact(trif, adt[rows])
            arow = acol.T
            dt_c = dt[rows]
            ys = []
            rowi = lax.broadcasted_iota(jnp.int32, (SSD_CHUNK, 1), 0)
            first = lax.broadcasted_iota(jnp.int32, (SSD_CHUNK, SSD_CHUNK), 1) < SSD_P
            for g in range(SSD_HEADS // 2):
                cols = slice(g * 128, (g + 1) * 128)
                cg, bg = _b(cm[rows, cols]), _b(bm[rows, cols])
                xg = xs[rows, cols]
                heads = (2 * g, 2 * g + 1)
                ac = [acol[:, h:h + 1] for h in heads]
                alast = [v[SSD_CHUNK - 1:SSD_CHUNK] for v in ac]
                dtw = jnp.where(first, dt_c[:, heads[0]:heads[0] + 1], dt_c[:, heads[1]:heads[1] + 1])
                eaw = jnp.where(first, jnp.exp(ac[0]), jnp.exp(ac[1]))
                wdw = jnp.where(first, jnp.exp(alast[0] - ac[0]), jnp.exp(alast[1] - ac[1]))
                xdt = xg * dtw
                xb = _b(xdt)
                gmat = _dot_nt(cg, bg)
                ydiag = []
                for k, h in enumerate(heads):
                    lm = jnp.exp(jnp.where(tri, ac[k] - arow[h:h + 1, :], -jnp.inf))
                    ydiag.append(_dot(_b(gmat * lm), xb[:, k * SSD_P:(k + 1) * SSD_P]))
                s_in = state[g]
                st_ref[c, g] = s_in
                ys.append(jnp.concatenate(ydiag, axis=1) + eaw * _dot_nt(cg, _b(s_in)) + xg * dk[:, cols])
                state[g] = jnp.where(rowi < SSD_P, jnp.exp(alast[0]), jnp.exp(alast[1])) * s_in + _dot_tn(_b(xdt * wdw), bg)
            yc = jnp.concatenate(ys, axis=1)
            y_ref[rows, :] = yc
            o_ref[rows, :] = yc * _silu(z_ref[rows, :])

    def col(c):
        return pl.BlockSpec((tb, GROUP_W), lambda i: (i, c))

    def prev(c):
        return pl.BlockSpec((8, GROUP_W), lambda i: (jnp.maximum(i * (tb // 8) - 1, 0), c))

    out = pl.BlockSpec((tb, GROUP_W), lambda i: (i, 0))
    return _call(
        body, name="ssd_fwd", grid=(t // tb,),
        out_shape=[jax.ShapeDtypeStruct((t, GROUP_W), f32), jax.ShapeDtypeStruct((t, GROUP_W), f32),
                   jax.ShapeDtypeStruct((t // SSD_CHUNK, 2, 128, 128), f32)],
        in_specs=[col(C_Z), col(C_XS), col(C_BM), col(C_CM), prev(C_XS), prev(C_BM), prev(C_CM),
                  pl.BlockSpec((tb, LANES), lambda i: (i, C_DT128)),
                  _full((8, 768)), _full((1, 768)), _full((1, LANES)), _full((1, LANES)), _full((1, GROUP_W))],
        out_specs=[out, out, pl.BlockSpec((cpt, 2, 128, 128), lambda i: (i, 0, 0, 0))],
        scratch_shapes=[pltpu.VMEM((2, 128, 128), f32)],
        semantics=("arbitrary",), xchg=xchg,
        args=(proj, proj, proj, proj, proj, proj, proj, proj, conv_w, conv_b, dt_bias, a_log, d_cols))


def _ssd_bwd(proj, dyc, y_pre, states, conv_w, conv_b, dt_bias, a_log, d_cols, tb, xchg=None):
    t = proj.shape[0]
    nt = t // tb
    cpt = tb // SSD_CHUNK

    def body(z_ref, xs_ref, bm_ref, cm_ref, xsh_ref, bmh_ref, cmh_ref, dt_ref, dy_ref, yp_ref, st_ref,
             cw_ref, cb_ref, dtb_ref, al_ref, dk_ref,
             dz_ref, dxs_ref, dbm_ref, dcm_ref, ddt_ref, dcw_ref, dcb_ref, ddtb_ref, dal_ref, ddk_ref,
             dstate, carry):
        i = pl.program_id(0)
        ti = nt - 1 - i

        @pl.when(i == 0)
        def _():
            dstate[...] = jnp.zeros_like(dstate)
            carry[...] = jnp.zeros_like(carry)

        cw, cb = cw_ref[...], cb_ref[...]
        pres, exts, acts = [], [], []
        for j, (r, hr) in enumerate(((xs_ref, xsh_ref), (bm_ref, bmh_ref), (cm_ref, cmh_ref))):
            halo = jnp.where(ti > 0, hr[...], 0.0)
            pre, ext = _conv4(r[...], halo, cw[:, j * 256:(j + 1) * 256], cb[:, j * 256:(j + 1) * 256])
            pres.append(pre)
            exts.append(ext)
            acts.append(_silu(pre))
        xs, bm, cm = acts
        raw = dt_ref[...] + dtb_ref[...]
        dt = _softplus(raw)
        a = -jnp.exp(al_ref[...])
        adt = dt * a
        tri = _tri()
        trif = tri.astype(f32)
        dk = dk_ref[...]
        z = z_ref[...]
        dyc = dy_ref[...]
        dz_ref[...] = _b(dyc * yp_ref[...] * _dsilu(z))
        dy_all = dyc * _silu(z)
        lane = lax.broadcasted_iota(jnp.int32, (1, LANES), 1)
        ddk_acc = jnp.zeros((1, LANES), f32)
        dal_acc = jnp.zeros((1, LANES), f32)
        dxs_c, dbm_c, dcm_c, ddt_c = [None] * cpt, [None] * cpt, [None] * cpt, [None] * cpt
        for c in reversed(range(cpt)):
            rows = slice(c * SSD_CHUNK, (c + 1) * SSD_CHUNK)
            acol = _dot_exact(trif, adt[rows])
            arow = acol.T
            dt_c = dt[rows]
            da_cols, da_rows, ddt_heads, dxs_groups, dbg, dcg = [], [], [], [], [], []
            rowi = lax.broadcasted_iota(jnp.int32, (SSD_CHUNK, 1), 0)
            first = lax.broadcasted_iota(jnp.int32, (SSD_CHUNK, SSD_CHUNK), 1) < SSD_P
            for g in range(SSD_HEADS // 2):
                cols = slice(g * 128, (g + 1) * 128)
                cgf, bgf = cm[rows, cols], bm[rows, cols]
                cg, bg = _b(cgf), _b(bgf)
                xg, dyg = xs[rows, cols], dy_all[rows, cols]
                s_in, dsn = st_ref[c, g], dstate[g]
                sb, dsnb = _b(s_in), _b(dsn)
                heads = (2 * g, 2 * g + 1)
                ac = [acol[:, h:h + 1] for h in heads]
                alast = [v[SSD_CHUNK - 1:SSD_CHUNK] for v in ac]
                el = [jnp.exp(v) for v in alast]
                dtw = jnp.where(first, dt_c[:, heads[0]:heads[0] + 1], dt_c[:, heads[1]:heads[1] + 1])
                eaw = jnp.where(first, jnp.exp(ac[0]), jnp.exp(ac[1]))
                wdw = jnp.where(first, jnp.exp(alast[0] - ac[0]), jnp.exp(alast[1] - ac[1]))
                xdt = xg * dtw
                xb, dyb = _b(xdt), _b(dyg)
                gmat = _dot_nt(cg, bg)
                dgs, dxh, da = None, [], []
                for k, h in enumerate(heads):
                    hc = slice(k * SSD_P, (k + 1) * SSD_P)
                    lm = jnp.exp(jnp.where(tri, ac[k] - arow[h:h + 1, :], -jnp.inf))
                    m = gmat * lm
                    dm = _dot_nt(dyb[:, hc], xb[:, hc])
                    dxh.append(_dot_tn(_b(m), dyb[:, hc]))
                    dgs = dm * lm if dgs is None else dgs + dm * lm
                    wm = dm * m
                    da.append(jnp.sum(wm, axis=1, keepdims=True))
                    da_rows.append(jnp.sum(wm, axis=0, keepdims=True))
                dgb = _b(dgs)
                dcg_g = _dot(dgb, bg)
                dbg_g = _dot_tn(dgb, cg)
                yoff = eaw * _dot_nt(cg, sb)
                dyoff = dyg * yoff
                dye = _b(dyg * eaw)
                dcg_g = dcg_g + _dot(dye, sb)
                ds_y = _dot_tn(dye, cg)
                u = _dot_nt(bg, dsnb)
                dx = jnp.concatenate(dxh, axis=1) + wdw * u
                dbg_g = dbg_g + _dot(_b(xdt * wdw), dsnb)
                xu = xdt * u * wdw
                ss = jnp.sum(dsn * s_in, axis=1, keepdims=True)
                dxx = dx * xg
                dyx = _colsum(dyg * xg)
                for k, h in enumerate(heads):
                    mine = first if k == 0 else jnp.logical_not(first)
                    dwv = jnp.sum(jnp.where(mine, xu, 0.0), axis=1, keepdims=True)
                    mine_rows = (rowi < SSD_P) if k == 0 else (rowi >= SSD_P)
                    dalast = jnp.sum(dwv, axis=0, keepdims=True) + el[k] * jnp.sum(jnp.where(mine_rows, ss, 0.0), axis=0, keepdims=True)
                    dah = da[k] + jnp.sum(jnp.where(mine, dyoff, 0.0), axis=1, keepdims=True) - dwv
                    da_cols.append(dah + jnp.where(rowi == SSD_CHUNK - 1, dalast, 0.0))
                    ddt_heads.append(jnp.sum(jnp.where(mine, dxx, 0.0), axis=1, keepdims=True))
                    ddk_acc = ddk_acc + jnp.where(lane == h, jnp.sum(jnp.where(mine[0:1], dyx, 0.0), axis=1, keepdims=True), 0.0)
                dstate[g] = jnp.where(rowi < SSD_P, el[0], el[1]) * dsn + ds_y
                dxs_groups.append(dx * dtw + dyg * dk[:, cols])
                dbg.append(dbg_g)
                dcg.append(dcg_g)
            da_blk = _lane_pick(da_cols)
            rowsel = lax.broadcasted_iota(jnp.int32, (SSD_CHUNK, SSD_CHUNK), 0)
            da_rows_blk = jnp.zeros((SSD_CHUNK, SSD_CHUNK), f32)
            for h in range(SSD_HEADS):
                da_rows_blk = jnp.where(rowsel == h, da_rows[h], da_rows_blk)
            da_blk = da_blk - da_rows_blk.T
            dadt = lax.dot_general(trif, da_blk, (((0,), (0,)), ((), ())), preferred_element_type=f32,
                                   precision=lax.Precision.HIGHEST)
            dal_acc = dal_acc + _colsum(dadt * dt_c)
            ddt_c[c] = dadt * a + _lane_pick(ddt_heads)
            dxs_c[c] = jnp.concatenate(dxs_groups, axis=1)
            dbm_c[c] = jnp.concatenate(dbg, axis=1)
            dcm_c[c] = jnp.concatenate(dcg, axis=1)
        ddt = jnp.concatenate(ddt_c, axis=0) if cpt > 1 else ddt_c[0]
        ddraw = jnp.where(lane < SSD_HEADS, ddt * jax.nn.sigmoid(raw), 0.0)
        ddt_ref[...] = _b(ddraw)
        _acc(ddtb_ref, _colsum(ddraw))
        _acc(dal_ref, jnp.where(lane < SSD_HEADS, dal_acc * a, 0.0))
        _acc(ddk_ref, ddk_acc)
        dcw_parts, dcb_parts = [], []
        for j, (dparts, out_ref) in enumerate(((dxs_c, dxs_ref), (dbm_c, dbm_ref), (dcm_c, dcm_ref))):
            dact = jnp.concatenate(dparts, axis=0) if cpt > 1 else dparts[0]
            dpre = dact * _dsilu(pres[j])
            w = cw[:, j * 256:(j + 1) * 256]
            ext = jnp.concatenate([dpre, carry[:, j * 256:(j + 1) * 256]], axis=0)
            out_ref[...] = _b(w[3:4] * ext[0:tb] + w[2:3] * ext[1:1 + tb] + w[1:2] * ext[2:2 + tb] + w[0:1] * ext[3:3 + tb])
            carry[:, j * 256:(j + 1) * 256] = dpre[0:8]
            xe = exts[j]
            dcw_parts.append(jnp.concatenate([_colsum(dpre * xe[5 + k:5 + k + tb]) for k in range(4)]
                                             + [jnp.zeros((4, GROUP_W), f32)], axis=0))
            dcb_parts.append(_colsum(dpre))
        _acc(dcw_ref, jnp.concatenate(dcw_parts, axis=1))
        _acc(dcb_ref, jnp.concatenate(dcb_parts, axis=1))

    def col(c):
        return pl.BlockSpec((tb, GROUP_W), lambda i: (nt - 1 - i, c))

    def prev(c):
        return pl.BlockSpec((8, GROUP_W), lambda i: (jnp.maximum((nt - 1 - i) * (tb // 8) - 1, 0), c))

    out = pl.BlockSpec((tb, GROUP_W), lambda i: (nt - 1 - i, 0))
    vec = _full((1, LANES))
    return _call(
        body, name="ssd_bwd", grid=(nt,),
        out_shape=[jax.ShapeDtypeStruct((t, GROUP_W), bf16)] * 4 + [jax.ShapeDtypeStruct((t, LANES), bf16),
                   jax.ShapeDtypeStruct((8, 768), f32), jax.ShapeDtypeStruct((1, 768), f32)]
        + [jax.ShapeDtypeStruct((1, LANES), f32)] * 3,
        in_specs=[col(C_Z), col(C_XS), col(C_BM), col(C_CM), prev(C_XS), prev(C_BM), prev(C_CM),
                  pl.BlockSpec((tb, LANES), lambda i: (nt - 1 - i, C_DT128)), out, out,
                  pl.BlockSpec((cpt, 2, 128, 128), lambda i: (nt - 1 - i, 0, 0, 0)),
                  _full((8, 768)), _full((1, 768)), vec, vec, _full((1, GROUP_W))],
        out_specs=[out, out, out, out, pl.BlockSpec((tb, LANES), lambda i: (nt - 1 - i, 0)),
                   _full((8, 768)), _full((1, 768)), vec, vec, vec],
        scratch_shapes=[pltpu.VMEM((2, 128, 128), f32), pltpu.VMEM((8, 768), f32)],
        semantics=("arbitrary",), xchg=xchg,
        args=(proj, proj, proj, proj, proj, proj, proj, proj, dyc, y_pre, states, conv_w, conv_b, dt_bias, a_log, d_cols))


def _s5_coeffs(are, aim, ls):
    step = jnp.exp(ls)
    mag = jnp.exp(are * step)
    th = aim * step
    lre, lim = mag * jnp.cos(th), mag * jnp.sin(th)
    den = are * are + aim * aim
    nr = lre - 1.0
    fre = (nr * are + lim * aim) / den
    fim = (lim * are - nr * aim) / den
    return step, lre, lim, den, fre, fim


def _s5_prep(are, aim, ls, bre_bd, bim_bd):
    def body(are_ref, aim_ref, ls_ref, bre_ref, bim_ref, lre_ref, lim_ref, bbr_ref, bbi_ref):
        _, lre, lim, _, fre, fim = _s5_coeffs(are_ref[...], aim_ref[...], ls_ref[...])
        lre_ref[...] = lre
        lim_ref[...] = lim
        bre, bim = bre_ref[...], bim_ref[...]
        bbr_ref[...] = fre * bre - fim * bim
        bbi_ref[...] = fre * bim + fim * bre

    col = jax.ShapeDtypeStruct((S5_N, 1), f32)
    mat = jax.ShapeDtypeStruct((S5_N, GROUP_W), f32)
    return pl.pallas_call(body, name="s5_prep", out_shape=[col, col, mat, mat], in_specs=[VMEM] * 5, out_specs=[VMEM] * 4,
                          compiler_params=_params())(are, aim, ls, bre_bd, bim_bd)


def _s5_prep_bwd(are, aim, ls, bre_bd, bim_bd, dlre, dlim, dbbr, dbbi):
    def body(are_ref, aim_ref, ls_ref, bre_ref, bim_ref, dlre_ref, dlim_ref, dbbr_ref, dbbi_ref,
             dare_ref, daim_ref, dls_ref, dbre_ref, dbim_ref):
        are, aim = are_ref[...], aim_ref[...]
        step, lre, lim, den, fre, fim = _s5_coeffs(are, aim, ls_ref[...])
        r = lax.broadcasted_iota(jnp.int32, (S5_N, GROUP_W), 0) // 64
        c = lax.broadcasted_iota(jnp.int32, (S5_N, GROUP_W), 1) // 16
        mask = r == c
        gr = jnp.where(mask, dbbr_ref[...], 0.0)
        gi = jnp.where(mask, dbbi_ref[...], 0.0)
        bre, bim = bre_ref[...], bim_ref[...]
        dbre_ref[...] = fre * gr + fim * gi
        dbim_ref[...] = fre * gi - fim * gr
        dfre = jnp.sum(bre * gr + bim * gi, axis=1, keepdims=True)
        dfim = jnp.sum(bre * gi - bim * gr, axis=1, keepdims=True)
        ire, iim = are / den, aim / den
        tre = dlre_ref[...] + ire * dfre - iim * dfim
        tim = dlim_ref[...] + ire * dfim + iim * dfre
        dzre = lre * tre + lim * tim
        dzim = lre * tim - lim * tre
        qre = (fre * are + fim * aim) / den
        qim = (fim * are - fre * aim) / den
        dare_ref[...] = step * dzre - (qre * dfre + qim * dfim)
        daim_ref[...] = step * dzim - (qre * dfim - qim * dfre)
        dls = (are * dzre + aim * dzim) * step
        sel = (lax.broadcasted_iota(jnp.int32, (S5_N, LANES), 0) // 64 == lax.broadcasted_iota(jnp.int32, (S5_N, LANES), 1)).astype(f32)
        dls_ref[...] = lax.dot_general(sel, jnp.broadcast_to(dls, (S5_N, LANES)), (((0,), (0,)), ((), ())),
                                       preferred_element_type=f32, precision=lax.Precision.HIGHEST)

    col = jax.ShapeDtypeStruct((S5_N, 1), f32)
    mat = jax.ShapeDtypeStruct((S5_N, GROUP_W), f32)
    return pl.pallas_call(body, name="s5_prep_bwd", out_shape=[col, col, jax.ShapeDtypeStruct((LANES, LANES), f32), mat, mat],
                          in_specs=[VMEM] * 9, out_specs=[VMEM] * 5, compiler_params=_params(),
                          )(are, aim, ls, bre_bd, bim_bd, dlre, dlim, dbbr, dbbi)


def _cmul(ar, ai, br, bi):
    return ar * br - ai * bi, ar * bi + ai * br


def _s5_scan(re_ref, im_ref, carry_ref, mr, mi, n_groups, reverse):
    p1 = (mr, mi)
    p2 = _cmul(*p1, *p1)
    p3 = _cmul(*p2, *p1)
    p4 = _cmul(*p2, *p2)
    p5 = _cmul(*p4, *p1)
    p6 = _cmul(*p4, *p2)
    p7 = _cmul(*p4, *p3)
    p8 = _cmul(*p4, *p4)
    pows = [p1, p2, p3, p4, p5, p6, p7, p8]
    row = lax.broadcasted_iota(jnp.int32, (8, S5_N), 0)
    tr = jnp.zeros((8, S5_N), f32)
    ti = jnp.zeros((8, S5_N), f32)
    for i in range(8):
        p = pows[7 - i] if reverse else pows[i]
        tr = jnp.where(row == i, p[0], tr)
        ti = jnp.where(row == i, p[1], ti)
    steps = []
    for k, p in ((1, p1), (2, p2), (4, p4)):
        keep = (row + k < 8) if reverse else (row >= k)
        steps.append((8 - k if reverse else k, jnp.where(keep, p[0], 0.0), jnp.where(keep, p[1], 0.0)))
    edge = 0 if reverse else 7

    def step(j, carry):
        cr, ci = carry
        g = (n_groups - 1 - j) if reverse else j
        r0 = pl.multiple_of(g * 8, 8)
        xr = re_ref[pl.ds(r0, 8), :]
        xi = im_ref[pl.ds(r0, 8), :]
        for shift, br, bi in steps:
            sr = pltpu.roll(xr, shift, 0)
            si = pltpu.roll(xi, shift, 0)
            xr, xi = xr + br * sr - bi * si, xi + br * si + bi * sr
        xr, xi = xr + tr * cr - ti * ci, xi + tr * ci + ti * cr
        re_ref[pl.ds(r0, 8), :] = xr
        im_ref[pl.ds(r0, 8), :] = xi
        return (jnp.broadcast_to(xr[edge:edge + 1, :], (8, S5_N)), jnp.broadcast_to(xi[edge:edge + 1, :], (8, S5_N)))

    cr, ci = lax.fori_loop(0, n_groups, step, (carry_ref[0], carry_ref[1]))
    carry_ref[0] = cr
    carry_ref[1] = ci


def _s5_output(u, xr, xi, ctr, cti, d):
    return _dot_nt(_b(xr), _b(ctr)) - _dot_nt(_b(xi), _b(cti)) + d * u


def _s5_fwd(proj, bbr, bbi, ctr, cti, lre, lim, d, glu_w, glu_b, tb, xchg=None):
    t = proj.shape[0]

    def body(u_ref, bbr_ref, bbi_ref, ctr_ref, cti_ref, lr_ref, li_ref, d_ref, gw_ref, gb_ref, o_ref, xr_ref, xi_ref, carry):
        @pl.when(pl.program_id(0) == 0)
        def _():
            carry[...] = jnp.zeros_like(carry)

        u = u_ref[...]
        ub = _b(u)
        xr_ref[...] = _dot_nt(ub, _b(bbr_ref[...]))
        xi_ref[...] = _dot_nt(ub, _b(bbi_ref[...]))
        _s5_scan(xr_ref, xi_ref, carry, lr_ref[...], li_ref[...], tb // 8, reverse=False)
        y = _s5_output(u, xr_ref[...], xi_ref[...], ctr_ref[...], cti_ref[...], d_ref[...])
        gl = _gelu(y)
        o_ref[...] = gl * jax.nn.sigmoid(_dot(_b(gl), _b(gw_ref[...])) + gb_ref[...])

    state = pl.BlockSpec((tb, S5_N), lambda i: (i, 0))
    return _call(
        body, name="s5_fwd", grid=(t // tb,),
        out_shape=[jax.ShapeDtypeStruct((t, GROUP_W), f32), jax.ShapeDtypeStruct((t, S5_N), f32), jax.ShapeDtypeStruct((t, S5_N), f32)],
        in_specs=[pl.BlockSpec((tb, GROUP_W), lambda i: (i, C_S5)), _full((S5_N, GROUP_W)), _full((S5_N, GROUP_W)),
                  _full((GROUP_W, S5_N)), _full((GROUP_W, S5_N)), _full((1, S5_N)), _full((1, S5_N)),
                  _full((1, GROUP_W)), _full((GROUP_W, GROUP_W)), _full((1, GROUP_W))],
        out_specs=[pl.BlockSpec((tb, GROUP_W), lambda i: (i, 0)), state, state],
        scratch_shapes=[pltpu.VMEM((2, 8, S5_N), f32)],
        semantics=("arbitrary",), xchg=xchg, args=(proj, bbr, bbi, ctr, cti, lre, lim, d, glu_w, glu_b))


def _s5_bwd(proj, dyd, xr_all, xi_all, bbr, bbi, ctr, cti, lre, lim, d, glu_w, glu_b, tb, xchg=None):
    t = proj.shape[0]
    nt = t // tb

    def body(u_ref, dy_ref, xr_ref, xi_ref, xrh_ref, xih_ref, bbr_ref, bbi_ref, ctr_ref, cti_ref, lr_ref, li_ref,
             d_ref, gw_ref, gb_ref,
             du_ref, dlr_ref, dli_ref, dbbr_ref, dbbi_ref, dctr_ref, dcti_ref, dd_ref, dgw_ref, dgb_ref,
             gr_ref, gi_ref, carry):
        i = pl.program_id(0)
        ti = nt - 1 - i

        @pl.when(i == 0)
        def _():
            carry[...] = jnp.zeros_like(carry)

        u = u_ref[...]
        ub = _b(u)
        xr, xi = xr_ref[...], xi_ref[...]
        ctr, cti = _b(ctr_ref[...]), _b(cti_ref[...])
        d = d_ref[...]
        gw = _b(gw_ref[...])
        y = _s5_output(u, xr, xi, ctr, cti, d)
        gl = _gelu(y)
        sg = jax.nn.sigmoid(_dot(_b(gl), gw) + gb_ref[...])
        dout = dy_ref[...]
        q = dout * gl * sg * (1.0 - sg)
        qb = _b(q)
        dgl = dout * sg + _dot_nt(qb, gw)
        _acc(dgw_ref, _dot_tn(_b(gl), qb))
        _acc(dgb_ref, _colsum(q))
        dyv = dgl * _dgelu(y)
        _acc(dd_ref, _colsum(dyv * u))
        dyb = _b(dyv)
        gr_ref[...] = _dot(dyb, ctr)
        gi_ref[...] = -_dot(dyb, cti)
        _acc(dctr_ref, _dot_tn(dyb, _b(xr)))
        _acc(dcti_ref, -_dot_tn(dyb, _b(xi)))
        _s5_scan(gr_ref, gi_ref, carry, lr_ref[...], -li_ref[...], tb // 8, reverse=True)
        gr, gi = gr_ref[...], gi_ref[...]
        xpr = jnp.concatenate([jnp.where(ti > 0, xrh_ref[...], 0.0), xr], axis=0)[7:7 + tb]
        xpi = jnp.concatenate([jnp.where(ti > 0, xih_ref[...], 0.0), xi], axis=0)[7:7 + tb]
        _acc(dlr_ref, _colsum(gr * xpr + gi * xpi))
        _acc(dli_ref, _colsum(gi * xpr - gr * xpi))
        grb, gib = _b(gr), _b(gi)
        _acc(dbbr_ref, _dot_tn(grb, ub))
        _acc(dbbi_ref, _dot_tn(gib, ub))
        du_ref[...] = _b(dyv * d + _dot(grb, _b(bbr_ref[...])) + _dot(gib, _b(bbi_ref[...])))

    state = pl.BlockSpec((tb, S5_N), lambda i: (nt - 1 - i, 0))
    prev = pl.BlockSpec((8, S5_N), lambda i: (jnp.maximum((nt - 1 - i) * (tb // 8) - 1, 0), 0))
    tile = pl.BlockSpec((tb, GROUP_W), lambda i: (nt - 1 - i, 0))
    return _call(
        body, name="s5_bwd", grid=(nt,),
        out_shape=[jax.ShapeDtypeStruct((t, GROUP_W), bf16), jax.ShapeDtypeStruct((1, S5_N), f32), jax.ShapeDtypeStruct((1, S5_N), f32),
                   jax.ShapeDtypeStruct((S5_N, GROUP_W), f32), jax.ShapeDtypeStruct((S5_N, GROUP_W), f32),
                   jax.ShapeDtypeStruct((GROUP_W, S5_N), f32), jax.ShapeDtypeStruct((GROUP_W, S5_N), f32),
                   jax.ShapeDtypeStruct((1, GROUP_W), f32), jax.ShapeDtypeStruct((GROUP_W, GROUP_W), f32),
                   jax.ShapeDtypeStruct((1, GROUP_W), f32)],
        in_specs=[pl.BlockSpec((tb, GROUP_W), lambda i: (nt - 1 - i, C_S5)), tile, state, state, prev, prev,
                  _full((S5_N, GROUP_W)), _full((S5_N, GROUP_W)), _full((GROUP_W, S5_N)), _full((GROUP_W, S5_N)),
                  _full((1, S5_N)), _full((1, S5_N)), _full((1, GROUP_W)), _full((GROUP_W, GROUP_W)), _full((1, GROUP_W))],
        out_specs=[tile, _full((1, S5_N)), _full((1, S5_N)), _full((S5_N, GROUP_W)), _full((S5_N, GROUP_W)),
                   _full((GROUP_W, S5_N)), _full((GROUP_W, S5_N)), _full((1, GROUP_W)), _full((GROUP_W, GROUP_W)), _full((1, GROUP_W))],
        scratch_shapes=[pltpu.VMEM((tb, S5_N), f32), pltpu.VMEM((tb, S5_N), f32), pltpu.VMEM((2, 8, S5_N), f32)],
        semantics=("arbitrary",), xchg=xchg,
        args=(proj, dyd, xr_all, xi_all, xr_all, xi_all, bbr, bbi, ctr, cti, lre, lim, d, glu_w, glu_b))


def _outproj_fwd(ys, h, bn_w, g1, w_out, tb):
    t = h.shape[0]

    def body(ya_ref, yb_ref, yc_ref, yd_ref, h_ref, bn_ref, g1_ref, w_ref, h1_ref, o_ref, gr_ref):
        bn = bn_ref[...]
        parts = []
        for g, r in enumerate((ya_ref, yb_ref, yc_ref, yd_ref)):
            n, _ = _rms(r[...])
            parts.append(n * bn[:, g * GROUP_W:(g + 1) * GROUP_W])
        groups = _b(jnp.concatenate(parts, axis=1))
        gr_ref[...] = groups
        o = _dot(groups, w_ref[...])
        o_ref[...] = _b(o)
        h1_ref[...] = h_ref[...] + g1_ref[...] * o

    grp = pl.BlockSpec((tb, GROUP_W), lambda i: (i, 0))
    row = pl.BlockSpec((tb, D_MODEL), lambda i: (i, 0))
    vec = _full((1, D_MODEL))
    return pl.pallas_call(
        body, name="outproj_fwd", grid=(t // tb,),
        out_shape=[jax.ShapeDtypeStruct((t, D_MODEL), f32), jax.ShapeDtypeStruct((t, D_MODEL), bf16),
                   jax.ShapeDtypeStruct((t, D_MODEL), bf16)],
        in_specs=[grp, grp, grp, grp, row, vec, vec, _full((D_MODEL, D_MODEL))],
        out_specs=[row, row, row],
        compiler_params=_params(("parallel",)),
    )(*ys, h, bn_w, g1, w_out)


def _outproj_bwd(dh1, o, ys, bn_w, g1, w_out, tb):
    t = dh1.shape[0]

    def body(dh_ref, o_ref, ya_ref, yb_ref, yc_ref, yd_ref, bn_ref, g1_ref, w_ref,
             da_ref, db_ref, dc_ref, dd_ref, do_ref, dg1_ref, dbn_ref):
        dh = dh_ref[...]
        _acc(dg1_ref, _colsum(dh * o_ref[...].astype(f32)))
        do = _b(dh * g1_ref[...])
        do_ref[...] = do
        dgroups = _dot_nt(do, w_ref[...])
        bn = bn_ref[...]
        dbn = []
        for g, (r, dr) in enumerate(((ya_ref, da_ref), (yb_ref, db_ref), (yc_ref, dc_ref), (yd_ref, dd_ref))):
            n, rr = _rms(r[...])
            dgr = dgroups[:, g * GROUP_W:(g + 1) * GROUP_W]
            dbn.append(_colsum(dgr * n))
            dr[...] = _rms_bwd(dgr * bn[:, g * GROUP_W:(g + 1) * GROUP_W], n, rr)
        _acc(dbn_ref, jnp.concatenate(dbn, axis=1))

    grp = pl.BlockSpec((tb, GROUP_W), lambda i: (i, 0))
    row = pl.BlockSpec((tb, D_MODEL), lambda i: (i, 0))
    vec = _full((1, D_MODEL))
    return pl.pallas_call(
        body, name="outproj_bwd", grid=(t // tb,),
        out_shape=[jax.ShapeDtypeStruct((t, GROUP_W), f32)] * 4 + [jax.ShapeDtypeStruct((t, D_MODEL), bf16),
                   jax.ShapeDtypeStruct((1, D_MODEL), f32), jax.ShapeDtypeStruct((1, D_MODEL), f32)],
        in_specs=[row, row, grp, grp, grp, grp, vec, vec, _full((D_MODEL, D_MODEL))],
        out_specs=[grp, grp, grp, grp, row, vec, vec],
        compiler_params=_params(("arbitrary",)),
    )(dh1, o, *ys, bn_w, g1, w_out)


def _mlp_fwd(h1, norm_w, sc, sh, g2, w1, w2, tb, xchg=None):
    t = h1.shape[0]
    nh = w1.shape[0] // MLP_SLABS

    def body(h_ref, nw_ref, sc_ref, sh_ref, g2_ref, w1_ref, w2_ref, h2_ref, m_ref, v_ref, r_ref, acc):
        j = pl.program_id(1)

        @pl.when(j == 0)
        def _():
            n, _ = _rms(h_ref[...])
            v_ref[...] = _b(n * nw_ref[...] * (1.0 + sc_ref[...]) + sh_ref[...])

        v = v_ref[...]
        p = None
        for s in range(MLP_SLABS):
            ra = jnp.maximum(_dot(v, w1_ref[s]), 0.0)
            r = _b(ra * ra)
            r_ref[:, s * MLP_HB:(s + 1) * MLP_HB] = r
            q = _dot(r, w2_ref[s])
            p = q if p is None else p + q

        @pl.when(j == 0)
        def _():
            acc[...] = p

        @pl.when(j != 0)
        def _():
            acc[...] += p

        @pl.when(j == nh - 1)
        def _():
            m = acc[...]
            m_ref[...] = _b(m)
            h2_ref[...] = h_ref[...] + g2_ref[...] * m

    row = pl.BlockSpec((tb, D_MODEL), lambda i, j: (i, 0))
    hid = pl.BlockSpec((tb, MLP_SLABS * MLP_HB), lambda i, j: (i, j))
    vec = _full((1, D_MODEL))
    return _call(
        body, name="mlp_fwd", grid=(t // tb, nh),
        out_shape=[jax.ShapeDtypeStruct((t, D_MODEL), f32), jax.ShapeDtypeStruct((t, D_MODEL), bf16),
                   jax.ShapeDtypeStruct((t, D_MODEL), bf16), jax.ShapeDtypeStruct((t, N_DEV * MLP_HB), bf16)],
        in_specs=[row, vec, vec, vec, vec, pl.BlockSpec((MLP_SLABS, D_MODEL, MLP_HB), lambda i, j: (j, 0, 0)),
                  pl.BlockSpec((MLP_SLABS, MLP_HB, D_MODEL), lambda i, j: (j, 0, 0))],
        out_specs=[row, row, row, hid],
        scratch_shapes=[pltpu.VMEM((tb, D_MODEL), f32)],
        semantics=("arbitrary", "arbitrary"), xchg=xchg, args=(h1, norm_w, sc, sh, g2, w1, w2))


def _mlp_bwd(dh2, m, h1, r, norm_w, sc, sh, g2, w1, w2, tb, xchg=None):
    t = h1.shape[0]
    slabs = MLP_BWD_SLABS
    nh = w1.shape[0] // slabs

    def body(dh_ref, m_ref, h_ref, r_ref, nw_ref, sc_ref, sh_ref, g2_ref, w1_ref, w2_ref,
             dh1_ref, do_ref, da_ref, dg2_ref, dsh_ref, dsc_ref, dnw_ref, acc):
        j = pl.program_id(1)

        @pl.when(j == 0)
        def _():
            dh = dh_ref[...]
            _acc(dg2_ref, _colsum(dh * m_ref[...].astype(f32)))
            do_ref[...] = _b(dh * g2_ref[...])

        do = do_ref[...]
        p = None
        for s in range(slabs):
            cols = slice(s * MLP_HB, (s + 1) * MLP_HB)
            dr = _dot_nt(do, w2_ref[s])
            da = _b(dr * 2.0 * jnp.sqrt(r_ref[:, cols].astype(f32)))
            da_ref[:, cols] = da
            q = _dot_nt(da, w1_ref[s])
            p = q if p is None else p + q

        @pl.when(j == 0)
        def _():
            acc[...] = p

        @pl.when(j != 0)
        def _():
            acc[...] += p

        @pl.when(j == nh - 1)
        def _():
            dv = acc[...]
            n, r = _rms(h_ref[...])
            nw = nw_ref[...]
            gain = 1.0 + sc_ref[...]
            _acc(dsh_ref, _colsum(dv))
            _acc(dsc_ref, _colsum(dv * n * nw))
            _acc(dnw_ref, _colsum(dv * gain * n))
            dh1_ref[...] = dh_ref[...] + _rms_bwd(dv * nw * gain, n, r)

    row = pl.BlockSpec((tb, D_MODEL), lambda i, j: (i, 0))
    hid = pl.BlockSpec((tb, slabs * MLP_HB), lambda i, j: (i, j))
    vec = _full((1, D_MODEL))
    return _call(
        body, name="mlp_bwd", grid=(t // tb, nh),
        out_shape=[jax.ShapeDtypeStruct((t, D_MODEL), f32), jax.ShapeDtypeStruct((t, D_MODEL), bf16),
                   jax.ShapeDtypeStruct((t, N_DEV * MLP_HB), bf16)] + [jax.ShapeDtypeStruct((1, D_MODEL), f32)] * 4,
        in_specs=[row, row, row, hid, vec, vec, vec, vec,
                  pl.BlockSpec((slabs, D_MODEL, MLP_HB), lambda i, j: (j, 0, 0)),
                  pl.BlockSpec((slabs, MLP_HB, D_MODEL), lambda i, j: (j, 0, 0))],
        out_specs=[row, row, hid, vec, vec, vec, vec],
        scratch_shapes=[pltpu.VMEM((tb, D_MODEL), f32)],
        semantics=("arbitrary", "arbitrary"), xchg=xchg, args=(dh2, m, h1, r, norm_w, sc, sh, g2, w1, w2))


def _loss_head(h, target, norm_w, tb):
    t = h.shape[0]

    def body(h_ref, t_ref, w_ref, loss_ref, dh_ref, dw_ref):
        n, r = _rms(h_ref[...])
        w = w_ref[...]
        err = n * w - t_ref[...]
        part = 0.5 * jnp.sum(jnp.sum(err * err, axis=1, keepdims=True), axis=0, keepdims=True) / D_MODEL
        _acc(loss_ref, jnp.broadcast_to(part, (8, LANES)))
        dy = err / D_MODEL
        _acc(dw_ref, _colsum(dy * n))
        dh_ref[...] = _rms_bwd(dy * w, n, r)

    row = pl.BlockSpec((tb, D_MODEL), lambda i: (i, 0))
    return pl.pallas_call(
        body, name="loss_head", grid=(t // tb,),
        out_shape=[jax.ShapeDtypeStruct((8, LANES), f32), jax.ShapeDtypeStruct((t, D_MODEL), f32),
                   jax.ShapeDtypeStruct((1, D_MODEL), f32)],
        in_specs=[row, row, _full((1, D_MODEL))],
        out_specs=[_full((8, LANES)), row, _full((1, D_MODEL))],
        compiler_params=_params(("arbitrary",)),
    )(h, target, norm_w)


def _adam_math(w, g, m, v):
    m2 = ADAM_B1 * m + (1.0 - ADAM_B1) * g
    v2 = ADAM_B2 * v + (1.0 - ADAM_B2) * (g * g)
    mh = m2 / (1.0 - ADAM_B1 ** ADAM_STEP)
    vh = v2 / (1.0 - ADAM_B2 ** ADAM_STEP)
    return -ADAM_LR * (mh / (jnp.sqrt(vh) + ADAM_EPS) + ADAM_WD * w), m2, v2


def _sum_adamw(parts, w, m, v, name, rb):
    n_src, r, c = parts.shape

    def body(p_ref, w_ref, m_ref, v_ref, g_ref, d_ref, m2_ref, v2_ref):
        g = p_ref[0].astype(f32)
        for s in range(1, n_src):
            g = g + p_ref[s].astype(f32)
        g_ref[...] = g
        d, m2, v2 = _adam_math(w_ref[...], g, m_ref[...], v_ref[...])
        d_ref[...] = d
        m2_ref[...] = m2
        v2_ref[...] = v2

    blk = pl.BlockSpec((rb, c), lambda i: (i, 0))
    return pl.pallas_call(
        body, name=name, grid=(r // rb,),
        out_shape=[jax.ShapeDtypeStruct((r, c), f32)] * 4,
        in_specs=[pl.BlockSpec((n_src, rb, c), lambda i: (0, i, 0)), blk, blk, blk],
        out_specs=[blk] * 4,
        compiler_params=_params(("parallel",)),
    )(parts, w, m, v)


def _sum_adamw_layers(parts0, parts1, w, m, v, name, rb):
    n_src, r, c = parts0.shape
    nb = r // rb

    def body(p0_ref, p1_ref, w_ref, m_ref, v_ref, g_ref, d_ref, m2_ref, v2_ref):
        def update(p_ref):
            g = p_ref[0].astype(f32)
            for s in range(1, n_src):
                g = g + p_ref[s].astype(f32)
            g_ref[0] = g
            d, m2, v2 = _adam_math(w_ref[0], g, m_ref[0], v_ref[0])
            d_ref[0] = d
            m2_ref[0] = m2
            v2_ref[0] = v2

        @pl.when(pl.program_id(0) == 0)
        def _():
            update(p0_ref)

        @pl.when(pl.program_id(0) == 1)
        def _():
            update(p1_ref)

    blk = pl.BlockSpec((1, rb, c), lambda l, i: (l, i, 0))
    return pl.pallas_call(
        body, name=name, grid=(2, nb),
        out_shape=[jax.ShapeDtypeStruct((2, r, c), f32)] * 4,
        in_specs=[pl.BlockSpec((n_src, rb, c), lambda l, i: (0, jnp.where(l == 0, i, nb - 1), 0)),
                  pl.BlockSpec((n_src, rb, c), lambda l, i: (0, jnp.where(l == 1, i, 0), 0)), blk, blk, blk],
        out_specs=[blk] * 4,
        compiler_params=_params(("arbitrary", "arbitrary")),
    )(parts0, parts1, w, m, v)


def _reorder_in(w):
    pad = jnp.zeros(w.shape[:-1] + (P_IN - 2308,), w.dtype)
    return jnp.concatenate([w[..., :2048], w[..., 2052:2308], w[..., 2048:2052], pad], axis=-1)


def _unreorder_in(w):
    return jnp.concatenate([w[..., :2048], w[..., 2304:2308], w[..., 2048:2304]], axis=-1)


def _block_diag(w2d, n_blocks):
    rows, cols = w2d.shape
    tiled = jnp.tile(w2d, (1, n_blocks))
    rb = lax.broadcasted_iota(jnp.int32, tiled.shape, 0) // (rows // n_blocks)
    cb = lax.broadcasted_iota(jnp.int32, tiled.shape, 1) // cols
    return jnp.where(rb == cb, tiled, jnp.zeros_like(tiled))


def _block_diag_extract(w_bd, n_blocks):
    rows, wide = w_bd.shape
    r, c = rows // n_blocks, wide // n_blocks
    w4 = w_bd.reshape(n_blocks, r, n_blocks, c)
    idx = jnp.arange(n_blocks)
    return w4[idx, :, idx, :]


def _lanes128(v):
    return jnp.pad(v.reshape(1, -1), ((0, 0), (0, LANES - v.size)))


def _rows_of(shape):
    n = 1
    for d in shape:
        n *= d
    return -(-n // (8 * LANES)) * 8, n


def _flat_pack(arrs, row_multiple=8):
    blocks = []
    for a in arrs:
        rows, n = _rows_of(a.shape)
        blocks.append(jnp.pad(a.reshape(-1), (0, rows * LANES - n)).reshape(rows, LANES))
    total = sum(b.shape[0] for b in blocks)
    pad = -total % row_multiple
    if pad:
        blocks.append(jnp.zeros((pad, LANES), blocks[0].dtype))
    return jnp.concatenate(blocks, axis=0)


def _flat_unpack(packed, shapes):
    out, off = [], 0
    for s in shapes:
        rows, n = _rows_of(s)
        out.append(packed[off:off + rows].reshape(-1)[:n].reshape(s))
        off += rows
    return out


_W_NAMES = ['norm_mix_w', 'norm_mlp_w', 'ada_w', 'ada_b', 'w_in', 'pool_w', 'pool_scale', 'sconv_w', 'ssd_conv_w',
            'ssd_conv_b', 'ssd_dt_bias', 'ssd_a_log', 'ssd_d', 's5_a_re', 's5_a_im', 's5_log_step', 's5_b_re', 's5_b_im',
            's5_c_re', 's5_c_im', 's5_d', 's5_glu_w', 's5_glu_b', 'branch_norm_w', 'w_out', 'mlp_w1', 'mlp_w2',
            'final_norm_w']
_BIG = ('ada_w', 'w_in', 'w_out', 'mlp_w1', 'mlp_w2')
_SMALL = [n for n in _W_NAMES if n not in _BIG]
_SHARDED_SMALL = {'sconv_w': (2, 32), 'ssd_conv_w': (2, 96), 's5_glu_w': (1, 32)}


def _gather(*blocks):
    return _ChipGather(blocks)


def _scatter(*parts):
    return _Exchange(parts, gather=False)


def _layer_forward(l, h, p, w, sh_b, tb):
    first = l == 0
    (proj, u_b), got = _inproj_fwd(h, p['norm_mix_w'][l], p['sc1'][l], p['sh1'][l], w['w_in', l], tb,
                                   xchg=_gather(sh_b[1][0]) if first else None)
    if first:
        w['w_out', 0] = got[0].reshape(D_MODEL, D_MODEL)
    ya = _pool_fwd(proj, p['pool_bd'][l], p['pool_scale'][l], tb)
    yb = _sconv_fwd(proj, p['sconv_w8'][l], tb)
    (yc, yc_pre, states), got = _ssd_fwd(proj, p['ssd_conv_w8'][l], p['ssd_conv_b'][l], p['ssd_dt_bias'][l], p['ssd_a_log'][l],
                                         p['ssd_d_cols'][l], tb, xchg=_gather(sh_b[2][0]) if first else None)
    if first:
        w['w1', 0] = got[0]
    (yd, xr, xi), got = _s5_fwd(proj, p['bbr'][l], p['bbi'][l], p['ctr'][l], p['cti'][l], p['lre'][l], p['lim'][l],
                                p['s5_d'][l], p['glu_w'][l], p['glu_b'][l], tb, xchg=_gather(sh_b[3][0]) if first else None)
    if first:
        w['w2', 0] = got[0]
    ys = (ya, yb, yc, yd)
    h1, o, groups_b = _outproj_fwd(ys, h, p['branch_norm_w'][l], p['g1'][l], w['w_out', l], tb)
    (h2, m, v_b, r_b), got = _mlp_fwd(h1, p['norm_mlp_w'][l], p['sc2'][l], p['sh2'][l], p['g2'][l], w['w1', l], w['w2', l],
                                      min(MLP_TB, h.shape[0]), xchg=_gather(*[sh_b[k][1] for k in range(4)]) if first else None)
    if first:
        w['w_in', 1] = got[0].reshape(D_MODEL, P_IN)
        w['w_out', 1] = got[1].reshape(D_MODEL, D_MODEL)
        w['w1', 1], w['w2', 1] = got[2], got[3]
    saved = dict(h=h, proj=proj, u_b=u_b, ys=ys, yc_pre=yc_pre, states=states, xr=xr, xi=xi, h1=h1, o=o,
                 groups_b=groups_b, m=m, v_b=v_b, r_b=r_b)
    return h2, saved


def _layer_backward(l, dh2, s, p, w, pending, recv, tb):
    def carry(names):
        names = [n for n in names if n in pending]
        return names, (_scatter(*[pending.pop(n) for n in names]) if names else None)

    def landed(names, got):
        for n, g in zip(names, got):
            recv[n] = g

    names, xchg = carry([('w_out', 1)])
    (dh1, do2_b, da_b, dg2, dsh2, dsc2, dnw_mlp), got = _mlp_bwd(dh2, s['m'], s['h1'], s['r_b'], p['norm_mlp_w'][l], p['sc2'][l],
                                                                p['sh2'][l], p['g2'][l], w['w1', l], w['w2', l], min(TB_BWD, tb),
                                                                xchg=xchg)
    landed(names, got)
    pending['mlp_w2', l] = _wgrad(s['r_b'], do2_b, 1, "wgrad_w2", tm=1024, tk=1024).reshape(N_DEV, MLP_HB, D_MODEL)
    pending['mlp_w1', l] = _wgrad(s['v_b'], da_b, N_DEV, "wgrad_w1", tm=1024, tk=2048)
    dya, dyb, dyc, dyd, do1_b, dg1, dbn = _outproj_bwd(dh1, s['o'], s['ys'], p['branch_norm_w'][l], p['g1'][l], w['w_out', l], tb)
    pending['w_out', l] = _wgrad(s['groups_b'], do1_b, 1, "wgrad_wout", tm=1024, tk=1024).reshape(N_DEV, D_MODEL // N_DEV, D_MODEL)
    proj = s['proj']
    dv, dpool_bd, dpool_scale = _pool_bwd(proj, dya, p['pool_bd'][l], p['pool_scale'][l], tb)
    dgb, dgc, dhh, dsconv = _sconv_bwd(proj, dyb, p['sconv_w8'][l], tb)
    names, xchg = carry([('mlp_w1', l)] + ([('w_out', 0)] if l == 0 else []))
    (dz, dxs, dbm, dcm, ddt, dconv_w, dconv_b, ddtb, dalog, ddskip), got = _ssd_bwd(
        proj, dyc, s['yc_pre'], s['states'], p['ssd_conv_w8'][l], p['ssd_conv_b'][l], p['ssd_dt_bias'][l], p['ssd_a_log'][l],
        p['ssd_d_cols'][l], tb, xchg=xchg)
    landed(names, got)
    names, xchg = carry([('mlp_w2', l)])
    (du5, dlr, dli, dbbr, dbbi, dctr, dcti, dd5, dgw, dgb5), got = _s5_bwd(
        proj, dyd, s['xr'], s['xi'], p['bbr'][l], p['bbi'][l], p['ctr'][l], p['cti'][l], p['lre'][l], p['lim'][l],
        p['s5_d'][l], p['glu_w'][l], p['glu_b'][l], min(TB_BWD, tb), xchg=xchg)
    landed(names, got)
    dare, daim, dls, dbre_bd, dbim_bd = _s5_prep_bwd(p['are_c'][l], p['aim_c'][l], p['ls_c'][l], p['bre_bd'][l], p['bim_bd'][l],
                                                     dlr.reshape(S5_N, 1), dli.reshape(S5_N, 1), dbbr, dbbi)
    dparts = (dv, dgb, dgc, dhh, dz, dxs, dbm, dcm, du5, ddt)
    pending['w_in', l] = _wgrad_parts(s['u_b'], dparts, "wgrad_win", tm=512, tk=1024).reshape(N_DEV, D_MODEL // N_DEV, P_IN)
    names, xchg = carry([('w_in', l)])
    (dh, dsh1, dsc1, dnw_mix), got = _inproj_bwd(dparts, dh1, s['h'], p['norm_mix_w'][l], p['sc1'][l], p['sh1'][l], w['w_in', l],
                                                 min(TB_BWD, tb), xchg=xchg)
    landed(names, got)
    small = {
        'norm_mix_w': dnw_mix.reshape(D_MODEL), 'norm_mlp_w': dnw_mlp.reshape(D_MODEL),
        'ada_b': jnp.concatenate([dsh1, dsc1, dg1, dsh2, dsc2, dg2], axis=1).reshape(6 * D_MODEL),
        'pool_w': _block_diag_extract(dpool_bd, 4), 'pool_scale': dpool_scale.reshape(GROUP_W),
        'sconv_w': dsconv[0:3], 'ssd_conv_w': dconv_w[0:4], 'ssd_conv_b': dconv_b.reshape(768),
        'ssd_dt_bias': ddtb[0, 0:4], 'ssd_a_log': dalog[0, 0:4], 'ssd_d': ddskip[0, 0:4],
        's5_a_re': dare.reshape(16, 64), 's5_a_im': daim.reshape(16, 64), 's5_log_step': dls[0:16, 0],
        's5_b_re': _block_diag_extract(dbre_bd, 16), 's5_b_im': _block_diag_extract(dbim_bd, 16),
        's5_c_re': _block_diag_extract(dctr, 16), 's5_c_im': _block_diag_extract(dcti, 16),
        's5_d': dd5.reshape(GROUP_W), 's5_glu_w': dgw, 's5_glu_b': dgb5.reshape(GROUP_W),
        'branch_norm_w': dbn.reshape(D_MODEL),
    }
    return dh, small


def _prepare_params(a, me):
    pack_shapes = [(1, D_MODEL), (2, 3, 32), (2, 4, 96), (2, 32, GROUP_W)]
    packed = _flat_pack([a['c'], a['sconv_w'], a['ssd_conv_w'], a['s5_glu_w']])
    gathered = _gather_rows(packed, "gather_small")
    pieces = [_flat_unpack(gathered[d], pack_shapes) for d in range(N_DEV)]
    c_all = jnp.concatenate([pc[0] for pc in pieces], axis=0)
    sconv_full = jnp.concatenate([pc[1] for pc in pieces], axis=2)
    ssd_conv_full = jnp.concatenate([pc[2] for pc in pieces], axis=2)
    glu_full = jnp.concatenate([pc[3] for pc in pieces], axis=1)

    ada_b_cols = lax.dynamic_slice_in_dim(a['ada_b'], me * 768, 768, axis=1).reshape(2, 1, 768)
    cond, modrows = _ada_forward(c_all, a['ada_w'], ada_b_cols)
    mod_recv = _all_to_all_rows(modrows.transpose(1, 0, 2), "exchange_mod")
    mod = mod_recv.transpose(1, 0, 2).reshape(2, 6 * D_MODEL)
    p = {'cond': cond}
    for k, name in enumerate(('sh1', 'sc1', 'g1', 'sh2', 'sc2', 'g2')):
        p[name] = mod[:, k * D_MODEL:(k + 1) * D_MODEL].reshape(2, 1, D_MODEL)

    for name in ('norm_mix_w', 'norm_mlp_w', 'branch_norm_w'):
        p[name] = a[name].reshape(2, 1, D_MODEL)
    p['pool_bd'] = jnp.stack([_block_diag(a['pool_w'][l].reshape(GROUP_W, 64), 4) for l in range(2)])
    p['pool_scale'] = a['pool_scale'].reshape(2, 1, GROUP_W)
    p['sconv_w8'] = jnp.pad(sconv_full, ((0, 0), (0, 5), (0, 0)))
    p['ssd_conv_w8'] = jnp.pad(ssd_conv_full, ((0, 0), (0, 4), (0, 0)))
    p['ssd_conv_b'] = a['ssd_conv_b'].reshape(2, 1, 768)
    p['ssd_dt_bias'] = jnp.pad(a['ssd_dt_bias'], ((0, 0), (0, LANES - 4))).reshape(2, 1, LANES)
    p['ssd_a_log'] = jnp.pad(a['ssd_a_log'], ((0, 0), (0, LANES - 4))).reshape(2, 1, LANES)
    p['ssd_d_cols'] = jnp.repeat(a['ssd_d'], SSD_P, axis=1).reshape(2, 1, GROUP_W)
    p['are_c'] = a['s5_a_re'].reshape(2, S5_N, 1)
    p['aim_c'] = a['s5_a_im'].reshape(2, S5_N, 1)
    p['ls_c'] = jnp.repeat(a['s5_log_step'], 64, axis=1).reshape(2, S5_N, 1)
    p['bre_bd'] = jnp.stack([_block_diag(a['s5_b_re'][l].reshape(S5_N, 16), 16) for l in range(2)])
    p['bim_bd'] = jnp.stack([_block_diag(a['s5_b_im'][l].reshape(S5_N, 16), 16) for l in range(2)])
    p['ctr'] = jnp.stack([_block_diag(a['s5_c_re'][l].reshape(GROUP_W, 64), 16) for l in range(2)])
    p['cti'] = jnp.stack([_block_diag(a['s5_c_im'][l].reshape(GROUP_W, 64), 16) for l in range(2)])
    p['s5_d'] = a['s5_d'].reshape(2, 1, GROUP_W)
    p['glu_w'] = glu_full
    p['glu_b'] = a['s5_glu_b'].reshape(2, 1, GROUP_W)
    lre, lim, bbr, bbi = [], [], [], []
    for l in range(2):
        r = _s5_prep(p['are_c'][l], p['aim_c'][l], p['ls_c'][l], p['bre_bd'][l], p['bim_bd'][l])
        lre.append(r[0].reshape(1, S5_N))
        lim.append(r[1].reshape(1, S5_N))
        bbr.append(r[2])
        bbi.append(r[3])
    p['lre'], p['lim'], p['bbr'], p['bbi'] = lre, lim, bbr, bbi
    return p


def kernel(x, c, norm_mix_w, norm_mlp_w, ada_w, ada_b, w_in, pool_w, pool_scale, sconv_w, ssd_conv_w, ssd_conv_b, ssd_dt_bias, ssd_a_log, ssd_d, s5_a_re, s5_a_im, s5_log_step, s5_b_re, s5_b_im, s5_c_re, s5_c_im, s5_d, s5_glu_w, s5_glu_b, branch_norm_w, w_out, mlp_w1, mlp_w2, final_norm_w, loss_target, m_norm_mix_w, m_norm_mlp_w, m_ada_w, m_ada_b, m_w_in, m_pool_w, m_pool_scale, m_sconv_w, m_ssd_conv_w, m_ssd_conv_b, m_ssd_dt_bias, m_ssd_a_log, m_ssd_d, m_s5_a_re, m_s5_a_im, m_s5_log_step, m_s5_b_re, m_s5_b_im, m_s5_c_re, m_s5_c_im, m_s5_d, m_s5_glu_w, m_s5_glu_b, m_branch_norm_w, m_w_out, m_mlp_w1, m_mlp_w2, m_final_norm_w, v_norm_mix_w, v_norm_mlp_w, v_ada_w, v_ada_b, v_w_in, v_pool_w, v_pool_scale, v_sconv_w, v_ssd_conv_w, v_ssd_conv_b, v_ssd_dt_bias, v_ssd_a_log, v_ssd_d, v_s5_a_re, v_s5_a_im, v_s5_log_step, v_s5_b_re, v_s5_b_im, v_s5_c_re, v_s5_c_im, v_s5_d, v_s5_glu_w, v_s5_glu_b, v_branch_norm_w, v_w_out, v_mlp_w1, v_mlp_w2, v_final_norm_w):
    a = dict(locals())
    t = x.shape[1]
    tb = min(TB, t)
    me = _my_index()
    p = _prepare_params(a, me)

    sh_b = _cast_shards([_reorder_in(w_in), w_out, mlp_w1, mlp_w2])
    w = {('w_in', 0): _exchange_alone(_gather(sh_b[0][0]), "gather_w_in0")[0].reshape(D_MODEL, P_IN)}

    h = x.reshape(t, D_MODEL)
    saved = []
    for l in range(2):
        h, s = _layer_forward(l, h, p, w, sh_b, tb)
        saved.append(s)
    loss_blk, dh, dfinal = _loss_head(h, loss_target.reshape(t, D_MODEL), final_norm_w.reshape(1, D_MODEL), tb)
    loss = lax.psum(loss_blk[0, 0], ("x", "y", "c"))

    pending, recv, small_parts = {}, {}, [None, None]
    for l in (1, 0):
        dh, small_parts[l] = _layer_backward(l, dh, saved[l], p, w, pending, recv, tb)
    grad_x = dh.reshape(1, t, D_MODEL)

    grads, deltas, new_m, new_v = {}, {}, {}, {}

    wmv_in = [_reorder_in(a[n]) for n in ('w_in', 'm_w_in', 'v_w_in')]
    outs = _sum_adamw_layers(recv['w_in', 0], recv['w_in', 1], *wmv_in, "adamw_w_in", 128)
    grads['w_in'], deltas['w_in'], new_m['w_in'], new_v['w_in'] = [_unreorder_in(o) for o in outs]
    for name, rb in (('w_out', 128), ('mlp_w1', 256), ('mlp_w2', 256)):
        grads[name], deltas[name], new_m[name], new_v[name] = _sum_adamw_layers(
            recv[name, 0], recv[name, 1], a[name], a['m_' + name], a['v_' + name], "adamw_" + name, rb)

    dmod = jnp.stack([small_parts[0]['ada_b'], small_parts[1]['ada_b']])
    dmod_recv = _all_to_all_rows(dmod.reshape(2, N_DEV, 768).transpose(1, 0, 2), "exchange_dmod")
    g_ada = _ada_backward(p['cond'], dmod_recv.transpose(1, 0, 2))
    grads['ada_w'], deltas['ada_w'], new_m['ada_w'], new_v['ada_w'] = _sum_adamw_layers(
        g_ada[0:1], g_ada[1:2], ada_w, m_ada_w, v_ada_w, "adamw_ada_w", 256)

    layered = [n for n in _SMALL if n != 'final_norm_w']
    full = [jnp.stack([small_parts[0][n], small_parts[1][n]]) for n in layered] + [dfinal.reshape(D_MODEL)]
    full_shapes = [f.shape for f in full]
    summed = _flat_unpack(_allreduce_rows(_flat_pack(full, row_multiple=64)), full_shapes)
    local = []
    for n, g in zip(_SMALL, summed):
        if n in _SHARDED_SMALL:
            axis, size = _SHARDED_SMALL[n]
            g = lax.dynamic_slice_in_dim(g, me * size, size, axis=axis)
        local.append(g.reshape(a[n].shape))
    local_shapes = [g.shape for g in local]
    packed = [_flat_pack(xs) for xs in (local, [a[n] for n in _SMALL], [a['m_' + n] for n in _SMALL], [a['v_' + n] for n in _SMALL])]
    outs = _sum_adamw(packed[0][None], packed[1], packed[2], packed[3], "adamw_small", packed[0].shape[0])
    for store, o in zip((grads, deltas, new_m, new_v), outs):
        for n, val in zip(_SMALL, _flat_unpack(o, local_shapes)):
            store[n] = val

    return (loss, grad_x, *[grads[n] for n in _W_NAMES], *[deltas[n] for n in _W_NAMES],
            *[new_m[n] for n in _W_NAMES], *[new_v[n] for n in _W_NAMES])
```

```python
import functools

import jax
import jax.numpy as jnp
from jax import lax
from jax.experimental import pallas as pl
from jax.experimental.pallas import tpu as pltpu

f32 = jnp.float32
bf16 = jnp.bfloat16

N_DEV = 8
D_MODEL = 1024
GROUP_W = 256
P_IN = 2432
DT_COL = 2304
SSD_CHUNK = 128
SSD_HEADS = 4
SSD_P = 64
S5_N = 1024
MLP_HB = 512
TB = 1024
TB_BWD = 512
MLP_TB = 1024
MLP_SLABS = 2
MLP_BWD_SLABS = 4
EPS = 1e-6
LANES = 128
VMEM_LIMIT = 56 * 1024 * 1024
ADAM_LR, ADAM_B1, ADAM_B2, ADAM_EPS, ADAM_WD, ADAM_STEP = 0.001, 0.9, 0.999, 1e-08, 0.01, 10
POOL_WINDOWS = (2, 4, 8, 16)

C_POOL, C_GB, C_GC, C_HH, C_Z, C_XS, C_BM, C_CM, C_S5 = range(9)
C_DT128 = DT_COL // LANES

MESH = pl.DeviceIdType.MESH
ANY = pl.BlockSpec(memory_space=pl.ANY)
VMEM = pl.BlockSpec(memory_space=pltpu.VMEM)


def _dot(a, b):
    return jnp.dot(a, b, preferred_element_type=f32)


def _dot_nt(a, b):
    return lax.dot_general(a, b, (((1,), (1,)), ((), ())), preferred_element_type=f32)


def _dot_tn(a, b):
    return lax.dot_general(a, b, (((0,), (0,)), ((), ())), preferred_element_type=f32)


def _dot_exact(a, b):
    return jnp.dot(a, b, preferred_element_type=f32, precision=lax.Precision.HIGHEST)


def _b(x):
    return x.astype(bf16)


def _silu(x):
    return x * jax.nn.sigmoid(x)


def _dsilu(x):
    s = jax.nn.sigmoid(x)
    return s * (1.0 + x * (1.0 - s))


def _softplus(x):
    return jnp.maximum(x, 0.0) + jnp.log1p(jnp.exp(-jnp.abs(x)))


_GELU_K = 0.7978845608028654
_GELU_C = 0.044715


def _gelu(x):
    return 0.5 * x * (1.0 + jnp.tanh(_GELU_K * (x + _GELU_C * x * x * x)))


def _dgelu(x):
    th = jnp.tanh(_GELU_K * (x + _GELU_C * x * x * x))
    return 0.5 * (1.0 + th) + 0.5 * x * (1.0 - th * th) * _GELU_K * (1.0 + 3.0 * _GELU_C * x * x)


def _rms(h):
    r = lax.rsqrt(jnp.mean(h * h, axis=-1, keepdims=True) + EPS)
    return h * r, r


def _rms_bwd(dn, n, r):
    return r * (dn - n * jnp.mean(dn * n, axis=-1, keepdims=True))


def _colsum(x):
    return jnp.sum(x, axis=0, keepdims=True)


def _params(sem=None):
    return pltpu.CompilerParams(dimension_semantics=sem, vmem_limit_bytes=VMEM_LIMIT)


def _full(shape):
    return pl.BlockSpec(shape, lambda *_: (0,) * len(shape))


def _acc(ref, val):
    @pl.when(pl.program_id(0) == 0)
    def _():
        ref[...] = val

    @pl.when(pl.program_id(0) != 0)
    def _():
        ref[...] += val


def _me():
    return lax.axis_index("x"), lax.axis_index("y"), lax.axis_index("c")


def _my_index():
    x, y, c = _me()
    return 4 * x + 2 * y + c


def _coords(p):
    return (p // 4, (p // 2) % 2, p % 2)


class _Exchange:
    def __init__(self, srcs, gather):
        self.srcs = list(srcs)
        self.gather = gather
        self.n = len(self.srcs)
        self.out_shape = [jax.ShapeDtypeStruct(((N_DEV,) + s.shape) if gather else s.shape, s.dtype) for s in self.srcs]
        self.scratch = [pltpu.SemaphoreType.DMA((self.n, N_DEV)), pltpu.SemaphoreType.DMA((self.n, N_DEV)),
                        pltpu.SemaphoreType.DMA((self.n,))]

    def _src(self, refs, t, dev):
        return refs[t] if self.gather else refs[t].at[dev]

    def _remote(self, xin, xout, sems, t, k, me, to):
        return pltpu.make_async_remote_copy(
            src_ref=self._src(xin, t, to), dst_ref=xout[t].at[me], send_sem=sems[0].at[t, k], recv_sem=sems[1].at[t, k],
            device_id=_coords(to), device_id_type=MESH)

    def start(self, xin, xout, sems):
        me = _my_index()
        for t in range(self.n):
            pltpu.make_async_copy(self._src(xin, t, me), xout[t].at[me], sems[2].at[t]).start()
            for k in range(1, N_DEV):
                self._remote(xin, xout, sems, t, k, me, (me + k) % N_DEV).start()

    def wait(self, xin, xout, sems):
        me = _my_index()
        for t in range(self.n):
            for k in range(1, N_DEV):
                src = (me + N_DEV - k) % N_DEV
                pltpu.make_async_remote_copy(
                    src_ref=self._src(xin, t, src), dst_ref=xout[t].at[src], send_sem=sems[0].at[t, k],
                    recv_sem=sems[1].at[t, k], device_id=_coords(src), device_id_type=MESH).wait_recv()
        for t in range(self.n):
            for k in range(1, N_DEV):
                self._remote(xin, xout, sems, t, k, me, (me + k) % N_DEV).wait_send()
            pltpu.make_async_copy(self._src(xin, t, me), xout[t].at[me], sems[2].at[t]).wait()

    def forward(self, xin, xout, sems):
        pass


class _ChipGather:
    def __init__(self, srcs):
        self.srcs = list(srcs)
        self.n = len(self.srcs)
        self.out_shape = [jax.ShapeDtypeStruct((N_DEV,) + s.shape, s.dtype) for s in self.srcs]
        self.scratch = [pltpu.SemaphoreType.DMA((self.n, 7)), pltpu.SemaphoreType.DMA((self.n, 7)),
                        pltpu.SemaphoreType.DMA((self.n,))]

    @staticmethod
    def _places():
        x, y, c = _me()
        chips = [(1 - x, y), (x, 1 - y), (1 - x, 1 - y)]
        return (x, y, c), (x, y, 1 - c), chips

    @staticmethod
    def _slab(ref, dev):
        return ref.at[4 * dev[0] + 2 * dev[1] + dev[2]]

    def _copy(self, xin, xout, sems, t, k, block, to, src=None):
        return pltpu.make_async_remote_copy(
            src_ref=self._slab(xout[t], block) if src is None else src, dst_ref=self._slab(xout[t], block),
            send_sem=sems[0].at[t, k], recv_sem=sems[1].at[t, k], device_id=to, device_id_type=MESH)

    def start(self, xin, xout, sems):
        me, sibling, chips = self._places()
        for t in range(self.n):
            pltpu.make_async_copy(xin[t], self._slab(xout[t], me), sems[2].at[t]).start()
            self._copy(xin, xout, sems, t, 0, me, sibling, src=xin[t]).start()
            for j, chip in enumerate(chips):
                self._copy(xin, xout, sems, t, 1 + j, me, (*chip, me[2]), src=xin[t]).start()

    def forward(self, xin, xout, sems):
        me, sibling, chips = self._places()
        for t in range(self.n):
            for j, chip in enumerate(chips):
                self._copy(xin, xout, sems, t, 1 + j, (*chip, me[2]), me).wait_recv()
                self._copy(xin, xout, sems, t, 4 + j, (*chip, me[2]), sibling).start()

    def wait(self, xin, xout, sems):
        me, sibling, chips = self._places()
        for t in range(self.n):
            self._copy(xin, xout, sems, t, 0, sibling, me).wait_recv()
            for j, chip in enumerate(chips):
                self._copy(xin, xout, sems, t, 4 + j, (*chip, 1 - me[2]), me).wait_recv()
        for t in range(self.n):
            self._copy(xin, xout, sems, t, 0, me, sibling, src=xin[t]).wait_send()
            for j, chip in enumerate(chips):
                self._copy(xin, xout, sems, t, 1 + j, me, (*chip, me[2]), src=xin[t]).wait_send()
                self._copy(xin, xout, sems, t, 4 + j, (*chip, me[2]), sibling).wait_send()
            pltpu.make_async_copy(xin[t], self._slab(xout[t], me), sems[2].at[t]).wait()


def _call(body, *, name, grid, in_specs, out_specs, out_shape, args, semantics, scratch_shapes=(), xchg=None):
    if xchg is None:
        outs = pl.pallas_call(body, name=name, grid=grid, in_specs=in_specs, out_specs=out_specs, out_shape=out_shape,
                              scratch_shapes=list(scratch_shapes), compiler_params=_params(semantics))(*args)
        return outs, ()
    n_in, n_out, n_scr, n = len(in_specs), len(out_specs), len(scratch_shapes), xchg.n

    def carried(*refs):
        ins, xin = refs[:n_in], refs[n_in:n_in + n]
        outs, xout = refs[n_in + n:n_in + n + n_out], refs[n_in + n + n_out:n_in + 2 * n + n_out]
        scr, sems = refs[n_in + 2 * n + n_out:n_in + 2 * n + n_out + n_scr], refs[n_in + 2 * n + n_out + n_scr:]
        step = pl.program_id(0)
        for d in range(1, len(grid)):
            step = step * grid[d] + pl.program_id(d)
        n_steps = functools.reduce(lambda a, b: a * b, grid)

        @pl.when(step == 0)
        def _():
            xchg.start(xin, xout, sems)

        @pl.when(step == (2 * n_steps) // 3)
        def _():
            xchg.forward(xin, xout, sems)

        body(*ins, *outs, *scr)

        @pl.when(step == n_steps - 1)
        def _():
            xchg.wait(xin, xout, sems)

    res = pl.pallas_call(
        carried, name=name, grid=grid, in_specs=list(in_specs) + [ANY] * n, out_specs=list(out_specs) + [ANY] * n,
        out_shape=list(out_shape) + xchg.out_shape, scratch_shapes=list(scratch_shapes) + xchg.scratch,
        compiler_params=_params(("arbitrary",) * len(grid)))(*args, *xchg.srcs)
    return res[:n_out], tuple(res[n_out:])


def _exchange_alone(xchg, name):
    def body(*refs):
        xin, xout, sems = refs[:xchg.n], refs[xchg.n:2 * xchg.n], refs[2 * xchg.n:]
        xchg.start(xin, xout, sems)
        xchg.forward(xin, xout, sems)
        xchg.wait(xin, xout, sems)

    return pl.pallas_call(body, name=name, out_shape=xchg.out_shape, in_specs=[ANY] * xchg.n, out_specs=[ANY] * xchg.n,
                          scratch_shapes=xchg.scratch)(*xchg.srcs)


def _cast_shards(shards):
    n = len(shards)

    def body(*refs):
        for i, o in zip(refs[:n], refs[n:]):
            o[...] = i[...].astype(bf16)

    return pl.pallas_call(body, name="cast_shards", out_shape=[jax.ShapeDtypeStruct(s.shape, bf16) for s in shards],
                          in_specs=[VMEM] * n, out_specs=[VMEM] * n, compiler_params=_params())(*shards)


def _allreduce_rows(v):
    r = v.shape[0]
    rp = r // N_DEV

    def body(v_ref, o_ref, parts, sums, send1, recv1, send2, recv2):
        me = _my_index()

        def piece(ref, d):
            return ref.at[pl.ds(pl.multiple_of(d * rp, 8), rp), :]

        def copy1(k, src_dev, to):
            return pltpu.make_async_remote_copy(src_ref=piece(v_ref, to), dst_ref=parts.at[src_dev], send_sem=send1.at[k],
                                                recv_sem=recv1.at[k], device_id=_coords(to), device_id_type=MESH)

        def copy2(k, owner, to):
            return pltpu.make_async_remote_copy(src_ref=sums, dst_ref=piece(o_ref, owner), send_sem=send2.at[k],
                                                recv_sem=recv2.at[k], device_id=_coords(to), device_id_type=MESH)

        for k in range(1, N_DEV):
            copy1(k, me, (me + k) % N_DEV).start()
        parts[me] = v_ref[pl.ds(pl.multiple_of(me * rp, 8), rp), :]
        for k in range(1, N_DEV):
            copy1(k, (me + N_DEV - k) % N_DEV, me).wait_recv()
        total = parts[0]
        for s in range(1, N_DEV):
            total = total + parts[s]
        sums[...] = total
        o_ref[pl.ds(pl.multiple_of(me * rp, 8), rp), :] = total
        for k in range(1, N_DEV):
            copy2(k, me, (me + k) % N_DEV).start()
        for k in range(1, N_DEV):
            copy2(k, (me + N_DEV - k) % N_DEV, me).wait_recv()
        for k in range(1, N_DEV):
            copy1(k, me, (me + k) % N_DEV).wait_send()
            copy2(k, me, (me + k) % N_DEV).wait_send()

    return pl.pallas_call(
        body, name="allreduce_small_grads", out_shape=jax.ShapeDtypeStruct(v.shape, v.dtype),
        in_specs=[VMEM], out_specs=VMEM,
        scratch_shapes=[pltpu.VMEM((N_DEV, rp, LANES), f32), pltpu.VMEM((rp, LANES), f32)]
        + [pltpu.SemaphoreType.DMA((N_DEV,))] * 4,
        compiler_params=_params(),
    )(v)


def _gather_rows(v, name):
    def body(v_ref, o_ref, send_sems, recv_sems):
        me = _my_index()
        o_ref[me] = v_ref[...]
        sends = []
        for k in range(1, N_DEV):
            peer = (me + k) % N_DEV
            rc = pltpu.make_async_remote_copy(src_ref=v_ref, dst_ref=o_ref.at[me], send_sem=send_sems.at[k],
                                              recv_sem=recv_sems.at[k], device_id=_coords(peer), device_id_type=MESH)
            rc.start()
            sends.append(rc)
        for k in range(1, N_DEV):
            src = (me + N_DEV - k) % N_DEV
            pltpu.make_async_remote_copy(src_ref=v_ref, dst_ref=o_ref.at[src], send_sem=send_sems.at[k],
                                         recv_sem=recv_sems.at[k], device_id=_coords(src), device_id_type=MESH).wait_recv()
        for rc in sends:
            rc.wait_send()

    return pl.pallas_call(
        body, name=name, out_shape=jax.ShapeDtypeStruct((N_DEV,) + v.shape, v.dtype),
        in_specs=[VMEM], out_specs=VMEM,
        scratch_shapes=[pltpu.SemaphoreType.DMA((N_DEV,)), pltpu.SemaphoreType.DMA((N_DEV,))],
        compiler_params=pltpu.CompilerParams(vmem_limit_bytes=VMEM_LIMIT),
    )(v)


def _all_to_all_rows(v, name):
    def body(v_ref, o_ref, send_sems, recv_sems):
        me = _my_index()
        o_ref[me] = v_ref[me]
        sends = []
        for k in range(1, N_DEV):
            peer = (me + k) % N_DEV
            rc = pltpu.make_async_remote_copy(src_ref=v_ref.at[peer], dst_ref=o_ref.at[me], send_sem=send_sems.at[k],
                                              recv_sem=recv_sems.at[k], device_id=_coords(peer), device_id_type=MESH)
            rc.start()
            sends.append(rc)
        for k in range(1, N_DEV):
            src = (me + N_DEV - k) % N_DEV
            pltpu.make_async_remote_copy(src_ref=v_ref.at[src], dst_ref=o_ref.at[src], send_sem=send_sems.at[k],
                                         recv_sem=recv_sems.at[k], device_id=_coords(src), device_id_type=MESH).wait_recv()
        for rc in sends:
            rc.wait_send()

    return pl.pallas_call(
        body, name=name, out_shape=jax.ShapeDtypeStruct(v.shape, v.dtype),
        in_specs=[VMEM], out_specs=VMEM,
        scratch_shapes=[pltpu.SemaphoreType.DMA((N_DEV,)), pltpu.SemaphoreType.DMA((N_DEV,))],
    )(v)


def _ada_forward(c_all, ada_w, ada_b_cols):
    def body(c_ref, w_ref, b_ref, cond_ref, o_ref):
        cond = _silu(c_ref[...])
        cond_ref[...] = cond
        for l in range(2):
            o_ref[l] = _dot(_b(cond), _b(w_ref[l])) + b_ref[l]

    return pl.pallas_call(
        body, name="ada_forward",
        out_shape=[jax.ShapeDtypeStruct((N_DEV, D_MODEL), f32), jax.ShapeDtypeStruct((2, N_DEV, 768), f32)],
        in_specs=[VMEM] * 3, out_specs=[VMEM] * 2, compiler_params=_params(),
    )(c_all, ada_w, ada_b_cols)


def _ada_backward(cond, dmod_rows):
    def body(c_ref, d_ref, o_ref):
        cb = _b(c_ref[...])
        for l in range(2):
            o_ref[l] = _dot_tn(cb, _b(d_ref[l]))

    return pl.pallas_call(
        body, name="ada_backward", out_shape=jax.ShapeDtypeStruct((2, D_MODEL, 768), f32),
        in_specs=[VMEM] * 2, out_specs=VMEM, compiler_params=_params(),
    )(cond, dmod_rows)


def _inproj_fwd(h, norm_w, sc, sh, w_in, tb, xchg=None):
    t = h.shape[0]

    def body(h_ref, nw_ref, sc_ref, sh_ref, w_ref, proj_ref, u_ref):
        n, _ = _rms(h_ref[...])
        u = _b(n * nw_ref[...] * (1.0 + sc_ref[...]) + sh_ref[...])
        u_ref[...] = u
        proj_ref[...] = _dot(u, w_ref[...])

    row = pl.BlockSpec((tb, D_MODEL), lambda i: (i, 0))
    vec = _full((1, D_MODEL))
    return _call(
        body, name="inproj_fwd", grid=(t // tb,),
        out_shape=[jax.ShapeDtypeStruct((t, P_IN), f32), jax.ShapeDtypeStruct((t, D_MODEL), bf16)],
        in_specs=[row, vec, vec, vec, _full((D_MODEL, P_IN))],
        out_specs=[pl.BlockSpec((tb, P_IN), lambda i: (i, 0)), row],
        semantics=("parallel",), args=(h, norm_w, sc, sh, w_in), xchg=xchg)


def _inproj_bwd(dparts, dh_res, h, norm_w, sc, sh, w_in, tb, xchg=None):
    t = h.shape[0]

    def body(*refs):
        parts = refs[:10]
        dres_ref, h_ref, nw_ref, sc_ref, sh_ref, w_ref = refs[10:16]
        dh_ref, dsh_ref, dsc_ref, dnw_ref = refs[16:]
        dproj = jnp.concatenate([p[...] for p in parts], axis=1)
        du = _dot_nt(dproj, w_ref[...])
        n, r = _rms(h_ref[...])
        nw = nw_ref[...]
        gain = 1.0 + sc_ref[...]
        _acc(dsh_ref, _colsum(du))
        _acc(dsc_ref, _colsum(du * n * nw))
        _acc(dnw_ref, _colsum(du * gain * n))
        dh_ref[...] = dres_ref[...] + _rms_bwd(du * nw * gain, n, r)

    row = pl.BlockSpec((tb, D_MODEL), lambda i: (i, 0))
    vec = _full((1, D_MODEL))
    part_specs = [pl.BlockSpec((tb, GROUP_W), lambda i: (i, 0))] * 9 + [pl.BlockSpec((tb, LANES), lambda i: (i, 0))]
    return _call(
        body, name="inproj_bwd", grid=(t // tb,),
        out_shape=[jax.ShapeDtypeStruct((t, D_MODEL), f32)] + [jax.ShapeDtypeStruct((1, D_MODEL), f32)] * 3,
        in_specs=part_specs + [row, row, vec, vec, vec, _full((D_MODEL, P_IN))],
        out_specs=[row, vec, vec, vec],
        semantics=("arbitrary",), xchg=xchg, args=(*dparts, dh_res, h, norm_w, sc, sh, w_in))


def _wgrad(a, b, n_blocks, name, tm, tk=512):
    t, m = a.shape
    nb = b.shape[1] // n_blocks
    tk = min(tk, t)
    nk = t // tk

    def body(a_ref, b_ref, o_ref, acc_ref):
        k = pl.program_id(2)
        p = _dot_tn(a_ref[...], b_ref[...])

        @pl.when(k == 0)
        def _():
            acc_ref[...] = p

        @pl.when(k != 0)
        def _():
            acc_ref[...] += p

        @pl.when(k == nk - 1)
        def _():
            o_ref[0] = acc_ref[...].astype(o_ref.dtype)

    return pl.pallas_call(
        body, name=name, grid=(m // tm, n_blocks, nk),
        out_shape=jax.ShapeDtypeStruct((n_blocks, m, nb), bf16),
        in_specs=[pl.BlockSpec((tk, tm), lambda i, j, k: (k, i)), pl.BlockSpec((tk, nb), lambda i, j, k: (k, j))],
        out_specs=pl.BlockSpec((1, tm, nb), lambda i, j, k: (j, i, 0)),
        scratch_shapes=[pltpu.VMEM((tm, nb), f32)],
        compiler_params=_params(("parallel", "parallel", "arbitrary")),
    )(a, b)


def _wgrad_parts(a, parts, name, tm, tk):
    t, m = a.shape
    n = sum(p.shape[1] for p in parts)
    n_parts = len(parts)
    tk = min(tk, t)
    nk = t // tk

    def body(*refs):
        a_ref, part_refs, o_ref, acc_ref = refs[0], refs[1:1 + n_parts], refs[1 + n_parts], refs[2 + n_parts]
        k = pl.program_id(1)
        p = _dot_tn(a_ref[...], jnp.concatenate([r[...] for r in part_refs], axis=1))

        @pl.when(k == 0)
        def _():
            acc_ref[...] = p

        @pl.when(k != 0)
        def _():
            acc_ref[...] += p

        @pl.when(k == nk - 1)
        def _():
            o_ref[...] = acc_ref[...].astype(o_ref.dtype)

    return pl.pallas_call(
        body, name=name, grid=(m // tm, nk),
        out_shape=jax.ShapeDtypeStruct((m, n), bf16),
        in_specs=[pl.BlockSpec((tk, tm), lambda i, k: (k, i))]
        + [pl.BlockSpec((tk, p.shape[1]), lambda i, k: (k, 0)) for p in parts],
        out_specs=pl.BlockSpec((tm, n), lambda i, k: (i, 0)),
        scratch_shapes=[pltpu.VMEM((tm, n), f32)],
        compiler_params=_params(("parallel", "arbitrary")),
    )(a, *parts)


def _pool_counts(rows, t0):
    tpos = (lax.broadcasted_iota(jnp.int32, (rows, GROUP_W), 0) + t0 + 1).astype(f32)
    grp = lax.broadcasted_iota(jnp.int32, (rows, GROUP_W), 1) // 64
    win = jnp.where(grp == 0, 2.0, jnp.where(grp == 1, 4.0, jnp.where(grp == 2, 8.0, 16.0)))
    return jnp.minimum(tpos, win), grp


def _pool_select(grp, l1, l2, l3, l4):
    return jnp.where(grp == 0, l1, jnp.where(grp == 1, l2, jnp.where(grp == 2, l3, l4)))


def _pool_means(v, halo, t0):
    tb = v.shape[0]
    ext = jnp.concatenate([halo, v], axis=0)
    n = tb + 16
    s1 = ext[1:n] + ext[0:n - 1]
    s2 = s1[2:n - 1] + s1[0:n - 3]
    s3 = s2[4:n - 3] + s2[0:n - 7]
    s4 = s3[8:n - 7] + s3[0:n - 15]
    cnt, grp = _pool_counts(tb, t0)
    wsum = _pool_select(grp, s1[15:15 + tb], s2[13:13 + tb], s3[9:9 + tb], s4[1:1 + tb])
    return wsum / cnt - v


def _pool_fwd(proj, pw_bd, scale, tb):
    t = proj.shape[0]

    def body(v_ref, vh_ref, pw_ref, sc_ref, o_ref):
        i = pl.program_id(0)
        halo = jnp.where(i > 0, vh_ref[...], 0.0)
        p = _pool_means(v_ref[...], halo, i * tb)
        o_ref[...] = _dot(_b(p), _b(pw_ref[...])) * sc_ref[...]

    return pl.pallas_call(
        body, name="pool_fwd", grid=(t // tb,),
        out_shape=jax.ShapeDtypeStruct((t, GROUP_W), f32),
        in_specs=[pl.BlockSpec((tb, GROUP_W), lambda i: (i, C_POOL)),
                  pl.BlockSpec((16, GROUP_W), lambda i: (jnp.maximum(i * (tb // 16) - 1, 0), C_POOL)),
                  _full((GROUP_W, GROUP_W)), _full((1, GROUP_W))],
        out_specs=pl.BlockSpec((tb, GROUP_W), lambda i: (i, 0)),
        compiler_params=_params(("parallel",)),
    )(proj, proj, pw_bd, scale)


def _pool_bwd(proj, dy, pw_bd, scale, tb):
    t = proj.shape[0]
    nt = t // tb
    last16 = t // 16 - 1

    def body(v_ref, vh_ref, dy_ref, dyh_ref, pw_ref, sc_ref, dv_ref, dpw_ref, dsc_ref):
        i = pl.program_id(0)
        halo = jnp.where(i > 0, vh_ref[...], 0.0)
        p = _pool_means(v_ref[...], halo, i * tb)
        pw = _b(pw_ref[...])
        sc = sc_ref[...]
        dy = dy_ref[...]
        ypre = _dot(_b(p), pw)
        _acc(dsc_ref, _colsum(dy * ypre))
        dys = _b(dy * sc)
        _acc(dpw_ref, _dot_tn(_b(p), dys))
        dp = _dot_nt(dys, pw)
        dph = _dot_nt(_b(jnp.where(i < nt - 1, dyh_ref[...], 0.0) * sc), pw)
        cnt, grp = _pool_counts(tb, i * tb)
        cnth, _ = _pool_counts(16, (i + 1) * tb)
        ext = jnp.concatenate([dp / cnt, dph / cnth], axis=0)
        n = tb + 16
        f1 = ext[0:n - 1] + ext[1:n]
        f2 = f1[0:n - 3] + f1[2:n - 1]
        f3 = f2[0:n - 7] + f2[4:n - 3]
        f4 = f3[0:n - 15] + f3[8:n - 7]
        dv_ref[...] = _b(_pool_select(grp, f1[0:tb], f2[0:tb], f3[0:tb], f4[0:tb]) - dp)

    return pl.pallas_call(
        body, name="pool_bwd", grid=(nt,),
        out_shape=[jax.ShapeDtypeStruct((t, GROUP_W), bf16), jax.ShapeDtypeStruct((GROUP_W, GROUP_W), f32),
                   jax.ShapeDtypeStruct((1, GROUP_W), f32)],
        in_specs=[pl.BlockSpec((tb, GROUP_W), lambda i: (i, C_POOL)),
                  pl.BlockSpec((16, GROUP_W), lambda i: (jnp.maximum(i * (tb // 16) - 1, 0), C_POOL)),
                  pl.BlockSpec((tb, GROUP_W), lambda i: (i, 0)),
                  pl.BlockSpec((16, GROUP_W), lambda i: (jnp.minimum((i + 1) * (tb // 16), last16), 0)),
                  _full((GROUP_W, GROUP_W)), _full((1, GROUP_W))],
        out_specs=[pl.BlockSpec((tb, GROUP_W), lambda i: (i, 0)), _full((GROUP_W, GROUP_W)), _full((1, GROUP_W))],
        compiler_params=_params(("arbitrary",)),
    )(proj, proj, dy, dy, pw_bd, scale)


def _sconv_fwd(proj, w, tb):
    t = proj.shape[0]

    def body(gb_ref, gc_ref, hh_ref, gch_ref, hhh_ref, w_ref, o_ref):
        i = pl.program_id(0)
        q = gc_ref[...] * hh_ref[...]
        qh = jnp.where(i > 0, gch_ref[...] * hhh_ref[...], 0.0)
        ext = jnp.concatenate([qh, q], axis=0)
        w = w_ref[...]
        conv = w[0:1] * ext[6:6 + tb] + w[1:2] * ext[7:7 + tb] + w[2:3] * ext[8:8 + tb]
        o_ref[...] = gb_ref[...] * conv

    def col(c):
        return pl.BlockSpec((tb, GROUP_W), lambda i: (i, c))

    def prev(c):
        return pl.BlockSpec((8, GROUP_W), lambda i: (jnp.maximum(i * (tb // 8) - 1, 0), c))

    return pl.pallas_call(
        body, name="sconv_fwd", grid=(t // tb,),
        out_shape=jax.ShapeDtypeStruct((t, GROUP_W), f32),
        in_specs=[col(C_GB), col(C_GC), col(C_HH), prev(C_GC), prev(C_HH), _full((8, GROUP_W))],
        out_specs=pl.BlockSpec((tb, GROUP_W), lambda i: (i, 0)),
        compiler_params=_params(("parallel",)),
    )(proj, proj, proj, proj, proj, w)


def _sconv_bwd(proj, dy, w, tb):
    t = proj.shape[0]
    nt = t // tb
    last8 = t // 8 - 1

    def body(gb_ref, gc_ref, hh_ref, gch_ref, hhh_ref, gbn_ref, dy_ref, dyn_ref, w_ref, dgb_ref, dgc_ref, dhh_ref, dw_ref):
        i = pl.program_id(0)
        gc, hh, gb, dy = gc_ref[...], hh_ref[...], gb_ref[...], dy_ref[...]
        q = gc * hh
        qh = jnp.where(i > 0, gch_ref[...] * hhh_ref[...], 0.0)
        ext = jnp.concatenate([qh, q], axis=0)
        w = w_ref[...]
        conv = w[0:1] * ext[6:6 + tb] + w[1:2] * ext[7:7 + tb] + w[2:3] * ext[8:8 + tb]
        dgb_ref[...] = _b(dy * conv)
        e = dy * gb
        en = jnp.where(i < nt - 1, dyn_ref[...] * gbn_ref[...], 0.0)
        exte = jnp.concatenate([e, en], axis=0)
        dq = w[2:3] * exte[0:tb] + w[1:2] * exte[1:1 + tb] + w[0:1] * exte[2:2 + tb]
        dgc_ref[...] = _b(dq * hh)
        dhh_ref[...] = _b(dq * gc)
        dw = jnp.concatenate([_colsum(e * ext[6:6 + tb]), _colsum(e * ext[7:7 + tb]), _colsum(e * ext[8:8 + tb]),
                              jnp.zeros((5, GROUP_W), f32)], axis=0)
        _acc(dw_ref, dw)

    def col(c):
        return pl.BlockSpec((tb, GROUP_W), lambda i: (i, c))

    def prev(c):
        return pl.BlockSpec((8, GROUP_W), lambda i: (jnp.maximum(i * (tb // 8) - 1, 0), c))

    def nxt(c):
        return pl.BlockSpec((8, GROUP_W), lambda i: (jnp.minimum((i + 1) * (tb // 8), last8), c))

    out = pl.BlockSpec((tb, GROUP_W), lambda i: (i, 0))
    return pl.pallas_call(
        body, name="sconv_bwd", grid=(nt,),
        out_shape=[jax.ShapeDtypeStruct((t, GROUP_W), bf16)] * 3 + [jax.ShapeDtypeStruct((8, GROUP_W), f32)],
        in_specs=[col(C_GB), col(C_GC), col(C_HH), prev(C_GC), prev(C_HH), nxt(C_GB), col(0), nxt(0), _full((8, GROUP_W))],
        out_specs=[out, out, out, _full((8, GROUP_W))],
        compiler_params=_params(("arbitrary",)),
    )(proj, proj, proj, proj, proj, proj, dy, dy, w)


def _conv4(xr, halo, w, bias):
    tb = xr.shape[0]
    ext = jnp.concatenate([halo, xr], axis=0)
    pre = w[0:1] * ext[5:5 + tb] + w[1:2] * ext[6:6 + tb] + w[2:3] * ext[7:7 + tb] + w[3:4] * ext[8:8 + tb] + bias
    return pre, ext


def _tri():
    r = lax.broadcasted_iota(jnp.int32, (SSD_CHUNK, SSD_CHUNK), 0)
    c = lax.broadcasted_iota(jnp.int32, (SSD_CHUNK, SSD_CHUNK), 1)
    return r >= c


def _lane_pick(vals):
    rows = vals[0].shape[0]
    lane = lax.broadcasted_iota(jnp.int32, (rows, LANES), 1)
    out = jnp.zeros((rows, LANES), f32)
    for h, v in enumerate(vals):
        out = jnp.where(lane == h, v, out)
    return out


def _ssd_fwd(proj, conv_w, conv_b, dt_bias, a_log, d_cols, tb, xchg=None):
    t = proj.shape[0]
    cpt = tb // SSD_CHUNK

    def body(z_ref, xs_ref, bm_ref, cm_ref, xsh_ref, bmh_ref, cmh_ref, dt_ref, cw_ref, cb_ref, dtb_ref, al_ref, dk_ref,
             o_ref, y_ref, st_ref, state):
        i = pl.program_id(0)

        @pl.when(i == 0)
        def _():
            state[...] = jnp.zeros_like(state)

        cw, cb = cw_ref[...], cb_ref[...]
        acts = []
        for j, (r, hr) in enumerate(((xs_ref, xsh_ref), (bm_ref, bmh_ref), (cm_ref, cmh_ref))):
            halo = jnp.where(i > 0, hr[...], 0.0)
            pre, _ = _conv4(r[...], halo, cw[:, j * 256:(j + 1) * 256], cb[:, j * 256:(j + 1) * 256])
            acts.append(_silu(pre))
        xs, bm, cm = acts
        dt = _softplus(dt_ref[...] + dtb_ref[...])
        a = -jnp.exp(al_ref[...])
        adt = dt * a
        tri = _tri()
        trif = tri.astype(f32)
        dk = dk_ref[...]
        for c in range(cpt):
            rows = slice(c * SSD_CHUNK, (c + 1) * SSD_CHUNK)
            acol = _dot_exact(trif, adt[rows])
            arow = acol.T
            dt_c = dt[rows]
            ys = []
            rowi = lax.broadcasted_iota(jnp.int32, (SSD_CHUNK, 1), 0)
            first = lax.broadcasted_iota(jnp.int32, (SSD_CHUNK, SSD_CHUNK), 1) < SSD_P
            for g in range(SSD_HEADS // 2):
                cols = slice(g * 128, (g + 1) * 128)
                cg, bg = _b(cm[rows, cols]), _b(bm[rows, cols])
                xg = xs[rows, cols]
                heads = (2 * g, 2 * g + 1)
                ac = [acol[:, h:h + 1] for h in heads]
                alast = [v[SSD_CHUNK - 1:SSD_CHUNK] for v in ac]
                dtw = jnp.where(first, dt_c[:, heads[0]:heads[0] + 1], dt_c[:, heads[1]:heads[1] + 1])
                eaw = jnp.where(first, jnp.exp(ac[0]), jnp.exp(ac[1]))
                wdw = jnp.where(first, jnp.exp(alast[0] - ac[0]), jnp.exp(alast[1] - ac[1]))
                xdt = xg * dtw
                xb = _b(xdt)
                gmat = _dot_nt(cg, bg)
                ydiag = []
                for k, h in enumerate(heads):
                    lm = jnp.exp(jnp.where(tri, ac[k] - arow[h:h + 1, :], -jnp.inf))
                    ydiag.append(_dot(_b(gmat * lm), xb[:, k * SSD_P:(k + 1) * SSD_P]))
                s_in = state[g]
                st_ref[c, g] = s_in
                ys.append(jnp.concatenate(ydiag, axis=1) + eaw * _dot_nt(cg, _b(s_in)) + xg * dk[:, cols])
                state[g] = jnp.where(rowi < SSD_P, jnp.exp(alast[0]), jnp.exp(alast[1])) * s_in + _dot_tn(_b(xdt * wdw), bg)
            yc = jnp.concatenate(ys, axis=1)
            y_ref[rows, :] = yc
            o_ref[rows, :] = yc * _silu(z_ref[rows, :])

    def col(c):
        return pl.BlockSpec((tb, GROUP_W), lambda i: (i, c))

    def prev(c):
        return pl.BlockSpec((8, GROUP_W), lambda i: (jnp.maximum(i * (tb // 8) - 1, 0), c))

    out = pl.BlockSpec((tb, GROUP_W), lambda i: (i, 0))
    return _call(
        body, name="ssd_fwd", grid=(t // tb,),
        out_shape=[jax.ShapeDtypeStruct((t, GROUP_W), f32), jax.ShapeDtypeStruct((t, GROUP_W), f32),
                   jax.ShapeDtypeStruct((t // SSD_CHUNK, 2, 128, 128), f32)],
        in_specs=[col(C_Z), col(C_XS), col(C_BM), col(C_CM), prev(C_XS), prev(C_BM), prev(C_CM),
                  pl.BlockSpec((tb, LANES), lambda i: (i, C_DT128)),
                  _full((8, 768)), _full((1, 768)), _full((1, LANES)), _full((1, LANES)), _full((1, GROUP_W))],
        out_specs=[out, out, pl.BlockSpec((cpt, 2, 128, 128), lambda i: (i, 0, 0, 0))],
        scratch_shapes=[pltpu.VMEM((2, 128, 128), f32)],
        semantics=("arbitrary",), xchg=xchg,
        args=(proj, proj, proj, proj, proj, proj, proj, proj, conv_w, conv_b, dt_bias, a_log, d_cols))


def _ssd_bwd(proj, dyc, y_pre, states, conv_w, conv_b, dt_bias, a_log, d_cols, tb, xchg=None):
    t = proj.shape[0]
    nt = t // tb
    cpt = tb // SSD_CHUNK

    def body(z_ref, xs_ref, bm_ref, cm_ref, xsh_ref, bmh_ref, cmh_ref, dt_ref, dy_ref, yp_ref, st_ref,
             cw_ref, cb_ref, dtb_ref, al_ref, dk_ref,
             dz_ref, dxs_ref, dbm_ref, dcm_ref, ddt_ref, dcw_ref, dcb_ref, ddtb_ref, dal_ref, ddk_ref,
             dstate, carry):
        i = pl.program_id(0)
        ti = nt - 1 - i

        @pl.when(i == 0)
        def _():
            dstate[...] = jnp.zeros_like(dstate)
            carry[...] = jnp.zeros_like(carry)

        cw, cb = cw_ref[...], cb_ref[...]
        pres, exts, acts = [], [], []
        for j, (r, hr) in enumerate(((xs_ref, xsh_ref), (bm_ref, bmh_ref), (cm_ref, cmh_ref))):
            halo = jnp.where(ti > 0, hr[...], 0.0)
            pre, ext = _conv4(r[...], halo, cw[:, j * 256:(j + 1) * 256], cb[:, j * 256:(j + 1) * 256])
            pres.append(pre)
            exts.append(ext)
            acts.append(_silu(pre))
        xs, bm, cm = acts
        raw = dt_ref[...] + dtb_ref[...]
        dt = _softplus(raw)
        a = -jnp.exp(al_ref[...])
        adt = dt * a
        tri = _tri()
        trif = tri.astype(f32)
        dk = dk_ref[...]
        z = z_ref[...]
        dyc = dy_ref[...]
        dz_ref[...] = _b(dyc * yp_ref[...] * _dsilu(z))
        dy_all = dyc * _silu(z)
        lane = lax.broadcasted_iota(jnp.int32, (1, LANES), 1)
        ddk_acc = jnp.zeros((1, LANES), f32)
        dal_acc = jnp.zeros((1, LANES), f32)
        dxs_c, dbm_c, dcm_c, ddt_c = [None] * cpt, [None] * cpt, [None] * cpt, [None] * cpt
        for c in reversed(range(cpt)):
            rows = slice(c * SSD_CHUNK, (c + 1) * SSD_CHUNK)
            acol = _dot_exact(trif, adt[rows])
            arow = acol.T
            dt_c = dt[rows]
            da_cols, da_rows, ddt_heads, dxs_groups, dbg, dcg = [], [], [], [], [], []
            rowi = lax.broadcasted_iota(jnp.int32, (SSD_CHUNK, 1), 0)
            first = lax.broadcasted_iota(jnp.int32, (SSD_CHUNK, SSD_CHUNK), 1) < SSD_P
            for g in range(SSD_HEADS // 2):
                cols = slice(g * 128, (g + 1) * 128)
                cgf, bgf = cm[rows, cols], bm[rows, cols]
                cg, bg = _b(cgf), _b(bgf)
                xg, dyg = xs[rows, cols], dy_all[rows, cols]
                s_in, dsn = st_ref[c, g], dstate[g]
                sb, dsnb = _b(s_in), _b(dsn)
                heads = (2 * g, 2 * g + 1)
                ac = [acol[:, h:h + 1] for h in heads]
                alast = [v[SSD_CHUNK - 1:SSD_CHUNK] for v in ac]
                el = [jnp.exp(v) for v in alast]
                dtw = jnp.where(first, dt_c[:, heads[0]:heads[0] + 1], dt_c[:, heads[1]:heads[1] + 1])
                eaw = jnp.where(first, jnp.exp(ac[0]), jnp.exp(ac[1]))
                wdw = jnp.where(first, jnp.exp(alast[0] - ac[0]), jnp.exp(alast[1] - ac[1]))
                xdt = xg * dtw
                xb, dyb = _b(xdt), _b(dyg)
                gmat = _dot_nt(cg, bg)
                dgs, dxh, da = None, [], []
                for k, h in enumerate(heads):
                    hc = slice(k * SSD_P, (k + 1) * SSD_P)
                    lm = jnp.exp(jnp.where(tri, ac[k] - arow[h:h + 1, :], -jnp.inf))
                    m = gmat * lm
                    dm = _dot_nt(dyb[:, hc], xb[:, hc])
                    dxh.append(_dot_tn(_b(m), dyb[:, hc]))
                    dgs = dm * lm if dgs is None else dgs + dm * lm
                    wm = dm * m
                    da.append(jnp.sum(wm, axis=1, keepdims=True))
                    da_rows.append(jnp.sum(wm, axis=0, keepdims=True))
                dgb = _b(dgs)
                dcg_g = _dot(dgb, bg)
                dbg_g = _dot_tn(dgb, cg)
                yoff = eaw * _dot_nt(cg, sb)
                dyoff = dyg * yoff
                dye = _b(dyg * eaw)
                dcg_g = dcg_g + _dot(dye, sb)
                ds_y = _dot_tn(dye, cg)
                u = _dot_nt(bg, dsnb)
                dx = jnp.concatenate(dxh, axis=1) + wdw * u
                dbg_g = dbg_g + _dot(_b(xdt * wdw), dsnb)
                xu = xdt * u * wdw
                ss = jnp.sum(dsn * s_in, axis=1, keepdims=True)
                dxx = dx * xg
                dyx = _colsum(dyg * xg)
                for k, h in enumerate(heads):
                    mine = first if k == 0 else jnp.logical_not(first)
                    dwv = jnp.sum(jnp.where(mine, xu, 0.0), axis=1, keepdims=True)
                    mine_rows = (rowi < SSD_P) if k == 0 else (rowi >= SSD_P)
                    dalast = jnp.sum(dwv, axis=0, keepdims=True) + el[k] * jnp.sum(jnp.where(mine_rows, ss, 0.0), axis=0, keepdims=True)
                    dah = da[k] + jnp.sum(jnp.where(mine, dyoff, 0.0), axis=1, keepdims=True) - dwv
                    da_cols.append(dah + jnp.where(rowi == SSD_CHUNK - 1, dalast, 0.0))
                    ddt_heads.append(jnp.sum(jnp.where(mine, dxx, 0.0), axis=1, keepdims=True))
                    ddk_acc = ddk_acc + jnp.where(lane == h, jnp.sum(jnp.where(mine[0:1], dyx, 0.0), axis=1, keepdims=True), 0.0)
                dstate[g] = jnp.where(rowi < SSD_P, el[0], el[1]) * dsn + ds_y
                dxs_groups.append(dx * dtw + dyg * dk[:, cols])
                dbg.append(dbg_g)
                dcg.append(dcg_g)
            da_blk = _lane_pick(da_cols)
            rowsel = lax.broadcasted_iota(jnp.int32, (SSD_CHUNK, SSD_CHUNK), 0)
            da_rows_blk = jnp.zeros((SSD_CHUNK, SSD_CHUNK), f32)
            for h in range(SSD_HEADS):
                da_rows_blk = jnp.where(rowsel == h, da_rows[h], da_rows_blk)
            da_blk = da_blk - da_rows_blk.T
            dadt = lax.dot_general(trif, da_blk, (((0,), (0,)), ((), ())), preferred_element_type=f32,
                                   precision=lax.Precision.HIGHEST)
            dal_acc = dal_acc + _colsum(dadt * dt_c)
            ddt_c[c] = dadt * a + _lane_pick(ddt_heads)
            dxs_c[c] = jnp.concatenate(dxs_groups, axis=1)
            dbm_c[c] = jnp.concatenate(dbg, axis=1)
            dcm_c[c] = jnp.concatenate(dcg, axis=1)
        ddt = jnp.concatenate(ddt_c, axis=0) if cpt > 1 else ddt_c[0]
        ddraw = jnp.where(lane < SSD_HEADS, ddt * jax.nn.sigmoid(raw), 0.0)
        ddt_ref[...] = _b(ddraw)
        _acc(ddtb_ref, _colsum(ddraw))
        _acc(dal_ref, jnp.where(lane < SSD_HEADS, dal_acc * a, 0.0))
        _acc(ddk_ref, ddk_acc)
        dcw_parts, dcb_parts = [], []
        for j, (dparts, out_ref) in enumerate(((dxs_c, dxs_ref), (dbm_c, dbm_ref), (dcm_c, dcm_ref))):
            dact = jnp.concatenate(dparts, axis=0) if cpt > 1 else dparts[0]
            dpre = dact * _dsilu(pres[j])
            w = cw[:, j * 256:(j + 1) * 256]
            ext = jnp.concatenate([dpre, carry[:, j * 256:(j + 1) * 256]], axis=0)
            out_ref[...] = _b(w[3:4] * ext[0:tb] + w[2:3] * ext[1:1 + tb] + w[1:2] * ext[2:2 + tb] + w[0:1] * ext[3:3 + tb])
            carry[:, j * 256:(j + 1) * 256] = dpre[0:8]
            xe = exts[j]
            dcw_parts.append(jnp.concatenate([_colsum(dpre * xe[5 + k:5 + k + tb]) for k in range(4)]
                                             + [jnp.zeros((4, GROUP_W), f32)], axis=0))
            dcb_parts.append(_colsum(dpre))
        _acc(dcw_ref, jnp.concatenate(dcw_parts, axis=1))
        _acc(dcb_ref, jnp.concatenate(dcb_parts, axis=1))

    def col(c):
        return pl.BlockSpec((tb, GROUP_W), lambda i: (nt - 1 - i, c))

    def prev(c):
        return pl.BlockSpec((8, GROUP_W), lambda i: (jnp.maximum((nt - 1 - i) * (tb // 8) - 1, 0), c))

    out = pl.BlockSpec((tb, GROUP_W), lambda i: (nt - 1 - i, 0))
    vec = _full((1, LANES))
    return _call(
        body, name="ssd_bwd", grid=(nt,),
        out_shape=[jax.ShapeDtypeStruct((t, GROUP_W), bf16)] * 4 + [jax.ShapeDtypeStruct((t, LANES), bf16),
                   jax.ShapeDtypeStruct((8, 768), f32), jax.ShapeDtypeStruct((1, 768), f32)]
        + [jax.ShapeDtypeStruct((1, LANES), f32)] * 3,
        in_specs=[col(C_Z), col(C_XS), col(C_BM), col(C_CM), prev(C_XS), prev(C_BM), prev(C_CM),
                  pl.BlockSpec((tb, LANES), lambda i: (nt - 1 - i, C_DT128)), out, out,
                  pl.BlockSpec((cpt, 2, 128, 128), lambda i: (nt - 1 - i, 0, 0, 0)),
                  _full((8, 768)), _full((1, 768)), vec, vec, _full((1, GROUP_W))],
        out_specs=[out, out, out, out, pl.BlockSpec((tb, LANES), lambda i: (nt - 1 - i, 0)),
                   _full((8, 768)), _full((1, 768)), vec, vec, vec],
        scratch_shapes=[pltpu.VMEM((2, 128, 128), f32), pltpu.VMEM((8, 768), f32)],
        semantics=("arbitrary",), xchg=xchg,
        args=(proj, proj, proj, proj, proj, proj, proj, proj, dyc, y_pre, states, conv_w, conv_b, dt_bias, a_log, d_cols))


def _s5_coeffs(are, aim, ls):
    step = jnp.exp(ls)
    mag = jnp.exp(are * step)
    th = aim * step
    lre, lim = mag * jnp.cos(th), mag * jnp.sin(th)
    den = are * are + aim * aim
    nr = lre - 1.0
    fre = (nr * are + lim * aim) / den
    fim = (lim * are - nr * aim) / den
    return step, lre, lim, den, fre, fim


def _s5_prep(are, aim, ls, bre_bd, bim_bd):
    def body(are_ref, aim_ref, ls_ref, bre_ref, bim_ref, lre_ref, lim_ref, bbr_ref, bbi_ref):
        _, lre, lim, _, fre, fim = _s5_coeffs(are_ref[...], aim_ref[...], ls_ref[...])
        lre_ref[...] = lre
        lim_ref[...] = lim
        bre, bim = bre_ref[...], bim_ref[...]
        bbr_ref[...] = fre * bre - fim * bim
        bbi_ref[...] = fre * bim + fim * bre

    col = jax.ShapeDtypeStruct((S5_N, 1), f32)
    mat = jax.ShapeDtypeStruct((S5_N, GROUP_W), f32)
    return pl.pallas_call(body, name="s5_prep", out_shape=[col, col, mat, mat], in_specs=[VMEM] * 5, out_specs=[VMEM] * 4,
                          compiler_params=_params())(are, aim, ls, bre_bd, bim_bd)


def _s5_prep_bwd(are, aim, ls, bre_bd, bim_bd, dlre, dlim, dbbr, dbbi):
    def body(are_ref, aim_ref, ls_ref, bre_ref, bim_ref, dlre_ref, dlim_ref, dbbr_ref, dbbi_ref,
             dare_ref, daim_ref, dls_ref, dbre_ref, dbim_ref):
        are, aim = are_ref[...], aim_ref[...]
        step, lre, lim, den, fre, fim = _s5_coeffs(are, aim, ls_ref[...])
        r = lax.broadcasted_iota(jnp.int32, (S5_N, GROUP_W), 0) // 64
        c = lax.broadcasted_iota(jnp.int32, (S5_N, GROUP_W), 1) // 16
        mask = r == c
        gr = jnp.where(mask, dbbr_ref[...], 0.0)
        gi = jnp.where(mask, dbbi_ref[...], 0.0)
        bre, bim = bre_ref[...], bim_ref[...]
        dbre_ref[...] = fre * gr + fim * gi
        dbim_ref[...] = fre * gi - fim * gr
        dfre = jnp.sum(bre * gr + bim * gi, axis=1, keepdims=True)
        dfim = jnp.sum(bre * gi - bim * gr, axis=1, keepdims=True)
        ire, iim = are / den, aim / den
        tre = dlre_ref[...] + ire * dfre - iim * dfim
        tim = dlim_ref[...] + ire * dfim + iim * dfre
        dzre = lre * tre + lim * tim
        dzim = lre * tim - lim * tre
        qre = (fre * are + fim * aim) / den
        qim = (fim * are - fre * aim) / den
        dare_ref[...] = step * dzre - (qre * dfre + qim * dfim)
        daim_ref[...] = step * dzim - (qre * dfim - qim * dfre)
        dls = (are * dzre + aim * dzim) * step
        sel = (lax.broadcasted_iota(jnp.int32, (S5_N, LANES), 0) // 64 == lax.broadcasted_iota(jnp.int32, (S5_N, LANES), 1)).astype(f32)
        dls_ref[...] = lax.dot_general(sel, jnp.broadcast_to(dls, (S5_N, LANES)), (((0,), (0,)), ((), ())),
                                       preferred_element_type=f32, precision=lax.Precision.HIGHEST)

    col = jax.ShapeDtypeStruct((S5_N, 1), f32)
    mat = jax.ShapeDtypeStruct((S5_N, GROUP_W), f32)
    return pl.pallas_call(body, name="s5_prep_bwd", out_shape=[col, col, jax.ShapeDtypeStruct((LANES, LANES), f32), mat, mat],
                          in_specs=[VMEM] * 9, out_specs=[VMEM] * 5, compiler_params=_params(),
                          )(are, aim, ls, bre_bd, bim_bd, dlre, dlim, dbbr, dbbi)


def _cmul(ar, ai, br, bi):
    return ar * br - ai * bi, ar * bi + ai * br


def _s5_scan(re_ref, im_ref, carry_ref, mr, mi, n_groups, reverse):
    p1 = (mr, mi)
    p2 = _cmul(*p1, *p1)
    p3 = _cmul(*p2, *p1)
    p4 = _cmul(*p2, *p2)
    p5 = _cmul(*p4, *p1)
    p6 = _cmul(*p4, *p2)
    p7 = _cmul(*p4, *p3)
    p8 = _cmul(*p4, *p4)
    pows = [p1, p2, p3, p4, p5, p6, p7, p8]
    row = lax.broadcasted_iota(jnp.int32, (8, S5_N), 0)
    tr = jnp.zeros((8, S5_N), f32)
    ti = jnp.zeros((8, S5_N), f32)
    for i in range(8):
        p = pows[7 - i] if reverse else pows[i]
        tr = jnp.where(row == i, p[0], tr)
        ti = jnp.where(row == i, p[1], ti)
    steps = []
    for k, p in ((1, p1), (2, p2), (4, p4)):
        keep = (row + k < 8) if reverse else (row >= k)
        steps.append((8 - k if reverse else k, jnp.where(keep, p[0], 0.0), jnp.where(keep, p[1], 0.0)))
    edge = 0 if reverse else 7

    def step(j, carry):
        cr, ci = carry
        g = (n_groups - 1 - j) if reverse else j
        r0 = pl.multiple_of(g * 8, 8)
        xr = re_ref[pl.ds(r0, 8), :]
        xi = im_ref[pl.ds(r0, 8), :]
        for shift, br, bi in steps:
            sr = pltpu.roll(xr, shift, 0)
            si = pltpu.roll(xi, shift, 0)
            xr, xi = xr + br * sr - bi * si, xi + br * si + bi * sr
        xr, xi = xr + tr * cr - ti * ci, xi + tr * ci + ti * cr
        re_ref[pl.ds(r0, 8), :] = xr
        im_ref[pl.ds(r0, 8), :] = xi
        return (jnp.broadcast_to(xr[edge:edge + 1, :], (8, S5_N)), jnp.broadcast_to(xi[edge:edge + 1, :], (8, S5_N)))

    cr, ci = lax.fori_loop(0, n_groups, step, (carry_ref[0], carry_ref[1]))
    carry_ref[0] = cr
    carry_ref[1] = ci


def _s5_output(u, xr, xi, ctr, cti, d):
    return _dot_nt(_b(xr), _b(ctr)) - _dot_nt(_b(xi), _b(cti)) + d * u


def _s5_fwd(proj, bbr, bbi, ctr, cti, lre, lim, d, glu_w, glu_b, tb, xchg=None):
    t = proj.shape[0]

    def body(u_ref, bbr_ref, bbi_ref, ctr_ref, cti_ref, lr_ref, li_ref, d_ref, gw_ref, gb_ref, o_ref, xr_ref, xi_ref, carry):
        @pl.when(pl.program_id(0) == 0)
        def _():
            carry[...] = jnp.zeros_like(carry)

        u = u_ref[...]
        ub = _b(u)
        xr_ref[...] = _dot_nt(ub, _b(bbr_ref[...]))
        xi_ref[...] = _dot_nt(ub, _b(bbi_ref[...]))
        _s5_scan(xr_ref, xi_ref, carry, lr_ref[...], li_ref[...], tb // 8, reverse=False)
        y = _s5_output(u, xr_ref[...], xi_ref[...], ctr_ref[...], cti_ref[...], d_ref[...])
        gl = _gelu(y)
        o_ref[...] = gl * jax.nn.sigmoid(_dot(_b(gl), _b(gw_ref[...])) + gb_ref[...])

    state = pl.BlockSpec((tb, S5_N), lambda i: (i, 0))
    return _call(
        body, name="s5_fwd", grid=(t // tb,),
        out_shape=[jax.ShapeDtypeStruct((t, GROUP_W), f32), jax.ShapeDtypeStruct((t, S5_N), f32), jax.ShapeDtypeStruct((t, S5_N), f32)],
        in_specs=[pl.BlockSpec((tb, GROUP_W), lambda i: (i, C_S5)), _full((S5_N, GROUP_W)), _full((S5_N, GROUP_W)),
                  _full((GROUP_W, S5_N)), _full((GROUP_W, S5_N)), _full((1, S5_N)), _full((1, S5_N)),
                  _full((1, GROUP_W)), _full((GROUP_W, GROUP_W)), _full((1, GROUP_W))],
        out_specs=[pl.BlockSpec((tb, GROUP_W), lambda i: (i, 0)), state, state],
        scratch_shapes=[pltpu.VMEM((2, 8, S5_N), f32)],
        semantics=("arbitrary",), xchg=xchg, args=(proj, bbr, bbi, ctr, cti, lre, lim, d, glu_w, glu_b))


def _s5_bwd(proj, dyd, xr_all, xi_all, bbr, bbi, ctr, cti, lre, lim, d, glu_w, glu_b, tb, xchg=None):
    t = proj.shape[0]
    nt = t // tb

    def body(u_ref, dy_ref, xr_ref, xi_ref, xrh_ref, xih_ref, bbr_ref, bbi_ref, ctr_ref, cti_ref, lr_ref, li_ref,
             d_ref, gw_ref, gb_ref,
             du_ref, dlr_ref, dli_ref, dbbr_ref, dbbi_ref, dctr_ref, dcti_ref, dd_ref, dgw_ref, dgb_ref,
             gr_ref, gi_ref, carry):
        i = pl.program_id(0)
        ti = nt - 1 - i

        @pl.when(i == 0)
        def _():
            carry[...] = jnp.zeros_like(carry)

        u = u_ref[...]
        ub = _b(u)
        xr, xi = xr_ref[...], xi_ref[...]
        ctr, cti = _b(ctr_ref[...]), _b(cti_ref[...])
        d = d_ref[...]
        gw = _b(gw_ref[...])
        y = _s5_output(u, xr, xi, ctr, cti, d)
        gl = _gelu(y)
        sg = jax.nn.sigmoid(_dot(_b(gl), gw) + gb_ref[...])
        dout = dy_ref[...]
        q = dout * gl * sg * (1.0 - sg)
        qb = _b(q)
        dgl = dout * sg + _dot_nt(qb, gw)
        _acc(dgw_ref, _dot_tn(_b(gl), qb))
        _acc(dgb_ref, _colsum(q))
        dyv = dgl * _dgelu(y)
        _acc(dd_ref, _colsum(dyv * u))
        dyb = _b(dyv)
        gr_ref[...] = _dot(dyb, ctr)
        gi_ref[...] = -_dot(dyb, cti)
        _acc(dctr_ref, _dot_tn(dyb, _b(xr)))
        _acc(dcti_ref, -_dot_tn(dyb, _b(xi)))
        _s5_scan(gr_ref, gi_ref, carry, lr_ref[...], -li_ref[...], tb // 8, reverse=True)
        gr, gi = gr_ref[...], gi_ref[...]
        xpr = jnp.concatenate([jnp.where(ti > 0, xrh_ref[...], 0.0), xr], axis=0)[7:7 + tb]
        xpi = jnp.concatenate([jnp.where(ti > 0, xih_ref[...], 0.0), xi], axis=0)[7:7 + tb]
        _acc(dlr_ref, _colsum(gr * xpr + gi * xpi))
        _acc(dli_ref, _colsum(gi * xpr - gr * xpi))
        grb, gib = _b(gr), _b(gi)
        _acc(dbbr_ref, _dot_tn(grb, ub))
        _acc(dbbi_ref, _dot_tn(gib, ub))
        du_ref[...] = _b(dyv * d + _dot(grb, _b(bbr_ref[...])) + _dot(gib, _b(bbi_ref[...])))

    state = pl.BlockSpec((tb, S5_N), lambda i: (nt - 1 - i, 0))
    prev = pl.BlockSpec((8, S5_N), lambda i: (jnp.maximum((nt - 1 - i) * (tb // 8) - 1, 0), 0))
    tile = pl.BlockSpec((tb, GROUP_W), lambda i: (nt - 1 - i, 0))
    return _call(
        body, name="s5_bwd", grid=(nt,),
        out_shape=[jax.ShapeDtypeStruct((t, GROUP_W), bf16), jax.ShapeDtypeStruct((1, S5_N), f32), jax.ShapeDtypeStruct((1, S5_N), f32),
                   jax.ShapeDtypeStruct((S5_N, GROUP_W), f32), jax.ShapeDtypeStruct((S5_N, GROUP_W), f32),
                   jax.ShapeDtypeStruct((GROUP_W, S5_N), f32), jax.ShapeDtypeStruct((GROUP_W, S5_N), f32),
                   jax.ShapeDtypeStruct((1, GROUP_W), f32), jax.ShapeDtypeStruct((GROUP_W, GROUP_W), f32),
                   jax.ShapeDtypeStruct((1, GROUP_W), f32)],
        in_specs=[pl.BlockSpec((tb, GROUP_W), lambda i: (nt - 1 - i, C_S5)), tile, state, state, prev, prev,
                  _full((S5_N, GROUP_W)), _full((S5_N, GROUP_W)), _full((GROUP_W, S5_N)), _full((GROUP_W, S5_N)),
                  _full((1, S5_N)), _full((1, S5_N)), _full((1, GROUP_W)), _full((GROUP_W, GROUP_W)), _full((1, GROUP_W))],
        out_specs=[tile, _full((1, S5_N)), _full((1, S5_N)), _full((S5_N, GROUP_W)), _full((S5_N, GROUP_W)),
                   _full((GROUP_W, S5_N)), _full((GROUP_W, S5_N)), _full((1, GROUP_W)), _full((GROUP_W, GROUP_W)), _full((1, GROUP_W))],
        scratch_shapes=[pltpu.VMEM((tb, S5_N), f32), pltpu.VMEM((tb, S5_N), f32), pltpu.VMEM((2, 8, S5_N), f32)],
        semantics=("arbitrary",), xchg=xchg,
        args=(proj, dyd, xr_all, xi_all, xr_all, xi_all, bbr, bbi, ctr, cti, lre, lim, d, glu_w, glu_b))


def _outproj_fwd(ys, h, bn_w, g1, w_out, tb):
    t = h.shape[0]

    def body(ya_ref, yb_ref, yc_ref, yd_ref, h_ref, bn_ref, g1_ref, w_ref, h1_ref, o_ref, gr_ref):
        bn = bn_ref[...]
        parts = []
        for g, r in enumerate((ya_ref, yb_ref, yc_ref, yd_ref)):
            n, _ = _rms(r[...])
            parts.append(n * bn[:, g * GROUP_W:(g + 1) * GROUP_W])
        groups = _b(jnp.concatenate(parts, axis=1))
        gr_ref[...] = groups
        o = _dot(groups, w_ref[...])
        o_ref[...] = _b(o)
        h1_ref[...] = h_ref[...] + g1_ref[...] * o

    grp = pl.BlockSpec((tb, GROUP_W), lambda i: (i, 0))
    row = pl.BlockSpec((tb, D_MODEL), lambda i: (i, 0))
    vec = _full((1, D_MODEL))
    return pl.pallas_call(
        body, name="outproj_fwd", grid=(t // tb,),
        out_shape=[jax.ShapeDtypeStruct((t, D_MODEL), f32), jax.ShapeDtypeStruct((t, D_MODEL), bf16),
                   jax.ShapeDtypeStruct((t, D_MODEL), bf16)],
        in_specs=[grp, grp, grp, grp, row, vec, vec, _full((D_MODEL, D_MODEL))],
        out_specs=[row, row, row],
        compiler_params=_params(("parallel",)),
    )(*ys, h, bn_w, g1, w_out)


def _outproj_bwd(dh1, o, ys, bn_w, g1, w_out, tb):
    t = dh1.shape[0]

    def body(dh_ref, o_ref, ya_ref, yb_ref, yc_ref, yd_ref, bn_ref, g1_ref, w_ref,
             da_ref, db_ref, dc_ref, dd_ref, do_ref, dg1_ref, dbn_ref):
        dh = dh_ref[...]
        _acc(dg1_ref, _colsum(dh * o_ref[...].astype(f32)))
        do = _b(dh * g1_ref[...])
        do_ref[...] = do
        dgroups = _dot_nt(do, w_ref[...])
        bn = bn_ref[...]
        dbn = []
        for g, (r, dr) in enumerate(((ya_ref, da_ref), (yb_ref, db_ref), (yc_ref, dc_ref), (yd_ref, dd_ref))):
            n, rr = _rms(r[...])
            dgr = dgroups[:, g * GROUP_W:(g + 1) * GROUP_W]
            dbn.append(_colsum(dgr * n))
            dr[...] = _rms_bwd(dgr * bn[:, g * GROUP_W:(g + 1) * GROUP_W], n, rr)
        _acc(dbn_ref, jnp.concatenate(dbn, axis=1))

    grp = pl.BlockSpec((tb, GROUP_W), lambda i: (i, 0))
    row = pl.BlockSpec((tb, D_MODEL), lambda i: (i, 0))
    vec = _full((1, D_MODEL))
    return pl.pallas_call(
        body, name="outproj_bwd", grid=(t // tb,),
        out_shape=[jax.ShapeDtypeStruct((t, GROUP_W), f32)] * 4 + [jax.ShapeDtypeStruct((t, D_MODEL), bf16),
                   jax.ShapeDtypeStruct((1, D_MODEL), f32), jax.ShapeDtypeStruct((1, D_MODEL), f32)],
        in_specs=[row, row, grp, grp, grp, grp, vec, vec, _full((D_MODEL, D_MODEL))],
        out_specs=[grp, grp, grp, grp, row, vec, vec],
        compiler_params=_params(("arbitrary",)),
    )(dh1, o, *ys, bn_w, g1, w_out)


def _mlp_fwd(h1, norm_w, sc, sh, g2, w1, w2, tb, xchg=None):
    t = h1.shape[0]
    nh = w1.shape[0] // MLP_SLABS

    def body(h_ref, nw_ref, sc_ref, sh_ref, g2_ref, w1_ref, w2_ref, h2_ref, m_ref, v_ref, r_ref, acc):
        j = pl.program_id(1)

        @pl.when(j == 0)
        def _():
            n, _ = _rms(h_ref[...])
            v_ref[...] = _b(n * nw_ref[...] * (1.0 + sc_ref[...]) + sh_ref[...])

        v = v_ref[...]
        p = None
        for s in range(MLP_SLABS):
            ra = jnp.maximum(_dot(v, w1_ref[s]), 0.0)
            r = _b(ra * ra)
            r_ref[:, s * MLP_HB:(s + 1) * MLP_HB] = r
            q = _dot(r, w2_ref[s])
            p = q if p is None else p + q

        @pl.when(j == 0)
        def _():
            acc[...] = p

        @pl.when(j != 0)
        def _():
            acc[...] += p

        @pl.when(j == nh - 1)
        def _():
            m = acc[...]
            m_ref[...] = _b(m)
            h2_ref[...] = h_ref[...] + g2_ref[...] * m

    row = pl.BlockSpec((tb, D_MODEL), lambda i, j: (i, 0))
    hid = pl.BlockSpec((tb, MLP_SLABS * MLP_HB), lambda i, j: (i, j))
    vec = _full((1, D_MODEL))
    return _call(
        body, name="mlp_fwd", grid=(t // tb, nh),
        out_shape=[jax.ShapeDtypeStruct((t, D_MODEL), f32), jax.ShapeDtypeStruct((t, D_MODEL), bf16),
                   jax.ShapeDtypeStruct((t, D_MODEL), bf16), jax.ShapeDtypeStruct((t, N_DEV * MLP_HB), bf16)],
        in_specs=[row, vec, vec, vec, vec, pl.BlockSpec((MLP_SLABS, D_MODEL, MLP_HB), lambda i, j: (j, 0, 0)),
                  pl.BlockSpec((MLP_SLABS, MLP_HB, D_MODEL), lambda i, j: (j, 0, 0))],
        out_specs=[row, row, row, hid],
        scratch_shapes=[pltpu.VMEM((tb, D_MODEL), f32)],
        semantics=("arbitrary", "arbitrary"), xchg=xchg, args=(h1, norm_w, sc, sh, g2, w1, w2))


def _mlp_bwd(dh2, m, h1, r, norm_w, sc, sh, g2, w1, w2, tb, xchg=None):
    t = h1.shape[0]
    slabs = MLP_BWD_SLABS
    nh = w1.shape[0] // slabs

    def body(dh_ref, m_ref, h_ref, r_ref, nw_ref, sc_ref, sh_ref, g2_ref, w1_ref, w2_ref,
             dh1_ref, do_ref, da_ref, dg2_ref, dsh_ref, dsc_ref, dnw_ref, acc):
        j = pl.program_id(1)

        @pl.when(j == 0)
        def _():
            dh = dh_ref[...]
            _acc(dg2_ref, _colsum(dh * m_ref[...].astype(f32)))
            do_ref[...] = _b(dh * g2_ref[...])

        do = do_ref[...]
        p = None
        for s in range(slabs):
            cols = slice(s * MLP_HB, (s + 1) * MLP_HB)
            dr = _dot_nt(do, w2_ref[s])
            da = _b(dr * 2.0 * jnp.sqrt(r_ref[:, cols].astype(f32)))
            da_ref[:, cols] = da
            q = _dot_nt(da, w1_ref[s])
            p = q if p is None else p + q

        @pl.when(j == 0)
        def _():
            acc[...] = p

        @pl.when(j != 0)
        def _():
            acc[...] += p

        @pl.when(j == nh - 1)
        def _():
            dv = acc[...]
            n, r = _rms(h_ref[...])
            nw = nw_ref[...]
            gain = 1.0 + sc_ref[...]
            _acc(dsh_ref, _colsum(dv))
            _acc(dsc_ref, _colsum(dv * n * nw))
            _acc(dnw_ref, _colsum(dv * gain * n))
            dh1_ref[...] = dh_ref[...] + _rms_bwd(dv * nw * gain, n, r)

    row = pl.BlockSpec((tb, D_MODEL), lambda i, j: (i, 0))
    hid = pl.BlockSpec((tb, slabs * MLP_HB), lambda i, j: (i, j))
    vec = _full((1, D_MODEL))
    return _call(
        body, name="mlp_bwd", grid=(t // tb, nh),
        out_shape=[jax.ShapeDtypeStruct((t, D_MODEL), f32), jax.ShapeDtypeStruct((t, D_MODEL), bf16),
                   jax.ShapeDtypeStruct((t, N_DEV * MLP_HB), bf16)] + [jax.ShapeDtypeStruct((1, D_MODEL), f32)] * 4,
        in_specs=[row, row, row, hid, vec, vec, vec, vec,
                  pl.BlockSpec((slabs, D_MODEL, MLP_HB), lambda i, j: (j, 0, 0)),
                  pl.BlockSpec((slabs, MLP_HB, D_MODEL), lambda i, j: (j, 0, 0))],
        out_specs=[row, row, hid, vec, vec, vec, vec],
        scratch_shapes=[pltpu.VMEM((tb, D_MODEL), f32)],
        semantics=("arbitrary", "arbitrary"), xchg=xchg, args=(dh2, m, h1, r, norm_w, sc, sh, g2, w1, w2))


def _loss_head(h, target, norm_w, tb):
    t = h.shape[0]

    def body(h_ref, t_ref, w_ref, loss_ref, dh_ref, dw_ref):
        n, r = _rms(h_ref[...])
        w = w_ref[...]
        err = n * w - t_ref[...]
        part = 0.5 * jnp.sum(jnp.sum(err * err, axis=1, keepdims=True), axis=0, keepdims=True) / D_MODEL
        _acc(loss_ref, jnp.broadcast_to(part, (8, LANES)))
        dy = err / D_MODEL
        _acc(dw_ref, _colsum(dy * n))
        dh_ref[...] = _rms_bwd(dy * w, n, r)

    row = pl.BlockSpec((tb, D_MODEL), lambda i: (i, 0))
    return pl.pallas_call(
        body, name="loss_head", grid=(t // tb,),
        out_shape=[jax.ShapeDtypeStruct((8, LANES), f32), jax.ShapeDtypeStruct((t, D_MODEL), f32),
                   jax.ShapeDtypeStruct((1, D_MODEL), f32)],
        in_specs=[row, row, _full((1, D_MODEL))],
        out_specs=[_full((8, LANES)), row, _full((1, D_MODEL))],
        compiler_params=_params(("arbitrary",)),
    )(h, target, norm_w)


def _adam_math(w, g, m, v):
    m2 = ADAM_B1 * m + (1.0 - ADAM_B1) * g
    v2 = ADAM_B2 * v + (1.0 - ADAM_B2) * (g * g)
    mh = m2 / (1.0 - ADAM_B1 ** ADAM_STEP)
    vh = v2 / (1.0 - ADAM_B2 ** ADAM_STEP)
    return -ADAM_LR * (mh / (jnp.sqrt(vh) + ADAM_EPS) + ADAM_WD * w), m2, v2


def _adamw_small(ws, gs, ms, vs):
    n = len(ws)
    shapes = [w.shape for w in ws]
    as2d = [(1,) + s if len(s) == 1 else s for s in shapes]
    flat = [x.reshape(s) for group in (ws, gs, ms, vs) for x, s in zip(group, as2d)]

    def body(*refs):
        w_refs, g_refs, m_refs, v_refs, outs = refs[:n], refs[n:2 * n], refs[2 * n:3 * n], refs[3 * n:4 * n], refs[4 * n:]
        for i in range(n):
            d, m2, v2 = _adam_math(w_refs[i][...], g_refs[i][...], m_refs[i][...], v_refs[i][...])
            outs[3 * i][...] = d
            outs[3 * i + 1][...] = m2
            outs[3 * i + 2][...] = v2

    res = pl.pallas_call(body, name="adamw_small", out_shape=[jax.ShapeDtypeStruct(s, f32) for s in as2d for _ in range(3)],
                         in_specs=[VMEM] * (4 * n), out_specs=[VMEM] * (3 * n), compiler_params=_params())(*flat)
    return [r.reshape(shapes[i // 3]) for i, r in enumerate(res)]


def _sum_adamw_layers(parts0, parts1, w, m, v, name, rb):
    n_src, r, c = parts0.shape
    nb = r // rb

    def body(p0_ref, p1_ref, w_ref, m_ref, v_ref, g_ref, d_ref, m2_ref, v2_ref):
        def update(p_ref):
            g = p_ref[0].astype(f32)
            for s in range(1, n_src):
                g = g + p_ref[s].astype(f32)
            g_ref[0] = g
            d, m2, v2 = _adam_math(w_ref[0], g, m_ref[0], v_ref[0])
            d_ref[0] = d
            m2_ref[0] = m2
            v2_ref[0] = v2

        @pl.when(pl.program_id(0) == 0)
        def _():
            update(p0_ref)

        @pl.when(pl.program_id(0) == 1)
        def _():
            update(p1_ref)

    blk = pl.BlockSpec((1, rb, c), lambda l, i: (l, i, 0))
    return pl.pallas_call(
        body, name=name, grid=(2, nb),
        out_shape=[jax.ShapeDtypeStruct((2, r, c), f32)] * 4,
        in_specs=[pl.BlockSpec((n_src, rb, c), lambda l, i: (0, jnp.where(l == 0, i, nb - 1), 0)),
                  pl.BlockSpec((n_src, rb, c), lambda l, i: (0, jnp.where(l == 1, i, 0), 0)), blk, blk, blk],
        out_specs=[blk] * 4,
        compiler_params=_params(("arbitrary", "arbitrary")),
    )(parts0, parts1, w, m, v)


def _reorder_in(w):
    pad = jnp.zeros(w.shape[:-1] + (P_IN - 2308,), w.dtype)
    return jnp.concatenate([w[..., :2048], w[..., 2052:2308], w[..., 2048:2052], pad], axis=-1)


def _unreorder_in(w):
    return jnp.concatenate([w[..., :2048], w[..., 2304:2308], w[..., 2048:2304]], axis=-1)


def _block_diag(w2d, n_blocks):
    rows, cols = w2d.shape
    tiled = jnp.tile(w2d, (1, n_blocks))
    rb = lax.broadcasted_iota(jnp.int32, tiled.shape, 0) // (rows // n_blocks)
    cb = lax.broadcasted_iota(jnp.int32, tiled.shape, 1) // cols
    return jnp.where(rb == cb, tiled, jnp.zeros_like(tiled))


def _block_diag_extract(w_bd, n_blocks):
    rows, wide = w_bd.shape
    r, c = rows // n_blocks, wide // n_blocks
    w4 = w_bd.reshape(n_blocks, r, n_blocks, c)
    idx = jnp.arange(n_blocks)
    return w4[idx, :, idx, :]


def _lanes128(v):
    return jnp.pad(v.reshape(1, -1), ((0, 0), (0, LANES - v.size)))


def _rows_of(shape):
    n = 1
    for d in shape:
        n *= d
    return -(-n // (8 * LANES)) * 8, n


def _flat_pack(arrs, row_multiple=8):
    blocks = []
    for a in arrs:
        rows, n = _rows_of(a.shape)
        blocks.append(jnp.pad(a.reshape(-1), (0, rows * LANES - n)).reshape(rows, LANES))
    total = sum(b.shape[0] for b in blocks)
    pad = -total % row_multiple
    if pad:
        blocks.append(jnp.zeros((pad, LANES), blocks[0].dtype))
    return jnp.concatenate(blocks, axis=0)


def _flat_unpack(packed, shapes):
    out, off = [], 0
    for s in shapes:
        rows, n = _rows_of(s)
        out.append(packed[off:off + rows].reshape(-1)[:n].reshape(s))
        off += rows
    return out


_W_NAMES = ['norm_mix_w', 'norm_mlp_w', 'ada_w', 'ada_b', 'w_in', 'pool_w', 'pool_scale', 'sconv_w', 'ssd_conv_w',
            'ssd_conv_b', 'ssd_dt_bias', 'ssd_a_log', 'ssd_d', 's5_a_re', 's5_a_im', 's5_log_step', 's5_b_re', 's5_b_im',
            's5_c_re', 's5_c_im', 's5_d', 's5_glu_w', 's5_glu_b', 'branch_norm_w', 'w_out', 'mlp_w1', 'mlp_w2',
            'final_norm_w']
_BIG = ('ada_w', 'w_in', 'w_out', 'mlp_w1', 'mlp_w2')
_SMALL = [n for n in _W_NAMES if n not in _BIG]
_SHARDED_SMALL = {'sconv_w': (2, 32), 'ssd_conv_w': (2, 96), 's5_glu_w': (1, 32)}


def _gather(*blocks):
    return _ChipGather(blocks)


def _scatter(*parts):
    return _Exchange(parts, gather=False)


def _layer_forward(l, h, p, w, sh_b, tb):
    first = l == 0
    (proj, u_b), got = _inproj_fwd(h, p['norm_mix_w'][l], p['sc1'][l], p['sh1'][l], w['w_in', l], tb,
                                   xchg=_gather(sh_b[1][0]) if first else None)
    if first:
        w['w_out', 0] = got[0].reshape(D_MODEL, D_MODEL)
    ya = _pool_fwd(proj, p['pool_bd'][l], p['pool_scale'][l], tb)
    yb = _sconv_fwd(proj, p['sconv_w8'][l], tb)
    (yc, yc_pre, states), got = _ssd_fwd(proj, p['ssd_conv_w8'][l], p['ssd_conv_b'][l], p['ssd_dt_bias'][l], p['ssd_a_log'][l],
                                         p['ssd_d_cols'][l], tb, xchg=_gather(sh_b[2][0]) if first else None)
    if first:
        w['w1', 0] = got[0]
    (yd, xr, xi), got = _s5_fwd(proj, p['bbr'][l], p['bbi'][l], p['ctr'][l], p['cti'][l], p['lre'][l], p['lim'][l],
                                p['s5_d'][l], p['glu_w'][l], p['glu_b'][l], tb, xchg=_gather(sh_b[3][0]) if first else None)
    if first:
        w['w2', 0] = got[0]
    ys = (ya, yb, yc, yd)
    h1, o, groups_b = _outproj_fwd(ys, h, p['branch_norm_w'][l], p['g1'][l], w['w_out', l], tb)
    (h2, m, v_b, r_b), got = _mlp_fwd(h1, p['norm_mlp_w'][l], p['sc2'][l], p['sh2'][l], p['g2'][l], w['w1', l], w['w2', l],
                                      min(MLP_TB, h.shape[0]), xchg=_gather(*[sh_b[k][1] for k in range(4)]) if first else None)
    if first:
        w['w_in', 1] = got[0].reshape(D_MODEL, P_IN)
        w['w_out', 1] = got[1].reshape(D_MODEL, D_MODEL)
        w['w1', 1], w['w2', 1] = got[2], got[3]
    saved = dict(h=h, proj=proj, u_b=u_b, ys=ys, yc_pre=yc_pre, states=states, xr=xr, xi=xi, h1=h1, o=o,
                 groups_b=groups_b, m=m, v_b=v_b, r_b=r_b)
    return h2, saved


def _layer_backward(l, dh2, s, p, w, pending, recv, tb):
    def carry(names):
        names = [n for n in names if n in pending]
        return names, (_scatter(*[pending.pop(n) for n in names]) if names else None)

    def landed(names, got):
        for n, g in zip(names, got):
            recv[n] = g

    names, xchg = carry([('w_out', 1)])
    (dh1, do2_b, da_b, dg2, dsh2, dsc2, dnw_mlp), got = _mlp_bwd(dh2, s['m'], s['h1'], s['r_b'], p['norm_mlp_w'][l], p['sc2'][l],
                                                                p['sh2'][l], p['g2'][l], w['w1', l], w['w2', l], min(TB_BWD, tb),
                                                                xchg=xchg)
    landed(names, got)
    pending['mlp_w2', l] = _wgrad(s['r_b'], do2_b, 1, "wgrad_w2", tm=1024, tk=1024).reshape(N_DEV, MLP_HB, D_MODEL)
    pending['mlp_w1', l] = _wgrad(s['v_b'], da_b, N_DEV, "wgrad_w1", tm=1024, tk=2048)
    dya, dyb, dyc, dyd, do1_b, dg1, dbn = _outproj_bwd(dh1, s['o'], s['ys'], p['branch_norm_w'][l], p['g1'][l], w['w_out', l], tb)
    pending['w_out', l] = _wgrad(s['groups_b'], do1_b, 1, "wgrad_wout", tm=1024, tk=1024).reshape(N_DEV, D_MODEL // N_DEV, D_MODEL)
    proj = s['proj']
    dv, dpool_bd, dpool_scale = _pool_bwd(proj, dya, p['pool_bd'][l], p['pool_scale'][l], tb)
    dgb, dgc, dhh, dsconv = _sconv_bwd(proj, dyb, p['sconv_w8'][l], tb)
    names, xchg = carry([('mlp_w1', l)] + ([('w_out', 0)] if l == 0 else []))
    (dz, dxs, dbm, dcm, ddt, dconv_w, dconv_b, ddtb, dalog, ddskip), got = _ssd_bwd(
        proj, dyc, s['yc_pre'], s['states'], p['ssd_conv_w8'][l], p['ssd_conv_b'][l], p['ssd_dt_bias'][l], p['ssd_a_log'][l],
        p['ssd_d_cols'][l], min(TB_BWD, tb), xchg=xchg)
    landed(names, got)
    names, xchg = carry([('mlp_w2', l)])
    (du5, dlr, dli, dbbr, dbbi, dctr, dcti, dd5, dgw, dgb5), got = _s5_bwd(
        proj, dyd, s['xr'], s['xi'], p['bbr'][l], p['bbi'][l], p['ctr'][l], p['cti'][l], p['lre'][l], p['lim'][l],
        p['s5_d'][l], p['glu_w'][l], p['glu_b'][l], min(TB_BWD, tb), xchg=xchg)
    landed(names, got)
    dare, daim, dls, dbre_bd, dbim_bd = _s5_prep_bwd(p['are_c'][l], p['aim_c'][l], p['ls_c'][l], p['bre_bd'][l], p['bim_bd'][l],
                                                     dlr.reshape(S5_N, 1), dli.reshape(S5_N, 1), dbbr, dbbi)
    dparts = (dv, dgb, dgc, dhh, dz, dxs, dbm, dcm, du5, ddt)
    pending['w_in', l] = _wgrad_parts(s['u_b'], dparts, "wgrad_win", tm=512, tk=1024).reshape(N_DEV, D_MODEL // N_DEV, P_IN)
    names, xchg = carry([('w_in', l)])
    (dh, dsh1, dsc1, dnw_mix), got = _inproj_bwd(dparts, dh1, s['h'], p['norm_mix_w'][l], p['sc1'][l], p['sh1'][l], w['w_in', l],
                                                 min(TB_BWD, tb), xchg=xchg)
    landed(names, got)
    small = {
        'norm_mix_w': dnw_mix.reshape(D_MODEL), 'norm_mlp_w': dnw_mlp.reshape(D_MODEL),
        'ada_b': jnp.concatenate([dsh1, dsc1, dg1, dsh2, dsc2, dg2], axis=1).reshape(6 * D_MODEL),
        'pool_w': _block_diag_extract(dpool_bd, 4), 'pool_scale': dpool_scale.reshape(GROUP_W),
        'sconv_w': dsconv[0:3], 'ssd_conv_w': dconv_w[0:4], 'ssd_conv_b': dconv_b.reshape(768),
        'ssd_dt_bias': ddtb[0, 0:4], 'ssd_a_log': dalog[0, 0:4], 'ssd_d': ddskip[0, 0:4],
        's5_a_re': dare.reshape(16, 64), 's5_a_im': daim.reshape(16, 64), 's5_log_step': dls[0:16, 0],
        's5_b_re': _block_diag_extract(dbre_bd, 16), 's5_b_im': _block_diag_extract(dbim_bd, 16),
        's5_c_re': _block_diag_extract(dctr, 16), 's5_c_im': _block_diag_extract(dcti, 16),
        's5_d': dd5.reshape(GROUP_W), 's5_glu_w': dgw, 's5_glu_b': dgb5.reshape(GROUP_W),
        'branch_norm_w': dbn.reshape(D_MODEL),
    }
    return dh, small


def _prepare_params(a, me):
    pack_shapes = [(1, D_MODEL), (2, 3, 32), (2, 4, 96), (2, 32, GROUP_W)]
    packed = _flat_pack([a['c'], a['sconv_w'], a['ssd_conv_w'], a['s5_glu_w']])
    gathered = _gather_rows(packed, "gather_small")
    pieces = [_flat_unpack(gathered[d], pack_shapes) for d in range(N_DEV)]
    c_all = jnp.concatenate([pc[0] for pc in pieces], axis=0)
    sconv_full = jnp.concatenate([pc[1] for pc in pieces], axis=2)
    ssd_conv_full = jnp.concatenate([pc[2] for pc in pieces], axis=2)
    glu_full = jnp.concatenate([pc[3] for pc in pieces], axis=1)

    ada_b_cols = lax.dynamic_slice_in_dim(a['ada_b'], me * 768, 768, axis=1).reshape(2, 1, 768)
    cond, modrows = _ada_forward(c_all, a['ada_w'], ada_b_cols)
    mod_recv = _all_to_all_rows(modrows.transpose(1, 0, 2), "exchange_mod")
    mod = mod_recv.transpose(1, 0, 2).reshape(2, 6 * D_MODEL)
    p = {'cond': cond}
    for k, name in enumerate(('sh1', 'sc1', 'g1', 'sh2', 'sc2', 'g2')):
        p[name] = mod[:, k * D_MODEL:(k + 1) * D_MODEL].reshape(2, 1, D_MODEL)

    for name in ('norm_mix_w', 'norm_mlp_w', 'branch_norm_w'):
        p[name] = a[name].reshape(2, 1, D_MODEL)
    p['pool_bd'] = jnp.stack([_block_diag(a['pool_w'][l].reshape(GROUP_W, 64), 4) for l in range(2)])
    p['pool_scale'] = a['pool_scale'].reshape(2, 1, GROUP_W)
    p['sconv_w8'] = jnp.pad(sconv_full, ((0, 0), (0, 5), (0, 0)))
    p['ssd_conv_w8'] = jnp.pad(ssd_conv_full, ((0, 0), (0, 4), (0, 0)))
    p['ssd_conv_b'] = a['ssd_conv_b'].reshape(2, 1, 768)
    p['ssd_dt_bias'] = jnp.pad(a['ssd_dt_bias'], ((0, 0), (0, LANES - 4))).reshape(2, 1, LANES)
    p['ssd_a_log'] = jnp.pad(a['ssd_a_log'], ((0, 0), (0, LANES - 4))).reshape(2, 1, LANES)
    p['ssd_d_cols'] = jnp.repeat(a['ssd_d'], SSD_P, axis=1).reshape(2, 1, GROUP_W)
    p['are_c'] = a['s5_a_re'].reshape(2, S5_N, 1)
    p['aim_c'] = a['s5_a_im'].reshape(2, S5_N, 1)
    p['ls_c'] = jnp.repeat(a['s5_log_step'], 64, axis=1).reshape(2, S5_N, 1)
    p['bre_bd'] = jnp.stack([_block_diag(a['s5_b_re'][l].reshape(S5_N, 16), 16) for l in range(2)])
    p['bim_bd'] = jnp.stack([_block_diag(a['s5_b_im'][l].reshape(S5_N, 16), 16) for l in range(2)])
    p['ctr'] = jnp.stack([_block_diag(a['s5_c_re'][l].reshape(GROUP_W, 64), 16) for l in range(2)])
    p['cti'] = jnp.stack([_block_diag(a['s5_c_im'][l].reshape(GROUP_W, 64), 16) for l in range(2)])
    p['s5_d'] = a['s5_d'].reshape(2, 1, GROUP_W)
    p['glu_w'] = glu_full
    p['glu_b'] = a['s5_glu_b'].reshape(2, 1, GROUP_W)
    lre, lim, bbr, bbi = [], [], [], []
    for l in range(2):
        r = _s5_prep(p['are_c'][l], p['aim_c'][l], p['ls_c'][l], p['bre_bd'][l], p['bim_bd'][l])
        lre.append(r[0].reshape(1, S5_N))
        lim.append(r[1].reshape(1, S5_N))
        bbr.append(r[2])
        bbi.append(r[3])
    p['lre'], p['lim'], p['bbr'], p['bbi'] = lre, lim, bbr, bbi
    return p


def kernel(x, c, norm_mix_w, norm_mlp_w, ada_w, ada_b, w_in, pool_w, pool_scale, sconv_w, ssd_conv_w, ssd_conv_b, ssd_dt_bias, ssd_a_log, ssd_d, s5_a_re, s5_a_im, s5_log_step, s5_b_re, s5_b_im, s5_c_re, s5_c_im, s5_d, s5_glu_w, s5_glu_b, branch_norm_w, w_out, mlp_w1, mlp_w2, final_norm_w, loss_target, m_norm_mix_w, m_norm_mlp_w, m_ada_w, m_ada_b, m_w_in, m_pool_w, m_pool_scale, m_sconv_w, m_ssd_conv_w, m_ssd_conv_b, m_ssd_dt_bias, m_ssd_a_log, m_ssd_d, m_s5_a_re, m_s5_a_im, m_s5_log_step, m_s5_b_re, m_s5_b_im, m_s5_c_re, m_s5_c_im, m_s5_d, m_s5_glu_w, m_s5_glu_b, m_branch_norm_w, m_w_out, m_mlp_w1, m_mlp_w2, m_final_norm_w, v_norm_mix_w, v_norm_mlp_w, v_ada_w, v_ada_b, v_w_in, v_pool_w, v_pool_scale, v_sconv_w, v_ssd_conv_w, v_ssd_conv_b, v_ssd_dt_bias, v_ssd_a_log, v_ssd_d, v_s5_a_re, v_s5_a_im, v_s5_log_step, v_s5_b_re, v_s5_b_im, v_s5_c_re, v_s5_c_im, v_s5_d, v_s5_glu_w, v_s5_glu_b, v_branch_norm_w, v_w_out, v_mlp_w1, v_mlp_w2, v_final_norm_w):
    a = dict(locals())
    t = x.shape[1]
    tb = min(TB, t)
    me = _my_index()
    p = _prepare_params(a, me)

    sh_b = _cast_shards([_reorder_in(w_in), w_out, mlp_w1, mlp_w2])
    w = {('w_in', 0): _exchange_alone(_gather(sh_b[0][0]), "gather_w_in0")[0].reshape(D_MODEL, P_IN)}

    h = x.reshape(t, D_MODEL)
    saved = []
    for l in range(2):
        h, s = _layer_forward(l, h, p, w, sh_b, tb)
        saved.append(s)
    loss_blk, dh, dfinal = _loss_head(h, loss_target.reshape(t, D_MODEL), final_norm_w.reshape(1, D_MODEL), tb)
    loss = lax.psum(loss_blk[0, 0], ("x", "y", "c"))

    pending, recv, small_parts = {}, {}, [None, None]
    for l in (1, 0):
        dh, small_parts[l] = _layer_backward(l, dh, saved[l], p, w, pending, recv, tb)
    grad_x = dh.reshape(1, t, D_MODEL)

    grads, deltas, new_m, new_v = {}, {}, {}, {}

    wmv_in = [_reorder_in(a[n]) for n in ('w_in', 'm_w_in', 'v_w_in')]
    outs = _sum_adamw_layers(recv['w_in', 0], recv['w_in', 1], *wmv_in, "adamw_w_in", 128)
    grads['w_in'], deltas['w_in'], new_m['w_in'], new_v['w_in'] = [_unreorder_in(o) for o in outs]
    for name, rb in (('w_out', 128), ('mlp_w1', 256), ('mlp_w2', 256)):
        grads[name], deltas[name], new_m[name], new_v[name] = _sum_adamw_layers(
            recv[name, 0], recv[name, 1], a[name], a['m_' + name], a['v_' + name], "adamw_" + name, rb)

    dmod = jnp.stack([small_parts[0]['ada_b'], small_parts[1]['ada_b']])
    dmod_recv = _all_to_all_rows(dmod.reshape(2, N_DEV, 768).transpose(1, 0, 2), "exchange_dmod")
    g_ada = _ada_backward(p['cond'], dmod_recv.transpose(1, 0, 2))
    grads['ada_w'], deltas['ada_w'], new_m['ada_w'], new_v['ada_w'] = _sum_adamw_layers(
        g_ada[0:1], g_ada[1:2], ada_w, m_ada_w, v_ada_w, "adamw_ada_w", 256)

    layered = [n for n in _SMALL if n != 'final_norm_w']
    full = [jnp.stack([small_parts[0][n], small_parts[1][n]]) for n in layered] + [dfinal.reshape(D_MODEL)]
    full_shapes = [f.shape for f in full]
    summed = _flat_unpack(_allreduce_rows(_flat_pack(full, row_multiple=64)), full_shapes)
    local = []
    for n, g in zip(_SMALL, summed):
        if n in _SHARDED_SMALL:
            axis, size = _SHARDED_SMALL[n]
            g = lax.dynamic_slice_in_dim(g, me * size, size, axis=axis)
        local.append(g.reshape(a[n].shape))
    outs = _adamw_small([a[n] for n in _SMALL], local, [a['m_' + n] for n in _SMALL], [a['v_' + n] for n in _SMALL])
    for i, n in enumerate(_SMALL):
        grads[n], deltas[n], new_m[n], new_v[n] = local[i], outs[3 * i], outs[3 * i + 1], outs[3 * i + 2]

    return (loss, grad_x, *[grads[n] for n in _W_NAMES], *[deltas[n] for n in _W_NAMES],
            *[new_m[n] for n in _W_NAMES], *[new_v[n] for n in _W_NAMES])
```

```python
import functools

import jax
import jax.numpy as jnp
from jax import lax
from jax.experimental import pallas as pl
from jax.experimental.pallas import tpu as pltpu

f32 = jnp.float32
bf16 = jnp.bfloat16

N_DEV = 8
D_MODEL = 1024
GROUP_W = 256
P_IN = 2432
DT_COL = 2304
SSD_CHUNK = 128
SSD_HEADS = 4
SSD_P = 64
S5_N = 1024
MLP_HB = 512
TB = 1024
TB_BWD = 512
MLP_TB = 1024
MLP_SLABS = 2
MLP_BWD_SLABS = 2
EPS = 1e-6
LANES = 128
VMEM_LIMIT = 56 * 1024 * 1024
ADAM_LR, ADAM_B1, ADAM_B2, ADAM_EPS, ADAM_WD, ADAM_STEP = 0.001, 0.9, 0.999, 1e-08, 0.01, 10
POOL_WINDOWS = (2, 4, 8, 16)

C_POOL, C_GB, C_GC, C_HH, C_Z, C_XS, C_BM, C_CM, C_S5 = range(9)
C_DT128 = DT_COL // LANES

MESH = pl.DeviceIdType.MESH
ANY = pl.BlockSpec(memory_space=pl.ANY)
VMEM = pl.BlockSpec(memory_space=pltpu.VMEM)


def _dot(a, b):
    return jnp.dot(a, b, preferred_element_type=f32)


def _dot_nt(a, b):
    return lax.dot_general(a, b, (((1,), (1,)), ((), ())), preferred_element_type=f32)


def _dot_tn(a, b):
    return lax.dot_general(a, b, (((0,), (0,)), ((), ())), preferred_element_type=f32)


def _dot_exact(a, b):
    return jnp.dot(a, b, preferred_element_type=f32, precision=lax.Precision.HIGHEST)


def _b(x):
    return x.astype(bf16)


def _silu(x):
    return x * jax.nn.sigmoid(x)


def _dsilu(x):
    s = jax.nn.sigmoid(x)
    return s * (1.0 + x * (1.0 - s))


def _softplus(x):
    return jnp.maximum(x, 0.0) + jnp.log1p(jnp.exp(-jnp.abs(x)))


_GELU_K = 0.7978845608028654
_GELU_C = 0.044715


def _gelu(x):
    return 0.5 * x * (1.0 + jnp.tanh(_GELU_K * (x + _GELU_C * x * x * x)))


def _dgelu(x):
    th = jnp.tanh(_GELU_K * (x + _GELU_C * x * x * x))
    return 0.5 * (1.0 + th) + 0.5 * x * (1.0 - th * th) * _GELU_K * (1.0 + 3.0 * _GELU_C * x * x)


def _rms(h):
    r = lax.rsqrt(jnp.mean(h * h, axis=-1, keepdims=True) + EPS)
    return h * r, r


def _rms_bwd(dn, n, r):
    return r * (dn - n * jnp.mean(dn * n, axis=-1, keepdims=True))


def _colsum(x):
    return jnp.sum(x, axis=0, keepdims=True)


def _params(sem=None):
    return pltpu.CompilerParams(dimension_semantics=sem, vmem_limit_bytes=VMEM_LIMIT)


def _full(shape):
    return pl.BlockSpec(shape, lambda *_: (0,) * len(shape))


def _acc(ref, val):
    @pl.when(pl.program_id(0) == 0)
    def _():
        ref[...] = val

    @pl.when(pl.program_id(0) != 0)
    def _():
        ref[...] += val


def _me():
    return lax.axis_index("x"), lax.axis_index("y"), lax.axis_index("c")


def _my_index():
    x, y, c = _me()
    return 4 * x + 2 * y + c


def _coords(p):
    return (p // 4, (p // 2) % 2, p % 2)


class _Exchange:
    def __init__(self, srcs, gather):
        self.srcs = list(srcs)
        self.gather = gather
        self.n = len(self.srcs)
        self.out_shape = [jax.ShapeDtypeStruct(((N_DEV,) + s.shape) if gather else s.shape, s.dtype) for s in self.srcs]
        self.scratch = [pltpu.SemaphoreType.DMA((self.n, N_DEV)), pltpu.SemaphoreType.DMA((self.n, N_DEV)),
                        pltpu.SemaphoreType.DMA((self.n,))]

    def _src(self, refs, t, dev):
        return refs[t] if self.gather else refs[t].at[dev]

    def _remote(self, xin, xout, sems, t, k, me, to):
        return pltpu.make_async_remote_copy(
            src_ref=self._src(xin, t, to), dst_ref=xout[t].at[me], send_sem=sems[0].at[t, k], recv_sem=sems[1].at[t, k],
            device_id=_coords(to), device_id_type=MESH)

    def start(self, xin, xout, sems):
        me = _my_index()
        for t in range(self.n):
            pltpu.make_async_copy(self._src(xin, t, me), xout[t].at[me], sems[2].at[t]).start()
            for k in range(1, N_DEV):
                self._remote(xin, xout, sems, t, k, me, (me + k) % N_DEV).start()

    def wait(self, xin, xout, sems):
        me = _my_index()
        for t in range(self.n):
            for k in range(1, N_DEV):
                src = (me + N_DEV - k) % N_DEV
                pltpu.make_async_remote_copy(
                    src_ref=self._src(xin, t, src), dst_ref=xout[t].at[src], send_sem=sems[0].at[t, k],
                    recv_sem=sems[1].at[t, k], device_id=_coords(src), device_id_type=MESH).wait_recv()
        for t in range(self.n):
            for k in range(1, N_DEV):
                self._remote(xin, xout, sems, t, k, me, (me + k) % N_DEV).wait_send()
            pltpu.make_async_copy(self._src(xin, t, me), xout[t].at[me], sems[2].at[t]).wait()

    def forward(self, xin, xout, sems):
        pass


class _ChipGather:
    def __init__(self, srcs):
        self.srcs = list(srcs)
        self.n = len(self.srcs)
        self.out_shape = [jax.ShapeDtypeStruct((N_DEV,) + s.shape, s.dtype) for s in self.srcs]
        self.scratch = [pltpu.SemaphoreType.DMA((self.n, 7)), pltpu.SemaphoreType.DMA((self.n, 7)),
                        pltpu.SemaphoreType.DMA((self.n,))]

    @staticmethod
    def _places():
        x, y, c = _me()
        chips = [(1 - x, y), (x, 1 - y), (1 - x, 1 - y)]
        return (x, y, c), (x, y, 1 - c), chips

    @staticmethod
    def _slab(ref, dev):
        return ref.at[4 * dev[0] + 2 * dev[1] + dev[2]]

    def _copy(self, xin, xout, sems, t, k, block, to, src=None):
        return pltpu.make_async_remote_copy(
            src_ref=self._slab(xout[t], block) if src is None else src, dst_ref=self._slab(xout[t], block),
            send_sem=sems[0].at[t, k], recv_sem=sems[1].at[t, k], device_id=to, device_id_type=MESH)

    def start(self, xin, xout, sems):
        me, sibling, chips = self._places()
        for t in range(self.n):
            pltpu.make_async_copy(xin[t], self._slab(xout[t], me), sems[2].at[t]).start()
            self._copy(xin, xout, sems, t, 0, me, sibling, src=xin[t]).start()
            for j, chip in enumerate(chips):
                self._copy(xin, xout, sems, t, 1 + j, me, (*chip, me[2]), src=xin[t]).start()

    def forward(self, xin, xout, sems):
        me, sibling, chips = self._places()
        for t in range(self.n):
            for j, chip in enumerate(chips):
                self._copy(xin, xout, sems, t, 1 + j, (*chip, me[2]), me).wait_recv()
                self._copy(xin, xout, sems, t, 4 + j, (*chip, me[2]), sibling).start()

    def wait(self, xin, xout, sems):
        me, sibling, chips = self._places()
        for t in range(self.n):
            self._copy(xin, xout, sems, t, 0, sibling, me).wait_recv()
            for j, chip in enumerate(chips):
                self._copy(xin, xout, sems, t, 4 + j, (*chip, 1 - me[2]), me).wait_recv()
        for t in range(self.n):
            self._copy(xin, xout, sems, t, 0, me, sibling, src=xin[t]).wait_send()
            for j, chip in enumerate(chips):
                self._copy(xin, xout, sems, t, 1 + j, me, (*chip, me[2]), src=xin[t]).wait_send()
                self._copy(xin, xout, sems, t, 4 + j, (*chip, me[2]), sibling).wait_send()
            pltpu.make_async_copy(xin[t], self._slab(xout[t], me), sems[2].at[t]).wait()


def _call(body, *, name, grid, in_specs, out_specs, out_shape, args, semantics, scratch_shapes=(), xchg=None):
    if xchg is None:
        outs = pl.pallas_call(body, name=name, grid=grid, in_specs=in_specs, out_specs=out_specs, out_shape=out_shape,
                              scratch_shapes=list(scratch_shapes), compiler_params=_params(semantics))(*args)
        return outs, ()
    n_in, n_out, n_scr, n = len(in_specs), len(out_specs), len(scratch_shapes), xchg.n

    def carried(*refs):
        ins, xin = refs[:n_in], refs[n_in:n_in + n]
        outs, xout = refs[n_in + n:n_in + n + n_out], refs[n_in + n + n_out:n_in + 2 * n + n_out]
        scr, sems = refs[n_in + 2 * n + n_out:n_in + 2 * n + n_out + n_scr], refs[n_in + 2 * n + n_out + n_scr:]
        step = pl.program_id(0)
        for d in range(1, len(grid)):
            step = step * grid[d] + pl.program_id(d)
        n_steps = functools.reduce(lambda a, b: a * b, grid)

        @pl.when(step == 0)
        def _():
            xchg.start(xin, xout, sems)

        @pl.when(step == (2 * n_steps) // 3)
        def _():
            xchg.forward(xin, xout, sems)

        body(*ins, *outs, *scr)

        @pl.when(step == n_steps - 1)
        def _():
            xchg.wait(xin, xout, sems)

    res = pl.pallas_call(
        carried, name=name, grid=grid, in_specs=list(in_specs) + [ANY] * n, out_specs=list(out_specs) + [ANY] * n,
        out_shape=list(out_shape) + xchg.out_shape, scratch_shapes=list(scratch_shapes) + xchg.scratch,
        compiler_params=_params(("arbitrary",) * len(grid)))(*args, *xchg.srcs)
    return res[:n_out], tuple(res[n_out:])


def _exchange_alone(xchg, name):
    def body(*refs):
        xin, xout, sems = refs[:xchg.n], refs[xchg.n:2 * xchg.n], refs[2 * xchg.n:]
        xchg.start(xin, xout, sems)
        xchg.forward(xin, xout, sems)
        xchg.wait(xin, xout, sems)

    return pl.pallas_call(body, name=name, out_shape=xchg.out_shape, in_specs=[ANY] * xchg.n, out_specs=[ANY] * xchg.n,
                          scratch_shapes=xchg.scratch)(*xchg.srcs)


def _cast_shards(shards):
    n = len(shards)

    def body(*refs):
        for i, o in zip(refs[:n], refs[n:]):
            o[...] = i[...].astype(bf16)

    return pl.pallas_call(body, name="cast_shards", out_shape=[jax.ShapeDtypeStruct(s.shape, bf16) for s in shards],
                          in_specs=[VMEM] * n, out_specs=[VMEM] * n, compiler_params=_params())(*shards)


def _allreduce_rows(v):
    r = v.shape[0]
    rp = r // N_DEV

    def body(v_ref, o_ref, parts, sums, send1, recv1, send2, recv2):
        me = _my_index()

        def piece(ref, d):
            return ref.at[pl.ds(pl.multiple_of(d * rp, 8), rp), :]

        def copy1(k, src_dev, to):
            return pltpu.make_async_remote_copy(src_ref=piece(v_ref, to), dst_ref=parts.at[src_dev], send_sem=send1.at[k],
                                                recv_sem=recv1.at[k], device_id=_coords(to), device_id_type=MESH)

        def copy2(k, owner, to):
            return pltpu.make_async_remote_copy(src_ref=sums, dst_ref=piece(o_ref, owner), send_sem=send2.at[k],
                                                recv_sem=recv2.at[k], device_id=_coords(to), device_id_type=MESH)

        for k in range(1, N_DEV):
            copy1(k, me, (me + k) % N_DEV).start()
        parts[me] = v_ref[pl.ds(pl.multiple_of(me * rp, 8), rp), :]
        for k in range(1, N_DEV):
            copy1(k, (me + N_DEV - k) % N_DEV, me).wait_recv()
        total = parts[0]
        for s in range(1, N_DEV):
            total = total + parts[s]
        sums[...] = total
        o_ref[pl.ds(pl.multiple_of(me * rp, 8), rp), :] = total
        for k in range(1, N_DEV):
            copy2(k, me, (me + k) % N_DEV).start()
        for k in range(1, N_DEV):
            copy2(k, (me + N_DEV - k) % N_DEV, me).wait_recv()
        for k in range(1, N_DEV):
            copy1(k, me, (me + k) % N_DEV).wait_send()
            copy2(k, me, (me + k) % N_DEV).wait_send()

    return pl.pallas_call(
        body, name="allreduce_small_grads", out_shape=jax.ShapeDtypeStruct(v.shape, v.dtype),
        in_specs=[VMEM], out_specs=VMEM,
        scratch_shapes=[pltpu.VMEM((N_DEV, rp, LANES), f32), pltpu.VMEM((rp, LANES), f32)]
        + [pltpu.SemaphoreType.DMA((N_DEV,))] * 4,
        compiler_params=_params(),
    )(v)


def _gather_rows(v, name):
    def body(v_ref, o_ref, send_sems, recv_sems):
        me = _my_index()
        o_ref[me] = v_ref[...]
        sends = []
        for k in range(1, N_DEV):
            peer = (me + k) % N_DEV
            rc = pltpu.make_async_remote_copy(src_ref=v_ref, dst_ref=o_ref.at[me], send_sem=send_sems.at[k],
                                              recv_sem=recv_sems.at[k], device_id=_coords(peer), device_id_type=MESH)
            rc.start()
            sends.append(rc)
        for k in range(1, N_DEV):
            src = (me + N_DEV - k) % N_DEV
            pltpu.make_async_remote_copy(src_ref=v_ref, dst_ref=o_ref.at[src], send_sem=send_sems.at[k],
                                         recv_sem=recv_sems.at[k], device_id=_coords(src), device_id_type=MESH).wait_recv()
        for rc in sends:
            rc.wait_send()

    return pl.pallas_call(
        body, name=name, out_shape=jax.ShapeDtypeStruct((N_DEV,) + v.shape, v.dtype),
        in_specs=[VMEM], out_specs=VMEM,
        scratch_shapes=[pltpu.SemaphoreType.DMA((N_DEV,)), pltpu.SemaphoreType.DMA((N_DEV,))],
        compiler_params=pltpu.CompilerParams(vmem_limit_bytes=VMEM_LIMIT),
    )(v)


def _all_to_all_rows(v, name):
    def body(v_ref, o_ref, send_sems, recv_sems):
        me = _my_index()
        o_ref[me] = v_ref[me]
        sends = []
        for k in range(1, N_DEV):
            peer = (me + k) % N_DEV
            rc = pltpu.make_async_remote_copy(src_ref=v_ref.at[peer], dst_ref=o_ref.at[me], send_sem=send_sems.at[k],
                                              recv_sem=recv_sems.at[k], device_id=_coords(peer), device_id_type=MESH)
            rc.start()
            sends.append(rc)
        for k in range(1, N_DEV):
            src = (me + N_DEV - k) % N_DEV
            pltpu.make_async_remote_copy(src_ref=v_ref.at[src], dst_ref=o_ref.at[src], send_sem=send_sems.at[k],
                                         recv_sem=recv_sems.at[k], device_id=_coords(src), device_id_type=MESH).wait_recv()
        for rc in sends:
            rc.wait_send()

    return pl.pallas_call(
        body, name=name, out_shape=jax.ShapeDtypeStruct(v.shape, v.dtype),
        in_specs=[VMEM], out_specs=VMEM,
        scratch_shapes=[pltpu.SemaphoreType.DMA((N_DEV,)), pltpu.SemaphoreType.DMA((N_DEV,))],
    )(v)


def _ada_forward(c_all, ada_w, ada_b_cols):
    def body(c_ref, w_ref, b_ref, cond_ref, o_ref):
        cond = _silu(c_ref[...])
        cond_ref[...] = cond
        for l in range(2):
            o_ref[l] = _dot(_b(cond), _b(w_ref[l])) + b_ref[l]

    return pl.pallas_call(
        body, name="ada_forward",
        out_shape=[jax.ShapeDtypeStruct((N_DEV, D_MODEL), f32), jax.ShapeDtypeStruct((2, N_DEV, 768), f32)],
        in_specs=[VMEM] * 3, out_specs=[VMEM] * 2, compiler_params=_params(),
    )(c_all, ada_w, ada_b_cols)


def _ada_backward(cond, dmod_rows):
    def body(c_ref, d_ref, o_ref):
        cb = _b(c_ref[...])
        for l in range(2):
            o_ref[l] = _dot_tn(cb, _b(d_ref[l]))

    return pl.pallas_call(
        body, name="ada_backward", out_shape=jax.ShapeDtypeStruct((2, D_MODEL, 768), f32),
        in_specs=[VMEM] * 2, out_specs=VMEM, compiler_params=_params(),
    )(cond, dmod_rows)


def _inproj_fwd(h, norm_w, sc, sh, w_in, tb, xchg=None):
    t = h.shape[0]

    def body(h_ref, nw_ref, sc_ref, sh_ref, w_ref, proj_ref, u_ref):
        n, _ = _rms(h_ref[...])
        u = _b(n * nw_ref[...] * (1.0 + sc_ref[...]) + sh_ref[...])
        u_ref[...] = u
        proj_ref[...] = _dot(u, w_ref[...])

    row = pl.BlockSpec((tb, D_MODEL), lambda i: (i, 0))
    vec = _full((1, D_MODEL))
    return _call(
        body, name="inproj_fwd", grid=(t // tb,),
        out_shape=[jax.ShapeDtypeStruct((t, P_IN), f32), jax.ShapeDtypeStruct((t, D_MODEL), bf16)],
        in_specs=[row, vec, vec, vec, _full((D_MODEL, P_IN))],
        out_specs=[pl.BlockSpec((tb, P_IN), lambda i: (i, 0)), row],
        semantics=("parallel",), args=(h, norm_w, sc, sh, w_in), xchg=xchg)


def _inproj_bwd(dparts, dh_res, h, norm_w, sc, sh, w_in, tb, xchg=None):
    t = h.shape[0]

    def body(*refs):
        parts = refs[:10]
        dres_ref, h_ref, nw_ref, sc_ref, sh_ref, w_ref = refs[10:16]
        dh_ref, dsh_ref, dsc_ref, dnw_ref = refs[16:]
        dproj = jnp.concatenate([p[...] for p in parts], axis=1)
        du = _dot_nt(dproj, w_ref[...])
        n, r = _rms(h_ref[...])
        nw = nw_ref[...]
        gain = 1.0 + sc_ref[...]
        _acc(dsh_ref, _colsum(du))
        _acc(dsc_ref, _colsum(du * n * nw))
        _acc(dnw_ref, _colsum(du * gain * n))
        dh_ref[...] = dres_ref[...] + _rms_bwd(du * nw * gain, n, r)

    row = pl.BlockSpec((tb, D_MODEL), lambda i: (i, 0))
    vec = _full((1, D_MODEL))
    part_specs = [pl.BlockSpec((tb, GROUP_W), lambda i: (i, 0))] * 9 + [pl.BlockSpec((tb, LANES), lambda i: (i, 0))]
    return _call(
        body, name="inproj_bwd", grid=(t // tb,),
        out_shape=[jax.ShapeDtypeStruct((t, D_MODEL), f32)] + [jax.ShapeDtypeStruct((1, D_MODEL), f32)] * 3,
        in_specs=part_specs + [row, row, vec, vec, vec, _full((D_MODEL, P_IN))],
        out_specs=[row, vec, vec, vec],
        semantics=("arbitrary",), xchg=xchg, args=(*dparts, dh_res, h, norm_w, sc, sh, w_in))


def _wgrad(a, b, n_blocks, name, tm, tk=512):
    t, m = a.shape
    nb = b.shape[1] // n_blocks
    tk = min(tk, t)
    nk = t // tk

    def body(a_ref, b_ref, o_ref, acc_ref):
        k = pl.program_id(2)
        p = _dot_tn(a_ref[...], b_ref[...])

        @pl.when(k == 0)
        def _():
            acc_ref[...] = p

        @pl.when(k != 0)
        def _():
            acc_ref[...] += p

        @pl.when(k == nk - 1)
        def _():
            o_ref[0] = acc_ref[...].astype(o_ref.dtype)

    return pl.pallas_call(
        body, name=name, grid=(m // tm, n_blocks, nk),
        out_shape=jax.ShapeDtypeStruct((n_blocks, m, nb), bf16),
        in_specs=[pl.BlockSpec((tk, tm), lambda i, j, k: (k, i)), pl.BlockSpec((tk, nb), lambda i, j, k: (k, j))],
        out_specs=pl.BlockSpec((1, tm, nb), lambda i, j, k: (j, i, 0)),
        scratch_shapes=[pltpu.VMEM((tm, nb), f32)],
        compiler_params=_params(("parallel", "parallel", "arbitrary")),
    )(a, b)


def _wgrad_parts(a, parts, name, tm, tk):
    t, m = a.shape
    n = sum(p.shape[1] for p in parts)
    n_parts = len(parts)
    tk = min(tk, t)
    nk = t // tk

    def body(*refs):
        a_ref, part_refs, o_ref, acc_ref = refs[0], refs[1:1 + n_parts], refs[1 + n_parts], refs[2 + n_parts]
        k = pl.program_id(1)
        p = _dot_tn(a_ref[...], jnp.concatenate([r[...] for r in part_refs], axis=1))

        @pl.when(k == 0)
        def _():
            acc_ref[...] = p

        @pl.when(k != 0)
        def _():
            acc_ref[...] += p

        @pl.when(k == nk - 1)
        def _():
            o_ref[...] = acc_ref[...].astype(o_ref.dtype)

    return pl.pallas_call(
        body, name=name, grid=(m // tm, nk),
        out_shape=jax.ShapeDtypeStruct((m, n), bf16),
        in_specs=[pl.BlockSpec((tk, tm), lambda i, k: (k, i))]
        + [pl.BlockSpec((tk, p.shape[1]), lambda i, k: (k, 0)) for p in parts],
        out_specs=pl.BlockSpec((tm, n), lambda i, k: (i, 0)),
        scratch_shapes=[pltpu.VMEM((tm, n), f32)],
        compiler_params=_params(("parallel", "arbitrary")),
    )(a, *parts)


def _pool_counts(rows, t0):
    tpos = (lax.broadcasted_iota(jnp.int32, (rows, GROUP_W), 0) + t0 + 1).astype(f32)
    grp = lax.broadcasted_iota(jnp.int32, (rows, GROUP_W), 1) // 64
    win = jnp.where(grp == 0, 2.0, jnp.where(grp == 1, 4.0, jnp.where(grp == 2, 8.0, 16.0)))
    return jnp.minimum(tpos, win), grp


def _pool_select(grp, l1, l2, l3, l4):
    return jnp.where(grp == 0, l1, jnp.where(grp == 1, l2, jnp.where(grp == 2, l3, l4)))


def _pool_means(v, halo, t0):
    tb = v.shape[0]
    ext = jnp.concatenate([halo, v], axis=0)
    n = tb + 16
    s1 = ext[1:n] + ext[0:n - 1]
    s2 = s1[2:n - 1] + s1[0:n - 3]
    s3 = s2[4:n - 3] + s2[0:n - 7]
    s4 = s3[8:n - 7] + s3[0:n - 15]
    cnt, grp = _pool_counts(tb, t0)
    wsum = _pool_select(grp, s1[15:15 + tb], s2[13:13 + tb], s3[9:9 + tb], s4[1:1 + tb])
    return wsum / cnt - v


def _pool_fwd(proj, pw_bd, scale, tb):
    t = proj.shape[0]

    def body(v_ref, vh_ref, pw_ref, sc_ref, o_ref):
        i = pl.program_id(0)
        halo = jnp.where(i > 0, vh_ref[...], 0.0)
        p = _pool_means(v_ref[...], halo, i * tb)
        o_ref[...] = _dot(_b(p), _b(pw_ref[...])) * sc_ref[...]

    return pl.pallas_call(
        body, name="pool_fwd", grid=(t // tb,),
        out_shape=jax.ShapeDtypeStruct((t, GROUP_W), f32),
        in_specs=[pl.BlockSpec((tb, GROUP_W), lambda i: (i, C_POOL)),
                  pl.BlockSpec((16, GROUP_W), lambda i: (jnp.maximum(i * (tb // 16) - 1, 0), C_POOL)),
                  _full((GROUP_W, GROUP_W)), _full((1, GROUP_W))],
        out_specs=pl.BlockSpec((tb, GROUP_W), lambda i: (i, 0)),
        compiler_params=_params(("parallel",)),
    )(proj, proj, pw_bd, scale)


def _pool_bwd(proj, dy, pw_bd, scale, tb):
    t = proj.shape[0]
    nt = t // tb
    last16 = t // 16 - 1

    def body(v_ref, vh_ref, dy_ref, dyh_ref, pw_ref, sc_ref, dv_ref, dpw_ref, dsc_ref):
        i = pl.program_id(0)
        halo = jnp.where(i > 0, vh_ref[...], 0.0)
        p = _pool_means(v_ref[...], halo, i * tb)
        pw = _b(pw_ref[...])
        sc = sc_ref[...]
        dy = dy_ref[...]
        ypre = _dot(_b(p), pw)
        _acc(dsc_ref, _colsum(dy * ypre))
        dys = _b(dy * sc)
        _acc(dpw_ref, _dot_tn(_b(p), dys))
        dp = _dot_nt(dys, pw)
        dph = _dot_nt(_b(jnp.where(i < nt - 1, dyh_ref[...], 0.0) * sc), pw)
        cnt, grp = _pool_counts(tb, i * tb)
        cnth, _ = _pool_counts(16, (i + 1) * tb)
        ext = jnp.concatenate([dp / cnt, dph / cnth], axis=0)
        n = tb + 16
        f1 = ext[0:n - 1] + ext[1:n]
        f2 = f1[0:n - 3] + f1[2:n - 1]
        f3 = f2[0:n - 7] + f2[4:n - 3]
        f4 = f3[0:n - 15] + f3[8:n - 7]
        dv_ref[...] = _b(_pool_select(grp, f1[0:tb], f2[0:tb], f3[0:tb], f4[0:tb]) - dp)

    return pl.pallas_call(
        body, name="pool_bwd", grid=(nt,),
        out_shape=[jax.ShapeDtypeStruct((t, GROUP_W), bf16), jax.ShapeDtypeStruct((GROUP_W, GROUP_W), f32),
                   jax.ShapeDtypeStruct((1, GROUP_W), f32)],
        in_specs=[pl.BlockSpec((tb, GROUP_W), lambda i: (i, C_POOL)),
                  pl.BlockSpec((16, GROUP_W), lambda i: (jnp.maximum(i * (tb // 16) - 1, 0), C_POOL)),
                  pl.BlockSpec((tb, GROUP_W), lambda i: (i, 0)),
                  pl.BlockSpec((16, GROUP_W), lambda i: (jnp.minimum((i + 1) * (tb // 16), last16), 0)),
                  _full((GROUP_W, GROUP_W)), _full((1, GROUP_W))],
        out_specs=[pl.BlockSpec((tb, GROUP_W), lambda i: (i, 0)), _full((GROUP_W, GROUP_W)), _full((1, GROUP_W))],
        compiler_params=_params(("arbitrary",)),
    )(proj, proj, dy, dy, pw_bd, scale)


def _sconv_fwd(proj, w, tb):
    t = proj.shape[0]

    def body(gb_ref, gc_ref, hh_ref, gch_ref, hhh_ref, w_ref, o_ref):
        i = pl.program_id(0)
        q = gc_ref[...] * hh_ref[...]
        qh = jnp.where(i > 0, gch_ref[...] * hhh_ref[...], 0.0)
        ext = jnp.concatenate([qh, q], axis=0)
        w = w_ref[...]
        conv = w[0:1] * ext[6:6 + tb] + w[1:2] * ext[7:7 + tb] + w[2:3] * ext[8:8 + tb]
        o_ref[...] = gb_ref[...] * conv

    def col(c):
        return pl.BlockSpec((tb, GROUP_W), lambda i: (i, c))

    def prev(c):
        return pl.BlockSpec((8, GROUP_W), lambda i: (jnp.maximum(i * (tb // 8) - 1, 0), c))

    return pl.pallas_call(
        body, name="sconv_fwd", grid=(t // tb,),
        out_shape=jax.ShapeDtypeStruct((t, GROUP_W), f32),
        in_specs=[col(C_GB), col(C_GC), col(C_HH), prev(C_GC), prev(C_HH), _full((8, GROUP_W))],
        out_specs=pl.BlockSpec((tb, GROUP_W), lambda i: (i, 0)),
        compiler_params=_params(("parallel",)),
    )(proj, proj, proj, proj, proj, w)


def _sconv_bwd(proj, dy, w, tb):
    t = proj.shape[0]
    nt = t // tb
    last8 = t // 8 - 1

    def body(gb_ref, gc_ref, hh_ref, gch_ref, hhh_ref, gbn_ref, dy_ref, dyn_ref, w_ref, dgb_ref, dgc_ref, dhh_ref, dw_ref):
        i = pl.program_id(0)
        gc, hh, gb, dy = gc_ref[...], hh_ref[...], gb_ref[...], dy_ref[...]
        q = gc * hh
        qh = jnp.where(i > 0, gch_ref[...] * hhh_ref[...], 0.0)
        ext = jnp.concatenate([qh, q], axis=0)
        w = w_ref[...]
        conv = w[0:1] * ext[6:6 + tb] + w[1:2] * ext[7:7 + tb] + w[2:3] * ext[8:8 + tb]
        dgb_ref[...] = _b(dy * conv)
        e = dy * gb
        en = jnp.where(i < nt - 1, dyn_ref[...] * gbn_ref[...], 0.0)
        exte = jnp.concatenate([e, en], axis=0)
        dq = w[2:3] * exte[0:tb] + w[1:2] * exte[1:1 + tb] + w[0:1] * exte[2:2 + tb]
        dgc_ref[...] = _b(dq * hh)
        dhh_ref[...] = _b(dq * gc)
        dw = jnp.concatenate([_colsum(e * ext[6:6 + tb]), _colsum(e * ext[7:7 + tb]), _colsum(e * ext[8:8 + tb]),
                              jnp.zeros((5, GROUP_W), f32)], axis=0)
        _acc(dw_ref, dw)

    def col(c):
        return pl.BlockSpec((tb, GROUP_W), lambda i: (i, c))

    def prev(c):
        return pl.BlockSpec((8, GROUP_W), lambda i: (jnp.maximum(i * (tb // 8) - 1, 0), c))

    def nxt(c):
        return pl.BlockSpec((8, GROUP_W), lambda i: (jnp.minimum((i + 1) * (tb // 8), last8), c))

    out = pl.BlockSpec((tb, GROUP_W), lambda i: (i, 0))
    return pl.pallas_call(
        body, name="sconv_bwd", grid=(nt,),
        out_shape=[jax.ShapeDtypeStruct((t, GROUP_W), bf16)] * 3 + [jax.ShapeDtypeStruct((8, GROUP_W), f32)],
        in_specs=[col(C_GB), col(C_GC), col(C_HH), prev(C_GC), prev(C_HH), nxt(C_GB), col(0), nxt(0), _full((8, GROUP_W))],
        out_specs=[out, out, out, _full((8, GROUP_W))],
        compiler_params=_params(("arbitrary",)),
    )(proj, proj, proj, proj, proj, proj, dy, dy, w)


def _conv4(xr, halo, w, bias):
    tb = xr.shape[0]
    ext = jnp.concatenate([halo, xr], axis=0)
    pre = w[0:1] * ext[5:5 + tb] + w[1:2] * ext[6:6 + tb] + w[2:3] * ext[7:7 + tb] + w[3:4] * ext[8:8 + tb] + bias
    return pre, ext


def _tri():
    r = lax.broadcasted_iota(jnp.int32, (SSD_CHUNK, SSD_CHUNK), 0)
    c = lax.broadcasted_iota(jnp.int32, (SSD_CHUNK, SSD_CHUNK), 1)
    return r >= c


def _lane_pick(vals):
    rows = vals[0].shape[0]
    lane = lax.broadcasted_iota(jnp.int32, (rows, LANES), 1)
    out = jnp.zeros((rows, LANES), f32)
    for h, v in enumerate(vals):
        out = jnp.where(lane == h, v, out)
    return out


def _ssd_fwd(proj, conv_w, conv_b, dt_bias, a_log, d_cols, tb, xchg=None):
    t = proj.shape[0]
    cpt = tb // SSD_CHUNK

    def body(z_ref, xs_ref, bm_ref, cm_ref, xsh_ref, bmh_ref, cmh_ref, dt_ref, cw_ref, cb_ref, dtb_ref, al_ref, dk_ref,
             o_ref, y_ref, st_ref, state):
        i = pl.program_id(0)

        @pl.when(i == 0)
        def _():
            state[...] = jnp.zeros_like(state)

        cw, cb = cw_ref[...], cb_ref[...]
        acts = []
        for j, (r, hr) in enumerate(((xs_ref, xsh_ref), (bm_ref, bmh_ref), (cm_ref, cmh_ref))):
            halo = jnp.where(i > 0, hr[...], 0.0)
            pre, _ = _conv4(r[...], halo, cw[:, j * 256:(j + 1) * 256], cb[:, j * 256:(j + 1) * 256])
            acts.append(_silu(pre))
        xs, bm, cm = acts
        dt = _softplus(dt_ref[...] + dtb_ref[...])
        a = -jnp.exp(al_ref[...])
        adt = dt * a
        tri = _tri()
        trif = tri.astype(f32)
        dk = dk_ref[...]
        for c in range(cpt):
            rows = slice(c * SSD_CHUNK, (c + 1) * SSD_CHUNK)
            acol = _dot_exact(trif, adt[rows])
            arow = acol.T
            dt_c = dt[rows]
            ys = []
            rowi = lax.broadcasted_iota(jnp.int32, (SSD_CHUNK, 1), 0)
            first = lax.broadcasted_iota(jnp.int32, (SSD_CHUNK, SSD_CHUNK), 1) < SSD_P
            for g in range(SSD_HEADS // 2):
                cols = slice(g * 128, (g + 1) * 128)
                cg, bg = _b(cm[rows, cols]), _b(bm[rows, cols])
                xg = xs[rows, cols]
                heads = (2 * g, 2 * g + 1)
                ac = [acol[:, h:h + 1] for h in heads]
                alast = [v[SSD_CHUNK - 1:SSD_CHUNK] for v in ac]
                dtw = jnp.where(first, dt_c[:, heads[0]:heads[0] + 1], dt_c[:, heads[1]:heads[1] + 1])
                eaw = jnp.where(first, jnp.exp(ac[0]), jnp.exp(ac[1]))
                wdw = jnp.where(first, jnp.exp(alast[0] - ac[0]), jnp.exp(alast[1] - ac[1]))
                xdt = xg * dtw
                xb = _b(xdt)
                gmat = _dot_nt(cg, bg)
                ydiag = []
                for k, h in enumerate(heads):
                    lm = jnp.exp(jnp.where(tri, ac[k] - arow[h:h + 1, :], -jnp.inf))
                    ydiag.append(_dot(_b(gmat * lm), xb[:, k * SSD_P:(k + 1) * SSD_P]))
                s_in = state[g]
                st_ref[c, g] = s_in
                ys.append(jnp.concatenate(ydiag, axis=1) + eaw * _dot_nt(cg, _b(s_in)) + xg * dk[:, cols])
                state[g] = jnp.where(rowi < SSD_P, jnp.exp(alast[0]), jnp.exp(alast[1])) * s_in + _dot_tn(_b(xdt * wdw), bg)
            yc = jnp.concatenate(ys, axis=1)
            y_ref[rows, :] = yc
            o_ref[rows, :] = yc * _silu(z_ref[rows, :])

    def col(c):
        return pl.BlockSpec((tb, GROUP_W), lambda i: (i, c))

    def prev(c):
        return pl.BlockSpec((8, GROUP_W), lambda i: (jnp.maximum(i * (tb // 8) - 1, 0), c))

    out = pl.BlockSpec((tb, GROUP_W), lambda i: (i, 0))
    return _call(
        body, name="ssd_fwd", grid=(t // tb,),
        out_shape=[jax.ShapeDtypeStruct((t, GROUP_W), f32), jax.ShapeDtypeStruct((t, GROUP_W), f32),
                   jax.ShapeDtypeStruct((t // SSD_CHUNK, 2, 128, 128), f32)],
        in_specs=[col(C_Z), col(C_XS), col(C_BM), col(C_CM), prev(C_XS), prev(C_BM), prev(C_CM),
                  pl.BlockSpec((tb, LANES), lambda i: (i, C_DT128)),
                  _full((8, 768)), _full((1, 768)), _full((1, LANES)), _full((1, LANES)), _full((1, GROUP_W))],
        out_specs=[out, out, pl.BlockSpec((cpt, 2, 128, 128), lambda i: (i, 0, 0, 0))],
        scratch_shapes=[pltpu.VMEM((2, 128, 128), f32)],
        semantics=("arbitrary",), xchg=xchg,
        args=(proj, proj, proj, proj, proj, proj, proj, proj, conv_w, conv_b, dt_bias, a_log, d_cols))


def _ssd_bwd(proj, dyc, y_pre, states, conv_w, conv_b, dt_bias, a_log, d_cols, tb, xchg=None):
    t = proj.shape[0]
    nt = t // tb
    cpt = tb // SSD_CHUNK

    def body(z_ref, xs_ref, bm_ref, cm_ref, xsh_ref, bmh_ref, cmh_ref, dt_ref, dy_ref, yp_ref, st_ref,
             cw_ref, cb_ref, dtb_ref, al_ref, dk_ref,
             dz_ref, dxs_ref, dbm_ref, dcm_ref, ddt_ref, dcw_ref, dcb_ref, ddtb_ref, dal_ref, ddk_ref,
             dstate, carry):
        i = pl.program_id(0)
        ti = nt - 1 - i

        @pl.when(i == 0)
        def _():
            dstate[...] = jnp.zeros_like(dstate)
            carry[...] = jnp.zeros_like(carry)

        cw, cb = cw_ref[...], cb_ref[...]
        pres, exts, acts = [], [], []
        for j, (r, hr) in enumerate(((xs_ref, xsh_ref), (bm_ref, bmh_ref), (cm_ref, cmh_ref))):
            halo = jnp.where(ti > 0, hr[...], 0.0)
            pre, ext = _conv4(r[...], halo, cw[:, j * 256:(j + 1) * 256], cb[:, j * 256:(j + 1) * 256])
            pres.append(pre)
            exts.append(ext)
            acts.append(_silu(pre))
        xs, bm, cm = acts
        raw = dt_ref[...] + dtb_ref[...]
        dt = _softplus(raw)
        a = -jnp.exp(al_ref[...])
        adt = dt * a
        tri = _tri()
        trif = tri.astype(f32)
        dk = dk_ref[...]
        z = z_ref[...]
        dyc = dy_ref[...]
        dz_ref[...] = _b(dyc * yp_ref[...] * _dsilu(z))
        dy_all = dyc * _silu(z)
        lane = lax.broadcasted_iota(jnp.int32, (1, LANES), 1)
        ddk_acc = jnp.zeros((1, LANES), f32)
        dal_acc = jnp.zeros((1, LANES), f32)
        dxs_c, dbm_c, dcm_c, ddt_c = [None] * cpt, [None] * cpt, [None] * cpt, [None] * cpt
        for c in reversed(range(cpt)):
            rows = slice(c * SSD_CHUNK, (c + 1) * SSD_CHUNK)
            acol = _dot_exact(trif, adt[rows])
            arow = acol.T
            dt_c = dt[rows]
            da_cols, da_rows, ddt_heads, dxs_groups, dbg, dcg = [], [], [], [], [], []
            rowi = lax.broadcasted_iota(jnp.int32, (SSD_CHUNK, 1), 0)
            first = lax.broadcasted_iota(jnp.int32, (SSD_CHUNK, SSD_CHUNK), 1) < SSD_P
            for g in range(SSD_HEADS // 2):
                cols = slice(g * 128, (g + 1) * 128)
                cgf, bgf = cm[rows, cols], bm[rows, cols]
                cg, bg = _b(cgf), _b(bgf)
                xg, dyg = xs[rows, cols], dy_all[rows, cols]
                s_in, dsn = st_ref[c, g], dstate[g]
                sb, dsnb = _b(s_in), _b(dsn)
                heads = (2 * g, 2 * g + 1)
                ac = [acol[:, h:h + 1] for h in heads]
                alast = [v[SSD_CHUNK - 1:SSD_CHUNK] for v in ac]
                el = [jnp.exp(v) for v in alast]
                dtw = jnp.where(first, dt_c[:, heads[0]:heads[0] + 1], dt_c[:, heads[1]:heads[1] + 1])
                eaw = jnp.where(first, jnp.exp(ac[0]), jnp.exp(ac[1]))
                wdw = jnp.where(first, jnp.exp(alast[0] - ac[0]), jnp.exp(alast[1] - ac[1]))
                xdt = xg * dtw
                xb, dyb = _b(xdt), _b(dyg)
                gmat = _dot_nt(cg, bg)
                dgs, dxh, da = None, [], []
                for k, h in enumerate(heads):
                    hc = slice(k * SSD_P, (k + 1) * SSD_P)
                    lm = jnp.exp(jnp.where(tri, ac[k] - arow[h:h + 1, :], -jnp.inf))
                    m = gmat * lm
                    dm = _dot_nt(dyb[:, hc], xb[:, hc])
                    dxh.append(_dot_tn(_b(m), dyb[:, hc]))
                    dgs = dm * lm if dgs is None else dgs + dm * lm
                    wm = dm * m
                    da.append(jnp.sum(wm, axis=1, keepdims=True))
                    da_rows.append(jnp.sum(wm, axis=0, keepdims=True))
                dgb = _b(dgs)
                dcg_g = _dot(dgb, bg)
                dbg_g = _dot_tn(dgb, cg)
                yoff = eaw * _dot_nt(cg, sb)
                dyoff = dyg * yoff
                dye = _b(dyg * eaw)
                dcg_g = dcg_g + _dot(dye, sb)
                ds_y = _dot_tn(dye, cg)
                u = _dot_nt(bg, dsnb)
                dx = jnp.concatenate(dxh, axis=1) + wdw * u
                dbg_g = dbg_g + _dot(_b(xdt * wdw), dsnb)
                xu = xdt * u * wdw
                ss = jnp.sum(dsn * s_in, axis=1, keepdims=True)
                dxx = dx * xg
                dyx = _colsum(dyg * xg)
                for k, h in enumerate(heads):
                    mine = first if k == 0 else jnp.logical_not(first)
                    dwv = jnp.sum(jnp.where(mine, xu, 0.0), axis=1, keepdims=True)
                    mine_rows = (rowi < SSD_P) if k == 0 else (rowi >= SSD_P)
                    dalast = jnp.sum(dwv, axis=0, keepdims=True) + el[k] * jnp.sum(jnp.where(mine_rows, ss, 0.0), axis=0, keepdims=True)
                    dah = da[k] + jnp.sum(jnp.where(mine, dyoff, 0.0), axis=1, keepdims=True) - dwv
                    da_cols.append(dah + jnp.where(rowi == SSD_CHUNK - 1, dalast, 0.0))
                    ddt_heads.append(jnp.sum(jnp.where(mine, dxx, 0.0), axis=1, keepdims=True))
                    ddk_acc = ddk_acc + jnp.where(lane == h, jnp.sum(jnp.where(mine[0:1], dyx, 0.0), axis=1, keepdims=True), 0.0)
                dstate[g] = jnp.where(rowi < SSD_P, el[0], el[1]) * dsn + ds_y
                dxs_groups.append(dx * dtw + dyg * dk[:, cols])
                dbg.append(dbg_g)
                dcg.append(dcg_g)
            da_blk = _lane_pick(da_cols)
            rowsel = lax.broadcasted_iota(jnp.int32, (SSD_CHUNK, SSD_CHUNK), 0)
            da_rows_blk = jnp.zeros((SSD_CHUNK, SSD_CHUNK), f32)
            for h in range(SSD_HEADS):
                da_rows_blk = jnp.where(rowsel == h, da_rows[h], da_rows_blk)
            da_blk = da_blk - da_rows_blk.T
            dadt = lax.dot_general(trif, da_blk, (((0,), (0,)), ((), ())), preferred_element_type=f32,
                                   precision=lax.Precision.HIGHEST)
            dal_acc = dal_acc + _colsum(dadt * dt_c)
            ddt_c[c] = dadt * a + _lane_pick(ddt_heads)
            dxs_c[c] = jnp.concatenate(dxs_groups, axis=1)
            dbm_c[c] = jnp.concatenate(dbg, axis=1)
            dcm_c[c] = jnp.concatenate(dcg, axis=1)
        ddt = jnp.concatenate(ddt_c, axis=0) if cpt > 1 else ddt_c[0]
        ddraw = jnp.where(lane < SSD_HEADS, ddt * jax.nn.sigmoid(raw), 0.0)
        ddt_ref[...] = _b(ddraw)
        _acc(ddtb_ref, _colsum(ddraw))
        _acc(dal_ref, jnp.where(lane < SSD_HEADS, dal_acc * a, 0.0))
        _acc(ddk_ref, ddk_acc)
        dcw_parts, dcb_parts = [], []
        for j, (dparts, out_ref) in enumerate(((dxs_c, dxs_ref), (dbm_c, dbm_ref), (dcm_c, dcm_ref))):
            dact = jnp.concatenate(dparts, axis=0) if cpt > 1 else dparts[0]
            dpre = dact * _dsilu(pres[j])
            w = cw[:, j * 256:(j + 1) * 256]
            ext = jnp.concatenate([dpre, carry[:, j * 256:(j + 1) * 256]], axis=0)
            out_ref[...] = _b(w[3:4] * ext[0:tb] + w[2:3] * ext[1:1 + tb] + w[1:2] * ext[2:2 + tb] + w[0:1] * ext[3:3 + tb])
            carry[:, j * 256:(j + 1) * 256] = dpre[0:8]
            xe = exts[j]
            dcw_parts.append(jnp.concatenate([_colsum(dpre * xe[5 + k:5 + k + tb]) for k in range(4)]
                                             + [jnp.zeros((4, GROUP_W), f32)], axis=0))
            dcb_parts.append(_colsum(dpre))
        _acc(dcw_ref, jnp.concatenate(dcw_parts, axis=1))
        _acc(dcb_ref, jnp.concatenate(dcb_parts, axis=1))

    def col(c):
        return pl.BlockSpec((tb, GROUP_W), lambda i: (nt - 1 - i, c))

    def prev(c):
        return pl.BlockSpec((8, GROUP_W), lambda i: (jnp.maximum((nt - 1 - i) * (tb // 8) - 1, 0), c))

    out = pl.BlockSpec((tb, GROUP_W), lambda i: (nt - 1 - i, 0))
    vec = _full((1, LANES))
    return _call(
        body, name="ssd_bwd", grid=(nt,),
        out_shape=[jax.ShapeDtypeStruct((t, GROUP_W), bf16)] * 4 + [jax.ShapeDtypeStruct((t, LANES), bf16),
                   jax.ShapeDtypeStruct((8, 768), f32), jax.ShapeDtypeStruct((1, 768), f32)]
        + [jax.ShapeDtypeStruct((1, LANES), f32)] * 3,
        in_specs=[col(C_Z), col(C_XS), col(C_BM), col(C_CM), prev(C_XS), prev(C_BM), prev(C_CM),
                  pl.BlockSpec((tb, LANES), lambda i: (nt - 1 - i, C_DT128)), out, out,
                  pl.BlockSpec((cpt, 2, 128, 128), lambda i: (nt - 1 - i, 0, 0, 0)),
                  _full((8, 768)), _full((1, 768)), vec, vec, _full((1, GROUP_W))],
        out_specs=[out, out, out, out, pl.BlockSpec((tb, LANES), lambda i: (nt - 1 - i, 0)),
                   _full((8, 768)), _full((1, 768)), vec, vec, vec],
        scratch_shapes=[pltpu.VMEM((2, 128, 128), f32), pltpu.VMEM((8, 768), f32)],
        semantics=("arbitrary",), xchg=xchg,
        args=(proj, proj, proj, proj, proj, proj, proj, proj, dyc, y_pre, states, conv_w, conv_b, dt_bias, a_log, d_cols))


def _s5_coeffs(are, aim, ls):
    step = jnp.exp(ls)
    mag = jnp.exp(are * step)
    th = aim * step
    lre, lim = mag * jnp.cos(th), mag * jnp.sin(th)
    den = are * are + aim * aim
    nr = lre - 1.0
    fre = (nr * are + lim * aim) / den
    fim = (lim * are - nr * aim) / den
    return step, lre, lim, den, fre, fim


def _s5_prep(are, aim, ls, bre_bd, bim_bd):
    def body(are_ref, aim_ref, ls_ref, bre_ref, bim_ref, lre_ref, lim_ref, bbr_ref, bbi_ref):
        _, lre, lim, _, fre, fim = _s5_coeffs(are_ref[...], aim_ref[...], ls_ref[...])
        lre_ref[...] = lre
        lim_ref[...] = lim
        bre, bim = bre_ref[...], bim_ref[...]
        bbr_ref[...] = fre * bre - fim * bim
        bbi_ref[...] = fre * bim + fim * bre

    col = jax.ShapeDtypeStruct((S5_N, 1), f32)
    mat = jax.ShapeDtypeStruct((S5_N, GROUP_W), f32)
    return pl.pallas_call(body, name="s5_prep", out_shape=[col, col, mat, mat], in_specs=[VMEM] * 5, out_specs=[VMEM] * 4,
                          compiler_params=_params())(are, aim, ls, bre_bd, bim_bd)


def _s5_prep_bwd(are, aim, ls, bre_bd, bim_bd, dlre, dlim, dbbr, dbbi):
    def body(are_ref, aim_ref, ls_ref, bre_ref, bim_ref, dlre_ref, dlim_ref, dbbr_ref, dbbi_ref,
             dare_ref, daim_ref, dls_ref, dbre_ref, dbim_ref):
        are, aim = are_ref[...], aim_ref[...]
        step, lre, lim, den, fre, fim = _s5_coeffs(are, aim, ls_ref[...])
        r = lax.broadcasted_iota(jnp.int32, (S5_N, GROUP_W), 0) // 64
        c = lax.broadcasted_iota(jnp.int32, (S5_N, GROUP_W), 1) // 16
        mask = r == c
        gr = jnp.where(mask, dbbr_ref[...], 0.0)
        gi = jnp.where(mask, dbbi_ref[...], 0.0)
        bre, bim = bre_ref[...], bim_ref[...]
        dbre_ref[...] = fre * gr + fim * gi
        dbim_ref[...] = fre * gi - fim * gr
        dfre = jnp.sum(bre * gr + bim * gi, axis=1, keepdims=True)
        dfim = jnp.sum(bre * gi - bim * gr, axis=1, keepdims=True)
        ire, iim = are / den, aim / den
        tre = dlre_ref[...] + ire * dfre - iim * dfim
        tim = dlim_ref[...] + ire * dfim + iim * dfre
        dzre = lre * tre + lim * tim
        dzim = lre * tim - lim * tre
        qre = (fre * are + fim * aim) / den
        qim = (fim * are - fre * aim) / den
        dare_ref[...] = step * dzre - (qre * dfre + qim * dfim)
        daim_ref[...] = step * dzim - (qre * dfim - qim * dfre)
        dls = (are * dzre + aim * dzim) * step
        sel = (lax.broadcasted_iota(jnp.int32, (S5_N, LANES), 0) // 64 == lax.broadcasted_iota(jnp.int32, (S5_N, LANES), 1)).astype(f32)
        dls_ref[...] = lax.dot_general(sel, jnp.broadcast_to(dls, (S5_N, LANES)), (((0,), (0,)), ((), ())),
                                       preferred_element_type=f32, precision=lax.Precision.HIGHEST)

    col = jax.ShapeDtypeStruct((S5_N, 1), f32)
    mat = jax.ShapeDtypeStruct((S5_N, GROUP_W), f32)
    return pl.pallas_call(body, name="s5_prep_bwd", out_shape=[col, col, jax.ShapeDtypeStruct((LANES, LANES), f32), mat, mat],
                          in_specs=[VMEM] * 9, out_specs=[VMEM] * 5, compiler_params=_params(),
                          )(are, aim, ls, bre_bd, bim_bd, dlre, dlim, dbbr, dbbi)


def _cmul(ar, ai, br, bi):
    return ar * br - ai * bi, ar * bi + ai * br


def _s5_scan(re_ref, im_ref, carry_ref, mr, mi, n_groups, reverse):
    p1 = (mr, mi)
    p2 = _cmul(*p1, *p1)
    p3 = _cmul(*p2, *p1)
    p4 = _cmul(*p2, *p2)
    p5 = _cmul(*p4, *p1)
    p6 = _cmul(*p4, *p2)
    p7 = _cmul(*p4, *p3)
    p8 = _cmul(*p4, *p4)
    pows = [p1, p2, p3, p4, p5, p6, p7, p8]
    row = lax.broadcasted_iota(jnp.int32, (8, S5_N), 0)
    tr = jnp.zeros((8, S5_N), f32)
    ti = jnp.zeros((8, S5_N), f32)
    for i in range(8):
        p = pows[7 - i] if reverse else pows[i]
        tr = jnp.where(row == i, p[0], tr)
        ti = jnp.where(row == i, p[1], ti)
    steps = []
    for k, p in ((1, p1), (2, p2), (4, p4)):
        keep = (row + k < 8) if reverse else (row >= k)
        steps.append((8 - k if reverse else k, jnp.where(keep, p[0], 0.0), jnp.where(keep, p[1], 0.0)))
    edge = 0 if reverse else 7

    def step(j, carry):
        cr, ci = carry
        g = (n_groups - 1 - j) if reverse else j
        r0 = pl.multiple_of(g * 8, 8)
        xr = re_ref[pl.ds(r0, 8), :]
        xi = im_ref[pl.ds(r0, 8), :]
        for shift, br, bi in steps:
            sr = pltpu.roll(xr, shift, 0)
            si = pltpu.roll(xi, shift, 0)
            xr, xi = xr + br * sr - bi * si, xi + br * si + bi * sr
        xr, xi = xr + tr * cr - ti * ci, xi + tr * ci + ti * cr
        re_ref[pl.ds(r0, 8), :] = xr
        im_ref[pl.ds(r0, 8), :] = xi
        return (jnp.broadcast_to(xr[edge:edge + 1, :], (8, S5_N)), jnp.broadcast_to(xi[edge:edge + 1, :], (8, S5_N)))

    cr, ci = lax.fori_loop(0, n_groups, step, (carry_ref[0], carry_ref[1]))
    carry_ref[0] = cr
    carry_ref[1] = ci


def _s5_output(u, xr, xi, ctr, cti, d):
    return _dot_nt(_b(xr), _b(ctr)) - _dot_nt(_b(xi), _b(cti)) + d * u


def _s5_fwd(proj, bbr, bbi, ctr, cti, lre, lim, d, glu_w, glu_b, tb, xchg=None):
    t = proj.shape[0]

    def body(u_ref, bbr_ref, bbi_ref, ctr_ref, cti_ref, lr_ref, li_ref, d_ref, gw_ref, gb_ref, o_ref, xr_ref, xi_ref, carry):
        @pl.when(pl.program_id(0) == 0)
        def _():
            carry[...] = jnp.zeros_like(carry)

        u = u_ref[...]
        ub = _b(u)
        xr_ref[...] = _dot_nt(ub, _b(bbr_ref[...]))
        xi_ref[...] = _dot_nt(ub, _b(bbi_ref[...]))
        _s5_scan(xr_ref, xi_ref, carry, lr_ref[...], li_ref[...], tb // 8, reverse=False)
        y = _s5_output(u, xr_ref[...], xi_ref[...], ctr_ref[...], cti_ref[...], d_ref[...])
        gl = _gelu(y)
        o_ref[...] = gl * jax.nn.sigmoid(_dot(_b(gl), _b(gw_ref[...])) + gb_ref[...])

    state = pl.BlockSpec((tb, S5_N), lambda i: (i, 0))
    return _call(
        body, name="s5_fwd", grid=(t // tb,),
        out_shape=[jax.ShapeDtypeStruct((t, GROUP_W), f32), jax.ShapeDtypeStruct((t, S5_N), f32), jax.ShapeDtypeStruct((t, S5_N), f32)],
        in_specs=[pl.BlockSpec((tb, GROUP_W), lambda i: (i, C_S5)), _full((S5_N, GROUP_W)), _full((S5_N, GROUP_W)),
                  _full((GROUP_W, S5_N)), _full((GROUP_W, S5_N)), _full((1, S5_N)), _full((1, S5_N)),
                  _full((1, GROUP_W)), _full((GROUP_W, GROUP_W)), _full((1, GROUP_W))],
        out_specs=[pl.BlockSpec((tb, GROUP_W), lambda i: (i, 0)), state, state],
        scratch_shapes=[pltpu.VMEM((2, 8, S5_N), f32)],
        semantics=("arbitrary",), xchg=xchg, args=(proj, bbr, bbi, ctr, cti, lre, lim, d, glu_w, glu_b))


def _s5_bwd(proj, dyd, xr_all, xi_all, bbr, bbi, ctr, cti, lre, lim, d, glu_w, glu_b, tb, xchg=None):
    t = proj.shape[0]
    nt = t // tb

    def body(u_ref, dy_ref, xr_ref, xi_ref, xrh_ref, xih_ref, bbr_ref, bbi_ref, ctr_ref, cti_ref, lr_ref, li_ref,
             d_ref, gw_ref, gb_ref,
             du_ref, dlr_ref, dli_ref, dbbr_ref, dbbi_ref, dctr_ref, dcti_ref, dd_ref, dgw_ref, dgb_ref,
             gr_ref, gi_ref, carry):
        i = pl.program_id(0)
        ti = nt - 1 - i

        @pl.when(i == 0)
        def _():
            carry[...] = jnp.zeros_like(carry)

        u = u_ref[...]
        ub = _b(u)
        xr, xi = xr_ref[...], xi_ref[...]
        ctr, cti = _b(ctr_ref[...]), _b(cti_ref[...])
        d = d_ref[...]
        gw = _b(gw_ref[...])
        y = _s5_output(u, xr, xi, ctr, cti, d)
        gl = _gelu(y)
        sg = jax.nn.sigmoid(_dot(_b(gl), gw) + gb_ref[...])
        dout = dy_ref[...]
        q = dout * gl * sg * (1.0 - sg)
        qb = _b(q)
        dgl = dout * sg + _dot_nt(qb, gw)
        _acc(dgw_ref, _dot_tn(_b(gl), qb))
        _acc(dgb_ref, _colsum(q))
        dyv = dgl * _dgelu(y)
        _acc(dd_ref, _colsum(dyv * u))
        dyb = _b(dyv)
        gr_ref[...] = _dot(dyb, ctr)
        gi_ref[...] = -_dot(dyb, cti)
        _acc(dctr_ref, _dot_tn(dyb, _b(xr)))
        _acc(dcti_ref, -_dot_tn(dyb, _b(xi)))
        _s5_scan(gr_ref, gi_ref, carry, lr_ref[...], -li_ref[...], tb // 8, reverse=True)
        gr, gi = gr_ref[...], gi_ref[...]
        xpr = jnp.concatenate([jnp.where(ti > 0, xrh_ref[...], 0.0), xr], axis=0)[7:7 + tb]
        xpi = jnp.concatenate([jnp.where(ti > 0, xih_ref[...], 0.0), xi], axis=0)[7:7 + tb]
        _acc(dlr_ref, _colsum(gr * xpr + gi * xpi))
        _acc(dli_ref, _colsum(gi * xpr - gr * xpi))
        grb, gib = _b(gr), _b(gi)
        _acc(dbbr_ref, _dot_tn(grb, ub))
        _acc(dbbi_ref, _dot_tn(gib, ub))
        du_ref[...] = _b(dyv * d + _dot(grb, _b(bbr_ref[...])) + _dot(gib, _b(bbi_ref[...])))

    state = pl.BlockSpec((tb, S5_N), lambda i: (nt - 1 - i, 0))
    prev = pl.BlockSpec((8, S5_N), lambda i: (jnp.maximum((nt - 1 - i) * (tb // 8) - 1, 0), 0))
    tile = pl.BlockSpec((tb, GROUP_W), lambda i: (nt - 1 - i, 0))
    return _call(
        body, name="s5_bwd", grid=(nt,),
        out_shape=[jax.ShapeDtypeStruct((t, GROUP_W), bf16), jax.ShapeDtypeStruct((1, S5_N), f32), jax.ShapeDtypeStruct((1, S5_N), f32),
                   jax.ShapeDtypeStruct((S5_N, GROUP_W), f32), jax.ShapeDtypeStruct((S5_N, GROUP_W), f32),
                   jax.ShapeDtypeStruct((GROUP_W, S5_N), f32), jax.ShapeDtypeStruct((GROUP_W, S5_N), f32),
                   jax.ShapeDtypeStruct((1, GROUP_W), f32), jax.ShapeDtypeStruct((GROUP_W, GROUP_W), f32),
                   jax.ShapeDtypeStruct((1, GROUP_W), f32)],
        in_specs=[pl.BlockSpec((tb, GROUP_W), lambda i: (nt - 1 - i, C_S5)), tile, state, state, prev, prev,
                  _full((S5_N, GROUP_W)), _full((S5_N, GROUP_W)), _full((GROUP_W, S5_N)), _full((GROUP_W, S5_N)),
                  _full((1, S5_N)), _full((1, S5_N)), _full((1, GROUP_W)), _full((GROUP_W, GROUP_W)), _full((1, GROUP_W))],
        out_specs=[tile, _full((1, S5_N)), _full((1, S5_N)), _full((S5_N, GROUP_W)), _full((S5_N, GROUP_W)),
                   _full((GROUP_W, S5_N)), _full((GROUP_W, S5_N)), _full((1, GROUP_W)), _full((GROUP_W, GROUP_W)), _full((1, GROUP_W))],
        scratch_shapes=[pltpu.VMEM((tb, S5_N), f32), pltpu.VMEM((tb, S5_N), f32), pltpu.VMEM((2, 8, S5_N), f32)],
        semantics=("arbitrary",), xchg=xchg,
        args=(proj, dyd, xr_all, xi_all, xr_all, xi_all, bbr, bbi, ctr, cti, lre, lim, d, glu_w, glu_b))


def _outproj_fwd(ys, h, bn_w, g1, w_out, tb):
    t = h.shape[0]

    def body(ya_ref, yb_ref, yc_ref, yd_ref, h_ref, bn_ref, g1_ref, w_ref, h1_ref, o_ref, gr_ref):
        bn = bn_ref[...]
        parts = []
        for g, r in enumerate((ya_ref, yb_ref, yc_ref, yd_ref)):
            n, _ = _rms(r[...])
            parts.append(n * bn[:, g * GROUP_W:(g + 1) * GROUP_W])
        groups = _b(jnp.concatenate(parts, axis=1))
        gr_ref[...] = groups
        o = _dot(groups, w_ref[...])
        o_ref[...] = _b(o)
        h1_ref[...] = h_ref[...] + g1_ref[...] * o

    grp = pl.BlockSpec((tb, GROUP_W), lambda i: (i, 0))
    row = pl.BlockSpec((tb, D_MODEL), lambda i: (i, 0))
    vec = _full((1, D_MODEL))
    return pl.pallas_call(
        body, name="outproj_fwd", grid=(t // tb,),
        out_shape=[jax.ShapeDtypeStruct((t, D_MODEL), f32), jax.ShapeDtypeStruct((t, D_MODEL), bf16),
                   jax.ShapeDtypeStruct((t, D_MODEL), bf16)],
        in_specs=[grp, grp, grp, grp, row, vec, vec, _full((D_MODEL, D_MODEL))],
        out_specs=[row, row, row],
        compiler_params=_params(("parallel",)),
    )(*ys, h, bn_w, g1, w_out)


def _outproj_bwd(dh1, o, ys, bn_w, g1, w_out, tb):
    t = dh1.shape[0]

    def body(dh_ref, o_ref, ya_ref, yb_ref, yc_ref, yd_ref, bn_ref, g1_ref, w_ref,
             da_ref, db_ref, dc_ref, dd_ref, do_ref, dg1_ref, dbn_ref):
        dh = dh_ref[...]
        _acc(dg1_ref, _colsum(dh * o_ref[...].astype(f32)))
        do = _b(dh * g1_ref[...])
        do_ref[...] = do
        dgroups = _dot_nt(do, w_ref[...])
        bn = bn_ref[...]
        dbn = []
        for g, (r, dr) in enumerate(((ya_ref, da_ref), (yb_ref, db_ref), (yc_ref, dc_ref), (yd_ref, dd_ref))):
            n, rr = _rms(r[...])
            dgr = dgroups[:, g * GROUP_W:(g + 1) * GROUP_W]
            dbn.append(_colsum(dgr * n))
            dr[...] = _rms_bwd(dgr * bn[:, g * GROUP_W:(g + 1) * GROUP_W], n, rr)
        _acc(dbn_ref, jnp.concatenate(dbn, axis=1))

    grp = pl.BlockSpec((tb, GROUP_W), lambda i: (i, 0))
    row = pl.BlockSpec((tb, D_MODEL), lambda i: (i, 0))
    vec = _full((1, D_MODEL))
    return pl.pallas_call(
        body, name="outproj_bwd", grid=(t // tb,),
        out_shape=[jax.ShapeDtypeStruct((t, GROUP_W), f32)] * 4 + [jax.ShapeDtypeStruct((t, D_MODEL), bf16),
                   jax.ShapeDtypeStruct((1, D_MODEL), f32), jax.ShapeDtypeStruct((1, D_MODEL), f32)],
        in_specs=[row, row, grp, grp, grp, grp, vec, vec, _full((D_MODEL, D_MODEL))],
        out_specs=[grp, grp, grp, grp, row, vec, vec],
        compiler_params=_params(("arbitrary",)),
    )(dh1, o, *ys, bn_w, g1, w_out)


def _mlp_fwd(h1, norm_w, sc, sh, g2, w1, w2, tb, xchg=None):
    t = h1.shape[0]
    nh = w1.shape[0] // MLP_SLABS

    def body(h_ref, nw_ref, sc_ref, sh_ref, g2_ref, w1_ref, w2_ref, h2_ref, m_ref, v_ref, r_ref, acc):
        j = pl.program_id(1)

        @pl.when(j == 0)
        def _():
            n, _ = _rms(h_ref[...])
            v_ref[...] = _b(n * nw_ref[...] * (1.0 + sc_ref[...]) + sh_ref[...])

        v = v_ref[...]
        p = None
        for s in range(MLP_SLABS):
            ra = jnp.maximum(_dot(v, w1_ref[s]), 0.0)
            r = _b(ra * ra)
            r_ref[:, s * MLP_HB:(s + 1) * MLP_HB] = r
            q = _dot(r, w2_ref[s])
            p = q if p is None else p + q

        @pl.when(j == 0)
        def _():
            acc[...] = p

        @pl.when(j != 0)
        def _():
            acc[...] += p

        @pl.when(j == nh - 1)
        def _():
            m = acc[...]
            m_ref[...] = _b(m)
            h2_ref[...] = h_ref[...] + g2_ref[...] * m

    row = pl.BlockSpec((tb, D_MODEL), lambda i, j: (i, 0))
    hid = pl.BlockSpec((tb, MLP_SLABS * MLP_HB), lambda i, j: (i, j))
    vec = _full((1, D_MODEL))
    return _call(
        body, name="mlp_fwd", grid=(t // tb, nh),
        out_shape=[jax.ShapeDtypeStruct((t, D_MODEL), f32), jax.ShapeDtypeStruct((t, D_MODEL), bf16),
                   jax.ShapeDtypeStruct((t, D_MODEL), bf16), jax.ShapeDtypeStruct((t, N_DEV * MLP_HB), bf16)],
        in_specs=[row, vec, vec, vec, vec, pl.BlockSpec((MLP_SLABS, D_MODEL, MLP_HB), lambda i, j: (j, 0, 0)),
                  pl.BlockSpec((MLP_SLABS, MLP_HB, D_MODEL), lambda i, j: (j, 0, 0))],
        out_specs=[row, row, row, hid],
        scratch_shapes=[pltpu.VMEM((tb, D_MODEL), f32)],
        semantics=("arbitrary", "arbitrary"), xchg=xchg, args=(h1, norm_w, sc, sh, g2, w1, w2))


def _mlp_bwd(dh2, m, h1, r, norm_w, sc, sh, g2, w1, w2, tb, xchg=None):
    t = h1.shape[0]
    slabs = MLP_BWD_SLABS
    nh = w1.shape[0] // slabs

    def body(dh_ref, m_ref, h_ref, r_ref, nw_ref, sc_ref, sh_ref, g2_ref, w1_ref, w2_ref,
             dh1_ref, do_ref, da_ref, dg2_ref, dsh_ref, dsc_ref, dnw_ref, acc):
        j = pl.program_id(1)

        @pl.when(j == 0)
        def _():
            dh = dh_ref[...]
            _acc(dg2_ref, _colsum(dh * m_ref[...].astype(f32)))
            do_ref[...] = _b(dh * g2_ref[...])

        do = do_ref[...]
        p = None
        for s in range(slabs):
            cols = slice(s * MLP_HB, (s + 1) * MLP_HB)
            dr = _dot_nt(do, w2_ref[s])
            da = _b(dr * 2.0 * jnp.sqrt(r_ref[:, cols].astype(f32)))
            da_ref[:, cols] = da
            q = _dot_nt(da, w1_ref[s])
            p = q if p is None else p + q

        @pl.when(j == 0)
        def _():
            acc[...] = p

        @pl.when(j != 0)
        def _():
            acc[...] += p

        @pl.when(j == nh - 1)
        def _():
            dv = acc[...]
            n, r = _rms(h_ref[...])
            nw = nw_ref[...]
            gain = 1.0 + sc_ref[...]
            _acc(dsh_ref, _colsum(dv))
            _acc(dsc_ref, _colsum(dv * n * nw))
            _acc(dnw_ref, _colsum(dv * gain * n))
            dh1_ref[...] = dh_ref[...] + _rms_bwd(dv * nw * gain, n, r)

    row = pl.BlockSpec((tb, D_MODEL), lambda i, j: (i, 0))
    row_in = pl.BlockSpec((tb, D_MODEL), lambda i, j: (i, 0), pipeline_mode=pl.Buffered(1))
    hid = pl.BlockSpec((tb, slabs * MLP_HB), lambda i, j: (i, j))
    vec = _full((1, D_MODEL))
    return _call(
        body, name="mlp_bwd", grid=(t // tb, nh),
        out_shape=[jax.ShapeDtypeStruct((t, D_MODEL), f32), jax.ShapeDtypeStruct((t, D_MODEL), bf16),
                   jax.ShapeDtypeStruct((t, N_DEV * MLP_HB), bf16)] + [jax.ShapeDtypeStruct((1, D_MODEL), f32)] * 4,
        in_specs=[row_in, row_in, row_in, hid, vec, vec, vec, vec,
                  pl.BlockSpec((slabs, D_MODEL, MLP_HB), lambda i, j: (j, 0, 0)),
                  pl.BlockSpec((slabs, MLP_HB, D_MODEL), lambda i, j: (j, 0, 0))],
        out_specs=[row, row, hid, vec, vec, vec, vec],
        scratch_shapes=[pltpu.VMEM((tb, D_MODEL), f32)],
        semantics=("arbitrary", "arbitrary"), xchg=xchg, args=(dh2, m, h1, r, norm_w, sc, sh, g2, w1, w2))


def _loss_head(h, target, norm_w, tb):
    t = h.shape[0]

    def body(h_ref, t_ref, w_ref, loss_ref, dh_ref, dw_ref):
        n, r = _rms(h_ref[...])
        w = w_ref[...]
        err = n * w - t_ref[...]
        part = 0.5 * jnp.sum(jnp.sum(err * err, axis=1, keepdims=True), axis=0, keepdims=True) / D_MODEL
        _acc(loss_ref, jnp.broadcast_to(part, (8, LANES)))
        dy = err / D_MODEL
        _acc(dw_ref, _colsum(dy * n))
        dh_ref[...] = _rms_bwd(dy * w, n, r)

    row = pl.BlockSpec((tb, D_MODEL), lambda i: (i, 0))
    return pl.pallas_call(
        body, name="loss_head", grid=(t // tb,),
        out_shape=[jax.ShapeDtypeStruct((8, LANES), f32), jax.ShapeDtypeStruct((t, D_MODEL), f32),
                   jax.ShapeDtypeStruct((1, D_MODEL), f32)],
        in_specs=[row, row, _full((1, D_MODEL))],
        out_specs=[_full((8, LANES)), row, _full((1, D_MODEL))],
        compiler_params=_params(("arbitrary",)),
    )(h, target, norm_w)


def _adam_math(w, g, m, v):
    m2 = ADAM_B1 * m + (1.0 - ADAM_B1) * g
    v2 = ADAM_B2 * v + (1.0 - ADAM_B2) * (g * g)
    mh = m2 / (1.0 - ADAM_B1 ** ADAM_STEP)
    vh = v2 / (1.0 - ADAM_B2 ** ADAM_STEP)
    return -ADAM_LR * (mh / (jnp.sqrt(vh) + ADAM_EPS) + ADAM_WD * w), m2, v2


def _adamw_small(ws, gs, ms, vs):
    n = len(ws)
    shapes = [w.shape for w in ws]
    as2d = [(1,) + s if len(s) == 1 else s for s in shapes]
    flat = [x.reshape(s) for group in (ws, gs, ms, vs) for x, s in zip(group, as2d)]

    def body(*refs):
        w_refs, g_refs, m_refs, v_refs, outs = refs[:n], refs[n:2 * n], refs[2 * n:3 * n], refs[3 * n:4 * n], refs[4 * n:]
        for i in range(n):
            d, m2, v2 = _adam_math(w_refs[i][...], g_refs[i][...], m_refs[i][...], v_refs[i][...])
            outs[3 * i][...] = d
            outs[3 * i + 1][...] = m2
            outs[3 * i + 2][...] = v2

    res = pl.pallas_call(body, name="adamw_small", out_shape=[jax.ShapeDtypeStruct(s, f32) for s in as2d for _ in range(3)],
                         in_specs=[VMEM] * (4 * n), out_specs=[VMEM] * (3 * n), compiler_params=_params())(*flat)
    return [r.reshape(shapes[i // 3]) for i, r in enumerate(res)]


def _sum_adamw_layers(parts0, parts1, w, m, v, name, rb):
    n_src, r, c = parts0.shape
    nb = r // rb

    def body(p0_ref, p1_ref, w_ref, m_ref, v_ref, g_ref, d_ref, m2_ref, v2_ref):
        def update(p_ref):
            g = p_ref[0].astype(f32)
            for s in range(1, n_src):
                g = g + p_ref[s].astype(f32)
            g_ref[0] = g
            d, m2, v2 = _adam_math(w_ref[0], g, m_ref[0], v_ref[0])
            d_ref[0] = d
            m2_ref[0] = m2
            v2_ref[0] = v2

        @pl.when(pl.program_id(0) == 0)
        def _():
            update(p0_ref)

        @pl.when(pl.program_id(0) == 1)
        def _():
            update(p1_ref)

    blk = pl.BlockSpec((1, rb, c), lambda l, i: (l, i, 0))
    return pl.pallas_call(
        body, name=name, grid=(2, nb),
        out_shape=[jax.ShapeDtypeStruct((2, r, c), f32)] * 4,
        in_specs=[pl.BlockSpec((n_src, rb, c), lambda l, i: (0, jnp.where(l == 0, i, nb - 1), 0)),
                  pl.BlockSpec((n_src, rb, c), lambda l, i: (0, jnp.where(l == 1, i, 0), 0)), blk, blk, blk],
        out_specs=[blk] * 4,
        compiler_params=_params(("arbitrary", "arbitrary")),
    )(parts0, parts1, w, m, v)


def _reorder_in(w):
    pad = jnp.zeros(w.shape[:-1] + (P_IN - 2308,), w.dtype)
    return jnp.concatenate([w[..., :2048], w[..., 2052:2308], w[..., 2048:2052], pad], axis=-1)


def _unreorder_in(w):
    return jnp.concatenate([w[..., :2048], w[..., 2304:2308], w[..., 2048:2304]], axis=-1)


def _block_diag(w2d, n_blocks):
    rows, cols = w2d.shape
    tiled = jnp.tile(w2d, (1, n_blocks))
    rb = lax.broadcasted_iota(jnp.int32, tiled.shape, 0) // (rows // n_blocks)
    cb = lax.broadcasted_iota(jnp.int32, tiled.shape, 1) // cols
    return jnp.where(rb == cb, tiled, jnp.zeros_like(tiled))


def _block_diag_extract(w_bd, n_blocks):
    rows, wide = w_bd.shape
    r, c = rows // n_blocks, wide // n_blocks
    w4 = w_bd.reshape(n_blocks, r, n_blocks, c)
    idx = jnp.arange(n_blocks)
    return w4[idx, :, idx, :]


def _lanes128(v):
    return jnp.pad(v.reshape(1, -1), ((0, 0), (0, LANES - v.size)))


def _rows_of(shape):
    n = 1
    for d in shape:
        n *= d
    return -(-n // (8 * LANES)) * 8, n


def _flat_pack(arrs, row_multiple=8):
    blocks = []
    for a in arrs:
        rows, n = _rows_of(a.shape)
        blocks.append(jnp.pad(a.reshape(-1), (0, rows * LANES - n)).reshape(rows, LANES))
    total = sum(b.shape[0] for b in blocks)
    pad = -total % row_multiple
    if pad:
        blocks.append(jnp.zeros((pad, LANES), blocks[0].dtype))
    return jnp.concatenate(blocks, axis=0)


def _flat_unpack(packed, shapes):
    out, off = [], 0
    for s in shapes:
        rows, n = _rows_of(s)
        out.append(packed[off:off + rows].reshape(-1)[:n].reshape(s))
        off += rows
    return out


_W_NAMES = ['norm_mix_w', 'norm_mlp_w', 'ada_w', 'ada_b', 'w_in', 'pool_w', 'pool_scale', 'sconv_w', 'ssd_conv_w',
            'ssd_conv_b', 'ssd_dt_bias', 'ssd_a_log', 'ssd_d', 's5_a_re', 's5_a_im', 's5_log_step', 's5_b_re', 's5_b_im',
            's5_c_re', 's5_c_im', 's5_d', 's5_glu_w', 's5_glu_b', 'branch_norm_w', 'w_out', 'mlp_w1', 'mlp_w2',
            'final_norm_w']
_BIG = ('ada_w', 'w_in', 'w_out', 'mlp_w1', 'mlp_w2')
_SMALL = [n for n in _W_NAMES if n not in _BIG]
_SHARDED_SMALL = {'sconv_w': (2, 32), 'ssd_conv_w': (2, 96), 's5_glu_w': (1, 32)}


def _gather(*blocks):
    return _ChipGather(blocks)


def _scatter(*parts):
    return _Exchange(parts, gather=False)


def _layer_forward(l, h, p, w, sh_b, tb):
    first = l == 0
    (proj, u_b), got = _inproj_fwd(h, p['norm_mix_w'][l], p['sc1'][l], p['sh1'][l], w['w_in', l], tb,
                                   xchg=_gather(sh_b[1][0]) if first else None)
    if first:
        w['w_out', 0] = got[0].reshape(D_MODEL, D_MODEL)
    ya = _pool_fwd(proj, p['pool_bd'][l], p['pool_scale'][l], tb)
    yb = _sconv_fwd(proj, p['sconv_w8'][l], tb)
    (yc, yc_pre, states), got = _ssd_fwd(proj, p['ssd_conv_w8'][l], p['ssd_conv_b'][l], p['ssd_dt_bias'][l], p['ssd_a_log'][l],
                                         p['ssd_d_cols'][l], tb, xchg=_gather(sh_b[2][0]) if first else None)
    if first:
        w['w1', 0] = got[0]
    (yd, xr, xi), got = _s5_fwd(proj, p['bbr'][l], p['bbi'][l], p['ctr'][l], p['cti'][l], p['lre'][l], p['lim'][l],
                                p['s5_d'][l], p['glu_w'][l], p['glu_b'][l], tb, xchg=_gather(sh_b[3][0]) if first else None)
    if first:
        w['w2', 0] = got[0]
    ys = (ya, yb, yc, yd)
    h1, o, groups_b = _outproj_fwd(ys, h, p['branch_norm_w'][l], p['g1'][l], w['w_out', l], tb)
    (h2, m, v_b, r_b), got = _mlp_fwd(h1, p['norm_mlp_w'][l], p['sc2'][l], p['sh2'][l], p['g2'][l], w['w1', l], w['w2', l],
                                      min(MLP_TB, h.shape[0]), xchg=_gather(*[sh_b[k][1] for k in range(4)]) if first else None)
    if first:
        w['w_in', 1] = got[0].reshape(D_MODEL, P_IN)
        w['w_out', 1] = got[1].reshape(D_MODEL, D_MODEL)
        w['w1', 1], w['w2', 1] = got[2], got[3]
    saved = dict(h=h, proj=proj, u_b=u_b, ys=ys, yc_pre=yc_pre, states=states, xr=xr, xi=xi, h1=h1, o=o,
                 groups_b=groups_b, m=m, v_b=v_b, r_b=r_b)
    return h2, saved


def _layer_backward(l, dh2, s, p, w, pending, recv, tb):
    def carry(names):
        names = [n for n in names if n in pending]
        return names, (_scatter(*[pending.pop(n) for n in names]) if names else None)

    def landed(names, got):
        for n, g in zip(names, got):
            recv[n] = g

    names, xchg = carry([('w_out', 1)])
    (dh1, do2_b, da_b, dg2, dsh2, dsc2, dnw_mlp), got = _mlp_bwd(dh2, s['m'], s['h1'], s['r_b'], p['norm_mlp_w'][l], p['sc2'][l],
                                                                p['sh2'][l], p['g2'][l], w['w1', l], w['w2', l], min(MLP_TB, tb),
                                                                xchg=xchg)
    landed(names, got)
    pending['mlp_w2', l] = _wgrad(s['r_b'], do2_b, 1, "wgrad_w2", tm=1024, tk=2048).reshape(N_DEV, MLP_HB, D_MODEL)
    pending['mlp_w1', l] = _wgrad(s['v_b'], da_b, N_DEV, "wgrad_w1", tm=1024, tk=2048)
    dya, dyb, dyc, dyd, do1_b, dg1, dbn = _outproj_bwd(dh1, s['o'], s['ys'], p['branch_norm_w'][l], p['g1'][l], w['w_out', l], tb)
    pending['w_out', l] = _wgrad(s['groups_b'], do1_b, 1, "wgrad_wout", tm=1024, tk=1024).reshape(N_DEV, D_MODEL // N_DEV, D_MODEL)
    proj = s['proj']
    dv, dpool_bd, dpool_scale = _pool_bwd(proj, dya, p['pool_bd'][l], p['pool_scale'][l], tb)
    dgb, dgc, dhh, dsconv = _sconv_bwd(proj, dyb, p['sconv_w8'][l], tb)
    names, xchg = carry([('mlp_w1', l)] + ([('w_out', 0)] if l == 0 else []))
    (dz, dxs, dbm, dcm, ddt, dconv_w, dconv_b, ddtb, dalog, ddskip), got = _ssd_bwd(
        proj, dyc, s['yc_pre'], s['states'], p['ssd_conv_w8'][l], p['ssd_conv_b'][l], p['ssd_dt_bias'][l], p['ssd_a_log'][l],
        p['ssd_d_cols'][l], min(TB_BWD, tb), xchg=xchg)
    landed(names, got)
    names, xchg = carry([('mlp_w2', l)])
    (du5, dlr, dli, dbbr, dbbi, dctr, dcti, dd5, dgw, dgb5), got = _s5_bwd(
        proj, dyd, s['xr'], s['xi'], p['bbr'][l], p['bbi'][l], p['ctr'][l], p['cti'][l], p['lre'][l], p['lim'][l],
        p['s5_d'][l], p['glu_w'][l], p['glu_b'][l], min(TB_BWD, tb), xchg=xchg)
    landed(names, got)
    dare, daim, dls, dbre_bd, dbim_bd = _s5_prep_bwd(p['are_c'][l], p['aim_c'][l], p['ls_c'][l], p['bre_bd'][l], p['bim_bd'][l],
                                                     dlr.reshape(S5_N, 1), dli.reshape(S5_N, 1), dbbr, dbbi)
    dparts = (dv, dgb, dgc, dhh, dz, dxs, dbm, dcm, du5, ddt)
    pending['w_in', l] = _wgrad_parts(s['u_b'], dparts, "wgrad_win", tm=1024, tk=1024).reshape(N_DEV, D_MODEL // N_DEV, P_IN)
    names, xchg = carry([('w_in', l)])
    (dh, dsh1, dsc1, dnw_mix), got = _inproj_bwd(dparts, dh1, s['h'], p['norm_mix_w'][l], p['sc1'][l], p['sh1'][l], w['w_in', l],
                                                 min(TB_BWD, tb), xchg=xchg)
    landed(names, got)
    small = {
        'norm_mix_w': dnw_mix.reshape(D_MODEL), 'norm_mlp_w': dnw_mlp.reshape(D_MODEL),
        'ada_b': jnp.concatenate([dsh1, dsc1, dg1, dsh2, dsc2, dg2], axis=1).reshape(6 * D_MODEL),
        'pool_w': _block_diag_extract(dpool_bd, 4), 'pool_scale': dpool_scale.reshape(GROUP_W),
        'sconv_w': dsconv[0:3], 'ssd_conv_w': dconv_w[0:4], 'ssd_conv_b': dconv_b.reshape(768),
        'ssd_dt_bias': ddtb[0, 0:4], 'ssd_a_log': dalog[0, 0:4], 'ssd_d': ddskip[0, 0:4],
        's5_a_re': dare.reshape(16, 64), 's5_a_im': daim.reshape(16, 64), 's5_log_step': dls[0:16, 0],
        's5_b_re': _block_diag_extract(dbre_bd, 16), 's5_b_im': _block_diag_extract(dbim_bd, 16),
        's5_c_re': _block_diag_extract(dctr, 16), 's5_c_im': _block_diag_extract(dcti, 16),
        's5_d': dd5.reshape(GROUP_W), 's5_glu_w': dgw, 's5_glu_b': dgb5.reshape(GROUP_W),
        'branch_norm_w': dbn.reshape(D_MODEL),
    }
    return dh, small


def _prepare_params(a, me):
    pack_shapes = [(1, D_MODEL), (2, 3, 32), (2, 4, 96), (2, 32, GROUP_W)]
    packed = _flat_pack([a['c'], a['sconv_w'], a['ssd_conv_w'], a['s5_glu_w']])
    gathered = _gather_rows(packed, "gather_small")
    pieces = [_flat_unpack(gathered[d], pack_shapes) for d in range(N_DEV)]
    c_all = jnp.concatenate([pc[0] for pc in pieces], axis=0)
    sconv_full = jnp.concatenate([pc[1] for pc in pieces], axis=2)
    ssd_conv_full = jnp.concatenate([pc[2] for pc in pieces], axis=2)
    glu_full = jnp.concatenate([pc[3] for pc in pieces], axis=1)

    ada_b_cols = lax.dynamic_slice_in_dim(a['ada_b'], me * 768, 768, axis=1).reshape(2, 1, 768)
    cond, modrows = _ada_forward(c_all, a['ada_w'], ada_b_cols)
    mod_recv = _all_to_all_rows(modrows.transpose(1, 0, 2), "exchange_mod")
    mod = mod_recv.transpose(1, 0, 2).reshape(2, 6 * D_MODEL)
    p = {'cond': cond}
    for k, name in enumerate(('sh1', 'sc1', 'g1', 'sh2', 'sc2', 'g2')):
        p[name] = mod[:, k * D_MODEL:(k + 1) * D_MODEL].reshape(2, 1, D_MODEL)

    for name in ('norm_mix_w', 'norm_mlp_w', 'branch_norm_w'):
        p[name] = a[name].reshape(2, 1, D_MODEL)
    p['pool_bd'] = jnp.stack([_block_diag(a['pool_w'][l].reshape(GROUP_W, 64), 4) for l in range(2)])
    p['pool_scale'] = a['pool_scale'].reshape(2, 1, GROUP_W)
    p['sconv_w8'] = jnp.pad(sconv_full, ((0, 0), (0, 5), (0, 0)))
    p['ssd_conv_w8'] = jnp.pad(ssd_conv_full, ((0, 0), (0, 4), (0, 0)))
    p['ssd_conv_b'] = a['ssd_conv_b'].reshape(2, 1, 768)
    p['ssd_dt_bias'] = jnp.pad(a['ssd_dt_bias'], ((0, 0), (0, LANES - 4))).reshape(2, 1, LANES)
    p['ssd_a_log'] = jnp.pad(a['ssd_a_log'], ((0, 0), (0, LANES - 4))).reshape(2, 1, LANES)
    p['ssd_d_cols'] = jnp.repeat(a['ssd_d'], SSD_P, axis=1).reshape(2, 1, GROUP_W)
    p['are_c'] = a['s5_a_re'].reshape(2, S5_N, 1)
    p['aim_c'] = a['s5_a_im'].reshape(2, S5_N, 1)
    p['ls_c'] = jnp.repeat(a['s5_log_step'], 64, axis=1).reshape(2, S5_N, 1)
    p['bre_bd'] = jnp.stack([_block_diag(a['s5_b_re'][l].reshape(S5_N, 16), 16) for l in range(2)])
    p['bim_bd'] = jnp.stack([_block_diag(a['s5_b_im'][l].reshape(S5_N, 16), 16) for l in range(2)])
    p['ctr'] = jnp.stack([_block_diag(a['s5_c_re'][l].reshape(GROUP_W, 64), 16) for l in range(2)])
    p['cti'] = jnp.stack([_block_diag(a['s5_c_im'][l].reshape(GROUP_W, 64), 16) for l in range(2)])
    p['s5_d'] = a['s5_d'].reshape(2, 1, GROUP_W)
    p['glu_w'] = glu_full
    p['glu_b'] = a['s5_glu_b'].reshape(2, 1, GROUP_W)
    lre, lim, bbr, bbi = [], [], [], []
    for l in range(2):
        r = _s5_prep(p['are_c'][l], p['aim_c'][l], p['ls_c'][l], p['bre_bd'][l], p['bim_bd'][l])
        lre.append(r[0].reshape(1, S5_N))
        lim.append(r[1].reshape(1, S5_N))
        bbr.append(r[2])
        bbi.append(r[3])
    p['lre'], p['lim'], p['bbr'], p['bbi'] = lre, lim, bbr, bbi
    return p


def kernel(x, c, norm_mix_w, norm_mlp_w, ada_w, ada_b, w_in, pool_w, pool_scale, sconv_w, ssd_conv_w, ssd_conv_b, ssd_dt_bias, ssd_a_log, ssd_d, s5_a_re, s5_a_im, s5_log_step, s5_b_re, s5_b_im, s5_c_re, s5_c_im, s5_d, s5_glu_w, s5_glu_b, branch_norm_w, w_out, mlp_w1, mlp_w2, final_norm_w, loss_target, m_norm_mix_w, m_norm_mlp_w, m_ada_w, m_ada_b, m_w_in, m_pool_w, m_pool_scale, m_sconv_w, m_ssd_conv_w, m_ssd_conv_b, m_ssd_dt_bias, m_ssd_a_log, m_ssd_d, m_s5_a_re, m_s5_a_im, m_s5_log_step, m_s5_b_re, m_s5_b_im, m_s5_c_re, m_s5_c_im, m_s5_d, m_s5_glu_w, m_s5_glu_b, m_branch_norm_w, m_w_out, m_mlp_w1, m_mlp_w2, m_final_norm_w, v_norm_mix_w, v_norm_mlp_w, v_ada_w, v_ada_b, v_w_in, v_pool_w, v_pool_scale, v_sconv_w, v_ssd_conv_w, v_ssd_conv_b, v_ssd_dt_bias, v_ssd_a_log, v_ssd_d, v_s5_a_re, v_s5_a_im, v_s5_log_step, v_s5_b_re, v_s5_b_im, v_s5_c_re, v_s5_c_im, v_s5_d, v_s5_glu_w, v_s5_glu_b, v_branch_norm_w, v_w_out, v_mlp_w1, v_mlp_w2, v_final_norm_w):
    a = dict(locals())
    t = x.shape[1]
    tb = min(TB, t)
    me = _my_index()
    p = _prepare_params(a, me)

    sh_b = _cast_shards([_reorder_in(w_in), w_out, mlp_w1, mlp_w2])
    w = {('w_in', 0): _exchange_alone(_gather(sh_b[0][0]), "gather_w_in0")[0].reshape(D_MODEL, P_IN)}

    h = x.reshape(t, D_MODEL)
    saved = []
    for l in range(2):
        h, s = _layer_forward(l, h, p, w, sh_b, tb)
        saved.append(s)
    loss_blk, dh, dfinal = _loss_head(h, loss_target.reshape(t, D_MODEL), final_norm_w.reshape(1, D_MODEL), tb)

    pending, recv, small_parts = {}, {}, [None, None]
    for l in (1, 0):
        dh, small_parts[l] = _layer_backward(l, dh, saved[l], p, w, pending, recv, tb)
    grad_x = dh.reshape(1, t, D_MODEL)

    grads, deltas, new_m, new_v = {}, {}, {}, {}

    wmv_in = [_reorder_in(a[n]) for n in ('w_in', 'm_w_in', 'v_w_in')]
    outs = _sum_adamw_layers(recv['w_in', 0], recv['w_in', 1], *wmv_in, "adamw_w_in", 128)
    grads['w_in'], deltas['w_in'], new_m['w_in'], new_v['w_in'] = [_unreorder_in(o) for o in outs]
    for name, rb in (('w_out', 128), ('mlp_w1', 256), ('mlp_w2', 256)):
        grads[name], deltas[name], new_m[name], new_v[name] = _sum_adamw_layers(
            recv[name, 0], recv[name, 1], a[name], a['m_' + name], a['v_' + name], "adamw_" + name, rb)

    dmod = jnp.stack([small_parts[0]['ada_b'], small_parts[1]['ada_b']])
    dmod_recv = _all_to_all_rows(dmod.reshape(2, N_DEV, 768).transpose(1, 0, 2), "exchange_dmod")
    g_ada = _ada_backward(p['cond'], dmod_recv.transpose(1, 0, 2))
    grads['ada_w'], deltas['ada_w'], new_m['ada_w'], new_v['ada_w'] = _sum_adamw_layers(
        g_ada[0:1], g_ada[1:2], ada_w, m_ada_w, v_ada_w, "adamw_ada_w", 256)

    layered = [n for n in _SMALL if n != 'final_norm_w']
    full = [jnp.stack([small_parts[0][n], small_parts[1][n]]) for n in layered] + [dfinal.reshape(D_MODEL)]
    full.append(loss_blk[0:1, 0:1])
    full_shapes = [f.shape for f in full]
    summed = _flat_unpack(_allreduce_rows(_flat_pack(full, row_multiple=64)), full_shapes)
    loss = summed[-1].reshape(())
    local = []
    for n, g in zip(_SMALL, summed):
        if n in _SHARDED_SMALL:
            axis, size = _SHARDED_SMALL[n]
            g = lax.dynamic_slice_in_dim(g, me * size, size, axis=axis)
        local.append(g.reshape(a[n].shape))
    outs = _adamw_small([a[n] for n in _SMALL], local, [a['m_' + n] for n in _SMALL], [a['v_' + n] for n in _SMALL])
    for i, n in enumerate(_SMALL):
        grads[n], deltas[n], new_m[n], new_v[n] = local[i], outs[3 * i], outs[3 * i + 1], outs[3 * i + 2]

    return (loss, grad_x, *[grads[n] for n in _W_NAMES], *[deltas[n] for n in _W_NAMES],
            *[new_m[n] for n in _W_NAMES], *[new_v[n] for n in _W_NAMES])
```

```python
import functools

import jax
import jax.numpy as jnp
from jax import lax
from jax.experimental import pallas as pl
from jax.experimental.pallas import tpu as pltpu

f32 = jnp.float32
bf16 = jnp.bfloat16

N_DEV = 8
D_MODEL = 1024
GROUP_W = 256
P_IN = 2432
DT_COL = 2304
SSD_CHUNK = 128
SSD_HEADS = 4
SSD_P = 64
S5_N = 1024
MLP_HB = 512
TB = 1024
TB_BWD = 512
MLP_TB = 1024
MLP_SLABS = 2
MLP_BWD_SLABS = 8
EPS = 1e-6
LANES = 128
VMEM_LIMIT = 56 * 1024 * 1024
ADAM_LR, ADAM_B1, ADAM_B2, ADAM_EPS, ADAM_WD, ADAM_STEP = 0.001, 0.9, 0.999, 1e-08, 0.01, 10
POOL_WINDOWS = (2, 4, 8, 16)

C_POOL, C_GB, C_GC, C_HH, C_Z, C_XS, C_BM, C_CM, C_S5 = range(9)
C_DT128 = DT_COL // LANES

MESH = pl.DeviceIdType.MESH
ANY = pl.BlockSpec(memory_space=pl.ANY)
VMEM = pl.BlockSpec(memory_space=pltpu.VMEM)


def _dot(a, b):
    return jnp.dot(a, b, preferred_element_type=f32)


def _dot_nt(a, b):
    return lax.dot_general(a, b, (((1,), (1,)), ((), ())), preferred_element_type=f32)


def _dot_tn(a, b):
    return lax.dot_general(a, b, (((0,), (0,)), ((), ())), preferred_element_type=f32)


def _dot_exact(a, b):
    return jnp.dot(a, b, preferred_element_type=f32, precision=lax.Precision.HIGHEST)


def _b(x):
    return x.astype(bf16)


def _silu(x):
    return x * jax.nn.sigmoid(x)


def _dsilu(x):
    s = jax.nn.sigmoid(x)
    return s * (1.0 + x * (1.0 - s))


def _softplus(x):
    return jnp.maximum(x, 0.0) + jnp.log1p(jnp.exp(-jnp.abs(x)))


_GELU_K = 0.7978845608028654
_GELU_C = 0.044715


def _gelu(x):
    return 0.5 * x * (1.0 + jnp.tanh(_GELU_K * (x + _GELU_C * x * x * x)))


def _dgelu(x):
    th = jnp.tanh(_GELU_K * (x + _GELU_C * x * x * x))
    return 0.5 * (1.0 + th) + 0.5 * x * (1.0 - th * th) * _GELU_K * (1.0 + 3.0 * _GELU_C * x * x)


def _rms(h):
    r = lax.rsqrt(jnp.mean(h * h, axis=-1, keepdims=True) + EPS)
    return h * r, r


def _rms_bwd(dn, n, r):
    return r * (dn - n * jnp.mean(dn * n, axis=-1, keepdims=True))


def _colsum(x):
    return jnp.sum(x, axis=0, keepdims=True)


def _params(sem=None):
    return pltpu.CompilerParams(dimension_semantics=sem, vmem_limit_bytes=VMEM_LIMIT)


def _full(shape):
    return pl.BlockSpec(shape, lambda *_: (0,) * len(shape))


def _acc(ref, val):
    @pl.when(pl.program_id(0) == 0)
    def _():
        ref[...] = val

    @pl.when(pl.program_id(0) != 0)
    def _():
        ref[...] += val


def _me():
    return lax.axis_index("x"), lax.axis_index("y"), lax.axis_index("c")


def _my_index():
    x, y, c = _me()
    return 4 * x + 2 * y + c


def _coords(p):
    return (p // 4, (p // 2) % 2, p % 2)


class _Exchange:
    def __init__(self, srcs, gather):
        self.srcs = list(srcs)
        self.gather = gather
        self.n = len(self.srcs)
        self.out_shape = [jax.ShapeDtypeStruct(((N_DEV,) + s.shape) if gather else s.shape, s.dtype) for s in self.srcs]
        self.scratch = [pltpu.SemaphoreType.DMA((self.n, N_DEV)), pltpu.SemaphoreType.DMA((self.n, N_DEV)),
                        pltpu.SemaphoreType.DMA((self.n,))]

    def _src(self, refs, t, dev):
        return refs[t] if self.gather else refs[t].at[dev]

    def _remote(self, xin, xout, sems, t, k, me, to):
        return pltpu.make_async_remote_copy(
            src_ref=self._src(xin, t, to), dst_ref=xout[t].at[me], send_sem=sems[0].at[t, k], recv_sem=sems[1].at[t, k],
            device_id=_coords(to), device_id_type=MESH)

    def start(self, xin, xout, sems):
        me = _my_index()
        for t in range(self.n):
            pltpu.make_async_copy(self._src(xin, t, me), xout[t].at[me], sems[2].at[t]).start()
            for k in range(1, N_DEV):
                self._remote(xin, xout, sems, t, k, me, (me + k) % N_DEV).start()

    def wait(self, xin, xout, sems):
        me = _my_index()
        for t in range(self.n):
            for k in range(1, N_DEV):
                src = (me + N_DEV - k) % N_DEV
                pltpu.make_async_remote_copy(
                    src_ref=self._src(xin, t, src), dst_ref=xout[t].at[src], send_sem=sems[0].at[t, k],
                    recv_sem=sems[1].at[t, k], device_id=_coords(src), device_id_type=MESH).wait_recv()
        for t in range(self.n):
            for k in range(1, N_DEV):
                self._remote(xin, xout, sems, t, k, me, (me + k) % N_DEV).wait_send()
            pltpu.make_async_copy(self._src(xin, t, me), xout[t].at[me], sems[2].at[t]).wait()

    def forward(self, xin, xout, sems):
        pass


class _ChipGather:
    def __init__(self, srcs):
        self.srcs = list(srcs)
        self.n = len(self.srcs)
        self.out_shape = [jax.ShapeDtypeStruct((N_DEV,) + s.shape, s.dtype) for s in self.srcs]
        self.scratch = [pltpu.SemaphoreType.DMA((self.n, 7)), pltpu.SemaphoreType.DMA((self.n, 7)),
                        pltpu.SemaphoreType.DMA((self.n,))]

    @staticmethod
    def _places():
        x, y, c = _me()
        chips = [(1 - x, y), (x, 1 - y), (1 - x, 1 - y)]
        return (x, y, c), (x, y, 1 - c), chips

    @staticmethod
    def _slab(ref, dev):
        return ref.at[4 * dev[0] + 2 * dev[1] + dev[2]]

    def _copy(self, xin, xout, sems, t, k, block, to, src=None):
        return pltpu.make_async_remote_copy(
            src_ref=self._slab(xout[t], block) if src is None else src, dst_ref=self._slab(xout[t], block),
            send_sem=sems[0].at[t, k], recv_sem=sems[1].at[t, k], device_id=to, device_id_type=MESH)

    def start(self, xin, xout, sems):
        me, sibling, chips = self._places()
        for t in range(self.n):
            pltpu.make_async_copy(xin[t], self._slab(xout[t], me), sems[2].at[t]).start()
            self._copy(xin, xout, sems, t, 0, me, sibling, src=xin[t]).start()
            for j, chip in enumerate(chips):
                self._copy(xin, xout, sems, t, 1 + j, me, (*chip, me[2]), src=xin[t]).start()

    def forward(self, xin, xout, sems):
        me, sibling, chips = self._places()
        for t in range(self.n):
            for j, chip in enumerate(chips):
                self._copy(xin, xout, sems, t, 1 + j, (*chip, me[2]), me).wait_recv()
                self._copy(xin, xout, sems, t, 4 + j, (*chip, me[2]), sibling).start()

    def wait(self, xin, xout, sems):
        me, sibling, chips = self._places()
        for t in range(self.n):
            self._copy(xin, xout, sems, t, 0, sibling, me).wait_recv()
            for j, chip in enumerate(chips):
                self._copy(xin, xout, sems, t, 4 + j, (*chip, 1 - me[2]), me).wait_recv()
        for t in range(self.n):
            self._copy(xin, xout, sems, t, 0, me, sibling, src=xin[t]).wait_send()
            for j, chip in enumerate(chips):
                self._copy(xin, xout, sems, t, 1 + j, me, (*chip, me[2]), src=xin[t]).wait_send()
                self._copy(xin, xout, sems, t, 4 + j, (*chip, me[2]), sibling).wait_send()
            pltpu.make_async_copy(xin[t], self._slab(xout[t], me), sems[2].at[t]).wait()


def _call(body, *, name, grid, in_specs, out_specs, out_shape, args, semantics, scratch_shapes=(), xchg=None):
    if xchg is None:
        outs = pl.pallas_call(body, name=name, grid=grid, in_specs=in_specs, out_specs=out_specs, out_shape=out_shape,
                              scratch_shapes=list(scratch_shapes), compiler_params=_params(semantics))(*args)
        return outs, ()
    n_in, n_out, n_scr, n = len(in_specs), len(out_specs), len(scratch_shapes), xchg.n

    def carried(*refs):
        ins, xin = refs[:n_in], refs[n_in:n_in + n]
        outs, xout = refs[n_in + n:n_in + n + n_out], refs[n_in + n + n_out:n_in + 2 * n + n_out]
        scr, sems = refs[n_in + 2 * n + n_out:n_in + 2 * n + n_out + n_scr], refs[n_in + 2 * n + n_out + n_scr:]
        step = pl.program_id(0)
        for d in range(1, len(grid)):
            step = step * grid[d] + pl.program_id(d)
        n_steps = functools.reduce(lambda a, b: a * b, grid)

        @pl.when(step == 0)
        def _():
            xchg.start(xin, xout, sems)

        @pl.when(step == (2 * n_steps) // 3)
        def _():
            xchg.forward(xin, xout, sems)

        body(*ins, *outs, *scr)

        @pl.when(step == n_steps - 1)
        def _():
            xchg.wait(xin, xout, sems)

    res = pl.pallas_call(
        carried, name=name, grid=grid, in_specs=list(in_specs) + [ANY] * n, out_specs=list(out_specs) + [ANY] * n,
        out_shape=list(out_shape) + xchg.out_shape, scratch_shapes=list(scratch_shapes) + xchg.scratch,
        compiler_params=_params(("arbitrary",) * len(grid)))(*args, *xchg.srcs)
    return res[:n_out], tuple(res[n_out:])


def _exchange_alone(xchg, name):
    def body(*refs):
        xin, xout, sems = refs[:xchg.n], refs[xchg.n:2 * xchg.n], refs[2 * xchg.n:]
        xchg.start(xin, xout, sems)
        xchg.forward(xin, xout, sems)
        xchg.wait(xin, xout, sems)

    return pl.pallas_call(body, name=name, out_shape=xchg.out_shape, in_specs=[ANY] * xchg.n, out_specs=[ANY] * xchg.n,
                          scratch_shapes=xchg.scratch)(*xchg.srcs)


def _cast_shards(shards):
    n = len(shards)

    def body(*refs):
        for i, o in zip(refs[:n], refs[n:]):
            o[...] = i[...].astype(bf16)

    return pl.pallas_call(body, name="cast_shards", out_shape=[jax.ShapeDtypeStruct(s.shape, bf16) for s in shards],
                          in_specs=[VMEM] * n, out_specs=[VMEM] * n, compiler_params=_params())(*shards)


def _allreduce_rows(v):
    r = v.shape[0]
    rp = r // N_DEV

    def body(v_ref, o_ref, parts, sums, send1, recv1, send2, recv2):
        me = _my_index()

        def piece(ref, d):
            return ref.at[pl.ds(pl.multiple_of(d * rp, 8), rp), :]

        def copy1(k, src_dev, to):
            return pltpu.make_async_remote_copy(src_ref=piece(v_ref, to), dst_ref=parts.at[src_dev], send_sem=send1.at[k],
                                                recv_sem=recv1.at[k], device_id=_coords(to), device_id_type=MESH)

        def copy2(k, owner, to):
            return pltpu.make_async_remote_copy(src_ref=sums, dst_ref=piece(o_ref, owner), send_sem=send2.at[k],
                                                recv_sem=recv2.at[k], device_id=_coords(to), device_id_type=MESH)

        for k in range(1, N_DEV):
            copy1(k, me, (me + k) % N_DEV).start()
        parts[me] = v_ref[pl.ds(pl.multiple_of(me * rp, 8), rp), :]
        for k in range(1, N_DEV):
            copy1(k, (me + N_DEV - k) % N_DEV, me).wait_recv()
        total = parts[0]
        for s in range(1, N_DEV):
            total = total + parts[s]
        sums[...] = total
        o_ref[pl.ds(pl.multiple_of(me * rp, 8), rp), :] = total
        for k in range(1, N_DEV):
            copy2(k, me, (me + k) % N_DEV).start()
        for k in range(1, N_DEV):
            copy2(k, (me + N_DEV - k) % N_DEV, me).wait_recv()
        for k in range(1, N_DEV):
            copy1(k, me, (me + k) % N_DEV).wait_send()
            copy2(k, me, (me + k) % N_DEV).wait_send()

    return pl.pallas_call(
        body, name="allreduce_small_grads", out_shape=jax.ShapeDtypeStruct(v.shape, v.dtype),
        in_specs=[VMEM], out_specs=VMEM,
        scratch_shapes=[pltpu.VMEM((N_DEV, rp, LANES), f32), pltpu.VMEM((rp, LANES), f32)]
        + [pltpu.SemaphoreType.DMA((N_DEV,))] * 4,
        compiler_params=_params(),
    )(v)


def _gather_rows(v, name):
    def body(v_ref, o_ref, send_sems, recv_sems):
        me = _my_index()
        o_ref[me] = v_ref[...]
        sends = []
        for k in range(1, N_DEV):
            peer = (me + k) % N_DEV
            rc = pltpu.make_async_remote_copy(src_ref=v_ref, dst_ref=o_ref.at[me], send_sem=send_sems.at[k],
                                              recv_sem=recv_sems.at[k], device_id=_coords(peer), device_id_type=MESH)
            rc.start()
            sends.append(rc)
        for k in range(1, N_DEV):
            src = (me + N_DEV - k) % N_DEV
            pltpu.make_async_remote_copy(src_ref=v_ref, dst_ref=o_ref.at[src], send_sem=send_sems.at[k],
                                         recv_sem=recv_sems.at[k], device_id=_coords(src), device_id_type=MESH).wait_recv()
        for rc in sends:
            rc.wait_send()

    return pl.pallas_call(
        body, name=name, out_shape=jax.ShapeDtypeStruct((N_DEV,) + v.shape, v.dtype),
        in_specs=[VMEM], out_specs=VMEM,
        scratch_shapes=[pltpu.SemaphoreType.DMA((N_DEV,)), pltpu.SemaphoreType.DMA((N_DEV,))],
        compiler_params=pltpu.CompilerParams(vmem_limit_bytes=VMEM_LIMIT),
    )(v)


def _all_to_all_rows(v, name):
    def body(v_ref, o_ref, send_sems, recv_sems):
        me = _my_index()
        o_ref[me] = v_ref[me]
        sends = []
        for k in range(1, N_DEV):
            peer = (me + k) % N_DEV
            rc = pltpu.make_async_remote_copy(src_ref=v_ref.at[peer], dst_ref=o_ref.at[me], send_sem=send_sems.at[k],
                                              recv_sem=recv_sems.at[k], device_id=_coords(peer), device_id_type=MESH)
            rc.start()
            sends.append(rc)
        for k in range(1, N_DEV):
            src = (me + N_DEV - k) % N_DEV
            pltpu.make_async_remote_copy(src_ref=v_ref.at[src], dst_ref=o_ref.at[src], send_sem=send_sems.at[k],
                                         recv_sem=recv_sems.at[k], device_id=_coords(src), device_id_type=MESH).wait_recv()
        for rc in sends:
            rc.wait_send()

    return pl.pallas_call(
        body, name=name, out_shape=jax.ShapeDtypeStruct(v.shape, v.dtype),
        in_specs=[VMEM], out_specs=VMEM,
        scratch_shapes=[pltpu.SemaphoreType.DMA((N_DEV,)), pltpu.SemaphoreType.DMA((N_DEV,))],
    )(v)


def _ada_forward(c_all, ada_w, ada_b_cols):
    def body(c_ref, w_ref, b_ref, cond_ref, o_ref):
        cond = _silu(c_ref[...])
        cond_ref[...] = cond
        for l in range(2):
            o_ref[l] = _dot(_b(cond), _b(w_ref[l])) + b_ref[l]

    return pl.pallas_call(
        body, name="ada_forward",
        out_shape=[jax.ShapeDtypeStruct((N_DEV, D_MODEL), f32), jax.ShapeDtypeStruct((2, N_DEV, 768), f32)],
        in_specs=[VMEM] * 3, out_specs=[VMEM] * 2, compiler_params=_params(),
    )(c_all, ada_w, ada_b_cols)


def _ada_backward(cond, dmod_rows):
    def body(c_ref, d_ref, o_ref):
        cb = _b(c_ref[...])
        for l in range(2):
            o_ref[l] = _dot_tn(cb, _b(d_ref[l]))

    return pl.pallas_call(
        body, name="ada_backward", out_shape=jax.ShapeDtypeStruct((2, D_MODEL, 768), f32),
        in_specs=[VMEM] * 2, out_specs=VMEM, compiler_params=_params(),
    )(cond, dmod_rows)


def _inproj_fwd(h, norm_w, sc, sh, w_in, tb, xchg=None):
    t = h.shape[0]

    def body(h_ref, nw_ref, sc_ref, sh_ref, w_ref, proj_ref, u_ref):
        n, _ = _rms(h_ref[...])
        u = _b(n * nw_ref[...] * (1.0 + sc_ref[...]) + sh_ref[...])
        u_ref[...] = u
        proj_ref[...] = _dot(u, w_ref[...])

    row = pl.BlockSpec((tb, D_MODEL), lambda i: (i, 0))
    vec = _full((1, D_MODEL))
    return _call(
        body, name="inproj_fwd", grid=(t // tb,),
        out_shape=[jax.ShapeDtypeStruct((t, P_IN), f32), jax.ShapeDtypeStruct((t, D_MODEL), bf16)],
        in_specs=[row, vec, vec, vec, _full((D_MODEL, P_IN))],
        out_specs=[pl.BlockSpec((tb, P_IN), lambda i: (i, 0)), row],
        semantics=("parallel",), args=(h, norm_w, sc, sh, w_in), xchg=xchg)


def _inproj_bwd(dparts, dh_res, h, norm_w, sc, sh, w_in, tb, xchg=None):
    t = h.shape[0]

    def body(*refs):
        parts = refs[:10]
        dres_ref, h_ref, nw_ref, sc_ref, sh_ref, w_ref = refs[10:16]
        dh_ref, dsh_ref, dsc_ref, dnw_ref = refs[16:]
        dproj = jnp.concatenate([p[...] for p in parts], axis=1)
        du = _dot_nt(dproj, w_ref[...])
        n, r = _rms(h_ref[...])
        nw = nw_ref[...]
        gain = 1.0 + sc_ref[...]
        _acc(dsh_ref, _colsum(du))
        _acc(dsc_ref, _colsum(du * n * nw))
        _acc(dnw_ref, _colsum(du * gain * n))
        dh_ref[...] = dres_ref[...] + _rms_bwd(du * nw * gain, n, r)

    row = pl.BlockSpec((tb, D_MODEL), lambda i: (i, 0))
    vec = _full((1, D_MODEL))
    part_specs = [pl.BlockSpec((tb, GROUP_W), lambda i: (i, 0))] * 9 + [pl.BlockSpec((tb, LANES), lambda i: (i, 0))]
    return _call(
        body, name="inproj_bwd", grid=(t // tb,),
        out_shape=[jax.ShapeDtypeStruct((t, D_MODEL), f32)] + [jax.ShapeDtypeStruct((1, D_MODEL), f32)] * 3,
        in_specs=part_specs + [row, row, vec, vec, vec, _full((D_MODEL, P_IN))],
        out_specs=[row, vec, vec, vec],
        semantics=("arbitrary",), xchg=xchg, args=(*dparts, dh_res, h, norm_w, sc, sh, w_in))


def _wgrad(a, b, n_blocks, name, tm, tk=512):
    t, m = a.shape
    nb = b.shape[1] // n_blocks
    tk = min(tk, t)
    nk = t // tk

    def body(a_ref, b_ref, o_ref, acc_ref):
        k = pl.program_id(2)
        p = _dot_tn(a_ref[...], b_ref[...])

        @pl.when(k == 0)
        def _():
            acc_ref[...] = p

        @pl.when(k != 0)
        def _():
            acc_ref[...] += p

        @pl.when(k == nk - 1)
        def _():
            o_ref[0] = acc_ref[...].astype(o_ref.dtype)

    return pl.pallas_call(
        body, name=name, grid=(m // tm, n_blocks, nk),
        out_shape=jax.ShapeDtypeStruct((n_blocks, m, nb), bf16),
        in_specs=[pl.BlockSpec((tk, tm), lambda i, j, k: (k, i)), pl.BlockSpec((tk, nb), lambda i, j, k: (k, j))],
        out_specs=pl.BlockSpec((1, tm, nb), lambda i, j, k: (j, i, 0)),
        scratch_shapes=[pltpu.VMEM((tm, nb), f32)],
        compiler_params=_params(("parallel", "parallel", "arbitrary")),
    )(a, b)


def _wgrad_parts(a, parts, name, tm, tk):
    t, m = a.shape
    n = sum(p.shape[1] for p in parts)
    n_parts = len(parts)
    tk = min(tk, t)
    nk = t // tk

    def body(*refs):
        a_ref, part_refs, o_ref, acc_ref = refs[0], refs[1:1 + n_parts], refs[1 + n_parts], refs[2 + n_parts]
        k = pl.program_id(1)
        p = _dot_tn(a_ref[...], jnp.concatenate([r[...] for r in part_refs], axis=1))

        @pl.when(k == 0)
        def _():
            acc_ref[...] = p

        @pl.when(k != 0)
        def _():
            acc_ref[...] += p

        @pl.when(k == nk - 1)
        def _():
            o_ref[...] = acc_ref[...].astype(o_ref.dtype)

    return pl.pallas_call(
        body, name=name, grid=(m // tm, nk),
        out_shape=jax.ShapeDtypeStruct((m, n), bf16),
        in_specs=[pl.BlockSpec((tk, tm), lambda i, k: (k, i))]
        + [pl.BlockSpec((tk, p.shape[1]), lambda i, k: (k, 0)) for p in parts],
        out_specs=pl.BlockSpec((tm, n), lambda i, k: (i, 0)),
        scratch_shapes=[pltpu.VMEM((tm, n), f32)],
        compiler_params=_params(("parallel", "arbitrary")),
    )(a, *parts)


def _pool_counts(rows, t0):
    tpos = (lax.broadcasted_iota(jnp.int32, (rows, GROUP_W), 0) + t0 + 1).astype(f32)
    grp = lax.broadcasted_iota(jnp.int32, (rows, GROUP_W), 1) // 64
    win = jnp.where(grp == 0, 2.0, jnp.where(grp == 1, 4.0, jnp.where(grp == 2, 8.0, 16.0)))
    return jnp.minimum(tpos, win), grp


def _pool_select(grp, l1, l2, l3, l4):
    return jnp.where(grp == 0, l1, jnp.where(grp == 1, l2, jnp.where(grp == 2, l3, l4)))


def _pool_means(v, halo, t0):
    tb = v.shape[0]
    ext = jnp.concatenate([halo, v], axis=0)
    n = tb + 16
    s1 = ext[1:n] + ext[0:n - 1]
    s2 = s1[2:n - 1] + s1[0:n - 3]
    s3 = s2[4:n - 3] + s2[0:n - 7]
    s4 = s3[8:n - 7] + s3[0:n - 15]
    cnt, grp = _pool_counts(tb, t0)
    wsum = _pool_select(grp, s1[15:15 + tb], s2[13:13 + tb], s3[9:9 + tb], s4[1:1 + tb])
    return wsum / cnt - v


def _pool_fwd(proj, pw_bd, scale, tb):
    t = proj.shape[0]

    def body(v_ref, vh_ref, pw_ref, sc_ref, o_ref):
        i = pl.program_id(0)
        halo = jnp.where(i > 0, vh_ref[...], 0.0)
        p = _pool_means(v_ref[...], halo, i * tb)
        o_ref[...] = _b(_dot(_b(p), _b(pw_ref[...])) * sc_ref[...])

    return pl.pallas_call(
        body, name="pool_fwd", grid=(t // tb,),
        out_shape=jax.ShapeDtypeStruct((t, GROUP_W), bf16),
        in_specs=[pl.BlockSpec((tb, GROUP_W), lambda i: (i, C_POOL)),
                  pl.BlockSpec((16, GROUP_W), lambda i: (jnp.maximum(i * (tb // 16) - 1, 0), C_POOL)),
                  _full((GROUP_W, GROUP_W)), _full((1, GROUP_W))],
        out_specs=pl.BlockSpec((tb, GROUP_W), lambda i: (i, 0)),
        compiler_params=_params(("parallel",)),
    )(proj, proj, pw_bd, scale)


def _pool_bwd(proj, dy, pw_bd, scale, tb):
    t = proj.shape[0]
    nt = t // tb
    last16 = t // 16 - 1

    def body(v_ref, vh_ref, dy_ref, dyh_ref, pw_ref, sc_ref, dv_ref, dpw_ref, dsc_ref):
        i = pl.program_id(0)
        halo = jnp.where(i > 0, vh_ref[...], 0.0)
        p = _pool_means(v_ref[...], halo, i * tb)
        pw = _b(pw_ref[...])
        sc = sc_ref[...]
        dy = dy_ref[...]
        ypre = _dot(_b(p), pw)
        _acc(dsc_ref, _colsum(dy * ypre))
        dys = _b(dy * sc)
        _acc(dpw_ref, _dot_tn(_b(p), dys))
        dp = _dot_nt(dys, pw)
        dph = _dot_nt(_b(jnp.where(i < nt - 1, dyh_ref[...], 0.0) * sc), pw)
        cnt, grp = _pool_counts(tb, i * tb)
        cnth, _ = _pool_counts(16, (i + 1) * tb)
        ext = jnp.concatenate([dp / cnt, dph / cnth], axis=0)
        n = tb + 16
        f1 = ext[0:n - 1] + ext[1:n]
        f2 = f1[0:n - 3] + f1[2:n - 1]
        f3 = f2[0:n - 7] + f2[4:n - 3]
        f4 = f3[0:n - 15] + f3[8:n - 7]
        dv_ref[...] = _b(_pool_select(grp, f1[0:tb], f2[0:tb], f3[0:tb], f4[0:tb]) - dp)

    return pl.pallas_call(
        body, name="pool_bwd", grid=(nt,),
        out_shape=[jax.ShapeDtypeStruct((t, GROUP_W), bf16), jax.ShapeDtypeStruct((GROUP_W, GROUP_W), f32),
                   jax.ShapeDtypeStruct((1, GROUP_W), f32)],
        in_specs=[pl.BlockSpec((tb, GROUP_W), lambda i: (i, C_POOL)),
                  pl.BlockSpec((16, GROUP_W), lambda i: (jnp.maximum(i * (tb // 16) - 1, 0), C_POOL)),
                  pl.BlockSpec((tb, GROUP_W), lambda i: (i, 0)),
                  pl.BlockSpec((16, GROUP_W), lambda i: (jnp.minimum((i + 1) * (tb // 16), last16), 0)),
                  _full((GROUP_W, GROUP_W)), _full((1, GROUP_W))],
        out_specs=[pl.BlockSpec((tb, GROUP_W), lambda i: (i, 0)), _full((GROUP_W, GROUP_W)), _full((1, GROUP_W))],
        compiler_params=_params(("arbitrary",)),
    )(proj, proj, dy, dy, pw_bd, scale)


def _sconv_fwd(proj, w, tb):
    t = proj.shape[0]

    def body(gb_ref, gc_ref, hh_ref, gch_ref, hhh_ref, w_ref, o_ref):
        i = pl.program_id(0)
        q = gc_ref[...] * hh_ref[...]
        qh = jnp.where(i > 0, gch_ref[...] * hhh_ref[...], 0.0)
        ext = jnp.concatenate([qh, q], axis=0)
        w = w_ref[...]
        conv = w[0:1] * ext[6:6 + tb] + w[1:2] * ext[7:7 + tb] + w[2:3] * ext[8:8 + tb]
        o_ref[...] = _b(gb_ref[...] * conv)

    def col(c):
        return pl.BlockSpec((tb, GROUP_W), lambda i: (i, c))

    def prev(c):
        return pl.BlockSpec((8, GROUP_W), lambda i: (jnp.maximum(i * (tb // 8) - 1, 0), c))

    return pl.pallas_call(
        body, name="sconv_fwd", grid=(t // tb,),
        out_shape=jax.ShapeDtypeStruct((t, GROUP_W), bf16),
        in_specs=[col(C_GB), col(C_GC), col(C_HH), prev(C_GC), prev(C_HH), _full((8, GROUP_W))],
        out_specs=pl.BlockSpec((tb, GROUP_W), lambda i: (i, 0)),
        compiler_params=_params(("parallel",)),
    )(proj, proj, proj, proj, proj, w)


def _sconv_bwd(proj, dy, w, tb):
    t = proj.shape[0]
    nt = t // tb
    last8 = t // 8 - 1

    def body(gb_ref, gc_ref, hh_ref, gch_ref, hhh_ref, gbn_ref, dy_ref, dyn_ref, w_ref, dgb_ref, dgc_ref, dhh_ref, dw_ref):
        i = pl.program_id(0)
        gc, hh, gb, dy = gc_ref[...], hh_ref[...], gb_ref[...], dy_ref[...]
        q = gc * hh
        qh = jnp.where(i > 0, gch_ref[...] * hhh_ref[...], 0.0)
        ext = jnp.concatenate([qh, q], axis=0)
        w = w_ref[...]
        conv = w[0:1] * ext[6:6 + tb] + w[1:2] * ext[7:7 + tb] + w[2:3] * ext[8:8 + tb]
        dgb_ref[...] = _b(dy * conv)
        e = dy * gb
        en = jnp.where(i < nt - 1, dyn_ref[...] * gbn_ref[...], 0.0)
        exte = jnp.concatenate([e, en], axis=0)
        dq = w[2:3] * exte[0:tb] + w[1:2] * exte[1:1 + tb] + w[0:1] * exte[2:2 + tb]
        dgc_ref[...] = _b(dq * hh)
        dhh_ref[...] = _b(dq * gc)
        dw = jnp.concatenate([_colsum(e * ext[6:6 + tb]), _colsum(e * ext[7:7 + tb]), _colsum(e * ext[8:8 + tb]),
                              jnp.zeros((5, GROUP_W), f32)], axis=0)
        _acc(dw_ref, dw)

    def col(c):
        return pl.BlockSpec((tb, GROUP_W), lambda i: (i, c))

    def prev(c):
        return pl.BlockSpec((8, GROUP_W), lambda i: (jnp.maximum(i * (tb // 8) - 1, 0), c))

    def nxt(c):
        return pl.BlockSpec((8, GROUP_W), lambda i: (jnp.minimum((i + 1) * (tb // 8), last8), c))

    out = pl.BlockSpec((tb, GROUP_W), lambda i: (i, 0))
    return pl.pallas_call(
        body, name="sconv_bwd", grid=(nt,),
        out_shape=[jax.ShapeDtypeStruct((t, GROUP_W), bf16)] * 3 + [jax.ShapeDtypeStruct((8, GROUP_W), f32)],
        in_specs=[col(C_GB), col(C_GC), col(C_HH), prev(C_GC), prev(C_HH), nxt(C_GB), col(0), nxt(0), _full((8, GROUP_W))],
        out_specs=[out, out, out, _full((8, GROUP_W))],
        compiler_params=_params(("arbitrary",)),
    )(proj, proj, proj, proj, proj, proj, dy, dy, w)


def _conv4(xr, halo, w, bias):
    tb = xr.shape[0]
    ext = jnp.concatenate([halo, xr], axis=0)
    pre = w[0:1] * ext[5:5 + tb] + w[1:2] * ext[6:6 + tb] + w[2:3] * ext[7:7 + tb] + w[3:4] * ext[8:8 + tb] + bias
    return pre, ext


def _tri():
    r = lax.broadcasted_iota(jnp.int32, (SSD_CHUNK, SSD_CHUNK), 0)
    c = lax.broadcasted_iota(jnp.int32, (SSD_CHUNK, SSD_CHUNK), 1)
    return r >= c


def _lane_pick(vals):
    rows = vals[0].shape[0]
    lane = lax.broadcasted_iota(jnp.int32, (rows, LANES), 1)
    out = jnp.zeros((rows, LANES), f32)
    for h, v in enumerate(vals):
        out = jnp.where(lane == h, v, out)
    return out


def _ssd_fwd(proj, conv_w, conv_b, dt_bias, a_log, d_cols, tb, xchg=None):
    t = proj.shape[0]
    cpt = tb // SSD_CHUNK

    def body(z_ref, xs_ref, bm_ref, cm_ref, xsh_ref, bmh_ref, cmh_ref, dt_ref, cw_ref, cb_ref, dtb_ref, al_ref, dk_ref,
             o_ref, y_ref, st_ref, state):
        i = pl.program_id(0)

        @pl.when(i == 0)
        def _():
            state[...] = jnp.zeros_like(state)

        cw, cb = cw_ref[...], cb_ref[...]
        acts = []
        for j, (r, hr) in enumerate(((xs_ref, xsh_ref), (bm_ref, bmh_ref), (cm_ref, cmh_ref))):
            halo = jnp.where(i > 0, hr[...], 0.0)
            pre, _ = _conv4(r[...], halo, cw[:, j * 256:(j + 1) * 256], cb[:, j * 256:(j + 1) * 256])
            acts.append(_silu(pre))
        xs, bm, cm = acts
        dt = _softplus(dt_ref[...] + dtb_ref[...])
        a = -jnp.exp(al_ref[...])
        adt = dt * a
        tri = _tri()
        trif = tri.astype(f32)
        dk = dk_ref[...]
        for c in range(cpt):
            rows = slice(c * SSD_CHUNK, (c + 1) * SSD_CHUNK)
            acol = _dot_exact(trif, adt[rows])
            arow = acol.T
            dt_c = dt[rows]
            ys = []
            rowi = lax.broadcasted_iota(jnp.int32, (SSD_CHUNK, 1), 0)
            first = lax.broadcasted_iota(jnp.int32, (SSD_CHUNK, SSD_CHUNK), 1) < SSD_P
            for g in range(SSD_HEADS // 2):
                cols = slice(g * 128, (g + 1) * 128)
                cg, bg = _b(cm[rows, cols]), _b(bm[rows, cols])
                xg = xs[rows, cols]
                heads = (2 * g, 2 * g + 1)
                ac = [acol[:, h:h + 1] for h in heads]
                alast = [v[SSD_CHUNK - 1:SSD_CHUNK] for v in ac]
                dtw = jnp.where(first, dt_c[:, heads[0]:heads[0] + 1], dt_c[:, heads[1]:heads[1] + 1])
                eaw = jnp.where(first, jnp.exp(ac[0]), jnp.exp(ac[1]))
                wdw = jnp.where(first, jnp.exp(alast[0] - ac[0]), jnp.exp(alast[1] - ac[1]))
                xdt = xg * dtw
                xb = _b(xdt)
                gmat = _dot_nt(cg, bg)
                ydiag = []
                for k, h in enumerate(heads):
                    lm = jnp.exp(jnp.where(tri, ac[k] - arow[h:h + 1, :], -jnp.inf))
                    ydiag.append(_dot(_b(gmat * lm), xb[:, k * SSD_P:(k + 1) * SSD_P]))
                s_in = state[g]
                st_ref[c, g] = s_in
                ys.append(jnp.concatenate(ydiag, axis=1) + eaw * _dot_nt(cg, _b(s_in)) + xg * dk[:, cols])
                state[g] = jnp.where(rowi < SSD_P, jnp.exp(alast[0]), jnp.exp(alast[1])) * s_in + _dot_tn(_b(xdt * wdw), bg)
            yc = jnp.concatenate(ys, axis=1)
            y_ref[rows, :] = yc
            o_ref[rows, :] = _b(yc * _silu(z_ref[rows, :]))

    def col(c):
        return pl.BlockSpec((tb, GROUP_W), lambda i: (i, c))

    def prev(c):
        return pl.BlockSpec((8, GROUP_W), lambda i: (jnp.maximum(i * (tb // 8) - 1, 0), c))

    out = pl.BlockSpec((tb, GROUP_W), lambda i: (i, 0))
    return _call(
        body, name="ssd_fwd", grid=(t // tb,),
        out_shape=[jax.ShapeDtypeStruct((t, GROUP_W), bf16), jax.ShapeDtypeStruct((t, GROUP_W), f32),
                   jax.ShapeDtypeStruct((t // SSD_CHUNK, 2, 128, 128), f32)],
        in_specs=[col(C_Z), col(C_XS), col(C_BM), col(C_CM), prev(C_XS), prev(C_BM), prev(C_CM),
                  pl.BlockSpec((tb, LANES), lambda i: (i, C_DT128)),
                  _full((8, 768)), _full((1, 768)), _full((1, LANES)), _full((1, LANES)), _full((1, GROUP_W))],
        out_specs=[out, out, pl.BlockSpec((cpt, 2, 128, 128), lambda i: (i, 0, 0, 0))],
        scratch_shapes=[pltpu.VMEM((2, 128, 128), f32)],
        semantics=("arbitrary",), xchg=xchg,
        args=(proj, proj, proj, proj, proj, proj, proj, proj, conv_w, conv_b, dt_bias, a_log, d_cols))


def _ssd_bwd(proj, dyc, y_pre, states, conv_w, conv_b, dt_bias, a_log, d_cols, tb, xchg=None):
    t = proj.shape[0]
    nt = t // tb
    cpt = tb // SSD_CHUNK

    def body(z_ref, xs_ref, bm_ref, cm_ref, xsh_ref, bmh_ref, cmh_ref, dt_ref, dy_ref, yp_ref, st_ref,
             cw_ref, cb_ref, dtb_ref, al_ref, dk_ref,
             dz_ref, dxs_ref, dbm_ref, dcm_ref, ddt_ref, dcw_ref, dcb_ref, ddtb_ref, dal_ref, ddk_ref,
             dstate, carry):
        i = pl.program_id(0)
        ti = nt - 1 - i

        @pl.when(i == 0)
        def _():
            dstate[...] = jnp.zeros_like(dstate)
            carry[...] = jnp.zeros_like(carry)

        cw, cb = cw_ref[...], cb_ref[...]
        pres, exts, acts = [], [], []
        for j, (r, hr) in enumerate(((xs_ref, xsh_ref), (bm_ref, bmh_ref), (cm_ref, cmh_ref))):
            halo = jnp.where(ti > 0, hr[...], 0.0)
            pre, ext = _conv4(r[...], halo, cw[:, j * 256:(j + 1) * 256], cb[:, j * 256:(j + 1) * 256])
            pres.append(pre)
            exts.append(ext)
            acts.append(_silu(pre))
        xs, bm, cm = acts
        raw = dt_ref[...] + dtb_ref[...]
        dt = _softplus(raw)
        a = -jnp.exp(al_ref[...])
        adt = dt * a
        tri = _tri()
        trif = tri.astype(f32)
        dk = dk_ref[...]
        z = z_ref[...]
        dyc = dy_ref[...]
        dz_ref[...] = _b(dyc * yp_ref[...] * _dsilu(z))
        dy_all = dyc * _silu(z)
        lane = lax.broadcasted_iota(jnp.int32, (1, LANES), 1)
        ddk_acc = jnp.zeros((1, LANES), f32)
        dal_acc = jnp.zeros((1, LANES), f32)
        dxs_c, dbm_c, dcm_c, ddt_c = [None] * cpt, [None] * cpt, [None] * cpt, [None] * cpt
        for c in reversed(range(cpt)):
            rows = slice(c * SSD_CHUNK, (c + 1) * SSD_CHUNK)
            acol = _dot_exact(trif, adt[rows])
            arow = acol.T
            dt_c = dt[rows]
            da_cols, da_rows, ddt_heads, dxs_groups, dbg, dcg = [], [], [], [], [], []
            rowi = lax.broadcasted_iota(jnp.int32, (SSD_CHUNK, 1), 0)
            first = lax.broadcasted_iota(jnp.int32, (SSD_CHUNK, SSD_CHUNK), 1) < SSD_P
            for g in range(SSD_HEADS // 2):
                cols = slice(g * 128, (g + 1) * 128)
                cgf, bgf = cm[rows, cols], bm[rows, cols]
                cg, bg = _b(cgf), _b(bgf)
                xg, dyg = xs[rows, cols], dy_all[rows, cols]
                s_in, dsn = st_ref[c, g], dstate[g]
                sb, dsnb = _b(s_in), _b(dsn)
                heads = (2 * g, 2 * g + 1)
                ac = [acol[:, h:h + 1] for h in heads]
                alast = [v[SSD_CHUNK - 1:SSD_CHUNK] for v in ac]
                el = [jnp.exp(v) for v in alast]
                dtw = jnp.where(first, dt_c[:, heads[0]:heads[0] + 1], dt_c[:, heads[1]:heads[1] + 1])
                eaw = jnp.where(first, jnp.exp(ac[0]), jnp.exp(ac[1]))
                wdw = jnp.where(first, jnp.exp(alast[0] - ac[0]), jnp.exp(alast[1] - ac[1]))
                xdt = xg * dtw
                xb, dyb = _b(xdt), _b(dyg)
                gmat = _dot_nt(cg, bg)
                dgs, dxh, da = None, [], []
                for k, h in enumerate(heads):
                    hc = slice(k * SSD_P, (k + 1) * SSD_P)
                    lm = jnp.exp(jnp.where(tri, ac[k] - arow[h:h + 1, :], -jnp.inf))
                    m = gmat * lm
                    dm = _dot_nt(dyb[:, hc], xb[:, hc])
                    dxh.append(_dot_tn(_b(m), dyb[:, hc]))
                    dgs = dm * lm if dgs is None else dgs + dm * lm
                    wm = dm * m
                    da.append(jnp.sum(wm, axis=1, keepdims=True))
                    da_rows.append(jnp.sum(wm, axis=0, keepdims=True))
                dgb = _b(dgs)
                dcg_g = _dot(dgb, bg)
                dbg_g = _dot_tn(dgb, cg)
                yoff = eaw * _dot_nt(cg, sb)
                dyoff = dyg * yoff
                dye = _b(dyg * eaw)
                dcg_g = dcg_g + _dot(dye, sb)
                ds_y = _dot_tn(dye, cg)
                u = _dot_nt(bg, dsnb)
                dx = jnp.concatenate(dxh, axis=1) + wdw * u
                dbg_g = dbg_g + _dot(_b(xdt * wdw), dsnb)
                xu = xdt * u * wdw
                ss = jnp.sum(dsn * s_in, axis=1, keepdims=True)
                dxx = dx * xg
                dyx = _colsum(dyg * xg)
                for k, h in enumerate(heads):
                    mine = first if k == 0 else jnp.logical_not(first)
                    dwv = jnp.sum(jnp.where(mine, xu, 0.0), axis=1, keepdims=True)
                    mine_rows = (rowi < SSD_P) if k == 0 else (rowi >= SSD_P)
                    dalast = jnp.sum(dwv, axis=0, keepdims=True) + el[k] * jnp.sum(jnp.where(mine_rows, ss, 0.0), axis=0, keepdims=True)
                    dah = da[k] + jnp.sum(jnp.where(mine, dyoff, 0.0), axis=1, keepdims=True) - dwv
                    da_cols.append(dah + jnp.where(rowi == SSD_CHUNK - 1, dalast, 0.0))
                    ddt_heads.append(jnp.sum(jnp.where(mine, dxx, 0.0), axis=1, keepdims=True))
                    ddk_acc = ddk_acc + jnp.where(lane == h, jnp.sum(jnp.where(mine[0:1], dyx, 0.0), axis=1, keepdims=True), 0.0)
                dstate[g] = jnp.where(rowi < SSD_P, el[0], el[1]) * dsn + ds_y
                dxs_groups.append(dx * dtw + dyg * dk[:, cols])
                dbg.append(dbg_g)
                dcg.append(dcg_g)
            da_blk = _lane_pick(da_cols)
            rowsel = lax.broadcasted_iota(jnp.int32, (SSD_CHUNK, SSD_CHUNK), 0)
            da_rows_blk = jnp.zeros((SSD_CHUNK, SSD_CHUNK), f32)
            for h in range(SSD_HEADS):
                da_rows_blk = jnp.where(rowsel == h, da_rows[h], da_rows_blk)
            da_blk = da_blk - da_rows_blk.T
            dadt = lax.dot_general(trif, da_blk, (((0,), (0,)), ((), ())), preferred_element_type=f32,
                                   precision=lax.Precision.HIGHEST)
            dal_acc = dal_acc + _colsum(dadt * dt_c)
            ddt_c[c] = dadt * a + _lane_pick(ddt_heads)
            dxs_c[c] = jnp.concatenate(dxs_groups, axis=1)
            dbm_c[c] = jnp.concatenate(dbg, axis=1)
            dcm_c[c] = jnp.concatenate(dcg, axis=1)
        ddt = jnp.concatenate(ddt_c, axis=0) if cpt > 1 else ddt_c[0]
        ddraw = jnp.where(lane < SSD_HEADS, ddt * jax.nn.sigmoid(raw), 0.0)
        ddt_ref[...] = _b(ddraw)
        _acc(ddtb_ref, _colsum(ddraw))
        _acc(dal_ref, jnp.where(lane < SSD_HEADS, dal_acc * a, 0.0))
        _acc(ddk_ref, ddk_acc)
        dcw_parts, dcb_parts = [], []
        for j, (dparts, out_ref) in enumerate(((dxs_c, dxs_ref), (dbm_c, dbm_ref), (dcm_c, dcm_ref))):
            dact = jnp.concatenate(dparts, axis=0) if cpt > 1 else dparts[0]
            dpre = dact * _dsilu(pres[j])
            w = cw[:, j * 256:(j + 1) * 256]
            ext = jnp.concatenate([dpre, carry[:, j * 256:(j + 1) * 256]], axis=0)
            out_ref[...] = _b(w[3:4] * ext[0:tb] + w[2:3] * ext[1:1 + tb] + w[1:2] * ext[2:2 + tb] + w[0:1] * ext[3:3 + tb])
            carry[:, j * 256:(j + 1) * 256] = dpre[0:8]
            xe = exts[j]
            dcw_parts.append(jnp.concatenate([_colsum(dpre * xe[5 + k:5 + k + tb]) for k in range(4)]
                                             + [jnp.zeros((4, GROUP_W), f32)], axis=0))
            dcb_parts.append(_colsum(dpre))
        _acc(dcw_ref, jnp.concatenate(dcw_parts, axis=1))
        _acc(dcb_ref, jnp.concatenate(dcb_parts, axis=1))

    def col(c):
        return pl.BlockSpec((tb, GROUP_W), lambda i: (nt - 1 - i, c))

    def prev(c):
        return pl.BlockSpec((8, GROUP_W), lambda i: (jnp.maximum((nt - 1 - i) * (tb // 8) - 1, 0), c))

    out = pl.BlockSpec((tb, GROUP_W), lambda i: (nt - 1 - i, 0))
    vec = _full((1, LANES))
    return _call(
        body, name="ssd_bwd", grid=(nt,),
        out_shape=[jax.ShapeDtypeStruct((t, GROUP_W), bf16)] * 4 + [jax.ShapeDtypeStruct((t, LANES), bf16),
                   jax.ShapeDtypeStruct((8, 768), f32), jax.ShapeDtypeStruct((1, 768), f32)]
        + [jax.ShapeDtypeStruct((1, LANES), f32)] * 3,
        in_specs=[col(C_Z), col(C_XS), col(C_BM), col(C_CM), prev(C_XS), prev(C_BM), prev(C_CM),
                  pl.BlockSpec((tb, LANES), lambda i: (nt - 1 - i, C_DT128)), out, out,
                  pl.BlockSpec((cpt, 2, 128, 128), lambda i: (nt - 1 - i, 0, 0, 0)),
                  _full((8, 768)), _full((1, 768)), vec, vec, _full((1, GROUP_W))],
        out_specs=[out, out, out, out, pl.BlockSpec((tb, LANES), lambda i: (nt - 1 - i, 0)),
                   _full((8, 768)), _full((1, 768)), vec, vec, vec],
        scratch_shapes=[pltpu.VMEM((2, 128, 128), f32), pltpu.VMEM((8, 768), f32)],
        semantics=("arbitrary",), xchg=xchg,
        args=(proj, proj, proj, proj, proj, proj, proj, proj, dyc, y_pre, states, conv_w, conv_b, dt_bias, a_log, d_cols))


def _s5_coeffs(are, aim, ls):
    step = jnp.exp(ls)
    mag = jnp.exp(are * step)
    th = aim * step
    lre, lim = mag * jnp.cos(th), mag * jnp.sin(th)
    den = are * are + aim * aim
    nr = lre - 1.0
    fre = (nr * are + lim * aim) / den
    fim = (lim * are - nr * aim) / den
    return step, lre, lim, den, fre, fim


def _s5_prep(are, aim, ls, bre_bd, bim_bd):
    def body(are_ref, aim_ref, ls_ref, bre_ref, bim_ref, lre_ref, lim_ref, bbr_ref, bbi_ref):
        _, lre, lim, _, fre, fim = _s5_coeffs(are_ref[...], aim_ref[...], ls_ref[...])
        lre_ref[...] = lre
        lim_ref[...] = lim
        bre, bim = bre_ref[...], bim_ref[...]
        bbr_ref[...] = fre * bre - fim * bim
        bbi_ref[...] = fre * bim + fim * bre

    col = jax.ShapeDtypeStruct((S5_N, 1), f32)
    mat = jax.ShapeDtypeStruct((S5_N, GROUP_W), f32)
    return pl.pallas_call(body, name="s5_prep", out_shape=[col, col, mat, mat], in_specs=[VMEM] * 5, out_specs=[VMEM] * 4,
                          compiler_params=_params())(are, aim, ls, bre_bd, bim_bd)


def _s5_prep_bwd(are, aim, ls, bre_bd, bim_bd, dlre, dlim, dbbr, dbbi):
    def body(are_ref, aim_ref, ls_ref, bre_ref, bim_ref, dlre_ref, dlim_ref, dbbr_ref, dbbi_ref,
             dare_ref, daim_ref, dls_ref, dbre_ref, dbim_ref):
        are, aim = are_ref[...], aim_ref[...]
        step, lre, lim, den, fre, fim = _s5_coeffs(are, aim, ls_ref[...])
        r = lax.broadcasted_iota(jnp.int32, (S5_N, GROUP_W), 0) // 64
        c = lax.broadcasted_iota(jnp.int32, (S5_N, GROUP_W), 1) // 16
        mask = r == c
        gr = jnp.where(mask, dbbr_ref[...], 0.0)
        gi = jnp.where(mask, dbbi_ref[...], 0.0)
        bre, bim = bre_ref[...], bim_ref[...]
        dbre_ref[...] = fre * gr + fim * gi
        dbim_ref[...] = fre * gi - fim * gr
        dfre = jnp.sum(bre * gr + bim * gi, axis=1, keepdims=True)
        dfim = jnp.sum(bre * gi - bim * gr, axis=1, keepdims=True)
        ire, iim = are / den, aim / den
        tre = dlre_ref[...] + ire * dfre - iim * dfim
        tim = dlim_ref[...] + ire * dfim + iim * dfre
        dzre = lre * tre + lim * tim
        dzim = lre * tim - lim * tre
        qre = (fre * are + fim * aim) / den
        qim = (fim * are - fre * aim) / den
        dare_ref[...] = step * dzre - (qre * dfre + qim * dfim)
        daim_ref[...] = step * dzim - (qre * dfim - qim * dfre)
        dls = (are * dzre + aim * dzim) * step
        sel = (lax.broadcasted_iota(jnp.int32, (S5_N, LANES), 0) // 64 == lax.broadcasted_iota(jnp.int32, (S5_N, LANES), 1)).astype(f32)
        dls_ref[...] = lax.dot_general(sel, jnp.broadcast_to(dls, (S5_N, LANES)), (((0,), (0,)), ((), ())),
                                       preferred_element_type=f32, precision=lax.Precision.HIGHEST)

    col = jax.ShapeDtypeStruct((S5_N, 1), f32)
    mat = jax.ShapeDtypeStruct((S5_N, GROUP_W), f32)
    return pl.pallas_call(body, name="s5_prep_bwd", out_shape=[col, col, jax.ShapeDtypeStruct((LANES, LANES), f32), mat, mat],
                          in_specs=[VMEM] * 9, out_specs=[VMEM] * 5, compiler_params=_params(),
                          )(are, aim, ls, bre_bd, bim_bd, dlre, dlim, dbbr, dbbi)


def _cmul(ar, ai, br, bi):
    return ar * br - ai * bi, ar * bi + ai * br


def _s5_scan(re_ref, im_ref, carry_ref, mr, mi, n_groups, reverse):
    p1 = (mr, mi)
    p2 = _cmul(*p1, *p1)
    p3 = _cmul(*p2, *p1)
    p4 = _cmul(*p2, *p2)
    p5 = _cmul(*p4, *p1)
    p6 = _cmul(*p4, *p2)
    p7 = _cmul(*p4, *p3)
    p8 = _cmul(*p4, *p4)
    pows = [p1, p2, p3, p4, p5, p6, p7, p8]
    row = lax.broadcasted_iota(jnp.int32, (8, S5_N), 0)
    tr = jnp.zeros((8, S5_N), f32)
    ti = jnp.zeros((8, S5_N), f32)
    for i in range(8):
        p = pows[7 - i] if reverse else pows[i]
        tr = jnp.where(row == i, p[0], tr)
        ti = jnp.where(row == i, p[1], ti)
    steps = []
    for k, p in ((1, p1), (2, p2), (4, p4)):
        keep = (row + k < 8) if reverse else (row >= k)
        steps.append((8 - k if reverse else k, jnp.where(keep, p[0], 0.0), jnp.where(keep, p[1], 0.0)))
    edge = 0 if reverse else 7

    def step(j, carry):
        cr, ci = carry
        g = (n_groups - 1 - j) if reverse else j
        r0 = pl.multiple_of(g * 8, 8)
        xr = re_ref[pl.ds(r0, 8), :]
        xi = im_ref[pl.ds(r0, 8), :]
        for shift, br, bi in steps:
            sr = pltpu.roll(xr, shift, 0)
            si = pltpu.roll(xi, shift, 0)
            xr, xi = xr + br * sr - bi * si, xi + br * si + bi * sr
        xr, xi = xr + tr * cr - ti * ci, xi + tr * ci + ti * cr
        re_ref[pl.ds(r0, 8), :] = xr
        im_ref[pl.ds(r0, 8), :] = xi
        return (jnp.broadcast_to(xr[edge:edge + 1, :], (8, S5_N)), jnp.broadcast_to(xi[edge:edge + 1, :], (8, S5_N)))

    cr, ci = lax.fori_loop(0, n_groups, step, (carry_ref[0], carry_ref[1]))
    carry_ref[0] = cr
    carry_ref[1] = ci


def _s5_output(u, xr, xi, ctr, cti, d):
    return _dot_nt(_b(xr), _b(ctr)) - _dot_nt(_b(xi), _b(cti)) + d * u


def _s5_fwd(proj, bbr, bbi, ctr, cti, lre, lim, d, glu_w, glu_b, tb, xchg=None):
    t = proj.shape[0]

    def body(u_ref, bbr_ref, bbi_ref, ctr_ref, cti_ref, lr_ref, li_ref, d_ref, gw_ref, gb_ref, o_ref, xr_ref, xi_ref, carry):
        @pl.when(pl.program_id(0) == 0)
        def _():
            carry[...] = jnp.zeros_like(carry)

        u = u_ref[...]
        ub = _b(u)
        xr_ref[...] = _dot_nt(ub, _b(bbr_ref[...]))
        xi_ref[...] = _dot_nt(ub, _b(bbi_ref[...]))
        _s5_scan(xr_ref, xi_ref, carry, lr_ref[...], li_ref[...], tb // 8, reverse=False)
        y = _s5_output(u, xr_ref[...], xi_ref[...], ctr_ref[...], cti_ref[...], d_ref[...])
        gl = _gelu(y)
        o_ref[...] = _b(gl * jax.nn.sigmoid(_dot(_b(gl), _b(gw_ref[...])) + gb_ref[...]))

    state = pl.BlockSpec((tb, S5_N), lambda i: (i, 0))
    return _call(
        body, name="s5_fwd", grid=(t // tb,),
        out_shape=[jax.ShapeDtypeStruct((t, GROUP_W), bf16), jax.ShapeDtypeStruct((t, S5_N), f32), jax.ShapeDtypeStruct((t, S5_N), f32)],
        in_specs=[pl.BlockSpec((tb, GROUP_W), lambda i: (i, C_S5)), _full((S5_N, GROUP_W)), _full((S5_N, GROUP_W)),
                  _full((GROUP_W, S5_N)), _full((GROUP_W, S5_N)), _full((1, S5_N)), _full((1, S5_N)),
                  _full((1, GROUP_W)), _full((GROUP_W, GROUP_W)), _full((1, GROUP_W))],
        out_specs=[pl.BlockSpec((tb, GROUP_W), lambda i: (i, 0)), state, state],
        scratch_shapes=[pltpu.VMEM((2, 8, S5_N), f32)],
        semantics=("arbitrary",), xchg=xchg, args=(proj, bbr, bbi, ctr, cti, lre, lim, d, glu_w, glu_b))


def _s5_bwd(proj, dyd, xr_all, xi_all, bbr, bbi, ctr, cti, lre, lim, d, glu_w, glu_b, tb, xchg=None):
    t = proj.shape[0]
    nt = t // tb

    def body(u_ref, dy_ref, xr_ref, xi_ref, xrh_ref, xih_ref, bbr_ref, bbi_ref, ctr_ref, cti_ref, lr_ref, li_ref,
             d_ref, gw_ref, gb_ref,
             du_ref, dlr_ref, dli_ref, dbbr_ref, dbbi_ref, dctr_ref, dcti_ref, dd_ref, dgw_ref, dgb_ref,
             gr_ref, gi_ref, carry):
        i = pl.program_id(0)
        ti = nt - 1 - i

        @pl.when(i == 0)
        def _():
            carry[...] = jnp.zeros_like(carry)

        u = u_ref[...]
        ub = _b(u)
        xr, xi = xr_ref[...], xi_ref[...]
        ctr, cti = _b(ctr_ref[...]), _b(cti_ref[...])
        d = d_ref[...]
        gw = _b(gw_ref[...])
        y = _s5_output(u, xr, xi, ctr, cti, d)
        gl = _gelu(y)
        sg = jax.nn.sigmoid(_dot(_b(gl), gw) + gb_ref[...])
        dout = dy_ref[...]
        q = dout * gl * sg * (1.0 - sg)
        qb = _b(q)
        dgl = dout * sg + _dot_nt(qb, gw)
        _acc(dgw_ref, _dot_tn(_b(gl), qb))
        _acc(dgb_ref, _colsum(q))
        dyv = dgl * _dgelu(y)
        _acc(dd_ref, _colsum(dyv * u))
        dyb = _b(dyv)
        gr_ref[...] = _dot(dyb, ctr)
        gi_ref[...] = -_dot(dyb, cti)
        _acc(dctr_ref, _dot_tn(dyb, _b(xr)))
        _acc(dcti_ref, -_dot_tn(dyb, _b(xi)))
        _s5_scan(gr_ref, gi_ref, carry, lr_ref[...], -li_ref[...], tb // 8, reverse=True)
        gr, gi = gr_ref[...], gi_ref[...]
        xpr = jnp.concatenate([jnp.where(ti > 0, xrh_ref[...], 0.0), xr], axis=0)[7:7 + tb]
        xpi = jnp.concatenate([jnp.where(ti > 0, xih_ref[...], 0.0), xi], axis=0)[7:7 + tb]
        _acc(dlr_ref, _colsum(gr * xpr + gi * xpi))
        _acc(dli_ref, _colsum(gi * xpr - gr * xpi))
        grb, gib = _b(gr), _b(gi)
        _acc(dbbr_ref, _dot_tn(grb, ub))
        _acc(dbbi_ref, _dot_tn(gib, ub))
        du_ref[...] = _b(dyv * d + _dot(grb, _b(bbr_ref[...])) + _dot(gib, _b(bbi_ref[...])))

    state = pl.BlockSpec((tb, S5_N), lambda i: (nt - 1 - i, 0))
    prev = pl.BlockSpec((8, S5_N), lambda i: (jnp.maximum((nt - 1 - i) * (tb // 8) - 1, 0), 0))
    tile = pl.BlockSpec((tb, GROUP_W), lambda i: (nt - 1 - i, 0))
    return _call(
        body, name="s5_bwd", grid=(nt,),
        out_shape=[jax.ShapeDtypeStruct((t, GROUP_W), bf16), jax.ShapeDtypeStruct((1, S5_N), f32), jax.ShapeDtypeStruct((1, S5_N), f32),
                   jax.ShapeDtypeStruct((S5_N, GROUP_W), f32), jax.ShapeDtypeStruct((S5_N, GROUP_W), f32),
                   jax.ShapeDtypeStruct((GROUP_W, S5_N), f32), jax.ShapeDtypeStruct((GROUP_W, S5_N), f32),
                   jax.ShapeDtypeStruct((1, GROUP_W), f32), jax.ShapeDtypeStruct((GROUP_W, GROUP_W), f32),
                   jax.ShapeDtypeStruct((1, GROUP_W), f32)],
        in_specs=[pl.BlockSpec((tb, GROUP_W), lambda i: (nt - 1 - i, C_S5)), tile, state, state, prev, prev,
                  _full((S5_N, GROUP_W)), _full((S5_N, GROUP_W)), _full((GROUP_W, S5_N)), _full((GROUP_W, S5_N)),
                  _full((1, S5_N)), _full((1, S5_N)), _full((1, GROUP_W)), _full((GROUP_W, GROUP_W)), _full((1, GROUP_W))],
        out_specs=[tile, _full((1, S5_N)), _full((1, S5_N)), _full((S5_N, GROUP_W)), _full((S5_N, GROUP_W)),
                   _full((GROUP_W, S5_N)), _full((GROUP_W, S5_N)), _full((1, GROUP_W)), _full((GROUP_W, GROUP_W)), _full((1, GROUP_W))],
        scratch_shapes=[pltpu.VMEM((tb, S5_N), f32), pltpu.VMEM((tb, S5_N), f32), pltpu.VMEM((2, 8, S5_N), f32)],
        semantics=("arbitrary",), xchg=xchg,
        args=(proj, dyd, xr_all, xi_all, xr_all, xi_all, bbr, bbi, ctr, cti, lre, lim, d, glu_w, glu_b))


def _outproj_fwd(ys, h, bn_w, g1, w_out, tb):
    t = h.shape[0]

    def body(ya_ref, yb_ref, yc_ref, yd_ref, h_ref, bn_ref, g1_ref, w_ref, h1_ref, o_ref, gr_ref):
        bn = bn_ref[...]
        parts = []
        for g, r in enumerate((ya_ref, yb_ref, yc_ref, yd_ref)):
            n, _ = _rms(r[...].astype(f32))
            parts.append(n * bn[:, g * GROUP_W:(g + 1) * GROUP_W])
        groups = _b(jnp.concatenate(parts, axis=1))
        gr_ref[...] = groups
        o = _dot(groups, w_ref[...])
        o_ref[...] = _b(o)
        h1_ref[...] = h_ref[...] + g1_ref[...] * o

    grp = pl.BlockSpec((tb, GROUP_W), lambda i: (i, 0))
    row = pl.BlockSpec((tb, D_MODEL), lambda i: (i, 0))
    vec = _full((1, D_MODEL))
    return pl.pallas_call(
        body, name="outproj_fwd", grid=(t // tb,),
        out_shape=[jax.ShapeDtypeStruct((t, D_MODEL), f32), jax.ShapeDtypeStruct((t, D_MODEL), bf16),
                   jax.ShapeDtypeStruct((t, D_MODEL), bf16)],
        in_specs=[grp, grp, grp, grp, row, vec, vec, _full((D_MODEL, D_MODEL))],
        out_specs=[row, row, row],
        compiler_params=_params(("parallel",)),
    )(*ys, h, bn_w, g1, w_out)


def _outproj_bwd(dh1, o, ys, bn_w, g1, w_out, tb):
    t = dh1.shape[0]

    def body(dh_ref, o_ref, ya_ref, yb_ref, yc_ref, yd_ref, bn_ref, g1_ref, w_ref,
             da_ref, db_ref, dc_ref, dd_ref, do_ref, dg1_ref, dbn_ref):
        dh = dh_ref[...]
        _acc(dg1_ref, _colsum(dh * o_ref[...].astype(f32)))
        do = _b(dh * g1_ref[...])
        do_ref[...] = do
        dgroups = _dot_nt(do, w_ref[...])
        bn = bn_ref[...]
        dbn = []
        for g, (r, dr) in enumerate(((ya_ref, da_ref), (yb_ref, db_ref), (yc_ref, dc_ref), (yd_ref, dd_ref))):
            n, rr = _rms(r[...].astype(f32))
            dgr = dgroups[:, g * GROUP_W:(g + 1) * GROUP_W]
            dbn.append(_colsum(dgr * n))
            dr[...] = _rms_bwd(dgr * bn[:, g * GROUP_W:(g + 1) * GROUP_W], n, rr)
        _acc(dbn_ref, jnp.concatenate(dbn, axis=1))

    grp = pl.BlockSpec((tb, GROUP_W), lambda i: (i, 0))
    row = pl.BlockSpec((tb, D_MODEL), lambda i: (i, 0))
    vec = _full((1, D_MODEL))
    return pl.pallas_call(
        body, name="outproj_bwd", grid=(t // tb,),
        out_shape=[jax.ShapeDtypeStruct((t, GROUP_W), f32)] * 4 + [jax.ShapeDtypeStruct((t, D_MODEL), bf16),
                   jax.ShapeDtypeStruct((1, D_MODEL), f32), jax.ShapeDtypeStruct((1, D_MODEL), f32)],
        in_specs=[row, row, grp, grp, grp, grp, vec, vec, _full((D_MODEL, D_MODEL))],
        out_specs=[grp, grp, grp, grp, row, vec, vec],
        compiler_params=_params(("arbitrary",)),
    )(dh1, o, *ys, bn_w, g1, w_out)


def _mlp_fwd(h1, norm_w, sc, sh, g2, w1, w2, tb, xchg=None):
    t = h1.shape[0]
    nh = w1.shape[0] // MLP_SLABS

    def body(h_ref, nw_ref, sc_ref, sh_ref, g2_ref, w1_ref, w2_ref, h2_ref, m_ref, v_ref, r_ref, acc):
        j = pl.program_id(1)

        @pl.when(j == 0)
        def _():
            n, _ = _rms(h_ref[...])
            v_ref[...] = _b(n * nw_ref[...] * (1.0 + sc_ref[...]) + sh_ref[...])

        v = v_ref[...]
        p = None
        for s in range(MLP_SLABS):
            ra = jnp.maximum(_dot(v, w1_ref[s]), 0.0)
            r = _b(ra * ra)
            r_ref[:, s * MLP_HB:(s + 1) * MLP_HB] = r
            q = _dot(r, w2_ref[s])
            p = q if p is None else p + q

        @pl.when(j == 0)
        def _():
            acc[...] = p

        @pl.when(j != 0)
        def _():
            acc[...] += p

        @pl.when(j == nh - 1)
        def _():
            m = acc[...]
            m_ref[...] = _b(m)
            h2_ref[...] = h_ref[...] + g2_ref[...] * m

    row = pl.BlockSpec((tb, D_MODEL), lambda i, j: (i, 0))
    hid = pl.BlockSpec((tb, MLP_SLABS * MLP_HB), lambda i, j: (i, j))
    vec = _full((1, D_MODEL))
    return _call(
        body, name="mlp_fwd", grid=(t // tb, nh),
        out_shape=[jax.ShapeDtypeStruct((t, D_MODEL), f32), jax.ShapeDtypeStruct((t, D_MODEL), bf16),
                   jax.ShapeDtypeStruct((t, D_MODEL), bf16), jax.ShapeDtypeStruct((t, N_DEV * MLP_HB), bf16)],
        in_specs=[row, vec, vec, vec, vec, pl.BlockSpec((MLP_SLABS, D_MODEL, MLP_HB), lambda i, j: (j, 0, 0)),
                  pl.BlockSpec((MLP_SLABS, MLP_HB, D_MODEL), lambda i, j: (j, 0, 0))],
        out_specs=[row, row, row, hid],
        scratch_shapes=[pltpu.VMEM((tb, D_MODEL), f32)],
        semantics=("arbitrary", "arbitrary"), xchg=xchg, args=(h1, norm_w, sc, sh, g2, w1, w2))


def _mlp_bwd(dh2, m, h1, r, norm_w, sc, sh, g2, w1, w2, tb, xchg=None):
    t = h1.shape[0]
    slabs = MLP_BWD_SLABS
    nh = w1.shape[0] // slabs

    def body(dh_ref, m_ref, h_ref, r_ref, nw_ref, sc_ref, sh_ref, g2_ref, w1_ref, w2_ref,
             dh1_ref, do_ref, da_ref, dg2_ref, dsh_ref, dsc_ref, dnw_ref, acc):
        j = pl.program_id(1)

        @pl.when(j == 0)
        def _():
            dh = dh_ref[...]
            _acc(dg2_ref, _colsum(dh * m_ref[...].astype(f32)))
            do_ref[...] = _b(dh * g2_ref[...])

        do = do_ref[...]
        p = None
        for s in range(slabs):
            cols = slice(s * MLP_HB, (s + 1) * MLP_HB)
            dr = _dot_nt(do, w2_ref[s])
            da = _b(dr * 2.0 * jnp.sqrt(r_ref[:, cols].astype(f32)))
            da_ref[:, cols] = da
            q = _dot_nt(da, w1_ref[s])
            p = q if p is None else p + q

        @pl.when(j == 0)
        def _():
            acc[...] = p

        @pl.when(j != 0)
        def _():
            acc[...] += p

        @pl.when(j == nh - 1)
        def _():
            dv = acc[...]
            n, r = _rms(h_ref[...])
            nw = nw_ref[...]
            gain = 1.0 + sc_ref[...]
            _acc(dsh_ref, _colsum(dv))
            _acc(dsc_ref, _colsum(dv * n * nw))
            _acc(dnw_ref, _colsum(dv * gain * n))
            dh1_ref[...] = dh_ref[...] + _rms_bwd(dv * nw * gain, n, r)

    row = pl.BlockSpec((tb, D_MODEL), lambda i, j: (i, 0))
    hid = pl.BlockSpec((tb, slabs * MLP_HB), lambda i, j: (i, j))
    vec = _full((1, D_MODEL))
    once = dict(pipeline_mode=pl.Buffered(1)) if nh == 1 else {}
    return _call(
        body, name="mlp_bwd", grid=(t // tb, nh),
        out_shape=[jax.ShapeDtypeStruct((t, D_MODEL), f32), jax.ShapeDtypeStruct((t, D_MODEL), bf16),
                   jax.ShapeDtypeStruct((t, N_DEV * MLP_HB), bf16)] + [jax.ShapeDtypeStruct((1, D_MODEL), f32)] * 4,
        in_specs=[row, row, row, hid, vec, vec, vec, vec,
                  pl.BlockSpec((slabs, D_MODEL, MLP_HB), lambda i, j: (j, 0, 0), **once),
                  pl.BlockSpec((slabs, MLP_HB, D_MODEL), lambda i, j: (j, 0, 0), **once)],
        out_specs=[row, row, hid, vec, vec, vec, vec],
        scratch_shapes=[pltpu.VMEM((tb, D_MODEL), f32)],
        semantics=("arbitrary", "arbitrary"), xchg=xchg, args=(dh2, m, h1, r, norm_w, sc, sh, g2, w1, w2))


def _loss_head(h, target, norm_w, tb):
    t = h.shape[0]

    def body(h_ref, t_ref, w_ref, loss_ref, dh_ref, dw_ref):
        n, r = _rms(h_ref[...])
        w = w_ref[...]
        err = n * w - t_ref[...]
        part = 0.5 * jnp.sum(jnp.sum(err * err, axis=1, keepdims=True), axis=0, keepdims=True) / D_MODEL
        _acc(loss_ref, jnp.broadcast_to(part, (8, LANES)))
        dy = err / D_MODEL
        _acc(dw_ref, _colsum(dy * n))
        dh_ref[...] = _rms_bwd(dy * w, n, r)

    row = pl.BlockSpec((tb, D_MODEL), lambda i: (i, 0))
    return pl.pallas_call(
        body, name="loss_head", grid=(t // tb,),
        out_shape=[jax.ShapeDtypeStruct((8, LANES), f32), jax.ShapeDtypeStruct((t, D_MODEL), f32),
                   jax.ShapeDtypeStruct((1, D_MODEL), f32)],
        in_specs=[row, row, _full((1, D_MODEL))],
        out_specs=[_full((8, LANES)), row, _full((1, D_MODEL))],
        compiler_params=_params(("arbitrary",)),
    )(h, target, norm_w)


def _adam_math(w, g, m, v):
    m2 = ADAM_B1 * m + (1.0 - ADAM_B1) * g
    v2 = ADAM_B2 * v + (1.0 - ADAM_B2) * (g * g)
    mh = m2 / (1.0 - ADAM_B1 ** ADAM_STEP)
    vh = v2 / (1.0 - ADAM_B2 ** ADAM_STEP)
    return -ADAM_LR * (mh / (jnp.sqrt(vh) + ADAM_EPS) + ADAM_WD * w), m2, v2


def _adamw_small(ws, gs, ms, vs):
    n = len(ws)
    shapes = [w.shape for w in ws]
    as2d = [(1,) + s if len(s) == 1 else s for s in shapes]
    flat = [x.reshape(s) for group in (ws, gs, ms, vs) for x, s in zip(group, as2d)]

    def body(*refs):
        w_refs, g_refs, m_refs, v_refs, outs = refs[:n], refs[n:2 * n], refs[2 * n:3 * n], refs[3 * n:4 * n], refs[4 * n:]
        for i in range(n):
            d, m2, v2 = _adam_math(w_refs[i][...], g_refs[i][...], m_refs[i][...], v_refs[i][...])
            outs[3 * i][...] = d
            outs[3 * i + 1][...] = m2
            outs[3 * i + 2][...] = v2

    res = pl.pallas_call(body, name="adamw_small", out_shape=[jax.ShapeDtypeStruct(s, f32) for s in as2d for _ in range(3)],
                         in_specs=[VMEM] * (4 * n), out_specs=[VMEM] * (3 * n), compiler_params=_params())(*flat)
    return [r.reshape(shapes[i // 3]) for i, r in enumerate(res)]


def _sum_adamw_layers(parts0, parts1, w, m, v, name, rb):
    n_src, r, c = parts0.shape
    nb = r // rb

    def body(p0_ref, p1_ref, w_ref, m_ref, v_ref, g_ref, d_ref, m2_ref, v2_ref):
        def update(p_ref):
            g = p_ref[0].astype(f32)
            for s in range(1, n_src):
                g = g + p_ref[s].astype(f32)
            g_ref[0] = g
            d, m2, v2 = _adam_math(w_ref[0], g, m_ref[0], v_ref[0])
            d_ref[0] = d
            m2_ref[0] = m2
            v2_ref[0] = v2

        @pl.when(pl.program_id(0) == 0)
        def _():
            update(p0_ref)

        @pl.when(pl.program_id(0) == 1)
        def _():
            update(p1_ref)

    blk = pl.BlockSpec((1, rb, c), lambda l, i: (l, i, 0))
    return pl.pallas_call(
        body, name=name, grid=(2, nb),
        out_shape=[jax.ShapeDtypeStruct((2, r, c), f32)] * 4,
        in_specs=[pl.BlockSpec((n_src, rb, c), lambda l, i: (0, jnp.where(l == 0, i, nb - 1), 0)),
                  pl.BlockSpec((n_src, rb, c), lambda l, i: (0, jnp.where(l == 1, i, 0), 0)), blk, blk, blk],
        out_specs=[blk] * 4,
        compiler_params=_params(("arbitrary", "arbitrary")),
    )(parts0, parts1, w, m, v)


def _reorder_in(w):
    pad = jnp.zeros(w.shape[:-1] + (P_IN - 2308,), w.dtype)
    return jnp.concatenate([w[..., :2048], w[..., 2052:2308], w[..., 2048:2052], pad], axis=-1)


def _unreorder_in(w):
    return jnp.concatenate([w[..., :2048], w[..., 2304:2308], w[..., 2048:2304]], axis=-1)


def _block_diag(w2d, n_blocks):
    rows, cols = w2d.shape
    tiled = jnp.tile(w2d, (1, n_blocks))
    rb = lax.broadcasted_iota(jnp.int32, tiled.shape, 0) // (rows // n_blocks)
    cb = lax.broadcasted_iota(jnp.int32, tiled.shape, 1) // cols
    return jnp.where(rb == cb, tiled, jnp.zeros_like(tiled))


def _block_diag_extract(w_bd, n_blocks):
    rows, wide = w_bd.shape
    r, c = rows // n_blocks, wide // n_blocks
    w4 = w_bd.reshape(n_blocks, r, n_blocks, c)
    idx = jnp.arange(n_blocks)
    return w4[idx, :, idx, :]


def _lanes128(v):
    return jnp.pad(v.reshape(1, -1), ((0, 0), (0, LANES - v.size)))


def _rows_of(shape):
    n = 1
    for d in shape:
        n *= d
    return -(-n // (8 * LANES)) * 8, n


def _flat_pack(arrs, row_multiple=8):
    blocks = []
    for a in arrs:
        rows, n = _rows_of(a.shape)
        blocks.append(jnp.pad(a.reshape(-1), (0, rows * LANES - n)).reshape(rows, LANES))
    total = sum(b.shape[0] for b in blocks)
    pad = -total % row_multiple
    if pad:
        blocks.append(jnp.zeros((pad, LANES), blocks[0].dtype))
    return jnp.concatenate(blocks, axis=0)


def _flat_unpack(packed, shapes):
    out, off = [], 0
    for s in shapes:
        rows, n = _rows_of(s)
        out.append(packed[off:off + rows].reshape(-1)[:n].reshape(s))
        off += rows
    return out


_W_NAMES = ['norm_mix_w', 'norm_mlp_w', 'ada_w', 'ada_b', 'w_in', 'pool_w', 'pool_scale', 'sconv_w', 'ssd_conv_w',
            'ssd_conv_b', 'ssd_dt_bias', 'ssd_a_log', 'ssd_d', 's5_a_re', 's5_a_im', 's5_log_step', 's5_b_re', 's5_b_im',
            's5_c_re', 's5_c_im', 's5_d', 's5_glu_w', 's5_glu_b', 'branch_norm_w', 'w_out', 'mlp_w1', 'mlp_w2',
            'final_norm_w']
_BIG = ('ada_w', 'w_in', 'w_out', 'mlp_w1', 'mlp_w2')
_SMALL = [n for n in _W_NAMES if n not in _BIG]
_SHARDED_SMALL = {'sconv_w': (2, 32), 'ssd_conv_w': (2, 96), 's5_glu_w': (1, 32)}


def _gather(*blocks):
    return _ChipGather(blocks)


def _scatter(*parts):
    return _Exchange(parts, gather=False)


def _layer_forward(l, h, p, w, sh_b, tb):
    first = l == 0
    (proj, u_b), got = _inproj_fwd(h, p['norm_mix_w'][l], p['sc1'][l], p['sh1'][l], w['w_in', l], tb,
                                   xchg=_gather(sh_b[1][0]) if first else None)
    if first:
        w['w_out', 0] = got[0].reshape(D_MODEL, D_MODEL)
    ya = _pool_fwd(proj, p['pool_bd'][l], p['pool_scale'][l], tb)
    yb = _sconv_fwd(proj, p['sconv_w8'][l], tb)
    (yc, yc_pre, states), got = _ssd_fwd(proj, p['ssd_conv_w8'][l], p['ssd_conv_b'][l], p['ssd_dt_bias'][l], p['ssd_a_log'][l],
                                         p['ssd_d_cols'][l], tb, xchg=_gather(sh_b[2][0]) if first else None)
    if first:
        w['w1', 0] = got[0]
    (yd, xr, xi), got = _s5_fwd(proj, p['bbr'][l], p['bbi'][l], p['ctr'][l], p['cti'][l], p['lre'][l], p['lim'][l],
                                p['s5_d'][l], p['glu_w'][l], p['glu_b'][l], tb, xchg=_gather(sh_b[3][0]) if first else None)
    if first:
        w['w2', 0] = got[0]
    ys = (ya, yb, yc, yd)
    h1, o, groups_b = _outproj_fwd(ys, h, p['branch_norm_w'][l], p['g1'][l], w['w_out', l], tb)
    (h2, m, v_b, r_b), got = _mlp_fwd(h1, p['norm_mlp_w'][l], p['sc2'][l], p['sh2'][l], p['g2'][l], w['w1', l], w['w2', l],
                                      min(MLP_TB, h.shape[0]), xchg=_gather(*[sh_b[k][1] for k in range(4)]) if first else None)
    if first:
        w['w_in', 1] = got[0].reshape(D_MODEL, P_IN)
        w['w_out', 1] = got[1].reshape(D_MODEL, D_MODEL)
        w['w1', 1], w['w2', 1] = got[2], got[3]
    saved = dict(h=h, proj=proj, u_b=u_b, ys=ys, yc_pre=yc_pre, states=states, xr=xr, xi=xi, h1=h1, o=o,
                 groups_b=groups_b, m=m, v_b=v_b, r_b=r_b)
    return h2, saved


def _layer_backward(l, dh2, s, p, w, pending, recv, tb):
    def carry(names):
        names = [n for n in names if n in pending]
        return names, (_scatter(*[pending.pop(n) for n in names]) if names else None)

    def landed(names, got):
        for n, g in zip(names, got):
            recv[n] = g

    names, xchg = carry([('w_out', 1)])
    (dh1, do2_b, da_b, dg2, dsh2, dsc2, dnw_mlp), got = _mlp_bwd(dh2, s['m'], s['h1'], s['r_b'], p['norm_mlp_w'][l], p['sc2'][l],
                                                                p['sh2'][l], p['g2'][l], w['w1', l], w['w2', l], min(TB_BWD, tb),
                                                                xchg=xchg)
    landed(names, got)
    pending['mlp_w2', l] = _wgrad(s['r_b'], do2_b, 1, "wgrad_w2", tm=1024, tk=2048).reshape(N_DEV, MLP_HB, D_MODEL)
    pending['mlp_w1', l] = _wgrad(s['v_b'], da_b, N_DEV, "wgrad_w1", tm=1024, tk=2048)
    dya, dyb, dyc, dyd, do1_b, dg1, dbn = _outproj_bwd(dh1, s['o'], s['ys'], p['branch_norm_w'][l], p['g1'][l], w['w_out', l], tb)
    pending['w_out', l] = _wgrad(s['groups_b'], do1_b, 1, "wgrad_wout", tm=1024, tk=1024).reshape(N_DEV, D_MODEL // N_DEV, D_MODEL)
    proj = s['proj']
    dv, dpool_bd, dpool_scale = _pool_bwd(proj, dya, p['pool_bd'][l], p['pool_scale'][l], tb)
    dgb, dgc, dhh, dsconv = _sconv_bwd(proj, dyb, p['sconv_w8'][l], tb)
    names, xchg = carry([('mlp_w1', l)] + ([('w_out', 0)] if l == 0 else []))
    (dz, dxs, dbm, dcm, ddt, dconv_w, dconv_b, ddtb, dalog, ddskip), got = _ssd_bwd(
        proj, dyc, s['yc_pre'], s['states'], p['ssd_conv_w8'][l], p['ssd_conv_b'][l], p['ssd_dt_bias'][l], p['ssd_a_log'][l],
        p['ssd_d_cols'][l], min(TB_BWD, tb), xchg=xchg)
    landed(names, got)
    names, xchg = carry([('mlp_w2', l)])
    (du5, dlr, dli, dbbr, dbbi, dctr, dcti, dd5, dgw, dgb5), got = _s5_bwd(
        proj, dyd, s['xr'], s['xi'], p['bbr'][l], p['bbi'][l], p['ctr'][l], p['cti'][l], p['lre'][l], p['lim'][l],
        p['s5_d'][l], p['glu_w'][l], p['glu_b'][l], min(TB_BWD, tb), xchg=xchg)
    landed(names, got)
    dare, daim, dls, dbre_bd, dbim_bd = _s5_prep_bwd(p['are_c'][l], p['aim_c'][l], p['ls_c'][l], p['bre_bd'][l], p['bim_bd'][l],
                                                     dlr.reshape(S5_N, 1), dli.reshape(S5_N, 1), dbbr, dbbi)
    dparts = (dv, dgb, dgc, dhh, dz, dxs, dbm, dcm, du5, ddt)
    pending['w_in', l] = _wgrad_parts(s['u_b'], dparts, "wgrad_win", tm=1024, tk=1024).reshape(N_DEV, D_MODEL // N_DEV, P_IN)
    names, xchg = carry([('w_in', l)])
    (dh, dsh1, dsc1, dnw_mix), got = _inproj_bwd(dparts, dh1, s['h'], p['norm_mix_w'][l], p['sc1'][l], p['sh1'][l], w['w_in', l],
                                                 min(TB_BWD, tb), xchg=xchg)
    landed(names, got)
    small = {
        'norm_mix_w': dnw_mix.reshape(D_MODEL), 'norm_mlp_w': dnw_mlp.reshape(D_MODEL),
        'ada_b': jnp.concatenate([dsh1, dsc1, dg1, dsh2, dsc2, dg2], axis=1).reshape(6 * D_MODEL),
        'pool_w': _block_diag_extract(dpool_bd, 4), 'pool_scale': dpool_scale.reshape(GROUP_W),
        'sconv_w': dsconv[0:3], 'ssd_conv_w': dconv_w[0:4], 'ssd_conv_b': dconv_b.reshape(768),
        'ssd_dt_bias': ddtb[0, 0:4], 'ssd_a_log': dalog[0, 0:4], 'ssd_d': ddskip[0, 0:4],
        's5_a_re': dare.reshape(16, 64), 's5_a_im': daim.reshape(16, 64), 's5_log_step': dls[0:16, 0],
        's5_b_re': _block_diag_extract(dbre_bd, 16), 's5_b_im': _block_diag_extract(dbim_bd, 16),
        's5_c_re': _block_diag_extract(dctr, 16), 's5_c_im': _block_diag_extract(dcti, 16),
        's5_d': dd5.reshape(GROUP_W), 's5_glu_w': dgw, 's5_glu_b': dgb5.reshape(GROUP_W),
        'branch_norm_w': dbn.reshape(D_MODEL),
    }
    return dh, small


def _prepare_params(a, me):
    pack_shapes = [(1, D_MODEL), (2, 3, 32), (2, 4, 96), (2, 32, GROUP_W)]
    packed = _flat_pack([a['c'], a['sconv_w'], a['ssd_conv_w'], a['s5_glu_w']])
    gathered = _gather_rows(packed, "gather_small")
    pieces = [_flat_unpack(gathered[d], pack_shapes) for d in range(N_DEV)]
    c_all = jnp.concatenate([pc[0] for pc in pieces], axis=0)
    sconv_full = jnp.concatenate([pc[1] for pc in pieces], axis=2)
    ssd_conv_full = jnp.concatenate([pc[2] for pc in pieces], axis=2)
    glu_full = jnp.concatenate([pc[3] for pc in pieces], axis=1)

    ada_b_cols = lax.dynamic_slice_in_dim(a['ada_b'], me * 768, 768, axis=1).reshape(2, 1, 768)
    cond, modrows = _ada_forward(c_all, a['ada_w'], ada_b_cols)
    mod_recv = _all_to_all_rows(modrows.transpose(1, 0, 2), "exchange_mod")
    mod = mod_recv.transpose(1, 0, 2).reshape(2, 6 * D_MODEL)
    p = {'cond': cond}
    for k, name in enumerate(('sh1', 'sc1', 'g1', 'sh2', 'sc2', 'g2')):
        p[name] = mod[:, k * D_MODEL:(k + 1) * D_MODEL].reshape(2, 1, D_MODEL)

    for name in ('norm_mix_w', 'norm_mlp_w', 'branch_norm_w'):
        p[name] = a[name].reshape(2, 1, D_MODEL)
    p['pool_bd'] = jnp.stack([_block_diag(a['pool_w'][l].reshape(GROUP_W, 64), 4) for l in range(2)])
    p['pool_scale'] = a['pool_scale'].reshape(2, 1, GROUP_W)
    p['sconv_w8'] = jnp.pad(sconv_full, ((0, 0), (0, 5), (0, 0)))
    p['ssd_conv_w8'] = jnp.pad(ssd_conv_full, ((0, 0), (0, 4), (0, 0)))
    p['ssd_conv_b'] = a['ssd_conv_b'].reshape(2, 1, 768)
    p['ssd_dt_bias'] = jnp.pad(a['ssd_dt_bias'], ((0, 0), (0, LANES - 4))).reshape(2, 1, LANES)
    p['ssd_a_log'] = jnp.pad(a['ssd_a_log'], ((0, 0), (0, LANES - 4))).reshape(2, 1, LANES)
    p['ssd_d_cols'] = jnp.repeat(a['ssd_d'], SSD_P, axis=1).reshape(2, 1, GROUP_W)
    p['are_c'] = a['s5_a_re'].reshape(2, S5_N, 1)
    p['aim_c'] = a['s5_a_im'].reshape(2, S5_N, 1)
    p['ls_c'] = jnp.repeat(a['s5_log_step'], 64, axis=1).reshape(2, S5_N, 1)
    p['bre_bd'] = jnp.stack([_block_diag(a['s5_b_re'][l].reshape(S5_N, 16), 16) for l in range(2)])
    p['bim_bd'] = jnp.stack([_block_diag(a['s5_b_im'][l].reshape(S5_N, 16), 16) for l in range(2)])
    p['ctr'] = jnp.stack([_block_diag(a['s5_c_re'][l].reshape(GROUP_W, 64), 16) for l in range(2)])
    p['cti'] = jnp.stack([_block_diag(a['s5_c_im'][l].reshape(GROUP_W, 64), 16) for l in range(2)])
    p['s5_d'] = a['s5_d'].reshape(2, 1, GROUP_W)
    p['glu_w'] = glu_full
    p['glu_b'] = a['s5_glu_b'].reshape(2, 1, GROUP_W)
    lre, lim, bbr, bbi = [], [], [], []
    for l in range(2):
        r = _s5_prep(p['are_c'][l], p['aim_c'][l], p['ls_c'][l], p['bre_bd'][l], p['bim_bd'][l])
        lre.append(r[0].reshape(1, S5_N))
        lim.append(r[1].reshape(1, S5_N))
        bbr.append(r[2])
        bbi.append(r[3])
    p['lre'], p['lim'], p['bbr'], p['bbi'] = lre, lim, bbr, bbi
    return p


def kernel(x, c, norm_mix_w, norm_mlp_w, ada_w, ada_b, w_in, pool_w, pool_scale, sconv_w, ssd_conv_w, ssd_conv_b, ssd_dt_bias, ssd_a_log, ssd_d, s5_a_re, s5_a_im, s5_log_step, s5_b_re, s5_b_im, s5_c_re, s5_c_im, s5_d, s5_glu_w, s5_glu_b, branch_norm_w, w_out, mlp_w1, mlp_w2, final_norm_w, loss_target, m_norm_mix_w, m_norm_mlp_w, m_ada_w, m_ada_b, m_w_in, m_pool_w, m_pool_scale, m_sconv_w, m_ssd_conv_w, m_ssd_conv_b, m_ssd_dt_bias, m_ssd_a_log, m_ssd_d, m_s5_a_re, m_s5_a_im, m_s5_log_step, m_s5_b_re, m_s5_b_im, m_s5_c_re, m_s5_c_im, m_s5_d, m_s5_glu_w, m_s5_glu_b, m_branch_norm_w, m_w_out, m_mlp_w1, m_mlp_w2, m_final_norm_w, v_norm_mix_w, v_norm_mlp_w, v_ada_w, v_ada_b, v_w_in, v_pool_w, v_pool_scale, v_sconv_w, v_ssd_conv_w, v_ssd_conv_b, v_ssd_dt_bias, v_ssd_a_log, v_ssd_d, v_s5_a_re, v_s5_a_im, v_s5_log_step, v_s5_b_re, v_s5_b_im, v_s5_c_re, v_s5_c_im, v_s5_d, v_s5_glu_w, v_s5_glu_b, v_branch_norm_w, v_w_out, v_mlp_w1, v_mlp_w2, v_final_norm_w):
    a = dict(locals())
    t = x.shape[1]
    tb = min(TB, t)
    me = _my_index()
    p = _prepare_params(a, me)

    sh_b = _cast_shards([_reorder_in(w_in), w_out, mlp_w1, mlp_w2])
    w = {('w_in', 0): _exchange_alone(_gather(sh_b[0][0]), "gather_w_in0")[0].reshape(D_MODEL, P_IN)}

    h = x.reshape(t, D_MODEL)
    saved = []
    for l in range(2):
        h, s = _layer_forward(l, h, p, w, sh_b, tb)
        saved.append(s)
    loss_blk, dh, dfinal = _loss_head(h, loss_target.reshape(t, D_MODEL), final_norm_w.reshape(1, D_MODEL), tb)

    pending, recv, small_parts = {}, {}, [None, None]
    for l in (1, 0):
        dh, small_parts[l] = _layer_backward(l, dh, saved[l], p, w, pending, recv, tb)
    grad_x = dh.reshape(1, t, D_MODEL)

    grads, deltas, new_m, new_v = {}, {}, {}, {}

    wmv_in = [_reorder_in(a[n]) for n in ('w_in', 'm_w_in', 'v_w_in')]
    outs = _sum_adamw_layers(recv['w_in', 0], recv['w_in', 1], *wmv_in, "adamw_w_in", 128)
    grads['w_in'], deltas['w_in'], new_m['w_in'], new_v['w_in'] = [_unreorder_in(o) for o in outs]
    for name, rb in (('w_out', 128), ('mlp_w1', 256), ('mlp_w2', 256)):
        grads[name], deltas[name], new_m[name], new_v[name] = _sum_adamw_layers(
            recv[name, 0], recv[name, 1], a[name], a['m_' + name], a['v_' + name], "adamw_" + name, rb)

    dmod = jnp.stack([small_parts[0]['ada_b'], small_parts[1]['ada_b']])
    dmod_recv = _all_to_all_rows(dmod.reshape(2, N_DEV, 768).transpose(1, 0, 2), "exchange_dmod")
    g_ada = _ada_backward(p['cond'], dmod_recv.transpose(1, 0, 2))
    grads['ada_w'], deltas['ada_w'], new_m['ada_w'], new_v['ada_w'] = _sum_adamw_layers(
        g_ada[0:1], g_ada[1:2], ada_w, m_ada_w, v_ada_w, "adamw_ada_w", 256)

    layered = [n for n in _SMALL if n != 'final_norm_w']
    full = [jnp.stack([small_parts[0][n], small_parts[1][n]]) for n in layered] + [dfinal.reshape(D_MODEL)]
    full.append(loss_blk[0:1, 0:1])
    full_shapes = [f.shape for f in full]
    summed = _flat_unpack(_allreduce_rows(_flat_pack(full, row_multiple=64)), full_shapes)
    loss = summed[-1].reshape(())
    local = []
    for n, g in zip(_SMALL, summed):
        if n in _SHARDED_SMALL:
            axis, size = _SHARDED_SMALL[n]
            g = lax.dynamic_slice_in_dim(g, me * size, size, axis=axis)
        local.append(g.reshape(a[n].shape))
    outs = _adamw_small([a[n] for n in _SMALL], local, [a['m_' + n] for n in _SMALL], [a['v_' + n] for n in _SMALL])
    for i, n in enumerate(_SMALL):
        grads[n], deltas[n], new_m[n], new_v[n] = local[i], outs[3 * i], outs[3 * i + 1], outs[3 * i + 2]

    return (loss, grad_x, *[grads[n] for n in _W_NAMES], *[deltas[n] for n in _W_NAMES],
            *[new_m[n] for n in _W_NAMES], *[new_v[n] for n in _W_NAMES])
```

```python
import functools

import jax
import jax.numpy as jnp
from jax import lax
from jax.experimental import pallas as pl
from jax.experimental.pallas import tpu as pltpu

f32 = jnp.float32
bf16 = jnp.bfloat16

N_DEV = 8
D_MODEL = 1024
GROUP_W = 256
P_IN = 2432
DT_COL = 2304
SSD_CHUNK = 128
SSD_HEADS = 4
SSD_P = 64
S5_N = 1024
MLP_HB = 512
TB = 1024
TB_BWD = 512
MLP_TB = 1024
MLP_SLABS = 2
MLP_BWD_SLABS = 8
EPS = 1e-6
LANES = 128
VMEM_LIMIT = 56 * 1024 * 1024
ADAM_LR, ADAM_B1, ADAM_B2, ADAM_EPS, ADAM_WD, ADAM_STEP = 0.001, 0.9, 0.999, 1e-08, 0.01, 10
POOL_WINDOWS = (2, 4, 8, 16)

C_POOL, C_GB, C_GC, C_HH, C_Z, C_XS, C_BM, C_CM, C_S5 = range(9)
C_DT128 = DT_COL // LANES

MESH = pl.DeviceIdType.MESH
ANY = pl.BlockSpec(memory_space=pl.ANY)
VMEM = pl.BlockSpec(memory_space=pltpu.VMEM)


def _dot(a, b):
    return jnp.dot(a, b, preferred_element_type=f32)


def _dot_nt(a, b):
    return lax.dot_general(a, b, (((1,), (1,)), ((), ())), preferred_element_type=f32)


def _dot_tn(a, b):
    return lax.dot_general(a, b, (((0,), (0,)), ((), ())), preferred_element_type=f32)


def _dot_exact(a, b):
    return jnp.dot(a, b, preferred_element_type=f32, precision=lax.Precision.HIGHEST)


def _b(x):
    return x.astype(bf16)


def _silu(x):
    return x * jax.nn.sigmoid(x)


def _dsilu(x):
    s = jax.nn.sigmoid(x)
    return s * (1.0 + x * (1.0 - s))


def _softplus(x):
    return jnp.maximum(x, 0.0) + jnp.log1p(jnp.exp(-jnp.abs(x)))


_GELU_K = 0.7978845608028654
_GELU_C = 0.044715


def _gelu(x):
    return 0.5 * x * (1.0 + jnp.tanh(_GELU_K * (x + _GELU_C * x * x * x)))


def _dgelu(x):
    th = jnp.tanh(_GELU_K * (x + _GELU_C * x * x * x))
    return 0.5 * (1.0 + th) + 0.5 * x * (1.0 - th * th) * _GELU_K * (1.0 + 3.0 * _GELU_C * x * x)


def _rms(h):
    r = lax.rsqrt(jnp.mean(h * h, axis=-1, keepdims=True) + EPS)
    return h * r, r


def _rms_bwd(dn, n, r):
    return r * (dn - n * jnp.mean(dn * n, axis=-1, keepdims=True))


def _colsum(x):
    return jnp.sum(x, axis=0, keepdims=True)


def _params(sem=None):
    return pltpu.CompilerParams(dimension_semantics=sem, vmem_limit_bytes=VMEM_LIMIT)


def _full(shape):
    return pl.BlockSpec(shape, lambda *_: (0,) * len(shape))


def _acc(ref, val):
    @pl.when(pl.program_id(0) == 0)
    def _():
        ref[...] = val

    @pl.when(pl.program_id(0) != 0)
    def _():
        ref[...] += val


def _me():
    return lax.axis_index("x"), lax.axis_index("y"), lax.axis_index("c")


def _my_index():
    x, y, c = _me()
    return 4 * x + 2 * y + c


def _coords(p):
    return (p // 4, (p // 2) % 2, p % 2)


class _Exchange:
    def __init__(self, srcs, gather):
        self.srcs = list(srcs)
        self.gather = gather
        self.n = len(self.srcs)
        self.out_shape = [jax.ShapeDtypeStruct(((N_DEV,) + s.shape) if gather else s.shape, s.dtype) for s in self.srcs]
        self.scratch = [pltpu.SemaphoreType.DMA((self.n, N_DEV)), pltpu.SemaphoreType.DMA((self.n, N_DEV)),
                        pltpu.SemaphoreType.DMA((self.n,))]

    def _src(self, refs, t, dev):
        return refs[t] if self.gather else refs[t].at[dev]

    def _remote(self, xin, xout, sems, t, k, me, to):
        return pltpu.make_async_remote_copy(
            src_ref=self._src(xin, t, to), dst_ref=xout[t].at[me], send_sem=sems[0].at[t, k], recv_sem=sems[1].at[t, k],
            device_id=_coords(to), device_id_type=MESH)

    def start(self, xin, xout, sems):
        me = _my_index()
        for t in range(self.n):
            pltpu.make_async_copy(self._src(xin, t, me), xout[t].at[me], sems[2].at[t]).start()
            for k in range(1, N_DEV):
                self._remote(xin, xout, sems, t, k, me, (me + k) % N_DEV).start()

    def wait(self, xin, xout, sems):
        me = _my_index()
        for t in range(self.n):
            for k in range(1, N_DEV):
                src = (me + N_DEV - k) % N_DEV
                pltpu.make_async_remote_copy(
                    src_ref=self._src(xin, t, src), dst_ref=xout[t].at[src], send_sem=sems[0].at[t, k],
                    recv_sem=sems[1].at[t, k], device_id=_coords(src), device_id_type=MESH).wait_recv()
        for t in range(self.n):
            for k in range(1, N_DEV):
                self._remote(xin, xout, sems, t, k, me, (me + k) % N_DEV).wait_send()
            pltpu.make_async_copy(self._src(xin, t, me), xout[t].at[me], sems[2].at[t]).wait()

    def forward(self, xin, xout, sems):
        pass


class _ChipGather:
    def __init__(self, srcs):
        self.srcs = list(srcs)
        self.n = len(self.srcs)
        self.out_shape = [jax.ShapeDtypeStruct((N_DEV,) + s.shape, s.dtype) for s in self.srcs]
        self.scratch = [pltpu.SemaphoreType.DMA((self.n, 7)), pltpu.SemaphoreType.DMA((self.n, 7)),
                        pltpu.SemaphoreType.DMA((self.n,))]

    @staticmethod
    def _places():
        x, y, c = _me()
        chips = [(1 - x, y), (x, 1 - y), (1 - x, 1 - y)]
        return (x, y, c), (x, y, 1 - c), chips

    @staticmethod
    def _slab(ref, dev):
        return ref.at[4 * dev[0] + 2 * dev[1] + dev[2]]

    def _copy(self, xin, xout, sems, t, k, block, to, src=None):
        return pltpu.make_async_remote_copy(
            src_ref=self._slab(xout[t], block) if src is None else src, dst_ref=self._slab(xout[t], block),
            send_sem=sems[0].at[t, k], recv_sem=sems[1].at[t, k], device_id=to, device_id_type=MESH)

    def start(self, xin, xout, sems):
        me, sibling, chips = self._places()
        for t in range(self.n):
            pltpu.make_async_copy(xin[t], self._slab(xout[t], me), sems[2].at[t]).start()
            self._copy(xin, xout, sems, t, 0, me, sibling, src=xin[t]).start()
            for j, chip in enumerate(chips):
                self._copy(xin, xout, sems, t, 1 + j, me, (*chip, me[2]), src=xin[t]).start()

    def forward(self, xin, xout, sems):
        me, sibling, chips = self._places()
        for t in range(self.n):
            for j, chip in enumerate(chips):
                self._copy(xin, xout, sems, t, 1 + j, (*chip, me[2]), me).wait_recv()
                self._copy(xin, xout, sems, t, 4 + j, (*chip, me[2]), sibling).start()

    def wait(self, xin, xout, sems):
        me, sibling, chips = self._places()
        for t in range(self.n):
            self._copy(xin, xout, sems, t, 0, sibling, me).wait_recv()
            for j, chip in enumerate(chips):
                self._copy(xin, xout, sems, t, 4 + j, (*chip, 1 - me[2]), me).wait_recv()
        for t in range(self.n):
            self._copy(xin, xout, sems, t, 0, me, sibling, src=xin[t]).wait_send()
            for j, chip in enumerate(chips):
                self._copy(xin, xout, sems, t, 1 + j, me, (*chip, me[2]), src=xin[t]).wait_send()
                self._copy(xin, xout, sems, t, 4 + j, (*chip, me[2]), sibling).wait_send()
            pltpu.make_async_copy(xin[t], self._slab(xout[t], me), sems[2].at[t]).wait()


def _call(body, *, name, grid, in_specs, out_specs, out_shape, args, semantics, scratch_shapes=(), xchg=None):
    if xchg is None:
        outs = pl.pallas_call(body, name=name, grid=grid, in_specs=in_specs, out_specs=out_specs, out_shape=out_shape,
                              scratch_shapes=list(scratch_shapes), compiler_params=_params(semantics))(*args)
        return outs, ()
    n_in, n_out, n_scr, n = len(in_specs), len(out_specs), len(scratch_shapes), xchg.n

    def carried(*refs):
        ins, xin = refs[:n_in], refs[n_in:n_in + n]
        outs, xout = refs[n_in + n:n_in + n + n_out], refs[n_in + n + n_out:n_in + 2 * n + n_out]
        scr, sems = refs[n_in + 2 * n + n_out:n_in + 2 * n + n_out + n_scr], refs[n_in + 2 * n + n_out + n_scr:]
        step = pl.program_id(0)
        for d in range(1, len(grid)):
            step = step * grid[d] + pl.program_id(d)
        n_steps = functools.reduce(lambda a, b: a * b, grid)

        @pl.when(step == 0)
        def _():
            xchg.start(xin, xout, sems)

        @pl.when(step == (2 * n_steps) // 3)
        def _():
            xchg.forward(xin, xout, sems)

        body(*ins, *outs, *scr)

        @pl.when(step == n_steps - 1)
        def _():
            xchg.wait(xin, xout, sems)

    res = pl.pallas_call(
        carried, name=name, grid=grid, in_specs=list(in_specs) + [ANY] * n, out_specs=list(out_specs) + [ANY] * n,
        out_shape=list(out_shape) + xchg.out_shape, scratch_shapes=list(scratch_shapes) + xchg.scratch,
        compiler_params=_params(("arbitrary",) * len(grid)))(*args, *xchg.srcs)
    return res[:n_out], tuple(res[n_out:])


def _exchange_alone(xchg, name):
    def body(*refs):
        xin, xout, sems = refs[:xchg.n], refs[xchg.n:2 * xchg.n], refs[2 * xchg.n:]
        xchg.start(xin, xout, sems)
        xchg.forward(xin, xout, sems)
        xchg.wait(xin, xout, sems)

    return pl.pallas_call(body, name=name, out_shape=xchg.out_shape, in_specs=[ANY] * xchg.n, out_specs=[ANY] * xchg.n,
                          scratch_shapes=xchg.scratch)(*xchg.srcs)


def _cast_shards(shards):
    n = len(shards)

    def body(*refs):
        for i, o in zip(refs[:n], refs[n:]):
            o[...] = i[...].astype(bf16)

    return pl.pallas_call(body, name="cast_shards", out_shape=[jax.ShapeDtypeStruct(s.shape, bf16) for s in shards],
                          in_specs=[VMEM] * n, out_specs=[VMEM] * n, compiler_params=_params())(*shards)


def _allreduce_rows(v):
    r = v.shape[0]
    rp = r // N_DEV

    def body(v_ref, o_ref, parts, sums, send1, recv1, send2, recv2):
        me = _my_index()

        def piece(ref, d):
            return ref.at[pl.ds(pl.multiple_of(d * rp, 8), rp), :]

        def copy1(k, src_dev, to):
            return pltpu.make_async_remote_copy(src_ref=piece(v_ref, to), dst_ref=parts.at[src_dev], send_sem=send1.at[k],
                                                recv_sem=recv1.at[k], device_id=_coords(to), device_id_type=MESH)

        def copy2(k, owner, to):
            return pltpu.make_async_remote_copy(src_ref=sums, dst_ref=piece(o_ref, owner), send_sem=send2.at[k],
                                                recv_sem=recv2.at[k], device_id=_coords(to), device_id_type=MESH)

        for k in range(1, N_DEV):
            copy1(k, me, (me + k) % N_DEV).start()
        parts[me] = v_ref[pl.ds(pl.multiple_of(me * rp, 8), rp), :]
        for k in range(1, N_DEV):
            copy1(k, (me + N_DEV - k) % N_DEV, me).wait_recv()
        total = parts[0]
        for s in range(1, N_DEV):
            total = total + parts[s]
        sums[...] = total
        o_ref[pl.ds(pl.multiple_of(me * rp, 8), rp), :] = total
        for k in range(1, N_DEV):
            copy2(k, me, (me + k) % N_DEV).start()
        for k in range(1, N_DEV):
            copy2(k, (me + N_DEV - k) % N_DEV, me).wait_recv()
        for k in range(1, N_DEV):
            copy1(k, me, (me + k) % N_DEV).wait_send()
            copy2(k, me, (me + k) % N_DEV).wait_send()

    return pl.pallas_call(
        body, name="allreduce_small_grads", out_shape=jax.ShapeDtypeStruct(v.shape, v.dtype),
        in_specs=[VMEM], out_specs=VMEM,
        scratch_shapes=[pltpu.VMEM((N_DEV, rp, LANES), f32), pltpu.VMEM((rp, LANES), f32)]
        + [pltpu.SemaphoreType.DMA((N_DEV,))] * 4,
        compiler_params=_params(),
    )(v)


def _gather_rows(v, name):
    def body(v_ref, o_ref, send_sems, recv_sems):
        me = _my_index()
        o_ref[me] = v_ref[...]
        sends = []
        for k in range(1, N_DEV):
            peer = (me + k) % N_DEV
            rc = pltpu.make_async_remote_copy(src_ref=v_ref, dst_ref=o_ref.at[me], send_sem=send_sems.at[k],
                                              recv_sem=recv_sems.at[k], device_id=_coords(peer), device_id_type=MESH)
            rc.start()
            sends.append(rc)
        for k in range(1, N_DEV):
            src = (me + N_DEV - k) % N_DEV
            pltpu.make_async_remote_copy(src_ref=v_ref, dst_ref=o_ref.at[src], send_sem=send_sems.at[k],
                                         recv_sem=recv_sems.at[k], device_id=_coords(src), device_id_type=MESH).wait_recv()
        for rc in sends:
            rc.wait_send()

    return pl.pallas_call(
        body, name=name, out_shape=jax.ShapeDtypeStruct((N_DEV,) + v.shape, v.dtype),
        in_specs=[VMEM], out_specs=VMEM,
        scratch_shapes=[pltpu.SemaphoreType.DMA((N_DEV,)), pltpu.SemaphoreType.DMA((N_DEV,))],
        compiler_params=pltpu.CompilerParams(vmem_limit_bytes=VMEM_LIMIT),
    )(v)


def _all_to_all_rows(v, name):
    def body(v_ref, o_ref, send_sems, recv_sems):
        me = _my_index()
        o_ref[me] = v_ref[me]
        sends = []
        for k in range(1, N_DEV):
            peer = (me + k) % N_DEV
            rc = pltpu.make_async_remote_copy(src_ref=v_ref.at[peer], dst_ref=o_ref.at[me], send_sem=send_sems.at[k],
                                              recv_sem=recv_sems.at[k], device_id=_coords(peer), device_id_type=MESH)
            rc.start()
            sends.append(rc)
        for k in range(1, N_DEV):
            src = (me + N_DEV - k) % N_DEV
            pltpu.make_async_remote_copy(src_ref=v_ref.at[src], dst_ref=o_ref.at[src], send_sem=send_sems.at[k],
                                         recv_sem=recv_sems.at[k], device_id=_coords(src), device_id_type=MESH).wait_recv()
        for rc in sends:
            rc.wait_send()

    return pl.pallas_call(
        body, name=name, out_shape=jax.ShapeDtypeStruct(v.shape, v.dtype),
        in_specs=[VMEM], out_specs=VMEM,
        scratch_shapes=[pltpu.SemaphoreType.DMA((N_DEV,)), pltpu.SemaphoreType.DMA((N_DEV,))],
    )(v)


def _ada_forward(c_all, ada_w, ada_b_cols):
    def body(c_ref, w_ref, b_ref, cond_ref, o_ref):
        cond = _silu(c_ref[...])
        cond_ref[...] = cond
        for l in range(2):
            o_ref[l] = _dot(_b(cond), _b(w_ref[l])) + b_ref[l]

    return pl.pallas_call(
        body, name="ada_forward",
        out_shape=[jax.ShapeDtypeStruct((N_DEV, D_MODEL), f32), jax.ShapeDtypeStruct((2, N_DEV, 768), f32)],
        in_specs=[VMEM] * 3, out_specs=[VMEM] * 2, compiler_params=_params(),
    )(c_all, ada_w, ada_b_cols)


def _ada_backward(cond, dmod_rows):
    def body(c_ref, d_ref, o_ref):
        cb = _b(c_ref[...])
        for l in range(2):
            o_ref[l] = _dot_tn(cb, _b(d_ref[l]))

    return pl.pallas_call(
        body, name="ada_backward", out_shape=jax.ShapeDtypeStruct((2, D_MODEL, 768), f32),
        in_specs=[VMEM] * 2, out_specs=VMEM, compiler_params=_params(),
    )(cond, dmod_rows)


def _inproj_fwd(h, norm_w, sc, sh, w_in, tb, xchg=None):
    t = h.shape[0]

    def body(h_ref, nw_ref, sc_ref, sh_ref, w_ref, proj_ref, u_ref):
        n, _ = _rms(h_ref[...])
        u = _b(n * nw_ref[...] * (1.0 + sc_ref[...]) + sh_ref[...])
        u_ref[...] = u
        proj_ref[...] = _dot(u, w_ref[...])

    row = pl.BlockSpec((tb, D_MODEL), lambda i: (i, 0))
    vec = _full((1, D_MODEL))
    return _call(
        body, name="inproj_fwd", grid=(t // tb,),
        out_shape=[jax.ShapeDtypeStruct((t, P_IN), f32), jax.ShapeDtypeStruct((t, D_MODEL), bf16)],
        in_specs=[row, vec, vec, vec, _full((D_MODEL, P_IN))],
        out_specs=[pl.BlockSpec((tb, P_IN), lambda i: (i, 0)), row],
        semantics=("parallel",), args=(h, norm_w, sc, sh, w_in), xchg=xchg)


def _inproj_bwd(dparts, dh_res, h, norm_w, sc, sh, w_in, tb, xchg=None):
    t = h.shape[0]

    def body(*refs):
        parts = refs[:10]
        dres_ref, h_ref, nw_ref, sc_ref, sh_ref, w_ref = refs[10:16]
        dh_ref, dsh_ref, dsc_ref, dnw_ref = refs[16:]
        dproj = jnp.concatenate([p[...] for p in parts], axis=1)
        du = _dot_nt(dproj, w_ref[...])
        n, r = _rms(h_ref[...])
        nw = nw_ref[...]
        gain = 1.0 + sc_ref[...]
        _acc(dsh_ref, _colsum(du))
        _acc(dsc_ref, _colsum(du * n * nw))
        _acc(dnw_ref, _colsum(du * gain * n))
        dh_ref[...] = dres_ref[...] + _rms_bwd(du * nw * gain, n, r)

    row = pl.BlockSpec((tb, D_MODEL), lambda i: (i, 0))
    vec = _full((1, D_MODEL))
    part_specs = [pl.BlockSpec((tb, GROUP_W), lambda i: (i, 0))] * 9 + [pl.BlockSpec((tb, LANES), lambda i: (i, 0))]
    return _call(
        body, name="inproj_bwd", grid=(t // tb,),
        out_shape=[jax.ShapeDtypeStruct((t, D_MODEL), f32)] + [jax.ShapeDtypeStruct((1, D_MODEL), f32)] * 3,
        in_specs=part_specs + [row, row, vec, vec, vec,
                               pl.BlockSpec((D_MODEL, P_IN), lambda i: (0, 0), pipeline_mode=pl.Buffered(1))],
        out_specs=[row, vec, vec, vec],
        semantics=("arbitrary",), xchg=xchg, args=(*dparts, dh_res, h, norm_w, sc, sh, w_in))


def _wgrad(a, b, n_blocks, name, tm, tk=512):
    t, m = a.shape
    nb = b.shape[1] // n_blocks
    tk = min(tk, t)
    nk = t // tk

    def body(a_ref, b_ref, o_ref, acc_ref):
        k = pl.program_id(2)
        p = _dot_tn(a_ref[...], b_ref[...])

        @pl.when(k == 0)
        def _():
            acc_ref[...] = p

        @pl.when(k != 0)
        def _():
            acc_ref[...] += p

        @pl.when(k == nk - 1)
        def _():
            o_ref[0] = acc_ref[...].astype(o_ref.dtype)

    return pl.pallas_call(
        body, name=name, grid=(m // tm, n_blocks, nk),
        out_shape=jax.ShapeDtypeStruct((n_blocks, m, nb), bf16),
        in_specs=[pl.BlockSpec((tk, tm), lambda i, j, k: (k, i)), pl.BlockSpec((tk, nb), lambda i, j, k: (k, j))],
        out_specs=pl.BlockSpec((1, tm, nb), lambda i, j, k: (j, i, 0)),
        scratch_shapes=[pltpu.VMEM((tm, nb), f32)],
        compiler_params=_params(("parallel", "parallel", "arbitrary")),
    )(a, b)


def _wgrad_parts(a, parts, name, tm, tk):
    t, m = a.shape
    n = sum(p.shape[1] for p in parts)
    n_parts = len(parts)
    tk = min(tk, t)
    nk = t // tk

    def body(*refs):
        a_ref, part_refs, o_ref, acc_ref = refs[0], refs[1:1 + n_parts], refs[1 + n_parts], refs[2 + n_parts]
        k = pl.program_id(1)
        p = _dot_tn(a_ref[...], jnp.concatenate([r[...] for r in part_refs], axis=1))

        @pl.when(k == 0)
        def _():
            acc_ref[...] = p

        @pl.when(k != 0)
        def _():
            acc_ref[...] += p

        @pl.when(k == nk - 1)
        def _():
            o_ref[...] = acc_ref[...].astype(o_ref.dtype)

    return pl.pallas_call(
        body, name=name, grid=(m // tm, nk),
        out_shape=jax.ShapeDtypeStruct((m, n), bf16),
        in_specs=[pl.BlockSpec((tk, tm), lambda i, k: (k, i))]
        + [pl.BlockSpec((tk, p.shape[1]), lambda i, k: (k, 0)) for p in parts],
        out_specs=pl.BlockSpec((tm, n), lambda i, k: (i, 0)),
        scratch_shapes=[pltpu.VMEM((tm, n), f32)],
        compiler_params=_params(("parallel", "arbitrary")),
    )(a, *parts)


def _pool_counts(rows, t0):
    tpos = (lax.broadcasted_iota(jnp.int32, (rows, GROUP_W), 0) + t0 + 1).astype(f32)
    grp = lax.broadcasted_iota(jnp.int32, (rows, GROUP_W), 1) // 64
    win = jnp.where(grp == 0, 2.0, jnp.where(grp == 1, 4.0, jnp.where(grp == 2, 8.0, 16.0)))
    return jnp.minimum(tpos, win), grp


def _pool_select(grp, l1, l2, l3, l4):
    return jnp.where(grp == 0, l1, jnp.where(grp == 1, l2, jnp.where(grp == 2, l3, l4)))


def _pool_means(v, halo, t0):
    tb = v.shape[0]
    ext = jnp.concatenate([halo, v], axis=0)
    n = tb + 16
    s1 = ext[1:n] + ext[0:n - 1]
    s2 = s1[2:n - 1] + s1[0:n - 3]
    s3 = s2[4:n - 3] + s2[0:n - 7]
    s4 = s3[8:n - 7] + s3[0:n - 15]
    cnt, grp = _pool_counts(tb, t0)
    wsum = _pool_select(grp, s1[15:15 + tb], s2[13:13 + tb], s3[9:9 + tb], s4[1:1 + tb])
    return wsum / cnt - v


def _pool_fwd(proj, pw_bd, scale, tb):
    t = proj.shape[0]

    def body(v_ref, vh_ref, pw_ref, sc_ref, o_ref):
        i = pl.program_id(0)
        halo = jnp.where(i > 0, vh_ref[...], 0.0)
        p = _pool_means(v_ref[...], halo, i * tb)
        o_ref[...] = _b(_dot(_b(p), _b(pw_ref[...])) * sc_ref[...])

    return pl.pallas_call(
        body, name="pool_fwd", grid=(t // tb,),
        out_shape=jax.ShapeDtypeStruct((t, GROUP_W), bf16),
        in_specs=[pl.BlockSpec((tb, GROUP_W), lambda i: (i, C_POOL)),
                  pl.BlockSpec((16, GROUP_W), lambda i: (jnp.maximum(i * (tb // 16) - 1, 0), C_POOL)),
                  _full((GROUP_W, GROUP_W)), _full((1, GROUP_W))],
        out_specs=pl.BlockSpec((tb, GROUP_W), lambda i: (i, 0)),
        compiler_params=_params(("parallel",)),
    )(proj, proj, pw_bd, scale)


def _pool_bwd(proj, dy, pw_bd, scale, tb):
    t = proj.shape[0]
    nt = t // tb
    last16 = t // 16 - 1

    def body(v_ref, vh_ref, dy_ref, dyh_ref, pw_ref, sc_ref, dv_ref, dpw_ref, dsc_ref):
        i = pl.program_id(0)
        halo = jnp.where(i > 0, vh_ref[...], 0.0)
        p = _pool_means(v_ref[...], halo, i * tb)
        pw = _b(pw_ref[...])
        sc = sc_ref[...]
        dy = dy_ref[...]
        ypre = _dot(_b(p), pw)
        _acc(dsc_ref, _colsum(dy * ypre))
        dys = _b(dy * sc)
        _acc(dpw_ref, _dot_tn(_b(p), dys))
        dp = _dot_nt(dys, pw)
        dph = _dot_nt(_b(jnp.where(i < nt - 1, dyh_ref[...], 0.0) * sc), pw)
        cnt, grp = _pool_counts(tb, i * tb)
        cnth, _ = _pool_counts(16, (i + 1) * tb)
        ext = jnp.concatenate([dp / cnt, dph / cnth], axis=0)
        n = tb + 16
        f1 = ext[0:n - 1] + ext[1:n]
        f2 = f1[0:n - 3] + f1[2:n - 1]
        f3 = f2[0:n - 7] + f2[4:n - 3]
        f4 = f3[0:n - 15] + f3[8:n - 7]
        dv_ref[...] = _b(_pool_select(grp, f1[0:tb], f2[0:tb], f3[0:tb], f4[0:tb]) - dp)

    return pl.pallas_call(
        body, name="pool_bwd", grid=(nt,),
        out_shape=[jax.ShapeDtypeStruct((t, GROUP_W), bf16), jax.ShapeDtypeStruct((GROUP_W, GROUP_W), f32),
                   jax.ShapeDtypeStruct((1, GROUP_W), f32)],
        in_specs=[pl.BlockSpec((tb, GROUP_W), lambda i: (i, C_POOL)),
                  pl.BlockSpec((16, GROUP_W), lambda i: (jnp.maximum(i * (tb // 16) - 1, 0), C_POOL)),
                  pl.BlockSpec((tb, GROUP_W), lambda i: (i, 0)),
                  pl.BlockSpec((16, GROUP_W), lambda i: (jnp.minimum((i + 1) * (tb // 16), last16), 0)),
                  _full((GROUP_W, GROUP_W)), _full((1, GROUP_W))],
        out_specs=[pl.BlockSpec((tb, GROUP_W), lambda i: (i, 0)), _full((GROUP_W, GROUP_W)), _full((1, GROUP_W))],
        compiler_params=_params(("arbitrary",)),
    )(proj, proj, dy, dy, pw_bd, scale)


def _sconv_fwd(proj, w, tb):
    t = proj.shape[0]

    def body(gb_ref, gc_ref, hh_ref, gch_ref, hhh_ref, w_ref, o_ref):
        i = pl.program_id(0)
        q = gc_ref[...] * hh_ref[...]
        qh = jnp.where(i > 0, gch_ref[...] * hhh_ref[...], 0.0)
        ext = jnp.concatenate([qh, q], axis=0)
        w = w_ref[...]
        conv = w[0:1] * ext[6:6 + tb] + w[1:2] * ext[7:7 + tb] + w[2:3] * ext[8:8 + tb]
        o_ref[...] = _b(gb_ref[...] * conv)

    def col(c):
        return pl.BlockSpec((tb, GROUP_W), lambda i: (i, c))

    def prev(c):
        return pl.BlockSpec((8, GROUP_W), lambda i: (jnp.maximum(i * (tb // 8) - 1, 0), c))

    return pl.pallas_call(
        body, name="sconv_fwd", grid=(t // tb,),
        out_shape=jax.ShapeDtypeStruct((t, GROUP_W), bf16),
        in_specs=[col(C_GB), col(C_GC), col(C_HH), prev(C_GC), prev(C_HH), _full((8, GROUP_W))],
        out_specs=pl.BlockSpec((tb, GROUP_W), lambda i: (i, 0)),
        compiler_params=_params(("parallel",)),
    )(proj, proj, proj, proj, proj, w)


def _sconv_bwd(proj, dy, w, tb):
    t = proj.shape[0]
    nt = t // tb
    last8 = t // 8 - 1

    def body(gb_ref, gc_ref, hh_ref, gch_ref, hhh_ref, gbn_ref, dy_ref, dyn_ref, w_ref, dgb_ref, dgc_ref, dhh_ref, dw_ref):
        i = pl.program_id(0)
        gc, hh, gb, dy = gc_ref[...], hh_ref[...], gb_ref[...], dy_ref[...]
        q = gc * hh
        qh = jnp.where(i > 0, gch_ref[...] * hhh_ref[...], 0.0)
        ext = jnp.concatenate([qh, q], axis=0)
        w = w_ref[...]
        conv = w[0:1] * ext[6:6 + tb] + w[1:2] * ext[7:7 + tb] + w[2:3] * ext[8:8 + tb]
        dgb_ref[...] = _b(dy * conv)
        e = dy * gb
        en = jnp.where(i < nt - 1, dyn_ref[...] * gbn_ref[...], 0.0)
        exte = jnp.concatenate([e, en], axis=0)
        dq = w[2:3] * exte[0:tb] + w[1:2] * exte[1:1 + tb] + w[0:1] * exte[2:2 + tb]
        dgc_ref[...] = _b(dq * hh)
        dhh_ref[...] = _b(dq * gc)
        dw = jnp.concatenate([_colsum(e * ext[6:6 + tb]), _colsum(e * ext[7:7 + tb]), _colsum(e * ext[8:8 + tb]),
                              jnp.zeros((5, GROUP_W), f32)], axis=0)
        _acc(dw_ref, dw)

    def col(c):
        return pl.BlockSpec((tb, GROUP_W), lambda i: (i, c))

    def prev(c):
        return pl.BlockSpec((8, GROUP_W), lambda i: (jnp.maximum(i * (tb // 8) - 1, 0), c))

    def nxt(c):
        return pl.BlockSpec((8, GROUP_W), lambda i: (jnp.minimum((i + 1) * (tb // 8), last8), c))

    out = pl.BlockSpec((tb, GROUP_W), lambda i: (i, 0))
    return pl.pallas_call(
        body, name="sconv_bwd", grid=(nt,),
        out_shape=[jax.ShapeDtypeStruct((t, GROUP_W), bf16)] * 3 + [jax.ShapeDtypeStruct((8, GROUP_W), f32)],
        in_specs=[col(C_GB), col(C_GC), col(C_HH), prev(C_GC), prev(C_HH), nxt(C_GB), col(0), nxt(0), _full((8, GROUP_W))],
        out_specs=[out, out, out, _full((8, GROUP_W))],
        compiler_params=_params(("arbitrary",)),
    )(proj, proj, proj, proj, proj, proj, dy, dy, w)


def _conv4(xr, halo, w, bias):
    tb = xr.shape[0]
    ext = jnp.concatenate([halo, xr], axis=0)
    pre = w[0:1] * ext[5:5 + tb] + w[1:2] * ext[6:6 + tb] + w[2:3] * ext[7:7 + tb] + w[3:4] * ext[8:8 + tb] + bias
    return pre, ext


def _tri():
    r = lax.broadcasted_iota(jnp.int32, (SSD_CHUNK, SSD_CHUNK), 0)
    c = lax.broadcasted_iota(jnp.int32, (SSD_CHUNK, SSD_CHUNK), 1)
    return r >= c


def _lane_pick(vals):
    rows = vals[0].shape[0]
    lane = lax.broadcasted_iota(jnp.int32, (rows, LANES), 1)
    out = jnp.zeros((rows, LANES), f32)
    for h, v in enumerate(vals):
        out = jnp.where(lane == h, v, out)
    return out


def _ssd_fwd(proj, conv_w, conv_b, dt_bias, a_log, d_cols, tb, xchg=None):
    t = proj.shape[0]
    cpt = tb // SSD_CHUNK

    def body(z_ref, xs_ref, bm_ref, cm_ref, xsh_ref, bmh_ref, cmh_ref, dt_ref, cw_ref, cb_ref, dtb_ref, al_ref, dk_ref,
             o_ref, y_ref, st_ref, state):
        i = pl.program_id(0)

        @pl.when(i == 0)
        def _():
            state[...] = jnp.zeros_like(state)

        cw, cb = cw_ref[...], cb_ref[...]
        acts = []
        for j, (r, hr) in enumerate(((xs_ref, xsh_ref), (bm_ref, bmh_ref), (cm_ref, cmh_ref))):
            halo = jnp.where(i > 0, hr[...], 0.0)
            pre, _ = _conv4(r[...], halo, cw[:, j * 256:(j + 1) * 256], cb[:, j * 256:(j + 1) * 256])
            acts.append(_silu(pre))
        xs, bm, cm = acts
        dt = _softplus(dt_ref[...] + dtb_ref[...])
        a = -jnp.exp(al_ref[...])
        adt = dt * a
        tri = _tri()
        trif = tri.astype(f32)
        dk = dk_ref[...]
        for c in range(cpt):
            rows = slice(c * SSD_CHUNK, (c + 1) * SSD_CHUNK)
            acol = _dot_exact(trif, adt[rows])
            arow = acol.T
            dt_c = dt[rows]
            ys = []
            rowi = lax.broadcasted_iota(jnp.int32, (SSD_CHUNK, 1), 0)
            first = lax.broadcasted_iota(jnp.int32, (SSD_CHUNK, SSD_CHUNK), 1) < SSD_P
            for g in range(SSD_HEADS // 2):
                cols = slice(g * 128, (g + 1) * 128)
                cg, bg = _b(cm[rows, cols]), _b(bm[rows, cols])
                xg = xs[rows, cols]
                heads = (2 * g, 2 * g + 1)
                ac = [acol[:, h:h + 1] for h in heads]
                alast = [v[SSD_CHUNK - 1:SSD_CHUNK] for v in ac]
                dtw = jnp.where(first, dt_c[:, heads[0]:heads[0] + 1], dt_c[:, heads[1]:heads[1] + 1])
                eaw = jnp.where(first, jnp.exp(ac[0]), jnp.exp(ac[1]))
                wdw = jnp.where(first, jnp.exp(alast[0] - ac[0]), jnp.exp(alast[1] - ac[1]))
                xdt = xg * dtw
                xb = _b(xdt)
                gmat = _dot_nt(cg, bg)
                ydiag = []
                for k, h in enumerate(heads):
                    lm = jnp.exp(jnp.where(tri, ac[k] - arow[h:h + 1, :], -jnp.inf))
                    ydiag.append(_dot(_b(gmat * lm), xb[:, k * SSD_P:(k + 1) * SSD_P]))
                s_in = state[g]
                st_ref[c, g] = s_in
                ys.append(jnp.concatenate(ydiag, axis=1) + eaw * _dot_nt(cg, _b(s_in)) + xg * dk[:, cols])
                state[g] = jnp.where(rowi < SSD_P, jnp.exp(alast[0]), jnp.exp(alast[1])) * s_in + _dot_tn(_b(xdt * wdw), bg)
            yc = jnp.concatenate(ys, axis=1)
            y_ref[rows, :] = yc
            o_ref[rows, :] = _b(yc * _silu(z_ref[rows, :]))

    def col(c):
        return pl.BlockSpec((tb, GROUP_W), lambda i: (i, c))

    def prev(c):
        return pl.BlockSpec((8, GROUP_W), lambda i: (jnp.maximum(i * (tb // 8) - 1, 0), c))

    out = pl.BlockSpec((tb, GROUP_W), lambda i: (i, 0))
    return _call(
        body, name="ssd_fwd", grid=(t // tb,),
        out_shape=[jax.ShapeDtypeStruct((t, GROUP_W), bf16), jax.ShapeDtypeStruct((t, GROUP_W), f32),
                   jax.ShapeDtypeStruct((t // SSD_CHUNK, 2, 128, 128), f32)],
        in_specs=[col(C_Z), col(C_XS), col(C_BM), col(C_CM), prev(C_XS), prev(C_BM), prev(C_CM),
                  pl.BlockSpec((tb, LANES), lambda i: (i, C_DT128)),
                  _full((8, 768)), _full((1, 768)), _full((1, LANES)), _full((1, LANES)), _full((1, GROUP_W))],
        out_specs=[out, out, pl.BlockSpec((cpt, 2, 128, 128), lambda i: (i, 0, 0, 0))],
        scratch_shapes=[pltpu.VMEM((2, 128, 128), f32)],
        semantics=("arbitrary",), xchg=xchg,
        args=(proj, proj, proj, proj, proj, proj, proj, proj, conv_w, conv_b, dt_bias, a_log, d_cols))


def _ssd_bwd(proj, dyc, y_pre, states, conv_w, conv_b, dt_bias, a_log, d_cols, tb, xchg=None):
    t = proj.shape[0]
    nt = t // tb
    cpt = tb // SSD_CHUNK

    def body(z_ref, xs_ref, bm_ref, cm_ref, xsh_ref, bmh_ref, cmh_ref, dt_ref, dy_ref, yp_ref, st_ref,
             cw_ref, cb_ref, dtb_ref, al_ref, dk_ref,
             dz_ref, dxs_ref, dbm_ref, dcm_ref, ddt_ref, dcw_ref, dcb_ref, ddtb_ref, dal_ref, ddk_ref,
             dstate, carry):
        i = pl.program_id(0)
        ti = nt - 1 - i

        @pl.when(i == 0)
        def _():
            dstate[...] = jnp.zeros_like(dstate)
            carry[...] = jnp.zeros_like(carry)

        cw, cb = cw_ref[...], cb_ref[...]
        pres, exts, acts = [], [], []
        for j, (r, hr) in enumerate(((xs_ref, xsh_ref), (bm_ref, bmh_ref), (cm_ref, cmh_ref))):
            halo = jnp.where(ti > 0, hr[...], 0.0)
            pre, ext = _conv4(r[...], halo, cw[:, j * 256:(j + 1) * 256], cb[:, j * 256:(j + 1) * 256])
            pres.append(pre)
            exts.append(ext)
            acts.append(_silu(pre))
        xs, bm, cm = acts
        raw = dt_ref[...] + dtb_ref[...]
        dt = _softplus(raw)
        a = -jnp.exp(al_ref[...])
        adt = dt * a
        tri = _tri()
        trif = tri.astype(f32)
        dk = dk_ref[...]
        z = z_ref[...]
        dyc = dy_ref[...]
        dz_ref[...] = _b(dyc * yp_ref[...] * _dsilu(z))
        dy_all = dyc * _silu(z)
        lane = lax.broadcasted_iota(jnp.int32, (1, LANES), 1)
        ddk_acc = jnp.zeros((1, LANES), f32)
        dal_acc = jnp.zeros((1, LANES), f32)
        dxs_c, dbm_c, dcm_c, ddt_c = [None] * cpt, [None] * cpt, [None] * cpt, [None] * cpt
        for c in reversed(range(cpt)):
            rows = slice(c * SSD_CHUNK, (c + 1) * SSD_CHUNK)
            acol = _dot_exact(trif, adt[rows])
            arow = acol.T
            dt_c = dt[rows]
            da_cols, da_rows, ddt_heads, dxs_groups, dbg, dcg = [], [], [], [], [], []
            rowi = lax.broadcasted_iota(jnp.int32, (SSD_CHUNK, 1), 0)
            first = lax.broadcasted_iota(jnp.int32, (SSD_CHUNK, SSD_CHUNK), 1) < SSD_P
            for g in range(SSD_HEADS // 2):
                cols = slice(g * 128, (g + 1) * 128)
                cgf, bgf = cm[rows, cols], bm[rows, cols]
                cg, bg = _b(cgf), _b(bgf)
                xg, dyg = xs[rows, cols], dy_all[rows, cols]
                s_in, dsn = st_ref[c, g], dstate[g]
                sb, dsnb = _b(s_in), _b(dsn)
                heads = (2 * g, 2 * g + 1)
                ac = [acol[:, h:h + 1] for h in heads]
                alast = [v[SSD_CHUNK - 1:SSD_CHUNK] for v in ac]
                el = [jnp.exp(v) for v in alast]
                dtw = jnp.where(first, dt_c[:, heads[0]:heads[0] + 1], dt_c[:, heads[1]:heads[1] + 1])
                eaw = jnp.where(first, jnp.exp(ac[0]), jnp.exp(ac[1]))
                wdw = jnp.where(first, jnp.exp(alast[0] - ac[0]), jnp.exp(alast[1] - ac[1]))
                xdt = xg * dtw
                xb, dyb = _b(xdt), _b(dyg)
                gmat = _dot_nt(cg, bg)
                dgs, dxh, da = None, [], []
                for k, h in enumerate(heads):
                    hc = slice(k * SSD_P, (k + 1) * SSD_P)
                    lm = jnp.exp(jnp.where(tri, ac[k] - arow[h:h + 1, :], -jnp.inf))
                    m = gmat * lm
                    dm = _dot_nt(dyb[:, hc], xb[:, hc])
                    dxh.append(_dot_tn(_b(m), dyb[:, hc]))
                    dgs = dm * lm if dgs is None else dgs + dm * lm
                    wm = dm * m
                    da.append(jnp.sum(wm, axis=1, keepdims=True))
                    da_rows.append(jnp.sum(wm, axis=0, keepdims=True))
                dgb = _b(dgs)
                dcg_g = _dot(dgb, bg)
                dbg_g = _dot_tn(dgb, cg)
                yoff = eaw * _dot_nt(cg, sb)
                dyoff = dyg * yoff
                dye = _b(dyg * eaw)
                dcg_g = dcg_g + _dot(dye, sb)
                ds_y = _dot_tn(dye, cg)
                u = _dot_nt(bg, dsnb)
                dx = jnp.concatenate(dxh, axis=1) + wdw * u
                dbg_g = dbg_g + _dot(_b(xdt * wdw), dsnb)
                xu = xdt * u * wdw
                ss = jnp.sum(dsn * s_in, axis=1, keepdims=True)
                dxx = dx * xg
                dyx = _colsum(dyg * xg)
                for k, h in enumerate(heads):
                    mine = first if k == 0 else jnp.logical_not(first)
                    dwv = jnp.sum(jnp.where(mine, xu, 0.0), axis=1, keepdims=True)
                    mine_rows = (rowi < SSD_P) if k == 0 else (rowi >= SSD_P)
                    dalast = jnp.sum(dwv, axis=0, keepdims=True) + el[k] * jnp.sum(jnp.where(mine_rows, ss, 0.0), axis=0, keepdims=True)
                    dah = da[k] + jnp.sum(jnp.where(mine, dyoff, 0.0), axis=1, keepdims=True) - dwv
                    da_cols.append(dah + jnp.where(rowi == SSD_CHUNK - 1, dalast, 0.0))
                    ddt_heads.append(jnp.sum(jnp.where(mine, dxx, 0.0), axis=1, keepdims=True))
                    ddk_acc = ddk_acc + jnp.where(lane == h, jnp.sum(jnp.where(mine[0:1], dyx, 0.0), axis=1, keepdims=True), 0.0)
                dstate[g] = jnp.where(rowi < SSD_P, el[0], el[1]) * dsn + ds_y
                dxs_groups.append(dx * dtw + dyg * dk[:, cols])
                dbg.append(dbg_g)
                dcg.append(dcg_g)
            da_blk = _lane_pick(da_cols)
            rowsel = lax.broadcasted_iota(jnp.int32, (SSD_CHUNK, SSD_CHUNK), 0)
            da_rows_blk = jnp.zeros((SSD_CHUNK, SSD_CHUNK), f32)
            for h in range(SSD_HEADS):
                da_rows_blk = jnp.where(rowsel == h, da_rows[h], da_rows_blk)
            da_blk = da_blk - da_rows_blk.T
            dadt = lax.dot_general(trif, da_blk, (((0,), (0,)), ((), ())), preferred_element_type=f32,
                                   precision=lax.Precision.HIGHEST)
            dal_acc = dal_acc + _colsum(dadt * dt_c)
            ddt_c[c] = dadt * a + _lane_pick(ddt_heads)
            dxs_c[c] = jnp.concatenate(dxs_groups, axis=1)
            dbm_c[c] = jnp.concatenate(dbg, axis=1)
            dcm_c[c] = jnp.concatenate(dcg, axis=1)
        ddt = jnp.concatenate(ddt_c, axis=0) if cpt > 1 else ddt_c[0]
        ddraw = jnp.where(lane < SSD_HEADS, ddt * jax.nn.sigmoid(raw), 0.0)
        ddt_ref[...] = _b(ddraw)
        _acc(ddtb_ref, _colsum(ddraw))
        _acc(dal_ref, jnp.where(lane < SSD_HEADS, dal_acc * a, 0.0))
        _acc(ddk_ref, ddk_acc)
        dcw_parts, dcb_parts = [], []
        for j, (dparts, out_ref) in enumerate(((dxs_c, dxs_ref), (dbm_c, dbm_ref), (dcm_c, dcm_ref))):
            dact = jnp.concatenate(dparts, axis=0) if cpt > 1 else dparts[0]
            dpre = dact * _dsilu(pres[j])
            w = cw[:, j * 256:(j + 1) * 256]
            ext = jnp.concatenate([dpre, carry[:, j * 256:(j + 1) * 256]], axis=0)
            out_ref[...] = _b(w[3:4] * ext[0:tb] + w[2:3] * ext[1:1 + tb] + w[1:2] * ext[2:2 + tb] + w[0:1] * ext[3:3 + tb])
            carry[:, j * 256:(j + 1) * 256] = dpre[0:8]
            xe = exts[j]
            dcw_parts.append(jnp.concatenate([_colsum(dpre * xe[5 + k:5 + k + tb]) for k in range(4)]
                                             + [jnp.zeros((4, GROUP_W), f32)], axis=0))
            dcb_parts.append(_colsum(dpre))
        _acc(dcw_ref, jnp.concatenate(dcw_parts, axis=1))
        _acc(dcb_ref, jnp.concatenate(dcb_parts, axis=1))

    def col(c):
        return pl.BlockSpec((tb, GROUP_W), lambda i: (nt - 1 - i, c))

    def prev(c):
        return pl.BlockSpec((8, GROUP_W), lambda i: (jnp.maximum((nt - 1 - i) * (tb // 8) - 1, 0), c))

    out = pl.BlockSpec((tb, GROUP_W), lambda i: (nt - 1 - i, 0))
    vec = _full((1, LANES))
    return _call(
        body, name="ssd_bwd", grid=(nt,),
        out_shape=[jax.ShapeDtypeStruct((t, GROUP_W), bf16)] * 4 + [jax.ShapeDtypeStruct((t, LANES), bf16),
                   jax.ShapeDtypeStruct((8, 768), f32), jax.ShapeDtypeStruct((1, 768), f32)]
        + [jax.ShapeDtypeStruct((1, LANES), f32)] * 3,
        in_specs=[col(C_Z), col(C_XS), col(C_BM), col(C_CM), prev(C_XS), prev(C_BM), prev(C_CM),
                  pl.BlockSpec((tb, LANES), lambda i: (nt - 1 - i, C_DT128)), out, out,
                  pl.BlockSpec((cpt, 2, 128, 128), lambda i: (nt - 1 - i, 0, 0, 0)),
                  _full((8, 768)), _full((1, 768)), vec, vec, _full((1, GROUP_W))],
        out_specs=[out, out, out, out, pl.BlockSpec((tb, LANES), lambda i: (nt - 1 - i, 0)),
                   _full((8, 768)), _full((1, 768)), vec, vec, vec],
        scratch_shapes=[pltpu.VMEM((2, 128, 128), f32), pltpu.VMEM((8, 768), f32)],
        semantics=("arbitrary",), xchg=xchg,
        args=(proj, proj, proj, proj, proj, proj, proj, proj, dyc, y_pre, states, conv_w, conv_b, dt_bias, a_log, d_cols))


def _s5_coeffs(are, aim, ls):
    step = jnp.exp(ls)
    mag = jnp.exp(are * step)
    th = aim * step
    lre, lim = mag * jnp.cos(th), mag * jnp.sin(th)
    den = are * are + aim * aim
    nr = lre - 1.0
    fre = (nr * are + lim * aim) / den
    fim = (lim * are - nr * aim) / den
    return step, lre, lim, den, fre, fim


def _s5_prep(are, aim, ls, bre_bd, bim_bd):
    def body(are_ref, aim_ref, ls_ref, bre_ref, bim_ref, lre_ref, lim_ref, bbr_ref, bbi_ref):
        _, lre, lim, _, fre, fim = _s5_coeffs(are_ref[...], aim_ref[...], ls_ref[...])
        lre_ref[...] = lre
        lim_ref[...] = lim
        bre, bim = bre_ref[...], bim_ref[...]
        bbr_ref[...] = fre * bre - fim * bim
        bbi_ref[...] = fre * bim + fim * bre

    col = jax.ShapeDtypeStruct((S5_N, 1), f32)
    mat = jax.ShapeDtypeStruct((S5_N, GROUP_W), f32)
    return pl.pallas_call(body, name="s5_prep", out_shape=[col, col, mat, mat], in_specs=[VMEM] * 5, out_specs=[VMEM] * 4,
                          compiler_params=_params())(are, aim, ls, bre_bd, bim_bd)


def _s5_prep_bwd(are, aim, ls, bre_bd, bim_bd, dlre, dlim, dbbr, dbbi):
    def body(are_ref, aim_ref, ls_ref, bre_ref, bim_ref, dlre_ref, dlim_ref, dbbr_ref, dbbi_ref,
             dare_ref, daim_ref, dls_ref, dbre_ref, dbim_ref):
        are, aim = are_ref[...], aim_ref[...]
        step, lre, lim, den, fre, fim = _s5_coeffs(are, aim, ls_ref[...])
        r = lax.broadcasted_iota(jnp.int32, (S5_N, GROUP_W), 0) // 64
        c = lax.broadcasted_iota(jnp.int32, (S5_N, GROUP_W), 1) // 16
        mask = r == c
        gr = jnp.where(mask, dbbr_ref[...], 0.0)
        gi = jnp.where(mask, dbbi_ref[...], 0.0)
        bre, bim = bre_ref[...], bim_ref[...]
        dbre_ref[...] = fre * gr + fim * gi
        dbim_ref[...] = fre * gi - fim * gr
        dfre = jnp.sum(bre * gr + bim * gi, axis=1, keepdims=True)
        dfim = jnp.sum(bre * gi - bim * gr, axis=1, keepdims=True)
        ire, iim = are / den, aim / den
        tre = dlre_ref[...] + ire * dfre - iim * dfim
        tim = dlim_ref[...] + ire * dfim + iim * dfre
        dzre = lre * tre + lim * tim
        dzim = lre * tim - lim * tre
        qre = (fre * are + fim * aim) / den
        qim = (fim * are - fre * aim) / den
        dare_ref[...] = step * dzre - (qre * dfre + qim * dfim)
        daim_ref[...] = step * dzim - (qre * dfim - qim * dfre)
        dls = (are * dzre + aim * dzim) * step
        sel = (lax.broadcasted_iota(jnp.int32, (S5_N, LANES), 0) // 64 == lax.broadcasted_iota(jnp.int32, (S5_N, LANES), 1)).astype(f32)
        dls_ref[...] = lax.dot_general(sel, jnp.broadcast_to(dls, (S5_N, LANES)), (((0,), (0,)), ((), ())),
                                       preferred_element_type=f32, precision=lax.Precision.HIGHEST)

    col = jax.ShapeDtypeStruct((S5_N, 1), f32)
    mat = jax.ShapeDtypeStruct((S5_N, GROUP_W), f32)
    return pl.pallas_call(body, name="s5_prep_bwd", out_shape=[col, col, jax.ShapeDtypeStruct((LANES, LANES), f32), mat, mat],
                          in_specs=[VMEM] * 9, out_specs=[VMEM] * 5, compiler_params=_params(),
                          )(are, aim, ls, bre_bd, bim_bd, dlre, dlim, dbbr, dbbi)


def _cmul(ar, ai, br, bi):
    return ar * br - ai * bi, ar * bi + ai * br


def _s5_scan(re_ref, im_ref, carry_ref, mr, mi, n_groups, reverse):
    p1 = (mr, mi)
    p2 = _cmul(*p1, *p1)
    p3 = _cmul(*p2, *p1)
    p4 = _cmul(*p2, *p2)
    p5 = _cmul(*p4, *p1)
    p6 = _cmul(*p4, *p2)
    p7 = _cmul(*p4, *p3)
    p8 = _cmul(*p4, *p4)
    pows = [p1, p2, p3, p4, p5, p6, p7, p8]
    row = lax.broadcasted_iota(jnp.int32, (8, S5_N), 0)
    tr = jnp.zeros((8, S5_N), f32)
    ti = jnp.zeros((8, S5_N), f32)
    for i in range(8):
        p = pows[7 - i] if reverse else pows[i]
        tr = jnp.where(row == i, p[0], tr)
        ti = jnp.where(row == i, p[1], ti)
    steps = []
    for k, p in ((1, p1), (2, p2), (4, p4)):
        keep = (row + k < 8) if reverse else (row >= k)
        steps.append((8 - k if reverse else k, jnp.where(keep, p[0], 0.0), jnp.where(keep, p[1], 0.0)))
    edge = 0 if reverse else 7

    def step(j, carry):
        cr, ci = carry
        g = (n_groups - 1 - j) if reverse else j
        r0 = pl.multiple_of(g * 8, 8)
        xr = re_ref[pl.ds(r0, 8), :]
        xi = im_ref[pl.ds(r0, 8), :]
        for shift, br, bi in steps:
            sr = pltpu.roll(xr, shift, 0)
            si = pltpu.roll(xi, shift, 0)
            xr, xi = xr + br * sr - bi * si, xi + br * si + bi * sr
        xr, xi = xr + tr * cr - ti * ci, xi + tr * ci + ti * cr
        re_ref[pl.ds(r0, 8), :] = xr
        im_ref[pl.ds(r0, 8), :] = xi
        return (jnp.broadcast_to(xr[edge:edge + 1, :], (8, S5_N)), jnp.broadcast_to(xi[edge:edge + 1, :], (8, S5_N)))

    cr, ci = lax.fori_loop(0, n_groups, step, (carry_ref[0], carry_ref[1]))
    carry_ref[0] = cr
    carry_ref[1] = ci


def _s5_output(u, xr, xi, ctr, cti, d):
    return _dot_nt(_b(xr), _b(ctr)) - _dot_nt(_b(xi), _b(cti)) + d * u


def _s5_fwd(proj, bbr, bbi, ctr, cti, lre, lim, d, glu_w, glu_b, tb, xchg=None):
    t = proj.shape[0]

    def body(u_ref, bbr_ref, bbi_ref, ctr_ref, cti_ref, lr_ref, li_ref, d_ref, gw_ref, gb_ref, o_ref, xr_ref, xi_ref, carry):
        @pl.when(pl.program_id(0) == 0)
        def _():
            carry[...] = jnp.zeros_like(carry)

        u = u_ref[...]
        ub = _b(u)
        xr_ref[...] = _dot_nt(ub, _b(bbr_ref[...]))
        xi_ref[...] = _dot_nt(ub, _b(bbi_ref[...]))
        _s5_scan(xr_ref, xi_ref, carry, lr_ref[...], li_ref[...], tb // 8, reverse=False)
        y = _s5_output(u, xr_ref[...], xi_ref[...], ctr_ref[...], cti_ref[...], d_ref[...])
        gl = _gelu(y)
        o_ref[...] = _b(gl * jax.nn.sigmoid(_dot(_b(gl), _b(gw_ref[...])) + gb_ref[...]))

    state = pl.BlockSpec((tb, S5_N), lambda i: (i, 0))
    return _call(
        body, name="s5_fwd", grid=(t // tb,),
        out_shape=[jax.ShapeDtypeStruct((t, GROUP_W), bf16), jax.ShapeDtypeStruct((t, S5_N), f32), jax.ShapeDtypeStruct((t, S5_N), f32)],
        in_specs=[pl.BlockSpec((tb, GROUP_W), lambda i: (i, C_S5)), _full((S5_N, GROUP_W)), _full((S5_N, GROUP_W)),
                  _full((GROUP_W, S5_N)), _full((GROUP_W, S5_N)), _full((1, S5_N)), _full((1, S5_N)),
                  _full((1, GROUP_W)), _full((GROUP_W, GROUP_W)), _full((1, GROUP_W))],
        out_specs=[pl.BlockSpec((tb, GROUP_W), lambda i: (i, 0)), state, state],
        scratch_shapes=[pltpu.VMEM((2, 8, S5_N), f32)],
        semantics=("arbitrary",), xchg=xchg, args=(proj, bbr, bbi, ctr, cti, lre, lim, d, glu_w, glu_b))


def _s5_bwd(proj, dyd, xr_all, xi_all, bbr, bbi, ctr, cti, lre, lim, d, glu_w, glu_b, tb, xchg=None):
    t = proj.shape[0]
    nt = t // tb

    def body(u_ref, dy_ref, xr_ref, xi_ref, xrh_ref, xih_ref, bbr_ref, bbi_ref, ctr_ref, cti_ref, lr_ref, li_ref,
             d_ref, gw_ref, gb_ref,
             du_ref, dlr_ref, dli_ref, dbbr_ref, dbbi_ref, dctr_ref, dcti_ref, dd_ref, dgw_ref, dgb_ref,
             gr_ref, gi_ref, carry):
        i = pl.program_id(0)
        ti = nt - 1 - i

        @pl.when(i == 0)
        def _():
            carry[...] = jnp.zeros_like(carry)

        u = u_ref[...]
        ub = _b(u)
        xr, xi = xr_ref[...], xi_ref[...]
        ctr, cti = _b(ctr_ref[...]), _b(cti_ref[...])
        d = d_ref[...]
        gw = _b(gw_ref[...])
        y = _s5_output(u, xr, xi, ctr, cti, d)
        gl = _gelu(y)
        sg = jax.nn.sigmoid(_dot(_b(gl), gw) + gb_ref[...])
        dout = dy_ref[...]
        q = dout * gl * sg * (1.0 - sg)
        qb = _b(q)
        dgl = dout * sg + _dot_nt(qb, gw)
        _acc(dgw_ref, _dot_tn(_b(gl), qb))
        _acc(dgb_ref, _colsum(q))
        dyv = dgl * _dgelu(y)
        _acc(dd_ref, _colsum(dyv * u))
        dyb = _b(dyv)
        gr_ref[...] = _dot(dyb, ctr)
        gi_ref[...] = -_dot(dyb, cti)
        _acc(dctr_ref, _dot_tn(dyb, _b(xr)))
        _acc(dcti_ref, -_dot_tn(dyb, _b(xi)))
        _s5_scan(gr_ref, gi_ref, carry, lr_ref[...], -li_ref[...], tb // 8, reverse=True)
        gr, gi = gr_ref[...], gi_ref[...]
        xpr = jnp.concatenate([jnp.where(ti > 0, xrh_ref[...], 0.0), xr], axis=0)[7:7 + tb]
        xpi = jnp.concatenate([jnp.where(ti > 0, xih_ref[...], 0.0), xi], axis=0)[7:7 + tb]
        _acc(dlr_ref, _colsum(gr * xpr + gi * xpi))
        _acc(dli_ref, _colsum(gi * xpr - gr * xpi))
        grb, gib = _b(gr), _b(gi)
        _acc(dbbr_ref, _dot_tn(grb, ub))
        _acc(dbbi_ref, _dot_tn(gib, ub))
        du_ref[...] = _b(dyv * d + _dot(grb, _b(bbr_ref[...])) + _dot(gib, _b(bbi_ref[...])))

    state = pl.BlockSpec((tb, S5_N), lambda i: (nt - 1 - i, 0))
    prev = pl.BlockSpec((8, S5_N), lambda i: (jnp.maximum((nt - 1 - i) * (tb // 8) - 1, 0), 0))
    tile = pl.BlockSpec((tb, GROUP_W), lambda i: (nt - 1 - i, 0))
    return _call(
        body, name="s5_bwd", grid=(nt,),
        out_shape=[jax.ShapeDtypeStruct((t, GROUP_W), bf16), jax.ShapeDtypeStruct((1, S5_N), f32), jax.ShapeDtypeStruct((1, S5_N), f32),
                   jax.ShapeDtypeStruct((S5_N, GROUP_W), f32), jax.ShapeDtypeStruct((S5_N, GROUP_W), f32),
                   jax.ShapeDtypeStruct((GROUP_W, S5_N), f32), jax.ShapeDtypeStruct((GROUP_W, S5_N), f32),
                   jax.ShapeDtypeStruct((1, GROUP_W), f32), jax.ShapeDtypeStruct((GROUP_W, GROUP_W), f32),
                   jax.ShapeDtypeStruct((1, GROUP_W), f32)],
        in_specs=[pl.BlockSpec((tb, GROUP_W), lambda i: (nt - 1 - i, C_S5)), tile, state, state, prev, prev,
                  _full((S5_N, GROUP_W)), _full((S5_N, GROUP_W)), _full((GROUP_W, S5_N)), _full((GROUP_W, S5_N)),
                  _full((1, S5_N)), _full((1, S5_N)), _full((1, GROUP_W)), _full((GROUP_W, GROUP_W)), _full((1, GROUP_W))],
        out_specs=[tile, _full((1, S5_N)), _full((1, S5_N)), _full((S5_N, GROUP_W)), _full((S5_N, GROUP_W)),
                   _full((GROUP_W, S5_N)), _full((GROUP_W, S5_N)), _full((1, GROUP_W)), _full((GROUP_W, GROUP_W)), _full((1, GROUP_W))],
        scratch_shapes=[pltpu.VMEM((tb, S5_N), f32), pltpu.VMEM((tb, S5_N), f32), pltpu.VMEM((2, 8, S5_N), f32)],
        semantics=("arbitrary",), xchg=xchg,
        args=(proj, dyd, xr_all, xi_all, xr_all, xi_all, bbr, bbi, ctr, cti, lre, lim, d, glu_w, glu_b))


def _outproj_fwd(ys, h, bn_w, g1, w_out, tb):
    t = h.shape[0]

    def body(ya_ref, yb_ref, yc_ref, yd_ref, h_ref, bn_ref, g1_ref, w_ref, h1_ref, o_ref, gr_ref):
        bn = bn_ref[...]
        parts = []
        for g, r in enumerate((ya_ref, yb_ref, yc_ref, yd_ref)):
            n, _ = _rms(r[...].astype(f32))
            parts.append(n * bn[:, g * GROUP_W:(g + 1) * GROUP_W])
        groups = _b(jnp.concatenate(parts, axis=1))
        gr_ref[...] = groups
        o = _dot(groups, w_ref[...])
        o_ref[...] = _b(o)
        h1_ref[...] = h_ref[...] + g1_ref[...] * o

    grp = pl.BlockSpec((tb, GROUP_W), lambda i: (i, 0))
    row = pl.BlockSpec((tb, D_MODEL), lambda i: (i, 0))
    vec = _full((1, D_MODEL))
    return pl.pallas_call(
        body, name="outproj_fwd", grid=(t // tb,),
        out_shape=[jax.ShapeDtypeStruct((t, D_MODEL), f32), jax.ShapeDtypeStruct((t, D_MODEL), bf16),
                   jax.ShapeDtypeStruct((t, D_MODEL), bf16)],
        in_specs=[grp, grp, grp, grp, row, vec, vec, _full((D_MODEL, D_MODEL))],
        out_specs=[row, row, row],
        compiler_params=_params(("parallel",)),
    )(*ys, h, bn_w, g1, w_out)


def _outproj_bwd(dh1, o, ys, bn_w, g1, w_out, tb):
    t = dh1.shape[0]

    def body(dh_ref, o_ref, ya_ref, yb_ref, yc_ref, yd_ref, bn_ref, g1_ref, w_ref,
             da_ref, db_ref, dc_ref, dd_ref, do_ref, dg1_ref, dbn_ref):
        dh = dh_ref[...]
        _acc(dg1_ref, _colsum(dh * o_ref[...].astype(f32)))
        do = _b(dh * g1_ref[...])
        do_ref[...] = do
        dgroups = _dot_nt(do, w_ref[...])
        bn = bn_ref[...]
        dbn = []
        for g, (r, dr) in enumerate(((ya_ref, da_ref), (yb_ref, db_ref), (yc_ref, dc_ref), (yd_ref, dd_ref))):
            n, rr = _rms(r[...].astype(f32))
            dgr = dgroups[:, g * GROUP_W:(g + 1) * GROUP_W]
            dbn.append(_colsum(dgr * n))
            dr[...] = _rms_bwd(dgr * bn[:, g * GROUP_W:(g + 1) * GROUP_W], n, rr)
        _acc(dbn_ref, jnp.concatenate(dbn, axis=1))

    grp = pl.BlockSpec((tb, GROUP_W), lambda i: (i, 0))
    row = pl.BlockSpec((tb, D_MODEL), lambda i: (i, 0))
    vec = _full((1, D_MODEL))
    return pl.pallas_call(
        body, name="outproj_bwd", grid=(t // tb,),
        out_shape=[jax.ShapeDtypeStruct((t, GROUP_W), f32)] * 4 + [jax.ShapeDtypeStruct((t, D_MODEL), bf16),
                   jax.ShapeDtypeStruct((1, D_MODEL), f32), jax.ShapeDtypeStruct((1, D_MODEL), f32)],
        in_specs=[row, row, grp, grp, grp, grp, vec, vec, _full((D_MODEL, D_MODEL))],
        out_specs=[grp, grp, grp, grp, row, vec, vec],
        compiler_params=_params(("arbitrary",)),
    )(dh1, o, *ys, bn_w, g1, w_out)


def _mlp_fwd(h1, norm_w, sc, sh, g2, w1, w2, tb, xchg=None):
    t = h1.shape[0]
    nh = w1.shape[0] // MLP_SLABS

    def body(h_ref, nw_ref, sc_ref, sh_ref, g2_ref, w1_ref, w2_ref, h2_ref, m_ref, v_ref, r_ref, acc):
        j = pl.program_id(1)

        @pl.when(j == 0)
        def _():
            n, _ = _rms(h_ref[...])
            v_ref[...] = _b(n * nw_ref[...] * (1.0 + sc_ref[...]) + sh_ref[...])

        v = v_ref[...]
        p = None
        for s in range(MLP_SLABS):
            ra = jnp.maximum(_dot(v, w1_ref[s]), 0.0)
            r = _b(ra * ra)
            r_ref[:, s * MLP_HB:(s + 1) * MLP_HB] = r
            q = _dot(r, w2_ref[s])
            p = q if p is None else p + q

        @pl.when(j == 0)
        def _():
            acc[...] = p

        @pl.when(j != 0)
        def _():
            acc[...] += p

        @pl.when(j == nh - 1)
        def _():
            m = acc[...]
            m_ref[...] = _b(m)
            h2_ref[...] = h_ref[...] + g2_ref[...] * m

    row = pl.BlockSpec((tb, D_MODEL), lambda i, j: (i, 0))
    hid = pl.BlockSpec((tb, MLP_SLABS * MLP_HB), lambda i, j: (i, j))
    vec = _full((1, D_MODEL))
    return _call(
        body, name="mlp_fwd", grid=(t // tb, nh),
        out_shape=[jax.ShapeDtypeStruct((t, D_MODEL), f32), jax.ShapeDtypeStruct((t, D_MODEL), bf16),
                   jax.ShapeDtypeStruct((t, D_MODEL), bf16), jax.ShapeDtypeStruct((t, N_DEV * MLP_HB), bf16)],
        in_specs=[row, vec, vec, vec, vec, pl.BlockSpec((MLP_SLABS, D_MODEL, MLP_HB), lambda i, j: (j, 0, 0)),
                  pl.BlockSpec((MLP_SLABS, MLP_HB, D_MODEL), lambda i, j: (j, 0, 0))],
        out_specs=[row, row, row, hid],
        scratch_shapes=[pltpu.VMEM((tb, D_MODEL), f32)],
        semantics=("arbitrary", "arbitrary"), xchg=xchg, args=(h1, norm_w, sc, sh, g2, w1, w2))


def _mlp_bwd(dh2, m, h1, r, norm_w, sc, sh, g2, w1, w2, tb, xchg=None):
    t = h1.shape[0]
    slabs = MLP_BWD_SLABS
    nh = w1.shape[0] // slabs

    def body(dh_ref, m_ref, h_ref, r_ref, nw_ref, sc_ref, sh_ref, g2_ref, w1_ref, w2_ref,
             dh1_ref, do_ref, da_ref, dg2_ref, dsh_ref, dsc_ref, dnw_ref, acc):
        j = pl.program_id(1)

        @pl.when(j == 0)
        def _():
            dh = dh_ref[...]
            _acc(dg2_ref, _colsum(dh * m_ref[...].astype(f32)))
            do_ref[...] = _b(dh * g2_ref[...])

        do = do_ref[...]
        p = None
        for s in range(slabs):
            cols = slice(s * MLP_HB, (s + 1) * MLP_HB)
            dr = _dot_nt(do, w2_ref[s])
            da = _b(dr * 2.0 * jnp.sqrt(r_ref[:, cols].astype(f32)))
            da_ref[:, cols] = da
            q = _dot_nt(da, w1_ref[s])
            p = q if p is None else p + q

        @pl.when(j == 0)
        def _():
            acc[...] = p

        @pl.when(j != 0)
        def _():
            acc[...] += p

        @pl.when(j == nh - 1)
        def _():
            dv = acc[...]
            n, r = _rms(h_ref[...])
            nw = nw_ref[...]
            gain = 1.0 + sc_ref[...]
            _acc(dsh_ref, _colsum(dv))
            _acc(dsc_ref, _colsum(dv * n * nw))
            _acc(dnw_ref, _colsum(dv * gain * n))
            dh1_ref[...] = dh_ref[...] + _rms_bwd(dv * nw * gain, n, r)

    row = pl.BlockSpec((tb, D_MODEL), lambda i, j: (i, 0))
    hid = pl.BlockSpec((tb, slabs * MLP_HB), lambda i, j: (i, j))
    vec = _full((1, D_MODEL))
    once = dict(pipeline_mode=pl.Buffered(1)) if nh == 1 else {}
    return _call(
        body, name="mlp_bwd", grid=(t // tb, nh),
        out_shape=[jax.ShapeDtypeStruct((t, D_MODEL), f32), jax.ShapeDtypeStruct((t, D_MODEL), bf16),
                   jax.ShapeDtypeStruct((t, N_DEV * MLP_HB), bf16)] + [jax.ShapeDtypeStruct((1, D_MODEL), f32)] * 4,
        in_specs=[row, row, row, hid, vec, vec, vec, vec,
                  pl.BlockSpec((slabs, D_MODEL, MLP_HB), lambda i, j: (j, 0, 0), **once),
                  pl.BlockSpec((slabs, MLP_HB, D_MODEL), lambda i, j: (j, 0, 0), **once)],
        out_specs=[row, row, hid, vec, vec, vec, vec],
        scratch_shapes=[pltpu.VMEM((tb, D_MODEL), f32)],
        semantics=("arbitrary", "arbitrary"), xchg=xchg, args=(dh2, m, h1, r, norm_w, sc, sh, g2, w1, w2))


def _loss_head(h, target, norm_w, tb):
    t = h.shape[0]

    def body(h_ref, t_ref, w_ref, loss_ref, dh_ref, dw_ref):
        n, r = _rms(h_ref[...])
        w = w_ref[...]
        err = n * w - t_ref[...]
        part = 0.5 * jnp.sum(jnp.sum(err * err, axis=1, keepdims=True), axis=0, keepdims=True) / D_MODEL
        _acc(loss_ref, jnp.broadcast_to(part, (8, LANES)))
        dy = err / D_MODEL
        _acc(dw_ref, _colsum(dy * n))
        dh_ref[...] = _rms_bwd(dy * w, n, r)

    row = pl.BlockSpec((tb, D_MODEL), lambda i: (i, 0))
    return pl.pallas_call(
        body, name="loss_head", grid=(t // tb,),
        out_shape=[jax.ShapeDtypeStruct((8, LANES), f32), jax.ShapeDtypeStruct((t, D_MODEL), f32),
                   jax.ShapeDtypeStruct((1, D_MODEL), f32)],
        in_specs=[row, row, _full((1, D_MODEL))],
        out_specs=[_full((8, LANES)), row, _full((1, D_MODEL))],
        compiler_params=_params(("arbitrary",)),
    )(h, target, norm_w)


def _adam_math(w, g, m, v):
    m2 = ADAM_B1 * m + (1.0 - ADAM_B1) * g
    v2 = ADAM_B2 * v + (1.0 - ADAM_B2) * (g * g)
    mh = m2 / (1.0 - ADAM_B1 ** ADAM_STEP)
    vh = v2 / (1.0 - ADAM_B2 ** ADAM_STEP)
    return -ADAM_LR * (mh / (jnp.sqrt(vh) + ADAM_EPS) + ADAM_WD * w), m2, v2


def _adamw_small(ws, gs, ms, vs):
    n = len(ws)
    shapes = [w.shape for w in ws]
    as2d = [(1,) + s if len(s) == 1 else s for s in shapes]
    flat = [x.reshape(s) for group in (ws, gs, ms, vs) for x, s in zip(group, as2d)]

    def body(*refs):
        w_refs, g_refs, m_refs, v_refs, outs = refs[:n], refs[n:2 * n], refs[2 * n:3 * n], refs[3 * n:4 * n], refs[4 * n:]
        for i in range(n):
            d, m2, v2 = _adam_math(w_refs[i][...], g_refs[i][...], m_refs[i][...], v_refs[i][...])
            outs[3 * i][...] = d
            outs[3 * i + 1][...] = m2
            outs[3 * i + 2][...] = v2

    res = pl.pallas_call(body, name="adamw_small", out_shape=[jax.ShapeDtypeStruct(s, f32) for s in as2d for _ in range(3)],
                         in_specs=[VMEM] * (4 * n), out_specs=[VMEM] * (3 * n), compiler_params=_params())(*flat)
    return [r.reshape(shapes[i // 3]) for i, r in enumerate(res)]


def _sum_adamw_layers(parts0, parts1, w, m, v, name, rb):
    n_src, r, c = parts0.shape
    nb = r // rb

    def body(p0_ref, p1_ref, w_ref, m_ref, v_ref, g_ref, d_ref, m2_ref, v2_ref):
        def update(p_ref):
            g = p_ref[0].astype(f32)
            for s in range(1, n_src):
                g = g + p_ref[s].astype(f32)
            g_ref[0] = g
            d, m2, v2 = _adam_math(w_ref[0], g, m_ref[0], v_ref[0])
            d_ref[0] = d
            m2_ref[0] = m2
            v2_ref[0] = v2

        @pl.when(pl.program_id(0) == 0)
        def _():
            update(p0_ref)

        @pl.when(pl.program_id(0) == 1)
        def _():
            update(p1_ref)

    blk = pl.BlockSpec((1, rb, c), lambda l, i: (l, i, 0))
    return pl.pallas_call(
        body, name=name, grid=(2, nb),
        out_shape=[jax.ShapeDtypeStruct((2, r, c), f32)] * 4,
        in_specs=[pl.BlockSpec((n_src, rb, c), lambda l, i: (0, jnp.where(l == 0, i, nb - 1), 0)),
                  pl.BlockSpec((n_src, rb, c), lambda l, i: (0, jnp.where(l == 1, i, 0), 0)), blk, blk, blk],
        out_specs=[blk] * 4,
        compiler_params=_params(("arbitrary", "arbitrary")),
    )(parts0, parts1, w, m, v)


def _reorder_in(w):
    pad = jnp.zeros(w.shape[:-1] + (P_IN - 2308,), w.dtype)
    return jnp.concatenate([w[..., :2048], w[..., 2052:2308], w[..., 2048:2052], pad], axis=-1)


def _unreorder_in(w):
    return jnp.concatenate([w[..., :2048], w[..., 2304:2308], w[..., 2048:2304]], axis=-1)


def _block_diag(w2d, n_blocks):
    rows, cols = w2d.shape
    tiled = jnp.tile(w2d, (1, n_blocks))
    rb = lax.broadcasted_iota(jnp.int32, tiled.shape, 0) // (rows // n_blocks)
    cb = lax.broadcasted_iota(jnp.int32, tiled.shape, 1) // cols
    return jnp.where(rb == cb, tiled, jnp.zeros_like(tiled))


def _block_diag_extract(w_bd, n_blocks):
    rows, wide = w_bd.shape
    r, c = rows // n_blocks, wide // n_blocks
    w4 = w_bd.reshape(n_blocks, r, n_blocks, c)
    idx = jnp.arange(n_blocks)
    return w4[idx, :, idx, :]


def _lanes128(v):
    return jnp.pad(v.reshape(1, -1), ((0, 0), (0, LANES - v.size)))


def _rows_of(shape):
    n = 1
    for d in shape:
        n *= d
    return -(-n // (8 * LANES)) * 8, n


def _flat_pack(arrs, row_multiple=8):
    blocks = []
    for a in arrs:
        rows, n = _rows_of(a.shape)
        blocks.append(jnp.pad(a.reshape(-1), (0, rows * LANES - n)).reshape(rows, LANES))
    total = sum(b.shape[0] for b in blocks)
    pad = -total % row_multiple
    if pad:
        blocks.append(jnp.zeros((pad, LANES), blocks[0].dtype))
    return jnp.concatenate(blocks, axis=0)


def _flat_unpack(packed, shapes):
    out, off = [], 0
    for s in shapes:
        rows, n = _rows_of(s)
        out.append(packed[off:off + rows].reshape(-1)[:n].reshape(s))
        off += rows
    return out


_W_NAMES = ['norm_mix_w', 'norm_mlp_w', 'ada_w', 'ada_b', 'w_in', 'pool_w', 'pool_scale', 'sconv_w', 'ssd_conv_w',
            'ssd_conv_b', 'ssd_dt_bias', 'ssd_a_log', 'ssd_d', 's5_a_re', 's5_a_im', 's5_log_step', 's5_b_re', 's5_b_im',
            's5_c_re', 's5_c_im', 's5_d', 's5_glu_w', 's5_glu_b', 'branch_norm_w', 'w_out', 'mlp_w1', 'mlp_w2',
            'final_norm_w']
_BIG = ('ada_w', 'w_in', 'w_out', 'mlp_w1', 'mlp_w2')
_SMALL = [n for n in _W_NAMES if n not in _BIG]
_SHARDED_SMALL = {'sconv_w': (2, 32), 'ssd_conv_w': (2, 96), 's5_glu_w': (1, 32)}


def _gather(*blocks):
    return _ChipGather(blocks)


def _scatter(*parts):
    return _Exchange(parts, gather=False)


def _layer_forward(l, h, p, w, sh_b, tb):
    first = l == 0
    (proj, u_b), got = _inproj_fwd(h, p['norm_mix_w'][l], p['sc1'][l], p['sh1'][l], w['w_in', l], tb,
                                   xchg=_gather(sh_b[1][0]) if first else None)
    if first:
        w['w_out', 0] = got[0].reshape(D_MODEL, D_MODEL)
    ya = _pool_fwd(proj, p['pool_bd'][l], p['pool_scale'][l], tb)
    yb = _sconv_fwd(proj, p['sconv_w8'][l], tb)
    (yc, yc_pre, states), got = _ssd_fwd(proj, p['ssd_conv_w8'][l], p['ssd_conv_b'][l], p['ssd_dt_bias'][l], p['ssd_a_log'][l],
                                         p['ssd_d_cols'][l], tb, xchg=_gather(sh_b[2][0]) if first else None)
    if first:
        w['w1', 0] = got[0]
    (yd, xr, xi), got = _s5_fwd(proj, p['bbr'][l], p['bbi'][l], p['ctr'][l], p['cti'][l], p['lre'][l], p['lim'][l],
                                p['s5_d'][l], p['glu_w'][l], p['glu_b'][l], tb, xchg=_gather(sh_b[3][0]) if first else None)
    if first:
        w['w2', 0] = got[0]
    ys = (ya, yb, yc, yd)
    h1, o, groups_b = _outproj_fwd(ys, h, p['branch_norm_w'][l], p['g1'][l], w['w_out', l], tb)
    (h2, m, v_b, r_b), got = _mlp_fwd(h1, p['norm_mlp_w'][l], p['sc2'][l], p['sh2'][l], p['g2'][l], w['w1', l], w['w2', l],
                                      min(MLP_TB, h.shape[0]), xchg=_gather(*[sh_b[k][1] for k in range(4)]) if first else None)
    if first:
        w['w_in', 1] = got[0].reshape(D_MODEL, P_IN)
        w['w_out', 1] = got[1].reshape(D_MODEL, D_MODEL)
        w['w1', 1], w['w2', 1] = got[2], got[3]
    saved = dict(h=h, proj=proj, u_b=u_b, ys=ys, yc_pre=yc_pre, states=states, xr=xr, xi=xi, h1=h1, o=o,
                 groups_b=groups_b, m=m, v_b=v_b, r_b=r_b)
    return h2, saved


def _layer_backward(l, dh2, s, p, w, pending, recv, tb):
    def carry(names):
        names = [n for n in names if n in pending]
        return names, (_scatter(*[pending.pop(n) for n in names]) if names else None)

    def landed(names, got):
        for n, g in zip(names, got):
            recv[n] = g

    names, xchg = carry([('w_out', 1)])
    (dh1, do2_b, da_b, dg2, dsh2, dsc2, dnw_mlp), got = _mlp_bwd(dh2, s['m'], s['h1'], s['r_b'], p['norm_mlp_w'][l], p['sc2'][l],
                                                                p['sh2'][l], p['g2'][l], w['w1', l], w['w2', l], min(TB_BWD, tb),
                                                                xchg=xchg)
    landed(names, got)
    pending['mlp_w2', l] = _wgrad(s['r_b'], do2_b, 1, "wgrad_w2", tm=1024, tk=2048).reshape(N_DEV, MLP_HB, D_MODEL)
    pending['mlp_w1', l] = _wgrad(s['v_b'], da_b, N_DEV, "wgrad_w1", tm=1024, tk=4096)
    dya, dyb, dyc, dyd, do1_b, dg1, dbn = _outproj_bwd(dh1, s['o'], s['ys'], p['branch_norm_w'][l], p['g1'][l], w['w_out', l], tb)
    pending['w_out', l] = _wgrad(s['groups_b'], do1_b, 1, "wgrad_wout", tm=1024, tk=1024).reshape(N_DEV, D_MODEL // N_DEV, D_MODEL)
    proj = s['proj']
    dv, dpool_bd, dpool_scale = _pool_bwd(proj, dya, p['pool_bd'][l], p['pool_scale'][l], tb)
    dgb, dgc, dhh, dsconv = _sconv_bwd(proj, dyb, p['sconv_w8'][l], tb)
    names, xchg = carry([('mlp_w1', l)] + ([('w_out', 0)] if l == 0 else []))
    (dz, dxs, dbm, dcm, ddt, dconv_w, dconv_b, ddtb, dalog, ddskip), got = _ssd_bwd(
        proj, dyc, s['yc_pre'], s['states'], p['ssd_conv_w8'][l], p['ssd_conv_b'][l], p['ssd_dt_bias'][l], p['ssd_a_log'][l],
        p['ssd_d_cols'][l], min(TB_BWD, tb), xchg=xchg)
    landed(names, got)
    names, xchg = carry([('mlp_w2', l)])
    (du5, dlr, dli, dbbr, dbbi, dctr, dcti, dd5, dgw, dgb5), got = _s5_bwd(
        proj, dyd, s['xr'], s['xi'], p['bbr'][l], p['bbi'][l], p['ctr'][l], p['cti'][l], p['lre'][l], p['lim'][l],
        p['s5_d'][l], p['glu_w'][l], p['glu_b'][l], min(TB_BWD, tb), xchg=xchg)
    landed(names, got)
    dare, daim, dls, dbre_bd, dbim_bd = _s5_prep_bwd(p['are_c'][l], p['aim_c'][l], p['ls_c'][l], p['bre_bd'][l], p['bim_bd'][l],
                                                     dlr.reshape(S5_N, 1), dli.reshape(S5_N, 1), dbbr, dbbi)
    dparts = (dv, dgb, dgc, dhh, dz, dxs, dbm, dcm, du5, ddt)
    pending['w_in', l] = _wgrad_parts(s['u_b'], dparts, "wgrad_win", tm=1024, tk=1024).reshape(N_DEV, D_MODEL // N_DEV, P_IN)
    names, xchg = carry([('w_in', l)])
    (dh, dsh1, dsc1, dnw_mix), got = _inproj_bwd(dparts, dh1, s['h'], p['norm_mix_w'][l], p['sc1'][l], p['sh1'][l], w['w_in', l],
                                                 tb, xchg=xchg)
    landed(names, got)
    small = {
        'norm_mix_w': dnw_mix.reshape(D_MODEL), 'norm_mlp_w': dnw_mlp.reshape(D_MODEL),
        'ada_b': jnp.concatenate([dsh1, dsc1, dg1, dsh2, dsc2, dg2], axis=1).reshape(6 * D_MODEL),
        'pool_w': _block_diag_extract(dpool_bd, 4), 'pool_scale': dpool_scale.reshape(GROUP_W),
        'sconv_w': dsconv[0:3], 'ssd_conv_w': dconv_w[0:4], 'ssd_conv_b': dconv_b.reshape(768),
        'ssd_dt_bias': ddtb[0, 0:4], 'ssd_a_log': dalog[0, 0:4], 'ssd_d': ddskip[0, 0:4],
        's5_a_re': dare.reshape(16, 64), 's5_a_im': daim.reshape(16, 64), 's5_log_step': dls[0:16, 0],
        's5_b_re': _block_diag_extract(dbre_bd, 16), 's5_b_im': _block_diag_extract(dbim_bd, 16),
        's5_c_re': _block_diag_extract(dctr, 16), 's5_c_im': _block_diag_extract(dcti, 16),
        's5_d': dd5.reshape(GROUP_W), 's5_glu_w': dgw, 's5_glu_b': dgb5.reshape(GROUP_W),
        'branch_norm_w': dbn.reshape(D_MODEL),
    }
    return dh, small


def _prepare_params(a, me):
    pack_shapes = [(1, D_MODEL), (2, 3, 32), (2, 4, 96), (2, 32, GROUP_W)]
    packed = _flat_pack([a['c'], a['sconv_w'], a['ssd_conv_w'], a['s5_glu_w']])
    gathered = _gather_rows(packed, "gather_small")
    pieces = [_flat_unpack(gathered[d], pack_shapes) for d in range(N_DEV)]
    c_all = jnp.concatenate([pc[0] for pc in pieces], axis=0)
    sconv_full = jnp.concatenate([pc[1] for pc in pieces], axis=2)
    ssd_conv_full = jnp.concatenate([pc[2] for pc in pieces], axis=2)
    glu_full = jnp.concatenate([pc[3] for pc in pieces], axis=1)

    ada_b_cols = lax.dynamic_slice_in_dim(a['ada_b'], me * 768, 768, axis=1).reshape(2, 1, 768)
    cond, modrows = _ada_forward(c_all, a['ada_w'], ada_b_cols)
    mod_recv = _all_to_all_rows(modrows.transpose(1, 0, 2), "exchange_mod")
    mod = mod_recv.transpose(1, 0, 2).reshape(2, 6 * D_MODEL)
    p = {'cond': cond}
    for k, name in enumerate(('sh1', 'sc1', 'g1', 'sh2', 'sc2', 'g2')):
        p[name] = mod[:, k * D_MODEL:(k + 1) * D_MODEL].reshape(2, 1, D_MODEL)

    for name in ('norm_mix_w', 'norm_mlp_w', 'branch_norm_w'):
        p[name] = a[name].reshape(2, 1, D_MODEL)
    p['pool_bd'] = jnp.stack([_block_diag(a['pool_w'][l].reshape(GROUP_W, 64), 4) for l in range(2)])
    p['pool_scale'] = a['pool_scale'].reshape(2, 1, GROUP_W)
    p['sconv_w8'] = jnp.pad(sconv_full, ((0, 0), (0, 5), (0, 0)))
    p['ssd_conv_w8'] = jnp.pad(ssd_conv_full, ((0, 0), (0, 4), (0, 0)))
    p['ssd_conv_b'] = a['ssd_conv_b'].reshape(2, 1, 768)
    p['ssd_dt_bias'] = jnp.pad(a['ssd_dt_bias'], ((0, 0), (0, LANES - 4))).reshape(2, 1, LANES)
    p['ssd_a_log'] = jnp.pad(a['ssd_a_log'], ((0, 0), (0, LANES - 4))).reshape(2, 1, LANES)
    p['ssd_d_cols'] = jnp.repeat(a['ssd_d'], SSD_P, axis=1).reshape(2, 1, GROUP_W)
    p['are_c'] = a['s5_a_re'].reshape(2, S5_N, 1)
    p['aim_c'] = a['s5_a_im'].reshape(2, S5_N, 1)
    p['ls_c'] = jnp.repeat(a['s5_log_step'], 64, axis=1).reshape(2, S5_N, 1)
    p['bre_bd'] = jnp.stack([_block_diag(a['s5_b_re'][l].reshape(S5_N, 16), 16) for l in range(2)])
    p['bim_bd'] = jnp.stack([_block_diag(a['s5_b_im'][l].reshape(S5_N, 16), 16) for l in range(2)])
    p['ctr'] = jnp.stack([_block_diag(a['s5_c_re'][l].reshape(GROUP_W, 64), 16) for l in range(2)])
    p['cti'] = jnp.stack([_block_diag(a['s5_c_im'][l].reshape(GROUP_W, 64), 16) for l in range(2)])
    p['s5_d'] = a['s5_d'].reshape(2, 1, GROUP_W)
    p['glu_w'] = glu_full
    p['glu_b'] = a['s5_glu_b'].reshape(2, 1, GROUP_W)
    lre, lim, bbr, bbi = [], [], [], []
    for l in range(2):
        r = _s5_prep(p['are_c'][l], p['aim_c'][l], p['ls_c'][l], p['bre_bd'][l], p['bim_bd'][l])
        lre.append(r[0].reshape(1, S5_N))
        lim.append(r[1].reshape(1, S5_N))
        bbr.append(r[2])
        bbi.append(r[3])
    p['lre'], p['lim'], p['bbr'], p['bbi'] = lre, lim, bbr, bbi
    return p


def kernel(x, c, norm_mix_w, norm_mlp_w, ada_w, ada_b, w_in, pool_w, pool_scale, sconv_w, ssd_conv_w, ssd_conv_b, ssd_dt_bias, ssd_a_log, ssd_d, s5_a_re, s5_a_im, s5_log_step, s5_b_re, s5_b_im, s5_c_re, s5_c_im, s5_d, s5_glu_w, s5_glu_b, branch_norm_w, w_out, mlp_w1, mlp_w2, final_norm_w, loss_target, m_norm_mix_w, m_norm_mlp_w, m_ada_w, m_ada_b, m_w_in, m_pool_w, m_pool_scale, m_sconv_w, m_ssd_conv_w, m_ssd_conv_b, m_ssd_dt_bias, m_ssd_a_log, m_ssd_d, m_s5_a_re, m_s5_a_im, m_s5_log_step, m_s5_b_re, m_s5_b_im, m_s5_c_re, m_s5_c_im, m_s5_d, m_s5_glu_w, m_s5_glu_b, m_branch_norm_w, m_w_out, m_mlp_w1, m_mlp_w2, m_final_norm_w, v_norm_mix_w, v_norm_mlp_w, v_ada_w, v_ada_b, v_w_in, v_pool_w, v_pool_scale, v_sconv_w, v_ssd_conv_w, v_ssd_conv_b, v_ssd_dt_bias, v_ssd_a_log, v_ssd_d, v_s5_a_re, v_s5_a_im, v_s5_log_step, v_s5_b_re, v_s5_b_im, v_s5_c_re, v_s5_c_im, v_s5_d, v_s5_glu_w, v_s5_glu_b, v_branch_norm_w, v_w_out, v_mlp_w1, v_mlp_w2, v_final_norm_w):
    a = dict(locals())
    t = x.shape[1]
    tb = min(TB, t)
    me = _my_index()
    p = _prepare_params(a, me)

    sh_b = _cast_shards([_reorder_in(w_in), w_out, mlp_w1, mlp_w2])
    w = {('w_in', 0): _exchange_alone(_gather(sh_b[0][0]), "gather_w_in0")[0].reshape(D_MODEL, P_IN)}

    h = x.reshape(t, D_MODEL)
    saved = []
    for l in range(2):
        h, s = _layer_forward(l, h, p, w, sh_b, tb)
        saved.append(s)
    loss_blk, dh, dfinal = _loss_head(h, loss_target.reshape(t, D_MODEL), final_norm_w.reshape(1, D_MODEL), tb)

    pending, recv, small_parts = {}, {}, [None, None]
    for l in (1, 0):
        dh, small_parts[l] = _layer_backward(l, dh, saved[l], p, w, pending, recv, tb)
    grad_x = dh.reshape(1, t, D_MODEL)

    grads, deltas, new_m, new_v = {}, {}, {}, {}

    wmv_in = [_reorder_in(a[n]) for n in ('w_in', 'm_w_in', 'v_w_in')]
    outs = _sum_adamw_layers(recv['w_in', 0], recv['w_in', 1], *wmv_in, "adamw_w_in", 128)
    grads['w_in'], deltas['w_in'], new_m['w_in'], new_v['w_in'] = [_unreorder_in(o) for o in outs]
    for name, rb in (('w_out', 128), ('mlp_w1', 256), ('mlp_w2', 256)):
        grads[name], deltas[name], new_m[name], new_v[name] = _sum_adamw_layers(
            recv[name, 0], recv[name, 1], a[name], a['m_' + name], a['v_' + name], "adamw_" + name, rb)

    dmod = jnp.stack([small_parts[0]['ada_b'], small_parts[1]['ada_b']])
    dmod_recv = _all_to_all_rows(dmod.reshape(2, N_DEV, 768).transpose(1, 0, 2), "exchange_dmod")
    g_ada = _ada_backward(p['cond'], dmod_recv.transpose(1, 0, 2))
    grads['ada_w'], deltas['ada_w'], new_m['ada_w'], new_v['ada_w'] = _sum_adamw_layers(
        g_ada[0:1], g_ada[1:2], ada_w, m_ada_w, v_ada_w, "adamw_ada_w", 256)

    layered = [n for n in _SMALL if n != 'final_norm_w']
    full = [jnp.stack([small_parts[0][n], small_parts[1][n]]) for n in layered] + [dfinal.reshape(D_MODEL)]
    full.append(loss_blk[0:1, 0:1])
    full_shapes = [f.shape for f in full]
    summed = _flat_unpack(_allreduce_rows(_flat_pack(full, row_multiple=64)), full_shapes)
    loss = summed[-1].reshape(())
    local = []
    for n, g in zip(_SMALL, summed):
        if n in _SHARDED_SMALL:
            axis, size = _SHARDED_SMALL[n]
            g = lax.dynamic_slice_in_dim(g, me * size, size, axis=axis)
        local.append(g.reshape(a[n].shape))
    outs = _adamw_small([a[n] for n in _SMALL], local, [a['m_' + n] for n in _SMALL], [a['v_' + n] for n in _SMALL])
    for i, n in enumerate(_SMALL):
        grads[n], deltas[n], new_m[n], new_v[n] = local[i], outs[3 * i], outs[3 * i + 1], outs[3 * i + 2]

    return (loss, grad_x, *[grads[n] for n in _W_NAMES], *[deltas[n] for n in _W_NAMES],
            *[new_m[n] for n in _W_NAMES], *[new_v[n] for n in _W_NAMES])
```

```python
import functools

import jax
import jax.numpy as jnp
from jax import lax
from jax.experimental import pallas as pl
from jax.experimental.pallas import tpu as pltpu

f32 = jnp.float32
bf16 = jnp.bfloat16

N_DEV = 8
D_MODEL = 1024
GROUP_W = 256
P_IN = 2432
DT_COL = 2304
SSD_CHUNK = 128
SSD_HEADS = 4
SSD_P = 64
S5_N = 1024
MLP_HB = 512
TB = 1024
TB_BWD = 512
MLP_TB = 1024
MLP_SLABS = 2
MLP_BWD_SLABS = 8
EPS = 1e-6
LANES = 128
VMEM_LIMIT = 56 * 1024 * 1024
ADAM_LR, ADAM_B1, ADAM_B2, ADAM_EPS, ADAM_WD, ADAM_STEP = 0.001, 0.9, 0.999, 1e-08, 0.01, 10
POOL_WINDOWS = (2, 4, 8, 16)

C_POOL, C_GB, C_GC, C_HH, C_Z, C_XS, C_BM, C_CM, C_S5 = range(9)
C_DT128 = DT_COL // LANES

MESH = pl.DeviceIdType.MESH
ANY = pl.BlockSpec(memory_space=pl.ANY)
VMEM = pl.BlockSpec(memory_space=pltpu.VMEM)


def _dot(a, b):
    return jnp.dot(a, b, preferred_element_type=f32)


def _dot_nt(a, b):
    return lax.dot_general(a, b, (((1,), (1,)), ((), ())), preferred_element_type=f32)


def _dot_tn(a, b):
    return lax.dot_general(a, b, (((0,), (0,)), ((), ())), preferred_element_type=f32)


def _dot_exact(a, b):
    return jnp.dot(a, b, preferred_element_type=f32, precision=lax.Precision.HIGHEST)


def _b(x):
    return x.astype(bf16)


def _silu(x):
    return x * jax.nn.sigmoid(x)


def _dsilu(x):
    s = jax.nn.sigmoid(x)
    return s * (1.0 + x * (1.0 - s))


def _softplus(x):
    return jnp.maximum(x, 0.0) + jnp.log1p(jnp.exp(-jnp.abs(x)))


_GELU_K = 0.7978845608028654
_GELU_C = 0.044715


def _gelu(x):
    return 0.5 * x * (1.0 + jnp.tanh(_GELU_K * (x + _GELU_C * x * x * x)))


def _dgelu(x):
    th = jnp.tanh(_GELU_K * (x + _GELU_C * x * x * x))
    return 0.5 * (1.0 + th) + 0.5 * x * (1.0 - th * th) * _GELU_K * (1.0 + 3.0 * _GELU_C * x * x)


def _rms(h):
    r = lax.rsqrt(jnp.mean(h * h, axis=-1, keepdims=True) + EPS)
    return h * r, r


def _rms_bwd(dn, n, r):
    return r * (dn - n * jnp.mean(dn * n, axis=-1, keepdims=True))


def _colsum(x):
    return jnp.sum(x, axis=0, keepdims=True)


def _params(sem=None):
    return pltpu.CompilerParams(dimension_semantics=sem, vmem_limit_bytes=VMEM_LIMIT)


def _full(shape):
    return pl.BlockSpec(shape, lambda *_: (0,) * len(shape))


def _acc(ref, val):
    @pl.when(pl.program_id(0) == 0)
    def _():
        ref[...] = val

    @pl.when(pl.program_id(0) != 0)
    def _():
        ref[...] += val


def _me():
    return lax.axis_index("x"), lax.axis_index("y"), lax.axis_index("c")


def _my_index():
    x, y, c = _me()
    return 4 * x + 2 * y + c


def _coords(p):
    return (p // 4, (p // 2) % 2, p % 2)


class _Scatter:
    def __init__(self, srcs):
        self.srcs = list(srcs)
        self.n = len(self.srcs)
        self.out_shape = [jax.ShapeDtypeStruct(s.shape, s.dtype) for s in self.srcs]
        self.scratch = [pltpu.SemaphoreType.DMA((self.n, N_DEV)), pltpu.SemaphoreType.DMA((self.n, N_DEV)),
                        pltpu.SemaphoreType.DMA((self.n,))]

    def _remote(self, xin, xout, sems, t, k, me, to):
        return pltpu.make_async_remote_copy(
            src_ref=xin[t].at[to], dst_ref=xout[t].at[me], send_sem=sems[0].at[t, k], recv_sem=sems[1].at[t, k],
            device_id=_coords(to), device_id_type=MESH)

    def start(self, xin, xout, sems):
        me = _my_index()
        for t in range(self.n):
            pltpu.make_async_copy(xin[t].at[me], xout[t].at[me], sems[2].at[t]).start()
            for k in range(1, N_DEV):
                self._remote(xin, xout, sems, t, k, me, (me + k) % N_DEV).start()

    def forward(self, xin, xout, sems):
        pass

    def wait(self, xin, xout, sems):
        me = _my_index()
        for t in range(self.n):
            for k in range(1, N_DEV):
                src = (me + N_DEV - k) % N_DEV
                pltpu.make_async_remote_copy(
                    src_ref=xin[t].at[src], dst_ref=xout[t].at[src], send_sem=sems[0].at[t, k],
                    recv_sem=sems[1].at[t, k], device_id=_coords(src), device_id_type=MESH).wait_recv()
        for t in range(self.n):
            for k in range(1, N_DEV):
                self._remote(xin, xout, sems, t, k, me, (me + k) % N_DEV).wait_send()
            pltpu.make_async_copy(xin[t].at[me], xout[t].at[me], sems[2].at[t]).wait()


class _ChipGather:
    def __init__(self, srcs):
        self.srcs = list(srcs)
        self.n = len(self.srcs)
        self.out_shape = [jax.ShapeDtypeStruct((N_DEV,) + s.shape, s.dtype) for s in self.srcs]
        self.scratch = [pltpu.SemaphoreType.DMA((self.n, 7)), pltpu.SemaphoreType.DMA((self.n, 7)),
                        pltpu.SemaphoreType.DMA((self.n,))]

    @staticmethod
    def _places():
        x, y, c = _me()
        chips = [(1 - x, y), (x, 1 - y), (1 - x, 1 - y)]
        return (x, y, c), (x, y, 1 - c), chips

    @staticmethod
    def _slab(ref, dev):
        return ref.at[4 * dev[0] + 2 * dev[1] + dev[2]]

    def _copy(self, xin, xout, sems, t, k, block, to, src=None):
        return pltpu.make_async_remote_copy(
            src_ref=self._slab(xout[t], block) if src is None else src, dst_ref=self._slab(xout[t], block),
            send_sem=sems[0].at[t, k], recv_sem=sems[1].at[t, k], device_id=to, device_id_type=MESH)

    def start(self, xin, xout, sems):
        me, sibling, chips = self._places()
        for t in range(self.n):
            pltpu.make_async_copy(xin[t], self._slab(xout[t], me), sems[2].at[t]).start()
            self._copy(xin, xout, sems, t, 0, me, sibling, src=xin[t]).start()
            for j, chip in enumerate(chips):
                self._copy(xin, xout, sems, t, 1 + j, me, (*chip, me[2]), src=xin[t]).start()

    def forward(self, xin, xout, sems):
        me, sibling, chips = self._places()
        for t in range(self.n):
            for j, chip in enumerate(chips):
                self._copy(xin, xout, sems, t, 1 + j, (*chip, me[2]), me).wait_recv()
                self._copy(xin, xout, sems, t, 4 + j, (*chip, me[2]), sibling).start()

    def wait(self, xin, xout, sems):
        me, sibling, chips = self._places()
        for t in range(self.n):
            self._copy(xin, xout, sems, t, 0, sibling, me).wait_recv()
            for j, chip in enumerate(chips):
                self._copy(xin, xout, sems, t, 4 + j, (*chip, 1 - me[2]), me).wait_recv()
        for t in range(self.n):
            self._copy(xin, xout, sems, t, 0, me, sibling, src=xin[t]).wait_send()
            for j, chip in enumerate(chips):
                self._copy(xin, xout, sems, t, 1 + j, me, (*chip, me[2]), src=xin[t]).wait_send()
                self._copy(xin, xout, sems, t, 4 + j, (*chip, me[2]), sibling).wait_send()
            pltpu.make_async_copy(xin[t], self._slab(xout[t], me), sems[2].at[t]).wait()


def _call(body, *, name, grid, in_specs, out_specs, out_shape, args, semantics, scratch_shapes=(), xchg=None):
    if xchg is None:
        outs = pl.pallas_call(body, name=name, grid=grid, in_specs=in_specs, out_specs=out_specs, out_shape=out_shape,
                              scratch_shapes=list(scratch_shapes), compiler_params=_params(semantics))(*args)
        return outs, ()
    n_in, n_out, n_scr, n = len(in_specs), len(out_specs), len(scratch_shapes), xchg.n

    def carried(*refs):
        ins, xin = refs[:n_in], refs[n_in:n_in + n]
        outs, xout = refs[n_in + n:n_in + n + n_out], refs[n_in + n + n_out:n_in + 2 * n + n_out]
        scr, sems = refs[n_in + 2 * n + n_out:n_in + 2 * n + n_out + n_scr], refs[n_in + 2 * n + n_out + n_scr:]
        step = pl.program_id(0)
        for d in range(1, len(grid)):
            step = step * grid[d] + pl.program_id(d)
        n_steps = functools.reduce(lambda a, b: a * b, grid)

        @pl.when(step == 0)
        def _():
            xchg.start(xin, xout, sems)

        @pl.when(step == (2 * n_steps) // 3)
        def _():
            xchg.forward(xin, xout, sems)

        body(*ins, *outs, *scr)

        @pl.when(step == n_steps - 1)
        def _():
            xchg.wait(xin, xout, sems)

    res = pl.pallas_call(
        carried, name=name, grid=grid, in_specs=list(in_specs) + [ANY] * n, out_specs=list(out_specs) + [ANY] * n,
        out_shape=list(out_shape) + xchg.out_shape, scratch_shapes=list(scratch_shapes) + xchg.scratch,
        compiler_params=_params(("arbitrary",) * len(grid)))(*args, *xchg.srcs)
    return res[:n_out], tuple(res[n_out:])


def _exchange_alone(xchg, name):
    def body(*refs):
        xin, xout, sems = refs[:xchg.n], refs[xchg.n:2 * xchg.n], refs[2 * xchg.n:]
        xchg.start(xin, xout, sems)
        xchg.forward(xin, xout, sems)
        xchg.wait(xin, xout, sems)

    return pl.pallas_call(body, name=name, out_shape=xchg.out_shape, in_specs=[ANY] * xchg.n, out_specs=[ANY] * xchg.n,
                          scratch_shapes=xchg.scratch)(*xchg.srcs)


def _cast_shards(shards):
    n = len(shards)

    def body(*refs):
        for i, o in zip(refs[:n], refs[n:]):
            o[...] = i[...].astype(bf16)

    return pl.pallas_call(body, name="cast_shards", out_shape=[jax.ShapeDtypeStruct(s.shape, bf16) for s in shards],
                          in_specs=[VMEM] * n, out_specs=[VMEM] * n, compiler_params=_params())(*shards)


def _allreduce_rows(v):
    r = v.shape[0]
    rp = r // N_DEV

    def body(v_ref, o_ref, parts, sums, send1, recv1, send2, recv2):
        me = _my_index()

        def piece(ref, d):
            return ref.at[pl.ds(pl.multiple_of(d * rp, 8), rp), :]

        def copy1(k, src_dev, to):
            return pltpu.make_async_remote_copy(src_ref=piece(v_ref, to), dst_ref=parts.at[src_dev], send_sem=send1.at[k],
                                                recv_sem=recv1.at[k], device_id=_coords(to), device_id_type=MESH)

        def copy2(k, owner, to):
            return pltpu.make_async_remote_copy(src_ref=sums, dst_ref=piece(o_ref, owner), send_sem=send2.at[k],
                                                recv_sem=recv2.at[k], device_id=_coords(to), device_id_type=MESH)

        for k in range(1, N_DEV):
            copy1(k, me, (me + k) % N_DEV).start()
        parts[me] = v_ref[pl.ds(pl.multiple_of(me * rp, 8), rp), :]
        for k in range(1, N_DEV):
            copy1(k, (me + N_DEV - k) % N_DEV, me).wait_recv()
        total = parts[0]
        for s in range(1, N_DEV):
            total = total + parts[s]
        sums[...] = total
        o_ref[pl.ds(pl.multiple_of(me * rp, 8), rp), :] = total
        for k in range(1, N_DEV):
            copy2(k, me, (me + k) % N_DEV).start()
        for k in range(1, N_DEV):
            copy2(k, (me + N_DEV - k) % N_DEV, me).wait_recv()
        for k in range(1, N_DEV):
            copy1(k, me, (me + k) % N_DEV).wait_send()
            copy2(k, me, (me + k) % N_DEV).wait_send()

    return pl.pallas_call(
        body, name="allreduce_small_grads", out_shape=jax.ShapeDtypeStruct(v.shape, v.dtype),
        in_specs=[VMEM], out_specs=VMEM,
        scratch_shapes=[pltpu.VMEM((N_DEV, rp, LANES), f32), pltpu.VMEM((rp, LANES), f32)]
        + [pltpu.SemaphoreType.DMA((N_DEV,))] * 4,
        compiler_params=_params(),
    )(v)


def _gather_rows(v, name):
    def body(v_ref, o_ref, send_sems, recv_sems):
        me = _my_index()
        o_ref[me] = v_ref[...]
        sends = []
        for k in range(1, N_DEV):
            peer = (me + k) % N_DEV
            rc = pltpu.make_async_remote_copy(src_ref=v_ref, dst_ref=o_ref.at[me], send_sem=send_sems.at[k],
                                              recv_sem=recv_sems.at[k], device_id=_coords(peer), device_id_type=MESH)
            rc.start()
            sends.append(rc)
        for k in range(1, N_DEV):
            src = (me + N_DEV - k) % N_DEV
            pltpu.make_async_remote_copy(src_ref=v_ref, dst_ref=o_ref.at[src], send_sem=send_sems.at[k],
                                         recv_sem=recv_sems.at[k], device_id=_coords(src), device_id_type=MESH).wait_recv()
        for rc in sends:
            rc.wait_send()

    return pl.pallas_call(
        body, name=name, out_shape=jax.ShapeDtypeStruct((N_DEV,) + v.shape, v.dtype),
        in_specs=[VMEM], out_specs=VMEM,
        scratch_shapes=[pltpu.SemaphoreType.DMA((N_DEV,)), pltpu.SemaphoreType.DMA((N_DEV,))],
        compiler_params=pltpu.CompilerParams(vmem_limit_bytes=VMEM_LIMIT),
    )(v)


def _all_to_all_rows(v, name):
    def body(v_ref, o_ref, send_sems, recv_sems):
        me = _my_index()
        o_ref[me] = v_ref[me]
        sends = []
        for k in range(1, N_DEV):
            peer = (me + k) % N_DEV
            rc = pltpu.make_async_remote_copy(src_ref=v_ref.at[peer], dst_ref=o_ref.at[me], send_sem=send_sems.at[k],
                                              recv_sem=recv_sems.at[k], device_id=_coords(peer), device_id_type=MESH)
            rc.start()
            sends.append(rc)
        for k in range(1, N_DEV):
            src = (me + N_DEV - k) % N_DEV
            pltpu.make_async_remote_copy(src_ref=v_ref.at[src], dst_ref=o_ref.at[src], send_sem=send_sems.at[k],
                                         recv_sem=recv_sems.at[k], device_id=_coords(src), device_id_type=MESH).wait_recv()
        for rc in sends:
            rc.wait_send()

    return pl.pallas_call(
        body, name=name, out_shape=jax.ShapeDtypeStruct(v.shape, v.dtype),
        in_specs=[VMEM], out_specs=VMEM,
        scratch_shapes=[pltpu.SemaphoreType.DMA((N_DEV,)), pltpu.SemaphoreType.DMA((N_DEV,))],
    )(v)


def _ada_forward(c_all, ada_w, ada_b_cols):
    def body(c_ref, w_ref, b_ref, cond_ref, o_ref):
        cond = _silu(c_ref[...])
        cond_ref[...] = cond
        for l in range(2):
            o_ref[l] = _dot(_b(cond), _b(w_ref[l])) + b_ref[l]

    return pl.pallas_call(
        body, name="ada_forward",
        out_shape=[jax.ShapeDtypeStruct((N_DEV, D_MODEL), f32), jax.ShapeDtypeStruct((2, N_DEV, 768), f32)],
        in_specs=[VMEM] * 3, out_specs=[VMEM] * 2, compiler_params=_params(),
    )(c_all, ada_w, ada_b_cols)


def _ada_backward(cond, dmod_rows):
    def body(c_ref, d_ref, o_ref):
        cb = _b(c_ref[...])
        for l in range(2):
            o_ref[l] = _dot_tn(cb, _b(d_ref[l]))

    return pl.pallas_call(
        body, name="ada_backward", out_shape=jax.ShapeDtypeStruct((2, D_MODEL, 768), f32),
        in_specs=[VMEM] * 2, out_specs=VMEM, compiler_params=_params(),
    )(cond, dmod_rows)


def _inproj_fwd(h, norm_w, sc, sh, w_in, tb, xchg=None):
    t = h.shape[0]

    def body(h_ref, nw_ref, sc_ref, sh_ref, w_ref, proj_ref, u_ref):
        n, _ = _rms(h_ref[...])
        u = _b(n * nw_ref[...] * (1.0 + sc_ref[...]) + sh_ref[...])
        u_ref[...] = u
        proj_ref[...] = _dot(u, w_ref[...])

    row = pl.BlockSpec((tb, D_MODEL), lambda i: (i, 0))
    vec = _full((1, D_MODEL))
    return _call(
        body, name="inproj_fwd", grid=(t // tb,),
        out_shape=[jax.ShapeDtypeStruct((t, P_IN), f32), jax.ShapeDtypeStruct((t, D_MODEL), bf16)],
        in_specs=[row, vec, vec, vec, _full((D_MODEL, P_IN))],
        out_specs=[pl.BlockSpec((tb, P_IN), lambda i: (i, 0)), row],
        semantics=("parallel",), args=(h, norm_w, sc, sh, w_in), xchg=xchg)


def _inproj_bwd(dparts, dh_res, h, norm_w, sc, sh, w_in, tb, xchg=None):
    t = h.shape[0]

    def body(*refs):
        parts = refs[:10]
        dres_ref, h_ref, nw_ref, sc_ref, sh_ref, w_ref = refs[10:16]
        dh_ref, dsh_ref, dsc_ref, dnw_ref = refs[16:]
        dproj = jnp.concatenate([p[...] for p in parts], axis=1)
        du = _dot_nt(dproj, w_ref[...])
        n, r = _rms(h_ref[...])
        nw = nw_ref[...]
        gain = 1.0 + sc_ref[...]
        _acc(dsh_ref, _colsum(du))
        _acc(dsc_ref, _colsum(du * n * nw))
        _acc(dnw_ref, _colsum(du * gain * n))
        dh_ref[...] = dres_ref[...] + _rms_bwd(du * nw * gain, n, r)

    row = pl.BlockSpec((tb, D_MODEL), lambda i: (i, 0))
    vec = _full((1, D_MODEL))
    part_specs = [pl.BlockSpec((tb, GROUP_W), lambda i: (i, 0))] * 9 + [pl.BlockSpec((tb, LANES), lambda i: (i, 0))]
    return _call(
        body, name="inproj_bwd", grid=(t // tb,),
        out_shape=[jax.ShapeDtypeStruct((t, D_MODEL), f32)] + [jax.ShapeDtypeStruct((1, D_MODEL), f32)] * 3,
        in_specs=part_specs + [row, row, vec, vec, vec,
                               pl.BlockSpec((D_MODEL, P_IN), lambda i: (0, 0), pipeline_mode=pl.Buffered(1))],
        out_specs=[row, vec, vec, vec],
        semantics=("arbitrary",), xchg=xchg, args=(*dparts, dh_res, h, norm_w, sc, sh, w_in))


def _wgrad(a, b, n_blocks, name, tm, tk=512):
    t, m = a.shape
    nb = b.shape[1] // n_blocks
    tk = min(tk, t)
    nk = t // tk

    def body(a_ref, b_ref, o_ref, acc_ref):
        k = pl.program_id(2)
        p = _dot_tn(a_ref[...], b_ref[...])

        @pl.when(k == 0)
        def _():
            acc_ref[...] = p

        @pl.when(k != 0)
        def _():
            acc_ref[...] += p

        @pl.when(k == nk - 1)
        def _():
            o_ref[0] = acc_ref[...].astype(o_ref.dtype)

    return pl.pallas_call(
        body, name=name, grid=(m // tm, n_blocks, nk),
        out_shape=jax.ShapeDtypeStruct((n_blocks, m, nb), bf16),
        in_specs=[pl.BlockSpec((tk, tm), lambda i, j, k: (k, i)), pl.BlockSpec((tk, nb), lambda i, j, k: (k, j))],
        out_specs=pl.BlockSpec((1, tm, nb), lambda i, j, k: (j, i, 0)),
        scratch_shapes=[pltpu.VMEM((tm, nb), f32)],
        compiler_params=_params(("parallel", "parallel", "arbitrary")),
    )(a, b)


def _wgrad_parts(a, parts, name, tm, tk):
    t, m = a.shape
    n = sum(p.shape[1] for p in parts)
    n_parts = len(parts)
    tk = min(tk, t)
    nk = t // tk

    def body(*refs):
        a_ref, part_refs, o_ref, acc_ref = refs[0], refs[1:1 + n_parts], refs[1 + n_parts], refs[2 + n_parts]
        k = pl.program_id(1)
        p = _dot_tn(a_ref[...], jnp.concatenate([r[...] for r in part_refs], axis=1))

        @pl.when(k == 0)
        def _():
            acc_ref[...] = p

        @pl.when(k != 0)
        def _():
            acc_ref[...] += p

        @pl.when(k == nk - 1)
        def _():
            o_ref[...] = acc_ref[...].astype(o_ref.dtype)

    return pl.pallas_call(
        body, name=name, grid=(m // tm, nk),
        out_shape=jax.ShapeDtypeStruct((m, n), bf16),
        in_specs=[pl.BlockSpec((tk, tm), lambda i, k: (k, i))]
        + [pl.BlockSpec((tk, p.shape[1]), lambda i, k: (k, 0)) for p in parts],
        out_specs=pl.BlockSpec((tm, n), lambda i, k: (i, 0)),
        scratch_shapes=[pltpu.VMEM((tm, n), f32)],
        compiler_params=_params(("parallel", "arbitrary")),
    )(a, *parts)


def _pool_counts(rows, t0):
    tpos = (lax.broadcasted_iota(jnp.int32, (rows, GROUP_W), 0) + t0 + 1).astype(f32)
    grp = lax.broadcasted_iota(jnp.int32, (rows, GROUP_W), 1) // 64
    win = jnp.where(grp == 0, 2.0, jnp.where(grp == 1, 4.0, jnp.where(grp == 2, 8.0, 16.0)))
    return jnp.minimum(tpos, win), grp


def _pool_select(grp, l1, l2, l3, l4):
    return jnp.where(grp == 0, l1, jnp.where(grp == 1, l2, jnp.where(grp == 2, l3, l4)))


def _pool_means(v, halo, t0):
    tb = v.shape[0]
    ext = jnp.concatenate([halo, v], axis=0)
    n = tb + 16
    s1 = ext[1:n] + ext[0:n - 1]
    s2 = s1[2:n - 1] + s1[0:n - 3]
    s3 = s2[4:n - 3] + s2[0:n - 7]
    s4 = s3[8:n - 7] + s3[0:n - 15]
    cnt, grp = _pool_counts(tb, t0)
    wsum = _pool_select(grp, s1[15:15 + tb], s2[13:13 + tb], s3[9:9 + tb], s4[1:1 + tb])
    return wsum / cnt - v


def _pool_fwd(proj, pw_bd, scale, tb):
    t = proj.shape[0]

    def body(v_ref, vh_ref, pw_ref, sc_ref, o_ref):
        i = pl.program_id(0)
        halo = jnp.where(i > 0, vh_ref[...], 0.0)
        p = _pool_means(v_ref[...], halo, i * tb)
        o_ref[...] = _b(_dot(_b(p), _b(pw_ref[...])) * sc_ref[...])

    return pl.pallas_call(
        body, name="pool_fwd", grid=(t // tb,),
        out_shape=jax.ShapeDtypeStruct((t, GROUP_W), bf16),
        in_specs=[pl.BlockSpec((tb, GROUP_W), lambda i: (i, C_POOL)),
                  pl.BlockSpec((16, GROUP_W), lambda i: (jnp.maximum(i * (tb // 16) - 1, 0), C_POOL)),
                  _full((GROUP_W, GROUP_W)), _full((1, GROUP_W))],
        out_specs=pl.BlockSpec((tb, GROUP_W), lambda i: (i, 0)),
        compiler_params=_params(("parallel",)),
    )(proj, proj, pw_bd, scale)


def _pool_bwd(proj, dy, pw_bd, scale, tb):
    t = proj.shape[0]
    nt = t // tb
    last16 = t // 16 - 1

    def body(v_ref, vh_ref, dy_ref, dyh_ref, pw_ref, sc_ref, dv_ref, dpw_ref, dsc_ref):
        i = pl.program_id(0)
        halo = jnp.where(i > 0, vh_ref[...], 0.0)
        p = _pool_means(v_ref[...], halo, i * tb)
        pw = _b(pw_ref[...])
        sc = sc_ref[...]
        dy = dy_ref[...]
        ypre = _dot(_b(p), pw)
        _acc(dsc_ref, _colsum(dy * ypre))
        dys = _b(dy * sc)
        _acc(dpw_ref, _dot_tn(_b(p), dys))
        dp = _dot_nt(dys, pw)
        dph = _dot_nt(_b(jnp.where(i < nt - 1, dyh_ref[...], 0.0) * sc), pw)
        cnt, grp = _pool_counts(tb, i * tb)
        cnth, _ = _pool_counts(16, (i + 1) * tb)
        ext = jnp.concatenate([dp / cnt, dph / cnth], axis=0)
        n = tb + 16
        f1 = ext[0:n - 1] + ext[1:n]
        f2 = f1[0:n - 3] + f1[2:n - 1]
        f3 = f2[0:n - 7] + f2[4:n - 3]
        f4 = f3[0:n - 15] + f3[8:n - 7]
        dv_ref[...] = _b(_pool_select(grp, f1[0:tb], f2[0:tb], f3[0:tb], f4[0:tb]) - dp)

    return pl.pallas_call(
        body, name="pool_bwd", grid=(nt,),
        out_shape=[jax.ShapeDtypeStruct((t, GROUP_W), bf16), jax.ShapeDtypeStruct((GROUP_W, GROUP_W), f32),
                   jax.ShapeDtypeStruct((1, GROUP_W), f32)],
        in_specs=[pl.BlockSpec((tb, GROUP_W), lambda i: (i, C_POOL)),
                  pl.BlockSpec((16, GROUP_W), lambda i: (jnp.maximum(i * (tb // 16) - 1, 0), C_POOL)),
                  pl.BlockSpec((tb, GROUP_W), lambda i: (i, 0)),
                  pl.BlockSpec((16, GROUP_W), lambda i: (jnp.minimum((i + 1) * (tb // 16), last16), 0)),
                  _full((GROUP_W, GROUP_W)), _full((1, GROUP_W))],
        out_specs=[pl.BlockSpec((tb, GROUP_W), lambda i: (i, 0)), _full((GROUP_W, GROUP_W)), _full((1, GROUP_W))],
        compiler_params=_params(("arbitrary",)),
    )(proj, proj, dy, dy, pw_bd, scale)


def _sconv_fwd(proj, w, tb):
    t = proj.shape[0]

    def body(gb_ref, gc_ref, hh_ref, gch_ref, hhh_ref, w_ref, o_ref):
        i = pl.program_id(0)
        q = gc_ref[...] * hh_ref[...]
        qh = jnp.where(i > 0, gch_ref[...] * hhh_ref[...], 0.0)
        ext = jnp.concatenate([qh, q], axis=0)
        w = w_ref[...]
        conv = w[0:1] * ext[6:6 + tb] + w[1:2] * ext[7:7 + tb] + w[2:3] * ext[8:8 + tb]
        o_ref[...] = _b(gb_ref[...] * conv)

    def col(c):
        return pl.BlockSpec((tb, GROUP_W), lambda i: (i, c))

    def prev(c):
        return pl.BlockSpec((8, GROUP_W), lambda i: (jnp.maximum(i * (tb // 8) - 1, 0), c))

    return pl.pallas_call(
        body, name="sconv_fwd", grid=(t // tb,),
        out_shape=jax.ShapeDtypeStruct((t, GROUP_W), bf16),
        in_specs=[col(C_GB), col(C_GC), col(C_HH), prev(C_GC), prev(C_HH), _full((8, GROUP_W))],
        out_specs=pl.BlockSpec((tb, GROUP_W), lambda i: (i, 0)),
        compiler_params=_params(("parallel",)),
    )(proj, proj, proj, proj, proj, w)


def _sconv_bwd(proj, dy, w, tb):
    t = proj.shape[0]
    nt = t // tb
    last8 = t // 8 - 1

    def body(gb_ref, gc_ref, hh_ref, gch_ref, hhh_ref, gbn_ref, dy_ref, dyn_ref, w_ref, dgb_ref, dgc_ref, dhh_ref, dw_ref):
        i = pl.program_id(0)
        gc, hh, gb, dy = gc_ref[...], hh_ref[...], gb_ref[...], dy_ref[...]
        q = gc * hh
        qh = jnp.where(i > 0, gch_ref[...] * hhh_ref[...], 0.0)
        ext = jnp.concatenate([qh, q], axis=0)
        w = w_ref[...]
        conv = w[0:1] * ext[6:6 + tb] + w[1:2] * ext[7:7 + tb] + w[2:3] * ext[8:8 + tb]
        dgb_ref[...] = _b(dy * conv)
        e = dy * gb
        en = jnp.where(i < nt - 1, dyn_ref[...] * gbn_ref[...], 0.0)
        exte = jnp.concatenate([e, en], axis=0)
        dq = w[2:3] * exte[0:tb] + w[1:2] * exte[1:1 + tb] + w[0:1] * exte[2:2 + tb]
        dgc_ref[...] = _b(dq * hh)
        dhh_ref[...] = _b(dq * gc)
        dw = jnp.concatenate([_colsum(e * ext[6:6 + tb]), _colsum(e * ext[7:7 + tb]), _colsum(e * ext[8:8 + tb]),
                              jnp.zeros((5, GROUP_W), f32)], axis=0)
        _acc(dw_ref, dw)

    def col(c):
        return pl.BlockSpec((tb, GROUP_W), lambda i: (i, c))

    def prev(c):
        return pl.BlockSpec((8, GROUP_W), lambda i: (jnp.maximum(i * (tb // 8) - 1, 0), c))

    def nxt(c):
        return pl.BlockSpec((8, GROUP_W), lambda i: (jnp.minimum((i + 1) * (tb // 8), last8), c))

    out = pl.BlockSpec((tb, GROUP_W), lambda i: (i, 0))
    return pl.pallas_call(
        body, name="sconv_bwd", grid=(nt,),
        out_shape=[jax.ShapeDtypeStruct((t, GROUP_W), bf16)] * 3 + [jax.ShapeDtypeStruct((8, GROUP_W), f32)],
        in_specs=[col(C_GB), col(C_GC), col(C_HH), prev(C_GC), prev(C_HH), nxt(C_GB), col(0), nxt(0), _full((8, GROUP_W))],
        out_specs=[out, out, out, _full((8, GROUP_W))],
        compiler_params=_params(("arbitrary",)),
    )(proj, proj, proj, proj, proj, proj, dy, dy, w)


def _conv4(xr, halo, w, bias):
    tb = xr.shape[0]
    ext = jnp.concatenate([halo, xr], axis=0)
    pre = w[0:1] * ext[5:5 + tb] + w[1:2] * ext[6:6 + tb] + w[2:3] * ext[7:7 + tb] + w[3:4] * ext[8:8 + tb] + bias
    return pre, ext


def _tri():
    r = lax.broadcasted_iota(jnp.int32, (SSD_CHUNK, SSD_CHUNK), 0)
    c = lax.broadcasted_iota(jnp.int32, (SSD_CHUNK, SSD_CHUNK), 1)
    return r >= c


def _lane_pick(vals):
    rows = vals[0].shape[0]
    lane = lax.broadcasted_iota(jnp.int32, (rows, LANES), 1)
    out = jnp.zeros((rows, LANES), f32)
    for h, v in enumerate(vals):
        out = jnp.where(lane == h, v, out)
    return out


def _ssd_fwd(proj, conv_w, conv_b, dt_bias, a_log, d_cols, tb, xchg=None):
    t = proj.shape[0]
    cpt = tb // SSD_CHUNK

    def body(z_ref, xs_ref, bm_ref, cm_ref, xsh_ref, bmh_ref, cmh_ref, dt_ref, cw_ref, cb_ref, dtb_ref, al_ref, dk_ref,
             o_ref, y_ref, st_ref, state):
        i = pl.program_id(0)

        @pl.when(i == 0)
        def _():
            state[...] = jnp.zeros_like(state)

        cw, cb = cw_ref[...], cb_ref[...]
        acts = []
        for j, (r, hr) in enumerate(((xs_ref, xsh_ref), (bm_ref, bmh_ref), (cm_ref, cmh_ref))):
            halo = jnp.where(i > 0, hr[...], 0.0)
            pre, _ = _conv4(r[...], halo, cw[:, j * 256:(j + 1) * 256], cb[:, j * 256:(j + 1) * 256])
            acts.append(_silu(pre))
        xs, bm, cm = acts
        dt = _softplus(dt_ref[...] + dtb_ref[...])
        a = -jnp.exp(al_ref[...])
        adt = dt * a
        tri = _tri()
        trif = tri.astype(f32)
        dk = dk_ref[...]
        for c in range(cpt):
            rows = slice(c * SSD_CHUNK, (c + 1) * SSD_CHUNK)
            acol = _dot_exact(trif, adt[rows])
            arow = acol.T
            dt_c = dt[rows]
            ys = []
            rowi = lax.broadcasted_iota(jnp.int32, (SSD_CHUNK, 1), 0)
            first = lax.broadcasted_iota(jnp.int32, (SSD_CHUNK, SSD_CHUNK), 1) < SSD_P
            for g in range(SSD_HEADS // 2):
                cols = slice(g * 128, (g + 1) * 128)
                cg, bg = _b(cm[rows, cols]), _b(bm[rows, cols])
                xg = xs[rows, cols]
                heads = (2 * g, 2 * g + 1)
                ac = [acol[:, h:h + 1] for h in heads]
                alast = [v[SSD_CHUNK - 1:SSD_CHUNK] for v in ac]
                dtw = jnp.where(first, dt_c[:, heads[0]:heads[0] + 1], dt_c[:, heads[1]:heads[1] + 1])
                eaw = jnp.where(first, jnp.exp(ac[0]), jnp.exp(ac[1]))
                wdw = jnp.where(first, jnp.exp(alast[0] - ac[0]), jnp.exp(alast[1] - ac[1]))
                xdt = xg * dtw
                xb = _b(xdt)
                gmat = _dot_nt(cg, bg)
                ydiag = []
                for k, h in enumerate(heads):
                    lm = jnp.exp(jnp.where(tri, ac[k] - arow[h:h + 1, :], -jnp.inf))
                    ydiag.append(_dot(_b(gmat * lm), xb[:, k * SSD_P:(k + 1) * SSD_P]))
                s_in = state[g]
                st_ref[c, g] = s_in
                ys.append(jnp.concatenate(ydiag, axis=1) + eaw * _dot_nt(cg, _b(s_in)) + xg * dk[:, cols])
                state[g] = jnp.where(rowi < SSD_P, jnp.exp(alast[0]), jnp.exp(alast[1])) * s_in + _dot_tn(_b(xdt * wdw), bg)
            yc = jnp.concatenate(ys, axis=1)
            y_ref[rows, :] = yc
            o_ref[rows, :] = _b(yc * _silu(z_ref[rows, :]))

    def col(c):
        return pl.BlockSpec((tb, GROUP_W), lambda i: (i, c))

    def prev(c):
        return pl.BlockSpec((8, GROUP_W), lambda i: (jnp.maximum(i * (tb // 8) - 1, 0), c))

    out = pl.BlockSpec((tb, GROUP_W), lambda i: (i, 0))
    return _call(
        body, name="ssd_fwd", grid=(t // tb,),
        out_shape=[jax.ShapeDtypeStruct((t, GROUP_W), bf16), jax.ShapeDtypeStruct((t, GROUP_W), f32),
                   jax.ShapeDtypeStruct((t // SSD_CHUNK, 2, 128, 128), f32)],
        in_specs=[col(C_Z), col(C_XS), col(C_BM), col(C_CM), prev(C_XS), prev(C_BM), prev(C_CM),
                  pl.BlockSpec((tb, LANES), lambda i: (i, C_DT128)),
                  _full((8, 768)), _full((1, 768)), _full((1, LANES)), _full((1, LANES)), _full((1, GROUP_W))],
        out_specs=[out, out, pl.BlockSpec((cpt, 2, 128, 128), lambda i: (i, 0, 0, 0))],
        scratch_shapes=[pltpu.VMEM((2, 128, 128), f32)],
        semantics=("arbitrary",), xchg=xchg,
        args=(proj, proj, proj, proj, proj, proj, proj, proj, conv_w, conv_b, dt_bias, a_log, d_cols))


def _ssd_bwd(proj, dyc, y_pre, states, conv_w, conv_b, dt_bias, a_log, d_cols, tb, xchg=None):
    t = proj.shape[0]
    nt = t // tb
    cpt = tb // SSD_CHUNK

    def body(z_ref, xs_ref, bm_ref, cm_ref, xsh_ref, bmh_ref, cmh_ref, dt_ref, dy_ref, yp_ref, st_ref,
             cw_ref, cb_ref, dtb_ref, al_ref, dk_ref,
             dz_ref, dxs_ref, dbm_ref, dcm_ref, ddt_ref, dcw_ref, dcb_ref, ddtb_ref, dal_ref, ddk_ref,
             dstate, carry):
        i = pl.program_id(0)
        ti = nt - 1 - i

        @pl.when(i == 0)
        def _():
            dstate[...] = jnp.zeros_like(dstate)
            carry[...] = jnp.zeros_like(carry)

        cw, cb = cw_ref[...], cb_ref[...]
        pres, exts, acts = [], [], []
        for j, (r, hr) in enumerate(((xs_ref, xsh_ref), (bm_ref, bmh_ref), (cm_ref, cmh_ref))):
            halo = jnp.where(ti > 0, hr[...], 0.0)
            pre, ext = _conv4(r[...], halo, cw[:, j * 256:(j + 1) * 256], cb[:, j * 256:(j + 1) * 256])
            pres.append(pre)
            exts.append(ext)
            acts.append(_silu(pre))
        xs, bm, cm = acts
        raw = dt_ref[...] + dtb_ref[...]
        dt = _softplus(raw)
        a = -jnp.exp(al_ref[...])
        adt = dt * a
        tri = _tri()
        trif = tri.astype(f32)
        dk = dk_ref[...]
        z = z_ref[...]
        dyc = dy_ref[...]
        dz_ref[...] = _b(dyc * yp_ref[...] * _dsilu(z))
        dy_all = dyc * _silu(z)
        lane = lax.broadcasted_iota(jnp.int32, (1, LANES), 1)
        ddk_acc = jnp.zeros((1, LANES), f32)
        dal_acc = jnp.zeros((1, LANES), f32)
        dxs_c, dbm_c, dcm_c, ddt_c = [None] * cpt, [None] * cpt, [None] * cpt, [None] * cpt
        for c in reversed(range(cpt)):
            rows = slice(c * SSD_CHUNK, (c + 1) * SSD_CHUNK)
            acol = _dot_exact(trif, adt[rows])
            arow = acol.T
            dt_c = dt[rows]
            da_cols, da_rows, ddt_heads, dxs_groups, dbg, dcg = [], [], [], [], [], []
            rowi = lax.broadcasted_iota(jnp.int32, (SSD_CHUNK, 1), 0)
            first = lax.broadcasted_iota(jnp.int32, (SSD_CHUNK, SSD_CHUNK), 1) < SSD_P
            for g in range(SSD_HEADS // 2):
                cols = slice(g * 128, (g + 1) * 128)
                cgf, bgf = cm[rows, cols], bm[rows, cols]
                cg, bg = _b(cgf), _b(bgf)
                xg, dyg = xs[rows, cols], dy_all[rows, cols]
                s_in, dsn = st_ref[c, g], dstate[g]
                sb, dsnb = _b(s_in), _b(dsn)
                heads = (2 * g, 2 * g + 1)
                ac = [acol[:, h:h + 1] for h in heads]
                alast = [v[SSD_CHUNK - 1:SSD_CHUNK] for v in ac]
                el = [jnp.exp(v) for v in alast]
                dtw = jnp.where(first, dt_c[:, heads[0]:heads[0] + 1], dt_c[:, heads[1]:heads[1] + 1])
                eaw = jnp.where(first, jnp.exp(ac[0]), jnp.exp(ac[1]))
                wdw = jnp.where(first, jnp.exp(alast[0] - ac[0]), jnp.exp(alast[1] - ac[1]))
                xdt = xg * dtw
                xb, dyb = _b(xdt), _b(dyg)
                gmat = _dot_nt(cg, bg)
                dgs, dxh, da = None, [], []
                for k, h in enumerate(heads):
                    hc = slice(k * SSD_P, (k + 1) * SSD_P)
                    lm = jnp.exp(jnp.where(tri, ac[k] - arow[h:h + 1, :], -jnp.inf))
                    m = gmat * lm
                    dm = _dot_nt(dyb[:, hc], xb[:, hc])
                    dxh.append(_dot_tn(_b(m), dyb[:, hc]))
                    dgs = dm * lm if dgs is None else dgs + dm * lm
                    wm = dm * m
                    da.append(jnp.sum(wm, axis=1, keepdims=True))
                    da_rows.append(jnp.sum(wm, axis=0, keepdims=True))
                dgb = _b(dgs)
                dcg_g = _dot(dgb, bg)
                dbg_g = _dot_tn(dgb, cg)
                yoff = eaw * _dot_nt(cg, sb)
                dyoff = dyg * yoff
                dye = _b(dyg * eaw)
                dcg_g = dcg_g + _dot(dye, sb)
                ds_y = _dot_tn(dye, cg)
                u = _dot_nt(bg, dsnb)
                dx = jnp.concatenate(dxh, axis=1) + wdw * u
                dbg_g = dbg_g + _dot(_b(xdt * wdw), dsnb)
                xu = xdt * u * wdw
                ss = jnp.sum(dsn * s_in, axis=1, keepdims=True)
                dxx = dx * xg
                dyx = _colsum(dyg * xg)
                for k, h in enumerate(heads):
                    mine = first if k == 0 else jnp.logical_not(first)
                    dwv = jnp.sum(jnp.where(mine, xu, 0.0), axis=1, keepdims=True)
                    mine_rows = (rowi < SSD_P) if k == 0 else (rowi >= SSD_P)
                    dalast = jnp.sum(dwv, axis=0, keepdims=True) + el[k] * jnp.sum(jnp.where(mine_rows, ss, 0.0), axis=0, keepdims=True)
                    dah = da[k] + jnp.sum(jnp.where(mine, dyoff, 0.0), axis=1, keepdims=True) - dwv
                    da_cols.append(dah + jnp.where(rowi == SSD_CHUNK - 1, dalast, 0.0))
                    ddt_heads.append(jnp.sum(jnp.where(mine, dxx, 0.0), axis=1, keepdims=True))
                    ddk_acc = ddk_acc + jnp.where(lane == h, jnp.sum(jnp.where(mine[0:1], dyx, 0.0), axis=1, keepdims=True), 0.0)
                dstate[g] = jnp.where(rowi < SSD_P, el[0], el[1]) * dsn + ds_y
                dxs_groups.append(dx * dtw + dyg * dk[:, cols])
                dbg.append(dbg_g)
                dcg.append(dcg_g)
            da_blk = _lane_pick(da_cols)
            rowsel = lax.broadcasted_iota(jnp.int32, (SSD_CHUNK, SSD_CHUNK), 0)
            da_rows_blk = jnp.zeros((SSD_CHUNK, SSD_CHUNK), f32)
            for h in range(SSD_HEADS):
                da_rows_blk = jnp.where(rowsel == h, da_rows[h], da_rows_blk)
            da_blk = da_blk - da_rows_blk.T
            dadt = lax.dot_general(trif, da_blk, (((0,), (0,)), ((), ())), preferred_element_type=f32,
                                   precision=lax.Precision.HIGHEST)
            dal_acc = dal_acc + _colsum(dadt * dt_c)
            ddt_c[c] = dadt * a + _lane_pick(ddt_heads)
            dxs_c[c] = jnp.concatenate(dxs_groups, axis=1)
            dbm_c[c] = jnp.concatenate(dbg, axis=1)
            dcm_c[c] = jnp.concatenate(dcg, axis=1)
        ddt = jnp.concatenate(ddt_c, axis=0) if cpt > 1 else ddt_c[0]
        ddraw = jnp.where(lane < SSD_HEADS, ddt * jax.nn.sigmoid(raw), 0.0)
        ddt_ref[...] = _b(ddraw)
        _acc(ddtb_ref, _colsum(ddraw))
        _acc(dal_ref, jnp.where(lane < SSD_HEADS, dal_acc * a, 0.0))
        _acc(ddk_ref, ddk_acc)
        dcw_parts, dcb_parts = [], []
        for j, (dparts, out_ref) in enumerate(((dxs_c, dxs_ref), (dbm_c, dbm_ref), (dcm_c, dcm_ref))):
            dact = jnp.concatenate(dparts, axis=0) if cpt > 1 else dparts[0]
            dpre = dact * _dsilu(pres[j])
            w = cw[:, j * 256:(j + 1) * 256]
            ext = jnp.concatenate([dpre, carry[:, j * 256:(j + 1) * 256]], axis=0)
            out_ref[...] = _b(w[3:4] * ext[0:tb] + w[2:3] * ext[1:1 + tb] + w[1:2] * ext[2:2 + tb] + w[0:1] * ext[3:3 + tb])
            carry[:, j * 256:(j + 1) * 256] = dpre[0:8]
            xe = exts[j]
            dcw_parts.append(jnp.concatenate([_colsum(dpre * xe[5 + k:5 + k + tb]) for k in range(4)]
                                             + [jnp.zeros((4, GROUP_W), f32)], axis=0))
            dcb_parts.append(_colsum(dpre))
        _acc(dcw_ref, jnp.concatenate(dcw_parts, axis=1))
        _acc(dcb_ref, jnp.concatenate(dcb_parts, axis=1))

    def col(c):
        return pl.BlockSpec((tb, GROUP_W), lambda i: (nt - 1 - i, c))

    def prev(c):
        return pl.BlockSpec((8, GROUP_W), lambda i: (jnp.maximum((nt - 1 - i) * (tb // 8) - 1, 0), c))

    out = pl.BlockSpec((tb, GROUP_W), lambda i: (nt - 1 - i, 0))
    vec = _full((1, LANES))
    return _call(
        body, name="ssd_bwd", grid=(nt,),
        out_shape=[jax.ShapeDtypeStruct((t, GROUP_W), bf16)] * 4 + [jax.ShapeDtypeStruct((t, LANES), bf16),
                   jax.ShapeDtypeStruct((8, 768), f32), jax.ShapeDtypeStruct((1, 768), f32)]
        + [jax.ShapeDtypeStruct((1, LANES), f32)] * 3,
        in_specs=[col(C_Z), col(C_XS), col(C_BM), col(C_CM), prev(C_XS), prev(C_BM), prev(C_CM),
                  pl.BlockSpec((tb, LANES), lambda i: (nt - 1 - i, C_DT128)), out, out,
                  pl.BlockSpec((cpt, 2, 128, 128), lambda i: (nt - 1 - i, 0, 0, 0)),
                  _full((8, 768)), _full((1, 768)), vec, vec, _full((1, GROUP_W))],
        out_specs=[out, out, out, out, pl.BlockSpec((tb, LANES), lambda i: (nt - 1 - i, 0)),
                   _full((8, 768)), _full((1, 768)), vec, vec, vec],
        scratch_shapes=[pltpu.VMEM((2, 128, 128), f32), pltpu.VMEM((8, 768), f32)],
        semantics=("arbitrary",), xchg=xchg,
        args=(proj, proj, proj, proj, proj, proj, proj, proj, dyc, y_pre, states, conv_w, conv_b, dt_bias, a_log, d_cols))


def _s5_coeffs(are, aim, ls):
    step = jnp.exp(ls)
    mag = jnp.exp(are * step)
    th = aim * step
    lre, lim = mag * jnp.cos(th), mag * jnp.sin(th)
    den = are * are + aim * aim
    nr = lre - 1.0
    fre = (nr * are + lim * aim) / den
    fim = (lim * are - nr * aim) / den
    return step, lre, lim, den, fre, fim


def _s5_prep(are, aim, ls, bre_bd, bim_bd):
    def body(are_ref, aim_ref, ls_ref, bre_ref, bim_ref, lre_ref, lim_ref, bbr_ref, bbi_ref):
        _, lre, lim, _, fre, fim = _s5_coeffs(are_ref[...], aim_ref[...], ls_ref[...])
        lre_ref[...] = lre
        lim_ref[...] = lim
        bre, bim = bre_ref[...], bim_ref[...]
        bbr_ref[...] = fre * bre - fim * bim
        bbi_ref[...] = fre * bim + fim * bre

    col = jax.ShapeDtypeStruct((S5_N, 1), f32)
    mat = jax.ShapeDtypeStruct((S5_N, GROUP_W), f32)
    return pl.pallas_call(body, name="s5_prep", out_shape=[col, col, mat, mat], in_specs=[VMEM] * 5, out_specs=[VMEM] * 4,
                          compiler_params=_params())(are, aim, ls, bre_bd, bim_bd)


def _s5_prep_bwd(are, aim, ls, bre_bd, bim_bd, dlre, dlim, dbbr, dbbi):
    def body(are_ref, aim_ref, ls_ref, bre_ref, bim_ref, dlre_ref, dlim_ref, dbbr_ref, dbbi_ref,
             dare_ref, daim_ref, dls_ref, dbre_ref, dbim_ref):
        are, aim = are_ref[...], aim_ref[...]
        step, lre, lim, den, fre, fim = _s5_coeffs(are, aim, ls_ref[...])
        r = lax.broadcasted_iota(jnp.int32, (S5_N, GROUP_W), 0) // 64
        c = lax.broadcasted_iota(jnp.int32, (S5_N, GROUP_W), 1) // 16
        mask = r == c
        gr = jnp.where(mask, dbbr_ref[...], 0.0)
        gi = jnp.where(mask, dbbi_ref[...], 0.0)
        bre, bim = bre_ref[...], bim_ref[...]
        dbre_ref[...] = fre * gr + fim * gi
        dbim_ref[...] = fre * gi - fim * gr
        dfre = jnp.sum(bre * gr + bim * gi, axis=1, keepdims=True)
        dfim = jnp.sum(bre * gi - bim * gr, axis=1, keepdims=True)
        ire, iim = are / den, aim / den
        tre = dlre_ref[...] + ire * dfre - iim * dfim
        tim = dlim_ref[...] + ire * dfim + iim * dfre
        dzre = lre * tre + lim * tim
        dzim = lre * tim - lim * tre
        qre = (fre * are + fim * aim) / den
        qim = (fim * are - fre * aim) / den
        dare_ref[...] = step * dzre - (qre * dfre + qim * dfim)
        daim_ref[...] = step * dzim - (qre * dfim - qim * dfre)
        dls = (are * dzre + aim * dzim) * step
        sel = (lax.broadcasted_iota(jnp.int32, (S5_N, LANES), 0) // 64 == lax.broadcasted_iota(jnp.int32, (S5_N, LANES), 1)).astype(f32)
        dls_ref[...] = lax.dot_general(sel, jnp.broadcast_to(dls, (S5_N, LANES)), (((0,), (0,)), ((), ())),
                                       preferred_element_type=f32, precision=lax.Precision.HIGHEST)

    col = jax.ShapeDtypeStruct((S5_N, 1), f32)
    mat = jax.ShapeDtypeStruct((S5_N, GROUP_W), f32)
    return pl.pallas_call(body, name="s5_prep_bwd", out_shape=[col, col, jax.ShapeDtypeStruct((LANES, LANES), f32), mat, mat],
                          in_specs=[VMEM] * 9, out_specs=[VMEM] * 5, compiler_params=_params(),
                          )(are, aim, ls, bre_bd, bim_bd, dlre, dlim, dbbr, dbbi)


def _cmul(ar, ai, br, bi):
    return ar * br - ai * bi, ar * bi + ai * br


def _s5_scan(re_ref, im_ref, carry_ref, mr, mi, n_groups, reverse):
    p1 = (mr, mi)
    p2 = _cmul(*p1, *p1)
    p3 = _cmul(*p2, *p1)
    p4 = _cmul(*p2, *p2)
    p5 = _cmul(*p4, *p1)
    p6 = _cmul(*p4, *p2)
    p7 = _cmul(*p4, *p3)
    p8 = _cmul(*p4, *p4)
    pows = [p1, p2, p3, p4, p5, p6, p7, p8]
    row = lax.broadcasted_iota(jnp.int32, (8, S5_N), 0)
    tr = jnp.zeros((8, S5_N), f32)
    ti = jnp.zeros((8, S5_N), f32)
    for i in range(8):
        p = pows[7 - i] if reverse else pows[i]
        tr = jnp.where(row == i, p[0], tr)
        ti = jnp.where(row == i, p[1], ti)
    steps = []
    for k, p in ((1, p1), (2, p2), (4, p4)):
        keep = (row + k < 8) if reverse else (row >= k)
        steps.append((8 - k if reverse else k, jnp.where(keep, p[0], 0.0), jnp.where(keep, p[1], 0.0)))
    edge = 0 if reverse else 7

    def step(j, carry):
        cr, ci = carry
        g = (n_groups - 1 - j) if reverse else j
        r0 = pl.multiple_of(g * 8, 8)
        xr = re_ref[pl.ds(r0, 8), :]
        xi = im_ref[pl.ds(r0, 8), :]
        for shift, br, bi in steps:
            sr = pltpu.roll(xr, shift, 0)
            si = pltpu.roll(xi, shift, 0)
            xr, xi = xr + br * sr - bi * si, xi + br * si + bi * sr
        xr, xi = xr + tr * cr - ti * ci, xi + tr * ci + ti * cr
        re_ref[pl.ds(r0, 8), :] = xr
        im_ref[pl.ds(r0, 8), :] = xi
        return (jnp.broadcast_to(xr[edge:edge + 1, :], (8, S5_N)), jnp.broadcast_to(xi[edge:edge + 1, :], (8, S5_N)))

    cr, ci = lax.fori_loop(0, n_groups, step, (carry_ref[0], carry_ref[1]))
    carry_ref[0] = cr
    carry_ref[1] = ci


def _s5_output(u, xr, xi, ctr, cti, d):
    return _dot_nt(_b(xr), _b(ctr)) - _dot_nt(_b(xi), _b(cti)) + d * u


def _s5_fwd(proj, bbr, bbi, ctr, cti, lre, lim, d, glu_w, glu_b, tb, xchg=None):
    t = proj.shape[0]

    def body(u_ref, bbr_ref, bbi_ref, ctr_ref, cti_ref, lr_ref, li_ref, d_ref, gw_ref, gb_ref, o_ref, xr_ref, xi_ref, carry):
        @pl.when(pl.program_id(0) == 0)
        def _():
            carry[...] = jnp.zeros_like(carry)

        u = u_ref[...]
        ub = _b(u)
        xr_ref[...] = _dot_nt(ub, _b(bbr_ref[...]))
        xi_ref[...] = _dot_nt(ub, _b(bbi_ref[...]))
        _s5_scan(xr_ref, xi_ref, carry, lr_ref[...], li_ref[...], tb // 8, reverse=False)
        y = _s5_output(u, xr_ref[...], xi_ref[...], ctr_ref[...], cti_ref[...], d_ref[...])
        gl = _gelu(y)
        o_ref[...] = _b(gl * jax.nn.sigmoid(_dot(_b(gl), _b(gw_ref[...])) + gb_ref[...]))

    state = pl.BlockSpec((tb, S5_N), lambda i: (i, 0))
    return _call(
        body, name="s5_fwd", grid=(t // tb,),
        out_shape=[jax.ShapeDtypeStruct((t, GROUP_W), bf16), jax.ShapeDtypeStruct((t, S5_N), f32), jax.ShapeDtypeStruct((t, S5_N), f32)],
        in_specs=[pl.BlockSpec((tb, GROUP_W), lambda i: (i, C_S5)), _full((S5_N, GROUP_W)), _full((S5_N, GROUP_W)),
                  _full((GROUP_W, S5_N)), _full((GROUP_W, S5_N)), _full((1, S5_N)), _full((1, S5_N)),
                  _full((1, GROUP_W)), _full((GROUP_W, GROUP_W)), _full((1, GROUP_W))],
        out_specs=[pl.BlockSpec((tb, GROUP_W), lambda i: (i, 0)), state, state],
        scratch_shapes=[pltpu.VMEM((2, 8, S5_N), f32)],
        semantics=("arbitrary",), xchg=xchg, args=(proj, bbr, bbi, ctr, cti, lre, lim, d, glu_w, glu_b))


def _s5_bwd(proj, dyd, xr_all, xi_all, bbr, bbi, ctr, cti, lre, lim, d, glu_w, glu_b, tb, xchg=None):
    t = proj.shape[0]
    nt = t // tb

    def body(u_ref, dy_ref, xr_ref, xi_ref, xrh_ref, xih_ref, bbr_ref, bbi_ref, ctr_ref, cti_ref, lr_ref, li_ref,
             d_ref, gw_ref, gb_ref,
             du_ref, dlr_ref, dli_ref, dbbr_ref, dbbi_ref, dctr_ref, dcti_ref, dd_ref, dgw_ref, dgb_ref,
             gr_ref, gi_ref, carry):
        i = pl.program_id(0)
        ti = nt - 1 - i

        @pl.when(i == 0)
        def _():
            carry[...] = jnp.zeros_like(carry)

        u = u_ref[...]
        ub = _b(u)
        xr, xi = xr_ref[...], xi_ref[...]
        ctr, cti = _b(ctr_ref[...]), _b(cti_ref[...])
        d = d_ref[...]
        gw = _b(gw_ref[...])
        y = _s5_output(u, xr, xi, ctr, cti, d)
        gl = _gelu(y)
        sg = jax.nn.sigmoid(_dot(_b(gl), gw) + gb_ref[...])
        dout = dy_ref[...]
        q = dout * gl * sg * (1.0 - sg)
        qb = _b(q)
        dgl = dout * sg + _dot_nt(qb, gw)
        _acc(dgw_ref, _dot_tn(_b(gl), qb))
        _acc(dgb_ref, _colsum(q))
        dyv = dgl * _dgelu(y)
        _acc(dd_ref, _colsum(dyv * u))
        dyb = _b(dyv)
        gr_ref[...] = _dot(dyb, ctr)
        gi_ref[...] = -_dot(dyb, cti)
        _acc(dctr_ref, _dot_tn(dyb, _b(xr)))
        _acc(dcti_ref, -_dot_tn(dyb, _b(xi)))
        _s5_scan(gr_ref, gi_ref, carry, lr_ref[...], -li_ref[...], tb // 8, reverse=True)
        gr, gi = gr_ref[...], gi_ref[...]
        xpr = jnp.concatenate([jnp.where(ti > 0, xrh_ref[...], 0.0), xr], axis=0)[7:7 + tb]
        xpi = jnp.concatenate([jnp.where(ti > 0, xih_ref[...], 0.0), xi], axis=0)[7:7 + tb]
        _acc(dlr_ref, _colsum(gr * xpr + gi * xpi))
        _acc(dli_ref, _colsum(gi * xpr - gr * xpi))
        grb, gib = _b(gr), _b(gi)
        _acc(dbbr_ref, _dot_tn(grb, ub))
        _acc(dbbi_ref, _dot_tn(gib, ub))
        du_ref[...] = _b(dyv * d + _dot(grb, _b(bbr_ref[...])) + _dot(gib, _b(bbi_ref[...])))

    state = pl.BlockSpec((tb, S5_N), lambda i: (nt - 1 - i, 0))
    prev = pl.BlockSpec((8, S5_N), lambda i: (jnp.maximum((nt - 1 - i) * (tb // 8) - 1, 0), 0))
    tile = pl.BlockSpec((tb, GROUP_W), lambda i: (nt - 1 - i, 0))
    return _call(
        body, name="s5_bwd", grid=(nt,),
        out_shape=[jax.ShapeDtypeStruct((t, GROUP_W), bf16), jax.ShapeDtypeStruct((1, S5_N), f32), jax.ShapeDtypeStruct((1, S5_N), f32),
                   jax.ShapeDtypeStruct((S5_N, GROUP_W), f32), jax.ShapeDtypeStruct((S5_N, GROUP_W), f32),
                   jax.ShapeDtypeStruct((GROUP_W, S5_N), f32), jax.ShapeDtypeStruct((GROUP_W, S5_N), f32),
                   jax.ShapeDtypeStruct((1, GROUP_W), f32), jax.ShapeDtypeStruct((GROUP_W, GROUP_W), f32),
                   jax.ShapeDtypeStruct((1, GROUP_W), f32)],
        in_specs=[pl.BlockSpec((tb, GROUP_W), lambda i: (nt - 1 - i, C_S5)), tile, state, state, prev, prev,
                  _full((S5_N, GROUP_W)), _full((S5_N, GROUP_W)), _full((GROUP_W, S5_N)), _full((GROUP_W, S5_N)),
                  _full((1, S5_N)), _full((1, S5_N)), _full((1, GROUP_W)), _full((GROUP_W, GROUP_W)), _full((1, GROUP_W))],
        out_specs=[tile, _full((1, S5_N)), _full((1, S5_N)), _full((S5_N, GROUP_W)), _full((S5_N, GROUP_W)),
                   _full((GROUP_W, S5_N)), _full((GROUP_W, S5_N)), _full((1, GROUP_W)), _full((GROUP_W, GROUP_W)), _full((1, GROUP_W))],
        scratch_shapes=[pltpu.VMEM((tb, S5_N), f32), pltpu.VMEM((tb, S5_N), f32), pltpu.VMEM((2, 8, S5_N), f32)],
        semantics=("arbitrary",), xchg=xchg,
        args=(proj, dyd, xr_all, xi_all, xr_all, xi_all, bbr, bbi, ctr, cti, lre, lim, d, glu_w, glu_b))


def _outproj_fwd(ys, h, bn_w, g1, w_out, tb):
    t = h.shape[0]

    def body(ya_ref, yb_ref, yc_ref, yd_ref, h_ref, bn_ref, g1_ref, w_ref, h1_ref, o_ref, gr_ref):
        bn = bn_ref[...]
        parts = []
        for g, r in enumerate((ya_ref, yb_ref, yc_ref, yd_ref)):
            n, _ = _rms(r[...].astype(f32))
            parts.append(n * bn[:, g * GROUP_W:(g + 1) * GROUP_W])
        groups = _b(jnp.concatenate(parts, axis=1))
        gr_ref[...] = groups
        o = _dot(groups, w_ref[...])
        o_ref[...] = _b(o)
        h1_ref[...] = h_ref[...] + g1_ref[...] * o

    grp = pl.BlockSpec((tb, GROUP_W), lambda i: (i, 0))
    row = pl.BlockSpec((tb, D_MODEL), lambda i: (i, 0))
    vec = _full((1, D_MODEL))
    return pl.pallas_call(
        body, name="outproj_fwd", grid=(t // tb,),
        out_shape=[jax.ShapeDtypeStruct((t, D_MODEL), f32), jax.ShapeDtypeStruct((t, D_MODEL), bf16),
                   jax.ShapeDtypeStruct((t, D_MODEL), bf16)],
        in_specs=[grp, grp, grp, grp, row, vec, vec, _full((D_MODEL, D_MODEL))],
        out_specs=[row, row, row],
        compiler_params=_params(("parallel",)),
    )(*ys, h, bn_w, g1, w_out)


def _outproj_bwd(dh1, o, ys, bn_w, g1, w_out, tb):
    t = dh1.shape[0]

    def body(dh_ref, o_ref, ya_ref, yb_ref, yc_ref, yd_ref, bn_ref, g1_ref, w_ref,
             da_ref, db_ref, dc_ref, dd_ref, do_ref, dg1_ref, dbn_ref):
        dh = dh_ref[...]
        _acc(dg1_ref, _colsum(dh * o_ref[...].astype(f32)))
        do = _b(dh * g1_ref[...])
        do_ref[...] = do
        dgroups = _dot_nt(do, w_ref[...])
        bn = bn_ref[...]
        dbn = []
        for g, (r, dr) in enumerate(((ya_ref, da_ref), (yb_ref, db_ref), (yc_ref, dc_ref), (yd_ref, dd_ref))):
            n, rr = _rms(r[...].astype(f32))
            dgr = dgroups[:, g * GROUP_W:(g + 1) * GROUP_W]
            dbn.append(_colsum(dgr * n))
            dr[...] = _rms_bwd(dgr * bn[:, g * GROUP_W:(g + 1) * GROUP_W], n, rr)
        _acc(dbn_ref, jnp.concatenate(dbn, axis=1))

    grp = pl.BlockSpec((tb, GROUP_W), lambda i: (i, 0))
    row = pl.BlockSpec((tb, D_MODEL), lambda i: (i, 0))
    vec = _full((1, D_MODEL))
    return pl.pallas_call(
        body, name="outproj_bwd", grid=(t // tb,),
        out_shape=[jax.ShapeDtypeStruct((t, GROUP_W), f32)] * 4 + [jax.ShapeDtypeStruct((t, D_MODEL), bf16),
                   jax.ShapeDtypeStruct((1, D_MODEL), f32), jax.ShapeDtypeStruct((1, D_MODEL), f32)],
        in_specs=[row, row, grp, grp, grp, grp, vec, vec, _full((D_MODEL, D_MODEL))],
        out_specs=[grp, grp, grp, grp, row, vec, vec],
        compiler_params=_params(("arbitrary",)),
    )(dh1, o, *ys, bn_w, g1, w_out)


def _mlp_fwd(h1, norm_w, sc, sh, g2, w1, w2, tb, xchg=None):
    t = h1.shape[0]
    nh = w1.shape[0] // MLP_SLABS

    def body(h_ref, nw_ref, sc_ref, sh_ref, g2_ref, w1_ref, w2_ref, h2_ref, m_ref, v_ref, r_ref, acc):
        j = pl.program_id(1)

        @pl.when(j == 0)
        def _():
            n, _ = _rms(h_ref[...])
            v_ref[...] = _b(n * nw_ref[...] * (1.0 + sc_ref[...]) + sh_ref[...])

        v = v_ref[...]
        p = None
        for s in range(MLP_SLABS):
            ra = jnp.maximum(_dot(v, w1_ref[s]), 0.0)
            r = _b(ra * ra)
            r_ref[:, s * MLP_HB:(s + 1) * MLP_HB] = r
            q = _dot(r, w2_ref[s])
            p = q if p is None else p + q

        @pl.when(j == 0)
        def _():
            acc[...] = p

        @pl.when(j != 0)
        def _():
            acc[...] += p

        @pl.when(j == nh - 1)
        def _():
            m = acc[...]
            m_ref[...] = _b(m)
            h2_ref[...] = h_ref[...] + g2_ref[...] * m

    row = pl.BlockSpec((tb, D_MODEL), lambda i, j: (i, 0))
    hid = pl.BlockSpec((tb, MLP_SLABS * MLP_HB), lambda i, j: (i, j))
    vec = _full((1, D_MODEL))
    return _call(
        body, name="mlp_fwd", grid=(t // tb, nh),
        out_shape=[jax.ShapeDtypeStruct((t, D_MODEL), f32), jax.ShapeDtypeStruct((t, D_MODEL), bf16),
                   jax.ShapeDtypeStruct((t, D_MODEL), bf16), jax.ShapeDtypeStruct((t, N_DEV * MLP_HB), bf16)],
        in_specs=[row, vec, vec, vec, vec, pl.BlockSpec((MLP_SLABS, D_MODEL, MLP_HB), lambda i, j: (j, 0, 0)),
                  pl.BlockSpec((MLP_SLABS, MLP_HB, D_MODEL), lambda i, j: (j, 0, 0))],
        out_specs=[row, row, row, hid],
        scratch_shapes=[pltpu.VMEM((tb, D_MODEL), f32)],
        semantics=("arbitrary", "arbitrary"), xchg=xchg, args=(h1, norm_w, sc, sh, g2, w1, w2))


def _mlp_bwd(dh2, m, h1, r, norm_w, sc, sh, g2, w1, w2, tb, xchg=None):
    t = h1.shape[0]
    slabs = MLP_BWD_SLABS
    nh = w1.shape[0] // slabs

    def body(dh_ref, m_ref, h_ref, r_ref, nw_ref, sc_ref, sh_ref, g2_ref, w1_ref, w2_ref,
             dh1_ref, do_ref, da_ref, dg2_ref, dsh_ref, dsc_ref, dnw_ref, acc):
        j = pl.program_id(1)

        @pl.when(j == 0)
        def _():
            dh = dh_ref[...]
            _acc(dg2_ref, _colsum(dh * m_ref[...].astype(f32)))
            do_ref[...] = _b(dh * g2_ref[...])

        do = do_ref[...]
        p = None
        for s in range(slabs):
            cols = slice(s * MLP_HB, (s + 1) * MLP_HB)
            dr = _dot_nt(do, w2_ref[s])
            da = _b(dr * 2.0 * jnp.sqrt(r_ref[:, cols].astype(f32)))
            da_ref[:, cols] = da
            q = _dot_nt(da, w1_ref[s])
            p = q if p is None else p + q

        @pl.when(j == 0)
        def _():
            acc[...] = p

        @pl.when(j != 0)
        def _():
            acc[...] += p

        @pl.when(j == nh - 1)
        def _():
            dv = acc[...]
            n, r = _rms(h_ref[...])
            nw = nw_ref[...]
            gain = 1.0 + sc_ref[...]
            _acc(dsh_ref, _colsum(dv))
            _acc(dsc_ref, _colsum(dv * n * nw))
            _acc(dnw_ref, _colsum(dv * gain * n))
            dh1_ref[...] = dh_ref[...] + _rms_bwd(dv * nw * gain, n, r)

    row = pl.BlockSpec((tb, D_MODEL), lambda i, j: (i, 0))
    hid = pl.BlockSpec((tb, slabs * MLP_HB), lambda i, j: (i, j))
    vec = _full((1, D_MODEL))
    once = dict(pipeline_mode=pl.Buffered(1)) if nh == 1 else {}
    return _call(
        body, name="mlp_bwd", grid=(t // tb, nh),
        out_shape=[jax.ShapeDtypeStruct((t, D_MODEL), f32), jax.ShapeDtypeStruct((t, D_MODEL), bf16),
                   jax.ShapeDtypeStruct((t, N_DEV * MLP_HB), bf16)] + [jax.ShapeDtypeStruct((1, D_MODEL), f32)] * 4,
        in_specs=[row, row, row, hid, vec, vec, vec, vec,
                  pl.BlockSpec((slabs, D_MODEL, MLP_HB), lambda i, j: (j, 0, 0), **once),
                  pl.BlockSpec((slabs, MLP_HB, D_MODEL), lambda i, j: (j, 0, 0), **once)],
        out_specs=[row, row, hid, vec, vec, vec, vec],
        scratch_shapes=[pltpu.VMEM((tb, D_MODEL), f32)],
        semantics=("arbitrary", "arbitrary"), xchg=xchg, args=(dh2, m, h1, r, norm_w, sc, sh, g2, w1, w2))


def _loss_head(h, target, norm_w, tb):
    t = h.shape[0]

    def body(h_ref, t_ref, w_ref, loss_ref, dh_ref, dw_ref):
        n, r = _rms(h_ref[...])
        w = w_ref[...]
        err = n * w - t_ref[...]
        part = 0.5 * jnp.sum(jnp.sum(err * err, axis=1, keepdims=True), axis=0, keepdims=True) / D_MODEL
        _acc(loss_ref, jnp.broadcast_to(part, (8, LANES)))
        dy = err / D_MODEL
        _acc(dw_ref, _colsum(dy * n))
        dh_ref[...] = _rms_bwd(dy * w, n, r)

    row = pl.BlockSpec((tb, D_MODEL), lambda i: (i, 0))
    return pl.pallas_call(
        body, name="loss_head", grid=(t // tb,),
        out_shape=[jax.ShapeDtypeStruct((8, LANES), f32), jax.ShapeDtypeStruct((t, D_MODEL), f32),
                   jax.ShapeDtypeStruct((1, D_MODEL), f32)],
        in_specs=[row, row, _full((1, D_MODEL))],
        out_specs=[_full((8, LANES)), row, _full((1, D_MODEL))],
        compiler_params=_params(("arbitrary",)),
    )(h, target, norm_w)


def _adam_math(w, g, m, v):
    m2 = ADAM_B1 * m + (1.0 - ADAM_B1) * g
    v2 = ADAM_B2 * v + (1.0 - ADAM_B2) * (g * g)
    mh = m2 / (1.0 - ADAM_B1 ** ADAM_STEP)
    vh = v2 / (1.0 - ADAM_B2 ** ADAM_STEP)
    return -ADAM_LR * (mh / (jnp.sqrt(vh) + ADAM_EPS) + ADAM_WD * w), m2, v2


def _adamw_small(ws, gs, ms, vs):
    n = len(ws)
    shapes = [w.shape for w in ws]
    as2d = [(1,) + s if len(s) == 1 else s for s in shapes]
    flat = [x.reshape(s) for group in (ws, gs, ms, vs) for x, s in zip(group, as2d)]

    def body(*refs):
        w_refs, g_refs, m_refs, v_refs, outs = refs[:n], refs[n:2 * n], refs[2 * n:3 * n], refs[3 * n:4 * n], refs[4 * n:]
        for i in range(n):
            d, m2, v2 = _adam_math(w_refs[i][...], g_refs[i][...], m_refs[i][...], v_refs[i][...])
            outs[3 * i][...] = d
            outs[3 * i + 1][...] = m2
            outs[3 * i + 2][...] = v2

    res = pl.pallas_call(body, name="adamw_small", out_shape=[jax.ShapeDtypeStruct(s, f32) for s in as2d for _ in range(3)],
                         in_specs=[VMEM] * (4 * n), out_specs=[VMEM] * (3 * n), compiler_params=_params())(*flat)
    return [r.reshape(shapes[i // 3]) for i, r in enumerate(res)]


def _sum_adamw_layers(parts0, parts1, w, m, v, name, rb):
    n_src, r, c = parts0.shape
    nb = r // rb

    def body(p0_ref, p1_ref, w_ref, m_ref, v_ref, g_ref, d_ref, m2_ref, v2_ref):
        def update(p_ref):
            g = p_ref[0].astype(f32)
            for s in range(1, n_src):
                g = g + p_ref[s].astype(f32)
            g_ref[0] = g
            d, m2, v2 = _adam_math(w_ref[0], g, m_ref[0], v_ref[0])
            d_ref[0] = d
            m2_ref[0] = m2
            v2_ref[0] = v2

        @pl.when(pl.program_id(0) == 0)
        def _():
            update(p0_ref)

        @pl.when(pl.program_id(0) == 1)
        def _():
            update(p1_ref)

    blk = pl.BlockSpec((1, rb, c), lambda l, i: (l, i, 0))
    return pl.pallas_call(
        body, name=name, grid=(2, nb),
        out_shape=[jax.ShapeDtypeStruct((2, r, c), f32)] * 4,
        in_specs=[pl.BlockSpec((n_src, rb, c), lambda l, i: (0, jnp.where(l == 0, i, nb - 1), 0)),
                  pl.BlockSpec((n_src, rb, c), lambda l, i: (0, jnp.where(l == 1, i, 0), 0)), blk, blk, blk],
        out_specs=[blk] * 4,
        compiler_params=_params(("arbitrary", "arbitrary")),
    )(parts0, parts1, w, m, v)


def _reorder_in(w):
    pad = jnp.zeros(w.shape[:-1] + (P_IN - 2308,), w.dtype)
    return jnp.concatenate([w[..., :2048], w[..., 2052:2308], w[..., 2048:2052], pad], axis=-1)


def _unreorder_in(w):
    return jnp.concatenate([w[..., :2048], w[..., 2304:2308], w[..., 2048:2304]], axis=-1)


def _block_diag(w2d, n_blocks):
    rows, cols = w2d.shape
    tiled = jnp.tile(w2d, (1, n_blocks))
    rb = lax.broadcasted_iota(jnp.int32, tiled.shape, 0) // (rows // n_blocks)
    cb = lax.broadcasted_iota(jnp.int32, tiled.shape, 1) // cols
    return jnp.where(rb == cb, tiled, jnp.zeros_like(tiled))


def _block_diag_extract(w_bd, n_blocks):
    rows, wide = w_bd.shape
    r, c = rows // n_blocks, wide // n_blocks
    w4 = w_bd.reshape(n_blocks, r, n_blocks, c)
    idx = jnp.arange(n_blocks)
    return w4[idx, :, idx, :]


def _rows_of(shape):
    n = 1
    for d in shape:
        n *= d
    return -(-n // (8 * LANES)) * 8, n


def _flat_pack(arrs, row_multiple=8):
    blocks = []
    for a in arrs:
        rows, n = _rows_of(a.shape)
        blocks.append(jnp.pad(a.reshape(-1), (0, rows * LANES - n)).reshape(rows, LANES))
    total = sum(b.shape[0] for b in blocks)
    pad = -total % row_multiple
    if pad:
        blocks.append(jnp.zeros((pad, LANES), blocks[0].dtype))
    return jnp.concatenate(blocks, axis=0)


def _flat_unpack(packed, shapes):
    out, off = [], 0
    for s in shapes:
        rows, n = _rows_of(s)
        out.append(packed[off:off + rows].reshape(-1)[:n].reshape(s))
        off += rows
    return out


_W_NAMES = ['norm_mix_w', 'norm_mlp_w', 'ada_w', 'ada_b', 'w_in', 'pool_w', 'pool_scale', 'sconv_w', 'ssd_conv_w',
            'ssd_conv_b', 'ssd_dt_bias', 'ssd_a_log', 'ssd_d', 's5_a_re', 's5_a_im', 's5_log_step', 's5_b_re', 's5_b_im',
            's5_c_re', 's5_c_im', 's5_d', 's5_glu_w', 's5_glu_b', 'branch_norm_w', 'w_out', 'mlp_w1', 'mlp_w2',
            'final_norm_w']
_BIG = ('ada_w', 'w_in', 'w_out', 'mlp_w1', 'mlp_w2')
_SMALL = [n for n in _W_NAMES if n not in _BIG]
_SHARDED_SMALL = {'sconv_w': (2, 32), 'ssd_conv_w': (2, 96), 's5_glu_w': (1, 32)}


def _gather(*blocks):
    return _ChipGather(blocks)


def _scatter(*parts):
    return _Scatter(parts)


def _layer_forward(l, h, p, w, sh_b, tb):
    first = l == 0
    (proj, u_b), got = _inproj_fwd(h, p['norm_mix_w'][l], p['sc1'][l], p['sh1'][l], w['w_in', l], tb,
                                   xchg=_gather(sh_b[1][0]) if first else None)
    if first:
        w['w_out', 0] = got[0].reshape(D_MODEL, D_MODEL)
    ya = _pool_fwd(proj, p['pool_bd'][l], p['pool_scale'][l], tb)
    yb = _sconv_fwd(proj, p['sconv_w8'][l], tb)
    (yc, yc_pre, states), got = _ssd_fwd(proj, p['ssd_conv_w8'][l], p['ssd_conv_b'][l], p['ssd_dt_bias'][l], p['ssd_a_log'][l],
                                         p['ssd_d_cols'][l], tb, xchg=_gather(sh_b[2][0]) if first else None)
    if first:
        w['w1', 0] = got[0]
    (yd, xr, xi), got = _s5_fwd(proj, p['bbr'][l], p['bbi'][l], p['ctr'][l], p['cti'][l], p['lre'][l], p['lim'][l],
                                p['s5_d'][l], p['glu_w'][l], p['glu_b'][l], tb, xchg=_gather(sh_b[3][0]) if first else None)
    if first:
        w['w2', 0] = got[0]
    ys = (ya, yb, yc, yd)
    h1, o, groups_b = _outproj_fwd(ys, h, p['branch_norm_w'][l], p['g1'][l], w['w_out', l], tb)
    (h2, m, v_b, r_b), got = _mlp_fwd(h1, p['norm_mlp_w'][l], p['sc2'][l], p['sh2'][l], p['g2'][l], w['w1', l], w['w2', l],
                                      min(MLP_TB, h.shape[0]), xchg=_gather(*[sh_b[k][1] for k in range(4)]) if first else None)
    if first:
        w['w_in', 1] = got[0].reshape(D_MODEL, P_IN)
        w['w_out', 1] = got[1].reshape(D_MODEL, D_MODEL)
        w['w1', 1], w['w2', 1] = got[2], got[3]
    saved = dict(h=h, proj=proj, u_b=u_b, ys=ys, yc_pre=yc_pre, states=states, xr=xr, xi=xi, h1=h1, o=o,
                 groups_b=groups_b, m=m, v_b=v_b, r_b=r_b)
    return h2, saved


def _layer_backward(l, dh2, s, p, w, pending, recv, tb):
    def carry(names):
        names = [n for n in names if n in pending]
        return names, (_scatter(*[pending.pop(n) for n in names]) if names else None)

    def landed(names, got):
        for n, g in zip(names, got):
            recv[n] = g

    names, xchg = carry([('w_out', 1)])
    (dh1, do2_b, da_b, dg2, dsh2, dsc2, dnw_mlp), got = _mlp_bwd(dh2, s['m'], s['h1'], s['r_b'], p['norm_mlp_w'][l], p['sc2'][l],
                                                                p['sh2'][l], p['g2'][l], w['w1', l], w['w2', l], min(TB_BWD, tb),
                                                                xchg=xchg)
    landed(names, got)
    pending['mlp_w2', l] = _wgrad(s['r_b'], do2_b, 1, "wgrad_w2", tm=1024, tk=4096).reshape(N_DEV, MLP_HB, D_MODEL)
    pending['mlp_w1', l] = _wgrad(s['v_b'], da_b, N_DEV, "wgrad_w1", tm=1024, tk=4096)
    dya, dyb, dyc, dyd, do1_b, dg1, dbn = _outproj_bwd(dh1, s['o'], s['ys'], p['branch_norm_w'][l], p['g1'][l], w['w_out', l], tb)
    pending['w_out', l] = _wgrad(s['groups_b'], do1_b, 1, "wgrad_wout", tm=1024, tk=1024).reshape(N_DEV, D_MODEL // N_DEV, D_MODEL)
    proj = s['proj']
    dv, dpool_bd, dpool_scale = _pool_bwd(proj, dya, p['pool_bd'][l], p['pool_scale'][l], tb)
    dgb, dgc, dhh, dsconv = _sconv_bwd(proj, dyb, p['sconv_w8'][l], tb)
    names, xchg = carry([('mlp_w1', l)] + ([('w_out', 0)] if l == 0 else []))
    (dz, dxs, dbm, dcm, ddt, dconv_w, dconv_b, ddtb, dalog, ddskip), got = _ssd_bwd(
        proj, dyc, s['yc_pre'], s['states'], p['ssd_conv_w8'][l], p['ssd_conv_b'][l], p['ssd_dt_bias'][l], p['ssd_a_log'][l],
        p['ssd_d_cols'][l], min(TB_BWD, tb), xchg=xchg)
    landed(names, got)
    names, xchg = carry([('mlp_w2', l)])
    (du5, dlr, dli, dbbr, dbbi, dctr, dcti, dd5, dgw, dgb5), got = _s5_bwd(
        proj, dyd, s['xr'], s['xi'], p['bbr'][l], p['bbi'][l], p['ctr'][l], p['cti'][l], p['lre'][l], p['lim'][l],
        p['s5_d'][l], p['glu_w'][l], p['glu_b'][l], min(TB_BWD, tb), xchg=xchg)
    landed(names, got)
    dare, daim, dls, dbre_bd, dbim_bd = _s5_prep_bwd(p['are_c'][l], p['aim_c'][l], p['ls_c'][l], p['bre_bd'][l], p['bim_bd'][l],
                                                     dlr.reshape(S5_N, 1), dli.reshape(S5_N, 1), dbbr, dbbi)
    dparts = (dv, dgb, dgc, dhh, dz, dxs, dbm, dcm, du5, ddt)
    pending['w_in', l] = _wgrad_parts(s['u_b'], dparts, "wgrad_win", tm=1024, tk=1024).reshape(N_DEV, D_MODEL // N_DEV, P_IN)
    names, xchg = carry([('w_in', l)])
    (dh, dsh1, dsc1, dnw_mix), got = _inproj_bwd(dparts, dh1, s['h'], p['norm_mix_w'][l], p['sc1'][l], p['sh1'][l], w['w_in', l],
                                                 tb, xchg=xchg)
    landed(names, got)
    small = {
        'norm_mix_w': dnw_mix.reshape(D_MODEL), 'norm_mlp_w': dnw_mlp.reshape(D_MODEL),
        'ada_b': jnp.concatenate([dsh1, dsc1, dg1, dsh2, dsc2, dg2], axis=1).reshape(6 * D_MODEL),
        'pool_w': _block_diag_extract(dpool_bd, 4), 'pool_scale': dpool_scale.reshape(GROUP_W),
        'sconv_w': dsconv[0:3], 'ssd_conv_w': dconv_w[0:4], 'ssd_conv_b': dconv_b.reshape(768),
        'ssd_dt_bias': ddtb[0, 0:4], 'ssd_a_log': dalog[0, 0:4], 'ssd_d': ddskip[0, 0:4],
        's5_a_re': dare.reshape(16, 64), 's5_a_im': daim.reshape(16, 64), 's5_log_step': dls[0:16, 0],
        's5_b_re': _block_diag_extract(dbre_bd, 16), 's5_b_im': _block_diag_extract(dbim_bd, 16),
        's5_c_re': _block_diag_extract(dctr, 16), 's5_c_im': _block_diag_extract(dcti, 16),
        's5_d': dd5.reshape(GROUP_W), 's5_glu_w': dgw, 's5_glu_b': dgb5.reshape(GROUP_W),
        'branch_norm_w': dbn.reshape(D_MODEL),
    }
    return dh, small


def _prepare_params(a, me):
    pack_shapes = [(1, D_MODEL), (2, 3, 32), (2, 4, 96), (2, 32, GROUP_W)]
    packed = _flat_pack([a['c'], a['sconv_w'], a['ssd_conv_w'], a['s5_glu_w']])
    gathered = _gather_rows(packed, "gather_small")
    pieces = [_flat_unpack(gathered[d], pack_shapes) for d in range(N_DEV)]
    c_all = jnp.concatenate([pc[0] for pc in pieces], axis=0)
    sconv_full = jnp.concatenate([pc[1] for pc in pieces], axis=2)
    ssd_conv_full = jnp.concatenate([pc[2] for pc in pieces], axis=2)
    glu_full = jnp.concatenate([pc[3] for pc in pieces], axis=1)

    ada_b_cols = lax.dynamic_slice_in_dim(a['ada_b'], me * 768, 768, axis=1).reshape(2, 1, 768)
    cond, modrows = _ada_forward(c_all, a['ada_w'], ada_b_cols)
    mod_recv = _all_to_all_rows(modrows.transpose(1, 0, 2), "exchange_mod")
    mod = mod_recv.transpose(1, 0, 2).reshape(2, 6 * D_MODEL)
    p = {'cond': cond}
    for k, name in enumerate(('sh1', 'sc1', 'g1', 'sh2', 'sc2', 'g2')):
        p[name] = mod[:, k * D_MODEL:(k + 1) * D_MODEL].reshape(2, 1, D_MODEL)

    for name in ('norm_mix_w', 'norm_mlp_w', 'branch_norm_w'):
        p[name] = a[name].reshape(2, 1, D_MODEL)
    p['pool_bd'] = jnp.stack([_block_diag(a['pool_w'][l].reshape(GROUP_W, 64), 4) for l in range(2)])
    p['pool_scale'] = a['pool_scale'].reshape(2, 1, GROUP_W)
    p['sconv_w8'] = jnp.pad(sconv_full, ((0, 0), (0, 5), (0, 0)))
    p['ssd_conv_w8'] = jnp.pad(ssd_conv_full, ((0, 0), (0, 4), (0, 0)))
    p['ssd_conv_b'] = a['ssd_conv_b'].reshape(2, 1, 768)
    p['ssd_dt_bias'] = jnp.pad(a['ssd_dt_bias'], ((0, 0), (0, LANES - 4))).reshape(2, 1, LANES)
    p['ssd_a_log'] = jnp.pad(a['ssd_a_log'], ((0, 0), (0, LANES - 4))).reshape(2, 1, LANES)
    p['ssd_d_cols'] = jnp.repeat(a['ssd_d'], SSD_P, axis=1).reshape(2, 1, GROUP_W)
    p['are_c'] = a['s5_a_re'].reshape(2, S5_N, 1)
    p['aim_c'] = a['s5_a_im'].reshape(2, S5_N, 1)
    p['ls_c'] = jnp.repeat(a['s5_log_step'], 64, axis=1).reshape(2, S5_N, 1)
    p['bre_bd'] = jnp.stack([_block_diag(a['s5_b_re'][l].reshape(S5_N, 16), 16) for l in range(2)])
    p['bim_bd'] = jnp.stack([_block_diag(a['s5_b_im'][l].reshape(S5_N, 16), 16) for l in range(2)])
    p['ctr'] = jnp.stack([_block_diag(a['s5_c_re'][l].reshape(GROUP_W, 64), 16) for l in range(2)])
    p['cti'] = jnp.stack([_block_diag(a['s5_c_im'][l].reshape(GROUP_W, 64), 16) for l in range(2)])
    p['s5_d'] = a['s5_d'].reshape(2, 1, GROUP_W)
    p['glu_w'] = glu_full
    p['glu_b'] = a['s5_glu_b'].reshape(2, 1, GROUP_W)
    lre, lim, bbr, bbi = [], [], [], []
    for l in range(2):
        r = _s5_prep(p['are_c'][l], p['aim_c'][l], p['ls_c'][l], p['bre_bd'][l], p['bim_bd'][l])
        lre.append(r[0].reshape(1, S5_N))
        lim.append(r[1].reshape(1, S5_N))
        bbr.append(r[2])
        bbi.append(r[3])
    p['lre'], p['lim'], p['bbr'], p['bbi'] = lre, lim, bbr, bbi
    return p


def kernel(x, c, norm_mix_w, norm_mlp_w, ada_w, ada_b, w_in, pool_w, pool_scale, sconv_w, ssd_conv_w, ssd_conv_b, ssd_dt_bias, ssd_a_log, ssd_d, s5_a_re, s5_a_im, s5_log_step, s5_b_re, s5_b_im, s5_c_re, s5_c_im, s5_d, s5_glu_w, s5_glu_b, branch_norm_w, w_out, mlp_w1, mlp_w2, final_norm_w, loss_target, m_norm_mix_w, m_norm_mlp_w, m_ada_w, m_ada_b, m_w_in, m_pool_w, m_pool_scale, m_sconv_w, m_ssd_conv_w, m_ssd_conv_b, m_ssd_dt_bias, m_ssd_a_log, m_ssd_d, m_s5_a_re, m_s5_a_im, m_s5_log_step, m_s5_b_re, m_s5_b_im, m_s5_c_re, m_s5_c_im, m_s5_d, m_s5_glu_w, m_s5_glu_b, m_branch_norm_w, m_w_out, m_mlp_w1, m_mlp_w2, m_final_norm_w, v_norm_mix_w, v_norm_mlp_w, v_ada_w, v_ada_b, v_w_in, v_pool_w, v_pool_scale, v_sconv_w, v_ssd_conv_w, v_ssd_conv_b, v_ssd_dt_bias, v_ssd_a_log, v_ssd_d, v_s5_a_re, v_s5_a_im, v_s5_log_step, v_s5_b_re, v_s5_b_im, v_s5_c_re, v_s5_c_im, v_s5_d, v_s5_glu_w, v_s5_glu_b, v_branch_norm_w, v_w_out, v_mlp_w1, v_mlp_w2, v_final_norm_w):
    a = dict(locals())
    t = x.shape[1]
    tb = min(TB, t)
    me = _my_index()
    p = _prepare_params(a, me)

    sh_b = _cast_shards([_reorder_in(w_in), w_out, mlp_w1, mlp_w2])
    w = {('w_in', 0): _exchange_alone(_gather(sh_b[0][0]), "gather_w_in0")[0].reshape(D_MODEL, P_IN)}

    h = x.reshape(t, D_MODEL)
    saved = []
    for l in range(2):
        h, s = _layer_forward(l, h, p, w, sh_b, tb)
        saved.append(s)
    loss_blk, dh, dfinal = _loss_head(h, loss_target.reshape(t, D_MODEL), final_norm_w.reshape(1, D_MODEL), tb)

    pending, recv, small_parts = {}, {}, [None, None]
    for l in (1, 0):
        dh, small_parts[l] = _layer_backward(l, dh, saved[l], p, w, pending, recv, tb)
    grad_x = dh.reshape(1, t, D_MODEL)

    grads, deltas, new_m, new_v = {}, {}, {}, {}

    wmv_in = [_reorder_in(a[n]) for n in ('w_in', 'm_w_in', 'v_w_in')]
    outs = _sum_adamw_layers(recv['w_in', 0], recv['w_in', 1], *wmv_in, "adamw_w_in", 128)
    grads['w_in'], deltas['w_in'], new_m['w_in'], new_v['w_in'] = [_unreorder_in(o) for o in outs]
    for name, rb in (('w_out', 128), ('mlp_w1', 256), ('mlp_w2', 256)):
        grads[name], deltas[name], new_m[name], new_v[name] = _sum_adamw_layers(
            recv[name, 0], recv[name, 1], a[name], a['m_' + name], a['v_' + name], "adamw_" + name, rb)

    dmod = jnp.stack([small_parts[0]['ada_b'], small_parts[1]['ada_b']])
    dmod_recv = _all_to_all_rows(dmod.reshape(2, N_DEV, 768).transpose(1, 0, 2), "exchange_dmod")
    g_ada = _ada_backward(p['cond'], dmod_recv.transpose(1, 0, 2))
    grads['ada_w'], deltas['ada_w'], new_m['ada_w'], new_v['ada_w'] = _sum_adamw_layers(
        g_ada[0:1], g_ada[1:2], ada_w, m_ada_w, v_ada_w, "adamw_ada_w", 256)

    layered = [n for n in _SMALL if n != 'final_norm_w']
    full = [jnp.stack([small_parts[0][n], small_parts[1][n]]) for n in layered] + [dfinal.reshape(D_MODEL)]
    full.append(loss_blk[0:1, 0:1])
    full_shapes = [f.shape for f in full]
    summed = _flat_unpack(_allreduce_rows(_flat_pack(full, row_multiple=64)), full_shapes)
    loss = summed[-1].reshape(())
    local = []
    for n, g in zip(_SMALL, summed):
        if n in _SHARDED_SMALL:
            axis, size = _SHARDED_SMALL[n]
            g = lax.dynamic_slice_in_dim(g, me * size, size, axis=axis)
        local.append(g.reshape(a[n].shape))
    outs = _adamw_small([a[n] for n in _SMALL], local, [a['m_' + n] for n in _SMALL], [a['v_' + n] for n in _SMALL])
    for i, n in enumerate(_SMALL):
        grads[n], deltas[n], new_m[n], new_v[n] = local[i], outs[3 * i], outs[3 * i + 1], outs[3 * i + 2]

    return (loss, grad_x, *[grads[n] for n in _W_NAMES], *[deltas[n] for n in _W_NAMES],
            *[new_m[n] for n in _W_NAMES], *[new_v[n] for n in _W_NAMES])
```

```python
import functools

import jax
import jax.numpy as jnp
from jax import lax
from jax.experimental import pallas as pl
from jax.experimental.pallas import tpu as pltpu

f32 = jnp.float32
bf16 = jnp.bfloat16

N_DEV = 8
D_MODEL = 1024
GROUP_W = 256
P_IN = 2432
DT_COL = 2304
SSD_CHUNK = 128
SSD_HEADS = 4
SSD_P = 64
S5_N = 1024
MLP_HB = 512
TB = 1024
TB_BWD = 512
MLP_TB = 1024
MLP_SLABS = 2
MLP_BWD_SLABS = 8
EPS = 1e-6
LANES = 128
VMEM_LIMIT = 56 * 1024 * 1024
ADAM_LR, ADAM_B1, ADAM_B2, ADAM_EPS, ADAM_WD, ADAM_STEP = 0.001, 0.9, 0.999, 1e-08, 0.01, 10
POOL_WINDOWS = (2, 4, 8, 16)

C_POOL, C_GB, C_GC, C_HH, C_Z, C_XS, C_BM, C_CM, C_S5 = range(9)
C_DT128 = DT_COL // LANES

MESH = pl.DeviceIdType.MESH
ANY = pl.BlockSpec(memory_space=pl.ANY)
VMEM = pl.BlockSpec(memory_space=pltpu.VMEM)


def _dot(a, b):
    return jnp.dot(a, b, preferred_element_type=f32)


def _dot_nt(a, b):
    return lax.dot_general(a, b, (((1,), (1,)), ((), ())), preferred_element_type=f32)


def _dot_tn(a, b):
    return lax.dot_general(a, b, (((0,), (0,)), ((), ())), preferred_element_type=f32)


def _dot_exact(a, b):
    return jnp.dot(a, b, preferred_element_type=f32, precision=lax.Precision.HIGHEST)


def _b(x):
    return x.astype(bf16)


def _silu(x):
    return x * jax.nn.sigmoid(x)


def _dsilu(x):
    s = jax.nn.sigmoid(x)
    return s * (1.0 + x * (1.0 - s))


def _softplus(x):
    return jnp.maximum(x, 0.0) + jnp.log1p(jnp.exp(-jnp.abs(x)))


_GELU_K = 0.7978845608028654
_GELU_C = 0.044715


def _gelu(x):
    return 0.5 * x * (1.0 + jnp.tanh(_GELU_K * (x + _GELU_C * x * x * x)))


def _dgelu(x):
    th = jnp.tanh(_GELU_K * (x + _GELU_C * x * x * x))
    return 0.5 * (1.0 + th) + 0.5 * x * (1.0 - th * th) * _GELU_K * (1.0 + 3.0 * _GELU_C * x * x)


def _rms(h):
    r = lax.rsqrt(jnp.mean(h * h, axis=-1, keepdims=True) + EPS)
    return h * r, r


def _rms_bwd(dn, n, r):
    return r * (dn - n * jnp.mean(dn * n, axis=-1, keepdims=True))


def _colsum(x):
    return jnp.sum(x, axis=0, keepdims=True)


def _params(sem=None):
    return pltpu.CompilerParams(dimension_semantics=sem, vmem_limit_bytes=VMEM_LIMIT)


def _full(shape):
    return pl.BlockSpec(shape, lambda *_: (0,) * len(shape))


def _acc(ref, val):
    @pl.when(pl.program_id(0) == 0)
    def _():
        ref[...] = val

    @pl.when(pl.program_id(0) != 0)
    def _():
        ref[...] += val


def _me():
    return lax.axis_index("x"), lax.axis_index("y"), lax.axis_index("c")


def _my_index():
    x, y, c = _me()
    return 4 * x + 2 * y + c


def _coords(p):
    return (p // 4, (p // 2) % 2, p % 2)


class _Scatter:
    def __init__(self, srcs):
        self.srcs = list(srcs)
        self.n = len(self.srcs)
        self.out_shape = [jax.ShapeDtypeStruct(s.shape, s.dtype) for s in self.srcs]
        self.scratch = [pltpu.SemaphoreType.DMA((self.n, N_DEV)), pltpu.SemaphoreType.DMA((self.n, N_DEV)),
                        pltpu.SemaphoreType.DMA((self.n,))]

    def _remote(self, xin, xout, sems, t, k, me, to):
        return pltpu.make_async_remote_copy(
            src_ref=xin[t].at[to], dst_ref=xout[t].at[me], send_sem=sems[0].at[t, k], recv_sem=sems[1].at[t, k],
            device_id=_coords(to), device_id_type=MESH)

    def start(self, xin, xout, sems):
        me = _my_index()
        for t in range(self.n):
            pltpu.make_async_copy(xin[t].at[me], xout[t].at[me], sems[2].at[t]).start()
            for k in range(1, N_DEV):
                self._remote(xin, xout, sems, t, k, me, (me + k) % N_DEV).start()

    def forward(self, xin, xout, sems):
        pass

    def wait(self, xin, xout, sems):
        me = _my_index()
        for t in range(self.n):
            for k in range(1, N_DEV):
                src = (me + N_DEV - k) % N_DEV
                pltpu.make_async_remote_copy(
                    src_ref=xin[t].at[src], dst_ref=xout[t].at[src], send_sem=sems[0].at[t, k],
                    recv_sem=sems[1].at[t, k], device_id=_coords(src), device_id_type=MESH).wait_recv()
        for t in range(self.n):
            for k in range(1, N_DEV):
                self._remote(xin, xout, sems, t, k, me, (me + k) % N_DEV).wait_send()
            pltpu.make_async_copy(xin[t].at[me], xout[t].at[me], sems[2].at[t]).wait()


class _ChipGather:
    def __init__(self, srcs):
        self.srcs = list(srcs)
        self.n = len(self.srcs)
        self.out_shape = [jax.ShapeDtypeStruct((N_DEV,) + s.shape, s.dtype) for s in self.srcs]
        self.scratch = [pltpu.SemaphoreType.DMA((self.n, 7)), pltpu.SemaphoreType.DMA((self.n, 7)),
                        pltpu.SemaphoreType.DMA((self.n,))]

    @staticmethod
    def _places():
        x, y, c = _me()
        chips = [(1 - x, y), (x, 1 - y), (1 - x, 1 - y)]
        return (x, y, c), (x, y, 1 - c), chips

    @staticmethod
    def _slab(ref, dev):
        return ref.at[4 * dev[0] + 2 * dev[1] + dev[2]]

    def _copy(self, xin, xout, sems, t, k, block, to, src=None):
        return pltpu.make_async_remote_copy(
            src_ref=self._slab(xout[t], block) if src is None else src, dst_ref=self._slab(xout[t], block),
            send_sem=sems[0].at[t, k], recv_sem=sems[1].at[t, k], device_id=to, device_id_type=MESH)

    def start(self, xin, xout, sems):
        me, sibling, chips = self._places()
        for t in range(self.n):
            pltpu.make_async_copy(xin[t], self._slab(xout[t], me), sems[2].at[t]).start()
            self._copy(xin, xout, sems, t, 0, me, sibling, src=xin[t]).start()
            for j, chip in enumerate(chips):
                self._copy(xin, xout, sems, t, 1 + j, me, (*chip, me[2]), src=xin[t]).start()

    def forward(self, xin, xout, sems):
        me, sibling, chips = self._places()
        for t in range(self.n):
            for j, chip in enumerate(chips):
                self._copy(xin, xout, sems, t, 1 + j, (*chip, me[2]), me).wait_recv()
                self._copy(xin, xout, sems, t, 4 + j, (*chip, me[2]), sibling).start()

    def wait(self, xin, xout, sems):
        me, sibling, chips = self._places()
        for t in range(self.n):
            self._copy(xin, xout, sems, t, 0, sibling, me).wait_recv()
            for j, chip in enumerate(chips):
                self._copy(xin, xout, sems, t, 4 + j, (*chip, 1 - me[2]), me).wait_recv()
        for t in range(self.n):
            self._copy(xin, xout, sems, t, 0, me, sibling, src=xin[t]).wait_send()
            for j, chip in enumerate(chips):
                self._copy(xin, xout, sems, t, 1 + j, me, (*chip, me[2]), src=xin[t]).wait_send()
                self._copy(xin, xout, sems, t, 4 + j, (*chip, me[2]), sibling).wait_send()
            pltpu.make_async_copy(xin[t], self._slab(xout[t], me), sems[2].at[t]).wait()


def _call(body, *, name, grid, in_specs, out_specs, out_shape, args, semantics, scratch_shapes=(), xchg=None):
    if xchg is None:
        outs = pl.pallas_call(body, name=name, grid=grid, in_specs=in_specs, out_specs=out_specs, out_shape=out_shape,
                              scratch_shapes=list(scratch_shapes), compiler_params=_params(semantics))(*args)
        return outs, ()
    n_in, n_out, n_scr, n = len(in_specs), len(out_specs), len(scratch_shapes), xchg.n

    def carried(*refs):
        ins, xin = refs[:n_in], refs[n_in:n_in + n]
        outs, xout = refs[n_in + n:n_in + n + n_out], refs[n_in + n + n_out:n_in + 2 * n + n_out]
        scr, sems = refs[n_in + 2 * n + n_out:n_in + 2 * n + n_out + n_scr], refs[n_in + 2 * n + n_out + n_scr:]
        step = pl.program_id(0)
        for d in range(1, len(grid)):
            step = step * grid[d] + pl.program_id(d)
        n_steps = functools.reduce(lambda a, b: a * b, grid)

        @pl.when(step == 0)
        def _():
            xchg.start(xin, xout, sems)

        @pl.when(step == (2 * n_steps) // 3)
        def _():
            xchg.forward(xin, xout, sems)

        body(*ins, *outs, *scr)

        @pl.when(step == n_steps - 1)
        def _():
            xchg.wait(xin, xout, sems)

    res = pl.pallas_call(
        carried, name=name, grid=grid, in_specs=list(in_specs) + [ANY] * n, out_specs=list(out_specs) + [ANY] * n,
        out_shape=list(out_shape) + xchg.out_shape, scratch_shapes=list(scratch_shapes) + xchg.scratch,
        compiler_params=_params(("arbitrary",) * len(grid)))(*args, *xchg.srcs)
    return res[:n_out], tuple(res[n_out:])


def _exchange_alone(xchg, name):
    def body(*refs):
        xin, xout, sems = refs[:xchg.n], refs[xchg.n:2 * xchg.n], refs[2 * xchg.n:]
        xchg.start(xin, xout, sems)
        xchg.forward(xin, xout, sems)
        xchg.wait(xin, xout, sems)

    return pl.pallas_call(body, name=name, out_shape=xchg.out_shape, in_specs=[ANY] * xchg.n, out_specs=[ANY] * xchg.n,
                          scratch_shapes=xchg.scratch)(*xchg.srcs)


def _cast_shards(shards):
    n = len(shards)

    def body(*refs):
        for i, o in zip(refs[:n], refs[n:]):
            o[...] = i[...].astype(bf16)

    return pl.pallas_call(body, name="cast_shards", out_shape=[jax.ShapeDtypeStruct(s.shape, bf16) for s in shards],
                          in_specs=[VMEM] * n, out_specs=[VMEM] * n, compiler_params=_params())(*shards)


def _allreduce_rows(v):
    r = v.shape[0]
    rp = r // N_DEV

    def body(v_ref, o_ref, parts, sums, send1, recv1, send2, recv2):
        me = _my_index()

        def piece(ref, d):
            return ref.at[pl.ds(pl.multiple_of(d * rp, 8), rp), :]

        def copy1(k, src_dev, to):
            return pltpu.make_async_remote_copy(src_ref=piece(v_ref, to), dst_ref=parts.at[src_dev], send_sem=send1.at[k],
                                                recv_sem=recv1.at[k], device_id=_coords(to), device_id_type=MESH)

        def copy2(k, owner, to):
            return pltpu.make_async_remote_copy(src_ref=sums, dst_ref=piece(o_ref, owner), send_sem=send2.at[k],
                                                recv_sem=recv2.at[k], device_id=_coords(to), device_id_type=MESH)

        for k in range(1, N_DEV):
            copy1(k, me, (me + k) % N_DEV).start()
        parts[me] = v_ref[pl.ds(pl.multiple_of(me * rp, 8), rp), :]
        for k in range(1, N_DEV):
            copy1(k, (me + N_DEV - k) % N_DEV, me).wait_recv()
        total = parts[0]
        for s in range(1, N_DEV):
            total = total + parts[s]
        sums[...] = total
        o_ref[pl.ds(pl.multiple_of(me * rp, 8), rp), :] = total
        for k in range(1, N_DEV):
            copy2(k, me, (me + k) % N_DEV).start()
        for k in range(1, N_DEV):
            copy2(k, (me + N_DEV - k) % N_DEV, me).wait_recv()
        for k in range(1, N_DEV):
            copy1(k, me, (me + k) % N_DEV).wait_send()
            copy2(k, me, (me + k) % N_DEV).wait_send()

    return pl.pallas_call(
        body, name="allreduce_small_grads", out_shape=jax.ShapeDtypeStruct(v.shape, v.dtype),
        in_specs=[VMEM], out_specs=VMEM,
        scratch_shapes=[pltpu.VMEM((N_DEV, rp, LANES), f32), pltpu.VMEM((rp, LANES), f32)]
        + [pltpu.SemaphoreType.DMA((N_DEV,))] * 4,
        compiler_params=_params(),
    )(v)


def _all_to_all_rows(v, name):
    def body(v_ref, o_ref, send_sems, recv_sems):
        me = _my_index()
        o_ref[me] = v_ref[me]
        sends = []
        for k in range(1, N_DEV):
            peer = (me + k) % N_DEV
            rc = pltpu.make_async_remote_copy(src_ref=v_ref.at[peer], dst_ref=o_ref.at[me], send_sem=send_sems.at[k],
                                              recv_sem=recv_sems.at[k], device_id=_coords(peer), device_id_type=MESH)
            rc.start()
            sends.append(rc)
        for k in range(1, N_DEV):
            src = (me + N_DEV - k) % N_DEV
            pltpu.make_async_remote_copy(src_ref=v_ref.at[src], dst_ref=o_ref.at[src], send_sem=send_sems.at[k],
                                         recv_sem=recv_sems.at[k], device_id=_coords(src), device_id_type=MESH).wait_recv()
        for rc in sends:
            rc.wait_send()

    return pl.pallas_call(
        body, name=name, out_shape=jax.ShapeDtypeStruct(v.shape, v.dtype),
        in_specs=[VMEM], out_specs=VMEM,
        scratch_shapes=[pltpu.SemaphoreType.DMA((N_DEV,)), pltpu.SemaphoreType.DMA((N_DEV,))],
    )(v)


def _ada_forward(c_all, ada_w, ada_b_cols):
    def body(c_ref, w_ref, b_ref, cond_ref, o_ref):
        cond = _silu(c_ref[...])
        cond_ref[...] = cond
        for l in range(2):
            o_ref[l] = _dot(_b(cond), _b(w_ref[l])) + b_ref[l]

    return pl.pallas_call(
        body, name="ada_forward",
        out_shape=[jax.ShapeDtypeStruct((N_DEV, D_MODEL), f32), jax.ShapeDtypeStruct((2, N_DEV, 768), f32)],
        in_specs=[VMEM] * 3, out_specs=[VMEM] * 2, compiler_params=_params(),
    )(c_all, ada_w, ada_b_cols)


def _ada_backward(cond, dmod_rows):
    def body(c_ref, d_ref, o_ref):
        cb = _b(c_ref[...])
        for l in range(2):
            o_ref[l] = _dot_tn(cb, _b(d_ref[l]))

    return pl.pallas_call(
        body, name="ada_backward", out_shape=jax.ShapeDtypeStruct((2, D_MODEL, 768), f32),
        in_specs=[VMEM] * 2, out_specs=VMEM, compiler_params=_params(),
    )(cond, dmod_rows)


def _inproj_fwd(h, norm_w, sc, sh, w_in, tb, xchg=None):
    t = h.shape[0]

    def body(h_ref, nw_ref, sc_ref, sh_ref, w_ref, proj_ref, u_ref):
        n, _ = _rms(h_ref[...])
        u = _b(n * nw_ref[...] * (1.0 + sc_ref[...]) + sh_ref[...])
        u_ref[...] = u
        proj_ref[...] = _dot(u, w_ref[...])

    row = pl.BlockSpec((tb, D_MODEL), lambda i: (i, 0))
    vec = _full((1, D_MODEL))
    return _call(
        body, name="inproj_fwd", grid=(t // tb,),
        out_shape=[jax.ShapeDtypeStruct((t, P_IN), f32), jax.ShapeDtypeStruct((t, D_MODEL), bf16)],
        in_specs=[row, vec, vec, vec, _full((D_MODEL, P_IN))],
        out_specs=[pl.BlockSpec((tb, P_IN), lambda i: (i, 0)), row],
        semantics=("parallel",), args=(h, norm_w, sc, sh, w_in), xchg=xchg)


def _inproj_bwd(dparts, dh_res, h, norm_w, sc, sh, w_in, tb, xchg=None):
    t = h.shape[0]

    def body(*refs):
        parts = refs[:10]
        dres_ref, h_ref, nw_ref, sc_ref, sh_ref, w_ref = refs[10:16]
        dh_ref, dsh_ref, dsc_ref, dnw_ref = refs[16:]
        dproj = jnp.concatenate([p[...] for p in parts], axis=1)
        du = _dot_nt(dproj, w_ref[...])
        n, r = _rms(h_ref[...])
        nw = nw_ref[...]
        gain = 1.0 + sc_ref[...]
        _acc(dsh_ref, _colsum(du))
        _acc(dsc_ref, _colsum(du * n * nw))
        _acc(dnw_ref, _colsum(du * gain * n))
        dh_ref[...] = dres_ref[...] + _rms_bwd(du * nw * gain, n, r)

    row = pl.BlockSpec((tb, D_MODEL), lambda i: (i, 0))
    vec = _full((1, D_MODEL))
    part_specs = [pl.BlockSpec((tb, GROUP_W), lambda i: (i, 0))] * 9 + [pl.BlockSpec((tb, LANES), lambda i: (i, 0))]
    return _call(
        body, name="inproj_bwd", grid=(t // tb,),
        out_shape=[jax.ShapeDtypeStruct((t, D_MODEL), f32)] + [jax.ShapeDtypeStruct((1, D_MODEL), f32)] * 3,
        in_specs=part_specs + [row, row, vec, vec, vec,
                               pl.BlockSpec((D_MODEL, P_IN), lambda i: (0, 0), pipeline_mode=pl.Buffered(1))],
        out_specs=[row, vec, vec, vec],
        semantics=("arbitrary",), xchg=xchg, args=(*dparts, dh_res, h, norm_w, sc, sh, w_in))


def _wgrad(a, b, n_blocks, name, tm, tk=512):
    t, m = a.shape
    nb = b.shape[1] // n_blocks
    tk = min(tk, t)
    nk = t // tk

    def body(a_ref, b_ref, o_ref, acc_ref):
        k = pl.program_id(2)
        p = _dot_tn(a_ref[...], b_ref[...])

        @pl.when(k == 0)
        def _():
            acc_ref[...] = p

        @pl.when(k != 0)
        def _():
            acc_ref[...] += p

        @pl.when(k == nk - 1)
        def _():
            o_ref[0] = acc_ref[...].astype(o_ref.dtype)

    return pl.pallas_call(
        body, name=name, grid=(m // tm, n_blocks, nk),
        out_shape=jax.ShapeDtypeStruct((n_blocks, m, nb), bf16),
        in_specs=[pl.BlockSpec((tk, tm), lambda i, j, k: (k, i)), pl.BlockSpec((tk, nb), lambda i, j, k: (k, j))],
        out_specs=pl.BlockSpec((1, tm, nb), lambda i, j, k: (j, i, 0)),
        scratch_shapes=[pltpu.VMEM((tm, nb), f32)],
        compiler_params=_params(("parallel", "parallel", "arbitrary")),
    )(a, b)


def _wgrad_parts(a, parts, name, tm, tk):
    t, m = a.shape
    n = sum(p.shape[1] for p in parts)
    n_parts = len(parts)
    tk = min(tk, t)
    nk = t // tk

    def body(*refs):
        a_ref, part_refs, o_ref, acc_ref = refs[0], refs[1:1 + n_parts], refs[1 + n_parts], refs[2 + n_parts]
        k = pl.program_id(1)
        p = _dot_tn(a_ref[...], jnp.concatenate([r[...] for r in part_refs], axis=1))

        @pl.when(k == 0)
        def _():
            acc_ref[...] = p

        @pl.when(k != 0)
        def _():
            acc_ref[...] += p

        @pl.when(k == nk - 1)
        def _():
            o_ref[...] = acc_ref[...].astype(o_ref.dtype)

    return pl.pallas_call(
        body, name=name, grid=(m // tm, nk),
        out_shape=jax.ShapeDtypeStruct((m, n), bf16),
        in_specs=[pl.BlockSpec((tk, tm), lambda i, k: (k, i))]
        + [pl.BlockSpec((tk, p.shape[1]), lambda i, k: (k, 0)) for p in parts],
        out_specs=pl.BlockSpec((tm, n), lambda i, k: (i, 0)),
        scratch_shapes=[pltpu.VMEM((tm, n), f32)],
        compiler_params=_params(("parallel", "arbitrary")),
    )(a, *parts)


def _pool_counts(rows, t0):
    tpos = (lax.broadcasted_iota(jnp.int32, (rows, GROUP_W), 0) + t0 + 1).astype(f32)
    grp = lax.broadcasted_iota(jnp.int32, (rows, GROUP_W), 1) // 64
    win = jnp.where(grp == 0, 2.0, jnp.where(grp == 1, 4.0, jnp.where(grp == 2, 8.0, 16.0)))
    return jnp.minimum(tpos, win), grp


def _pool_select(grp, l1, l2, l3, l4):
    return jnp.where(grp == 0, l1, jnp.where(grp == 1, l2, jnp.where(grp == 2, l3, l4)))


def _pool_means(v, halo, t0):
    tb = v.shape[0]
    ext = jnp.concatenate([halo, v], axis=0)
    n = tb + 16
    s1 = ext[1:n] + ext[0:n - 1]
    s2 = s1[2:n - 1] + s1[0:n - 3]
    s3 = s2[4:n - 3] + s2[0:n - 7]
    s4 = s3[8:n - 7] + s3[0:n - 15]
    cnt, grp = _pool_counts(tb, t0)
    wsum = _pool_select(grp, s1[15:15 + tb], s2[13:13 + tb], s3[9:9 + tb], s4[1:1 + tb])
    return wsum / cnt - v


def _pool_fwd(proj, pw_bd, scale, tb):
    t = proj.shape[0]

    def body(v_ref, vh_ref, pw_ref, sc_ref, o_ref):
        i = pl.program_id(0)
        halo = jnp.where(i > 0, vh_ref[...], 0.0)
        p = _pool_means(v_ref[...], halo, i * tb)
        o_ref[...] = _b(_dot(_b(p), _b(pw_ref[...])) * sc_ref[...])

    return pl.pallas_call(
        body, name="pool_fwd", grid=(t // tb,),
        out_shape=jax.ShapeDtypeStruct((t, GROUP_W), bf16),
        in_specs=[pl.BlockSpec((tb, GROUP_W), lambda i: (i, C_POOL)),
                  pl.BlockSpec((16, GROUP_W), lambda i: (jnp.maximum(i * (tb // 16) - 1, 0), C_POOL)),
                  _full((GROUP_W, GROUP_W)), _full((1, GROUP_W))],
        out_specs=pl.BlockSpec((tb, GROUP_W), lambda i: (i, 0)),
        compiler_params=_params(("parallel",)),
    )(proj, proj, pw_bd, scale)


def _pool_bwd(proj, dy, pw_bd, scale, tb):
    t = proj.shape[0]
    nt = t // tb
    last16 = t // 16 - 1

    def body(v_ref, vh_ref, dy_ref, dyh_ref, pw_ref, sc_ref, dv_ref, dpw_ref, dsc_ref):
        i = pl.program_id(0)
        halo = jnp.where(i > 0, vh_ref[...], 0.0)
        p = _pool_means(v_ref[...], halo, i * tb)
        pw = _b(pw_ref[...])
        sc = sc_ref[...]
        dy = dy_ref[...]
        ypre = _dot(_b(p), pw)
        _acc(dsc_ref, _colsum(dy * ypre))
        dys = _b(dy * sc)
        _acc(dpw_ref, _dot_tn(_b(p), dys))
        dp = _dot_nt(dys, pw)
        dph = _dot_nt(_b(jnp.where(i < nt - 1, dyh_ref[...], 0.0) * sc), pw)
        cnt, grp = _pool_counts(tb, i * tb)
        cnth, _ = _pool_counts(16, (i + 1) * tb)
        ext = jnp.concatenate([dp / cnt, dph / cnth], axis=0)
        n = tb + 16
        f1 = ext[0:n - 1] + ext[1:n]
        f2 = f1[0:n - 3] + f1[2:n - 1]
        f3 = f2[0:n - 7] + f2[4:n - 3]
        f4 = f3[0:n - 15] + f3[8:n - 7]
        dv_ref[...] = _b(_pool_select(grp, f1[0:tb], f2[0:tb], f3[0:tb], f4[0:tb]) - dp)

    return pl.pallas_call(
        body, name="pool_bwd", grid=(nt,),
        out_shape=[jax.ShapeDtypeStruct((t, GROUP_W), bf16), jax.ShapeDtypeStruct((GROUP_W, GROUP_W), f32),
                   jax.ShapeDtypeStruct((1, GROUP_W), f32)],
        in_specs=[pl.BlockSpec((tb, GROUP_W), lambda i: (i, C_POOL)),
                  pl.BlockSpec((16, GROUP_W), lambda i: (jnp.maximum(i * (tb // 16) - 1, 0), C_POOL)),
                  pl.BlockSpec((tb, GROUP_W), lambda i: (i, 0)),
                  pl.BlockSpec((16, GROUP_W), lambda i: (jnp.minimum((i + 1) * (tb // 16), last16), 0)),
                  _full((GROUP_W, GROUP_W)), _full((1, GROUP_W))],
        out_specs=[pl.BlockSpec((tb, GROUP_W), lambda i: (i, 0)), _full((GROUP_W, GROUP_W)), _full((1, GROUP_W))],
        compiler_params=_params(("arbitrary",)),
    )(proj, proj, dy, dy, pw_bd, scale)


def _sconv_fwd(proj, w, tb):
    t = proj.shape[0]

    def body(gb_ref, gc_ref, hh_ref, gch_ref, hhh_ref, w_ref, o_ref):
        i = pl.program_id(0)
        q = gc_ref[...] * hh_ref[...]
        qh = jnp.where(i > 0, gch_ref[...] * hhh_ref[...], 0.0)
        ext = jnp.concatenate([qh, q], axis=0)
        w = w_ref[...]
        conv = w[0:1] * ext[6:6 + tb] + w[1:2] * ext[7:7 + tb] + w[2:3] * ext[8:8 + tb]
        o_ref[...] = _b(gb_ref[...] * conv)

    def col(c):
        return pl.BlockSpec((tb, GROUP_W), lambda i: (i, c))

    def prev(c):
        return pl.BlockSpec((8, GROUP_W), lambda i: (jnp.maximum(i * (tb // 8) - 1, 0), c))

    return pl.pallas_call(
        body, name="sconv_fwd", grid=(t // tb,),
        out_shape=jax.ShapeDtypeStruct((t, GROUP_W), bf16),
        in_specs=[col(C_GB), col(C_GC), col(C_HH), prev(C_GC), prev(C_HH), _full((8, GROUP_W))],
        out_specs=pl.BlockSpec((tb, GROUP_W), lambda i: (i, 0)),
        compiler_params=_params(("parallel",)),
    )(proj, proj, proj, proj, proj, w)


def _sconv_bwd(proj, dy, w, tb):
    t = proj.shape[0]
    nt = t // tb
    last8 = t // 8 - 1

    def body(gb_ref, gc_ref, hh_ref, gch_ref, hhh_ref, gbn_ref, dy_ref, dyn_ref, w_ref, dgb_ref, dgc_ref, dhh_ref, dw_ref):
        i = pl.program_id(0)
        gc, hh, gb, dy = gc_ref[...], hh_ref[...], gb_ref[...], dy_ref[...]
        q = gc * hh
        qh = jnp.where(i > 0, gch_ref[...] * hhh_ref[...], 0.0)
        ext = jnp.concatenate([qh, q], axis=0)
        w = w_ref[...]
        conv = w[0:1] * ext[6:6 + tb] + w[1:2] * ext[7:7 + tb] + w[2:3] * ext[8:8 + tb]
        dgb_ref[...] = _b(dy * conv)
        e = dy * gb
        en = jnp.where(i < nt - 1, dyn_ref[...] * gbn_ref[...], 0.0)
        exte = jnp.concatenate([e, en], axis=0)
        dq = w[2:3] * exte[0:tb] + w[1:2] * exte[1:1 + tb] + w[0:1] * exte[2:2 + tb]
        dgc_ref[...] = _b(dq * hh)
        dhh_ref[...] = _b(dq * gc)
        dw = jnp.concatenate([_colsum(e * ext[6:6 + tb]), _colsum(e * ext[7:7 + tb]), _colsum(e * ext[8:8 + tb]),
                              jnp.zeros((5, GROUP_W), f32)], axis=0)
        _acc(dw_ref, dw)

    def col(c):
        return pl.BlockSpec((tb, GROUP_W), lambda i: (i, c))

    def prev(c):
        return pl.BlockSpec((8, GROUP_W), lambda i: (jnp.maximum(i * (tb // 8) - 1, 0), c))

    def nxt(c):
        return pl.BlockSpec((8, GROUP_W), lambda i: (jnp.minimum((i + 1) * (tb // 8), last8), c))

    out = pl.BlockSpec((tb, GROUP_W), lambda i: (i, 0))
    return pl.pallas_call(
        body, name="sconv_bwd", grid=(nt,),
        out_shape=[jax.ShapeDtypeStruct((t, GROUP_W), bf16)] * 3 + [jax.ShapeDtypeStruct((8, GROUP_W), f32)],
        in_specs=[col(C_GB), col(C_GC), col(C_HH), prev(C_GC), prev(C_HH), nxt(C_GB), col(0), nxt(0), _full((8, GROUP_W))],
        out_specs=[out, out, out, _full((8, GROUP_W))],
        compiler_params=_params(("arbitrary",)),
    )(proj, proj, proj, proj, proj, proj, dy, dy, w)


def _conv4(xr, halo, w, bias):
    tb = xr.shape[0]
    ext = jnp.concatenate([halo, xr], axis=0)
    pre = w[0:1] * ext[5:5 + tb] + w[1:2] * ext[6:6 + tb] + w[2:3] * ext[7:7 + tb] + w[3:4] * ext[8:8 + tb] + bias
    return pre, ext


def _tri():
    r = lax.broadcasted_iota(jnp.int32, (SSD_CHUNK, SSD_CHUNK), 0)
    c = lax.broadcasted_iota(jnp.int32, (SSD_CHUNK, SSD_CHUNK), 1)
    return r >= c


def _lane_pick(vals):
    rows = vals[0].shape[0]
    lane = lax.broadcasted_iota(jnp.int32, (rows, LANES), 1)
    out = jnp.zeros((rows, LANES), f32)
    for h, v in enumerate(vals):
        out = jnp.where(lane == h, v, out)
    return out


def _ssd_fwd(proj, conv_w, conv_b, dt_bias, a_log, d_cols, tb, xchg=None):
    t = proj.shape[0]
    cpt = tb // SSD_CHUNK

    def body(z_ref, xs_ref, bm_ref, cm_ref, xsh_ref, bmh_ref, cmh_ref, dt_ref, cw_ref, cb_ref, dtb_ref, al_ref, dk_ref,
             o_ref, y_ref, st_ref, state):
        i = pl.program_id(0)

        @pl.when(i == 0)
        def _():
            state[...] = jnp.zeros_like(state)

        cw, cb = cw_ref[...], cb_ref[...]
        acts = []
        for j, (r, hr) in enumerate(((xs_ref, xsh_ref), (bm_ref, bmh_ref), (cm_ref, cmh_ref))):
            halo = jnp.where(i > 0, hr[...], 0.0)
            pre, _ = _conv4(r[...], halo, cw[:, j * 256:(j + 1) * 256], cb[:, j * 256:(j + 1) * 256])
            acts.append(_silu(pre))
        xs, bm, cm = acts
        dt = _softplus(dt_ref[...] + dtb_ref[...])
        a = -jnp.exp(al_ref[...])
        adt = dt * a
        tri = _tri()
        trif = tri.astype(f32)
        dk = dk_ref[...]
        for c in range(cpt):
            rows = slice(c * SSD_CHUNK, (c + 1) * SSD_CHUNK)
            acol = _dot_exact(trif, adt[rows])
            arow = acol.T
            dt_c = dt[rows]
            ys = []
            rowi = lax.broadcasted_iota(jnp.int32, (SSD_CHUNK, 1), 0)
            first = lax.broadcasted_iota(jnp.int32, (SSD_CHUNK, SSD_CHUNK), 1) < SSD_P
            for g in range(SSD_HEADS // 2):
                cols = slice(g * 128, (g + 1) * 128)
                cg, bg = _b(cm[rows, cols]), _b(bm[rows, cols])
                xg = xs[rows, cols]
                heads = (2 * g, 2 * g + 1)
                ac = [acol[:, h:h + 1] for h in heads]
                alast = [v[SSD_CHUNK - 1:SSD_CHUNK] for v in ac]
                dtw = jnp.where(first, dt_c[:, heads[0]:heads[0] + 1], dt_c[:, heads[1]:heads[1] + 1])
                eaw = jnp.where(first, jnp.exp(ac[0]), jnp.exp(ac[1]))
                wdw = jnp.where(first, jnp.exp(alast[0] - ac[0]), jnp.exp(alast[1] - ac[1]))
                xdt = xg * dtw
                xb = _b(xdt)
                gmat = _dot_nt(cg, bg)
                ydiag = []
                for k, h in enumerate(heads):
                    lm = jnp.exp(jnp.where(tri, ac[k] - arow[h:h + 1, :], -jnp.inf))
                    ydiag.append(_dot(_b(gmat * lm), xb[:, k * SSD_P:(k + 1) * SSD_P]))
                s_in = state[g]
                st_ref[c, g] = s_in
                ys.append(jnp.concatenate(ydiag, axis=1) + eaw * _dot_nt(cg, _b(s_in)) + xg * dk[:, cols])
                state[g] = jnp.where(rowi < SSD_P, jnp.exp(alast[0]), jnp.exp(alast[1])) * s_in + _dot_tn(_b(xdt * wdw), bg)
            yc = jnp.concatenate(ys, axis=1)
            y_ref[rows, :] = yc
            o_ref[rows, :] = _b(yc * _silu(z_ref[rows, :]))

    def col(c):
        return pl.BlockSpec((tb, GROUP_W), lambda i: (i, c))

    def prev(c):
        return pl.BlockSpec((8, GROUP_W), lambda i: (jnp.maximum(i * (tb // 8) - 1, 0), c))

    out = pl.BlockSpec((tb, GROUP_W), lambda i: (i, 0))
    return _call(
        body, name="ssd_fwd", grid=(t // tb,),
        out_shape=[jax.ShapeDtypeStruct((t, GROUP_W), bf16), jax.ShapeDtypeStruct((t, GROUP_W), f32),
                   jax.ShapeDtypeStruct((t // SSD_CHUNK, 2, 128, 128), f32)],
        in_specs=[col(C_Z), col(C_XS), col(C_BM), col(C_CM), prev(C_XS), prev(C_BM), prev(C_CM),
                  pl.BlockSpec((tb, LANES), lambda i: (i, C_DT128)),
                  _full((8, 768)), _full((1, 768)), _full((1, LANES)), _full((1, LANES)), _full((1, GROUP_W))],
        out_specs=[out, out, pl.BlockSpec((cpt, 2, 128, 128), lambda i: (i, 0, 0, 0))],
        scratch_shapes=[pltpu.VMEM((2, 128, 128), f32)],
        semantics=("arbitrary",), xchg=xchg,
        args=(proj, proj, proj, proj, proj, proj, proj, proj, conv_w, conv_b, dt_bias, a_log, d_cols))


def _ssd_bwd(proj, dyc, y_pre, states, conv_w, conv_b, dt_bias, a_log, d_cols, tb, xchg=None):
    t = proj.shape[0]
    nt = t // tb
    cpt = tb // SSD_CHUNK

    def body(z_ref, xs_ref, bm_ref, cm_ref, xsh_ref, bmh_ref, cmh_ref, dt_ref, dy_ref, yp_ref, st_ref,
             cw_ref, cb_ref, dtb_ref, al_ref, dk_ref,
             dz_ref, dxs_ref, dbm_ref, dcm_ref, ddt_ref, dcw_ref, dcb_ref, ddtb_ref, dal_ref, ddk_ref,
             dstate, carry):
        i = pl.program_id(0)
        ti = nt - 1 - i

        @pl.when(i == 0)
        def _():
            dstate[...] = jnp.zeros_like(dstate)
            carry[...] = jnp.zeros_like(carry)

        cw, cb = cw_ref[...], cb_ref[...]
        pres, exts, acts = [], [], []
        for j, (r, hr) in enumerate(((xs_ref, xsh_ref), (bm_ref, bmh_ref), (cm_ref, cmh_ref))):
            halo = jnp.where(ti > 0, hr[...], 0.0)
            pre, ext = _conv4(r[...], halo, cw[:, j * 256:(j + 1) * 256], cb[:, j * 256:(j + 1) * 256])
            pres.append(pre)
            exts.append(ext)
            acts.append(_silu(pre))
        xs, bm, cm = acts
        raw = dt_ref[...] + dtb_ref[...]
        dt = _softplus(raw)
        a = -jnp.exp(al_ref[...])
        adt = dt * a
        tri = _tri()
        trif = tri.astype(f32)
        dk = dk_ref[...]
        z = z_ref[...]
        dyc = dy_ref[...]
        dz_ref[...] = _b(dyc * yp_ref[...] * _dsilu(z))
        dy_all = dyc * _silu(z)
        lane = lax.broadcasted_iota(jnp.int32, (1, LANES), 1)
        ddk_acc = jnp.zeros((1, LANES), f32)
        dal_acc = jnp.zeros((1, LANES), f32)
        dxs_c, dbm_c, dcm_c, ddt_c = [None] * cpt, [None] * cpt, [None] * cpt, [None] * cpt
        for c in reversed(range(cpt)):
            rows = slice(c * SSD_CHUNK, (c + 1) * SSD_CHUNK)
            acol = _dot_exact(trif, adt[rows])
            arow = acol.T
            dt_c = dt[rows]
            da_cols, da_rows, ddt_heads, dxs_groups, dbg, dcg = [], [], [], [], [], []
            rowi = lax.broadcasted_iota(jnp.int32, (SSD_CHUNK, 1), 0)
            first = lax.broadcasted_iota(jnp.int32, (SSD_CHUNK, SSD_CHUNK), 1) < SSD_P
            for g in range(SSD_HEADS // 2):
                cols = slice(g * 128, (g + 1) * 128)
                cgf, bgf = cm[rows, cols], bm[rows, cols]
                cg, bg = _b(cgf), _b(bgf)
                xg, dyg = xs[rows, cols], dy_all[rows, cols]
                s_in, dsn = st_ref[c, g], dstate[g]
                sb, dsnb = _b(s_in), _b(dsn)
                heads = (2 * g, 2 * g + 1)
                ac = [acol[:, h:h + 1] for h in heads]
                alast = [v[SSD_CHUNK - 1:SSD_CHUNK] for v in ac]
                el = [jnp.exp(v) for v in alast]
                dtw = jnp.where(first, dt_c[:, heads[0]:heads[0] + 1], dt_c[:, heads[1]:heads[1] + 1])
                eaw = jnp.where(first, jnp.exp(ac[0]), jnp.exp(ac[1]))
                wdw = jnp.where(first, jnp.exp(alast[0] - ac[0]), jnp.exp(alast[1] - ac[1]))
                xdt = xg * dtw
                xb, dyb = _b(xdt), _b(dyg)
                gmat = _dot_nt(cg, bg)
                dgs, dxh, da = None, [], []
                for k, h in enumerate(heads):
                    hc = slice(k * SSD_P, (k + 1) * SSD_P)
                    lm = jnp.exp(jnp.where(tri, ac[k] - arow[h:h + 1, :], -jnp.inf))
                    m = gmat * lm
                    dm = _dot_nt(dyb[:, hc], xb[:, hc])
                    dxh.append(_dot_tn(_b(m), dyb[:, hc]))
                    dgs = dm * lm if dgs is None else dgs + dm * lm
                    wm = dm * m
                    da.append(jnp.sum(wm, axis=1, keepdims=True))
                    da_rows.append(jnp.sum(wm, axis=0, keepdims=True))
                dgb = _b(dgs)
                dcg_g = _dot(dgb, bg)
                dbg_g = _dot_tn(dgb, cg)
                yoff = eaw * _dot_nt(cg, sb)
                dyoff = dyg * yoff
                dye = _b(dyg * eaw)
                dcg_g = dcg_g + _dot(dye, sb)
                ds_y = _dot_tn(dye, cg)
                u = _dot_nt(bg, dsnb)
                dx = jnp.concatenate(dxh, axis=1) + wdw * u
                dbg_g = dbg_g + _dot(_b(xdt * wdw), dsnb)
                xu = xdt * u * wdw
                ss = jnp.sum(dsn * s_in, axis=1, keepdims=True)
                dxx = dx * xg
                dyx = _colsum(dyg * xg)
                for k, h in enumerate(heads):
                    mine = first if k == 0 else jnp.logical_not(first)
                    dwv = jnp.sum(jnp.where(mine, xu, 0.0), axis=1, keepdims=True)
                    mine_rows = (rowi < SSD_P) if k == 0 else (rowi >= SSD_P)
                    dalast = jnp.sum(dwv, axis=0, keepdims=True) + el[k] * jnp.sum(jnp.where(mine_rows, ss, 0.0), axis=0, keepdims=True)
                    dah = da[k] + jnp.sum(jnp.where(mine, dyoff, 0.0), axis=1, keepdims=True) - dwv
                    da_cols.append(dah + jnp.where(rowi == SSD_CHUNK - 1, dalast, 0.0))
                    ddt_heads.append(jnp.sum(jnp.where(mine, dxx, 0.0), axis=1, keepdims=True))
                    ddk_acc = ddk_acc + jnp.where(lane == h, jnp.sum(jnp.where(mine[0:1], dyx, 0.0), axis=1, keepdims=True), 0.0)
                dstate[g] = jnp.where(rowi < SSD_P, el[0], el[1]) * dsn + ds_y
                dxs_groups.append(dx * dtw + dyg * dk[:, cols])
                dbg.append(dbg_g)
                dcg.append(dcg_g)
            da_blk = _lane_pick(da_cols)
            rowsel = lax.broadcasted_iota(jnp.int32, (SSD_CHUNK, SSD_CHUNK), 0)
            da_rows_blk = jnp.zeros((SSD_CHUNK, SSD_CHUNK), f32)
            for h in range(SSD_HEADS):
                da_rows_blk = jnp.where(rowsel == h, da_rows[h], da_rows_blk)
            da_blk = da_blk - da_rows_blk.T
            dadt = lax.dot_general(trif, da_blk, (((0,), (0,)), ((), ())), preferred_element_type=f32,
                                   precision=lax.Precision.HIGHEST)
            dal_acc = dal_acc + _colsum(dadt * dt_c)
            ddt_c[c] = dadt * a + _lane_pick(ddt_heads)
            dxs_c[c] = jnp.concatenate(dxs_groups, axis=1)
            dbm_c[c] = jnp.concatenate(dbg, axis=1)
            dcm_c[c] = jnp.concatenate(dcg, axis=1)
        ddt = jnp.concatenate(ddt_c, axis=0) if cpt > 1 else ddt_c[0]
        ddraw = jnp.where(lane < SSD_HEADS, ddt * jax.nn.sigmoid(raw), 0.0)
        ddt_ref[...] = _b(ddraw)
        _acc(ddtb_ref, _colsum(ddraw))
        _acc(dal_ref, jnp.where(lane < SSD_HEADS, dal_acc * a, 0.0))
        _acc(ddk_ref, ddk_acc)
        dcw_parts, dcb_parts = [], []
        for j, (dparts, out_ref) in enumerate(((dxs_c, dxs_ref), (dbm_c, dbm_ref), (dcm_c, dcm_ref))):
            dact = jnp.concatenate(dparts, axis=0) if cpt > 1 else dparts[0]
            dpre = dact * _dsilu(pres[j])
            w = cw[:, j * 256:(j + 1) * 256]
            ext = jnp.concatenate([dpre, carry[:, j * 256:(j + 1) * 256]], axis=0)
            out_ref[...] = _b(w[3:4] * ext[0:tb] + w[2:3] * ext[1:1 + tb] + w[1:2] * ext[2:2 + tb] + w[0:1] * ext[3:3 + tb])
            carry[:, j * 256:(j + 1) * 256] = dpre[0:8]
            xe = exts[j]
            dcw_parts.append(jnp.concatenate([_colsum(dpre * xe[5 + k:5 + k + tb]) for k in range(4)]
                                             + [jnp.zeros((4, GROUP_W), f32)], axis=0))
            dcb_parts.append(_colsum(dpre))
        _acc(dcw_ref, jnp.concatenate(dcw_parts, axis=1))
        _acc(dcb_ref, jnp.concatenate(dcb_parts, axis=1))

    def col(c):
        return pl.BlockSpec((tb, GROUP_W), lambda i: (nt - 1 - i, c))

    def prev(c):
        return pl.BlockSpec((8, GROUP_W), lambda i: (jnp.maximum((nt - 1 - i) * (tb // 8) - 1, 0), c))

    out = pl.BlockSpec((tb, GROUP_W), lambda i: (nt - 1 - i, 0))
    vec = _full((1, LANES))
    return _call(
        body, name="ssd_bwd", grid=(nt,),
        out_shape=[jax.ShapeDtypeStruct((t, GROUP_W), bf16)] * 4 + [jax.ShapeDtypeStruct((t, LANES), bf16),
                   jax.ShapeDtypeStruct((8, 768), f32), jax.ShapeDtypeStruct((1, 768), f32)]
        + [jax.ShapeDtypeStruct((1, LANES), f32)] * 3,
        in_specs=[col(C_Z), col(C_XS), col(C_BM), col(C_CM), prev(C_XS), prev(C_BM), prev(C_CM),
                  pl.BlockSpec((tb, LANES), lambda i: (nt - 1 - i, C_DT128)), out, out,
                  pl.BlockSpec((cpt, 2, 128, 128), lambda i: (nt - 1 - i, 0, 0, 0)),
                  _full((8, 768)), _full((1, 768)), vec, vec, _full((1, GROUP_W))],
        out_specs=[out, out, out, out, pl.BlockSpec((tb, LANES), lambda i: (nt - 1 - i, 0)),
                   _full((8, 768)), _full((1, 768)), vec, vec, vec],
        scratch_shapes=[pltpu.VMEM((2, 128, 128), f32), pltpu.VMEM((8, 768), f32)],
        semantics=("arbitrary",), xchg=xchg,
        args=(proj, proj, proj, proj, proj, proj, proj, proj, dyc, y_pre, states, conv_w, conv_b, dt_bias, a_log, d_cols))


def _s5_coeffs(are, aim, ls):
    step = jnp.exp(ls)
    mag = jnp.exp(are * step)
    th = aim * step
    lre, lim = mag * jnp.cos(th), mag * jnp.sin(th)
    den = are * are + aim * aim
    nr = lre - 1.0
    fre = (nr * are + lim * aim) / den
    fim = (lim * are - nr * aim) / den
    return step, lre, lim, den, fre, fim


def _s5_prep(are, aim, ls, bre_bd, bim_bd):
    def body(are_ref, aim_ref, ls_ref, bre_ref, bim_ref, lre_ref, lim_ref, bbr_ref, bbi_ref):
        _, lre, lim, _, fre, fim = _s5_coeffs(are_ref[...], aim_ref[...], ls_ref[...])
        lre_ref[...] = lre
        lim_ref[...] = lim
        bre, bim = bre_ref[...], bim_ref[...]
        bbr_ref[...] = fre * bre - fim * bim
        bbi_ref[...] = fre * bim + fim * bre

    col = jax.ShapeDtypeStruct((S5_N, 1), f32)
    mat = jax.ShapeDtypeStruct((S5_N, GROUP_W), f32)
    return pl.pallas_call(body, name="s5_prep", out_shape=[col, col, mat, mat], in_specs=[VMEM] * 5, out_specs=[VMEM] * 4,
                          compiler_params=_params())(are, aim, ls, bre_bd, bim_bd)


def _s5_prep_bwd(are, aim, ls, bre_bd, bim_bd, dlre, dlim, dbbr, dbbi):
    def body(are_ref, aim_ref, ls_ref, bre_ref, bim_ref, dlre_ref, dlim_ref, dbbr_ref, dbbi_ref,
             dare_ref, daim_ref, dls_ref, dbre_ref, dbim_ref):
        are, aim = are_ref[...], aim_ref[...]
        step, lre, lim, den, fre, fim = _s5_coeffs(are, aim, ls_ref[...])
        r = lax.broadcasted_iota(jnp.int32, (S5_N, GROUP_W), 0) // 64
        c = lax.broadcasted_iota(jnp.int32, (S5_N, GROUP_W), 1) // 16
        mask = r == c
        gr = jnp.where(mask, dbbr_ref[...], 0.0)
        gi = jnp.where(mask, dbbi_ref[...], 0.0)
        bre, bim = bre_ref[...], bim_ref[...]
        dbre_ref[...] = fre * gr + fim * gi
        dbim_ref[...] = fre * gi - fim * gr
        dfre = jnp.sum(bre * gr + bim * gi, axis=1, keepdims=True)
        dfim = jnp.sum(bre * gi - bim * gr, axis=1, keepdims=True)
        ire, iim = are / den, aim / den
        tre = dlre_ref[...] + ire * dfre - iim * dfim
        tim = dlim_ref[...] + ire * dfim + iim * dfre
        dzre = lre * tre + lim * tim
        dzim = lre * tim - lim * tre
        qre = (fre * are + fim * aim) / den
        qim = (fim * are - fre * aim) / den
        dare_ref[...] = step * dzre - (qre * dfre + qim * dfim)
        daim_ref[...] = step * dzim - (qre * dfim - qim * dfre)
        dls = (are * dzre + aim * dzim) * step
        sel = (lax.broadcasted_iota(jnp.int32, (S5_N, LANES), 0) // 64 == lax.broadcasted_iota(jnp.int32, (S5_N, LANES), 1)).astype(f32)
        dls_ref[...] = lax.dot_general(sel, jnp.broadcast_to(dls, (S5_N, LANES)), (((0,), (0,)), ((), ())),
                                       preferred_element_type=f32, precision=lax.Precision.HIGHEST)

    col = jax.ShapeDtypeStruct((S5_N, 1), f32)
    mat = jax.ShapeDtypeStruct((S5_N, GROUP_W), f32)
    return pl.pallas_call(body, name="s5_prep_bwd", out_shape=[col, col, jax.ShapeDtypeStruct((LANES, LANES), f32), mat, mat],
                          in_specs=[VMEM] * 9, out_specs=[VMEM] * 5, compiler_params=_params(),
                          )(are, aim, ls, bre_bd, bim_bd, dlre, dlim, dbbr, dbbi)


def _cmul(ar, ai, br, bi):
    return ar * br - ai * bi, ar * bi + ai * br


def _s5_scan(re_ref, im_ref, carry_ref, mr, mi, n_groups, reverse):
    p1 = (mr, mi)
    p2 = _cmul(*p1, *p1)
    p3 = _cmul(*p2, *p1)
    p4 = _cmul(*p2, *p2)
    p5 = _cmul(*p4, *p1)
    p6 = _cmul(*p4, *p2)
    p7 = _cmul(*p4, *p3)
    p8 = _cmul(*p4, *p4)
    pows = [p1, p2, p3, p4, p5, p6, p7, p8]
    row = lax.broadcasted_iota(jnp.int32, (8, S5_N), 0)
    tr = jnp.zeros((8, S5_N), f32)
    ti = jnp.zeros((8, S5_N), f32)
    for i in range(8):
        p = pows[7 - i] if reverse else pows[i]
        tr = jnp.where(row == i, p[0], tr)
        ti = jnp.where(row == i, p[1], ti)
    steps = []
    for k, p in ((1, p1), (2, p2), (4, p4)):
        keep = (row + k < 8) if reverse else (row >= k)
        steps.append((8 - k if reverse else k, jnp.where(keep, p[0], 0.0), jnp.where(keep, p[1], 0.0)))
    edge = 0 if reverse else 7

    def step(j, carry):
        cr, ci = carry
        g = (n_groups - 1 - j) if reverse else j
        r0 = pl.multiple_of(g * 8, 8)
        xr = re_ref[pl.ds(r0, 8), :]
        xi = im_ref[pl.ds(r0, 8), :]
        for shift, br, bi in steps:
            sr = pltpu.roll(xr, shift, 0)
            si = pltpu.roll(xi, shift, 0)
            xr, xi = xr + br * sr - bi * si, xi + br * si + bi * sr
        xr, xi = xr + tr * cr - ti * ci, xi + tr * ci + ti * cr
        re_ref[pl.ds(r0, 8), :] = xr
        im_ref[pl.ds(r0, 8), :] = xi
        return (jnp.broadcast_to(xr[edge:edge + 1, :], (8, S5_N)), jnp.broadcast_to(xi[edge:edge + 1, :], (8, S5_N)))

    cr, ci = lax.fori_loop(0, n_groups, step, (carry_ref[0], carry_ref[1]))
    carry_ref[0] = cr
    carry_ref[1] = ci


def _s5_output(u, xr, xi, ctr, cti, d):
    return _dot_nt(_b(xr), _b(ctr)) - _dot_nt(_b(xi), _b(cti)) + d * u


def _s5_fwd(proj, bbr, bbi, ctr, cti, lre, lim, d, glu_w, glu_b, tb, xchg=None):
    t = proj.shape[0]

    def body(u_ref, bbr_ref, bbi_ref, ctr_ref, cti_ref, lr_ref, li_ref, d_ref, gw_ref, gb_ref, o_ref, xr_ref, xi_ref, carry):
        @pl.when(pl.program_id(0) == 0)
        def _():
            carry[...] = jnp.zeros_like(carry)

        u = u_ref[...]
        ub = _b(u)
        xr_ref[...] = _dot_nt(ub, _b(bbr_ref[...]))
        xi_ref[...] = _dot_nt(ub, _b(bbi_ref[...]))
        _s5_scan(xr_ref, xi_ref, carry, lr_ref[...], li_ref[...], tb // 8, reverse=False)
        y = _s5_output(u, xr_ref[...], xi_ref[...], ctr_ref[...], cti_ref[...], d_ref[...])
        gl = _gelu(y)
        o_ref[...] = _b(gl * jax.nn.sigmoid(_dot(_b(gl), _b(gw_ref[...])) + gb_ref[...]))

    state = pl.BlockSpec((tb, S5_N), lambda i: (i, 0))
    return _call(
        body, name="s5_fwd", grid=(t // tb,),
        out_shape=[jax.ShapeDtypeStruct((t, GROUP_W), bf16), jax.ShapeDtypeStruct((t, S5_N), f32), jax.ShapeDtypeStruct((t, S5_N), f32)],
        in_specs=[pl.BlockSpec((tb, GROUP_W), lambda i: (i, C_S5)), _full((S5_N, GROUP_W)), _full((S5_N, GROUP_W)),
                  _full((GROUP_W, S5_N)), _full((GROUP_W, S5_N)), _full((1, S5_N)), _full((1, S5_N)),
                  _full((1, GROUP_W)), _full((GROUP_W, GROUP_W)), _full((1, GROUP_W))],
        out_specs=[pl.BlockSpec((tb, GROUP_W), lambda i: (i, 0)), state, state],
        scratch_shapes=[pltpu.VMEM((2, 8, S5_N), f32)],
        semantics=("arbitrary",), xchg=xchg, args=(proj, bbr, bbi, ctr, cti, lre, lim, d, glu_w, glu_b))


def _s5_bwd(proj, dyd, xr_all, xi_all, bbr, bbi, ctr, cti, lre, lim, d, glu_w, glu_b, tb, xchg=None):
    t = proj.shape[0]
    nt = t // tb

    def body(u_ref, dy_ref, xr_ref, xi_ref, xrh_ref, xih_ref, bbr_ref, bbi_ref, ctr_ref, cti_ref, lr_ref, li_ref,
             d_ref, gw_ref, gb_ref,
             du_ref, dlr_ref, dli_ref, dbbr_ref, dbbi_ref, dctr_ref, dcti_ref, dd_ref, dgw_ref, dgb_ref,
             gr_ref, gi_ref, carry):
        i = pl.program_id(0)
        ti = nt - 1 - i

        @pl.when(i == 0)
        def _():
            carry[...] = jnp.zeros_like(carry)

        u = u_ref[...]
        ub = _b(u)
        xr, xi = xr_ref[...], xi_ref[...]
        ctr, cti = _b(ctr_ref[...]), _b(cti_ref[...])
        d = d_ref[...]
        gw = _b(gw_ref[...])
        y = _s5_output(u, xr, xi, ctr, cti, d)
        gl = _gelu(y)
        sg = jax.nn.sigmoid(_dot(_b(gl), gw) + gb_ref[...])
        dout = dy_ref[...]
        q = dout * gl * sg * (1.0 - sg)
        qb = _b(q)
        dgl = dout * sg + _dot_nt(qb, gw)
        _acc(dgw_ref, _dot_tn(_b(gl), qb))
        _acc(dgb_ref, _colsum(q))
        dyv = dgl * _dgelu(y)
        _acc(dd_ref, _colsum(dyv * u))
        dyb = _b(dyv)
        gr_ref[...] = _dot(dyb, ctr)
        gi_ref[...] = -_dot(dyb, cti)
        _acc(dctr_ref, _dot_tn(dyb, _b(xr)))
        _acc(dcti_ref, -_dot_tn(dyb, _b(xi)))
        _s5_scan(gr_ref, gi_ref, carry, lr_ref[...], -li_ref[...], tb // 8, reverse=True)
        gr, gi = gr_ref[...], gi_ref[...]
        xpr = jnp.concatenate([jnp.where(ti > 0, xrh_ref[...], 0.0), xr], axis=0)[7:7 + tb]
        xpi = jnp.concatenate([jnp.where(ti > 0, xih_ref[...], 0.0), xi], axis=0)[7:7 + tb]
        _acc(dlr_ref, _colsum(gr * xpr + gi * xpi))
        _acc(dli_ref, _colsum(gi * xpr - gr * xpi))
        grb, gib = _b(gr), _b(gi)
        _acc(dbbr_ref, _dot_tn(grb, ub))
        _acc(dbbi_ref, _dot_tn(gib, ub))
        du_ref[...] = _b(dyv * d + _dot(grb, _b(bbr_ref[...])) + _dot(gib, _b(bbi_ref[...])))

    state = pl.BlockSpec((tb, S5_N), lambda i: (nt - 1 - i, 0))
    prev = pl.BlockSpec((8, S5_N), lambda i: (jnp.maximum((nt - 1 - i) * (tb // 8) - 1, 0), 0))
    tile = pl.BlockSpec((tb, GROUP_W), lambda i: (nt - 1 - i, 0))
    return _call(
        body, name="s5_bwd", grid=(nt,),
        out_shape=[jax.ShapeDtypeStruct((t, GROUP_W), bf16), jax.ShapeDtypeStruct((1, S5_N), f32), jax.ShapeDtypeStruct((1, S5_N), f32),
                   jax.ShapeDtypeStruct((S5_N, GROUP_W), f32), jax.ShapeDtypeStruct((S5_N, GROUP_W), f32),
                   jax.ShapeDtypeStruct((GROUP_W, S5_N), f32), jax.ShapeDtypeStruct((GROUP_W, S5_N), f32),
                   jax.ShapeDtypeStruct((1, GROUP_W), f32), jax.ShapeDtypeStruct((GROUP_W, GROUP_W), f32),
                   jax.ShapeDtypeStruct((1, GROUP_W), f32)],
        in_specs=[pl.BlockSpec((tb, GROUP_W), lambda i: (nt - 1 - i, C_S5)), tile, state, state, prev, prev,
                  _full((S5_N, GROUP_W)), _full((S5_N, GROUP_W)), _full((GROUP_W, S5_N)), _full((GROUP_W, S5_N)),
                  _full((1, S5_N)), _full((1, S5_N)), _full((1, GROUP_W)), _full((GROUP_W, GROUP_W)), _full((1, GROUP_W))],
        out_specs=[tile, _full((1, S5_N)), _full((1, S5_N)), _full((S5_N, GROUP_W)), _full((S5_N, GROUP_W)),
                   _full((GROUP_W, S5_N)), _full((GROUP_W, S5_N)), _full((1, GROUP_W)), _full((GROUP_W, GROUP_W)), _full((1, GROUP_W))],
        scratch_shapes=[pltpu.VMEM((tb, S5_N), f32), pltpu.VMEM((tb, S5_N), f32), pltpu.VMEM((2, 8, S5_N), f32)],
        semantics=("arbitrary",), xchg=xchg,
        args=(proj, dyd, xr_all, xi_all, xr_all, xi_all, bbr, bbi, ctr, cti, lre, lim, d, glu_w, glu_b))


def _outproj_fwd(ys, h, bn_w, g1, w_out, tb):
    t = h.shape[0]

    def body(ya_ref, yb_ref, yc_ref, yd_ref, h_ref, bn_ref, g1_ref, w_ref, h1_ref, o_ref, gr_ref):
        bn = bn_ref[...]
        parts = []
        for g, r in enumerate((ya_ref, yb_ref, yc_ref, yd_ref)):
            n, _ = _rms(r[...].astype(f32))
            parts.append(n * bn[:, g * GROUP_W:(g + 1) * GROUP_W])
        groups = _b(jnp.concatenate(parts, axis=1))
        gr_ref[...] = groups
        o = _dot(groups, w_ref[...])
        o_ref[...] = _b(o)
        h1_ref[...] = h_ref[...] + g1_ref[...] * o

    grp = pl.BlockSpec((tb, GROUP_W), lambda i: (i, 0))
    row = pl.BlockSpec((tb, D_MODEL), lambda i: (i, 0))
    vec = _full((1, D_MODEL))
    return pl.pallas_call(
        body, name="outproj_fwd", grid=(t // tb,),
        out_shape=[jax.ShapeDtypeStruct((t, D_MODEL), f32), jax.ShapeDtypeStruct((t, D_MODEL), bf16),
                   jax.ShapeDtypeStruct((t, D_MODEL), bf16)],
        in_specs=[grp, grp, grp, grp, row, vec, vec, _full((D_MODEL, D_MODEL))],
        out_specs=[row, row, row],
        compiler_params=_params(("parallel",)),
    )(*ys, h, bn_w, g1, w_out)


def _outproj_bwd(dh1, o, ys, bn_w, g1, w_out, tb):
    t = dh1.shape[0]

    def body(dh_ref, o_ref, ya_ref, yb_ref, yc_ref, yd_ref, bn_ref, g1_ref, w_ref,
             da_ref, db_ref, dc_ref, dd_ref, do_ref, dg1_ref, dbn_ref):
        dh = dh_ref[...]
        _acc(dg1_ref, _colsum(dh * o_ref[...].astype(f32)))
        do = _b(dh * g1_ref[...])
        do_ref[...] = do
        dgroups = _dot_nt(do, w_ref[...])
        bn = bn_ref[...]
        dbn = []
        for g, (r, dr) in enumerate(((ya_ref, da_ref), (yb_ref, db_ref), (yc_ref, dc_ref), (yd_ref, dd_ref))):
            n, rr = _rms(r[...].astype(f32))
            dgr = dgroups[:, g * GROUP_W:(g + 1) * GROUP_W]
            dbn.append(_colsum(dgr * n))
            dr[...] = _rms_bwd(dgr * bn[:, g * GROUP_W:(g + 1) * GROUP_W], n, rr)
        _acc(dbn_ref, jnp.concatenate(dbn, axis=1))

    grp = pl.BlockSpec((tb, GROUP_W), lambda i: (i, 0))
    row = pl.BlockSpec((tb, D_MODEL), lambda i: (i, 0))
    vec = _full((1, D_MODEL))
    return pl.pallas_call(
        body, name="outproj_bwd", grid=(t // tb,),
        out_shape=[jax.ShapeDtypeStruct((t, GROUP_W), f32)] * 4 + [jax.ShapeDtypeStruct((t, D_MODEL), bf16),
                   jax.ShapeDtypeStruct((1, D_MODEL), f32), jax.ShapeDtypeStruct((1, D_MODEL), f32)],
        in_specs=[row, row, grp, grp, grp, grp, vec, vec, _full((D_MODEL, D_MODEL))],
        out_specs=[grp, grp, grp, grp, row, vec, vec],
        compiler_params=_params(("arbitrary",)),
    )(dh1, o, *ys, bn_w, g1, w_out)


def _mlp_fwd(h1, norm_w, sc, sh, g2, w1, w2, tb, xchg=None):
    t = h1.shape[0]
    nh = w1.shape[0] // MLP_SLABS

    def body(h_ref, nw_ref, sc_ref, sh_ref, g2_ref, w1_ref, w2_ref, h2_ref, m_ref, v_ref, r_ref, acc):
        j = pl.program_id(1)

        @pl.when(j == 0)
        def _():
            n, _ = _rms(h_ref[...])
            v_ref[...] = _b(n * nw_ref[...] * (1.0 + sc_ref[...]) + sh_ref[...])

        v = v_ref[...]
        p = None
        for s in range(MLP_SLABS):
            ra = jnp.maximum(_dot(v, w1_ref[s]), 0.0)
            r = _b(ra * ra)
            r_ref[:, s * MLP_HB:(s + 1) * MLP_HB] = r
            q = _dot(r, w2_ref[s])
            p = q if p is None else p + q

        @pl.when(j == 0)
        def _():
            acc[...] = p

        @pl.when(j != 0)
        def _():
            acc[...] += p

        @pl.when(j == nh - 1)
        def _():
            m = acc[...]
            m_ref[...] = _b(m)
            h2_ref[...] = h_ref[...] + g2_ref[...] * m

    row = pl.BlockSpec((tb, D_MODEL), lambda i, j: (i, 0))
    hid = pl.BlockSpec((tb, MLP_SLABS * MLP_HB), lambda i, j: (i, j))
    vec = _full((1, D_MODEL))
    return _call(
        body, name="mlp_fwd", grid=(t // tb, nh),
        out_shape=[jax.ShapeDtypeStruct((t, D_MODEL), f32), jax.ShapeDtypeStruct((t, D_MODEL), bf16),
                   jax.ShapeDtypeStruct((t, D_MODEL), bf16), jax.ShapeDtypeStruct((t, N_DEV * MLP_HB), bf16)],
        in_specs=[row, vec, vec, vec, vec, pl.BlockSpec((MLP_SLABS, D_MODEL, MLP_HB), lambda i, j: (j, 0, 0)),
                  pl.BlockSpec((MLP_SLABS, MLP_HB, D_MODEL), lambda i, j: (j, 0, 0))],
        out_specs=[row, row, row, hid],
        scratch_shapes=[pltpu.VMEM((tb, D_MODEL), f32)],
        semantics=("arbitrary", "arbitrary"), xchg=xchg, args=(h1, norm_w, sc, sh, g2, w1, w2))


def _mlp_bwd(dh2, m, h1, r, norm_w, sc, sh, g2, w1, w2, tb, xchg=None):
    t = h1.shape[0]
    slabs = MLP_BWD_SLABS
    nh = w1.shape[0] // slabs

    def body(dh_ref, m_ref, h_ref, r_ref, nw_ref, sc_ref, sh_ref, g2_ref, w1_ref, w2_ref,
             dh1_ref, do_ref, da_ref, dg2_ref, dsh_ref, dsc_ref, dnw_ref, acc):
        j = pl.program_id(1)

        @pl.when(j == 0)
        def _():
            dh = dh_ref[...]
            _acc(dg2_ref, _colsum(dh * m_ref[...].astype(f32)))
            do_ref[...] = _b(dh * g2_ref[...])

        do = do_ref[...]
        p = None
        for s in range(slabs):
            cols = slice(s * MLP_HB, (s + 1) * MLP_HB)
            dr = _dot_nt(do, w2_ref[s])
            da = _b(dr * 2.0 * jnp.sqrt(r_ref[:, cols].astype(f32)))
            da_ref[:, cols] = da
            q = _dot_nt(da, w1_ref[s])
            p = q if p is None else p + q

        @pl.when(j == 0)
        def _():
            acc[...] = p

        @pl.when(j != 0)
        def _():
            acc[...] += p

        @pl.when(j == nh - 1)
        def _():
            dv = acc[...]
            n, r = _rms(h_ref[...])
            nw = nw_ref[...]
            gain = 1.0 + sc_ref[...]
            _acc(dsh_ref, _colsum(dv))
            _acc(dsc_ref, _colsum(dv * n * nw))
            _acc(dnw_ref, _colsum(dv * gain * n))
            dh1_ref[...] = dh_ref[...] + _rms_bwd(dv * nw * gain, n, r)

    row = pl.BlockSpec((tb, D_MODEL), lambda i, j: (i, 0))
    hid = pl.BlockSpec((tb, slabs * MLP_HB), lambda i, j: (i, j))
    vec = _full((1, D_MODEL))
    once = dict(pipeline_mode=pl.Buffered(1)) if nh == 1 else {}
    return _call(
        body, name="mlp_bwd", grid=(t // tb, nh),
        out_shape=[jax.ShapeDtypeStruct((t, D_MODEL), f32), jax.ShapeDtypeStruct((t, D_MODEL), bf16),
                   jax.ShapeDtypeStruct((t, N_DEV * MLP_HB), bf16)] + [jax.ShapeDtypeStruct((1, D_MODEL), f32)] * 4,
        in_specs=[row, row, row, hid, vec, vec, vec, vec,
                  pl.BlockSpec((slabs, D_MODEL, MLP_HB), lambda i, j: (j, 0, 0), **once),
                  pl.BlockSpec((slabs, MLP_HB, D_MODEL), lambda i, j: (j, 0, 0), **once)],
        out_specs=[row, row, hid, vec, vec, vec, vec],
        scratch_shapes=[pltpu.VMEM((tb, D_MODEL), f32)],
        semantics=("arbitrary", "arbitrary"), xchg=xchg, args=(dh2, m, h1, r, norm_w, sc, sh, g2, w1, w2))


def _loss_head(h, target, norm_w, tb):
    t = h.shape[0]

    def body(h_ref, t_ref, w_ref, loss_ref, dh_ref, dw_ref):
        n, r = _rms(h_ref[...])
        w = w_ref[...]
        err = n * w - t_ref[...]
        part = 0.5 * jnp.sum(jnp.sum(err * err, axis=1, keepdims=True), axis=0, keepdims=True) / D_MODEL
        _acc(loss_ref, jnp.broadcast_to(part, (8, LANES)))
        dy = err / D_MODEL
        _acc(dw_ref, _colsum(dy * n))
        dh_ref[...] = _rms_bwd(dy * w, n, r)

    row = pl.BlockSpec((tb, D_MODEL), lambda i: (i, 0))
    return pl.pallas_call(
        body, name="loss_head", grid=(t // tb,),
        out_shape=[jax.ShapeDtypeStruct((8, LANES), f32), jax.ShapeDtypeStruct((t, D_MODEL), f32),
                   jax.ShapeDtypeStruct((1, D_MODEL), f32)],
        in_specs=[row, row, _full((1, D_MODEL))],
        out_specs=[_full((8, LANES)), row, _full((1, D_MODEL))],
        compiler_params=_params(("arbitrary",)),
    )(h, target, norm_w)


def _adam_math(w, g, m, v):
    m2 = ADAM_B1 * m + (1.0 - ADAM_B1) * g
    v2 = ADAM_B2 * v + (1.0 - ADAM_B2) * (g * g)
    mh = m2 / (1.0 - ADAM_B1 ** ADAM_STEP)
    vh = v2 / (1.0 - ADAM_B2 ** ADAM_STEP)
    return -ADAM_LR * (mh / (jnp.sqrt(vh) + ADAM_EPS) + ADAM_WD * w), m2, v2


def _adamw_small(ws, gs, ms, vs):
    n = len(ws)
    shapes = [w.shape for w in ws]
    as2d = [(1,) + s if len(s) == 1 else s for s in shapes]
    flat = [x.reshape(s) for group in (ws, gs, ms, vs) for x, s in zip(group, as2d)]

    def body(*refs):
        w_refs, g_refs, m_refs, v_refs, outs = refs[:n], refs[n:2 * n], refs[2 * n:3 * n], refs[3 * n:4 * n], refs[4 * n:]
        for i in range(n):
            d, m2, v2 = _adam_math(w_refs[i][...], g_refs[i][...], m_refs[i][...], v_refs[i][...])
            outs[3 * i][...] = d
            outs[3 * i + 1][...] = m2
            outs[3 * i + 2][...] = v2

    res = pl.pallas_call(body, name="adamw_small", out_shape=[jax.ShapeDtypeStruct(s, f32) for s in as2d for _ in range(3)],
                         in_specs=[VMEM] * (4 * n), out_specs=[VMEM] * (3 * n), compiler_params=_params())(*flat)
    return [r.reshape(shapes[i // 3]) for i, r in enumerate(res)]


def _sum_adamw_layers(parts0, parts1, w, m, v, name, rb):
    n_src, r, c = parts0.shape
    nb = r // rb

    def body(p0_ref, p1_ref, w_ref, m_ref, v_ref, g_ref, d_ref, m2_ref, v2_ref):
        def update(p_ref):
            g = p_ref[0].astype(f32)
            for s in range(1, n_src):
                g = g + p_ref[s].astype(f32)
            g_ref[0] = g
            d, m2, v2 = _adam_math(w_ref[0], g, m_ref[0], v_ref[0])
            d_ref[0] = d
            m2_ref[0] = m2
            v2_ref[0] = v2

        @pl.when(pl.program_id(0) == 0)
        def _():
            update(p0_ref)

        @pl.when(pl.program_id(0) == 1)
        def _():
            update(p1_ref)

    blk = pl.BlockSpec((1, rb, c), lambda l, i: (l, i, 0))
    return pl.pallas_call(
        body, name=name, grid=(2, nb),
        out_shape=[jax.ShapeDtypeStruct((2, r, c), f32)] * 4,
        in_specs=[pl.BlockSpec((n_src, rb, c), lambda l, i: (0, jnp.where(l == 0, i, nb - 1), 0)),
                  pl.BlockSpec((n_src, rb, c), lambda l, i: (0, jnp.where(l == 1, i, 0), 0)), blk, blk, blk],
        out_specs=[blk] * 4,
        compiler_params=_params(("arbitrary", "arbitrary")),
    )(parts0, parts1, w, m, v)


def _reorder_in(w):
    pad = jnp.zeros(w.shape[:-1] + (P_IN - 2308,), w.dtype)
    return jnp.concatenate([w[..., :2048], w[..., 2052:2308], w[..., 2048:2052], pad], axis=-1)


def _unreorder_in(w):
    return jnp.concatenate([w[..., :2048], w[..., 2304:2308], w[..., 2048:2304]], axis=-1)


def _block_diag(w2d, n_blocks):
    rows, cols = w2d.shape
    tiled = jnp.tile(w2d, (1, n_blocks))
    rb = lax.broadcasted_iota(jnp.int32, tiled.shape, 0) // (rows // n_blocks)
    cb = lax.broadcasted_iota(jnp.int32, tiled.shape, 1) // cols
    return jnp.where(rb == cb, tiled, jnp.zeros_like(tiled))


def _block_diag_extract(w_bd, n_blocks):
    rows, wide = w_bd.shape
    r, c = rows // n_blocks, wide // n_blocks
    w4 = w_bd.reshape(n_blocks, r, n_blocks, c)
    idx = jnp.arange(n_blocks)
    return w4[idx, :, idx, :]


def _rows_of(shape):
    n = 1
    for d in shape:
        n *= d
    return -(-n // (8 * LANES)) * 8, n


def _flat_pack(arrs, row_multiple=8):
    blocks = []
    for a in arrs:
        rows, n = _rows_of(a.shape)
        blocks.append(jnp.pad(a.reshape(-1), (0, rows * LANES - n)).reshape(rows, LANES))
    total = sum(b.shape[0] for b in blocks)
    pad = -total % row_multiple
    if pad:
        blocks.append(jnp.zeros((pad, LANES), blocks[0].dtype))
    return jnp.concatenate(blocks, axis=0)


def _flat_unpack(packed, shapes):
    out, off = [], 0
    for s in shapes:
        rows, n = _rows_of(s)
        out.append(packed[off:off + rows].reshape(-1)[:n].reshape(s))
        off += rows
    return out


_W_NAMES = ['norm_mix_w', 'norm_mlp_w', 'ada_w', 'ada_b', 'w_in', 'pool_w', 'pool_scale', 'sconv_w', 'ssd_conv_w',
            'ssd_conv_b', 'ssd_dt_bias', 'ssd_a_log', 'ssd_d', 's5_a_re', 's5_a_im', 's5_log_step', 's5_b_re', 's5_b_im',
            's5_c_re', 's5_c_im', 's5_d', 's5_glu_w', 's5_glu_b', 'branch_norm_w', 'w_out', 'mlp_w1', 'mlp_w2',
            'final_norm_w']
_BIG = ('ada_w', 'w_in', 'w_out', 'mlp_w1', 'mlp_w2')
_SMALL = [n for n in _W_NAMES if n not in _BIG]
_SHARDED_SMALL = {'sconv_w': (2, 32), 'ssd_conv_w': (2, 96), 's5_glu_w': (1, 32)}


def _gather(*blocks):
    return _ChipGather(blocks)


def _scatter(*parts):
    return _Scatter(parts)


def _layer_forward(l, h, p, w, sh_b, tb):
    first = l == 0
    (proj, u_b), got = _inproj_fwd(h, p['norm_mix_w'][l], p['sc1'][l], p['sh1'][l], w['w_in', l], tb,
                                   xchg=_gather(sh_b[1][0]) if first else None)
    if first:
        w['w_out', 0] = got[0].reshape(D_MODEL, D_MODEL)
    ya = _pool_fwd(proj, p['pool_bd'][l], p['pool_scale'][l], tb)
    yb = _sconv_fwd(proj, p['sconv_w8'][l], tb)
    (yc, yc_pre, states), got = _ssd_fwd(proj, p['ssd_conv_w8'][l], p['ssd_conv_b'][l], p['ssd_dt_bias'][l], p['ssd_a_log'][l],
                                         p['ssd_d_cols'][l], tb, xchg=_gather(sh_b[2][0]) if first else None)
    if first:
        w['w1', 0] = got[0]
    (yd, xr, xi), got = _s5_fwd(proj, p['bbr'][l], p['bbi'][l], p['ctr'][l], p['cti'][l], p['lre'][l], p['lim'][l],
                                p['s5_d'][l], p['glu_w'][l], p['glu_b'][l], tb, xchg=_gather(sh_b[3][0]) if first else None)
    if first:
        w['w2', 0] = got[0]
    ys = (ya, yb, yc, yd)
    h1, o, groups_b = _outproj_fwd(ys, h, p['branch_norm_w'][l], p['g1'][l], w['w_out', l], tb)
    (h2, m, v_b, r_b), got = _mlp_fwd(h1, p['norm_mlp_w'][l], p['sc2'][l], p['sh2'][l], p['g2'][l], w['w1', l], w['w2', l],
                                      min(MLP_TB, h.shape[0]), xchg=_gather(*[sh_b[k][1] for k in range(4)]) if first else None)
    if first:
        w['w_in', 1] = got[0].reshape(D_MODEL, P_IN)
        w['w_out', 1] = got[1].reshape(D_MODEL, D_MODEL)
        w['w1', 1], w['w2', 1] = got[2], got[3]
    saved = dict(h=h, proj=proj, u_b=u_b, ys=ys, yc_pre=yc_pre, states=states, xr=xr, xi=xi, h1=h1, o=o,
                 groups_b=groups_b, m=m, v_b=v_b, r_b=r_b)
    return h2, saved


def _layer_backward(l, dh2, s, p, w, pending, recv, tb):
    def carry(names):
        names = [n for n in names if n in pending]
        return names, (_scatter(*[pending.pop(n) for n in names]) if names else None)

    def landed(names, got):
        for n, g in zip(names, got):
            recv[n] = g

    names, xchg = carry([('w_out', 1)])
    (dh1, do2_b, da_b, dg2, dsh2, dsc2, dnw_mlp), got = _mlp_bwd(dh2, s['m'], s['h1'], s['r_b'], p['norm_mlp_w'][l], p['sc2'][l],
                                                                p['sh2'][l], p['g2'][l], w['w1', l], w['w2', l], min(TB_BWD, tb),
                                                                xchg=xchg)
    landed(names, got)
    pending['mlp_w2', l] = _wgrad(s['r_b'], do2_b, 1, "wgrad_w2", tm=1024, tk=4096).reshape(N_DEV, MLP_HB, D_MODEL)
    pending['mlp_w1', l] = _wgrad(s['v_b'], da_b, N_DEV, "wgrad_w1", tm=1024, tk=4096)
    dya, dyb, dyc, dyd, do1_b, dg1, dbn = _outproj_bwd(dh1, s['o'], s['ys'], p['branch_norm_w'][l], p['g1'][l], w['w_out', l], tb)
    pending['w_out', l] = _wgrad(s['groups_b'], do1_b, 1, "wgrad_wout", tm=1024, tk=1024).reshape(N_DEV, D_MODEL // N_DEV, D_MODEL)
    proj = s['proj']
    dv, dpool_bd, dpool_scale = _pool_bwd(proj, dya, p['pool_bd'][l], p['pool_scale'][l], tb)
    dgb, dgc, dhh, dsconv = _sconv_bwd(proj, dyb, p['sconv_w8'][l], tb)
    names, xchg = carry([('mlp_w1', l)] + ([('w_out', 0)] if l == 0 else []))
    (dz, dxs, dbm, dcm, ddt, dconv_w, dconv_b, ddtb, dalog, ddskip), got = _ssd_bwd(
        proj, dyc, s['yc_pre'], s['states'], p['ssd_conv_w8'][l], p['ssd_conv_b'][l], p['ssd_dt_bias'][l], p['ssd_a_log'][l],
        p['ssd_d_cols'][l], min(TB_BWD, tb), xchg=xchg)
    landed(names, got)
    names, xchg = carry([('mlp_w2', l)])
    (du5, dlr, dli, dbbr, dbbi, dctr, dcti, dd5, dgw, dgb5), got = _s5_bwd(
        proj, dyd, s['xr'], s['xi'], p['bbr'][l], p['bbi'][l], p['ctr'][l], p['cti'][l], p['lre'][l], p['lim'][l],
        p['s5_d'][l], p['glu_w'][l], p['glu_b'][l], min(TB_BWD, tb), xchg=xchg)
    landed(names, got)
    dare, daim, dls, dbre_bd, dbim_bd = _s5_prep_bwd(p['are_c'][l], p['aim_c'][l], p['ls_c'][l], p['bre_bd'][l], p['bim_bd'][l],
                                                     dlr.reshape(S5_N, 1), dli.reshape(S5_N, 1), dbbr, dbbi)
    dparts = (dv, dgb, dgc, dhh, dz, dxs, dbm, dcm, du5, ddt)
    pending['w_in', l] = _wgrad_parts(s['u_b'], dparts, "wgrad_win", tm=1024, tk=1024).reshape(N_DEV, D_MODEL // N_DEV, P_IN)
    names, xchg = carry([('w_in', l)])
    (dh, dsh1, dsc1, dnw_mix), got = _inproj_bwd(dparts, dh1, s['h'], p['norm_mix_w'][l], p['sc1'][l], p['sh1'][l], w['w_in', l],
                                                 tb, xchg=xchg)
    landed(names, got)
    small = {
        'norm_mix_w': dnw_mix.reshape(D_MODEL), 'norm_mlp_w': dnw_mlp.reshape(D_MODEL),
        'ada_b': jnp.concatenate([dsh1, dsc1, dg1, dsh2, dsc2, dg2], axis=1).reshape(6 * D_MODEL),
        'pool_w': _block_diag_extract(dpool_bd, 4), 'pool_scale': dpool_scale.reshape(GROUP_W),
        'sconv_w': dsconv[0:3], 'ssd_conv_w': dconv_w[0:4], 'ssd_conv_b': dconv_b.reshape(768),
        'ssd_dt_bias': ddtb[0, 0:4], 'ssd_a_log': dalog[0, 0:4], 'ssd_d': ddskip[0, 0:4],
        's5_a_re': dare.reshape(16, 64), 's5_a_im': daim.reshape(16, 64), 's5_log_step': dls[0:16, 0],
        's5_b_re': _block_diag_extract(dbre_bd, 16), 's5_b_im': _block_diag_extract(dbim_bd, 16),
        's5_c_re': _block_diag_extract(dctr, 16), 's5_c_im': _block_diag_extract(dcti, 16),
        's5_d': dd5.reshape(GROUP_W), 's5_glu_w': dgw, 's5_glu_b': dgb5.reshape(GROUP_W),
        'branch_norm_w': dbn.reshape(D_MODEL),
    }
    return dh, small


def _prepare_params(a, me, w_in0_shard):
    pack_shapes = [(1, D_MODEL), (2, 3, 32), (2, 4, 96), (2, 32, GROUP_W)]
    packed = _flat_pack([a['c'], a['sconv_w'], a['ssd_conv_w'], a['s5_glu_w']])
    w_in0, gathered = _exchange_alone(_gather(w_in0_shard, packed), "gather_first")
    pieces = [_flat_unpack(gathered[d], pack_shapes) for d in range(N_DEV)]
    c_all = jnp.concatenate([pc[0] for pc in pieces], axis=0)
    sconv_full = jnp.concatenate([pc[1] for pc in pieces], axis=2)
    ssd_conv_full = jnp.concatenate([pc[2] for pc in pieces], axis=2)
    glu_full = jnp.concatenate([pc[3] for pc in pieces], axis=1)

    ada_b_cols = lax.dynamic_slice_in_dim(a['ada_b'], me * 768, 768, axis=1).reshape(2, 1, 768)
    cond, modrows = _ada_forward(c_all, a['ada_w'], ada_b_cols)
    mod_recv = _all_to_all_rows(modrows.transpose(1, 0, 2), "exchange_mod")
    mod = mod_recv.transpose(1, 0, 2).reshape(2, 6 * D_MODEL)
    p = {'cond': cond}
    for k, name in enumerate(('sh1', 'sc1', 'g1', 'sh2', 'sc2', 'g2')):
        p[name] = mod[:, k * D_MODEL:(k + 1) * D_MODEL].reshape(2, 1, D_MODEL)

    for name in ('norm_mix_w', 'norm_mlp_w', 'branch_norm_w'):
        p[name] = a[name].reshape(2, 1, D_MODEL)
    p['pool_bd'] = jnp.stack([_block_diag(a['pool_w'][l].reshape(GROUP_W, 64), 4) for l in range(2)])
    p['pool_scale'] = a['pool_scale'].reshape(2, 1, GROUP_W)
    p['sconv_w8'] = jnp.pad(sconv_full, ((0, 0), (0, 5), (0, 0)))
    p['ssd_conv_w8'] = jnp.pad(ssd_conv_full, ((0, 0), (0, 4), (0, 0)))
    p['ssd_conv_b'] = a['ssd_conv_b'].reshape(2, 1, 768)
    p['ssd_dt_bias'] = jnp.pad(a['ssd_dt_bias'], ((0, 0), (0, LANES - 4))).reshape(2, 1, LANES)
    p['ssd_a_log'] = jnp.pad(a['ssd_a_log'], ((0, 0), (0, LANES - 4))).reshape(2, 1, LANES)
    p['ssd_d_cols'] = jnp.repeat(a['ssd_d'], SSD_P, axis=1).reshape(2, 1, GROUP_W)
    p['are_c'] = a['s5_a_re'].reshape(2, S5_N, 1)
    p['aim_c'] = a['s5_a_im'].reshape(2, S5_N, 1)
    p['ls_c'] = jnp.repeat(a['s5_log_step'], 64, axis=1).reshape(2, S5_N, 1)
    p['bre_bd'] = jnp.stack([_block_diag(a['s5_b_re'][l].reshape(S5_N, 16), 16) for l in range(2)])
    p['bim_bd'] = jnp.stack([_block_diag(a['s5_b_im'][l].reshape(S5_N, 16), 16) for l in range(2)])
    p['ctr'] = jnp.stack([_block_diag(a['s5_c_re'][l].reshape(GROUP_W, 64), 16) for l in range(2)])
    p['cti'] = jnp.stack([_block_diag(a['s5_c_im'][l].reshape(GROUP_W, 64), 16) for l in range(2)])
    p['s5_d'] = a['s5_d'].reshape(2, 1, GROUP_W)
    p['glu_w'] = glu_full
    p['glu_b'] = a['s5_glu_b'].reshape(2, 1, GROUP_W)
    lre, lim, bbr, bbi = [], [], [], []
    for l in range(2):
        r = _s5_prep(p['are_c'][l], p['aim_c'][l], p['ls_c'][l], p['bre_bd'][l], p['bim_bd'][l])
        lre.append(r[0].reshape(1, S5_N))
        lim.append(r[1].reshape(1, S5_N))
        bbr.append(r[2])
        bbi.append(r[3])
    p['lre'], p['lim'], p['bbr'], p['bbi'] = lre, lim, bbr, bbi
    return p, w_in0


def kernel(x, c, norm_mix_w, norm_mlp_w, ada_w, ada_b, w_in, pool_w, pool_scale, sconv_w, ssd_conv_w, ssd_conv_b, ssd_dt_bias, ssd_a_log, ssd_d, s5_a_re, s5_a_im, s5_log_step, s5_b_re, s5_b_im, s5_c_re, s5_c_im, s5_d, s5_glu_w, s5_glu_b, branch_norm_w, w_out, mlp_w1, mlp_w2, final_norm_w, loss_target, m_norm_mix_w, m_norm_mlp_w, m_ada_w, m_ada_b, m_w_in, m_pool_w, m_pool_scale, m_sconv_w, m_ssd_conv_w, m_ssd_conv_b, m_ssd_dt_bias, m_ssd_a_log, m_ssd_d, m_s5_a_re, m_s5_a_im, m_s5_log_step, m_s5_b_re, m_s5_b_im, m_s5_c_re, m_s5_c_im, m_s5_d, m_s5_glu_w, m_s5_glu_b, m_branch_norm_w, m_w_out, m_mlp_w1, m_mlp_w2, m_final_norm_w, v_norm_mix_w, v_norm_mlp_w, v_ada_w, v_ada_b, v_w_in, v_pool_w, v_pool_scale, v_sconv_w, v_ssd_conv_w, v_ssd_conv_b, v_ssd_dt_bias, v_ssd_a_log, v_ssd_d, v_s5_a_re, v_s5_a_im, v_s5_log_step, v_s5_b_re, v_s5_b_im, v_s5_c_re, v_s5_c_im, v_s5_d, v_s5_glu_w, v_s5_glu_b, v_branch_norm_w, v_w_out, v_mlp_w1, v_mlp_w2, v_final_norm_w):
    a = dict(locals())
    t = x.shape[1]
    tb = min(TB, t)
    me = _my_index()
    sh_b = _cast_shards([_reorder_in(w_in), w_out, mlp_w1, mlp_w2])
    p, w_in0 = _prepare_params(a, me, sh_b[0][0])
    w = {('w_in', 0): w_in0.reshape(D_MODEL, P_IN)}

    h = x.reshape(t, D_MODEL)
    saved = []
    for l in range(2):
        h, s = _layer_forward(l, h, p, w, sh_b, tb)
        saved.append(s)
    loss_blk, dh, dfinal = _loss_head(h, loss_target.reshape(t, D_MODEL), final_norm_w.reshape(1, D_MODEL), tb)

    pending, recv, small_parts = {}, {}, [None, None]
    for l in (1, 0):
        dh, small_parts[l] = _layer_backward(l, dh, saved[l], p, w, pending, recv, tb)
    grad_x = dh.reshape(1, t, D_MODEL)

    grads, deltas, new_m, new_v = {}, {}, {}, {}

    wmv_in = [_reorder_in(a[n]) for n in ('w_in', 'm_w_in', 'v_w_in')]
    outs = _sum_adamw_layers(recv['w_in', 0], recv['w_in', 1], *wmv_in, "adamw_w_in", 128)
    grads['w_in'], deltas['w_in'], new_m['w_in'], new_v['w_in'] = [_unreorder_in(o) for o in outs]
    for name, rb in (('w_out', 128), ('mlp_w1', 256), ('mlp_w2', 256)):
        grads[name], deltas[name], new_m[name], new_v[name] = _sum_adamw_layers(
            recv[name, 0], recv[name, 1], a[name], a['m_' + name], a['v_' + name], "adamw_" + name, rb)

    dmod = jnp.stack([small_parts[0]['ada_b'], small_parts[1]['ada_b']])
    dmod_recv = _all_to_all_rows(dmod.reshape(2, N_DEV, 768).transpose(1, 0, 2), "exchange_dmod")
    g_ada = _ada_backward(p['cond'], dmod_recv.transpose(1, 0, 2))
    grads['ada_w'], deltas['ada_w'], new_m['ada_w'], new_v['ada_w'] = _sum_adamw_layers(
        g_ada[0:1], g_ada[1:2], ada_w, m_ada_w, v_ada_w, "adamw_ada_w", 256)

    layered = [n for n in _SMALL if n != 'final_norm_w']
    full = [jnp.stack([small_parts[0][n], small_parts[1][n]]) for n in layered] + [dfinal.reshape(D_MODEL)]
    full.append(loss_blk[0:1, 0:1])
    full_shapes = [f.shape for f in full]
    summed = _flat_unpack(_allreduce_rows(_flat_pack(full, row_multiple=64)), full_shapes)
    loss = summed[-1].reshape(())
    local = []
    for n, g in zip(_SMALL, summed):
        if n in _SHARDED_SMALL:
            axis, size = _SHARDED_SMALL[n]
            g = lax.dynamic_slice_in_dim(g, me * size, size, axis=axis)
        local.append(g.reshape(a[n].shape))
    outs = _adamw_small([a[n] for n in _SMALL], local, [a['m_' + n] for n in _SMALL], [a['v_' + n] for n in _SMALL])
    for i, n in enumerate(_SMALL):
        grads[n], deltas[n], new_m[n], new_v[n] = local[i], outs[3 * i], outs[3 * i + 1], outs[3 * i + 2]

    return (loss, grad_x, *[grads[n] for n in _W_NAMES], *[deltas[n] for n in _W_NAMES],
            *[new_m[n] for n in _W_NAMES], *[new_v[n] for n in _W_NAMES])
```

```python
import functools

import jax
import jax.numpy as jnp
from jax import lax
from jax.experimental import pallas as pl
from jax.experimental.pallas import tpu as pltpu

f32 = jnp.float32
bf16 = jnp.bfloat16

N_DEV = 8
D_MODEL = 1024
GROUP_W = 256
P_IN = 2432
DT_COL = 2304
SSD_CHUNK = 128
SSD_HEADS = 4
SSD_P = 64
S5_N = 1024
MLP_HB = 512
TB = 1024
TB_BWD = 512
MLP_TB = 1024
MLP_SLABS = 2
MLP_BWD_SLABS = 8
EPS = 1e-6
LANES = 128
VMEM_LIMIT = 56 * 1024 * 1024
ADAM_LR, ADAM_B1, ADAM_B2, ADAM_EPS, ADAM_WD, ADAM_STEP = 0.001, 0.9, 0.999, 1e-08, 0.01, 10
POOL_WINDOWS = (2, 4, 8, 16)

C_POOL, C_GB, C_GC, C_HH, C_Z, C_XS, C_BM, C_CM, C_S5 = range(9)
C_DT128 = DT_COL // LANES

MESH = pl.DeviceIdType.MESH
ANY = pl.BlockSpec(memory_space=pl.ANY)
VMEM = pl.BlockSpec(memory_space=pltpu.VMEM)


def _dot(a, b):
    return jnp.dot(a, b, preferred_element_type=f32)


def _dot_nt(a, b):
    return lax.dot_general(a, b, (((1,), (1,)), ((), ())), preferred_element_type=f32)


def _dot_tn(a, b):
    return lax.dot_general(a, b, (((0,), (0,)), ((), ())), preferred_element_type=f32)


def _dot_exact(a, b):
    return jnp.dot(a, b, preferred_element_type=f32, precision=lax.Precision.HIGHEST)


def _b(x):
    return x.astype(bf16)


def _silu(x):
    return x * jax.nn.sigmoid(x)


def _dsilu(x):
    s = jax.nn.sigmoid(x)
    return s * (1.0 + x * (1.0 - s))


def _softplus(x):
    return jnp.maximum(x, 0.0) + jnp.log1p(jnp.exp(-jnp.abs(x)))


_GELU_K = 0.7978845608028654
_GELU_C = 0.044715


def _gelu(x):
    return 0.5 * x * (1.0 + jnp.tanh(_GELU_K * (x + _GELU_C * x * x * x)))


def _dgelu(x):
    th = jnp.tanh(_GELU_K * (x + _GELU_C * x * x * x))
    return 0.5 * (1.0 + th) + 0.5 * x * (1.0 - th * th) * _GELU_K * (1.0 + 3.0 * _GELU_C * x * x)


def _rms(h):
    r = lax.rsqrt(jnp.mean(h * h, axis=-1, keepdims=True) + EPS)
    return h * r, r


def _rms_bwd(dn, n, r):
    return r * (dn - n * jnp.mean(dn * n, axis=-1, keepdims=True))


def _colsum(x):
    return jnp.sum(x, axis=0, keepdims=True)


def _params(sem=None):
    return pltpu.CompilerParams(dimension_semantics=sem, vmem_limit_bytes=VMEM_LIMIT)


def _full(shape):
    return pl.BlockSpec(shape, lambda *_: (0,) * len(shape))


def _acc(ref, val):
    @pl.when(pl.program_id(0) == 0)
    def _():
        ref[...] = val

    @pl.when(pl.program_id(0) != 0)
    def _():
        ref[...] += val


def _me():
    return lax.axis_index("x"), lax.axis_index("y"), lax.axis_index("c")


def _my_index():
    x, y, c = _me()
    return 4 * x + 2 * y + c


def _coords(p):
    return (p // 4, (p // 2) % 2, p % 2)


class _Scatter:
    def __init__(self, srcs):
        self.srcs = list(srcs)
        self.n = len(self.srcs)
        self.out_shape = [jax.ShapeDtypeStruct(s.shape, s.dtype) for s in self.srcs]
        self.scratch = [pltpu.SemaphoreType.DMA((self.n, N_DEV)), pltpu.SemaphoreType.DMA((self.n, N_DEV)),
                        pltpu.SemaphoreType.DMA((self.n,))]

    def _remote(self, xin, xout, sems, t, k, me, to):
        return pltpu.make_async_remote_copy(
            src_ref=xin[t].at[to], dst_ref=xout[t].at[me], send_sem=sems[0].at[t, k], recv_sem=sems[1].at[t, k],
            device_id=_coords(to), device_id_type=MESH)

    def start(self, xin, xout, sems):
        me = _my_index()
        for t in range(self.n):
            pltpu.make_async_copy(xin[t].at[me], xout[t].at[me], sems[2].at[t]).start()
            for k in range(1, N_DEV):
                self._remote(xin, xout, sems, t, k, me, (me + k) % N_DEV).start()

    def forward(self, xin, xout, sems):
        pass

    def wait(self, xin, xout, sems):
        me = _my_index()
        for t in range(self.n):
            for k in range(1, N_DEV):
                src = (me + N_DEV - k) % N_DEV
                pltpu.make_async_remote_copy(
                    src_ref=xin[t].at[src], dst_ref=xout[t].at[src], send_sem=sems[0].at[t, k],
                    recv_sem=sems[1].at[t, k], device_id=_coords(src), device_id_type=MESH).wait_recv()
        for t in range(self.n):
            for k in range(1, N_DEV):
                self._remote(xin, xout, sems, t, k, me, (me + k) % N_DEV).wait_send()
            pltpu.make_async_copy(xin[t].at[me], xout[t].at[me], sems[2].at[t]).wait()


class _ChipGather:
    def __init__(self, srcs):
        self.srcs = list(srcs)
        self.n = len(self.srcs)
        self.out_shape = [jax.ShapeDtypeStruct((N_DEV,) + s.shape, s.dtype) for s in self.srcs]
        self.scratch = [pltpu.SemaphoreType.DMA((self.n, 7)), pltpu.SemaphoreType.DMA((self.n, 7)),
                        pltpu.SemaphoreType.DMA((self.n,))]

    @staticmethod
    def _places():
        x, y, c = _me()
        chips = [(1 - x, y), (x, 1 - y), (1 - x, 1 - y)]
        return (x, y, c), (x, y, 1 - c), chips

    @staticmethod
    def _slab(ref, dev):
        return ref.at[4 * dev[0] + 2 * dev[1] + dev[2]]

    def _copy(self, xin, xout, sems, t, k, block, to, src=None):
        return pltpu.make_async_remote_copy(
            src_ref=self._slab(xout[t], block) if src is None else src, dst_ref=self._slab(xout[t], block),
            send_sem=sems[0].at[t, k], recv_sem=sems[1].at[t, k], device_id=to, device_id_type=MESH)

    def start(self, xin, xout, sems):
        me, sibling, chips = self._places()
        for t in range(self.n):
            pltpu.make_async_copy(xin[t], self._slab(xout[t], me), sems[2].at[t]).start()
            self._copy(xin, xout, sems, t, 0, me, sibling, src=xin[t]).start()
            for j, chip in enumerate(chips):
                self._copy(xin, xout, sems, t, 1 + j, me, (*chip, me[2]), src=xin[t]).start()

    def forward(self, xin, xout, sems):
        me, sibling, chips = self._places()
        for t in range(self.n):
            for j, chip in enumerate(chips):
                self._copy(xin, xout, sems, t, 1 + j, (*chip, me[2]), me).wait_recv()
                self._copy(xin, xout, sems, t, 4 + j, (*chip, me[2]), sibling).start()

    def wait(self, xin, xout, sems):
        me, sibling, chips = self._places()
        for t in range(self.n):
            self._copy(xin, xout, sems, t, 0, sibling, me).wait_recv()
            for j, chip in enumerate(chips):
                self._copy(xin, xout, sems, t, 4 + j, (*chip, 1 - me[2]), me).wait_recv()
        for t in range(self.n):
            self._copy(xin, xout, sems, t, 0, me, sibling, src=xin[t]).wait_send()
            for j, chip in enumerate(chips):
                self._copy(xin, xout, sems, t, 1 + j, me, (*chip, me[2]), src=xin[t]).wait_send()
                self._copy(xin, xout, sems, t, 4 + j, (*chip, me[2]), sibling).wait_send()
            pltpu.make_async_copy(xin[t], self._slab(xout[t], me), sems[2].at[t]).wait()


def _call(body, *, name, grid, in_specs, out_specs, out_shape, args, semantics, scratch_shapes=(), xchg=None):
    if xchg is None:
        outs = pl.pallas_call(body, name=name, grid=grid, in_specs=in_specs, out_specs=out_specs, out_shape=out_shape,
                              scratch_shapes=list(scratch_shapes), compiler_params=_params(semantics))(*args)
        return outs, ()
    n_in, n_out, n_scr, n = len(in_specs), len(out_specs), len(scratch_shapes), xchg.n

    def carried(*refs):
        ins, xin = refs[:n_in], refs[n_in:n_in + n]
        outs, xout = refs[n_in + n:n_in + n + n_out], refs[n_in + n + n_out:n_in + 2 * n + n_out]
        scr, sems = refs[n_in + 2 * n + n_out:n_in + 2 * n + n_out + n_scr], refs[n_in + 2 * n + n_out + n_scr:]
        step = pl.program_id(0)
        for d in range(1, len(grid)):
            step = step * grid[d] + pl.program_id(d)
        n_steps = functools.reduce(lambda a, b: a * b, grid)

        @pl.when(step == 0)
        def _():
            xchg.start(xin, xout, sems)

        @pl.when(step == (2 * n_steps) // 3)
        def _():
            xchg.forward(xin, xout, sems)

        body(*ins, *outs, *scr)

        @pl.when(step == n_steps - 1)
        def _():
            xchg.wait(xin, xout, sems)

    res = pl.pallas_call(
        carried, name=name, grid=grid, in_specs=list(in_specs) + [ANY] * n, out_specs=list(out_specs) + [ANY] * n,
        out_shape=list(out_shape) + xchg.out_shape, scratch_shapes=list(scratch_shapes) + xchg.scratch,
        compiler_params=_params(("arbitrary",) * len(grid)))(*args, *xchg.srcs)
    return res[:n_out], tuple(res[n_out:])


def _exchange_alone(xchg, name):
    def body(*refs):
        xin, xout, sems = refs[:xchg.n], refs[xchg.n:2 * xchg.n], refs[2 * xchg.n:]
        xchg.start(xin, xout, sems)
        xchg.forward(xin, xout, sems)
        xchg.wait(xin, xout, sems)

    return pl.pallas_call(body, name=name, out_shape=xchg.out_shape, in_specs=[ANY] * xchg.n, out_specs=[ANY] * xchg.n,
                          scratch_shapes=xchg.scratch)(*xchg.srcs)


def _cast_shards(shards):
    n = len(shards)

    def body(*refs):
        for i, o in zip(refs[:n], refs[n:]):
            o[...] = i[...].astype(bf16)

    return pl.pallas_call(body, name="cast_shards", out_shape=[jax.ShapeDtypeStruct(s.shape, bf16) for s in shards],
                          in_specs=[VMEM] * n, out_specs=[VMEM] * n, compiler_params=_params())(*shards)


def _allreduce_rows(v):
    r = v.shape[0]
    rp = r // N_DEV

    def body(v_ref, o_ref, parts, sums, send1, recv1, send2, recv2):
        me = _my_index()

        def piece(ref, d):
            return ref.at[pl.ds(pl.multiple_of(d * rp, 8), rp), :]

        def copy1(k, src_dev, to):
            return pltpu.make_async_remote_copy(src_ref=piece(v_ref, to), dst_ref=parts.at[src_dev], send_sem=send1.at[k],
                                                recv_sem=recv1.at[k], device_id=_coords(to), device_id_type=MESH)

        def copy2(k, owner, to):
            return pltpu.make_async_remote_copy(src_ref=sums, dst_ref=piece(o_ref, owner), send_sem=send2.at[k],
                                                recv_sem=recv2.at[k], device_id=_coords(to), device_id_type=MESH)

        for k in range(1, N_DEV):
            copy1(k, me, (me + k) % N_DEV).start()
        parts[me] = v_ref[pl.ds(pl.multiple_of(me * rp, 8), rp), :]
        for k in range(1, N_DEV):
            copy1(k, (me + N_DEV - k) % N_DEV, me).wait_recv()
        total = parts[0]
        for s in range(1, N_DEV):
            total = total + parts[s]
        sums[...] = total
        o_ref[pl.ds(pl.multiple_of(me * rp, 8), rp), :] = total
        for k in range(1, N_DEV):
            copy2(k, me, (me + k) % N_DEV).start()
        for k in range(1, N_DEV):
            copy2(k, (me + N_DEV - k) % N_DEV, me).wait_recv()
        for k in range(1, N_DEV):
            copy1(k, me, (me + k) % N_DEV).wait_send()
            copy2(k, me, (me + k) % N_DEV).wait_send()

    return pl.pallas_call(
        body, name="allreduce_small_grads", out_shape=jax.ShapeDtypeStruct(v.shape, v.dtype),
        in_specs=[VMEM], out_specs=VMEM,
        scratch_shapes=[pltpu.VMEM((N_DEV, rp, LANES), f32), pltpu.VMEM((rp, LANES), f32)]
        + [pltpu.SemaphoreType.DMA((N_DEV,))] * 4,
        compiler_params=_params(),
    )(v)


def _all_to_all_rows(v, name):
    def body(v_ref, o_ref, send_sems, recv_sems):
        me = _my_index()
        o_ref[me] = v_ref[me]
        sends = []
        for k in range(1, N_DEV):
            peer = (me + k) % N_DEV
            rc = pltpu.make_async_remote_copy(src_ref=v_ref.at[peer], dst_ref=o_ref.at[me], send_sem=send_sems.at[k],
                                              recv_sem=recv_sems.at[k], device_id=_coords(peer), device_id_type=MESH)
            rc.start()
            sends.append(rc)
        for k in range(1, N_DEV):
            src = (me + N_DEV - k) % N_DEV
            pltpu.make_async_remote_copy(src_ref=v_ref.at[src], dst_ref=o_ref.at[src], send_sem=send_sems.at[k],
                                         recv_sem=recv_sems.at[k], device_id=_coords(src), device_id_type=MESH).wait_recv()
        for rc in sends:
            rc.wait_send()

    return pl.pallas_call(
        body, name=name, out_shape=jax.ShapeDtypeStruct(v.shape, v.dtype),
        in_specs=[VMEM], out_specs=VMEM,
        scratch_shapes=[pltpu.SemaphoreType.DMA((N_DEV,)), pltpu.SemaphoreType.DMA((N_DEV,))],
    )(v)


def _ada_forward(c_all, ada_w, ada_b_cols):
    def body(c_ref, w_ref, b_ref, cond_ref, o_ref):
        cond = _silu(c_ref[...])
        cond_ref[...] = cond
        for l in range(2):
            o_ref[l] = _dot(_b(cond), _b(w_ref[l])) + b_ref[l]

    return pl.pallas_call(
        body, name="ada_forward",
        out_shape=[jax.ShapeDtypeStruct((N_DEV, D_MODEL), f32), jax.ShapeDtypeStruct((2, N_DEV, 768), f32)],
        in_specs=[VMEM] * 3, out_specs=[VMEM] * 2, compiler_params=_params(),
    )(c_all, ada_w, ada_b_cols)


def _ada_backward(cond, dmod_rows):
    def body(c_ref, d_ref, o_ref):
        cb = _b(c_ref[...])
        for l in range(2):
            o_ref[l] = _dot_tn(cb, _b(d_ref[l]))

    return pl.pallas_call(
        body, name="ada_backward", out_shape=jax.ShapeDtypeStruct((2, D_MODEL, 768), f32),
        in_specs=[VMEM] * 2, out_specs=VMEM, compiler_params=_params(),
    )(cond, dmod_rows)


def _inproj_fwd(h, norm_w, sc, sh, w_in, tb, xchg=None):
    t = h.shape[0]

    def body(h_ref, nw_ref, sc_ref, sh_ref, w_ref, proj_ref, u_ref):
        n, _ = _rms(h_ref[...])
        u = _b(n * nw_ref[...] * (1.0 + sc_ref[...]) + sh_ref[...])
        u_ref[...] = u
        proj_ref[...] = _dot(u, w_ref[...])

    row = pl.BlockSpec((tb, D_MODEL), lambda i: (i, 0))
    vec = _full((1, D_MODEL))
    return _call(
        body, name="inproj_fwd", grid=(t // tb,),
        out_shape=[jax.ShapeDtypeStruct((t, P_IN), f32), jax.ShapeDtypeStruct((t, D_MODEL), bf16)],
        in_specs=[row, vec, vec, vec, _full((D_MODEL, P_IN))],
        out_specs=[pl.BlockSpec((tb, P_IN), lambda i: (i, 0)), row],
        semantics=("parallel",), args=(h, norm_w, sc, sh, w_in), xchg=xchg)


def _inproj_bwd(dparts, dh_res, h, norm_w, sc, sh, w_in, tb, xchg=None):
    t = h.shape[0]

    def body(*refs):
        parts = refs[:10]
        dres_ref, h_ref, nw_ref, sc_ref, sh_ref, w_ref = refs[10:16]
        dh_ref, dsh_ref, dsc_ref, dnw_ref = refs[16:]
        dproj = jnp.concatenate([p[...] for p in parts], axis=1)
        du = _dot_nt(dproj, w_ref[...])
        n, r = _rms(h_ref[...])
        nw = nw_ref[...]
        gain = 1.0 + sc_ref[...]
        _acc(dsh_ref, _colsum(du))
        _acc(dsc_ref, _colsum(du * n * nw))
        _acc(dnw_ref, _colsum(du * gain * n))
        dh_ref[...] = dres_ref[...] + _rms_bwd(du * nw * gain, n, r)

    row = pl.BlockSpec((tb, D_MODEL), lambda i: (i, 0))
    vec = _full((1, D_MODEL))
    part_specs = [pl.BlockSpec((tb, GROUP_W), lambda i: (i, 0))] * 9 + [pl.BlockSpec((tb, LANES), lambda i: (i, 0))]
    return _call(
        body, name="inproj_bwd", grid=(t // tb,),
        out_shape=[jax.ShapeDtypeStruct((t, D_MODEL), f32)] + [jax.ShapeDtypeStruct((1, D_MODEL), f32)] * 3,
        in_specs=part_specs + [row, row, vec, vec, vec,
                               pl.BlockSpec((D_MODEL, P_IN), lambda i: (0, 0), pipeline_mode=pl.Buffered(1))],
        out_specs=[row, vec, vec, vec],
        semantics=("arbitrary",), xchg=xchg, args=(*dparts, dh_res, h, norm_w, sc, sh, w_in))


def _wgrad(a, b, n_blocks, name, tm, tk=512):
    t, m = a.shape
    nb = b.shape[1] // n_blocks
    tk = min(tk, t)
    nk = t // tk

    def body(a_ref, b_ref, o_ref, acc_ref):
        k = pl.program_id(2)
        p = _dot_tn(a_ref[...], b_ref[...])

        @pl.when(k == 0)
        def _():
            acc_ref[...] = p

        @pl.when(k != 0)
        def _():
            acc_ref[...] += p

        @pl.when(k == nk - 1)
        def _():
            o_ref[0] = acc_ref[...].astype(o_ref.dtype)

    return pl.pallas_call(
        body, name=name, grid=(m // tm, n_blocks, nk),
        out_shape=jax.ShapeDtypeStruct((n_blocks, m, nb), bf16),
        in_specs=[pl.BlockSpec((tk, tm), lambda i, j, k: (k, i)), pl.BlockSpec((tk, nb), lambda i, j, k: (k, j))],
        out_specs=pl.BlockSpec((1, tm, nb), lambda i, j, k: (j, i, 0)),
        scratch_shapes=[pltpu.VMEM((tm, nb), f32)],
        compiler_params=_params(("parallel", "parallel", "arbitrary")),
    )(a, b)


def _wgrad_parts(a, parts, name, tm, tk):
    t, m = a.shape
    n = sum(p.shape[1] for p in parts)
    n_parts = len(parts)
    tk = min(tk, t)
    nk = t // tk

    def body(*refs):
        a_ref, part_refs, o_ref, acc_ref = refs[0], refs[1:1 + n_parts], refs[1 + n_parts], refs[2 + n_parts]
        k = pl.program_id(1)
        p = _dot_tn(a_ref[...], jnp.concatenate([r[...] for r in part_refs], axis=1))

        @pl.when(k == 0)
        def _():
            acc_ref[...] = p

        @pl.when(k != 0)
        def _():
            acc_ref[...] += p

        @pl.when(k == nk - 1)
        def _():
            o_ref[...] = acc_ref[...].astype(o_ref.dtype)

    return pl.pallas_call(
        body, name=name, grid=(m // tm, nk),
        out_shape=jax.ShapeDtypeStruct((m, n), bf16),
        in_specs=[pl.BlockSpec((tk, tm), lambda i, k: (k, i))]
        + [pl.BlockSpec((tk, p.shape[1]), lambda i, k: (k, 0)) for p in parts],
        out_specs=pl.BlockSpec((tm, n), lambda i, k: (i, 0)),
        scratch_shapes=[pltpu.VMEM((tm, n), f32)],
        compiler_params=_params(("parallel", "arbitrary")),
    )(a, *parts)


def _pool_counts(rows, t0):
    tpos = (lax.broadcasted_iota(jnp.int32, (rows, GROUP_W), 0) + t0 + 1).astype(f32)
    grp = lax.broadcasted_iota(jnp.int32, (rows, GROUP_W), 1) // 64
    win = jnp.where(grp == 0, 2.0, jnp.where(grp == 1, 4.0, jnp.where(grp == 2, 8.0, 16.0)))
    return jnp.minimum(tpos, win), grp


def _pool_select(grp, l1, l2, l3, l4):
    return jnp.where(grp == 0, l1, jnp.where(grp == 1, l2, jnp.where(grp == 2, l3, l4)))


def _pool_means(v, halo, t0):
    tb = v.shape[0]
    ext = jnp.concatenate([halo, v], axis=0)
    n = tb + 16
    s1 = ext[1:n] + ext[0:n - 1]
    s2 = s1[2:n - 1] + s1[0:n - 3]
    s3 = s2[4:n - 3] + s2[0:n - 7]
    s4 = s3[8:n - 7] + s3[0:n - 15]
    cnt, grp = _pool_counts(tb, t0)
    wsum = _pool_select(grp, s1[15:15 + tb], s2[13:13 + tb], s3[9:9 + tb], s4[1:1 + tb])
    return wsum / cnt - v


def _pool_fwd(proj, pw_bd, scale, tb):
    t = proj.shape[0]

    def body(v_ref, vh_ref, pw_ref, sc_ref, o_ref):
        i = pl.program_id(0)
        halo = jnp.where(i > 0, vh_ref[...], 0.0)
        p = _pool_means(v_ref[...], halo, i * tb)
        o_ref[...] = _b(_dot(_b(p), _b(pw_ref[...])) * sc_ref[...])

    return pl.pallas_call(
        body, name="pool_fwd", grid=(t // tb,),
        out_shape=jax.ShapeDtypeStruct((t, GROUP_W), bf16),
        in_specs=[pl.BlockSpec((tb, GROUP_W), lambda i: (i, C_POOL)),
                  pl.BlockSpec((16, GROUP_W), lambda i: (jnp.maximum(i * (tb // 16) - 1, 0), C_POOL)),
                  _full((GROUP_W, GROUP_W)), _full((1, GROUP_W))],
        out_specs=pl.BlockSpec((tb, GROUP_W), lambda i: (i, 0)),
        compiler_params=_params(("parallel",)),
    )(proj, proj, pw_bd, scale)


def _pool_bwd(proj, dy, pw_bd, scale, tb):
    t = proj.shape[0]
    nt = t // tb
    last16 = t // 16 - 1

    def body(v_ref, vh_ref, dy_ref, dyh_ref, pw_ref, sc_ref, dv_ref, dpw_ref, dsc_ref):
        i = pl.program_id(0)
        halo = jnp.where(i > 0, vh_ref[...], 0.0)
        p = _pool_means(v_ref[...], halo, i * tb)
        pw = _b(pw_ref[...])
        sc = sc_ref[...]
        dy = dy_ref[...]
        ypre = _dot(_b(p), pw)
        _acc(dsc_ref, _colsum(dy * ypre))
        dys = _b(dy * sc)
        _acc(dpw_ref, _dot_tn(_b(p), dys))
        dp = _dot_nt(dys, pw)
        dph = _dot_nt(_b(jnp.where(i < nt - 1, dyh_ref[...], 0.0) * sc), pw)
        cnt, grp = _pool_counts(tb, i * tb)
        cnth, _ = _pool_counts(16, (i + 1) * tb)
        ext = jnp.concatenate([dp / cnt, dph / cnth], axis=0)
        n = tb + 16
        f1 = ext[0:n - 1] + ext[1:n]
        f2 = f1[0:n - 3] + f1[2:n - 1]
        f3 = f2[0:n - 7] + f2[4:n - 3]
        f4 = f3[0:n - 15] + f3[8:n - 7]
        dv_ref[...] = _b(_pool_select(grp, f1[0:tb], f2[0:tb], f3[0:tb], f4[0:tb]) - dp)

    return pl.pallas_call(
        body, name="pool_bwd", grid=(nt,),
        out_shape=[jax.ShapeDtypeStruct((t, GROUP_W), bf16), jax.ShapeDtypeStruct((GROUP_W, GROUP_W), f32),
                   jax.ShapeDtypeStruct((1, GROUP_W), f32)],
        in_specs=[pl.BlockSpec((tb, GROUP_W), lambda i: (i, C_POOL)),
                  pl.BlockSpec((16, GROUP_W), lambda i: (jnp.maximum(i * (tb // 16) - 1, 0), C_POOL)),
                  pl.BlockSpec((tb, GROUP_W), lambda i: (i, 0)),
                  pl.BlockSpec((16, GROUP_W), lambda i: (jnp.minimum((i + 1) * (tb // 16), last16), 0)),
                  _full((GROUP_W, GROUP_W)), _full((1, GROUP_W))],
        out_specs=[pl.BlockSpec((tb, GROUP_W), lambda i: (i, 0)), _full((GROUP_W, GROUP_W)), _full((1, GROUP_W))],
        compiler_params=_params(("arbitrary",)),
    )(proj, proj, dy, dy, pw_bd, scale)


def _sconv_fwd(proj, w, tb):
    t = proj.shape[0]

    def body(gb_ref, gc_ref, hh_ref, gch_ref, hhh_ref, w_ref, o_ref):
        i = pl.program_id(0)
        q = gc_ref[...] * hh_ref[...]
        qh = jnp.where(i > 0, gch_ref[...] * hhh_ref[...], 0.0)
        ext = jnp.concatenate([qh, q], axis=0)
        w = w_ref[...]
        conv = w[0:1] * ext[6:6 + tb] + w[1:2] * ext[7:7 + tb] + w[2:3] * ext[8:8 + tb]
        o_ref[...] = _b(gb_ref[...] * conv)

    def col(c):
        return pl.BlockSpec((tb, GROUP_W), lambda i: (i, c))

    def prev(c):
        return pl.BlockSpec((8, GROUP_W), lambda i: (jnp.maximum(i * (tb // 8) - 1, 0), c))

    return pl.pallas_call(
        body, name="sconv_fwd", grid=(t // tb,),
        out_shape=jax.ShapeDtypeStruct((t, GROUP_W), bf16),
        in_specs=[col(C_GB), col(C_GC), col(C_HH), prev(C_GC), prev(C_HH), _full((8, GROUP_W))],
        out_specs=pl.BlockSpec((tb, GROUP_W), lambda i: (i, 0)),
        compiler_params=_params(("parallel",)),
    )(proj, proj, proj, proj, proj, w)


def _sconv_bwd(proj, dy, w, tb):
    t = proj.shape[0]
    nt = t // tb
    last8 = t // 8 - 1

    def body(gb_ref, gc_ref, hh_ref, gch_ref, hhh_ref, gbn_ref, dy_ref, dyn_ref, w_ref, dgb_ref, dgc_ref, dhh_ref, dw_ref):
        i = pl.program_id(0)
        gc, hh, gb, dy = gc_ref[...], hh_ref[...], gb_ref[...], dy_ref[...]
        q = gc * hh
        qh = jnp.where(i > 0, gch_ref[...] * hhh_ref[...], 0.0)
        ext = jnp.concatenate([qh, q], axis=0)
        w = w_ref[...]
        conv = w[0:1] * ext[6:6 + tb] + w[1:2] * ext[7:7 + tb] + w[2:3] * ext[8:8 + tb]
        dgb_ref[...] = _b(dy * conv)
        e = dy * gb
        en = jnp.where(i < nt - 1, dyn_ref[...] * gbn_ref[...], 0.0)
        exte = jnp.concatenate([e, en], axis=0)
        dq = w[2:3] * exte[0:tb] + w[1:2] * exte[1:1 + tb] + w[0:1] * exte[2:2 + tb]
        dgc_ref[...] = _b(dq * hh)
        dhh_ref[...] = _b(dq * gc)
        dw = jnp.concatenate([_colsum(e * ext[6:6 + tb]), _colsum(e * ext[7:7 + tb]), _colsum(e * ext[8:8 + tb]),
                              jnp.zeros((5, GROUP_W), f32)], axis=0)
        _acc(dw_ref, dw)

    def col(c):
        return pl.BlockSpec((tb, GROUP_W), lambda i: (i, c))

    def prev(c):
        return pl.BlockSpec((8, GROUP_W), lambda i: (jnp.maximum(i * (tb // 8) - 1, 0), c))

    def nxt(c):
        return pl.BlockSpec((8, GROUP_W), lambda i: (jnp.minimum((i + 1) * (tb // 8), last8), c))

    out = pl.BlockSpec((tb, GROUP_W), lambda i: (i, 0))
    return pl.pallas_call(
        body, name="sconv_bwd", grid=(nt,),
        out_shape=[jax.ShapeDtypeStruct((t, GROUP_W), bf16)] * 3 + [jax.ShapeDtypeStruct((8, GROUP_W), f32)],
        in_specs=[col(C_GB), col(C_GC), col(C_HH), prev(C_GC), prev(C_HH), nxt(C_GB), col(0), nxt(0), _full((8, GROUP_W))],
        out_specs=[out, out, out, _full((8, GROUP_W))],
        compiler_params=_params(("arbitrary",)),
    )(proj, proj, proj, proj, proj, proj, dy, dy, w)


def _conv4(xr, halo, w, bias):
    tb = xr.shape[0]
    ext = jnp.concatenate([halo, xr], axis=0)
    pre = w[0:1] * ext[5:5 + tb] + w[1:2] * ext[6:6 + tb] + w[2:3] * ext[7:7 + tb] + w[3:4] * ext[8:8 + tb] + bias
    return pre, ext


def _tri():
    r = lax.broadcasted_iota(jnp.int32, (SSD_CHUNK, SSD_CHUNK), 0)
    c = lax.broadcasted_iota(jnp.int32, (SSD_CHUNK, SSD_CHUNK), 1)
    return r >= c


def _lane_pick(vals):
    rows = vals[0].shape[0]
    lane = lax.broadcasted_iota(jnp.int32, (rows, LANES), 1)
    out = jnp.zeros((rows, LANES), f32)
    for h, v in enumerate(vals):
        out = jnp.where(lane == h, v, out)
    return out


def _ssd_fwd(proj, conv_w, conv_b, dt_bias, a_log, d_cols, tb, xchg=None):
    t = proj.shape[0]
    cpt = tb // SSD_CHUNK

    def body(z_ref, xs_ref, bm_ref, cm_ref, xsh_ref, bmh_ref, cmh_ref, dt_ref, cw_ref, cb_ref, dtb_ref, al_ref, dk_ref,
             o_ref, y_ref, st_ref, state):
        i = pl.program_id(0)

        @pl.when(i == 0)
        def _():
            state[...] = jnp.zeros_like(state)

        cw, cb = cw_ref[...], cb_ref[...]
        acts = []
        for j, (r, hr) in enumerate(((xs_ref, xsh_ref), (bm_ref, bmh_ref), (cm_ref, cmh_ref))):
            halo = jnp.where(i > 0, hr[...], 0.0)
            pre, _ = _conv4(r[...], halo, cw[:, j * 256:(j + 1) * 256], cb[:, j * 256:(j + 1) * 256])
            acts.append(_silu(pre))
        xs, bm, cm = acts
        dt = _softplus(dt_ref[...] + dtb_ref[...])
        a = -jnp.exp(al_ref[...])
        adt = dt * a
        tri = _tri()
        trif = tri.astype(f32)
        dk = dk_ref[...]
        for c in range(cpt):
            rows = slice(c * SSD_CHUNK, (c + 1) * SSD_CHUNK)
            acol = _dot_exact(trif, adt[rows])
            arow = acol.T
            dt_c = dt[rows]
            ys = []
            rowi = lax.broadcasted_iota(jnp.int32, (SSD_CHUNK, 1), 0)
            first = lax.broadcasted_iota(jnp.int32, (SSD_CHUNK, SSD_CHUNK), 1) < SSD_P
            for g in range(SSD_HEADS // 2):
                cols = slice(g * 128, (g + 1) * 128)
                cg, bg = _b(cm[rows, cols]), _b(bm[rows, cols])
                xg = xs[rows, cols]
                heads = (2 * g, 2 * g + 1)
                ac = [acol[:, h:h + 1] for h in heads]
                alast = [v[SSD_CHUNK - 1:SSD_CHUNK] for v in ac]
                dtw = jnp.where(first, dt_c[:, heads[0]:heads[0] + 1], dt_c[:, heads[1]:heads[1] + 1])
                eaw = jnp.where(first, jnp.exp(ac[0]), jnp.exp(ac[1]))
                wdw = jnp.where(first, jnp.exp(alast[0] - ac[0]), jnp.exp(alast[1] - ac[1]))
                xdt = xg * dtw
                xb = _b(xdt)
                gmat = _dot_nt(cg, bg)
                ydiag = []
                for k, h in enumerate(heads):
                    lm = jnp.exp(jnp.where(tri, ac[k] - arow[h:h + 1, :], -jnp.inf))
                    ydiag.append(_dot(_b(gmat * lm), xb[:, k * SSD_P:(k + 1) * SSD_P]))
                s_in = state[g]
                st_ref[c, g] = s_in
                ys.append(jnp.concatenate(ydiag, axis=1) + eaw * _dot_nt(cg, _b(s_in)) + xg * dk[:, cols])
                state[g] = jnp.where(rowi < SSD_P, jnp.exp(alast[0]), jnp.exp(alast[1])) * s_in + _dot_tn(_b(xdt * wdw), bg)
            yc = jnp.concatenate(ys, axis=1)
            y_ref[rows, :] = yc
            o_ref[rows, :] = _b(yc * _silu(z_ref[rows, :]))

    def col(c):
        return pl.BlockSpec((tb, GROUP_W), lambda i: (i, c))

    def prev(c):
        return pl.BlockSpec((8, GROUP_W), lambda i: (jnp.maximum(i * (tb // 8) - 1, 0), c))

    out = pl.BlockSpec((tb, GROUP_W), lambda i: (i, 0))
    return _call(
        body, name="ssd_fwd", grid=(t // tb,),
        out_shape=[jax.ShapeDtypeStruct((t, GROUP_W), bf16), jax.ShapeDtypeStruct((t, GROUP_W), f32),
                   jax.ShapeDtypeStruct((t // SSD_CHUNK, 2, 128, 128), f32)],
        in_specs=[col(C_Z), col(C_XS), col(C_BM), col(C_CM), prev(C_XS), prev(C_BM), prev(C_CM),
                  pl.BlockSpec((tb, LANES), lambda i: (i, C_DT128)),
                  _full((8, 768)), _full((1, 768)), _full((1, LANES)), _full((1, LANES)), _full((1, GROUP_W))],
        out_specs=[out, out, pl.BlockSpec((cpt, 2, 128, 128), lambda i: (i, 0, 0, 0))],
        scratch_shapes=[pltpu.VMEM((2, 128, 128), f32)],
        semantics=("arbitrary",), xchg=xchg,
        args=(proj, proj, proj, proj, proj, proj, proj, proj, conv_w, conv_b, dt_bias, a_log, d_cols))


def _ssd_bwd(proj, dyc, y_pre, states, conv_w, conv_b, dt_bias, a_log, d_cols, tb, xchg=None):
    t = proj.shape[0]
    nt = t // tb
    cpt = tb // SSD_CHUNK

    def body(z_ref, xs_ref, bm_ref, cm_ref, xsh_ref, bmh_ref, cmh_ref, dt_ref, dy_ref, yp_ref, st_ref,
             cw_ref, cb_ref, dtb_ref, al_ref, dk_ref,
             dz_ref, dxs_ref, dbm_ref, dcm_ref, ddt_ref, dcw_ref, dcb_ref, ddtb_ref, dal_ref, ddk_ref,
             dstate, carry):
        i = pl.program_id(0)
        ti = nt - 1 - i

        @pl.when(i == 0)
        def _():
            dstate[...] = jnp.zeros_like(dstate)
            carry[...] = jnp.zeros_like(carry)

        cw, cb = cw_ref[...], cb_ref[...]
        pres, exts, acts = [], [], []
        for j, (r, hr) in enumerate(((xs_ref, xsh_ref), (bm_ref, bmh_ref), (cm_ref, cmh_ref))):
            halo = jnp.where(ti > 0, hr[...], 0.0)
            pre, ext = _conv4(r[...], halo, cw[:, j * 256:(j + 1) * 256], cb[:, j * 256:(j + 1) * 256])
            pres.append(pre)
            exts.append(ext)
            acts.append(_silu(pre))
        xs, bm, cm = acts
        raw = dt_ref[...] + dtb_ref[...]
        dt = _softplus(raw)
        a = -jnp.exp(al_ref[...])
        adt = dt * a
        tri = _tri()
        trif = tri.astype(f32)
        dk = dk_ref[...]
        z = z_ref[...]
        dyc = dy_ref[...]
        dz_ref[...] = _b(dyc * yp_ref[...] * _dsilu(z))
        dy_all = dyc * _silu(z)
        lane = lax.broadcasted_iota(jnp.int32, (1, LANES), 1)
        ddk_acc = jnp.zeros((1, LANES), f32)
        dal_acc = jnp.zeros((1, LANES), f32)
        dxs_c, dbm_c, dcm_c, ddt_c = [None] * cpt, [None] * cpt, [None] * cpt, [None] * cpt
        for c in reversed(range(cpt)):
            rows = slice(c * SSD_CHUNK, (c + 1) * SSD_CHUNK)
            acol = _dot_exact(trif, adt[rows])
            arow = acol.T
            dt_c = dt[rows]
            da_cols, da_rows, ddt_heads, dxs_groups, dbg, dcg = [], [], [], [], [], []
            rowi = lax.broadcasted_iota(jnp.int32, (SSD_CHUNK, 1), 0)
            first = lax.broadcasted_iota(jnp.int32, (SSD_CHUNK, SSD_CHUNK), 1) < SSD_P
            for g in range(SSD_HEADS // 2):
                cols = slice(g * 128, (g + 1) * 128)
                cgf, bgf = cm[rows, cols], bm[rows, cols]
                cg, bg = _b(cgf), _b(bgf)
                xg, dyg = xs[rows, cols], dy_all[rows, cols]
                s_in, dsn = st_ref[c, g], dstate[g]
                sb, dsnb = _b(s_in), _b(dsn)
                heads = (2 * g, 2 * g + 1)
                ac = [acol[:, h:h + 1] for h in heads]
                alast = [v[SSD_CHUNK - 1:SSD_CHUNK] for v in ac]
                el = [jnp.exp(v) for v in alast]
                dtw = jnp.where(first, dt_c[:, heads[0]:heads[0] + 1], dt_c[:, heads[1]:heads[1] + 1])
                eaw = jnp.where(first, jnp.exp(ac[0]), jnp.exp(ac[1]))
                wdw = jnp.where(first, jnp.exp(alast[0] - ac[0]), jnp.exp(alast[1] - ac[1]))
                xdt = xg * dtw
                xb, dyb = _b(xdt), _b(dyg)
                gmat = _dot_nt(cg, bg)
                dgs, dxh, da = None, [], []
                for k, h in enumerate(heads):
                    hc = slice(k * SSD_P, (k + 1) * SSD_P)
                    lm = jnp.exp(jnp.where(tri, ac[k] - arow[h:h + 1, :], -jnp.inf))
                    m = gmat * lm
                    dm = _dot_nt(dyb[:, hc], xb[:, hc])
                    dxh.append(_dot_tn(_b(m), dyb[:, hc]))
                    dgs = dm * lm if dgs is None else dgs + dm * lm
                    wm = dm * m
                    da.append(jnp.sum(wm, axis=1, keepdims=True))
                    da_rows.append(jnp.sum(wm, axis=0, keepdims=True))
                dgb = _b(dgs)
                dcg_g = _dot(dgb, bg)
                dbg_g = _dot_tn(dgb, cg)
                yoff = eaw * _dot_nt(cg, sb)
                dyoff = dyg * yoff
                dye = _b(dyg * eaw)
                dcg_g = dcg_g + _dot(dye, sb)
                ds_y = _dot_tn(dye, cg)
                u = _dot_nt(bg, dsnb)
                dx = jnp.concatenate(dxh, axis=1) + wdw * u
                dbg_g = dbg_g + _dot(_b(xdt * wdw), dsnb)
                xu = xdt * u * wdw
                ss = jnp.sum(dsn * s_in, axis=1, keepdims=True)
                dxx = dx * xg
                dyx = _colsum(dyg * xg)
                for k, h in enumerate(heads):
                    mine = first if k == 0 else jnp.logical_not(first)
                    dwv = jnp.sum(jnp.where(mine, xu, 0.0), axis=1, keepdims=True)
                    mine_rows = (rowi < SSD_P) if k == 0 else (rowi >= SSD_P)
                    dalast = jnp.sum(dwv, axis=0, keepdims=True) + el[k] * jnp.sum(jnp.where(mine_rows, ss, 0.0), axis=0, keepdims=True)
                    dah = da[k] + jnp.sum(jnp.where(mine, dyoff, 0.0), axis=1, keepdims=True) - dwv
                    da_cols.append(dah + jnp.where(rowi == SSD_CHUNK - 1, dalast, 0.0))
                    ddt_heads.append(jnp.sum(jnp.where(mine, dxx, 0.0), axis=1, keepdims=True))
                    ddk_acc = ddk_acc + jnp.where(lane == h, jnp.sum(jnp.where(mine[0:1], dyx, 0.0), axis=1, keepdims=True), 0.0)
                dstate[g] = jnp.where(rowi < SSD_P, el[0], el[1]) * dsn + ds_y
                dxs_groups.append(dx * dtw + dyg * dk[:, cols])
                dbg.append(dbg_g)
                dcg.append(dcg_g)
            da_blk = _lane_pick(da_cols)
            rowsel = lax.broadcasted_iota(jnp.int32, (SSD_CHUNK, SSD_CHUNK), 0)
            da_rows_blk = jnp.zeros((SSD_CHUNK, SSD_CHUNK), f32)
            for h in range(SSD_HEADS):
                da_rows_blk = jnp.where(rowsel == h, da_rows[h], da_rows_blk)
            da_blk = da_blk - da_rows_blk.T
            dadt = lax.dot_general(trif, da_blk, (((0,), (0,)), ((), ())), preferred_element_type=f32,
                                   precision=lax.Precision.HIGHEST)
            dal_acc = dal_acc + _colsum(dadt * dt_c)
            ddt_c[c] = dadt * a + _lane_pick(ddt_heads)
            dxs_c[c] = jnp.concatenate(dxs_groups, axis=1)
            dbm_c[c] = jnp.concatenate(dbg, axis=1)
            dcm_c[c] = jnp.concatenate(dcg, axis=1)
        ddt = jnp.concatenate(ddt_c, axis=0) if cpt > 1 else ddt_c[0]
        ddraw = jnp.where(lane < SSD_HEADS, ddt * jax.nn.sigmoid(raw), 0.0)
        ddt_ref[...] = _b(ddraw)
        _acc(ddtb_ref, _colsum(ddraw))
        _acc(dal_ref, jnp.where(lane < SSD_HEADS, dal_acc * a, 0.0))
        _acc(ddk_ref, ddk_acc)
        dcw_parts, dcb_parts = [], []
        for j, (dparts, out_ref) in enumerate(((dxs_c, dxs_ref), (dbm_c, dbm_ref), (dcm_c, dcm_ref))):
            dact = jnp.concatenate(dparts, axis=0) if cpt > 1 else dparts[0]
            dpre = dact * _dsilu(pres[j])
            w = cw[:, j * 256:(j + 1) * 256]
            ext = jnp.concatenate([dpre, carry[:, j * 256:(j + 1) * 256]], axis=0)
            out_ref[...] = _b(w[3:4] * ext[0:tb] + w[2:3] * ext[1:1 + tb] + w[1:2] * ext[2:2 + tb] + w[0:1] * ext[3:3 + tb])
            carry[:, j * 256:(j + 1) * 256] = dpre[0:8]
            xe = exts[j]
            dcw_parts.append(jnp.concatenate([_colsum(dpre * xe[5 + k:5 + k + tb]) for k in range(4)]
                                             + [jnp.zeros((4, GROUP_W), f32)], axis=0))
            dcb_parts.append(_colsum(dpre))
        _acc(dcw_ref, jnp.concatenate(dcw_parts, axis=1))
        _acc(dcb_ref, jnp.concatenate(dcb_parts, axis=1))

    def col(c):
        return pl.BlockSpec((tb, GROUP_W), lambda i: (nt - 1 - i, c))

    def prev(c):
        return pl.BlockSpec((8, GROUP_W), lambda i: (jnp.maximum((nt - 1 - i) * (tb // 8) - 1, 0), c))

    out = pl.BlockSpec((tb, GROUP_W), lambda i: (nt - 1 - i, 0))
    vec = _full((1, LANES))
    return _call(
        body, name="ssd_bwd", grid=(nt,),
        out_shape=[jax.ShapeDtypeStruct((t, GROUP_W), bf16)] * 4 + [jax.ShapeDtypeStruct((t, LANES), bf16),
                   jax.ShapeDtypeStruct((8, 768), f32), jax.ShapeDtypeStruct((1, 768), f32)]
        + [jax.ShapeDtypeStruct((1, LANES), f32)] * 3,
        in_specs=[col(C_Z), col(C_XS), col(C_BM), col(C_CM), prev(C_XS), prev(C_BM), prev(C_CM),
                  pl.BlockSpec((tb, LANES), lambda i: (nt - 1 - i, C_DT128)), out, out,
                  pl.BlockSpec((cpt, 2, 128, 128), lambda i: (nt - 1 - i, 0, 0, 0)),
                  _full((8, 768)), _full((1, 768)), vec, vec, _full((1, GROUP_W))],
        out_specs=[out, out, out, out, pl.BlockSpec((tb, LANES), lambda i: (nt - 1 - i, 0)),
                   _full((8, 768)), _full((1, 768)), vec, vec, vec],
        scratch_shapes=[pltpu.VMEM((2, 128, 128), f32), pltpu.VMEM((8, 768), f32)],
        semantics=("arbitrary",), xchg=xchg,
        args=(proj, proj, proj, proj, proj, proj, proj, proj, dyc, y_pre, states, conv_w, conv_b, dt_bias, a_log, d_cols))


def _s5_coeffs(are, aim, ls):
    step = jnp.exp(ls)
    mag = jnp.exp(are * step)
    th = aim * step
    lre, lim = mag * jnp.cos(th), mag * jnp.sin(th)
    den = are * are + aim * aim
    nr = lre - 1.0
    fre = (nr * are + lim * aim) / den
    fim = (lim * are - nr * aim) / den
    return step, lre, lim, den, fre, fim


def _s5_prep(are, aim, ls, bre_bd, bim_bd):
    def body(are_ref, aim_ref, ls_ref, bre_ref, bim_ref, lre_ref, lim_ref, bbr_ref, bbi_ref):
        _, lre, lim, _, fre, fim = _s5_coeffs(are_ref[...], aim_ref[...], ls_ref[...])
        lre_ref[...] = lre
        lim_ref[...] = lim
        bre, bim = bre_ref[...], bim_ref[...]
        bbr_ref[...] = fre * bre - fim * bim
        bbi_ref[...] = fre * bim + fim * bre

    col = jax.ShapeDtypeStruct((S5_N, 1), f32)
    mat = jax.ShapeDtypeStruct((S5_N, GROUP_W), f32)
    return pl.pallas_call(body, name="s5_prep", out_shape=[col, col, mat, mat], in_specs=[VMEM] * 5, out_specs=[VMEM] * 4,
                          compiler_params=_params())(are, aim, ls, bre_bd, bim_bd)


def _s5_prep_bwd(are, aim, ls, bre_bd, bim_bd, dlre, dlim, dbbr, dbbi):
    def body(are_ref, aim_ref, ls_ref, bre_ref, bim_ref, dlre_ref, dlim_ref, dbbr_ref, dbbi_ref,
             dare_ref, daim_ref, dls_ref, dbre_ref, dbim_ref):
        are, aim = are_ref[...], aim_ref[...]
        step, lre, lim, den, fre, fim = _s5_coeffs(are, aim, ls_ref[...])
        r = lax.broadcasted_iota(jnp.int32, (S5_N, GROUP_W), 0) // 64
        c = lax.broadcasted_iota(jnp.int32, (S5_N, GROUP_W), 1) // 16
        mask = r == c
        gr = jnp.where(mask, dbbr_ref[...], 0.0)
        gi = jnp.where(mask, dbbi_ref[...], 0.0)
        bre, bim = bre_ref[...], bim_ref[...]
        dbre_ref[...] = fre * gr + fim * gi
        dbim_ref[...] = fre * gi - fim * gr
        dfre = jnp.sum(bre * gr + bim * gi, axis=1, keepdims=True)
        dfim = jnp.sum(bre * gi - bim * gr, axis=1, keepdims=True)
        ire, iim = are / den, aim / den
        tre = dlre_ref[...] + ire * dfre - iim * dfim
        tim = dlim_ref[...] + ire * dfim + iim * dfre
        dzre = lre * tre + lim * tim
        dzim = lre * tim - lim * tre
        qre = (fre * are + fim * aim) / den
        qim = (fim * are - fre * aim) / den
        dare_ref[...] = step * dzre - (qre * dfre + qim * dfim)
        daim_ref[...] = step * dzim - (qre * dfim - qim * dfre)
        dls = (are * dzre + aim * dzim) * step
        sel = (lax.broadcasted_iota(jnp.int32, (S5_N, LANES), 0) // 64 == lax.broadcasted_iota(jnp.int32, (S5_N, LANES), 1)).astype(f32)
        dls_ref[...] = lax.dot_general(sel, jnp.broadcast_to(dls, (S5_N, LANES)), (((0,), (0,)), ((), ())),
                                       preferred_element_type=f32, precision=lax.Precision.HIGHEST)

    col = jax.ShapeDtypeStruct((S5_N, 1), f32)
    mat = jax.ShapeDtypeStruct((S5_N, GROUP_W), f32)
    return pl.pallas_call(body, name="s5_prep_bwd", out_shape=[col, col, jax.ShapeDtypeStruct((LANES, LANES), f32), mat, mat],
                          in_specs=[VMEM] * 9, out_specs=[VMEM] * 5, compiler_params=_params(),
                          )(are, aim, ls, bre_bd, bim_bd, dlre, dlim, dbbr, dbbi)


def _cmul(ar, ai, br, bi):
    return ar * br - ai * bi, ar * bi + ai * br


def _s5_scan(re_ref, im_ref, carry_ref, mr, mi, n_groups, reverse):
    p1 = (mr, mi)
    p2 = _cmul(*p1, *p1)
    p3 = _cmul(*p2, *p1)
    p4 = _cmul(*p2, *p2)
    p5 = _cmul(*p4, *p1)
    p6 = _cmul(*p4, *p2)
    p7 = _cmul(*p4, *p3)
    p8 = _cmul(*p4, *p4)
    pows = [p1, p2, p3, p4, p5, p6, p7, p8]
    row = lax.broadcasted_iota(jnp.int32, (8, S5_N), 0)
    tr = jnp.zeros((8, S5_N), f32)
    ti = jnp.zeros((8, S5_N), f32)
    for i in range(8):
        p = pows[7 - i] if reverse else pows[i]
        tr = jnp.where(row == i, p[0], tr)
        ti = jnp.where(row == i, p[1], ti)
    steps = []
    for k, p in ((1, p1), (2, p2), (4, p4)):
        keep = (row + k < 8) if reverse else (row >= k)
        steps.append((8 - k if reverse else k, jnp.where(keep, p[0], 0.0), jnp.where(keep, p[1], 0.0)))
    edge = 0 if reverse else 7

    def step(j, carry):
        cr, ci = carry
        g = (n_groups - 1 - j) if reverse else j
        r0 = pl.multiple_of(g * 8, 8)
        xr = re_ref[pl.ds(r0, 8), :]
        xi = im_ref[pl.ds(r0, 8), :]
        for shift, br, bi in steps:
            sr = pltpu.roll(xr, shift, 0)
            si = pltpu.roll(xi, shift, 0)
            xr, xi = xr + br * sr - bi * si, xi + br * si + bi * sr
        xr, xi = xr + tr * cr - ti * ci, xi + tr * ci + ti * cr
        re_ref[pl.ds(r0, 8), :] = xr
        im_ref[pl.ds(r0, 8), :] = xi
        return (jnp.broadcast_to(xr[edge:edge + 1, :], (8, S5_N)), jnp.broadcast_to(xi[edge:edge + 1, :], (8, S5_N)))

    cr, ci = lax.fori_loop(0, n_groups, step, (carry_ref[0], carry_ref[1]))
    carry_ref[0] = cr
    carry_ref[1] = ci


def _s5_output(u, xr, xi, ctr, cti, d):
    return _dot_nt(_b(xr), _b(ctr)) - _dot_nt(_b(xi), _b(cti)) + d * u


def _s5_fwd(proj, bbr, bbi, ctr, cti, lre, lim, d, glu_w, glu_b, tb, xchg=None):
    t = proj.shape[0]

    def body(u_ref, bbr_ref, bbi_ref, ctr_ref, cti_ref, lr_ref, li_ref, d_ref, gw_ref, gb_ref, o_ref, xr_ref, xi_ref, carry):
        @pl.when(pl.program_id(0) == 0)
        def _():
            carry[...] = jnp.zeros_like(carry)

        u = u_ref[...]
        ub = _b(u)
        xr_ref[...] = _dot_nt(ub, _b(bbr_ref[...]))
        xi_ref[...] = _dot_nt(ub, _b(bbi_ref[...]))
        _s5_scan(xr_ref, xi_ref, carry, lr_ref[...], li_ref[...], tb // 8, reverse=False)
        y = _s5_output(u, xr_ref[...], xi_ref[...], ctr_ref[...], cti_ref[...], d_ref[...])
        gl = _gelu(y)
        o_ref[...] = _b(gl * jax.nn.sigmoid(_dot(_b(gl), _b(gw_ref[...])) + gb_ref[...]))

    state = pl.BlockSpec((tb, S5_N), lambda i: (i, 0))
    return _call(
        body, name="s5_fwd", grid=(t // tb,),
        out_shape=[jax.ShapeDtypeStruct((t, GROUP_W), bf16), jax.ShapeDtypeStruct((t, S5_N), f32), jax.ShapeDtypeStruct((t, S5_N), f32)],
        in_specs=[pl.BlockSpec((tb, GROUP_W), lambda i: (i, C_S5)), _full((S5_N, GROUP_W)), _full((S5_N, GROUP_W)),
                  _full((GROUP_W, S5_N)), _full((GROUP_W, S5_N)), _full((1, S5_N)), _full((1, S5_N)),
                  _full((1, GROUP_W)), _full((GROUP_W, GROUP_W)), _full((1, GROUP_W))],
        out_specs=[pl.BlockSpec((tb, GROUP_W), lambda i: (i, 0)), state, state],
        scratch_shapes=[pltpu.VMEM((2, 8, S5_N), f32)],
        semantics=("arbitrary",), xchg=xchg, args=(proj, bbr, bbi, ctr, cti, lre, lim, d, glu_w, glu_b))


def _s5_bwd(proj, dyd, xr_all, xi_all, bbr, bbi, ctr, cti, lre, lim, d, glu_w, glu_b, tb, xchg=None):
    t = proj.shape[0]
    nt = t // tb

    def body(u_ref, dy_ref, xr_ref, xi_ref, xrh_ref, xih_ref, bbr_ref, bbi_ref, ctr_ref, cti_ref, lr_ref, li_ref,
             d_ref, gw_ref, gb_ref,
             du_ref, dlr_ref, dli_ref, dbbr_ref, dbbi_ref, dctr_ref, dcti_ref, dd_ref, dgw_ref, dgb_ref,
             gr_ref, gi_ref, carry):
        i = pl.program_id(0)
        ti = nt - 1 - i

        @pl.when(i == 0)
        def _():
            carry[...] = jnp.zeros_like(carry)

        u = u_ref[...]
        ub = _b(u)
        xr, xi = xr_ref[...], xi_ref[...]
        ctr, cti = _b(ctr_ref[...]), _b(cti_ref[...])
        d = d_ref[...]
        gw = _b(gw_ref[...])
        y = _s5_output(u, xr, xi, ctr, cti, d)
        gl = _gelu(y)
        sg = jax.nn.sigmoid(_dot(_b(gl), gw) + gb_ref[...])
        dout = dy_ref[...]
        q = dout * gl * sg * (1.0 - sg)
        qb = _b(q)
        dgl = dout * sg + _dot_nt(qb, gw)
        _acc(dgw_ref, _dot_tn(_b(gl), qb))
        _acc(dgb_ref, _colsum(q))
        dyv = dgl * _dgelu(y)
        _acc(dd_ref, _colsum(dyv * u))
        dyb = _b(dyv)
        gr_ref[...] = _dot(dyb, ctr)
        gi_ref[...] = -_dot(dyb, cti)
        _acc(dctr_ref, _dot_tn(dyb, _b(xr)))
        _acc(dcti_ref, -_dot_tn(dyb, _b(xi)))
        _s5_scan(gr_ref, gi_ref, carry, lr_ref[...], -li_ref[...], tb // 8, reverse=True)
        gr, gi = gr_ref[...], gi_ref[...]
        xpr = jnp.concatenate([jnp.where(ti > 0, xrh_ref[...], 0.0), xr], axis=0)[7:7 + tb]
        xpi = jnp.concatenate([jnp.where(ti > 0, xih_ref[...], 0.0), xi], axis=0)[7:7 + tb]
        _acc(dlr_ref, _colsum(gr * xpr + gi * xpi))
        _acc(dli_ref, _colsum(gi * xpr - gr * xpi))
        grb, gib = _b(gr), _b(gi)
        _acc(dbbr_ref, _dot_tn(grb, ub))
        _acc(dbbi_ref, _dot_tn(gib, ub))
        du_ref[...] = _b(dyv * d + _dot(grb, _b(bbr_ref[...])) + _dot(gib, _b(bbi_ref[...])))

    state = pl.BlockSpec((tb, S5_N), lambda i: (nt - 1 - i, 0))
    prev = pl.BlockSpec((8, S5_N), lambda i: (jnp.maximum((nt - 1 - i) * (tb // 8) - 1, 0), 0))
    tile = pl.BlockSpec((tb, GROUP_W), lambda i: (nt - 1 - i, 0))
    return _call(
        body, name="s5_bwd", grid=(nt,),
        out_shape=[jax.ShapeDtypeStruct((t, GROUP_W), bf16), jax.ShapeDtypeStruct((1, S5_N), f32), jax.ShapeDtypeStruct((1, S5_N), f32),
                   jax.ShapeDtypeStruct((S5_N, GROUP_W), f32), jax.ShapeDtypeStruct((S5_N, GROUP_W), f32),
                   jax.ShapeDtypeStruct((GROUP_W, S5_N), f32), jax.ShapeDtypeStruct((GROUP_W, S5_N), f32),
                   jax.ShapeDtypeStruct((1, GROUP_W), f32), jax.ShapeDtypeStruct((GROUP_W, GROUP_W), f32),
                   jax.ShapeDtypeStruct((1, GROUP_W), f32)],
        in_specs=[pl.BlockSpec((tb, GROUP_W), lambda i: (nt - 1 - i, C_S5)), tile, state, state, prev, prev,
                  _full((S5_N, GROUP_W)), _full((S5_N, GROUP_W)), _full((GROUP_W, S5_N)), _full((GROUP_W, S5_N)),
                  _full((1, S5_N)), _full((1, S5_N)), _full((1, GROUP_W)), _full((GROUP_W, GROUP_W)), _full((1, GROUP_W))],
        out_specs=[tile, _full((1, S5_N)), _full((1, S5_N)), _full((S5_N, GROUP_W)), _full((S5_N, GROUP_W)),
                   _full((GROUP_W, S5_N)), _full((GROUP_W, S5_N)), _full((1, GROUP_W)), _full((GROUP_W, GROUP_W)), _full((1, GROUP_W))],
        scratch_shapes=[pltpu.VMEM((tb, S5_N), f32), pltpu.VMEM((tb, S5_N), f32), pltpu.VMEM((2, 8, S5_N), f32)],
        semantics=("arbitrary",), xchg=xchg,
        args=(proj, dyd, xr_all, xi_all, xr_all, xi_all, bbr, bbi, ctr, cti, lre, lim, d, glu_w, glu_b))


def _outproj_fwd(ys, h, bn_w, g1, w_out, tb):
    t = h.shape[0]

    def body(ya_ref, yb_ref, yc_ref, yd_ref, h_ref, bn_ref, g1_ref, w_ref, h1_ref, o_ref, gr_ref):
        bn = bn_ref[...]
        parts = []
        for g, r in enumerate((ya_ref, yb_ref, yc_ref, yd_ref)):
            n, _ = _rms(r[...].astype(f32))
            parts.append(n * bn[:, g * GROUP_W:(g + 1) * GROUP_W])
        groups = _b(jnp.concatenate(parts, axis=1))
        gr_ref[...] = groups
        o = _dot(groups, w_ref[...])
        o_ref[...] = _b(o)
        h1_ref[...] = h_ref[...] + g1_ref[...] * o

    grp = pl.BlockSpec((tb, GROUP_W), lambda i: (i, 0))
    row = pl.BlockSpec((tb, D_MODEL), lambda i: (i, 0))
    vec = _full((1, D_MODEL))
    return pl.pallas_call(
        body, name="outproj_fwd", grid=(t // tb,),
        out_shape=[jax.ShapeDtypeStruct((t, D_MODEL), f32), jax.ShapeDtypeStruct((t, D_MODEL), bf16),
                   jax.ShapeDtypeStruct((t, D_MODEL), bf16)],
        in_specs=[grp, grp, grp, grp, row, vec, vec, _full((D_MODEL, D_MODEL))],
        out_specs=[row, row, row],
        compiler_params=_params(("parallel",)),
    )(*ys, h, bn_w, g1, w_out)


def _outproj_bwd(dh1, o, ys, bn_w, g1, w_out, tb):
    t = dh1.shape[0]

    def body(dh_ref, o_ref, ya_ref, yb_ref, yc_ref, yd_ref, bn_ref, g1_ref, w_ref,
             da_ref, db_ref, dc_ref, dd_ref, do_ref, dg1_ref, dbn_ref):
        dh = dh_ref[...]
        _acc(dg1_ref, _colsum(dh * o_ref[...].astype(f32)))
        do = _b(dh * g1_ref[...])
        do_ref[...] = do
        dgroups = _dot_nt(do, w_ref[...])
        bn = bn_ref[...]
        dbn = []
        for g, (r, dr) in enumerate(((ya_ref, da_ref), (yb_ref, db_ref), (yc_ref, dc_ref), (yd_ref, dd_ref))):
            n, rr = _rms(r[...].astype(f32))
            dgr = dgroups[:, g * GROUP_W:(g + 1) * GROUP_W]
            dbn.append(_colsum(dgr * n))
            dr[...] = _rms_bwd(dgr * bn[:, g * GROUP_W:(g + 1) * GROUP_W], n, rr)
        _acc(dbn_ref, jnp.concatenate(dbn, axis=1))

    grp = pl.BlockSpec((tb, GROUP_W), lambda i: (i, 0))
    row = pl.BlockSpec((tb, D_MODEL), lambda i: (i, 0))
    vec = _full((1, D_MODEL))
    return pl.pallas_call(
        body, name="outproj_bwd", grid=(t // tb,),
        out_shape=[jax.ShapeDtypeStruct((t, GROUP_W), f32)] * 4 + [jax.ShapeDtypeStruct((t, D_MODEL), bf16),
                   jax.ShapeDtypeStruct((1, D_MODEL), f32), jax.ShapeDtypeStruct((1, D_MODEL), f32)],
        in_specs=[row, row, grp, grp, grp, grp, vec, vec, _full((D_MODEL, D_MODEL))],
        out_specs=[grp, grp, grp, grp, row, vec, vec],
        compiler_params=_params(("arbitrary",)),
    )(dh1, o, *ys, bn_w, g1, w_out)


def _mlp_fwd(h1, norm_w, sc, sh, g2, w1, w2, tb, xchg=None, head=None):
    t = h1.shape[0]
    nh = w1.shape[0] // MLP_SLABS
    n_head = 0 if head is None else 2

    def body(*refs):
        h_ref, nw_ref, sc_ref, sh_ref, g2_ref, w1_ref, w2_ref = refs[:7]
        head_refs, outs = refs[7:7 + n_head], refs[7 + n_head:]
        h2_ref, m_ref, v_ref, r_ref, acc = outs[0], outs[1], outs[2], outs[3], outs[-1]
        j = pl.program_id(1)

        @pl.when(j == 0)
        def _():
            n, _ = _rms(h_ref[...])
            v_ref[...] = _b(n * nw_ref[...] * (1.0 + sc_ref[...]) + sh_ref[...])

        v = v_ref[...]
        p = None
        for s in range(MLP_SLABS):
            ra = jnp.maximum(_dot(v, w1_ref[s]), 0.0)
            r = _b(ra * ra)
            r_ref[:, s * MLP_HB:(s + 1) * MLP_HB] = r
            q = _dot(r, w2_ref[s])
            p = q if p is None else p + q

        @pl.when(j == 0)
        def _():
            acc[...] = p

        @pl.when(j != 0)
        def _():
            acc[...] += p

        @pl.when(j == nh - 1)
        def _():
            m = acc[...]
            m_ref[...] = _b(m)
            h2 = h_ref[...] + g2_ref[...] * m
            if head is None:
                h2_ref[...] = h2
            else:
                tgt_ref, fw_ref = head_refs
                loss_ref, dfw_ref = outs[4], outs[5]
                n, r = _rms(h2)
                fw = fw_ref[...]
                err = n * fw - tgt_ref[...]
                part = 0.5 * jnp.sum(jnp.sum(err * err, axis=1, keepdims=True), axis=0, keepdims=True) / D_MODEL
                _acc(loss_ref, jnp.broadcast_to(part, (8, LANES)))
                dy = err / D_MODEL
                _acc(dfw_ref, _colsum(dy * n))
                h2_ref[...] = _rms_bwd(dy * fw, n, r)

    row = pl.BlockSpec((tb, D_MODEL), lambda i, j: (i, 0))
    hid = pl.BlockSpec((tb, MLP_SLABS * MLP_HB), lambda i, j: (i, j))
    vec = _full((1, D_MODEL))
    head_shapes = [] if head is None else [jax.ShapeDtypeStruct((8, LANES), f32), jax.ShapeDtypeStruct((1, D_MODEL), f32)]
    return _call(
        body, name="mlp_fwd", grid=(t // tb, nh),
        out_shape=[jax.ShapeDtypeStruct((t, D_MODEL), f32), jax.ShapeDtypeStruct((t, D_MODEL), bf16),
                   jax.ShapeDtypeStruct((t, D_MODEL), bf16), jax.ShapeDtypeStruct((t, N_DEV * MLP_HB), bf16)] + head_shapes,
        in_specs=[row, vec, vec, vec, vec, pl.BlockSpec((MLP_SLABS, D_MODEL, MLP_HB), lambda i, j: (j, 0, 0)),
                  pl.BlockSpec((MLP_SLABS, MLP_HB, D_MODEL), lambda i, j: (j, 0, 0))] + ([] if head is None else [row, vec]),
        out_specs=[row, row, row, hid] + ([] if head is None else [_full((8, LANES)), vec]),
        scratch_shapes=[pltpu.VMEM((tb, D_MODEL), f32)],
        semantics=("arbitrary", "arbitrary"), xchg=xchg, args=(h1, norm_w, sc, sh, g2, w1, w2) + (() if head is None else tuple(head)))


def _mlp_bwd(dh2, m, h1, r, norm_w, sc, sh, g2, w1, w2, tb, xchg=None):
    t = h1.shape[0]
    slabs = MLP_BWD_SLABS
    nh = w1.shape[0] // slabs

    def body(dh_ref, m_ref, h_ref, r_ref, nw_ref, sc_ref, sh_ref, g2_ref, w1_ref, w2_ref,
             dh1_ref, do_ref, da_ref, dg2_ref, dsh_ref, dsc_ref, dnw_ref, acc):
        j = pl.program_id(1)

        @pl.when(j == 0)
        def _():
            dh = dh_ref[...]
            _acc(dg2_ref, _colsum(dh * m_ref[...].astype(f32)))
            do_ref[...] = _b(dh * g2_ref[...])

        do = do_ref[...]
        p = None
        for s in range(slabs):
            cols = slice(s * MLP_HB, (s + 1) * MLP_HB)
            dr = _dot_nt(do, w2_ref[s])
            da = _b(dr * 2.0 * jnp.sqrt(r_ref[:, cols].astype(f32)))
            da_ref[:, cols] = da
            q = _dot_nt(da, w1_ref[s])
            p = q if p is None else p + q

        @pl.when(j == 0)
        def _():
            acc[...] = p

        @pl.when(j != 0)
        def _():
            acc[...] += p

        @pl.when(j == nh - 1)
        def _():
            dv = acc[...]
            n, r = _rms(h_ref[...])
            nw = nw_ref[...]
            gain = 1.0 + sc_ref[...]
            _acc(dsh_ref, _colsum(dv))
            _acc(dsc_ref, _colsum(dv * n * nw))
            _acc(dnw_ref, _colsum(dv * gain * n))
            dh1_ref[...] = dh_ref[...] + _rms_bwd(dv * nw * gain, n, r)

    row = pl.BlockSpec((tb, D_MODEL), lambda i, j: (i, 0))
    hid = pl.BlockSpec((tb, slabs * MLP_HB), lambda i, j: (i, j))
    vec = _full((1, D_MODEL))
    once = dict(pipeline_mode=pl.Buffered(1)) if nh == 1 else {}
    return _call(
        body, name="mlp_bwd", grid=(t // tb, nh),
        out_shape=[jax.ShapeDtypeStruct((t, D_MODEL), f32), jax.ShapeDtypeStruct((t, D_MODEL), bf16),
                   jax.ShapeDtypeStruct((t, N_DEV * MLP_HB), bf16)] + [jax.ShapeDtypeStruct((1, D_MODEL), f32)] * 4,
        in_specs=[row, row, row, hid, vec, vec, vec, vec,
                  pl.BlockSpec((slabs, D_MODEL, MLP_HB), lambda i, j: (j, 0, 0), **once),
                  pl.BlockSpec((slabs, MLP_HB, D_MODEL), lambda i, j: (j, 0, 0), **once)],
        out_specs=[row, row, hid, vec, vec, vec, vec],
        scratch_shapes=[pltpu.VMEM((tb, D_MODEL), f32)],
        semantics=("arbitrary", "arbitrary"), xchg=xchg, args=(dh2, m, h1, r, norm_w, sc, sh, g2, w1, w2))


def _adam_math(w, g, m, v):
    m2 = ADAM_B1 * m + (1.0 - ADAM_B1) * g
    v2 = ADAM_B2 * v + (1.0 - ADAM_B2) * (g * g)
    mh = m2 / (1.0 - ADAM_B1 ** ADAM_STEP)
    vh = v2 / (1.0 - ADAM_B2 ** ADAM_STEP)
    return -ADAM_LR * (mh / (jnp.sqrt(vh) + ADAM_EPS) + ADAM_WD * w), m2, v2


def _adamw_small(ws, gs, ms, vs):
    n = len(ws)
    shapes = [w.shape for w in ws]
    as2d = [(1,) + s if len(s) == 1 else s for s in shapes]
    flat = [x.reshape(s) for group in (ws, gs, ms, vs) for x, s in zip(group, as2d)]

    def body(*refs):
        w_refs, g_refs, m_refs, v_refs, outs = refs[:n], refs[n:2 * n], refs[2 * n:3 * n], refs[3 * n:4 * n], refs[4 * n:]
        for i in range(n):
            d, m2, v2 = _adam_math(w_refs[i][...], g_refs[i][...], m_refs[i][...], v_refs[i][...])
            outs[3 * i][...] = d
            outs[3 * i + 1][...] = m2
            outs[3 * i + 2][...] = v2

    res = pl.pallas_call(body, name="adamw_small", out_shape=[jax.ShapeDtypeStruct(s, f32) for s in as2d for _ in range(3)],
                         in_specs=[VMEM] * (4 * n), out_specs=[VMEM] * (3 * n), compiler_params=_params())(*flat)
    return [r.reshape(shapes[i // 3]) for i, r in enumerate(res)]


def _sum_adamw_layers(parts0, parts1, w, m, v, name, rb):
    n_src, r, c = parts0.shape
    nb = r // rb

    def body(p0_ref, p1_ref, w_ref, m_ref, v_ref, g_ref, d_ref, m2_ref, v2_ref):
        def update(p_ref):
            g = p_ref[0].astype(f32)
            for s in range(1, n_src):
                g = g + p_ref[s].astype(f32)
            g_ref[0] = g
            d, m2, v2 = _adam_math(w_ref[0], g, m_ref[0], v_ref[0])
            d_ref[0] = d
            m2_ref[0] = m2
            v2_ref[0] = v2

        @pl.when(pl.program_id(0) == 0)
        def _():
            update(p0_ref)

        @pl.when(pl.program_id(0) == 1)
        def _():
            update(p1_ref)

    blk = pl.BlockSpec((1, rb, c), lambda l, i: (l, i, 0))
    return pl.pallas_call(
        body, name=name, grid=(2, nb),
        out_shape=[jax.ShapeDtypeStruct((2, r, c), f32)] * 4,
        in_specs=[pl.BlockSpec((n_src, rb, c), lambda l, i: (0, jnp.where(l == 0, i, nb - 1), 0)),
                  pl.BlockSpec((n_src, rb, c), lambda l, i: (0, jnp.where(l == 1, i, 0), 0)), blk, blk, blk],
        out_specs=[blk] * 4,
        compiler_params=_params(("arbitrary", "arbitrary")),
    )(parts0, parts1, w, m, v)


def _reorder_in(w):
    pad = jnp.zeros(w.shape[:-1] + (P_IN - 2308,), w.dtype)
    return jnp.concatenate([w[..., :2048], w[..., 2052:2308], w[..., 2048:2052], pad], axis=-1)


def _unreorder_in(w):
    return jnp.concatenate([w[..., :2048], w[..., 2304:2308], w[..., 2048:2304]], axis=-1)


def _block_diag(w2d, n_blocks):
    rows, cols = w2d.shape
    tiled = jnp.tile(w2d, (1, n_blocks))
    rb = lax.broadcasted_iota(jnp.int32, tiled.shape, 0) // (rows // n_blocks)
    cb = lax.broadcasted_iota(jnp.int32, tiled.shape, 1) // cols
    return jnp.where(rb == cb, tiled, jnp.zeros_like(tiled))


def _block_diag_extract(w_bd, n_blocks):
    rows, wide = w_bd.shape
    r, c = rows // n_blocks, wide // n_blocks
    w4 = w_bd.reshape(n_blocks, r, n_blocks, c)
    idx = jnp.arange(n_blocks)
    return w4[idx, :, idx, :]


def _rows_of(shape):
    n = 1
    for d in shape:
        n *= d
    return -(-n // (8 * LANES)) * 8, n


def _flat_pack(arrs, row_multiple=8):
    blocks = []
    for a in arrs:
        rows, n = _rows_of(a.shape)
        blocks.append(jnp.pad(a.reshape(-1), (0, rows * LANES - n)).reshape(rows, LANES))
    total = sum(b.shape[0] for b in blocks)
    pad = -total % row_multiple
    if pad:
        blocks.append(jnp.zeros((pad, LANES), blocks[0].dtype))
    return jnp.concatenate(blocks, axis=0)


def _flat_unpack(packed, shapes):
    out, off = [], 0
    for s in shapes:
        rows, n = _rows_of(s)
        out.append(packed[off:off + rows].reshape(-1)[:n].reshape(s))
        off += rows
    return out


_W_NAMES = ['norm_mix_w', 'norm_mlp_w', 'ada_w', 'ada_b', 'w_in', 'pool_w', 'pool_scale', 'sconv_w', 'ssd_conv_w',
            'ssd_conv_b', 'ssd_dt_bias', 'ssd_a_log', 'ssd_d', 's5_a_re', 's5_a_im', 's5_log_step', 's5_b_re', 's5_b_im',
            's5_c_re', 's5_c_im', 's5_d', 's5_glu_w', 's5_glu_b', 'branch_norm_w', 'w_out', 'mlp_w1', 'mlp_w2',
            'final_norm_w']
_BIG = ('ada_w', 'w_in', 'w_out', 'mlp_w1', 'mlp_w2')
_SMALL = [n for n in _W_NAMES if n not in _BIG]
_SHARDED_SMALL = {'sconv_w': (2, 32), 'ssd_conv_w': (2, 96), 's5_glu_w': (1, 32)}


def _gather(*blocks):
    return _ChipGather(blocks)


def _scatter(*parts):
    return _Scatter(parts)


def _layer_forward(l, h, p, w, sh_b, tb, head=None):
    first = l == 0
    (proj, u_b), got = _inproj_fwd(h, p['norm_mix_w'][l], p['sc1'][l], p['sh1'][l], w['w_in', l], tb,
                                   xchg=_gather(sh_b[1][0]) if first else None)
    if first:
        w['w_out', 0] = got[0].reshape(D_MODEL, D_MODEL)
    ya = _pool_fwd(proj, p['pool_bd'][l], p['pool_scale'][l], tb)
    yb = _sconv_fwd(proj, p['sconv_w8'][l], tb)
    (yc, yc_pre, states), got = _ssd_fwd(proj, p['ssd_conv_w8'][l], p['ssd_conv_b'][l], p['ssd_dt_bias'][l], p['ssd_a_log'][l],
                                         p['ssd_d_cols'][l], tb, xchg=_gather(sh_b[2][0]) if first else None)
    if first:
        w['w1', 0] = got[0]
    (yd, xr, xi), got = _s5_fwd(proj, p['bbr'][l], p['bbi'][l], p['ctr'][l], p['cti'][l], p['lre'][l], p['lim'][l],
                                p['s5_d'][l], p['glu_w'][l], p['glu_b'][l], tb, xchg=_gather(sh_b[3][0]) if first else None)
    if first:
        w['w2', 0] = got[0]
    ys = (ya, yb, yc, yd)
    h1, o, groups_b = _outproj_fwd(ys, h, p['branch_norm_w'][l], p['g1'][l], w['w_out', l], tb)
    (h2, m, v_b, r_b, *head_out), got = _mlp_fwd(
        h1, p['norm_mlp_w'][l], p['sc2'][l], p['sh2'][l], p['g2'][l], w['w1', l], w['w2', l],
        min(MLP_TB if head is None else TB_BWD, h.shape[0]),
        xchg=_gather(*[sh_b[k][1] for k in range(4)]) if first else None, head=head)
    if first:
        w['w_in', 1] = got[0].reshape(D_MODEL, P_IN)
        w['w_out', 1] = got[1].reshape(D_MODEL, D_MODEL)
        w['w1', 1], w['w2', 1] = got[2], got[3]
    saved = dict(h=h, proj=proj, u_b=u_b, ys=ys, yc_pre=yc_pre, states=states, xr=xr, xi=xi, h1=h1, o=o,
                 groups_b=groups_b, m=m, v_b=v_b, r_b=r_b)
    return (h2, *head_out), saved


def _layer_backward(l, dh2, s, p, w, pending, recv, tb):
    def carry(names):
        names = [n for n in names if n in pending]
        return names, (_scatter(*[pending.pop(n) for n in names]) if names else None)

    def landed(names, got):
        for n, g in zip(names, got):
            recv[n] = g

    names, xchg = carry([('w_out', 1)])
    (dh1, do2_b, da_b, dg2, dsh2, dsc2, dnw_mlp), got = _mlp_bwd(dh2, s['m'], s['h1'], s['r_b'], p['norm_mlp_w'][l], p['sc2'][l],
                                                                p['sh2'][l], p['g2'][l], w['w1', l], w['w2', l], min(TB_BWD, tb),
                                                                xchg=xchg)
    landed(names, got)
    pending['mlp_w2', l] = _wgrad(s['r_b'], do2_b, 1, "wgrad_w2", tm=1024, tk=4096).reshape(N_DEV, MLP_HB, D_MODEL)
    pending['mlp_w1', l] = _wgrad(s['v_b'], da_b, N_DEV, "wgrad_w1", tm=1024, tk=4096)
    dya, dyb, dyc, dyd, do1_b, dg1, dbn = _outproj_bwd(dh1, s['o'], s['ys'], p['branch_norm_w'][l], p['g1'][l], w['w_out', l], tb)
    pending['w_out', l] = _wgrad(s['groups_b'], do1_b, 1, "wgrad_wout", tm=1024, tk=1024).reshape(N_DEV, D_MODEL // N_DEV, D_MODEL)
    proj = s['proj']
    dv, dpool_bd, dpool_scale = _pool_bwd(proj, dya, p['pool_bd'][l], p['pool_scale'][l], tb)
    dgb, dgc, dhh, dsconv = _sconv_bwd(proj, dyb, p['sconv_w8'][l], tb)
    names, xchg = carry([('mlp_w1', l)] + ([('w_out', 0)] if l == 0 else []))
    (dz, dxs, dbm, dcm, ddt, dconv_w, dconv_b, ddtb, dalog, ddskip), got = _ssd_bwd(
        proj, dyc, s['yc_pre'], s['states'], p['ssd_conv_w8'][l], p['ssd_conv_b'][l], p['ssd_dt_bias'][l], p['ssd_a_log'][l],
        p['ssd_d_cols'][l], min(TB_BWD, tb), xchg=xchg)
    landed(names, got)
    names, xchg = carry([('mlp_w2', l)])
    (du5, dlr, dli, dbbr, dbbi, dctr, dcti, dd5, dgw, dgb5), got = _s5_bwd(
        proj, dyd, s['xr'], s['xi'], p['bbr'][l], p['bbi'][l], p['ctr'][l], p['cti'][l], p['lre'][l], p['lim'][l],
        p['s5_d'][l], p['glu_w'][l], p['glu_b'][l], min(TB_BWD, tb), xchg=xchg)
    landed(names, got)
    dare, daim, dls, dbre_bd, dbim_bd = _s5_prep_bwd(p['are_c'][l], p['aim_c'][l], p['ls_c'][l], p['bre_bd'][l], p['bim_bd'][l],
                                                     dlr.reshape(S5_N, 1), dli.reshape(S5_N, 1), dbbr, dbbi)
    dparts = (dv, dgb, dgc, dhh, dz, dxs, dbm, dcm, du5, ddt)
    pending['w_in', l] = _wgrad_parts(s['u_b'], dparts, "wgrad_win", tm=1024, tk=1024).reshape(N_DEV, D_MODEL // N_DEV, P_IN)
    names, xchg = carry([('w_in', l)])
    (dh, dsh1, dsc1, dnw_mix), got = _inproj_bwd(dparts, dh1, s['h'], p['norm_mix_w'][l], p['sc1'][l], p['sh1'][l], w['w_in', l],
                                                 tb, xchg=xchg)
    landed(names, got)
    small = {
        'norm_mix_w': dnw_mix.reshape(D_MODEL), 'norm_mlp_w': dnw_mlp.reshape(D_MODEL),
        'ada_b': jnp.concatenate([dsh1, dsc1, dg1, dsh2, dsc2, dg2], axis=1).reshape(6 * D_MODEL),
        'pool_w': _block_diag_extract(dpool_bd, 4), 'pool_scale': dpool_scale.reshape(GROUP_W),
        'sconv_w': dsconv[0:3], 'ssd_conv_w': dconv_w[0:4], 'ssd_conv_b': dconv_b.reshape(768),
        'ssd_dt_bias': ddtb[0, 0:4], 'ssd_a_log': dalog[0, 0:4], 'ssd_d': ddskip[0, 0:4],
        's5_a_re': dare.reshape(16, 64), 's5_a_im': daim.reshape(16, 64), 's5_log_step': dls[0:16, 0],
        's5_b_re': _block_diag_extract(dbre_bd, 16), 's5_b_im': _block_diag_extract(dbim_bd, 16),
        's5_c_re': _block_diag_extract(dctr, 16), 's5_c_im': _block_diag_extract(dcti, 16),
        's5_d': dd5.reshape(GROUP_W), 's5_glu_w': dgw, 's5_glu_b': dgb5.reshape(GROUP_W),
        'branch_norm_w': dbn.reshape(D_MODEL),
    }
    return dh, small


def _prepare_params(a, me, w_in0_shard):
    pack_shapes = [(1, D_MODEL), (2, 3, 32), (2, 4, 96), (2, 32, GROUP_W)]
    packed = _flat_pack([a['c'], a['sconv_w'], a['ssd_conv_w'], a['s5_glu_w']])
    w_in0, gathered = _exchange_alone(_gather(w_in0_shard, packed), "gather_first")
    pieces = [_flat_unpack(gathered[d], pack_shapes) for d in range(N_DEV)]
    c_all = jnp.concatenate([pc[0] for pc in pieces], axis=0)
    sconv_full = jnp.concatenate([pc[1] for pc in pieces], axis=2)
    ssd_conv_full = jnp.concatenate([pc[2] for pc in pieces], axis=2)
    glu_full = jnp.concatenate([pc[3] for pc in pieces], axis=1)

    ada_b_cols = lax.dynamic_slice_in_dim(a['ada_b'], me * 768, 768, axis=1).reshape(2, 1, 768)
    cond, modrows = _ada_forward(c_all, a['ada_w'], ada_b_cols)
    mod_recv = _all_to_all_rows(modrows.transpose(1, 0, 2), "exchange_mod")
    mod = mod_recv.transpose(1, 0, 2).reshape(2, 6 * D_MODEL)
    p = {'cond': cond}
    for k, name in enumerate(('sh1', 'sc1', 'g1', 'sh2', 'sc2', 'g2')):
        p[name] = mod[:, k * D_MODEL:(k + 1) * D_MODEL].reshape(2, 1, D_MODEL)

    for name in ('norm_mix_w', 'norm_mlp_w', 'branch_norm_w'):
        p[name] = a[name].reshape(2, 1, D_MODEL)
    p['pool_bd'] = jnp.stack([_block_diag(a['pool_w'][l].reshape(GROUP_W, 64), 4) for l in range(2)])
    p['pool_scale'] = a['pool_scale'].reshape(2, 1, GROUP_W)
    p['sconv_w8'] = jnp.pad(sconv_full, ((0, 0), (0, 5), (0, 0)))
    p['ssd_conv_w8'] = jnp.pad(ssd_conv_full, ((0, 0), (0, 4), (0, 0)))
    p['ssd_conv_b'] = a['ssd_conv_b'].reshape(2, 1, 768)
    p['ssd_dt_bias'] = jnp.pad(a['ssd_dt_bias'], ((0, 0), (0, LANES - 4))).reshape(2, 1, LANES)
    p['ssd_a_log'] = jnp.pad(a['ssd_a_log'], ((0, 0), (0, LANES - 4))).reshape(2, 1, LANES)
    p['ssd_d_cols'] = jnp.repeat(a['ssd_d'], SSD_P, axis=1).reshape(2, 1, GROUP_W)
    p['are_c'] = a['s5_a_re'].reshape(2, S5_N, 1)
    p['aim_c'] = a['s5_a_im'].reshape(2, S5_N, 1)
    p['ls_c'] = jnp.repeat(a['s5_log_step'], 64, axis=1).reshape(2, S5_N, 1)
    p['bre_bd'] = jnp.stack([_block_diag(a['s5_b_re'][l].reshape(S5_N, 16), 16) for l in range(2)])
    p['bim_bd'] = jnp.stack([_block_diag(a['s5_b_im'][l].reshape(S5_N, 16), 16) for l in range(2)])
    p['ctr'] = jnp.stack([_block_diag(a['s5_c_re'][l].reshape(GROUP_W, 64), 16) for l in range(2)])
    p['cti'] = jnp.stack([_block_diag(a['s5_c_im'][l].reshape(GROUP_W, 64), 16) for l in range(2)])
    p['s5_d'] = a['s5_d'].reshape(2, 1, GROUP_W)
    p['glu_w'] = glu_full
    p['glu_b'] = a['s5_glu_b'].reshape(2, 1, GROUP_W)
    lre, lim, bbr, bbi = [], [], [], []
    for l in range(2):
        r = _s5_prep(p['are_c'][l], p['aim_c'][l], p['ls_c'][l], p['bre_bd'][l], p['bim_bd'][l])
        lre.append(r[0].reshape(1, S5_N))
        lim.append(r[1].reshape(1, S5_N))
        bbr.append(r[2])
        bbi.append(r[3])
    p['lre'], p['lim'], p['bbr'], p['bbi'] = lre, lim, bbr, bbi
    return p, w_in0


def kernel(x, c, norm_mix_w, norm_mlp_w, ada_w, ada_b, w_in, pool_w, pool_scale, sconv_w, ssd_conv_w, ssd_conv_b, ssd_dt_bias, ssd_a_log, ssd_d, s5_a_re, s5_a_im, s5_log_step, s5_b_re, s5_b_im, s5_c_re, s5_c_im, s5_d, s5_glu_w, s5_glu_b, branch_norm_w, w_out, mlp_w1, mlp_w2, final_norm_w, loss_target, m_norm_mix_w, m_norm_mlp_w, m_ada_w, m_ada_b, m_w_in, m_pool_w, m_pool_scale, m_sconv_w, m_ssd_conv_w, m_ssd_conv_b, m_ssd_dt_bias, m_ssd_a_log, m_ssd_d, m_s5_a_re, m_s5_a_im, m_s5_log_step, m_s5_b_re, m_s5_b_im, m_s5_c_re, m_s5_c_im, m_s5_d, m_s5_glu_w, m_s5_glu_b, m_branch_norm_w, m_w_out, m_mlp_w1, m_mlp_w2, m_final_norm_w, v_norm_mix_w, v_norm_mlp_w, v_ada_w, v_ada_b, v_w_in, v_pool_w, v_pool_scale, v_sconv_w, v_ssd_conv_w, v_ssd_conv_b, v_ssd_dt_bias, v_ssd_a_log, v_ssd_d, v_s5_a_re, v_s5_a_im, v_s5_log_step, v_s5_b_re, v_s5_b_im, v_s5_c_re, v_s5_c_im, v_s5_d, v_s5_glu_w, v_s5_glu_b, v_branch_norm_w, v_w_out, v_mlp_w1, v_mlp_w2, v_final_norm_w):
    a = dict(locals())
    t = x.shape[1]
    tb = min(TB, t)
    me = _my_index()
    sh_b = _cast_shards([_reorder_in(w_in), w_out, mlp_w1, mlp_w2])
    p, w_in0 = _prepare_params(a, me, sh_b[0][0])
    w = {('w_in', 0): w_in0.reshape(D_MODEL, P_IN)}

    h = x.reshape(t, D_MODEL)
    saved = []
    (h,), s = _layer_forward(0, h, p, w, sh_b, tb)
    saved.append(s)
    (dh, loss_blk, dfinal), s = _layer_forward(1, h, p, w, sh_b, tb,
                                               head=(loss_target.reshape(t, D_MODEL), final_norm_w.reshape(1, D_MODEL)))
    saved.append(s)

    pending, recv, small_parts = {}, {}, [None, None]
    for l in (1, 0):
        dh, small_parts[l] = _layer_backward(l, dh, saved[l], p, w, pending, recv, tb)
    grad_x = dh.reshape(1, t, D_MODEL)

    grads, deltas, new_m, new_v = {}, {}, {}, {}

    wmv_in = [_reorder_in(a[n]) for n in ('w_in', 'm_w_in', 'v_w_in')]
    outs = _sum_adamw_layers(recv['w_in', 0], recv['w_in', 1], *wmv_in, "adamw_w_in", 128)
    grads['w_in'], deltas['w_in'], new_m['w_in'], new_v['w_in'] = [_unreorder_in(o) for o in outs]
    for name, rb in (('w_out', 128), ('mlp_w1', 256), ('mlp_w2', 256)):
        grads[name], deltas[name], new_m[name], new_v[name] = _sum_adamw_layers(
            recv[name, 0], recv[name, 1], a[name], a['m_' + name], a['v_' + name], "adamw_" + name, rb)

    dmod = jnp.stack([small_parts[0]['ada_b'], small_parts[1]['ada_b']])
    dmod_recv = _all_to_all_rows(dmod.reshape(2, N_DEV, 768).transpose(1, 0, 2), "exchange_dmod")
    g_ada = _ada_backward(p['cond'], dmod_recv.transpose(1, 0, 2))
    grads['ada_w'], deltas['ada_w'], new_m['ada_w'], new_v['ada_w'] = _sum_adamw_layers(
        g_ada[0:1], g_ada[1:2], ada_w, m_ada_w, v_ada_w, "adamw_ada_w", 256)

    layered = [n for n in _SMALL if n != 'final_norm_w']
    full = [jnp.stack([small_parts[0][n], small_parts[1][n]]) for n in layered] + [dfinal.reshape(D_MODEL)]
    full.append(loss_blk[0:1, 0:1])
    full_shapes = [f.shape for f in full]
    summed = _flat_unpack(_allreduce_rows(_flat_pack(full, row_multiple=64)), full_shapes)
    loss = summed[-1].reshape(())
    local = []
    for n, g in zip(_SMALL, summed):
        if n in _SHARDED_SMALL:
            axis, size = _SHARDED_SMALL[n]
            g = lax.dynamic_slice_in_dim(g, me * size, size, axis=axis)
        local.append(g.reshape(a[n].shape))
    outs = _adamw_small([a[n] for n in _SMALL], local, [a['m_' + n] for n in _SMALL], [a['v_' + n] for n in _SMALL])
    for i, n in enumerate(_SMALL):
        grads[n], deltas[n], new_m[n], new_v[n] = local[i], outs[3 * i], outs[3 * i + 1], outs[3 * i + 2]

    return (loss, grad_x, *[grads[n] for n in _W_NAMES], *[deltas[n] for n in _W_NAMES],
            *[new_m[n] for n in _W_NAMES], *[new_v[n] for n in _W_NAMES])
```

```python
import functools

import jax
import jax.numpy as jnp
from jax import lax
from jax.experimental import pallas as pl
from jax.experimental.pallas import tpu as pltpu

f32 = jnp.float32
bf16 = jnp.bfloat16

N_DEV = 8
D_MODEL = 1024
GROUP_W = 256
P_IN = 2432
DT_COL = 2304
SSD_CHUNK = 128
SSD_HEADS = 4
SSD_P = 64
S5_N = 1024
MLP_HB = 512
TB = 1024
TB_BWD = 512
MLP_TB = 1024
HEAD_ROWS = 256
MLP_SLABS = 2
MLP_BWD_SLABS = 8
EPS = 1e-6
LANES = 128
VMEM_LIMIT = 56 * 1024 * 1024
ADAM_LR, ADAM_B1, ADAM_B2, ADAM_EPS, ADAM_WD, ADAM_STEP = 0.001, 0.9, 0.999, 1e-08, 0.01, 10
POOL_WINDOWS = (2, 4, 8, 16)

C_POOL, C_GB, C_GC, C_HH, C_Z, C_XS, C_BM, C_CM, C_S5 = range(9)
C_DT128 = DT_COL // LANES

MESH = pl.DeviceIdType.MESH
ANY = pl.BlockSpec(memory_space=pl.ANY)
VMEM = pl.BlockSpec(memory_space=pltpu.VMEM)


def _dot(a, b):
    return jnp.dot(a, b, preferred_element_type=f32)


def _dot_nt(a, b):
    return lax.dot_general(a, b, (((1,), (1,)), ((), ())), preferred_element_type=f32)


def _dot_tn(a, b):
    return lax.dot_general(a, b, (((0,), (0,)), ((), ())), preferred_element_type=f32)


def _dot_exact(a, b):
    return jnp.dot(a, b, preferred_element_type=f32, precision=lax.Precision.HIGHEST)


def _b(x):
    return x.astype(bf16)


def _silu(x):
    return x * jax.nn.sigmoid(x)


def _dsilu(x):
    s = jax.nn.sigmoid(x)
    return s * (1.0 + x * (1.0 - s))


def _softplus(x):
    return jnp.maximum(x, 0.0) + jnp.log1p(jnp.exp(-jnp.abs(x)))


_GELU_K = 0.7978845608028654
_GELU_C = 0.044715


def _gelu(x):
    return 0.5 * x * (1.0 + jnp.tanh(_GELU_K * (x + _GELU_C * x * x * x)))


def _dgelu(x):
    th = jnp.tanh(_GELU_K * (x + _GELU_C * x * x * x))
    return 0.5 * (1.0 + th) + 0.5 * x * (1.0 - th * th) * _GELU_K * (1.0 + 3.0 * _GELU_C * x * x)


def _rms(h):
    r = lax.rsqrt(jnp.mean(h * h, axis=-1, keepdims=True) + EPS)
    return h * r, r


def _rms_bwd(dn, n, r):
    return r * (dn - n * jnp.mean(dn * n, axis=-1, keepdims=True))


def _colsum(x):
    return jnp.sum(x, axis=0, keepdims=True)


def _params(sem=None):
    return pltpu.CompilerParams(dimension_semantics=sem, vmem_limit_bytes=VMEM_LIMIT)


def _full(shape):
    return pl.BlockSpec(shape, lambda *_: (0,) * len(shape))


def _acc(ref, val):
    @pl.when(pl.program_id(0) == 0)
    def _():
        ref[...] = val

    @pl.when(pl.program_id(0) != 0)
    def _():
        ref[...] += val


def _me():
    return lax.axis_index("x"), lax.axis_index("y"), lax.axis_index("c")


def _my_index():
    x, y, c = _me()
    return 4 * x + 2 * y + c


def _coords(p):
    return (p // 4, (p // 2) % 2, p % 2)


class _Scatter:
    def __init__(self, srcs):
        self.srcs = list(srcs)
        self.n = len(self.srcs)
        self.out_shape = [jax.ShapeDtypeStruct(s.shape, s.dtype) for s in self.srcs]
        self.scratch = [pltpu.SemaphoreType.DMA((self.n, N_DEV)), pltpu.SemaphoreType.DMA((self.n, N_DEV)),
                        pltpu.SemaphoreType.DMA((self.n,))]

    def _remote(self, xin, xout, sems, t, k, me, to):
        return pltpu.make_async_remote_copy(
            src_ref=xin[t].at[to], dst_ref=xout[t].at[me], send_sem=sems[0].at[t, k], recv_sem=sems[1].at[t, k],
            device_id=_coords(to), device_id_type=MESH)

    def start(self, xin, xout, sems):
        me = _my_index()
        for t in range(self.n):
            pltpu.make_async_copy(xin[t].at[me], xout[t].at[me], sems[2].at[t]).start()
            for k in range(1, N_DEV):
                self._remote(xin, xout, sems, t, k, me, (me + k) % N_DEV).start()

    def forward(self, xin, xout, sems):
        pass

    def wait(self, xin, xout, sems):
        me = _my_index()
        for t in range(self.n):
            for k in range(1, N_DEV):
                src = (me + N_DEV - k) % N_DEV
                pltpu.make_async_remote_copy(
                    src_ref=xin[t].at[src], dst_ref=xout[t].at[src], send_sem=sems[0].at[t, k],
                    recv_sem=sems[1].at[t, k], device_id=_coords(src), device_id_type=MESH).wait_recv()
        for t in range(self.n):
            for k in range(1, N_DEV):
                self._remote(xin, xout, sems, t, k, me, (me + k) % N_DEV).wait_send()
            pltpu.make_async_copy(xin[t].at[me], xout[t].at[me], sems[2].at[t]).wait()


class _ChipGather:
    def __init__(self, srcs):
        self.srcs = list(srcs)
        self.n = len(self.srcs)
        self.out_shape = [jax.ShapeDtypeStruct((N_DEV,) + s.shape, s.dtype) for s in self.srcs]
        self.scratch = [pltpu.SemaphoreType.DMA((self.n, 7)), pltpu.SemaphoreType.DMA((self.n, 7)),
                        pltpu.SemaphoreType.DMA((self.n,))]

    @staticmethod
    def _places():
        x, y, c = _me()
        chips = [(1 - x, y), (x, 1 - y), (1 - x, 1 - y)]
        return (x, y, c), (x, y, 1 - c), chips

    @staticmethod
    def _slab(ref, dev):
        return ref.at[4 * dev[0] + 2 * dev[1] + dev[2]]

    def _copy(self, xin, xout, sems, t, k, block, to, src=None):
        return pltpu.make_async_remote_copy(
            src_ref=self._slab(xout[t], block) if src is None else src, dst_ref=self._slab(xout[t], block),
            send_sem=sems[0].at[t, k], recv_sem=sems[1].at[t, k], device_id=to, device_id_type=MESH)

    def start(self, xin, xout, sems):
        me, sibling, chips = self._places()
        for t in range(self.n):
            pltpu.make_async_copy(xin[t], self._slab(xout[t], me), sems[2].at[t]).start()
            self._copy(xin, xout, sems, t, 0, me, sibling, src=xin[t]).start()
            for j, chip in enumerate(chips):
                self._copy(xin, xout, sems, t, 1 + j, me, (*chip, me[2]), src=xin[t]).start()

    def forward(self, xin, xout, sems):
        me, sibling, chips = self._places()
        for t in range(self.n):
            for j, chip in enumerate(chips):
                self._copy(xin, xout, sems, t, 1 + j, (*chip, me[2]), me).wait_recv()
                self._copy(xin, xout, sems, t, 4 + j, (*chip, me[2]), sibling).start()

    def wait(self, xin, xout, sems):
        me, sibling, chips = self._places()
        for t in range(self.n):
            self._copy(xin, xout, sems, t, 0, sibling, me).wait_recv()
            for j, chip in enumerate(chips):
                self._copy(xin, xout, sems, t, 4 + j, (*chip, 1 - me[2]), me).wait_recv()
        for t in range(self.n):
            self._copy(xin, xout, sems, t, 0, me, sibling, src=xin[t]).wait_send()
            for j, chip in enumerate(chips):
                self._copy(xin, xout, sems, t, 1 + j, me, (*chip, me[2]), src=xin[t]).wait_send()
                self._copy(xin, xout, sems, t, 4 + j, (*chip, me[2]), sibling).wait_send()
            pltpu.make_async_copy(xin[t], self._slab(xout[t], me), sems[2].at[t]).wait()


def _call(body, *, name, grid, in_specs, out_specs, out_shape, args, semantics, scratch_shapes=(), xchg=None):
    if xchg is None:
        outs = pl.pallas_call(body, name=name, grid=grid, in_specs=in_specs, out_specs=out_specs, out_shape=out_shape,
                              scratch_shapes=list(scratch_shapes), compiler_params=_params(semantics))(*args)
        return outs, ()
    n_in, n_out, n_scr, n = len(in_specs), len(out_specs), len(scratch_shapes), xchg.n

    def carried(*refs):
        ins, xin = refs[:n_in], refs[n_in:n_in + n]
        outs, xout = refs[n_in + n:n_in + n + n_out], refs[n_in + n + n_out:n_in + 2 * n + n_out]
        scr, sems = refs[n_in + 2 * n + n_out:n_in + 2 * n + n_out + n_scr], refs[n_in + 2 * n + n_out + n_scr:]
        step = pl.program_id(0)
        for d in range(1, len(grid)):
            step = step * grid[d] + pl.program_id(d)
        n_steps = functools.reduce(lambda a, b: a * b, grid)

        @pl.when(step == 0)
        def _():
            xchg.start(xin, xout, sems)

        @pl.when(step == (2 * n_steps) // 3)
        def _():
            xchg.forward(xin, xout, sems)

        body(*ins, *outs, *scr)

        @pl.when(step == n_steps - 1)
        def _():
            xchg.wait(xin, xout, sems)

    res = pl.pallas_call(
        carried, name=name, grid=grid, in_specs=list(in_specs) + [ANY] * n, out_specs=list(out_specs) + [ANY] * n,
        out_shape=list(out_shape) + xchg.out_shape, scratch_shapes=list(scratch_shapes) + xchg.scratch,
        compiler_params=_params(("arbitrary",) * len(grid)))(*args, *xchg.srcs)
    return res[:n_out], tuple(res[n_out:])


def _exchange_alone(xchg, name):
    def body(*refs):
        xin, xout, sems = refs[:xchg.n], refs[xchg.n:2 * xchg.n], refs[2 * xchg.n:]
        xchg.start(xin, xout, sems)
        xchg.forward(xin, xout, sems)
        xchg.wait(xin, xout, sems)

    return pl.pallas_call(body, name=name, out_shape=xchg.out_shape, in_specs=[ANY] * xchg.n, out_specs=[ANY] * xchg.n,
                          scratch_shapes=xchg.scratch)(*xchg.srcs)


def _cast_shards(shards):
    n = len(shards)

    def body(*refs):
        for i, o in zip(refs[:n], refs[n:]):
            o[...] = i[...].astype(bf16)

    return pl.pallas_call(body, name="cast_shards", out_shape=[jax.ShapeDtypeStruct(s.shape, bf16) for s in shards],
                          in_specs=[VMEM] * n, out_specs=[VMEM] * n, compiler_params=_params())(*shards)


def _allreduce_rows(v):
    r = v.shape[0]
    rp = r // N_DEV

    def body(v_ref, o_ref, parts, sums, send1, recv1, send2, recv2):
        me = _my_index()

        def piece(ref, d):
            return ref.at[pl.ds(pl.multiple_of(d * rp, 8), rp), :]

        def copy1(k, src_dev, to):
            return pltpu.make_async_remote_copy(src_ref=piece(v_ref, to), dst_ref=parts.at[src_dev], send_sem=send1.at[k],
                                                recv_sem=recv1.at[k], device_id=_coords(to), device_id_type=MESH)

        def copy2(k, owner, to):
            return pltpu.make_async_remote_copy(src_ref=sums, dst_ref=piece(o_ref, owner), send_sem=send2.at[k],
                                                recv_sem=recv2.at[k], device_id=_coords(to), device_id_type=MESH)

        for k in range(1, N_DEV):
            copy1(k, me, (me + k) % N_DEV).start()
        parts[me] = v_ref[pl.ds(pl.multiple_of(me * rp, 8), rp), :]
        for k in range(1, N_DEV):
            copy1(k, (me + N_DEV - k) % N_DEV, me).wait_recv()
        total = parts[0]
        for s in range(1, N_DEV):
            total = total + parts[s]
        sums[...] = total
        o_ref[pl.ds(pl.multiple_of(me * rp, 8), rp), :] = total
        for k in range(1, N_DEV):
            copy2(k, me, (me + k) % N_DEV).start()
        for k in range(1, N_DEV):
            copy2(k, (me + N_DEV - k) % N_DEV, me).wait_recv()
        for k in range(1, N_DEV):
            copy1(k, me, (me + k) % N_DEV).wait_send()
            copy2(k, me, (me + k) % N_DEV).wait_send()

    return pl.pallas_call(
        body, name="allreduce_small_grads", out_shape=jax.ShapeDtypeStruct(v.shape, v.dtype),
        in_specs=[VMEM], out_specs=VMEM,
        scratch_shapes=[pltpu.VMEM((N_DEV, rp, LANES), f32), pltpu.VMEM((rp, LANES), f32)]
        + [pltpu.SemaphoreType.DMA((N_DEV,))] * 4,
        compiler_params=_params(),
    )(v)


def _all_to_all_rows(v, name):
    def body(v_ref, o_ref, send_sems, recv_sems):
        me = _my_index()
        o_ref[me] = v_ref[me]
        sends = []
        for k in range(1, N_DEV):
            peer = (me + k) % N_DEV
            rc = pltpu.make_async_remote_copy(src_ref=v_ref.at[peer], dst_ref=o_ref.at[me], send_sem=send_sems.at[k],
                                              recv_sem=recv_sems.at[k], device_id=_coords(peer), device_id_type=MESH)
            rc.start()
            sends.append(rc)
        for k in range(1, N_DEV):
            src = (me + N_DEV - k) % N_DEV
            pltpu.make_async_remote_copy(src_ref=v_ref.at[src], dst_ref=o_ref.at[src], send_sem=send_sems.at[k],
                                         recv_sem=recv_sems.at[k], device_id=_coords(src), device_id_type=MESH).wait_recv()
        for rc in sends:
            rc.wait_send()

    return pl.pallas_call(
        body, name=name, out_shape=jax.ShapeDtypeStruct(v.shape, v.dtype),
        in_specs=[VMEM], out_specs=VMEM,
        scratch_shapes=[pltpu.SemaphoreType.DMA((N_DEV,)), pltpu.SemaphoreType.DMA((N_DEV,))],
    )(v)


def _ada_forward(c_all, ada_w, ada_b_cols):
    def body(c_ref, w_ref, b_ref, cond_ref, o_ref):
        cond = _silu(c_ref[...])
        cond_ref[...] = cond
        for l in range(2):
            o_ref[l] = _dot(_b(cond), _b(w_ref[l])) + b_ref[l]

    return pl.pallas_call(
        body, name="ada_forward",
        out_shape=[jax.ShapeDtypeStruct((N_DEV, D_MODEL), f32), jax.ShapeDtypeStruct((2, N_DEV, 768), f32)],
        in_specs=[VMEM] * 3, out_specs=[VMEM] * 2, compiler_params=_params(),
    )(c_all, ada_w, ada_b_cols)


def _ada_backward(cond, dmod_rows):
    def body(c_ref, d_ref, o_ref):
        cb = _b(c_ref[...])
        for l in range(2):
            o_ref[l] = _dot_tn(cb, _b(d_ref[l]))

    return pl.pallas_call(
        body, name="ada_backward", out_shape=jax.ShapeDtypeStruct((2, D_MODEL, 768), f32),
        in_specs=[VMEM] * 2, out_specs=VMEM, compiler_params=_params(),
    )(cond, dmod_rows)


def _inproj_fwd(h, norm_w, sc, sh, w_in, tb, xchg=None):
    t = h.shape[0]

    def body(h_ref, nw_ref, sc_ref, sh_ref, w_ref, proj_ref, u_ref):
        n, _ = _rms(h_ref[...])
        u = _b(n * nw_ref[...] * (1.0 + sc_ref[...]) + sh_ref[...])
        u_ref[...] = u
        proj_ref[...] = _dot(u, w_ref[...])

    row = pl.BlockSpec((tb, D_MODEL), lambda i: (i, 0))
    vec = _full((1, D_MODEL))
    return _call(
        body, name="inproj_fwd", grid=(t // tb,),
        out_shape=[jax.ShapeDtypeStruct((t, P_IN), f32), jax.ShapeDtypeStruct((t, D_MODEL), bf16)],
        in_specs=[row, vec, vec, vec, _full((D_MODEL, P_IN))],
        out_specs=[pl.BlockSpec((tb, P_IN), lambda i: (i, 0)), row],
        semantics=("parallel",), args=(h, norm_w, sc, sh, w_in), xchg=xchg)


def _inproj_bwd(dparts, dh_res, h, norm_w, sc, sh, w_in, tb, xchg=None):
    t = h.shape[0]

    def body(*refs):
        parts = refs[:10]
        dres_ref, h_ref, nw_ref, sc_ref, sh_ref, w_ref = refs[10:16]
        dh_ref, dsh_ref, dsc_ref, dnw_ref = refs[16:]
        dproj = jnp.concatenate([p[...] for p in parts], axis=1)
        du = _dot_nt(dproj, w_ref[...])
        n, r = _rms(h_ref[...])
        nw = nw_ref[...]
        gain = 1.0 + sc_ref[...]
        _acc(dsh_ref, _colsum(du))
        _acc(dsc_ref, _colsum(du * n * nw))
        _acc(dnw_ref, _colsum(du * gain * n))
        dh_ref[...] = dres_ref[...] + _rms_bwd(du * nw * gain, n, r)

    row = pl.BlockSpec((tb, D_MODEL), lambda i: (i, 0))
    vec = _full((1, D_MODEL))
    part_specs = [pl.BlockSpec((tb, GROUP_W), lambda i: (i, 0))] * 9 + [pl.BlockSpec((tb, LANES), lambda i: (i, 0))]
    return _call(
        body, name="inproj_bwd", grid=(t // tb,),
        out_shape=[jax.ShapeDtypeStruct((t, D_MODEL), f32)] + [jax.ShapeDtypeStruct((1, D_MODEL), f32)] * 3,
        in_specs=part_specs + [row, row, vec, vec, vec,
                               pl.BlockSpec((D_MODEL, P_IN), lambda i: (0, 0), pipeline_mode=pl.Buffered(1))],
        out_specs=[row, vec, vec, vec],
        semantics=("arbitrary",), xchg=xchg, args=(*dparts, dh_res, h, norm_w, sc, sh, w_in))


def _wgrad(a, b, n_blocks, name, tm, tk=512):
    t, m = a.shape
    nb = b.shape[1] // n_blocks
    tk = min(tk, t)
    nk = t // tk

    def body(a_ref, b_ref, o_ref, acc_ref):
        k = pl.program_id(2)
        p = _dot_tn(a_ref[...], b_ref[...])

        @pl.when(k == 0)
        def _():
            acc_ref[...] = p

        @pl.when(k != 0)
        def _():
            acc_ref[...] += p

        @pl.when(k == nk - 1)
        def _():
            o_ref[0] = acc_ref[...].astype(o_ref.dtype)

    return pl.pallas_call(
        body, name=name, grid=(m // tm, n_blocks, nk),
        out_shape=jax.ShapeDtypeStruct((n_blocks, m, nb), bf16),
        in_specs=[pl.BlockSpec((tk, tm), lambda i, j, k: (k, i)), pl.BlockSpec((tk, nb), lambda i, j, k: (k, j))],
        out_specs=pl.BlockSpec((1, tm, nb), lambda i, j, k: (j, i, 0)),
        scratch_shapes=[pltpu.VMEM((tm, nb), f32)],
        compiler_params=_params(("parallel", "parallel", "arbitrary")),
    )(a, b)


def _wgrad_parts(a, parts, name, tm, tk):
    t, m = a.shape
    n = sum(p.shape[1] for p in parts)
    n_parts = len(parts)
    tk = min(tk, t)
    nk = t // tk

    def body(*refs):
        a_ref, part_refs, o_ref, acc_ref = refs[0], refs[1:1 + n_parts], refs[1 + n_parts], refs[2 + n_parts]
        k = pl.program_id(1)
        p = _dot_tn(a_ref[...], jnp.concatenate([r[...] for r in part_refs], axis=1))

        @pl.when(k == 0)
        def _():
            acc_ref[...] = p

        @pl.when(k != 0)
        def _():
            acc_ref[...] += p

        @pl.when(k == nk - 1)
        def _():
            o_ref[...] = acc_ref[...].astype(o_ref.dtype)

    return pl.pallas_call(
        body, name=name, grid=(m // tm, nk),
        out_shape=jax.ShapeDtypeStruct((m, n), bf16),
        in_specs=[pl.BlockSpec((tk, tm), lambda i, k: (k, i))]
        + [pl.BlockSpec((tk, p.shape[1]), lambda i, k: (k, 0)) for p in parts],
        out_specs=pl.BlockSpec((tm, n), lambda i, k: (i, 0)),
        scratch_shapes=[pltpu.VMEM((tm, n), f32)],
        compiler_params=_params(("parallel", "arbitrary")),
    )(a, *parts)


def _pool_counts(rows, t0):
    tpos = (lax.broadcasted_iota(jnp.int32, (rows, GROUP_W), 0) + t0 + 1).astype(f32)
    grp = lax.broadcasted_iota(jnp.int32, (rows, GROUP_W), 1) // 64
    win = jnp.where(grp == 0, 2.0, jnp.where(grp == 1, 4.0, jnp.where(grp == 2, 8.0, 16.0)))
    return jnp.minimum(tpos, win), grp


def _pool_select(grp, l1, l2, l3, l4):
    return jnp.where(grp == 0, l1, jnp.where(grp == 1, l2, jnp.where(grp == 2, l3, l4)))


def _pool_means(v, halo, t0):
    tb = v.shape[0]
    ext = jnp.concatenate([halo, v], axis=0)
    n = tb + 16
    s1 = ext[1:n] + ext[0:n - 1]
    s2 = s1[2:n - 1] + s1[0:n - 3]
    s3 = s2[4:n - 3] + s2[0:n - 7]
    s4 = s3[8:n - 7] + s3[0:n - 15]
    cnt, grp = _pool_counts(tb, t0)
    wsum = _pool_select(grp, s1[15:15 + tb], s2[13:13 + tb], s3[9:9 + tb], s4[1:1 + tb])
    return wsum / cnt - v


def _pool_fwd(proj, pw_bd, scale, tb):
    t = proj.shape[0]

    def body(v_ref, vh_ref, pw_ref, sc_ref, o_ref):
        i = pl.program_id(0)
        halo = jnp.where(i > 0, vh_ref[...], 0.0)
        p = _pool_means(v_ref[...], halo, i * tb)
        o_ref[...] = _b(_dot(_b(p), _b(pw_ref[...])) * sc_ref[...])

    return pl.pallas_call(
        body, name="pool_fwd", grid=(t // tb,),
        out_shape=jax.ShapeDtypeStruct((t, GROUP_W), bf16),
        in_specs=[pl.BlockSpec((tb, GROUP_W), lambda i: (i, C_POOL)),
                  pl.BlockSpec((16, GROUP_W), lambda i: (jnp.maximum(i * (tb // 16) - 1, 0), C_POOL)),
                  _full((GROUP_W, GROUP_W)), _full((1, GROUP_W))],
        out_specs=pl.BlockSpec((tb, GROUP_W), lambda i: (i, 0)),
        compiler_params=_params(("parallel",)),
    )(proj, proj, pw_bd, scale)


def _pool_bwd(proj, dy, pw_bd, scale, tb):
    t = proj.shape[0]
    nt = t // tb
    last16 = t // 16 - 1

    def body(v_ref, vh_ref, dy_ref, dyh_ref, pw_ref, sc_ref, dv_ref, dpw_ref, dsc_ref):
        i = pl.program_id(0)
        halo = jnp.where(i > 0, vh_ref[...], 0.0)
        p = _pool_means(v_ref[...], halo, i * tb)
        pw = _b(pw_ref[...])
        sc = sc_ref[...]
        dy = dy_ref[...]
        ypre = _dot(_b(p), pw)
        _acc(dsc_ref, _colsum(dy * ypre))
        dys = _b(dy * sc)
        _acc(dpw_ref, _dot_tn(_b(p), dys))
        dp = _dot_nt(dys, pw)
        dph = _dot_nt(_b(jnp.where(i < nt - 1, dyh_ref[...], 0.0) * sc), pw)
        cnt, grp = _pool_counts(tb, i * tb)
        cnth, _ = _pool_counts(16, (i + 1) * tb)
        ext = jnp.concatenate([dp / cnt, dph / cnth], axis=0)
        n = tb + 16
        f1 = ext[0:n - 1] + ext[1:n]
        f2 = f1[0:n - 3] + f1[2:n - 1]
        f3 = f2[0:n - 7] + f2[4:n - 3]
        f4 = f3[0:n - 15] + f3[8:n - 7]
        dv_ref[...] = _b(_pool_select(grp, f1[0:tb], f2[0:tb], f3[0:tb], f4[0:tb]) - dp)

    return pl.pallas_call(
        body, name="pool_bwd", grid=(nt,),
        out_shape=[jax.ShapeDtypeStruct((t, GROUP_W), bf16), jax.ShapeDtypeStruct((GROUP_W, GROUP_W), f32),
                   jax.ShapeDtypeStruct((1, GROUP_W), f32)],
        in_specs=[pl.BlockSpec((tb, GROUP_W), lambda i: (i, C_POOL)),
                  pl.BlockSpec((16, GROUP_W), lambda i: (jnp.maximum(i * (tb // 16) - 1, 0), C_POOL)),
                  pl.BlockSpec((tb, GROUP_W), lambda i: (i, 0)),
                  pl.BlockSpec((16, GROUP_W), lambda i: (jnp.minimum((i + 1) * (tb // 16), last16), 0)),
                  _full((GROUP_W, GROUP_W)), _full((1, GROUP_W))],
        out_specs=[pl.BlockSpec((tb, GROUP_W), lambda i: (i, 0)), _full((GROUP_W, GROUP_W)), _full((1, GROUP_W))],
        compiler_params=_params(("arbitrary",)),
    )(proj, proj, dy, dy, pw_bd, scale)


def _sconv_fwd(proj, w, tb):
    t = proj.shape[0]

    def body(gb_ref, gc_ref, hh_ref, gch_ref, hhh_ref, w_ref, o_ref):
        i = pl.program_id(0)
        q = gc_ref[...] * hh_ref[...]
        qh = jnp.where(i > 0, gch_ref[...] * hhh_ref[...], 0.0)
        ext = jnp.concatenate([qh, q], axis=0)
        w = w_ref[...]
        conv = w[0:1] * ext[6:6 + tb] + w[1:2] * ext[7:7 + tb] + w[2:3] * ext[8:8 + tb]
        o_ref[...] = _b(gb_ref[...] * conv)

    def col(c):
        return pl.BlockSpec((tb, GROUP_W), lambda i: (i, c))

    def prev(c):
        return pl.BlockSpec((8, GROUP_W), lambda i: (jnp.maximum(i * (tb // 8) - 1, 0), c))

    return pl.pallas_call(
        body, name="sconv_fwd", grid=(t // tb,),
        out_shape=jax.ShapeDtypeStruct((t, GROUP_W), bf16),
        in_specs=[col(C_GB), col(C_GC), col(C_HH), prev(C_GC), prev(C_HH), _full((8, GROUP_W))],
        out_specs=pl.BlockSpec((tb, GROUP_W), lambda i: (i, 0)),
        compiler_params=_params(("parallel",)),
    )(proj, proj, proj, proj, proj, w)


def _sconv_bwd(proj, dy, w, tb):
    t = proj.shape[0]
    nt = t // tb
    last8 = t // 8 - 1

    def body(gb_ref, gc_ref, hh_ref, gch_ref, hhh_ref, gbn_ref, dy_ref, dyn_ref, w_ref, dgb_ref, dgc_ref, dhh_ref, dw_ref):
        i = pl.program_id(0)
        gc, hh, gb, dy = gc_ref[...], hh_ref[...], gb_ref[...], dy_ref[...]
        q = gc * hh
        qh = jnp.where(i > 0, gch_ref[...] * hhh_ref[...], 0.0)
        ext = jnp.concatenate([qh, q], axis=0)
        w = w_ref[...]
        conv = w[0:1] * ext[6:6 + tb] + w[1:2] * ext[7:7 + tb] + w[2:3] * ext[8:8 + tb]
        dgb_ref[...] = _b(dy * conv)
        e = dy * gb
        en = jnp.where(i < nt - 1, dyn_ref[...] * gbn_ref[...], 0.0)
        exte = jnp.concatenate([e, en], axis=0)
        dq = w[2:3] * exte[0:tb] + w[1:2] * exte[1:1 + tb] + w[0:1] * exte[2:2 + tb]
        dgc_ref[...] = _b(dq * hh)
        dhh_ref[...] = _b(dq * gc)
        dw = jnp.concatenate([_colsum(e * ext[6:6 + tb]), _colsum(e * ext[7:7 + tb]), _colsum(e * ext[8:8 + tb]),
                              jnp.zeros((5, GROUP_W), f32)], axis=0)
        _acc(dw_ref, dw)

    def col(c):
        return pl.BlockSpec((tb, GROUP_W), lambda i: (i, c))

    def prev(c):
        return pl.BlockSpec((8, GROUP_W), lambda i: (jnp.maximum(i * (tb // 8) - 1, 0), c))

    def nxt(c):
        return pl.BlockSpec((8, GROUP_W), lambda i: (jnp.minimum((i + 1) * (tb // 8), last8), c))

    out = pl.BlockSpec((tb, GROUP_W), lambda i: (i, 0))
    return pl.pallas_call(
        body, name="sconv_bwd", grid=(nt,),
        out_shape=[jax.ShapeDtypeStruct((t, GROUP_W), bf16)] * 3 + [jax.ShapeDtypeStruct((8, GROUP_W), f32)],
        in_specs=[col(C_GB), col(C_GC), col(C_HH), prev(C_GC), prev(C_HH), nxt(C_GB), col(0), nxt(0), _full((8, GROUP_W))],
        out_specs=[out, out, out, _full((8, GROUP_W))],
        compiler_params=_params(("arbitrary",)),
    )(proj, proj, proj, proj, proj, proj, dy, dy, w)


def _conv4(xr, halo, w, bias):
    tb = xr.shape[0]
    ext = jnp.concatenate([halo, xr], axis=0)
    pre = w[0:1] * ext[5:5 + tb] + w[1:2] * ext[6:6 + tb] + w[2:3] * ext[7:7 + tb] + w[3:4] * ext[8:8 + tb] + bias
    return pre, ext


def _tri():
    r = lax.broadcasted_iota(jnp.int32, (SSD_CHUNK, SSD_CHUNK), 0)
    c = lax.broadcasted_iota(jnp.int32, (SSD_CHUNK, SSD_CHUNK), 1)
    return r >= c


def _lane_pick(vals):
    rows = vals[0].shape[0]
    lane = lax.broadcasted_iota(jnp.int32, (rows, LANES), 1)
    out = jnp.zeros((rows, LANES), f32)
    for h, v in enumerate(vals):
        out = jnp.where(lane == h, v, out)
    return out


def _ssd_fwd(proj, conv_w, conv_b, dt_bias, a_log, d_cols, tb, xchg=None):
    t = proj.shape[0]
    cpt = tb // SSD_CHUNK

    def body(z_ref, xs_ref, bm_ref, cm_ref, xsh_ref, bmh_ref, cmh_ref, dt_ref, cw_ref, cb_ref, dtb_ref, al_ref, dk_ref,
             o_ref, y_ref, st_ref, state):
        i = pl.program_id(0)

        @pl.when(i == 0)
        def _():
            state[...] = jnp.zeros_like(state)

        cw, cb = cw_ref[...], cb_ref[...]
        acts = []
        for j, (r, hr) in enumerate(((xs_ref, xsh_ref), (bm_ref, bmh_ref), (cm_ref, cmh_ref))):
            halo = jnp.where(i > 0, hr[...], 0.0)
            pre, _ = _conv4(r[...], halo, cw[:, j * 256:(j + 1) * 256], cb[:, j * 256:(j + 1) * 256])
            acts.append(_silu(pre))
        xs, bm, cm = acts
        dt = _softplus(dt_ref[...] + dtb_ref[...])
        a = -jnp.exp(al_ref[...])
        adt = dt * a
        tri = _tri()
        trif = tri.astype(f32)
        dk = dk_ref[...]
        for c in range(cpt):
            rows = slice(c * SSD_CHUNK, (c + 1) * SSD_CHUNK)
            acol = _dot_exact(trif, adt[rows])
            arow = acol.T
            dt_c = dt[rows]
            ys = []
            rowi = lax.broadcasted_iota(jnp.int32, (SSD_CHUNK, 1), 0)
            first = lax.broadcasted_iota(jnp.int32, (SSD_CHUNK, SSD_CHUNK), 1) < SSD_P
            for g in range(SSD_HEADS // 2):
                cols = slice(g * 128, (g + 1) * 128)
                cg, bg = _b(cm[rows, cols]), _b(bm[rows, cols])
                xg = xs[rows, cols]
                heads = (2 * g, 2 * g + 1)
                ac = [acol[:, h:h + 1] for h in heads]
                alast = [v[SSD_CHUNK - 1:SSD_CHUNK] for v in ac]
                dtw = jnp.where(first, dt_c[:, heads[0]:heads[0] + 1], dt_c[:, heads[1]:heads[1] + 1])
                eaw = jnp.where(first, jnp.exp(ac[0]), jnp.exp(ac[1]))
                wdw = jnp.where(first, jnp.exp(alast[0] - ac[0]), jnp.exp(alast[1] - ac[1]))
                xdt = xg * dtw
                xb = _b(xdt)
                gmat = _dot_nt(cg, bg)
                ydiag = []
                for k, h in enumerate(heads):
                    lm = jnp.exp(jnp.where(tri, ac[k] - arow[h:h + 1, :], -jnp.inf))
                    ydiag.append(_dot(_b(gmat * lm), xb[:, k * SSD_P:(k + 1) * SSD_P]))
                s_in = state[g]
                st_ref[c, g] = s_in
                ys.append(jnp.concatenate(ydiag, axis=1) + eaw * _dot_nt(cg, _b(s_in)) + xg * dk[:, cols])
                state[g] = jnp.where(rowi < SSD_P, jnp.exp(alast[0]), jnp.exp(alast[1])) * s_in + _dot_tn(_b(xdt * wdw), bg)
            yc = jnp.concatenate(ys, axis=1)
            y_ref[rows, :] = yc
            o_ref[rows, :] = _b(yc * _silu(z_ref[rows, :]))

    def col(c):
        return pl.BlockSpec((tb, GROUP_W), lambda i: (i, c))

    def prev(c):
        return pl.BlockSpec((8, GROUP_W), lambda i: (jnp.maximum(i * (tb // 8) - 1, 0), c))

    out = pl.BlockSpec((tb, GROUP_W), lambda i: (i, 0))
    return _call(
        body, name="ssd_fwd", grid=(t // tb,),
        out_shape=[jax.ShapeDtypeStruct((t, GROUP_W), bf16), jax.ShapeDtypeStruct((t, GROUP_W), f32),
                   jax.ShapeDtypeStruct((t // SSD_CHUNK, 2, 128, 128), f32)],
        in_specs=[col(C_Z), col(C_XS), col(C_BM), col(C_CM), prev(C_XS), prev(C_BM), prev(C_CM),
                  pl.BlockSpec((tb, LANES), lambda i: (i, C_DT128)),
                  _full((8, 768)), _full((1, 768)), _full((1, LANES)), _full((1, LANES)), _full((1, GROUP_W))],
        out_specs=[out, out, pl.BlockSpec((cpt, 2, 128, 128), lambda i: (i, 0, 0, 0))],
        scratch_shapes=[pltpu.VMEM((2, 128, 128), f32)],
        semantics=("arbitrary",), xchg=xchg,
        args=(proj, proj, proj, proj, proj, proj, proj, proj, conv_w, conv_b, dt_bias, a_log, d_cols))


def _ssd_bwd(proj, dyc, y_pre, states, conv_w, conv_b, dt_bias, a_log, d_cols, tb, xchg=None):
    t = proj.shape[0]
    nt = t // tb
    cpt = tb // SSD_CHUNK

    def body(z_ref, xs_ref, bm_ref, cm_ref, xsh_ref, bmh_ref, cmh_ref, dt_ref, dy_ref, yp_ref, st_ref,
             cw_ref, cb_ref, dtb_ref, al_ref, dk_ref,
             dz_ref, dxs_ref, dbm_ref, dcm_ref, ddt_ref, dcw_ref, dcb_ref, ddtb_ref, dal_ref, ddk_ref,
             dstate, carry):
        i = pl.program_id(0)
        ti = nt - 1 - i

        @pl.when(i == 0)
        def _():
            dstate[...] = jnp.zeros_like(dstate)
            carry[...] = jnp.zeros_like(carry)

        cw, cb = cw_ref[...], cb_ref[...]
        pres, exts, acts = [], [], []
        for j, (r, hr) in enumerate(((xs_ref, xsh_ref), (bm_ref, bmh_ref), (cm_ref, cmh_ref))):
            halo = jnp.where(ti > 0, hr[...], 0.0)
            pre, ext = _conv4(r[...], halo, cw[:, j * 256:(j + 1) * 256], cb[:, j * 256:(j + 1) * 256])
            pres.append(pre)
            exts.append(ext)
            acts.append(_silu(pre))
        xs, bm, cm = acts
        raw = dt_ref[...] + dtb_ref[...]
        dt = _softplus(raw)
        a = -jnp.exp(al_ref[...])
        adt = dt * a
        tri = _tri()
        trif = tri.astype(f32)
        dk = dk_ref[...]
        z = z_ref[...]
        dyc = dy_ref[...]
        dz_ref[...] = _b(dyc * yp_ref[...] * _dsilu(z))
        dy_all = dyc * _silu(z)
        lane = lax.broadcasted_iota(jnp.int32, (1, LANES), 1)
        ddk_acc = jnp.zeros((1, LANES), f32)
        dal_acc = jnp.zeros((1, LANES), f32)
        dxs_c, dbm_c, dcm_c, ddt_c = [None] * cpt, [None] * cpt, [None] * cpt, [None] * cpt
        for c in reversed(range(cpt)):
            rows = slice(c * SSD_CHUNK, (c + 1) * SSD_CHUNK)
            acol = _dot_exact(trif, adt[rows])
            arow = acol.T
            dt_c = dt[rows]
            da_cols, da_rows, ddt_heads, dxs_groups, dbg, dcg = [], [], [], [], [], []
            rowi = lax.broadcasted_iota(jnp.int32, (SSD_CHUNK, 1), 0)
            first = lax.broadcasted_iota(jnp.int32, (SSD_CHUNK, SSD_CHUNK), 1) < SSD_P
            for g in range(SSD_HEADS // 2):
                cols = slice(g * 128, (g + 1) * 128)
                cgf, bgf = cm[rows, cols], bm[rows, cols]
                cg, bg = _b(cgf), _b(bgf)
                xg, dyg = xs[rows, cols], dy_all[rows, cols]
                s_in, dsn = st_ref[c, g], dstate[g]
                sb, dsnb = _b(s_in), _b(dsn)
                heads = (2 * g, 2 * g + 1)
                ac = [acol[:, h:h + 1] for h in heads]
                alast = [v[SSD_CHUNK - 1:SSD_CHUNK] for v in ac]
                el = [jnp.exp(v) for v in alast]
                dtw = jnp.where(first, dt_c[:, heads[0]:heads[0] + 1], dt_c[:, heads[1]:heads[1] + 1])
                eaw = jnp.where(first, jnp.exp(ac[0]), jnp.exp(ac[1]))
                wdw = jnp.where(first, jnp.exp(alast[0] - ac[0]), jnp.exp(alast[1] - ac[1]))
                xdt = xg * dtw
                xb, dyb = _b(xdt), _b(dyg)
                gmat = _dot_nt(cg, bg)
                dgs, dxh, da = None, [], []
                for k, h in enumerate(heads):
                    hc = slice(k * SSD_P, (k + 1) * SSD_P)
                    lm = jnp.exp(jnp.where(tri, ac[k] - arow[h:h + 1, :], -jnp.inf))
                    m = gmat * lm
                    dm = _dot_nt(dyb[:, hc], xb[:, hc])
                    dxh.append(_dot_tn(_b(m), dyb[:, hc]))
                    dgs = dm * lm if dgs is None else dgs + dm * lm
                    wm = dm * m
                    da.append(jnp.sum(wm, axis=1, keepdims=True))
                    da_rows.append(jnp.sum(wm, axis=0, keepdims=True))
                dgb = _b(dgs)
                dcg_g = _dot(dgb, bg)
                dbg_g = _dot_tn(dgb, cg)
                yoff = eaw * _dot_nt(cg, sb)
                dyoff = dyg * yoff
                dye = _b(dyg * eaw)
                dcg_g = dcg_g + _dot(dye, sb)
                ds_y = _dot_tn(dye, cg)
                u = _dot_nt(bg, dsnb)
                dx = jnp.concatenate(dxh, axis=1) + wdw * u
                dbg_g = dbg_g + _dot(_b(xdt * wdw), dsnb)
                xu = xdt * u * wdw
                ss = jnp.sum(dsn * s_in, axis=1, keepdims=True)
                dxx = dx * xg
                dyx = _colsum(dyg * xg)
                for k, h in enumerate(heads):
                    mine = first if k == 0 else jnp.logical_not(first)
                    dwv = jnp.sum(jnp.where(mine, xu, 0.0), axis=1, keepdims=True)
                    mine_rows = (rowi < SSD_P) if k == 0 else (rowi >= SSD_P)
                    dalast = jnp.sum(dwv, axis=0, keepdims=True) + el[k] * jnp.sum(jnp.where(mine_rows, ss, 0.0), axis=0, keepdims=True)
                    dah = da[k] + jnp.sum(jnp.where(mine, dyoff, 0.0), axis=1, keepdims=True) - dwv
                    da_cols.append(dah + jnp.where(rowi == SSD_CHUNK - 1, dalast, 0.0))
                    ddt_heads.append(jnp.sum(jnp.where(mine, dxx, 0.0), axis=1, keepdims=True))
                    ddk_acc = ddk_acc + jnp.where(lane == h, jnp.sum(jnp.where(mine[0:1], dyx, 0.0), axis=1, keepdims=True), 0.0)
                dstate[g] = jnp.where(rowi < SSD_P, el[0], el[1]) * dsn + ds_y
                dxs_groups.append(dx * dtw + dyg * dk[:, cols])
                dbg.append(dbg_g)
                dcg.append(dcg_g)
            da_blk = _lane_pick(da_cols)
            rowsel = lax.broadcasted_iota(jnp.int32, (SSD_CHUNK, SSD_CHUNK), 0)
            da_rows_blk = jnp.zeros((SSD_CHUNK, SSD_CHUNK), f32)
            for h in range(SSD_HEADS):
                da_rows_blk = jnp.where(rowsel == h, da_rows[h], da_rows_blk)
            da_blk = da_blk - da_rows_blk.T
            dadt = lax.dot_general(trif, da_blk, (((0,), (0,)), ((), ())), preferred_element_type=f32,
                                   precision=lax.Precision.HIGHEST)
            dal_acc = dal_acc + _colsum(dadt * dt_c)
            ddt_c[c] = dadt * a + _lane_pick(ddt_heads)
            dxs_c[c] = jnp.concatenate(dxs_groups, axis=1)
            dbm_c[c] = jnp.concatenate(dbg, axis=1)
            dcm_c[c] = jnp.concatenate(dcg, axis=1)
        ddt = jnp.concatenate(ddt_c, axis=0) if cpt > 1 else ddt_c[0]
        ddraw = jnp.where(lane < SSD_HEADS, ddt * jax.nn.sigmoid(raw), 0.0)
        ddt_ref[...] = _b(ddraw)
        _acc(ddtb_ref, _colsum(ddraw))
        _acc(dal_ref, jnp.where(lane < SSD_HEADS, dal_acc * a, 0.0))
        _acc(ddk_ref, ddk_acc)
        dcw_parts, dcb_parts = [], []
        for j, (dparts, out_ref) in enumerate(((dxs_c, dxs_ref), (dbm_c, dbm_ref), (dcm_c, dcm_ref))):
            dact = jnp.concatenate(dparts, axis=0) if cpt > 1 else dparts[0]
            dpre = dact * _dsilu(pres[j])
            w = cw[:, j * 256:(j + 1) * 256]
            ext = jnp.concatenate([dpre, carry[:, j * 256:(j + 1) * 256]], axis=0)
            out_ref[...] = _b(w[3:4] * ext[0:tb] + w[2:3] * ext[1:1 + tb] + w[1:2] * ext[2:2 + tb] + w[0:1] * ext[3:3 + tb])
            carry[:, j * 256:(j + 1) * 256] = dpre[0:8]
            xe = exts[j]
            dcw_parts.append(jnp.concatenate([_colsum(dpre * xe[5 + k:5 + k + tb]) for k in range(4)]
                                             + [jnp.zeros((4, GROUP_W), f32)], axis=0))
            dcb_parts.append(_colsum(dpre))
        _acc(dcw_ref, jnp.concatenate(dcw_parts, axis=1))
        _acc(dcb_ref, jnp.concatenate(dcb_parts, axis=1))

    def col(c):
        return pl.BlockSpec((tb, GROUP_W), lambda i: (nt - 1 - i, c))

    def prev(c):
        return pl.BlockSpec((8, GROUP_W), lambda i: (jnp.maximum((nt - 1 - i) * (tb // 8) - 1, 0), c))

    out = pl.BlockSpec((tb, GROUP_W), lambda i: (nt - 1 - i, 0))
    vec = _full((1, LANES))
    return _call(
        body, name="ssd_bwd", grid=(nt,),
        out_shape=[jax.ShapeDtypeStruct((t, GROUP_W), bf16)] * 4 + [jax.ShapeDtypeStruct((t, LANES), bf16),
                   jax.ShapeDtypeStruct((8, 768), f32), jax.ShapeDtypeStruct((1, 768), f32)]
        + [jax.ShapeDtypeStruct((1, LANES), f32)] * 3,
        in_specs=[col(C_Z), col(C_XS), col(C_BM), col(C_CM), prev(C_XS), prev(C_BM), prev(C_CM),
                  pl.BlockSpec((tb, LANES), lambda i: (nt - 1 - i, C_DT128)), out, out,
                  pl.BlockSpec((cpt, 2, 128, 128), lambda i: (nt - 1 - i, 0, 0, 0)),
                  _full((8, 768)), _full((1, 768)), vec, vec, _full((1, GROUP_W))],
        out_specs=[out, out, out, out, pl.BlockSpec((tb, LANES), lambda i: (nt - 1 - i, 0)),
                   _full((8, 768)), _full((1, 768)), vec, vec, vec],
        scratch_shapes=[pltpu.VMEM((2, 128, 128), f32), pltpu.VMEM((8, 768), f32)],
        semantics=("arbitrary",), xchg=xchg,
        args=(proj, proj, proj, proj, proj, proj, proj, proj, dyc, y_pre, states, conv_w, conv_b, dt_bias, a_log, d_cols))


def _s5_coeffs(are, aim, ls):
    step = jnp.exp(ls)
    mag = jnp.exp(are * step)
    th = aim * step
    lre, lim = mag * jnp.cos(th), mag * jnp.sin(th)
    den = are * are + aim * aim
    nr = lre - 1.0
    fre = (nr * are + lim * aim) / den
    fim = (lim * are - nr * aim) / den
    return step, lre, lim, den, fre, fim


def _s5_prep(are, aim, ls, bre_bd, bim_bd):
    def body(are_ref, aim_ref, ls_ref, bre_ref, bim_ref, lre_ref, lim_ref, bbr_ref, bbi_ref):
        _, lre, lim, _, fre, fim = _s5_coeffs(are_ref[...], aim_ref[...], ls_ref[...])
        lre_ref[...] = lre
        lim_ref[...] = lim
        bre, bim = bre_ref[...], bim_ref[...]
        bbr_ref[...] = fre * bre - fim * bim
        bbi_ref[...] = fre * bim + fim * bre

    col = jax.ShapeDtypeStruct((S5_N, 1), f32)
    mat = jax.ShapeDtypeStruct((S5_N, GROUP_W), f32)
    return pl.pallas_call(body, name="s5_prep", out_shape=[col, col, mat, mat], in_specs=[VMEM] * 5, out_specs=[VMEM] * 4,
                          compiler_params=_params())(are, aim, ls, bre_bd, bim_bd)


def _s5_prep_bwd(are, aim, ls, bre_bd, bim_bd, dlre, dlim, dbbr, dbbi):
    def body(are_ref, aim_ref, ls_ref, bre_ref, bim_ref, dlre_ref, dlim_ref, dbbr_ref, dbbi_ref,
             dare_ref, daim_ref, dls_ref, dbre_ref, dbim_ref):
        are, aim = are_ref[...], aim_ref[...]
        step, lre, lim, den, fre, fim = _s5_coeffs(are, aim, ls_ref[...])
        r = lax.broadcasted_iota(jnp.int32, (S5_N, GROUP_W), 0) // 64
        c = lax.broadcasted_iota(jnp.int32, (S5_N, GROUP_W), 1) // 16
        mask = r == c
        gr = jnp.where(mask, dbbr_ref[...], 0.0)
        gi = jnp.where(mask, dbbi_ref[...], 0.0)
        bre, bim = bre_ref[...], bim_ref[...]
        dbre_ref[...] = fre * gr + fim * gi
        dbim_ref[...] = fre * gi - fim * gr
        dfre = jnp.sum(bre * gr + bim * gi, axis=1, keepdims=True)
        dfim = jnp.sum(bre * gi - bim * gr, axis=1, keepdims=True)
        ire, iim = are / den, aim / den
        tre = dlre_ref[...] + ire * dfre - iim * dfim
        tim = dlim_ref[...] + ire * dfim + iim * dfre
        dzre = lre * tre + lim * tim
        dzim = lre * tim - lim * tre
        qre = (fre * are + fim * aim) / den
        qim = (fim * are - fre * aim) / den
        dare_ref[...] = step * dzre - (qre * dfre + qim * dfim)
        daim_ref[...] = step * dzim - (qre * dfim - qim * dfre)
        dls = (are * dzre + aim * dzim) * step
        sel = (lax.broadcasted_iota(jnp.int32, (S5_N, LANES), 0) // 64 == lax.broadcasted_iota(jnp.int32, (S5_N, LANES), 1)).astype(f32)
        dls_ref[...] = lax.dot_general(sel, jnp.broadcast_to(dls, (S5_N, LANES)), (((0,), (0,)), ((), ())),
                                       preferred_element_type=f32, precision=lax.Precision.HIGHEST)

    col = jax.ShapeDtypeStruct((S5_N, 1), f32)
    mat = jax.ShapeDtypeStruct((S5_N, GROUP_W), f32)
    return pl.pallas_call(body, name="s5_prep_bwd", out_shape=[col, col, jax.ShapeDtypeStruct((LANES, LANES), f32), mat, mat],
                          in_specs=[VMEM] * 9, out_specs=[VMEM] * 5, compiler_params=_params(),
                          )(are, aim, ls, bre_bd, bim_bd, dlre, dlim, dbbr, dbbi)


def _cmul(ar, ai, br, bi):
    return ar * br - ai * bi, ar * bi + ai * br


def _s5_scan(re_ref, im_ref, carry_ref, mr, mi, n_groups, reverse):
    p1 = (mr, mi)
    p2 = _cmul(*p1, *p1)
    p3 = _cmul(*p2, *p1)
    p4 = _cmul(*p2, *p2)
    p5 = _cmul(*p4, *p1)
    p6 = _cmul(*p4, *p2)
    p7 = _cmul(*p4, *p3)
    p8 = _cmul(*p4, *p4)
    pows = [p1, p2, p3, p4, p5, p6, p7, p8]
    row = lax.broadcasted_iota(jnp.int32, (8, S5_N), 0)
    tr = jnp.zeros((8, S5_N), f32)
    ti = jnp.zeros((8, S5_N), f32)
    for i in range(8):
        p = pows[7 - i] if reverse else pows[i]
        tr = jnp.where(row == i, p[0], tr)
        ti = jnp.where(row == i, p[1], ti)
    steps = []
    for k, p in ((1, p1), (2, p2), (4, p4)):
        keep = (row + k < 8) if reverse else (row >= k)
        steps.append((8 - k if reverse else k, jnp.where(keep, p[0], 0.0), jnp.where(keep, p[1], 0.0)))
    edge = 0 if reverse else 7

    def step(j, carry):
        cr, ci = carry
        g = (n_groups - 1 - j) if reverse else j
        r0 = pl.multiple_of(g * 8, 8)
        xr = re_ref[pl.ds(r0, 8), :]
        xi = im_ref[pl.ds(r0, 8), :]
        for shift, br, bi in steps:
            sr = pltpu.roll(xr, shift, 0)
            si = pltpu.roll(xi, shift, 0)
            xr, xi = xr + br * sr - bi * si, xi + br * si + bi * sr
        xr, xi = xr + tr * cr - ti * ci, xi + tr * ci + ti * cr
        re_ref[pl.ds(r0, 8), :] = xr
        im_ref[pl.ds(r0, 8), :] = xi
        return (jnp.broadcast_to(xr[edge:edge + 1, :], (8, S5_N)), jnp.broadcast_to(xi[edge:edge + 1, :], (8, S5_N)))

    cr, ci = lax.fori_loop(0, n_groups, step, (carry_ref[0], carry_ref[1]))
    carry_ref[0] = cr
    carry_ref[1] = ci


def _s5_output(u, xr, xi, ctr, cti, d):
    return _dot_nt(_b(xr), _b(ctr)) - _dot_nt(_b(xi), _b(cti)) + d * u


def _s5_fwd(proj, bbr, bbi, ctr, cti, lre, lim, d, glu_w, glu_b, tb, xchg=None):
    t = proj.shape[0]

    def body(u_ref, bbr_ref, bbi_ref, ctr_ref, cti_ref, lr_ref, li_ref, d_ref, gw_ref, gb_ref, o_ref, xr_ref, xi_ref, carry):
        @pl.when(pl.program_id(0) == 0)
        def _():
            carry[...] = jnp.zeros_like(carry)

        u = u_ref[...]
        ub = _b(u)
        xr_ref[...] = _dot_nt(ub, _b(bbr_ref[...]))
        xi_ref[...] = _dot_nt(ub, _b(bbi_ref[...]))
        _s5_scan(xr_ref, xi_ref, carry, lr_ref[...], li_ref[...], tb // 8, reverse=False)
        y = _s5_output(u, xr_ref[...], xi_ref[...], ctr_ref[...], cti_ref[...], d_ref[...])
        gl = _gelu(y)
        o_ref[...] = _b(gl * jax.nn.sigmoid(_dot(_b(gl), _b(gw_ref[...])) + gb_ref[...]))

    state = pl.BlockSpec((tb, S5_N), lambda i: (i, 0))
    return _call(
        body, name="s5_fwd", grid=(t // tb,),
        out_shape=[jax.ShapeDtypeStruct((t, GROUP_W), bf16), jax.ShapeDtypeStruct((t, S5_N), f32), jax.ShapeDtypeStruct((t, S5_N), f32)],
        in_specs=[pl.BlockSpec((tb, GROUP_W), lambda i: (i, C_S5)), _full((S5_N, GROUP_W)), _full((S5_N, GROUP_W)),
                  _full((GROUP_W, S5_N)), _full((GROUP_W, S5_N)), _full((1, S5_N)), _full((1, S5_N)),
                  _full((1, GROUP_W)), _full((GROUP_W, GROUP_W)), _full((1, GROUP_W))],
        out_specs=[pl.BlockSpec((tb, GROUP_W), lambda i: (i, 0)), state, state],
        scratch_shapes=[pltpu.VMEM((2, 8, S5_N), f32)],
        semantics=("arbitrary",), xchg=xchg, args=(proj, bbr, bbi, ctr, cti, lre, lim, d, glu_w, glu_b))


def _s5_bwd(proj, dyd, xr_all, xi_all, bbr, bbi, ctr, cti, lre, lim, d, glu_w, glu_b, tb, xchg=None):
    t = proj.shape[0]
    nt = t // tb

    def body(u_ref, dy_ref, xr_ref, xi_ref, xrh_ref, xih_ref, bbr_ref, bbi_ref, ctr_ref, cti_ref, lr_ref, li_ref,
             d_ref, gw_ref, gb_ref,
             du_ref, dlr_ref, dli_ref, dbbr_ref, dbbi_ref, dctr_ref, dcti_ref, dd_ref, dgw_ref, dgb_ref,
             gr_ref, gi_ref, carry):
        i = pl.program_id(0)
        ti = nt - 1 - i

        @pl.when(i == 0)
        def _():
            carry[...] = jnp.zeros_like(carry)

        u = u_ref[...]
        ub = _b(u)
        xr, xi = xr_ref[...], xi_ref[...]
        ctr, cti = _b(ctr_ref[...]), _b(cti_ref[...])
        d = d_ref[...]
        gw = _b(gw_ref[...])
        y = _s5_output(u, xr, xi, ctr, cti, d)
        gl = _gelu(y)
        sg = jax.nn.sigmoid(_dot(_b(gl), gw) + gb_ref[...])
        dout = dy_ref[...]
        q = dout * gl * sg * (1.0 - sg)
        qb = _b(q)
        dgl = dout * sg + _dot_nt(qb, gw)
        _acc(dgw_ref, _dot_tn(_b(gl), qb))
        _acc(dgb_ref, _colsum(q))
        dyv = dgl * _dgelu(y)
        _acc(dd_ref, _colsum(dyv * u))
        dyb = _b(dyv)
        gr_ref[...] = _dot(dyb, ctr)
        gi_ref[...] = -_dot(dyb, cti)
        _acc(dctr_ref, _dot_tn(dyb, _b(xr)))
        _acc(dcti_ref, -_dot_tn(dyb, _b(xi)))
        _s5_scan(gr_ref, gi_ref, carry, lr_ref[...], -li_ref[...], tb // 8, reverse=True)
        gr, gi = gr_ref[...], gi_ref[...]
        xpr = jnp.concatenate([jnp.where(ti > 0, xrh_ref[...], 0.0), xr], axis=0)[7:7 + tb]
        xpi = jnp.concatenate([jnp.where(ti > 0, xih_ref[...], 0.0), xi], axis=0)[7:7 + tb]
        _acc(dlr_ref, _colsum(gr * xpr + gi * xpi))
        _acc(dli_ref, _colsum(gi * xpr - gr * xpi))
        grb, gib = _b(gr), _b(gi)
        _acc(dbbr_ref, _dot_tn(grb, ub))
        _acc(dbbi_ref, _dot_tn(gib, ub))
        du_ref[...] = _b(dyv * d + _dot(grb, _b(bbr_ref[...])) + _dot(gib, _b(bbi_ref[...])))

    state = pl.BlockSpec((tb, S5_N), lambda i: (nt - 1 - i, 0))
    prev = pl.BlockSpec((8, S5_N), lambda i: (jnp.maximum((nt - 1 - i) * (tb // 8) - 1, 0), 0))
    tile = pl.BlockSpec((tb, GROUP_W), lambda i: (nt - 1 - i, 0))
    return _call(
        body, name="s5_bwd", grid=(nt,),
        out_shape=[jax.ShapeDtypeStruct((t, GROUP_W), bf16), jax.ShapeDtypeStruct((1, S5_N), f32), jax.ShapeDtypeStruct((1, S5_N), f32),
                   jax.ShapeDtypeStruct((S5_N, GROUP_W), f32), jax.ShapeDtypeStruct((S5_N, GROUP_W), f32),
                   jax.ShapeDtypeStruct((GROUP_W, S5_N), f32), jax.ShapeDtypeStruct((GROUP_W, S5_N), f32),
                   jax.ShapeDtypeStruct((1, GROUP_W), f32), jax.ShapeDtypeStruct((GROUP_W, GROUP_W), f32),
                   jax.ShapeDtypeStruct((1, GROUP_W), f32)],
        in_specs=[pl.BlockSpec((tb, GROUP_W), lambda i: (nt - 1 - i, C_S5)), tile, state, state, prev, prev,
                  _full((S5_N, GROUP_W)), _full((S5_N, GROUP_W)), _full((GROUP_W, S5_N)), _full((GROUP_W, S5_N)),
                  _full((1, S5_N)), _full((1, S5_N)), _full((1, GROUP_W)), _full((GROUP_W, GROUP_W)), _full((1, GROUP_W))],
        out_specs=[tile, _full((1, S5_N)), _full((1, S5_N)), _full((S5_N, GROUP_W)), _full((S5_N, GROUP_W)),
                   _full((GROUP_W, S5_N)), _full((GROUP_W, S5_N)), _full((1, GROUP_W)), _full((GROUP_W, GROUP_W)), _full((1, GROUP_W))],
        scratch_shapes=[pltpu.VMEM((tb, S5_N), f32), pltpu.VMEM((tb, S5_N), f32), pltpu.VMEM((2, 8, S5_N), f32)],
        semantics=("arbitrary",), xchg=xchg,
        args=(proj, dyd, xr_all, xi_all, xr_all, xi_all, bbr, bbi, ctr, cti, lre, lim, d, glu_w, glu_b))


def _outproj_fwd(ys, h, bn_w, g1, w_out, tb):
    t = h.shape[0]

    def body(ya_ref, yb_ref, yc_ref, yd_ref, h_ref, bn_ref, g1_ref, w_ref, h1_ref, o_ref, gr_ref):
        bn = bn_ref[...]
        parts = []
        for g, r in enumerate((ya_ref, yb_ref, yc_ref, yd_ref)):
            n, _ = _rms(r[...].astype(f32))
            parts.append(n * bn[:, g * GROUP_W:(g + 1) * GROUP_W])
        groups = _b(jnp.concatenate(parts, axis=1))
        gr_ref[...] = groups
        o = _dot(groups, w_ref[...])
        o_ref[...] = _b(o)
        h1_ref[...] = h_ref[...] + g1_ref[...] * o

    grp = pl.BlockSpec((tb, GROUP_W), lambda i: (i, 0))
    row = pl.BlockSpec((tb, D_MODEL), lambda i: (i, 0))
    vec = _full((1, D_MODEL))
    return pl.pallas_call(
        body, name="outproj_fwd", grid=(t // tb,),
        out_shape=[jax.ShapeDtypeStruct((t, D_MODEL), f32), jax.ShapeDtypeStruct((t, D_MODEL), bf16),
                   jax.ShapeDtypeStruct((t, D_MODEL), bf16)],
        in_specs=[grp, grp, grp, grp, row, vec, vec, _full((D_MODEL, D_MODEL))],
        out_specs=[row, row, row],
        compiler_params=_params(("parallel",)),
    )(*ys, h, bn_w, g1, w_out)


def _outproj_bwd(dh1, o, ys, bn_w, g1, w_out, tb):
    t = dh1.shape[0]

    def body(dh_ref, o_ref, ya_ref, yb_ref, yc_ref, yd_ref, bn_ref, g1_ref, w_ref,
             da_ref, db_ref, dc_ref, dd_ref, do_ref, dg1_ref, dbn_ref):
        dh = dh_ref[...]
        _acc(dg1_ref, _colsum(dh * o_ref[...].astype(f32)))
        do = _b(dh * g1_ref[...])
        do_ref[...] = do
        dgroups = _dot_nt(do, w_ref[...])
        bn = bn_ref[...]
        dbn = []
        for g, (r, dr) in enumerate(((ya_ref, da_ref), (yb_ref, db_ref), (yc_ref, dc_ref), (yd_ref, dd_ref))):
            n, rr = _rms(r[...].astype(f32))
            dgr = dgroups[:, g * GROUP_W:(g + 1) * GROUP_W]
            dbn.append(_colsum(dgr * n))
            dr[...] = _rms_bwd(dgr * bn[:, g * GROUP_W:(g + 1) * GROUP_W], n, rr)
        _acc(dbn_ref, jnp.concatenate(dbn, axis=1))

    grp = pl.BlockSpec((tb, GROUP_W), lambda i: (i, 0))
    row = pl.BlockSpec((tb, D_MODEL), lambda i: (i, 0))
    vec = _full((1, D_MODEL))
    return pl.pallas_call(
        body, name="outproj_bwd", grid=(t // tb,),
        out_shape=[jax.ShapeDtypeStruct((t, GROUP_W), f32)] * 4 + [jax.ShapeDtypeStruct((t, D_MODEL), bf16),
                   jax.ShapeDtypeStruct((1, D_MODEL), f32), jax.ShapeDtypeStruct((1, D_MODEL), f32)],
        in_specs=[row, row, grp, grp, grp, grp, vec, vec, _full((D_MODEL, D_MODEL))],
        out_specs=[grp, grp, grp, grp, row, vec, vec],
        compiler_params=_params(("arbitrary",)),
    )(dh1, o, *ys, bn_w, g1, w_out)


def _mlp_fwd(h1, norm_w, sc, sh, g2, w1, w2, tb, xchg=None, head=None):
    t = h1.shape[0]
    nh = w1.shape[0] // MLP_SLABS
    n_head = 0 if head is None else 2

    def body(*refs):
        h_ref, nw_ref, sc_ref, sh_ref, g2_ref, w1_ref, w2_ref = refs[:7]
        head_refs, outs = refs[7:7 + n_head], refs[7 + n_head:]
        h2_ref, m_ref, v_ref, r_ref, acc = outs[0], outs[1], outs[2], outs[3], outs[-1]
        j = pl.program_id(1)

        @pl.when(j == 0)
        def _():
            n, _ = _rms(h_ref[...])
            v_ref[...] = _b(n * nw_ref[...] * (1.0 + sc_ref[...]) + sh_ref[...])

        v = v_ref[...]
        p = None
        for s in range(MLP_SLABS):
            ra = jnp.maximum(_dot(v, w1_ref[s]), 0.0)
            r = _b(ra * ra)
            r_ref[:, s * MLP_HB:(s + 1) * MLP_HB] = r
            q = _dot(r, w2_ref[s])
            p = q if p is None else p + q

        @pl.when(j == 0)
        def _():
            acc[...] = p

        @pl.when(j != 0)
        def _():
            acc[...] += p

        @pl.when(j == nh - 1)
        def _():
            m_ref[...] = _b(acc[...])
            if head is None:
                h2_ref[...] = h_ref[...] + g2_ref[...] * acc[...]
            else:
                tgt_ref, fw_ref = head_refs
                loss_ref, dfw_ref = outs[4], outs[5]
                fw, g2 = fw_ref[...], g2_ref[...]
                ch = min(HEAD_ROWS, tb)

                def chunk(c, carry):
                    sq, dfw = carry
                    rows = pl.ds(pl.multiple_of(c * ch, ch), ch)
                    n, r = _rms(h_ref[rows, :] + g2 * acc[rows, :])
                    err = n * fw - tgt_ref[rows, :]
                    dy = err / D_MODEL
                    h2_ref[rows, :] = _rms_bwd(dy * fw, n, r)
                    return (sq + jnp.sum(jnp.sum(err * err, axis=1, keepdims=True), axis=0, keepdims=True),
                            dfw + _colsum(dy * n))

                sq, dfw = lax.fori_loop(0, tb // ch, chunk, (jnp.zeros((1, 1), f32), jnp.zeros((1, D_MODEL), f32)))
                _acc(loss_ref, jnp.broadcast_to(0.5 * sq / D_MODEL, (8, LANES)))
                _acc(dfw_ref, dfw)

    row = pl.BlockSpec((tb, D_MODEL), lambda i, j: (i, 0))
    hid = pl.BlockSpec((tb, MLP_SLABS * MLP_HB), lambda i, j: (i, j))
    vec = _full((1, D_MODEL))
    head_shapes = [] if head is None else [jax.ShapeDtypeStruct((8, LANES), f32), jax.ShapeDtypeStruct((1, D_MODEL), f32)]
    return _call(
        body, name="mlp_fwd", grid=(t // tb, nh),
        out_shape=[jax.ShapeDtypeStruct((t, D_MODEL), f32), jax.ShapeDtypeStruct((t, D_MODEL), bf16),
                   jax.ShapeDtypeStruct((t, D_MODEL), bf16), jax.ShapeDtypeStruct((t, N_DEV * MLP_HB), bf16)] + head_shapes,
        in_specs=[row, vec, vec, vec, vec, pl.BlockSpec((MLP_SLABS, D_MODEL, MLP_HB), lambda i, j: (j, 0, 0)),
                  pl.BlockSpec((MLP_SLABS, MLP_HB, D_MODEL), lambda i, j: (j, 0, 0))] + ([] if head is None else [row, vec]),
        out_specs=[row, row, row, hid] + ([] if head is None else [_full((8, LANES)), vec]),
        scratch_shapes=[pltpu.VMEM((tb, D_MODEL), f32)],
        semantics=("arbitrary", "arbitrary"), xchg=xchg, args=(h1, norm_w, sc, sh, g2, w1, w2) + (() if head is None else tuple(head)))


def _mlp_bwd(dh2, m, h1, r, norm_w, sc, sh, g2, w1, w2, tb, xchg=None):
    t = h1.shape[0]
    slabs = MLP_BWD_SLABS
    nh = w1.shape[0] // slabs

    def body(dh_ref, m_ref, h_ref, r_ref, nw_ref, sc_ref, sh_ref, g2_ref, w1_ref, w2_ref,
             dh1_ref, do_ref, da_ref, dg2_ref, dsh_ref, dsc_ref, dnw_ref, acc):
        j = pl.program_id(1)

        @pl.when(j == 0)
        def _():
            dh = dh_ref[...]
            _acc(dg2_ref, _colsum(dh * m_ref[...].astype(f32)))
            do_ref[...] = _b(dh * g2_ref[...])

        do = do_ref[...]
        p = None
        for s in range(slabs):
            cols = slice(s * MLP_HB, (s + 1) * MLP_HB)
            dr = _dot_nt(do, w2_ref[s])
            da = _b(dr * 2.0 * jnp.sqrt(r_ref[:, cols].astype(f32)))
            da_ref[:, cols] = da
            q = _dot_nt(da, w1_ref[s])
            p = q if p is None else p + q

        @pl.when(j == 0)
        def _():
            acc[...] = p

        @pl.when(j != 0)
        def _():
            acc[...] += p

        @pl.when(j == nh - 1)
        def _():
            dv = acc[...]
            n, r = _rms(h_ref[...])
            nw = nw_ref[...]
            gain = 1.0 + sc_ref[...]
            _acc(dsh_ref, _colsum(dv))
            _acc(dsc_ref, _colsum(dv * n * nw))
            _acc(dnw_ref, _colsum(dv * gain * n))
            dh1_ref[...] = dh_ref[...] + _rms_bwd(dv * nw * gain, n, r)

    row = pl.BlockSpec((tb, D_MODEL), lambda i, j: (i, 0))
    hid = pl.BlockSpec((tb, slabs * MLP_HB), lambda i, j: (i, j))
    vec = _full((1, D_MODEL))
    once = dict(pipeline_mode=pl.Buffered(1)) if nh == 1 else {}
    return _call(
        body, name="mlp_bwd", grid=(t // tb, nh),
        out_shape=[jax.ShapeDtypeStruct((t, D_MODEL), f32), jax.ShapeDtypeStruct((t, D_MODEL), bf16),
                   jax.ShapeDtypeStruct((t, N_DEV * MLP_HB), bf16)] + [jax.ShapeDtypeStruct((1, D_MODEL), f32)] * 4,
        in_specs=[row, row, row, hid, vec, vec, vec, vec,
                  pl.BlockSpec((slabs, D_MODEL, MLP_HB), lambda i, j: (j, 0, 0), **once),
                  pl.BlockSpec((slabs, MLP_HB, D_MODEL), lambda i, j: (j, 0, 0), **once)],
        out_specs=[row, row, hid, vec, vec, vec, vec],
        scratch_shapes=[pltpu.VMEM((tb, D_MODEL), f32)],
        semantics=("arbitrary", "arbitrary"), xchg=xchg, args=(dh2, m, h1, r, norm_w, sc, sh, g2, w1, w2))


def _adam_math(w, g, m, v):
    m2 = ADAM_B1 * m + (1.0 - ADAM_B1) * g
    v2 = ADAM_B2 * v + (1.0 - ADAM_B2) * (g * g)
    mh = m2 / (1.0 - ADAM_B1 ** ADAM_STEP)
    vh = v2 / (1.0 - ADAM_B2 ** ADAM_STEP)
    return -ADAM_LR * (mh / (jnp.sqrt(vh) + ADAM_EPS) + ADAM_WD * w), m2, v2


def _adamw_small(ws, gs, ms, vs):
    n = len(ws)
    shapes = [w.shape for w in ws]
    as2d = [(1,) + s if len(s) == 1 else s for s in shapes]
    flat = [x.reshape(s) for group in (ws, gs, ms, vs) for x, s in zip(group, as2d)]

    def body(*refs):
        w_refs, g_refs, m_refs, v_refs, outs = refs[:n], refs[n:2 * n], refs[2 * n:3 * n], refs[3 * n:4 * n], refs[4 * n:]
        for i in range(n):
            d, m2, v2 = _adam_math(w_refs[i][...], g_refs[i][...], m_refs[i][...], v_refs[i][...])
            outs[3 * i][...] = d
            outs[3 * i + 1][...] = m2
            outs[3 * i + 2][...] = v2

    res = pl.pallas_call(body, name="adamw_small", out_shape=[jax.ShapeDtypeStruct(s, f32) for s in as2d for _ in range(3)],
                         in_specs=[VMEM] * (4 * n), out_specs=[VMEM] * (3 * n), compiler_params=_params())(*flat)
    return [r.reshape(shapes[i // 3]) for i, r in enumerate(res)]


def _sum_adamw_layers(parts0, parts1, w, m, v, name, rb):
    n_src, r, c = parts0.shape
    nb = r // rb

    def body(p0_ref, p1_ref, w_ref, m_ref, v_ref, g_ref, d_ref, m2_ref, v2_ref):
        def update(p_ref):
            g = p_ref[0].astype(f32)
            for s in range(1, n_src):
                g = g + p_ref[s].astype(f32)
            g_ref[0] = g
            d, m2, v2 = _adam_math(w_ref[0], g, m_ref[0], v_ref[0])
            d_ref[0] = d
            m2_ref[0] = m2
            v2_ref[0] = v2

        @pl.when(pl.program_id(0) == 0)
        def _():
            update(p0_ref)

        @pl.when(pl.program_id(0) == 1)
        def _():
            update(p1_ref)

    blk = pl.BlockSpec((1, rb, c), lambda l, i: (l, i, 0))
    return pl.pallas_call(
        body, name=name, grid=(2, nb),
        out_shape=[jax.ShapeDtypeStruct((2, r, c), f32)] * 4,
        in_specs=[pl.BlockSpec((n_src, rb, c), lambda l, i: (0, jnp.where(l == 0, i, nb - 1), 0)),
                  pl.BlockSpec((n_src, rb, c), lambda l, i: (0, jnp.where(l == 1, i, 0), 0)), blk, blk, blk],
        out_specs=[blk] * 4,
        compiler_params=_params(("arbitrary", "arbitrary")),
    )(parts0, parts1, w, m, v)


def _reorder_in(w):
    pad = jnp.zeros(w.shape[:-1] + (P_IN - 2308,), w.dtype)
    return jnp.concatenate([w[..., :2048], w[..., 2052:2308], w[..., 2048:2052], pad], axis=-1)


def _unreorder_in(w):
    return jnp.concatenate([w[..., :2048], w[..., 2304:2308], w[..., 2048:2304]], axis=-1)


def _block_diag(w2d, n_blocks):
    rows, cols = w2d.shape
    tiled = jnp.tile(w2d, (1, n_blocks))
    rb = lax.broadcasted_iota(jnp.int32, tiled.shape, 0) // (rows // n_blocks)
    cb = lax.broadcasted_iota(jnp.int32, tiled.shape, 1) // cols
    return jnp.where(rb == cb, tiled, jnp.zeros_like(tiled))


def _block_diag_extract(w_bd, n_blocks):
    rows, wide = w_bd.shape
    r, c = rows // n_blocks, wide // n_blocks
    w4 = w_bd.reshape(n_blocks, r, n_blocks, c)
    idx = jnp.arange(n_blocks)
    return w4[idx, :, idx, :]


def _rows_of(shape):
    n = 1
    for d in shape:
        n *= d
    return -(-n // (8 * LANES)) * 8, n


def _flat_pack(arrs, row_multiple=8):
    blocks = []
    for a in arrs:
        rows, n = _rows_of(a.shape)
        blocks.append(jnp.pad(a.reshape(-1), (0, rows * LANES - n)).reshape(rows, LANES))
    total = sum(b.shape[0] for b in blocks)
    pad = -total % row_multiple
    if pad:
        blocks.append(jnp.zeros((pad, LANES), blocks[0].dtype))
    return jnp.concatenate(blocks, axis=0)


def _flat_unpack(packed, shapes):
    out, off = [], 0
    for s in shapes:
        rows, n = _rows_of(s)
        out.append(packed[off:off + rows].reshape(-1)[:n].reshape(s))
        off += rows
    return out


_W_NAMES = ['norm_mix_w', 'norm_mlp_w', 'ada_w', 'ada_b', 'w_in', 'pool_w', 'pool_scale', 'sconv_w', 'ssd_conv_w',
            'ssd_conv_b', 'ssd_dt_bias', 'ssd_a_log', 'ssd_d', 's5_a_re', 's5_a_im', 's5_log_step', 's5_b_re', 's5_b_im',
            's5_c_re', 's5_c_im', 's5_d', 's5_glu_w', 's5_glu_b', 'branch_norm_w', 'w_out', 'mlp_w1', 'mlp_w2',
            'final_norm_w']
_BIG = ('ada_w', 'w_in', 'w_out', 'mlp_w1', 'mlp_w2')
_SMALL = [n for n in _W_NAMES if n not in _BIG]
_SHARDED_SMALL = {'sconv_w': (2, 32), 'ssd_conv_w': (2, 96), 's5_glu_w': (1, 32)}


def _gather(*blocks):
    return _ChipGather(blocks)


def _scatter(*parts):
    return _Scatter(parts)


def _layer_forward(l, h, p, w, sh_b, tb, head=None):
    first = l == 0
    (proj, u_b), got = _inproj_fwd(h, p['norm_mix_w'][l], p['sc1'][l], p['sh1'][l], w['w_in', l], tb,
                                   xchg=_gather(sh_b[1][0]) if first else None)
    if first:
        w['w_out', 0] = got[0].reshape(D_MODEL, D_MODEL)
    ya = _pool_fwd(proj, p['pool_bd'][l], p['pool_scale'][l], tb)
    yb = _sconv_fwd(proj, p['sconv_w8'][l], tb)
    (yc, yc_pre, states), got = _ssd_fwd(proj, p['ssd_conv_w8'][l], p['ssd_conv_b'][l], p['ssd_dt_bias'][l], p['ssd_a_log'][l],
                                         p['ssd_d_cols'][l], tb, xchg=_gather(sh_b[2][0]) if first else None)
    if first:
        w['w1', 0] = got[0]
    (yd, xr, xi), got = _s5_fwd(proj, p['bbr'][l], p['bbi'][l], p['ctr'][l], p['cti'][l], p['lre'][l], p['lim'][l],
                                p['s5_d'][l], p['glu_w'][l], p['glu_b'][l], tb, xchg=_gather(sh_b[3][0]) if first else None)
    if first:
        w['w2', 0] = got[0]
    ys = (ya, yb, yc, yd)
    h1, o, groups_b = _outproj_fwd(ys, h, p['branch_norm_w'][l], p['g1'][l], w['w_out', l], tb)
    (h2, m, v_b, r_b, *head_out), got = _mlp_fwd(
        h1, p['norm_mlp_w'][l], p['sc2'][l], p['sh2'][l], p['g2'][l], w['w1', l], w['w2', l],
        min(MLP_TB, h.shape[0]),
        xchg=_gather(*[sh_b[k][1] for k in range(4)]) if first else None, head=head)
    if first:
        w['w_in', 1] = got[0].reshape(D_MODEL, P_IN)
        w['w_out', 1] = got[1].reshape(D_MODEL, D_MODEL)
        w['w1', 1], w['w2', 1] = got[2], got[3]
    saved = dict(h=h, proj=proj, u_b=u_b, ys=ys, yc_pre=yc_pre, states=states, xr=xr, xi=xi, h1=h1, o=o,
                 groups_b=groups_b, m=m, v_b=v_b, r_b=r_b)
    return (h2, *head_out), saved


def _layer_backward(l, dh2, s, p, w, pending, recv, tb):
    def carry(names):
        names = [n for n in names if n in pending]
        return names, (_scatter(*[pending.pop(n) for n in names]) if names else None)

    def landed(names, got):
        for n, g in zip(names, got):
            recv[n] = g

    names, xchg = carry([('w_out', 1)])
    (dh1, do2_b, da_b, dg2, dsh2, dsc2, dnw_mlp), got = _mlp_bwd(dh2, s['m'], s['h1'], s['r_b'], p['norm_mlp_w'][l], p['sc2'][l],
                                                                p['sh2'][l], p['g2'][l], w['w1', l], w['w2', l], min(TB_BWD, tb),
                                                                xchg=xchg)
    landed(names, got)
    pending['mlp_w2', l] = _wgrad(s['r_b'], do2_b, 1, "wgrad_w2", tm=1024, tk=4096).reshape(N_DEV, MLP_HB, D_MODEL)
    pending['mlp_w1', l] = _wgrad(s['v_b'], da_b, N_DEV, "wgrad_w1", tm=1024, tk=4096)
    dya, dyb, dyc, dyd, do1_b, dg1, dbn = _outproj_bwd(dh1, s['o'], s['ys'], p['branch_norm_w'][l], p['g1'][l], w['w_out', l], tb)
    pending['w_out', l] = _wgrad(s['groups_b'], do1_b, 1, "wgrad_wout", tm=1024, tk=1024).reshape(N_DEV, D_MODEL // N_DEV, D_MODEL)
    proj = s['proj']
    dv, dpool_bd, dpool_scale = _pool_bwd(proj, dya, p['pool_bd'][l], p['pool_scale'][l], tb)
    dgb, dgc, dhh, dsconv = _sconv_bwd(proj, dyb, p['sconv_w8'][l], tb)
    names, xchg = carry([('mlp_w1', l)] + ([('w_out', 0)] if l == 0 else []))
    (dz, dxs, dbm, dcm, ddt, dconv_w, dconv_b, ddtb, dalog, ddskip), got = _ssd_bwd(
        proj, dyc, s['yc_pre'], s['states'], p['ssd_conv_w8'][l], p['ssd_conv_b'][l], p['ssd_dt_bias'][l], p['ssd_a_log'][l],
        p['ssd_d_cols'][l], min(TB_BWD, tb), xchg=xchg)
    landed(names, got)
    names, xchg = carry([('mlp_w2', l)])
    (du5, dlr, dli, dbbr, dbbi, dctr, dcti, dd5, dgw, dgb5), got = _s5_bwd(
        proj, dyd, s['xr'], s['xi'], p['bbr'][l], p['bbi'][l], p['ctr'][l], p['cti'][l], p['lre'][l], p['lim'][l],
        p['s5_d'][l], p['glu_w'][l], p['glu_b'][l], min(TB_BWD, tb), xchg=xchg)
    landed(names, got)
    dare, daim, dls, dbre_bd, dbim_bd = _s5_prep_bwd(p['are_c'][l], p['aim_c'][l], p['ls_c'][l], p['bre_bd'][l], p['bim_bd'][l],
                                                     dlr.reshape(S5_N, 1), dli.reshape(S5_N, 1), dbbr, dbbi)
    dparts = (dv, dgb, dgc, dhh, dz, dxs, dbm, dcm, du5, ddt)
    pending['w_in', l] = _wgrad_parts(s['u_b'], dparts, "wgrad_win", tm=1024, tk=1024).reshape(N_DEV, D_MODEL // N_DEV, P_IN)
    names, xchg = carry([('w_in', l)])
    (dh, dsh1, dsc1, dnw_mix), got = _inproj_bwd(dparts, dh1, s['h'], p['norm_mix_w'][l], p['sc1'][l], p['sh1'][l], w['w_in', l],
                                                 tb, xchg=xchg)
    landed(names, got)
    small = {
        'norm_mix_w': dnw_mix.reshape(D_MODEL), 'norm_mlp_w': dnw_mlp.reshape(D_MODEL),
        'ada_b': jnp.concatenate([dsh1, dsc1, dg1, dsh2, dsc2, dg2], axis=1).reshape(6 * D_MODEL),
        'pool_w': _block_diag_extract(dpool_bd, 4), 'pool_scale': dpool_scale.reshape(GROUP_W),
        'sconv_w': dsconv[0:3], 'ssd_conv_w': dconv_w[0:4], 'ssd_conv_b': dconv_b.reshape(768),
        'ssd_dt_bias': ddtb[0, 0:4], 'ssd_a_log': dalog[0, 0:4], 'ssd_d': ddskip[0, 0:4],
        's5_a_re': dare.reshape(16, 64), 's5_a_im': daim.reshape(16, 64), 's5_log_step': dls[0:16, 0],
        's5_b_re': _block_diag_extract(dbre_bd, 16), 's5_b_im': _block_diag_extract(dbim_bd, 16),
        's5_c_re': _block_diag_extract(dctr, 16), 's5_c_im': _block_diag_extract(dcti, 16),
        's5_d': dd5.reshape(GROUP_W), 's5_glu_w': dgw, 's5_glu_b': dgb5.reshape(GROUP_W),
        'branch_norm_w': dbn.reshape(D_MODEL),
    }
    return dh, small


def _prepare_params(a, me, w_in0_shard):
    pack_shapes = [(1, D_MODEL), (2, 3, 32), (2, 4, 96), (2, 32, GROUP_W)]
    packed = _flat_pack([a['c'], a['sconv_w'], a['ssd_conv_w'], a['s5_glu_w']])
    w_in0, gathered = _exchange_alone(_gather(w_in0_shard, packed), "gather_first")
    pieces = [_flat_unpack(gathered[d], pack_shapes) for d in range(N_DEV)]
    c_all = jnp.concatenate([pc[0] for pc in pieces], axis=0)
    sconv_full = jnp.concatenate([pc[1] for pc in pieces], axis=2)
    ssd_conv_full = jnp.concatenate([pc[2] for pc in pieces], axis=2)
    glu_full = jnp.concatenate([pc[3] for pc in pieces], axis=1)

    ada_b_cols = lax.dynamic_slice_in_dim(a['ada_b'], me * 768, 768, axis=1).reshape(2, 1, 768)
    cond, modrows = _ada_forward(c_all, a['ada_w'], ada_b_cols)
    mod_recv = _all_to_all_rows(modrows.transpose(1, 0, 2), "exchange_mod")
    mod = mod_recv.transpose(1, 0, 2).reshape(2, 6 * D_MODEL)
    p = {'cond': cond}
    for k, name in enumerate(('sh1', 'sc1', 'g1', 'sh2', 'sc2', 'g2')):
        p[name] = mod[:, k * D_MODEL:(k + 1) * D_MODEL].reshape(2, 1, D_MODEL)

    for name in ('norm_mix_w', 'norm_mlp_w', 'branch_norm_w'):
        p[name] = a[name].reshape(2, 1, D_MODEL)
    p['pool_bd'] = jnp.stack([_block_diag(a['pool_w'][l].reshape(GROUP_W, 64), 4) for l in range(2)])
    p['pool_scale'] = a['pool_scale'].reshape(2, 1, GROUP_W)
    p['sconv_w8'] = jnp.pad(sconv_full, ((0, 0), (0, 5), (0, 0)))
    p['ssd_conv_w8'] = jnp.pad(ssd_conv_full, ((0, 0), (0, 4), (0, 0)))
    p['ssd_conv_b'] = a['ssd_conv_b'].reshape(2, 1, 768)
    p['ssd_dt_bias'] = jnp.pad(a['ssd_dt_bias'], ((0, 0), (0, LANES - 4))).reshape(2, 1, LANES)
    p['ssd_a_log'] = jnp.pad(a['ssd_a_log'], ((0, 0), (0, LANES - 4))).reshape(2, 1, LANES)
    p['ssd_d_cols'] = jnp.repeat(a['ssd_d'], SSD_P, axis=1).reshape(2, 1, GROUP_W)
    p['are_c'] = a['s5_a_re'].reshape(2, S5_N, 1)
    p['aim_c'] = a['s5_a_im'].reshape(2, S5_N, 1)
    p['ls_c'] = jnp.repeat(a['s5_log_step'], 64, axis=1).reshape(2, S5_N, 1)
    p['bre_bd'] = jnp.stack([_block_diag(a['s5_b_re'][l].reshape(S5_N, 16), 16) for l in range(2)])
    p['bim_bd'] = jnp.stack([_block_diag(a['s5_b_im'][l].reshape(S5_N, 16), 16) for l in range(2)])
    p['ctr'] = jnp.stack([_block_diag(a['s5_c_re'][l].reshape(GROUP_W, 64), 16) for l in range(2)])
    p['cti'] = jnp.stack([_block_diag(a['s5_c_im'][l].reshape(GROUP_W, 64), 16) for l in range(2)])
    p['s5_d'] = a['s5_d'].reshape(2, 1, GROUP_W)
    p['glu_w'] = glu_full
    p['glu_b'] = a['s5_glu_b'].reshape(2, 1, GROUP_W)
    lre, lim, bbr, bbi = [], [], [], []
    for l in range(2):
        r = _s5_prep(p['are_c'][l], p['aim_c'][l], p['ls_c'][l], p['bre_bd'][l], p['bim_bd'][l])
        lre.append(r[0].reshape(1, S5_N))
        lim.append(r[1].reshape(1, S5_N))
        bbr.append(r[2])
        bbi.append(r[3])
    p['lre'], p['lim'], p['bbr'], p['bbi'] = lre, lim, bbr, bbi
    return p, w_in0


def kernel(x, c, norm_mix_w, norm_mlp_w, ada_w, ada_b, w_in, pool_w, pool_scale, sconv_w, ssd_conv_w, ssd_conv_b, ssd_dt_bias, ssd_a_log, ssd_d, s5_a_re, s5_a_im, s5_log_step, s5_b_re, s5_b_im, s5_c_re, s5_c_im, s5_d, s5_glu_w, s5_glu_b, branch_norm_w, w_out, mlp_w1, mlp_w2, final_norm_w, loss_target, m_norm_mix_w, m_norm_mlp_w, m_ada_w, m_ada_b, m_w_in, m_pool_w, m_pool_scale, m_sconv_w, m_ssd_conv_w, m_ssd_conv_b, m_ssd_dt_bias, m_ssd_a_log, m_ssd_d, m_s5_a_re, m_s5_a_im, m_s5_log_step, m_s5_b_re, m_s5_b_im, m_s5_c_re, m_s5_c_im, m_s5_d, m_s5_glu_w, m_s5_glu_b, m_branch_norm_w, m_w_out, m_mlp_w1, m_mlp_w2, m_final_norm_w, v_norm_mix_w, v_norm_mlp_w, v_ada_w, v_ada_b, v_w_in, v_pool_w, v_pool_scale, v_sconv_w, v_ssd_conv_w, v_ssd_conv_b, v_ssd_dt_bias, v_ssd_a_log, v_ssd_d, v_s5_a_re, v_s5_a_im, v_s5_log_step, v_s5_b_re, v_s5_b_im, v_s5_c_re, v_s5_c_im, v_s5_d, v_s5_glu_w, v_s5_glu_b, v_branch_norm_w, v_w_out, v_mlp_w1, v_mlp_w2, v_final_norm_w):
    a = dict(locals())
    t = x.shape[1]
    tb = min(TB, t)
    me = _my_index()
    sh_b = _cast_shards([_reorder_in(w_in), w_out, mlp_w1, mlp_w2])
    p, w_in0 = _prepare_params(a, me, sh_b[0][0])
    w = {('w_in', 0): w_in0.reshape(D_MODEL, P_IN)}

    h = x.reshape(t, D_MODEL)
    saved = []
    (h,), s = _layer_forward(0, h, p, w, sh_b, tb)
    saved.append(s)
    (dh, loss_blk, dfinal), s = _layer_forward(1, h, p, w, sh_b, tb,
                                               head=(loss_target.reshape(t, D_MODEL), final_norm_w.reshape(1, D_MODEL)))
    saved.append(s)

    pending, recv, small_parts = {}, {}, [None, None]
    for l in (1, 0):
        dh, small_parts[l] = _layer_backward(l, dh, saved[l], p, w, pending, recv, tb)
    grad_x = dh.reshape(1, t, D_MODEL)

    grads, deltas, new_m, new_v = {}, {}, {}, {}

    wmv_in = [_reorder_in(a[n]) for n in ('w_in', 'm_w_in', 'v_w_in')]
    outs = _sum_adamw_layers(recv['w_in', 0], recv['w_in', 1], *wmv_in, "adamw_w_in", 128)
    grads['w_in'], deltas['w_in'], new_m['w_in'], new_v['w_in'] = [_unreorder_in(o) for o in outs]
    for name, rb in (('w_out', 128), ('mlp_w1', 256), ('mlp_w2', 256)):
        grads[name], deltas[name], new_m[name], new_v[name] = _sum_adamw_layers(
            recv[name, 0], recv[name, 1], a[name], a['m_' + name], a['v_' + name], "adamw_" + name, rb)

    dmod = jnp.stack([small_parts[0]['ada_b'], small_parts[1]['ada_b']])
    dmod_recv = _all_to_all_rows(dmod.reshape(2, N_DEV, 768).transpose(1, 0, 2), "exchange_dmod")
    g_ada = _ada_backward(p['cond'], dmod_recv.transpose(1, 0, 2))
    grads['ada_w'], deltas['ada_w'], new_m['ada_w'], new_v['ada_w'] = _sum_adamw_layers(
        g_ada[0:1], g_ada[1:2], ada_w, m_ada_w, v_ada_w, "adamw_ada_w", 256)

    layered = [n for n in _SMALL if n != 'final_norm_w']
    full = [jnp.stack([small_parts[0][n], small_parts[1][n]]) for n in layered] + [dfinal.reshape(D_MODEL)]
    full.append(loss_blk[0:1, 0:1])
    full_shapes = [f.shape for f in full]
    summed = _flat_unpack(_allreduce_rows(_flat_pack(full, row_multiple=64)), full_shapes)
    loss = summed[-1].reshape(())
    local = []
    for n, g in zip(_SMALL, summed):
        if n in _SHARDED_SMALL:
            axis, size = _SHARDED_SMALL[n]
            g = lax.dynamic_slice_in_dim(g, me * size, size, axis=axis)
        local.append(g.reshape(a[n].shape))
    outs = _adamw_small([a[n] for n in _SMALL], local, [a['m_' + n] for n in _SMALL], [a['v_' + n] for n in _SMALL])
    for i, n in enumerate(_SMALL):
        grads[n], deltas[n], new_m[n], new_v[n] = local[i], outs[3 * i], outs[3 * i + 1], outs[3 * i + 2]

    return (loss, grad_x, *[grads[n] for n in _W_NAMES], *[deltas[n] for n in _W_NAMES],
            *[new_m[n] for n in _W_NAMES], *[new_v[n] for n in _W_NAMES])
```

```python
import functools

import jax
import jax.numpy as jnp
from jax import lax
from jax.experimental import pallas as pl
from jax.experimental.pallas import tpu as pltpu

f32 = jnp.float32
bf16 = jnp.bfloat16

N_DEV = 8
D_MODEL = 1024
GROUP_W = 256
P_IN = 2432
DT_COL = 2304
SSD_CHUNK = 128
SSD_HEADS = 4
SSD_P = 64
S5_N = 1024
MLP_HB = 512
TB = 1024
TB_BWD = 512
MLP_TB = 1024
HEAD_ROWS = 256
MLP_SLABS = 2
MLP_BWD_SLABS = 8
EPS = 1e-6
LANES = 128
VMEM_LIMIT = 56 * 1024 * 1024
ADAM_LR, ADAM_B1, ADAM_B2, ADAM_EPS, ADAM_WD, ADAM_STEP = 0.001, 0.9, 0.999, 1e-08, 0.01, 10
POOL_WINDOWS = (2, 4, 8, 16)

C_POOL, C_GB, C_GC, C_HH, C_Z, C_XS, C_BM, C_CM, C_S5 = range(9)
C_DT128 = DT_COL // LANES

MESH = pl.DeviceIdType.MESH
ANY = pl.BlockSpec(memory_space=pl.ANY)
VMEM = pl.BlockSpec(memory_space=pltpu.VMEM)


def _dot(a, b):
    return jnp.dot(a, b, preferred_element_type=f32)


def _dot_nt(a, b):
    return lax.dot_general(a, b, (((1,), (1,)), ((), ())), preferred_element_type=f32)


def _dot_tn(a, b):
    return lax.dot_general(a, b, (((0,), (0,)), ((), ())), preferred_element_type=f32)


def _dot_exact(a, b):
    return jnp.dot(a, b, preferred_element_type=f32, precision=lax.Precision.HIGHEST)


def _b(x):
    return x.astype(bf16)


def _silu(x):
    return x * jax.nn.sigmoid(x)


def _dsilu(x):
    s = jax.nn.sigmoid(x)
    return s * (1.0 + x * (1.0 - s))


def _softplus(x):
    return jnp.maximum(x, 0.0) + jnp.log1p(jnp.exp(-jnp.abs(x)))


_GELU_K = 0.7978845608028654
_GELU_C = 0.044715


def _gelu(x):
    return 0.5 * x * (1.0 + jnp.tanh(_GELU_K * (x + _GELU_C * x * x * x)))


def _dgelu(x):
    th = jnp.tanh(_GELU_K * (x + _GELU_C * x * x * x))
    return 0.5 * (1.0 + th) + 0.5 * x * (1.0 - th * th) * _GELU_K * (1.0 + 3.0 * _GELU_C * x * x)


def _rms(h):
    r = lax.rsqrt(jnp.mean(h * h, axis=-1, keepdims=True) + EPS)
    return h * r, r


def _rms_bwd(dn, n, r):
    return r * (dn - n * jnp.mean(dn * n, axis=-1, keepdims=True))


def _colsum(x):
    return jnp.sum(x, axis=0, keepdims=True)


def _params(sem=None):
    return pltpu.CompilerParams(dimension_semantics=sem, vmem_limit_bytes=VMEM_LIMIT)


def _full(shape):
    return pl.BlockSpec(shape, lambda *_: (0,) * len(shape))


def _acc(ref, val):
    @pl.when(pl.program_id(0) == 0)
    def _():
        ref[...] = val

    @pl.when(pl.program_id(0) != 0)
    def _():
        ref[...] += val


def _me():
    return lax.axis_index("x"), lax.axis_index("y"), lax.axis_index("c")


def _my_index():
    x, y, c = _me()
    return 4 * x + 2 * y + c


def _coords(p):
    return (p // 4, (p // 2) % 2, p % 2)


class _Scatter:
    def __init__(self, srcs):
        self.srcs = list(srcs)
        self.n = len(self.srcs)
        self.out_shape = [jax.ShapeDtypeStruct(s.shape, s.dtype) for s in self.srcs]
        self.scratch = [pltpu.SemaphoreType.DMA((self.n, N_DEV)), pltpu.SemaphoreType.DMA((self.n, N_DEV)),
                        pltpu.SemaphoreType.DMA((self.n,))]

    def _remote(self, xin, xout, sems, t, k, me, to):
        return pltpu.make_async_remote_copy(
            src_ref=xin[t].at[to], dst_ref=xout[t].at[me], send_sem=sems[0].at[t, k], recv_sem=sems[1].at[t, k],
            device_id=_coords(to), device_id_type=MESH)

    def start(self, xin, xout, sems):
        me = _my_index()
        for t in range(self.n):
            pltpu.make_async_copy(xin[t].at[me], xout[t].at[me], sems[2].at[t]).start()
            for k in range(1, N_DEV):
                self._remote(xin, xout, sems, t, k, me, (me + k) % N_DEV).start()

    def forward(self, xin, xout, sems):
        pass

    def wait(self, xin, xout, sems):
        me = _my_index()
        for t in range(self.n):
            for k in range(1, N_DEV):
                src = (me + N_DEV - k) % N_DEV
                pltpu.make_async_remote_copy(
                    src_ref=xin[t].at[src], dst_ref=xout[t].at[src], send_sem=sems[0].at[t, k],
                    recv_sem=sems[1].at[t, k], device_id=_coords(src), device_id_type=MESH).wait_recv()
        for t in range(self.n):
            for k in range(1, N_DEV):
                self._remote(xin, xout, sems, t, k, me, (me + k) % N_DEV).wait_send()
            pltpu.make_async_copy(xin[t].at[me], xout[t].at[me], sems[2].at[t]).wait()


class _ChipGather:
    def __init__(self, srcs):
        self.srcs = list(srcs)
        self.n = len(self.srcs)
        self.out_shape = [jax.ShapeDtypeStruct((N_DEV,) + s.shape, s.dtype) for s in self.srcs]
        self.scratch = [pltpu.SemaphoreType.DMA((self.n, 7)), pltpu.SemaphoreType.DMA((self.n, 7)),
                        pltpu.SemaphoreType.DMA((self.n,))]

    @staticmethod
    def _places():
        x, y, c = _me()
        chips = [(1 - x, y), (x, 1 - y), (1 - x, 1 - y)]
        return (x, y, c), (x, y, 1 - c), chips

    @staticmethod
    def _slab(ref, dev):
        return ref.at[4 * dev[0] + 2 * dev[1] + dev[2]]

    def _copy(self, xin, xout, sems, t, k, block, to, src=None):
        return pltpu.make_async_remote_copy(
            src_ref=self._slab(xout[t], block) if src is None else src, dst_ref=self._slab(xout[t], block),
            send_sem=sems[0].at[t, k], recv_sem=sems[1].at[t, k], device_id=to, device_id_type=MESH)

    def start(self, xin, xout, sems):
        me, sibling, chips = self._places()
        for t in range(self.n):
            pltpu.make_async_copy(xin[t], self._slab(xout[t], me), sems[2].at[t]).start()
            self._copy(xin, xout, sems, t, 0, me, sibling, src=xin[t]).start()
            for j, chip in enumerate(chips):
                self._copy(xin, xout, sems, t, 1 + j, me, (*chip, me[2]), src=xin[t]).start()

    def forward(self, xin, xout, sems):
        me, sibling, chips = self._places()
        for t in range(self.n):
            for j, chip in enumerate(chips):
                self._copy(xin, xout, sems, t, 1 + j, (*chip, me[2]), me).wait_recv()
                self._copy(xin, xout, sems, t, 4 + j, (*chip, me[2]), sibling).start()

    def wait(self, xin, xout, sems):
        me, sibling, chips = self._places()
        for t in range(self.n):
            self._copy(xin, xout, sems, t, 0, sibling, me).wait_recv()
            for j, chip in enumerate(chips):
                self._copy(xin, xout, sems, t, 4 + j, (*chip, 1 - me[2]), me).wait_recv()
        for t in range(self.n):
            self._copy(xin, xout, sems, t, 0, me, sibling, src=xin[t]).wait_send()
            for j, chip in enumerate(chips):
                self._copy(xin, xout, sems, t, 1 + j, me, (*chip, me[2]), src=xin[t]).wait_send()
                self._copy(xin, xout, sems, t, 4 + j, (*chip, me[2]), sibling).wait_send()
            pltpu.make_async_copy(xin[t], self._slab(xout[t], me), sems[2].at[t]).wait()


def _call(body, *, name, grid, in_specs, out_specs, out_shape, args, semantics, scratch_shapes=(), xchg=None):
    if xchg is None:
        outs = pl.pallas_call(body, name=name, grid=grid, in_specs=in_specs, out_specs=out_specs, out_shape=out_shape,
                              scratch_shapes=list(scratch_shapes), compiler_params=_params(semantics))(*args)
        return outs, ()
    n_in, n_out, n_scr, n = len(in_specs), len(out_specs), len(scratch_shapes), xchg.n

    def carried(*refs):
        ins, xin = refs[:n_in], refs[n_in:n_in + n]
        outs, xout = refs[n_in + n:n_in + n + n_out], refs[n_in + n + n_out:n_in + 2 * n + n_out]
        scr, sems = refs[n_in + 2 * n + n_out:n_in + 2 * n + n_out + n_scr], refs[n_in + 2 * n + n_out + n_scr:]
        step = pl.program_id(0)
        for d in range(1, len(grid)):
            step = step * grid[d] + pl.program_id(d)
        n_steps = functools.reduce(lambda a, b: a * b, grid)

        @pl.when(step == 0)
        def _():
            xchg.start(xin, xout, sems)

        @pl.when(step == (2 * n_steps) // 3)
        def _():
            xchg.forward(xin, xout, sems)

        body(*ins, *outs, *scr)

        @pl.when(step == n_steps - 1)
        def _():
            xchg.wait(xin, xout, sems)

    res = pl.pallas_call(
        carried, name=name, grid=grid, in_specs=list(in_specs) + [ANY] * n, out_specs=list(out_specs) + [ANY] * n,
        out_shape=list(out_shape) + xchg.out_shape, scratch_shapes=list(scratch_shapes) + xchg.scratch,
        compiler_params=_params(("arbitrary",) * len(grid)))(*args, *xchg.srcs)
    return res[:n_out], tuple(res[n_out:])


def _exchange_alone(xchg, name):
    def body(*refs):
        xin, xout, sems = refs[:xchg.n], refs[xchg.n:2 * xchg.n], refs[2 * xchg.n:]
        xchg.start(xin, xout, sems)
        xchg.forward(xin, xout, sems)
        xchg.wait(xin, xout, sems)

    return pl.pallas_call(body, name=name, out_shape=xchg.out_shape, in_specs=[ANY] * xchg.n, out_specs=[ANY] * xchg.n,
                          scratch_shapes=xchg.scratch)(*xchg.srcs)


def _cast_shards(shards):
    n = len(shards)

    def body(*refs):
        for i, o in zip(refs[:n], refs[n:]):
            o[...] = i[...].astype(bf16)

    return pl.pallas_call(body, name="cast_shards", out_shape=[jax.ShapeDtypeStruct(s.shape, bf16) for s in shards],
                          in_specs=[VMEM] * n, out_specs=[VMEM] * n, compiler_params=_params())(*shards)


def _allreduce_rows(v):
    r = v.shape[0]
    rp = r // N_DEV

    def body(v_ref, o_ref, parts, sums, send1, recv1, send2, recv2):
        me = _my_index()

        def piece(ref, d):
            return ref.at[pl.ds(pl.multiple_of(d * rp, 8), rp), :]

        def copy1(k, src_dev, to):
            return pltpu.make_async_remote_copy(src_ref=piece(v_ref, to), dst_ref=parts.at[src_dev], send_sem=send1.at[k],
                                                recv_sem=recv1.at[k], device_id=_coords(to), device_id_type=MESH)

        def copy2(k, owner, to):
            return pltpu.make_async_remote_copy(src_ref=sums, dst_ref=piece(o_ref, owner), send_sem=send2.at[k],
                                                recv_sem=recv2.at[k], device_id=_coords(to), device_id_type=MESH)

        for k in range(1, N_DEV):
            copy1(k, me, (me + k) % N_DEV).start()
        parts[me] = v_ref[pl.ds(pl.multiple_of(me * rp, 8), rp), :]
        for k in range(1, N_DEV):
            copy1(k, (me + N_DEV - k) % N_DEV, me).wait_recv()
        total = parts[0]
        for s in range(1, N_DEV):
            total = total + parts[s]
        sums[...] = total
        o_ref[pl.ds(pl.multiple_of(me * rp, 8), rp), :] = total
        for k in range(1, N_DEV):
            copy2(k, me, (me + k) % N_DEV).start()
        for k in range(1, N_DEV):
            copy2(k, (me + N_DEV - k) % N_DEV, me).wait_recv()
        for k in range(1, N_DEV):
            copy1(k, me, (me + k) % N_DEV).wait_send()
            copy2(k, me, (me + k) % N_DEV).wait_send()

    return pl.pallas_call(
        body, name="allreduce_small_grads", out_shape=jax.ShapeDtypeStruct(v.shape, v.dtype),
        in_specs=[VMEM], out_specs=VMEM,
        scratch_shapes=[pltpu.VMEM((N_DEV, rp, LANES), f32), pltpu.VMEM((rp, LANES), f32)]
        + [pltpu.SemaphoreType.DMA((N_DEV,))] * 4,
        compiler_params=_params(),
    )(v)


def _all_to_all_rows(v, name):
    def body(v_ref, o_ref, send_sems, recv_sems):
        me = _my_index()
        o_ref[me] = v_ref[me]
        sends = []
        for k in range(1, N_DEV):
            peer = (me + k) % N_DEV
            rc = pltpu.make_async_remote_copy(src_ref=v_ref.at[peer], dst_ref=o_ref.at[me], send_sem=send_sems.at[k],
                                              recv_sem=recv_sems.at[k], device_id=_coords(peer), device_id_type=MESH)
            rc.start()
            sends.append(rc)
        for k in range(1, N_DEV):
            src = (me + N_DEV - k) % N_DEV
            pltpu.make_async_remote_copy(src_ref=v_ref.at[src], dst_ref=o_ref.at[src], send_sem=send_sems.at[k],
                                         recv_sem=recv_sems.at[k], device_id=_coords(src), device_id_type=MESH).wait_recv()
        for rc in sends:
            rc.wait_send()

    return pl.pallas_call(
        body, name=name, out_shape=jax.ShapeDtypeStruct(v.shape, v.dtype),
        in_specs=[VMEM], out_specs=VMEM,
        scratch_shapes=[pltpu.SemaphoreType.DMA((N_DEV,)), pltpu.SemaphoreType.DMA((N_DEV,))],
    )(v)


def _ada_forward(c_all, ada_w, ada_b_cols):
    def body(c_ref, w_ref, b_ref, cond_ref, o_ref):
        cond = _silu(c_ref[...])
        cond_ref[...] = cond
        for l in range(2):
            o_ref[l] = _dot(_b(cond), _b(w_ref[l])) + b_ref[l]

    return pl.pallas_call(
        body, name="ada_forward",
        out_shape=[jax.ShapeDtypeStruct((N_DEV, D_MODEL), f32), jax.ShapeDtypeStruct((2, N_DEV, 768), f32)],
        in_specs=[VMEM] * 3, out_specs=[VMEM] * 2, compiler_params=_params(),
    )(c_all, ada_w, ada_b_cols)


def _ada_backward(cond, dmod_rows):
    def body(c_ref, d_ref, o_ref):
        cb = _b(c_ref[...])
        for l in range(2):
            o_ref[l] = _dot_tn(cb, _b(d_ref[l]))

    return pl.pallas_call(
        body, name="ada_backward", out_shape=jax.ShapeDtypeStruct((2, D_MODEL, 768), f32),
        in_specs=[VMEM] * 2, out_specs=VMEM, compiler_params=_params(),
    )(cond, dmod_rows)


def _inproj_fwd(h, norm_w, sc, sh, w_in, tb, xchg=None):
    t = h.shape[0]

    def body(h_ref, nw_ref, sc_ref, sh_ref, w_ref, proj_ref, u_ref):
        n, _ = _rms(h_ref[...])
        u = _b(n * nw_ref[...] * (1.0 + sc_ref[...]) + sh_ref[...])
        u_ref[...] = u
        proj_ref[...] = _dot(u, w_ref[...])

    row = pl.BlockSpec((tb, D_MODEL), lambda i: (i, 0))
    vec = _full((1, D_MODEL))
    return _call(
        body, name="inproj_fwd", grid=(t // tb,),
        out_shape=[jax.ShapeDtypeStruct((t, P_IN), f32), jax.ShapeDtypeStruct((t, D_MODEL), bf16)],
        in_specs=[row, vec, vec, vec, _full((D_MODEL, P_IN))],
        out_specs=[pl.BlockSpec((tb, P_IN), lambda i: (i, 0)), row],
        semantics=("parallel",), args=(h, norm_w, sc, sh, w_in), xchg=xchg)


def _inproj_bwd(dparts, dh_res, h, norm_w, sc, sh, w_in, tb, xchg=None):
    t = h.shape[0]

    def body(*refs):
        parts = refs[:10]
        dres_ref, h_ref, nw_ref, sc_ref, sh_ref, w_ref = refs[10:16]
        dh_ref, dsh_ref, dsc_ref, dnw_ref = refs[16:]
        dproj = jnp.concatenate([p[...] for p in parts], axis=1)
        du = _dot_nt(dproj, w_ref[...])
        n, r = _rms(h_ref[...])
        nw = nw_ref[...]
        gain = 1.0 + sc_ref[...]
        _acc(dsh_ref, _colsum(du))
        _acc(dsc_ref, _colsum(du * n * nw))
        _acc(dnw_ref, _colsum(du * gain * n))
        dh_ref[...] = dres_ref[...] + _rms_bwd(du * nw * gain, n, r)

    row = pl.BlockSpec((tb, D_MODEL), lambda i: (i, 0))
    vec = _full((1, D_MODEL))
    part_specs = [pl.BlockSpec((tb, GROUP_W), lambda i: (i, 0))] * 9 + [pl.BlockSpec((tb, LANES), lambda i: (i, 0))]
    return _call(
        body, name="inproj_bwd", grid=(t // tb,),
        out_shape=[jax.ShapeDtypeStruct((t, D_MODEL), f32)] + [jax.ShapeDtypeStruct((1, D_MODEL), f32)] * 3,
        in_specs=part_specs + [row, row, vec, vec, vec,
                               pl.BlockSpec((D_MODEL, P_IN), lambda i: (0, 0), pipeline_mode=pl.Buffered(1))],
        out_specs=[row, vec, vec, vec],
        semantics=("arbitrary",), xchg=xchg, args=(*dparts, dh_res, h, norm_w, sc, sh, w_in))


def _wgrad(a, b, n_blocks, name, tm, tk=512):
    t, m = a.shape
    nb = b.shape[1] // n_blocks
    tk = min(tk, t)
    nk = t // tk

    def body(a_ref, b_ref, o_ref, acc_ref):
        k = pl.program_id(2)
        p = _dot_tn(a_ref[...], b_ref[...])

        @pl.when(k == 0)
        def _():
            acc_ref[...] = p

        @pl.when(k != 0)
        def _():
            acc_ref[...] += p

        @pl.when(k == nk - 1)
        def _():
            o_ref[0] = acc_ref[...].astype(o_ref.dtype)

    return pl.pallas_call(
        body, name=name, grid=(m // tm, n_blocks, nk),
        out_shape=jax.ShapeDtypeStruct((n_blocks, m, nb), bf16),
        in_specs=[pl.BlockSpec((tk, tm), lambda i, j, k: (k, i)), pl.BlockSpec((tk, nb), lambda i, j, k: (k, j))],
        out_specs=pl.BlockSpec((1, tm, nb), lambda i, j, k: (j, i, 0)),
        scratch_shapes=[pltpu.VMEM((tm, nb), f32)],
        compiler_params=_params(("parallel", "parallel", "arbitrary")),
    )(a, b)


def _wgrad_parts(a, parts, name, tm, tk):
    t, m = a.shape
    n = sum(p.shape[1] for p in parts)
    n_parts = len(parts)
    tk = min(tk, t)
    nk = t // tk

    def body(*refs):
        a_ref, part_refs, o_ref, acc_ref = refs[0], refs[1:1 + n_parts], refs[1 + n_parts], refs[2 + n_parts]
        k = pl.program_id(1)
        p = _dot_tn(a_ref[...], jnp.concatenate([r[...] for r in part_refs], axis=1))

        @pl.when(k == 0)
        def _():
            acc_ref[...] = p

        @pl.when(k != 0)
        def _():
            acc_ref[...] += p

        @pl.when(k == nk - 1)
        def _():
            o_ref[...] = acc_ref[...].astype(o_ref.dtype)

    return pl.pallas_call(
        body, name=name, grid=(m // tm, nk),
        out_shape=jax.ShapeDtypeStruct((m, n), bf16),
        in_specs=[pl.BlockSpec((tk, tm), lambda i, k: (k, i))]
        + [pl.BlockSpec((tk, p.shape[1]), lambda i, k: (k, 0)) for p in parts],
        out_specs=pl.BlockSpec((tm, n), lambda i, k: (i, 0)),
        scratch_shapes=[pltpu.VMEM((tm, n), f32)],
        compiler_params=_params(("parallel", "arbitrary")),
    )(a, *parts)


def _pool_counts(rows, t0):
    tpos = (lax.broadcasted_iota(jnp.int32, (rows, GROUP_W), 0) + t0 + 1).astype(f32)
    grp = lax.broadcasted_iota(jnp.int32, (rows, GROUP_W), 1) // 64
    win = jnp.where(grp == 0, 2.0, jnp.where(grp == 1, 4.0, jnp.where(grp == 2, 8.0, 16.0)))
    return jnp.minimum(tpos, win), grp


def _pool_select(grp, l1, l2, l3, l4):
    return jnp.where(grp == 0, l1, jnp.where(grp == 1, l2, jnp.where(grp == 2, l3, l4)))


def _pool_means(v, halo, t0):
    tb = v.shape[0]
    ext = jnp.concatenate([halo, v], axis=0)
    n = tb + 16
    s1 = ext[1:n] + ext[0:n - 1]
    s2 = s1[2:n - 1] + s1[0:n - 3]
    s3 = s2[4:n - 3] + s2[0:n - 7]
    s4 = s3[8:n - 7] + s3[0:n - 15]
    cnt, grp = _pool_counts(tb, t0)
    wsum = _pool_select(grp, s1[15:15 + tb], s2[13:13 + tb], s3[9:9 + tb], s4[1:1 + tb])
    return wsum / cnt - v


def _pool_fwd(proj, pw_bd, scale, tb):
    t = proj.shape[0]

    def body(v_ref, vh_ref, pw_ref, sc_ref, o_ref):
        i = pl.program_id(0)
        halo = jnp.where(i > 0, vh_ref[...], 0.0)
        p = _pool_means(v_ref[...], halo, i * tb)
        o_ref[...] = _b(_dot(_b(p), _b(pw_ref[...])) * sc_ref[...])

    return pl.pallas_call(
        body, name="pool_fwd", grid=(t // tb,),
        out_shape=jax.ShapeDtypeStruct((t, GROUP_W), bf16),
        in_specs=[pl.BlockSpec((tb, GROUP_W), lambda i: (i, C_POOL)),
                  pl.BlockSpec((16, GROUP_W), lambda i: (jnp.maximum(i * (tb // 16) - 1, 0), C_POOL)),
                  _full((GROUP_W, GROUP_W)), _full((1, GROUP_W))],
        out_specs=pl.BlockSpec((tb, GROUP_W), lambda i: (i, 0)),
        compiler_params=_params(("parallel",)),
    )(proj, proj, pw_bd, scale)


def _pool_bwd(proj, dy, pw_bd, scale, tb):
    t = proj.shape[0]
    nt = t // tb
    last16 = t // 16 - 1

    def body(v_ref, vh_ref, dy_ref, dyh_ref, pw_ref, sc_ref, dv_ref, dpw_ref, dsc_ref):
        i = pl.program_id(0)
        halo = jnp.where(i > 0, vh_ref[...], 0.0)
        p = _pool_means(v_ref[...], halo, i * tb)
        pw = _b(pw_ref[...])
        sc = sc_ref[...]
        dy = dy_ref[...]
        ypre = _dot(_b(p), pw)
        _acc(dsc_ref, _colsum(dy * ypre))
        dys = _b(dy * sc)
        _acc(dpw_ref, _dot_tn(_b(p), dys))
        dp = _dot_nt(dys, pw)
        dph = _dot_nt(_b(jnp.where(i < nt - 1, dyh_ref[...], 0.0) * sc), pw)
        cnt, grp = _pool_counts(tb, i * tb)
        cnth, _ = _pool_counts(16, (i + 1) * tb)
        ext = jnp.concatenate([dp / cnt, dph / cnth], axis=0)
        n = tb + 16
        f1 = ext[0:n - 1] + ext[1:n]
        f2 = f1[0:n - 3] + f1[2:n - 1]
        f3 = f2[0:n - 7] + f2[4:n - 3]
        f4 = f3[0:n - 15] + f3[8:n - 7]
        dv_ref[...] = _b(_pool_select(grp, f1[0:tb], f2[0:tb], f3[0:tb], f4[0:tb]) - dp)

    return pl.pallas_call(
        body, name="pool_bwd", grid=(nt,),
        out_shape=[jax.ShapeDtypeStruct((t, GROUP_W), bf16), jax.ShapeDtypeStruct((GROUP_W, GROUP_W), f32),
                   jax.ShapeDtypeStruct((1, GROUP_W), f32)],
        in_specs=[pl.BlockSpec((tb, GROUP_W), lambda i: (i, C_POOL)),
                  pl.BlockSpec((16, GROUP_W), lambda i: (jnp.maximum(i * (tb // 16) - 1, 0), C_POOL)),
                  pl.BlockSpec((tb, GROUP_W), lambda i: (i, 0)),
                  pl.BlockSpec((16, GROUP_W), lambda i: (jnp.minimum((i + 1) * (tb // 16), last16), 0)),
                  _full((GROUP_W, GROUP_W)), _full((1, GROUP_W))],
        out_specs=[pl.BlockSpec((tb, GROUP_W), lambda i: (i, 0)), _full((GROUP_W, GROUP_W)), _full((1, GROUP_W))],
        compiler_params=_params(("arbitrary",)),
    )(proj, proj, dy, dy, pw_bd, scale)


def _sconv_fwd(proj, w, tb):
    t = proj.shape[0]

    def body(gb_ref, gc_ref, hh_ref, gch_ref, hhh_ref, w_ref, o_ref):
        i = pl.program_id(0)
        q = gc_ref[...] * hh_ref[...]
        qh = jnp.where(i > 0, gch_ref[...] * hhh_ref[...], 0.0)
        ext = jnp.concatenate([qh, q], axis=0)
        w = w_ref[...]
        conv = w[0:1] * ext[6:6 + tb] + w[1:2] * ext[7:7 + tb] + w[2:3] * ext[8:8 + tb]
        o_ref[...] = _b(gb_ref[...] * conv)

    def col(c):
        return pl.BlockSpec((tb, GROUP_W), lambda i: (i, c))

    def prev(c):
        return pl.BlockSpec((8, GROUP_W), lambda i: (jnp.maximum(i * (tb // 8) - 1, 0), c))

    return pl.pallas_call(
        body, name="sconv_fwd", grid=(t // tb,),
        out_shape=jax.ShapeDtypeStruct((t, GROUP_W), bf16),
        in_specs=[col(C_GB), col(C_GC), col(C_HH), prev(C_GC), prev(C_HH), _full((8, GROUP_W))],
        out_specs=pl.BlockSpec((tb, GROUP_W), lambda i: (i, 0)),
        compiler_params=_params(("parallel",)),
    )(proj, proj, proj, proj, proj, w)


def _sconv_bwd(proj, dy, w, tb):
    t = proj.shape[0]
    nt = t // tb
    last8 = t // 8 - 1

    def body(gb_ref, gc_ref, hh_ref, gch_ref, hhh_ref, gbn_ref, dy_ref, dyn_ref, w_ref, dgb_ref, dgc_ref, dhh_ref, dw_ref):
        i = pl.program_id(0)
        gc, hh, gb, dy = gc_ref[...], hh_ref[...], gb_ref[...], dy_ref[...]
        q = gc * hh
        qh = jnp.where(i > 0, gch_ref[...] * hhh_ref[...], 0.0)
        ext = jnp.concatenate([qh, q], axis=0)
        w = w_ref[...]
        conv = w[0:1] * ext[6:6 + tb] + w[1:2] * ext[7:7 + tb] + w[2:3] * ext[8:8 + tb]
        dgb_ref[...] = _b(dy * conv)
        e = dy * gb
        en = jnp.where(i < nt - 1, dyn_ref[...] * gbn_ref[...], 0.0)
        exte = jnp.concatenate([e, en], axis=0)
        dq = w[2:3] * exte[0:tb] + w[1:2] * exte[1:1 + tb] + w[0:1] * exte[2:2 + tb]
        dgc_ref[...] = _b(dq * hh)
        dhh_ref[...] = _b(dq * gc)
        dw = jnp.concatenate([_colsum(e * ext[6:6 + tb]), _colsum(e * ext[7:7 + tb]), _colsum(e * ext[8:8 + tb]),
                              jnp.zeros((5, GROUP_W), f32)], axis=0)
        _acc(dw_ref, dw)

    def col(c):
        return pl.BlockSpec((tb, GROUP_W), lambda i: (i, c))

    def prev(c):
        return pl.BlockSpec((8, GROUP_W), lambda i: (jnp.maximum(i * (tb // 8) - 1, 0), c))

    def nxt(c):
        return pl.BlockSpec((8, GROUP_W), lambda i: (jnp.minimum((i + 1) * (tb // 8), last8), c))

    out = pl.BlockSpec((tb, GROUP_W), lambda i: (i, 0))
    return pl.pallas_call(
        body, name="sconv_bwd", grid=(nt,),
        out_shape=[jax.ShapeDtypeStruct((t, GROUP_W), bf16)] * 3 + [jax.ShapeDtypeStruct((8, GROUP_W), f32)],
        in_specs=[col(C_GB), col(C_GC), col(C_HH), prev(C_GC), prev(C_HH), nxt(C_GB), col(0), nxt(0), _full((8, GROUP_W))],
        out_specs=[out, out, out, _full((8, GROUP_W))],
        compiler_params=_params(("arbitrary",)),
    )(proj, proj, proj, proj, proj, proj, dy, dy, w)


def _conv4(xr, halo, w, bias):
    tb = xr.shape[0]
    ext = jnp.concatenate([halo, xr], axis=0)
    pre = w[0:1] * ext[5:5 + tb] + w[1:2] * ext[6:6 + tb] + w[2:3] * ext[7:7 + tb] + w[3:4] * ext[8:8 + tb] + bias
    return pre, ext


def _tri():
    r = lax.broadcasted_iota(jnp.int32, (SSD_CHUNK, SSD_CHUNK), 0)
    c = lax.broadcasted_iota(jnp.int32, (SSD_CHUNK, SSD_CHUNK), 1)
    return r >= c


def _lane_pick(vals):
    rows = vals[0].shape[0]
    lane = lax.broadcasted_iota(jnp.int32, (rows, LANES), 1)
    out = jnp.zeros((rows, LANES), f32)
    for h, v in enumerate(vals):
        out = jnp.where(lane == h, v, out)
    return out


def _ssd_fwd(proj, conv_w, conv_b, dt_bias, a_log, d_cols, tb, xchg=None):
    t = proj.shape[0]
    cpt = tb // SSD_CHUNK

    def body(z_ref, xs_ref, bm_ref, cm_ref, xsh_ref, bmh_ref, cmh_ref, dt_ref, cw_ref, cb_ref, dtb_ref, al_ref, dk_ref,
             o_ref, y_ref, st_ref, state):
        i = pl.program_id(0)

        @pl.when(i == 0)
        def _():
            state[...] = jnp.zeros_like(state)

        cw, cb = cw_ref[...], cb_ref[...]
        acts = []
        for j, (r, hr) in enumerate(((xs_ref, xsh_ref), (bm_ref, bmh_ref), (cm_ref, cmh_ref))):
            halo = jnp.where(i > 0, hr[...], 0.0)
            pre, _ = _conv4(r[...], halo, cw[:, j * 256:(j + 1) * 256], cb[:, j * 256:(j + 1) * 256])
            acts.append(_silu(pre))
        xs, bm, cm = acts
        dt = _softplus(dt_ref[...] + dtb_ref[...])
        a = -jnp.exp(al_ref[...])
        adt = dt * a
        tri = _tri()
        trif = tri.astype(f32)
        dk = dk_ref[...]
        for c in range(cpt):
            rows = slice(c * SSD_CHUNK, (c + 1) * SSD_CHUNK)
            acol = _dot_exact(trif, adt[rows])
            arow = acol.T
            dt_c = dt[rows]
            ys = []
            rowi = lax.broadcasted_iota(jnp.int32, (SSD_CHUNK, 1), 0)
            first = lax.broadcasted_iota(jnp.int32, (SSD_CHUNK, SSD_CHUNK), 1) < SSD_P
            for g in range(SSD_HEADS // 2):
                cols = slice(g * 128, (g + 1) * 128)
                cg, bg = _b(cm[rows, cols]), _b(bm[rows, cols])
                xg = xs[rows, cols]
                heads = (2 * g, 2 * g + 1)
                ac = [acol[:, h:h + 1] for h in heads]
                alast = [v[SSD_CHUNK - 1:SSD_CHUNK] for v in ac]
                dtw = jnp.where(first, dt_c[:, heads[0]:heads[0] + 1], dt_c[:, heads[1]:heads[1] + 1])
                eaw = jnp.where(first, jnp.exp(ac[0]), jnp.exp(ac[1]))
                wdw = jnp.where(first, jnp.exp(alast[0] - ac[0]), jnp.exp(alast[1] - ac[1]))
                xdt = xg * dtw
                xb = _b(xdt)
                gmat = _dot_nt(cg, bg)
                ydiag = []
                for k, h in enumerate(heads):
                    lm = jnp.exp(jnp.where(tri, ac[k] - arow[h:h + 1, :], -jnp.inf))
                    ydiag.append(_dot(_b(gmat * lm), xb[:, k * SSD_P:(k + 1) * SSD_P]))
                s_in = state[g]
                st_ref[c, g] = s_in
                ys.append(jnp.concatenate(ydiag, axis=1) + eaw * _dot_nt(cg, _b(s_in)) + xg * dk[:, cols])
                state[g] = jnp.where(rowi < SSD_P, jnp.exp(alast[0]), jnp.exp(alast[1])) * s_in + _dot_tn(_b(xdt * wdw), bg)
            yc = jnp.concatenate(ys, axis=1)
            y_ref[rows, :] = yc
            o_ref[rows, :] = _b(yc * _silu(z_ref[rows, :]))

    def col(c):
        return pl.BlockSpec((tb, GROUP_W), lambda i: (i, c))

    def prev(c):
        return pl.BlockSpec((8, GROUP_W), lambda i: (jnp.maximum(i * (tb // 8) - 1, 0), c))

    out = pl.BlockSpec((tb, GROUP_W), lambda i: (i, 0))
    return _call(
        body, name="ssd_fwd", grid=(t // tb,),
        out_shape=[jax.ShapeDtypeStruct((t, GROUP_W), bf16), jax.ShapeDtypeStruct((t, GROUP_W), f32),
                   jax.ShapeDtypeStruct((t // SSD_CHUNK, 2, 128, 128), f32)],
        in_specs=[col(C_Z), col(C_XS), col(C_BM), col(C_CM), prev(C_XS), prev(C_BM), prev(C_CM),
                  pl.BlockSpec((tb, LANES), lambda i: (i, C_DT128)),
                  _full((8, 768)), _full((1, 768)), _full((1, LANES)), _full((1, LANES)), _full((1, GROUP_W))],
        out_specs=[out, out, pl.BlockSpec((cpt, 2, 128, 128), lambda i: (i, 0, 0, 0))],
        scratch_shapes=[pltpu.VMEM((2, 128, 128), f32)],
        semantics=("arbitrary",), xchg=xchg,
        args=(proj, proj, proj, proj, proj, proj, proj, proj, conv_w, conv_b, dt_bias, a_log, d_cols))


def _ssd_bwd(proj, dyc, y_pre, states, conv_w, conv_b, dt_bias, a_log, d_cols, tb, xchg=None):
    t = proj.shape[0]
    nt = t // tb
    cpt = tb // SSD_CHUNK

    def body(z_ref, xs_ref, bm_ref, cm_ref, xsh_ref, bmh_ref, cmh_ref, dt_ref, dy_ref, yp_ref, st_ref,
             cw_ref, cb_ref, dtb_ref, al_ref, dk_ref,
             dz_ref, dxs_ref, dbm_ref, dcm_ref, ddt_ref, dcw_ref, dcb_ref, ddtb_ref, dal_ref, ddk_ref,
             dstate, carry):
        i = pl.program_id(0)
        ti = nt - 1 - i

        @pl.when(i == 0)
        def _():
            dstate[...] = jnp.zeros_like(dstate)
            carry[...] = jnp.zeros_like(carry)

        cw, cb = cw_ref[...], cb_ref[...]
        pres, exts, acts = [], [], []
        for j, (r, hr) in enumerate(((xs_ref, xsh_ref), (bm_ref, bmh_ref), (cm_ref, cmh_ref))):
            halo = jnp.where(ti > 0, hr[...], 0.0)
            pre, ext = _conv4(r[...], halo, cw[:, j * 256:(j + 1) * 256], cb[:, j * 256:(j + 1) * 256])
            pres.append(pre)
            exts.append(ext)
            acts.append(_silu(pre))
        xs, bm, cm = acts
        raw = dt_ref[...] + dtb_ref[...]
        dt = _softplus(raw)
        a = -jnp.exp(al_ref[...])
        adt = dt * a
        tri = _tri()
        trif = tri.astype(f32)
        dk = dk_ref[...]
        z = z_ref[...]
        dyc = dy_ref[...]
        dz_ref[...] = _b(dyc * yp_ref[...] * _dsilu(z))
        dy_all = dyc * _silu(z)
        lane = lax.broadcasted_iota(jnp.int32, (1, LANES), 1)
        ddk_acc = jnp.zeros((1, LANES), f32)
        dal_acc = jnp.zeros((1, LANES), f32)
        dxs_c, dbm_c, dcm_c, ddt_c = [None] * cpt, [None] * cpt, [None] * cpt, [None] * cpt
        for c in reversed(range(cpt)):
            rows = slice(c * SSD_CHUNK, (c + 1) * SSD_CHUNK)
            acol = _dot_exact(trif, adt[rows])
            arow = acol.T
            dt_c = dt[rows]
            da_cols, da_rows, ddt_heads, dxs_groups, dbg, dcg = [], [], [], [], [], []
            rowi = lax.broadcasted_iota(jnp.int32, (SSD_CHUNK, 1), 0)
            first = lax.broadcasted_iota(jnp.int32, (SSD_CHUNK, SSD_CHUNK), 1) < SSD_P
            for g in range(SSD_HEADS // 2):
                cols = slice(g * 128, (g + 1) * 128)
                cgf, bgf = cm[rows, cols], bm[rows, cols]
                cg, bg = _b(cgf), _b(bgf)
                xg, dyg = xs[rows, cols], dy_all[rows, cols]
                s_in, dsn = st_ref[c, g], dstate[g]
                sb, dsnb = _b(s_in), _b(dsn)
                heads = (2 * g, 2 * g + 1)
                ac = [acol[:, h:h + 1] for h in heads]
                alast = [v[SSD_CHUNK - 1:SSD_CHUNK] for v in ac]
                el = [jnp.exp(v) for v in alast]
                dtw = jnp.where(first, dt_c[:, heads[0]:heads[0] + 1], dt_c[:, heads[1]:heads[1] + 1])
                eaw = jnp.where(first, jnp.exp(ac[0]), jnp.exp(ac[1]))
                wdw = jnp.where(first, jnp.exp(alast[0] - ac[0]), jnp.exp(alast[1] - ac[1]))
                xdt = xg * dtw
                xb, dyb = _b(xdt), _b(dyg)
                gmat = _dot_nt(cg, bg)
                dgs, dxh, da = None, [], []
                for k, h in enumerate(heads):
                    hc = slice(k * SSD_P, (k + 1) * SSD_P)
                    lm = jnp.exp(jnp.where(tri, ac[k] - arow[h:h + 1, :], -jnp.inf))
                    m = gmat * lm
                    dm = _dot_nt(dyb[:, hc], xb[:, hc])
                    dxh.append(_dot_tn(_b(m), dyb[:, hc]))
                    dgs = dm * lm if dgs is None else dgs + dm * lm
                    wm = dm * m
                    da.append(jnp.sum(wm, axis=1, keepdims=True))
                    da_rows.append(jnp.sum(wm, axis=0, keepdims=True))
                dgb = _b(dgs)
                dcg_g = _dot(dgb, bg)
                dbg_g = _dot_tn(dgb, cg)
                yoff = eaw * _dot_nt(cg, sb)
                dyoff = dyg * yoff
                dye = _b(dyg * eaw)
                dcg_g = dcg_g + _dot(dye, sb)
                ds_y = _dot_tn(dye, cg)
                u = _dot_nt(bg, dsnb)
                dx = jnp.concatenate(dxh, axis=1) + wdw * u
                dbg_g = dbg_g + _dot(_b(xdt * wdw), dsnb)
                xu = xdt * u * wdw
                ss = jnp.sum(dsn * s_in, axis=1, keepdims=True)
                dxx = dx * xg
                dyx = _colsum(dyg * xg)
                for k, h in enumerate(heads):
                    mine = first if k == 0 else jnp.logical_not(first)
                    dwv = jnp.sum(jnp.where(mine, xu, 0.0), axis=1, keepdims=True)
                    mine_rows = (rowi < SSD_P) if k == 0 else (rowi >= SSD_P)
                    dalast = jnp.sum(dwv, axis=0, keepdims=True) + el[k] * jnp.sum(jnp.where(mine_rows, ss, 0.0), axis=0, keepdims=True)
                    dah = da[k] + jnp.sum(jnp.where(mine, dyoff, 0.0), axis=1, keepdims=True) - dwv
                    da_cols.append(dah + jnp.where(rowi == SSD_CHUNK - 1, dalast, 0.0))
                    ddt_heads.append(jnp.sum(jnp.where(mine, dxx, 0.0), axis=1, keepdims=True))
                    ddk_acc = ddk_acc + jnp.where(lane == h, jnp.sum(jnp.where(mine[0:1], dyx, 0.0), axis=1, keepdims=True), 0.0)
                dstate[g] = jnp.where(rowi < SSD_P, el[0], el[1]) * dsn + ds_y
                dxs_groups.append(dx * dtw + dyg * dk[:, cols])
                dbg.append(dbg_g)
                dcg.append(dcg_g)
            da_blk = _lane_pick(da_cols)
            rowsel = lax.broadcasted_iota(jnp.int32, (SSD_CHUNK, SSD_CHUNK), 0)
            da_rows_blk = jnp.zeros((SSD_CHUNK, SSD_CHUNK), f32)
            for h in range(SSD_HEADS):
                da_rows_blk = jnp.where(rowsel == h, da_rows[h], da_rows_blk)
            da_blk = da_blk - da_rows_blk.T
            dadt = lax.dot_general(trif, da_blk, (((0,), (0,)), ((), ())), preferred_element_type=f32,
                                   precision=lax.Precision.HIGHEST)
            dal_acc = dal_acc + _colsum(dadt * dt_c)
            ddt_c[c] = dadt * a + _lane_pick(ddt_heads)
            dxs_c[c] = jnp.concatenate(dxs_groups, axis=1)
            dbm_c[c] = jnp.concatenate(dbg, axis=1)
            dcm_c[c] = jnp.concatenate(dcg, axis=1)
        ddt = jnp.concatenate(ddt_c, axis=0) if cpt > 1 else ddt_c[0]
        ddraw = jnp.where(lane < SSD_HEADS, ddt * jax.nn.sigmoid(raw), 0.0)
        ddt_ref[...] = _b(ddraw)
        _acc(ddtb_ref, _colsum(ddraw))
        _acc(dal_ref, jnp.where(lane < SSD_HEADS, dal_acc * a, 0.0))
        _acc(ddk_ref, ddk_acc)
        dcw_parts, dcb_parts = [], []
        for j, (dparts, out_ref) in enumerate(((dxs_c, dxs_ref), (dbm_c, dbm_ref), (dcm_c, dcm_ref))):
            dact = jnp.concatenate(dparts, axis=0) if cpt > 1 else dparts[0]
            dpre = dact * _dsilu(pres[j])
            w = cw[:, j * 256:(j + 1) * 256]
            ext = jnp.concatenate([dpre, carry[:, j * 256:(j + 1) * 256]], axis=0)
            out_ref[...] = _b(w[3:4] * ext[0:tb] + w[2:3] * ext[1:1 + tb] + w[1:2] * ext[2:2 + tb] + w[0:1] * ext[3:3 + tb])
            carry[:, j * 256:(j + 1) * 256] = dpre[0:8]
            xe = exts[j]
            dcw_parts.append(jnp.concatenate([_colsum(dpre * xe[5 + k:5 + k + tb]) for k in range(4)]
                                             + [jnp.zeros((4, GROUP_W), f32)], axis=0))
            dcb_parts.append(_colsum(dpre))
        _acc(dcw_ref, jnp.concatenate(dcw_parts, axis=1))
        _acc(dcb_ref, jnp.concatenate(dcb_parts, axis=1))

    def col(c):
        return pl.BlockSpec((tb, GROUP_W), lambda i: (nt - 1 - i, c))

    def prev(c):
        return pl.BlockSpec((8, GROUP_W), lambda i: (jnp.maximum((nt - 1 - i) * (tb // 8) - 1, 0), c))

    out = pl.BlockSpec((tb, GROUP_W), lambda i: (nt - 1 - i, 0))
    vec = _full((1, LANES))
    return _call(
        body, name="ssd_bwd", grid=(nt,),
        out_shape=[jax.ShapeDtypeStruct((t, GROUP_W), bf16)] * 4 + [jax.ShapeDtypeStruct((t, LANES), bf16),
                   jax.ShapeDtypeStruct((8, 768), f32), jax.ShapeDtypeStruct((1, 768), f32)]
        + [jax.ShapeDtypeStruct((1, LANES), f32)] * 3,
        in_specs=[col(C_Z), col(C_XS), col(C_BM), col(C_CM), prev(C_XS), prev(C_BM), prev(C_CM),
                  pl.BlockSpec((tb, LANES), lambda i: (nt - 1 - i, C_DT128)), out, out,
                  pl.BlockSpec((cpt, 2, 128, 128), lambda i: (nt - 1 - i, 0, 0, 0)),
                  _full((8, 768)), _full((1, 768)), vec, vec, _full((1, GROUP_W))],
        out_specs=[out, out, out, out, pl.BlockSpec((tb, LANES), lambda i: (nt - 1 - i, 0)),
                   _full((8, 768)), _full((1, 768)), vec, vec, vec],
        scratch_shapes=[pltpu.VMEM((2, 128, 128), f32), pltpu.VMEM((8, 768), f32)],
        semantics=("arbitrary",), xchg=xchg,
        args=(proj, proj, proj, proj, proj, proj, proj, proj, dyc, y_pre, states, conv_w, conv_b, dt_bias, a_log, d_cols))


def _s5_coeffs(are, aim, ls):
    step = jnp.exp(ls)
    mag = jnp.exp(are * step)
    th = aim * step
    lre, lim = mag * jnp.cos(th), mag * jnp.sin(th)
    den = are * are + aim * aim
    nr = lre - 1.0
    fre = (nr * are + lim * aim) / den
    fim = (lim * are - nr * aim) / den
    return step, lre, lim, den, fre, fim


def _s5_prep(are, aim, ls, bre_bd, bim_bd):
    def body(are_ref, aim_ref, ls_ref, bre_ref, bim_ref, lre_ref, lim_ref, bbr_ref, bbi_ref):
        _, lre, lim, _, fre, fim = _s5_coeffs(are_ref[...], aim_ref[...], ls_ref[...])
        lre_ref[...] = lre
        lim_ref[...] = lim
        bre, bim = bre_ref[...], bim_ref[...]
        bbr_ref[...] = fre * bre - fim * bim
        bbi_ref[...] = fre * bim + fim * bre

    col = jax.ShapeDtypeStruct((S5_N, 1), f32)
    mat = jax.ShapeDtypeStruct((S5_N, GROUP_W), f32)
    return pl.pallas_call(body, name="s5_prep", out_shape=[col, col, mat, mat], in_specs=[VMEM] * 5, out_specs=[VMEM] * 4,
                          compiler_params=_params())(are, aim, ls, bre_bd, bim_bd)


def _s5_prep_bwd(are, aim, ls, bre_bd, bim_bd, dlre, dlim, dbbr, dbbi):
    def body(are_ref, aim_ref, ls_ref, bre_ref, bim_ref, dlre_ref, dlim_ref, dbbr_ref, dbbi_ref,
             dare_ref, daim_ref, dls_ref, dbre_ref, dbim_ref):
        are, aim = are_ref[...], aim_ref[...]
        step, lre, lim, den, fre, fim = _s5_coeffs(are, aim, ls_ref[...])
        r = lax.broadcasted_iota(jnp.int32, (S5_N, GROUP_W), 0) // 64
        c = lax.broadcasted_iota(jnp.int32, (S5_N, GROUP_W), 1) // 16
        mask = r == c
        gr = jnp.where(mask, dbbr_ref[...], 0.0)
        gi = jnp.where(mask, dbbi_ref[...], 0.0)
        bre, bim = bre_ref[...], bim_ref[...]
        dbre_ref[...] = fre * gr + fim * gi
        dbim_ref[...] = fre * gi - fim * gr
        dfre = jnp.sum(bre * gr + bim * gi, axis=1, keepdims=True)
        dfim = jnp.sum(bre * gi - bim * gr, axis=1, keepdims=True)
        ire, iim = are / den, aim / den
        tre = dlre_ref[...] + ire * dfre - iim * dfim
        tim = dlim_ref[...] + ire * dfim + iim * dfre
        dzre = lre * tre + lim * tim
        dzim = lre * tim - lim * tre
        qre = (fre * are + fim * aim) / den
        qim = (fim * are - fre * aim) / den
        dare_ref[...] = step * dzre - (qre * dfre + qim * dfim)
        daim_ref[...] = step * dzim - (qre * dfim - qim * dfre)
        dls = (are * dzre + aim * dzim) * step
        sel = (lax.broadcasted_iota(jnp.int32, (S5_N, LANES), 0) // 64 == lax.broadcasted_iota(jnp.int32, (S5_N, LANES), 1)).astype(f32)
        dls_ref[...] = lax.dot_general(sel, jnp.broadcast_to(dls, (S5_N, LANES)), (((0,), (0,)), ((), ())),
                                       preferred_element_type=f32, precision=lax.Precision.HIGHEST)

    col = jax.ShapeDtypeStruct((S5_N, 1), f32)
    mat = jax.ShapeDtypeStruct((S5_N, GROUP_W), f32)
    return pl.pallas_call(body, name="s5_prep_bwd", out_shape=[col, col, jax.ShapeDtypeStruct((LANES, LANES), f32), mat, mat],
                          in_specs=[VMEM] * 9, out_specs=[VMEM] * 5, compiler_params=_params(),
                          )(are, aim, ls, bre_bd, bim_bd, dlre, dlim, dbbr, dbbi)


def _cmul(ar, ai, br, bi):
    return ar * br - ai * bi, ar * bi + ai * br


def _s5_scan(re_ref, im_ref, carry_ref, mr, mi, n_groups, reverse):
    p1 = (mr, mi)
    p2 = _cmul(*p1, *p1)
    p3 = _cmul(*p2, *p1)
    p4 = _cmul(*p2, *p2)
    p5 = _cmul(*p4, *p1)
    p6 = _cmul(*p4, *p2)
    p7 = _cmul(*p4, *p3)
    p8 = _cmul(*p4, *p4)
    pows = [p1, p2, p3, p4, p5, p6, p7, p8]
    row = lax.broadcasted_iota(jnp.int32, (8, S5_N), 0)
    tr = jnp.zeros((8, S5_N), f32)
    ti = jnp.zeros((8, S5_N), f32)
    for i in range(8):
        p = pows[7 - i] if reverse else pows[i]
        tr = jnp.where(row == i, p[0], tr)
        ti = jnp.where(row == i, p[1], ti)
    steps = []
    for k, p in ((1, p1), (2, p2), (4, p4)):
        keep = (row + k < 8) if reverse else (row >= k)
        steps.append((8 - k if reverse else k, jnp.where(keep, p[0], 0.0), jnp.where(keep, p[1], 0.0)))
    edge = 0 if reverse else 7

    def step(j, carry):
        cr, ci = carry
        g = (n_groups - 1 - j) if reverse else j
        r0 = pl.multiple_of(g * 8, 8)
        xr = re_ref[pl.ds(r0, 8), :]
        xi = im_ref[pl.ds(r0, 8), :]
        for shift, br, bi in steps:
            sr = pltpu.roll(xr, shift, 0)
            si = pltpu.roll(xi, shift, 0)
            xr, xi = xr + br * sr - bi * si, xi + br * si + bi * sr
        xr, xi = xr + tr * cr - ti * ci, xi + tr * ci + ti * cr
        re_ref[pl.ds(r0, 8), :] = xr
        im_ref[pl.ds(r0, 8), :] = xi
        return (jnp.broadcast_to(xr[edge:edge + 1, :], (8, S5_N)), jnp.broadcast_to(xi[edge:edge + 1, :], (8, S5_N)))

    cr, ci = lax.fori_loop(0, n_groups, step, (carry_ref[0], carry_ref[1]))
    carry_ref[0] = cr
    carry_ref[1] = ci


def _s5_output(u, xr, xi, ctr, cti, d):
    return _dot_nt(_b(xr), _b(ctr)) - _dot_nt(_b(xi), _b(cti)) + d * u


def _s5_fwd(proj, bbr, bbi, ctr, cti, lre, lim, d, glu_w, glu_b, tb, xchg=None):
    t = proj.shape[0]

    def body(u_ref, bbr_ref, bbi_ref, ctr_ref, cti_ref, lr_ref, li_ref, d_ref, gw_ref, gb_ref, o_ref, xr_ref, xi_ref, carry):
        @pl.when(pl.program_id(0) == 0)
        def _():
            carry[...] = jnp.zeros_like(carry)

        u = u_ref[...]
        ub = _b(u)
        xr_ref[...] = _dot_nt(ub, _b(bbr_ref[...]))
        xi_ref[...] = _dot_nt(ub, _b(bbi_ref[...]))
        _s5_scan(xr_ref, xi_ref, carry, lr_ref[...], li_ref[...], tb // 8, reverse=False)
        y = _s5_output(u, xr_ref[...], xi_ref[...], ctr_ref[...], cti_ref[...], d_ref[...])
        gl = _gelu(y)
        o_ref[...] = _b(gl * jax.nn.sigmoid(_dot(_b(gl), _b(gw_ref[...])) + gb_ref[...]))

    state = pl.BlockSpec((tb, S5_N), lambda i: (i, 0))
    return _call(
        body, name="s5_fwd", grid=(t // tb,),
        out_shape=[jax.ShapeDtypeStruct((t, GROUP_W), bf16), jax.ShapeDtypeStruct((t, S5_N), f32), jax.ShapeDtypeStruct((t, S5_N), f32)],
        in_specs=[pl.BlockSpec((tb, GROUP_W), lambda i: (i, C_S5)), _full((S5_N, GROUP_W)), _full((S5_N, GROUP_W)),
                  _full((GROUP_W, S5_N)), _full((GROUP_W, S5_N)), _full((1, S5_N)), _full((1, S5_N)),
                  _full((1, GROUP_W)), _full((GROUP_W, GROUP_W)), _full((1, GROUP_W))],
        out_specs=[pl.BlockSpec((tb, GROUP_W), lambda i: (i, 0)), state, state],
        scratch_shapes=[pltpu.VMEM((2, 8, S5_N), f32)],
        semantics=("arbitrary",), xchg=xchg, args=(proj, bbr, bbi, ctr, cti, lre, lim, d, glu_w, glu_b))


def _s5_bwd(proj, dyd, xr_all, xi_all, bbr, bbi, ctr, cti, lre, lim, d, glu_w, glu_b, tb, xchg=None):
    t = proj.shape[0]
    nt = t // tb

    def body(u_ref, dy_ref, xr_ref, xi_ref, xrh_ref, xih_ref, bbr_ref, bbi_ref, ctr_ref, cti_ref, lr_ref, li_ref,
             d_ref, gw_ref, gb_ref,
             du_ref, dlr_ref, dli_ref, dbbr_ref, dbbi_ref, dctr_ref, dcti_ref, dd_ref, dgw_ref, dgb_ref,
             gr_ref, gi_ref, carry):
        i = pl.program_id(0)
        ti = nt - 1 - i

        @pl.when(i == 0)
        def _():
            carry[...] = jnp.zeros_like(carry)

        u = u_ref[...]
        ub = _b(u)
        xr, xi = xr_ref[...], xi_ref[...]
        ctr, cti = _b(ctr_ref[...]), _b(cti_ref[...])
        d = d_ref[...]
        gw = _b(gw_ref[...])
        y = _s5_output(u, xr, xi, ctr, cti, d)
        gl = _gelu(y)
        sg = jax.nn.sigmoid(_dot(_b(gl), gw) + gb_ref[...])
        dout = dy_ref[...]
        q = dout * gl * sg * (1.0 - sg)
        qb = _b(q)
        dgl = dout * sg + _dot_nt(qb, gw)
        _acc(dgw_ref, _dot_tn(_b(gl), qb))
        _acc(dgb_ref, _colsum(q))
        dyv = dgl * _dgelu(y)
        _acc(dd_ref, _colsum(dyv * u))
        dyb = _b(dyv)
        gr_ref[...] = _dot(dyb, ctr)
        gi_ref[...] = -_dot(dyb, cti)
        _acc(dctr_ref, _dot_tn(dyb, _b(xr)))
        _acc(dcti_ref, -_dot_tn(dyb, _b(xi)))
        _s5_scan(gr_ref, gi_ref, carry, lr_ref[...], -li_ref[...], tb // 8, reverse=True)
        gr, gi = gr_ref[...], gi_ref[...]
        xpr = jnp.concatenate([jnp.where(ti > 0, xrh_ref[...], 0.0), xr], axis=0)[7:7 + tb]
        xpi = jnp.concatenate([jnp.where(ti > 0, xih_ref[...], 0.0), xi], axis=0)[7:7 + tb]
        _acc(dlr_ref, _colsum(gr * xpr + gi * xpi))
        _acc(dli_ref, _colsum(gi * xpr - gr * xpi))
        grb, gib = _b(gr), _b(gi)
        _acc(dbbr_ref, _dot_tn(grb, ub))
        _acc(dbbi_ref, _dot_tn(gib, ub))
        du_ref[...] = _b(dyv * d + _dot(grb, _b(bbr_ref[...])) + _dot(gib, _b(bbi_ref[...])))

    state = pl.BlockSpec((tb, S5_N), lambda i: (nt - 1 - i, 0))
    prev = pl.BlockSpec((8, S5_N), lambda i: (jnp.maximum((nt - 1 - i) * (tb // 8) - 1, 0), 0))
    tile = pl.BlockSpec((tb, GROUP_W), lambda i: (nt - 1 - i, 0))
    return _call(
        body, name="s5_bwd", grid=(nt,),
        out_shape=[jax.ShapeDtypeStruct((t, GROUP_W), bf16), jax.ShapeDtypeStruct((1, S5_N), f32), jax.ShapeDtypeStruct((1, S5_N), f32),
                   jax.ShapeDtypeStruct((S5_N, GROUP_W), f32), jax.ShapeDtypeStruct((S5_N, GROUP_W), f32),
                   jax.ShapeDtypeStruct((GROUP_W, S5_N), f32), jax.ShapeDtypeStruct((GROUP_W, S5_N), f32),
                   jax.ShapeDtypeStruct((1, GROUP_W), f32), jax.ShapeDtypeStruct((GROUP_W, GROUP_W), f32),
                   jax.ShapeDtypeStruct((1, GROUP_W), f32)],
        in_specs=[pl.BlockSpec((tb, GROUP_W), lambda i: (nt - 1 - i, C_S5)), tile, state, state, prev, prev,
                  _full((S5_N, GROUP_W)), _full((S5_N, GROUP_W)), _full((GROUP_W, S5_N)), _full((GROUP_W, S5_N)),
                  _full((1, S5_N)), _full((1, S5_N)), _full((1, GROUP_W)), _full((GROUP_W, GROUP_W)), _full((1, GROUP_W))],
        out_specs=[tile, _full((1, S5_N)), _full((1, S5_N)), _full((S5_N, GROUP_W)), _full((S5_N, GROUP_W)),
                   _full((GROUP_W, S5_N)), _full((GROUP_W, S5_N)), _full((1, GROUP_W)), _full((GROUP_W, GROUP_W)), _full((1, GROUP_W))],
        scratch_shapes=[pltpu.VMEM((tb, S5_N), f32), pltpu.VMEM((tb, S5_N), f32), pltpu.VMEM((2, 8, S5_N), f32)],
        semantics=("arbitrary",), xchg=xchg,
        args=(proj, dyd, xr_all, xi_all, xr_all, xi_all, bbr, bbi, ctr, cti, lre, lim, d, glu_w, glu_b))


def _outproj_fwd(ys, h, bn_w, g1, w_out, tb, xchg=None):
    t = h.shape[0]

    def body(ya_ref, yb_ref, yc_ref, yd_ref, h_ref, bn_ref, g1_ref, w_ref, h1_ref, o_ref, gr_ref):
        bn = bn_ref[...]
        parts = []
        for g, r in enumerate((ya_ref, yb_ref, yc_ref, yd_ref)):
            n, _ = _rms(r[...].astype(f32))
            parts.append(n * bn[:, g * GROUP_W:(g + 1) * GROUP_W])
        groups = _b(jnp.concatenate(parts, axis=1))
        gr_ref[...] = groups
        o = _dot(groups, w_ref[...])
        o_ref[...] = _b(o)
        h1_ref[...] = h_ref[...] + g1_ref[...] * o

    grp = pl.BlockSpec((tb, GROUP_W), lambda i: (i, 0))
    row = pl.BlockSpec((tb, D_MODEL), lambda i: (i, 0))
    vec = _full((1, D_MODEL))
    return _call(
        body, name="outproj_fwd", grid=(t // tb,),
        out_shape=[jax.ShapeDtypeStruct((t, D_MODEL), f32), jax.ShapeDtypeStruct((t, D_MODEL), bf16),
                   jax.ShapeDtypeStruct((t, D_MODEL), bf16)],
        in_specs=[grp, grp, grp, grp, row, vec, vec, _full((D_MODEL, D_MODEL))],
        out_specs=[row, row, row],
        semantics=("parallel",), xchg=xchg, args=(*ys, h, bn_w, g1, w_out))


def _outproj_bwd(dh1, o, ys, bn_w, g1, w_out, tb):
    t = dh1.shape[0]

    def body(dh_ref, o_ref, ya_ref, yb_ref, yc_ref, yd_ref, bn_ref, g1_ref, w_ref,
             da_ref, db_ref, dc_ref, dd_ref, do_ref, dg1_ref, dbn_ref):
        dh = dh_ref[...]
        _acc(dg1_ref, _colsum(dh * o_ref[...].astype(f32)))
        do = _b(dh * g1_ref[...])
        do_ref[...] = do
        dgroups = _dot_nt(do, w_ref[...])
        bn = bn_ref[...]
        dbn = []
        for g, (r, dr) in enumerate(((ya_ref, da_ref), (yb_ref, db_ref), (yc_ref, dc_ref), (yd_ref, dd_ref))):
            n, rr = _rms(r[...].astype(f32))
            dgr = dgroups[:, g * GROUP_W:(g + 1) * GROUP_W]
            dbn.append(_colsum(dgr * n))
            dr[...] = _rms_bwd(dgr * bn[:, g * GROUP_W:(g + 1) * GROUP_W], n, rr)
        _acc(dbn_ref, jnp.concatenate(dbn, axis=1))

    grp = pl.BlockSpec((tb, GROUP_W), lambda i: (i, 0))
    row = pl.BlockSpec((tb, D_MODEL), lambda i: (i, 0))
    vec = _full((1, D_MODEL))
    return pl.pallas_call(
        body, name="outproj_bwd", grid=(t // tb,),
        out_shape=[jax.ShapeDtypeStruct((t, GROUP_W), f32)] * 4 + [jax.ShapeDtypeStruct((t, D_MODEL), bf16),
                   jax.ShapeDtypeStruct((1, D_MODEL), f32), jax.ShapeDtypeStruct((1, D_MODEL), f32)],
        in_specs=[row, row, grp, grp, grp, grp, vec, vec, _full((D_MODEL, D_MODEL))],
        out_specs=[grp, grp, grp, grp, row, vec, vec],
        compiler_params=_params(("arbitrary",)),
    )(dh1, o, *ys, bn_w, g1, w_out)


def _mlp_fwd(h1, norm_w, sc, sh, g2, w1, w2, tb, xchg=None, head=None):
    t = h1.shape[0]
    nh = w1.shape[0] // MLP_SLABS
    n_head = 0 if head is None else 2

    def body(*refs):
        h_ref, nw_ref, sc_ref, sh_ref, g2_ref, w1_ref, w2_ref = refs[:7]
        head_refs, outs = refs[7:7 + n_head], refs[7 + n_head:]
        h2_ref, m_ref, v_ref, r_ref, acc = outs[0], outs[1], outs[2], outs[3], outs[-1]
        j = pl.program_id(1)

        @pl.when(j == 0)
        def _():
            n, _ = _rms(h_ref[...])
            v_ref[...] = _b(n * nw_ref[...] * (1.0 + sc_ref[...]) + sh_ref[...])

        v = v_ref[...]
        p = None
        for s in range(MLP_SLABS):
            ra = jnp.maximum(_dot(v, w1_ref[s]), 0.0)
            r = _b(ra * ra)
            r_ref[:, s * MLP_HB:(s + 1) * MLP_HB] = r
            q = _dot(r, w2_ref[s])
            p = q if p is None else p + q

        @pl.when(j == 0)
        def _():
            acc[...] = p

        @pl.when(j != 0)
        def _():
            acc[...] += p

        @pl.when(j == nh - 1)
        def _():
            m_ref[...] = _b(acc[...])
            if head is None:
                h2_ref[...] = h_ref[...] + g2_ref[...] * acc[...]
            else:
                tgt_ref, fw_ref = head_refs
                loss_ref, dfw_ref = outs[4], outs[5]
                fw, g2 = fw_ref[...], g2_ref[...]
                ch = min(HEAD_ROWS, tb)

                def chunk(c, carry):
                    sq, dfw = carry
                    rows = pl.ds(pl.multiple_of(c * ch, ch), ch)
                    n, r = _rms(h_ref[rows, :] + g2 * acc[rows, :])
                    err = n * fw - tgt_ref[rows, :]
                    dy = err / D_MODEL
                    h2_ref[rows, :] = _rms_bwd(dy * fw, n, r)
                    return (sq + jnp.sum(jnp.sum(err * err, axis=1, keepdims=True), axis=0, keepdims=True),
                            dfw + _colsum(dy * n))

                sq, dfw = lax.fori_loop(0, tb // ch, chunk, (jnp.zeros((1, 1), f32), jnp.zeros((1, D_MODEL), f32)))
                _acc(loss_ref, jnp.broadcast_to(0.5 * sq / D_MODEL, (8, LANES)))
                _acc(dfw_ref, dfw)

    row = pl.BlockSpec((tb, D_MODEL), lambda i, j: (i, 0))
    hid = pl.BlockSpec((tb, MLP_SLABS * MLP_HB), lambda i, j: (i, j))
    vec = _full((1, D_MODEL))
    head_shapes = [] if head is None else [jax.ShapeDtypeStruct((8, LANES), f32), jax.ShapeDtypeStruct((1, D_MODEL), f32)]
    return _call(
        body, name="mlp_fwd", grid=(t // tb, nh),
        out_shape=[jax.ShapeDtypeStruct((t, D_MODEL), f32), jax.ShapeDtypeStruct((t, D_MODEL), bf16),
                   jax.ShapeDtypeStruct((t, D_MODEL), bf16), jax.ShapeDtypeStruct((t, N_DEV * MLP_HB), bf16)] + head_shapes,
        in_specs=[row, vec, vec, vec, vec, pl.BlockSpec((MLP_SLABS, D_MODEL, MLP_HB), lambda i, j: (j, 0, 0)),
                  pl.BlockSpec((MLP_SLABS, MLP_HB, D_MODEL), lambda i, j: (j, 0, 0))] + ([] if head is None else [row, vec]),
        out_specs=[row, row, row, hid] + ([] if head is None else [_full((8, LANES)), vec]),
        scratch_shapes=[pltpu.VMEM((tb, D_MODEL), f32)],
        semantics=("arbitrary", "arbitrary"), xchg=xchg, args=(h1, norm_w, sc, sh, g2, w1, w2) + (() if head is None else tuple(head)))


def _mlp_bwd(dh2, m, h1, r, norm_w, sc, sh, g2, w1, w2, tb, xchg=None):
    t = h1.shape[0]
    slabs = MLP_BWD_SLABS
    nh = w1.shape[0] // slabs

    def body(dh_ref, m_ref, h_ref, r_ref, nw_ref, sc_ref, sh_ref, g2_ref, w1_ref, w2_ref,
             dh1_ref, do_ref, da_ref, dg2_ref, dsh_ref, dsc_ref, dnw_ref, acc):
        j = pl.program_id(1)

        @pl.when(j == 0)
        def _():
            dh = dh_ref[...]
            _acc(dg2_ref, _colsum(dh * m_ref[...].astype(f32)))
            do_ref[...] = _b(dh * g2_ref[...])

        do = do_ref[...]
        p = None
        for s in range(slabs):
            cols = slice(s * MLP_HB, (s + 1) * MLP_HB)
            dr = _dot_nt(do, w2_ref[s])
            da = _b(dr * 2.0 * jnp.sqrt(r_ref[:, cols].astype(f32)))
            da_ref[:, cols] = da
            q = _dot_nt(da, w1_ref[s])
            p = q if p is None else p + q

        @pl.when(j == 0)
        def _():
            acc[...] = p

        @pl.when(j != 0)
        def _():
            acc[...] += p

        @pl.when(j == nh - 1)
        def _():
            dv = acc[...]
            n, r = _rms(h_ref[...])
            nw = nw_ref[...]
            gain = 1.0 + sc_ref[...]
            _acc(dsh_ref, _colsum(dv))
            _acc(dsc_ref, _colsum(dv * n * nw))
            _acc(dnw_ref, _colsum(dv * gain * n))
            dh1_ref[...] = dh_ref[...] + _rms_bwd(dv * nw * gain, n, r)

    row = pl.BlockSpec((tb, D_MODEL), lambda i, j: (i, 0))
    hid = pl.BlockSpec((tb, slabs * MLP_HB), lambda i, j: (i, j))
    vec = _full((1, D_MODEL))
    once = dict(pipeline_mode=pl.Buffered(1)) if nh == 1 else {}
    return _call(
        body, name="mlp_bwd", grid=(t // tb, nh),
        out_shape=[jax.ShapeDtypeStruct((t, D_MODEL), f32), jax.ShapeDtypeStruct((t, D_MODEL), bf16),
                   jax.ShapeDtypeStruct((t, N_DEV * MLP_HB), bf16)] + [jax.ShapeDtypeStruct((1, D_MODEL), f32)] * 4,
        in_specs=[row, row, row, hid, vec, vec, vec, vec,
                  pl.BlockSpec((slabs, D_MODEL, MLP_HB), lambda i, j: (j, 0, 0), **once),
                  pl.BlockSpec((slabs, MLP_HB, D_MODEL), lambda i, j: (j, 0, 0), **once)],
        out_specs=[row, row, hid, vec, vec, vec, vec],
        scratch_shapes=[pltpu.VMEM((tb, D_MODEL), f32)],
        semantics=("arbitrary", "arbitrary"), xchg=xchg, args=(dh2, m, h1, r, norm_w, sc, sh, g2, w1, w2))


def _adam_math(w, g, m, v):
    m2 = ADAM_B1 * m + (1.0 - ADAM_B1) * g
    v2 = ADAM_B2 * v + (1.0 - ADAM_B2) * (g * g)
    mh = m2 / (1.0 - ADAM_B1 ** ADAM_STEP)
    vh = v2 / (1.0 - ADAM_B2 ** ADAM_STEP)
    return -ADAM_LR * (mh / (jnp.sqrt(vh) + ADAM_EPS) + ADAM_WD * w), m2, v2


def _adamw_small(ws, gs, ms, vs):
    n = len(ws)
    shapes = [w.shape for w in ws]
    as2d = [(1,) + s if len(s) == 1 else s for s in shapes]
    flat = [x.reshape(s) for group in (ws, gs, ms, vs) for x, s in zip(group, as2d)]

    def body(*refs):
        w_refs, g_refs, m_refs, v_refs, outs = refs[:n], refs[n:2 * n], refs[2 * n:3 * n], refs[3 * n:4 * n], refs[4 * n:]
        for i in range(n):
            d, m2, v2 = _adam_math(w_refs[i][...], g_refs[i][...], m_refs[i][...], v_refs[i][...])
            outs[3 * i][...] = d
            outs[3 * i + 1][...] = m2
            outs[3 * i + 2][...] = v2

    res = pl.pallas_call(body, name="adamw_small", out_shape=[jax.ShapeDtypeStruct(s, f32) for s in as2d for _ in range(3)],
                         in_specs=[VMEM] * (4 * n), out_specs=[VMEM] * (3 * n), compiler_params=_params())(*flat)
    return [r.reshape(shapes[i // 3]) for i, r in enumerate(res)]


def _sum_adamw_layers(parts0, parts1, w, m, v, name, rb):
    n_src, r, c = parts0.shape
    nb = r // rb

    def body(p0_ref, p1_ref, w_ref, m_ref, v_ref, g_ref, d_ref, m2_ref, v2_ref):
        def update(p_ref):
            g = p_ref[0].astype(f32)
            for s in range(1, n_src):
                g = g + p_ref[s].astype(f32)
            g_ref[0] = g
            d, m2, v2 = _adam_math(w_ref[0], g, m_ref[0], v_ref[0])
            d_ref[0] = d
            m2_ref[0] = m2
            v2_ref[0] = v2

        @pl.when(pl.program_id(0) == 0)
        def _():
            update(p0_ref)

        @pl.when(pl.program_id(0) == 1)
        def _():
            update(p1_ref)

    blk = pl.BlockSpec((1, rb, c), lambda l, i: (l, i, 0))
    return pl.pallas_call(
        body, name=name, grid=(2, nb),
        out_shape=[jax.ShapeDtypeStruct((2, r, c), f32)] * 4,
        in_specs=[pl.BlockSpec((n_src, rb, c), lambda l, i: (0, jnp.where(l == 0, i, nb - 1), 0)),
                  pl.BlockSpec((n_src, rb, c), lambda l, i: (0, jnp.where(l == 1, i, 0), 0)), blk, blk, blk],
        out_specs=[blk] * 4,
        compiler_params=_params(("arbitrary", "arbitrary")),
    )(parts0, parts1, w, m, v)


def _reorder_in(w):
    pad = jnp.zeros(w.shape[:-1] + (P_IN - 2308,), w.dtype)
    return jnp.concatenate([w[..., :2048], w[..., 2052:2308], w[..., 2048:2052], pad], axis=-1)


def _unreorder_in(w):
    return jnp.concatenate([w[..., :2048], w[..., 2304:2308], w[..., 2048:2304]], axis=-1)


def _block_diag(w2d, n_blocks):
    rows, cols = w2d.shape
    tiled = jnp.tile(w2d, (1, n_blocks))
    rb = lax.broadcasted_iota(jnp.int32, tiled.shape, 0) // (rows // n_blocks)
    cb = lax.broadcasted_iota(jnp.int32, tiled.shape, 1) // cols
    return jnp.where(rb == cb, tiled, jnp.zeros_like(tiled))


def _block_diag_extract(w_bd, n_blocks):
    rows, wide = w_bd.shape
    r, c = rows // n_blocks, wide // n_blocks
    w4 = w_bd.reshape(n_blocks, r, n_blocks, c)
    idx = jnp.arange(n_blocks)
    return w4[idx, :, idx, :]


def _rows_of(shape):
    n = 1
    for d in shape:
        n *= d
    return -(-n // (8 * LANES)) * 8, n


def _flat_pack(arrs, row_multiple=8):
    blocks = []
    for a in arrs:
        rows, n = _rows_of(a.shape)
        blocks.append(jnp.pad(a.reshape(-1), (0, rows * LANES - n)).reshape(rows, LANES))
    total = sum(b.shape[0] for b in blocks)
    pad = -total % row_multiple
    if pad:
        blocks.append(jnp.zeros((pad, LANES), blocks[0].dtype))
    return jnp.concatenate(blocks, axis=0)


def _flat_unpack(packed, shapes):
    out, off = [], 0
    for s in shapes:
        rows, n = _rows_of(s)
        out.append(packed[off:off + rows].reshape(-1)[:n].reshape(s))
        off += rows
    return out


_W_NAMES = ['norm_mix_w', 'norm_mlp_w', 'ada_w', 'ada_b', 'w_in', 'pool_w', 'pool_scale', 'sconv_w', 'ssd_conv_w',
            'ssd_conv_b', 'ssd_dt_bias', 'ssd_a_log', 'ssd_d', 's5_a_re', 's5_a_im', 's5_log_step', 's5_b_re', 's5_b_im',
            's5_c_re', 's5_c_im', 's5_d', 's5_glu_w', 's5_glu_b', 'branch_norm_w', 'w_out', 'mlp_w1', 'mlp_w2',
            'final_norm_w']
_BIG = ('ada_w', 'w_in', 'w_out', 'mlp_w1', 'mlp_w2')
_SMALL = [n for n in _W_NAMES if n not in _BIG]
_SHARDED_SMALL = {'sconv_w': (2, 32), 'ssd_conv_w': (2, 96), 's5_glu_w': (1, 32)}


def _gather(*blocks):
    return _ChipGather(blocks)


def _scatter(*parts):
    return _Scatter(parts)


def _layer_forward(l, h, p, w, sh_b, tb, head=None):
    first = l == 0
    (proj, u_b), got = _inproj_fwd(h, p['norm_mix_w'][l], p['sc1'][l], p['sh1'][l], w['w_in', l], tb,
                                   xchg=_gather(sh_b[1][0]) if first else None)
    if first:
        w['w_out', 0] = got[0].reshape(D_MODEL, D_MODEL)
    ya = _pool_fwd(proj, p['pool_bd'][l], p['pool_scale'][l], tb)
    yb = _sconv_fwd(proj, p['sconv_w8'][l], tb)
    (yc, yc_pre, states), got = _ssd_fwd(proj, p['ssd_conv_w8'][l], p['ssd_conv_b'][l], p['ssd_dt_bias'][l], p['ssd_a_log'][l],
                                         p['ssd_d_cols'][l], tb, xchg=_gather(sh_b[2][0]) if first else None)
    if first:
        w['w1', 0] = got[0]
    (yd, xr, xi), got = _s5_fwd(proj, p['bbr'][l], p['bbi'][l], p['ctr'][l], p['cti'][l], p['lre'][l], p['lim'][l],
                                p['s5_d'][l], p['glu_w'][l], p['glu_b'][l], tb, xchg=_gather(sh_b[3][0]) if first else None)
    if first:
        w['w2', 0] = got[0]
    ys = (ya, yb, yc, yd)
    (h1, o, groups_b), got = _outproj_fwd(ys, h, p['branch_norm_w'][l], p['g1'][l], w['w_out', l], tb,
                                          xchg=_gather(sh_b[0][1], sh_b[1][1]) if first else None)
    if first:
        w['w_in', 1] = got[0].reshape(D_MODEL, P_IN)
        w['w_out', 1] = got[1].reshape(D_MODEL, D_MODEL)
    (h2, m, v_b, r_b, *head_out), got = _mlp_fwd(
        h1, p['norm_mlp_w'][l], p['sc2'][l], p['sh2'][l], p['g2'][l], w['w1', l], w['w2', l],
        min(MLP_TB, h.shape[0]),
        xchg=_gather(sh_b[2][1], sh_b[3][1]) if first else None, head=head)
    if first:
        w['w1', 1], w['w2', 1] = got[0], got[1]
    saved = dict(h=h, proj=proj, u_b=u_b, ys=ys, yc_pre=yc_pre, states=states, xr=xr, xi=xi, h1=h1, o=o,
                 groups_b=groups_b, m=m, v_b=v_b, r_b=r_b)
    return (h2, *head_out), saved


def _layer_backward(l, dh2, s, p, w, pending, recv, tb):
    def carry(names):
        names = [n for n in names if n in pending]
        return names, (_scatter(*[pending.pop(n) for n in names]) if names else None)

    def landed(names, got):
        for n, g in zip(names, got):
            recv[n] = g

    names, xchg = carry([('w_out', 1)])
    (dh1, do2_b, da_b, dg2, dsh2, dsc2, dnw_mlp), got = _mlp_bwd(dh2, s['m'], s['h1'], s['r_b'], p['norm_mlp_w'][l], p['sc2'][l],
                                                                p['sh2'][l], p['g2'][l], w['w1', l], w['w2', l], min(TB_BWD, tb),
                                                                xchg=xchg)
    landed(names, got)
    pending['mlp_w2', l] = _wgrad(s['r_b'], do2_b, 1, "wgrad_w2", tm=1024, tk=4096).reshape(N_DEV, MLP_HB, D_MODEL)
    pending['mlp_w1', l] = _wgrad(s['v_b'], da_b, N_DEV, "wgrad_w1", tm=1024, tk=4096)
    dya, dyb, dyc, dyd, do1_b, dg1, dbn = _outproj_bwd(dh1, s['o'], s['ys'], p['branch_norm_w'][l], p['g1'][l], w['w_out', l], tb)
    pending['w_out', l] = _wgrad(s['groups_b'], do1_b, 1, "wgrad_wout", tm=1024, tk=1024).reshape(N_DEV, D_MODEL // N_DEV, D_MODEL)
    proj = s['proj']
    dv, dpool_bd, dpool_scale = _pool_bwd(proj, dya, p['pool_bd'][l], p['pool_scale'][l], tb)
    dgb, dgc, dhh, dsconv = _sconv_bwd(proj, dyb, p['sconv_w8'][l], tb)
    names, xchg = carry([('mlp_w1', l)] + ([('w_out', 0)] if l == 0 else []))
    (dz, dxs, dbm, dcm, ddt, dconv_w, dconv_b, ddtb, dalog, ddskip), got = _ssd_bwd(
        proj, dyc, s['yc_pre'], s['states'], p['ssd_conv_w8'][l], p['ssd_conv_b'][l], p['ssd_dt_bias'][l], p['ssd_a_log'][l],
        p['ssd_d_cols'][l], min(TB_BWD, tb), xchg=xchg)
    landed(names, got)
    names, xchg = carry([('mlp_w2', l)])
    (du5, dlr, dli, dbbr, dbbi, dctr, dcti, dd5, dgw, dgb5), got = _s5_bwd(
        proj, dyd, s['xr'], s['xi'], p['bbr'][l], p['bbi'][l], p['ctr'][l], p['cti'][l], p['lre'][l], p['lim'][l],
        p['s5_d'][l], p['glu_w'][l], p['glu_b'][l], min(TB_BWD, tb), xchg=xchg)
    landed(names, got)
    dare, daim, dls, dbre_bd, dbim_bd = _s5_prep_bwd(p['are_c'][l], p['aim_c'][l], p['ls_c'][l], p['bre_bd'][l], p['bim_bd'][l],
                                                     dlr.reshape(S5_N, 1), dli.reshape(S5_N, 1), dbbr, dbbi)
    dparts = (dv, dgb, dgc, dhh, dz, dxs, dbm, dcm, du5, ddt)
    pending['w_in', l] = _wgrad_parts(s['u_b'], dparts, "wgrad_win", tm=1024, tk=1024).reshape(N_DEV, D_MODEL // N_DEV, P_IN)
    names, xchg = carry([('w_in', l)])
    (dh, dsh1, dsc1, dnw_mix), got = _inproj_bwd(dparts, dh1, s['h'], p['norm_mix_w'][l], p['sc1'][l], p['sh1'][l], w['w_in', l],
                                                 tb, xchg=xchg)
    landed(names, got)
    small = {
        'norm_mix_w': dnw_mix.reshape(D_MODEL), 'norm_mlp_w': dnw_mlp.reshape(D_MODEL),
        'ada_b': jnp.concatenate([dsh1, dsc1, dg1, dsh2, dsc2, dg2], axis=1).reshape(6 * D_MODEL),
        'pool_w': _block_diag_extract(dpool_bd, 4), 'pool_scale': dpool_scale.reshape(GROUP_W),
        'sconv_w': dsconv[0:3], 'ssd_conv_w': dconv_w[0:4], 'ssd_conv_b': dconv_b.reshape(768),
        'ssd_dt_bias': ddtb[0, 0:4], 'ssd_a_log': dalog[0, 0:4], 'ssd_d': ddskip[0, 0:4],
        's5_a_re': dare.reshape(16, 64), 's5_a_im': daim.reshape(16, 64), 's5_log_step': dls[0:16, 0],
        's5_b_re': _block_diag_extract(dbre_bd, 16), 's5_b_im': _block_diag_extract(dbim_bd, 16),
        's5_c_re': _block_diag_extract(dctr, 16), 's5_c_im': _block_diag_extract(dcti, 16),
        's5_d': dd5.reshape(GROUP_W), 's5_glu_w': dgw, 's5_glu_b': dgb5.reshape(GROUP_W),
        'branch_norm_w': dbn.reshape(D_MODEL),
    }
    return dh, small


def _prepare_params(a, me, w_in0_shard):
    pack_shapes = [(1, D_MODEL), (2, 3, 32), (2, 4, 96), (2, 32, GROUP_W)]
    packed = _flat_pack([a['c'], a['sconv_w'], a['ssd_conv_w'], a['s5_glu_w']])
    w_in0, gathered = _exchange_alone(_gather(w_in0_shard, packed), "gather_first")
    pieces = [_flat_unpack(gathered[d], pack_shapes) for d in range(N_DEV)]
    c_all = jnp.concatenate([pc[0] for pc in pieces], axis=0)
    sconv_full = jnp.concatenate([pc[1] for pc in pieces], axis=2)
    ssd_conv_full = jnp.concatenate([pc[2] for pc in pieces], axis=2)
    glu_full = jnp.concatenate([pc[3] for pc in pieces], axis=1)

    ada_b_cols = lax.dynamic_slice_in_dim(a['ada_b'], me * 768, 768, axis=1).reshape(2, 1, 768)
    cond, modrows = _ada_forward(c_all, a['ada_w'], ada_b_cols)
    mod_recv = _all_to_all_rows(modrows.transpose(1, 0, 2), "exchange_mod")
    mod = mod_recv.transpose(1, 0, 2).reshape(2, 6 * D_MODEL)
    p = {'cond': cond}
    for k, name in enumerate(('sh1', 'sc1', 'g1', 'sh2', 'sc2', 'g2')):
        p[name] = mod[:, k * D_MODEL:(k + 1) * D_MODEL].reshape(2, 1, D_MODEL)

    for name in ('norm_mix_w', 'norm_mlp_w', 'branch_norm_w'):
        p[name] = a[name].reshape(2, 1, D_MODEL)
    p['pool_bd'] = jnp.stack([_block_diag(a['pool_w'][l].reshape(GROUP_W, 64), 4) for l in range(2)])
    p['pool_scale'] = a['pool_scale'].reshape(2, 1, GROUP_W)
    p['sconv_w8'] = jnp.pad(sconv_full, ((0, 0), (0, 5), (0, 0)))
    p['ssd_conv_w8'] = jnp.pad(ssd_conv_full, ((0, 0), (0, 4), (0, 0)))
    p['ssd_conv_b'] = a['ssd_conv_b'].reshape(2, 1, 768)
    p['ssd_dt_bias'] = jnp.pad(a['ssd_dt_bias'], ((0, 0), (0, LANES - 4))).reshape(2, 1, LANES)
    p['ssd_a_log'] = jnp.pad(a['ssd_a_log'], ((0, 0), (0, LANES - 4))).reshape(2, 1, LANES)
    p['ssd_d_cols'] = jnp.repeat(a['ssd_d'], SSD_P, axis=1).reshape(2, 1, GROUP_W)
    p['are_c'] = a['s5_a_re'].reshape(2, S5_N, 1)
    p['aim_c'] = a['s5_a_im'].reshape(2, S5_N, 1)
    p['ls_c'] = jnp.repeat(a['s5_log_step'], 64, axis=1).reshape(2, S5_N, 1)
    p['bre_bd'] = jnp.stack([_block_diag(a['s5_b_re'][l].reshape(S5_N, 16), 16) for l in range(2)])
    p['bim_bd'] = jnp.stack([_block_diag(a['s5_b_im'][l].reshape(S5_N, 16), 16) for l in range(2)])
    p['ctr'] = jnp.stack([_block_diag(a['s5_c_re'][l].reshape(GROUP_W, 64), 16) for l in range(2)])
    p['cti'] = jnp.stack([_block_diag(a['s5_c_im'][l].reshape(GROUP_W, 64), 16) for l in range(2)])
    p['s5_d'] = a['s5_d'].reshape(2, 1, GROUP_W)
    p['glu_w'] = glu_full
    p['glu_b'] = a['s5_glu_b'].reshape(2, 1, GROUP_W)
    lre, lim, bbr, bbi = [], [], [], []
    for l in range(2):
        r = _s5_prep(p['are_c'][l], p['aim_c'][l], p['ls_c'][l], p['bre_bd'][l], p['bim_bd'][l])
        lre.append(r[0].reshape(1, S5_N))
        lim.append(r[1].reshape(1, S5_N))
        bbr.append(r[2])
        bbi.append(r[3])
    p['lre'], p['lim'], p['bbr'], p['bbi'] = lre, lim, bbr, bbi
    return p, w_in0


def kernel(x, c, norm_mix_w, norm_mlp_w, ada_w, ada_b, w_in, pool_w, pool_scale, sconv_w, ssd_conv_w, ssd_conv_b, ssd_dt_bias, ssd_a_log, ssd_d, s5_a_re, s5_a_im, s5_log_step, s5_b_re, s5_b_im, s5_c_re, s5_c_im, s5_d, s5_glu_w, s5_glu_b, branch_norm_w, w_out, mlp_w1, mlp_w2, final_norm_w, loss_target, m_norm_mix_w, m_norm_mlp_w, m_ada_w, m_ada_b, m_w_in, m_pool_w, m_pool_scale, m_sconv_w, m_ssd_conv_w, m_ssd_conv_b, m_ssd_dt_bias, m_ssd_a_log, m_ssd_d, m_s5_a_re, m_s5_a_im, m_s5_log_step, m_s5_b_re, m_s5_b_im, m_s5_c_re, m_s5_c_im, m_s5_d, m_s5_glu_w, m_s5_glu_b, m_branch_norm_w, m_w_out, m_mlp_w1, m_mlp_w2, m_final_norm_w, v_norm_mix_w, v_norm_mlp_w, v_ada_w, v_ada_b, v_w_in, v_pool_w, v_pool_scale, v_sconv_w, v_ssd_conv_w, v_ssd_conv_b, v_ssd_dt_bias, v_ssd_a_log, v_ssd_d, v_s5_a_re, v_s5_a_im, v_s5_log_step, v_s5_b_re, v_s5_b_im, v_s5_c_re, v_s5_c_im, v_s5_d, v_s5_glu_w, v_s5_glu_b, v_branch_norm_w, v_w_out, v_mlp_w1, v_mlp_w2, v_final_norm_w):
    a = dict(locals())
    t = x.shape[1]
    tb = min(TB, t)
    me = _my_index()
    sh_b = _cast_shards([_reorder_in(w_in), w_out, mlp_w1, mlp_w2])
    p, w_in0 = _prepare_params(a, me, sh_b[0][0])
    w = {('w_in', 0): w_in0.reshape(D_MODEL, P_IN)}

    h = x.reshape(t, D_MODEL)
    saved = []
    (h,), s = _layer_forward(0, h, p, w, sh_b, tb)
    saved.append(s)
    (dh, loss_blk, dfinal), s = _layer_forward(1, h, p, w, sh_b, tb,
                                               head=(loss_target.reshape(t, D_MODEL), final_norm_w.reshape(1, D_MODEL)))
    saved.append(s)

    pending, recv, small_parts = {}, {}, [None, None]
    for l in (1, 0):
        dh, small_parts[l] = _layer_backward(l, dh, saved[l], p, w, pending, recv, tb)
    grad_x = dh.reshape(1, t, D_MODEL)

    grads, deltas, new_m, new_v = {}, {}, {}, {}

    wmv_in = [_reorder_in(a[n]) for n in ('w_in', 'm_w_in', 'v_w_in')]
    outs = _sum_adamw_layers(recv['w_in', 0], recv['w_in', 1], *wmv_in, "adamw_w_in", 128)
    grads['w_in'], deltas['w_in'], new_m['w_in'], new_v['w_in'] = [_unreorder_in(o) for o in outs]
    for name, rb in (('w_out', 128), ('mlp_w1', 256), ('mlp_w2', 256)):
        grads[name], deltas[name], new_m[name], new_v[name] = _sum_adamw_layers(
            recv[name, 0], recv[name, 1], a[name], a['m_' + name], a['v_' + name], "adamw_" + name, rb)

    dmod = jnp.stack([small_parts[0]['ada_b'], small_parts[1]['ada_b']])
    dmod_recv = _all_to_all_rows(dmod.reshape(2, N_DEV, 768).transpose(1, 0, 2), "exchange_dmod")
    g_ada = _ada_backward(p['cond'], dmod_recv.transpose(1, 0, 2))
    grads['ada_w'], deltas['ada_w'], new_m['ada_w'], new_v['ada_w'] = _sum_adamw_layers(
        g_ada[0:1], g_ada[1:2], ada_w, m_ada_w, v_ada_w, "adamw_ada_w", 256)

    layered = [n for n in _SMALL if n != 'final_norm_w']
    full = [jnp.stack([small_parts[0][n], small_parts[1][n]]) for n in layered] + [dfinal.reshape(D_MODEL)]
    full.append(loss_blk[0:1, 0:1])
    full_shapes = [f.shape for f in full]
    summed = _flat_unpack(_allreduce_rows(_flat_pack(full, row_multiple=64)), full_shapes)
    loss = summed[-1].reshape(())
    local = []
    for n, g in zip(_SMALL, summed):
        if n in _SHARDED_SMALL:
            axis, size = _SHARDED_SMALL[n]
            g = lax.dynamic_slice_in_dim(g, me * size, size, axis=axis)
        local.append(g.reshape(a[n].shape))
    outs = _adamw_small([a[n] for n in _SMALL], local, [a['m_' + n] for n in _SMALL], [a['v_' + n] for n in _SMALL])
    for i, n in enumerate(_SMALL):
        grads[n], deltas[n], new_m[n], new_v[n] = local[i], outs[3 * i], outs[3 * i + 1], outs[3 * i + 2]

    return (loss, grad_x, *[grads[n] for n in _W_NAMES], *[deltas[n] for n in _W_NAMES],
            *[new_m[n] for n in _W_NAMES], *[new_v[n] for n in _W_NAMES])
```

```python
import functools

import jax
import jax.numpy as jnp
from jax import lax
from jax.experimental import pallas as pl
from jax.experimental.pallas import tpu as pltpu

f32 = jnp.float32
bf16 = jnp.bfloat16

N_DEV = 8
D_MODEL = 1024
GROUP_W = 256
P_IN = 2432
DT_COL = 2304
SSD_CHUNK = 128
SSD_HEADS = 4
SSD_P = 64
S5_N = 1024
MLP_HB = 512
TB = 1024
TB_BWD = 512
MLP_TB = 1024
HEAD_ROWS = 256
MLP_SLABS = 2
MLP_BWD_SLABS = 8
EPS = 1e-6
LANES = 128
VMEM_LIMIT = 56 * 1024 * 1024
ADAM_LR, ADAM_B1, ADAM_B2, ADAM_EPS, ADAM_WD, ADAM_STEP = 0.001, 0.9, 0.999, 1e-08, 0.01, 10
POOL_WINDOWS = (2, 4, 8, 16)

C_POOL, C_GB, C_GC, C_HH, C_Z, C_XS, C_BM, C_CM, C_S5 = range(9)
C_DT128 = DT_COL // LANES

MESH = pl.DeviceIdType.MESH
ANY = pl.BlockSpec(memory_space=pl.ANY)
VMEM = pl.BlockSpec(memory_space=pltpu.VMEM)


def _dot(a, b):
    return jnp.dot(a, b, preferred_element_type=f32)


def _dot_nt(a, b):
    return lax.dot_general(a, b, (((1,), (1,)), ((), ())), preferred_element_type=f32)


def _dot_tn(a, b):
    return lax.dot_general(a, b, (((0,), (0,)), ((), ())), preferred_element_type=f32)


def _dot_exact(a, b):
    return jnp.dot(a, b, preferred_element_type=f32, precision=lax.Precision.HIGHEST)


def _b(x):
    return x.astype(bf16)


def _silu(x):
    return x * jax.nn.sigmoid(x)


def _dsilu(x):
    s = jax.nn.sigmoid(x)
    return s * (1.0 + x * (1.0 - s))


def _softplus(x):
    return jnp.maximum(x, 0.0) + jnp.log1p(jnp.exp(-jnp.abs(x)))


_GELU_K = 0.7978845608028654
_GELU_C = 0.044715


def _gelu(x):
    return 0.5 * x * (1.0 + jnp.tanh(_GELU_K * (x + _GELU_C * x * x * x)))


def _dgelu(x):
    th = jnp.tanh(_GELU_K * (x + _GELU_C * x * x * x))
    return 0.5 * (1.0 + th) + 0.5 * x * (1.0 - th * th) * _GELU_K * (1.0 + 3.0 * _GELU_C * x * x)


def _rms(h):
    r = lax.rsqrt(jnp.mean(h * h, axis=-1, keepdims=True) + EPS)
    return h * r, r


def _rms_bwd(dn, n, r):
    return r * (dn - n * jnp.mean(dn * n, axis=-1, keepdims=True))


def _colsum(x):
    return jnp.sum(x, axis=0, keepdims=True)


def _params(sem=None):
    return pltpu.CompilerParams(dimension_semantics=sem, vmem_limit_bytes=VMEM_LIMIT)


def _full(shape):
    return pl.BlockSpec(shape, lambda *_: (0,) * len(shape))


def _acc(ref, val):
    @pl.when(pl.program_id(0) == 0)
    def _():
        ref[...] = val

    @pl.when(pl.program_id(0) != 0)
    def _():
        ref[...] += val


def _me():
    return lax.axis_index("x"), lax.axis_index("y"), lax.axis_index("c")


def _my_index():
    x, y, c = _me()
    return 4 * x + 2 * y + c


def _coords(p):
    return (p // 4, (p // 2) % 2, p % 2)


class _Scatter:
    def __init__(self, srcs):
        self.srcs = list(srcs)
        self.n = len(self.srcs)
        self.out_shape = [jax.ShapeDtypeStruct(s.shape, s.dtype) for s in self.srcs]
        self.scratch = [pltpu.SemaphoreType.DMA((self.n, N_DEV)), pltpu.SemaphoreType.DMA((self.n, N_DEV)),
                        pltpu.SemaphoreType.DMA((self.n,))]

    def _remote(self, xin, xout, sems, t, k, me, to):
        return pltpu.make_async_remote_copy(
            src_ref=xin[t].at[to], dst_ref=xout[t].at[me], send_sem=sems[0].at[t, k], recv_sem=sems[1].at[t, k],
            device_id=_coords(to), device_id_type=MESH)

    def start(self, xin, xout, sems):
        me = _my_index()
        for t in range(self.n):
            pltpu.make_async_copy(xin[t].at[me], xout[t].at[me], sems[2].at[t]).start()
            for k in range(1, N_DEV):
                self._remote(xin, xout, sems, t, k, me, (me + k) % N_DEV).start()

    def forward(self, xin, xout, sems):
        pass

    def wait(self, xin, xout, sems):
        me = _my_index()
        for t in range(self.n):
            for k in range(1, N_DEV):
                src = (me + N_DEV - k) % N_DEV
                pltpu.make_async_remote_copy(
                    src_ref=xin[t].at[src], dst_ref=xout[t].at[src], send_sem=sems[0].at[t, k],
                    recv_sem=sems[1].at[t, k], device_id=_coords(src), device_id_type=MESH).wait_recv()
        for t in range(self.n):
            for k in range(1, N_DEV):
                self._remote(xin, xout, sems, t, k, me, (me + k) % N_DEV).wait_send()
            pltpu.make_async_copy(xin[t].at[me], xout[t].at[me], sems[2].at[t]).wait()


class _ChipGather:
    def __init__(self, srcs):
        self.srcs = list(srcs)
        self.n = len(self.srcs)
        self.out_shape = [jax.ShapeDtypeStruct((N_DEV,) + s.shape, s.dtype) for s in self.srcs]
        self.scratch = [pltpu.SemaphoreType.DMA((self.n, 7)), pltpu.SemaphoreType.DMA((self.n, 7)),
                        pltpu.SemaphoreType.DMA((self.n,))]

    @staticmethod
    def _places():
        x, y, c = _me()
        chips = [(1 - x, y), (x, 1 - y), (1 - x, 1 - y)]
        return (x, y, c), (x, y, 1 - c), chips

    @staticmethod
    def _slab(ref, dev):
        return ref.at[4 * dev[0] + 2 * dev[1] + dev[2]]

    def _copy(self, xin, xout, sems, t, k, block, to, src=None):
        return pltpu.make_async_remote_copy(
            src_ref=self._slab(xout[t], block) if src is None else src, dst_ref=self._slab(xout[t], block),
            send_sem=sems[0].at[t, k], recv_sem=sems[1].at[t, k], device_id=to, device_id_type=MESH)

    def start(self, xin, xout, sems):
        me, sibling, chips = self._places()
        for t in range(self.n):
            pltpu.make_async_copy(xin[t], self._slab(xout[t], me), sems[2].at[t]).start()
            self._copy(xin, xout, sems, t, 0, me, sibling, src=xin[t]).start()
            for j, chip in enumerate(chips):
                self._copy(xin, xout, sems, t, 1 + j, me, (*chip, me[2]), src=xin[t]).start()

    def forward(self, xin, xout, sems):
        me, sibling, chips = self._places()
        for t in range(self.n):
            for j, chip in enumerate(chips):
                self._copy(xin, xout, sems, t, 1 + j, (*chip, me[2]), me).wait_recv()
                self._copy(xin, xout, sems, t, 4 + j, (*chip, me[2]), sibling).start()

    def wait(self, xin, xout, sems):
        me, sibling, chips = self._places()
        for t in range(self.n):
            self._copy(xin, xout, sems, t, 0, sibling, me).wait_recv()
            for j, chip in enumerate(chips):
                self._copy(xin, xout, sems, t, 4 + j, (*chip, 1 - me[2]), me).wait_recv()
        for t in range(self.n):
            self._copy(xin, xout, sems, t, 0, me, sibling, src=xin[t]).wait_send()
            for j, chip in enumerate(chips):
                self._copy(xin, xout, sems, t, 1 + j, me, (*chip, me[2]), src=xin[t]).wait_send()
                self._copy(xin, xout, sems, t, 4 + j, (*chip, me[2]), sibling).wait_send()
            pltpu.make_async_copy(xin[t], self._slab(xout[t], me), sems[2].at[t]).wait()


def _call(body, *, name, grid, in_specs, out_specs, out_shape, args, semantics, scratch_shapes=(), xchg=None):
    if xchg is None:
        outs = pl.pallas_call(body, name=name, grid=grid, in_specs=in_specs, out_specs=out_specs, out_shape=out_shape,
                              scratch_shapes=list(scratch_shapes), compiler_params=_params(semantics))(*args)
        return outs, ()
    n_in, n_out, n_scr, n = len(in_specs), len(out_specs), len(scratch_shapes), xchg.n

    def carried(*refs):
        ins, xin = refs[:n_in], refs[n_in:n_in + n]
        outs, xout = refs[n_in + n:n_in + n + n_out], refs[n_in + n + n_out:n_in + 2 * n + n_out]
        scr, sems = refs[n_in + 2 * n + n_out:n_in + 2 * n + n_out + n_scr], refs[n_in + 2 * n + n_out + n_scr:]
        step = pl.program_id(0)
        for d in range(1, len(grid)):
            step = step * grid[d] + pl.program_id(d)
        n_steps = functools.reduce(lambda a, b: a * b, grid)

        @pl.when(step == 0)
        def _():
            xchg.start(xin, xout, sems)

        @pl.when(step == (2 * n_steps) // 3)
        def _():
            xchg.forward(xin, xout, sems)

        body(*ins, *outs, *scr)

        @pl.when(step == n_steps - 1)
        def _():
            xchg.wait(xin, xout, sems)

    res = pl.pallas_call(
        carried, name=name, grid=grid, in_specs=list(in_specs) + [ANY] * n, out_specs=list(out_specs) + [ANY] * n,
        out_shape=list(out_shape) + xchg.out_shape, scratch_shapes=list(scratch_shapes) + xchg.scratch,
        compiler_params=_params(("arbitrary",) * len(grid)))(*args, *xchg.srcs)
    return res[:n_out], tuple(res[n_out:])


def _exchange_alone(xchg, name):
    def body(*refs):
        xin, xout, sems = refs[:xchg.n], refs[xchg.n:2 * xchg.n], refs[2 * xchg.n:]
        xchg.start(xin, xout, sems)
        xchg.forward(xin, xout, sems)
        xchg.wait(xin, xout, sems)

    return pl.pallas_call(body, name=name, out_shape=xchg.out_shape, in_specs=[ANY] * xchg.n, out_specs=[ANY] * xchg.n,
                          scratch_shapes=xchg.scratch)(*xchg.srcs)


def _cast_shards(shards):
    n = len(shards)

    def body(*refs):
        for i, o in zip(refs[:n], refs[n:]):
            o[...] = i[...].astype(bf16)

    return pl.pallas_call(body, name="cast_shards", out_shape=[jax.ShapeDtypeStruct(s.shape, bf16) for s in shards],
                          in_specs=[VMEM] * n, out_specs=[VMEM] * n, compiler_params=_params())(*shards)


def _allreduce_rows(v):
    r = v.shape[0]
    rp = r // N_DEV

    def body(v_ref, o_ref, parts, sums, send1, recv1, send2, recv2):
        me = _my_index()

        def piece(ref, d):
            return ref.at[pl.ds(pl.multiple_of(d * rp, 8), rp), :]

        def copy1(k, src_dev, to):
            return pltpu.make_async_remote_copy(src_ref=piece(v_ref, to), dst_ref=parts.at[src_dev], send_sem=send1.at[k],
                                                recv_sem=recv1.at[k], device_id=_coords(to), device_id_type=MESH)

        def copy2(k, owner, to):
            return pltpu.make_async_remote_copy(src_ref=sums, dst_ref=piece(o_ref, owner), send_sem=send2.at[k],
                                                recv_sem=recv2.at[k], device_id=_coords(to), device_id_type=MESH)

        for k in range(1, N_DEV):
            copy1(k, me, (me + k) % N_DEV).start()
        parts[me] = v_ref[pl.ds(pl.multiple_of(me * rp, 8), rp), :]
        for k in range(1, N_DEV):
            copy1(k, (me + N_DEV - k) % N_DEV, me).wait_recv()
        total = parts[0]
        for s in range(1, N_DEV):
            total = total + parts[s]
        sums[...] = total
        o_ref[pl.ds(pl.multiple_of(me * rp, 8), rp), :] = total
        for k in range(1, N_DEV):
            copy2(k, me, (me + k) % N_DEV).start()
        for k in range(1, N_DEV):
            copy2(k, (me + N_DEV - k) % N_DEV, me).wait_recv()
        for k in range(1, N_DEV):
            copy1(k, me, (me + k) % N_DEV).wait_send()
            copy2(k, me, (me + k) % N_DEV).wait_send()

    return pl.pallas_call(
        body, name="allreduce_small_grads", out_shape=jax.ShapeDtypeStruct(v.shape, v.dtype),
        in_specs=[VMEM], out_specs=VMEM,
        scratch_shapes=[pltpu.VMEM((N_DEV, rp, LANES), f32), pltpu.VMEM((rp, LANES), f32)]
        + [pltpu.SemaphoreType.DMA((N_DEV,))] * 4,
        compiler_params=_params(),
    )(v)


def _all_to_all_rows(v, name):
    def body(v_ref, o_ref, send_sems, recv_sems):
        me = _my_index()
        o_ref[me] = v_ref[me]
        sends = []
        for k in range(1, N_DEV):
            peer = (me + k) % N_DEV
            rc = pltpu.make_async_remote_copy(src_ref=v_ref.at[peer], dst_ref=o_ref.at[me], send_sem=send_sems.at[k],
                                              recv_sem=recv_sems.at[k], device_id=_coords(peer), device_id_type=MESH)
            rc.start()
            sends.append(rc)
        for k in range(1, N_DEV):
            src = (me + N_DEV - k) % N_DEV
            pltpu.make_async_remote_copy(src_ref=v_ref.at[src], dst_ref=o_ref.at[src], send_sem=send_sems.at[k],
                                         recv_sem=recv_sems.at[k], device_id=_coords(src), device_id_type=MESH).wait_recv()
        for rc in sends:
            rc.wait_send()

    return pl.pallas_call(
        body, name=name, out_shape=jax.ShapeDtypeStruct(v.shape, v.dtype),
        in_specs=[VMEM], out_specs=VMEM,
        scratch_shapes=[pltpu.SemaphoreType.DMA((N_DEV,)), pltpu.SemaphoreType.DMA((N_DEV,))],
    )(v)


def _ada_forward(c_all, ada_w, ada_b_cols):
    def body(c_ref, w_ref, b_ref, cond_ref, o_ref):
        cond = _silu(c_ref[...])
        cond_ref[...] = cond
        for l in range(2):
            o_ref[l] = _dot(_b(cond), _b(w_ref[l])) + b_ref[l]

    return pl.pallas_call(
        body, name="ada_forward",
        out_shape=[jax.ShapeDtypeStruct((N_DEV, D_MODEL), f32), jax.ShapeDtypeStruct((2, N_DEV, 768), f32)],
        in_specs=[VMEM] * 3, out_specs=[VMEM] * 2, compiler_params=_params(),
    )(c_all, ada_w, ada_b_cols)


def _ada_backward(cond, dmod_rows):
    def body(c_ref, d_ref, o_ref):
        cb = _b(c_ref[...])
        for l in range(2):
            o_ref[l] = _dot_tn(cb, _b(d_ref[l]))

    return pl.pallas_call(
        body, name="ada_backward", out_shape=jax.ShapeDtypeStruct((2, D_MODEL, 768), f32),
        in_specs=[VMEM] * 2, out_specs=VMEM, compiler_params=_params(),
    )(cond, dmod_rows)


def _inproj_fwd(h, norm_w, sc, sh, w_in, tb, xchg=None):
    t = h.shape[0]

    def body(h_ref, nw_ref, sc_ref, sh_ref, w_ref, proj_ref, u_ref):
        n, _ = _rms(h_ref[...])
        u = _b(n * nw_ref[...] * (1.0 + sc_ref[...]) + sh_ref[...])
        u_ref[...] = u
        proj_ref[...] = _dot(u, w_ref[...])

    row = pl.BlockSpec((tb, D_MODEL), lambda i: (i, 0))
    vec = _full((1, D_MODEL))
    return _call(
        body, name="inproj_fwd", grid=(t // tb,),
        out_shape=[jax.ShapeDtypeStruct((t, P_IN), f32), jax.ShapeDtypeStruct((t, D_MODEL), bf16)],
        in_specs=[row, vec, vec, vec, _full((D_MODEL, P_IN))],
        out_specs=[pl.BlockSpec((tb, P_IN), lambda i: (i, 0)), row],
        semantics=("parallel",), args=(h, norm_w, sc, sh, w_in), xchg=xchg)


def _inproj_bwd(dparts, dh_res, h, norm_w, sc, sh, w_in, tb, xchg=None):
    t = h.shape[0]

    def body(*refs):
        parts = refs[:10]
        dres_ref, h_ref, nw_ref, sc_ref, sh_ref, w_ref = refs[10:16]
        dh_ref, dsh_ref, dsc_ref, dnw_ref = refs[16:]
        dproj = jnp.concatenate([p[...] for p in parts], axis=1)
        du = _dot_nt(dproj, w_ref[...])
        n, r = _rms(h_ref[...])
        nw = nw_ref[...]
        gain = 1.0 + sc_ref[...]
        _acc(dsh_ref, _colsum(du))
        _acc(dsc_ref, _colsum(du * n * nw))
        _acc(dnw_ref, _colsum(du * gain * n))
        dh_ref[...] = dres_ref[...] + _rms_bwd(du * nw * gain, n, r)

    row = pl.BlockSpec((tb, D_MODEL), lambda i: (i, 0))
    vec = _full((1, D_MODEL))
    part_specs = [pl.BlockSpec((tb, GROUP_W), lambda i: (i, 0))] * 9 + [pl.BlockSpec((tb, LANES), lambda i: (i, 0))]
    return _call(
        body, name="inproj_bwd", grid=(t // tb,),
        out_shape=[jax.ShapeDtypeStruct((t, D_MODEL), f32)] + [jax.ShapeDtypeStruct((1, D_MODEL), f32)] * 3,
        in_specs=part_specs + [row, row, vec, vec, vec,
                               pl.BlockSpec((D_MODEL, P_IN), lambda i: (0, 0), pipeline_mode=pl.Buffered(1))],
        out_specs=[row, vec, vec, vec],
        semantics=("arbitrary",), xchg=xchg, args=(*dparts, dh_res, h, norm_w, sc, sh, w_in))


def _wgrad(a, b, n_blocks, name, tm, tk=512):
    t, m = a.shape
    nb = b.shape[1] // n_blocks
    tk = min(tk, t)
    nk = t // tk

    def body(a_ref, b_ref, o_ref, acc_ref):
        k = pl.program_id(2)
        p = _dot_tn(a_ref[...], b_ref[...])

        @pl.when(k == 0)
        def _():
            acc_ref[...] = p

        @pl.when(k != 0)
        def _():
            acc_ref[...] += p

        @pl.when(k == nk - 1)
        def _():
            o_ref[0] = acc_ref[...].astype(o_ref.dtype)

    return pl.pallas_call(
        body, name=name, grid=(m // tm, n_blocks, nk),
        out_shape=jax.ShapeDtypeStruct((n_blocks, m, nb), bf16),
        in_specs=[pl.BlockSpec((tk, tm), lambda i, j, k: (k, i)), pl.BlockSpec((tk, nb), lambda i, j, k: (k, j))],
        out_specs=pl.BlockSpec((1, tm, nb), lambda i, j, k: (j, i, 0)),
        scratch_shapes=[pltpu.VMEM((tm, nb), f32)],
        compiler_params=_params(("parallel", "parallel", "arbitrary")),
    )(a, b)


def _wgrad_parts(a, parts, name, tm, tk):
    t, m = a.shape
    n = sum(p.shape[1] for p in parts)
    n_parts = len(parts)
    tk = min(tk, t)
    nk = t // tk

    def body(*refs):
        a_ref, part_refs, o_ref, acc_ref = refs[0], refs[1:1 + n_parts], refs[1 + n_parts], refs[2 + n_parts]
        k = pl.program_id(1)
        p = _dot_tn(a_ref[...], jnp.concatenate([r[...] for r in part_refs], axis=1))

        @pl.when(k == 0)
        def _():
            acc_ref[...] = p

        @pl.when(k != 0)
        def _():
            acc_ref[...] += p

        @pl.when(k == nk - 1)
        def _():
            o_ref[...] = acc_ref[...].astype(o_ref.dtype)

    return pl.pallas_call(
        body, name=name, grid=(m // tm, nk),
        out_shape=jax.ShapeDtypeStruct((m, n), bf16),
        in_specs=[pl.BlockSpec((tk, tm), lambda i, k: (k, i))]
        + [pl.BlockSpec((tk, p.shape[1]), lambda i, k: (k, 0)) for p in parts],
        out_specs=pl.BlockSpec((tm, n), lambda i, k: (i, 0)),
        scratch_shapes=[pltpu.VMEM((tm, n), f32)],
        compiler_params=_params(("parallel", "arbitrary")),
    )(a, *parts)


def _pool_counts(rows, t0):
    tpos = (lax.broadcasted_iota(jnp.int32, (rows, GROUP_W), 0) + t0 + 1).astype(f32)
    grp = lax.broadcasted_iota(jnp.int32, (rows, GROUP_W), 1) // 64
    win = jnp.where(grp == 0, 2.0, jnp.where(grp == 1, 4.0, jnp.where(grp == 2, 8.0, 16.0)))
    return jnp.minimum(tpos, win), grp


def _pool_select(grp, l1, l2, l3, l4):
    return jnp.where(grp == 0, l1, jnp.where(grp == 1, l2, jnp.where(grp == 2, l3, l4)))


def _pool_means(v, halo, t0):
    tb = v.shape[0]
    ext = jnp.concatenate([halo, v], axis=0)
    n = tb + 16
    s1 = ext[1:n] + ext[0:n - 1]
    s2 = s1[2:n - 1] + s1[0:n - 3]
    s3 = s2[4:n - 3] + s2[0:n - 7]
    s4 = s3[8:n - 7] + s3[0:n - 15]
    cnt, grp = _pool_counts(tb, t0)
    wsum = _pool_select(grp, s1[15:15 + tb], s2[13:13 + tb], s3[9:9 + tb], s4[1:1 + tb])
    return wsum / cnt - v


def _pool_fwd(proj, pw_bd, scale, tb):
    t = proj.shape[0]

    def body(v_ref, vh_ref, pw_ref, sc_ref, o_ref):
        i = pl.program_id(0)
        halo = jnp.where(i > 0, vh_ref[...], 0.0)
        p = _pool_means(v_ref[...], halo, i * tb)
        o_ref[...] = _b(_dot(_b(p), _b(pw_ref[...])) * sc_ref[...])

    return pl.pallas_call(
        body, name="pool_fwd", grid=(t // tb,),
        out_shape=jax.ShapeDtypeStruct((t, GROUP_W), bf16),
        in_specs=[pl.BlockSpec((tb, GROUP_W), lambda i: (i, C_POOL)),
                  pl.BlockSpec((16, GROUP_W), lambda i: (jnp.maximum(i * (tb // 16) - 1, 0), C_POOL)),
                  _full((GROUP_W, GROUP_W)), _full((1, GROUP_W))],
        out_specs=pl.BlockSpec((tb, GROUP_W), lambda i: (i, 0)),
        compiler_params=_params(("parallel",)),
    )(proj, proj, pw_bd, scale)


def _pool_bwd(proj, dy, pw_bd, scale, tb):
    t = proj.shape[0]
    nt = t // tb
    last16 = t // 16 - 1

    def body(v_ref, vh_ref, dy_ref, dyh_ref, pw_ref, sc_ref, dv_ref, dpw_ref, dsc_ref):
        i = pl.program_id(0)
        halo = jnp.where(i > 0, vh_ref[...], 0.0)
        p = _pool_means(v_ref[...], halo, i * tb)
        pw = _b(pw_ref[...])
        sc = sc_ref[...]
        dy = dy_ref[...]
        ypre = _dot(_b(p), pw)
        _acc(dsc_ref, _colsum(dy * ypre))
        dys = _b(dy * sc)
        _acc(dpw_ref, _dot_tn(_b(p), dys))
        dp = _dot_nt(dys, pw)
        dph = _dot_nt(_b(jnp.where(i < nt - 1, dyh_ref[...], 0.0) * sc), pw)
        cnt, grp = _pool_counts(tb, i * tb)
        cnth, _ = _pool_counts(16, (i + 1) * tb)
        ext = jnp.concatenate([dp / cnt, dph / cnth], axis=0)
        n = tb + 16
        f1 = ext[0:n - 1] + ext[1:n]
        f2 = f1[0:n - 3] + f1[2:n - 1]
        f3 = f2[0:n - 7] + f2[4:n - 3]
        f4 = f3[0:n - 15] + f3[8:n - 7]
        dv_ref[...] = _b(_pool_select(grp, f1[0:tb], f2[0:tb], f3[0:tb], f4[0:tb]) - dp)

    return pl.pallas_call(
        body, name="pool_bwd", grid=(nt,),
        out_shape=[jax.ShapeDtypeStruct((t, GROUP_W), bf16), jax.ShapeDtypeStruct((GROUP_W, GROUP_W), f32),
                   jax.ShapeDtypeStruct((1, GROUP_W), f32)],
        in_specs=[pl.BlockSpec((tb, GROUP_W), lambda i: (i, C_POOL)),
                  pl.BlockSpec((16, GROUP_W), lambda i: (jnp.maximum(i * (tb // 16) - 1, 0), C_POOL)),
                  pl.BlockSpec((tb, GROUP_W), lambda i: (i, 0)),
                  pl.BlockSpec((16, GROUP_W), lambda i: (jnp.minimum((i + 1) * (tb // 16), last16), 0)),
                  _full((GROUP_W, GROUP_W)), _full((1, GROUP_W))],
        out_specs=[pl.BlockSpec((tb, GROUP_W), lambda i: (i, 0)), _full((GROUP_W, GROUP_W)), _full((1, GROUP_W))],
        compiler_params=_params(("arbitrary",)),
    )(proj, proj, dy, dy, pw_bd, scale)


def _sconv_fwd(proj, w, tb):
    t = proj.shape[0]

    def body(gb_ref, gc_ref, hh_ref, gch_ref, hhh_ref, w_ref, o_ref):
        i = pl.program_id(0)
        q = gc_ref[...] * hh_ref[...]
        qh = jnp.where(i > 0, gch_ref[...] * hhh_ref[...], 0.0)
        ext = jnp.concatenate([qh, q], axis=0)
        w = w_ref[...]
        conv = w[0:1] * ext[6:6 + tb] + w[1:2] * ext[7:7 + tb] + w[2:3] * ext[8:8 + tb]
        o_ref[...] = _b(gb_ref[...] * conv)

    def col(c):
        return pl.BlockSpec((tb, GROUP_W), lambda i: (i, c))

    def prev(c):
        return pl.BlockSpec((8, GROUP_W), lambda i: (jnp.maximum(i * (tb // 8) - 1, 0), c))

    return pl.pallas_call(
        body, name="sconv_fwd", grid=(t // tb,),
        out_shape=jax.ShapeDtypeStruct((t, GROUP_W), bf16),
        in_specs=[col(C_GB), col(C_GC), col(C_HH), prev(C_GC), prev(C_HH), _full((8, GROUP_W))],
        out_specs=pl.BlockSpec((tb, GROUP_W), lambda i: (i, 0)),
        compiler_params=_params(("parallel",)),
    )(proj, proj, proj, proj, proj, w)


def _sconv_bwd(proj, dy, w, tb):
    t = proj.shape[0]
    nt = t // tb
    last8 = t // 8 - 1

    def body(gb_ref, gc_ref, hh_ref, gch_ref, hhh_ref, gbn_ref, dy_ref, dyn_ref, w_ref, dgb_ref, dgc_ref, dhh_ref, dw_ref):
        i = pl.program_id(0)
        gc, hh, gb, dy = gc_ref[...], hh_ref[...], gb_ref[...], dy_ref[...]
        q = gc * hh
        qh = jnp.where(i > 0, gch_ref[...] * hhh_ref[...], 0.0)
        ext = jnp.concatenate([qh, q], axis=0)
        w = w_ref[...]
        conv = w[0:1] * ext[6:6 + tb] + w[1:2] * ext[7:7 + tb] + w[2:3] * ext[8:8 + tb]
        dgb_ref[...] = _b(dy * conv)
        e = dy * gb
        en = jnp.where(i < nt - 1, dyn_ref[...] * gbn_ref[...], 0.0)
        exte = jnp.concatenate([e, en], axis=0)
        dq = w[2:3] * exte[0:tb] + w[1:2] * exte[1:1 + tb] + w[0:1] * exte[2:2 + tb]
        dgc_ref[...] = _b(dq * hh)
        dhh_ref[...] = _b(dq * gc)
        dw = jnp.concatenate([_colsum(e * ext[6:6 + tb]), _colsum(e * ext[7:7 + tb]), _colsum(e * ext[8:8 + tb]),
                              jnp.zeros((5, GROUP_W), f32)], axis=0)
        _acc(dw_ref, dw)

    def col(c):
        return pl.BlockSpec((tb, GROUP_W), lambda i: (i, c))

    def prev(c):
        return pl.BlockSpec((8, GROUP_W), lambda i: (jnp.maximum(i * (tb // 8) - 1, 0), c))

    def nxt(c):
        return pl.BlockSpec((8, GROUP_W), lambda i: (jnp.minimum((i + 1) * (tb // 8), last8), c))

    out = pl.BlockSpec((tb, GROUP_W), lambda i: (i, 0))
    return pl.pallas_call(
        body, name="sconv_bwd", grid=(nt,),
        out_shape=[jax.ShapeDtypeStruct((t, GROUP_W), bf16)] * 3 + [jax.ShapeDtypeStruct((8, GROUP_W), f32)],
        in_specs=[col(C_GB), col(C_GC), col(C_HH), prev(C_GC), prev(C_HH), nxt(C_GB), col(0), nxt(0), _full((8, GROUP_W))],
        out_specs=[out, out, out, _full((8, GROUP_W))],
        compiler_params=_params(("arbitrary",)),
    )(proj, proj, proj, proj, proj, proj, dy, dy, w)


def _conv4(xr, halo, w, bias):
    tb = xr.shape[0]
    ext = jnp.concatenate([halo, xr], axis=0)
    pre = w[0:1] * ext[5:5 + tb] + w[1:2] * ext[6:6 + tb] + w[2:3] * ext[7:7 + tb] + w[3:4] * ext[8:8 + tb] + bias
    return pre, ext


def _tri():
    r = lax.broadcasted_iota(jnp.int32, (SSD_CHUNK, SSD_CHUNK), 0)
    c = lax.broadcasted_iota(jnp.int32, (SSD_CHUNK, SSD_CHUNK), 1)
    return r >= c


def _lane_pick(vals):
    rows = vals[0].shape[0]
    lane = lax.broadcasted_iota(jnp.int32, (rows, LANES), 1)
    out = jnp.zeros((rows, LANES), f32)
    for h, v in enumerate(vals):
        out = jnp.where(lane == h, v, out)
    return out


def _ssd_fwd(proj, conv_w, conv_b, dt_bias, a_log, d_cols, tb, xchg=None):
    t = proj.shape[0]
    cpt = tb // SSD_CHUNK

    def body(z_ref, xs_ref, bm_ref, cm_ref, xsh_ref, bmh_ref, cmh_ref, dt_ref, cw_ref, cb_ref, dtb_ref, al_ref, dk_ref,
             o_ref, y_ref, st_ref, state):
        i = pl.program_id(0)

        @pl.when(i == 0)
        def _():
            state[...] = jnp.zeros_like(state)

        cw, cb = cw_ref[...], cb_ref[...]
        acts = []
        for j, (r, hr) in enumerate(((xs_ref, xsh_ref), (bm_ref, bmh_ref), (cm_ref, cmh_ref))):
            halo = jnp.where(i > 0, hr[...], 0.0)
            pre, _ = _conv4(r[...], halo, cw[:, j * 256:(j + 1) * 256], cb[:, j * 256:(j + 1) * 256])
            acts.append(_silu(pre))
        xs, bm, cm = acts
        dt = _softplus(dt_ref[...] + dtb_ref[...])
        a = -jnp.exp(al_ref[...])
        adt = dt * a
        tri = _tri()
        trif = tri.astype(f32)
        dk = dk_ref[...]
        for c in range(cpt):
            rows = slice(c * SSD_CHUNK, (c + 1) * SSD_CHUNK)
            acol = _dot_exact(trif, adt[rows])
            arow = acol.T
            dt_c = dt[rows]
            ys = []
            rowi = lax.broadcasted_iota(jnp.int32, (SSD_CHUNK, 1), 0)
            first = lax.broadcasted_iota(jnp.int32, (SSD_CHUNK, SSD_CHUNK), 1) < SSD_P
            for g in range(SSD_HEADS // 2):
                cols = slice(g * 128, (g + 1) * 128)
                cg, bg = _b(cm[rows, cols]), _b(bm[rows, cols])
                xg = xs[rows, cols]
                heads = (2 * g, 2 * g + 1)
                ac = [acol[:, h:h + 1] for h in heads]
                alast = [v[SSD_CHUNK - 1:SSD_CHUNK] for v in ac]
                dtw = jnp.where(first, dt_c[:, heads[0]:heads[0] + 1], dt_c[:, heads[1]:heads[1] + 1])
                eaw = jnp.where(first, jnp.exp(ac[0]), jnp.exp(ac[1]))
                wdw = jnp.where(first, jnp.exp(alast[0] - ac[0]), jnp.exp(alast[1] - ac[1]))
                xdt = xg * dtw
                xb = _b(xdt)
                gmat = _dot_nt(cg, bg)
                ydiag = []
                for k, h in enumerate(heads):
                    lm = jnp.exp(jnp.where(tri, ac[k] - arow[h:h + 1, :], -jnp.inf))
                    ydiag.append(_dot(_b(gmat * lm), xb[:, k * SSD_P:(k + 1) * SSD_P]))
                s_in = state[g]
                st_ref[c, g] = s_in
                ys.append(jnp.concatenate(ydiag, axis=1) + eaw * _dot_nt(cg, _b(s_in)) + xg * dk[:, cols])
                state[g] = jnp.where(rowi < SSD_P, jnp.exp(alast[0]), jnp.exp(alast[1])) * s_in + _dot_tn(_b(xdt * wdw), bg)
            yc = jnp.concatenate(ys, axis=1)
            y_ref[rows, :] = yc
            o_ref[rows, :] = _b(yc * _silu(z_ref[rows, :]))

    def col(c):
        return pl.BlockSpec((tb, GROUP_W), lambda i: (i, c))

    def prev(c):
        return pl.BlockSpec((8, GROUP_W), lambda i: (jnp.maximum(i * (tb // 8) - 1, 0), c))

    out = pl.BlockSpec((tb, GROUP_W), lambda i: (i, 0))
    return _call(
        body, name="ssd_fwd", grid=(t // tb,),
        out_shape=[jax.ShapeDtypeStruct((t, GROUP_W), bf16), jax.ShapeDtypeStruct((t, GROUP_W), f32),
                   jax.ShapeDtypeStruct((t // SSD_CHUNK, 2, 128, 128), f32)],
        in_specs=[col(C_Z), col(C_XS), col(C_BM), col(C_CM), prev(C_XS), prev(C_BM), prev(C_CM),
                  pl.BlockSpec((tb, LANES), lambda i: (i, C_DT128)),
                  _full((8, 768)), _full((1, 768)), _full((1, LANES)), _full((1, LANES)), _full((1, GROUP_W))],
        out_specs=[out, out, pl.BlockSpec((cpt, 2, 128, 128), lambda i: (i, 0, 0, 0))],
        scratch_shapes=[pltpu.VMEM((2, 128, 128), f32)],
        semantics=("arbitrary",), xchg=xchg,
        args=(proj, proj, proj, proj, proj, proj, proj, proj, conv_w, conv_b, dt_bias, a_log, d_cols))


def _ssd_bwd(proj, dyc, y_pre, states, conv_w, conv_b, dt_bias, a_log, d_cols, tb, xchg=None):
    t = proj.shape[0]
    nt = t // tb
    cpt = tb // SSD_CHUNK

    def body(z_ref, xs_ref, bm_ref, cm_ref, xsh_ref, bmh_ref, cmh_ref, dt_ref, dy_ref, yp_ref, st_ref,
             cw_ref, cb_ref, dtb_ref, al_ref, dk_ref,
             dz_ref, dxs_ref, dbm_ref, dcm_ref, ddt_ref, dcw_ref, dcb_ref, ddtb_ref, dal_ref, ddk_ref,
             dstate, carry):
        i = pl.program_id(0)
        ti = nt - 1 - i

        @pl.when(i == 0)
        def _():
            dstate[...] = jnp.zeros_like(dstate)
            carry[...] = jnp.zeros_like(carry)

        cw, cb = cw_ref[...], cb_ref[...]
        pres, exts, acts = [], [], []
        for j, (r, hr) in enumerate(((xs_ref, xsh_ref), (bm_ref, bmh_ref), (cm_ref, cmh_ref))):
            halo = jnp.where(ti > 0, hr[...], 0.0)
            pre, ext = _conv4(r[...], halo, cw[:, j * 256:(j + 1) * 256], cb[:, j * 256:(j + 1) * 256])
            pres.append(pre)
            exts.append(ext)
            acts.append(_silu(pre))
        xs, bm, cm = acts
        raw = dt_ref[...] + dtb_ref[...]
        dt = _softplus(raw)
        a = -jnp.exp(al_ref[...])
        adt = dt * a
        tri = _tri()
        trif = tri.astype(f32)
        dk = dk_ref[...]
        z = z_ref[...]
        dyc = dy_ref[...]
        dz_ref[...] = _b(dyc * yp_ref[...] * _dsilu(z))
        dy_all = dyc * _silu(z)
        lane = lax.broadcasted_iota(jnp.int32, (1, LANES), 1)
        ddk_acc = jnp.zeros((1, LANES), f32)
        dal_acc = jnp.zeros((1, LANES), f32)
        dxs_c, dbm_c, dcm_c, ddt_c = [None] * cpt, [None] * cpt, [None] * cpt, [None] * cpt
        for c in reversed(range(cpt)):
            rows = slice(c * SSD_CHUNK, (c + 1) * SSD_CHUNK)
            acol = _dot_exact(trif, adt[rows])
            arow = acol.T
            dt_c = dt[rows]
            da_cols, da_rows, ddt_heads, dxs_groups, dbg, dcg = [], [], [], [], [], []
            rowi = lax.broadcasted_iota(jnp.int32, (SSD_CHUNK, 1), 0)
            first = lax.broadcasted_iota(jnp.int32, (SSD_CHUNK, SSD_CHUNK), 1) < SSD_P
            for g in range(SSD_HEADS // 2):
                cols = slice(g * 128, (g + 1) * 128)
                cgf, bgf = cm[rows, cols], bm[rows, cols]
                cg, bg = _b(cgf), _b(bgf)
                xg, dyg = xs[rows, cols], dy_all[rows, cols]
                s_in, dsn = st_ref[c, g], dstate[g]
                sb, dsnb = _b(s_in), _b(dsn)
                heads = (2 * g, 2 * g + 1)
                ac = [acol[:, h:h + 1] for h in heads]
                alast = [v[SSD_CHUNK - 1:SSD_CHUNK] for v in ac]
                el = [jnp.exp(v) for v in alast]
                dtw = jnp.where(first, dt_c[:, heads[0]:heads[0] + 1], dt_c[:, heads[1]:heads[1] + 1])
                eaw = jnp.where(first, jnp.exp(ac[0]), jnp.exp(ac[1]))
                wdw = jnp.where(first, jnp.exp(alast[0] - ac[0]), jnp.exp(alast[1] - ac[1]))
                xdt = xg * dtw
                xb, dyb = _b(xdt), _b(dyg)
                gmat = _dot_nt(cg, bg)
                dgs, dxh, da = None, [], []
                for k, h in enumerate(heads):
                    hc = slice(k * SSD_P, (k + 1) * SSD_P)
                    lm = jnp.exp(jnp.where(tri, ac[k] - arow[h:h + 1, :], -jnp.inf))
                    m = gmat * lm
                    dm = _dot_nt(dyb[:, hc], xb[:, hc])
                    dxh.append(_dot_tn(_b(m), dyb[:, hc]))
                    dgs = dm * lm if dgs is None else dgs + dm * lm
                    wm = dm * m
                    da.append(jnp.sum(wm, axis=1, keepdims=True))
                    da_rows.append(jnp.sum(wm, axis=0, keepdims=True))
                dgb = _b(dgs)
                dcg_g = _dot(dgb, bg)
                dbg_g = _dot_tn(dgb, cg)
                yoff = eaw * _dot_nt(cg, sb)
                dyoff = dyg * yoff
                dye = _b(dyg * eaw)
                dcg_g = dcg_g + _dot(dye, sb)
                ds_y = _dot_tn(dye, cg)
                u = _dot_nt(bg, dsnb)
                dx = jnp.concatenate(dxh, axis=1) + wdw * u
                dbg_g = dbg_g + _dot(_b(xdt * wdw), dsnb)
                xu = xdt * u * wdw
                ss = jnp.sum(dsn * s_in, axis=1, keepdims=True)
                dxx = dx * xg
                dyx = _colsum(dyg * xg)
                for k, h in enumerate(heads):
                    mine = first if k == 0 else jnp.logical_not(first)
                    dwv = jnp.sum(jnp.where(mine, xu, 0.0), axis=1, keepdims=True)
                    mine_rows = (rowi < SSD_P) if k == 0 else (rowi >= SSD_P)
                    dalast = jnp.sum(dwv, axis=0, keepdims=True) + el[k] * jnp.sum(jnp.where(mine_rows, ss, 0.0), axis=0, keepdims=True)
                    dah = da[k] + jnp.sum(jnp.where(mine, dyoff, 0.0), axis=1, keepdims=True) - dwv
                    da_cols.append(dah + jnp.where(rowi == SSD_CHUNK - 1, dalast, 0.0))
                    ddt_heads.append(jnp.sum(jnp.where(mine, dxx, 0.0), axis=1, keepdims=True))
                    ddk_acc = ddk_acc + jnp.where(lane == h, jnp.sum(jnp.where(mine[0:1], dyx, 0.0), axis=1, keepdims=True), 0.0)
                dstate[g] = jnp.where(rowi < SSD_P, el[0], el[1]) * dsn + ds_y
                dxs_groups.append(dx * dtw + dyg * dk[:, cols])
                dbg.append(dbg_g)
                dcg.append(dcg_g)
            da_blk = _lane_pick(da_cols)
            rowsel = lax.broadcasted_iota(jnp.int32, (SSD_CHUNK, SSD_CHUNK), 0)
            da_rows_blk = jnp.zeros((SSD_CHUNK, SSD_CHUNK), f32)
            for h in range(SSD_HEADS):
                da_rows_blk = jnp.where(rowsel == h, da_rows[h], da_rows_blk)
            da_blk = da_blk - da_rows_blk.T
            dadt = lax.dot_general(trif, da_blk, (((0,), (0,)), ((), ())), preferred_element_type=f32,
                                   precision=lax.Precision.HIGHEST)
            dal_acc = dal_acc + _colsum(dadt * dt_c)
            ddt_c[c] = dadt * a + _lane_pick(ddt_heads)
            dxs_c[c] = jnp.concatenate(dxs_groups, axis=1)
            dbm_c[c] = jnp.concatenate(dbg, axis=1)
            dcm_c[c] = jnp.concatenate(dcg, axis=1)
        ddt = jnp.concatenate(ddt_c, axis=0) if cpt > 1 else ddt_c[0]
        ddraw = jnp.where(lane < SSD_HEADS, ddt * jax.nn.sigmoid(raw), 0.0)
        ddt_ref[...] = _b(ddraw)
        _acc(ddtb_ref, _colsum(ddraw))
        _acc(dal_ref, jnp.where(lane < SSD_HEADS, dal_acc * a, 0.0))
        _acc(ddk_ref, ddk_acc)
        dcw_parts, dcb_parts = [], []
        for j, (dparts, out_ref) in enumerate(((dxs_c, dxs_ref), (dbm_c, dbm_ref), (dcm_c, dcm_ref))):
            dact = jnp.concatenate(dparts, axis=0) if cpt > 1 else dparts[0]
            dpre = dact * _dsilu(pres[j])
            w = cw[:, j * 256:(j + 1) * 256]
            ext = jnp.concatenate([dpre, carry[:, j * 256:(j + 1) * 256]], axis=0)
            out_ref[...] = _b(w[3:4] * ext[0:tb] + w[2:3] * ext[1:1 + tb] + w[1:2] * ext[2:2 + tb] + w[0:1] * ext[3:3 + tb])
            carry[:, j * 256:(j + 1) * 256] = dpre[0:8]
            xe = exts[j]
            dcw_parts.append(jnp.concatenate([_colsum(dpre * xe[5 + k:5 + k + tb]) for k in range(4)]
                                             + [jnp.zeros((4, GROUP_W), f32)], axis=0))
            dcb_parts.append(_colsum(dpre))
        _acc(dcw_ref, jnp.concatenate(dcw_parts, axis=1))
        _acc(dcb_ref, jnp.concatenate(dcb_parts, axis=1))

    def col(c):
        return pl.BlockSpec((tb, GROUP_W), lambda i: (nt - 1 - i, c))

    def prev(c):
        return pl.BlockSpec((8, GROUP_W), lambda i: (jnp.maximum((nt - 1 - i) * (tb // 8) - 1, 0), c))

    out = pl.BlockSpec((tb, GROUP_W), lambda i: (nt - 1 - i, 0))
    vec = _full((1, LANES))
    return _call(
        body, name="ssd_bwd", grid=(nt,),
        out_shape=[jax.ShapeDtypeStruct((t, GROUP_W), bf16)] * 4 + [jax.ShapeDtypeStruct((t, LANES), bf16),
                   jax.ShapeDtypeStruct((8, 768), f32), jax.ShapeDtypeStruct((1, 768), f32)]
        + [jax.ShapeDtypeStruct((1, LANES), f32)] * 3,
        in_specs=[col(C_Z), col(C_XS), col(C_BM), col(C_CM), prev(C_XS), prev(C_BM), prev(C_CM),
                  pl.BlockSpec((tb, LANES), lambda i: (nt - 1 - i, C_DT128)), out, out,
                  pl.BlockSpec((cpt, 2, 128, 128), lambda i: (nt - 1 - i, 0, 0, 0)),
                  _full((8, 768)), _full((1, 768)), vec, vec, _full((1, GROUP_W))],
        out_specs=[out, out, out, out, pl.BlockSpec((tb, LANES), lambda i: (nt - 1 - i, 0)),
                   _full((8, 768)), _full((1, 768)), vec, vec, vec],
        scratch_shapes=[pltpu.VMEM((2, 128, 128), f32), pltpu.VMEM((8, 768), f32)],
        semantics=("arbitrary",), xchg=xchg,
        args=(proj, proj, proj, proj, proj, proj, proj, proj, dyc, y_pre, states, conv_w, conv_b, dt_bias, a_log, d_cols))


def _s5_coeffs(are, aim, ls):
    step = jnp.exp(ls)
    mag = jnp.exp(are * step)
    th = aim * step
    lre, lim = mag * jnp.cos(th), mag * jnp.sin(th)
    den = are * are + aim * aim
    nr = lre - 1.0
    fre = (nr * are + lim * aim) / den
    fim = (lim * are - nr * aim) / den
    return step, lre, lim, den, fre, fim


def _s5_prep(are, aim, ls, bre_bd, bim_bd):
    def body(are_ref, aim_ref, ls_ref, bre_ref, bim_ref, lre_ref, lim_ref, bbr_ref, bbi_ref):
        _, lre, lim, _, fre, fim = _s5_coeffs(are_ref[...], aim_ref[...], ls_ref[...])
        lre_ref[...] = lre
        lim_ref[...] = lim
        bre, bim = bre_ref[...], bim_ref[...]
        bbr_ref[...] = fre * bre - fim * bim
        bbi_ref[...] = fre * bim + fim * bre

    col = jax.ShapeDtypeStruct((S5_N, 1), f32)
    mat = jax.ShapeDtypeStruct((S5_N, GROUP_W), f32)
    return pl.pallas_call(body, name="s5_prep", out_shape=[col, col, mat, mat], in_specs=[VMEM] * 5, out_specs=[VMEM] * 4,
                          compiler_params=_params())(are, aim, ls, bre_bd, bim_bd)


def _s5_prep_bwd(are, aim, ls, bre_bd, bim_bd, dlre, dlim, dbbr, dbbi):
    def body(are_ref, aim_ref, ls_ref, bre_ref, bim_ref, dlre_ref, dlim_ref, dbbr_ref, dbbi_ref,
             dare_ref, daim_ref, dls_ref, dbre_ref, dbim_ref):
        are, aim = are_ref[...], aim_ref[...]
        step, lre, lim, den, fre, fim = _s5_coeffs(are, aim, ls_ref[...])
        r = lax.broadcasted_iota(jnp.int32, (S5_N, GROUP_W), 0) // 64
        c = lax.broadcasted_iota(jnp.int32, (S5_N, GROUP_W), 1) // 16
        mask = r == c
        gr = jnp.where(mask, dbbr_ref[...], 0.0)
        gi = jnp.where(mask, dbbi_ref[...], 0.0)
        bre, bim = bre_ref[...], bim_ref[...]
        dbre_ref[...] = fre * gr + fim * gi
        dbim_ref[...] = fre * gi - fim * gr
        dfre = jnp.sum(bre * gr + bim * gi, axis=1, keepdims=True)
        dfim = jnp.sum(bre * gi - bim * gr, axis=1, keepdims=True)
        ire, iim = are / den, aim / den
        tre = dlre_ref[...] + ire * dfre - iim * dfim
        tim = dlim_ref[...] + ire * dfim + iim * dfre
        dzre = lre * tre + lim * tim
        dzim = lre * tim - lim * tre
        qre = (fre * are + fim * aim) / den
        qim = (fim * are - fre * aim) / den
        dare_ref[...] = step * dzre - (qre * dfre + qim * dfim)
        daim_ref[...] = step * dzim - (qre * dfim - qim * dfre)
        dls = (are * dzre + aim * dzim) * step
        sel = (lax.broadcasted_iota(jnp.int32, (S5_N, LANES), 0) // 64 == lax.broadcasted_iota(jnp.int32, (S5_N, LANES), 1)).astype(f32)
        dls_ref[...] = lax.dot_general(sel, jnp.broadcast_to(dls, (S5_N, LANES)), (((0,), (0,)), ((), ())),
                                       preferred_element_type=f32, precision=lax.Precision.HIGHEST)

    col = jax.ShapeDtypeStruct((S5_N, 1), f32)
    mat = jax.ShapeDtypeStruct((S5_N, GROUP_W), f32)
    return pl.pallas_call(body, name="s5_prep_bwd", out_shape=[col, col, jax.ShapeDtypeStruct((LANES, LANES), f32), mat, mat],
                          in_specs=[VMEM] * 9, out_specs=[VMEM] * 5, compiler_params=_params(),
                          )(are, aim, ls, bre_bd, bim_bd, dlre, dlim, dbbr, dbbi)


def _cmul(ar, ai, br, bi):
    return ar * br - ai * bi, ar * bi + ai * br


def _s5_scan(re_ref, im_ref, carry_ref, mr, mi, n_groups, reverse):
    p1 = (mr, mi)
    p2 = _cmul(*p1, *p1)
    p3 = _cmul(*p2, *p1)
    p4 = _cmul(*p2, *p2)
    p5 = _cmul(*p4, *p1)
    p6 = _cmul(*p4, *p2)
    p7 = _cmul(*p4, *p3)
    p8 = _cmul(*p4, *p4)
    pows = [p1, p2, p3, p4, p5, p6, p7, p8]
    row = lax.broadcasted_iota(jnp.int32, (8, S5_N), 0)
    tr = jnp.zeros((8, S5_N), f32)
    ti = jnp.zeros((8, S5_N), f32)
    for i in range(8):
        p = pows[7 - i] if reverse else pows[i]
        tr = jnp.where(row == i, p[0], tr)
        ti = jnp.where(row == i, p[1], ti)
    steps = []
    for k, p in ((1, p1), (2, p2), (4, p4)):
        keep = (row + k < 8) if reverse else (row >= k)
        steps.append((8 - k if reverse else k, jnp.where(keep, p[0], 0.0), jnp.where(keep, p[1], 0.0)))
    edge = 0 if reverse else 7

    def step(j, carry):
        cr, ci = carry
        g = (n_groups - 1 - j) if reverse else j
        r0 = pl.multiple_of(g * 8, 8)
        xr = re_ref[pl.ds(r0, 8), :]
        xi = im_ref[pl.ds(r0, 8), :]
        for shift, br, bi in steps:
            sr = pltpu.roll(xr, shift, 0)
            si = pltpu.roll(xi, shift, 0)
            xr, xi = xr + br * sr - bi * si, xi + br * si + bi * sr
        xr, xi = xr + tr * cr - ti * ci, xi + tr * ci + ti * cr
        re_ref[pl.ds(r0, 8), :] = xr
        im_ref[pl.ds(r0, 8), :] = xi
        return (jnp.broadcast_to(xr[edge:edge + 1, :], (8, S5_N)), jnp.broadcast_to(xi[edge:edge + 1, :], (8, S5_N)))

    cr, ci = lax.fori_loop(0, n_groups, step, (carry_ref[0], carry_ref[1]))
    carry_ref[0] = cr
    carry_ref[1] = ci


def _s5_output(u, xr, xi, ctr, cti, d):
    return _dot_nt(_b(xr), _b(ctr)) - _dot_nt(_b(xi), _b(cti)) + d * u


def _s5_fwd(proj, bbr, bbi, ctr, cti, lre, lim, d, glu_w, glu_b, tb, xchg=None):
    t = proj.shape[0]

    def body(u_ref, bbr_ref, bbi_ref, ctr_ref, cti_ref, lr_ref, li_ref, d_ref, gw_ref, gb_ref, o_ref, xr_ref, xi_ref, carry):
        @pl.when(pl.program_id(0) == 0)
        def _():
            carry[...] = jnp.zeros_like(carry)

        u = u_ref[...]
        ub = _b(u)
        xr_ref[...] = _dot_nt(ub, _b(bbr_ref[...]))
        xi_ref[...] = _dot_nt(ub, _b(bbi_ref[...]))
        _s5_scan(xr_ref, xi_ref, carry, lr_ref[...], li_ref[...], tb // 8, reverse=False)
        y = _s5_output(u, xr_ref[...], xi_ref[...], ctr_ref[...], cti_ref[...], d_ref[...])
        gl = _gelu(y)
        o_ref[...] = _b(gl * jax.nn.sigmoid(_dot(_b(gl), _b(gw_ref[...])) + gb_ref[...]))

    state = pl.BlockSpec((tb, S5_N), lambda i: (i, 0))
    return _call(
        body, name="s5_fwd", grid=(t // tb,),
        out_shape=[jax.ShapeDtypeStruct((t, GROUP_W), bf16), jax.ShapeDtypeStruct((t, S5_N), f32), jax.ShapeDtypeStruct((t, S5_N), f32)],
        in_specs=[pl.BlockSpec((tb, GROUP_W), lambda i: (i, C_S5)), _full((S5_N, GROUP_W)), _full((S5_N, GROUP_W)),
                  _full((GROUP_W, S5_N)), _full((GROUP_W, S5_N)), _full((1, S5_N)), _full((1, S5_N)),
                  _full((1, GROUP_W)), _full((GROUP_W, GROUP_W)), _full((1, GROUP_W))],
        out_specs=[pl.BlockSpec((tb, GROUP_W), lambda i: (i, 0)), state, state],
        scratch_shapes=[pltpu.VMEM((2, 8, S5_N), f32)],
        semantics=("arbitrary",), xchg=xchg, args=(proj, bbr, bbi, ctr, cti, lre, lim, d, glu_w, glu_b))


def _s5_bwd(proj, dyd, xr_all, xi_all, bbr, bbi, ctr, cti, lre, lim, d, glu_w, glu_b, tb, xchg=None):
    t = proj.shape[0]
    nt = t // tb

    def body(u_ref, dy_ref, xr_ref, xi_ref, xrh_ref, xih_ref, bbr_ref, bbi_ref, ctr_ref, cti_ref, lr_ref, li_ref,
             d_ref, gw_ref, gb_ref,
             du_ref, dlr_ref, dli_ref, dbbr_ref, dbbi_ref, dctr_ref, dcti_ref, dd_ref, dgw_ref, dgb_ref,
             gr_ref, gi_ref, carry):
        i = pl.program_id(0)
        ti = nt - 1 - i

        @pl.when(i == 0)
        def _():
            carry[...] = jnp.zeros_like(carry)

        u = u_ref[...]
        ub = _b(u)
        xr, xi = xr_ref[...], xi_ref[...]
        ctr, cti = _b(ctr_ref[...]), _b(cti_ref[...])
        d = d_ref[...]
        gw = _b(gw_ref[...])
        y = _s5_output(u, xr, xi, ctr, cti, d)
        gl = _gelu(y)
        sg = jax.nn.sigmoid(_dot(_b(gl), gw) + gb_ref[...])
        dout = dy_ref[...]
        q = dout * gl * sg * (1.0 - sg)
        qb = _b(q)
        dgl = dout * sg + _dot_nt(qb, gw)
        _acc(dgw_ref, _dot_tn(_b(gl), qb))
        _acc(dgb_ref, _colsum(q))
        dyv = dgl * _dgelu(y)
        _acc(dd_ref, _colsum(dyv * u))
        dyb = _b(dyv)
        gr_ref[...] = _dot(dyb, ctr)
        gi_ref[...] = -_dot(dyb, cti)
        _acc(dctr_ref, _dot_tn(dyb, _b(xr)))
        _acc(dcti_ref, -_dot_tn(dyb, _b(xi)))
        _s5_scan(gr_ref, gi_ref, carry, lr_ref[...], -li_ref[...], tb // 8, reverse=True)
        gr, gi = gr_ref[...], gi_ref[...]
        xpr = jnp.concatenate([jnp.where(ti > 0, xrh_ref[...], 0.0), xr], axis=0)[7:7 + tb]
        xpi = jnp.concatenate([jnp.where(ti > 0, xih_ref[...], 0.0), xi], axis=0)[7:7 + tb]
        _acc(dlr_ref, _colsum(gr * xpr + gi * xpi))
        _acc(dli_ref, _colsum(gi * xpr - gr * xpi))
        grb, gib = _b(gr), _b(gi)
        _acc(dbbr_ref, _dot_tn(grb, ub))
        _acc(dbbi_ref, _dot_tn(gib, ub))
        du_ref[...] = _b(dyv * d + _dot(grb, _b(bbr_ref[...])) + _dot(gib, _b(bbi_ref[...])))

    state = pl.BlockSpec((tb, S5_N), lambda i: (nt - 1 - i, 0))
    prev = pl.BlockSpec((8, S5_N), lambda i: (jnp.maximum((nt - 1 - i) * (tb // 8) - 1, 0), 0))
    tile = pl.BlockSpec((tb, GROUP_W), lambda i: (nt - 1 - i, 0))
    return _call(
        body, name="s5_bwd", grid=(nt,),
        out_shape=[jax.ShapeDtypeStruct((t, GROUP_W), bf16), jax.ShapeDtypeStruct((1, S5_N), f32), jax.ShapeDtypeStruct((1, S5_N), f32),
                   jax.ShapeDtypeStruct((S5_N, GROUP_W), f32), jax.ShapeDtypeStruct((S5_N, GROUP_W), f32),
                   jax.ShapeDtypeStruct((GROUP_W, S5_N), f32), jax.ShapeDtypeStruct((GROUP_W, S5_N), f32),
                   jax.ShapeDtypeStruct((1, GROUP_W), f32), jax.ShapeDtypeStruct((GROUP_W, GROUP_W), f32),
                   jax.ShapeDtypeStruct((1, GROUP_W), f32)],
        in_specs=[pl.BlockSpec((tb, GROUP_W), lambda i: (nt - 1 - i, C_S5)), tile, state, state, prev, prev,
                  _full((S5_N, GROUP_W)), _full((S5_N, GROUP_W)), _full((GROUP_W, S5_N)), _full((GROUP_W, S5_N)),
                  _full((1, S5_N)), _full((1, S5_N)), _full((1, GROUP_W)), _full((GROUP_W, GROUP_W)), _full((1, GROUP_W))],
        out_specs=[tile, _full((1, S5_N)), _full((1, S5_N)), _full((S5_N, GROUP_W)), _full((S5_N, GROUP_W)),
                   _full((GROUP_W, S5_N)), _full((GROUP_W, S5_N)), _full((1, GROUP_W)), _full((GROUP_W, GROUP_W)), _full((1, GROUP_W))],
        scratch_shapes=[pltpu.VMEM((tb, S5_N), f32), pltpu.VMEM((tb, S5_N), f32), pltpu.VMEM((2, 8, S5_N), f32)],
        semantics=("arbitrary",), xchg=xchg,
        args=(proj, dyd, xr_all, xi_all, xr_all, xi_all, bbr, bbi, ctr, cti, lre, lim, d, glu_w, glu_b))


def _outproj_fwd(ys, h, bn_w, g1, w_out, tb, xchg=None):
    t = h.shape[0]

    def body(ya_ref, yb_ref, yc_ref, yd_ref, h_ref, bn_ref, g1_ref, w_ref, h1_ref, o_ref, gr_ref):
        bn = bn_ref[...]
        parts = []
        for g, r in enumerate((ya_ref, yb_ref, yc_ref, yd_ref)):
            n, _ = _rms(r[...].astype(f32))
            parts.append(n * bn[:, g * GROUP_W:(g + 1) * GROUP_W])
        groups = _b(jnp.concatenate(parts, axis=1))
        gr_ref[...] = groups
        o = _dot(groups, w_ref[...])
        o_ref[...] = _b(o)
        h1_ref[...] = h_ref[...] + g1_ref[...] * o

    grp = pl.BlockSpec((tb, GROUP_W), lambda i: (i, 0))
    row = pl.BlockSpec((tb, D_MODEL), lambda i: (i, 0))
    vec = _full((1, D_MODEL))
    return _call(
        body, name="outproj_fwd", grid=(t // tb,),
        out_shape=[jax.ShapeDtypeStruct((t, D_MODEL), f32), jax.ShapeDtypeStruct((t, D_MODEL), bf16),
                   jax.ShapeDtypeStruct((t, D_MODEL), bf16)],
        in_specs=[grp, grp, grp, grp, row, vec, vec, _full((D_MODEL, D_MODEL))],
        out_specs=[row, row, row],
        semantics=("parallel",), xchg=xchg, args=(*ys, h, bn_w, g1, w_out))


def _outproj_bwd(dh1, o, ys, bn_w, g1, w_out, tb):
    t = dh1.shape[0]

    def body(dh_ref, o_ref, ya_ref, yb_ref, yc_ref, yd_ref, bn_ref, g1_ref, w_ref,
             da_ref, db_ref, dc_ref, dd_ref, do_ref, dg1_ref, dbn_ref):
        dh = dh_ref[...]
        _acc(dg1_ref, _colsum(dh * o_ref[...].astype(f32)))
        do = _b(dh * g1_ref[...])
        do_ref[...] = do
        dgroups = _dot_nt(do, w_ref[...])
        bn = bn_ref[...]
        dbn = []
        for g, (r, dr) in enumerate(((ya_ref, da_ref), (yb_ref, db_ref), (yc_ref, dc_ref), (yd_ref, dd_ref))):
            n, rr = _rms(r[...].astype(f32))
            dgr = dgroups[:, g * GROUP_W:(g + 1) * GROUP_W]
            dbn.append(_colsum(dgr * n))
            dr[...] = _rms_bwd(dgr * bn[:, g * GROUP_W:(g + 1) * GROUP_W], n, rr)
        _acc(dbn_ref, jnp.concatenate(dbn, axis=1))

    grp = pl.BlockSpec((tb, GROUP_W), lambda i: (i, 0))
    row = pl.BlockSpec((tb, D_MODEL), lambda i: (i, 0))
    vec = _full((1, D_MODEL))
    return pl.pallas_call(
        body, name="outproj_bwd", grid=(t // tb,),
        out_shape=[jax.ShapeDtypeStruct((t, GROUP_W), f32)] * 4 + [jax.ShapeDtypeStruct((t, D_MODEL), bf16),
                   jax.ShapeDtypeStruct((1, D_MODEL), f32), jax.ShapeDtypeStruct((1, D_MODEL), f32)],
        in_specs=[row, row, grp, grp, grp, grp, vec, vec, _full((D_MODEL, D_MODEL))],
        out_specs=[grp, grp, grp, grp, row, vec, vec],
        compiler_params=_params(("arbitrary",)),
    )(dh1, o, *ys, bn_w, g1, w_out)


def _mlp_fwd(h1, norm_w, sc, sh, g2, w1, w2, tb, xchg=None, head=None):
    t = h1.shape[0]
    nh = w1.shape[0] // MLP_SLABS
    n_head = 0 if head is None else 2

    def body(*refs):
        h_ref, nw_ref, sc_ref, sh_ref, g2_ref, w1_ref, w2_ref = refs[:7]
        head_refs, outs = refs[7:7 + n_head], refs[7 + n_head:]
        h2_ref, m_ref, v_ref, r_ref, acc = outs[0], outs[1], outs[2], outs[3], outs[-1]
        j = pl.program_id(1)

        @pl.when(j == 0)
        def _():
            n, _ = _rms(h_ref[...])
            v_ref[...] = _b(n * nw_ref[...] * (1.0 + sc_ref[...]) + sh_ref[...])

        v = v_ref[...]
        p = None
        for s in range(MLP_SLABS):
            ra = jnp.maximum(_dot(v, w1_ref[s]), 0.0)
            r = _b(ra * ra)
            r_ref[:, s * MLP_HB:(s + 1) * MLP_HB] = r
            q = _dot(r, w2_ref[s])
            p = q if p is None else p + q

        @pl.when(j == 0)
        def _():
            acc[...] = p

        @pl.when(j != 0)
        def _():
            acc[...] += p

        @pl.when(j == nh - 1)
        def _():
            m_ref[...] = _b(acc[...])
            if head is None:
                h2_ref[...] = h_ref[...] + g2_ref[...] * acc[...]
            else:
                tgt_ref, fw_ref = head_refs
                loss_ref, dfw_ref = outs[4], outs[5]
                fw, g2 = fw_ref[...], g2_ref[...]
                ch = min(HEAD_ROWS, tb)

                def chunk(c, carry):
                    sq, dfw = carry
                    rows = pl.ds(pl.multiple_of(c * ch, ch), ch)
                    n, r = _rms(h_ref[rows, :] + g2 * acc[rows, :])
                    err = n * fw - tgt_ref[rows, :]
                    dy = err / D_MODEL
                    h2_ref[rows, :] = _rms_bwd(dy * fw, n, r)
                    return (sq + jnp.sum(jnp.sum(err * err, axis=1, keepdims=True), axis=0, keepdims=True),
                            dfw + _colsum(dy * n))

                sq, dfw = lax.fori_loop(0, tb // ch, chunk, (jnp.zeros((1, 1), f32), jnp.zeros((1, D_MODEL), f32)))
                _acc(loss_ref, jnp.broadcast_to(0.5 * sq / D_MODEL, (8, LANES)))
                _acc(dfw_ref, dfw)

    row = pl.BlockSpec((tb, D_MODEL), lambda i, j: (i, 0))
    hid = pl.BlockSpec((tb, MLP_SLABS * MLP_HB), lambda i, j: (i, j))
    vec = _full((1, D_MODEL))
    head_shapes = [] if head is None else [jax.ShapeDtypeStruct((8, LANES), f32), jax.ShapeDtypeStruct((1, D_MODEL), f32)]
    return _call(
        body, name="mlp_fwd", grid=(t // tb, nh),
        out_shape=[jax.ShapeDtypeStruct((t, D_MODEL), f32), jax.ShapeDtypeStruct((t, D_MODEL), bf16),
                   jax.ShapeDtypeStruct((t, D_MODEL), bf16), jax.ShapeDtypeStruct((t, N_DEV * MLP_HB), bf16)] + head_shapes,
        in_specs=[row, vec, vec, vec, vec, pl.BlockSpec((MLP_SLABS, D_MODEL, MLP_HB), lambda i, j: (j, 0, 0)),
                  pl.BlockSpec((MLP_SLABS, MLP_HB, D_MODEL), lambda i, j: (j, 0, 0))] + ([] if head is None else [row, vec]),
        out_specs=[row, row, row, hid] + ([] if head is None else [_full((8, LANES)), vec]),
        scratch_shapes=[pltpu.VMEM((tb, D_MODEL), f32)],
        semantics=("arbitrary", "arbitrary"), xchg=xchg, args=(h1, norm_w, sc, sh, g2, w1, w2) + (() if head is None else tuple(head)))


def _mlp_bwd(dh2, m, h1, r, norm_w, sc, sh, g2, w1, w2, tb, xchg=None):
    t = h1.shape[0]
    slabs = MLP_BWD_SLABS
    nh = w1.shape[0] // slabs

    def body(dh_ref, m_ref, h_ref, r_ref, nw_ref, sc_ref, sh_ref, g2_ref, w1_ref, w2_ref,
             dh1_ref, do_ref, da_ref, dg2_ref, dsh_ref, dsc_ref, dnw_ref, acc):
        j = pl.program_id(1)

        @pl.when(j == 0)
        def _():
            dh = dh_ref[...]
            _acc(dg2_ref, _colsum(dh * m_ref[...].astype(f32)))
            do_ref[...] = _b(dh * g2_ref[...])

        do = do_ref[...]
        p = None
        for s in range(slabs):
            cols = slice(s * MLP_HB, (s + 1) * MLP_HB)
            dr = _dot_nt(do, w2_ref[s])
            da = _b(dr * 2.0 * jnp.sqrt(r_ref[:, cols].astype(f32)))
            da_ref[:, cols] = da
            q = _dot_nt(da, w1_ref[s])
            p = q if p is None else p + q

        @pl.when(j == 0)
        def _():
            acc[...] = p

        @pl.when(j != 0)
        def _():
            acc[...] += p

        @pl.when(j == nh - 1)
        def _():
            dv = acc[...]
            n, r = _rms(h_ref[...])
            nw = nw_ref[...]
            gain = 1.0 + sc_ref[...]
            _acc(dsh_ref, _colsum(dv))
            _acc(dsc_ref, _colsum(dv * n * nw))
            _acc(dnw_ref, _colsum(dv * gain * n))
            dh1_ref[...] = dh_ref[...] + _rms_bwd(dv * nw * gain, n, r)

    row = pl.BlockSpec((tb, D_MODEL), lambda i, j: (i, 0))
    hid = pl.BlockSpec((tb, slabs * MLP_HB), lambda i, j: (i, j))
    vec = _full((1, D_MODEL))
    once = dict(pipeline_mode=pl.Buffered(1)) if nh == 1 else {}
    return _call(
        body, name="mlp_bwd", grid=(t // tb, nh),
        out_shape=[jax.ShapeDtypeStruct((t, D_MODEL), f32), jax.ShapeDtypeStruct((t, D_MODEL), bf16),
                   jax.ShapeDtypeStruct((t, N_DEV * MLP_HB), bf16)] + [jax.ShapeDtypeStruct((1, D_MODEL), f32)] * 4,
        in_specs=[row, row, row, hid, vec, vec, vec, vec,
                  pl.BlockSpec((slabs, D_MODEL, MLP_HB), lambda i, j: (j, 0, 0), **once),
                  pl.BlockSpec((slabs, MLP_HB, D_MODEL), lambda i, j: (j, 0, 0), **once)],
        out_specs=[row, row, hid, vec, vec, vec, vec],
        scratch_shapes=[pltpu.VMEM((tb, D_MODEL), f32)],
        semantics=("arbitrary", "arbitrary"), xchg=xchg, args=(dh2, m, h1, r, norm_w, sc, sh, g2, w1, w2))


def _adam_math(w, g, m, v):
    m2 = ADAM_B1 * m + (1.0 - ADAM_B1) * g
    v2 = ADAM_B2 * v + (1.0 - ADAM_B2) * (g * g)
    mh = m2 / (1.0 - ADAM_B1 ** ADAM_STEP)
    vh = v2 / (1.0 - ADAM_B2 ** ADAM_STEP)
    return -ADAM_LR * (mh / (jnp.sqrt(vh) + ADAM_EPS) + ADAM_WD * w), m2, v2


def _adamw_small(ws, gs, ms, vs):
    n = len(ws)
    shapes = [w.shape for w in ws]
    as2d = [(1,) + s if len(s) == 1 else s for s in shapes]
    flat = [x.reshape(s) for group in (ws, gs, ms, vs) for x, s in zip(group, as2d)]

    def body(*refs):
        w_refs, g_refs, m_refs, v_refs, outs = refs[:n], refs[n:2 * n], refs[2 * n:3 * n], refs[3 * n:4 * n], refs[4 * n:]
        for i in range(n):
            d, m2, v2 = _adam_math(w_refs[i][...], g_refs[i][...], m_refs[i][...], v_refs[i][...])
            outs[3 * i][...] = d
            outs[3 * i + 1][...] = m2
            outs[3 * i + 2][...] = v2

    res = pl.pallas_call(body, name="adamw_small", out_shape=[jax.ShapeDtypeStruct(s, f32) for s in as2d for _ in range(3)],
                         in_specs=[VMEM] * (4 * n), out_specs=[VMEM] * (3 * n), compiler_params=_params())(*flat)
    return [r.reshape(shapes[i // 3]) for i, r in enumerate(res)]


def _sum_adamw_layers(parts0, parts1, w, m, v, name, rb):
    n_src, r, c = parts0.shape
    nb = r // rb

    def body(p0_ref, p1_ref, w_ref, m_ref, v_ref, g_ref, d_ref, m2_ref, v2_ref):
        def update(p_ref):
            g = p_ref[0].astype(f32)
            for s in range(1, n_src):
                g = g + p_ref[s].astype(f32)
            g_ref[0] = g
            d, m2, v2 = _adam_math(w_ref[0], g, m_ref[0], v_ref[0])
            d_ref[0] = d
            m2_ref[0] = m2
            v2_ref[0] = v2

        @pl.when(pl.program_id(0) == 0)
        def _():
            update(p0_ref)

        @pl.when(pl.program_id(0) == 1)
        def _():
            update(p1_ref)

    blk = pl.BlockSpec((1, rb, c), lambda l, i: (l, i, 0))
    return pl.pallas_call(
        body, name=name, grid=(2, nb),
        out_shape=[jax.ShapeDtypeStruct((2, r, c), f32)] * 4,
        in_specs=[pl.BlockSpec((n_src, rb, c), lambda l, i: (0, jnp.where(l == 0, i, nb - 1), 0)),
                  pl.BlockSpec((n_src, rb, c), lambda l, i: (0, jnp.where(l == 1, i, 0), 0)), blk, blk, blk],
        out_specs=[blk] * 4,
        compiler_params=_params(("arbitrary", "arbitrary")),
    )(parts0, parts1, w, m, v)


def _reorder_in(w):
    pad = jnp.zeros(w.shape[:-1] + (P_IN - 2308,), w.dtype)
    return jnp.concatenate([w[..., :2048], w[..., 2052:2308], w[..., 2048:2052], pad], axis=-1)


def _unreorder_in(w):
    return jnp.concatenate([w[..., :2048], w[..., 2304:2308], w[..., 2048:2304]], axis=-1)


def _block_diag(w2d, n_blocks):
    rows, cols = w2d.shape
    tiled = jnp.tile(w2d, (1, n_blocks))
    rb = lax.broadcasted_iota(jnp.int32, tiled.shape, 0) // (rows // n_blocks)
    cb = lax.broadcasted_iota(jnp.int32, tiled.shape, 1) // cols
    return jnp.where(rb == cb, tiled, jnp.zeros_like(tiled))


def _block_diag_extract(w_bd, n_blocks):
    rows, wide = w_bd.shape
    r, c = rows // n_blocks, wide // n_blocks
    w4 = w_bd.reshape(n_blocks, r, n_blocks, c)
    idx = jnp.arange(n_blocks)
    return w4[idx, :, idx, :]


def _rows_of(shape):
    n = 1
    for d in shape:
        n *= d
    return -(-n // (8 * LANES)) * 8, n


def _flat_pack(arrs, row_multiple=8):
    blocks = []
    for a in arrs:
        rows, n = _rows_of(a.shape)
        blocks.append(jnp.pad(a.reshape(-1), (0, rows * LANES - n)).reshape(rows, LANES))
    total = sum(b.shape[0] for b in blocks)
    pad = -total % row_multiple
    if pad:
        blocks.append(jnp.zeros((pad, LANES), blocks[0].dtype))
    return jnp.concatenate(blocks, axis=0)


def _flat_unpack(packed, shapes):
    out, off = [], 0
    for s in shapes:
        rows, n = _rows_of(s)
        out.append(packed[off:off + rows].reshape(-1)[:n].reshape(s))
        off += rows
    return out


_W_NAMES = ['norm_mix_w', 'norm_mlp_w', 'ada_w', 'ada_b', 'w_in', 'pool_w', 'pool_scale', 'sconv_w', 'ssd_conv_w',
            'ssd_conv_b', 'ssd_dt_bias', 'ssd_a_log', 'ssd_d', 's5_a_re', 's5_a_im', 's5_log_step', 's5_b_re', 's5_b_im',
            's5_c_re', 's5_c_im', 's5_d', 's5_glu_w', 's5_glu_b', 'branch_norm_w', 'w_out', 'mlp_w1', 'mlp_w2',
            'final_norm_w']
_BIG = ('ada_w', 'w_in', 'w_out', 'mlp_w1', 'mlp_w2')
_SMALL = [n for n in _W_NAMES if n not in _BIG]
_SHARDED_SMALL = {'sconv_w': (2, 32), 'ssd_conv_w': (2, 96), 's5_glu_w': (1, 32)}


def _gather(*blocks):
    return _ChipGather(blocks)


def _scatter(*parts):
    return _Scatter(parts)


def _layer_forward(l, h, p, w, sh_b, tb, head=None):
    first = l == 0
    (proj, u_b), got = _inproj_fwd(h, p['norm_mix_w'][l], p['sc1'][l], p['sh1'][l], w['w_in', l], tb,
                                   xchg=_gather(sh_b[1][0]) if first else None)
    if first:
        w['w_out', 0] = got[0].reshape(D_MODEL, D_MODEL)
    ya = _pool_fwd(proj, p['pool_bd'][l], p['pool_scale'][l], tb)
    yb = _sconv_fwd(proj, p['sconv_w8'][l], tb)
    (yc, yc_pre, states), got = _ssd_fwd(proj, p['ssd_conv_w8'][l], p['ssd_conv_b'][l], p['ssd_dt_bias'][l], p['ssd_a_log'][l],
                                         p['ssd_d_cols'][l], tb, xchg=_gather(sh_b[2][0]) if first else None)
    if first:
        w['w1', 0] = got[0]
    (yd, xr, xi), got = _s5_fwd(proj, p['bbr'][l], p['bbi'][l], p['ctr'][l], p['cti'][l], p['lre'][l], p['lim'][l],
                                p['s5_d'][l], p['glu_w'][l], p['glu_b'][l], tb, xchg=_gather(sh_b[3][0]) if first else None)
    if first:
        w['w2', 0] = got[0]
    ys = (ya, yb, yc, yd)
    (h1, o, groups_b), got = _outproj_fwd(ys, h, p['branch_norm_w'][l], p['g1'][l], w['w_out', l], tb,
                                          xchg=_gather(sh_b[1][1]) if first else None)
    if first:
        w['w_out', 1] = got[0].reshape(D_MODEL, D_MODEL)
    (h2, m, v_b, r_b, *head_out), got = _mlp_fwd(
        h1, p['norm_mlp_w'][l], p['sc2'][l], p['sh2'][l], p['g2'][l], w['w1', l], w['w2', l],
        min(MLP_TB, h.shape[0]),
        xchg=_gather(sh_b[0][1], sh_b[2][1], sh_b[3][1]) if first else None, head=head)
    if first:
        w['w_in', 1] = got[0].reshape(D_MODEL, P_IN)
        w['w1', 1], w['w2', 1] = got[1], got[2]
    saved = dict(h=h, proj=proj, u_b=u_b, ys=ys, yc_pre=yc_pre, states=states, xr=xr, xi=xi, h1=h1, o=o,
                 groups_b=groups_b, m=m, v_b=v_b, r_b=r_b)
    return (h2, *head_out), saved


def _layer_backward(l, dh2, s, p, w, pending, recv, tb):
    def carry(names):
        names = [n for n in names if n in pending]
        return names, (_scatter(*[pending.pop(n) for n in names]) if names else None)

    def landed(names, got):
        for n, g in zip(names, got):
            recv[n] = g

    names, xchg = carry([('w_out', 1)])
    (dh1, do2_b, da_b, dg2, dsh2, dsc2, dnw_mlp), got = _mlp_bwd(dh2, s['m'], s['h1'], s['r_b'], p['norm_mlp_w'][l], p['sc2'][l],
                                                                p['sh2'][l], p['g2'][l], w['w1', l], w['w2', l], min(TB_BWD, tb),
                                                                xchg=xchg)
    landed(names, got)
    pending['mlp_w2', l] = _wgrad(s['r_b'], do2_b, 1, "wgrad_w2", tm=1024, tk=4096).reshape(N_DEV, MLP_HB, D_MODEL)
    pending['mlp_w1', l] = _wgrad(s['v_b'], da_b, N_DEV, "wgrad_w1", tm=1024, tk=4096)
    dya, dyb, dyc, dyd, do1_b, dg1, dbn = _outproj_bwd(dh1, s['o'], s['ys'], p['branch_norm_w'][l], p['g1'][l], w['w_out', l], tb)
    pending['w_out', l] = _wgrad(s['groups_b'], do1_b, 1, "wgrad_wout", tm=1024, tk=1024).reshape(N_DEV, D_MODEL // N_DEV, D_MODEL)
    proj = s['proj']
    dv, dpool_bd, dpool_scale = _pool_bwd(proj, dya, p['pool_bd'][l], p['pool_scale'][l], tb)
    dgb, dgc, dhh, dsconv = _sconv_bwd(proj, dyb, p['sconv_w8'][l], tb)
    names, xchg = carry([('mlp_w1', l)] + ([('w_out', 0)] if l == 0 else []))
    (dz, dxs, dbm, dcm, ddt, dconv_w, dconv_b, ddtb, dalog, ddskip), got = _ssd_bwd(
        proj, dyc, s['yc_pre'], s['states'], p['ssd_conv_w8'][l], p['ssd_conv_b'][l], p['ssd_dt_bias'][l], p['ssd_a_log'][l],
        p['ssd_d_cols'][l], min(TB_BWD, tb), xchg=xchg)
    landed(names, got)
    names, xchg = carry([('mlp_w2', l)])
    (du5, dlr, dli, dbbr, dbbi, dctr, dcti, dd5, dgw, dgb5), got = _s5_bwd(
        proj, dyd, s['xr'], s['xi'], p['bbr'][l], p['bbi'][l], p['ctr'][l], p['cti'][l], p['lre'][l], p['lim'][l],
        p['s5_d'][l], p['glu_w'][l], p['glu_b'][l], min(TB_BWD, tb), xchg=xchg)
    landed(names, got)
    dare, daim, dls, dbre_bd, dbim_bd = _s5_prep_bwd(p['are_c'][l], p['aim_c'][l], p['ls_c'][l], p['bre_bd'][l], p['bim_bd'][l],
                                                     dlr.reshape(S5_N, 1), dli.reshape(S5_N, 1), dbbr, dbbi)
    dparts = (dv, dgb, dgc, dhh, dz, dxs, dbm, dcm, du5, ddt)
    pending['w_in', l] = _wgrad_parts(s['u_b'], dparts, "wgrad_win", tm=1024, tk=1024).reshape(N_DEV, D_MODEL // N_DEV, P_IN)
    names, xchg = carry([('w_in', l)])
    (dh, dsh1, dsc1, dnw_mix), got = _inproj_bwd(dparts, dh1, s['h'], p['norm_mix_w'][l], p['sc1'][l], p['sh1'][l], w['w_in', l],
                                                 tb, xchg=xchg)
    landed(names, got)
    small = {
        'norm_mix_w': dnw_mix.reshape(D_MODEL), 'norm_mlp_w': dnw_mlp.reshape(D_MODEL),
        'ada_b': jnp.concatenate([dsh1, dsc1, dg1, dsh2, dsc2, dg2], axis=1).reshape(6 * D_MODEL),
        'pool_w': _block_diag_extract(dpool_bd, 4), 'pool_scale': dpool_scale.reshape(GROUP_W),
        'sconv_w': dsconv[0:3], 'ssd_conv_w': dconv_w[0:4], 'ssd_conv_b': dconv_b.reshape(768),
        'ssd_dt_bias': ddtb[0, 0:4], 'ssd_a_log': dalog[0, 0:4], 'ssd_d': ddskip[0, 0:4],
        's5_a_re': dare.reshape(16, 64), 's5_a_im': daim.reshape(16, 64), 's5_log_step': dls[0:16, 0],
        's5_b_re': _block_diag_extract(dbre_bd, 16), 's5_b_im': _block_diag_extract(dbim_bd, 16),
        's5_c_re': _block_diag_extract(dctr, 16), 's5_c_im': _block_diag_extract(dcti, 16),
        's5_d': dd5.reshape(GROUP_W), 's5_glu_w': dgw, 's5_glu_b': dgb5.reshape(GROUP_W),
        'branch_norm_w': dbn.reshape(D_MODEL),
    }
    return dh, small


def _prepare_params(a, me, w_in0_shard):
    pack_shapes = [(1, D_MODEL), (2, 3, 32), (2, 4, 96), (2, 32, GROUP_W)]
    packed = _flat_pack([a['c'], a['sconv_w'], a['ssd_conv_w'], a['s5_glu_w']])
    w_in0, gathered = _exchange_alone(_gather(w_in0_shard, packed), "gather_first")
    pieces = [_flat_unpack(gathered[d], pack_shapes) for d in range(N_DEV)]
    c_all = jnp.concatenate([pc[0] for pc in pieces], axis=0)
    sconv_full = jnp.concatenate([pc[1] for pc in pieces], axis=2)
    ssd_conv_full = jnp.concatenate([pc[2] for pc in pieces], axis=2)
    glu_full = jnp.concatenate([pc[3] for pc in pieces], axis=1)

    ada_b_cols = lax.dynamic_slice_in_dim(a['ada_b'], me * 768, 768, axis=1).reshape(2, 1, 768)
    cond, modrows = _ada_forward(c_all, a['ada_w'], ada_b_cols)
    mod_recv = _all_to_all_rows(modrows.transpose(1, 0, 2), "exchange_mod")
    mod = mod_recv.transpose(1, 0, 2).reshape(2, 6 * D_MODEL)
    p = {'cond': cond}
    for k, name in enumerate(('sh1', 'sc1', 'g1', 'sh2', 'sc2', 'g2')):
        p[name] = mod[:, k * D_MODEL:(k + 1) * D_MODEL].reshape(2, 1, D_MODEL)

    for name in ('norm_mix_w', 'norm_mlp_w', 'branch_norm_w'):
        p[name] = a[name].reshape(2, 1, D_MODEL)
    p['pool_bd'] = jnp.stack([_block_diag(a['pool_w'][l].reshape(GROUP_W, 64), 4) for l in range(2)])
    p['pool_scale'] = a['pool_scale'].reshape(2, 1, GROUP_W)
    p['sconv_w8'] = jnp.pad(sconv_full, ((0, 0), (0, 5), (0, 0)))
    p['ssd_conv_w8'] = jnp.pad(ssd_conv_full, ((0, 0), (0, 4), (0, 0)))
    p['ssd_conv_b'] = a['ssd_conv_b'].reshape(2, 1, 768)
    p['ssd_dt_bias'] = jnp.pad(a['ssd_dt_bias'], ((0, 0), (0, LANES - 4))).reshape(2, 1, LANES)
    p['ssd_a_log'] = jnp.pad(a['ssd_a_log'], ((0, 0), (0, LANES - 4))).reshape(2, 1, LANES)
    p['ssd_d_cols'] = jnp.repeat(a['ssd_d'], SSD_P, axis=1).reshape(2, 1, GROUP_W)
    p['are_c'] = a['s5_a_re'].reshape(2, S5_N, 1)
    p['aim_c'] = a['s5_a_im'].reshape(2, S5_N, 1)
    p['ls_c'] = jnp.repeat(a['s5_log_step'], 64, axis=1).reshape(2, S5_N, 1)
    p['bre_bd'] = jnp.stack([_block_diag(a['s5_b_re'][l].reshape(S5_N, 16), 16) for l in range(2)])
    p['bim_bd'] = jnp.stack([_block_diag(a['s5_b_im'][l].reshape(S5_N, 16), 16) for l in range(2)])
    p['ctr'] = jnp.stack([_block_diag(a['s5_c_re'][l].reshape(GROUP_W, 64), 16) for l in range(2)])
    p['cti'] = jnp.stack([_block_diag(a['s5_c_im'][l].reshape(GROUP_W, 64), 16) for l in range(2)])
    p['s5_d'] = a['s5_d'].reshape(2, 1, GROUP_W)
    p['glu_w'] = glu_full
    p['glu_b'] = a['s5_glu_b'].reshape(2, 1, GROUP_W)
    lre, lim, bbr, bbi = [], [], [], []
    for l in range(2):
        r = _s5_prep(p['are_c'][l], p['aim_c'][l], p['ls_c'][l], p['bre_bd'][l], p['bim_bd'][l])
        lre.append(r[0].reshape(1, S5_N))
        lim.append(r[1].reshape(1, S5_N))
        bbr.append(r[2])
        bbi.append(r[3])
    p['lre'], p['lim'], p['bbr'], p['bbi'] = lre, lim, bbr, bbi
    return p, w_in0


def kernel(x, c, norm_mix_w, norm_mlp_w, ada_w, ada_b, w_in, pool_w, pool_scale, sconv_w, ssd_conv_w, ssd_conv_b, ssd_dt_bias, ssd_a_log, ssd_d, s5_a_re, s5_a_im, s5_log_step, s5_b_re, s5_b_im, s5_c_re, s5_c_im, s5_d, s5_glu_w, s5_glu_b, branch_norm_w, w_out, mlp_w1, mlp_w2, final_norm_w, loss_target, m_norm_mix_w, m_norm_mlp_w, m_ada_w, m_ada_b, m_w_in, m_pool_w, m_pool_scale, m_sconv_w, m_ssd_conv_w, m_ssd_conv_b, m_ssd_dt_bias, m_ssd_a_log, m_ssd_d, m_s5_a_re, m_s5_a_im, m_s5_log_step, m_s5_b_re, m_s5_b_im, m_s5_c_re, m_s5_c_im, m_s5_d, m_s5_glu_w, m_s5_glu_b, m_branch_norm_w, m_w_out, m_mlp_w1, m_mlp_w2, m_final_norm_w, v_norm_mix_w, v_norm_mlp_w, v_ada_w, v_ada_b, v_w_in, v_pool_w, v_pool_scale, v_sconv_w, v_ssd_conv_w, v_ssd_conv_b, v_ssd_dt_bias, v_ssd_a_log, v_ssd_d, v_s5_a_re, v_s5_a_im, v_s5_log_step, v_s5_b_re, v_s5_b_im, v_s5_c_re, v_s5_c_im, v_s5_d, v_s5_glu_w, v_s5_glu_b, v_branch_norm_w, v_w_out, v_mlp_w1, v_mlp_w2, v_final_norm_w):
    a = dict(locals())
    t = x.shape[1]
    tb = min(TB, t)
    me = _my_index()
    sh_b = _cast_shards([_reorder_in(w_in), w_out, mlp_w1, mlp_w2])
    p, w_in0 = _prepare_params(a, me, sh_b[0][0])
    w = {('w_in', 0): w_in0.reshape(D_MODEL, P_IN)}

    h = x.reshape(t, D_MODEL)
    saved = []
    (h,), s = _layer_forward(0, h, p, w, sh_b, tb)
    saved.append(s)
    (dh, loss_blk, dfinal), s = _layer_forward(1, h, p, w, sh_b, tb,
                                               head=(loss_target.reshape(t, D_MODEL), final_norm_w.reshape(1, D_MODEL)))
    saved.append(s)

    pending, recv, small_parts = {}, {}, [None, None]
    for l in (1, 0):
        dh, small_parts[l] = _layer_backward(l, dh, saved[l], p, w, pending, recv, tb)
    grad_x = dh.reshape(1, t, D_MODEL)

    grads, deltas, new_m, new_v = {}, {}, {}, {}

    wmv_in = [_reorder_in(a[n]) for n in ('w_in', 'm_w_in', 'v_w_in')]
    outs = _sum_adamw_layers(recv['w_in', 0], recv['w_in', 1], *wmv_in, "adamw_w_in", 128)
    grads['w_in'], deltas['w_in'], new_m['w_in'], new_v['w_in'] = [_unreorder_in(o) for o in outs]
    for name, rb in (('w_out', 128), ('mlp_w1', 256), ('mlp_w2', 256)):
        grads[name], deltas[name], new_m[name], new_v[name] = _sum_adamw_layers(
            recv[name, 0], recv[name, 1], a[name], a['m_' + name], a['v_' + name], "adamw_" + name, rb)

    dmod = jnp.stack([small_parts[0]['ada_b'], small_parts[1]['ada_b']])
    dmod_recv = _all_to_all_rows(dmod.reshape(2, N_DEV, 768).transpose(1, 0, 2), "exchange_dmod")
    g_ada = _ada_backward(p['cond'], dmod_recv.transpose(1, 0, 2))
    grads['ada_w'], deltas['ada_w'], new_m['ada_w'], new_v['ada_w'] = _sum_adamw_layers(
        g_ada[0:1], g_ada[1:2], ada_w, m_ada_w, v_ada_w, "adamw_ada_w", 256)

    layered = [n for n in _SMALL if n != 'final_norm_w']
    full = [jnp.stack([small_parts[0][n], small_parts[1][n]]) for n in layered] + [dfinal.reshape(D_MODEL)]
    full.append(loss_blk[0:1, 0:1])
    full_shapes = [f.shape for f in full]
    summed = _flat_unpack(_allreduce_rows(_flat_pack(full, row_multiple=64)), full_shapes)
    loss = summed[-1].reshape(())
    local = []
    for n, g in zip(_SMALL, summed):
        if n in _SHARDED_SMALL:
            axis, size = _SHARDED_SMALL[n]
            g = lax.dynamic_slice_in_dim(g, me * size, size, axis=axis)
        local.append(g.reshape(a[n].shape))
    outs = _adamw_small([a[n] for n in _SMALL], local, [a['m_' + n] for n in _SMALL], [a['v_' + n] for n in _SMALL])
    for i, n in enumerate(_SMALL):
        grads[n], deltas[n], new_m[n], new_v[n] = local[i], outs[3 * i], outs[3 * i + 1], outs[3 * i + 2]

    return (loss, grad_x, *[grads[n] for n in _W_NAMES], *[deltas[n] for n in _W_NAMES],
            *[new_m[n] for n in _W_NAMES], *[new_v[n] for n in _W_NAMES])
```
